```python
import jax, jax.numpy as jnp
from jax import lax
import numpy as np

D_MODEL = 1024
BATCH = 16
SEQ = 2048
DEPTH = 2

N_MIXERS = 2
N_MLA_LAYERS = (DEPTH + 1) // 2
N_FOX_LAYERS = DEPTH // 2

MLA_HEADS = 8
MLA_NOPE_DIM = 128
MLA_ROPE_DIM = 64
MLA_V_DIM = 128
MLA_Q_RANK = 256
MLA_KV_RANK = 256
ROPE_THETA = 10000.0

FOX_HEADS = 16
FOX_HEAD_DIM = D_MODEL // FOX_HEADS

D_FF = -(-8 * D_MODEL // (3 * 256)) * 256

Q_BLOCK = 128
DEEPNORM_ALPHA = (2.0 * DEPTH) ** 0.25
DEEPNORM_BETA = (8.0 * DEPTH) ** -0.25
NORM_EPS = 1e-5
MLA_IN_DIM = MLA_Q_RANK + MLA_KV_RANK + MLA_ROPE_DIM
FOX_IN_DIM = 3 * D_MODEL + FOX_HEADS

kernel_name = "hybrid_mla_fox_deepnorm_adaln"


def rms_norm(x, g):
    xf = x.astype(jnp.float32)
    y = xf * lax.rsqrt(jnp.mean(xf * xf, axis=-1, keepdims=True) + NORM_EPS)
    return (y * g.astype(jnp.float32)).astype(x.dtype)


def layer_norm(x, g, b):
    xf = x.astype(jnp.float32)
    mu = jnp.mean(xf, axis=-1, keepdims=True)
    var = jnp.mean(jnp.square(xf - mu), axis=-1, keepdims=True)
    y = (xf - mu) * lax.rsqrt(var + NORM_EPS)
    return (y * g.astype(jnp.float32) + b.astype(jnp.float32)).astype(x.dtype)


def rotary_angles(positions, dim):
    half = dim // 2
    inv_freq = ROPE_THETA ** (-jnp.arange(half, dtype=jnp.float32) / half)
    ang = positions.astype(jnp.float32)[..., None] * inv_freq
    return jnp.cos(ang), jnp.sin(ang)


def apply_rotary(x, cos, sin):
    half = x.shape[-1] // 2
    x1, x2 = x[..., :half], x[..., half:]
    cos = cos.astype(x.dtype)
    sin = sin.astype(x.dtype)
    return jnp.concatenate([x1 * cos - x2 * sin, x2 * cos + x1 * sin], axis=-1)


def causal_block_attention(logits_fn, v):
    b, h, s, dv = v.shape
    key_pos = jnp.arange(s)

    def one_block(blk):
        q_start = blk * Q_BLOCK
        logits = logits_fn(q_start)
        q_pos = q_start + jnp.arange(Q_BLOCK)
        causal = q_pos[:, None] >= key_pos[None, :]
        probs = jax.nn.softmax(jnp.where(causal, logits, -jnp.inf), axis=-1)
        return jnp.einsum('bhqs,bhsd->bhqd', probs.astype(v.dtype), v)

    out = lax.map(one_block, jnp.arange(s // Q_BLOCK))
    return out.transpose(1, 0, 3, 2, 4).reshape(b, s, h * dv)


def mla_mixer(u, cos, sin, w_in, g_q, w_uq, g_kv, w_uk, w_uv, w_o):
    b, s, _ = u.shape
    h_in = u @ w_in
    c_q = rms_norm(h_in[..., :MLA_Q_RANK], g_q)
    c_kv = rms_norm(h_in[..., MLA_Q_RANK:MLA_Q_RANK + MLA_KV_RANK], g_kv)
    k_rope = apply_rotary(h_in[..., MLA_Q_RANK + MLA_KV_RANK:], cos, sin)

    q = (c_q @ w_uq).reshape(b, s, MLA_HEADS, MLA_NOPE_DIM + MLA_ROPE_DIM)
    q_nope = q[..., :MLA_NOPE_DIM].transpose(0, 2, 1, 3)
    q_rope = apply_rotary(q[..., MLA_NOPE_DIM:], cos[:, :, None, :], sin[:, :, None, :])
    q_rope = q_rope.transpose(0, 2, 1, 3)
    k_nope = (c_kv @ w_uk).reshape(b, s, MLA_HEADS, MLA_NOPE_DIM).transpose(0, 2, 1, 3)
    v = (c_kv @ w_uv).reshape(b, s, MLA_HEADS, MLA_V_DIM).transpose(0, 2, 1, 3)
    scale = (MLA_NOPE_DIM + MLA_ROPE_DIM) ** -0.5

    def logits_fn(q_start):
        qn = lax.dynamic_slice_in_dim(q_nope, q_start, Q_BLOCK, axis=2)
        qr = lax.dynamic_slice_in_dim(q_rope, q_start, Q_BLOCK, axis=2)
        sc = (jnp.einsum('bhqd,bhsd->bhqs', qn, k_nope)
              + jnp.einsum('bhqr,bsr->bhqs', qr, k_rope))
        return sc.astype(jnp.float32) * scale

    return causal_block_attention(logits_fn, v) @ w_o


def fox_mixer(u, w_in, b_f, w_o):
    b, s, d = u.shape
    h_in = u @ w_in
    def heads(t):
        return t.reshape(b, s, FOX_HEADS, FOX_HEAD_DIM).transpose(0, 2, 1, 3)
    q = heads(h_in[..., :d])
    k = heads(h_in[..., d:2 * d])
    v = heads(h_in[..., 2 * d:3 * d])
    log_f = jax.nn.log_sigmoid(h_in[..., 3 * d:].astype(jnp.float32) + b_f.astype(jnp.float32))
    cum_log_f = lax.cumsum(log_f, axis=1).transpose(0, 2, 1)
    scale = FOX_HEAD_DIM ** -0.5

    def logits_fn(q_start):
        qb = lax.dynamic_slice_in_dim(q, q_start, Q_BLOCK, axis=2)
        fq = lax.dynamic_slice_in_dim(cum_log_f, q_start, Q_BLOCK, axis=2)
        sc = jnp.einsum('bhqd,bhsd->bhqs', qb, k).astype(jnp.float32) * scale
        return sc + fq[..., :, None] - cum_log_f[:, :, None, :]

    return causal_block_attention(logits_fn, v) @ w_o


def swiglu(u, w_gate, w_up, w_down):
    return (jax.nn.silu(u @ w_gate) * (u @ w_up)) @ w_down


def modulate(x, shift, scale):
    return x * (1.0 + scale[:, None, :]) + shift[:, None, :]


def _fwd_setup_inputs(seed: int = 0) -> dict:
    key = jax.random.key(seed)
    ks = iter(jax.random.split(key, 40))
    f32 = jnp.float32
    def nrm(shape, std):
        return jax.random.normal(next(ks), shape, f32) * std
    D, H, Hf = D_MODEL, MLA_HEADS, FOX_HEADS
    beta = DEEPNORM_BETA
    nm, nf = N_MLA_LAYERS, N_FOX_LAYERS

    x = jax.random.normal(next(ks), (BATCH, SEQ, D), f32)
    c = jax.random.normal(next(ks), (BATCH, D), f32)
    positions = (jnp.arange(SEQ, dtype=jnp.int32)[None, :]
                 + jax.random.randint(next(ks), (BATCH, 1), 0, 128, dtype=jnp.int32))

    mla_w_in = nrm((nm, D, MLA_IN_DIM), D ** -0.5)
    mla_g_q = 1.0 + nrm((nm, MLA_Q_RANK), 0.02)
    mla_w_uq = nrm((nm, MLA_Q_RANK, H * (MLA_NOPE_DIM + MLA_ROPE_DIM)), MLA_Q_RANK ** -0.5)
    mla_g_kv = 1.0 + nrm((nm, MLA_KV_RANK), 0.02)
    mla_w_uk = nrm((nm, MLA_KV_RANK, H * MLA_NOPE_DIM), MLA_KV_RANK ** -0.5)
    mla_w_uv = nrm((nm, MLA_KV_RANK, H * MLA_V_DIM), beta * MLA_KV_RANK ** -0.5)
    mla_w_o = nrm((nm, H * MLA_V_DIM, D), beta * (H * MLA_V_DIM) ** -0.5)

    fox_w_in = jnp.concatenate([
        nrm((nf, D, 2 * D), D ** -0.5),
        nrm((nf, D, D), beta * D ** -0.5),
        nrm((nf, D, Hf), D ** -0.5),
    ], axis=-1)
    fox_b_f = 2.0 + nrm((nf, Hf), 0.5)
    fox_w_o = nrm((nf, D, D), beta * D ** -0.5)

    ada_w = nrm((DEPTH, D, 6 * D), 0.1 * D ** -0.5)
    ada_b = nrm((DEPTH, 6 * D), 0.02)

    ffn_w_gate = nrm((DEPTH, D, D_FF), beta * D ** -0.5)
    ffn_w_up = nrm((DEPTH, D, D_FF), beta * D ** -0.5)
    ffn_w_down = nrm((DEPTH, D_FF, D), beta * D_FF ** -0.5)

    ln_g = 1.0 + nrm((DEPTH, 2, D), 0.02)
    ln_b = nrm((DEPTH, 2, D), 0.02)

    return {"x": x, "c": c, "positions": positions,
            "mla_w_in": mla_w_in, "mla_g_q": mla_g_q, "mla_w_uq": mla_w_uq,
            "mla_g_kv": mla_g_kv, "mla_w_uk": mla_w_uk, "mla_w_uv": mla_w_uv, "mla_w_o": mla_w_o,
            "fox_w_in": fox_w_in, "fox_b_f": fox_b_f, "fox_w_o": fox_w_o,
            "ada_w": ada_w, "ada_b": ada_b,
            "ffn_w_gate": ffn_w_gate, "ffn_w_up": ffn_w_up, "ffn_w_down": ffn_w_down,
            "ln_g": ln_g, "ln_b": ln_b}


def _fwd_reference(x, c, positions, mla_w_in, mla_g_q, mla_w_uq, mla_g_kv, mla_w_uk, mla_w_uv, mla_w_o,
              fox_w_in, fox_b_f, fox_w_o, ada_w, ada_b, ffn_w_gate, ffn_w_up, ffn_w_down,
              ln_g, ln_b):
    d = D_MODEL
    cos, sin = rotary_angles(positions, MLA_ROPE_DIM)
    c_act = jax.nn.silu(c)
    for i in range(DEPTH):
        mod = c_act @ ada_w[i] + ada_b[i]
        sh_a, sc_a, gt_a = mod[:, :d], mod[:, d:2 * d], mod[:, 2 * d:3 * d]
        sh_f, sc_f, gt_f = mod[:, 3 * d:4 * d], mod[:, 4 * d:5 * d], mod[:, 5 * d:]

        u = modulate(x, sh_a, sc_a)
        j = i // N_MIXERS
        if i % N_MIXERS == 0:
            y = mla_mixer(u, cos, sin, mla_w_in[j], mla_g_q[j], mla_w_uq[j], mla_g_kv[j],
                          mla_w_uk[j], mla_w_uv[j], mla_w_o[j])
        else:
            y = fox_mixer(u, fox_w_in[j], fox_b_f[j], fox_w_o[j])
        x = layer_norm(DEEPNORM_ALPHA * x + (1.0 + gt_a[:, None, :]) * y, ln_g[i, 0], ln_b[i, 0])

        u = modulate(x, sh_f, sc_f)
        y = swiglu(u, ffn_w_gate[i], ffn_w_up[i], ffn_w_down[i])
        x = layer_norm(DEEPNORM_ALPHA * x + (1.0 + gt_f[:, None, :]) * y, ln_g[i, 1], ln_b[i, 1])
    return x


import jax as _jax
import jax.numpy as _jnp

TWIN_FORMAT = 'train_step'
FWD_PARAMS = ['x', 'c', 'positions', 'mla_w_in', 'mla_g_q', 'mla_w_uq', 'mla_g_kv', 'mla_w_uk', 'mla_w_uv', 'mla_w_o', 'fox_w_in', 'fox_b_f', 'fox_w_o', 'ada_w', 'ada_b', 'ffn_w_gate', 'ffn_w_up', 'ffn_w_down', 'ln_g', 'ln_b']
TWIN_WEIGHTS = ['mla_w_in', 'mla_g_q', 'mla_w_uq', 'mla_g_kv', 'mla_w_uk', 'mla_w_uv', 'mla_w_o', 'fox_w_in', 'fox_b_f', 'fox_w_o', 'ada_w', 'ada_b', 'ffn_w_gate', 'ffn_w_up', 'ffn_w_down', 'ln_g', 'ln_b']
TWIN_DIFF_INPUT = 'x'
TWIN_INPUTS = ['x', 'c', 'positions', 'mla_w_in', 'mla_g_q', 'mla_w_uq', 'mla_g_kv', 'mla_w_uk', 'mla_w_uv', 'mla_w_o', 'fox_w_in', 'fox_b_f', 'fox_w_o', 'ada_w', 'ada_b', 'ffn_w_gate', 'ffn_w_up', 'ffn_w_down', 'ln_g', 'ln_b', 'loss_target', 'm_mla_w_in', 'm_mla_g_q', 'm_mla_w_uq', 'm_mla_g_kv', 'm_mla_w_uk', 'm_mla_w_uv', 'm_mla_w_o', 'm_fox_w_in', 'm_fox_b_f', 'm_fox_w_o', 'm_ada_w', 'm_ada_b', 'm_ffn_w_gate', 'm_ffn_w_up', 'm_ffn_w_down', 'm_ln_g', 'm_ln_b', 'v_mla_w_in', 'v_mla_g_q', 'v_mla_w_uq', 'v_mla_g_kv', 'v_mla_w_uk', 'v_mla_w_uv', 'v_mla_w_o', 'v_fox_w_in', 'v_fox_b_f', 'v_fox_w_o', 'v_ada_w', 'v_ada_b', 'v_ffn_w_gate', 'v_ffn_w_up', 'v_ffn_w_down', 'v_ln_g', 'v_ln_b']
TWIN_OUTPUTS = ['loss', 'grad_x', 'grad_mla_w_in', 'grad_mla_g_q', 'grad_mla_w_uq', 'grad_mla_g_kv', 'grad_mla_w_uk', 'grad_mla_w_uv', 'grad_mla_w_o', 'grad_fox_w_in', 'grad_fox_b_f', 'grad_fox_w_o', 'grad_ada_w', 'grad_ada_b', 'grad_ffn_w_gate', 'grad_ffn_w_up', 'grad_ffn_w_down', 'grad_ln_g', 'grad_ln_b', 'delta_mla_w_in', 'delta_mla_g_q', 'delta_mla_w_uq', 'delta_mla_g_kv', 'delta_mla_w_uk', 'delta_mla_w_uv', 'delta_mla_w_o', 'delta_fox_w_in', 'delta_fox_b_f', 'delta_fox_w_o', 'delta_ada_w', 'delta_ada_b', 'delta_ffn_w_gate', 'delta_ffn_w_up', 'delta_ffn_w_down', 'delta_ln_g', 'delta_ln_b', 'new_m_mla_w_in', 'new_m_mla_g_q', 'new_m_mla_w_uq', 'new_m_mla_g_kv', 'new_m_mla_w_uk', 'new_m_mla_w_uv', 'new_m_mla_w_o', 'new_m_fox_w_in', 'new_m_fox_b_f', 'new_m_fox_w_o', 'new_m_ada_w', 'new_m_ada_b', 'new_m_ffn_w_gate', 'new_m_ffn_w_up', 'new_m_ffn_w_down', 'new_m_ln_g', 'new_m_ln_b', 'new_v_mla_w_in', 'new_v_mla_g_q', 'new_v_mla_w_uq', 'new_v_mla_g_kv', 'new_v_mla_w_uk', 'new_v_mla_w_uv', 'new_v_mla_w_o', 'new_v_fox_w_in', 'new_v_fox_b_f', 'new_v_fox_w_o', 'new_v_ada_w', 'new_v_ada_b', 'new_v_ffn_w_gate', 'new_v_ffn_w_up', 'new_v_ffn_w_down', 'new_v_ln_g', 'new_v_ln_b']
TWIN_LEAF_KINDS = {'loss': 'loss', 'grad_x': 'grad_x', 'grad_mla_w_in': 'grad_w', 'grad_mla_g_q': 'grad_w', 'grad_mla_w_uq': 'grad_w', 'grad_mla_g_kv': 'grad_w', 'grad_mla_w_uk': 'grad_w', 'grad_mla_w_uv': 'grad_w', 'grad_mla_w_o': 'grad_w', 'grad_fox_w_in': 'grad_w', 'grad_fox_b_f': 'grad_w', 'grad_fox_w_o': 'grad_w', 'grad_ada_w': 'grad_w', 'grad_ada_b': 'grad_w', 'grad_ffn_w_gate': 'grad_w', 'grad_ffn_w_up': 'grad_w', 'grad_ffn_w_down': 'grad_w', 'grad_ln_g': 'grad_w', 'grad_ln_b': 'grad_w', 'delta_mla_w_in': 'delta_w', 'delta_mla_g_q': 'delta_w', 'delta_mla_w_uq': 'delta_w', 'delta_mla_g_kv': 'delta_w', 'delta_mla_w_uk': 'delta_w', 'delta_mla_w_uv': 'delta_w', 'delta_mla_w_o': 'delta_w', 'delta_fox_w_in': 'delta_w', 'delta_fox_b_f': 'delta_w', 'delta_fox_w_o': 'delta_w', 'delta_ada_w': 'delta_w', 'delta_ada_b': 'delta_w', 'delta_ffn_w_gate': 'delta_w', 'delta_ffn_w_up': 'delta_w', 'delta_ffn_w_down': 'delta_w', 'delta_ln_g': 'delta_w', 'delta_ln_b': 'delta_w', 'new_m_mla_w_in': 'new_m', 'new_m_mla_g_q': 'new_m', 'new_m_mla_w_uq': 'new_m', 'new_m_mla_g_kv': 'new_m', 'new_m_mla_w_uk': 'new_m', 'new_m_mla_w_uv': 'new_m', 'new_m_mla_w_o': 'new_m', 'new_m_fox_w_in': 'new_m', 'new_m_fox_b_f': 'new_m', 'new_m_fox_w_o': 'new_m', 'new_m_ada_w': 'new_m', 'new_m_ada_b': 'new_m', 'new_m_ffn_w_gate': 'new_m', 'new_m_ffn_w_up': 'new_m', 'new_m_ffn_w_down': 'new_m', 'new_m_ln_g': 'new_m', 'new_m_ln_b': 'new_m', 'new_v_mla_w_in': 'new_v', 'new_v_mla_g_q': 'new_v', 'new_v_mla_w_uq': 'new_v', 'new_v_mla_g_kv': 'new_v', 'new_v_mla_w_uk': 'new_v', 'new_v_mla_w_uv': 'new_v', 'new_v_mla_w_o': 'new_v', 'new_v_fox_w_in': 'new_v', 'new_v_fox_b_f': 'new_v', 'new_v_fox_w_o': 'new_v', 'new_v_ada_w': 'new_v', 'new_v_ada_b': 'new_v', 'new_v_ffn_w_gate': 'new_v', 'new_v_ffn_w_up': 'new_v', 'new_v_ffn_w_down': 'new_v', 'new_v_ln_g': 'new_v', 'new_v_ln_b': 'new_v'}


def _forward(args):
    return _fwd_reference(*[args[k] for k in FWD_PARAMS])


def _output_shape():
    out = _jax.eval_shape(lambda: _forward(_fwd_setup_inputs(0)))
    return out.shape, out.dtype

N_MICROBATCH = 1
ADAM_LR = 0.001
ADAM_B1 = 0.9
ADAM_B2 = 0.999
ADAM_EPS = 1e-08
ADAM_WD = 0.01
ADAM_STEP = 10
PER_EXAMPLE_BATCH_AXIS = {'x': 0, 'c': 0, 'positions': 0, 'loss_target': 0}
SHARED_INPUTS = []
_WEIGHT_DTYPES = {'mla_w_in': _jnp.float32, 'mla_g_q': _jnp.float32, 'mla_w_uq': _jnp.float32, 'mla_g_kv': _jnp.float32, 'mla_w_uk': _jnp.float32, 'mla_w_uv': _jnp.float32, 'mla_w_o': _jnp.float32, 'fox_w_in': _jnp.float32, 'fox_b_f': _jnp.float32, 'fox_w_o': _jnp.float32, 'ada_w': _jnp.float32, 'ada_b': _jnp.float32, 'ffn_w_gate': _jnp.float32, 'ffn_w_up': _jnp.float32, 'ffn_w_down': _jnp.float32, 'ln_g': _jnp.float32, 'ln_b': _jnp.float32}
MOMENT_SCALE = {'mla_w_in': 1.441204e-02, 'mla_g_q': 1.105238e-02, 'mla_w_uq': 4.676474e-03, 'mla_g_kv': 1.870320e-02, 'mla_w_uk': 4.737057e-03, 'mla_w_uv': 1.489956e-02, 'mla_w_o': 1.485799e-02, 'fox_w_in': 1.679084e-02, 'fox_b_f': 1.123272e-01, 'fox_w_o': 2.563123e-02, 'ada_w': 1.316144e-02, 'ada_b': 3.160083e-02, 'ffn_w_gate': 1.047976e-02, 'ffn_w_up': 1.028543e-02, 'ffn_w_down': 1.705588e-02, 'ln_g': 1.604803e+01, 'ln_b': 6.484235e-01}


def _to_microbatches(a, axis):
    t = _jnp.moveaxis(a, axis, 0)
    t = t.reshape((N_MICROBATCH, t.shape[0] // N_MICROBATCH) + t.shape[1:])
    return _jnp.moveaxis(t, 1, axis + 1)


def setup_inputs(seed: int = 0) -> dict:
    inp = _fwd_setup_inputs(seed)
    key = _jax.random.fold_in(_jax.random.key(seed), 7919)
    shape, _ = _output_shape()
    out = dict(inp)
    out["loss_target"] = _jax.random.normal(_jax.random.fold_in(key, 0), shape, _jnp.float32)
    for i, name in enumerate(TWIN_WEIGHTS):
        w = inp[name].astype(_jnp.float32)
        if MOMENT_SCALE is None:
            s = _jnp.sqrt(_jnp.mean(_jnp.square(w)) + 1e-30)
        else:
            s = MOMENT_SCALE[name]
        km, kv = _jax.random.split(_jax.random.fold_in(key, i + 1))
        out[name] = w
        out["m_" + name] = s * _jax.random.normal(km, w.shape, _jnp.float32)
        out["v_" + name] = (s * s) * _jax.random.uniform(kv, w.shape, _jnp.float32, 0.5, 1.5)
    if N_MICROBATCH > 1:
        for name, axis in PER_EXAMPLE_BATCH_AXIS.items():
            out[name] = _to_microbatches(out[name], axis)
    return {'x': out['x'], 'c': out['c'], 'positions': out['positions'], 'mla_w_in': out['mla_w_in'], 'mla_g_q': out['mla_g_q'], 'mla_w_uq': out['mla_w_uq'], 'mla_g_kv': out['mla_g_kv'], 'mla_w_uk': out['mla_w_uk'], 'mla_w_uv': out['mla_w_uv'], 'mla_w_o': out['mla_w_o'], 'fox_w_in': out['fox_w_in'], 'fox_b_f': out['fox_b_f'], 'fox_w_o': out['fox_w_o'], 'ada_w': out['ada_w'], 'ada_b': out['ada_b'], 'ffn_w_gate': out['ffn_w_gate'], 'ffn_w_up': out['ffn_w_up'], 'ffn_w_down': out['ffn_w_down'], 'ln_g': out['ln_g'], 'ln_b': out['ln_b'], 'loss_target': out['loss_target'], 'm_mla_w_in': out['m_mla_w_in'], 'm_mla_g_q': out['m_mla_g_q'], 'm_mla_w_uq': out['m_mla_w_uq'], 'm_mla_g_kv': out['m_mla_g_kv'], 'm_mla_w_uk': out['m_mla_w_uk'], 'm_mla_w_uv': out['m_mla_w_uv'], 'm_mla_w_o': out['m_mla_w_o'], 'm_fox_w_in': out['m_fox_w_in'], 'm_fox_b_f': out['m_fox_b_f'], 'm_fox_w_o': out['m_fox_w_o'], 'm_ada_w': out['m_ada_w'], 'm_ada_b': out['m_ada_b'], 'm_ffn_w_gate': out['m_ffn_w_gate'], 'm_ffn_w_up': out['m_ffn_w_up'], 'm_ffn_w_down': out['m_ffn_w_down'], 'm_ln_g': out['m_ln_g'], 'm_ln_b': out['m_ln_b'], 'v_mla_w_in': out['v_mla_w_in'], 'v_mla_g_q': out['v_mla_g_q'], 'v_mla_w_uq': out['v_mla_w_uq'], 'v_mla_g_kv': out['v_mla_g_kv'], 'v_mla_w_uk': out['v_mla_w_uk'], 'v_mla_w_uv': out['v_mla_w_uv'], 'v_mla_w_o': out['v_mla_w_o'], 'v_fox_w_in': out['v_fox_w_in'], 'v_fox_b_f': out['v_fox_b_f'], 'v_fox_w_o': out['v_fox_w_o'], 'v_ada_w': out['v_ada_w'], 'v_ada_b': out['v_ada_b'], 'v_ffn_w_gate': out['v_ffn_w_gate'], 'v_ffn_w_up': out['v_ffn_w_up'], 'v_ffn_w_down': out['v_ffn_w_down'], 'v_ln_g': out['v_ln_g'], 'v_ln_b': out['v_ln_b']}


def _loss(weights, diff, rest, loss_target):
    with _jax.named_scope("forward"):
        args = {**rest, TWIN_DIFF_INPUT: diff, **{k: w.astype(_WEIGHT_DTYPES[k]) for k, w in weights.items()}}
        y = _forward(args)
    with _jax.named_scope("loss_head"):
        err = _jnp.square(y.astype(_jnp.float32) - loss_target)
        return 0.5 * _jnp.sum(_jnp.mean(err, axis=-1)) if err.ndim else 0.5 * err


def _adamw(w, g, m, v):
    m = ADAM_B1 * m + (1.0 - ADAM_B1) * g
    v = ADAM_B2 * v + (1.0 - ADAM_B2) * _jnp.square(g)
    m_hat = m / (1.0 - ADAM_B1 ** ADAM_STEP)
    v_hat = v / (1.0 - ADAM_B2 ** ADAM_STEP)
    delta = -ADAM_LR * (m_hat / (_jnp.sqrt(v_hat) + ADAM_EPS) + ADAM_WD * w)
    return delta, m, v


def reference(x, c, positions, mla_w_in, mla_g_q, mla_w_uq, mla_g_kv, mla_w_uk, mla_w_uv, mla_w_o, fox_w_in, fox_b_f, fox_w_o, ada_w, ada_b, ffn_w_gate, ffn_w_up, ffn_w_down, ln_g, ln_b, loss_target, m_mla_w_in, m_mla_g_q, m_mla_w_uq, m_mla_g_kv, m_mla_w_uk, m_mla_w_uv, m_mla_w_o, m_fox_w_in, m_fox_b_f, m_fox_w_o, m_ada_w, m_ada_b, m_ffn_w_gate, m_ffn_w_up, m_ffn_w_down, m_ln_g, m_ln_b, v_mla_w_in, v_mla_g_q, v_mla_w_uq, v_mla_g_kv, v_mla_w_uk, v_mla_w_uv, v_mla_w_o, v_fox_w_in, v_fox_b_f, v_fox_w_o, v_ada_w, v_ada_b, v_ffn_w_gate, v_ffn_w_up, v_ffn_w_down, v_ln_g, v_ln_b):
    given = dict(x=x, c=c, positions=positions, mla_w_in=mla_w_in, mla_g_q=mla_g_q, mla_w_uq=mla_w_uq, mla_g_kv=mla_g_kv, mla_w_uk=mla_w_uk, mla_w_uv=mla_w_uv, mla_w_o=mla_w_o, fox_w_in=fox_w_in, fox_b_f=fox_b_f, fox_w_o=fox_w_o, ada_w=ada_w, ada_b=ada_b, ffn_w_gate=ffn_w_gate, ffn_w_up=ffn_w_up, ffn_w_down=ffn_w_down, ln_g=ln_g, ln_b=ln_b, loss_target=loss_target, m_mla_w_in=m_mla_w_in, m_mla_g_q=m_mla_g_q, m_mla_w_uq=m_mla_w_uq, m_mla_g_kv=m_mla_g_kv, m_mla_w_uk=m_mla_w_uk, m_mla_w_uv=m_mla_w_uv, m_mla_w_o=m_mla_w_o, m_fox_w_in=m_fox_w_in, m_fox_b_f=m_fox_b_f, m_fox_w_o=m_fox_w_o, m_ada_w=m_ada_w, m_ada_b=m_ada_b, m_ffn_w_gate=m_ffn_w_gate, m_ffn_w_up=m_ffn_w_up, m_ffn_w_down=m_ffn_w_down, m_ln_g=m_ln_g, m_ln_b=m_ln_b, v_mla_w_in=v_mla_w_in, v_mla_g_q=v_mla_g_q, v_mla_w_uq=v_mla_w_uq, v_mla_g_kv=v_mla_g_kv, v_mla_w_uk=v_mla_w_uk, v_mla_w_uv=v_mla_w_uv, v_mla_w_o=v_mla_w_o, v_fox_w_in=v_fox_w_in, v_fox_b_f=v_fox_b_f, v_fox_w_o=v_fox_w_o, v_ada_w=v_ada_w, v_ada_b=v_ada_b, v_ffn_w_gate=v_ffn_w_gate, v_ffn_w_up=v_ffn_w_up, v_ffn_w_down=v_ffn_w_down, v_ln_g=v_ln_g, v_ln_b=v_ln_b)
    weights = {n: given[n] for n in TWIN_WEIGHTS}
    shared = {n: given[n] for n in SHARED_INPUTS}
    per_example = {n: given[n] for n in ['x', 'c', 'positions']}
    grad_fn = _jax.value_and_grad(_loss, argnums=(0, 1))

    def one_microbatch(ex, loss_target):
        ex = dict(ex)
        diff = ex.pop(TWIN_DIFF_INPUT)
        return grad_fn(weights, diff, {**shared, **ex}, loss_target)

    if N_MICROBATCH == 1:
        loss, (grad_w, grad_x) = one_microbatch(per_example, given["loss_target"])
    else:
        def body(carry, xs):
            loss_sum, grad_sum = carry
            l_k, (gw_k, gx_k) = one_microbatch(xs[0], xs[1])
            with _jax.named_scope("update"):
                return (loss_sum + l_k, _jax.tree.map(_jnp.add, grad_sum, gw_k)), gx_k

        init = (_jnp.zeros((), _jnp.float32), _jax.tree.map(_jnp.zeros_like, weights))
        (loss, grad_w), grad_x = _jax.lax.scan(body, init, (per_example, given["loss_target"]))
    with _jax.named_scope("update"):
        delta_w, new_m, new_v = {}, {}, {}
        for n in TWIN_WEIGHTS:
            delta_w[n], new_m[n], new_v[n] = _adamw(weights[n], grad_w[n], given["m_" + n], given["v_" + n])
    return (loss, grad_x, *[grad_w[n] for n in TWIN_WEIGHTS], *[delta_w[n] for n in TWIN_WEIGHTS],
            *[new_m[n] for n in TWIN_WEIGHTS], *[new_v[n] for n in TWIN_WEIGHTS])
```

```python
import functools

import jax
import jax.numpy as jnp
from jax import lax
from jax.experimental import pallas as pl
from jax.experimental.pallas import tpu as pltpu

F32 = jnp.float32
BF16 = jnp.bfloat16
LANES = 128
N_DEV = 8
VMEM_LIMIT_BYTES = 56 * 1024 * 1024

DEPTH = 2
MLA_HEADS = 8
MLA_NOPE = 128
MLA_ROPE = 64
MLA_V = 128
MLA_QR = 256
MLA_KVR = 256
ROPE_THETA = 10000.0
FOX_HEADS = 16
FOX_HD = 64
ALPHA = (2.0 * DEPTH) ** 0.25
NORM_EPS = 1e-5
ADAM_LR = 0.001
ADAM_B1 = 0.9
ADAM_B2 = 0.999
ADAM_EPS = 1e-08
ADAM_WD = 0.01
ADAM_STEP = 10

MESH_AXES = ("x", "y", "c")
MESH = pl.DeviceIdType.MESH


def _params(*sem):
    return pltpu.CompilerParams(dimension_semantics=sem, vmem_limit_bytes=VMEM_LIMIT_BYTES)


def _tile(n, cap, mult=LANES):
    if n <= cap:
        return n
    best = None
    for t in range(mult, cap + 1, mult):
        if n % t == 0:
            best = t
    assert best is not None, (n, cap, mult)
    return best


def _dot(a, b, dims):
    return lax.dot_general(a, b, (dims, ((), ())), preferred_element_type=F32)


def _nn(a, b):
    return _dot(a, b, ((1,), (0,)))


def _nt(a, b):
    return _dot(a, b, ((1,), (1,)))


def _tn(a, b):
    return _dot(a, b, ((0,), (0,)))


def _me():
    return lax.axis_index("x"), lax.axis_index("y"), lax.axis_index("c")


def all_gather(x_loc, name):
    r, c = x_loc.shape

    def body(x_ref, out_ref, send_sems, recv_sems, local_sem):
        x, y, cc = _me()
        me, sibling = (x, y, cc), (x, y, 1 - cc)
        chips = [(1 - x, y), (x, 1 - y), (1 - x, 1 - y)]

        def rows(px, py, pc):
            return out_ref.at[4 * px + 2 * py + pc]

        def copy(k, block, to, src=None):
            return pltpu.make_async_remote_copy(
                src_ref=rows(*block) if src is None else src, dst_ref=rows(*block),
                send_sem=send_sems.at[k], recv_sem=recv_sems.at[k], device_id=to, device_id_type=MESH)

        mine = pltpu.make_async_copy(x_ref, rows(*me), local_sem)
        mine.start()
        first = [copy(0, me, sibling, src=x_ref)]
        first += [copy(1 + j, me, (*chip, cc), src=x_ref) for j, chip in enumerate(chips)]
        for cp in first:
            cp.start()
        passed = [copy(4 + j, (*chip, cc), sibling) for j, chip in enumerate(chips)]
        for j, chip in enumerate(chips):
            copy(1 + j, (*chip, cc), me).wait_recv()
            passed[j].start()
        copy(0, sibling, me).wait_recv()
        for j, chip in enumerate(chips):
            copy(4 + j, (*chip, 1 - cc), me).wait_recv()
        for cp in first + passed:
            cp.wait_send()
        mine.wait()

    return pl.pallas_call(
        body, name=name,
        out_shape=jax.ShapeDtypeStruct((N_DEV, r, c), x_loc.dtype),
        in_specs=[pl.BlockSpec(memory_space=pl.ANY)],
        out_specs=pl.BlockSpec(memory_space=pl.ANY),
        scratch_shapes=[pltpu.SemaphoreType.DMA((7,)), pltpu.SemaphoreType.DMA((7,)), pltpu.SemaphoreType.DMA(())],
    )(x_loc)


def rs_sibling_exchange(g, name):
    _, r, c = g.shape

    def body(g_ref, land_ref, send_sems, recv_sems):
        x, y, cc = _me()
        copies = []
        for k in range(4):
            px, py = k // 2, k % 2
            copies.append(pltpu.make_async_remote_copy(
                src_ref=g_ref.at[4 * px + 2 * py + (1 - cc)], dst_ref=land_ref.at[k],
                send_sem=send_sems.at[k], recv_sem=recv_sems.at[k], device_id=(x, y, 1 - cc), device_id_type=MESH))
        for cp in copies:
            cp.start()
        for cp in copies:
            cp.wait_recv()
        for cp in copies:
            cp.wait_send()

    return pl.pallas_call(
        body, name=name,
        out_shape=jax.ShapeDtypeStruct((4, r, c), g.dtype),
        in_specs=[pl.BlockSpec(memory_space=pl.ANY)],
        out_specs=pl.BlockSpec(memory_space=pl.ANY),
        scratch_shapes=[pltpu.SemaphoreType.DMA((4,)), pltpu.SemaphoreType.DMA((4,))],
    )(g)


def rs_chip_exchange(p, name):
    _, r, c = p.shape

    def body(p_ref, land_ref, send_sems, recv_sems, local_sem):
        x, y, cc = _me()
        mine = pltpu.make_async_copy(p_ref.at[2 * x + y], land_ref.at[2 * x + y], local_sem)
        mine.start()
        chips = [(1 - x, y), (x, 1 - y), (1 - x, 1 - y)]
        sends = [pltpu.make_async_remote_copy(
            src_ref=p_ref.at[2 * px + py], dst_ref=land_ref.at[2 * x + y],
            send_sem=send_sems.at[j], recv_sem=recv_sems.at[j], device_id=(px, py, cc), device_id_type=MESH)
            for j, (px, py) in enumerate(chips)]
        for cp in sends:
            cp.start()
        for j, (px, py) in enumerate(chips):
            pltpu.make_async_remote_copy(
                src_ref=p_ref.at[2 * x + y], dst_ref=land_ref.at[2 * px + py],
                send_sem=send_sems.at[j], recv_sem=recv_sems.at[j], device_id=(px, py, cc), device_id_type=MESH).wait_recv()
        for cp in sends:
            cp.wait_send()
        mine.wait()

    return pl.pallas_call(
        body, name=name,
        out_shape=jax.ShapeDtypeStruct((4, r, c), p.dtype),
        in_specs=[pl.BlockSpec(memory_space=pl.ANY)],
        out_specs=pl.BlockSpec(memory_space=pl.ANY),
        scratch_shapes=[pltpu.SemaphoreType.DMA((3,)), pltpu.SemaphoreType.DMA((3,)), pltpu.SemaphoreType.DMA(())],
    )(p)


def rs_add_sibling(g, land, name):
    _, r, c = g.shape
    tr = _tile(r, 512, 8)
    core = lax.axis_index("c").astype(jnp.int32).reshape(1)

    def body(core_ref, g_ref, l_ref, o_ref):
        o_ref[...] = g_ref[...] + l_ref[...]

    return pl.pallas_call(
        body, name=name,
        out_shape=jax.ShapeDtypeStruct((4, r, c), g.dtype),
        grid_spec=pltpu.PrefetchScalarGridSpec(
            num_scalar_prefetch=1, grid=(4, r // tr),
            in_specs=[pl.BlockSpec((1, tr, c), lambda k, i, core_ref: (2 * k + core_ref[0], i, 0)),
                      pl.BlockSpec((1, tr, c), lambda k, i, core_ref: (k, i, 0))],
            out_specs=pl.BlockSpec((1, tr, c), lambda k, i, core_ref: (k, i, 0))),
        compiler_params=_params("arbitrary", "arbitrary"),
    )(core, g, land)


def sum_leading(x, name):
    n, r, c = x.shape
    tr = _tile(r, 512, 8)

    def body(x_ref, o_ref):
        acc = x_ref[0]
        for k in range(1, n):
            acc = acc + x_ref[k]
        o_ref[...] = acc

    return pl.pallas_call(
        body, name=name,
        out_shape=jax.ShapeDtypeStruct((r, c), x.dtype),
        grid=(r // tr,),
        in_specs=[pl.BlockSpec((n, tr, c), lambda i: (0, i, 0))],
        out_specs=pl.BlockSpec((tr, c), lambda i: (i, 0)),
        compiler_params=_params("arbitrary"),
    )(x)


def mm(pairs, *, trans_b, out_dtype, name, out_slab=False, bias=None):
    a0 = pairs[0][0]
    m = a0.shape[1] if a0.ndim == 3 else a0.shape[0]
    n = pairs[0][1].shape[0] if trans_b else pairs[0][1].shape[1]
    k_total = sum(b.shape[1] if trans_b else b.shape[0] for _, b in pairs)
    deep = k_total > 2048
    tm = _tile(m, 256 if deep else 512, 8)
    tn = _tile(n, 512 if deep else 1024)
    slabs = [a.ndim == 3 for a, _ in pairs]
    n_pairs = len(pairs)

    def body(*refs):
        o_ref = refs[-1]
        acc = bias_ref = None
        if bias is not None:
            bias_ref = refs[2 * n_pairs]
        for i in range(n_pairs):
            a_ref, b_ref = refs[2 * i], refs[2 * i + 1]
            if slabs[i]:
                a = jnp.concatenate([a_ref[s].astype(BF16) for s in range(a_ref.shape[0])], axis=1)
            else:
                a = a_ref[...].astype(BF16)
            b = b_ref[...].astype(BF16)
            part = _nt(a, b) if trans_b else _nn(a, b)
            acc = part if acc is None else acc + part
        if bias_ref is not None:
            acc = acc + bias_ref[...]
        if out_slab:
            for s in range(tn // LANES):
                o_ref[s] = acc[:, s * LANES:(s + 1) * LANES].astype(out_dtype)
        else:
            o_ref[...] = acc.astype(out_dtype)

    in_specs, args = [], []
    for (a, b), slab in zip(pairs, slabs):
        if slab:
            in_specs.append(pl.BlockSpec((a.shape[0], tm, LANES), lambda i, j: (0, i, 0)))
        else:
            in_specs.append(pl.BlockSpec((tm, a.shape[1]), lambda i, j: (i, 0)))
        if trans_b:
            in_specs.append(pl.BlockSpec((tn, b.shape[1]), lambda i, j: (j, 0)))
        else:
            in_specs.append(pl.BlockSpec((b.shape[0], tn), lambda i, j: (0, j)))
        args += [a, b]
    if bias is not None:
        in_specs.append(pl.BlockSpec((1, tn), lambda i, j: (0, j)))
        args.append(bias)
    if out_slab:
        out_shape = jax.ShapeDtypeStruct((n // LANES, m, LANES), out_dtype)
        out_spec = pl.BlockSpec((tn // LANES, tm, LANES), lambda i, j: (j, i, 0))
    else:
        out_shape = jax.ShapeDtypeStruct((m, n), out_dtype)
        out_spec = pl.BlockSpec((tm, tn), lambda i, j: (i, j))
    return pl.pallas_call(
        body, name=name, out_shape=out_shape, grid=(m // tm, n // tn),
        in_specs=in_specs, out_specs=out_spec,
        compiler_params=_params("arbitrary", "arbitrary"),
    )(*args)


def mm_tn(a, b, *, name, tk_cap=1536, tn_cap=1024, tm_cap=512):
    slab = a.ndim == 3
    m = a.shape[1] if slab else a.shape[0]
    k = a.shape[0] * LANES if slab else a.shape[1]
    n = b.shape[1]
    tk = _tile(k, tk_cap)
    tn = _tile(n, tn_cap)
    tm = _tile(m, tm_cap, 8)

    def body(a_ref, b_ref, o_ref):
        @pl.when(pl.program_id(2) == 0)
        def _():
            o_ref[...] = jnp.zeros_like(o_ref)

        bb = b_ref[...].astype(BF16)
        if slab:
            for s in range(tk // LANES):
                o_ref[s * LANES:(s + 1) * LANES, :] += _tn(a_ref[s].astype(BF16), bb)
        else:
            o_ref[...] += _tn(a_ref[...].astype(BF16), bb)

    if slab:
        a_spec = pl.BlockSpec((tk // LANES, tm, LANES), lambda i, j, t: (i, t, 0))
    else:
        a_spec = pl.BlockSpec((tm, tk), lambda i, j, t: (t, i))
    return pl.pallas_call(
        body, name=name, out_shape=jax.ShapeDtypeStruct((k, n), F32), grid=(k // tk, n // tn, m // tm),
        in_specs=[a_spec, pl.BlockSpec((tm, tn), lambda i, j, t: (t, j))],
        out_specs=pl.BlockSpec((tk, tn), lambda i, j, t: (i, j)),
        compiler_params=_params("arbitrary", "arbitrary", "arbitrary"),
    )(a, b)


def _row_spec(d, k):
    return pl.BlockSpec((1, 1, d), lambda b, i: (6 * b + k, 0, 0))


def modulate(x, mod, k_shift, k_scale, bl, name):
    t, d = x.shape
    s = t // bl
    tm = _tile(s, 512, 8)
    nt = s // tm

    def body(x_ref, sh_ref, sc_ref, o_ref):
        o_ref[...] = (x_ref[...] * (1.0 + sc_ref[0]) + sh_ref[0]).astype(BF16)

    return pl.pallas_call(
        body, name=name, out_shape=jax.ShapeDtypeStruct((t, d), BF16), grid=(bl, nt),
        in_specs=[pl.BlockSpec((tm, d), lambda b, i: (b * nt + i, 0)), _row_spec(d, k_shift), _row_spec(d, k_scale)],
        out_specs=pl.BlockSpec((tm, d), lambda b, i: (b * nt + i, 0)),
        compiler_params=_params("arbitrary", "arbitrary"),
    )(x, mod, mod)


def _layer_norm_stats(r):
    mu = jnp.mean(r, axis=-1, keepdims=True)
    rc = r - mu
    var = jnp.mean(rc * rc, axis=-1, keepdims=True)
    rstd = lax.rsqrt(var + NORM_EPS)
    return rc * rstd, rstd


def residual_layer_norm(x, y, mod, k_gate, g, b, bl, name, next_mod=None):
    t, d = x.shape
    s = t // bl
    tm = _tile(s, 256, 8)
    nt = s // tm
    has_next = next_mod is not None

    def body(*refs):
        x_ref, y_ref, gt_ref, g_ref, b_ref = refs[:5]
        rest = refs[5:]
        if has_next:
            sh_ref, sc_ref, o_ref, r_ref, u_ref = rest
        else:
            o_ref, r_ref = rest
        r = ALPHA * x_ref[...] + (1.0 + gt_ref[0]) * y_ref[...]
        xhat, _ = _layer_norm_stats(r)
        out = xhat * g_ref[...] + b_ref[...]
        o_ref[...] = out
        r_ref[...] = r
        if has_next:
            u_ref[...] = (out * (1.0 + sc_ref[0]) + sh_ref[0]).astype(BF16)

    tok = pl.BlockSpec((tm, d), lambda bb, i: (bb * nt + i, 0))
    vec = pl.BlockSpec((1, d), lambda bb, i: (0, 0))
    in_specs = [tok, tok, _row_spec(d, k_gate), vec, vec]
    args = [x, y, mod, g, b]
    out_shape = [jax.ShapeDtypeStruct((t, d), F32), jax.ShapeDtypeStruct((t, d), F32)]
    out_specs = [tok, tok]
    if has_next:
        in_specs += [_row_spec(d, next_mod[0]), _row_spec(d, next_mod[1])]
        args += [mod if len(next_mod) == 2 else next_mod[2]] * 2
        out_shape.append(jax.ShapeDtypeStruct((t, d), BF16))
        out_specs.append(tok)
    return pl.pallas_call(
        body, name=name, out_shape=out_shape, grid=(bl, nt), in_specs=in_specs, out_specs=out_specs,
        compiler_params=_params("arbitrary", "arbitrary"),
    )(*args)


def loss_head(xo, target, name):
    t, d = xo.shape
    tm = _tile(t, 512, 8)

    def body(x_ref, t_ref, l_ref, dx_ref):
        @pl.when(pl.program_id(0) == 0)
        def _():
            l_ref[...] = jnp.zeros_like(l_ref)

        e = x_ref[...] - t_ref[...]
        l_ref[...] += jnp.sum(e * e, axis=0, keepdims=True) * (0.5 / d)
        dx_ref[...] = e * (1.0 / d)

    tok = pl.BlockSpec((tm, d), lambda i: (i, 0))
    return pl.pallas_call(
        body, name=name,
        out_shape=[jax.ShapeDtypeStruct((1, d), F32), jax.ShapeDtypeStruct((t, d), F32)],
        grid=(t // tm,), in_specs=[tok, tok],
        out_specs=[pl.BlockSpec((1, d), lambda i: (0, 0)), tok],
        compiler_params=_params("arbitrary"),
    )(xo, target)


def sublayer_backward(d_a, bl, name, *, du=None, scale=None, x_in=None, ln=None):
    t, d = d_a.shape
    s = t // bl
    tm = _tile(s, 256, 8)
    nt = s // tm
    has_mod = du is not None
    has_ln = ln is not None
    assert has_mod or has_ln
    assert has_ln or x_in is not None

    def body(*refs):
        refs = list(refs)
        da_ref = refs.pop(0)
        if has_mod:
            du_ref, sc_ref = refs.pop(0), refs.pop(0)
        if has_ln:
            r_ref, y_ref, g_ref, b_ref, gt_ref = (refs.pop(0) for _ in range(5))
        elif has_mod:
            xin_ref = refs.pop(0)
        dx_ref = refs.pop(0)
        if has_ln:
            dy_ref, dg_ref, db_ref, dgt_ref = (refs.pop(0) for _ in range(4))
        if has_mod:
            dsc_ref, dsh_ref = refs.pop(0), refs.pop(0)
        first_tile = pl.program_id(1) == 0
        first_step = jnp.logical_and(pl.program_id(0) == 0, first_tile)

        dout = da_ref[...]
        if has_ln:
            xhat, rstd = _layer_norm_stats(r_ref[...])
        if has_mod:
            duv = du_ref[...]
            dout = dout + duv * (1.0 + sc_ref[0])
            xin = xhat * g_ref[...] + b_ref[...] if has_ln else xin_ref[...]

            @pl.when(first_tile)
            def _():
                dsc_ref[...] = jnp.zeros_like(dsc_ref)
                dsh_ref[...] = jnp.zeros_like(dsh_ref)

            dsc_ref[0] += jnp.sum(duv * xin, axis=0, keepdims=True)
            dsh_ref[0] += jnp.sum(duv, axis=0, keepdims=True)
        if not has_ln:
            dx_ref[...] = dout
            return

        @pl.when(first_step)
        def _():
            dg_ref[...] = jnp.zeros_like(dg_ref)
            db_ref[...] = jnp.zeros_like(db_ref)

        @pl.when(first_tile)
        def _():
            dgt_ref[...] = jnp.zeros_like(dgt_ref)

        dg_ref[...] += jnp.sum(dout * xhat, axis=0, keepdims=True)
        db_ref[...] += jnp.sum(dout, axis=0, keepdims=True)
        dxh = dout * g_ref[...]
        dr = rstd * (dxh - jnp.mean(dxh, axis=-1, keepdims=True) - xhat * jnp.mean(dxh * xhat, axis=-1, keepdims=True))
        dx_ref[...] = ALPHA * dr
        dy_ref[...] = ((1.0 + gt_ref[0]) * dr).astype(BF16)
        dgt_ref[0] += jnp.sum(dr * y_ref[...], axis=0, keepdims=True)

    tok = pl.BlockSpec((tm, d), lambda bb, i: (bb * nt + i, 0))
    vec = pl.BlockSpec((1, d), lambda bb, i: (0, 0))
    seq = pl.BlockSpec((1, 1, d), lambda bb, i: (bb, 0, 0))
    in_specs, args = [tok], [d_a]
    if has_mod:
        in_specs += [tok, _row_spec(d, scale[1])]
        args += [du, scale[0]]
    if has_ln:
        r, y, g, b, gate = ln
        in_specs += [tok, tok, vec, vec, _row_spec(d, gate[1])]
        args += [r, y, g, b, gate[0]]
    elif has_mod:
        in_specs.append(tok)
        args.append(x_in)
    names = ["dx"]
    out_shape, out_specs = [jax.ShapeDtypeStruct((t, d), F32)], [tok]
    if has_ln:
        names += ["dy", "dg", "db", "dgate"]
        out_shape += [jax.ShapeDtypeStruct((t, d), BF16), jax.ShapeDtypeStruct((1, d), F32),
                      jax.ShapeDtypeStruct((1, d), F32), jax.ShapeDtypeStruct((bl, 1, d), F32)]
        out_specs += [tok, vec, vec, seq]
    if has_mod:
        names += ["dscale", "dshift"]
        out_shape += [jax.ShapeDtypeStruct((bl, 1, d), F32)] * 2
        out_specs += [seq, seq]
    outs = pl.pallas_call(
        body, name=name, out_shape=out_shape, grid=(bl, nt), in_specs=in_specs, out_specs=out_specs,
        compiler_params=_params("arbitrary", "arbitrary"),
    )(*args)
    return dict(zip(names, outs))


def _silu(a):
    return a * jax.nn.sigmoid(a)


def silu_rows(a, name):
    def body(a_ref, o_ref):
        o_ref[...] = _silu(a_ref[...]).astype(BF16)

    return pl.pallas_call(body, name=name, out_shape=jax.ShapeDtypeStruct(a.shape, BF16))(a)


def swiglu_forward(a, b, name):
    t, f = a.shape
    tm, tf = _tile(t, 512, 8), _tile(f, 1536)

    def body(a_ref, b_ref, h_ref):
        h_ref[...] = (_silu(a_ref[...]) * b_ref[...]).astype(BF16)

    spec = pl.BlockSpec((tm, tf), lambda i, j: (i, j))
    return pl.pallas_call(
        body, name=name, out_shape=jax.ShapeDtypeStruct((t, f), BF16), grid=(t // tm, f // tf),
        in_specs=[spec, spec], out_specs=spec, compiler_params=_params("arbitrary", "arbitrary"),
    )(a, b)


def swiglu_backward(dh, a, b, name):
    t, f = a.shape
    tm, tf = _tile(t, 512, 8), _tile(f, 1536)

    def body(dh_ref, a_ref, b_ref, da_ref, db_ref):
        av = a_ref[...]
        sig = jax.nn.sigmoid(av)
        dhv = dh_ref[...]
        da_ref[...] = (dhv * b_ref[...] * (sig * (1.0 + av * (1.0 - sig)))).astype(BF16)
        db_ref[...] = (dhv * (av * sig)).astype(BF16)

    spec = pl.BlockSpec((tm, tf), lambda i, j: (i, j))
    return pl.pallas_call(
        body, name=name, out_shape=[jax.ShapeDtypeStruct((t, f), BF16)] * 2, grid=(t // tm, f // tf),
        in_specs=[spec, spec, spec], out_specs=[spec, spec], compiler_params=_params("arbitrary", "arbitrary"),
    )(dh, a, b)


def rope_tables(pos, inv_freq, sign, name):
    t = pos.shape[0]
    tm = _tile(t, 512, 8)

    def body(p_ref, f_ref, s_ref, c_out, s_out):
        ang = p_ref[...] * f_ref[...]
        c_out[...] = jnp.cos(ang)
        s_out[...] = jnp.sin(ang) * s_ref[...]

    vec = pl.BlockSpec((1, LANES), lambda i: (0, 0))
    tab = pl.BlockSpec((tm, LANES), lambda i: (i, 0))
    return pl.pallas_call(
        body, name=name, out_shape=[jax.ShapeDtypeStruct((t, LANES), F32)] * 2, grid=(t // tm,),
        in_specs=[pl.BlockSpec((tm, 1), lambda i: (i, 0)), vec, vec], out_specs=[tab, tab],
        compiler_params=_params("arbitrary"),
    )(pos, inv_freq, sign)


def _rot_half(v):
    lane = lax.broadcasted_iota(jnp.int32, v.shape, v.ndim - 1)
    up = pltpu.roll(v, LANES - MLA_ROPE // 2, v.ndim - 1)
    down = pltpu.roll(v, MLA_ROPE // 2, v.ndim - 1)
    return jnp.where(lane % MLA_ROPE < MLA_ROPE // 2, up, down)


def _rope(v, cos, sin_signed):
    return v * cos + _rot_half(v) * sin_signed


def _rope_transposed(dv, cos, sin_signed):
    return dv * cos + _rot_half(dv * sin_signed)


def rope_slabs(v, cos, sin_signed, out_dtype, name, transposed=False):
    ns, t, _ = v.shape
    tm = _tile(t, 512, 8)
    fn = _rope_transposed if transposed else _rope

    def body(v_ref, c_ref, s_ref, o_ref):
        o_ref[0] = fn(v_ref[0].astype(F32), c_ref[...], s_ref[...]).astype(out_dtype)

    tab = pl.BlockSpec((tm, LANES), lambda j, i: (i, 0))
    spec = pl.BlockSpec((1, tm, LANES), lambda j, i: (j, i, 0))
    return pl.pallas_call(
        body, name=name, out_shape=jax.ShapeDtypeStruct(v.shape, out_dtype), grid=(ns, t // tm),
        in_specs=[spec, tab, tab], out_specs=spec, compiler_params=_params("arbitrary", "arbitrary"),
    )(v, cos, sin_signed)


def _rms(x):
    rinv = lax.rsqrt(jnp.mean(x * x, axis=-1, keepdims=True) + NORM_EPS)
    return x * rinv, rinv


def mla_latents_forward(h_in, g_q, g_kv, cos, sin_signed, name):
    t = h_in.shape[0]
    tm = _tile(t, 512, 8)

    def body(h_ref, gq_ref, gkv_ref, c_ref, s_ref, cq_ref, ckv_ref, kr_ref):
        cq_ref[...] = (_rms(h_ref[:, 0:MLA_QR])[0] * gq_ref[...]).astype(BF16)
        ckv_ref[...] = (_rms(h_ref[:, MLA_QR:MLA_QR + MLA_KVR])[0] * gkv_ref[...]).astype(BF16)
        kr_ref[...] = _rope(h_ref[:, MLA_QR + MLA_KVR:], c_ref[...], s_ref[...]).astype(BF16)

    def tok(w):
        return pl.BlockSpec((tm, w), lambda i: (i, 0))

    def vec(w):
        return pl.BlockSpec((1, w), lambda i: (0, 0))

    return pl.pallas_call(
        body, name=name,
        out_shape=[jax.ShapeDtypeStruct((t, MLA_QR), BF16), jax.ShapeDtypeStruct((t, MLA_KVR), BF16),
                   jax.ShapeDtypeStruct((t, LANES), BF16)],
        grid=(t // tm,),
        in_specs=[tok(h_in.shape[1]), vec(MLA_QR), vec(MLA_KVR), tok(LANES), tok(LANES)],
        out_specs=[tok(MLA_QR), tok(MLA_KVR), tok(LANES)],
        compiler_params=_params("arbitrary"),
    )(h_in, g_q, g_kv, cos, sin_signed)


def mla_latents_backward(h_in, dcq, dckv, dkr, g_q, g_kv, cos, sin_signed, name):
    t, w = h_in.shape
    tm = _tile(t, 512, 8)

    def body(h_ref, dcq_ref, dckv_ref, dkr_ref, gq_ref, gkv_ref, c_ref, s_ref, dh_ref, dgq_ref, dgkv_ref):
        @pl.when(pl.program_id(0) == 0)
        def _():
            dgq_ref[...] = jnp.zeros_like(dgq_ref)
            dgkv_ref[...] = jnp.zeros_like(dgkv_ref)

        def rms_bwd(x, dc, g_ref, dg_ref):
            xn, rinv = _rms(x)
            dg_ref[...] += jnp.sum(dc * xn, axis=0, keepdims=True)
            dxn = dc * g_ref[...]
            return rinv * (dxn - xn * jnp.mean(dxn * xn, axis=-1, keepdims=True))

        dq = rms_bwd(h_ref[:, 0:MLA_QR], dcq_ref[...], gq_ref, dgq_ref)
        dkv = rms_bwd(h_ref[:, MLA_QR:MLA_QR + MLA_KVR], dckv_ref[...], gkv_ref, dgkv_ref)
        dr = _rope_transposed(dkr_ref[...], c_ref[...], s_ref[...])
        dh_ref[...] = jnp.concatenate([dq, dkv, dr], axis=1).astype(BF16)

    def tok(ww):
        return pl.BlockSpec((tm, ww), lambda i: (i, 0))

    def vec(ww):
        return pl.BlockSpec((1, ww), lambda i: (0, 0))

    return pl.pallas_call(
        body, name=name,
        out_shape=[jax.ShapeDtypeStruct((t, w), BF16), jax.ShapeDtypeStruct((1, MLA_QR), F32),
                   jax.ShapeDtypeStruct((1, MLA_KVR), F32)],
        grid=(t // tm,),
        in_specs=[tok(w), tok(MLA_QR), tok(MLA_KVR), tok(LANES), vec(MLA_QR), vec(MLA_KVR), tok(LANES), tok(LANES)],
        out_specs=[tok(w), vec(MLA_QR), vec(MLA_KVR)],
        compiler_params=_params("arbitrary"),
    )(h_in, dcq, dckv, dkr, g_q, g_kv, cos, sin_signed)


def _tri(n, lower):
    r = lax.broadcasted_iota(jnp.int32, (n, n), 0)
    c = lax.broadcasted_iota(jnp.int32, (n, n), 1)
    return jnp.where(r >= c if lower else r <= c, 1.0, 0.0).astype(F32)


def _dot_exact(tri, v):
    hi = v.astype(BF16)
    mid = (v - hi.astype(F32)).astype(BF16)
    lo = (v - hi.astype(F32) - mid.astype(F32)).astype(BF16)
    t = tri.astype(BF16)
    return _nn(t, hi) + _nn(t, mid) + _nn(t, lo)


def fox_gate_forward(z, b_f, bl, name):
    t = z.shape[0]
    s = t // bl
    ch = LANES
    n_ch = s // ch

    def body(z_ref, b_ref, f_ref, fs_ref):
        tri = _tri(ch, True)
        carry = jnp.zeros((1, LANES), F32)
        for k in range(n_ch):
            x = z_ref[k * ch:(k + 1) * ch, :] + b_ref[...]
            logf = jnp.minimum(x, 0.0) - jnp.log(1.0 + jnp.exp(-jnp.abs(x)))
            cs = _dot_exact(tri, logf) + carry
            carry = cs[ch - 1:ch, :]
            f_ref[k * ch:(k + 1) * ch, :] = cs
            for j in range(FOX_HEADS // 2):
                fs_ref[j, k * ch:(k + 1) * ch, :] = jnp.concatenate(
                    [jnp.broadcast_to(cs[:, 2 * j:2 * j + 1], (ch, FOX_HD)),
                     jnp.broadcast_to(cs[:, 2 * j + 1:2 * j + 2], (ch, FOX_HD))], axis=1)

    return pl.pallas_call(
        body, name=name,
        out_shape=[jax.ShapeDtypeStruct((t, LANES), F32), jax.ShapeDtypeStruct((FOX_HEADS // 2, t, LANES), F32)],
        grid=(bl,),
        in_specs=[pl.BlockSpec((s, LANES), lambda b: (b, 0)), pl.BlockSpec((1, LANES), lambda b: (0, 0))],
        out_specs=[pl.BlockSpec((s, LANES), lambda b: (b, 0)),
                   pl.BlockSpec((FOX_HEADS // 2, s, LANES), lambda b: (0, b, 0))],
        compiler_params=_params("arbitrary"),
    )(z, b_f)


def fox_gate_backward(z, b_f, df, bl, name):
    t = z.shape[0]
    s = t // bl
    ch = LANES
    n_ch = s // ch

    def body(z_ref, b_ref, df_ref, dz_ref, db_ref):
        @pl.when(pl.program_id(0) == 0)
        def _():
            db_ref[...] = jnp.zeros_like(db_ref)

        tri = _tri(ch, False)
        carry = jnp.zeros((1, LANES), F32)
        for k in reversed(range(n_ch)):
            cs = _dot_exact(tri, df_ref[k * ch:(k + 1) * ch, :]) + carry
            carry = cs[0:1, :]
            x = z_ref[k * ch:(k + 1) * ch, :] + b_ref[...]
            dz = cs * (1.0 - jax.nn.sigmoid(x))
            dz_ref[k * ch:(k + 1) * ch, :] = dz
            db_ref[...] += jnp.sum(dz, axis=0, keepdims=True)

    tok = pl.BlockSpec((s, LANES), lambda b: (b, 0))
    vec = pl.BlockSpec((1, LANES), lambda b: (0, 0))
    return pl.pallas_call(
        body, name=name,
        out_shape=[jax.ShapeDtypeStruct((t, LANES), F32), jax.ShapeDtypeStruct((1, LANES), F32)],
        grid=(bl,), in_specs=[tok, vec, tok], out_specs=[tok, vec],
        compiler_params=_params("arbitrary"),
    )(z, b_f, df)


NEG_INF = float("-inf")


def _attn_tiles(s):
    return _tile(s, 256, 8)


def attention_forward(kind, ops, bl, scale, name):
    fox = kind == "fox"
    if fox:
        qkv, fq, fk = ops
        t = qkv.shape[1]
        n_pair = FOX_HEADS // 2
    else:
        qn, qr, kn, kr, v = ops
        t = qn.shape[1]
        n_pair = MLA_HEADS // 2
    s = t // bl
    tq = _attn_tiles(s)
    nq = s // tq
    half = LANES // 2

    def body(*refs):
        if fox:
            q_ref, k_ref, v_ref, fq_ref, fk_ref, o_ref, lse_ref, o32_ref = refs
        else:
            qn_ref, qr_ref, kn_ref, kr_ref, v_ref, o_ref, lse_ref = refs
        i = pl.program_id(2)
        row = lax.broadcasted_iota(jnp.int32, (tq, tq), 0)
        col = lax.broadcasted_iota(jnp.int32, (tq, tq), 1)
        outs, lses = [], []
        for e in range(2):
            sl = slice(e * half, (e + 1) * half)
            if fox:
                q = q_ref[0, :, sl]
                f_q = fq_ref[0, :, e * half:e * half + 1]
            else:
                q_n = qn_ref[e]
                q_r = qr_ref[0, :, sl]
            dv = half if fox else LANES

            def step(j, carry, masked):
                m, l, acc = carry
                rows = pl.ds(pl.multiple_of(j * tq, tq), tq)
                if fox:
                    sc = _nt(q, k_ref[0, rows, sl]) * scale + f_q - fk_ref[0, j, e:e + 1, :]
                    vv = v_ref[0, rows, sl]
                else:
                    sc = (_nt(q_n, kn_ref[e, rows, :]) + _nt(q_r, kr_ref[rows, 0:half])) * scale
                    vv = v_ref[e, rows, :]
                if masked:
                    sc = jnp.where(row >= col, sc, NEG_INF)
                m_new = jnp.maximum(m, jnp.max(sc, axis=1, keepdims=True))
                p = jnp.exp(sc - m_new)
                a = jnp.exp(m - m_new)
                l = a * l + jnp.sum(p, axis=1, keepdims=True)
                p_hi = p.astype(BF16)
                acc = a * acc + _nn(p_hi, vv)
                if fox:
                    acc = acc + _nn((p - p_hi.astype(F32)).astype(BF16), vv)
                return m_new, l, acc

            init = (jnp.full((tq, 1), NEG_INF, F32), jnp.zeros((tq, 1), F32), jnp.zeros((tq, dv), F32))
            carry = step(i, init, True)
            m, l, acc = lax.fori_loop(0, i, lambda j, c: step(j, c, False), carry)
            outs.append(acc / l)
            lses.append(jnp.broadcast_to(m + jnp.log(l), (tq, half)))
        if fox:
            o32 = jnp.concatenate(outs, axis=1)
            o32_ref[0] = o32
            o_ref[0] = o32.astype(BF16)
        else:
            o_ref[0] = outs[0].astype(BF16)
            o_ref[1] = outs[1].astype(BF16)
        lse_ref[0] = jnp.concatenate(lses, axis=1)

    def q_idx(b, g, i):
        return (g, b * nq + i, 0)

    if fox:
        nk = fk.shape[1]
        in_specs = [pl.BlockSpec((1, tq, LANES), q_idx),
                    pl.BlockSpec((1, s, LANES), lambda b, g, i: (n_pair + g, b, 0)),
                    pl.BlockSpec((1, s, LANES), lambda b, g, i: (2 * n_pair + g, b, 0)),
                    pl.BlockSpec((1, tq, LANES), q_idx),
                    pl.BlockSpec((1, nk, 8, tq), lambda b, g, i: (b * n_pair + g, 0, 0, 0))]
        args = [qkv, qkv, qkv, fq, fk]
        o_spec = pl.BlockSpec((1, tq, LANES), q_idx)
    else:
        in_specs = [pl.BlockSpec((2, tq, LANES), q_idx),
                    pl.BlockSpec((1, tq, LANES), q_idx),
                    pl.BlockSpec((2, s, LANES), lambda b, g, i: (g, b, 0)),
                    pl.BlockSpec((s, LANES), lambda b, g, i: (b, 0)),
                    pl.BlockSpec((2, s, LANES), lambda b, g, i: (g, b, 0))]
        args = [qn, qr, kn, kr, v]
        o_spec = pl.BlockSpec((2, tq, LANES), q_idx)
    out_shape = [jax.ShapeDtypeStruct((8, t, LANES), BF16), jax.ShapeDtypeStruct((n_pair, t, LANES), F32)]
    out_specs = [o_spec, pl.BlockSpec((1, tq, LANES), q_idx)]
    if fox:
        out_shape.append(jax.ShapeDtypeStruct((8, t, LANES), F32))
        out_specs.append(o_spec)
    outs = pl.pallas_call(
        body, name=name, out_shape=out_shape, grid=(bl, n_pair, nq), in_specs=in_specs, out_specs=out_specs,
        compiler_params=_params("arbitrary", "arbitrary", "arbitrary"),
    )(*args)
    return (outs[0], outs[1], outs[2] if fox else outs[0])


def attention_backward(kind, ops, o, do, lse, bl, scale, name):
    fox = kind == "fox"
    if fox:
        qkv, fq, fk = ops
        t = qkv.shape[1]
        n_pair = FOX_HEADS // 2
    else:
        qn, qr, kn, kr, v = ops
        t = qn.shape[1]
        n_pair = MLA_HEADS // 2
    s = t // bl
    tq = _attn_tiles(s)
    nq = s // tq
    half = LANES // 2

    def body(*refs):
        if fox:
            (q_ref, k_ref, v_ref, fq_ref, fk_ref, o_ref, do_ref, lse_ref,
             dq_ref, dk_ref, dv_ref, dfk_ref) = refs
        else:
            (qn_ref, qr_ref, kn_ref, kr_ref, v_ref, o_ref, do_ref, lse_ref,
             dqn_ref, dqr_ref, dkn_ref, dv_ref, dkr_ref) = refs
        g, j = pl.program_id(1), pl.program_id(2)
        row = lax.broadcasted_iota(jnp.int32, (tq, tq), 0)
        col = lax.broadcasted_iota(jnp.int32, (tq, tq), 1)
        krows = pl.ds(pl.multiple_of(j * tq, tq), tq)

        @pl.when(j == 0)
        def _():
            if fox:
                dq_ref[...] = jnp.zeros_like(dq_ref)
            else:
                dqn_ref[...] = jnp.zeros_like(dqn_ref)
                dqr_ref[...] = jnp.zeros_like(dqr_ref)

        if fox:
            dfk_ref[...] = jnp.zeros_like(dfk_ref)
        else:
            @pl.when(jnp.logical_and(g == 0, j == 0))
            def _():
                dkr_ref[...] = jnp.zeros_like(dkr_ref)

        dks, dvs = [], []
        dkr_sum = None
        for e in range(2):
            sl = slice(e * half, (e + 1) * half)
            if fox:
                k_e, v_e = k_ref[0, :, sl], v_ref[0, :, sl]
                f_k = fk_ref[0, 0, e:e + 1, :]
                dk_w = dv_w = half
            else:
                kn_e, v_e = kn_ref[e], v_ref[e]
                kr_e = kr_ref[krows, 0:half]
                dk_w = dv_w = LANES

            def step(i, carry, masked):
                rows = pl.ds(pl.multiple_of(i * tq, tq), tq)
                if fox:
                    dk_acc, dv_acc, dfk_acc = carry
                    q_i = q_ref[0, rows, sl]
                    do_i, o_i = do_ref[0, rows, sl], o_ref[0, rows, sl]
                    sc = _nt(q_i, k_e) * scale + fq_ref[0, rows, e * half:e * half + 1] - f_k
                else:
                    dk_acc, dv_acc, dkr_acc = carry
                    qn_i, qr_i = qn_ref[e, rows, :], qr_ref[0, rows, sl]
                    do_i, o_i = do_ref[e, rows, :], o_ref[e, rows, :]
                    sc = (_nt(qn_i, kn_e) + _nt(qr_i, kr_e)) * scale
                if masked:
                    sc = jnp.where(row >= col, sc, NEG_INF)
                p = jnp.exp(sc - lse_ref[0, rows, e * half:e * half + 1])
                dp = _nt(do_i, v_e)
                delta = jnp.sum(do_i.astype(F32) * o_i.astype(F32), axis=1, keepdims=True)
                ds = p * (dp - delta)
                dsb = (ds * scale).astype(BF16)
                dv_acc = dv_acc + _tn(p.astype(BF16), do_i)
                if fox:
                    dk_acc = dk_acc + _tn(dsb, q_i)
                    dq_ref[0, rows, sl] += _nn(dsb, k_e)
                    return dk_acc, dv_acc, dfk_acc - jnp.sum(ds, axis=0, keepdims=True)
                dk_acc = dk_acc + _tn(dsb, qn_i)
                dqn_ref[e, rows, :] += _nn(dsb, kn_e)
                dqr_ref[0, rows, sl] += _nn(dsb, kr_e)
                return dk_acc, dv_acc, dkr_acc + _tn(dsb, qr_i)

            last = jnp.zeros((1, tq), F32) if fox else jnp.zeros((tq, half), F32)
            carry = (jnp.zeros((tq, dk_w), F32), jnp.zeros((tq, dv_w), F32), last)
            carry = step(j, carry, True)
            dk_acc, dv_acc, last = lax.fori_loop(j + 1, nq, lambda i, c: step(i, c, False), carry)
            dks.append(dk_acc)
            dvs.append(dv_acc)
            if fox:
                dfk_ref[0, 0, e:e + 1, :] = last
            else:
                dkr_sum = last if dkr_sum is None else dkr_sum + last
        if fox:
            dk_ref[0] = jnp.concatenate(dks, axis=1).astype(BF16)
            dv_ref[0] = jnp.concatenate(dvs, axis=1).astype(BF16)
        else:
            for e in range(2):
                dkn_ref[e] = dks[e].astype(BF16)
                dv_ref[e] = dvs[e].astype(BF16)
            dkr_ref[krows, 0:half] += dkr_sum

    def whole(b, g, j):
        return (g, b, 0)

    def kblk(b, g, j):
        return (g, b * nq + j, 0)

    if fox:
        in_specs = [pl.BlockSpec((1, s, LANES), whole),
                    pl.BlockSpec((1, tq, LANES), lambda b, g, j: (n_pair + g, b * nq + j, 0)),
                    pl.BlockSpec((1, tq, LANES), lambda b, g, j: (2 * n_pair + g, b * nq + j, 0)),
                    pl.BlockSpec((1, s, LANES), whole),
                    pl.BlockSpec((1, 1, 8, tq), lambda b, g, j: (b * n_pair + g, j, 0, 0)),
                    pl.BlockSpec((1, s, LANES), whole), pl.BlockSpec((1, s, LANES), whole),
                    pl.BlockSpec((1, s, LANES), whole)]
        args = [qkv, qkv, qkv, fq, fk, o, do, lse]
        out_shape = [jax.ShapeDtypeStruct((8, t, LANES), F32), jax.ShapeDtypeStruct((8, t, LANES), BF16),
                     jax.ShapeDtypeStruct((8, t, LANES), BF16), jax.ShapeDtypeStruct(fk.shape, F32)]
        out_specs = [pl.BlockSpec((1, s, LANES), whole), pl.BlockSpec((1, tq, LANES), kblk),
                     pl.BlockSpec((1, tq, LANES), kblk),
                     pl.BlockSpec((1, 1, 8, tq), lambda b, g, j: (b * n_pair + g, j, 0, 0))]
    else:
        pair = pl.BlockSpec((2, s, LANES), whole)
        pair_k = pl.BlockSpec((2, tq, LANES), kblk)
        in_specs = [pair, pl.BlockSpec((1, s, LANES), whole), pair_k,
                    pl.BlockSpec((s, LANES), lambda b, g, j: (b, 0)), pair_k,
                    pair, pair, pl.BlockSpec((1, s, LANES), whole)]
        args = [qn, qr, kn, kr, v, o, do, lse]
        out_shape = [jax.ShapeDtypeStruct((8, t, LANES), F32), jax.ShapeDtypeStruct((4, t, LANES), F32),
                     jax.ShapeDtypeStruct((8, t, LANES), BF16), jax.ShapeDtypeStruct((8, t, LANES), BF16),
                     jax.ShapeDtypeStruct((t, LANES), F32)]
        out_specs = [pair, pl.BlockSpec((1, s, LANES), whole), pair_k, pair_k,
                     pl.BlockSpec((s, LANES), lambda b, g, j: (b, 0))]
    return pl.pallas_call(
        body, name=name, out_shape=out_shape, grid=(bl, n_pair, nq), in_specs=in_specs, out_specs=out_specs,
        compiler_params=_params("arbitrary", "arbitrary", "arbitrary"),
    )(*args)


def adamw(w, g, m, v, name):
    shape = w.shape
    c = shape[-1]
    r = w.size // c
    tr = _tile(r, 512, 8)

    def body(w_ref, g_ref, m_ref, v_ref, d_ref, nm_ref, nv_ref):
        gv = g_ref[...]
        m2 = ADAM_B1 * m_ref[...] + (1.0 - ADAM_B1) * gv
        v2 = ADAM_B2 * v_ref[...] + (1.0 - ADAM_B2) * (gv * gv)
        m_hat = m2 / (1.0 - ADAM_B1 ** ADAM_STEP)
        v_hat = v2 / (1.0 - ADAM_B2 ** ADAM_STEP)
        d_ref[...] = -ADAM_LR * (m_hat / (jnp.sqrt(v_hat) + ADAM_EPS) + ADAM_WD * w_ref[...])
        nm_ref[...] = m2
        nv_ref[...] = v2

    spec = pl.BlockSpec((tr, c), lambda i: (i, 0))
    outs = pl.pallas_call(
        body, name=name, out_shape=[jax.ShapeDtypeStruct((r, c), F32)] * 3, grid=(r // tr,),
        in_specs=[spec] * 4, out_specs=[spec] * 3, compiler_params=_params("arbitrary"),
    )(*(a.reshape(r, c) for a in (w, g, m, v)))
    return tuple(a.reshape(shape) for a in outs)


PACK_COLS = 1024


def _pack_rows(a):
    return a.reshape(-1, PACK_COLS)


def kernel(x, c, positions, mla_w_in, mla_g_q, mla_w_uq, mla_g_kv, mla_w_uk, mla_w_uv, mla_w_o, fox_w_in, fox_b_f, fox_w_o, ada_w, ada_b, ffn_w_gate, ffn_w_up, ffn_w_down, ln_g, ln_b, loss_target, m_mla_w_in, m_mla_g_q, m_mla_w_uq, m_mla_g_kv, m_mla_w_uk, m_mla_w_uv, m_mla_w_o, m_fox_w_in, m_fox_b_f, m_fox_w_o, m_ada_w, m_ada_b, m_ffn_w_gate, m_ffn_w_up, m_ffn_w_down, m_ln_g, m_ln_b, v_mla_w_in, v_mla_g_q, v_mla_w_uq, v_mla_g_kv, v_mla_w_uk, v_mla_w_uv, v_mla_w_o, v_fox_w_in, v_fox_b_f, v_fox_w_o, v_ada_w, v_ada_b, v_ffn_w_gate, v_ffn_w_up, v_ffn_w_down, v_ln_g, v_ln_b):
    bl, s, d = x.shape
    t = bl * s
    ff = ffn_w_gate.shape[-1] * N_DEV
    dev = 4 * lax.axis_index("x") + 2 * lax.axis_index("y") + lax.axis_index("c")
    ada_cols = ada_w.shape[-1]
    fox_in = fox_w_in.shape[-1] * N_DEV
    mla_in = mla_w_in.shape[-1]
    mla_in_pad = mla_in + (-mla_in) % LANES

    def t_last(a):
        return jnp.swapaxes(a, -1, -2)

    local = [
        ("mla_w_in", mla_w_in[0]),
        ("mla_w_uq", t_last(mla_w_uq[0])),
        ("mla_w_uk", t_last(mla_w_uk[0])),
        ("mla_w_uv", t_last(mla_w_uv[0])),
        ("mla_w_o", mla_w_o[0]),
        ("fox_w_in", t_last(fox_w_in[0])),
        ("fox_w_o", fox_w_o[0]),
    ]
    for i in range(DEPTH):
        local += [(f"gate{i}", t_last(ffn_w_gate[i])), (f"up{i}", t_last(ffn_w_up[i])), (f"down{i}", ffn_w_down[i])]
    offsets, rows_of, slot_of = {}, {}, {}
    pack_rows = 0
    for nm, a in local:
        rows_of[nm] = a.size // PACK_COLS
        slot_of[nm] = rows_of[nm] + (-rows_of[nm]) % 16
        offsets[nm] = pack_rows
        pack_rows += slot_of[nm]

    def slot(nm, rows):
        pad = [(0, 0)] * rows.ndim
        pad[-2] = (0, slot_of[nm] - rows_of[nm])
        return jnp.pad(rows, pad)

    packed = jnp.concatenate([slot(nm, _pack_rows(a).astype(BF16)) for nm, a in local], axis=0)
    gathered = all_gather(packed, "gather_weights")

    def full(nm, cols):
        blk = gathered[:, offsets[nm]:offsets[nm] + rows_of[nm], :]
        return blk.reshape(-1, cols)

    w_in = jnp.pad(full("mla_w_in", mla_in), ((0, 0), (0, mla_in_pad - mla_in)))
    wt_uq = full("mla_w_uq", MLA_QR).reshape(MLA_HEADS, MLA_NOPE + MLA_ROPE, MLA_QR)
    wt_uq_n = wt_uq[:, :MLA_NOPE].reshape(MLA_HEADS * MLA_NOPE, MLA_QR)
    wt_uq_r = wt_uq[:, MLA_NOPE:].reshape(MLA_HEADS * MLA_ROPE, MLA_QR)
    wt_uk = full("mla_w_uk", MLA_KVR)
    wt_uv = full("mla_w_uv", MLA_KVR)
    w_mo = full("mla_w_o", d)
    wt_fox = full("fox_w_in", d)
    wt_qkv = wt_fox[:3 * d]
    wt_f = jnp.pad(wt_fox[3 * d:], ((0, LANES - FOX_HEADS), (0, 0)))
    w_fo = full("fox_w_o", d)
    wt_gate = [full(f"gate{i}", d) for i in range(DEPTH)]
    wt_up = [full(f"up{i}", d) for i in range(DEPTH)]
    w_down = [full(f"down{i}", d) for i in range(DEPTH)]

    small = jnp.concatenate([c.reshape(-1, LANES), ln_g.reshape(-1, LANES), ln_b.reshape(-1, LANES)], axis=0)
    small_rows = small.shape[0]
    small = jnp.pad(small, ((0, (-small_rows) % 8), (0, 0)))
    small_all = all_gather(small, "gather_small")
    c_rows = bl * d // LANES
    c_all = small_all[:, :c_rows].reshape(N_DEV * bl, d)
    n_ln = DEPTH * 2
    ln_g_all = small_all[:, c_rows:c_rows + n_ln, :].transpose(1, 0, 2).reshape(DEPTH, 2, 1, d)
    ln_b_all = small_all[:, c_rows + n_ln:c_rows + 2 * n_ln, :].transpose(1, 0, 2).reshape(DEPTH, 2, 1, d)

    c_act = silu_rows(c_all, "silu_c")
    ada_b_loc = lax.dynamic_slice_in_dim(ada_b, dev * ada_cols, ada_cols, axis=1)
    mod_cols = [mm([(c_act, ada_w[i])], trans_b=False, out_dtype=F32, name=f"ada_fwd{i}", bias=ada_b_loc[i][None, :])
                for i in range(DEPTH)]
    mod_all = all_gather(jnp.concatenate(mod_cols, axis=0), "gather_mod")
    mod_all = mod_all.reshape(N_DEV, DEPTH, N_DEV * bl, ada_cols).transpose(1, 2, 0, 3).reshape(DEPTH, N_DEV * bl, 6 * d)
    mod_mine = lax.dynamic_slice_in_dim(mod_all, dev * bl, bl, axis=1)
    mods = [mod_mine[i].reshape(bl * 6, 1, d) for i in range(DEPTH)]

    half_r = MLA_ROPE // 2
    inv_freq = ROPE_THETA ** (-jnp.arange(half_r, dtype=F32) / half_r)
    inv_freq = jnp.tile(inv_freq, LANES // half_r)[None, :]
    sign = jnp.tile(jnp.concatenate([-jnp.ones((half_r,), F32), jnp.ones((half_r,), F32)]), LANES // MLA_ROPE)[None, :]
    cos_t, sin_t = rope_tables(positions.astype(F32).reshape(t, 1), inv_freq, sign, "rope_tables")

    x2d = x.reshape(t, d)
    g_q, g_kv = mla_g_q.reshape(1, MLA_QR), mla_g_kv.reshape(1, MLA_KVR)
    b_f = jnp.pad(fox_b_f.reshape(1, FOX_HEADS), ((0, 0), (0, LANES - FOX_HEADS)))
    mla_scale = (MLA_NOPE + MLA_ROPE) ** -0.5
    fox_scale = FOX_HD ** -0.5
    tq = _attn_tiles(s)
    nk = s // tq

    saved = []
    u = modulate(x2d, mods[0], 0, 1, bl, "modulate0")
    xin = x2d
    for i in range(DEPTH):
        sv = {"u": u, "x_in": xin}
        if i % 2 == 0:
            h_in = mm([(u, w_in)], trans_b=False, out_dtype=F32, name=f"mla_in{i}")
            c_q, c_kv, k_r = mla_latents_forward(h_in, g_q, g_kv, cos_t, sin_t, f"mla_latents{i}")
            q_n = mm([(c_q, wt_uq_n)], trans_b=True, out_dtype=BF16, out_slab=True, name=f"mla_qn{i}")
            q_r_raw = mm([(c_q, wt_uq_r)], trans_b=True, out_dtype=F32, out_slab=True, name=f"mla_qr{i}")
            q_r = rope_slabs(q_r_raw, cos_t, sin_t, BF16, f"mla_qrope{i}")
            k_n = mm([(c_kv, wt_uk)], trans_b=True, out_dtype=BF16, out_slab=True, name=f"mla_kn{i}")
            v_m = mm([(c_kv, wt_uv)], trans_b=True, out_dtype=BF16, out_slab=True, name=f"mla_v{i}")
            ops = (q_n, q_r, k_n, k_r, v_m)
            o, lse, o_delta = attention_forward("mla", ops, bl, mla_scale, f"mla_attn{i}")
            y = mm([(o, w_mo)], trans_b=False, out_dtype=F32, name=f"mla_out{i}")
            sv.update(h_in=h_in, c_q=c_q, c_kv=c_kv, ops=ops, o=o, lse=lse, o_delta=o_delta)
        else:
            qkv = mm([(u, wt_qkv)], trans_b=True, out_dtype=BF16, out_slab=True, name=f"fox_qkv{i}")
            z = mm([(u, wt_f)], trans_b=True, out_dtype=F32, name=f"fox_z{i}")
            f_tok, f_q = fox_gate_forward(z, b_f, bl, f"fox_gate{i}")
            f_k = f_tok[:, :FOX_HEADS].reshape(bl, nk, tq, FOX_HEADS // 2, 2).transpose(0, 3, 1, 4, 2)
            f_k = jnp.pad(f_k.reshape(bl * FOX_HEADS // 2, nk, 2, tq), ((0, 0), (0, 0), (0, 6), (0, 0)))
            ops = (qkv, f_q, f_k)
            o, lse, o_delta = attention_forward("fox", ops, bl, fox_scale, f"fox_attn{i}")
            y = mm([(o, w_fo)], trans_b=False, out_dtype=F32, name=f"fox_out{i}")
            sv.update(z=z, ops=ops, o=o, lse=lse, o_delta=o_delta)
        x1, r1, u2 = residual_layer_norm(xin, y, mods[i], 2, ln_g_all[i, 0], ln_b_all[i, 0], bl, f"ln_mix{i}",
                                         next_mod=(3, 4))
        a = mm([(u2, wt_gate[i])], trans_b=True, out_dtype=F32, name=f"ffn_gate{i}")
        bb = mm([(u2, wt_up[i])], trans_b=True, out_dtype=F32, name=f"ffn_up{i}")
        h = swiglu_forward(a, bb, f"swiglu{i}")
        y2 = mm([(h, w_down[i])], trans_b=False, out_dtype=F32, name=f"ffn_down{i}")
        sv.update(y=y, r1=r1, u2=u2, a=a, bb=bb, h=h, y2=y2)
        if i + 1 < DEPTH:
            xin, r2, u = residual_layer_norm(x1, y2, mods[i], 5, ln_g_all[i, 1], ln_b_all[i, 1], bl, f"ln_ffn{i}",
                                             next_mod=(0, 1, mods[i + 1]))
        else:
            xin, r2 = residual_layer_norm(x1, y2, mods[i], 5, ln_g_all[i, 1], ln_b_all[i, 1], bl, f"ln_ffn{i}")
        sv.update(r2=r2)
        saved.append(sv)

    loss_cols, d_x = loss_head(xin, loss_target.reshape(t, d), "loss_head")
    loss = lax.psum(jnp.sum(loss_cols), MESH_AXES)

    grads_full = {}
    dmod = [[None] * 6 for _ in range(DEPTH)]
    dg_ln = [[None, None] for _ in range(DEPTH)]
    db_ln = [[None, None] for _ in range(DEPTH)]
    dg_q = dg_kv = db_f = None
    d_a, du = d_x, None
    for i in reversed(range(DEPTH)):
        sv = saved[i]
        ln2 = (sv["r2"], sv["y2"], ln_g_all[i, 1], ln_b_all[i, 1], (mods[i], 5))
        if du is None:
            bw = sublayer_backward(d_a, bl, f"bwd_ln_ffn{i}", ln=ln2)
        else:
            bw = sublayer_backward(d_a, bl, f"bwd_ln_ffn{i}", du=du, scale=(mods[i + 1], 1), ln=ln2)
            dmod[i + 1][0], dmod[i + 1][1] = bw["dshift"], bw["dscale"]
        dmod[i][5], dg_ln[i][1], db_ln[i][1] = bw["dgate"], bw["dg"], bw["db"]
        dy2 = bw["dy"]
        dh = mm([(dy2, w_down[i])], trans_b=True, out_dtype=F32, name=f"bwd_ffn_dh{i}")
        da, dbb = swiglu_backward(dh, sv["a"], sv["bb"], f"bwd_swiglu{i}")
        du2 = mm([(da, wt_gate[i]), (dbb, wt_up[i])], trans_b=False, out_dtype=F32, name=f"bwd_ffn_du{i}")
        grads_full[f"down{i}"] = mm_tn(sv["h"], dy2, name=f"bwd_w_down{i}")
        grads_full[f"gate{i}"] = mm_tn(da, sv["u2"], name=f"bwd_w_gate{i}")
        grads_full[f"up{i}"] = mm_tn(dbb, sv["u2"], name=f"bwd_w_up{i}")
        bw = sublayer_backward(bw["dx"], bl, f"bwd_ln_mix{i}", du=du2, scale=(mods[i], 4),
                               ln=(sv["r1"], sv["y"], ln_g_all[i, 0], ln_b_all[i, 0], (mods[i], 2)))
        dmod[i][3], dmod[i][4], dmod[i][2] = bw["dshift"], bw["dscale"], bw["dgate"]
        dg_ln[i][0], db_ln[i][0] = bw["dg"], bw["db"]
        d_a, dy = bw["dx"], bw["dy"]
        o, lse, ops = sv["o"], sv["lse"], sv["ops"]
        if i % 2 == 0:
            do = mm([(dy, w_mo)], trans_b=True, out_dtype=BF16, out_slab=True, name=f"bwd_mla_do{i}")
            grads_full["mla_w_o"] = mm_tn(o, dy, name=f"bwd_w_mla_o{i}")
            dqn, dqr, dkn, dvm, dkr = attention_backward("mla", ops, sv["o_delta"], do, lse, bl, mla_scale,
                                                         f"bwd_mla_attn{i}")
            dqr = rope_slabs(dqr, cos_t, sin_t, F32, f"bwd_mla_qrope{i}", transposed=True)
            dcq = mm([(dqn, wt_uq_n), (dqr, wt_uq_r)], trans_b=False, out_dtype=F32, name=f"bwd_mla_dcq{i}")
            dckv = mm([(dkn, wt_uk), (dvm, wt_uv)], trans_b=False, out_dtype=F32, name=f"bwd_mla_dckv{i}")
            d_uq_n = mm_tn(dqn, sv["c_q"], name=f"bwd_w_uq_n{i}").reshape(MLA_HEADS, MLA_NOPE, MLA_QR)
            d_uq_r = mm_tn(dqr, sv["c_q"], name=f"bwd_w_uq_r{i}").reshape(MLA_HEADS, MLA_ROPE, MLA_QR)
            grads_full["mla_w_uq"] = jnp.concatenate([d_uq_n, d_uq_r], axis=1)
            grads_full["mla_w_uk"] = mm_tn(dkn, sv["c_kv"], name=f"bwd_w_uk{i}")
            grads_full["mla_w_uv"] = mm_tn(dvm, sv["c_kv"], name=f"bwd_w_uv{i}")
            dh_in, dg_q, dg_kv = mla_latents_backward(sv["h_in"], dcq, dckv, dkr, g_q, g_kv, cos_t, sin_t,
                                                      f"bwd_mla_latents{i}")
            du = mm([(dh_in, w_in)], trans_b=True, out_dtype=F32, name=f"bwd_mla_du{i}")
            grads_full["mla_w_in"] = mm_tn(sv["u"], dh_in, name=f"bwd_w_mla_in{i}")[:, :mla_in]
        else:
            do = mm([(dy, w_fo)], trans_b=True, out_dtype=BF16, out_slab=True, name=f"bwd_fox_do{i}")
            grads_full["fox_w_o"] = mm_tn(o, dy, name=f"bwd_w_fox_o{i}")
            dq, dk, dvf, dfk = attention_backward("fox", ops, sv["o_delta"], do, lse, bl, fox_scale, f"bwd_fox_attn{i}")
            df = dfk[:, :, :2, :].reshape(bl, FOX_HEADS // 2, nk, 2, tq).transpose(0, 2, 4, 1, 3).reshape(t, FOX_HEADS)
            df = jnp.pad(df, ((0, 0), (0, LANES - FOX_HEADS)))
            dz, db_f = fox_gate_backward(sv["z"], b_f, df, bl, f"bwd_fox_gate{i}")
            du = mm([(dq, wt_fox[0:d]), (dk, wt_fox[d:2 * d]), (dvf, wt_fox[2 * d:3 * d]), (dz, wt_f)],
                    trans_b=False, out_dtype=F32, name=f"bwd_fox_du{i}")
            u_f = sv["u"]
            grads_full["fox_w_in"] = jnp.concatenate(
                [mm_tn(dq, u_f, name=f"bwd_w_fox_q{i}"), mm_tn(dk, u_f, name=f"bwd_w_fox_k{i}"),
                 mm_tn(dvf, u_f, name=f"bwd_w_fox_v{i}"), mm_tn(dz, u_f, name=f"bwd_w_fox_f{i}")[:FOX_HEADS]], axis=0)
    bw = sublayer_backward(d_a, bl, "bwd_input", du=du, scale=(mods[0], 1), x_in=x2d)
    dmod[0][0], dmod[0][1] = bw["dshift"], bw["dscale"]
    grad_x = bw["dx"].reshape(bl, s, d)

    dmod_rows = jnp.concatenate([r.reshape(bl, d) for layer in dmod for r in layer], axis=0)
    dmod_rows = dmod_rows.reshape(DEPTH, 6, bl, d).transpose(0, 2, 1, 3)
    n_mod = dmod_rows.size // LANES
    ln_parts = [dg_ln[i][k] for i in range(DEPTH) for k in range(2)] + [db_ln[i][k] for i in range(DEPTH) for k in range(2)]
    small_g = jnp.concatenate([dmod_rows.reshape(-1, LANES), dg_q.reshape(-1, LANES), dg_kv.reshape(-1, LANES), db_f]
                              + [p.reshape(-1, LANES) for p in ln_parts], axis=0)
    n_small = small_g.shape[0]
    small_g = jnp.pad(small_g, ((0, (-n_small) % 8), (0, 0)))
    small_g_all = all_gather(small_g, "gather_small_grads")
    small_sum = sum_leading(small_g_all, "sum_small_grads")
    per_seq = DEPTH * 6 * d // LANES
    dmod_all = small_g_all[:, :n_mod].reshape(N_DEV, DEPTH, bl, 6 * d).transpose(1, 0, 2, 3)
    dmod_all = dmod_all.reshape(DEPTH, N_DEV * bl, 6 * d)
    o1 = n_mod
    grad_g_q = small_sum[o1:o1 + MLA_QR // LANES].reshape(1, MLA_QR)
    o1 += MLA_QR // LANES
    grad_g_kv = small_sum[o1:o1 + MLA_KVR // LANES].reshape(1, MLA_KVR)
    o1 += MLA_KVR // LANES
    grad_b_f = small_sum[o1:o1 + 1, :FOX_HEADS]
    o1 += 1
    n_ln_rows = DEPTH * 2 * d // LANES
    grad_ln_g_full = small_sum[o1:o1 + n_ln_rows].reshape(DEPTH, 2, d)
    grad_ln_b_full = small_sum[o1 + n_ln_rows:o1 + 2 * n_ln_rows].reshape(DEPTH, 2, d)
    shard = d // N_DEV
    grad_ln_g = lax.dynamic_slice_in_dim(grad_ln_g_full, dev * shard, shard, axis=2)
    grad_ln_b = lax.dynamic_slice_in_dim(grad_ln_b_full, dev * shard, shard, axis=2)
    by_seq = small_g_all[:, :n_mod].reshape(N_DEV, DEPTH, bl, 6 * d // LANES, LANES).transpose(0, 2, 1, 3, 4)
    grad_ada_b = sum_leading(by_seq.reshape(N_DEV * bl, per_seq, LANES), "sum_ada_b").reshape(DEPTH, 6 * d)
    dmod_cols = lax.dynamic_slice_in_dim(dmod_all, dev * ada_cols, ada_cols, axis=2)
    grad_ada_w = jnp.stack([mm_tn(c_act, dmod_cols[i], name=f"bwd_w_ada{i}") for i in range(DEPTH)])

    g_packed = jnp.concatenate([slot(nm, grads_full[nm].reshape(N_DEV, rows_of[nm], PACK_COLS)) for nm, _ in local],
                               axis=1)
    from_sibling = rs_sibling_exchange(g_packed, "rs_sibling")
    chip_partial = rs_add_sibling(g_packed, from_sibling, "rs_add_sibling")
    from_chips = rs_chip_exchange(chip_partial, "rs_chips")
    g_mine = sum_leading(from_chips, "rs_sum_chips")

    def mine(nm, shape):
        return g_mine[offsets[nm]:offsets[nm] + rows_of[nm]].reshape(shape)

    def shard_t(nm, a):
        return t_last(mine(nm, t_last(a).shape))

    grads = {
        "mla_w_in": mine("mla_w_in", mla_w_in[0].shape)[None],
        "mla_g_q": grad_g_q,
        "mla_w_uq": shard_t("mla_w_uq", mla_w_uq[0])[None],
        "mla_g_kv": grad_g_kv,
        "mla_w_uk": shard_t("mla_w_uk", mla_w_uk[0])[None],
        "mla_w_uv": shard_t("mla_w_uv", mla_w_uv[0])[None],
        "mla_w_o": mine("mla_w_o", mla_w_o[0].shape)[None],
        "fox_w_in": shard_t("fox_w_in", fox_w_in[0])[None],
        "fox_b_f": grad_b_f,
        "fox_w_o": mine("fox_w_o", fox_w_o[0].shape)[None],
        "ada_w": grad_ada_w,
        "ada_b": grad_ada_b,
        "ffn_w_gate": jnp.stack([shard_t(f"gate{i}", ffn_w_gate[i]) for i in range(DEPTH)]),
        "ffn_w_up": jnp.stack([shard_t(f"up{i}", ffn_w_up[i]) for i in range(DEPTH)]),
        "ffn_w_down": jnp.stack([mine(f"down{i}", ffn_w_down[i].shape) for i in range(DEPTH)]),
        "ln_g": grad_ln_g,
        "ln_b": grad_ln_b,
    }
    weights = dict(mla_w_in=mla_w_in, mla_g_q=mla_g_q, mla_w_uq=mla_w_uq, mla_g_kv=mla_g_kv, mla_w_uk=mla_w_uk,
                   mla_w_uv=mla_w_uv, mla_w_o=mla_w_o, fox_w_in=fox_w_in, fox_b_f=fox_b_f, fox_w_o=fox_w_o,
                   ada_w=ada_w, ada_b=ada_b, ffn_w_gate=ffn_w_gate, ffn_w_up=ffn_w_up, ffn_w_down=ffn_w_down,
                   ln_g=ln_g, ln_b=ln_b)
    first = dict(mla_w_in=m_mla_w_in, mla_g_q=m_mla_g_q, mla_w_uq=m_mla_w_uq, mla_g_kv=m_mla_g_kv, mla_w_uk=m_mla_w_uk,
                 mla_w_uv=m_mla_w_uv, mla_w_o=m_mla_w_o, fox_w_in=m_fox_w_in, fox_b_f=m_fox_b_f, fox_w_o=m_fox_w_o,
                 ada_w=m_ada_w, ada_b=m_ada_b, ffn_w_gate=m_ffn_w_gate, ffn_w_up=m_ffn_w_up, ffn_w_down=m_ffn_w_down,
                 ln_g=m_ln_g, ln_b=m_ln_b)
    second = dict(mla_w_in=v_mla_w_in, mla_g_q=v_mla_g_q, mla_w_uq=v_mla_w_uq, mla_g_kv=v_mla_g_kv, mla_w_uk=v_mla_w_uk,
                  mla_w_uv=v_mla_w_uv, mla_w_o=v_mla_w_o, fox_w_in=v_fox_w_in, fox_b_f=v_fox_b_f, fox_w_o=v_fox_w_o,
                  ada_w=v_ada_w, ada_b=v_ada_b, ffn_w_gate=v_ffn_w_gate, ffn_w_up=v_ffn_w_up, ffn_w_down=v_ffn_w_down,
                  ln_g=v_ln_g, ln_b=v_ln_b)
    order = list(weights)
    g_out, d_out, m_out, v_out = [], [], [], []
    for nm in order:
        g = grads[nm].reshape(weights[nm].shape)
        delta, new_m, new_v = adamw(weights[nm], g, first[nm], second[nm], f"adamw_{nm}")
        g_out.append(g)
        d_out.append(delta)
        m_out.append(new_m)
        v_out.append(new_v)
    return (loss, grad_x, *g_out, *d_out, *m_out, *v_out)
```

```python
import functools

import jax
import jax.numpy as jnp
from jax import lax
from jax.experimental import pallas as pl
from jax.experimental.pallas import tpu as pltpu

F32 = jnp.float32
BF16 = jnp.bfloat16
LANES = 128
N_DEV = 8
VMEM_LIMIT_BYTES = 56 * 1024 * 1024

DEPTH = 2
MLA_HEADS = 8
MLA_NOPE = 128
MLA_ROPE = 64
MLA_V = 128
MLA_QR = 256
MLA_KVR = 256
ROPE_THETA = 10000.0
FOX_HEADS = 16
FOX_HD = 64
ALPHA = (2.0 * DEPTH) ** 0.25
NORM_EPS = 1e-5
ADAM_LR = 0.001
ADAM_B1 = 0.9
ADAM_B2 = 0.999
ADAM_EPS = 1e-08
ADAM_WD = 0.01
ADAM_STEP = 10

MESH_AXES = ("x", "y", "c")
MESH = pl.DeviceIdType.MESH


def _params(*sem):
    return pltpu.CompilerParams(dimension_semantics=sem, vmem_limit_bytes=VMEM_LIMIT_BYTES)


def _tile(n, cap, mult=LANES):
    if n <= cap:
        return n
    best = None
    for t in range(mult, cap + 1, mult):
        if n % t == 0:
            best = t
    assert best is not None, (n, cap, mult)
    return best


def _dot(a, b, dims):
    return lax.dot_general(a, b, (dims, ((), ())), preferred_element_type=F32)


def _nn(a, b):
    return _dot(a, b, ((1,), (0,)))


def _nt(a, b):
    return _dot(a, b, ((1,), (1,)))


def _tn(a, b):
    return _dot(a, b, ((0,), (0,)))


def _me():
    return lax.axis_index("x"), lax.axis_index("y"), lax.axis_index("c")


def all_gather(x_loc, name):
    r, c = x_loc.shape

    def body(x_ref, out_ref, send_sems, recv_sems, local_sem):
        x, y, cc = _me()
        me, sibling = (x, y, cc), (x, y, 1 - cc)
        chips = [(1 - x, y), (x, 1 - y), (1 - x, 1 - y)]

        def rows(px, py, pc):
            return out_ref.at[4 * px + 2 * py + pc]

        def copy(k, block, to, src=None):
            return pltpu.make_async_remote_copy(
                src_ref=rows(*block) if src is None else src, dst_ref=rows(*block),
                send_sem=send_sems.at[k], recv_sem=recv_sems.at[k], device_id=to, device_id_type=MESH)

        mine = pltpu.make_async_copy(x_ref, rows(*me), local_sem)
        mine.start()
        first = [copy(0, me, sibling, src=x_ref)]
        first += [copy(1 + j, me, (*chip, cc), src=x_ref) for j, chip in enumerate(chips)]
        for cp in first:
            cp.start()
        passed = [copy(4 + j, (*chip, cc), sibling) for j, chip in enumerate(chips)]
        for j, chip in enumerate(chips):
            copy(1 + j, (*chip, cc), me).wait_recv()
            passed[j].start()
        copy(0, sibling, me).wait_recv()
        for j, chip in enumerate(chips):
            copy(4 + j, (*chip, 1 - cc), me).wait_recv()
        for cp in first + passed:
            cp.wait_send()
        mine.wait()

    return pl.pallas_call(
        body, name=name,
        out_shape=jax.ShapeDtypeStruct((N_DEV, r, c), x_loc.dtype),
        in_specs=[pl.BlockSpec(memory_space=pl.ANY)],
        out_specs=pl.BlockSpec(memory_space=pl.ANY),
        scratch_shapes=[pltpu.SemaphoreType.DMA((7,)), pltpu.SemaphoreType.DMA((7,)), pltpu.SemaphoreType.DMA(())],
    )(x_loc)


def rs_sibling_exchange(g, name):
    _, r, c = g.shape

    def body(g_ref, land_ref, send_sems, recv_sems):
        x, y, cc = _me()
        copies = []
        for k in range(4):
            px, py = k // 2, k % 2
            copies.append(pltpu.make_async_remote_copy(
                src_ref=g_ref.at[4 * px + 2 * py + (1 - cc)], dst_ref=land_ref.at[k],
                send_sem=send_sems.at[k], recv_sem=recv_sems.at[k], device_id=(x, y, 1 - cc), device_id_type=MESH))
        for cp in copies:
            cp.start()
        for cp in copies:
            cp.wait_recv()
        for cp in copies:
            cp.wait_send()

    return pl.pallas_call(
        body, name=name,
        out_shape=jax.ShapeDtypeStruct((4, r, c), g.dtype),
        in_specs=[pl.BlockSpec(memory_space=pl.ANY)],
        out_specs=pl.BlockSpec(memory_space=pl.ANY),
        scratch_shapes=[pltpu.SemaphoreType.DMA((4,)), pltpu.SemaphoreType.DMA((4,))],
    )(g)


def rs_chip_exchange(p, name):
    _, r, c = p.shape

    def body(p_ref, land_ref, send_sems, recv_sems, local_sem):
        x, y, cc = _me()
        mine = pltpu.make_async_copy(p_ref.at[2 * x + y], land_ref.at[2 * x + y], local_sem)
        mine.start()
        chips = [(1 - x, y), (x, 1 - y), (1 - x, 1 - y)]
        sends = [pltpu.make_async_remote_copy(
            src_ref=p_ref.at[2 * px + py], dst_ref=land_ref.at[2 * x + y],
            send_sem=send_sems.at[j], recv_sem=recv_sems.at[j], device_id=(px, py, cc), device_id_type=MESH)
            for j, (px, py) in enumerate(chips)]
        for cp in sends:
            cp.start()
        for j, (px, py) in enumerate(chips):
            pltpu.make_async_remote_copy(
                src_ref=p_ref.at[2 * x + y], dst_ref=land_ref.at[2 * px + py],
                send_sem=send_sems.at[j], recv_sem=recv_sems.at[j], device_id=(px, py, cc), device_id_type=MESH).wait_recv()
        for cp in sends:
            cp.wait_send()
        mine.wait()

    return pl.pallas_call(
        body, name=name,
        out_shape=jax.ShapeDtypeStruct((4, r, c), p.dtype),
        in_specs=[pl.BlockSpec(memory_space=pl.ANY)],
        out_specs=pl.BlockSpec(memory_space=pl.ANY),
        scratch_shapes=[pltpu.SemaphoreType.DMA((3,)), pltpu.SemaphoreType.DMA((3,)), pltpu.SemaphoreType.DMA(())],
    )(p)


def rs_add_sibling(g, land, name):
    _, r, c = g.shape
    tr = _tile(r, 512, 8)
    core = lax.axis_index("c").astype(jnp.int32).reshape(1)

    def body(core_ref, g_ref, l_ref, o_ref):
        o_ref[...] = g_ref[...] + l_ref[...]

    return pl.pallas_call(
        body, name=name,
        out_shape=jax.ShapeDtypeStruct((4, r, c), g.dtype),
        grid_spec=pltpu.PrefetchScalarGridSpec(
            num_scalar_prefetch=1, grid=(4, r // tr),
            in_specs=[pl.BlockSpec((1, tr, c), lambda k, i, core_ref: (2 * k + core_ref[0], i, 0)),
                      pl.BlockSpec((1, tr, c), lambda k, i, core_ref: (k, i, 0))],
            out_specs=pl.BlockSpec((1, tr, c), lambda k, i, core_ref: (k, i, 0))),
        compiler_params=_params("arbitrary", "arbitrary"),
    )(core, g, land)


def sum_leading(x, name):
    n, r, c = x.shape
    tr = _tile(r, 512, 8)

    def body(x_ref, o_ref):
        acc = x_ref[0]
        for k in range(1, n):
            acc = acc + x_ref[k]
        o_ref[...] = acc

    return pl.pallas_call(
        body, name=name,
        out_shape=jax.ShapeDtypeStruct((r, c), x.dtype),
        grid=(r // tr,),
        in_specs=[pl.BlockSpec((n, tr, c), lambda i: (0, i, 0))],
        out_specs=pl.BlockSpec((tr, c), lambda i: (i, 0)),
        compiler_params=_params("arbitrary"),
    )(x)


def mm(pairs, *, trans_b, out_dtype, name, out_slab=False, bias=None):
    a0 = pairs[0][0]
    m = a0.shape[1] if a0.ndim == 3 else a0.shape[0]
    n = pairs[0][1].shape[0] if trans_b else pairs[0][1].shape[1]
    k_total = sum(b.shape[1] if trans_b else b.shape[0] for _, b in pairs)
    deep = k_total > 2048
    tm = _tile(m, 256 if deep else 512, 8)
    tn = _tile(n, 512 if deep else 1024)
    slabs = [a.ndim == 3 for a, _ in pairs]
    n_pairs = len(pairs)

    def body(*refs):
        o_ref = refs[-1]
        acc = bias_ref = None
        if bias is not None:
            bias_ref = refs[2 * n_pairs]
        for i in range(n_pairs):
            a_ref, b_ref = refs[2 * i], refs[2 * i + 1]
            if slabs[i]:
                a = jnp.concatenate([a_ref[s].astype(BF16) for s in range(a_ref.shape[0])], axis=1)
            else:
                a = a_ref[...].astype(BF16)
            b = b_ref[...].astype(BF16)
            part = _nt(a, b) if trans_b else _nn(a, b)
            acc = part if acc is None else acc + part
        if bias_ref is not None:
            acc = acc + bias_ref[...]
        if out_slab:
            for s in range(tn // LANES):
                o_ref[s] = acc[:, s * LANES:(s + 1) * LANES].astype(out_dtype)
        else:
            o_ref[...] = acc.astype(out_dtype)

    in_specs, args = [], []
    for (a, b), slab in zip(pairs, slabs):
        if slab:
            in_specs.append(pl.BlockSpec((a.shape[0], tm, LANES), lambda i, j: (0, i, 0)))
        else:
            in_specs.append(pl.BlockSpec((tm, a.shape[1]), lambda i, j: (i, 0)))
        if trans_b:
            in_specs.append(pl.BlockSpec((tn, b.shape[1]), lambda i, j: (j, 0)))
        else:
            in_specs.append(pl.BlockSpec((b.shape[0], tn), lambda i, j: (0, j)))
        args += [a, b]
    if bias is not None:
        in_specs.append(pl.BlockSpec((1, tn), lambda i, j: (0, j)))
        args.append(bias)
    if out_slab:
        out_shape = jax.ShapeDtypeStruct((n // LANES, m, LANES), out_dtype)
        out_spec = pl.BlockSpec((tn // LANES, tm, LANES), lambda i, j: (j, i, 0))
    else:
        out_shape = jax.ShapeDtypeStruct((m, n), out_dtype)
        out_spec = pl.BlockSpec((tm, tn), lambda i, j: (i, j))
    return pl.pallas_call(
        body, name=name, out_shape=out_shape, grid=(m // tm, n // tn),
        in_specs=in_specs, out_specs=out_spec,
        compiler_params=_params("arbitrary", "arbitrary"),
    )(*args)


def mm_tn(a, b, *, name, tk_cap=1536, tn_cap=1024, tm_cap=512):
    slab = a.ndim == 3
    m = a.shape[1] if slab else a.shape[0]
    k = a.shape[0] * LANES if slab else a.shape[1]
    n = b.shape[1]
    tk = _tile(k, tk_cap)
    tn = _tile(n, tn_cap)
    tm = _tile(m, tm_cap, 8)

    def body(a_ref, b_ref, o_ref):
        @pl.when(pl.program_id(2) == 0)
        def _():
            o_ref[...] = jnp.zeros_like(o_ref)

        bb = b_ref[...].astype(BF16)
        if slab:
            for s in range(tk // LANES):
                o_ref[s * LANES:(s + 1) * LANES, :] += _tn(a_ref[s].astype(BF16), bb)
        else:
            o_ref[...] += _tn(a_ref[...].astype(BF16), bb)

    if slab:
        a_spec = pl.BlockSpec((tk // LANES, tm, LANES), lambda i, j, t: (i, t, 0))
    else:
        a_spec = pl.BlockSpec((tm, tk), lambda i, j, t: (t, i))
    return pl.pallas_call(
        body, name=name, out_shape=jax.ShapeDtypeStruct((k, n), F32), grid=(k // tk, n // tn, m // tm),
        in_specs=[a_spec, pl.BlockSpec((tm, tn), lambda i, j, t: (t, j))],
        out_specs=pl.BlockSpec((tk, tn), lambda i, j, t: (i, j)),
        compiler_params=_params("arbitrary", "arbitrary", "arbitrary"),
    )(a, b)


def _row_spec(d, k):
    return pl.BlockSpec((1, 1, d), lambda b, i: (6 * b + k, 0, 0))


def modulate(x, mod, k_shift, k_scale, bl, name):
    t, d = x.shape
    s = t // bl
    tm = _tile(s, 512, 8)
    nt = s // tm

    def body(x_ref, sh_ref, sc_ref, o_ref):
        o_ref[...] = (x_ref[...] * (1.0 + sc_ref[0]) + sh_ref[0]).astype(BF16)

    return pl.pallas_call(
        body, name=name, out_shape=jax.ShapeDtypeStruct((t, d), BF16), grid=(bl, nt),
        in_specs=[pl.BlockSpec((tm, d), lambda b, i: (b * nt + i, 0)), _row_spec(d, k_shift), _row_spec(d, k_scale)],
        out_specs=pl.BlockSpec((tm, d), lambda b, i: (b * nt + i, 0)),
        compiler_params=_params("arbitrary", "arbitrary"),
    )(x, mod, mod)


def _layer_norm_stats(r):
    mu = jnp.mean(r, axis=-1, keepdims=True)
    rc = r - mu
    var = jnp.mean(rc * rc, axis=-1, keepdims=True)
    rstd = lax.rsqrt(var + NORM_EPS)
    return rc * rstd, rstd


def residual_layer_norm(x, y, mod, k_gate, g, b, bl, name, next_mod=None):
    t, d = x.shape
    s = t // bl
    tm = _tile(s, 256, 8)
    nt = s // tm
    has_next = next_mod is not None

    def body(*refs):
        x_ref, y_ref, gt_ref, g_ref, b_ref = refs[:5]
        rest = refs[5:]
        if has_next:
            sh_ref, sc_ref, o_ref, r_ref, u_ref = rest
        else:
            o_ref, r_ref = rest
        r = ALPHA * x_ref[...] + (1.0 + gt_ref[0]) * y_ref[...]
        xhat, _ = _layer_norm_stats(r)
        out = xhat * g_ref[...] + b_ref[...]
        o_ref[...] = out
        r_ref[...] = r
        if has_next:
            u_ref[...] = (out * (1.0 + sc_ref[0]) + sh_ref[0]).astype(BF16)

    tok = pl.BlockSpec((tm, d), lambda bb, i: (bb * nt + i, 0))
    vec = pl.BlockSpec((1, d), lambda bb, i: (0, 0))
    in_specs = [tok, tok, _row_spec(d, k_gate), vec, vec]
    args = [x, y, mod, g, b]
    out_shape = [jax.ShapeDtypeStruct((t, d), F32), jax.ShapeDtypeStruct((t, d), F32)]
    out_specs = [tok, tok]
    if has_next:
        in_specs += [_row_spec(d, next_mod[0]), _row_spec(d, next_mod[1])]
        args += [mod if len(next_mod) == 2 else next_mod[2]] * 2
        out_shape.append(jax.ShapeDtypeStruct((t, d), BF16))
        out_specs.append(tok)
    return pl.pallas_call(
        body, name=name, out_shape=out_shape, grid=(bl, nt), in_specs=in_specs, out_specs=out_specs,
        compiler_params=_params("arbitrary", "arbitrary"),
    )(*args)


def loss_head(xo, target, name):
    t, d = xo.shape
    tm = _tile(t, 512, 8)

    def body(x_ref, t_ref, l_ref, dx_ref):
        @pl.when(pl.program_id(0) == 0)
        def _():
            l_ref[...] = jnp.zeros_like(l_ref)

        e = x_ref[...] - t_ref[...]
        l_ref[...] += jnp.sum(e * e, axis=0, keepdims=True) * (0.5 / d)
        dx_ref[...] = e * (1.0 / d)

    tok = pl.BlockSpec((tm, d), lambda i: (i, 0))
    return pl.pallas_call(
        body, name=name,
        out_shape=[jax.ShapeDtypeStruct((1, d), F32), jax.ShapeDtypeStruct((t, d), F32)],
        grid=(t // tm,), in_specs=[tok, tok],
        out_specs=[pl.BlockSpec((1, d), lambda i: (0, 0)), tok],
        compiler_params=_params("arbitrary"),
    )(xo, target)


def sublayer_backward(d_a, bl, name, *, du=None, scale=None, x_in=None, ln=None):
    t, d = d_a.shape
    s = t // bl
    tm = _tile(s, 256, 8)
    nt = s // tm
    has_mod = du is not None
    has_ln = ln is not None
    assert has_mod or has_ln
    assert has_ln or x_in is not None

    def body(*refs):
        refs = list(refs)
        da_ref = refs.pop(0)
        if has_mod:
            du_ref, sc_ref = refs.pop(0), refs.pop(0)
        if has_ln:
            r_ref, y_ref, g_ref, b_ref, gt_ref = (refs.pop(0) for _ in range(5))
        elif has_mod:
            xin_ref = refs.pop(0)
        dx_ref = refs.pop(0)
        if has_ln:
            dy_ref, dg_ref, db_ref, dgt_ref = (refs.pop(0) for _ in range(4))
        if has_mod:
            dsc_ref, dsh_ref = refs.pop(0), refs.pop(0)
        first_tile = pl.program_id(1) == 0
        first_step = jnp.logical_and(pl.program_id(0) == 0, first_tile)

        dout = da_ref[...]
        if has_ln:
            xhat, rstd = _layer_norm_stats(r_ref[...])
        if has_mod:
            duv = du_ref[...]
            dout = dout + duv * (1.0 + sc_ref[0])
            xin = xhat * g_ref[...] + b_ref[...] if has_ln else xin_ref[...]

            @pl.when(first_tile)
            def _():
                dsc_ref[...] = jnp.zeros_like(dsc_ref)
                dsh_ref[...] = jnp.zeros_like(dsh_ref)

            dsc_ref[0] += jnp.sum(duv * xin, axis=0, keepdims=True)
            dsh_ref[0] += jnp.sum(duv, axis=0, keepdims=True)
        if not has_ln:
            dx_ref[...] = dout
            return

        @pl.when(first_step)
        def _():
            dg_ref[...] = jnp.zeros_like(dg_ref)
            db_ref[...] = jnp.zeros_like(db_ref)

        @pl.when(first_tile)
        def _():
            dgt_ref[...] = jnp.zeros_like(dgt_ref)

        dg_ref[...] += jnp.sum(dout * xhat, axis=0, keepdims=True)
        db_ref[...] += jnp.sum(dout, axis=0, keepdims=True)
        dxh = dout * g_ref[...]
        dr = rstd * (dxh - jnp.mean(dxh, axis=-1, keepdims=True) - xhat * jnp.mean(dxh * xhat, axis=-1, keepdims=True))
        dx_ref[...] = ALPHA * dr
        dy_ref[...] = ((1.0 + gt_ref[0]) * dr).astype(BF16)
        dgt_ref[0] += jnp.sum(dr * y_ref[...], axis=0, keepdims=True)

    tok = pl.BlockSpec((tm, d), lambda bb, i: (bb * nt + i, 0))
    vec = pl.BlockSpec((1, d), lambda bb, i: (0, 0))
    seq = pl.BlockSpec((1, 1, d), lambda bb, i: (bb, 0, 0))
    in_specs, args = [tok], [d_a]
    if has_mod:
        in_specs += [tok, _row_spec(d, scale[1])]
        args += [du, scale[0]]
    if has_ln:
        r, y, g, b, gate = ln
        in_specs += [tok, tok, vec, vec, _row_spec(d, gate[1])]
        args += [r, y, g, b, gate[0]]
    elif has_mod:
        in_specs.append(tok)
        args.append(x_in)
    names = ["dx"]
    out_shape, out_specs = [jax.ShapeDtypeStruct((t, d), F32)], [tok]
    if has_ln:
        names += ["dy", "dg", "db", "dgate"]
        out_shape += [jax.ShapeDtypeStruct((t, d), BF16), jax.ShapeDtypeStruct((1, d), F32),
                      jax.ShapeDtypeStruct((1, d), F32), jax.ShapeDtypeStruct((bl, 1, d), F32)]
        out_specs += [tok, vec, vec, seq]
    if has_mod:
        names += ["dscale", "dshift"]
        out_shape += [jax.ShapeDtypeStruct((bl, 1, d), F32)] * 2
        out_specs += [seq, seq]
    outs = pl.pallas_call(
        body, name=name, out_shape=out_shape, grid=(bl, nt), in_specs=in_specs, out_specs=out_specs,
        compiler_params=_params("arbitrary", "arbitrary"),
    )(*args)
    return dict(zip(names, outs))


def _silu(a):
    return a * jax.nn.sigmoid(a)


def silu_rows(a, name):
    def body(a_ref, o_ref):
        o_ref[...] = _silu(a_ref[...]).astype(BF16)

    return pl.pallas_call(body, name=name, out_shape=jax.ShapeDtypeStruct(a.shape, BF16))(a)


def swiglu_forward(a, b, name):
    t, f = a.shape
    tm, tf = _tile(t, 512, 8), _tile(f, 1536)

    def body(a_ref, b_ref, h_ref):
        h_ref[...] = (_silu(a_ref[...]) * b_ref[...]).astype(BF16)

    spec = pl.BlockSpec((tm, tf), lambda i, j: (i, j))
    return pl.pallas_call(
        body, name=name, out_shape=jax.ShapeDtypeStruct((t, f), BF16), grid=(t // tm, f // tf),
        in_specs=[spec, spec], out_specs=spec, compiler_params=_params("arbitrary", "arbitrary"),
    )(a, b)


def swiglu_backward(dh, a, b, name):
    t, f = a.shape
    tm, tf = _tile(t, 512, 8), _tile(f, 1536)

    def body(dh_ref, a_ref, b_ref, da_ref, db_ref):
        av = a_ref[...]
        sig = jax.nn.sigmoid(av)
        dhv = dh_ref[...]
        da_ref[...] = (dhv * b_ref[...] * (sig * (1.0 + av * (1.0 - sig)))).astype(BF16)
        db_ref[...] = (dhv * (av * sig)).astype(BF16)

    spec = pl.BlockSpec((tm, tf), lambda i, j: (i, j))
    return pl.pallas_call(
        body, name=name, out_shape=[jax.ShapeDtypeStruct((t, f), BF16)] * 2, grid=(t // tm, f // tf),
        in_specs=[spec, spec, spec], out_specs=[spec, spec], compiler_params=_params("arbitrary", "arbitrary"),
    )(dh, a, b)


def rope_tables(pos, inv_freq, sign, name):
    t = pos.shape[0]
    tm = _tile(t, 512, 8)

    def body(p_ref, f_ref, s_ref, c_out, s_out):
        ang = p_ref[...] * f_ref[...]
        c_out[...] = jnp.cos(ang)
        s_out[...] = jnp.sin(ang) * s_ref[...]

    vec = pl.BlockSpec((1, LANES), lambda i: (0, 0))
    tab = pl.BlockSpec((tm, LANES), lambda i: (i, 0))
    return pl.pallas_call(
        body, name=name, out_shape=[jax.ShapeDtypeStruct((t, LANES), F32)] * 2, grid=(t // tm,),
        in_specs=[pl.BlockSpec((tm, 1), lambda i: (i, 0)), vec, vec], out_specs=[tab, tab],
        compiler_params=_params("arbitrary"),
    )(pos, inv_freq, sign)


def _rot_half(v):
    lane = lax.broadcasted_iota(jnp.int32, v.shape, v.ndim - 1)
    up = pltpu.roll(v, LANES - MLA_ROPE // 2, v.ndim - 1)
    down = pltpu.roll(v, MLA_ROPE // 2, v.ndim - 1)
    return jnp.where(lane % MLA_ROPE < MLA_ROPE // 2, up, down)


def _rope(v, cos, sin_signed):
    return v * cos + _rot_half(v) * sin_signed


def _rope_transposed(dv, cos, sin_signed):
    return dv * cos + _rot_half(dv * sin_signed)


def rope_slabs(v, cos, sin_signed, out_dtype, name, transposed=False):
    ns, t, _ = v.shape
    tm = _tile(t, 512, 8)
    fn = _rope_transposed if transposed else _rope

    def body(v_ref, c_ref, s_ref, o_ref):
        o_ref[0] = fn(v_ref[0].astype(F32), c_ref[...], s_ref[...]).astype(out_dtype)

    tab = pl.BlockSpec((tm, LANES), lambda j, i: (i, 0))
    spec = pl.BlockSpec((1, tm, LANES), lambda j, i: (j, i, 0))
    return pl.pallas_call(
        body, name=name, out_shape=jax.ShapeDtypeStruct(v.shape, out_dtype), grid=(ns, t // tm),
        in_specs=[spec, tab, tab], out_specs=spec, compiler_params=_params("arbitrary", "arbitrary"),
    )(v, cos, sin_signed)


def _rms(x):
    rinv = lax.rsqrt(jnp.mean(x * x, axis=-1, keepdims=True) + NORM_EPS)
    return x * rinv, rinv


def mla_latents_forward(h_in, g_q, g_kv, cos, sin_signed, name):
    t = h_in.shape[0]
    tm = _tile(t, 512, 8)

    def body(h_ref, gq_ref, gkv_ref, c_ref, s_ref, cq_ref, ckv_ref, kr_ref):
        cq_ref[...] = (_rms(h_ref[:, 0:MLA_QR])[0] * gq_ref[...]).astype(BF16)
        ckv_ref[...] = (_rms(h_ref[:, MLA_QR:MLA_QR + MLA_KVR])[0] * gkv_ref[...]).astype(BF16)
        kr_ref[...] = _rope(h_ref[:, MLA_QR + MLA_KVR:], c_ref[...], s_ref[...]).astype(BF16)

    def tok(w):
        return pl.BlockSpec((tm, w), lambda i: (i, 0))

    def vec(w):
        return pl.BlockSpec((1, w), lambda i: (0, 0))

    return pl.pallas_call(
        body, name=name,
        out_shape=[jax.ShapeDtypeStruct((t, MLA_QR), BF16), jax.ShapeDtypeStruct((t, MLA_KVR), BF16),
                   jax.ShapeDtypeStruct((t, LANES), BF16)],
        grid=(t // tm,),
        in_specs=[tok(h_in.shape[1]), vec(MLA_QR), vec(MLA_KVR), tok(LANES), tok(LANES)],
        out_specs=[tok(MLA_QR), tok(MLA_KVR), tok(LANES)],
        compiler_params=_params("arbitrary"),
    )(h_in, g_q, g_kv, cos, sin_signed)


def mla_latents_backward(h_in, dcq, dckv, dkr, g_q, g_kv, cos, sin_signed, name):
    t, w = h_in.shape
    tm = _tile(t, 512, 8)

    def body(h_ref, dcq_ref, dckv_ref, dkr_ref, gq_ref, gkv_ref, c_ref, s_ref, dh_ref, dgq_ref, dgkv_ref):
        @pl.when(pl.program_id(0) == 0)
        def _():
            dgq_ref[...] = jnp.zeros_like(dgq_ref)
            dgkv_ref[...] = jnp.zeros_like(dgkv_ref)

        def rms_bwd(x, dc, g_ref, dg_ref):
            xn, rinv = _rms(x)
            dg_ref[...] += jnp.sum(dc * xn, axis=0, keepdims=True)
            dxn = dc * g_ref[...]
            return rinv * (dxn - xn * jnp.mean(dxn * xn, axis=-1, keepdims=True))

        dq = rms_bwd(h_ref[:, 0:MLA_QR], dcq_ref[...], gq_ref, dgq_ref)
        dkv = rms_bwd(h_ref[:, MLA_QR:MLA_QR + MLA_KVR], dckv_ref[...], gkv_ref, dgkv_ref)
        dr = _rope_transposed(dkr_ref[...], c_ref[...], s_ref[...])
        dh_ref[...] = jnp.concatenate([dq, dkv, dr], axis=1).astype(BF16)

    def tok(ww):
        return pl.BlockSpec((tm, ww), lambda i: (i, 0))

    def vec(ww):
        return pl.BlockSpec((1, ww), lambda i: (0, 0))

    return pl.pallas_call(
        body, name=name,
        out_shape=[jax.ShapeDtypeStruct((t, w), BF16), jax.ShapeDtypeStruct((1, MLA_QR), F32),
                   jax.ShapeDtypeStruct((1, MLA_KVR), F32)],
        grid=(t // tm,),
        in_specs=[tok(w), tok(MLA_QR), tok(MLA_KVR), tok(LANES), vec(MLA_QR), vec(MLA_KVR), tok(LANES), tok(LANES)],
        out_specs=[tok(w), vec(MLA_QR), vec(MLA_KVR)],
        compiler_params=_params("arbitrary"),
    )(h_in, dcq, dckv, dkr, g_q, g_kv, cos, sin_signed)


def _tri(n, lower):
    r = lax.broadcasted_iota(jnp.int32, (n, n), 0)
    c = lax.broadcasted_iota(jnp.int32, (n, n), 1)
    return jnp.where(r >= c if lower else r <= c, 1.0, 0.0).astype(F32)


def _dot_exact(tri, v):
    hi = v.astype(BF16)
    mid = (v - hi.astype(F32)).astype(BF16)
    lo = (v - hi.astype(F32) - mid.astype(F32)).astype(BF16)
    t = tri.astype(BF16)
    return _nn(t, hi) + _nn(t, mid) + _nn(t, lo)


def fox_gate_forward(z, b_f, bl, name):
    t = z.shape[0]
    s = t // bl
    ch = LANES
    n_ch = s // ch

    def body(z_ref, b_ref, f_ref, fs_ref):
        tri = _tri(ch, True)
        carry = jnp.zeros((1, LANES), F32)
        for k in range(n_ch):
            x = z_ref[k * ch:(k + 1) * ch, :] + b_ref[...]
            logf = jnp.minimum(x, 0.0) - jnp.log(1.0 + jnp.exp(-jnp.abs(x)))
            cs = _dot_exact(tri, logf) + carry
            carry = cs[ch - 1:ch, :]
            f_ref[k * ch:(k + 1) * ch, :] = cs
            for h in range(FOX_HEADS):
                fs_ref[h, k * ch:(k + 1) * ch, :] = jnp.broadcast_to(cs[:, h:h + 1], (ch, LANES))

    return pl.pallas_call(
        body, name=name,
        out_shape=[jax.ShapeDtypeStruct((t, LANES), F32), jax.ShapeDtypeStruct((FOX_HEADS, t, LANES), F32)],
        grid=(bl,),
        in_specs=[pl.BlockSpec((s, LANES), lambda b: (b, 0)), pl.BlockSpec((1, LANES), lambda b: (0, 0))],
        out_specs=[pl.BlockSpec((s, LANES), lambda b: (b, 0)),
                   pl.BlockSpec((FOX_HEADS, s, LANES), lambda b: (0, b, 0))],
        compiler_params=_params("arbitrary"),
    )(z, b_f)


def fox_gate_backward(z, b_f, df, bl, name):
    t = z.shape[0]
    s = t // bl
    ch = LANES
    n_ch = s // ch

    def body(z_ref, b_ref, df_ref, dz_ref, db_ref):
        @pl.when(pl.program_id(0) == 0)
        def _():
            db_ref[...] = jnp.zeros_like(db_ref)

        tri = _tri(ch, False)
        carry = jnp.zeros((1, LANES), F32)
        for k in reversed(range(n_ch)):
            cs = _dot_exact(tri, df_ref[k * ch:(k + 1) * ch, :]) + carry
            carry = cs[0:1, :]
            x = z_ref[k * ch:(k + 1) * ch, :] + b_ref[...]
            dz = cs * (1.0 - jax.nn.sigmoid(x))
            dz_ref[k * ch:(k + 1) * ch, :] = dz
            db_ref[...] += jnp.sum(dz, axis=0, keepdims=True)

    tok = pl.BlockSpec((s, LANES), lambda b: (b, 0))
    vec = pl.BlockSpec((1, LANES), lambda b: (0, 0))
    return pl.pallas_call(
        body, name=name,
        out_shape=[jax.ShapeDtypeStruct((t, LANES), F32), jax.ShapeDtypeStruct((1, LANES), F32)],
        grid=(bl,), in_specs=[tok, vec, tok], out_specs=[tok, vec],
        compiler_params=_params("arbitrary"),
    )(z, b_f, df)


NEG_INF = float("-inf")


def _attn_tiles(s):
    return _tile(s, 256, 8)


def attention_forward(kind, ops, bl, scale, name):
    fox = kind == "fox"
    if fox:
        qkv, fq, fk = ops
        t = qkv.shape[1]
        n_pair = FOX_HEADS // 2
    else:
        qn, qr, kn, kr, v = ops
        t = qn.shape[1]
        n_pair = MLA_HEADS // 2
    s = t // bl
    tq = _attn_tiles(s)
    nq = s // tq
    half = LANES // 2

    def body(*refs):
        if fox:
            q_ref, k_ref, v_ref, fq_ref, fk_ref, o_ref, lse_ref, o32_ref = refs
        else:
            qn_ref, qr_ref, kn_ref, kr_ref, v_ref, o_ref, lse_ref = refs
        i = pl.program_id(2)
        row = lax.broadcasted_iota(jnp.int32, (tq, tq), 0)
        col = lax.broadcasted_iota(jnp.int32, (tq, tq), 1)
        heads = []
        for e in range(2):
            sl = slice(e * half, (e + 1) * half)
            if fox:
                heads.append((sl, q_ref[0, :, sl], None))
            else:
                heads.append((sl, qn_ref[e], qr_ref[0, :, sl]))
        dv = half if fox else LANES

        def wide(stat):
            return jnp.concatenate([stat] * (tq // LANES), axis=1)

        def step(j, carry, masked):
            rows = pl.ds(pl.multiple_of(j * tq, tq), tq)
            new = []
            for e, (sl, qa, qb) in enumerate(heads):
                m, l, acc = carry[e]
                if fox:
                    sc = _nt(qa, k_ref[0, rows, sl]) * scale + wide(fq_ref[e]) - fk_ref[0, j, e:e + 1, :]
                    vv = v_ref[0, rows, sl]
                else:
                    sc = (_nt(qa, kn_ref[e, rows, :]) + _nt(qb, kr_ref[rows, 0:half])) * scale
                    vv = v_ref[e, rows, :]
                if masked:
                    sc = jnp.where(row >= col, sc, NEG_INF)
                m_new = jnp.maximum(m, jnp.max(sc, axis=1, keepdims=True))
                p = jnp.exp(sc - m_new)
                a = jnp.exp(m - m_new)
                l = a * l + jnp.sum(p, axis=1, keepdims=True)
                p_hi = p.astype(BF16)
                acc = a * acc + _nn(p_hi, vv)
                if fox:
                    acc = acc + _nn((p - p_hi.astype(F32)).astype(BF16), vv)
                new.append((m_new, l, acc))
            return tuple(new)

        init = (jnp.full((tq, 1), NEG_INF, F32), jnp.zeros((tq, 1), F32), jnp.zeros((tq, dv), F32))
        carry = step(i, (init, init), True)
        carry = lax.fori_loop(0, i, lambda j, c: step(j, c, False), carry)
        outs = [acc / l for _, l, acc in carry]
        for e, (m, l, _) in enumerate(carry):
            lse_ref[e] = jnp.broadcast_to(m + jnp.log(l), (tq, LANES))
        if fox:
            o32 = jnp.concatenate(outs, axis=1)
            o32_ref[0] = o32
            o_ref[0] = o32.astype(BF16)
        else:
            o_ref[0] = outs[0].astype(BF16)
            o_ref[1] = outs[1].astype(BF16)

    def q_idx(b, g, i):
        return (g, b * nq + i, 0)

    if fox:
        nk = fk.shape[1]
        in_specs = [pl.BlockSpec((1, tq, LANES), q_idx),
                    pl.BlockSpec((1, s, LANES), lambda b, g, i: (n_pair + g, b, 0)),
                    pl.BlockSpec((1, s, LANES), lambda b, g, i: (2 * n_pair + g, b, 0)),
                    pl.BlockSpec((2, tq, LANES), q_idx),
                    pl.BlockSpec((1, nk, 8, tq), lambda b, g, i: (b * n_pair + g, 0, 0, 0))]
        args = [qkv, qkv, qkv, fq, fk]
        o_spec = pl.BlockSpec((1, tq, LANES), q_idx)
    else:
        in_specs = [pl.BlockSpec((2, tq, LANES), q_idx),
                    pl.BlockSpec((1, tq, LANES), q_idx),
                    pl.BlockSpec((2, s, LANES), lambda b, g, i: (g, b, 0)),
                    pl.BlockSpec((s, LANES), lambda b, g, i: (b, 0)),
                    pl.BlockSpec((2, s, LANES), lambda b, g, i: (g, b, 0))]
        args = [qn, qr, kn, kr, v]
        o_spec = pl.BlockSpec((2, tq, LANES), q_idx)
    out_shape = [jax.ShapeDtypeStruct((8, t, LANES), BF16), jax.ShapeDtypeStruct((2 * n_pair, t, LANES), F32)]
    out_specs = [o_spec, pl.BlockSpec((2, tq, LANES), q_idx)]
    if fox:
        out_shape.append(jax.ShapeDtypeStruct((8, t, LANES), F32))
        out_specs.append(o_spec)
    outs = pl.pallas_call(
        body, name=name, out_shape=out_shape, grid=(bl, n_pair, nq), in_specs=in_specs, out_specs=out_specs,
        compiler_params=_params("arbitrary", "arbitrary", "arbitrary"),
    )(*args)
    return (outs[0], outs[1], outs[2] if fox else outs[0])


def attention_backward(kind, ops, o, do, lse, bl, scale, name):
    fox = kind == "fox"
    if fox:
        qkv, fq, fk = ops
        t = qkv.shape[1]
        n_pair = FOX_HEADS // 2
    else:
        qn, qr, kn, kr, v = ops
        t = qn.shape[1]
        n_pair = MLA_HEADS // 2
    s = t // bl
    tq = _attn_tiles(s)
    nq = s // tq
    half = LANES // 2

    def body(*refs):
        if fox:
            (q_ref, k_ref, v_ref, fq_ref, fk_ref, o_ref, do_ref, lse_ref,
             dq_ref, dk_ref, dv_ref, dfk_ref, delta_scr, qt_scr, dot_scr) = refs
        else:
            (qn_ref, qr_ref, kn_ref, kr_ref, v_ref, o_ref, do_ref, lse_ref,
             dqn_ref, dqr_ref, dkn_ref, dv_ref, dkr_ref, delta_scr, qt_scr, qrt_scr, dot_scr) = refs
        g, j = pl.program_id(1), pl.program_id(2)
        row = lax.broadcasted_iota(jnp.int32, (tq, tq), 0)
        col = lax.broadcasted_iota(jnp.int32, (tq, tq), 1)
        krows = pl.ds(pl.multiple_of(j * tq, tq), tq)

        def transposed(v):
            return v.astype(F32).T.astype(BF16)

        def wide(stat):
            return jnp.concatenate([stat] * (tq // LANES), axis=1)

        @pl.when(j == 0)
        def _():
            if fox:
                dq_ref[...] = jnp.zeros_like(dq_ref)
            else:
                dqn_ref[...] = jnp.zeros_like(dqn_ref)
                dqr_ref[...] = jnp.zeros_like(dqr_ref)
            for ii in range(nq):
                rws = slice(ii * tq, (ii + 1) * tq)
                deltas = []
                if fox:
                    prod = do_ref[0, rws, :].astype(F32) * o_ref[0, rws, :].astype(F32)
                    for e in range(2):
                        deltas.append(jnp.sum(prod[:, e * half:(e + 1) * half], axis=1, keepdims=True))
                    qt_scr[ii] = transposed(q_ref[0, rws, :])
                    dot_scr[ii] = transposed(do_ref[0, rws, :])
                else:
                    for e in range(2):
                        prod = do_ref[e, rws, :].astype(F32) * o_ref[e, rws, :].astype(F32)
                        deltas.append(jnp.sum(prod, axis=1, keepdims=True))
                        qt_scr[e, ii] = transposed(qn_ref[e, rws, :])
                        dot_scr[e, ii] = transposed(do_ref[e, rws, :])
                    qrt_scr[ii] = transposed(qr_ref[0, rws, :])
                for e in range(2):
                    delta_scr[e, rws, :] = jnp.broadcast_to(deltas[e], (tq, LANES))

        if fox:
            dfk_ref[...] = jnp.zeros_like(dfk_ref)
        else:
            @pl.when(jnp.logical_and(g == 0, j == 0))
            def _():
                dkr_ref[...] = jnp.zeros_like(dkr_ref)

        heads = []
        for e in range(2):
            sl = slice(e * half, (e + 1) * half)
            if fox:
                heads.append((sl, k_ref[0, :, sl], v_ref[0, :, sl], fk_ref[0, 0, e:e + 1, :]))
            else:
                heads.append((sl, kn_ref[e], v_ref[e], kr_ref[krows, 0:half]))
        dk_w = dv_w = half if fox else LANES

        def step(i, carry, masked):
            rows = pl.ds(pl.multiple_of(i * tq, tq), tq)
            new = []
            for e, (sl, k_e, v_e, x_e) in enumerate(heads):
                dk_acc, dv_acc, last = carry[e]
                if fox:
                    do_i = do_ref[0, rows, sl]
                    sc = _nt(q_ref[0, rows, sl], k_e) * scale + wide(fq_ref[e, rows, :]) - x_e
                else:
                    do_i = do_ref[e, rows, :]
                    sc = (_nt(qn_ref[e, rows, :], k_e) + _nt(qr_ref[0, rows, sl], x_e)) * scale
                if masked:
                    sc = jnp.where(row >= col, sc, NEG_INF)
                p = jnp.exp(sc - wide(lse_ref[e, rows, :]))
                dp = _nt(do_i, v_e)
                ds = p * (dp - wide(delta_scr[e, rows, :]))
                dsb = (ds * scale).astype(BF16)
                if fox:
                    fsl = slice(e * half, (e + 1) * half)
                    dv_acc = dv_acc + _nn(dot_scr[i, fsl, :], p.astype(BF16))
                    dk_acc = dk_acc + _nn(qt_scr[i, fsl, :], dsb)
                    dq_ref[0, rows, sl] += _nn(dsb, k_e)
                    last = last - jnp.sum(ds, axis=0, keepdims=True)
                else:
                    dv_acc = dv_acc + _nn(dot_scr[e, i], p.astype(BF16))
                    dk_acc = dk_acc + _nn(qt_scr[e, i], dsb)
                    dqn_ref[e, rows, :] += _nn(dsb, k_e)
                    dqr_ref[0, rows, sl] += _nn(dsb, x_e)
                    last = last + _nn(qrt_scr[i, e * half:(e + 1) * half, :], dsb)
                new.append((dk_acc, dv_acc, last))
            return tuple(new)

        last0 = jnp.zeros((1, tq), F32) if fox else jnp.zeros((half, tq), F32)
        init = (jnp.zeros((dk_w, tq), F32), jnp.zeros((dv_w, tq), F32), last0)
        carry = step(j, (init, init), True)
        carry = lax.fori_loop(j + 1, nq, lambda i, c: step(i, c, False), carry)
        if fox:
            for e in range(2):
                dfk_ref[0, 0, e:e + 1, :] = carry[e][2]
            dk_ref[0] = jnp.concatenate([carry[0][0], carry[1][0]], axis=0).T.astype(BF16)
            dv_ref[0] = jnp.concatenate([carry[0][1], carry[1][1]], axis=0).T.astype(BF16)
        else:
            for e in range(2):
                dkn_ref[e] = carry[e][0].T.astype(BF16)
                dv_ref[e] = carry[e][1].T.astype(BF16)
            dkr_t = carry[0][2] + carry[1][2]
            dkr_ref[krows, :] += jnp.concatenate([dkr_t, jnp.zeros_like(dkr_t)], axis=0).T

    def whole(b, g, j):
        return (g, b, 0)

    def kblk(b, g, j):
        return (g, b * nq + j, 0)

    if fox:
        in_specs = [pl.BlockSpec((1, s, LANES), whole),
                    pl.BlockSpec((1, tq, LANES), lambda b, g, j: (n_pair + g, b * nq + j, 0)),
                    pl.BlockSpec((1, tq, LANES), lambda b, g, j: (2 * n_pair + g, b * nq + j, 0)),
                    pl.BlockSpec((2, s, LANES), whole),
                    pl.BlockSpec((1, 1, 8, tq), lambda b, g, j: (b * n_pair + g, j, 0, 0)),
                    pl.BlockSpec((1, s, LANES), whole), pl.BlockSpec((1, s, LANES), whole),
                    pl.BlockSpec((2, s, LANES), whole)]
        args = [qkv, qkv, qkv, fq, fk, o, do, lse]
        out_shape = [jax.ShapeDtypeStruct((8, t, LANES), F32), jax.ShapeDtypeStruct((8, t, LANES), BF16),
                     jax.ShapeDtypeStruct((8, t, LANES), BF16), jax.ShapeDtypeStruct(fk.shape, F32)]
        out_specs = [pl.BlockSpec((1, s, LANES), whole), pl.BlockSpec((1, tq, LANES), kblk),
                     pl.BlockSpec((1, tq, LANES), kblk),
                     pl.BlockSpec((1, 1, 8, tq), lambda b, g, j: (b * n_pair + g, j, 0, 0))]
    else:
        pair = pl.BlockSpec((2, s, LANES), whole)
        pair_k = pl.BlockSpec((2, tq, LANES), kblk)
        in_specs = [pair, pl.BlockSpec((1, s, LANES), whole), pair_k,
                    pl.BlockSpec((s, LANES), lambda b, g, j: (b, 0)), pair_k,
                    pair, pair, pair]
        args = [qn, qr, kn, kr, v, o, do, lse]
        out_shape = [jax.ShapeDtypeStruct((8, t, LANES), F32), jax.ShapeDtypeStruct((4, t, LANES), F32),
                     jax.ShapeDtypeStruct((8, t, LANES), BF16), jax.ShapeDtypeStruct((8, t, LANES), BF16),
                     jax.ShapeDtypeStruct((t, LANES), F32)]
        out_specs = [pair, pl.BlockSpec((1, s, LANES), whole), pair_k, pair_k,
                     pl.BlockSpec((s, LANES), lambda b, g, j: (b, 0))]
    t_blocks = pltpu.VMEM((nq, LANES, tq), BF16)
    t_pairs = pltpu.VMEM((2, nq, LANES, tq), BF16)
    scratch = [pltpu.VMEM((2, s, LANES), F32)] + ([t_blocks, t_blocks] if fox else [t_pairs, t_blocks, t_pairs])
    return pl.pallas_call(
        body, name=name, out_shape=out_shape, grid=(bl, n_pair, nq), in_specs=in_specs, out_specs=out_specs,
        scratch_shapes=scratch, compiler_params=_params("arbitrary", "arbitrary", "arbitrary"),
    )(*args)


def adamw(w, g, m, v, name):
    shape = w.shape
    c = shape[-1]
    r = w.size // c
    tr = _tile(r, 512, 8)

    def body(w_ref, g_ref, m_ref, v_ref, d_ref, nm_ref, nv_ref):
        gv = g_ref[...]
        m2 = ADAM_B1 * m_ref[...] + (1.0 - ADAM_B1) * gv
        v2 = ADAM_B2 * v_ref[...] + (1.0 - ADAM_B2) * (gv * gv)
        m_hat = m2 / (1.0 - ADAM_B1 ** ADAM_STEP)
        v_hat = v2 / (1.0 - ADAM_B2 ** ADAM_STEP)
        d_ref[...] = -ADAM_LR * (m_hat / (jnp.sqrt(v_hat) + ADAM_EPS) + ADAM_WD * w_ref[...])
        nm_ref[...] = m2
        nv_ref[...] = v2

    spec = pl.BlockSpec((tr, c), lambda i: (i, 0))
    outs = pl.pallas_call(
        body, name=name, out_shape=[jax.ShapeDtypeStruct((r, c), F32)] * 3, grid=(r // tr,),
        in_specs=[spec] * 4, out_specs=[spec] * 3, compiler_params=_params("arbitrary"),
    )(*(a.reshape(r, c) for a in (w, g, m, v)))
    return tuple(a.reshape(shape) for a in outs)


PACK_COLS = 1024


def _pack_rows(a):
    return a.reshape(-1, PACK_COLS)


def kernel(x, c, positions, mla_w_in, mla_g_q, mla_w_uq, mla_g_kv, mla_w_uk, mla_w_uv, mla_w_o, fox_w_in, fox_b_f, fox_w_o, ada_w, ada_b, ffn_w_gate, ffn_w_up, ffn_w_down, ln_g, ln_b, loss_target, m_mla_w_in, m_mla_g_q, m_mla_w_uq, m_mla_g_kv, m_mla_w_uk, m_mla_w_uv, m_mla_w_o, m_fox_w_in, m_fox_b_f, m_fox_w_o, m_ada_w, m_ada_b, m_ffn_w_gate, m_ffn_w_up, m_ffn_w_down, m_ln_g, m_ln_b, v_mla_w_in, v_mla_g_q, v_mla_w_uq, v_mla_g_kv, v_mla_w_uk, v_mla_w_uv, v_mla_w_o, v_fox_w_in, v_fox_b_f, v_fox_w_o, v_ada_w, v_ada_b, v_ffn_w_gate, v_ffn_w_up, v_ffn_w_down, v_ln_g, v_ln_b):
    bl, s, d = x.shape
    t = bl * s
    ff = ffn_w_gate.shape[-1] * N_DEV
    dev = 4 * lax.axis_index("x") + 2 * lax.axis_index("y") + lax.axis_index("c")
    ada_cols = ada_w.shape[-1]
    fox_in = fox_w_in.shape[-1] * N_DEV
    mla_in = mla_w_in.shape[-1]
    mla_in_pad = mla_in + (-mla_in) % LANES

    def t_last(a):
        return jnp.swapaxes(a, -1, -2)

    local = [
        ("mla_w_in", mla_w_in[0]),
        ("mla_w_uq", t_last(mla_w_uq[0])),
        ("mla_w_uk", t_last(mla_w_uk[0])),
        ("mla_w_uv", t_last(mla_w_uv[0])),
        ("mla_w_o", mla_w_o[0]),
        ("fox_w_in", t_last(fox_w_in[0])),
        ("fox_w_o", fox_w_o[0]),
    ]
    for i in range(DEPTH):
        local += [(f"gate{i}", t_last(ffn_w_gate[i])), (f"up{i}", t_last(ffn_w_up[i])), (f"down{i}", ffn_w_down[i])]
    offsets, rows_of, slot_of = {}, {}, {}
    pack_rows = 0
    for nm, a in local:
        rows_of[nm] = a.size // PACK_COLS
        slot_of[nm] = rows_of[nm] + (-rows_of[nm]) % 16
        offsets[nm] = pack_rows
        pack_rows += slot_of[nm]

    def slot(nm, rows):
        pad = [(0, 0)] * rows.ndim
        pad[-2] = (0, slot_of[nm] - rows_of[nm])
        return jnp.pad(rows, pad)

    packed = jnp.concatenate([slot(nm, _pack_rows(a).astype(BF16)) for nm, a in local], axis=0)
    gathered = all_gather(packed, "gather_weights")

    def full(nm, cols):
        blk = gathered[:, offsets[nm]:offsets[nm] + rows_of[nm], :]
        return blk.reshape(-1, cols)

    w_in = jnp.pad(full("mla_w_in", mla_in), ((0, 0), (0, mla_in_pad - mla_in)))
    wt_uq = full("mla_w_uq", MLA_QR).reshape(MLA_HEADS, MLA_NOPE + MLA_ROPE, MLA_QR)
    wt_uq_n = wt_uq[:, :MLA_NOPE].reshape(MLA_HEADS * MLA_NOPE, MLA_QR)
    wt_uq_r = wt_uq[:, MLA_NOPE:].reshape(MLA_HEADS * MLA_ROPE, MLA_QR)
    wt_uk = full("mla_w_uk", MLA_KVR)
    wt_uv = full("mla_w_uv", MLA_KVR)
    w_mo = full("mla_w_o", d)
    wt_fox = full("fox_w_in", d)
    wt_qkv = wt_fox[:3 * d]
    wt_f = jnp.pad(wt_fox[3 * d:], ((0, LANES - FOX_HEADS), (0, 0)))
    w_fo = full("fox_w_o", d)
    wt_gate = [full(f"gate{i}", d) for i in range(DEPTH)]
    wt_up = [full(f"up{i}", d) for i in range(DEPTH)]
    w_down = [full(f"down{i}", d) for i in range(DEPTH)]

    small = jnp.concatenate([c.reshape(-1, LANES), ln_g.reshape(-1, LANES), ln_b.reshape(-1, LANES)], axis=0)
    small_rows = small.shape[0]
    small = jnp.pad(small, ((0, (-small_rows) % 8), (0, 0)))
    small_all = all_gather(small, "gather_small")
    c_rows = bl * d // LANES
    c_all = small_all[:, :c_rows].reshape(N_DEV * bl, d)
    n_ln = DEPTH * 2
    ln_g_all = small_all[:, c_rows:c_rows + n_ln, :].transpose(1, 0, 2).reshape(DEPTH, 2, 1, d)
    ln_b_all = small_all[:, c_rows + n_ln:c_rows + 2 * n_ln, :].transpose(1, 0, 2).reshape(DEPTH, 2, 1, d)

    c_act = silu_rows(c_all, "silu_c")
    ada_b_loc = lax.dynamic_slice_in_dim(ada_b, dev * ada_cols, ada_cols, axis=1)
    mod_cols = [mm([(c_act, ada_w[i])], trans_b=False, out_dtype=F32, name=f"ada_fwd{i}", bias=ada_b_loc[i][None, :])
                for i in range(DEPTH)]
    mod_all = all_gather(jnp.concatenate(mod_cols, axis=0), "gather_mod")
    mod_all = mod_all.reshape(N_DEV, DEPTH, N_DEV * bl, ada_cols).transpose(1, 2, 0, 3).reshape(DEPTH, N_DEV * bl, 6 * d)
    mod_mine = lax.dynamic_slice_in_dim(mod_all, dev * bl, bl, axis=1)
    mods = [mod_mine[i].reshape(bl * 6, 1, d) for i in range(DEPTH)]

    half_r = MLA_ROPE // 2
    inv_freq = ROPE_THETA ** (-jnp.arange(half_r, dtype=F32) / half_r)
    inv_freq = jnp.tile(inv_freq, LANES // half_r)[None, :]
    sign = jnp.tile(jnp.concatenate([-jnp.ones((half_r,), F32), jnp.ones((half_r,), F32)]), LANES // MLA_ROPE)[None, :]
    cos_t, sin_t = rope_tables(positions.astype(F32).reshape(t, 1), inv_freq, sign, "rope_tables")

    x2d = x.reshape(t, d)
    g_q, g_kv = mla_g_q.reshape(1, MLA_QR), mla_g_kv.reshape(1, MLA_KVR)
    b_f = jnp.pad(fox_b_f.reshape(1, FOX_HEADS), ((0, 0), (0, LANES - FOX_HEADS)))
    mla_scale = (MLA_NOPE + MLA_ROPE) ** -0.5
    fox_scale = FOX_HD ** -0.5
    tq = _attn_tiles(s)
    nk = s // tq

    saved = []
    u = modulate(x2d, mods[0], 0, 1, bl, "modulate0")
    xin = x2d
    for i in range(DEPTH):
        sv = {"u": u, "x_in": xin}
        if i % 2 == 0:
            h_in = mm([(u, w_in)], trans_b=False, out_dtype=F32, name=f"mla_in{i}")
            c_q, c_kv, k_r = mla_latents_forward(h_in, g_q, g_kv, cos_t, sin_t, f"mla_latents{i}")
            q_n = mm([(c_q, wt_uq_n)], trans_b=True, out_dtype=BF16, out_slab=True, name=f"mla_qn{i}")
            q_r_raw = mm([(c_q, wt_uq_r)], trans_b=True, out_dtype=F32, out_slab=True, name=f"mla_qr{i}")
            q_r = rope_slabs(q_r_raw, cos_t, sin_t, BF16, f"mla_qrope{i}")
            k_n = mm([(c_kv, wt_uk)], trans_b=True, out_dtype=BF16, out_slab=True, name=f"mla_kn{i}")
            v_m = mm([(c_kv, wt_uv)], trans_b=True, out_dtype=BF16, out_slab=True, name=f"mla_v{i}")
            ops = (q_n, q_r, k_n, k_r, v_m)
            o, lse, o_delta = attention_forward("mla", ops, bl, mla_scale, f"mla_attn{i}")
            y = mm([(o, w_mo)], trans_b=False, out_dtype=F32, name=f"mla_out{i}")
            sv.update(h_in=h_in, c_q=c_q, c_kv=c_kv, ops=ops, o=o, lse=lse, o_delta=o_delta)
        else:
            qkv = mm([(u, wt_qkv)], trans_b=True, out_dtype=BF16, out_slab=True, name=f"fox_qkv{i}")
            z = mm([(u, wt_f)], trans_b=True, out_dtype=F32, name=f"fox_z{i}")
            f_tok, f_q = fox_gate_forward(z, b_f, bl, f"fox_gate{i}")
            f_k = f_tok[:, :FOX_HEADS].reshape(bl, nk, tq, FOX_HEADS // 2, 2).transpose(0, 3, 1, 4, 2)
            f_k = jnp.pad(f_k.reshape(bl * FOX_HEADS // 2, nk, 2, tq), ((0, 0), (0, 0), (0, 6), (0, 0)))
            ops = (qkv, f_q, f_k)
            o, lse, o_delta = attention_forward("fox", ops, bl, fox_scale, f"fox_attn{i}")
            y = mm([(o, w_fo)], trans_b=False, out_dtype=F32, name=f"fox_out{i}")
            sv.update(z=z, ops=ops, o=o, lse=lse, o_delta=o_delta)
        x1, r1, u2 = residual_layer_norm(xin, y, mods[i], 2, ln_g_all[i, 0], ln_b_all[i, 0], bl, f"ln_mix{i}",
                                         next_mod=(3, 4))
        a = mm([(u2, wt_gate[i])], trans_b=True, out_dtype=F32, name=f"ffn_gate{i}")
        bb = mm([(u2, wt_up[i])], trans_b=True, out_dtype=F32, name=f"ffn_up{i}")
        h = swiglu_forward(a, bb, f"swiglu{i}")
        y2 = mm([(h, w_down[i])], trans_b=False, out_dtype=F32, name=f"ffn_down{i}")
        sv.update(y=y, r1=r1, u2=u2, a=a, bb=bb, h=h, y2=y2)
        if i + 1 < DEPTH:
            xin, r2, u = residual_layer_norm(x1, y2, mods[i], 5, ln_g_all[i, 1], ln_b_all[i, 1], bl, f"ln_ffn{i}",
                                             next_mod=(0, 1, mods[i + 1]))
        else:
            xin, r2 = residual_layer_norm(x1, y2, mods[i], 5, ln_g_all[i, 1], ln_b_all[i, 1], bl, f"ln_ffn{i}")
        sv.update(r2=r2)
        saved.append(sv)

    loss_cols, d_x = loss_head(xin, loss_target.reshape(t, d), "loss_head")
    loss = lax.psum(jnp.sum(loss_cols), MESH_AXES)

    grads_full = {}
    dmod = [[None] * 6 for _ in range(DEPTH)]
    dg_ln = [[None, None] for _ in range(DEPTH)]
    db_ln = [[None, None] for _ in range(DEPTH)]
    dg_q = dg_kv = db_f = None
    d_a, du = d_x, None
    for i in reversed(range(DEPTH)):
        sv = saved[i]
        ln2 = (sv["r2"], sv["y2"], ln_g_all[i, 1], ln_b_all[i, 1], (mods[i], 5))
        if du is None:
            bw = sublayer_backward(d_a, bl, f"bwd_ln_ffn{i}", ln=ln2)
        else:
            bw = sublayer_backward(d_a, bl, f"bwd_ln_ffn{i}", du=du, scale=(mods[i + 1], 1), ln=ln2)
            dmod[i + 1][0], dmod[i + 1][1] = bw["dshift"], bw["dscale"]
        dmod[i][5], dg_ln[i][1], db_ln[i][1] = bw["dgate"], bw["dg"], bw["db"]
        dy2 = bw["dy"]
        dh = mm([(dy2, w_down[i])], trans_b=True, out_dtype=F32, name=f"bwd_ffn_dh{i}")
        da, dbb = swiglu_backward(dh, sv["a"], sv["bb"], f"bwd_swiglu{i}")
        du2 = mm([(da, wt_gate[i]), (dbb, wt_up[i])], trans_b=False, out_dtype=F32, name=f"bwd_ffn_du{i}")
        grads_full[f"down{i}"] = mm_tn(sv["h"], dy2, name=f"bwd_w_down{i}")
        grads_full[f"gate{i}"] = mm_tn(da, sv["u2"], name=f"bwd_w_gate{i}")
        grads_full[f"up{i}"] = mm_tn(dbb, sv["u2"], name=f"bwd_w_up{i}")
        bw = sublayer_backward(bw["dx"], bl, f"bwd_ln_mix{i}", du=du2, scale=(mods[i], 4),
                               ln=(sv["r1"], sv["y"], ln_g_all[i, 0], ln_b_all[i, 0], (mods[i], 2)))
        dmod[i][3], dmod[i][4], dmod[i][2] = bw["dshift"], bw["dscale"], bw["dgate"]
        dg_ln[i][0], db_ln[i][0] = bw["dg"], bw["db"]
        d_a, dy = bw["dx"], bw["dy"]
        o, lse, ops = sv["o"], sv["lse"], sv["ops"]
        if i % 2 == 0:
            do = mm([(dy, w_mo)], trans_b=True, out_dtype=BF16, out_slab=True, name=f"bwd_mla_do{i}")
            grads_full["mla_w_o"] = mm_tn(o, dy, name=f"bwd_w_mla_o{i}")
            dqn, dqr, dkn, dvm, dkr = attention_backward("mla", ops, sv["o_delta"], do, lse, bl, mla_scale,
                                                         f"bwd_mla_attn{i}")
            dqr = rope_slabs(dqr, cos_t, sin_t, F32, f"bwd_mla_qrope{i}", transposed=True)
            dcq = mm([(dqn, wt_uq_n), (dqr, wt_uq_r)], trans_b=False, out_dtype=F32, name=f"bwd_mla_dcq{i}")
            dckv = mm([(dkn, wt_uk), (dvm, wt_uv)], trans_b=False, out_dtype=F32, name=f"bwd_mla_dckv{i}")
            d_uq_n = mm_tn(dqn, sv["c_q"], name=f"bwd_w_uq_n{i}").reshape(MLA_HEADS, MLA_NOPE, MLA_QR)
            d_uq_r = mm_tn(dqr, sv["c_q"], name=f"bwd_w_uq_r{i}").reshape(MLA_HEADS, MLA_ROPE, MLA_QR)
            grads_full["mla_w_uq"] = jnp.concatenate([d_uq_n, d_uq_r], axis=1)
            grads_full["mla_w_uk"] = mm_tn(dkn, sv["c_kv"], name=f"bwd_w_uk{i}")
            grads_full["mla_w_uv"] = mm_tn(dvm, sv["c_kv"], name=f"bwd_w_uv{i}")
            dh_in, dg_q, dg_kv = mla_latents_backward(sv["h_in"], dcq, dckv, dkr, g_q, g_kv, cos_t, sin_t,
                                                      f"bwd_mla_latents{i}")
            du = mm([(dh_in, w_in)], trans_b=True, out_dtype=F32, name=f"bwd_mla_du{i}")
            grads_full["mla_w_in"] = mm_tn(sv["u"], dh_in, name=f"bwd_w_mla_in{i}")[:, :mla_in]
        else:
            do = mm([(dy, w_fo)], trans_b=True, out_dtype=BF16, out_slab=True, name=f"bwd_fox_do{i}")
            grads_full["fox_w_o"] = mm_tn(o, dy, name=f"bwd_w_fox_o{i}")
            dq, dk, dvf, dfk = attention_backward("fox", ops, sv["o_delta"], do, lse, bl, fox_scale, f"bwd_fox_attn{i}")
            df = dfk[:, :, :2, :].reshape(bl, FOX_HEADS // 2, nk, 2, tq).transpose(0, 2, 4, 1, 3).reshape(t, FOX_HEADS)
            df = jnp.pad(df, ((0, 0), (0, LANES - FOX_HEADS)))
            dz, db_f = fox_gate_backward(sv["z"], b_f, df, bl, f"bwd_fox_gate{i}")
            du = mm([(dq, wt_fox[0:d]), (dk, wt_fox[d:2 * d]), (dvf, wt_fox[2 * d:3 * d]), (dz, wt_f)],
                    trans_b=False, out_dtype=F32, name=f"bwd_fox_du{i}")
            u_f = sv["u"]
            grads_full["fox_w_in"] = jnp.concatenate(
                [mm_tn(dq, u_f, name=f"bwd_w_fox_q{i}"), mm_tn(dk, u_f, name=f"bwd_w_fox_k{i}"),
                 mm_tn(dvf, u_f, name=f"bwd_w_fox_v{i}"), mm_tn(dz, u_f, name=f"bwd_w_fox_f{i}")[:FOX_HEADS]], axis=0)
    bw = sublayer_backward(d_a, bl, "bwd_input", du=du, scale=(mods[0], 1), x_in=x2d)
    dmod[0][0], dmod[0][1] = bw["dshift"], bw["dscale"]
    grad_x = bw["dx"].reshape(bl, s, d)

    dmod_rows = jnp.concatenate([r.reshape(bl, d) for layer in dmod for r in layer], axis=0)
    dmod_rows = dmod_rows.reshape(DEPTH, 6, bl, d).transpose(0, 2, 1, 3)
    n_mod = dmod_rows.size // LANES
    ln_parts = [dg_ln[i][k] for i in range(DEPTH) for k in range(2)] + [db_ln[i][k] for i in range(DEPTH) for k in range(2)]
    small_g = jnp.concatenate([dmod_rows.reshape(-1, LANES), dg_q.reshape(-1, LANES), dg_kv.reshape(-1, LANES), db_f]
                              + [p.reshape(-1, LANES) for p in ln_parts], axis=0)
    n_small = small_g.shape[0]
    small_g = jnp.pad(small_g, ((0, (-n_small) % 8), (0, 0)))
    small_g_all = all_gather(small_g, "gather_small_grads")
    small_sum = sum_leading(small_g_all, "sum_small_grads")
    per_seq = DEPTH * 6 * d // LANES
    dmod_all = small_g_all[:, :n_mod].reshape(N_DEV, DEPTH, bl, 6 * d).transpose(1, 0, 2, 3)
    dmod_all = dmod_all.reshape(DEPTH, N_DEV * bl, 6 * d)
    o1 = n_mod
    grad_g_q = small_sum[o1:o1 + MLA_QR // LANES].reshape(1, MLA_QR)
    o1 += MLA_QR // LANES
    grad_g_kv = small_sum[o1:o1 + MLA_KVR // LANES].reshape(1, MLA_KVR)
    o1 += MLA_KVR // LANES
    grad_b_f = small_sum[o1:o1 + 1, :FOX_HEADS]
    o1 += 1
    n_ln_rows = DEPTH * 2 * d // LANES
    grad_ln_g_full = small_sum[o1:o1 + n_ln_rows].reshape(DEPTH, 2, d)
    grad_ln_b_full = small_sum[o1 + n_ln_rows:o1 + 2 * n_ln_rows].reshape(DEPTH, 2, d)
    shard = d // N_DEV
    grad_ln_g = lax.dynamic_slice_in_dim(grad_ln_g_full, dev * shard, shard, axis=2)
    grad_ln_b = lax.dynamic_slice_in_dim(grad_ln_b_full, dev * shard, shard, axis=2)
    by_seq = small_g_all[:, :n_mod].reshape(N_DEV, DEPTH, bl, 6 * d // LANES, LANES).transpose(0, 2, 1, 3, 4)
    grad_ada_b = sum_leading(by_seq.reshape(N_DEV * bl, per_seq, LANES), "sum_ada_b").reshape(DEPTH, 6 * d)
    dmod_cols = lax.dynamic_slice_in_dim(dmod_all, dev * ada_cols, ada_cols, axis=2)
    grad_ada_w = jnp.stack([mm_tn(c_act, dmod_cols[i], name=f"bwd_w_ada{i}") for i in range(DEPTH)])

    g_packed = jnp.concatenate([slot(nm, grads_full[nm].reshape(N_DEV, rows_of[nm], PACK_COLS)) for nm, _ in local],
                               axis=1)
    from_sibling = rs_sibling_exchange(g_packed, "rs_sibling")
    chip_partial = rs_add_sibling(g_packed, from_sibling, "rs_add_sibling")
    from_chips = rs_chip_exchange(chip_partial, "rs_chips")
    g_mine = sum_leading(from_chips, "rs_sum_chips")

    def mine(nm, shape):
        return g_mine[offsets[nm]:offsets[nm] + rows_of[nm]].reshape(shape)

    def shard_t(nm, a):
        return t_last(mine(nm, t_last(a).shape))

    grads = {
        "mla_w_in": mine("mla_w_in", mla_w_in[0].shape)[None],
        "mla_g_q": grad_g_q,
        "mla_w_uq": shard_t("mla_w_uq", mla_w_uq[0])[None],
        "mla_g_kv": grad_g_kv,
        "mla_w_uk": shard_t("mla_w_uk", mla_w_uk[0])[None],
        "mla_w_uv": shard_t("mla_w_uv", mla_w_uv[0])[None],
        "mla_w_o": mine("mla_w_o", mla_w_o[0].shape)[None],
        "fox_w_in": shard_t("fox_w_in", fox_w_in[0])[None],
        "fox_b_f": grad_b_f,
        "fox_w_o": mine("fox_w_o", fox_w_o[0].shape)[None],
        "ada_w": grad_ada_w,
        "ada_b": grad_ada_b,
        "ffn_w_gate": jnp.stack([shard_t(f"gate{i}", ffn_w_gate[i]) for i in range(DEPTH)]),
        "ffn_w_up": jnp.stack([shard_t(f"up{i}", ffn_w_up[i]) for i in range(DEPTH)]),
        "ffn_w_down": jnp.stack([mine(f"down{i}", ffn_w_down[i].shape) for i in range(DEPTH)]),
        "ln_g": grad_ln_g,
        "ln_b": grad_ln_b,
    }
    weights = dict(mla_w_in=mla_w_in, mla_g_q=mla_g_q, mla_w_uq=mla_w_uq, mla_g_kv=mla_g_kv, mla_w_uk=mla_w_uk,
                   mla_w_uv=mla_w_uv, mla_w_o=mla_w_o, fox_w_in=fox_w_in, fox_b_f=fox_b_f, fox_w_o=fox_w_o,
                   ada_w=ada_w, ada_b=ada_b, ffn_w_gate=ffn_w_gate, ffn_w_up=ffn_w_up, ffn_w_down=ffn_w_down,
                   ln_g=ln_g, ln_b=ln_b)
    first = dict(mla_w_in=m_mla_w_in, mla_g_q=m_mla_g_q, mla_w_uq=m_mla_w_uq, mla_g_kv=m_mla_g_kv, mla_w_uk=m_mla_w_uk,
                 mla_w_uv=m_mla_w_uv, mla_w_o=m_mla_w_o, fox_w_in=m_fox_w_in, fox_b_f=m_fox_b_f, fox_w_o=m_fox_w_o,
                 ada_w=m_ada_w, ada_b=m_ada_b, ffn_w_gate=m_ffn_w_gate, ffn_w_up=m_ffn_w_up, ffn_w_down=m_ffn_w_down,
                 ln_g=m_ln_g, ln_b=m_ln_b)
    second = dict(mla_w_in=v_mla_w_in, mla_g_q=v_mla_g_q, mla_w_uq=v_mla_w_uq, mla_g_kv=v_mla_g_kv, mla_w_uk=v_mla_w_uk,
                  mla_w_uv=v_mla_w_uv, mla_w_o=v_mla_w_o, fox_w_in=v_fox_w_in, fox_b_f=v_fox_b_f, fox_w_o=v_fox_w_o,
                  ada_w=v_ada_w, ada_b=v_ada_b, ffn_w_gate=v_ffn_w_gate, ffn_w_up=v_ffn_w_up, ffn_w_down=v_ffn_w_down,
                  ln_g=v_ln_g, ln_b=v_ln_b)
    order = list(weights)
    g_out, d_out, m_out, v_out = [], [], [], []
    for nm in order:
        g = grads[nm].reshape(weights[nm].shape)
        delta, new_m, new_v = adamw(weights[nm], g, first[nm], second[nm], f"adamw_{nm}")
        g_out.append(g)
        d_out.append(delta)
        m_out.append(new_m)
        v_out.append(new_v)
    return (loss, grad_x, *g_out, *d_out, *m_out, *v_out)
```

```python
import functools

import jax
import jax.numpy as jnp
from jax import lax
from jax.experimental import pallas as pl
from jax.experimental.pallas import tpu as pltpu

F32 = jnp.float32
BF16 = jnp.bfloat16
LANES = 128
N_DEV = 8
VMEM_LIMIT_BYTES = 56 * 1024 * 1024

DEPTH = 2
MLA_HEADS = 8
MLA_NOPE = 128
MLA_ROPE = 64
MLA_V = 128
MLA_QR = 256
MLA_KVR = 256
ROPE_THETA = 10000.0
FOX_HEADS = 16
FOX_HD = 64
ALPHA = (2.0 * DEPTH) ** 0.25
NORM_EPS = 1e-5
ADAM_LR = 0.001
ADAM_B1 = 0.9
ADAM_B2 = 0.999
ADAM_EPS = 1e-08
ADAM_WD = 0.01
ADAM_STEP = 10

MESH_AXES = ("x", "y", "c")
MESH = pl.DeviceIdType.MESH


def _params(*sem):
    return pltpu.CompilerParams(dimension_semantics=sem, vmem_limit_bytes=VMEM_LIMIT_BYTES)


def _tile(n, cap, mult=LANES):
    if n <= cap:
        return n
    best = None
    for t in range(mult, cap + 1, mult):
        if n % t == 0:
            best = t
    assert best is not None, (n, cap, mult)
    return best


def _dot(a, b, dims):
    return lax.dot_general(a, b, (dims, ((), ())), preferred_element_type=F32)


def _nn(a, b):
    return _dot(a, b, ((1,), (0,)))


def _nt(a, b):
    return _dot(a, b, ((1,), (1,)))


def _tn(a, b):
    return _dot(a, b, ((0,), (0,)))


def _me():
    return lax.axis_index("x"), lax.axis_index("y"), lax.axis_index("c")


def all_gather(x_loc, name):
    r, c = x_loc.shape

    def body(x_ref, out_ref, send_sems, recv_sems, local_sem):
        x, y, cc = _me()
        me, sibling = (x, y, cc), (x, y, 1 - cc)
        chips = [(1 - x, y), (x, 1 - y), (1 - x, 1 - y)]

        def rows(px, py, pc):
            return out_ref.at[4 * px + 2 * py + pc]

        def copy(k, block, to, src=None):
            return pltpu.make_async_remote_copy(
                src_ref=rows(*block) if src is None else src, dst_ref=rows(*block),
                send_sem=send_sems.at[k], recv_sem=recv_sems.at[k], device_id=to, device_id_type=MESH)

        mine = pltpu.make_async_copy(x_ref, rows(*me), local_sem)
        mine.start()
        first = [copy(0, me, sibling, src=x_ref)]
        first += [copy(1 + j, me, (*chip, cc), src=x_ref) for j, chip in enumerate(chips)]
        for cp in first:
            cp.start()
        passed = [copy(4 + j, (*chip, cc), sibling) for j, chip in enumerate(chips)]
        for j, chip in enumerate(chips):
            copy(1 + j, (*chip, cc), me).wait_recv()
            passed[j].start()
        copy(0, sibling, me).wait_recv()
        for j, chip in enumerate(chips):
            copy(4 + j, (*chip, 1 - cc), me).wait_recv()
        for cp in first + passed:
            cp.wait_send()
        mine.wait()

    return pl.pallas_call(
        body, name=name,
        out_shape=jax.ShapeDtypeStruct((N_DEV, r, c), x_loc.dtype),
        in_specs=[pl.BlockSpec(memory_space=pl.ANY)],
        out_specs=pl.BlockSpec(memory_space=pl.ANY),
        scratch_shapes=[pltpu.SemaphoreType.DMA((7,)), pltpu.SemaphoreType.DMA((7,)), pltpu.SemaphoreType.DMA(())],
    )(x_loc)


def rs_sibling_exchange(g, name):
    _, r, c = g.shape

    def body(g_ref, land_ref, send_sems, recv_sems):
        x, y, cc = _me()
        copies = []
        for k in range(4):
            px, py = k // 2, k % 2
            copies.append(pltpu.make_async_remote_copy(
                src_ref=g_ref.at[4 * px + 2 * py + (1 - cc)], dst_ref=land_ref.at[k],
                send_sem=send_sems.at[k], recv_sem=recv_sems.at[k], device_id=(x, y, 1 - cc), device_id_type=MESH))
        for cp in copies:
            cp.start()
        for cp in copies:
            cp.wait_recv()
        for cp in copies:
            cp.wait_send()

    return pl.pallas_call(
        body, name=name,
        out_shape=jax.ShapeDtypeStruct((4, r, c), g.dtype),
        in_specs=[pl.BlockSpec(memory_space=pl.ANY)],
        out_specs=pl.BlockSpec(memory_space=pl.ANY),
        scratch_shapes=[pltpu.SemaphoreType.DMA((4,)), pltpu.SemaphoreType.DMA((4,))],
    )(g)


def rs_chip_exchange(p, name):
    _, r, c = p.shape

    def body(p_ref, land_ref, send_sems, recv_sems, local_sem):
        x, y, cc = _me()
        mine = pltpu.make_async_copy(p_ref.at[2 * x + y], land_ref.at[2 * x + y], local_sem)
        mine.start()
        chips = [(1 - x, y), (x, 1 - y), (1 - x, 1 - y)]
        sends = [pltpu.make_async_remote_copy(
            src_ref=p_ref.at[2 * px + py], dst_ref=land_ref.at[2 * x + y],
            send_sem=send_sems.at[j], recv_sem=recv_sems.at[j], device_id=(px, py, cc), device_id_type=MESH)
            for j, (px, py) in enumerate(chips)]
        for cp in sends:
            cp.start()
        for j, (px, py) in enumerate(chips):
            pltpu.make_async_remote_copy(
                src_ref=p_ref.at[2 * x + y], dst_ref=land_ref.at[2 * px + py],
                send_sem=send_sems.at[j], recv_sem=recv_sems.at[j], device_id=(px, py, cc), device_id_type=MESH).wait_recv()
        for cp in sends:
            cp.wait_send()
        mine.wait()

    return pl.pallas_call(
        body, name=name,
        out_shape=jax.ShapeDtypeStruct((4, r, c), p.dtype),
        in_specs=[pl.BlockSpec(memory_space=pl.ANY)],
        out_specs=pl.BlockSpec(memory_space=pl.ANY),
        scratch_shapes=[pltpu.SemaphoreType.DMA((3,)), pltpu.SemaphoreType.DMA((3,)), pltpu.SemaphoreType.DMA(())],
    )(p)


def rs_add_sibling(g, land, name):
    _, r, c = g.shape
    tr = _tile(r, 512, 8)
    core = lax.axis_index("c").astype(jnp.int32).reshape(1)

    def body(core_ref, g_ref, l_ref, o_ref):
        o_ref[...] = (g_ref[...].astype(F32) + l_ref[...].astype(F32)).astype(o_ref.dtype)

    return pl.pallas_call(
        body, name=name,
        out_shape=jax.ShapeDtypeStruct((4, r, c), g.dtype),
        grid_spec=pltpu.PrefetchScalarGridSpec(
            num_scalar_prefetch=1, grid=(4, r // tr),
            in_specs=[pl.BlockSpec((1, tr, c), lambda k, i, core_ref: (2 * k + core_ref[0], i, 0)),
                      pl.BlockSpec((1, tr, c), lambda k, i, core_ref: (k, i, 0))],
            out_specs=pl.BlockSpec((1, tr, c), lambda k, i, core_ref: (k, i, 0))),
        compiler_params=_params("arbitrary", "arbitrary"),
    )(core, g, land)


def sum_leading(x, name):
    n, r, c = x.shape
    tr = _tile(r, 512, 16)

    def body(x_ref, o_ref):
        acc = x_ref[0].astype(F32)
        for k in range(1, n):
            acc = acc + x_ref[k].astype(F32)
        o_ref[...] = acc

    return pl.pallas_call(
        body, name=name,
        out_shape=jax.ShapeDtypeStruct((r, c), F32),
        grid=(r // tr,),
        in_specs=[pl.BlockSpec((n, tr, c), lambda i: (0, i, 0))],
        out_specs=pl.BlockSpec((tr, c), lambda i: (i, 0)),
        compiler_params=_params("arbitrary"),
    )(x)


MM_VMEM_BUDGET = 36 * 1024 * 1024
GRID_STEP_AS_BYTES = 1 << 20


def _mm_tiles(m, n, a_row_bytes, b_col_bytes, out_bytes):
    tms = [c for c in (2048, 1024, 512, 256, 128, 64, 32, 16, 8) if m % c == 0] or [m]
    tns = [c for c in range(LANES, min(n, 2048) + 1, LANES) if n % c == 0] or [n]
    best = None
    for tm in tms:
        for tn in tns:
            vmem = 2 * (tm * a_row_bytes + tn * b_col_bytes) + 2 * tm * tn * out_bytes + tm * tn * 4
            if vmem > MM_VMEM_BUDGET:
                continue
            steps = (m // tm) * (n // tn)
            cost = steps * GRID_STEP_AS_BYTES + (m // tm) * n * b_col_bytes + m * a_row_bytes
            if best is None or cost < best[0]:
                best = (cost, tm, tn)
    assert best is not None, (m, n, a_row_bytes, b_col_bytes)
    return best[1], best[2]


def mm(pairs, *, trans_b, out_dtype, name, out_slab=False, bias=None):
    a0 = pairs[0][0]
    m = a0.shape[1] if a0.ndim == 3 else a0.shape[0]
    n = pairs[0][1].shape[0] if trans_b else pairs[0][1].shape[1]
    a_row_bytes = sum((b.shape[1] if trans_b else b.shape[0]) * a.dtype.itemsize for a, b in pairs)
    b_col_bytes = sum((b.shape[1] if trans_b else b.shape[0]) * b.dtype.itemsize for _, b in pairs)
    tm, tn = _mm_tiles(m, n, a_row_bytes, b_col_bytes, jnp.dtype(out_dtype).itemsize)
    slabs = [a.ndim == 3 for a, _ in pairs]
    n_pairs = len(pairs)

    def body(*refs):
        o_ref = refs[-1]
        acc = bias_ref = None
        if bias is not None:
            bias_ref = refs[2 * n_pairs]
        for i in range(n_pairs):
            a_ref, b_ref = refs[2 * i], refs[2 * i + 1]
            if slabs[i]:
                a = jnp.concatenate([a_ref[s].astype(BF16) for s in range(a_ref.shape[0])], axis=1)
            else:
                a = a_ref[...].astype(BF16)
            b = b_ref[...].astype(BF16)
            part = _nt(a, b) if trans_b else _nn(a, b)
            acc = part if acc is None else acc + part
        if bias_ref is not None:
            acc = acc + bias_ref[...]
        if out_slab:
            for s in range(tn // LANES):
                o_ref[s] = acc[:, s * LANES:(s + 1) * LANES].astype(out_dtype)
        else:
            o_ref[...] = acc.astype(out_dtype)

    in_specs, args = [], []
    for (a, b), slab in zip(pairs, slabs):
        if slab:
            in_specs.append(pl.BlockSpec((a.shape[0], tm, LANES), lambda i, j: (0, i, 0)))
        else:
            in_specs.append(pl.BlockSpec((tm, a.shape[1]), lambda i, j: (i, 0)))
        if trans_b:
            in_specs.append(pl.BlockSpec((tn, b.shape[1]), lambda i, j: (j, 0)))
        else:
            in_specs.append(pl.BlockSpec((b.shape[0], tn), lambda i, j: (0, j)))
        args += [a, b]
    if bias is not None:
        in_specs.append(pl.BlockSpec((1, tn), lambda i, j: (0, j)))
        args.append(bias)
    if out_slab:
        out_shape = jax.ShapeDtypeStruct((n // LANES, m, LANES), out_dtype)
        out_spec = pl.BlockSpec((tn // LANES, tm, LANES), lambda i, j: (j, i, 0))
    else:
        out_shape = jax.ShapeDtypeStruct((m, n), out_dtype)
        out_spec = pl.BlockSpec((tm, tn), lambda i, j: (i, j))
    return pl.pallas_call(
        body, name=name, out_shape=out_shape, grid=(m // tm, n // tn),
        in_specs=in_specs, out_specs=out_spec,
        compiler_params=_params("arbitrary", "arbitrary"),
    )(*args)


def mm_tn(a, b, *, name, tk_cap=1536, tn_cap=1024, tm_cap=512):
    slab = a.ndim == 3
    m = a.shape[1] if slab else a.shape[0]
    k = a.shape[0] * LANES if slab else a.shape[1]
    n = b.shape[1]
    tk = _tile(k, tk_cap)
    tn = _tile(n, tn_cap)
    tm = _tile(m, tm_cap, 8)

    def body(a_ref, b_ref, o_ref):
        @pl.when(pl.program_id(2) == 0)
        def _():
            o_ref[...] = jnp.zeros_like(o_ref)

        bb = b_ref[...].astype(BF16)
        if slab:
            for s in range(tk // LANES):
                o_ref[s * LANES:(s + 1) * LANES, :] += _tn(a_ref[s].astype(BF16), bb)
        else:
            o_ref[...] += _tn(a_ref[...].astype(BF16), bb)

    if slab:
        a_spec = pl.BlockSpec((tk // LANES, tm, LANES), lambda i, j, t: (i, t, 0))
    else:
        a_spec = pl.BlockSpec((tm, tk), lambda i, j, t: (t, i))
    return pl.pallas_call(
        body, name=name, out_shape=jax.ShapeDtypeStruct((k, n), F32), grid=(k // tk, n // tn, m // tm),
        in_specs=[a_spec, pl.BlockSpec((tm, tn), lambda i, j, t: (t, j))],
        out_specs=pl.BlockSpec((tk, tn), lambda i, j, t: (i, j)),
        compiler_params=_params("arbitrary", "arbitrary", "arbitrary"),
    )(a, b)


def _row_spec(d, k):
    return pl.BlockSpec((1, 1, d), lambda b, i: (6 * b + k, 0, 0))


def modulate(x, mod, k_shift, k_scale, bl, name):
    t, d = x.shape
    s = t // bl
    tm = _tile(s, 512, 8)
    nt = s // tm

    def body(x_ref, sh_ref, sc_ref, o_ref):
        o_ref[...] = (x_ref[...] * (1.0 + sc_ref[0]) + sh_ref[0]).astype(BF16)

    return pl.pallas_call(
        body, name=name, out_shape=jax.ShapeDtypeStruct((t, d), BF16), grid=(bl, nt),
        in_specs=[pl.BlockSpec((tm, d), lambda b, i: (b * nt + i, 0)), _row_spec(d, k_shift), _row_spec(d, k_scale)],
        out_specs=pl.BlockSpec((tm, d), lambda b, i: (b * nt + i, 0)),
        compiler_params=_params("arbitrary", "arbitrary"),
    )(x, mod, mod)


def _layer_norm_stats(r):
    mu = jnp.mean(r, axis=-1, keepdims=True)
    rc = r - mu
    var = jnp.mean(rc * rc, axis=-1, keepdims=True)
    rstd = lax.rsqrt(var + NORM_EPS)
    return rc * rstd, rstd


def residual_layer_norm(x, y, mod, k_gate, g, b, bl, name, next_mod=None):
    t, d = x.shape
    s = t // bl
    tm = _tile(s, 256, 8)
    nt = s // tm
    has_next = next_mod is not None

    def body(*refs):
        x_ref, y_ref, gt_ref, g_ref, b_ref = refs[:5]
        rest = refs[5:]
        if has_next:
            sh_ref, sc_ref, o_ref, r_ref, u_ref = rest
        else:
            o_ref, r_ref = rest
        r = ALPHA * x_ref[...] + (1.0 + gt_ref[0]) * y_ref[...]
        xhat, _ = _layer_norm_stats(r)
        out = xhat * g_ref[...] + b_ref[...]
        o_ref[...] = out
        r_ref[...] = r
        if has_next:
            u_ref[...] = (out * (1.0 + sc_ref[0]) + sh_ref[0]).astype(BF16)

    tok = pl.BlockSpec((tm, d), lambda bb, i: (bb * nt + i, 0))
    vec = pl.BlockSpec((1, d), lambda bb, i: (0, 0))
    in_specs = [tok, tok, _row_spec(d, k_gate), vec, vec]
    args = [x, y, mod, g, b]
    out_shape = [jax.ShapeDtypeStruct((t, d), F32), jax.ShapeDtypeStruct((t, d), F32)]
    out_specs = [tok, tok]
    if has_next:
        in_specs += [_row_spec(d, next_mod[0]), _row_spec(d, next_mod[1])]
        args += [mod if len(next_mod) == 2 else next_mod[2]] * 2
        out_shape.append(jax.ShapeDtypeStruct((t, d), BF16))
        out_specs.append(tok)
    return pl.pallas_call(
        body, name=name, out_shape=out_shape, grid=(bl, nt), in_specs=in_specs, out_specs=out_specs,
        compiler_params=_params("arbitrary", "arbitrary"),
    )(*args)


def loss_head(xo, target, name):
    t, d = xo.shape
    tm = _tile(t, 512, 8)

    def body(x_ref, t_ref, l_ref, dx_ref):
        @pl.when(pl.program_id(0) == 0)
        def _():
            l_ref[...] = jnp.zeros_like(l_ref)

        e = x_ref[...] - t_ref[...]
        l_ref[...] += jnp.sum(e * e, axis=0, keepdims=True) * (0.5 / d)
        dx_ref[...] = e * (1.0 / d)

    tok = pl.BlockSpec((tm, d), lambda i: (i, 0))
    return pl.pallas_call(
        body, name=name,
        out_shape=[jax.ShapeDtypeStruct((1, d), F32), jax.ShapeDtypeStruct((t, d), F32)],
        grid=(t // tm,), in_specs=[tok, tok],
        out_specs=[pl.BlockSpec((1, d), lambda i: (0, 0)), tok],
        compiler_params=_params("arbitrary"),
    )(xo, target)


def sublayer_backward(d_a, bl, name, *, du=None, scale=None, x_in=None, ln=None):
    t, d = d_a.shape
    s = t // bl
    tm = _tile(s, 256, 8)
    nt = s // tm
    has_mod = du is not None
    has_ln = ln is not None
    assert has_mod or has_ln
    assert has_ln or x_in is not None

    def body(*refs):
        refs = list(refs)
        da_ref = refs.pop(0)
        if has_mod:
            du_ref, sc_ref = refs.pop(0), refs.pop(0)
        if has_ln:
            r_ref, y_ref, g_ref, b_ref, gt_ref = (refs.pop(0) for _ in range(5))
        elif has_mod:
            xin_ref = refs.pop(0)
        dx_ref = refs.pop(0)
        if has_ln:
            dy_ref, dg_ref, db_ref, dgt_ref = (refs.pop(0) for _ in range(4))
        if has_mod:
            dsc_ref, dsh_ref = refs.pop(0), refs.pop(0)
        first_tile = pl.program_id(1) == 0
        first_step = jnp.logical_and(pl.program_id(0) == 0, first_tile)

        dout = da_ref[...]
        if has_ln:
            xhat, rstd = _layer_norm_stats(r_ref[...])
        if has_mod:
            duv = du_ref[...]
            dout = dout + duv * (1.0 + sc_ref[0])
            xin = xhat * g_ref[...] + b_ref[...] if has_ln else xin_ref[...]

            @pl.when(first_tile)
            def _():
                dsc_ref[...] = jnp.zeros_like(dsc_ref)
                dsh_ref[...] = jnp.zeros_like(dsh_ref)

            dsc_ref[0] += jnp.sum(duv * xin, axis=0, keepdims=True)
            dsh_ref[0] += jnp.sum(duv, axis=0, keepdims=True)
        if not has_ln:
            dx_ref[...] = dout
            return

        @pl.when(first_step)
        def _():
            dg_ref[...] = jnp.zeros_like(dg_ref)
            db_ref[...] = jnp.zeros_like(db_ref)

        @pl.when(first_tile)
        def _():
            dgt_ref[...] = jnp.zeros_like(dgt_ref)

        dg_ref[...] += jnp.sum(dout * xhat, axis=0, keepdims=True)
        db_ref[...] += jnp.sum(dout, axis=0, keepdims=True)
        dxh = dout * g_ref[...]
        dr = rstd * (dxh - jnp.mean(dxh, axis=-1, keepdims=True) - xhat * jnp.mean(dxh * xhat, axis=-1, keepdims=True))
        dx_ref[...] = ALPHA * dr
        dy_ref[...] = ((1.0 + gt_ref[0]) * dr).astype(BF16)
        dgt_ref[0] += jnp.sum(dr * y_ref[...], axis=0, keepdims=True)

    tok = pl.BlockSpec((tm, d), lambda bb, i: (bb * nt + i, 0))
    vec = pl.BlockSpec((1, d), lambda bb, i: (0, 0))
    seq = pl.BlockSpec((1, 1, d), lambda bb, i: (bb, 0, 0))
    in_specs, args = [tok], [d_a]
    if has_mod:
        in_specs += [tok, _row_spec(d, scale[1])]
        args += [du, scale[0]]
    if has_ln:
        r, y, g, b, gate = ln
        in_specs += [tok, tok, vec, vec, _row_spec(d, gate[1])]
        args += [r, y, g, b, gate[0]]
    elif has_mod:
        in_specs.append(tok)
        args.append(x_in)
    names = ["dx"]
    out_shape, out_specs = [jax.ShapeDtypeStruct((t, d), F32)], [tok]
    if has_ln:
        names += ["dy", "dg", "db", "dgate"]
        out_shape += [jax.ShapeDtypeStruct((t, d), BF16), jax.ShapeDtypeStruct((1, d), F32),
                      jax.ShapeDtypeStruct((1, d), F32), jax.ShapeDtypeStruct((bl, 1, d), F32)]
        out_specs += [tok, vec, vec, seq]
    if has_mod:
        names += ["dscale", "dshift"]
        out_shape += [jax.ShapeDtypeStruct((bl, 1, d), F32)] * 2
        out_specs += [seq, seq]
    outs = pl.pallas_call(
        body, name=name, out_shape=out_shape, grid=(bl, nt), in_specs=in_specs, out_specs=out_specs,
        compiler_params=_params("arbitrary", "arbitrary"),
    )(*args)
    return dict(zip(names, outs))


def _silu(a):
    return a * jax.nn.sigmoid(a)


def silu_rows(a, name):
    def body(a_ref, o_ref):
        o_ref[...] = _silu(a_ref[...]).astype(BF16)

    return pl.pallas_call(body, name=name, out_shape=jax.ShapeDtypeStruct(a.shape, BF16))(a)


def swiglu_forward(a, b, name):
    t, f = a.shape
    tm, tf = _tile(t, 512, 8), _tile(f, 1536)

    def body(a_ref, b_ref, h_ref):
        h_ref[...] = (_silu(a_ref[...]) * b_ref[...]).astype(BF16)

    spec = pl.BlockSpec((tm, tf), lambda i, j: (i, j))
    return pl.pallas_call(
        body, name=name, out_shape=jax.ShapeDtypeStruct((t, f), BF16), grid=(t // tm, f // tf),
        in_specs=[spec, spec], out_specs=spec, compiler_params=_params("arbitrary", "arbitrary"),
    )(a, b)


def swiglu_backward(dh, a, b, name):
    t, f = a.shape
    tm, tf = _tile(t, 512, 8), _tile(f, 1536)

    def body(dh_ref, a_ref, b_ref, da_ref, db_ref):
        av = a_ref[...]
        sig = jax.nn.sigmoid(av)
        dhv = dh_ref[...]
        da_ref[...] = (dhv * b_ref[...] * (sig * (1.0 + av * (1.0 - sig)))).astype(BF16)
        db_ref[...] = (dhv * (av * sig)).astype(BF16)

    spec = pl.BlockSpec((tm, tf), lambda i, j: (i, j))
    return pl.pallas_call(
        body, name=name, out_shape=[jax.ShapeDtypeStruct((t, f), BF16)] * 2, grid=(t // tm, f // tf),
        in_specs=[spec, spec, spec], out_specs=[spec, spec], compiler_params=_params("arbitrary", "arbitrary"),
    )(dh, a, b)


def rope_tables(pos, inv_freq, sign, name):
    t = pos.shape[0]
    tm = _tile(t, 512, 8)

    def body(p_ref, f_ref, s_ref, c_out, s_out):
        ang = p_ref[...] * f_ref[...]
        c_out[...] = jnp.cos(ang)
        s_out[...] = jnp.sin(ang) * s_ref[...]

    vec = pl.BlockSpec((1, LANES), lambda i: (0, 0))
    tab = pl.BlockSpec((tm, LANES), lambda i: (i, 0))
    return pl.pallas_call(
        body, name=name, out_shape=[jax.ShapeDtypeStruct((t, LANES), F32)] * 2, grid=(t // tm,),
        in_specs=[pl.BlockSpec((tm, 1), lambda i: (i, 0)), vec, vec], out_specs=[tab, tab],
        compiler_params=_params("arbitrary"),
    )(pos, inv_freq, sign)


def _rot_half(v):
    lane = lax.broadcasted_iota(jnp.int32, v.shape, v.ndim - 1)
    up = pltpu.roll(v, LANES - MLA_ROPE // 2, v.ndim - 1)
    down = pltpu.roll(v, MLA_ROPE // 2, v.ndim - 1)
    return jnp.where(lane % MLA_ROPE < MLA_ROPE // 2, up, down)


def _rope(v, cos, sin_signed):
    return v * cos + _rot_half(v) * sin_signed


def _rope_transposed(dv, cos, sin_signed):
    return dv * cos + _rot_half(dv * sin_signed)


def rope_slabs(v, cos, sin_signed, out_dtype, name, transposed=False):
    ns, t, _ = v.shape
    tm = _tile(t, 512, 8)
    fn = _rope_transposed if transposed else _rope

    def body(v_ref, c_ref, s_ref, o_ref):
        o_ref[0] = fn(v_ref[0].astype(F32), c_ref[...], s_ref[...]).astype(out_dtype)

    tab = pl.BlockSpec((tm, LANES), lambda j, i: (i, 0))
    spec = pl.BlockSpec((1, tm, LANES), lambda j, i: (j, i, 0))
    return pl.pallas_call(
        body, name=name, out_shape=jax.ShapeDtypeStruct(v.shape, out_dtype), grid=(ns, t // tm),
        in_specs=[spec, tab, tab], out_specs=spec, compiler_params=_params("arbitrary", "arbitrary"),
    )(v, cos, sin_signed)


def _rms(x):
    rinv = lax.rsqrt(jnp.mean(x * x, axis=-1, keepdims=True) + NORM_EPS)
    return x * rinv, rinv


def mla_latents_forward(h_in, g_q, g_kv, cos, sin_signed, name):
    t = h_in.shape[0]
    tm = _tile(t, 512, 8)

    def body(h_ref, gq_ref, gkv_ref, c_ref, s_ref, cq_ref, ckv_ref, kr_ref):
        cq_ref[...] = (_rms(h_ref[:, 0:MLA_QR])[0] * gq_ref[...]).astype(BF16)
        ckv_ref[...] = (_rms(h_ref[:, MLA_QR:MLA_QR + MLA_KVR])[0] * gkv_ref[...]).astype(BF16)
        kr_ref[...] = _rope(h_ref[:, MLA_QR + MLA_KVR:], c_ref[...], s_ref[...]).astype(BF16)

    def tok(w):
        return pl.BlockSpec((tm, w), lambda i: (i, 0))

    def vec(w):
        return pl.BlockSpec((1, w), lambda i: (0, 0))

    return pl.pallas_call(
        body, name=name,
        out_shape=[jax.ShapeDtypeStruct((t, MLA_QR), BF16), jax.ShapeDtypeStruct((t, MLA_KVR), BF16),
                   jax.ShapeDtypeStruct((t, LANES), BF16)],
        grid=(t // tm,),
        in_specs=[tok(h_in.shape[1]), vec(MLA_QR), vec(MLA_KVR), tok(LANES), tok(LANES)],
        out_specs=[tok(MLA_QR), tok(MLA_KVR), tok(LANES)],
        compiler_params=_params("arbitrary"),
    )(h_in, g_q, g_kv, cos, sin_signed)


def mla_latents_backward(h_in, dcq, dckv, dkr, g_q, g_kv, cos, sin_signed, name):
    t, w = h_in.shape
    tm = _tile(t, 512, 8)

    def body(h_ref, dcq_ref, dckv_ref, dkr_ref, gq_ref, gkv_ref, c_ref, s_ref, dh_ref, dgq_ref, dgkv_ref):
        @pl.when(pl.program_id(0) == 0)
        def _():
            dgq_ref[...] = jnp.zeros_like(dgq_ref)
            dgkv_ref[...] = jnp.zeros_like(dgkv_ref)

        def rms_bwd(x, dc, g_ref, dg_ref):
            xn, rinv = _rms(x)
            dg_ref[...] += jnp.sum(dc * xn, axis=0, keepdims=True)
            dxn = dc * g_ref[...]
            return rinv * (dxn - xn * jnp.mean(dxn * xn, axis=-1, keepdims=True))

        dq = rms_bwd(h_ref[:, 0:MLA_QR], dcq_ref[...], gq_ref, dgq_ref)
        dkv = rms_bwd(h_ref[:, MLA_QR:MLA_QR + MLA_KVR], dckv_ref[...], gkv_ref, dgkv_ref)
        dr = _rope_transposed(dkr_ref[...], c_ref[...], s_ref[...])
        dh_ref[...] = jnp.concatenate([dq, dkv, dr], axis=1).astype(BF16)

    def tok(ww):
        return pl.BlockSpec((tm, ww), lambda i: (i, 0))

    def vec(ww):
        return pl.BlockSpec((1, ww), lambda i: (0, 0))

    return pl.pallas_call(
        body, name=name,
        out_shape=[jax.ShapeDtypeStruct((t, w), BF16), jax.ShapeDtypeStruct((1, MLA_QR), F32),
                   jax.ShapeDtypeStruct((1, MLA_KVR), F32)],
        grid=(t // tm,),
        in_specs=[tok(w), tok(MLA_QR), tok(MLA_KVR), tok(LANES), vec(MLA_QR), vec(MLA_KVR), tok(LANES), tok(LANES)],
        out_specs=[tok(w), vec(MLA_QR), vec(MLA_KVR)],
        compiler_params=_params("arbitrary"),
    )(h_in, dcq, dckv, dkr, g_q, g_kv, cos, sin_signed)


def _tri(n, lower):
    r = lax.broadcasted_iota(jnp.int32, (n, n), 0)
    c = lax.broadcasted_iota(jnp.int32, (n, n), 1)
    return jnp.where(r >= c if lower else r <= c, 1.0, 0.0).astype(F32)


def _dot_exact(tri, v):
    hi = v.astype(BF16)
    mid = (v - hi.astype(F32)).astype(BF16)
    lo = (v - hi.astype(F32) - mid.astype(F32)).astype(BF16)
    t = tri.astype(BF16)
    return _nn(t, hi) + _nn(t, mid) + _nn(t, lo)


def fox_gate_forward(z, b_f, bl, name):
    t = z.shape[0]
    s = t // bl
    ch = LANES
    n_ch = s // ch

    def body(z_ref, b_ref, f_ref, fs_ref):
        tri = _tri(ch, True)
        carry = jnp.zeros((1, LANES), F32)
        for k in range(n_ch):
            x = z_ref[k * ch:(k + 1) * ch, :] + b_ref[...]
            logf = jnp.minimum(x, 0.0) - jnp.log(1.0 + jnp.exp(-jnp.abs(x)))
            cs = _dot_exact(tri, logf) + carry
            carry = cs[ch - 1:ch, :]
            f_ref[k * ch:(k + 1) * ch, :] = cs
            for h in range(FOX_HEADS):
                fs_ref[h, k * ch:(k + 1) * ch, :] = jnp.broadcast_to(cs[:, h:h + 1], (ch, LANES))

    return pl.pallas_call(
        body, name=name,
        out_shape=[jax.ShapeDtypeStruct((t, LANES), F32), jax.ShapeDtypeStruct((FOX_HEADS, t, LANES), F32)],
        grid=(bl,),
        in_specs=[pl.BlockSpec((s, LANES), lambda b: (b, 0)), pl.BlockSpec((1, LANES), lambda b: (0, 0))],
        out_specs=[pl.BlockSpec((s, LANES), lambda b: (b, 0)),
                   pl.BlockSpec((FOX_HEADS, s, LANES), lambda b: (0, b, 0))],
        compiler_params=_params("arbitrary"),
    )(z, b_f)


def fox_gate_backward(z, b_f, df, bl, name):
    t = z.shape[0]
    s = t // bl
    ch = LANES
    n_ch = s // ch

    def body(z_ref, b_ref, df_ref, dz_ref, db_ref):
        @pl.when(pl.program_id(0) == 0)
        def _():
            db_ref[...] = jnp.zeros_like(db_ref)

        tri = _tri(ch, False)
        carry = jnp.zeros((1, LANES), F32)
        for k in reversed(range(n_ch)):
            cs = _dot_exact(tri, df_ref[k * ch:(k + 1) * ch, :]) + carry
            carry = cs[0:1, :]
            x = z_ref[k * ch:(k + 1) * ch, :] + b_ref[...]
            dz = cs * (1.0 - jax.nn.sigmoid(x))
            dz_ref[k * ch:(k + 1) * ch, :] = dz
            db_ref[...] += jnp.sum(dz, axis=0, keepdims=True)

    tok = pl.BlockSpec((s, LANES), lambda b: (b, 0))
    vec = pl.BlockSpec((1, LANES), lambda b: (0, 0))
    return pl.pallas_call(
        body, name=name,
        out_shape=[jax.ShapeDtypeStruct((t, LANES), F32), jax.ShapeDtypeStruct((1, LANES), F32)],
        grid=(bl,), in_specs=[tok, vec, tok], out_specs=[tok, vec],
        compiler_params=_params("arbitrary"),
    )(z, b_f, df)


NEG_INF = float("-inf")


def _attn_tiles(s):
    return _tile(s, 256, 8)


def attention_forward(kind, ops, bl, scale, name):
    fox = kind == "fox"
    if fox:
        qkv, fq, fk = ops
        t = qkv.shape[1]
        n_pair = FOX_HEADS // 2
    else:
        qn, qr, kn, kr, v = ops
        t = qn.shape[1]
        n_pair = MLA_HEADS // 2
    s = t // bl
    tq = _attn_tiles(s)
    nq = s // tq
    half = LANES // 2

    def body(*refs):
        if fox:
            q_ref, k_ref, v_ref, fq_ref, fk_ref, o_ref, lse_ref, o32_ref = refs
        else:
            qn_ref, qr_ref, kn_ref, kr_ref, v_ref, o_ref, lse_ref = refs
        i = pl.program_id(2)
        row = lax.broadcasted_iota(jnp.int32, (tq, tq), 0)
        col = lax.broadcasted_iota(jnp.int32, (tq, tq), 1)
        heads = []
        for e in range(2):
            sl = slice(e * half, (e + 1) * half)
            if fox:
                heads.append((sl, q_ref[0, :, sl], None))
            else:
                heads.append((sl, qn_ref[e], qr_ref[0, :, sl]))
        dv = half if fox else LANES

        def wide(stat):
            return jnp.concatenate([stat] * (tq // LANES), axis=1)

        def step(j, carry, masked):
            rows = pl.ds(pl.multiple_of(j * tq, tq), tq)
            new = []
            for e, (sl, qa, qb) in enumerate(heads):
                m, l, acc = carry[e]
                if fox:
                    sc = _nt(qa, k_ref[0, rows, sl]) * scale + wide(fq_ref[e]) - fk_ref[0, j, e:e + 1, :]
                    vv = v_ref[0, rows, sl]
                else:
                    sc = (_nt(qa, kn_ref[e, rows, :]) + _nt(qb, kr_ref[rows, 0:half])) * scale
                    vv = v_ref[e, rows, :]
                if masked:
                    sc = jnp.where(row >= col, sc, NEG_INF)
                m_new = jnp.maximum(m, jnp.max(sc, axis=1, keepdims=True))
                p = jnp.exp(sc - m_new)
                a = jnp.exp(m - m_new)
                l = a * l + jnp.sum(p, axis=1, keepdims=True)
                p_hi = p.astype(BF16)
                acc = a * acc + _nn(p_hi, vv)
                if fox:
                    acc = acc + _nn((p - p_hi.astype(F32)).astype(BF16), vv)
                new.append((m_new, l, acc))
            return tuple(new)

        init = (jnp.full((tq, 1), NEG_INF, F32), jnp.zeros((tq, 1), F32), jnp.zeros((tq, dv), F32))
        carry = step(i, (init, init), True)
        carry = lax.fori_loop(0, i, lambda j, c: step(j, c, False), carry)
        outs = [acc / l for _, l, acc in carry]
        for e, (m, l, _) in enumerate(carry):
            lse_ref[e] = jnp.broadcast_to(m + jnp.log(l), (tq, LANES))
        if fox:
            o32 = jnp.concatenate(outs, axis=1)
            o32_ref[0] = o32
            o_ref[0] = o32.astype(BF16)
        else:
            o_ref[0] = outs[0].astype(BF16)
            o_ref[1] = outs[1].astype(BF16)

    def q_idx(b, g, i):
        return (g, b * nq + i, 0)

    if fox:
        nk = fk.shape[1]
        in_specs = [pl.BlockSpec((1, tq, LANES), q_idx),
                    pl.BlockSpec((1, s, LANES), lambda b, g, i: (n_pair + g, b, 0)),
                    pl.BlockSpec((1, s, LANES), lambda b, g, i: (2 * n_pair + g, b, 0)),
                    pl.BlockSpec((2, tq, LANES), q_idx),
                    pl.BlockSpec((1, nk, 8, tq), lambda b, g, i: (b * n_pair + g, 0, 0, 0))]
        args = [qkv, qkv, qkv, fq, fk]
        o_spec = pl.BlockSpec((1, tq, LANES), q_idx)
    else:
        in_specs = [pl.BlockSpec((2, tq, LANES), q_idx),
                    pl.BlockSpec((1, tq, LANES), q_idx),
                    pl.BlockSpec((2, s, LANES), lambda b, g, i: (g, b, 0)),
                    pl.BlockSpec((s, LANES), lambda b, g, i: (b, 0)),
                    pl.BlockSpec((2, s, LANES), lambda b, g, i: (g, b, 0))]
        args = [qn, qr, kn, kr, v]
        o_spec = pl.BlockSpec((2, tq, LANES), q_idx)
    out_shape = [jax.ShapeDtypeStruct((8, t, LANES), BF16), jax.ShapeDtypeStruct((2 * n_pair, t, LANES), F32)]
    out_specs = [o_spec, pl.BlockSpec((2, tq, LANES), q_idx)]
    if fox:
        out_shape.append(jax.ShapeDtypeStruct((8, t, LANES), F32))
        out_specs.append(o_spec)
    outs = pl.pallas_call(
        body, name=name, out_shape=out_shape, grid=(bl, n_pair, nq), in_specs=in_specs, out_specs=out_specs,
        compiler_params=_params("arbitrary", "arbitrary", "arbitrary"),
    )(*args)
    return (outs[0], outs[1], outs[2] if fox else outs[0])


def attention_backward(kind, ops, o, do, lse, bl, scale, name):
    fox = kind == "fox"
    if fox:
        qkv, fq, fk = ops
        t = qkv.shape[1]
        n_pair = FOX_HEADS // 2
    else:
        qn, qr, kn, kr, v = ops
        t = qn.shape[1]
        n_pair = MLA_HEADS // 2
    s = t // bl
    tq = _attn_tiles(s)
    nq = s // tq
    half = LANES // 2

    def body(*refs):
        if fox:
            (q_ref, k_ref, v_ref, fq_ref, fk_ref, o_ref, do_ref, lse_ref,
             dq_ref, dk_ref, dv_ref, dfk_ref, delta_scr, qt_scr, dot_scr) = refs
        else:
            (qn_ref, qr_ref, kn_ref, kr_ref, v_ref, o_ref, do_ref, lse_ref,
             dqn_ref, dqr_ref, dkn_ref, dv_ref, dkr_ref, delta_scr, qt_scr, qrt_scr, dot_scr) = refs
        g, j = pl.program_id(1), pl.program_id(2)
        row = lax.broadcasted_iota(jnp.int32, (tq, tq), 0)
        col = lax.broadcasted_iota(jnp.int32, (tq, tq), 1)
        krows = pl.ds(pl.multiple_of(j * tq, tq), tq)

        def transposed(v):
            return v.astype(F32).T.astype(BF16)

        def wide(stat):
            return jnp.concatenate([stat] * (tq // LANES), axis=1)

        @pl.when(j == 0)
        def _():
            if fox:
                dq_ref[...] = jnp.zeros_like(dq_ref)
            else:
                dqn_ref[...] = jnp.zeros_like(dqn_ref)
                dqr_ref[...] = jnp.zeros_like(dqr_ref)
            for ii in range(nq):
                rws = slice(ii * tq, (ii + 1) * tq)
                deltas = []
                if fox:
                    prod = do_ref[0, rws, :].astype(F32) * o_ref[0, rws, :].astype(F32)
                    for e in range(2):
                        deltas.append(jnp.sum(prod[:, e * half:(e + 1) * half], axis=1, keepdims=True))
                    qt_scr[ii] = transposed(q_ref[0, rws, :])
                    dot_scr[ii] = transposed(do_ref[0, rws, :])
                else:
                    for e in range(2):
                        prod = do_ref[e, rws, :].astype(F32) * o_ref[e, rws, :].astype(F32)
                        deltas.append(jnp.sum(prod, axis=1, keepdims=True))
                        qt_scr[e, ii] = transposed(qn_ref[e, rws, :])
                        dot_scr[e, ii] = transposed(do_ref[e, rws, :])
                    qrt_scr[ii] = transposed(qr_ref[0, rws, :])
                for e in range(2):
                    delta_scr[e, rws, :] = jnp.broadcast_to(deltas[e], (tq, LANES))

        if fox:
            dfk_ref[...] = jnp.zeros_like(dfk_ref)
        else:
            @pl.when(jnp.logical_and(g == 0, j == 0))
            def _():
                dkr_ref[...] = jnp.zeros_like(dkr_ref)

        heads = []
        for e in range(2):
            sl = slice(e * half, (e + 1) * half)
            if fox:
                heads.append((sl, k_ref[0, :, sl], v_ref[0, :, sl], fk_ref[0, 0, e:e + 1, :]))
            else:
                heads.append((sl, kn_ref[e], v_ref[e], kr_ref[krows, 0:half]))
        dk_w = dv_w = half if fox else LANES

        def step(i, carry, masked):
            rows = pl.ds(pl.multiple_of(i * tq, tq), tq)
            new = []
            for e, (sl, k_e, v_e, x_e) in enumerate(heads):
                dk_acc, dv_acc, last = carry[e]
                if fox:
                    do_i = do_ref[0, rows, sl]
                    sc = _nt(q_ref[0, rows, sl], k_e) * scale + wide(fq_ref[e, rows, :]) - x_e
                else:
                    do_i = do_ref[e, rows, :]
                    sc = (_nt(qn_ref[e, rows, :], k_e) + _nt(qr_ref[0, rows, sl], x_e)) * scale
                if masked:
                    sc = jnp.where(row >= col, sc, NEG_INF)
                p = jnp.exp(sc - wide(lse_ref[e, rows, :]))
                dp = _nt(do_i, v_e)
                ds = p * (dp - wide(delta_scr[e, rows, :]))
                dsb = (ds * scale).astype(BF16)
                if fox:
                    fsl = slice(e * half, (e + 1) * half)
                    dv_acc = dv_acc + _nn(dot_scr[i, fsl, :], p.astype(BF16))
                    dk_acc = dk_acc + _nn(qt_scr[i, fsl, :], dsb)
                    dq_ref[0, rows, sl] += _nn(dsb, k_e)
                    last = last - jnp.sum(ds, axis=0, keepdims=True)
                else:
                    dv_acc = dv_acc + _nn(dot_scr[e, i], p.astype(BF16))
                    dk_acc = dk_acc + _nn(qt_scr[e, i], dsb)
                    dqn_ref[e, rows, :] += _nn(dsb, k_e)
                    dqr_ref[0, rows, sl] += _nn(dsb, x_e)
                    last = last + _nn(qrt_scr[i, e * half:(e + 1) * half, :], dsb)
                new.append((dk_acc, dv_acc, last))
            return tuple(new)

        last0 = jnp.zeros((1, tq), F32) if fox else jnp.zeros((half, tq), F32)
        init = (jnp.zeros((dk_w, tq), F32), jnp.zeros((dv_w, tq), F32), last0)
        carry = step(j, (init, init), True)
        carry = lax.fori_loop(j + 1, nq, lambda i, c: step(i, c, False), carry)
        if fox:
            for e in range(2):
                dfk_ref[0, 0, e:e + 1, :] = carry[e][2]
            dk_ref[0] = jnp.concatenate([carry[0][0], carry[1][0]], axis=0).T.astype(BF16)
            dv_ref[0] = jnp.concatenate([carry[0][1], carry[1][1]], axis=0).T.astype(BF16)
        else:
            for e in range(2):
                dkn_ref[e] = carry[e][0].T.astype(BF16)
                dv_ref[e] = carry[e][1].T.astype(BF16)
            dkr_t = carry[0][2] + carry[1][2]
            dkr_ref[krows, :] += jnp.concatenate([dkr_t, jnp.zeros_like(dkr_t)], axis=0).T

    def whole(b, g, j):
        return (g, b, 0)

    def kblk(b, g, j):
        return (g, b * nq + j, 0)

    if fox:
        in_specs = [pl.BlockSpec((1, s, LANES), whole),
                    pl.BlockSpec((1, tq, LANES), lambda b, g, j: (n_pair + g, b * nq + j, 0)),
                    pl.BlockSpec((1, tq, LANES), lambda b, g, j: (2 * n_pair + g, b * nq + j, 0)),
                    pl.BlockSpec((2, s, LANES), whole),
                    pl.BlockSpec((1, 1, 8, tq), lambda b, g, j: (b * n_pair + g, j, 0, 0)),
                    pl.BlockSpec((1, s, LANES), whole), pl.BlockSpec((1, s, LANES), whole),
                    pl.BlockSpec((2, s, LANES), whole)]
        args = [qkv, qkv, qkv, fq, fk, o, do, lse]
        out_shape = [jax.ShapeDtypeStruct((8, t, LANES), F32), jax.ShapeDtypeStruct((8, t, LANES), BF16),
                     jax.ShapeDtypeStruct((8, t, LANES), BF16), jax.ShapeDtypeStruct(fk.shape, F32)]
        out_specs = [pl.BlockSpec((1, s, LANES), whole), pl.BlockSpec((1, tq, LANES), kblk),
                     pl.BlockSpec((1, tq, LANES), kblk),
                     pl.BlockSpec((1, 1, 8, tq), lambda b, g, j: (b * n_pair + g, j, 0, 0))]
    else:
        pair = pl.BlockSpec((2, s, LANES), whole)
        pair_k = pl.BlockSpec((2, tq, LANES), kblk)
        in_specs = [pair, pl.BlockSpec((1, s, LANES), whole), pair_k,
                    pl.BlockSpec((s, LANES), lambda b, g, j: (b, 0)), pair_k,
                    pair, pair, pair]
        args = [qn, qr, kn, kr, v, o, do, lse]
        out_shape = [jax.ShapeDtypeStruct((8, t, LANES), F32), jax.ShapeDtypeStruct((4, t, LANES), F32),
                     jax.ShapeDtypeStruct((8, t, LANES), BF16), jax.ShapeDtypeStruct((8, t, LANES), BF16),
                     jax.ShapeDtypeStruct((t, LANES), F32)]
        out_specs = [pair, pl.BlockSpec((1, s, LANES), whole), pair_k, pair_k,
                     pl.BlockSpec((s, LANES), lambda b, g, j: (b, 0))]
    t_blocks = pltpu.VMEM((nq, LANES, tq), BF16)
    t_pairs = pltpu.VMEM((2, nq, LANES, tq), BF16)
    scratch = [pltpu.VMEM((2, s, LANES), F32)] + ([t_blocks, t_blocks] if fox else [t_pairs, t_blocks, t_pairs])
    return pl.pallas_call(
        body, name=name, out_shape=out_shape, grid=(bl, n_pair, nq), in_specs=in_specs, out_specs=out_specs,
        scratch_shapes=scratch, compiler_params=_params("arbitrary", "arbitrary", "arbitrary"),
    )(*args)


def adamw(w, g, m, v, name):
    shape = w.shape
    c = shape[-1]
    r = w.size // c
    tr = _tile(r, 512, 8)

    def body(w_ref, g_ref, m_ref, v_ref, d_ref, nm_ref, nv_ref):
        gv = g_ref[...]
        m2 = ADAM_B1 * m_ref[...] + (1.0 - ADAM_B1) * gv
        v2 = ADAM_B2 * v_ref[...] + (1.0 - ADAM_B2) * (gv * gv)
        m_hat = m2 / (1.0 - ADAM_B1 ** ADAM_STEP)
        v_hat = v2 / (1.0 - ADAM_B2 ** ADAM_STEP)
        d_ref[...] = -ADAM_LR * (m_hat / (jnp.sqrt(v_hat) + ADAM_EPS) + ADAM_WD * w_ref[...])
        nm_ref[...] = m2
        nv_ref[...] = v2

    spec = pl.BlockSpec((tr, c), lambda i: (i, 0))
    outs = pl.pallas_call(
        body, name=name, out_shape=[jax.ShapeDtypeStruct((r, c), F32)] * 3, grid=(r // tr,),
        in_specs=[spec] * 4, out_specs=[spec] * 3, compiler_params=_params("arbitrary"),
    )(*(a.reshape(r, c) for a in (w, g, m, v)))
    return tuple(a.reshape(shape) for a in outs)


PACK_COLS = 1024


def _pack_rows(a):
    return a.reshape(-1, PACK_COLS)


def kernel(x, c, positions, mla_w_in, mla_g_q, mla_w_uq, mla_g_kv, mla_w_uk, mla_w_uv, mla_w_o, fox_w_in, fox_b_f, fox_w_o, ada_w, ada_b, ffn_w_gate, ffn_w_up, ffn_w_down, ln_g, ln_b, loss_target, m_mla_w_in, m_mla_g_q, m_mla_w_uq, m_mla_g_kv, m_mla_w_uk, m_mla_w_uv, m_mla_w_o, m_fox_w_in, m_fox_b_f, m_fox_w_o, m_ada_w, m_ada_b, m_ffn_w_gate, m_ffn_w_up, m_ffn_w_down, m_ln_g, m_ln_b, v_mla_w_in, v_mla_g_q, v_mla_w_uq, v_mla_g_kv, v_mla_w_uk, v_mla_w_uv, v_mla_w_o, v_fox_w_in, v_fox_b_f, v_fox_w_o, v_ada_w, v_ada_b, v_ffn_w_gate, v_ffn_w_up, v_ffn_w_down, v_ln_g, v_ln_b):
    bl, s, d = x.shape
    t = bl * s
    ff = ffn_w_gate.shape[-1] * N_DEV
    dev = 4 * lax.axis_index("x") + 2 * lax.axis_index("y") + lax.axis_index("c")
    ada_cols = ada_w.shape[-1]
    fox_in = fox_w_in.shape[-1] * N_DEV
    mla_in = mla_w_in.shape[-1]
    mla_in_pad = mla_in + (-mla_in) % LANES

    def t_last(a):
        return jnp.swapaxes(a, -1, -2)

    local = [
        ("mla_w_in", mla_w_in[0]),
        ("mla_w_uq", t_last(mla_w_uq[0])),
        ("mla_w_uk", t_last(mla_w_uk[0])),
        ("mla_w_uv", t_last(mla_w_uv[0])),
        ("mla_w_o", mla_w_o[0]),
        ("fox_w_in", t_last(fox_w_in[0])),
        ("fox_w_o", fox_w_o[0]),
    ]
    for i in range(DEPTH):
        local += [(f"gate{i}", t_last(ffn_w_gate[i])), (f"up{i}", t_last(ffn_w_up[i])), (f"down{i}", ffn_w_down[i])]
    offsets, rows_of, slot_of = {}, {}, {}
    pack_rows = 0
    for nm, a in local:
        rows_of[nm] = a.size // PACK_COLS
        slot_of[nm] = rows_of[nm] + (-rows_of[nm]) % 16
        offsets[nm] = pack_rows
        pack_rows += slot_of[nm]

    def slot(nm, rows):
        pad = [(0, 0)] * rows.ndim
        pad[-2] = (0, slot_of[nm] - rows_of[nm])
        return jnp.pad(rows, pad)

    packed = jnp.concatenate([slot(nm, _pack_rows(a).astype(BF16)) for nm, a in local], axis=0)
    gathered = all_gather(packed, "gather_weights")

    def full(nm, cols):
        blk = gathered[:, offsets[nm]:offsets[nm] + rows_of[nm], :]
        return blk.reshape(-1, cols)

    w_in = jnp.pad(full("mla_w_in", mla_in), ((0, 0), (0, mla_in_pad - mla_in)))
    wt_uq = full("mla_w_uq", MLA_QR).reshape(MLA_HEADS, MLA_NOPE + MLA_ROPE, MLA_QR)
    wt_uq_n = wt_uq[:, :MLA_NOPE].reshape(MLA_HEADS * MLA_NOPE, MLA_QR)
    wt_uq_r = wt_uq[:, MLA_NOPE:].reshape(MLA_HEADS * MLA_ROPE, MLA_QR)
    wt_uk = full("mla_w_uk", MLA_KVR)
    wt_uv = full("mla_w_uv", MLA_KVR)
    w_mo = full("mla_w_o", d)
    wt_fox = full("fox_w_in", d)
    wt_qkv = wt_fox[:3 * d]
    wt_f = jnp.pad(wt_fox[3 * d:], ((0, LANES - FOX_HEADS), (0, 0)))
    w_fo = full("fox_w_o", d)
    wt_gate = [full(f"gate{i}", d) for i in range(DEPTH)]
    wt_up = [full(f"up{i}", d) for i in range(DEPTH)]
    w_down = [full(f"down{i}", d) for i in range(DEPTH)]

    small = jnp.concatenate([c.reshape(-1, LANES), ln_g.reshape(-1, LANES), ln_b.reshape(-1, LANES)], axis=0)
    small_rows = small.shape[0]
    small = jnp.pad(small, ((0, (-small_rows) % 8), (0, 0)))
    small_all = all_gather(small, "gather_small")
    c_rows = bl * d // LANES
    c_all = small_all[:, :c_rows].reshape(N_DEV * bl, d)
    n_ln = DEPTH * 2
    ln_g_all = small_all[:, c_rows:c_rows + n_ln, :].transpose(1, 0, 2).reshape(DEPTH, 2, 1, d)
    ln_b_all = small_all[:, c_rows + n_ln:c_rows + 2 * n_ln, :].transpose(1, 0, 2).reshape(DEPTH, 2, 1, d)

    c_act = silu_rows(c_all, "silu_c")
    ada_b_loc = lax.dynamic_slice_in_dim(ada_b, dev * ada_cols, ada_cols, axis=1)
    mod_cols = [mm([(c_act, ada_w[i])], trans_b=False, out_dtype=F32, name=f"ada_fwd{i}", bias=ada_b_loc[i][None, :])
                for i in range(DEPTH)]
    mod_all = all_gather(jnp.concatenate(mod_cols, axis=0), "gather_mod")
    mod_all = mod_all.reshape(N_DEV, DEPTH, N_DEV * bl, ada_cols).transpose(1, 2, 0, 3).reshape(DEPTH, N_DEV * bl, 6 * d)
    mod_mine = lax.dynamic_slice_in_dim(mod_all, dev * bl, bl, axis=1)
    mods = [mod_mine[i].reshape(bl * 6, 1, d) for i in range(DEPTH)]

    half_r = MLA_ROPE // 2
    inv_freq = ROPE_THETA ** (-jnp.arange(half_r, dtype=F32) / half_r)
    inv_freq = jnp.tile(inv_freq, LANES // half_r)[None, :]
    sign = jnp.tile(jnp.concatenate([-jnp.ones((half_r,), F32), jnp.ones((half_r,), F32)]), LANES // MLA_ROPE)[None, :]
    cos_t, sin_t = rope_tables(positions.astype(F32).reshape(t, 1), inv_freq, sign, "rope_tables")

    x2d = x.reshape(t, d)
    g_q, g_kv = mla_g_q.reshape(1, MLA_QR), mla_g_kv.reshape(1, MLA_KVR)
    b_f = jnp.pad(fox_b_f.reshape(1, FOX_HEADS), ((0, 0), (0, LANES - FOX_HEADS)))
    mla_scale = (MLA_NOPE + MLA_ROPE) ** -0.5
    fox_scale = FOX_HD ** -0.5
    tq = _attn_tiles(s)
    nk = s // tq

    saved = []
    u = modulate(x2d, mods[0], 0, 1, bl, "modulate0")
    xin = x2d
    for i in range(DEPTH):
        sv = {"u": u, "x_in": xin}
        if i % 2 == 0:
            h_in = mm([(u, w_in)], trans_b=False, out_dtype=F32, name=f"mla_in{i}")
            c_q, c_kv, k_r = mla_latents_forward(h_in, g_q, g_kv, cos_t, sin_t, f"mla_latents{i}")
            q_n = mm([(c_q, wt_uq_n)], trans_b=True, out_dtype=BF16, out_slab=True, name=f"mla_qn{i}")
            q_r_raw = mm([(c_q, wt_uq_r)], trans_b=True, out_dtype=F32, out_slab=True, name=f"mla_qr{i}")
            q_r = rope_slabs(q_r_raw, cos_t, sin_t, BF16, f"mla_qrope{i}")
            k_n = mm([(c_kv, wt_uk)], trans_b=True, out_dtype=BF16, out_slab=True, name=f"mla_kn{i}")
            v_m = mm([(c_kv, wt_uv)], trans_b=True, out_dtype=BF16, out_slab=True, name=f"mla_v{i}")
            ops = (q_n, q_r, k_n, k_r, v_m)
            o, lse, o_delta = attention_forward("mla", ops, bl, mla_scale, f"mla_attn{i}")
            y = mm([(o, w_mo)], trans_b=False, out_dtype=F32, name=f"mla_out{i}")
            sv.update(h_in=h_in, c_q=c_q, c_kv=c_kv, ops=ops, o=o, lse=lse, o_delta=o_delta)
        else:
            qkv = mm([(u, wt_qkv)], trans_b=True, out_dtype=BF16, out_slab=True, name=f"fox_qkv{i}")
            z = mm([(u, wt_f)], trans_b=True, out_dtype=F32, name=f"fox_z{i}")
            f_tok, f_q = fox_gate_forward(z, b_f, bl, f"fox_gate{i}")
            f_k = f_tok[:, :FOX_HEADS].reshape(bl, nk, tq, FOX_HEADS // 2, 2).transpose(0, 3, 1, 4, 2)
            f_k = jnp.pad(f_k.reshape(bl * FOX_HEADS // 2, nk, 2, tq), ((0, 0), (0, 0), (0, 6), (0, 0)))
            ops = (qkv, f_q, f_k)
            o, lse, o_delta = attention_forward("fox", ops, bl, fox_scale, f"fox_attn{i}")
            y = mm([(o, w_fo)], trans_b=False, out_dtype=F32, name=f"fox_out{i}")
            sv.update(z=z, ops=ops, o=o, lse=lse, o_delta=o_delta)
        x1, r1, u2 = residual_layer_norm(xin, y, mods[i], 2, ln_g_all[i, 0], ln_b_all[i, 0], bl, f"ln_mix{i}",
                                         next_mod=(3, 4))
        a = mm([(u2, wt_gate[i])], trans_b=True, out_dtype=F32, name=f"ffn_gate{i}")
        bb = mm([(u2, wt_up[i])], trans_b=True, out_dtype=F32, name=f"ffn_up{i}")
        h = swiglu_forward(a, bb, f"swiglu{i}")
        y2 = mm([(h, w_down[i])], trans_b=False, out_dtype=F32, name=f"ffn_down{i}")
        sv.update(y=y, r1=r1, u2=u2, a=a, bb=bb, h=h, y2=y2)
        if i + 1 < DEPTH:
            xin, r2, u = residual_layer_norm(x1, y2, mods[i], 5, ln_g_all[i, 1], ln_b_all[i, 1], bl, f"ln_ffn{i}",
                                             next_mod=(0, 1, mods[i + 1]))
        else:
            xin, r2 = residual_layer_norm(x1, y2, mods[i], 5, ln_g_all[i, 1], ln_b_all[i, 1], bl, f"ln_ffn{i}")
        sv.update(r2=r2)
        saved.append(sv)

    loss_cols, d_x = loss_head(xin, loss_target.reshape(t, d), "loss_head")

    grads_full = {}
    dmod = [[None] * 6 for _ in range(DEPTH)]
    dg_ln = [[None, None] for _ in range(DEPTH)]
    db_ln = [[None, None] for _ in range(DEPTH)]
    dg_q = dg_kv = db_f = None
    d_a, du = d_x, None
    for i in reversed(range(DEPTH)):
        sv = saved[i]
        ln2 = (sv["r2"], sv["y2"], ln_g_all[i, 1], ln_b_all[i, 1], (mods[i], 5))
        if du is None:
            bw = sublayer_backward(d_a, bl, f"bwd_ln_ffn{i}", ln=ln2)
        else:
            bw = sublayer_backward(d_a, bl, f"bwd_ln_ffn{i}", du=du, scale=(mods[i + 1], 1), ln=ln2)
            dmod[i + 1][0], dmod[i + 1][1] = bw["dshift"], bw["dscale"]
        dmod[i][5], dg_ln[i][1], db_ln[i][1] = bw["dgate"], bw["dg"], bw["db"]
        dy2 = bw["dy"]
        dh = mm([(dy2, w_down[i])], trans_b=True, out_dtype=F32, name=f"bwd_ffn_dh{i}")
        da, dbb = swiglu_backward(dh, sv["a"], sv["bb"], f"bwd_swiglu{i}")
        du2 = mm([(da, wt_gate[i]), (dbb, wt_up[i])], trans_b=False, out_dtype=F32, name=f"bwd_ffn_du{i}")
        grads_full[f"down{i}"] = mm_tn(sv["h"], dy2, name=f"bwd_w_down{i}")
        grads_full[f"gate{i}"] = mm_tn(da, sv["u2"], name=f"bwd_w_gate{i}")
        grads_full[f"up{i}"] = mm_tn(dbb, sv["u2"], name=f"bwd_w_up{i}")
        bw = sublayer_backward(bw["dx"], bl, f"bwd_ln_mix{i}", du=du2, scale=(mods[i], 4),
                               ln=(sv["r1"], sv["y"], ln_g_all[i, 0], ln_b_all[i, 0], (mods[i], 2)))
        dmod[i][3], dmod[i][4], dmod[i][2] = bw["dshift"], bw["dscale"], bw["dgate"]
        dg_ln[i][0], db_ln[i][0] = bw["dg"], bw["db"]
        d_a, dy = bw["dx"], bw["dy"]
        o, lse, ops = sv["o"], sv["lse"], sv["ops"]
        if i % 2 == 0:
            do = mm([(dy, w_mo)], trans_b=True, out_dtype=BF16, out_slab=True, name=f"bwd_mla_do{i}")
            grads_full["mla_w_o"] = mm_tn(o, dy, name=f"bwd_w_mla_o{i}")
            dqn, dqr, dkn, dvm, dkr = attention_backward("mla", ops, sv["o_delta"], do, lse, bl, mla_scale,
                                                         f"bwd_mla_attn{i}")
            dqr = rope_slabs(dqr, cos_t, sin_t, F32, f"bwd_mla_qrope{i}", transposed=True)
            dcq = mm([(dqn, wt_uq_n), (dqr, wt_uq_r)], trans_b=False, out_dtype=F32, name=f"bwd_mla_dcq{i}")
            dckv = mm([(dkn, wt_uk), (dvm, wt_uv)], trans_b=False, out_dtype=F32, name=f"bwd_mla_dckv{i}")
            d_uq_n = mm_tn(dqn, sv["c_q"], name=f"bwd_w_uq_n{i}").reshape(MLA_HEADS, MLA_NOPE, MLA_QR)
            d_uq_r = mm_tn(dqr, sv["c_q"], name=f"bwd_w_uq_r{i}").reshape(MLA_HEADS, MLA_ROPE, MLA_QR)
            grads_full["mla_w_uq"] = jnp.concatenate([d_uq_n, d_uq_r], axis=1)
            grads_full["mla_w_uk"] = mm_tn(dkn, sv["c_kv"], name=f"bwd_w_uk{i}")
            grads_full["mla_w_uv"] = mm_tn(dvm, sv["c_kv"], name=f"bwd_w_uv{i}")
            dh_in, dg_q, dg_kv = mla_latents_backward(sv["h_in"], dcq, dckv, dkr, g_q, g_kv, cos_t, sin_t,
                                                      f"bwd_mla_latents{i}")
            du = mm([(dh_in, w_in)], trans_b=True, out_dtype=F32, name=f"bwd_mla_du{i}")
            grads_full["mla_w_in"] = mm_tn(sv["u"], dh_in, name=f"bwd_w_mla_in{i}")[:, :mla_in]
        else:
            do = mm([(dy, w_fo)], trans_b=True, out_dtype=BF16, out_slab=True, name=f"bwd_fox_do{i}")
            grads_full["fox_w_o"] = mm_tn(o, dy, name=f"bwd_w_fox_o{i}")
            dq, dk, dvf, dfk = attention_backward("fox", ops, sv["o_delta"], do, lse, bl, fox_scale, f"bwd_fox_attn{i}")
            df = dfk[:, :, :2, :].reshape(bl, FOX_HEADS // 2, nk, 2, tq).transpose(0, 2, 4, 1, 3).reshape(t, FOX_HEADS)
            df = jnp.pad(df, ((0, 0), (0, LANES - FOX_HEADS)))
            dz, db_f = fox_gate_backward(sv["z"], b_f, df, bl, f"bwd_fox_gate{i}")
            du = mm([(dq, wt_fox[0:d]), (dk, wt_fox[d:2 * d]), (dvf, wt_fox[2 * d:3 * d]), (dz, wt_f)],
                    trans_b=False, out_dtype=F32, name=f"bwd_fox_du{i}")
            u_f = sv["u"]
            grads_full["fox_w_in"] = jnp.concatenate(
                [mm_tn(dq, u_f, name=f"bwd_w_fox_q{i}"), mm_tn(dk, u_f, name=f"bwd_w_fox_k{i}"),
                 mm_tn(dvf, u_f, name=f"bwd_w_fox_v{i}"), mm_tn(dz, u_f, name=f"bwd_w_fox_f{i}")[:FOX_HEADS]], axis=0)
    bw = sublayer_backward(d_a, bl, "bwd_input", du=du, scale=(mods[0], 1), x_in=x2d)
    dmod[0][0], dmod[0][1] = bw["dshift"], bw["dscale"]
    grad_x = bw["dx"].reshape(bl, s, d)

    dmod_rows = jnp.concatenate([r.reshape(bl, d) for layer in dmod for r in layer], axis=0)
    dmod_rows = dmod_rows.reshape(DEPTH, 6, bl, d).transpose(0, 2, 1, 3)
    n_mod = dmod_rows.size // LANES
    ln_parts = [dg_ln[i][k] for i in range(DEPTH) for k in range(2)] + [db_ln[i][k] for i in range(DEPTH) for k in range(2)]
    small_g = jnp.concatenate([dmod_rows.reshape(-1, LANES), dg_q.reshape(-1, LANES), dg_kv.reshape(-1, LANES), db_f]
                              + [p.reshape(-1, LANES) for p in ln_parts] + [loss_cols.reshape(-1, LANES)], axis=0)
    n_small = small_g.shape[0]
    small_g = jnp.pad(small_g, ((0, (-n_small) % 8), (0, 0)))
    small_g_all = all_gather(small_g, "gather_small_grads")
    small_sum = sum_leading(small_g_all, "sum_small_grads")
    per_seq = DEPTH * 6 * d // LANES
    dmod_all = small_g_all[:, :n_mod].reshape(N_DEV, DEPTH, bl, 6 * d).transpose(1, 0, 2, 3)
    dmod_all = dmod_all.reshape(DEPTH, N_DEV * bl, 6 * d)
    o1 = n_mod
    grad_g_q = small_sum[o1:o1 + MLA_QR // LANES].reshape(1, MLA_QR)
    o1 += MLA_QR // LANES
    grad_g_kv = small_sum[o1:o1 + MLA_KVR // LANES].reshape(1, MLA_KVR)
    o1 += MLA_KVR // LANES
    grad_b_f = small_sum[o1:o1 + 1, :FOX_HEADS]
    o1 += 1
    n_ln_rows = DEPTH * 2 * d // LANES
    grad_ln_g_full = small_sum[o1:o1 + n_ln_rows].reshape(DEPTH, 2, d)
    grad_ln_b_full = small_sum[o1 + n_ln_rows:o1 + 2 * n_ln_rows].reshape(DEPTH, 2, d)
    loss = jnp.sum(small_sum[o1 + 2 * n_ln_rows:o1 + 2 * n_ln_rows + d // LANES])
    shard = d // N_DEV
    grad_ln_g = lax.dynamic_slice_in_dim(grad_ln_g_full, dev * shard, shard, axis=2)
    grad_ln_b = lax.dynamic_slice_in_dim(grad_ln_b_full, dev * shard, shard, axis=2)
    by_seq = small_g_all[:, :n_mod].reshape(N_DEV, DEPTH, bl, 6 * d // LANES, LANES).transpose(0, 2, 1, 3, 4)
    grad_ada_b = sum_leading(by_seq.reshape(N_DEV * bl, per_seq, LANES), "sum_ada_b").reshape(DEPTH, 6 * d)
    dmod_cols = lax.dynamic_slice_in_dim(dmod_all, dev * ada_cols, ada_cols, axis=2)
    grad_ada_w = jnp.stack([mm_tn(c_act, dmod_cols[i], name=f"bwd_w_ada{i}") for i in range(DEPTH)])

    g_packed = jnp.concatenate(
        [slot(nm, grads_full[nm].reshape(N_DEV, rows_of[nm], PACK_COLS).astype(BF16)) for nm, _ in local], axis=1)
    from_sibling = rs_sibling_exchange(g_packed, "rs_sibling")
    chip_partial = rs_add_sibling(g_packed, from_sibling, "rs_add_sibling")
    from_chips = rs_chip_exchange(chip_partial, "rs_chips")
    g_mine = sum_leading(from_chips, "rs_sum_chips")

    def mine(nm, shape):
        return g_mine[offsets[nm]:offsets[nm] + rows_of[nm]].reshape(shape)

    def shard_t(nm, a):
        return t_last(mine(nm, t_last(a).shape))

    grads = {
        "mla_w_in": mine("mla_w_in", mla_w_in[0].shape)[None],
        "mla_g_q": grad_g_q,
        "mla_w_uq": shard_t("mla_w_uq", mla_w_uq[0])[None],
        "mla_g_kv": grad_g_kv,
        "mla_w_uk": shard_t("mla_w_uk", mla_w_uk[0])[None],
        "mla_w_uv": shard_t("mla_w_uv", mla_w_uv[0])[None],
        "mla_w_o": mine("mla_w_o", mla_w_o[0].shape)[None],
        "fox_w_in": shard_t("fox_w_in", fox_w_in[0])[None],
        "fox_b_f": grad_b_f,
        "fox_w_o": mine("fox_w_o", fox_w_o[0].shape)[None],
        "ada_w": grad_ada_w,
        "ada_b": grad_ada_b,
        "ffn_w_gate": jnp.stack([shard_t(f"gate{i}", ffn_w_gate[i]) for i in range(DEPTH)]),
        "ffn_w_up": jnp.stack([shard_t(f"up{i}", ffn_w_up[i]) for i in range(DEPTH)]),
        "ffn_w_down": jnp.stack([mine(f"down{i}", ffn_w_down[i].shape) for i in range(DEPTH)]),
        "ln_g": grad_ln_g,
        "ln_b": grad_ln_b,
    }
    weights = dict(mla_w_in=mla_w_in, mla_g_q=mla_g_q, mla_w_uq=mla_w_uq, mla_g_kv=mla_g_kv, mla_w_uk=mla_w_uk,
                   mla_w_uv=mla_w_uv, mla_w_o=mla_w_o, fox_w_in=fox_w_in, fox_b_f=fox_b_f, fox_w_o=fox_w_o,
                   ada_w=ada_w, ada_b=ada_b, ffn_w_gate=ffn_w_gate, ffn_w_up=ffn_w_up, ffn_w_down=ffn_w_down,
                   ln_g=ln_g, ln_b=ln_b)
    first = dict(mla_w_in=m_mla_w_in, mla_g_q=m_mla_g_q, mla_w_uq=m_mla_w_uq, mla_g_kv=m_mla_g_kv, mla_w_uk=m_mla_w_uk,
                 mla_w_uv=m_mla_w_uv, mla_w_o=m_mla_w_o, fox_w_in=m_fox_w_in, fox_b_f=m_fox_b_f, fox_w_o=m_fox_w_o,
                 ada_w=m_ada_w, ada_b=m_ada_b, ffn_w_gate=m_ffn_w_gate, ffn_w_up=m_ffn_w_up, ffn_w_down=m_ffn_w_down,
                 ln_g=m_ln_g, ln_b=m_ln_b)
    second = dict(mla_w_in=v_mla_w_in, mla_g_q=v_mla_g_q, mla_w_uq=v_mla_w_uq, mla_g_kv=v_mla_g_kv, mla_w_uk=v_mla_w_uk,
                  mla_w_uv=v_mla_w_uv, mla_w_o=v_mla_w_o, fox_w_in=v_fox_w_in, fox_b_f=v_fox_b_f, fox_w_o=v_fox_w_o,
                  ada_w=v_ada_w, ada_b=v_ada_b, ffn_w_gate=v_ffn_w_gate, ffn_w_up=v_ffn_w_up, ffn_w_down=v_ffn_w_down,
                  ln_g=v_ln_g, ln_b=v_ln_b)
    order = list(weights)
    g_out, d_out, m_out, v_out = [], [], [], []
    for nm in order:
        g = grads[nm].reshape(weights[nm].shape)
        delta, new_m, new_v = adamw(weights[nm], g, first[nm], second[nm], f"adamw_{nm}")
        g_out.append(g)
        d_out.append(delta)
        m_out.append(new_m)
        v_out.append(new_v)
    return (loss, grad_x, *g_out, *d_out, *m_out, *v_out)
```

```python
import functools

import jax
import jax.numpy as jnp
from jax import lax
from jax.experimental import pallas as pl
from jax.experimental.pallas import tpu as pltpu

F32 = jnp.float32
BF16 = jnp.bfloat16
LANES = 128
N_DEV = 8
VMEM_LIMIT_BYTES = 56 * 1024 * 1024

DEPTH = 2
MLA_HEADS = 8
MLA_NOPE = 128
MLA_ROPE = 64
MLA_V = 128
MLA_QR = 256
MLA_KVR = 256
ROPE_THETA = 10000.0
FOX_HEADS = 16
FOX_HD = 64
ALPHA = (2.0 * DEPTH) ** 0.25
NORM_EPS = 1e-5
ADAM_LR = 0.001
ADAM_B1 = 0.9
ADAM_B2 = 0.999
ADAM_EPS = 1e-08
ADAM_WD = 0.01
ADAM_STEP = 10

MESH_AXES = ("x", "y", "c")
MESH = pl.DeviceIdType.MESH


def _params(*sem):
    return pltpu.CompilerParams(dimension_semantics=sem, vmem_limit_bytes=VMEM_LIMIT_BYTES)


def _tile(n, cap, mult=LANES):
    if n <= cap:
        return n
    best = None
    for t in range(mult, cap + 1, mult):
        if n % t == 0:
            best = t
    assert best is not None, (n, cap, mult)
    return best


def _dot(a, b, dims):
    return lax.dot_general(a, b, (dims, ((), ())), preferred_element_type=F32)


def _nn(a, b):
    return _dot(a, b, ((1,), (0,)))


def _nt(a, b):
    return _dot(a, b, ((1,), (1,)))


def _tn(a, b):
    return _dot(a, b, ((0,), (0,)))


def _me():
    return lax.axis_index("x"), lax.axis_index("y"), lax.axis_index("c")


def all_gather(x_loc, name):
    r, c = x_loc.shape

    def body(x_ref, out_ref, send_sems, recv_sems, local_sem):
        x, y, cc = _me()
        me, sibling = (x, y, cc), (x, y, 1 - cc)
        chips = [(1 - x, y), (x, 1 - y), (1 - x, 1 - y)]

        def rows(px, py, pc):
            return out_ref.at[4 * px + 2 * py + pc]

        def copy(k, block, to, src=None):
            return pltpu.make_async_remote_copy(
                src_ref=rows(*block) if src is None else src, dst_ref=rows(*block),
                send_sem=send_sems.at[k], recv_sem=recv_sems.at[k], device_id=to, device_id_type=MESH)

        mine = pltpu.make_async_copy(x_ref, rows(*me), local_sem)
        mine.start()
        first = [copy(0, me, sibling, src=x_ref)]
        first += [copy(1 + j, me, (*chip, cc), src=x_ref) for j, chip in enumerate(chips)]
        for cp in first:
            cp.start()
        passed = [copy(4 + j, (*chip, cc), sibling) for j, chip in enumerate(chips)]
        for j, chip in enumerate(chips):
            copy(1 + j, (*chip, cc), me).wait_recv()
            passed[j].start()
        copy(0, sibling, me).wait_recv()
        for j, chip in enumerate(chips):
            copy(4 + j, (*chip, 1 - cc), me).wait_recv()
        for cp in first + passed:
            cp.wait_send()
        mine.wait()

    return pl.pallas_call(
        body, name=name,
        out_shape=jax.ShapeDtypeStruct((N_DEV, r, c), x_loc.dtype),
        in_specs=[pl.BlockSpec(memory_space=pl.ANY)],
        out_specs=pl.BlockSpec(memory_space=pl.ANY),
        scratch_shapes=[pltpu.SemaphoreType.DMA((7,)), pltpu.SemaphoreType.DMA((7,)), pltpu.SemaphoreType.DMA(())],
    )(x_loc)


def rs_sibling_exchange(g, name):
    _, r, c = g.shape

    def body(g_ref, land_ref, send_sems, recv_sems):
        x, y, cc = _me()
        copies = []
        for k in range(4):
            px, py = k // 2, k % 2
            copies.append(pltpu.make_async_remote_copy(
                src_ref=g_ref.at[4 * px + 2 * py + (1 - cc)], dst_ref=land_ref.at[k],
                send_sem=send_sems.at[k], recv_sem=recv_sems.at[k], device_id=(x, y, 1 - cc), device_id_type=MESH))
        for cp in copies:
            cp.start()
        for cp in copies:
            cp.wait_recv()
        for cp in copies:
            cp.wait_send()

    return pl.pallas_call(
        body, name=name,
        out_shape=jax.ShapeDtypeStruct((4, r, c), g.dtype),
        in_specs=[pl.BlockSpec(memory_space=pl.ANY)],
        out_specs=pl.BlockSpec(memory_space=pl.ANY),
        scratch_shapes=[pltpu.SemaphoreType.DMA((4,)), pltpu.SemaphoreType.DMA((4,))],
    )(g)


def rs_chip_exchange(p, name):
    _, r, c = p.shape

    def body(p_ref, land_ref, send_sems, recv_sems, local_sem):
        x, y, cc = _me()
        mine = pltpu.make_async_copy(p_ref.at[2 * x + y], land_ref.at[2 * x + y], local_sem)
        mine.start()
        chips = [(1 - x, y), (x, 1 - y), (1 - x, 1 - y)]
        sends = [pltpu.make_async_remote_copy(
            src_ref=p_ref.at[2 * px + py], dst_ref=land_ref.at[2 * x + y],
            send_sem=send_sems.at[j], recv_sem=recv_sems.at[j], device_id=(px, py, cc), device_id_type=MESH)
            for j, (px, py) in enumerate(chips)]
        for cp in sends:
            cp.start()
        for j, (px, py) in enumerate(chips):
            pltpu.make_async_remote_copy(
                src_ref=p_ref.at[2 * x + y], dst_ref=land_ref.at[2 * px + py],
                send_sem=send_sems.at[j], recv_sem=recv_sems.at[j], device_id=(px, py, cc), device_id_type=MESH).wait_recv()
        for cp in sends:
            cp.wait_send()
        mine.wait()

    return pl.pallas_call(
        body, name=name,
        out_shape=jax.ShapeDtypeStruct((4, r, c), p.dtype),
        in_specs=[pl.BlockSpec(memory_space=pl.ANY)],
        out_specs=pl.BlockSpec(memory_space=pl.ANY),
        scratch_shapes=[pltpu.SemaphoreType.DMA((3,)), pltpu.SemaphoreType.DMA((3,)), pltpu.SemaphoreType.DMA(())],
    )(p)


def rs_add_sibling(g, land, name):
    _, r, c = g.shape
    tr = _tile(r, 512, 8)
    core = lax.axis_index("c").astype(jnp.int32).reshape(1)

    def body(core_ref, g_ref, l_ref, o_ref):
        o_ref[...] = (g_ref[...].astype(F32) + l_ref[...].astype(F32)).astype(o_ref.dtype)

    return pl.pallas_call(
        body, name=name,
        out_shape=jax.ShapeDtypeStruct((4, r, c), g.dtype),
        grid_spec=pltpu.PrefetchScalarGridSpec(
            num_scalar_prefetch=1, grid=(4, r // tr),
            in_specs=[pl.BlockSpec((1, tr, c), lambda k, i, core_ref: (2 * k + core_ref[0], i, 0)),
                      pl.BlockSpec((1, tr, c), lambda k, i, core_ref: (k, i, 0))],
            out_specs=pl.BlockSpec((1, tr, c), lambda k, i, core_ref: (k, i, 0))),
        compiler_params=_params("arbitrary", "arbitrary"),
    )(core, g, land)


def sum_leading(x, name):
    n, r, c = x.shape
    tr = _tile(r, 512, 16)

    def body(x_ref, o_ref):
        acc = x_ref[0].astype(F32)
        for k in range(1, n):
            acc = acc + x_ref[k].astype(F32)
        o_ref[...] = acc

    return pl.pallas_call(
        body, name=name,
        out_shape=jax.ShapeDtypeStruct((r, c), F32),
        grid=(r // tr,),
        in_specs=[pl.BlockSpec((n, tr, c), lambda i: (0, i, 0))],
        out_specs=pl.BlockSpec((tr, c), lambda i: (i, 0)),
        compiler_params=_params("arbitrary"),
    )(x)


MM_VMEM_BUDGET = 36 * 1024 * 1024
GRID_STEP_AS_BYTES = 1 << 20


def _mm_tiles(m, n, a_row_bytes, b_col_bytes, out_bytes):
    tms = [c for c in (2048, 1024, 512, 256, 128, 64, 32, 16, 8) if m % c == 0] or [m]
    tns = [c for c in range(LANES, min(n, 2048) + 1, LANES) if n % c == 0] or [n]
    best = None
    for tm in tms:
        for tn in tns:
            vmem = 2 * (tm * a_row_bytes + tn * b_col_bytes) + 2 * tm * tn * out_bytes + tm * tn * 4
            if vmem > MM_VMEM_BUDGET:
                continue
            steps = (m // tm) * (n // tn)
            cost = steps * GRID_STEP_AS_BYTES + (m // tm) * n * b_col_bytes + m * a_row_bytes
            if best is None or cost < best[0]:
                best = (cost, tm, tn)
    assert best is not None, (m, n, a_row_bytes, b_col_bytes)
    return best[1], best[2]


def mm(pairs, *, trans_b, out_dtype, name, out_slab=False, bias=None):
    a0 = pairs[0][0]
    m = a0.shape[1] if a0.ndim == 3 else a0.shape[0]
    n = pairs[0][1].shape[0] if trans_b else pairs[0][1].shape[1]
    a_row_bytes = sum((b.shape[1] if trans_b else b.shape[0]) * a.dtype.itemsize for a, b in pairs)
    b_col_bytes = sum((b.shape[1] if trans_b else b.shape[0]) * b.dtype.itemsize for _, b in pairs)
    tm, tn = _mm_tiles(m, n, a_row_bytes, b_col_bytes, jnp.dtype(out_dtype).itemsize)
    slabs = [a.ndim == 3 for a, _ in pairs]
    n_pairs = len(pairs)

    def body(*refs):
        o_ref = refs[-1]
        acc = bias_ref = None
        if bias is not None:
            bias_ref = refs[2 * n_pairs]
        for i in range(n_pairs):
            a_ref, b_ref = refs[2 * i], refs[2 * i + 1]
            if slabs[i]:
                a = jnp.concatenate([a_ref[s].astype(BF16) for s in range(a_ref.shape[0])], axis=1)
            else:
                a = a_ref[...].astype(BF16)
            b = b_ref[...].astype(BF16)
            part = _nt(a, b) if trans_b else _nn(a, b)
            acc = part if acc is None else acc + part
        if bias_ref is not None:
            acc = acc + bias_ref[...]
        if out_slab:
            for s in range(tn // LANES):
                o_ref[s] = acc[:, s * LANES:(s + 1) * LANES].astype(out_dtype)
        else:
            o_ref[...] = acc.astype(out_dtype)

    in_specs, args = [], []
    for (a, b), slab in zip(pairs, slabs):
        if slab:
            in_specs.append(pl.BlockSpec((a.shape[0], tm, LANES), lambda i, j: (0, i, 0)))
        else:
            in_specs.append(pl.BlockSpec((tm, a.shape[1]), lambda i, j: (i, 0)))
        if trans_b:
            in_specs.append(pl.BlockSpec((tn, b.shape[1]), lambda i, j: (j, 0)))
        else:
            in_specs.append(pl.BlockSpec((b.shape[0], tn), lambda i, j: (0, j)))
        args += [a, b]
    if bias is not None:
        in_specs.append(pl.BlockSpec((1, tn), lambda i, j: (0, j)))
        args.append(bias)
    if out_slab:
        out_shape = jax.ShapeDtypeStruct((n // LANES, m, LANES), out_dtype)
        out_spec = pl.BlockSpec((tn // LANES, tm, LANES), lambda i, j: (j, i, 0))
    else:
        out_shape = jax.ShapeDtypeStruct((m, n), out_dtype)
        out_spec = pl.BlockSpec((tm, tn), lambda i, j: (i, j))
    return pl.pallas_call(
        body, name=name, out_shape=out_shape, grid=(m // tm, n // tn),
        in_specs=in_specs, out_specs=out_spec,
        compiler_params=_params("arbitrary", "arbitrary"),
    )(*args)


def mm_tn(a, b, *, name, tk_cap=1536, tn_cap=1024, tm_cap=512):
    slab = a.ndim == 3
    m = a.shape[1] if slab else a.shape[0]
    k = a.shape[0] * LANES if slab else a.shape[1]
    n = b.shape[1]
    tk = _tile(k, tk_cap)
    tn = _tile(n, tn_cap)
    tm = _tile(m, tm_cap, 8)

    def body(a_ref, b_ref, o_ref):
        @pl.when(pl.program_id(2) == 0)
        def _():
            o_ref[...] = jnp.zeros_like(o_ref)

        bb = b_ref[...].astype(BF16)
        if slab:
            for s in range(tk // LANES):
                o_ref[s * LANES:(s + 1) * LANES, :] += _tn(a_ref[s].astype(BF16), bb)
        else:
            o_ref[...] += _tn(a_ref[...].astype(BF16), bb)

    if slab:
        a_spec = pl.BlockSpec((tk // LANES, tm, LANES), lambda i, j, t: (i, t, 0))
    else:
        a_spec = pl.BlockSpec((tm, tk), lambda i, j, t: (t, i))
    return pl.pallas_call(
        body, name=name, out_shape=jax.ShapeDtypeStruct((k, n), F32), grid=(k // tk, n // tn, m // tm),
        in_specs=[a_spec, pl.BlockSpec((tm, tn), lambda i, j, t: (t, j))],
        out_specs=pl.BlockSpec((tk, tn), lambda i, j, t: (i, j)),
        compiler_params=_params("arbitrary", "arbitrary", "arbitrary"),
    )(a, b)


def _row_spec(d, k):
    return pl.BlockSpec((1, 1, d), lambda b, i: (6 * b + k, 0, 0))


def modulate(x, mod, k_shift, k_scale, bl, name):
    t, d = x.shape
    s = t // bl
    tm = _tile(s, 512, 8)
    nt = s // tm

    def body(x_ref, sh_ref, sc_ref, o_ref):
        o_ref[...] = (x_ref[...] * (1.0 + sc_ref[0]) + sh_ref[0]).astype(BF16)

    return pl.pallas_call(
        body, name=name, out_shape=jax.ShapeDtypeStruct((t, d), BF16), grid=(bl, nt),
        in_specs=[pl.BlockSpec((tm, d), lambda b, i: (b * nt + i, 0)), _row_spec(d, k_shift), _row_spec(d, k_scale)],
        out_specs=pl.BlockSpec((tm, d), lambda b, i: (b * nt + i, 0)),
        compiler_params=_params("arbitrary", "arbitrary"),
    )(x, mod, mod)


def _layer_norm_stats(r):
    mu = jnp.mean(r, axis=-1, keepdims=True)
    rc = r - mu
    var = jnp.mean(rc * rc, axis=-1, keepdims=True)
    rstd = lax.rsqrt(var + NORM_EPS)
    return rc * rstd, rstd


def residual_layer_norm(x, y, mod, k_gate, g, b, bl, name, next_mod=None):
    t, d = x.shape
    s = t // bl
    tm = _tile(s, 256, 8)
    nt = s // tm
    has_next = next_mod is not None

    def body(*refs):
        x_ref, y_ref, gt_ref, g_ref, b_ref = refs[:5]
        rest = refs[5:]
        if has_next:
            sh_ref, sc_ref, o_ref, r_ref, u_ref = rest
        else:
            o_ref, r_ref = rest
        r = ALPHA * x_ref[...] + (1.0 + gt_ref[0]) * y_ref[...]
        xhat, _ = _layer_norm_stats(r)
        out = xhat * g_ref[...] + b_ref[...]
        o_ref[...] = out
        r_ref[...] = r
        if has_next:
            u_ref[...] = (out * (1.0 + sc_ref[0]) + sh_ref[0]).astype(BF16)

    tok = pl.BlockSpec((tm, d), lambda bb, i: (bb * nt + i, 0))
    vec = pl.BlockSpec((1, d), lambda bb, i: (0, 0))
    in_specs = [tok, tok, _row_spec(d, k_gate), vec, vec]
    args = [x, y, mod, g, b]
    out_shape = [jax.ShapeDtypeStruct((t, d), F32), jax.ShapeDtypeStruct((t, d), F32)]
    out_specs = [tok, tok]
    if has_next:
        in_specs += [_row_spec(d, next_mod[0]), _row_spec(d, next_mod[1])]
        args += [mod if len(next_mod) == 2 else next_mod[2]] * 2
        out_shape.append(jax.ShapeDtypeStruct((t, d), BF16))
        out_specs.append(tok)
    return pl.pallas_call(
        body, name=name, out_shape=out_shape, grid=(bl, nt), in_specs=in_specs, out_specs=out_specs,
        compiler_params=_params("arbitrary", "arbitrary"),
    )(*args)


def loss_head(xo, target, name):
    t, d = xo.shape
    tm = _tile(t, 512, 8)

    def body(x_ref, t_ref, l_ref, dx_ref):
        @pl.when(pl.program_id(0) == 0)
        def _():
            l_ref[...] = jnp.zeros_like(l_ref)

        e = x_ref[...] - t_ref[...]
        l_ref[...] += jnp.sum(e * e, axis=0, keepdims=True) * (0.5 / d)
        dx_ref[...] = e * (1.0 / d)

    tok = pl.BlockSpec((tm, d), lambda i: (i, 0))
    return pl.pallas_call(
        body, name=name,
        out_shape=[jax.ShapeDtypeStruct((1, d), F32), jax.ShapeDtypeStruct((t, d), F32)],
        grid=(t // tm,), in_specs=[tok, tok],
        out_specs=[pl.BlockSpec((1, d), lambda i: (0, 0)), tok],
        compiler_params=_params("arbitrary"),
    )(xo, target)


def sublayer_backward(d_a, bl, name, *, du=None, scale=None, x_in=None, ln=None):
    t, d = d_a.shape
    s = t // bl
    tm = _tile(s, 256, 8)
    nt = s // tm
    has_mod = du is not None
    has_ln = ln is not None
    assert has_mod or has_ln
    assert has_ln or x_in is not None

    def body(*refs):
        refs = list(refs)
        da_ref = refs.pop(0)
        if has_mod:
            du_ref, sc_ref = refs.pop(0), refs.pop(0)
        if has_ln:
            r_ref, y_ref, g_ref, b_ref, gt_ref = (refs.pop(0) for _ in range(5))
        elif has_mod:
            xin_ref = refs.pop(0)
        dx_ref = refs.pop(0)
        if has_ln:
            dy_ref, dg_ref, db_ref, dgt_ref = (refs.pop(0) for _ in range(4))
        if has_mod:
            dsc_ref, dsh_ref = refs.pop(0), refs.pop(0)
        first_tile = pl.program_id(1) == 0
        first_step = jnp.logical_and(pl.program_id(0) == 0, first_tile)

        dout = da_ref[...]
        if has_ln:
            xhat, rstd = _layer_norm_stats(r_ref[...])
        if has_mod:
            duv = du_ref[...]
            dout = dout + duv * (1.0 + sc_ref[0])
            xin = xhat * g_ref[...] + b_ref[...] if has_ln else xin_ref[...]

            @pl.when(first_tile)
            def _():
                dsc_ref[...] = jnp.zeros_like(dsc_ref)
                dsh_ref[...] = jnp.zeros_like(dsh_ref)

            dsc_ref[0] += jnp.sum(duv * xin, axis=0, keepdims=True)
            dsh_ref[0] += jnp.sum(duv, axis=0, keepdims=True)
        if not has_ln:
            dx_ref[...] = dout
            return

        @pl.when(first_step)
        def _():
            dg_ref[...] = jnp.zeros_like(dg_ref)
            db_ref[...] = jnp.zeros_like(db_ref)

        @pl.when(first_tile)
        def _():
            dgt_ref[...] = jnp.zeros_like(dgt_ref)

        dg_ref[...] += jnp.sum(dout * xhat, axis=0, keepdims=True)
        db_ref[...] += jnp.sum(dout, axis=0, keepdims=True)
        dxh = dout * g_ref[...]
        dr = rstd * (dxh - jnp.mean(dxh, axis=-1, keepdims=True) - xhat * jnp.mean(dxh * xhat, axis=-1, keepdims=True))
        dx_ref[...] = ALPHA * dr
        dy_ref[...] = ((1.0 + gt_ref[0]) * dr).astype(BF16)
        dgt_ref[0] += jnp.sum(dr * y_ref[...], axis=0, keepdims=True)

    tok = pl.BlockSpec((tm, d), lambda bb, i: (bb * nt + i, 0))
    vec = pl.BlockSpec((1, d), lambda bb, i: (0, 0))
    seq = pl.BlockSpec((1, 1, d), lambda bb, i: (bb, 0, 0))
    in_specs, args = [tok], [d_a]
    if has_mod:
        in_specs += [tok, _row_spec(d, scale[1])]
        args += [du, scale[0]]
    if has_ln:
        r, y, g, b, gate = ln
        in_specs += [tok, tok, vec, vec, _row_spec(d, gate[1])]
        args += [r, y, g, b, gate[0]]
    elif has_mod:
        in_specs.append(tok)
        args.append(x_in)
    names = ["dx"]
    out_shape, out_specs = [jax.ShapeDtypeStruct((t, d), F32)], [tok]
    if has_ln:
        names += ["dy", "dg", "db", "dgate"]
        out_shape += [jax.ShapeDtypeStruct((t, d), BF16), jax.ShapeDtypeStruct((1, d), F32),
                      jax.ShapeDtypeStruct((1, d), F32), jax.ShapeDtypeStruct((bl, 1, d), F32)]
        out_specs += [tok, vec, vec, seq]
    if has_mod:
        names += ["dscale", "dshift"]
        out_shape += [jax.ShapeDtypeStruct((bl, 1, d), F32)] * 2
        out_specs += [seq, seq]
    outs = pl.pallas_call(
        body, name=name, out_shape=out_shape, grid=(bl, nt), in_specs=in_specs, out_specs=out_specs,
        compiler_params=_params("arbitrary", "arbitrary"),
    )(*args)
    return dict(zip(names, outs))


def _silu(a):
    return a * jax.nn.sigmoid(a)


def silu_rows(a, name):
    def body(a_ref, o_ref):
        o_ref[...] = _silu(a_ref[...]).astype(BF16)

    return pl.pallas_call(body, name=name, out_shape=jax.ShapeDtypeStruct(a.shape, BF16))(a)


def swiglu_forward(a, b, name):
    t, f = a.shape
    tm, tf = _tile(t, 512, 8), _tile(f, 1536)

    def body(a_ref, b_ref, h_ref):
        h_ref[...] = (_silu(a_ref[...]) * b_ref[...]).astype(BF16)

    spec = pl.BlockSpec((tm, tf), lambda i, j: (i, j))
    return pl.pallas_call(
        body, name=name, out_shape=jax.ShapeDtypeStruct((t, f), BF16), grid=(t // tm, f // tf),
        in_specs=[spec, spec], out_specs=spec, compiler_params=_params("arbitrary", "arbitrary"),
    )(a, b)


def swiglu_backward(dh, a, b, name):
    t, f = a.shape
    tm, tf = _tile(t, 512, 8), _tile(f, 1536)

    def body(dh_ref, a_ref, b_ref, da_ref, db_ref):
        av = a_ref[...]
        sig = jax.nn.sigmoid(av)
        dhv = dh_ref[...]
        da_ref[...] = (dhv * b_ref[...] * (sig * (1.0 + av * (1.0 - sig)))).astype(BF16)
        db_ref[...] = (dhv * (av * sig)).astype(BF16)

    spec = pl.BlockSpec((tm, tf), lambda i, j: (i, j))
    return pl.pallas_call(
        body, name=name, out_shape=[jax.ShapeDtypeStruct((t, f), BF16)] * 2, grid=(t // tm, f // tf),
        in_specs=[spec, spec, spec], out_specs=[spec, spec], compiler_params=_params("arbitrary", "arbitrary"),
    )(dh, a, b)


def rope_tables(pos, inv_freq, sign, name):
    t = pos.shape[0]
    tm = _tile(t, 512, 8)

    def body(p_ref, f_ref, s_ref, c_out, s_out):
        ang = p_ref[...] * f_ref[...]
        c_out[...] = jnp.cos(ang)
        s_out[...] = jnp.sin(ang) * s_ref[...]

    vec = pl.BlockSpec((1, LANES), lambda i: (0, 0))
    tab = pl.BlockSpec((tm, LANES), lambda i: (i, 0))
    return pl.pallas_call(
        body, name=name, out_shape=[jax.ShapeDtypeStruct((t, LANES), F32)] * 2, grid=(t // tm,),
        in_specs=[pl.BlockSpec((tm, 1), lambda i: (i, 0)), vec, vec], out_specs=[tab, tab],
        compiler_params=_params("arbitrary"),
    )(pos, inv_freq, sign)


def _rot_half(v):
    lane = lax.broadcasted_iota(jnp.int32, v.shape, v.ndim - 1)
    up = pltpu.roll(v, LANES - MLA_ROPE // 2, v.ndim - 1)
    down = pltpu.roll(v, MLA_ROPE // 2, v.ndim - 1)
    return jnp.where(lane % MLA_ROPE < MLA_ROPE // 2, up, down)


def _rope(v, cos, sin_signed):
    return v * cos + _rot_half(v) * sin_signed


def _rope_transposed(dv, cos, sin_signed):
    return dv * cos + _rot_half(dv * sin_signed)


def rope_slabs(v, cos, sin_signed, out_dtype, name, transposed=False):
    ns, t, _ = v.shape
    tm = _tile(t, 512, 8)
    fn = _rope_transposed if transposed else _rope

    def body(v_ref, c_ref, s_ref, o_ref):
        o_ref[0] = fn(v_ref[0].astype(F32), c_ref[...], s_ref[...]).astype(out_dtype)

    tab = pl.BlockSpec((tm, LANES), lambda j, i: (i, 0))
    spec = pl.BlockSpec((1, tm, LANES), lambda j, i: (j, i, 0))
    return pl.pallas_call(
        body, name=name, out_shape=jax.ShapeDtypeStruct(v.shape, out_dtype), grid=(ns, t // tm),
        in_specs=[spec, tab, tab], out_specs=spec, compiler_params=_params("arbitrary", "arbitrary"),
    )(v, cos, sin_signed)


def _rms(x):
    rinv = lax.rsqrt(jnp.mean(x * x, axis=-1, keepdims=True) + NORM_EPS)
    return x * rinv, rinv


def mla_latents_forward(h_in, g_q, g_kv, cos, sin_signed, name):
    t = h_in.shape[0]
    tm = _tile(t, 512, 8)

    def body(h_ref, gq_ref, gkv_ref, c_ref, s_ref, cq_ref, ckv_ref, kr_ref):
        cq_ref[...] = (_rms(h_ref[:, 0:MLA_QR])[0] * gq_ref[...]).astype(BF16)
        ckv_ref[...] = (_rms(h_ref[:, MLA_QR:MLA_QR + MLA_KVR])[0] * gkv_ref[...]).astype(BF16)
        kr_ref[...] = _rope(h_ref[:, MLA_QR + MLA_KVR:], c_ref[...], s_ref[...]).astype(BF16)

    def tok(w):
        return pl.BlockSpec((tm, w), lambda i: (i, 0))

    def vec(w):
        return pl.BlockSpec((1, w), lambda i: (0, 0))

    return pl.pallas_call(
        body, name=name,
        out_shape=[jax.ShapeDtypeStruct((t, MLA_QR), BF16), jax.ShapeDtypeStruct((t, MLA_KVR), BF16),
                   jax.ShapeDtypeStruct((t, LANES), BF16)],
        grid=(t // tm,),
        in_specs=[tok(h_in.shape[1]), vec(MLA_QR), vec(MLA_KVR), tok(LANES), tok(LANES)],
        out_specs=[tok(MLA_QR), tok(MLA_KVR), tok(LANES)],
        compiler_params=_params("arbitrary"),
    )(h_in, g_q, g_kv, cos, sin_signed)


def mla_latents_backward(h_in, dcq, dckv, dkr, g_q, g_kv, cos, sin_signed, name):
    t, w = h_in.shape
    tm = _tile(t, 512, 8)

    def body(h_ref, dcq_ref, dckv_ref, dkr_ref, gq_ref, gkv_ref, c_ref, s_ref, dh_ref, dgq_ref, dgkv_ref):
        @pl.when(pl.program_id(0) == 0)
        def _():
            dgq_ref[...] = jnp.zeros_like(dgq_ref)
            dgkv_ref[...] = jnp.zeros_like(dgkv_ref)

        def rms_bwd(x, dc, g_ref, dg_ref):
            xn, rinv = _rms(x)
            dg_ref[...] += jnp.sum(dc * xn, axis=0, keepdims=True)
            dxn = dc * g_ref[...]
            return rinv * (dxn - xn * jnp.mean(dxn * xn, axis=-1, keepdims=True))

        dq = rms_bwd(h_ref[:, 0:MLA_QR], dcq_ref[...], gq_ref, dgq_ref)
        dkv = rms_bwd(h_ref[:, MLA_QR:MLA_QR + MLA_KVR], dckv_ref[...], gkv_ref, dgkv_ref)
        dr = _rope_transposed(dkr_ref[...], c_ref[...], s_ref[...])
        dh_ref[...] = jnp.concatenate([dq, dkv, dr], axis=1).astype(BF16)

    def tok(ww):
        return pl.BlockSpec((tm, ww), lambda i: (i, 0))

    def vec(ww):
        return pl.BlockSpec((1, ww), lambda i: (0, 0))

    return pl.pallas_call(
        body, name=name,
        out_shape=[jax.ShapeDtypeStruct((t, w), BF16), jax.ShapeDtypeStruct((1, MLA_QR), F32),
                   jax.ShapeDtypeStruct((1, MLA_KVR), F32)],
        grid=(t // tm,),
        in_specs=[tok(w), tok(MLA_QR), tok(MLA_KVR), tok(LANES), vec(MLA_QR), vec(MLA_KVR), tok(LANES), tok(LANES)],
        out_specs=[tok(w), vec(MLA_QR), vec(MLA_KVR)],
        compiler_params=_params("arbitrary"),
    )(h_in, dcq, dckv, dkr, g_q, g_kv, cos, sin_signed)


def _tri(n, lower):
    r = lax.broadcasted_iota(jnp.int32, (n, n), 0)
    c = lax.broadcasted_iota(jnp.int32, (n, n), 1)
    return jnp.where(r >= c if lower else r <= c, 1.0, 0.0).astype(F32)


def _dot_exact(tri, v):
    hi = v.astype(BF16)
    mid = (v - hi.astype(F32)).astype(BF16)
    lo = (v - hi.astype(F32) - mid.astype(F32)).astype(BF16)
    t = tri.astype(BF16)
    return _nn(t, hi) + _nn(t, mid) + _nn(t, lo)


def fox_gate_forward(z, b_f, bl, name):
    t = z.shape[0]
    s = t // bl
    ch = LANES
    n_ch = s // ch

    def body(z_ref, b_ref, f_ref, fs_ref):
        tri = _tri(ch, True)
        carry = jnp.zeros((1, LANES), F32)
        for k in range(n_ch):
            x = z_ref[k * ch:(k + 1) * ch, :] + b_ref[...]
            logf = jnp.minimum(x, 0.0) - jnp.log(1.0 + jnp.exp(-jnp.abs(x)))
            cs = _dot_exact(tri, logf) + carry
            carry = cs[ch - 1:ch, :]
            f_ref[k * ch:(k + 1) * ch, :] = cs
            for h in range(FOX_HEADS):
                fs_ref[h, k * ch:(k + 1) * ch, :] = jnp.broadcast_to(cs[:, h:h + 1], (ch, LANES))

    return pl.pallas_call(
        body, name=name,
        out_shape=[jax.ShapeDtypeStruct((t, LANES), F32), jax.ShapeDtypeStruct((FOX_HEADS, t, LANES), F32)],
        grid=(bl,),
        in_specs=[pl.BlockSpec((s, LANES), lambda b: (b, 0)), pl.BlockSpec((1, LANES), lambda b: (0, 0))],
        out_specs=[pl.BlockSpec((s, LANES), lambda b: (b, 0)),
                   pl.BlockSpec((FOX_HEADS, s, LANES), lambda b: (0, b, 0))],
        compiler_params=_params("arbitrary"),
    )(z, b_f)


def fox_gate_backward(z, b_f, df, bl, name):
    t = z.shape[0]
    s = t // bl
    ch = LANES
    n_ch = s // ch

    def body(z_ref, b_ref, df_ref, dz_ref, db_ref):
        @pl.when(pl.program_id(0) == 0)
        def _():
            db_ref[...] = jnp.zeros_like(db_ref)

        tri = _tri(ch, False)
        carry = jnp.zeros((1, LANES), F32)
        for k in reversed(range(n_ch)):
            cs = _dot_exact(tri, df_ref[k * ch:(k + 1) * ch, :]) + carry
            carry = cs[0:1, :]
            x = z_ref[k * ch:(k + 1) * ch, :] + b_ref[...]
            dz = cs * (1.0 - jax.nn.sigmoid(x))
            dz_ref[k * ch:(k + 1) * ch, :] = dz
            db_ref[...] += jnp.sum(dz, axis=0, keepdims=True)

    tok = pl.BlockSpec((s, LANES), lambda b: (b, 0))
    vec = pl.BlockSpec((1, LANES), lambda b: (0, 0))
    return pl.pallas_call(
        body, name=name,
        out_shape=[jax.ShapeDtypeStruct((t, LANES), F32), jax.ShapeDtypeStruct((1, LANES), F32)],
        grid=(bl,), in_specs=[tok, vec, tok], out_specs=[tok, vec],
        compiler_params=_params("arbitrary"),
    )(z, b_f, df)


NEG_INF = float("-inf")


def _attn_tiles(s):
    return _tile(s, 512, 8)


def attention_forward(kind, ops, bl, scale, name):
    fox = kind == "fox"
    if fox:
        qkv, fq, fk = ops
        t = qkv.shape[1]
        n_pair = FOX_HEADS // 2
    else:
        qn, qr, kn, kr, v = ops
        t = qn.shape[1]
        n_pair = MLA_HEADS // 2
    s = t // bl
    tq = _attn_tiles(s)
    nq = s // tq
    half = LANES // 2

    def body(*refs):
        if fox:
            q_ref, k_ref, v_ref, fq_ref, fk_ref, o_ref, lse_ref, o32_ref = refs
        else:
            qn_ref, qr_ref, kn_ref, kr_ref, v_ref, o_ref, lse_ref = refs
        i = pl.program_id(2)
        row = lax.broadcasted_iota(jnp.int32, (tq, tq), 0)
        col = lax.broadcasted_iota(jnp.int32, (tq, tq), 1)
        heads = []
        for e in range(2):
            sl = slice(e * half, (e + 1) * half)
            if fox:
                heads.append((sl, q_ref[0, :, sl], None))
            else:
                heads.append((sl, qn_ref[e], qr_ref[0, :, sl]))
        dv = half if fox else LANES

        def wide(stat):
            return jnp.concatenate([stat] * (tq // LANES), axis=1)

        def step(j, carry, masked):
            rows = pl.ds(pl.multiple_of(j * tq, tq), tq)
            new = []
            for e, (sl, qa, qb) in enumerate(heads):
                m, l, acc = carry[e]
                if fox:
                    sc = _nt(qa, k_ref[0, rows, sl]) * scale + wide(fq_ref[e]) - fk_ref[0, j, e:e + 1, :]
                    vv = v_ref[0, rows, sl]
                else:
                    sc = (_nt(qa, kn_ref[e, rows, :]) + _nt(qb, kr_ref[rows, 0:half])) * scale
                    vv = v_ref[e, rows, :]
                if masked:
                    sc = jnp.where(row >= col, sc, NEG_INF)
                m_new = jnp.maximum(m, jnp.max(sc, axis=1, keepdims=True))
                p = jnp.exp(sc - m_new)
                a = jnp.exp(m - m_new)
                l = a * l + jnp.sum(p, axis=1, keepdims=True)
                p_hi = p.astype(BF16)
                acc = a * acc + _nn(p_hi, vv)
                if fox:
                    acc = acc + _nn((p - p_hi.astype(F32)).astype(BF16), vv)
                new.append((m_new, l, acc))
            return tuple(new)

        init = (jnp.full((tq, 1), NEG_INF, F32), jnp.zeros((tq, 1), F32), jnp.zeros((tq, dv), F32))
        carry = step(i, (init, init), True)
        carry = lax.fori_loop(0, i, lambda j, c: step(j, c, False), carry)
        outs = [acc / l for _, l, acc in carry]
        for e, (m, l, _) in enumerate(carry):
            lse_ref[e] = jnp.broadcast_to(m + jnp.log(l), (tq, LANES))
        if fox:
            o32 = jnp.concatenate(outs, axis=1)
            o32_ref[0] = o32
            o_ref[0] = o32.astype(BF16)
        else:
            o_ref[0] = outs[0].astype(BF16)
            o_ref[1] = outs[1].astype(BF16)

    def q_idx(b, g, i):
        return (g, b * nq + i, 0)

    if fox:
        nk = fk.shape[1]
        in_specs = [pl.BlockSpec((1, tq, LANES), q_idx),
                    pl.BlockSpec((1, s, LANES), lambda b, g, i: (n_pair + g, b, 0)),
                    pl.BlockSpec((1, s, LANES), lambda b, g, i: (2 * n_pair + g, b, 0)),
                    pl.BlockSpec((2, tq, LANES), q_idx),
                    pl.BlockSpec((1, nk, 8, tq), lambda b, g, i: (b * n_pair + g, 0, 0, 0))]
        args = [qkv, qkv, qkv, fq, fk]
        o_spec = pl.BlockSpec((1, tq, LANES), q_idx)
    else:
        in_specs = [pl.BlockSpec((2, tq, LANES), q_idx),
                    pl.BlockSpec((1, tq, LANES), q_idx),
                    pl.BlockSpec((2, s, LANES), lambda b, g, i: (g, b, 0)),
                    pl.BlockSpec((s, LANES), lambda b, g, i: (b, 0)),
                    pl.BlockSpec((2, s, LANES), lambda b, g, i: (g, b, 0))]
        args = [qn, qr, kn, kr, v]
        o_spec = pl.BlockSpec((2, tq, LANES), q_idx)
    out_shape = [jax.ShapeDtypeStruct((8, t, LANES), BF16), jax.ShapeDtypeStruct((2 * n_pair, t, LANES), F32)]
    out_specs = [o_spec, pl.BlockSpec((2, tq, LANES), q_idx)]
    if fox:
        out_shape.append(jax.ShapeDtypeStruct((8, t, LANES), F32))
        out_specs.append(o_spec)
    outs = pl.pallas_call(
        body, name=name, out_shape=out_shape, grid=(bl, n_pair, nq), in_specs=in_specs, out_specs=out_specs,
        compiler_params=_params("arbitrary", "arbitrary", "arbitrary"),
    )(*args)
    return (outs[0], outs[1], outs[2] if fox else outs[0])


def attention_backward(kind, ops, o, do, lse, bl, scale, name):
    fox = kind == "fox"
    if fox:
        qkv, fq, fk = ops
        t = qkv.shape[1]
        n_pair = FOX_HEADS // 2
    else:
        qn, qr, kn, kr, v = ops
        t = qn.shape[1]
        n_pair = MLA_HEADS // 2
    s = t // bl
    tq = _attn_tiles(s)
    nq = s // tq
    half = LANES // 2

    def body(*refs):
        if fox:
            (q_ref, k_ref, v_ref, fq_ref, fk_ref, o_ref, do_ref, lse_ref,
             dq_ref, dk_ref, dv_ref, dfk_ref, delta_scr, qt_scr, dot_scr) = refs
        else:
            (qn_ref, qr_ref, kn_ref, kr_ref, v_ref, o_ref, do_ref, lse_ref,
             dqn_ref, dqr_ref, dkn_ref, dv_ref, dkr_ref, delta_scr, qt_scr, qrt_scr, dot_scr) = refs
        g, j = pl.program_id(1), pl.program_id(2)
        row = lax.broadcasted_iota(jnp.int32, (tq, tq), 0)
        col = lax.broadcasted_iota(jnp.int32, (tq, tq), 1)
        krows = pl.ds(pl.multiple_of(j * tq, tq), tq)

        def transposed(v):
            return v.astype(F32).T.astype(BF16)

        def wide(stat):
            return jnp.concatenate([stat] * (tq // LANES), axis=1)

        @pl.when(j == 0)
        def _():
            if fox:
                dq_ref[...] = jnp.zeros_like(dq_ref)
            else:
                dqn_ref[...] = jnp.zeros_like(dqn_ref)
                dqr_ref[...] = jnp.zeros_like(dqr_ref)
            for ii in range(nq):
                rws = slice(ii * tq, (ii + 1) * tq)
                deltas = []
                if fox:
                    prod = do_ref[0, rws, :].astype(F32) * o_ref[0, rws, :].astype(F32)
                    for e in range(2):
                        deltas.append(jnp.sum(prod[:, e * half:(e + 1) * half], axis=1, keepdims=True))
                    qt_scr[ii] = transposed(q_ref[0, rws, :])
                    dot_scr[ii] = transposed(do_ref[0, rws, :])
                else:
                    for e in range(2):
                        prod = do_ref[e, rws, :].astype(F32) * o_ref[e, rws, :].astype(F32)
                        deltas.append(jnp.sum(prod, axis=1, keepdims=True))
                        qt_scr[e, ii] = transposed(qn_ref[e, rws, :])
                        dot_scr[e, ii] = transposed(do_ref[e, rws, :])
                    qrt_scr[ii] = transposed(qr_ref[0, rws, :])
                for e in range(2):
                    delta_scr[e, rws, :] = jnp.broadcast_to(deltas[e], (tq, LANES))

        if fox:
            dfk_ref[...] = jnp.zeros_like(dfk_ref)
        else:
            @pl.when(jnp.logical_and(g == 0, j == 0))
            def _():
                dkr_ref[...] = jnp.zeros_like(dkr_ref)

        heads = []
        for e in range(2):
            sl = slice(e * half, (e + 1) * half)
            if fox:
                heads.append((sl, k_ref[0, :, sl], v_ref[0, :, sl], fk_ref[0, 0, e:e + 1, :]))
            else:
                heads.append((sl, kn_ref[e], v_ref[e], kr_ref[krows, 0:half]))
        dk_w = dv_w = half if fox else LANES

        def step(i, carry, masked):
            rows = pl.ds(pl.multiple_of(i * tq, tq), tq)
            new = []
            for e, (sl, k_e, v_e, x_e) in enumerate(heads):
                dk_acc, dv_acc, last = carry[e]
                if fox:
                    do_i = do_ref[0, rows, sl]
                    sc = _nt(q_ref[0, rows, sl], k_e) * scale + wide(fq_ref[e, rows, :]) - x_e
                else:
                    do_i = do_ref[e, rows, :]
                    sc = (_nt(qn_ref[e, rows, :], k_e) + _nt(qr_ref[0, rows, sl], x_e)) * scale
                if masked:
                    sc = jnp.where(row >= col, sc, NEG_INF)
                p = jnp.exp(sc - wide(lse_ref[e, rows, :]))
                dp = _nt(do_i, v_e)
                ds = p * (dp - wide(delta_scr[e, rows, :]))
                dsb = (ds * scale).astype(BF16)
                if fox:
                    fsl = slice(e * half, (e + 1) * half)
                    dv_acc = dv_acc + _nn(dot_scr[i, fsl, :], p.astype(BF16))
                    dk_acc = dk_acc + _nn(qt_scr[i, fsl, :], dsb)
                    dq_ref[0, rows, sl] += _nn(dsb, k_e)
                    last = last - jnp.sum(ds, axis=0, keepdims=True)
                else:
                    dv_acc = dv_acc + _nn(dot_scr[e, i], p.astype(BF16))
                    dk_acc = dk_acc + _nn(qt_scr[e, i], dsb)
                    dqn_ref[e, rows, :] += _nn(dsb, k_e)
                    dqr_ref[0, rows, sl] += _nn(dsb, x_e)
                    last = last + _nn(qrt_scr[i, e * half:(e + 1) * half, :], dsb)
                new.append((dk_acc, dv_acc, last))
            return tuple(new)

        last0 = jnp.zeros((1, tq), F32) if fox else jnp.zeros((half, tq), F32)
        init = (jnp.zeros((dk_w, tq), F32), jnp.zeros((dv_w, tq), F32), last0)
        carry = step(j, (init, init), True)
        carry = lax.fori_loop(j + 1, nq, lambda i, c: step(i, c, False), carry)
        if fox:
            for e in range(2):
                dfk_ref[0, 0, e:e + 1, :] = carry[e][2]
            dk_ref[0] = jnp.concatenate([carry[0][0], carry[1][0]], axis=0).T.astype(BF16)
            dv_ref[0] = jnp.concatenate([carry[0][1], carry[1][1]], axis=0).T.astype(BF16)
        else:
            for e in range(2):
                dkn_ref[e] = carry[e][0].T.astype(BF16)
                dv_ref[e] = carry[e][1].T.astype(BF16)
            dkr_t = carry[0][2] + carry[1][2]
            dkr_ref[krows, :] += jnp.concatenate([dkr_t, jnp.zeros_like(dkr_t)], axis=0).T

    def whole(b, g, j):
        return (g, b, 0)

    def kblk(b, g, j):
        return (g, b * nq + j, 0)

    if fox:
        in_specs = [pl.BlockSpec((1, s, LANES), whole),
                    pl.BlockSpec((1, tq, LANES), lambda b, g, j: (n_pair + g, b * nq + j, 0)),
                    pl.BlockSpec((1, tq, LANES), lambda b, g, j: (2 * n_pair + g, b * nq + j, 0)),
                    pl.BlockSpec((2, s, LANES), whole),
                    pl.BlockSpec((1, 1, 8, tq), lambda b, g, j: (b * n_pair + g, j, 0, 0)),
                    pl.BlockSpec((1, s, LANES), whole), pl.BlockSpec((1, s, LANES), whole),
                    pl.BlockSpec((2, s, LANES), whole)]
        args = [qkv, qkv, qkv, fq, fk, o, do, lse]
        out_shape = [jax.ShapeDtypeStruct((8, t, LANES), F32), jax.ShapeDtypeStruct((8, t, LANES), BF16),
                     jax.ShapeDtypeStruct((8, t, LANES), BF16), jax.ShapeDtypeStruct(fk.shape, F32)]
        out_specs = [pl.BlockSpec((1, s, LANES), whole), pl.BlockSpec((1, tq, LANES), kblk),
                     pl.BlockSpec((1, tq, LANES), kblk),
                     pl.BlockSpec((1, 1, 8, tq), lambda b, g, j: (b * n_pair + g, j, 0, 0))]
    else:
        pair = pl.BlockSpec((2, s, LANES), whole)
        pair_k = pl.BlockSpec((2, tq, LANES), kblk)
        in_specs = [pair, pl.BlockSpec((1, s, LANES), whole), pair_k,
                    pl.BlockSpec((s, LANES), lambda b, g, j: (b, 0)), pair_k,
                    pair, pair, pair]
        args = [qn, qr, kn, kr, v, o, do, lse]
        out_shape = [jax.ShapeDtypeStruct((8, t, LANES), F32), jax.ShapeDtypeStruct((4, t, LANES), F32),
                     jax.ShapeDtypeStruct((8, t, LANES), BF16), jax.ShapeDtypeStruct((8, t, LANES), BF16),
                     jax.ShapeDtypeStruct((t, LANES), F32)]
        out_specs = [pair, pl.BlockSpec((1, s, LANES), whole), pair_k, pair_k,
                     pl.BlockSpec((s, LANES), lambda b, g, j: (b, 0))]
    t_blocks = pltpu.VMEM((nq, LANES, tq), BF16)
    t_pairs = pltpu.VMEM((2, nq, LANES, tq), BF16)
    scratch = [pltpu.VMEM((2, s, LANES), F32)] + ([t_blocks, t_blocks] if fox else [t_pairs, t_blocks, t_pairs])
    return pl.pallas_call(
        body, name=name, out_shape=out_shape, grid=(bl, n_pair, nq), in_specs=in_specs, out_specs=out_specs,
        scratch_shapes=scratch, compiler_params=_params("arbitrary", "arbitrary", "arbitrary"),
    )(*args)


def adamw(w, g, m, v, name):
    shape = w.shape
    c = shape[-1]
    r = w.size // c
    tr = _tile(r, 512, 8)

    def body(w_ref, g_ref, m_ref, v_ref, d_ref, nm_ref, nv_ref):
        gv = g_ref[...]
        m2 = ADAM_B1 * m_ref[...] + (1.0 - ADAM_B1) * gv
        v2 = ADAM_B2 * v_ref[...] + (1.0 - ADAM_B2) * (gv * gv)
        m_hat = m2 / (1.0 - ADAM_B1 ** ADAM_STEP)
        v_hat = v2 / (1.0 - ADAM_B2 ** ADAM_STEP)
        d_ref[...] = -ADAM_LR * (m_hat / (jnp.sqrt(v_hat) + ADAM_EPS) + ADAM_WD * w_ref[...])
        nm_ref[...] = m2
        nv_ref[...] = v2

    spec = pl.BlockSpec((tr, c), lambda i: (i, 0))
    outs = pl.pallas_call(
        body, name=name, out_shape=[jax.ShapeDtypeStruct((r, c), F32)] * 3, grid=(r // tr,),
        in_specs=[spec] * 4, out_specs=[spec] * 3, compiler_params=_params("arbitrary"),
    )(*(a.reshape(r, c) for a in (w, g, m, v)))
    return tuple(a.reshape(shape) for a in outs)


PACK_COLS = 1024


def _pack_rows(a):
    return a.reshape(-1, PACK_COLS)


def kernel(x, c, positions, mla_w_in, mla_g_q, mla_w_uq, mla_g_kv, mla_w_uk, mla_w_uv, mla_w_o, fox_w_in, fox_b_f, fox_w_o, ada_w, ada_b, ffn_w_gate, ffn_w_up, ffn_w_down, ln_g, ln_b, loss_target, m_mla_w_in, m_mla_g_q, m_mla_w_uq, m_mla_g_kv, m_mla_w_uk, m_mla_w_uv, m_mla_w_o, m_fox_w_in, m_fox_b_f, m_fox_w_o, m_ada_w, m_ada_b, m_ffn_w_gate, m_ffn_w_up, m_ffn_w_down, m_ln_g, m_ln_b, v_mla_w_in, v_mla_g_q, v_mla_w_uq, v_mla_g_kv, v_mla_w_uk, v_mla_w_uv, v_mla_w_o, v_fox_w_in, v_fox_b_f, v_fox_w_o, v_ada_w, v_ada_b, v_ffn_w_gate, v_ffn_w_up, v_ffn_w_down, v_ln_g, v_ln_b):
    bl, s, d = x.shape
    t = bl * s
    ff = ffn_w_gate.shape[-1] * N_DEV
    dev = 4 * lax.axis_index("x") + 2 * lax.axis_index("y") + lax.axis_index("c")
    ada_cols = ada_w.shape[-1]
    fox_in = fox_w_in.shape[-1] * N_DEV
    mla_in = mla_w_in.shape[-1]
    mla_in_pad = mla_in + (-mla_in) % LANES

    def t_last(a):
        return jnp.swapaxes(a, -1, -2)

    local = [
        ("mla_w_in", mla_w_in[0]),
        ("mla_w_uq", t_last(mla_w_uq[0])),
        ("mla_w_uk", t_last(mla_w_uk[0])),
        ("mla_w_uv", t_last(mla_w_uv[0])),
        ("mla_w_o", mla_w_o[0]),
        ("fox_w_in", t_last(fox_w_in[0])),
        ("fox_w_o", fox_w_o[0]),
    ]
    for i in range(DEPTH):
        local += [(f"gate{i}", t_last(ffn_w_gate[i])), (f"up{i}", t_last(ffn_w_up[i])), (f"down{i}", ffn_w_down[i])]
    offsets, rows_of, slot_of = {}, {}, {}
    pack_rows = 0
    for nm, a in local:
        rows_of[nm] = a.size // PACK_COLS
        slot_of[nm] = rows_of[nm] + (-rows_of[nm]) % 16
        offsets[nm] = pack_rows
        pack_rows += slot_of[nm]

    def slot(nm, rows):
        pad = [(0, 0)] * rows.ndim
        pad[-2] = (0, slot_of[nm] - rows_of[nm])
        return jnp.pad(rows, pad)

    packed = jnp.concatenate([slot(nm, _pack_rows(a).astype(BF16)) for nm, a in local], axis=0)
    gathered = all_gather(packed, "gather_weights")

    def full(nm, cols):
        blk = gathered[:, offsets[nm]:offsets[nm] + rows_of[nm], :]
        return blk.reshape(-1, cols)

    w_in = jnp.pad(full("mla_w_in", mla_in), ((0, 0), (0, mla_in_pad - mla_in)))
    wt_uq = full("mla_w_uq", MLA_QR).reshape(MLA_HEADS, MLA_NOPE + MLA_ROPE, MLA_QR)
    wt_uq_n = wt_uq[:, :MLA_NOPE].reshape(MLA_HEADS * MLA_NOPE, MLA_QR)
    wt_uq_r = wt_uq[:, MLA_NOPE:].reshape(MLA_HEADS * MLA_ROPE, MLA_QR)
    wt_uk = full("mla_w_uk", MLA_KVR)
    wt_uv = full("mla_w_uv", MLA_KVR)
    w_mo = full("mla_w_o", d)
    wt_fox = full("fox_w_in", d)
    wt_qkv = wt_fox[:3 * d]
    wt_f = jnp.pad(wt_fox[3 * d:], ((0, LANES - FOX_HEADS), (0, 0)))
    w_fo = full("fox_w_o", d)
    wt_gate = [full(f"gate{i}", d) for i in range(DEPTH)]
    wt_up = [full(f"up{i}", d) for i in range(DEPTH)]
    w_down = [full(f"down{i}", d) for i in range(DEPTH)]

    small = jnp.concatenate([c.reshape(-1, LANES), ln_g.reshape(-1, LANES), ln_b.reshape(-1, LANES)], axis=0)
    small_rows = small.shape[0]
    small = jnp.pad(small, ((0, (-small_rows) % 8), (0, 0)))
    small_all = all_gather(small, "gather_small")
    c_rows = bl * d // LANES
    c_all = small_all[:, :c_rows].reshape(N_DEV * bl, d)
    n_ln = DEPTH * 2
    ln_g_all = small_all[:, c_rows:c_rows + n_ln, :].transpose(1, 0, 2).reshape(DEPTH, 2, 1, d)
    ln_b_all = small_all[:, c_rows + n_ln:c_rows + 2 * n_ln, :].transpose(1, 0, 2).reshape(DEPTH, 2, 1, d)

    c_act = silu_rows(c_all, "silu_c")
    ada_b_loc = lax.dynamic_slice_in_dim(ada_b, dev * ada_cols, ada_cols, axis=1)
    mod_cols = [mm([(c_act, ada_w[i])], trans_b=False, out_dtype=F32, name=f"ada_fwd{i}", bias=ada_b_loc[i][None, :])
                for i in range(DEPTH)]
    mod_all = all_gather(jnp.concatenate(mod_cols, axis=0), "gather_mod")
    mod_all = mod_all.reshape(N_DEV, DEPTH, N_DEV * bl, ada_cols).transpose(1, 2, 0, 3).reshape(DEPTH, N_DEV * bl, 6 * d)
    mod_mine = lax.dynamic_slice_in_dim(mod_all, dev * bl, bl, axis=1)
    mods = [mod_mine[i].reshape(bl * 6, 1, d) for i in range(DEPTH)]

    half_r = MLA_ROPE // 2
    inv_freq = ROPE_THETA ** (-jnp.arange(half_r, dtype=F32) / half_r)
    inv_freq = jnp.tile(inv_freq, LANES // half_r)[None, :]
    sign = jnp.tile(jnp.concatenate([-jnp.ones((half_r,), F32), jnp.ones((half_r,), F32)]), LANES // MLA_ROPE)[None, :]
    cos_t, sin_t = rope_tables(positions.astype(F32).reshape(t, 1), inv_freq, sign, "rope_tables")

    x2d = x.reshape(t, d)
    g_q, g_kv = mla_g_q.reshape(1, MLA_QR), mla_g_kv.reshape(1, MLA_KVR)
    b_f = jnp.pad(fox_b_f.reshape(1, FOX_HEADS), ((0, 0), (0, LANES - FOX_HEADS)))
    mla_scale = (MLA_NOPE + MLA_ROPE) ** -0.5
    fox_scale = FOX_HD ** -0.5
    tq = _attn_tiles(s)
    nk = s // tq

    saved = []
    u = modulate(x2d, mods[0], 0, 1, bl, "modulate0")
    xin = x2d
    for i in range(DEPTH):
        sv = {"u": u, "x_in": xin}
        if i % 2 == 0:
            h_in = mm([(u, w_in)], trans_b=False, out_dtype=F32, name=f"mla_in{i}")
            c_q, c_kv, k_r = mla_latents_forward(h_in, g_q, g_kv, cos_t, sin_t, f"mla_latents{i}")
            q_n = mm([(c_q, wt_uq_n)], trans_b=True, out_dtype=BF16, out_slab=True, name=f"mla_qn{i}")
            q_r_raw = mm([(c_q, wt_uq_r)], trans_b=True, out_dtype=F32, out_slab=True, name=f"mla_qr{i}")
            q_r = rope_slabs(q_r_raw, cos_t, sin_t, BF16, f"mla_qrope{i}")
            k_n = mm([(c_kv, wt_uk)], trans_b=True, out_dtype=BF16, out_slab=True, name=f"mla_kn{i}")
            v_m = mm([(c_kv, wt_uv)], trans_b=True, out_dtype=BF16, out_slab=True, name=f"mla_v{i}")
            ops = (q_n, q_r, k_n, k_r, v_m)
            o, lse, o_delta = attention_forward("mla", ops, bl, mla_scale, f"mla_attn{i}")
            y = mm([(o, w_mo)], trans_b=False, out_dtype=F32, name=f"mla_out{i}")
            sv.update(h_in=h_in, c_q=c_q, c_kv=c_kv, ops=ops, o=o, lse=lse, o_delta=o_delta)
        else:
            qkv = mm([(u, wt_qkv)], trans_b=True, out_dtype=BF16, out_slab=True, name=f"fox_qkv{i}")
            z = mm([(u, wt_f)], trans_b=True, out_dtype=F32, name=f"fox_z{i}")
            f_tok, f_q = fox_gate_forward(z, b_f, bl, f"fox_gate{i}")
            f_k = f_tok[:, :FOX_HEADS].reshape(bl, nk, tq, FOX_HEADS // 2, 2).transpose(0, 3, 1, 4, 2)
            f_k = jnp.pad(f_k.reshape(bl * FOX_HEADS // 2, nk, 2, tq), ((0, 0), (0, 0), (0, 6), (0, 0)))
            ops = (qkv, f_q, f_k)
            o, lse, o_delta = attention_forward("fox", ops, bl, fox_scale, f"fox_attn{i}")
            y = mm([(o, w_fo)], trans_b=False, out_dtype=F32, name=f"fox_out{i}")
            sv.update(z=z, ops=ops, o=o, lse=lse, o_delta=o_delta)
        x1, r1, u2 = residual_layer_norm(xin, y, mods[i], 2, ln_g_all[i, 0], ln_b_all[i, 0], bl, f"ln_mix{i}",
                                         next_mod=(3, 4))
        a = mm([(u2, wt_gate[i])], trans_b=True, out_dtype=F32, name=f"ffn_gate{i}")
        bb = mm([(u2, wt_up[i])], trans_b=True, out_dtype=F32, name=f"ffn_up{i}")
        h = swiglu_forward(a, bb, f"swiglu{i}")
        y2 = mm([(h, w_down[i])], trans_b=False, out_dtype=F32, name=f"ffn_down{i}")
        sv.update(y=y, r1=r1, u2=u2, a=a, bb=bb, h=h, y2=y2)
        if i + 1 < DEPTH:
            xin, r2, u = residual_layer_norm(x1, y2, mods[i], 5, ln_g_all[i, 1], ln_b_all[i, 1], bl, f"ln_ffn{i}",
                                             next_mod=(0, 1, mods[i + 1]))
        else:
            xin, r2 = residual_layer_norm(x1, y2, mods[i], 5, ln_g_all[i, 1], ln_b_all[i, 1], bl, f"ln_ffn{i}")
        sv.update(r2=r2)
        saved.append(sv)

    loss_cols, d_x = loss_head(xin, loss_target.reshape(t, d), "loss_head")

    grads_full = {}
    dmod = [[None] * 6 for _ in range(DEPTH)]
    dg_ln = [[None, None] for _ in range(DEPTH)]
    db_ln = [[None, None] for _ in range(DEPTH)]
    dg_q = dg_kv = db_f = None
    d_a, du = d_x, None
    for i in reversed(range(DEPTH)):
        sv = saved[i]
        ln2 = (sv["r2"], sv["y2"], ln_g_all[i, 1], ln_b_all[i, 1], (mods[i], 5))
        if du is None:
            bw = sublayer_backward(d_a, bl, f"bwd_ln_ffn{i}", ln=ln2)
        else:
            bw = sublayer_backward(d_a, bl, f"bwd_ln_ffn{i}", du=du, scale=(mods[i + 1], 1), ln=ln2)
            dmod[i + 1][0], dmod[i + 1][1] = bw["dshift"], bw["dscale"]
        dmod[i][5], dg_ln[i][1], db_ln[i][1] = bw["dgate"], bw["dg"], bw["db"]
        dy2 = bw["dy"]
        dh = mm([(dy2, w_down[i])], trans_b=True, out_dtype=F32, name=f"bwd_ffn_dh{i}")
        da, dbb = swiglu_backward(dh, sv["a"], sv["bb"], f"bwd_swiglu{i}")
        du2 = mm([(da, wt_gate[i]), (dbb, wt_up[i])], trans_b=False, out_dtype=F32, name=f"bwd_ffn_du{i}")
        grads_full[f"down{i}"] = mm_tn(sv["h"], dy2, name=f"bwd_w_down{i}")
        grads_full[f"gate{i}"] = mm_tn(da, sv["u2"], name=f"bwd_w_gate{i}")
        grads_full[f"up{i}"] = mm_tn(dbb, sv["u2"], name=f"bwd_w_up{i}")
        bw = sublayer_backward(bw["dx"], bl, f"bwd_ln_mix{i}", du=du2, scale=(mods[i], 4),
                               ln=(sv["r1"], sv["y"], ln_g_all[i, 0], ln_b_all[i, 0], (mods[i], 2)))
        dmod[i][3], dmod[i][4], dmod[i][2] = bw["dshift"], bw["dscale"], bw["dgate"]
        dg_ln[i][0], db_ln[i][0] = bw["dg"], bw["db"]
        d_a, dy = bw["dx"], bw["dy"]
        o, lse, ops = sv["o"], sv["lse"], sv["ops"]
        if i % 2 == 0:
            do = mm([(dy, w_mo)], trans_b=True, out_dtype=BF16, out_slab=True, name=f"bwd_mla_do{i}")
            grads_full["mla_w_o"] = mm_tn(o, dy, name=f"bwd_w_mla_o{i}")
            dqn, dqr, dkn, dvm, dkr = attention_backward("mla", ops, sv["o_delta"], do, lse, bl, mla_scale,
                                                         f"bwd_mla_attn{i}")
            dqr = rope_slabs(dqr, cos_t, sin_t, F32, f"bwd_mla_qrope{i}", transposed=True)
            dcq = mm([(dqn, wt_uq_n), (dqr, wt_uq_r)], trans_b=False, out_dtype=F32, name=f"bwd_mla_dcq{i}")
            dckv = mm([(dkn, wt_uk), (dvm, wt_uv)], trans_b=False, out_dtype=F32, name=f"bwd_mla_dckv{i}")
            d_uq_n = mm_tn(dqn, sv["c_q"], name=f"bwd_w_uq_n{i}").reshape(MLA_HEADS, MLA_NOPE, MLA_QR)
            d_uq_r = mm_tn(dqr, sv["c_q"], name=f"bwd_w_uq_r{i}").reshape(MLA_HEADS, MLA_ROPE, MLA_QR)
            grads_full["mla_w_uq"] = jnp.concatenate([d_uq_n, d_uq_r], axis=1)
            grads_full["mla_w_uk"] = mm_tn(dkn, sv["c_kv"], name=f"bwd_w_uk{i}")
            grads_full["mla_w_uv"] = mm_tn(dvm, sv["c_kv"], name=f"bwd_w_uv{i}")
            dh_in, dg_q, dg_kv = mla_latents_backward(sv["h_in"], dcq, dckv, dkr, g_q, g_kv, cos_t, sin_t,
                                                      f"bwd_mla_latents{i}")
            du = mm([(dh_in, w_in)], trans_b=True, out_dtype=F32, name=f"bwd_mla_du{i}")
            grads_full["mla_w_in"] = mm_tn(sv["u"], dh_in, name=f"bwd_w_mla_in{i}")[:, :mla_in]
        else:
            do = mm([(dy, w_fo)], trans_b=True, out_dtype=BF16, out_slab=True, name=f"bwd_fox_do{i}")
            grads_full["fox_w_o"] = mm_tn(o, dy, name=f"bwd_w_fox_o{i}")
            dq, dk, dvf, dfk = attention_backward("fox", ops, sv["o_delta"], do, lse, bl, fox_scale, f"bwd_fox_attn{i}")
            df = dfk[:, :, :2, :].reshape(bl, FOX_HEADS // 2, nk, 2, tq).transpose(0, 2, 4, 1, 3).reshape(t, FOX_HEADS)
            df = jnp.pad(df, ((0, 0), (0, LANES - FOX_HEADS)))
            dz, db_f = fox_gate_backward(sv["z"], b_f, df, bl, f"bwd_fox_gate{i}")
            du = mm([(dq, wt_fox[0:d]), (dk, wt_fox[d:2 * d]), (dvf, wt_fox[2 * d:3 * d]), (dz, wt_f)],
                    trans_b=False, out_dtype=F32, name=f"bwd_fox_du{i}")
            u_f = sv["u"]
            grads_full["fox_w_in"] = jnp.concatenate(
                [mm_tn(dq, u_f, name=f"bwd_w_fox_q{i}"), mm_tn(dk, u_f, name=f"bwd_w_fox_k{i}"),
                 mm_tn(dvf, u_f, name=f"bwd_w_fox_v{i}"), mm_tn(dz, u_f, name=f"bwd_w_fox_f{i}")[:FOX_HEADS]], axis=0)
    bw = sublayer_backward(d_a, bl, "bwd_input", du=du, scale=(mods[0], 1), x_in=x2d)
    dmod[0][0], dmod[0][1] = bw["dshift"], bw["dscale"]
    grad_x = bw["dx"].reshape(bl, s, d)

    dmod_rows = jnp.concatenate([r.reshape(bl, d) for layer in dmod for r in layer], axis=0)
    dmod_rows = dmod_rows.reshape(DEPTH, 6, bl, d).transpose(0, 2, 1, 3)
    n_mod = dmod_rows.size // LANES
    ln_parts = [dg_ln[i][k] for i in range(DEPTH) for k in range(2)] + [db_ln[i][k] for i in range(DEPTH) for k in range(2)]
    small_g = jnp.concatenate([dmod_rows.reshape(-1, LANES), dg_q.reshape(-1, LANES), dg_kv.reshape(-1, LANES), db_f]
                              + [p.reshape(-1, LANES) for p in ln_parts] + [loss_cols.reshape(-1, LANES)], axis=0)
    n_small = small_g.shape[0]
    small_g = jnp.pad(small_g, ((0, (-n_small) % 8), (0, 0)))
    small_g_all = all_gather(small_g, "gather_small_grads")
    small_sum = sum_leading(small_g_all, "sum_small_grads")
    per_seq = DEPTH * 6 * d // LANES
    dmod_all = small_g_all[:, :n_mod].reshape(N_DEV, DEPTH, bl, 6 * d).transpose(1, 0, 2, 3)
    dmod_all = dmod_all.reshape(DEPTH, N_DEV * bl, 6 * d)
    o1 = n_mod
    grad_g_q = small_sum[o1:o1 + MLA_QR // LANES].reshape(1, MLA_QR)
    o1 += MLA_QR // LANES
    grad_g_kv = small_sum[o1:o1 + MLA_KVR // LANES].reshape(1, MLA_KVR)
    o1 += MLA_KVR // LANES
    grad_b_f = small_sum[o1:o1 + 1, :FOX_HEADS]
    o1 += 1
    n_ln_rows = DEPTH * 2 * d // LANES
    grad_ln_g_full = small_sum[o1:o1 + n_ln_rows].reshape(DEPTH, 2, d)
    grad_ln_b_full = small_sum[o1 + n_ln_rows:o1 + 2 * n_ln_rows].reshape(DEPTH, 2, d)
    loss = jnp.sum(small_sum[o1 + 2 * n_ln_rows:o1 + 2 * n_ln_rows + d // LANES])
    shard = d // N_DEV
    grad_ln_g = lax.dynamic_slice_in_dim(grad_ln_g_full, dev * shard, shard, axis=2)
    grad_ln_b = lax.dynamic_slice_in_dim(grad_ln_b_full, dev * shard, shard, axis=2)
    by_seq = small_g_all[:, :n_mod].reshape(N_DEV, DEPTH, bl, 6 * d // LANES, LANES).transpose(0, 2, 1, 3, 4)
    grad_ada_b = sum_leading(by_seq.reshape(N_DEV * bl, per_seq, LANES), "sum_ada_b").reshape(DEPTH, 6 * d)
    dmod_cols = lax.dynamic_slice_in_dim(dmod_all, dev * ada_cols, ada_cols, axis=2)
    grad_ada_w = jnp.stack([mm_tn(c_act, dmod_cols[i], name=f"bwd_w_ada{i}") for i in range(DEPTH)])

    g_packed = jnp.concatenate(
        [slot(nm, grads_full[nm].reshape(N_DEV, rows_of[nm], PACK_COLS).astype(BF16)) for nm, _ in local], axis=1)
    from_sibling = rs_sibling_exchange(g_packed, "rs_sibling")
    chip_partial = rs_add_sibling(g_packed, from_sibling, "rs_add_sibling")
    from_chips = rs_chip_exchange(chip_partial, "rs_chips")
    g_mine = sum_leading(from_chips, "rs_sum_chips")

    def mine(nm, shape):
        return g_mine[offsets[nm]:offsets[nm] + rows_of[nm]].reshape(shape)

    def shard_t(nm, a):
        return t_last(mine(nm, t_last(a).shape))

    grads = {
        "mla_w_in": mine("mla_w_in", mla_w_in[0].shape)[None],
        "mla_g_q": grad_g_q,
        "mla_w_uq": shard_t("mla_w_uq", mla_w_uq[0])[None],
        "mla_g_kv": grad_g_kv,
        "mla_w_uk": shard_t("mla_w_uk", mla_w_uk[0])[None],
        "mla_w_uv": shard_t("mla_w_uv", mla_w_uv[0])[None],
        "mla_w_o": mine("mla_w_o", mla_w_o[0].shape)[None],
        "fox_w_in": shard_t("fox_w_in", fox_w_in[0])[None],
        "fox_b_f": grad_b_f,
        "fox_w_o": mine("fox_w_o", fox_w_o[0].shape)[None],
        "ada_w": grad_ada_w,
        "ada_b": grad_ada_b,
        "ffn_w_gate": jnp.stack([shard_t(f"gate{i}", ffn_w_gate[i]) for i in range(DEPTH)]),
        "ffn_w_up": jnp.stack([shard_t(f"up{i}", ffn_w_up[i]) for i in range(DEPTH)]),
        "ffn_w_down": jnp.stack([mine(f"down{i}", ffn_w_down[i].shape) for i in range(DEPTH)]),
        "ln_g": grad_ln_g,
        "ln_b": grad_ln_b,
    }
    weights = dict(mla_w_in=mla_w_in, mla_g_q=mla_g_q, mla_w_uq=mla_w_uq, mla_g_kv=mla_g_kv, mla_w_uk=mla_w_uk,
                   mla_w_uv=mla_w_uv, mla_w_o=mla_w_o, fox_w_in=fox_w_in, fox_b_f=fox_b_f, fox_w_o=fox_w_o,
                   ada_w=ada_w, ada_b=ada_b, ffn_w_gate=ffn_w_gate, ffn_w_up=ffn_w_up, ffn_w_down=ffn_w_down,
                   ln_g=ln_g, ln_b=ln_b)
    first = dict(mla_w_in=m_mla_w_in, mla_g_q=m_mla_g_q, mla_w_uq=m_mla_w_uq, mla_g_kv=m_mla_g_kv, mla_w_uk=m_mla_w_uk,
                 mla_w_uv=m_mla_w_uv, mla_w_o=m_mla_w_o, fox_w_in=m_fox_w_in, fox_b_f=m_fox_b_f, fox_w_o=m_fox_w_o,
                 ada_w=m_ada_w, ada_b=m_ada_b, ffn_w_gate=m_ffn_w_gate, ffn_w_up=m_ffn_w_up, ffn_w_down=m_ffn_w_down,
                 ln_g=m_ln_g, ln_b=m_ln_b)
    second = dict(mla_w_in=v_mla_w_in, mla_g_q=v_mla_g_q, mla_w_uq=v_mla_w_uq, mla_g_kv=v_mla_g_kv, mla_w_uk=v_mla_w_uk,
                  mla_w_uv=v_mla_w_uv, mla_w_o=v_mla_w_o, fox_w_in=v_fox_w_in, fox_b_f=v_fox_b_f, fox_w_o=v_fox_w_o,
                  ada_w=v_ada_w, ada_b=v_ada_b, ffn_w_gate=v_ffn_w_gate, ffn_w_up=v_ffn_w_up, ffn_w_down=v_ffn_w_down,
                  ln_g=v_ln_g, ln_b=v_ln_b)
    order = list(weights)
    g_out, d_out, m_out, v_out = [], [], [], []
    for nm in order:
        g = grads[nm].reshape(weights[nm].shape)
        delta, new_m, new_v = adamw(weights[nm], g, first[nm], second[nm], f"adamw_{nm}")
        g_out.append(g)
        d_out.append(delta)
        m_out.append(new_m)
        v_out.append(new_v)
    return (loss, grad_x, *g_out, *d_out, *m_out, *v_out)
```

```python
import functools

import jax
import jax.numpy as jnp
from jax import lax
from jax.experimental import pallas as pl
from jax.experimental.pallas import tpu as pltpu

F32 = jnp.float32
BF16 = jnp.bfloat16
LANES = 128
N_DEV = 8
VMEM_LIMIT_BYTES = 56 * 1024 * 1024

DEPTH = 2
MLA_HEADS = 8
MLA_NOPE = 128
MLA_ROPE = 64
MLA_V = 128
MLA_QR = 256
MLA_KVR = 256
ROPE_THETA = 10000.0
FOX_HEADS = 16
FOX_HD = 64
ALPHA = (2.0 * DEPTH) ** 0.25
NORM_EPS = 1e-5
ADAM_LR = 0.001
ADAM_B1 = 0.9
ADAM_B2 = 0.999
ADAM_EPS = 1e-08
ADAM_WD = 0.01
ADAM_STEP = 10

MESH_AXES = ("x", "y", "c")
MESH = pl.DeviceIdType.MESH


def _params(*sem):
    return pltpu.CompilerParams(dimension_semantics=sem, vmem_limit_bytes=VMEM_LIMIT_BYTES)


def _tile(n, cap, mult=LANES):
    if n <= cap:
        return n
    best = None
    for t in range(mult, cap + 1, mult):
        if n % t == 0:
            best = t
    assert best is not None, (n, cap, mult)
    return best


def _dot(a, b, dims):
    return lax.dot_general(a, b, (dims, ((), ())), preferred_element_type=F32)


def _nn(a, b):
    return _dot(a, b, ((1,), (0,)))


def _nt(a, b):
    return _dot(a, b, ((1,), (1,)))


def _tn(a, b):
    return _dot(a, b, ((0,), (0,)))


def _me():
    return lax.axis_index("x"), lax.axis_index("y"), lax.axis_index("c")


def all_gather(x_loc, name):
    r, c = x_loc.shape

    def body(x_ref, out_ref, send_sems, recv_sems, local_sem):
        x, y, cc = _me()
        me, sibling = (x, y, cc), (x, y, 1 - cc)
        chips = [(1 - x, y), (x, 1 - y), (1 - x, 1 - y)]

        def rows(px, py, pc):
            return out_ref.at[4 * px + 2 * py + pc]

        def copy(k, block, to, src=None):
            return pltpu.make_async_remote_copy(
                src_ref=rows(*block) if src is None else src, dst_ref=rows(*block),
                send_sem=send_sems.at[k], recv_sem=recv_sems.at[k], device_id=to, device_id_type=MESH)

        mine = pltpu.make_async_copy(x_ref, rows(*me), local_sem)
        mine.start()
        first = [copy(0, me, sibling, src=x_ref)]
        first += [copy(1 + j, me, (*chip, cc), src=x_ref) for j, chip in enumerate(chips)]
        for cp in first:
            cp.start()
        passed = [copy(4 + j, (*chip, cc), sibling) for j, chip in enumerate(chips)]
        for j, chip in enumerate(chips):
            copy(1 + j, (*chip, cc), me).wait_recv()
            passed[j].start()
        copy(0, sibling, me).wait_recv()
        for j, chip in enumerate(chips):
            copy(4 + j, (*chip, 1 - cc), me).wait_recv()
        for cp in first + passed:
            cp.wait_send()
        mine.wait()

    return pl.pallas_call(
        body, name=name,
        out_shape=jax.ShapeDtypeStruct((N_DEV, r, c), x_loc.dtype),
        in_specs=[pl.BlockSpec(memory_space=pl.ANY)],
        out_specs=pl.BlockSpec(memory_space=pl.ANY),
        scratch_shapes=[pltpu.SemaphoreType.DMA((7,)), pltpu.SemaphoreType.DMA((7,)), pltpu.SemaphoreType.DMA(())],
    )(x_loc)


HBM_SPEC = pl.BlockSpec(memory_space=pltpu.HBM)
SEM_SPEC = pl.BlockSpec(memory_space=pltpu.SEMAPHORE)
N_PEERS = N_DEV - 1


def _peer(k):
    x, y, c = _me()
    return (1 - x if k & 4 else x, 1 - y if k & 2 else y, 1 - c if k & 1 else c)


def _exchange_copies(src_ref, land_ref, send_sems, recv_sems, scatter):
    x, y, c = _me()
    mine = 4 * x + 2 * y + c
    copies = []
    for k in range(1, N_DEV):
        px, py, pc = _peer(k)
        src = src_ref.at[4 * px + 2 * py + pc] if scatter else src_ref
        copies.append(pltpu.make_async_remote_copy(
            src_ref=src, dst_ref=land_ref.at[mine], send_sem=send_sems.at[k - 1], recv_sem=recv_sems.at[k - 1],
            device_id=(px, py, pc), device_id_type=MESH))
    return copies


def exchange_start(src, land, name, scatter):
    def body(src_ref, land_ref, send_sems, recv_sems, src_thru, land_thru, token):
        for cp in _exchange_copies(src_ref, land_ref, send_sems, recv_sems, scatter):
            cp.start()
        token[...] = jnp.zeros_like(token)

    return pl.pallas_call(
        body, name=name,
        out_shape=(pltpu.SemaphoreType.DMA((N_PEERS,)), pltpu.SemaphoreType.DMA((N_PEERS,)),
                   pltpu.HBM(src.shape, src.dtype), pltpu.HBM(land.shape, land.dtype),
                   jax.ShapeDtypeStruct((8, LANES), F32)),
        in_specs=(HBM_SPEC, HBM_SPEC),
        out_specs=(SEM_SPEC, SEM_SPEC, HBM_SPEC, HBM_SPEC, pl.BlockSpec(memory_space=pltpu.VMEM)),
        input_output_aliases={0: 2, 1: 3},
        compiler_params=pltpu.CompilerParams(has_side_effects=pltpu.SideEffectType.DATAFLOW_SIDE_EFFECTING),
    )(pltpu.with_memory_space_constraint(src, pltpu.HBM), pltpu.with_memory_space_constraint(land, pltpu.HBM))


def exchange_wait(started, after, name, scatter):
    send_sems, recv_sems, src_thru, land_thru, _ = started

    def body(src_ref, land_ref, send_sems, recv_sems, after_ref, src_dead, got_ref):
        for cp in _exchange_copies(src_ref, land_ref, send_sems, recv_sems, scatter):
            cp.wait_send()
            cp.wait_recv()

    return pl.pallas_call(
        body, name=name,
        out_shape=(pltpu.HBM(src_thru.shape, src_thru.dtype), pltpu.HBM(land_thru.shape, land_thru.dtype)),
        in_specs=(HBM_SPEC, HBM_SPEC, SEM_SPEC, SEM_SPEC, pl.BlockSpec(memory_space=pl.ANY)),
        out_specs=(HBM_SPEC, HBM_SPEC), input_output_aliases={0: 0, 1: 1},
        compiler_params=pltpu.CompilerParams(has_side_effects=pltpu.SideEffectType.DATAFLOW_SIDE_EFFECTING),
    )(src_thru, land_thru, send_sems, recv_sems, after)[1]


def after_token(small, started):
    return small + started[4][0, 0]


def sum_leading(x, name):
    n, r, c = x.shape
    tr = _tile(r, 512, 16)

    def body(x_ref, o_ref):
        acc = x_ref[0].astype(F32)
        for k in range(1, n):
            acc = acc + x_ref[k].astype(F32)
        o_ref[...] = acc

    return pl.pallas_call(
        body, name=name,
        out_shape=jax.ShapeDtypeStruct((r, c), F32),
        grid=(r // tr,),
        in_specs=[pl.BlockSpec((n, tr, c), lambda i: (0, i, 0))],
        out_specs=pl.BlockSpec((tr, c), lambda i: (i, 0)),
        compiler_params=_params("arbitrary"),
    )(x)


MM_VMEM_BUDGET = 36 * 1024 * 1024
GRID_STEP_AS_BYTES = 1 << 20


def _mm_tiles(m, n, a_row_bytes, b_col_bytes, out_bytes):
    tms = [c for c in (2048, 1024, 512, 256, 128, 64, 32, 16, 8) if m % c == 0] or [m]
    tns = [c for c in range(LANES, min(n, 2048) + 1, LANES) if n % c == 0] or [n]
    best = None
    for tm in tms:
        for tn in tns:
            vmem = 2 * (tm * a_row_bytes + tn * b_col_bytes) + 2 * tm * tn * out_bytes + tm * tn * 4
            if vmem > MM_VMEM_BUDGET:
                continue
            steps = (m // tm) * (n // tn)
            cost = steps * GRID_STEP_AS_BYTES + (m // tm) * n * b_col_bytes + m * a_row_bytes
            if best is None or cost < best[0]:
                best = (cost, tm, tn)
    assert best is not None, (m, n, a_row_bytes, b_col_bytes)
    return best[1], best[2]


def mm(pairs, *, trans_b, out_dtype, name, out_slab=False, bias=None):
    a0 = pairs[0][0]
    m = a0.shape[1] if a0.ndim == 3 else a0.shape[0]
    n = pairs[0][1].shape[0] if trans_b else pairs[0][1].shape[1]
    a_row_bytes = sum((b.shape[1] if trans_b else b.shape[0]) * a.dtype.itemsize for a, b in pairs)
    b_col_bytes = sum((b.shape[1] if trans_b else b.shape[0]) * b.dtype.itemsize for _, b in pairs)
    tm, tn = _mm_tiles(m, n, a_row_bytes, b_col_bytes, jnp.dtype(out_dtype).itemsize)
    slabs = [a.ndim == 3 for a, _ in pairs]
    n_pairs = len(pairs)

    def body(*refs):
        o_ref = refs[-1]
        acc = bias_ref = None
        if bias is not None:
            bias_ref = refs[2 * n_pairs]
        for i in range(n_pairs):
            a_ref, b_ref = refs[2 * i], refs[2 * i + 1]
            if slabs[i]:
                a = jnp.concatenate([a_ref[s].astype(BF16) for s in range(a_ref.shape[0])], axis=1)
            else:
                a = a_ref[...].astype(BF16)
            b = b_ref[...].astype(BF16)
            part = _nt(a, b) if trans_b else _nn(a, b)
            acc = part if acc is None else acc + part
        if bias_ref is not None:
            acc = acc + bias_ref[...]
        if out_slab:
            for s in range(tn // LANES):
                o_ref[s] = acc[:, s * LANES:(s + 1) * LANES].astype(out_dtype)
        else:
            o_ref[...] = acc.astype(out_dtype)

    in_specs, args = [], []
    for (a, b), slab in zip(pairs, slabs):
        if slab:
            in_specs.append(pl.BlockSpec((a.shape[0], tm, LANES), lambda i, j: (0, i, 0)))
        else:
            in_specs.append(pl.BlockSpec((tm, a.shape[1]), lambda i, j: (i, 0)))
        if trans_b:
            in_specs.append(pl.BlockSpec((tn, b.shape[1]), lambda i, j: (j, 0)))
        else:
            in_specs.append(pl.BlockSpec((b.shape[0], tn), lambda i, j: (0, j)))
        args += [a, b]
    if bias is not None:
        in_specs.append(pl.BlockSpec((1, tn), lambda i, j: (0, j)))
        args.append(bias)
    if out_slab:
        out_shape = jax.ShapeDtypeStruct((n // LANES, m, LANES), out_dtype)
        out_spec = pl.BlockSpec((tn // LANES, tm, LANES), lambda i, j: (j, i, 0))
    else:
        out_shape = jax.ShapeDtypeStruct((m, n), out_dtype)
        out_spec = pl.BlockSpec((tm, tn), lambda i, j: (i, j))
    return pl.pallas_call(
        body, name=name, out_shape=out_shape, grid=(m // tm, n // tn),
        in_specs=in_specs, out_specs=out_spec,
        compiler_params=_params("arbitrary", "arbitrary"),
    )(*args)


def mm_tn(a, b, *, name, tk_cap=1536, tn_cap=1024, tm_cap=512):
    slab = a.ndim == 3
    m = a.shape[1] if slab else a.shape[0]
    k = a.shape[0] * LANES if slab else a.shape[1]
    n = b.shape[1]
    tk = _tile(k, tk_cap)
    tn = _tile(n, tn_cap)
    tm = _tile(m, tm_cap, 8)

    def body(a_ref, b_ref, o_ref):
        @pl.when(pl.program_id(2) == 0)
        def _():
            o_ref[...] = jnp.zeros_like(o_ref)

        bb = b_ref[...].astype(BF16)
        if slab:
            for s in range(tk // LANES):
                o_ref[s * LANES:(s + 1) * LANES, :] += _tn(a_ref[s].astype(BF16), bb)
        else:
            o_ref[...] += _tn(a_ref[...].astype(BF16), bb)

    if slab:
        a_spec = pl.BlockSpec((tk // LANES, tm, LANES), lambda i, j, t: (i, t, 0))
    else:
        a_spec = pl.BlockSpec((tm, tk), lambda i, j, t: (t, i))
    return pl.pallas_call(
        body, name=name, out_shape=jax.ShapeDtypeStruct((k, n), F32), grid=(k // tk, n // tn, m // tm),
        in_specs=[a_spec, pl.BlockSpec((tm, tn), lambda i, j, t: (t, j))],
        out_specs=pl.BlockSpec((tk, tn), lambda i, j, t: (i, j)),
        compiler_params=_params("arbitrary", "arbitrary", "arbitrary"),
    )(a, b)


def _row_spec(d, k):
    return pl.BlockSpec((1, 1, d), lambda b, i: (6 * b + k, 0, 0))


def modulate(x, mod, k_shift, k_scale, bl, name):
    t, d = x.shape
    s = t // bl
    tm = _tile(s, 512, 8)
    nt = s // tm

    def body(x_ref, sh_ref, sc_ref, o_ref):
        o_ref[...] = (x_ref[...] * (1.0 + sc_ref[0]) + sh_ref[0]).astype(BF16)

    return pl.pallas_call(
        body, name=name, out_shape=jax.ShapeDtypeStruct((t, d), BF16), grid=(bl, nt),
        in_specs=[pl.BlockSpec((tm, d), lambda b, i: (b * nt + i, 0)), _row_spec(d, k_shift), _row_spec(d, k_scale)],
        out_specs=pl.BlockSpec((tm, d), lambda b, i: (b * nt + i, 0)),
        compiler_params=_params("arbitrary", "arbitrary"),
    )(x, mod, mod)


def _layer_norm_stats(r):
    mu = jnp.mean(r, axis=-1, keepdims=True)
    rc = r - mu
    var = jnp.mean(rc * rc, axis=-1, keepdims=True)
    rstd = lax.rsqrt(var + NORM_EPS)
    return rc * rstd, rstd


def residual_layer_norm(x, y, mod, k_gate, g, b, bl, name, next_mod=None):
    t, d = x.shape
    s = t // bl
    tm = _tile(s, 256, 8)
    nt = s // tm
    has_next = next_mod is not None

    def body(*refs):
        x_ref, y_ref, gt_ref, g_ref, b_ref = refs[:5]
        rest = refs[5:]
        if has_next:
            sh_ref, sc_ref, o_ref, r_ref, u_ref = rest
        else:
            o_ref, r_ref = rest
        r = ALPHA * x_ref[...] + (1.0 + gt_ref[0]) * y_ref[...]
        xhat, _ = _layer_norm_stats(r)
        out = xhat * g_ref[...] + b_ref[...]
        o_ref[...] = out
        r_ref[...] = r
        if has_next:
            u_ref[...] = (out * (1.0 + sc_ref[0]) + sh_ref[0]).astype(BF16)

    tok = pl.BlockSpec((tm, d), lambda bb, i: (bb * nt + i, 0))
    vec = pl.BlockSpec((1, d), lambda bb, i: (0, 0))
    in_specs = [tok, tok, _row_spec(d, k_gate), vec, vec]
    args = [x, y, mod, g, b]
    out_shape = [jax.ShapeDtypeStruct((t, d), F32), jax.ShapeDtypeStruct((t, d), F32)]
    out_specs = [tok, tok]
    if has_next:
        in_specs += [_row_spec(d, next_mod[0]), _row_spec(d, next_mod[1])]
        args += [mod if len(next_mod) == 2 else next_mod[2]] * 2
        out_shape.append(jax.ShapeDtypeStruct((t, d), BF16))
        out_specs.append(tok)
    return pl.pallas_call(
        body, name=name, out_shape=out_shape, grid=(bl, nt), in_specs=in_specs, out_specs=out_specs,
        compiler_params=_params("arbitrary", "arbitrary"),
    )(*args)


def loss_head(xo, target, name):
    t, d = xo.shape
    tm = _tile(t, 512, 8)

    def body(x_ref, t_ref, l_ref, dx_ref):
        @pl.when(pl.program_id(0) == 0)
        def _():
            l_ref[...] = jnp.zeros_like(l_ref)

        e = x_ref[...] - t_ref[...]
        l_ref[...] += jnp.sum(e * e, axis=0, keepdims=True) * (0.5 / d)
        dx_ref[...] = e * (1.0 / d)

    tok = pl.BlockSpec((tm, d), lambda i: (i, 0))
    return pl.pallas_call(
        body, name=name,
        out_shape=[jax.ShapeDtypeStruct((1, d), F32), jax.ShapeDtypeStruct((t, d), F32)],
        grid=(t // tm,), in_specs=[tok, tok],
        out_specs=[pl.BlockSpec((1, d), lambda i: (0, 0)), tok],
        compiler_params=_params("arbitrary"),
    )(xo, target)


def sublayer_backward(d_a, bl, name, *, du=None, scale=None, x_in=None, ln=None):
    t, d = d_a.shape
    s = t // bl
    tm = _tile(s, 256, 8)
    nt = s // tm
    has_mod = du is not None
    has_ln = ln is not None
    assert has_mod or has_ln
    assert has_ln or x_in is not None

    def body(*refs):
        refs = list(refs)
        da_ref = refs.pop(0)
        if has_mod:
            du_ref, sc_ref = refs.pop(0), refs.pop(0)
        if has_ln:
            r_ref, y_ref, g_ref, b_ref, gt_ref = (refs.pop(0) for _ in range(5))
        elif has_mod:
            xin_ref = refs.pop(0)
        dx_ref = refs.pop(0)
        if has_ln:
            dy_ref, dg_ref, db_ref, dgt_ref = (refs.pop(0) for _ in range(4))
        if has_mod:
            dsc_ref, dsh_ref = refs.pop(0), refs.pop(0)
        first_tile = pl.program_id(1) == 0
        first_step = jnp.logical_and(pl.program_id(0) == 0, first_tile)

        dout = da_ref[...]
        if has_ln:
            xhat, rstd = _layer_norm_stats(r_ref[...])
        if has_mod:
            duv = du_ref[...]
            dout = dout + duv * (1.0 + sc_ref[0])
            xin = xhat * g_ref[...] + b_ref[...] if has_ln else xin_ref[...]

            @pl.when(first_tile)
            def _():
                dsc_ref[...] = jnp.zeros_like(dsc_ref)
                dsh_ref[...] = jnp.zeros_like(dsh_ref)

            dsc_ref[0] += jnp.sum(duv * xin, axis=0, keepdims=True)
            dsh_ref[0] += jnp.sum(duv, axis=0, keepdims=True)
        if not has_ln:
            dx_ref[...] = dout
            return

        @pl.when(first_step)
        def _():
            dg_ref[...] = jnp.zeros_like(dg_ref)
            db_ref[...] = jnp.zeros_like(db_ref)

        @pl.when(first_tile)
        def _():
            dgt_ref[...] = jnp.zeros_like(dgt_ref)

        dg_ref[...] += jnp.sum(dout * xhat, axis=0, keepdims=True)
        db_ref[...] += jnp.sum(dout, axis=0, keepdims=True)
        dxh = dout * g_ref[...]
        dr = rstd * (dxh - jnp.mean(dxh, axis=-1, keepdims=True) - xhat * jnp.mean(dxh * xhat, axis=-1, keepdims=True))
        dx_ref[...] = ALPHA * dr
        dy_ref[...] = ((1.0 + gt_ref[0]) * dr).astype(BF16)
        dgt_ref[0] += jnp.sum(dr * y_ref[...], axis=0, keepdims=True)

    tok = pl.BlockSpec((tm, d), lambda bb, i: (bb * nt + i, 0))
    vec = pl.BlockSpec((1, d), lambda bb, i: (0, 0))
    seq = pl.BlockSpec((1, 1, d), lambda bb, i: (bb, 0, 0))
    in_specs, args = [tok], [d_a]
    if has_mod:
        in_specs += [tok, _row_spec(d, scale[1])]
        args += [du, scale[0]]
    if has_ln:
        r, y, g, b, gate = ln
        in_specs += [tok, tok, vec, vec, _row_spec(d, gate[1])]
        args += [r, y, g, b, gate[0]]
    elif has_mod:
        in_specs.append(tok)
        args.append(x_in)
    names = ["dx"]
    out_shape, out_specs = [jax.ShapeDtypeStruct((t, d), F32)], [tok]
    if has_ln:
        names += ["dy", "dg", "db", "dgate"]
        out_shape += [jax.ShapeDtypeStruct((t, d), BF16), jax.ShapeDtypeStruct((1, d), F32),
                      jax.ShapeDtypeStruct((1, d), F32), jax.ShapeDtypeStruct((bl, 1, d), F32)]
        out_specs += [tok, vec, vec, seq]
    if has_mod:
        names += ["dscale", "dshift"]
        out_shape += [jax.ShapeDtypeStruct((bl, 1, d), F32)] * 2
        out_specs += [seq, seq]
    outs = pl.pallas_call(
        body, name=name, out_shape=out_shape, grid=(bl, nt), in_specs=in_specs, out_specs=out_specs,
        compiler_params=_params("arbitrary", "arbitrary"),
    )(*args)
    return dict(zip(names, outs))


def _silu(a):
    return a * jax.nn.sigmoid(a)


def silu_rows(a, name):
    def body(a_ref, o_ref):
        o_ref[...] = _silu(a_ref[...]).astype(BF16)

    return pl.pallas_call(body, name=name, out_shape=jax.ShapeDtypeStruct(a.shape, BF16))(a)


def swiglu_forward(a, b, name):
    t, f = a.shape
    tm, tf = _tile(t, 512, 8), _tile(f, 1536)

    def body(a_ref, b_ref, h_ref):
        h_ref[...] = (_silu(a_ref[...]) * b_ref[...]).astype(BF16)

    spec = pl.BlockSpec((tm, tf), lambda i, j: (i, j))
    return pl.pallas_call(
        body, name=name, out_shape=jax.ShapeDtypeStruct((t, f), BF16), grid=(t // tm, f // tf),
        in_specs=[spec, spec], out_specs=spec, compiler_params=_params("arbitrary", "arbitrary"),
    )(a, b)


def swiglu_backward(dh, a, b, name):
    t, f = a.shape
    tm, tf = _tile(t, 512, 8), _tile(f, 1536)

    def body(dh_ref, a_ref, b_ref, da_ref, db_ref):
        av = a_ref[...]
        sig = jax.nn.sigmoid(av)
        dhv = dh_ref[...]
        da_ref[...] = (dhv * b_ref[...] * (sig * (1.0 + av * (1.0 - sig)))).astype(BF16)
        db_ref[...] = (dhv * (av * sig)).astype(BF16)

    spec = pl.BlockSpec((tm, tf), lambda i, j: (i, j))
    return pl.pallas_call(
        body, name=name, out_shape=[jax.ShapeDtypeStruct((t, f), BF16)] * 2, grid=(t // tm, f // tf),
        in_specs=[spec, spec, spec], out_specs=[spec, spec], compiler_params=_params("arbitrary", "arbitrary"),
    )(dh, a, b)


def rope_tables(pos, inv_freq, sign, name):
    t = pos.shape[0]
    tm = _tile(t, 512, 8)

    def body(p_ref, f_ref, s_ref, c_out, s_out):
        ang = p_ref[...] * f_ref[...]
        c_out[...] = jnp.cos(ang)
        s_out[...] = jnp.sin(ang) * s_ref[...]

    vec = pl.BlockSpec((1, LANES), lambda i: (0, 0))
    tab = pl.BlockSpec((tm, LANES), lambda i: (i, 0))
    return pl.pallas_call(
        body, name=name, out_shape=[jax.ShapeDtypeStruct((t, LANES), F32)] * 2, grid=(t // tm,),
        in_specs=[pl.BlockSpec((tm, 1), lambda i: (i, 0)), vec, vec], out_specs=[tab, tab],
        compiler_params=_params("arbitrary"),
    )(pos, inv_freq, sign)


def _rot_half(v):
    lane = lax.broadcasted_iota(jnp.int32, v.shape, v.ndim - 1)
    up = pltpu.roll(v, LANES - MLA_ROPE // 2, v.ndim - 1)
    down = pltpu.roll(v, MLA_ROPE // 2, v.ndim - 1)
    return jnp.where(lane % MLA_ROPE < MLA_ROPE // 2, up, down)


def _rope(v, cos, sin_signed):
    return v * cos + _rot_half(v) * sin_signed


def _rope_transposed(dv, cos, sin_signed):
    return dv * cos + _rot_half(dv * sin_signed)


def rope_slabs(v, cos, sin_signed, out_dtype, name, transposed=False):
    ns, t, _ = v.shape
    tm = _tile(t, 512, 8)
    fn = _rope_transposed if transposed else _rope

    def body(v_ref, c_ref, s_ref, o_ref):
        o_ref[0] = fn(v_ref[0].astype(F32), c_ref[...], s_ref[...]).astype(out_dtype)

    tab = pl.BlockSpec((tm, LANES), lambda j, i: (i, 0))
    spec = pl.BlockSpec((1, tm, LANES), lambda j, i: (j, i, 0))
    return pl.pallas_call(
        body, name=name, out_shape=jax.ShapeDtypeStruct(v.shape, out_dtype), grid=(ns, t // tm),
        in_specs=[spec, tab, tab], out_specs=spec, compiler_params=_params("arbitrary", "arbitrary"),
    )(v, cos, sin_signed)


def _rms(x):
    rinv = lax.rsqrt(jnp.mean(x * x, axis=-1, keepdims=True) + NORM_EPS)
    return x * rinv, rinv


def mla_latents_forward(h_in, g_q, g_kv, cos, sin_signed, name):
    t = h_in.shape[0]
    tm = _tile(t, 512, 8)

    def body(h_ref, gq_ref, gkv_ref, c_ref, s_ref, cq_ref, ckv_ref, kr_ref):
        cq_ref[...] = (_rms(h_ref[:, 0:MLA_QR])[0] * gq_ref[...]).astype(BF16)
        ckv_ref[...] = (_rms(h_ref[:, MLA_QR:MLA_QR + MLA_KVR])[0] * gkv_ref[...]).astype(BF16)
        kr_ref[...] = _rope(h_ref[:, MLA_QR + MLA_KVR:], c_ref[...], s_ref[...]).astype(BF16)

    def tok(w):
        return pl.BlockSpec((tm, w), lambda i: (i, 0))

    def vec(w):
        return pl.BlockSpec((1, w), lambda i: (0, 0))

    return pl.pallas_call(
        body, name=name,
        out_shape=[jax.ShapeDtypeStruct((t, MLA_QR), BF16), jax.ShapeDtypeStruct((t, MLA_KVR), BF16),
                   jax.ShapeDtypeStruct((t, LANES), BF16)],
        grid=(t // tm,),
        in_specs=[tok(h_in.shape[1]), vec(MLA_QR), vec(MLA_KVR), tok(LANES), tok(LANES)],
        out_specs=[tok(MLA_QR), tok(MLA_KVR), tok(LANES)],
        compiler_params=_params("arbitrary"),
    )(h_in, g_q, g_kv, cos, sin_signed)


def mla_latents_backward(h_in, dcq, dckv, dkr, g_q, g_kv, cos, sin_signed, name):
    t, w = h_in.shape
    tm = _tile(t, 512, 8)

    def body(h_ref, dcq_ref, dckv_ref, dkr_ref, gq_ref, gkv_ref, c_ref, s_ref, dh_ref, dgq_ref, dgkv_ref):
        @pl.when(pl.program_id(0) == 0)
        def _():
            dgq_ref[...] = jnp.zeros_like(dgq_ref)
            dgkv_ref[...] = jnp.zeros_like(dgkv_ref)

        def rms_bwd(x, dc, g_ref, dg_ref):
            xn, rinv = _rms(x)
            dg_ref[...] += jnp.sum(dc * xn, axis=0, keepdims=True)
            dxn = dc * g_ref[...]
            return rinv * (dxn - xn * jnp.mean(dxn * xn, axis=-1, keepdims=True))

        dq = rms_bwd(h_ref[:, 0:MLA_QR], dcq_ref[...], gq_ref, dgq_ref)
        dkv = rms_bwd(h_ref[:, MLA_QR:MLA_QR + MLA_KVR], dckv_ref[...], gkv_ref, dgkv_ref)
        dr = _rope_transposed(dkr_ref[...], c_ref[...], s_ref[...])
        dh_ref[...] = jnp.concatenate([dq, dkv, dr], axis=1).astype(BF16)

    def tok(ww):
        return pl.BlockSpec((tm, ww), lambda i: (i, 0))

    def vec(ww):
        return pl.BlockSpec((1, ww), lambda i: (0, 0))

    return pl.pallas_call(
        body, name=name,
        out_shape=[jax.ShapeDtypeStruct((t, w), BF16), jax.ShapeDtypeStruct((1, MLA_QR), F32),
                   jax.ShapeDtypeStruct((1, MLA_KVR), F32)],
        grid=(t // tm,),
        in_specs=[tok(w), tok(MLA_QR), tok(MLA_KVR), tok(LANES), vec(MLA_QR), vec(MLA_KVR), tok(LANES), tok(LANES)],
        out_specs=[tok(w), vec(MLA_QR), vec(MLA_KVR)],
        compiler_params=_params("arbitrary"),
    )(h_in, dcq, dckv, dkr, g_q, g_kv, cos, sin_signed)


def _tri(n, lower):
    r = lax.broadcasted_iota(jnp.int32, (n, n), 0)
    c = lax.broadcasted_iota(jnp.int32, (n, n), 1)
    return jnp.where(r >= c if lower else r <= c, 1.0, 0.0).astype(F32)


def _dot_exact(tri, v):
    hi = v.astype(BF16)
    mid = (v - hi.astype(F32)).astype(BF16)
    lo = (v - hi.astype(F32) - mid.astype(F32)).astype(BF16)
    t = tri.astype(BF16)
    return _nn(t, hi) + _nn(t, mid) + _nn(t, lo)


def fox_gate_forward(z, b_f, bl, name):
    t = z.shape[0]
    s = t // bl
    ch = LANES
    n_ch = s // ch

    def body(z_ref, b_ref, f_ref, fs_ref):
        tri = _tri(ch, True)
        carry = jnp.zeros((1, LANES), F32)
        for k in range(n_ch):
            x = z_ref[k * ch:(k + 1) * ch, :] + b_ref[...]
            logf = jnp.minimum(x, 0.0) - jnp.log(1.0 + jnp.exp(-jnp.abs(x)))
            cs = _dot_exact(tri, logf) + carry
            carry = cs[ch - 1:ch, :]
            f_ref[k * ch:(k + 1) * ch, :] = cs
            for h in range(FOX_HEADS):
                fs_ref[h, k * ch:(k + 1) * ch, :] = jnp.broadcast_to(cs[:, h:h + 1], (ch, LANES))

    return pl.pallas_call(
        body, name=name,
        out_shape=[jax.ShapeDtypeStruct((t, LANES), F32), jax.ShapeDtypeStruct((FOX_HEADS, t, LANES), F32)],
        grid=(bl,),
        in_specs=[pl.BlockSpec((s, LANES), lambda b: (b, 0)), pl.BlockSpec((1, LANES), lambda b: (0, 0))],
        out_specs=[pl.BlockSpec((s, LANES), lambda b: (b, 0)),
                   pl.BlockSpec((FOX_HEADS, s, LANES), lambda b: (0, b, 0))],
        compiler_params=_params("arbitrary"),
    )(z, b_f)


def fox_gate_backward(z, b_f, df, bl, name):
    t = z.shape[0]
    s = t // bl
    ch = LANES
    n_ch = s // ch

    def body(z_ref, b_ref, df_ref, dz_ref, db_ref):
        @pl.when(pl.program_id(0) == 0)
        def _():
            db_ref[...] = jnp.zeros_like(db_ref)

        tri = _tri(ch, False)
        carry = jnp.zeros((1, LANES), F32)
        for k in reversed(range(n_ch)):
            cs = _dot_exact(tri, df_ref[k * ch:(k + 1) * ch, :]) + carry
            carry = cs[0:1, :]
            x = z_ref[k * ch:(k + 1) * ch, :] + b_ref[...]
            dz = cs * (1.0 - jax.nn.sigmoid(x))
            dz_ref[k * ch:(k + 1) * ch, :] = dz
            db_ref[...] += jnp.sum(dz, axis=0, keepdims=True)

    tok = pl.BlockSpec((s, LANES), lambda b: (b, 0))
    vec = pl.BlockSpec((1, LANES), lambda b: (0, 0))
    return pl.pallas_call(
        body, name=name,
        out_shape=[jax.ShapeDtypeStruct((t, LANES), F32), jax.ShapeDtypeStruct((1, LANES), F32)],
        grid=(bl,), in_specs=[tok, vec, tok], out_specs=[tok, vec],
        compiler_params=_params("arbitrary"),
    )(z, b_f, df)


NEG_INF = float("-inf")


def _attn_tiles(s):
    return _tile(s, 512, 8)


def attention_forward(kind, ops, bl, scale, name):
    fox = kind == "fox"
    if fox:
        qkv, fq, fk = ops
        t = qkv.shape[1]
        n_pair = FOX_HEADS // 2
    else:
        qn, qr, kn, kr, v = ops
        t = qn.shape[1]
        n_pair = MLA_HEADS // 2
    s = t // bl
    tq = _attn_tiles(s)
    nq = s // tq
    half = LANES // 2

    def body(*refs):
        if fox:
            q_ref, k_ref, v_ref, fq_ref, fk_ref, o_ref, lse_ref, o32_ref = refs
        else:
            qn_ref, qr_ref, kn_ref, kr_ref, v_ref, o_ref, lse_ref = refs
        i = pl.program_id(2)
        row = lax.broadcasted_iota(jnp.int32, (tq, tq), 0)
        col = lax.broadcasted_iota(jnp.int32, (tq, tq), 1)
        heads = []
        for e in range(2):
            sl = slice(e * half, (e + 1) * half)
            if fox:
                heads.append((sl, q_ref[0, :, sl], None))
            else:
                heads.append((sl, qn_ref[e], qr_ref[0, :, sl]))
        dv = half if fox else LANES

        def wide(stat):
            return jnp.concatenate([stat] * (tq // LANES), axis=1)

        def step(j, carry, masked):
            rows = pl.ds(pl.multiple_of(j * tq, tq), tq)
            new = []
            for e, (sl, qa, qb) in enumerate(heads):
                m, l, acc = carry[e]
                if fox:
                    sc = _nt(qa, k_ref[0, rows, sl]) * scale + wide(fq_ref[e]) - fk_ref[0, j, e:e + 1, :]
                    vv = v_ref[0, rows, sl]
                else:
                    sc = (_nt(qa, kn_ref[e, rows, :]) + _nt(qb, kr_ref[rows, 0:half])) * scale
                    vv = v_ref[e, rows, :]
                if masked:
                    sc = jnp.where(row >= col, sc, NEG_INF)
                m_new = jnp.maximum(m, jnp.max(sc, axis=1, keepdims=True))
                p = jnp.exp(sc - m_new)
                a = jnp.exp(m - m_new)
                l = a * l + jnp.sum(p, axis=1, keepdims=True)
                p_hi = p.astype(BF16)
                acc = a * acc + _nn(p_hi, vv)
                if fox:
                    acc = acc + _nn((p - p_hi.astype(F32)).astype(BF16), vv)
                new.append((m_new, l, acc))
            return tuple(new)

        init = (jnp.full((tq, 1), NEG_INF, F32), jnp.zeros((tq, 1), F32), jnp.zeros((tq, dv), F32))
        carry = step(i, (init, init), True)
        carry = lax.fori_loop(0, i, lambda j, c: step(j, c, False), carry)
        outs = [acc / l for _, l, acc in carry]
        for e, (m, l, _) in enumerate(carry):
            lse_ref[e] = jnp.broadcast_to(m + jnp.log(l), (tq, LANES))
        if fox:
            o32 = jnp.concatenate(outs, axis=1)
            o32_ref[0] = o32
            o_ref[0] = o32.astype(BF16)
        else:
            o_ref[0] = outs[0].astype(BF16)
            o_ref[1] = outs[1].astype(BF16)

    def q_idx(b, g, i):
        return (g, b * nq + i, 0)

    if fox:
        nk = fk.shape[1]
        in_specs = [pl.BlockSpec((1, tq, LANES), q_idx),
                    pl.BlockSpec((1, s, LANES), lambda b, g, i: (n_pair + g, b, 0)),
                    pl.BlockSpec((1, s, LANES), lambda b, g, i: (2 * n_pair + g, b, 0)),
                    pl.BlockSpec((2, tq, LANES), q_idx),
                    pl.BlockSpec((1, nk, 8, tq), lambda b, g, i: (b * n_pair + g, 0, 0, 0))]
        args = [qkv, qkv, qkv, fq, fk]
        o_spec = pl.BlockSpec((1, tq, LANES), q_idx)
    else:
        in_specs = [pl.BlockSpec((2, tq, LANES), q_idx),
                    pl.BlockSpec((1, tq, LANES), q_idx),
                    pl.BlockSpec((2, s, LANES), lambda b, g, i: (g, b, 0)),
                    pl.BlockSpec((s, LANES), lambda b, g, i: (b, 0)),
                    pl.BlockSpec((2, s, LANES), lambda b, g, i: (g, b, 0))]
        args = [qn, qr, kn, kr, v]
        o_spec = pl.BlockSpec((2, tq, LANES), q_idx)
    out_shape = [jax.ShapeDtypeStruct((8, t, LANES), BF16), jax.ShapeDtypeStruct((2 * n_pair, t, LANES), F32)]
    out_specs = [o_spec, pl.BlockSpec((2, tq, LANES), q_idx)]
    if fox:
        out_shape.append(jax.ShapeDtypeStruct((8, t, LANES), F32))
        out_specs.append(o_spec)
    outs = pl.pallas_call(
        body, name=name, out_shape=out_shape, grid=(bl, n_pair, nq), in_specs=in_specs, out_specs=out_specs,
        compiler_params=_params("arbitrary", "arbitrary", "arbitrary"),
    )(*args)
    return (outs[0], outs[1], outs[2] if fox else outs[0])


def attention_backward(kind, ops, o, do, lse, bl, scale, name):
    fox = kind == "fox"
    if fox:
        qkv, fq, fk = ops
        t = qkv.shape[1]
        n_pair = FOX_HEADS // 2
    else:
        qn, qr, kn, kr, v = ops
        t = qn.shape[1]
        n_pair = MLA_HEADS // 2
    s = t // bl
    tq = _attn_tiles(s)
    nq = s // tq
    half = LANES // 2

    def body(*refs):
        if fox:
            (q_ref, k_ref, v_ref, fq_ref, fk_ref, o_ref, do_ref, lse_ref,
             dq_ref, dk_ref, dv_ref, dfk_ref, delta_scr, qt_scr, dot_scr) = refs
        else:
            (qn_ref, qr_ref, kn_ref, kr_ref, v_ref, o_ref, do_ref, lse_ref,
             dqn_ref, dqr_ref, dkn_ref, dv_ref, dkr_ref, delta_scr, qt_scr, qrt_scr, dot_scr) = refs
        g, j = pl.program_id(1), pl.program_id(2)
        row = lax.broadcasted_iota(jnp.int32, (tq, tq), 0)
        col = lax.broadcasted_iota(jnp.int32, (tq, tq), 1)
        krows = pl.ds(pl.multiple_of(j * tq, tq), tq)

        def transposed(v):
            return v.astype(F32).T.astype(BF16)

        def wide(stat):
            return jnp.concatenate([stat] * (tq // LANES), axis=1)

        @pl.when(j == 0)
        def _():
            if fox:
                dq_ref[...] = jnp.zeros_like(dq_ref)
            else:
                dqn_ref[...] = jnp.zeros_like(dqn_ref)
                dqr_ref[...] = jnp.zeros_like(dqr_ref)
            for ii in range(nq):
                rws = slice(ii * tq, (ii + 1) * tq)
                deltas = []
                if fox:
                    prod = do_ref[0, rws, :].astype(F32) * o_ref[0, rws, :].astype(F32)
                    for e in range(2):
                        deltas.append(jnp.sum(prod[:, e * half:(e + 1) * half], axis=1, keepdims=True))
                    qt_scr[ii] = transposed(q_ref[0, rws, :])
                    dot_scr[ii] = transposed(do_ref[0, rws, :])
                else:
                    for e in range(2):
                        prod = do_ref[e, rws, :].astype(F32) * o_ref[e, rws, :].astype(F32)
                        deltas.append(jnp.sum(prod, axis=1, keepdims=True))
                        qt_scr[e, ii] = transposed(qn_ref[e, rws, :])
                        dot_scr[e, ii] = transposed(do_ref[e, rws, :])
                    qrt_scr[ii] = transposed(qr_ref[0, rws, :])
                for e in range(2):
                    delta_scr[e, rws, :] = jnp.broadcast_to(deltas[e], (tq, LANES))

        if fox:
            dfk_ref[...] = jnp.zeros_like(dfk_ref)
        else:
            @pl.when(jnp.logical_and(g == 0, j == 0))
            def _():
                dkr_ref[...] = jnp.zeros_like(dkr_ref)

        heads = []
        for e in range(2):
            sl = slice(e * half, (e + 1) * half)
            if fox:
                heads.append((sl, k_ref[0, :, sl], v_ref[0, :, sl], fk_ref[0, 0, e:e + 1, :]))
            else:
                heads.append((sl, kn_ref[e], v_ref[e], kr_ref[krows, 0:half]))
        dk_w = dv_w = half if fox else LANES

        def step(i, carry, masked):
            rows = pl.ds(pl.multiple_of(i * tq, tq), tq)
            new = []
            for e, (sl, k_e, v_e, x_e) in enumerate(heads):
                dk_acc, dv_acc, last = carry[e]
                if fox:
                    do_i = do_ref[0, rows, sl]
                    sc = _nt(q_ref[0, rows, sl], k_e) * scale + wide(fq_ref[e, rows, :]) - x_e
                else:
                    do_i = do_ref[e, rows, :]
                    sc = (_nt(qn_ref[e, rows, :], k_e) + _nt(qr_ref[0, rows, sl], x_e)) * scale
                if masked:
                    sc = jnp.where(row >= col, sc, NEG_INF)
                p = jnp.exp(sc - wide(lse_ref[e, rows, :]))
                dp = _nt(do_i, v_e)
                ds = p * (dp - wide(delta_scr[e, rows, :]))
                dsb = (ds * scale).astype(BF16)
                if fox:
                    fsl = slice(e * half, (e + 1) * half)
                    dv_acc = dv_acc + _nn(dot_scr[i, fsl, :], p.astype(BF16))
                    dk_acc = dk_acc + _nn(qt_scr[i, fsl, :], dsb)
                    dq_ref[0, rows, sl] += _nn(dsb, k_e)
                    last = last - jnp.sum(ds, axis=0, keepdims=True)
                else:
                    dv_acc = dv_acc + _nn(dot_scr[e, i], p.astype(BF16))
                    dk_acc = dk_acc + _nn(qt_scr[e, i], dsb)
                    dqn_ref[e, rows, :] += _nn(dsb, k_e)
                    dqr_ref[0, rows, sl] += _nn(dsb, x_e)
                    last = last + _nn(qrt_scr[i, e * half:(e + 1) * half, :], dsb)
                new.append((dk_acc, dv_acc, last))
            return tuple(new)

        last0 = jnp.zeros((1, tq), F32) if fox else jnp.zeros((half, tq), F32)
        init = (jnp.zeros((dk_w, tq), F32), jnp.zeros((dv_w, tq), F32), last0)
        carry = step(j, (init, init), True)
        carry = lax.fori_loop(j + 1, nq, lambda i, c: step(i, c, False), carry)
        if fox:
            for e in range(2):
                dfk_ref[0, 0, e:e + 1, :] = carry[e][2]
            dk_ref[0] = jnp.concatenate([carry[0][0], carry[1][0]], axis=0).T.astype(BF16)
            dv_ref[0] = jnp.concatenate([carry[0][1], carry[1][1]], axis=0).T.astype(BF16)
        else:
            for e in range(2):
                dkn_ref[e] = carry[e][0].T.astype(BF16)
                dv_ref[e] = carry[e][1].T.astype(BF16)
            dkr_t = carry[0][2] + carry[1][2]
            dkr_ref[krows, :] += jnp.concatenate([dkr_t, jnp.zeros_like(dkr_t)], axis=0).T

    def whole(b, g, j):
        return (g, b, 0)

    def kblk(b, g, j):
        return (g, b * nq + j, 0)

    if fox:
        in_specs = [pl.BlockSpec((1, s, LANES), whole),
                    pl.BlockSpec((1, tq, LANES), lambda b, g, j: (n_pair + g, b * nq + j, 0)),
                    pl.BlockSpec((1, tq, LANES), lambda b, g, j: (2 * n_pair + g, b * nq + j, 0)),
                    pl.BlockSpec((2, s, LANES), whole),
                    pl.BlockSpec((1, 1, 8, tq), lambda b, g, j: (b * n_pair + g, j, 0, 0)),
                    pl.BlockSpec((1, s, LANES), whole), pl.BlockSpec((1, s, LANES), whole),
                    pl.BlockSpec((2, s, LANES), whole)]
        args = [qkv, qkv, qkv, fq, fk, o, do, lse]
        out_shape = [jax.ShapeDtypeStruct((8, t, LANES), F32), jax.ShapeDtypeStruct((8, t, LANES), BF16),
                     jax.ShapeDtypeStruct((8, t, LANES), BF16), jax.ShapeDtypeStruct(fk.shape, F32)]
        out_specs = [pl.BlockSpec((1, s, LANES), whole), pl.BlockSpec((1, tq, LANES), kblk),
                     pl.BlockSpec((1, tq, LANES), kblk),
                     pl.BlockSpec((1, 1, 8, tq), lambda b, g, j: (b * n_pair + g, j, 0, 0))]
    else:
        pair = pl.BlockSpec((2, s, LANES), whole)
        pair_k = pl.BlockSpec((2, tq, LANES), kblk)
        in_specs = [pair, pl.BlockSpec((1, s, LANES), whole), pair_k,
                    pl.BlockSpec((s, LANES), lambda b, g, j: (b, 0)), pair_k,
                    pair, pair, pair]
        args = [qn, qr, kn, kr, v, o, do, lse]
        out_shape = [jax.ShapeDtypeStruct((8, t, LANES), F32), jax.ShapeDtypeStruct((4, t, LANES), F32),
                     jax.ShapeDtypeStruct((8, t, LANES), BF16), jax.ShapeDtypeStruct((8, t, LANES), BF16),
                     jax.ShapeDtypeStruct((t, LANES), F32)]
        out_specs = [pair, pl.BlockSpec((1, s, LANES), whole), pair_k, pair_k,
                     pl.BlockSpec((s, LANES), lambda b, g, j: (b, 0))]
    t_blocks = pltpu.VMEM((nq, LANES, tq), BF16)
    t_pairs = pltpu.VMEM((2, nq, LANES, tq), BF16)
    scratch = [pltpu.VMEM((2, s, LANES), F32)] + ([t_blocks, t_blocks] if fox else [t_pairs, t_blocks, t_pairs])
    return pl.pallas_call(
        body, name=name, out_shape=out_shape, grid=(bl, n_pair, nq), in_specs=in_specs, out_specs=out_specs,
        scratch_shapes=scratch, compiler_params=_params("arbitrary", "arbitrary", "arbitrary"),
    )(*args)


def adamw(w, g, m, v, name):
    shape = w.shape
    c = shape[-1]
    r = w.size // c
    tr = _tile(r, 512, 8)

    def body(w_ref, g_ref, m_ref, v_ref, d_ref, nm_ref, nv_ref):
        gv = g_ref[...]
        m2 = ADAM_B1 * m_ref[...] + (1.0 - ADAM_B1) * gv
        v2 = ADAM_B2 * v_ref[...] + (1.0 - ADAM_B2) * (gv * gv)
        m_hat = m2 / (1.0 - ADAM_B1 ** ADAM_STEP)
        v_hat = v2 / (1.0 - ADAM_B2 ** ADAM_STEP)
        d_ref[...] = -ADAM_LR * (m_hat / (jnp.sqrt(v_hat) + ADAM_EPS) + ADAM_WD * w_ref[...])
        nm_ref[...] = m2
        nv_ref[...] = v2

    spec = pl.BlockSpec((tr, c), lambda i: (i, 0))
    outs = pl.pallas_call(
        body, name=name, out_shape=[jax.ShapeDtypeStruct((r, c), F32)] * 3, grid=(r // tr,),
        in_specs=[spec] * 4, out_specs=[spec] * 3, compiler_params=_params("arbitrary"),
    )(*(a.reshape(r, c) for a in (w, g, m, v)))
    return tuple(a.reshape(shape) for a in outs)


PACK_COLS = 1024


def _pack_rows(a):
    return a.reshape(-1, PACK_COLS)


def kernel(x, c, positions, mla_w_in, mla_g_q, mla_w_uq, mla_g_kv, mla_w_uk, mla_w_uv, mla_w_o, fox_w_in, fox_b_f, fox_w_o, ada_w, ada_b, ffn_w_gate, ffn_w_up, ffn_w_down, ln_g, ln_b, loss_target, m_mla_w_in, m_mla_g_q, m_mla_w_uq, m_mla_g_kv, m_mla_w_uk, m_mla_w_uv, m_mla_w_o, m_fox_w_in, m_fox_b_f, m_fox_w_o, m_ada_w, m_ada_b, m_ffn_w_gate, m_ffn_w_up, m_ffn_w_down, m_ln_g, m_ln_b, v_mla_w_in, v_mla_g_q, v_mla_w_uq, v_mla_g_kv, v_mla_w_uk, v_mla_w_uv, v_mla_w_o, v_fox_w_in, v_fox_b_f, v_fox_w_o, v_ada_w, v_ada_b, v_ffn_w_gate, v_ffn_w_up, v_ffn_w_down, v_ln_g, v_ln_b):
    bl, s, d = x.shape
    t = bl * s
    ff = ffn_w_gate.shape[-1] * N_DEV
    dev = 4 * lax.axis_index("x") + 2 * lax.axis_index("y") + lax.axis_index("c")
    ada_cols = ada_w.shape[-1]
    fox_in = fox_w_in.shape[-1] * N_DEV
    mla_in = mla_w_in.shape[-1]
    mla_in_pad = mla_in + (-mla_in) % LANES

    def t_last(a):
        return jnp.swapaxes(a, -1, -2)

    local = {
        "mla_w_in": mla_w_in[0],
        "mla_w_uq": t_last(mla_w_uq[0]),
        "mla_w_uk": t_last(mla_w_uk[0]),
        "mla_w_uv": t_last(mla_w_uv[0]),
        "mla_w_o": mla_w_o[0],
        "fox_w_in": t_last(fox_w_in[0]),
        "fox_w_o": fox_w_o[0],
    }
    for i in range(DEPTH):
        local.update({f"gate{i}": t_last(ffn_w_gate[i]), f"up{i}": t_last(ffn_w_up[i]), f"down{i}": ffn_w_down[i]})
    groups = [["mla_w_in", "mla_w_uq", "mla_w_uk", "mla_w_uv", "mla_w_o"],
              ["gate0", "up0", "down0"],
              ["fox_w_in", "fox_w_o", "gate1", "up1", "down1"]]
    offsets, rows_of, slot_of, group_of = {}, {}, {}, {}
    group_rows = []
    for gi, names in enumerate(groups):
        rows = 0
        for nm in names:
            rows_of[nm] = local[nm].size // PACK_COLS
            slot_of[nm] = rows_of[nm] + (-rows_of[nm]) % 16
            offsets[nm] = rows
            group_of[nm] = gi
            rows += slot_of[nm]
        group_rows.append(rows)

    def slot(nm, rows):
        pad = [(0, 0)] * rows.ndim
        pad[-2] = (0, slot_of[nm] - rows_of[nm])
        return jnp.pad(rows, pad)

    def landing(block):
        land = lax.empty((N_DEV,) + block.shape, block.dtype)
        return lax.dynamic_update_slice(land, block[None], (dev, 0, 0))

    packed = [jnp.concatenate([slot(nm, _pack_rows(local[nm]).astype(BF16)) for nm in names], axis=0)
              for names in groups]
    gathered = [all_gather(packed[0], "gather_mla_weights"), None, None]
    gather_started = [None] + [exchange_start(packed[gi], landing(packed[gi]), f"gather_group{gi}_start", False)
                               for gi in (1, 2)]

    def full(nm, cols):
        blk = gathered[group_of[nm]][:, offsets[nm]:offsets[nm] + rows_of[nm], :]
        return blk.reshape(-1, cols)

    w_in = jnp.pad(full("mla_w_in", mla_in), ((0, 0), (0, mla_in_pad - mla_in)))
    wt_uq = full("mla_w_uq", MLA_QR).reshape(MLA_HEADS, MLA_NOPE + MLA_ROPE, MLA_QR)
    wt_uq_n = wt_uq[:, :MLA_NOPE].reshape(MLA_HEADS * MLA_NOPE, MLA_QR)
    wt_uq_r = wt_uq[:, MLA_NOPE:].reshape(MLA_HEADS * MLA_ROPE, MLA_QR)
    wt_uk = full("mla_w_uk", MLA_KVR)
    wt_uv = full("mla_w_uv", MLA_KVR)
    w_mo = full("mla_w_o", d)
    wt_gate, wt_up, w_down = [None] * DEPTH, [None] * DEPTH, [None] * DEPTH

    def arrive(gi, after):
        gathered[gi] = exchange_wait(gather_started[gi], after, f"gather_group{gi}_wait", False)
        for i in range(DEPTH):
            if group_of[f"gate{i}"] == gi:
                wt_gate[i], wt_up[i], w_down[i] = full(f"gate{i}", d), full(f"up{i}", d), full(f"down{i}", d)

    small = jnp.concatenate([c.reshape(-1, LANES), ln_g.reshape(-1, LANES), ln_b.reshape(-1, LANES)], axis=0)
    small_rows = small.shape[0]
    small = jnp.pad(small, ((0, (-small_rows) % 8), (0, 0)))
    for gi in (1, 2):
        small = after_token(small, gather_started[gi])
    small_all = all_gather(small, "gather_small")
    c_rows = bl * d // LANES
    c_all = small_all[:, :c_rows].reshape(N_DEV * bl, d)
    n_ln = DEPTH * 2
    ln_g_all = small_all[:, c_rows:c_rows + n_ln, :].transpose(1, 0, 2).reshape(DEPTH, 2, 1, d)
    ln_b_all = small_all[:, c_rows + n_ln:c_rows + 2 * n_ln, :].transpose(1, 0, 2).reshape(DEPTH, 2, 1, d)

    c_act = silu_rows(c_all, "silu_c")
    ada_b_loc = lax.dynamic_slice_in_dim(ada_b, dev * ada_cols, ada_cols, axis=1)
    mod_cols = [mm([(c_act, ada_w[i])], trans_b=False, out_dtype=F32, name=f"ada_fwd{i}", bias=ada_b_loc[i][None, :])
                for i in range(DEPTH)]
    mod_all = all_gather(jnp.concatenate(mod_cols, axis=0), "gather_mod")
    mod_all = mod_all.reshape(N_DEV, DEPTH, N_DEV * bl, ada_cols).transpose(1, 2, 0, 3).reshape(DEPTH, N_DEV * bl, 6 * d)
    mod_mine = lax.dynamic_slice_in_dim(mod_all, dev * bl, bl, axis=1)
    mods = [mod_mine[i].reshape(bl * 6, 1, d) for i in range(DEPTH)]

    half_r = MLA_ROPE // 2
    inv_freq = ROPE_THETA ** (-jnp.arange(half_r, dtype=F32) / half_r)
    inv_freq = jnp.tile(inv_freq, LANES // half_r)[None, :]
    sign = jnp.tile(jnp.concatenate([-jnp.ones((half_r,), F32), jnp.ones((half_r,), F32)]), LANES // MLA_ROPE)[None, :]
    cos_t, sin_t = rope_tables(positions.astype(F32).reshape(t, 1), inv_freq, sign, "rope_tables")

    x2d = x.reshape(t, d)
    g_q, g_kv = mla_g_q.reshape(1, MLA_QR), mla_g_kv.reshape(1, MLA_KVR)
    b_f = jnp.pad(fox_b_f.reshape(1, FOX_HEADS), ((0, 0), (0, LANES - FOX_HEADS)))
    mla_scale = (MLA_NOPE + MLA_ROPE) ** -0.5
    fox_scale = FOX_HD ** -0.5
    tq = _attn_tiles(s)
    nk = s // tq

    saved = []
    u = modulate(x2d, mods[0], 0, 1, bl, "modulate0")
    xin = x2d
    for i in range(DEPTH):
        sv = {"u": u, "x_in": xin}
        if i % 2 == 0:
            h_in = mm([(u, w_in)], trans_b=False, out_dtype=F32, name=f"mla_in{i}")
            c_q, c_kv, k_r = mla_latents_forward(h_in, g_q, g_kv, cos_t, sin_t, f"mla_latents{i}")
            q_n = mm([(c_q, wt_uq_n)], trans_b=True, out_dtype=BF16, out_slab=True, name=f"mla_qn{i}")
            q_r_raw = mm([(c_q, wt_uq_r)], trans_b=True, out_dtype=F32, out_slab=True, name=f"mla_qr{i}")
            q_r = rope_slabs(q_r_raw, cos_t, sin_t, BF16, f"mla_qrope{i}")
            k_n = mm([(c_kv, wt_uk)], trans_b=True, out_dtype=BF16, out_slab=True, name=f"mla_kn{i}")
            v_m = mm([(c_kv, wt_uv)], trans_b=True, out_dtype=BF16, out_slab=True, name=f"mla_v{i}")
            ops = (q_n, q_r, k_n, k_r, v_m)
            o, lse, o_delta = attention_forward("mla", ops, bl, mla_scale, f"mla_attn{i}")
            y = mm([(o, w_mo)], trans_b=False, out_dtype=F32, name=f"mla_out{i}")
            sv.update(h_in=h_in, c_q=c_q, c_kv=c_kv, ops=ops, o=o, lse=lse, o_delta=o_delta)
        else:
            arrive(2, u)
            wt_fox = full("fox_w_in", d)
            wt_qkv = wt_fox[:3 * d]
            wt_f = jnp.pad(wt_fox[3 * d:], ((0, LANES - FOX_HEADS), (0, 0)))
            w_fo = full("fox_w_o", d)
            qkv = mm([(u, wt_qkv)], trans_b=True, out_dtype=BF16, out_slab=True, name=f"fox_qkv{i}")
            z = mm([(u, wt_f)], trans_b=True, out_dtype=F32, name=f"fox_z{i}")
            f_tok, f_q = fox_gate_forward(z, b_f, bl, f"fox_gate{i}")
            f_k = f_tok[:, :FOX_HEADS].reshape(bl, nk, tq, FOX_HEADS // 2, 2).transpose(0, 3, 1, 4, 2)
            f_k = jnp.pad(f_k.reshape(bl * FOX_HEADS // 2, nk, 2, tq), ((0, 0), (0, 0), (0, 6), (0, 0)))
            ops = (qkv, f_q, f_k)
            o, lse, o_delta = attention_forward("fox", ops, bl, fox_scale, f"fox_attn{i}")
            y = mm([(o, w_fo)], trans_b=False, out_dtype=F32, name=f"fox_out{i}")
            sv.update(z=z, ops=ops, o=o, lse=lse, o_delta=o_delta)
        x1, r1, u2 = residual_layer_norm(xin, y, mods[i], 2, ln_g_all[i, 0], ln_b_all[i, 0], bl, f"ln_mix{i}",
                                         next_mod=(3, 4))
        if wt_gate[i] is None:
            arrive(group_of[f"gate{i}"], u2)
        a = mm([(u2, wt_gate[i])], trans_b=True, out_dtype=F32, name=f"ffn_gate{i}")
        bb = mm([(u2, wt_up[i])], trans_b=True, out_dtype=F32, name=f"ffn_up{i}")
        h = swiglu_forward(a, bb, f"swiglu{i}")
        y2 = mm([(h, w_down[i])], trans_b=False, out_dtype=F32, name=f"ffn_down{i}")
        sv.update(y=y, r1=r1, u2=u2, a=a, bb=bb, h=h, y2=y2)
        if i + 1 < DEPTH:
            xin, r2, u = residual_layer_norm(x1, y2, mods[i], 5, ln_g_all[i, 1], ln_b_all[i, 1], bl, f"ln_ffn{i}",
                                             next_mod=(0, 1, mods[i + 1]))
        else:
            xin, r2 = residual_layer_norm(x1, y2, mods[i], 5, ln_g_all[i, 1], ln_b_all[i, 1], bl, f"ln_ffn{i}")
        sv.update(r2=r2)
        saved.append(sv)

    loss_cols, d_x = loss_head(xin, loss_target.reshape(t, d), "loss_head")

    grads_full = {}
    dmod = [[None] * 6 for _ in range(DEPTH)]
    dg_ln = [[None, None] for _ in range(DEPTH)]
    db_ln = [[None, None] for _ in range(DEPTH)]
    dg_q = dg_kv = db_f = None
    d_a, du = d_x, None
    scatter_started = [None] * len(groups)

    def scatter_start(gi):
        g = jnp.concatenate(
            [slot(nm, grads_full[nm].reshape(N_DEV, rows_of[nm], PACK_COLS).astype(BF16)) for nm in groups[gi]], axis=1)
        own = lax.dynamic_index_in_dim(g, dev, 0, keepdims=False)
        scatter_started[gi] = exchange_start(g, landing(own), f"scatter_group{gi}_start", True)

    ln_g_bwd = [[ln_g_all[i, k] for k in range(2)] for i in range(DEPTH)]
    for i in reversed(range(DEPTH)):
        sv = saved[i]
        if i + 1 < DEPTH:
            scatter_start(2)
            ln_g_bwd[i][1] = after_token(ln_g_bwd[i][1], scatter_started[2])
        ln2 = (sv["r2"], sv["y2"], ln_g_bwd[i][1], ln_b_all[i, 1], (mods[i], 5))
        if du is None:
            bw = sublayer_backward(d_a, bl, f"bwd_ln_ffn{i}", ln=ln2)
        else:
            bw = sublayer_backward(d_a, bl, f"bwd_ln_ffn{i}", du=du, scale=(mods[i + 1], 1), ln=ln2)
            dmod[i + 1][0], dmod[i + 1][1] = bw["dshift"], bw["dscale"]
        dmod[i][5], dg_ln[i][1], db_ln[i][1] = bw["dgate"], bw["dg"], bw["db"]
        dy2 = bw["dy"]
        dh = mm([(dy2, w_down[i])], trans_b=True, out_dtype=F32, name=f"bwd_ffn_dh{i}")
        da, dbb = swiglu_backward(dh, sv["a"], sv["bb"], f"bwd_swiglu{i}")
        du2 = mm([(da, wt_gate[i]), (dbb, wt_up[i])], trans_b=False, out_dtype=F32, name=f"bwd_ffn_du{i}")
        grads_full[f"down{i}"] = mm_tn(sv["h"], dy2, name=f"bwd_w_down{i}")
        grads_full[f"gate{i}"] = mm_tn(da, sv["u2"], name=f"bwd_w_gate{i}")
        grads_full[f"up{i}"] = mm_tn(dbb, sv["u2"], name=f"bwd_w_up{i}")
        if i == 0:
            scatter_start(1)
            ln_g_bwd[i][0] = after_token(ln_g_bwd[i][0], scatter_started[1])
        bw = sublayer_backward(bw["dx"], bl, f"bwd_ln_mix{i}", du=du2, scale=(mods[i], 4),
                               ln=(sv["r1"], sv["y"], ln_g_bwd[i][0], ln_b_all[i, 0], (mods[i], 2)))
        dmod[i][3], dmod[i][4], dmod[i][2] = bw["dshift"], bw["dscale"], bw["dgate"]
        dg_ln[i][0], db_ln[i][0] = bw["dg"], bw["db"]
        d_a, dy = bw["dx"], bw["dy"]
        o, lse, ops = sv["o"], sv["lse"], sv["ops"]
        if i % 2 == 0:
            do = mm([(dy, w_mo)], trans_b=True, out_dtype=BF16, out_slab=True, name=f"bwd_mla_do{i}")
            grads_full["mla_w_o"] = mm_tn(o, dy, name=f"bwd_w_mla_o{i}")
            dqn, dqr, dkn, dvm, dkr = attention_backward("mla", ops, sv["o_delta"], do, lse, bl, mla_scale,
                                                         f"bwd_mla_attn{i}")
            dqr = rope_slabs(dqr, cos_t, sin_t, F32, f"bwd_mla_qrope{i}", transposed=True)
            dcq = mm([(dqn, wt_uq_n), (dqr, wt_uq_r)], trans_b=False, out_dtype=F32, name=f"bwd_mla_dcq{i}")
            dckv = mm([(dkn, wt_uk), (dvm, wt_uv)], trans_b=False, out_dtype=F32, name=f"bwd_mla_dckv{i}")
            d_uq_n = mm_tn(dqn, sv["c_q"], name=f"bwd_w_uq_n{i}").reshape(MLA_HEADS, MLA_NOPE, MLA_QR)
            d_uq_r = mm_tn(dqr, sv["c_q"], name=f"bwd_w_uq_r{i}").reshape(MLA_HEADS, MLA_ROPE, MLA_QR)
            grads_full["mla_w_uq"] = jnp.concatenate([d_uq_n, d_uq_r], axis=1)
            grads_full["mla_w_uk"] = mm_tn(dkn, sv["c_kv"], name=f"bwd_w_uk{i}")
            grads_full["mla_w_uv"] = mm_tn(dvm, sv["c_kv"], name=f"bwd_w_uv{i}")
            dh_in, dg_q, dg_kv = mla_latents_backward(sv["h_in"], dcq, dckv, dkr, g_q, g_kv, cos_t, sin_t,
                                                      f"bwd_mla_latents{i}")
            du = mm([(dh_in, w_in)], trans_b=True, out_dtype=F32, name=f"bwd_mla_du{i}")
            grads_full["mla_w_in"] = mm_tn(sv["u"], dh_in, name=f"bwd_w_mla_in{i}")[:, :mla_in]
        else:
            do = mm([(dy, w_fo)], trans_b=True, out_dtype=BF16, out_slab=True, name=f"bwd_fox_do{i}")
            grads_full["fox_w_o"] = mm_tn(o, dy, name=f"bwd_w_fox_o{i}")
            dq, dk, dvf, dfk = attention_backward("fox", ops, sv["o_delta"], do, lse, bl, fox_scale, f"bwd_fox_attn{i}")
            df = dfk[:, :, :2, :].reshape(bl, FOX_HEADS // 2, nk, 2, tq).transpose(0, 2, 4, 1, 3).reshape(t, FOX_HEADS)
            df = jnp.pad(df, ((0, 0), (0, LANES - FOX_HEADS)))
            dz, db_f = fox_gate_backward(sv["z"], b_f, df, bl, f"bwd_fox_gate{i}")
            du = mm([(dq, wt_fox[0:d]), (dk, wt_fox[d:2 * d]), (dvf, wt_fox[2 * d:3 * d]), (dz, wt_f)],
                    trans_b=False, out_dtype=F32, name=f"bwd_fox_du{i}")
            u_f = sv["u"]
            grads_full["fox_w_in"] = jnp.concatenate(
                [mm_tn(dq, u_f, name=f"bwd_w_fox_q{i}"), mm_tn(dk, u_f, name=f"bwd_w_fox_k{i}"),
                 mm_tn(dvf, u_f, name=f"bwd_w_fox_v{i}"), mm_tn(dz, u_f, name=f"bwd_w_fox_f{i}")[:FOX_HEADS]], axis=0)
    bw = sublayer_backward(d_a, bl, "bwd_input", du=du, scale=(mods[0], 1), x_in=x2d)
    dmod[0][0], dmod[0][1] = bw["dshift"], bw["dscale"]
    grad_x = bw["dx"].reshape(bl, s, d)

    dmod_rows = jnp.concatenate([r.reshape(bl, d) for layer in dmod for r in layer], axis=0)
    dmod_rows = dmod_rows.reshape(DEPTH, 6, bl, d).transpose(0, 2, 1, 3)
    n_mod = dmod_rows.size // LANES
    ln_parts = [dg_ln[i][k] for i in range(DEPTH) for k in range(2)] + [db_ln[i][k] for i in range(DEPTH) for k in range(2)]
    small_g = jnp.concatenate([dmod_rows.reshape(-1, LANES), dg_q.reshape(-1, LANES), dg_kv.reshape(-1, LANES), db_f]
                              + [p.reshape(-1, LANES) for p in ln_parts] + [loss_cols.reshape(-1, LANES)], axis=0)
    n_small = small_g.shape[0]
    small_g = jnp.pad(small_g, ((0, (-n_small) % 8), (0, 0)))
    small_g_all = all_gather(small_g, "gather_small_grads")
    small_sum = sum_leading(small_g_all, "sum_small_grads")
    per_seq = DEPTH * 6 * d // LANES
    dmod_all = small_g_all[:, :n_mod].reshape(N_DEV, DEPTH, bl, 6 * d).transpose(1, 0, 2, 3)
    dmod_all = dmod_all.reshape(DEPTH, N_DEV * bl, 6 * d)
    o1 = n_mod
    grad_g_q = small_sum[o1:o1 + MLA_QR // LANES].reshape(1, MLA_QR)
    o1 += MLA_QR // LANES
    grad_g_kv = small_sum[o1:o1 + MLA_KVR // LANES].reshape(1, MLA_KVR)
    o1 += MLA_KVR // LANES
    grad_b_f = small_sum[o1:o1 + 1, :FOX_HEADS]
    o1 += 1
    n_ln_rows = DEPTH * 2 * d // LANES
    grad_ln_g_full = small_sum[o1:o1 + n_ln_rows].reshape(DEPTH, 2, d)
    grad_ln_b_full = small_sum[o1 + n_ln_rows:o1 + 2 * n_ln_rows].reshape(DEPTH, 2, d)
    loss = jnp.sum(small_sum[o1 + 2 * n_ln_rows:o1 + 2 * n_ln_rows + d // LANES])
    shard = d // N_DEV
    grad_ln_g = lax.dynamic_slice_in_dim(grad_ln_g_full, dev * shard, shard, axis=2)
    grad_ln_b = lax.dynamic_slice_in_dim(grad_ln_b_full, dev * shard, shard, axis=2)
    by_seq = small_g_all[:, :n_mod].reshape(N_DEV, DEPTH, bl, 6 * d // LANES, LANES).transpose(0, 2, 1, 3, 4)
    grad_ada_b = sum_leading(by_seq.reshape(N_DEV * bl, per_seq, LANES), "sum_ada_b").reshape(DEPTH, 6 * d)
    dmod_cols = lax.dynamic_slice_in_dim(dmod_all, dev * ada_cols, ada_cols, axis=2)
    grad_ada_w = jnp.stack([mm_tn(c_act, dmod_cols[i], name=f"bwd_w_ada{i}") for i in range(DEPTH)])

    scatter_start(0)
    after = bw["dx"]
    g_mine = [None] * len(groups)
    for gi in (2, 1, 0):
        landed = exchange_wait(scatter_started[gi], after, f"scatter_group{gi}_wait", True)
        g_mine[gi] = sum_leading(landed, f"scatter_group{gi}_sum")
        after = g_mine[gi]

    def mine(nm, shape):
        return g_mine[group_of[nm]][offsets[nm]:offsets[nm] + rows_of[nm]].reshape(shape)

    def shard_t(nm, a):
        return t_last(mine(nm, t_last(a).shape))

    grads = {
        "mla_w_in": mine("mla_w_in", mla_w_in[0].shape)[None],
        "mla_g_q": grad_g_q,
        "mla_w_uq": shard_t("mla_w_uq", mla_w_uq[0])[None],
        "mla_g_kv": grad_g_kv,
        "mla_w_uk": shard_t("mla_w_uk", mla_w_uk[0])[None],
        "mla_w_uv": shard_t("mla_w_uv", mla_w_uv[0])[None],
        "mla_w_o": mine("mla_w_o", mla_w_o[0].shape)[None],
        "fox_w_in": shard_t("fox_w_in", fox_w_in[0])[None],
        "fox_b_f": grad_b_f,
        "fox_w_o": mine("fox_w_o", fox_w_o[0].shape)[None],
        "ada_w": grad_ada_w,
        "ada_b": grad_ada_b,
        "ffn_w_gate": jnp.stack([shard_t(f"gate{i}", ffn_w_gate[i]) for i in range(DEPTH)]),
        "ffn_w_up": jnp.stack([shard_t(f"up{i}", ffn_w_up[i]) for i in range(DEPTH)]),
        "ffn_w_down": jnp.stack([mine(f"down{i}", ffn_w_down[i].shape) for i in range(DEPTH)]),
        "ln_g": grad_ln_g,
        "ln_b": grad_ln_b,
    }
    weights = dict(mla_w_in=mla_w_in, mla_g_q=mla_g_q, mla_w_uq=mla_w_uq, mla_g_kv=mla_g_kv, mla_w_uk=mla_w_uk,
                   mla_w_uv=mla_w_uv, mla_w_o=mla_w_o, fox_w_in=fox_w_in, fox_b_f=fox_b_f, fox_w_o=fox_w_o,
                   ada_w=ada_w, ada_b=ada_b, ffn_w_gate=ffn_w_gate, ffn_w_up=ffn_w_up, ffn_w_down=ffn_w_down,
                   ln_g=ln_g, ln_b=ln_b)
    first = dict(mla_w_in=m_mla_w_in, mla_g_q=m_mla_g_q, mla_w_uq=m_mla_w_uq, mla_g_kv=m_mla_g_kv, mla_w_uk=m_mla_w_uk,
                 mla_w_uv=m_mla_w_uv, mla_w_o=m_mla_w_o, fox_w_in=m_fox_w_in, fox_b_f=m_fox_b_f, fox_w_o=m_fox_w_o,
                 ada_w=m_ada_w, ada_b=m_ada_b, ffn_w_gate=m_ffn_w_gate, ffn_w_up=m_ffn_w_up, ffn_w_down=m_ffn_w_down,
                 ln_g=m_ln_g, ln_b=m_ln_b)
    second = dict(mla_w_in=v_mla_w_in, mla_g_q=v_mla_g_q, mla_w_uq=v_mla_w_uq, mla_g_kv=v_mla_g_kv, mla_w_uk=v_mla_w_uk,
                  mla_w_uv=v_mla_w_uv, mla_w_o=v_mla_w_o, fox_w_in=v_fox_w_in, fox_b_f=v_fox_b_f, fox_w_o=v_fox_w_o,
                  ada_w=v_ada_w, ada_b=v_ada_b, ffn_w_gate=v_ffn_w_gate, ffn_w_up=v_ffn_w_up, ffn_w_down=v_ffn_w_down,
                  ln_g=v_ln_g, ln_b=v_ln_b)
    order = list(weights)
    g_out, d_out, m_out, v_out = [], [], [], []
    for nm in order:
        g = grads[nm].reshape(weights[nm].shape)
        delta, new_m, new_v = adamw(weights[nm], g, first[nm], second[nm], f"adamw_{nm}")
        g_out.append(g)
        d_out.append(delta)
        m_out.append(new_m)
        v_out.append(new_v)
    return (loss, grad_x, *g_out, *d_out, *m_out, *v_out)
```

```python
import functools

import jax
import jax.numpy as jnp
from jax import lax
from jax.experimental import pallas as pl
from jax.experimental.pallas import tpu as pltpu

F32 = jnp.float32
BF16 = jnp.bfloat16
LANES = 128
N_DEV = 8
VMEM_LIMIT_BYTES = 56 * 1024 * 1024

DEPTH = 2
MLA_HEADS = 8
MLA_NOPE = 128
MLA_ROPE = 64
MLA_V = 128
MLA_QR = 256
MLA_KVR = 256
ROPE_THETA = 10000.0
FOX_HEADS = 16
FOX_HD = 64
ALPHA = (2.0 * DEPTH) ** 0.25
NORM_EPS = 1e-5
ADAM_LR = 0.001
ADAM_B1 = 0.9
ADAM_B2 = 0.999
ADAM_EPS = 1e-08
ADAM_WD = 0.01
ADAM_STEP = 10

MESH_AXES = ("x", "y", "c")
MESH = pl.DeviceIdType.MESH


def _params(*sem):
    return pltpu.CompilerParams(dimension_semantics=sem, vmem_limit_bytes=VMEM_LIMIT_BYTES)


def _tile(n, cap, mult=LANES):
    if n <= cap:
        return n
    best = None
    for t in range(mult, cap + 1, mult):
        if n % t == 0:
            best = t
    assert best is not None, (n, cap, mult)
    return best


def _dot(a, b, dims):
    return lax.dot_general(a, b, (dims, ((), ())), preferred_element_type=F32)


def _nn(a, b):
    return _dot(a, b, ((1,), (0,)))


def _nt(a, b):
    return _dot(a, b, ((1,), (1,)))


def _tn(a, b):
    return _dot(a, b, ((0,), (0,)))


def _me():
    return lax.axis_index("x"), lax.axis_index("y"), lax.axis_index("c")


def all_gather(x_loc, name):
    r, c = x_loc.shape

    def body(x_ref, out_ref, send_sems, recv_sems, local_sem):
        x, y, cc = _me()
        me, sibling = (x, y, cc), (x, y, 1 - cc)
        chips = [(1 - x, y), (x, 1 - y), (1 - x, 1 - y)]

        def rows(px, py, pc):
            return out_ref.at[4 * px + 2 * py + pc]

        def copy(k, block, to, src=None):
            return pltpu.make_async_remote_copy(
                src_ref=rows(*block) if src is None else src, dst_ref=rows(*block),
                send_sem=send_sems.at[k], recv_sem=recv_sems.at[k], device_id=to, device_id_type=MESH)

        mine = pltpu.make_async_copy(x_ref, rows(*me), local_sem)
        mine.start()
        first = [copy(0, me, sibling, src=x_ref)]
        first += [copy(1 + j, me, (*chip, cc), src=x_ref) for j, chip in enumerate(chips)]
        for cp in first:
            cp.start()
        passed = [copy(4 + j, (*chip, cc), sibling) for j, chip in enumerate(chips)]
        for j, chip in enumerate(chips):
            copy(1 + j, (*chip, cc), me).wait_recv()
            passed[j].start()
        copy(0, sibling, me).wait_recv()
        for j, chip in enumerate(chips):
            copy(4 + j, (*chip, 1 - cc), me).wait_recv()
        for cp in first + passed:
            cp.wait_send()
        mine.wait()

    return pl.pallas_call(
        body, name=name,
        out_shape=jax.ShapeDtypeStruct((N_DEV, r, c), x_loc.dtype),
        in_specs=[pl.BlockSpec(memory_space=pl.ANY)],
        out_specs=pl.BlockSpec(memory_space=pl.ANY),
        scratch_shapes=[pltpu.SemaphoreType.DMA((7,)), pltpu.SemaphoreType.DMA((7,)), pltpu.SemaphoreType.DMA(())],
    )(x_loc)


HBM_SPEC = pl.BlockSpec(memory_space=pltpu.HBM)
SEM_SPEC = pl.BlockSpec(memory_space=pltpu.SEMAPHORE)
N_PEERS = N_DEV - 1


def _peer(k):
    x, y, c = _me()
    return (1 - x if k & 4 else x, 1 - y if k & 2 else y, 1 - c if k & 1 else c)


def _exchange_copies(src_ref, land_ref, send_sems, recv_sems, scatter):
    x, y, c = _me()
    mine = 4 * x + 2 * y + c
    copies = []
    for k in range(1, N_DEV):
        px, py, pc = _peer(k)
        src = src_ref.at[4 * px + 2 * py + pc] if scatter else src_ref
        copies.append(pltpu.make_async_remote_copy(
            src_ref=src, dst_ref=land_ref.at[mine], send_sem=send_sems.at[k - 1], recv_sem=recv_sems.at[k - 1],
            device_id=(px, py, pc), device_id_type=MESH))
    return copies


def exchange_start(src, land, name, scatter):
    def body(src_ref, land_ref, send_sems, recv_sems, src_thru, land_thru, token):
        for cp in _exchange_copies(src_ref, land_ref, send_sems, recv_sems, scatter):
            cp.start()
        token[...] = jnp.zeros_like(token)

    return pl.pallas_call(
        body, name=name,
        out_shape=(pltpu.SemaphoreType.DMA((N_PEERS,)), pltpu.SemaphoreType.DMA((N_PEERS,)),
                   pltpu.HBM(src.shape, src.dtype), pltpu.HBM(land.shape, land.dtype),
                   jax.ShapeDtypeStruct((8, LANES), F32)),
        in_specs=(HBM_SPEC, HBM_SPEC),
        out_specs=(SEM_SPEC, SEM_SPEC, HBM_SPEC, HBM_SPEC, pl.BlockSpec(memory_space=pltpu.VMEM)),
        input_output_aliases={0: 2, 1: 3},
        compiler_params=pltpu.CompilerParams(has_side_effects=pltpu.SideEffectType.DATAFLOW_SIDE_EFFECTING),
    )(pltpu.with_memory_space_constraint(src, pltpu.HBM), pltpu.with_memory_space_constraint(land, pltpu.HBM))


def exchange_wait(started, after, name, scatter):
    send_sems, recv_sems, src_thru, land_thru, _ = started

    def body(src_ref, land_ref, send_sems, recv_sems, after_ref, src_dead, got_ref):
        for cp in _exchange_copies(src_ref, land_ref, send_sems, recv_sems, scatter):
            cp.wait_send()
            cp.wait_recv()

    return pl.pallas_call(
        body, name=name,
        out_shape=(pltpu.HBM(src_thru.shape, src_thru.dtype), pltpu.HBM(land_thru.shape, land_thru.dtype)),
        in_specs=(HBM_SPEC, HBM_SPEC, SEM_SPEC, SEM_SPEC, pl.BlockSpec(memory_space=pl.ANY)),
        out_specs=(HBM_SPEC, HBM_SPEC), input_output_aliases={0: 0, 1: 1},
        compiler_params=pltpu.CompilerParams(has_side_effects=pltpu.SideEffectType.DATAFLOW_SIDE_EFFECTING),
    )(src_thru, land_thru, send_sems, recv_sems, after)[1]


def after_token(small, started):
    return small + started[4][0, 0]


def sum_leading(x, name):
    n, r, c = x.shape
    tr = _tile(r, 512, 16)

    def body(x_ref, o_ref):
        acc = x_ref[0].astype(F32)
        for k in range(1, n):
            acc = acc + x_ref[k].astype(F32)
        o_ref[...] = acc

    return pl.pallas_call(
        body, name=name,
        out_shape=jax.ShapeDtypeStruct((r, c), F32),
        grid=(r // tr,),
        in_specs=[pl.BlockSpec((n, tr, c), lambda i: (0, i, 0))],
        out_specs=pl.BlockSpec((tr, c), lambda i: (i, 0)),
        compiler_params=_params("arbitrary"),
    )(x)


MM_VMEM_BUDGET = 36 * 1024 * 1024
GRID_STEP_AS_BYTES = 1 << 20


def _mm_tiles(m, n, a_row_bytes, b_col_bytes, out_bytes):
    tms = [c for c in (2048, 1024, 512, 256, 128, 64, 32, 16, 8) if m % c == 0] or [m]
    tns = [c for c in range(LANES, min(n, 2048) + 1, LANES) if n % c == 0] or [n]
    best = None
    for tm in tms:
        for tn in tns:
            vmem = 2 * (tm * a_row_bytes + tn * b_col_bytes) + 2 * tm * tn * out_bytes + tm * tn * 4
            if vmem > MM_VMEM_BUDGET:
                continue
            steps = (m // tm) * (n // tn)
            cost = steps * GRID_STEP_AS_BYTES + (m // tm) * n * b_col_bytes + m * a_row_bytes
            if best is None or cost < best[0]:
                best = (cost, tm, tn)
    assert best is not None, (m, n, a_row_bytes, b_col_bytes)
    return best[1], best[2]


def mm(pairs, *, trans_b, out_dtype, name, out_slab=False, bias=None):
    a0 = pairs[0][0]
    m = a0.shape[1] if a0.ndim == 3 else a0.shape[0]
    n = pairs[0][1].shape[0] if trans_b else pairs[0][1].shape[1]
    a_row_bytes = sum((b.shape[1] if trans_b else b.shape[0]) * a.dtype.itemsize for a, b in pairs)
    b_col_bytes = sum((b.shape[1] if trans_b else b.shape[0]) * b.dtype.itemsize for _, b in pairs)
    tm, tn = _mm_tiles(m, n, a_row_bytes, b_col_bytes, jnp.dtype(out_dtype).itemsize)
    slabs = [a.ndim == 3 for a, _ in pairs]
    n_pairs = len(pairs)

    def body(*refs):
        o_ref = refs[-1]
        acc = bias_ref = None
        if bias is not None:
            bias_ref = refs[2 * n_pairs]
        for i in range(n_pairs):
            a_ref, b_ref = refs[2 * i], refs[2 * i + 1]
            if slabs[i]:
                a = jnp.concatenate([a_ref[s].astype(BF16) for s in range(a_ref.shape[0])], axis=1)
            else:
                a = a_ref[...].astype(BF16)
            b = b_ref[...].astype(BF16)
            part = _nt(a, b) if trans_b else _nn(a, b)
            acc = part if acc is None else acc + part
        if bias_ref is not None:
            acc = acc + bias_ref[...]
        if out_slab:
            for s in range(tn // LANES):
                o_ref[s] = acc[:, s * LANES:(s + 1) * LANES].astype(out_dtype)
        else:
            o_ref[...] = acc.astype(out_dtype)

    in_specs, args = [], []
    for (a, b), slab in zip(pairs, slabs):
        if slab:
            in_specs.append(pl.BlockSpec((a.shape[0], tm, LANES), lambda i, j: (0, i, 0)))
        else:
            in_specs.append(pl.BlockSpec((tm, a.shape[1]), lambda i, j: (i, 0)))
        if trans_b:
            in_specs.append(pl.BlockSpec((tn, b.shape[1]), lambda i, j: (j, 0)))
        else:
            in_specs.append(pl.BlockSpec((b.shape[0], tn), lambda i, j: (0, j)))
        args += [a, b]
    if bias is not None:
        in_specs.append(pl.BlockSpec((1, tn), lambda i, j: (0, j)))
        args.append(bias)
    if out_slab:
        out_shape = jax.ShapeDtypeStruct((n // LANES, m, LANES), out_dtype)
        out_spec = pl.BlockSpec((tn // LANES, tm, LANES), lambda i, j: (j, i, 0))
    else:
        out_shape = jax.ShapeDtypeStruct((m, n), out_dtype)
        out_spec = pl.BlockSpec((tm, tn), lambda i, j: (i, j))
    return pl.pallas_call(
        body, name=name, out_shape=out_shape, grid=(m // tm, n // tn),
        in_specs=in_specs, out_specs=out_spec,
        compiler_params=_params("arbitrary", "arbitrary"),
    )(*args)


def mm_tn(a, b, *, name, tk_cap=1536, tn_cap=1024, tm_cap=512):
    slab = a.ndim == 3
    m = a.shape[1] if slab else a.shape[0]
    k = a.shape[0] * LANES if slab else a.shape[1]
    n = b.shape[1]
    tk = _tile(k, tk_cap)
    tn = _tile(n, tn_cap)
    tm = _tile(m, tm_cap, 8)

    def body(a_ref, b_ref, o_ref):
        @pl.when(pl.program_id(2) == 0)
        def _():
            o_ref[...] = jnp.zeros_like(o_ref)

        bb = b_ref[...].astype(BF16)
        if slab:
            for s in range(tk // LANES):
                o_ref[s * LANES:(s + 1) * LANES, :] += _tn(a_ref[s].astype(BF16), bb)
        else:
            o_ref[...] += _tn(a_ref[...].astype(BF16), bb)

    if slab:
        a_spec = pl.BlockSpec((tk // LANES, tm, LANES), lambda i, j, t: (i, t, 0))
    else:
        a_spec = pl.BlockSpec((tm, tk), lambda i, j, t: (t, i))
    return pl.pallas_call(
        body, name=name, out_shape=jax.ShapeDtypeStruct((k, n), F32), grid=(k // tk, n // tn, m // tm),
        in_specs=[a_spec, pl.BlockSpec((tm, tn), lambda i, j, t: (t, j))],
        out_specs=pl.BlockSpec((tk, tn), lambda i, j, t: (i, j)),
        compiler_params=_params("arbitrary", "arbitrary", "arbitrary"),
    )(a, b)


def _row_spec(d, k):
    return pl.BlockSpec((1, 1, d), lambda b, i: (6 * b + k, 0, 0))


def modulate(x, mod, k_shift, k_scale, bl, name):
    t, d = x.shape
    s = t // bl
    tm = _tile(s, 512, 8)
    nt = s // tm

    def body(x_ref, sh_ref, sc_ref, o_ref):
        o_ref[...] = (x_ref[...] * (1.0 + sc_ref[0]) + sh_ref[0]).astype(BF16)

    return pl.pallas_call(
        body, name=name, out_shape=jax.ShapeDtypeStruct((t, d), BF16), grid=(bl, nt),
        in_specs=[pl.BlockSpec((tm, d), lambda b, i: (b * nt + i, 0)), _row_spec(d, k_shift), _row_spec(d, k_scale)],
        out_specs=pl.BlockSpec((tm, d), lambda b, i: (b * nt + i, 0)),
        compiler_params=_params("arbitrary", "arbitrary"),
    )(x, mod, mod)


def _layer_norm_stats(r):
    mu = jnp.mean(r, axis=-1, keepdims=True)
    rc = r - mu
    var = jnp.mean(rc * rc, axis=-1, keepdims=True)
    rstd = lax.rsqrt(var + NORM_EPS)
    return rc * rstd, rstd


def residual_layer_norm(x, y, mod, k_gate, g, b, bl, name, next_mod=None):
    t, d = x.shape
    s = t // bl
    tm = _tile(s, 256, 8)
    nt = s // tm
    has_next = next_mod is not None

    def body(*refs):
        x_ref, y_ref, gt_ref, g_ref, b_ref = refs[:5]
        rest = refs[5:]
        if has_next:
            sh_ref, sc_ref, o_ref, r_ref, u_ref = rest
        else:
            o_ref, r_ref = rest
        r = ALPHA * x_ref[...] + (1.0 + gt_ref[0]) * y_ref[...]
        xhat, _ = _layer_norm_stats(r)
        out = xhat * g_ref[...] + b_ref[...]
        o_ref[...] = out
        r_ref[...] = r
        if has_next:
            u_ref[...] = (out * (1.0 + sc_ref[0]) + sh_ref[0]).astype(BF16)

    tok = pl.BlockSpec((tm, d), lambda bb, i: (bb * nt + i, 0))
    vec = pl.BlockSpec((1, d), lambda bb, i: (0, 0))
    in_specs = [tok, tok, _row_spec(d, k_gate), vec, vec]
    args = [x, y, mod, g, b]
    out_shape = [jax.ShapeDtypeStruct((t, d), F32), jax.ShapeDtypeStruct((t, d), F32)]
    out_specs = [tok, tok]
    if has_next:
        in_specs += [_row_spec(d, next_mod[0]), _row_spec(d, next_mod[1])]
        args += [mod if len(next_mod) == 2 else next_mod[2]] * 2
        out_shape.append(jax.ShapeDtypeStruct((t, d), BF16))
        out_specs.append(tok)
    return pl.pallas_call(
        body, name=name, out_shape=out_shape, grid=(bl, nt), in_specs=in_specs, out_specs=out_specs,
        compiler_params=_params("arbitrary", "arbitrary"),
    )(*args)


def loss_head(xo, target, name):
    t, d = xo.shape
    tm = _tile(t, 512, 8)

    def body(x_ref, t_ref, l_ref, dx_ref):
        @pl.when(pl.program_id(0) == 0)
        def _():
            l_ref[...] = jnp.zeros_like(l_ref)

        e = x_ref[...] - t_ref[...]
        l_ref[...] += jnp.sum(e * e, axis=0, keepdims=True) * (0.5 / d)
        dx_ref[...] = e * (1.0 / d)

    tok = pl.BlockSpec((tm, d), lambda i: (i, 0))
    return pl.pallas_call(
        body, name=name,
        out_shape=[jax.ShapeDtypeStruct((1, d), F32), jax.ShapeDtypeStruct((t, d), F32)],
        grid=(t // tm,), in_specs=[tok, tok],
        out_specs=[pl.BlockSpec((1, d), lambda i: (0, 0)), tok],
        compiler_params=_params("arbitrary"),
    )(xo, target)


def sublayer_backward(d_a, bl, name, *, du=None, scale=None, x_in=None, ln=None):
    t, d = d_a.shape
    s = t // bl
    tm = _tile(s, 256, 8)
    nt = s // tm
    has_mod = du is not None
    has_ln = ln is not None
    assert has_mod or has_ln
    assert has_ln or x_in is not None

    def body(*refs):
        refs = list(refs)
        da_ref = refs.pop(0)
        if has_mod:
            du_ref, sc_ref = refs.pop(0), refs.pop(0)
        if has_ln:
            r_ref, y_ref, g_ref, b_ref, gt_ref = (refs.pop(0) for _ in range(5))
        elif has_mod:
            xin_ref = refs.pop(0)
        dx_ref = refs.pop(0)
        if has_ln:
            dy_ref, dg_ref, db_ref, dgt_ref = (refs.pop(0) for _ in range(4))
        if has_mod:
            dsc_ref, dsh_ref = refs.pop(0), refs.pop(0)
        first_tile = pl.program_id(1) == 0
        first_step = jnp.logical_and(pl.program_id(0) == 0, first_tile)

        dout = da_ref[...]
        if has_ln:
            xhat, rstd = _layer_norm_stats(r_ref[...])
        if has_mod:
            duv = du_ref[...]
            dout = dout + duv * (1.0 + sc_ref[0])
            xin = xhat * g_ref[...] + b_ref[...] if has_ln else xin_ref[...]

            @pl.when(first_tile)
            def _():
                dsc_ref[...] = jnp.zeros_like(dsc_ref)
                dsh_ref[...] = jnp.zeros_like(dsh_ref)

            dsc_ref[0] += jnp.sum(duv * xin, axis=0, keepdims=True)
            dsh_ref[0] += jnp.sum(duv, axis=0, keepdims=True)
        if not has_ln:
            dx_ref[...] = dout
            return

        @pl.when(first_step)
        def _():
            dg_ref[...] = jnp.zeros_like(dg_ref)
            db_ref[...] = jnp.zeros_like(db_ref)

        @pl.when(first_tile)
        def _():
            dgt_ref[...] = jnp.zeros_like(dgt_ref)

        dg_ref[...] += jnp.sum(dout * xhat, axis=0, keepdims=True)
        db_ref[...] += jnp.sum(dout, axis=0, keepdims=True)
        dxh = dout * g_ref[...]
        dr = rstd * (dxh - jnp.mean(dxh, axis=-1, keepdims=True) - xhat * jnp.mean(dxh * xhat, axis=-1, keepdims=True))
        dx_ref[...] = ALPHA * dr
        dy_ref[...] = ((1.0 + gt_ref[0]) * dr).astype(BF16)
        dgt_ref[0] += jnp.sum(dr * y_ref[...], axis=0, keepdims=True)

    tok = pl.BlockSpec((tm, d), lambda bb, i: (bb * nt + i, 0))
    vec = pl.BlockSpec((1, d), lambda bb, i: (0, 0))
    seq = pl.BlockSpec((1, 1, d), lambda bb, i: (bb, 0, 0))
    in_specs, args = [tok], [d_a]
    if has_mod:
        in_specs += [tok, _row_spec(d, scale[1])]
        args += [du, scale[0]]
    if has_ln:
        r, y, g, b, gate = ln
        in_specs += [tok, tok, vec, vec, _row_spec(d, gate[1])]
        args += [r, y, g, b, gate[0]]
    elif has_mod:
        in_specs.append(tok)
        args.append(x_in)
    names = ["dx"]
    out_shape, out_specs = [jax.ShapeDtypeStruct((t, d), F32)], [tok]
    if has_ln:
        names += ["dy", "dg", "db", "dgate"]
        out_shape += [jax.ShapeDtypeStruct((t, d), BF16), jax.ShapeDtypeStruct((1, d), F32),
                      jax.ShapeDtypeStruct((1, d), F32), jax.ShapeDtypeStruct((bl, 1, d), F32)]
        out_specs += [tok, vec, vec, seq]
    if has_mod:
        names += ["dscale", "dshift"]
        out_shape += [jax.ShapeDtypeStruct((bl, 1, d), F32)] * 2
        out_specs += [seq, seq]
    outs = pl.pallas_call(
        body, name=name, out_shape=out_shape, grid=(bl, nt), in_specs=in_specs, out_specs=out_specs,
        compiler_params=_params("arbitrary", "arbitrary"),
    )(*args)
    return dict(zip(names, outs))


def _silu(a):
    return a * jax.nn.sigmoid(a)


def silu_rows(a, name):
    def body(a_ref, o_ref):
        o_ref[...] = _silu(a_ref[...]).astype(BF16)

    return pl.pallas_call(body, name=name, out_shape=jax.ShapeDtypeStruct(a.shape, BF16))(a)


def swiglu_forward(a, b, name):
    t, f = a.shape
    tm, tf = _tile(t, 512, 8), _tile(f, 1536)

    def body(a_ref, b_ref, h_ref):
        h_ref[...] = (_silu(a_ref[...]) * b_ref[...]).astype(BF16)

    spec = pl.BlockSpec((tm, tf), lambda i, j: (i, j))
    return pl.pallas_call(
        body, name=name, out_shape=jax.ShapeDtypeStruct((t, f), BF16), grid=(t // tm, f // tf),
        in_specs=[spec, spec], out_specs=spec, compiler_params=_params("arbitrary", "arbitrary"),
    )(a, b)


def swiglu_backward(dh, a, b, name):
    t, f = a.shape
    tm, tf = _tile(t, 512, 8), _tile(f, 1536)

    def body(dh_ref, a_ref, b_ref, da_ref, db_ref):
        av = a_ref[...]
        sig = jax.nn.sigmoid(av)
        dhv = dh_ref[...]
        da_ref[...] = (dhv * b_ref[...] * (sig * (1.0 + av * (1.0 - sig)))).astype(BF16)
        db_ref[...] = (dhv * (av * sig)).astype(BF16)

    spec = pl.BlockSpec((tm, tf), lambda i, j: (i, j))
    return pl.pallas_call(
        body, name=name, out_shape=[jax.ShapeDtypeStruct((t, f), BF16)] * 2, grid=(t // tm, f // tf),
        in_specs=[spec, spec, spec], out_specs=[spec, spec], compiler_params=_params("arbitrary", "arbitrary"),
    )(dh, a, b)


def rope_tables(pos, inv_freq, sign, name):
    t = pos.shape[0]
    tm = _tile(t, 512, 8)

    def body(p_ref, f_ref, s_ref, c_out, s_out):
        ang = p_ref[...] * f_ref[...]
        c_out[...] = jnp.cos(ang)
        s_out[...] = jnp.sin(ang) * s_ref[...]

    vec = pl.BlockSpec((1, LANES), lambda i: (0, 0))
    tab = pl.BlockSpec((tm, LANES), lambda i: (i, 0))
    return pl.pallas_call(
        body, name=name, out_shape=[jax.ShapeDtypeStruct((t, LANES), F32)] * 2, grid=(t // tm,),
        in_specs=[pl.BlockSpec((tm, 1), lambda i: (i, 0)), vec, vec], out_specs=[tab, tab],
        compiler_params=_params("arbitrary"),
    )(pos, inv_freq, sign)


def _rot_half(v):
    lane = lax.broadcasted_iota(jnp.int32, v.shape, v.ndim - 1)
    up = pltpu.roll(v, LANES - MLA_ROPE // 2, v.ndim - 1)
    down = pltpu.roll(v, MLA_ROPE // 2, v.ndim - 1)
    return jnp.where(lane % MLA_ROPE < MLA_ROPE // 2, up, down)


def _rope(v, cos, sin_signed):
    return v * cos + _rot_half(v) * sin_signed


def _rope_transposed(dv, cos, sin_signed):
    return dv * cos + _rot_half(dv * sin_signed)


def rope_slabs(v, cos, sin_signed, out_dtype, name, transposed=False):
    ns, t, _ = v.shape
    tm = _tile(t, 512, 8)
    fn = _rope_transposed if transposed else _rope

    def body(v_ref, c_ref, s_ref, o_ref):
        o_ref[0] = fn(v_ref[0].astype(F32), c_ref[...], s_ref[...]).astype(out_dtype)

    tab = pl.BlockSpec((tm, LANES), lambda j, i: (i, 0))
    spec = pl.BlockSpec((1, tm, LANES), lambda j, i: (j, i, 0))
    return pl.pallas_call(
        body, name=name, out_shape=jax.ShapeDtypeStruct(v.shape, out_dtype), grid=(ns, t // tm),
        in_specs=[spec, tab, tab], out_specs=spec, compiler_params=_params("arbitrary", "arbitrary"),
    )(v, cos, sin_signed)


def _rms(x):
    rinv = lax.rsqrt(jnp.mean(x * x, axis=-1, keepdims=True) + NORM_EPS)
    return x * rinv, rinv


def mla_latents_forward(h_in, g_q, g_kv, cos, sin_signed, name):
    t = h_in.shape[0]
    tm = _tile(t, 512, 8)

    def body(h_ref, gq_ref, gkv_ref, c_ref, s_ref, cq_ref, ckv_ref, kr_ref):
        cq_ref[...] = (_rms(h_ref[:, 0:MLA_QR])[0] * gq_ref[...]).astype(BF16)
        ckv_ref[...] = (_rms(h_ref[:, MLA_QR:MLA_QR + MLA_KVR])[0] * gkv_ref[...]).astype(BF16)
        kr_ref[...] = _rope(h_ref[:, MLA_QR + MLA_KVR:], c_ref[...], s_ref[...]).astype(BF16)

    def tok(w):
        return pl.BlockSpec((tm, w), lambda i: (i, 0))

    def vec(w):
        return pl.BlockSpec((1, w), lambda i: (0, 0))

    return pl.pallas_call(
        body, name=name,
        out_shape=[jax.ShapeDtypeStruct((t, MLA_QR), BF16), jax.ShapeDtypeStruct((t, MLA_KVR), BF16),
                   jax.ShapeDtypeStruct((t, LANES), BF16)],
        grid=(t // tm,),
        in_specs=[tok(h_in.shape[1]), vec(MLA_QR), vec(MLA_KVR), tok(LANES), tok(LANES)],
        out_specs=[tok(MLA_QR), tok(MLA_KVR), tok(LANES)],
        compiler_params=_params("arbitrary"),
    )(h_in, g_q, g_kv, cos, sin_signed)


def mla_latents_backward(h_in, dcq, dckv, dkr, g_q, g_kv, cos, sin_signed, name):
    t, w = h_in.shape
    tm = _tile(t, 512, 8)

    def body(h_ref, dcq_ref, dckv_ref, dkr_ref, gq_ref, gkv_ref, c_ref, s_ref, dh_ref, dgq_ref, dgkv_ref):
        @pl.when(pl.program_id(0) == 0)
        def _():
            dgq_ref[...] = jnp.zeros_like(dgq_ref)
            dgkv_ref[...] = jnp.zeros_like(dgkv_ref)

        def rms_bwd(x, dc, g_ref, dg_ref):
            xn, rinv = _rms(x)
            dg_ref[...] += jnp.sum(dc * xn, axis=0, keepdims=True)
            dxn = dc * g_ref[...]
            return rinv * (dxn - xn * jnp.mean(dxn * xn, axis=-1, keepdims=True))

        dq = rms_bwd(h_ref[:, 0:MLA_QR], dcq_ref[...], gq_ref, dgq_ref)
        dkv = rms_bwd(h_ref[:, MLA_QR:MLA_QR + MLA_KVR], dckv_ref[...], gkv_ref, dgkv_ref)
        dr = _rope_transposed(dkr_ref[...], c_ref[...], s_ref[...])
        dh_ref[...] = jnp.concatenate([dq, dkv, dr], axis=1).astype(BF16)

    def tok(ww):
        return pl.BlockSpec((tm, ww), lambda i: (i, 0))

    def vec(ww):
        return pl.BlockSpec((1, ww), lambda i: (0, 0))

    return pl.pallas_call(
        body, name=name,
        out_shape=[jax.ShapeDtypeStruct((t, w), BF16), jax.ShapeDtypeStruct((1, MLA_QR), F32),
                   jax.ShapeDtypeStruct((1, MLA_KVR), F32)],
        grid=(t // tm,),
        in_specs=[tok(w), tok(MLA_QR), tok(MLA_KVR), tok(LANES), vec(MLA_QR), vec(MLA_KVR), tok(LANES), tok(LANES)],
        out_specs=[tok(w), vec(MLA_QR), vec(MLA_KVR)],
        compiler_params=_params("arbitrary"),
    )(h_in, dcq, dckv, dkr, g_q, g_kv, cos, sin_signed)


def _tri(n, lower):
    r = lax.broadcasted_iota(jnp.int32, (n, n), 0)
    c = lax.broadcasted_iota(jnp.int32, (n, n), 1)
    return jnp.where(r >= c if lower else r <= c, 1.0, 0.0).astype(F32)


def _dot_exact(tri, v):
    hi = v.astype(BF16)
    mid = (v - hi.astype(F32)).astype(BF16)
    lo = (v - hi.astype(F32) - mid.astype(F32)).astype(BF16)
    t = tri.astype(BF16)
    return _nn(t, hi) + _nn(t, mid) + _nn(t, lo)


def fox_gate_forward(z, b_f, bl, name):
    t = z.shape[0]
    s = t // bl
    ch = LANES
    n_ch = s // ch

    def body(z_ref, b_ref, f_ref, fs_ref):
        tri = _tri(ch, True)
        carry = jnp.zeros((1, LANES), F32)
        for k in range(n_ch):
            x = z_ref[k * ch:(k + 1) * ch, :] + b_ref[...]
            logf = jnp.minimum(x, 0.0) - jnp.log(1.0 + jnp.exp(-jnp.abs(x)))
            cs = _dot_exact(tri, logf) + carry
            carry = cs[ch - 1:ch, :]
            f_ref[k * ch:(k + 1) * ch, :] = cs
            for h in range(FOX_HEADS):
                fs_ref[h, k * ch:(k + 1) * ch, :] = jnp.broadcast_to(cs[:, h:h + 1], (ch, LANES))

    return pl.pallas_call(
        body, name=name,
        out_shape=[jax.ShapeDtypeStruct((t, LANES), F32), jax.ShapeDtypeStruct((FOX_HEADS, t, LANES), F32)],
        grid=(bl,),
        in_specs=[pl.BlockSpec((s, LANES), lambda b: (b, 0)), pl.BlockSpec((1, LANES), lambda b: (0, 0))],
        out_specs=[pl.BlockSpec((s, LANES), lambda b: (b, 0)),
                   pl.BlockSpec((FOX_HEADS, s, LANES), lambda b: (0, b, 0))],
        compiler_params=_params("arbitrary"),
    )(z, b_f)


def fox_gate_backward(z, b_f, df, bl, name):
    t = z.shape[0]
    s = t // bl
    ch = LANES
    n_ch = s // ch

    def body(z_ref, b_ref, df_ref, dz_ref, db_ref):
        @pl.when(pl.program_id(0) == 0)
        def _():
            db_ref[...] = jnp.zeros_like(db_ref)

        tri = _tri(ch, False)
        carry = jnp.zeros((1, LANES), F32)
        for k in reversed(range(n_ch)):
            cs = _dot_exact(tri, df_ref[k * ch:(k + 1) * ch, :]) + carry
            carry = cs[0:1, :]
            x = z_ref[k * ch:(k + 1) * ch, :] + b_ref[...]
            dz = cs * (1.0 - jax.nn.sigmoid(x))
            dz_ref[k * ch:(k + 1) * ch, :] = dz
            db_ref[...] += jnp.sum(dz, axis=0, keepdims=True)

    tok = pl.BlockSpec((s, LANES), lambda b: (b, 0))
    vec = pl.BlockSpec((1, LANES), lambda b: (0, 0))
    return pl.pallas_call(
        body, name=name,
        out_shape=[jax.ShapeDtypeStruct((t, LANES), F32), jax.ShapeDtypeStruct((1, LANES), F32)],
        grid=(bl,), in_specs=[tok, vec, tok], out_specs=[tok, vec],
        compiler_params=_params("arbitrary"),
    )(z, b_f, df)


NEG_INF = float("-inf")


def _attn_tiles(s):
    return _tile(s, 512, 8)


def attention_forward(kind, ops, bl, scale, name):
    fox = kind == "fox"
    if fox:
        qkv, fq, fk = ops
        t = qkv.shape[1]
        n_pair = FOX_HEADS // 2
    else:
        qn, qr, kn, kr, v = ops
        t = qn.shape[1]
        n_pair = MLA_HEADS // 2
    s = t // bl
    tq = _attn_tiles(s)
    nq = s // tq
    half = LANES // 2

    def body(*refs):
        if fox:
            q_ref, k_ref, v_ref, fq_ref, fk_ref, o_ref, lse_ref, o32_ref = refs
        else:
            qn_ref, qr_ref, kn_ref, kr_ref, v_ref, o_ref, lse_ref = refs
        i = pl.program_id(2)
        row = lax.broadcasted_iota(jnp.int32, (tq, tq), 0)
        col = lax.broadcasted_iota(jnp.int32, (tq, tq), 1)
        heads = []
        for e in range(2):
            sl = slice(e * half, (e + 1) * half)
            if fox:
                heads.append((sl, q_ref[0, :, sl], None))
            else:
                heads.append((sl, qn_ref[e], qr_ref[0, :, sl]))
        dv = half if fox else LANES

        def wide(stat):
            return jnp.concatenate([stat] * (tq // LANES), axis=1)

        def step(j, carry, masked):
            rows = pl.ds(pl.multiple_of(j * tq, tq), tq)
            new = []
            for e, (sl, qa, qb) in enumerate(heads):
                m, l, acc = carry[e]
                if fox:
                    sc = _nt(qa, k_ref[0, rows, sl]) * scale + wide(fq_ref[e]) - fk_ref[0, j, e:e + 1, :]
                    vv = v_ref[0, rows, sl]
                else:
                    sc = (_nt(qa, kn_ref[e, rows, :]) + _nt(qb, kr_ref[rows, 0:half])) * scale
                    vv = v_ref[e, rows, :]
                if masked:
                    sc = jnp.where(row >= col, sc, NEG_INF)
                m_new = jnp.maximum(m, jnp.max(sc, axis=1, keepdims=True))
                p = jnp.exp(sc - m_new)
                a = jnp.exp(m - m_new)
                l = a * l + jnp.sum(p, axis=1, keepdims=True)
                p_hi = p.astype(BF16)
                acc = a * acc + _nn(p_hi, vv)
                if fox:
                    acc = acc + _nn((p - p_hi.astype(F32)).astype(BF16), vv)
                new.append((m_new, l, acc))
            return tuple(new)

        init = (jnp.full((tq, 1), NEG_INF, F32), jnp.zeros((tq, 1), F32), jnp.zeros((tq, dv), F32))
        carry = step(i, (init, init), True)
        carry = lax.fori_loop(0, i, lambda j, c: step(j, c, False), carry)
        outs = [acc / l for _, l, acc in carry]
        for e, (m, l, _) in enumerate(carry):
            lse_ref[e] = jnp.broadcast_to(m + jnp.log(l), (tq, LANES))
        if fox:
            o32 = jnp.concatenate(outs, axis=1)
            o32_ref[0] = o32
            o_ref[0] = o32.astype(BF16)
        else:
            o_ref[0] = outs[0].astype(BF16)
            o_ref[1] = outs[1].astype(BF16)

    def q_idx(b, g, i):
        return (g, b * nq + i, 0)

    if fox:
        nk = fk.shape[1]
        in_specs = [pl.BlockSpec((1, tq, LANES), q_idx),
                    pl.BlockSpec((1, s, LANES), lambda b, g, i: (n_pair + g, b, 0)),
                    pl.BlockSpec((1, s, LANES), lambda b, g, i: (2 * n_pair + g, b, 0)),
                    pl.BlockSpec((2, tq, LANES), q_idx),
                    pl.BlockSpec((1, nk, 8, tq), lambda b, g, i: (b * n_pair + g, 0, 0, 0))]
        args = [qkv, qkv, qkv, fq, fk]
        o_spec = pl.BlockSpec((1, tq, LANES), q_idx)
    else:
        in_specs = [pl.BlockSpec((2, tq, LANES), q_idx),
                    pl.BlockSpec((1, tq, LANES), q_idx),
                    pl.BlockSpec((2, s, LANES), lambda b, g, i: (g, b, 0)),
                    pl.BlockSpec((s, LANES), lambda b, g, i: (b, 0)),
                    pl.BlockSpec((2, s, LANES), lambda b, g, i: (g, b, 0))]
        args = [qn, qr, kn, kr, v]
        o_spec = pl.BlockSpec((2, tq, LANES), q_idx)
    out_shape = [jax.ShapeDtypeStruct((8, t, LANES), BF16), jax.ShapeDtypeStruct((2 * n_pair, t, LANES), F32)]
    out_specs = [o_spec, pl.BlockSpec((2, tq, LANES), q_idx)]
    if fox:
        out_shape.append(jax.ShapeDtypeStruct((8, t, LANES), F32))
        out_specs.append(o_spec)
    outs = pl.pallas_call(
        body, name=name, out_shape=out_shape, grid=(bl, n_pair, nq), in_specs=in_specs, out_specs=out_specs,
        compiler_params=_params("arbitrary", "arbitrary", "arbitrary"),
    )(*args)
    return (outs[0], outs[1], outs[2] if fox else outs[0])


def attention_backward(kind, ops, o, do, lse, bl, scale, name):
    fox = kind == "fox"
    if fox:
        qkv, fq, fk = ops
        t = qkv.shape[1]
        n_pair = FOX_HEADS // 2
    else:
        qn, qr, kn, kr, v = ops
        t = qn.shape[1]
        n_pair = MLA_HEADS // 2
    s = t // bl
    tq = _attn_tiles(s)
    nq = s // tq
    half = LANES // 2

    def body(*refs):
        if fox:
            (q_ref, k_ref, v_ref, fq_ref, fk_ref, o_ref, do_ref, lse_ref,
             dq_ref, dk_ref, dv_ref, dfk_ref, delta_scr, qt_scr, dot_scr) = refs
        else:
            (qn_ref, qr_ref, kn_ref, kr_ref, v_ref, o_ref, do_ref, lse_ref,
             dqn_ref, dqr_ref, dkn_ref, dv_ref, dkr_ref, delta_scr, qt_scr, qrt_scr, dot_scr) = refs
        g, j = pl.program_id(1), pl.program_id(2)
        row = lax.broadcasted_iota(jnp.int32, (tq, tq), 0)
        col = lax.broadcasted_iota(jnp.int32, (tq, tq), 1)
        krows = pl.ds(pl.multiple_of(j * tq, tq), tq)

        def transposed(v):
            return v.astype(F32).T.astype(BF16)

        def wide(stat):
            return jnp.concatenate([stat] * (tq // LANES), axis=1)

        @pl.when(j == 0)
        def _():
            if fox:
                dq_ref[...] = jnp.zeros_like(dq_ref)
            else:
                dqn_ref[...] = jnp.zeros_like(dqn_ref)
                dqr_ref[...] = jnp.zeros_like(dqr_ref)
            for ii in range(nq):
                rws = slice(ii * tq, (ii + 1) * tq)
                deltas = []
                if fox:
                    prod = do_ref[0, rws, :].astype(F32) * o_ref[0, rws, :].astype(F32)
                    for e in range(2):
                        deltas.append(jnp.sum(prod[:, e * half:(e + 1) * half], axis=1, keepdims=True))
                    qt_scr[ii] = transposed(q_ref[0, rws, :])
                    dot_scr[ii] = transposed(do_ref[0, rws, :])
                else:
                    for e in range(2):
                        prod = do_ref[e, rws, :].astype(F32) * o_ref[e, rws, :].astype(F32)
                        deltas.append(jnp.sum(prod, axis=1, keepdims=True))
                        qt_scr[e, ii] = transposed(qn_ref[e, rws, :])
                        dot_scr[e, ii] = transposed(do_ref[e, rws, :])
                    qrt_scr[ii] = transposed(qr_ref[0, rws, :])
                for e in range(2):
                    delta_scr[e, rws, :] = jnp.broadcast_to(deltas[e], (tq, LANES))

        if fox:
            dfk_ref[...] = jnp.zeros_like(dfk_ref)
        else:
            @pl.when(jnp.logical_and(g == 0, j == 0))
            def _():
                dkr_ref[...] = jnp.zeros_like(dkr_ref)

        heads = []
        for e in range(2):
            sl = slice(e * half, (e + 1) * half)
            if fox:
                heads.append((sl, k_ref[0, :, sl], v_ref[0, :, sl], fk_ref[0, 0, e:e + 1, :]))
            else:
                heads.append((sl, kn_ref[e], v_ref[e], kr_ref[krows, 0:half]))
        dk_w = dv_w = half if fox else LANES

        def step(i, carry, masked):
            rows = pl.ds(pl.multiple_of(i * tq, tq), tq)
            new = []
            for e, (sl, k_e, v_e, x_e) in enumerate(heads):
                dk_acc, dv_acc, last = carry[e]
                if fox:
                    do_i = do_ref[0, rows, sl]
                    sc = _nt(q_ref[0, rows, sl], k_e) * scale + wide(fq_ref[e, rows, :]) - x_e
                else:
                    do_i = do_ref[e, rows, :]
                    sc = (_nt(qn_ref[e, rows, :], k_e) + _nt(qr_ref[0, rows, sl], x_e)) * scale
                if masked:
                    sc = jnp.where(row >= col, sc, NEG_INF)
                p = jnp.exp(sc - wide(lse_ref[e, rows, :]))
                dp = _nt(do_i, v_e)
                ds = p * (dp - wide(delta_scr[e, rows, :]))
                dsb = (ds * scale).astype(BF16)
                if fox:
                    fsl = slice(e * half, (e + 1) * half)
                    dv_acc = dv_acc + _nn(dot_scr[i, fsl, :], p.astype(BF16))
                    dk_acc = dk_acc + _nn(qt_scr[i, fsl, :], dsb)
                    dq_ref[0, rows, sl] += _nn(dsb, k_e)
                    last = last - jnp.sum(ds, axis=0, keepdims=True)
                else:
                    dv_acc = dv_acc + _nn(dot_scr[e, i], p.astype(BF16))
                    dk_acc = dk_acc + _nn(qt_scr[e, i], dsb)
                    dqn_ref[e, rows, :] += _nn(dsb, k_e)
                    dqr_ref[0, rows, sl] += _nn(dsb, x_e)
                    last = last + _nn(qrt_scr[i, e * half:(e + 1) * half, :], dsb)
                new.append((dk_acc, dv_acc, last))
            return tuple(new)

        last0 = jnp.zeros((1, tq), F32) if fox else jnp.zeros((half, tq), F32)
        init = (jnp.zeros((dk_w, tq), F32), jnp.zeros((dv_w, tq), F32), last0)
        carry = step(j, (init, init), True)
        carry = lax.fori_loop(j + 1, nq, lambda i, c: step(i, c, False), carry)
        if fox:
            for e in range(2):
                dfk_ref[0, 0, e:e + 1, :] = carry[e][2]
            dk_ref[0] = jnp.concatenate([carry[0][0], carry[1][0]], axis=0).T.astype(BF16)
            dv_ref[0] = jnp.concatenate([carry[0][1], carry[1][1]], axis=0).T.astype(BF16)
        else:
            for e in range(2):
                dkn_ref[e] = carry[e][0].T.astype(BF16)
                dv_ref[e] = carry[e][1].T.astype(BF16)
            dkr_t = carry[0][2] + carry[1][2]
            dkr_ref[krows, :] += jnp.concatenate([dkr_t, jnp.zeros_like(dkr_t)], axis=0).T

    def whole(b, g, j):
        return (g, b, 0)

    def kblk(b, g, j):
        return (g, b * nq + j, 0)

    if fox:
        in_specs = [pl.BlockSpec((1, s, LANES), whole),
                    pl.BlockSpec((1, tq, LANES), lambda b, g, j: (n_pair + g, b * nq + j, 0)),
                    pl.BlockSpec((1, tq, LANES), lambda b, g, j: (2 * n_pair + g, b * nq + j, 0)),
                    pl.BlockSpec((2, s, LANES), whole),
                    pl.BlockSpec((1, 1, 8, tq), lambda b, g, j: (b * n_pair + g, j, 0, 0)),
                    pl.BlockSpec((1, s, LANES), whole), pl.BlockSpec((1, s, LANES), whole),
                    pl.BlockSpec((2, s, LANES), whole)]
        args = [qkv, qkv, qkv, fq, fk, o, do, lse]
        out_shape = [jax.ShapeDtypeStruct((8, t, LANES), F32), jax.ShapeDtypeStruct((8, t, LANES), BF16),
                     jax.ShapeDtypeStruct((8, t, LANES), BF16), jax.ShapeDtypeStruct(fk.shape, F32)]
        out_specs = [pl.BlockSpec((1, s, LANES), whole), pl.BlockSpec((1, tq, LANES), kblk),
                     pl.BlockSpec((1, tq, LANES), kblk),
                     pl.BlockSpec((1, 1, 8, tq), lambda b, g, j: (b * n_pair + g, j, 0, 0))]
    else:
        pair = pl.BlockSpec((2, s, LANES), whole)
        pair_k = pl.BlockSpec((2, tq, LANES), kblk)
        in_specs = [pair, pl.BlockSpec((1, s, LANES), whole), pair_k,
                    pl.BlockSpec((s, LANES), lambda b, g, j: (b, 0)), pair_k,
                    pair, pair, pair]
        args = [qn, qr, kn, kr, v, o, do, lse]
        out_shape = [jax.ShapeDtypeStruct((8, t, LANES), F32), jax.ShapeDtypeStruct((4, t, LANES), F32),
                     jax.ShapeDtypeStruct((8, t, LANES), BF16), jax.ShapeDtypeStruct((8, t, LANES), BF16),
                     jax.ShapeDtypeStruct((t, LANES), F32)]
        out_specs = [pair, pl.BlockSpec((1, s, LANES), whole), pair_k, pair_k,
                     pl.BlockSpec((s, LANES), lambda b, g, j: (b, 0))]
    t_blocks = pltpu.VMEM((nq, LANES, tq), BF16)
    t_pairs = pltpu.VMEM((2, nq, LANES, tq), BF16)
    scratch = [pltpu.VMEM((2, s, LANES), F32)] + ([t_blocks, t_blocks] if fox else [t_pairs, t_blocks, t_pairs])
    return pl.pallas_call(
        body, name=name, out_shape=out_shape, grid=(bl, n_pair, nq), in_specs=in_specs, out_specs=out_specs,
        scratch_shapes=scratch, compiler_params=_params("arbitrary", "arbitrary", "arbitrary"),
    )(*args)


def adamw(w, g, m, v, name):
    shape = w.shape
    c = shape[-1]
    r = w.size // c
    tr = _tile(r, 512, 8)

    def body(w_ref, g_ref, m_ref, v_ref, d_ref, nm_ref, nv_ref):
        gv = g_ref[...]
        m2 = ADAM_B1 * m_ref[...] + (1.0 - ADAM_B1) * gv
        v2 = ADAM_B2 * v_ref[...] + (1.0 - ADAM_B2) * (gv * gv)
        m_hat = m2 / (1.0 - ADAM_B1 ** ADAM_STEP)
        v_hat = v2 / (1.0 - ADAM_B2 ** ADAM_STEP)
        d_ref[...] = -ADAM_LR * (m_hat / (jnp.sqrt(v_hat) + ADAM_EPS) + ADAM_WD * w_ref[...])
        nm_ref[...] = m2
        nv_ref[...] = v2

    spec = pl.BlockSpec((tr, c), lambda i: (i, 0))
    outs = pl.pallas_call(
        body, name=name, out_shape=[jax.ShapeDtypeStruct((r, c), F32)] * 3, grid=(r // tr,),
        in_specs=[spec] * 4, out_specs=[spec] * 3, compiler_params=_params("arbitrary"),
    )(*(a.reshape(r, c) for a in (w, g, m, v)))
    return tuple(a.reshape(shape) for a in outs)


PACK_COLS = 1024


def _pack_rows(a):
    return a.reshape(-1, PACK_COLS)


def kernel(x, c, positions, mla_w_in, mla_g_q, mla_w_uq, mla_g_kv, mla_w_uk, mla_w_uv, mla_w_o, fox_w_in, fox_b_f, fox_w_o, ada_w, ada_b, ffn_w_gate, ffn_w_up, ffn_w_down, ln_g, ln_b, loss_target, m_mla_w_in, m_mla_g_q, m_mla_w_uq, m_mla_g_kv, m_mla_w_uk, m_mla_w_uv, m_mla_w_o, m_fox_w_in, m_fox_b_f, m_fox_w_o, m_ada_w, m_ada_b, m_ffn_w_gate, m_ffn_w_up, m_ffn_w_down, m_ln_g, m_ln_b, v_mla_w_in, v_mla_g_q, v_mla_w_uq, v_mla_g_kv, v_mla_w_uk, v_mla_w_uv, v_mla_w_o, v_fox_w_in, v_fox_b_f, v_fox_w_o, v_ada_w, v_ada_b, v_ffn_w_gate, v_ffn_w_up, v_ffn_w_down, v_ln_g, v_ln_b):
    bl, s, d = x.shape
    t = bl * s
    ff = ffn_w_gate.shape[-1] * N_DEV
    dev = 4 * lax.axis_index("x") + 2 * lax.axis_index("y") + lax.axis_index("c")
    ada_cols = ada_w.shape[-1]
    fox_in = fox_w_in.shape[-1] * N_DEV
    mla_in = mla_w_in.shape[-1]
    mla_in_pad = mla_in + (-mla_in) % LANES

    def t_last(a):
        return jnp.swapaxes(a, -1, -2)

    local = {
        "mla_w_in": mla_w_in[0],
        "mla_w_uq": t_last(mla_w_uq[0]),
        "mla_w_uk": t_last(mla_w_uk[0]),
        "mla_w_uv": t_last(mla_w_uv[0]),
        "mla_w_o": mla_w_o[0],
        "fox_w_in": t_last(fox_w_in[0]),
        "fox_w_o": fox_w_o[0],
    }
    for i in range(DEPTH):
        local.update({f"gate{i}": t_last(ffn_w_gate[i]), f"up{i}": t_last(ffn_w_up[i]), f"down{i}": ffn_w_down[i]})
    groups = [["mla_w_in", "mla_w_uq", "mla_w_uk", "mla_w_uv", "mla_w_o"],
              ["gate0", "up0", "down0"],
              ["fox_w_in", "fox_w_o", "gate1", "up1", "down1"]]
    offsets, rows_of, slot_of, group_of = {}, {}, {}, {}
    group_rows = []
    for gi, names in enumerate(groups):
        rows = 0
        for nm in names:
            rows_of[nm] = local[nm].size // PACK_COLS
            slot_of[nm] = rows_of[nm] + (-rows_of[nm]) % 16
            offsets[nm] = rows
            group_of[nm] = gi
            rows += slot_of[nm]
        group_rows.append(rows)

    def slot(nm, rows):
        pad = [(0, 0)] * rows.ndim
        pad[-2] = (0, slot_of[nm] - rows_of[nm])
        return jnp.pad(rows, pad)

    def landing(block):
        land = lax.empty((N_DEV,) + block.shape, block.dtype)
        return lax.dynamic_update_slice(land, block[None], (dev, 0, 0))

    packed = [jnp.concatenate([slot(nm, _pack_rows(local[nm]).astype(BF16)) for nm in names], axis=0)
              for names in groups]
    gathered = [all_gather(packed[0], "gather_mla_weights"), None, None]
    gather_started = [None] * len(groups)

    def full(nm, cols):
        blk = gathered[group_of[nm]][:, offsets[nm]:offsets[nm] + rows_of[nm], :]
        return blk.reshape(-1, cols)

    w_in = jnp.pad(full("mla_w_in", mla_in), ((0, 0), (0, mla_in_pad - mla_in)))
    wt_uq = full("mla_w_uq", MLA_QR).reshape(MLA_HEADS, MLA_NOPE + MLA_ROPE, MLA_QR)
    wt_uq_n = wt_uq[:, :MLA_NOPE].reshape(MLA_HEADS * MLA_NOPE, MLA_QR)
    wt_uq_r = wt_uq[:, MLA_NOPE:].reshape(MLA_HEADS * MLA_ROPE, MLA_QR)
    wt_uk = full("mla_w_uk", MLA_KVR)
    wt_uv = full("mla_w_uv", MLA_KVR)
    w_mo = full("mla_w_o", d)
    wt_gate, wt_up, w_down = [None] * DEPTH, [None] * DEPTH, [None] * DEPTH

    def arrive(gi, after):
        gathered[gi] = exchange_wait(gather_started[gi], after, f"gather_group{gi}_wait", False)
        for i in range(DEPTH):
            if group_of[f"gate{i}"] == gi:
                wt_gate[i], wt_up[i], w_down[i] = full(f"gate{i}", d), full(f"up{i}", d), full(f"down{i}", d)

    small = jnp.concatenate([c.reshape(-1, LANES), ln_g.reshape(-1, LANES), ln_b.reshape(-1, LANES)], axis=0)
    small_rows = small.shape[0]
    small = jnp.pad(small, ((0, (-small_rows) % 8), (0, 0)))
    small_all = all_gather(small, "gather_small")
    c_rows = bl * d // LANES
    c_all = small_all[:, :c_rows].reshape(N_DEV * bl, d)
    n_ln = DEPTH * 2
    ln_g_all = small_all[:, c_rows:c_rows + n_ln, :].transpose(1, 0, 2).reshape(DEPTH, 2, 1, d)
    ln_b_all = small_all[:, c_rows + n_ln:c_rows + 2 * n_ln, :].transpose(1, 0, 2).reshape(DEPTH, 2, 1, d)

    c_act = silu_rows(c_all, "silu_c")
    ada_b_loc = lax.dynamic_slice_in_dim(ada_b, dev * ada_cols, ada_cols, axis=1)
    mod_cols = [mm([(c_act, ada_w[i])], trans_b=False, out_dtype=F32, name=f"ada_fwd{i}", bias=ada_b_loc[i][None, :])
                for i in range(DEPTH)]
    mod_all = all_gather(jnp.concatenate(mod_cols, axis=0), "gather_mod")
    mod_all = mod_all.reshape(N_DEV, DEPTH, N_DEV * bl, ada_cols).transpose(1, 2, 0, 3).reshape(DEPTH, N_DEV * bl, 6 * d)
    mod_mine = lax.dynamic_slice_in_dim(mod_all, dev * bl, bl, axis=1)
    mods = [mod_mine[i].reshape(bl * 6, 1, d) for i in range(DEPTH)]
    for gi in (1, 2):
        block, _ = lax.optimization_barrier((packed[gi], mod_mine))
        gather_started[gi] = exchange_start(block, landing(block), f"gather_group{gi}_start", False)
        mods[0] = after_token(mods[0], gather_started[gi])

    half_r = MLA_ROPE // 2
    inv_freq = ROPE_THETA ** (-jnp.arange(half_r, dtype=F32) / half_r)
    inv_freq = jnp.tile(inv_freq, LANES // half_r)[None, :]
    sign = jnp.tile(jnp.concatenate([-jnp.ones((half_r,), F32), jnp.ones((half_r,), F32)]), LANES // MLA_ROPE)[None, :]
    cos_t, sin_t = rope_tables(positions.astype(F32).reshape(t, 1), inv_freq, sign, "rope_tables")

    x2d = x.reshape(t, d)
    g_q, g_kv = mla_g_q.reshape(1, MLA_QR), mla_g_kv.reshape(1, MLA_KVR)
    b_f = jnp.pad(fox_b_f.reshape(1, FOX_HEADS), ((0, 0), (0, LANES - FOX_HEADS)))
    mla_scale = (MLA_NOPE + MLA_ROPE) ** -0.5
    fox_scale = FOX_HD ** -0.5
    tq = _attn_tiles(s)
    nk = s // tq

    saved = []
    u = modulate(x2d, mods[0], 0, 1, bl, "modulate0")
    xin = x2d
    for i in range(DEPTH):
        sv = {"u": u, "x_in": xin}
        if i % 2 == 0:
            h_in = mm([(u, w_in)], trans_b=False, out_dtype=F32, name=f"mla_in{i}")
            c_q, c_kv, k_r = mla_latents_forward(h_in, g_q, g_kv, cos_t, sin_t, f"mla_latents{i}")
            q_n = mm([(c_q, wt_uq_n)], trans_b=True, out_dtype=BF16, out_slab=True, name=f"mla_qn{i}")
            q_r_raw = mm([(c_q, wt_uq_r)], trans_b=True, out_dtype=F32, out_slab=True, name=f"mla_qr{i}")
            q_r = rope_slabs(q_r_raw, cos_t, sin_t, BF16, f"mla_qrope{i}")
            k_n = mm([(c_kv, wt_uk)], trans_b=True, out_dtype=BF16, out_slab=True, name=f"mla_kn{i}")
            v_m = mm([(c_kv, wt_uv)], trans_b=True, out_dtype=BF16, out_slab=True, name=f"mla_v{i}")
            ops = (q_n, q_r, k_n, k_r, v_m)
            o, lse, o_delta = attention_forward("mla", ops, bl, mla_scale, f"mla_attn{i}")
            y = mm([(o, w_mo)], trans_b=False, out_dtype=F32, name=f"mla_out{i}")
            sv.update(h_in=h_in, c_q=c_q, c_kv=c_kv, ops=ops, o=o, lse=lse, o_delta=o_delta)
        else:
            arrive(2, u)
            wt_fox = full("fox_w_in", d)
            wt_qkv = wt_fox[:3 * d]
            wt_f = jnp.pad(wt_fox[3 * d:], ((0, LANES - FOX_HEADS), (0, 0)))
            w_fo = full("fox_w_o", d)
            qkv = mm([(u, wt_qkv)], trans_b=True, out_dtype=BF16, out_slab=True, name=f"fox_qkv{i}")
            z = mm([(u, wt_f)], trans_b=True, out_dtype=F32, name=f"fox_z{i}")
            f_tok, f_q = fox_gate_forward(z, b_f, bl, f"fox_gate{i}")
            f_k = f_tok[:, :FOX_HEADS].reshape(bl, nk, tq, FOX_HEADS // 2, 2).transpose(0, 3, 1, 4, 2)
            f_k = jnp.pad(f_k.reshape(bl * FOX_HEADS // 2, nk, 2, tq), ((0, 0), (0, 0), (0, 6), (0, 0)))
            ops = (qkv, f_q, f_k)
            o, lse, o_delta = attention_forward("fox", ops, bl, fox_scale, f"fox_attn{i}")
            y = mm([(o, w_fo)], trans_b=False, out_dtype=F32, name=f"fox_out{i}")
            sv.update(z=z, ops=ops, o=o, lse=lse, o_delta=o_delta)
        x1, r1, u2 = residual_layer_norm(xin, y, mods[i], 2, ln_g_all[i, 0], ln_b_all[i, 0], bl, f"ln_mix{i}",
                                         next_mod=(3, 4))
        if wt_gate[i] is None:
            arrive(group_of[f"gate{i}"], u2)
        a = mm([(u2, wt_gate[i])], trans_b=True, out_dtype=F32, name=f"ffn_gate{i}")
        bb = mm([(u2, wt_up[i])], trans_b=True, out_dtype=F32, name=f"ffn_up{i}")
        h = swiglu_forward(a, bb, f"swiglu{i}")
        y2 = mm([(h, w_down[i])], trans_b=False, out_dtype=F32, name=f"ffn_down{i}")
        sv.update(y=y, r1=r1, u2=u2, a=a, bb=bb, h=h, y2=y2)
        if i + 1 < DEPTH:
            xin, r2, u = residual_layer_norm(x1, y2, mods[i], 5, ln_g_all[i, 1], ln_b_all[i, 1], bl, f"ln_ffn{i}",
                                             next_mod=(0, 1, mods[i + 1]))
        else:
            xin, r2 = residual_layer_norm(x1, y2, mods[i], 5, ln_g_all[i, 1], ln_b_all[i, 1], bl, f"ln_ffn{i}")
        sv.update(r2=r2)
        saved.append(sv)

    loss_cols, d_x = loss_head(xin, loss_target.reshape(t, d), "loss_head")

    grads_full = {}
    dmod = [[None] * 6 for _ in range(DEPTH)]
    dg_ln = [[None, None] for _ in range(DEPTH)]
    db_ln = [[None, None] for _ in range(DEPTH)]
    dg_q = dg_kv = db_f = None
    d_a, du = d_x, None
    scatter_started = [None] * len(groups)

    def scatter_start(gi):
        g = jnp.concatenate(
            [slot(nm, grads_full[nm].reshape(N_DEV, rows_of[nm], PACK_COLS).astype(BF16)) for nm in groups[gi]], axis=1)
        own = lax.dynamic_index_in_dim(g, dev, 0, keepdims=False)
        scatter_started[gi] = exchange_start(g, landing(own), f"scatter_group{gi}_start", True)

    ln_g_bwd = [[ln_g_all[i, k] for k in range(2)] for i in range(DEPTH)]
    for i in reversed(range(DEPTH)):
        sv = saved[i]
        if i + 1 < DEPTH:
            scatter_start(2)
            ln_g_bwd[i][1] = after_token(ln_g_bwd[i][1], scatter_started[2])
        ln2 = (sv["r2"], sv["y2"], ln_g_bwd[i][1], ln_b_all[i, 1], (mods[i], 5))
        if du is None:
            bw = sublayer_backward(d_a, bl, f"bwd_ln_ffn{i}", ln=ln2)
        else:
            bw = sublayer_backward(d_a, bl, f"bwd_ln_ffn{i}", du=du, scale=(mods[i + 1], 1), ln=ln2)
            dmod[i + 1][0], dmod[i + 1][1] = bw["dshift"], bw["dscale"]
        dmod[i][5], dg_ln[i][1], db_ln[i][1] = bw["dgate"], bw["dg"], bw["db"]
        dy2 = bw["dy"]
        dh = mm([(dy2, w_down[i])], trans_b=True, out_dtype=F32, name=f"bwd_ffn_dh{i}")
        da, dbb = swiglu_backward(dh, sv["a"], sv["bb"], f"bwd_swiglu{i}")
        du2 = mm([(da, wt_gate[i]), (dbb, wt_up[i])], trans_b=False, out_dtype=F32, name=f"bwd_ffn_du{i}")
        grads_full[f"down{i}"] = mm_tn(sv["h"], dy2, name=f"bwd_w_down{i}")
        grads_full[f"gate{i}"] = mm_tn(da, sv["u2"], name=f"bwd_w_gate{i}")
        grads_full[f"up{i}"] = mm_tn(dbb, sv["u2"], name=f"bwd_w_up{i}")
        if i == 0:
            scatter_start(1)
            ln_g_bwd[i][0] = after_token(ln_g_bwd[i][0], scatter_started[1])
        bw = sublayer_backward(bw["dx"], bl, f"bwd_ln_mix{i}", du=du2, scale=(mods[i], 4),
                               ln=(sv["r1"], sv["y"], ln_g_bwd[i][0], ln_b_all[i, 0], (mods[i], 2)))
        dmod[i][3], dmod[i][4], dmod[i][2] = bw["dshift"], bw["dscale"], bw["dgate"]
        dg_ln[i][0], db_ln[i][0] = bw["dg"], bw["db"]
        d_a, dy = bw["dx"], bw["dy"]
        o, lse, ops = sv["o"], sv["lse"], sv["ops"]
        if i % 2 == 0:
            do = mm([(dy, w_mo)], trans_b=True, out_dtype=BF16, out_slab=True, name=f"bwd_mla_do{i}")
            grads_full["mla_w_o"] = mm_tn(o, dy, name=f"bwd_w_mla_o{i}")
            dqn, dqr, dkn, dvm, dkr = attention_backward("mla", ops, sv["o_delta"], do, lse, bl, mla_scale,
                                                         f"bwd_mla_attn{i}")
            dqr = rope_slabs(dqr, cos_t, sin_t, F32, f"bwd_mla_qrope{i}", transposed=True)
            dcq = mm([(dqn, wt_uq_n), (dqr, wt_uq_r)], trans_b=False, out_dtype=F32, name=f"bwd_mla_dcq{i}")
            dckv = mm([(dkn, wt_uk), (dvm, wt_uv)], trans_b=False, out_dtype=F32, name=f"bwd_mla_dckv{i}")
            d_uq_n = mm_tn(dqn, sv["c_q"], name=f"bwd_w_uq_n{i}").reshape(MLA_HEADS, MLA_NOPE, MLA_QR)
            d_uq_r = mm_tn(dqr, sv["c_q"], name=f"bwd_w_uq_r{i}").reshape(MLA_HEADS, MLA_ROPE, MLA_QR)
            grads_full["mla_w_uq"] = jnp.concatenate([d_uq_n, d_uq_r], axis=1)
            grads_full["mla_w_uk"] = mm_tn(dkn, sv["c_kv"], name=f"bwd_w_uk{i}")
            grads_full["mla_w_uv"] = mm_tn(dvm, sv["c_kv"], name=f"bwd_w_uv{i}")
            dh_in, dg_q, dg_kv = mla_latents_backward(sv["h_in"], dcq, dckv, dkr, g_q, g_kv, cos_t, sin_t,
                                                      f"bwd_mla_latents{i}")
            du = mm([(dh_in, w_in)], trans_b=True, out_dtype=F32, name=f"bwd_mla_du{i}")
            grads_full["mla_w_in"] = mm_tn(sv["u"], dh_in, name=f"bwd_w_mla_in{i}")[:, :mla_in]
        else:
            do = mm([(dy, w_fo)], trans_b=True, out_dtype=BF16, out_slab=True, name=f"bwd_fox_do{i}")
            grads_full["fox_w_o"] = mm_tn(o, dy, name=f"bwd_w_fox_o{i}")
            dq, dk, dvf, dfk = attention_backward("fox", ops, sv["o_delta"], do, lse, bl, fox_scale, f"bwd_fox_attn{i}")
            df = dfk[:, :, :2, :].reshape(bl, FOX_HEADS // 2, nk, 2, tq).transpose(0, 2, 4, 1, 3).reshape(t, FOX_HEADS)
            df = jnp.pad(df, ((0, 0), (0, LANES - FOX_HEADS)))
            dz, db_f = fox_gate_backward(sv["z"], b_f, df, bl, f"bwd_fox_gate{i}")
            du = mm([(dq, wt_fox[0:d]), (dk, wt_fox[d:2 * d]), (dvf, wt_fox[2 * d:3 * d]), (dz, wt_f)],
                    trans_b=False, out_dtype=F32, name=f"bwd_fox_du{i}")
            u_f = sv["u"]
            grads_full["fox_w_in"] = jnp.concatenate(
                [mm_tn(dq, u_f, name=f"bwd_w_fox_q{i}"), mm_tn(dk, u_f, name=f"bwd_w_fox_k{i}"),
                 mm_tn(dvf, u_f, name=f"bwd_w_fox_v{i}"), mm_tn(dz, u_f, name=f"bwd_w_fox_f{i}")[:FOX_HEADS]], axis=0)
    bw = sublayer_backward(d_a, bl, "bwd_input", du=du, scale=(mods[0], 1), x_in=x2d)
    dmod[0][0], dmod[0][1] = bw["dshift"], bw["dscale"]
    grad_x = bw["dx"].reshape(bl, s, d)

    dmod_rows = jnp.concatenate([r.reshape(bl, d) for layer in dmod for r in layer], axis=0)
    dmod_rows = dmod_rows.reshape(DEPTH, 6, bl, d).transpose(0, 2, 1, 3)
    n_mod = dmod_rows.size // LANES
    ln_parts = [dg_ln[i][k] for i in range(DEPTH) for k in range(2)] + [db_ln[i][k] for i in range(DEPTH) for k in range(2)]
    small_g = jnp.concatenate([dmod_rows.reshape(-1, LANES), dg_q.reshape(-1, LANES), dg_kv.reshape(-1, LANES), db_f]
                              + [p.reshape(-1, LANES) for p in ln_parts] + [loss_cols.reshape(-1, LANES)], axis=0)
    n_small = small_g.shape[0]
    small_g = jnp.pad(small_g, ((0, (-n_small) % 8), (0, 0)))
    small_g_all = all_gather(small_g, "gather_small_grads")
    small_sum = sum_leading(small_g_all, "sum_small_grads")
    per_seq = DEPTH * 6 * d // LANES
    dmod_all = small_g_all[:, :n_mod].reshape(N_DEV, DEPTH, bl, 6 * d).transpose(1, 0, 2, 3)
    dmod_all = dmod_all.reshape(DEPTH, N_DEV * bl, 6 * d)
    o1 = n_mod
    grad_g_q = small_sum[o1:o1 + MLA_QR // LANES].reshape(1, MLA_QR)
    o1 += MLA_QR // LANES
    grad_g_kv = small_sum[o1:o1 + MLA_KVR // LANES].reshape(1, MLA_KVR)
    o1 += MLA_KVR // LANES
    grad_b_f = small_sum[o1:o1 + 1, :FOX_HEADS]
    o1 += 1
    n_ln_rows = DEPTH * 2 * d // LANES
    grad_ln_g_full = small_sum[o1:o1 + n_ln_rows].reshape(DEPTH, 2, d)
    grad_ln_b_full = small_sum[o1 + n_ln_rows:o1 + 2 * n_ln_rows].reshape(DEPTH, 2, d)
    loss = jnp.sum(small_sum[o1 + 2 * n_ln_rows:o1 + 2 * n_ln_rows + d // LANES])
    shard = d // N_DEV
    grad_ln_g = lax.dynamic_slice_in_dim(grad_ln_g_full, dev * shard, shard, axis=2)
    grad_ln_b = lax.dynamic_slice_in_dim(grad_ln_b_full, dev * shard, shard, axis=2)
    by_seq = small_g_all[:, :n_mod].reshape(N_DEV, DEPTH, bl, 6 * d // LANES, LANES).transpose(0, 2, 1, 3, 4)
    grad_ada_b = sum_leading(by_seq.reshape(N_DEV * bl, per_seq, LANES), "sum_ada_b").reshape(DEPTH, 6 * d)
    dmod_cols = lax.dynamic_slice_in_dim(dmod_all, dev * ada_cols, ada_cols, axis=2)
    grad_ada_w = jnp.stack([mm_tn(c_act, dmod_cols[i], name=f"bwd_w_ada{i}") for i in range(DEPTH)])

    scatter_start(0)
    after = bw["dx"]
    g_mine = [None] * len(groups)
    for gi in (2, 1, 0):
        landed = exchange_wait(scatter_started[gi], after, f"scatter_group{gi}_wait", True)
        g_mine[gi] = sum_leading(landed, f"scatter_group{gi}_sum")
        after = g_mine[gi]

    def mine(nm, shape):
        return g_mine[group_of[nm]][offsets[nm]:offsets[nm] + rows_of[nm]].reshape(shape)

    def shard_t(nm, a):
        return t_last(mine(nm, t_last(a).shape))

    grads = {
        "mla_w_in": mine("mla_w_in", mla_w_in[0].shape)[None],
        "mla_g_q": grad_g_q,
        "mla_w_uq": shard_t("mla_w_uq", mla_w_uq[0])[None],
        "mla_g_kv": grad_g_kv,
        "mla_w_uk": shard_t("mla_w_uk", mla_w_uk[0])[None],
        "mla_w_uv": shard_t("mla_w_uv", mla_w_uv[0])[None],
        "mla_w_o": mine("mla_w_o", mla_w_o[0].shape)[None],
        "fox_w_in": shard_t("fox_w_in", fox_w_in[0])[None],
        "fox_b_f": grad_b_f,
        "fox_w_o": mine("fox_w_o", fox_w_o[0].shape)[None],
        "ada_w": grad_ada_w,
        "ada_b": grad_ada_b,
        "ffn_w_gate": jnp.stack([shard_t(f"gate{i}", ffn_w_gate[i]) for i in range(DEPTH)]),
        "ffn_w_up": jnp.stack([shard_t(f"up{i}", ffn_w_up[i]) for i in range(DEPTH)]),
        "ffn_w_down": jnp.stack([mine(f"down{i}", ffn_w_down[i].shape) for i in range(DEPTH)]),
        "ln_g": grad_ln_g,
        "ln_b": grad_ln_b,
    }
    weights = dict(mla_w_in=mla_w_in, mla_g_q=mla_g_q, mla_w_uq=mla_w_uq, mla_g_kv=mla_g_kv, mla_w_uk=mla_w_uk,
                   mla_w_uv=mla_w_uv, mla_w_o=mla_w_o, fox_w_in=fox_w_in, fox_b_f=fox_b_f, fox_w_o=fox_w_o,
                   ada_w=ada_w, ada_b=ada_b, ffn_w_gate=ffn_w_gate, ffn_w_up=ffn_w_up, ffn_w_down=ffn_w_down,
                   ln_g=ln_g, ln_b=ln_b)
    first = dict(mla_w_in=m_mla_w_in, mla_g_q=m_mla_g_q, mla_w_uq=m_mla_w_uq, mla_g_kv=m_mla_g_kv, mla_w_uk=m_mla_w_uk,
                 mla_w_uv=m_mla_w_uv, mla_w_o=m_mla_w_o, fox_w_in=m_fox_w_in, fox_b_f=m_fox_b_f, fox_w_o=m_fox_w_o,
                 ada_w=m_ada_w, ada_b=m_ada_b, ffn_w_gate=m_ffn_w_gate, ffn_w_up=m_ffn_w_up, ffn_w_down=m_ffn_w_down,
                 ln_g=m_ln_g, ln_b=m_ln_b)
    second = dict(mla_w_in=v_mla_w_in, mla_g_q=v_mla_g_q, mla_w_uq=v_mla_w_uq, mla_g_kv=v_mla_g_kv, mla_w_uk=v_mla_w_uk,
                  mla_w_uv=v_mla_w_uv, mla_w_o=v_mla_w_o, fox_w_in=v_fox_w_in, fox_b_f=v_fox_b_f, fox_w_o=v_fox_w_o,
                  ada_w=v_ada_w, ada_b=v_ada_b, ffn_w_gate=v_ffn_w_gate, ffn_w_up=v_ffn_w_up, ffn_w_down=v_ffn_w_down,
                  ln_g=v_ln_g, ln_b=v_ln_b)
    order = list(weights)
    g_out, d_out, m_out, v_out = [], [], [], []
    for nm in order:
        g = grads[nm].reshape(weights[nm].shape)
        delta, new_m, new_v = adamw(weights[nm], g, first[nm], second[nm], f"adamw_{nm}")
        g_out.append(g)
        d_out.append(delta)
        m_out.append(new_m)
        v_out.append(new_v)
    return (loss, grad_x, *g_out, *d_out, *m_out, *v_out)
```

```python
import functools

import jax
import jax.numpy as jnp
from jax import lax
from jax.experimental import pallas as pl
from jax.experimental.pallas import tpu as pltpu

F32 = jnp.float32
BF16 = jnp.bfloat16
LANES = 128
N_DEV = 8
VMEM_LIMIT_BYTES = 56 * 1024 * 1024

DEPTH = 2
MLA_HEADS = 8
MLA_NOPE = 128
MLA_ROPE = 64
MLA_V = 128
MLA_QR = 256
MLA_KVR = 256
ROPE_THETA = 10000.0
FOX_HEADS = 16
FOX_HD = 64
ALPHA = (2.0 * DEPTH) ** 0.25
NORM_EPS = 1e-5
ADAM_LR = 0.001
ADAM_B1 = 0.9
ADAM_B2 = 0.999
ADAM_EPS = 1e-08
ADAM_WD = 0.01
ADAM_STEP = 10

MESH_AXES = ("x", "y", "c")
MESH = pl.DeviceIdType.MESH


def _params(*sem):
    return pltpu.CompilerParams(dimension_semantics=sem, vmem_limit_bytes=VMEM_LIMIT_BYTES)


def _tile(n, cap, mult=LANES):
    if n <= cap:
        return n
    best = None
    for t in range(mult, cap + 1, mult):
        if n % t == 0:
            best = t
    assert best is not None, (n, cap, mult)
    return best


def _dot(a, b, dims):
    return lax.dot_general(a, b, (dims, ((), ())), preferred_element_type=F32)


def _nn(a, b):
    return _dot(a, b, ((1,), (0,)))


def _nt(a, b):
    return _dot(a, b, ((1,), (1,)))


def _tn(a, b):
    return _dot(a, b, ((0,), (0,)))


def _me():
    return lax.axis_index("x"), lax.axis_index("y"), lax.axis_index("c")


def all_gather(x_loc, name):
    r, c = x_loc.shape

    def body(x_ref, out_ref, send_sems, recv_sems, local_sem):
        x, y, cc = _me()
        me, sibling = (x, y, cc), (x, y, 1 - cc)
        chips = [(1 - x, y), (x, 1 - y), (1 - x, 1 - y)]

        def rows(px, py, pc):
            return out_ref.at[4 * px + 2 * py + pc]

        def copy(k, block, to, src=None):
            return pltpu.make_async_remote_copy(
                src_ref=rows(*block) if src is None else src, dst_ref=rows(*block),
                send_sem=send_sems.at[k], recv_sem=recv_sems.at[k], device_id=to, device_id_type=MESH)

        mine = pltpu.make_async_copy(x_ref, rows(*me), local_sem)
        mine.start()
        first = [copy(0, me, sibling, src=x_ref)]
        first += [copy(1 + j, me, (*chip, cc), src=x_ref) for j, chip in enumerate(chips)]
        for cp in first:
            cp.start()
        passed = [copy(4 + j, (*chip, cc), sibling) for j, chip in enumerate(chips)]
        for j, chip in enumerate(chips):
            copy(1 + j, (*chip, cc), me).wait_recv()
            passed[j].start()
        copy(0, sibling, me).wait_recv()
        for j, chip in enumerate(chips):
            copy(4 + j, (*chip, 1 - cc), me).wait_recv()
        for cp in first + passed:
            cp.wait_send()
        mine.wait()

    return pl.pallas_call(
        body, name=name,
        out_shape=jax.ShapeDtypeStruct((N_DEV, r, c), x_loc.dtype),
        in_specs=[pl.BlockSpec(memory_space=pl.ANY)],
        out_specs=pl.BlockSpec(memory_space=pl.ANY),
        scratch_shapes=[pltpu.SemaphoreType.DMA((7,)), pltpu.SemaphoreType.DMA((7,)), pltpu.SemaphoreType.DMA(())],
    )(x_loc)


HBM_SPEC = pl.BlockSpec(memory_space=pltpu.HBM)
SEM_SPEC = pl.BlockSpec(memory_space=pltpu.SEMAPHORE)
N_PEERS = N_DEV - 1


def _peer(k):
    x, y, c = _me()
    return (1 - x if k & 4 else x, 1 - y if k & 2 else y, 1 - c if k & 1 else c)


def _exchange_copies(src_ref, land_ref, send_sems, recv_sems, scatter):
    x, y, c = _me()
    mine = 4 * x + 2 * y + c
    copies = []
    for k in range(1, N_DEV):
        px, py, pc = _peer(k)
        src = src_ref.at[4 * px + 2 * py + pc] if scatter else src_ref
        copies.append(pltpu.make_async_remote_copy(
            src_ref=src, dst_ref=land_ref.at[mine], send_sem=send_sems.at[k - 1], recv_sem=recv_sems.at[k - 1],
            device_id=(px, py, pc), device_id_type=MESH))
    return copies


def exchange_start(src, land, name, scatter):
    def body(src_ref, land_ref, send_sems, recv_sems, src_thru, land_thru, token):
        for cp in _exchange_copies(src_ref, land_ref, send_sems, recv_sems, scatter):
            cp.start()
        token[...] = jnp.zeros_like(token)

    return pl.pallas_call(
        body, name=name,
        out_shape=(pltpu.SemaphoreType.DMA((N_PEERS,)), pltpu.SemaphoreType.DMA((N_PEERS,)),
                   pltpu.HBM(src.shape, src.dtype), pltpu.HBM(land.shape, land.dtype),
                   jax.ShapeDtypeStruct((8, LANES), F32)),
        in_specs=(HBM_SPEC, HBM_SPEC),
        out_specs=(SEM_SPEC, SEM_SPEC, HBM_SPEC, HBM_SPEC, pl.BlockSpec(memory_space=pltpu.VMEM)),
        input_output_aliases={0: 2, 1: 3},
        compiler_params=pltpu.CompilerParams(has_side_effects=pltpu.SideEffectType.DATAFLOW_SIDE_EFFECTING),
    )(pltpu.with_memory_space_constraint(src, pltpu.HBM), pltpu.with_memory_space_constraint(land, pltpu.HBM))


def exchange_wait(started, after, name, scatter):
    send_sems, recv_sems, src_thru, land_thru, _ = started

    def body(src_ref, land_ref, send_sems, recv_sems, after_ref, src_dead, got_ref):
        for cp in _exchange_copies(src_ref, land_ref, send_sems, recv_sems, scatter):
            cp.wait_send()
            cp.wait_recv()

    return pl.pallas_call(
        body, name=name,
        out_shape=(pltpu.HBM(src_thru.shape, src_thru.dtype), pltpu.HBM(land_thru.shape, land_thru.dtype)),
        in_specs=(HBM_SPEC, HBM_SPEC, SEM_SPEC, SEM_SPEC, pl.BlockSpec(memory_space=pl.ANY)),
        out_specs=(HBM_SPEC, HBM_SPEC), input_output_aliases={0: 0, 1: 1},
        compiler_params=pltpu.CompilerParams(has_side_effects=pltpu.SideEffectType.DATAFLOW_SIDE_EFFECTING),
    )(src_thru, land_thru, send_sems, recv_sems, after)[1]


def after_token(small, started):
    return small + started[4][0, 0]


def sum_leading(x, name):
    n, r, c = x.shape
    tr = _tile(r, 512, 16)

    def body(x_ref, o_ref):
        acc = x_ref[0].astype(F32)
        for k in range(1, n):
            acc = acc + x_ref[k].astype(F32)
        o_ref[...] = acc

    return pl.pallas_call(
        body, name=name,
        out_shape=jax.ShapeDtypeStruct((r, c), F32),
        grid=(r // tr,),
        in_specs=[pl.BlockSpec((n, tr, c), lambda i: (0, i, 0))],
        out_specs=pl.BlockSpec((tr, c), lambda i: (i, 0)),
        compiler_params=_params("arbitrary"),
    )(x)


MM_VMEM_BUDGET = 36 * 1024 * 1024
GRID_STEP_AS_BYTES = 1 << 20


def _mm_tiles(m, n, a_row_bytes, b_col_bytes, out_bytes):
    tms = [c for c in (2048, 1024, 512, 256, 128, 64, 32, 16, 8) if m % c == 0] or [m]
    tns = [c for c in range(LANES, min(n, 2048) + 1, LANES) if n % c == 0] or [n]
    best = None
    for tm in tms:
        for tn in tns:
            vmem = 2 * (tm * a_row_bytes + tn * b_col_bytes) + 2 * tm * tn * out_bytes + tm * tn * 4
            if vmem > MM_VMEM_BUDGET:
                continue
            steps = (m // tm) * (n // tn)
            cost = steps * GRID_STEP_AS_BYTES + (m // tm) * n * b_col_bytes + m * a_row_bytes
            if best is None or cost < best[0]:
                best = (cost, tm, tn)
    assert best is not None, (m, n, a_row_bytes, b_col_bytes)
    return best[1], best[2]


def mm(pairs, *, trans_b, out_dtype, name, out_slab=False, bias=None):
    a0 = pairs[0][0]
    m = a0.shape[1] if a0.ndim == 3 else a0.shape[0]
    n = pairs[0][1].shape[0] if trans_b else pairs[0][1].shape[1]
    a_row_bytes = sum((b.shape[1] if trans_b else b.shape[0]) * a.dtype.itemsize for a, b in pairs)
    b_col_bytes = sum((b.shape[1] if trans_b else b.shape[0]) * b.dtype.itemsize for _, b in pairs)
    tm, tn = _mm_tiles(m, n, a_row_bytes, b_col_bytes, jnp.dtype(out_dtype).itemsize)
    slabs = [a.ndim == 3 for a, _ in pairs]
    n_pairs = len(pairs)

    def body(*refs):
        o_ref = refs[-1]
        acc = bias_ref = None
        if bias is not None:
            bias_ref = refs[2 * n_pairs]
        for i in range(n_pairs):
            a_ref, b_ref = refs[2 * i], refs[2 * i + 1]
            if slabs[i]:
                a = jnp.concatenate([a_ref[s].astype(BF16) for s in range(a_ref.shape[0])], axis=1)
            else:
                a = a_ref[...].astype(BF16)
            b = b_ref[...].astype(BF16)
            part = _nt(a, b) if trans_b else _nn(a, b)
            acc = part if acc is None else acc + part
        if bias_ref is not None:
            acc = acc + bias_ref[...]
        if out_slab:
            for s in range(tn // LANES):
                o_ref[s] = acc[:, s * LANES:(s + 1) * LANES].astype(out_dtype)
        else:
            o_ref[...] = acc.astype(out_dtype)

    in_specs, args = [], []
    for (a, b), slab in zip(pairs, slabs):
        if slab:
            in_specs.append(pl.BlockSpec((a.shape[0], tm, LANES), lambda i, j: (0, i, 0)))
        else:
            in_specs.append(pl.BlockSpec((tm, a.shape[1]), lambda i, j: (i, 0)))
        if trans_b:
            in_specs.append(pl.BlockSpec((tn, b.shape[1]), lambda i, j: (j, 0)))
        else:
            in_specs.append(pl.BlockSpec((b.shape[0], tn), lambda i, j: (0, j)))
        args += [a, b]
    if bias is not None:
        in_specs.append(pl.BlockSpec((1, tn), lambda i, j: (0, j)))
        args.append(bias)
    if out_slab:
        out_shape = jax.ShapeDtypeStruct((n // LANES, m, LANES), out_dtype)
        out_spec = pl.BlockSpec((tn // LANES, tm, LANES), lambda i, j: (j, i, 0))
    else:
        out_shape = jax.ShapeDtypeStruct((m, n), out_dtype)
        out_spec = pl.BlockSpec((tm, tn), lambda i, j: (i, j))
    return pl.pallas_call(
        body, name=name, out_shape=out_shape, grid=(m // tm, n // tn),
        in_specs=in_specs, out_specs=out_spec,
        compiler_params=_params("arbitrary", "arbitrary"),
    )(*args)


def mm_tn(a, b, *, name, tk_cap=1536, tn_cap=1024, tm_cap=512):
    slab = a.ndim == 3
    m = a.shape[1] if slab else a.shape[0]
    k = a.shape[0] * LANES if slab else a.shape[1]
    n = b.shape[1]
    tk = _tile(k, tk_cap)
    tn = _tile(n, tn_cap)
    tm = _tile(m, tm_cap, 8)

    def body(a_ref, b_ref, o_ref):
        @pl.when(pl.program_id(2) == 0)
        def _():
            o_ref[...] = jnp.zeros_like(o_ref)

        bb = b_ref[...].astype(BF16)
        if slab:
            for s in range(tk // LANES):
                o_ref[s * LANES:(s + 1) * LANES, :] += _tn(a_ref[s].astype(BF16), bb)
        else:
            o_ref[...] += _tn(a_ref[...].astype(BF16), bb)

    if slab:
        a_spec = pl.BlockSpec((tk // LANES, tm, LANES), lambda i, j, t: (i, t, 0))
    else:
        a_spec = pl.BlockSpec((tm, tk), lambda i, j, t: (t, i))
    return pl.pallas_call(
        body, name=name, out_shape=jax.ShapeDtypeStruct((k, n), F32), grid=(k // tk, n // tn, m // tm),
        in_specs=[a_spec, pl.BlockSpec((tm, tn), lambda i, j, t: (t, j))],
        out_specs=pl.BlockSpec((tk, tn), lambda i, j, t: (i, j)),
        compiler_params=_params("arbitrary", "arbitrary", "arbitrary"),
    )(a, b)


def _row_spec(d, k):
    return pl.BlockSpec((1, 1, d), lambda b, i: (6 * b + k, 0, 0))


def modulate(x, mod, k_shift, k_scale, bl, name):
    t, d = x.shape
    s = t // bl
    tm = _tile(s, 512, 8)
    nt = s // tm

    def body(x_ref, sh_ref, sc_ref, o_ref):
        o_ref[...] = (x_ref[...] * (1.0 + sc_ref[0]) + sh_ref[0]).astype(BF16)

    return pl.pallas_call(
        body, name=name, out_shape=jax.ShapeDtypeStruct((t, d), BF16), grid=(bl, nt),
        in_specs=[pl.BlockSpec((tm, d), lambda b, i: (b * nt + i, 0)), _row_spec(d, k_shift), _row_spec(d, k_scale)],
        out_specs=pl.BlockSpec((tm, d), lambda b, i: (b * nt + i, 0)),
        compiler_params=_params("arbitrary", "arbitrary"),
    )(x, mod, mod)


def _layer_norm_stats(r):
    mu = jnp.mean(r, axis=-1, keepdims=True)
    rc = r - mu
    var = jnp.mean(rc * rc, axis=-1, keepdims=True)
    rstd = lax.rsqrt(var + NORM_EPS)
    return rc * rstd, rstd


def residual_layer_norm(x, y, mod, k_gate, g, b, bl, name, next_mod=None):
    t, d = x.shape
    s = t // bl
    tm = _tile(s, 256, 8)
    nt = s // tm
    has_next = next_mod is not None

    def body(*refs):
        x_ref, y_ref, gt_ref, g_ref, b_ref = refs[:5]
        rest = refs[5:]
        if has_next:
            sh_ref, sc_ref, o_ref, r_ref, u_ref = rest
        else:
            o_ref, r_ref = rest
        r = ALPHA * x_ref[...] + (1.0 + gt_ref[0]) * y_ref[...]
        xhat, _ = _layer_norm_stats(r)
        out = xhat * g_ref[...] + b_ref[...]
        o_ref[...] = out
        r_ref[...] = r
        if has_next:
            u_ref[...] = (out * (1.0 + sc_ref[0]) + sh_ref[0]).astype(BF16)

    tok = pl.BlockSpec((tm, d), lambda bb, i: (bb * nt + i, 0))
    vec = pl.BlockSpec((1, d), lambda bb, i: (0, 0))
    in_specs = [tok, tok, _row_spec(d, k_gate), vec, vec]
    args = [x, y, mod, g, b]
    out_shape = [jax.ShapeDtypeStruct((t, d), F32), jax.ShapeDtypeStruct((t, d), F32)]
    out_specs = [tok, tok]
    if has_next:
        in_specs += [_row_spec(d, next_mod[0]), _row_spec(d, next_mod[1])]
        args += [mod if len(next_mod) == 2 else next_mod[2]] * 2
        out_shape.append(jax.ShapeDtypeStruct((t, d), BF16))
        out_specs.append(tok)
    return pl.pallas_call(
        body, name=name, out_shape=out_shape, grid=(bl, nt), in_specs=in_specs, out_specs=out_specs,
        compiler_params=_params("arbitrary", "arbitrary"),
    )(*args)


def loss_head(xo, target, name):
    t, d = xo.shape
    tm = _tile(t, 512, 8)

    def body(x_ref, t_ref, l_ref, dx_ref):
        @pl.when(pl.program_id(0) == 0)
        def _():
            l_ref[...] = jnp.zeros_like(l_ref)

        e = x_ref[...] - t_ref[...]
        l_ref[...] += jnp.sum(e * e, axis=0, keepdims=True) * (0.5 / d)
        dx_ref[...] = e * (1.0 / d)

    tok = pl.BlockSpec((tm, d), lambda i: (i, 0))
    return pl.pallas_call(
        body, name=name,
        out_shape=[jax.ShapeDtypeStruct((1, d), F32), jax.ShapeDtypeStruct((t, d), F32)],
        grid=(t // tm,), in_specs=[tok, tok],
        out_specs=[pl.BlockSpec((1, d), lambda i: (0, 0)), tok],
        compiler_params=_params("arbitrary"),
    )(xo, target)


def sublayer_backward(d_a, bl, name, *, du=None, scale=None, x_in=None, ln=None):
    t, d = d_a.shape
    s = t // bl
    tm = _tile(s, 256, 8)
    nt = s // tm
    has_mod = du is not None
    has_ln = ln is not None
    assert has_mod or has_ln
    assert has_ln or x_in is not None

    def body(*refs):
        refs = list(refs)
        da_ref = refs.pop(0)
        if has_mod:
            du_ref, sc_ref = refs.pop(0), refs.pop(0)
        if has_ln:
            r_ref, y_ref, g_ref, b_ref, gt_ref = (refs.pop(0) for _ in range(5))
        elif has_mod:
            xin_ref = refs.pop(0)
        dx_ref = refs.pop(0)
        if has_ln:
            dy_ref, dg_ref, db_ref, dgt_ref = (refs.pop(0) for _ in range(4))
        if has_mod:
            dsc_ref, dsh_ref = refs.pop(0), refs.pop(0)
        first_tile = pl.program_id(1) == 0
        first_step = jnp.logical_and(pl.program_id(0) == 0, first_tile)

        dout = da_ref[...]
        if has_ln:
            xhat, rstd = _layer_norm_stats(r_ref[...])
        if has_mod:
            duv = du_ref[...]
            dout = dout + duv * (1.0 + sc_ref[0])
            xin = xhat * g_ref[...] + b_ref[...] if has_ln else xin_ref[...]

            @pl.when(first_tile)
            def _():
                dsc_ref[...] = jnp.zeros_like(dsc_ref)
                dsh_ref[...] = jnp.zeros_like(dsh_ref)

            dsc_ref[0] += jnp.sum(duv * xin, axis=0, keepdims=True)
            dsh_ref[0] += jnp.sum(duv, axis=0, keepdims=True)
        if not has_ln:
            dx_ref[...] = dout
            return

        @pl.when(first_step)
        def _():
            dg_ref[...] = jnp.zeros_like(dg_ref)
            db_ref[...] = jnp.zeros_like(db_ref)

        @pl.when(first_tile)
        def _():
            dgt_ref[...] = jnp.zeros_like(dgt_ref)

        dg_ref[...] += jnp.sum(dout * xhat, axis=0, keepdims=True)
        db_ref[...] += jnp.sum(dout, axis=0, keepdims=True)
        dxh = dout * g_ref[...]
        dr = rstd * (dxh - jnp.mean(dxh, axis=-1, keepdims=True) - xhat * jnp.mean(dxh * xhat, axis=-1, keepdims=True))
        dx_ref[...] = ALPHA * dr
        dy_ref[...] = ((1.0 + gt_ref[0]) * dr).astype(BF16)
        dgt_ref[0] += jnp.sum(dr * y_ref[...], axis=0, keepdims=True)

    tok = pl.BlockSpec((tm, d), lambda bb, i: (bb * nt + i, 0))
    vec = pl.BlockSpec((1, d), lambda bb, i: (0, 0))
    seq = pl.BlockSpec((1, 1, d), lambda bb, i: (bb, 0, 0))
    in_specs, args = [tok], [d_a]
    if has_mod:
        in_specs += [tok, _row_spec(d, scale[1])]
        args += [du, scale[0]]
    if has_ln:
        r, y, g, b, gate = ln
        in_specs += [tok, tok, vec, vec, _row_spec(d, gate[1])]
        args += [r, y, g, b, gate[0]]
    elif has_mod:
        in_specs.append(tok)
        args.append(x_in)
    names = ["dx"]
    out_shape, out_specs = [jax.ShapeDtypeStruct((t, d), F32)], [tok]
    if has_ln:
        names += ["dy", "dg", "db", "dgate"]
        out_shape += [jax.ShapeDtypeStruct((t, d), BF16), jax.ShapeDtypeStruct((1, d), F32),
                      jax.ShapeDtypeStruct((1, d), F32), jax.ShapeDtypeStruct((bl, 1, d), F32)]
        out_specs += [tok, vec, vec, seq]
    if has_mod:
        names += ["dscale", "dshift"]
        out_shape += [jax.ShapeDtypeStruct((bl, 1, d), F32)] * 2
        out_specs += [seq, seq]
    outs = pl.pallas_call(
        body, name=name, out_shape=out_shape, grid=(bl, nt), in_specs=in_specs, out_specs=out_specs,
        compiler_params=_params("arbitrary", "arbitrary"),
    )(*args)
    return dict(zip(names, outs))


def _silu(a):
    return a * jax.nn.sigmoid(a)


def silu_rows(a, name):
    def body(a_ref, o_ref):
        o_ref[...] = _silu(a_ref[...]).astype(BF16)

    return pl.pallas_call(body, name=name, out_shape=jax.ShapeDtypeStruct(a.shape, BF16))(a)


def swiglu_forward(a, b, name):
    t, f = a.shape
    tm, tf = _tile(t, 512, 8), _tile(f, 1536)

    def body(a_ref, b_ref, h_ref):
        h_ref[...] = (_silu(a_ref[...]) * b_ref[...]).astype(BF16)

    spec = pl.BlockSpec((tm, tf), lambda i, j: (i, j))
    return pl.pallas_call(
        body, name=name, out_shape=jax.ShapeDtypeStruct((t, f), BF16), grid=(t // tm, f // tf),
        in_specs=[spec, spec], out_specs=spec, compiler_params=_params("arbitrary", "arbitrary"),
    )(a, b)


def swiglu_backward(dh, a, b, name):
    t, f = a.shape
    tm, tf = _tile(t, 512, 8), _tile(f, 1536)

    def body(dh_ref, a_ref, b_ref, da_ref, db_ref):
        av = a_ref[...]
        sig = jax.nn.sigmoid(av)
        dhv = dh_ref[...]
        da_ref[...] = (dhv * b_ref[...] * (sig * (1.0 + av * (1.0 - sig)))).astype(BF16)
        db_ref[...] = (dhv * (av * sig)).astype(BF16)

    spec = pl.BlockSpec((tm, tf), lambda i, j: (i, j))
    return pl.pallas_call(
        body, name=name, out_shape=[jax.ShapeDtypeStruct((t, f), BF16)] * 2, grid=(t // tm, f // tf),
        in_specs=[spec, spec, spec], out_specs=[spec, spec], compiler_params=_params("arbitrary", "arbitrary"),
    )(dh, a, b)


def rope_tables(pos, inv_freq, sign, name):
    t = pos.shape[0]
    tm = _tile(t, 512, 8)

    def body(p_ref, f_ref, s_ref, c_out, s_out):
        ang = p_ref[...] * f_ref[...]
        c_out[...] = jnp.cos(ang)
        s_out[...] = jnp.sin(ang) * s_ref[...]

    vec = pl.BlockSpec((1, LANES), lambda i: (0, 0))
    tab = pl.BlockSpec((tm, LANES), lambda i: (i, 0))
    return pl.pallas_call(
        body, name=name, out_shape=[jax.ShapeDtypeStruct((t, LANES), F32)] * 2, grid=(t // tm,),
        in_specs=[pl.BlockSpec((tm, 1), lambda i: (i, 0)), vec, vec], out_specs=[tab, tab],
        compiler_params=_params("arbitrary"),
    )(pos, inv_freq, sign)


def _rot_half(v):
    lane = lax.broadcasted_iota(jnp.int32, v.shape, v.ndim - 1)
    up = pltpu.roll(v, LANES - MLA_ROPE // 2, v.ndim - 1)
    down = pltpu.roll(v, MLA_ROPE // 2, v.ndim - 1)
    return jnp.where(lane % MLA_ROPE < MLA_ROPE // 2, up, down)


def _rope(v, cos, sin_signed):
    return v * cos + _rot_half(v) * sin_signed


def _rope_transposed(dv, cos, sin_signed):
    return dv * cos + _rot_half(dv * sin_signed)


def rope_slabs(v, cos, sin_signed, out_dtype, name, transposed=False):
    ns, t, _ = v.shape
    tm = _tile(t, 512, 8)
    fn = _rope_transposed if transposed else _rope

    def body(v_ref, c_ref, s_ref, o_ref):
        o_ref[0] = fn(v_ref[0].astype(F32), c_ref[...], s_ref[...]).astype(out_dtype)

    tab = pl.BlockSpec((tm, LANES), lambda j, i: (i, 0))
    spec = pl.BlockSpec((1, tm, LANES), lambda j, i: (j, i, 0))
    return pl.pallas_call(
        body, name=name, out_shape=jax.ShapeDtypeStruct(v.shape, out_dtype), grid=(ns, t // tm),
        in_specs=[spec, tab, tab], out_specs=spec, compiler_params=_params("arbitrary", "arbitrary"),
    )(v, cos, sin_signed)


def _rms(x):
    rinv = lax.rsqrt(jnp.mean(x * x, axis=-1, keepdims=True) + NORM_EPS)
    return x * rinv, rinv


def mla_latents_forward(h_in, g_q, g_kv, cos, sin_signed, name):
    t = h_in.shape[0]
    tm = _tile(t, 512, 8)

    def body(h_ref, gq_ref, gkv_ref, c_ref, s_ref, cq_ref, ckv_ref, kr_ref):
        cq_ref[...] = (_rms(h_ref[:, 0:MLA_QR])[0] * gq_ref[...]).astype(BF16)
        ckv_ref[...] = (_rms(h_ref[:, MLA_QR:MLA_QR + MLA_KVR])[0] * gkv_ref[...]).astype(BF16)
        kr_ref[...] = _rope(h_ref[:, MLA_QR + MLA_KVR:], c_ref[...], s_ref[...]).astype(BF16)

    def tok(w):
        return pl.BlockSpec((tm, w), lambda i: (i, 0))

    def vec(w):
        return pl.BlockSpec((1, w), lambda i: (0, 0))

    return pl.pallas_call(
        body, name=name,
        out_shape=[jax.ShapeDtypeStruct((t, MLA_QR), BF16), jax.ShapeDtypeStruct((t, MLA_KVR), BF16),
                   jax.ShapeDtypeStruct((t, LANES), BF16)],
        grid=(t // tm,),
        in_specs=[tok(h_in.shape[1]), vec(MLA_QR), vec(MLA_KVR), tok(LANES), tok(LANES)],
        out_specs=[tok(MLA_QR), tok(MLA_KVR), tok(LANES)],
        compiler_params=_params("arbitrary"),
    )(h_in, g_q, g_kv, cos, sin_signed)


def mla_latents_backward(h_in, dcq, dckv, dkr, g_q, g_kv, cos, sin_signed, name):
    t, w = h_in.shape
    tm = _tile(t, 512, 8)

    def body(h_ref, dcq_ref, dckv_ref, dkr_ref, gq_ref, gkv_ref, c_ref, s_ref, dh_ref, dgq_ref, dgkv_ref):
        @pl.when(pl.program_id(0) == 0)
        def _():
            dgq_ref[...] = jnp.zeros_like(dgq_ref)
            dgkv_ref[...] = jnp.zeros_like(dgkv_ref)

        def rms_bwd(x, dc, g_ref, dg_ref):
            xn, rinv = _rms(x)
            dg_ref[...] += jnp.sum(dc * xn, axis=0, keepdims=True)
            dxn = dc * g_ref[...]
            return rinv * (dxn - xn * jnp.mean(dxn * xn, axis=-1, keepdims=True))

        dq = rms_bwd(h_ref[:, 0:MLA_QR], dcq_ref[...], gq_ref, dgq_ref)
        dkv = rms_bwd(h_ref[:, MLA_QR:MLA_QR + MLA_KVR], dckv_ref[...], gkv_ref, dgkv_ref)
        dr = _rope_transposed(dkr_ref[...], c_ref[...], s_ref[...])
        dh_ref[...] = jnp.concatenate([dq, dkv, dr], axis=1).astype(BF16)

    def tok(ww):
        return pl.BlockSpec((tm, ww), lambda i: (i, 0))

    def vec(ww):
        return pl.BlockSpec((1, ww), lambda i: (0, 0))

    return pl.pallas_call(
        body, name=name,
        out_shape=[jax.ShapeDtypeStruct((t, w), BF16), jax.ShapeDtypeStruct((1, MLA_QR), F32),
                   jax.ShapeDtypeStruct((1, MLA_KVR), F32)],
        grid=(t // tm,),
        in_specs=[tok(w), tok(MLA_QR), tok(MLA_KVR), tok(LANES), vec(MLA_QR), vec(MLA_KVR), tok(LANES), tok(LANES)],
        out_specs=[tok(w), vec(MLA_QR), vec(MLA_KVR)],
        compiler_params=_params("arbitrary"),
    )(h_in, dcq, dckv, dkr, g_q, g_kv, cos, sin_signed)


def _tri(n, lower):
    r = lax.broadcasted_iota(jnp.int32, (n, n), 0)
    c = lax.broadcasted_iota(jnp.int32, (n, n), 1)
    return jnp.where(r >= c if lower else r <= c, 1.0, 0.0).astype(F32)


def _dot_exact(tri, v):
    hi = v.astype(BF16)
    mid = (v - hi.astype(F32)).astype(BF16)
    lo = (v - hi.astype(F32) - mid.astype(F32)).astype(BF16)
    t = tri.astype(BF16)
    return _nn(t, hi) + _nn(t, mid) + _nn(t, lo)


def fox_gate_forward(z, b_f, bl, name):
    t = z.shape[0]
    s = t // bl
    ch = LANES
    n_ch = s // ch

    def body(z_ref, b_ref, f_ref, fs_ref):
        tri = _tri(ch, True)
        carry = jnp.zeros((1, LANES), F32)
        for k in range(n_ch):
            x = z_ref[k * ch:(k + 1) * ch, :] + b_ref[...]
            logf = jnp.minimum(x, 0.0) - jnp.log(1.0 + jnp.exp(-jnp.abs(x)))
            cs = _dot_exact(tri, logf) + carry
            carry = cs[ch - 1:ch, :]
            f_ref[k * ch:(k + 1) * ch, :] = cs
            for h in range(FOX_HEADS):
                fs_ref[h, k * ch:(k + 1) * ch, :] = jnp.broadcast_to(cs[:, h:h + 1], (ch, LANES))

    return pl.pallas_call(
        body, name=name,
        out_shape=[jax.ShapeDtypeStruct((t, LANES), F32), jax.ShapeDtypeStruct((FOX_HEADS, t, LANES), F32)],
        grid=(bl,),
        in_specs=[pl.BlockSpec((s, LANES), lambda b: (b, 0)), pl.BlockSpec((1, LANES), lambda b: (0, 0))],
        out_specs=[pl.BlockSpec((s, LANES), lambda b: (b, 0)),
                   pl.BlockSpec((FOX_HEADS, s, LANES), lambda b: (0, b, 0))],
        compiler_params=_params("arbitrary"),
    )(z, b_f)


def fox_gate_backward(z, b_f, df, bl, name):
    t = z.shape[0]
    s = t // bl
    ch = LANES
    n_ch = s // ch

    def body(z_ref, b_ref, df_ref, dz_ref, db_ref):
        @pl.when(pl.program_id(0) == 0)
        def _():
            db_ref[...] = jnp.zeros_like(db_ref)

        tri = _tri(ch, False)
        carry = jnp.zeros((1, LANES), F32)
        for k in reversed(range(n_ch)):
            cs = _dot_exact(tri, df_ref[k * ch:(k + 1) * ch, :]) + carry
            carry = cs[0:1, :]
            x = z_ref[k * ch:(k + 1) * ch, :] + b_ref[...]
            dz = cs * (1.0 - jax.nn.sigmoid(x))
            dz_ref[k * ch:(k + 1) * ch, :] = dz
            db_ref[...] += jnp.sum(dz, axis=0, keepdims=True)

    tok = pl.BlockSpec((s, LANES), lambda b: (b, 0))
    vec = pl.BlockSpec((1, LANES), lambda b: (0, 0))
    return pl.pallas_call(
        body, name=name,
        out_shape=[jax.ShapeDtypeStruct((t, LANES), F32), jax.ShapeDtypeStruct((1, LANES), F32)],
        grid=(bl,), in_specs=[tok, vec, tok], out_specs=[tok, vec],
        compiler_params=_params("arbitrary"),
    )(z, b_f, df)


NEG_INF = float("-inf")


def _attn_tiles(s):
    return _tile(s, 512, 8)


def attention_forward(kind, ops, bl, scale, name):
    fox = kind == "fox"
    if fox:
        qkv, fq, fk = ops
        t = qkv.shape[1]
        n_pair = FOX_HEADS // 2
    else:
        qn, qr, kn, kr, v = ops
        t = qn.shape[1]
        n_pair = MLA_HEADS // 2
    s = t // bl
    tq = _attn_tiles(s)
    nq = s // tq
    half = LANES // 2

    def body(*refs):
        if fox:
            q_ref, k_ref, v_ref, fq_ref, fk_ref, o_ref, lse_ref, o32_ref = refs
        else:
            qn_ref, qr_ref, kn_ref, kr_ref, v_ref, o_ref, lse_ref = refs
        i = pl.program_id(2)
        row = lax.broadcasted_iota(jnp.int32, (tq, tq), 0)
        col = lax.broadcasted_iota(jnp.int32, (tq, tq), 1)
        heads = []
        for e in range(2):
            sl = slice(e * half, (e + 1) * half)
            if fox:
                heads.append((sl, q_ref[0, :, sl], None))
            else:
                heads.append((sl, qn_ref[e], qr_ref[0, :, sl]))
        dv = half if fox else LANES

        def wide(stat):
            return jnp.concatenate([stat] * (tq // LANES), axis=1)

        def step(j, carry, masked):
            rows = pl.ds(pl.multiple_of(j * tq, tq), tq)
            new = []
            for e, (sl, qa, qb) in enumerate(heads):
                m, l, acc = carry[e]
                if fox:
                    sc = _nt(qa, k_ref[0, rows, sl]) * scale + wide(fq_ref[e]) - fk_ref[0, j, e:e + 1, :]
                    vv = v_ref[0, rows, sl]
                else:
                    sc = (_nt(qa, kn_ref[e, rows, :]) + _nt(qb, kr_ref[rows, 0:half])) * scale
                    vv = v_ref[e, rows, :]
                if masked:
                    sc = jnp.where(row >= col, sc, NEG_INF)
                m_new = jnp.maximum(m, jnp.max(sc, axis=1, keepdims=True))
                p = jnp.exp(sc - m_new)
                a = jnp.exp(m - m_new)
                l = a * l + jnp.sum(p, axis=1, keepdims=True)
                p_hi = p.astype(BF16)
                acc = a * acc + _nn(p_hi, vv)
                if fox:
                    acc = acc + _nn((p - p_hi.astype(F32)).astype(BF16), vv)
                new.append((m_new, l, acc))
            return tuple(new)

        init = (jnp.full((tq, 1), NEG_INF, F32), jnp.zeros((tq, 1), F32), jnp.zeros((tq, dv), F32))
        carry = step(i, (init, init), True)
        carry = lax.fori_loop(0, i, lambda j, c: step(j, c, False), carry)
        outs = [acc / l for _, l, acc in carry]
        for e, (m, l, _) in enumerate(carry):
            lse_ref[e] = jnp.broadcast_to(m + jnp.log(l), (tq, LANES))
        if fox:
            o32 = jnp.concatenate(outs, axis=1)
            o32_ref[0] = o32
            o_ref[0] = o32.astype(BF16)
        else:
            o_ref[0] = outs[0].astype(BF16)
            o_ref[1] = outs[1].astype(BF16)

    def q_idx(b, g, i):
        return (g, b * nq + i, 0)

    if fox:
        nk = fk.shape[1]
        in_specs = [pl.BlockSpec((1, tq, LANES), q_idx),
                    pl.BlockSpec((1, s, LANES), lambda b, g, i: (n_pair + g, b, 0)),
                    pl.BlockSpec((1, s, LANES), lambda b, g, i: (2 * n_pair + g, b, 0)),
                    pl.BlockSpec((2, tq, LANES), q_idx),
                    pl.BlockSpec((1, nk, 8, tq), lambda b, g, i: (b * n_pair + g, 0, 0, 0))]
        args = [qkv, qkv, qkv, fq, fk]
        o_spec = pl.BlockSpec((1, tq, LANES), q_idx)
    else:
        in_specs = [pl.BlockSpec((2, tq, LANES), q_idx),
                    pl.BlockSpec((1, tq, LANES), q_idx),
                    pl.BlockSpec((2, s, LANES), lambda b, g, i: (g, b, 0)),
                    pl.BlockSpec((s, LANES), lambda b, g, i: (b, 0)),
                    pl.BlockSpec((2, s, LANES), lambda b, g, i: (g, b, 0))]
        args = [qn, qr, kn, kr, v]
        o_spec = pl.BlockSpec((2, tq, LANES), q_idx)
    out_shape = [jax.ShapeDtypeStruct((8, t, LANES), BF16), jax.ShapeDtypeStruct((2 * n_pair, t, LANES), F32)]
    out_specs = [o_spec, pl.BlockSpec((2, tq, LANES), q_idx)]
    if fox:
        out_shape.append(jax.ShapeDtypeStruct((8, t, LANES), F32))
        out_specs.append(o_spec)
    outs = pl.pallas_call(
        body, name=name, out_shape=out_shape, grid=(bl, n_pair, nq), in_specs=in_specs, out_specs=out_specs,
        compiler_params=_params("arbitrary", "arbitrary", "arbitrary"),
    )(*args)
    return (outs[0], outs[1], outs[2] if fox else outs[0])


def attention_backward(kind, ops, o, do, lse, bl, scale, name):
    fox = kind == "fox"
    if fox:
        qkv, fq, fk = ops
        t = qkv.shape[1]
        n_pair = FOX_HEADS // 2
    else:
        qn, qr, kn, kr, v = ops
        t = qn.shape[1]
        n_pair = MLA_HEADS // 2
    s = t // bl
    tq = _attn_tiles(s)
    nq = s // tq
    half = LANES // 2

    def body(*refs):
        if fox:
            (q_ref, k_ref, v_ref, fq_ref, fk_ref, o_ref, do_ref, lse_ref,
             dq_ref, dk_ref, dv_ref, dfk_ref, delta_scr, qt_scr, dot_scr) = refs
        else:
            (qn_ref, qr_ref, kn_ref, kr_ref, v_ref, o_ref, do_ref, lse_ref,
             dqn_ref, dqr_ref, dkn_ref, dv_ref, dkr_ref, delta_scr, qt_scr, qrt_scr, dot_scr) = refs
        g, j = pl.program_id(1), pl.program_id(2)
        row = lax.broadcasted_iota(jnp.int32, (tq, tq), 0)
        col = lax.broadcasted_iota(jnp.int32, (tq, tq), 1)
        krows = pl.ds(pl.multiple_of(j * tq, tq), tq)

        def transposed(v):
            return v.astype(F32).T.astype(BF16)

        def wide(stat):
            return jnp.concatenate([stat] * (tq // LANES), axis=1)

        @pl.when(j == 0)
        def _():
            if fox:
                dq_ref[...] = jnp.zeros_like(dq_ref)
            else:
                dqn_ref[...] = jnp.zeros_like(dqn_ref)
                dqr_ref[...] = jnp.zeros_like(dqr_ref)
            for ii in range(nq):
                rws = slice(ii * tq, (ii + 1) * tq)
                deltas = []
                if fox:
                    prod = do_ref[0, rws, :].astype(F32) * o_ref[0, rws, :].astype(F32)
                    for e in range(2):
                        deltas.append(jnp.sum(prod[:, e * half:(e + 1) * half], axis=1, keepdims=True))
                    qt_scr[ii] = transposed(q_ref[0, rws, :])
                    dot_scr[ii] = transposed(do_ref[0, rws, :])
                else:
                    for e in range(2):
                        prod = do_ref[e, rws, :].astype(F32) * o_ref[e, rws, :].astype(F32)
                        deltas.append(jnp.sum(prod, axis=1, keepdims=True))
                        qt_scr[e, ii] = transposed(qn_ref[e, rws, :])
                        dot_scr[e, ii] = transposed(do_ref[e, rws, :])
                    qrt_scr[ii] = transposed(qr_ref[0, rws, :])
                for e in range(2):
                    delta_scr[e, rws, :] = jnp.broadcast_to(deltas[e], (tq, LANES))

        if fox:
            dfk_ref[...] = jnp.zeros_like(dfk_ref)
        else:
            @pl.when(jnp.logical_and(g == 0, j == 0))
            def _():
                dkr_ref[...] = jnp.zeros_like(dkr_ref)

        heads = []
        for e in range(2):
            sl = slice(e * half, (e + 1) * half)
            if fox:
                heads.append((sl, k_ref[0, :, sl], v_ref[0, :, sl], fk_ref[0, 0, e:e + 1, :]))
            else:
                heads.append((sl, kn_ref[e], v_ref[e], kr_ref[krows, 0:half]))
        dk_w = dv_w = half if fox else LANES

        def step(i, carry, masked):
            rows = pl.ds(pl.multiple_of(i * tq, tq), tq)
            new = []
            for e, (sl, k_e, v_e, x_e) in enumerate(heads):
                dk_acc, dv_acc, last = carry[e]
                if fox:
                    do_i = do_ref[0, rows, sl]
                    sc = _nt(q_ref[0, rows, sl], k_e) * scale + wide(fq_ref[e, rows, :]) - x_e
                else:
                    do_i = do_ref[e, rows, :]
                    sc = (_nt(qn_ref[e, rows, :], k_e) + _nt(qr_ref[0, rows, sl], x_e)) * scale
                if masked:
                    sc = jnp.where(row >= col, sc, NEG_INF)
                p = jnp.exp(sc - wide(lse_ref[e, rows, :]))
                dp = _nt(do_i, v_e)
                ds = p * (dp - wide(delta_scr[e, rows, :]))
                dsb = (ds * scale).astype(BF16)
                if fox:
                    fsl = slice(e * half, (e + 1) * half)
                    dv_acc = dv_acc + _nn(dot_scr[i, fsl, :], p.astype(BF16))
                    dk_acc = dk_acc + _nn(qt_scr[i, fsl, :], dsb)
                    dq_ref[0, rows, sl] += _nn(dsb, k_e)
                    last = last - jnp.sum(ds, axis=0, keepdims=True)
                else:
                    dv_acc = dv_acc + _nn(dot_scr[e, i], p.astype(BF16))
                    dk_acc = dk_acc + _nn(qt_scr[e, i], dsb)
                    dqn_ref[e, rows, :] += _nn(dsb, k_e)
                    dqr_ref[0, rows, sl] += _nn(dsb, x_e)
                    last = last + _nn(qrt_scr[i, e * half:(e + 1) * half, :], dsb)
                new.append((dk_acc, dv_acc, last))
            return tuple(new)

        last0 = jnp.zeros((1, tq), F32) if fox else jnp.zeros((half, tq), F32)
        init = (jnp.zeros((dk_w, tq), F32), jnp.zeros((dv_w, tq), F32), last0)
        carry = step(j, (init, init), True)
        carry = lax.fori_loop(j + 1, nq, lambda i, c: step(i, c, False), carry)
        if fox:
            for e in range(2):
                dfk_ref[0, 0, e:e + 1, :] = carry[e][2]
            dk_ref[0] = jnp.concatenate([carry[0][0], carry[1][0]], axis=0).T.astype(BF16)
            dv_ref[0] = jnp.concatenate([carry[0][1], carry[1][1]], axis=0).T.astype(BF16)
        else:
            for e in range(2):
                dkn_ref[e] = carry[e][0].T.astype(BF16)
                dv_ref[e] = carry[e][1].T.astype(BF16)
            dkr_t = carry[0][2] + carry[1][2]
            dkr_ref[krows, :] += jnp.concatenate([dkr_t, jnp.zeros_like(dkr_t)], axis=0).T

    def whole(b, g, j):
        return (g, b, 0)

    def kblk(b, g, j):
        return (g, b * nq + j, 0)

    if fox:
        in_specs = [pl.BlockSpec((1, s, LANES), whole),
                    pl.BlockSpec((1, tq, LANES), lambda b, g, j: (n_pair + g, b * nq + j, 0)),
                    pl.BlockSpec((1, tq, LANES), lambda b, g, j: (2 * n_pair + g, b * nq + j, 0)),
                    pl.BlockSpec((2, s, LANES), whole),
                    pl.BlockSpec((1, 1, 8, tq), lambda b, g, j: (b * n_pair + g, j, 0, 0)),
                    pl.BlockSpec((1, s, LANES), whole), pl.BlockSpec((1, s, LANES), whole),
                    pl.BlockSpec((2, s, LANES), whole)]
        args = [qkv, qkv, qkv, fq, fk, o, do, lse]
        out_shape = [jax.ShapeDtypeStruct((8, t, LANES), F32), jax.ShapeDtypeStruct((8, t, LANES), BF16),
                     jax.ShapeDtypeStruct((8, t, LANES), BF16), jax.ShapeDtypeStruct(fk.shape, F32)]
        out_specs = [pl.BlockSpec((1, s, LANES), whole), pl.BlockSpec((1, tq, LANES), kblk),
                     pl.BlockSpec((1, tq, LANES), kblk),
                     pl.BlockSpec((1, 1, 8, tq), lambda b, g, j: (b * n_pair + g, j, 0, 0))]
    else:
        pair = pl.BlockSpec((2, s, LANES), whole)
        pair_k = pl.BlockSpec((2, tq, LANES), kblk)
        in_specs = [pair, pl.BlockSpec((1, s, LANES), whole), pair_k,
                    pl.BlockSpec((s, LANES), lambda b, g, j: (b, 0)), pair_k,
                    pair, pair, pair]
        args = [qn, qr, kn, kr, v, o, do, lse]
        out_shape = [jax.ShapeDtypeStruct((8, t, LANES), F32), jax.ShapeDtypeStruct((4, t, LANES), F32),
                     jax.ShapeDtypeStruct((8, t, LANES), BF16), jax.ShapeDtypeStruct((8, t, LANES), BF16),
                     jax.ShapeDtypeStruct((t, LANES), F32)]
        out_specs = [pair, pl.BlockSpec((1, s, LANES), whole), pair_k, pair_k,
                     pl.BlockSpec((s, LANES), lambda b, g, j: (b, 0))]
    t_blocks = pltpu.VMEM((nq, LANES, tq), BF16)
    t_pairs = pltpu.VMEM((2, nq, LANES, tq), BF16)
    scratch = [pltpu.VMEM((2, s, LANES), F32)] + ([t_blocks, t_blocks] if fox else [t_pairs, t_blocks, t_pairs])
    return pl.pallas_call(
        body, name=name, out_shape=out_shape, grid=(bl, n_pair, nq), in_specs=in_specs, out_specs=out_specs,
        scratch_shapes=scratch, compiler_params=_params("arbitrary", "arbitrary", "arbitrary"),
    )(*args)


def adamw(w, g, m, v, name):
    shape = w.shape
    c = shape[-1]
    r = w.size // c
    tr = _tile(r, 512, 8)

    def body(w_ref, g_ref, m_ref, v_ref, d_ref, nm_ref, nv_ref):
        gv = g_ref[...]
        m2 = ADAM_B1 * m_ref[...] + (1.0 - ADAM_B1) * gv
        v2 = ADAM_B2 * v_ref[...] + (1.0 - ADAM_B2) * (gv * gv)
        m_hat = m2 / (1.0 - ADAM_B1 ** ADAM_STEP)
        v_hat = v2 / (1.0 - ADAM_B2 ** ADAM_STEP)
        d_ref[...] = -ADAM_LR * (m_hat / (jnp.sqrt(v_hat) + ADAM_EPS) + ADAM_WD * w_ref[...])
        nm_ref[...] = m2
        nv_ref[...] = v2

    spec = pl.BlockSpec((tr, c), lambda i: (i, 0))
    outs = pl.pallas_call(
        body, name=name, out_shape=[jax.ShapeDtypeStruct((r, c), F32)] * 3, grid=(r // tr,),
        in_specs=[spec] * 4, out_specs=[spec] * 3, compiler_params=_params("arbitrary"),
    )(*(a.reshape(r, c) for a in (w, g, m, v)))
    return tuple(a.reshape(shape) for a in outs)


PACK_COLS = 1024


def _pack_rows(a):
    return a.reshape(-1, PACK_COLS)


def kernel(x, c, positions, mla_w_in, mla_g_q, mla_w_uq, mla_g_kv, mla_w_uk, mla_w_uv, mla_w_o, fox_w_in, fox_b_f, fox_w_o, ada_w, ada_b, ffn_w_gate, ffn_w_up, ffn_w_down, ln_g, ln_b, loss_target, m_mla_w_in, m_mla_g_q, m_mla_w_uq, m_mla_g_kv, m_mla_w_uk, m_mla_w_uv, m_mla_w_o, m_fox_w_in, m_fox_b_f, m_fox_w_o, m_ada_w, m_ada_b, m_ffn_w_gate, m_ffn_w_up, m_ffn_w_down, m_ln_g, m_ln_b, v_mla_w_in, v_mla_g_q, v_mla_w_uq, v_mla_g_kv, v_mla_w_uk, v_mla_w_uv, v_mla_w_o, v_fox_w_in, v_fox_b_f, v_fox_w_o, v_ada_w, v_ada_b, v_ffn_w_gate, v_ffn_w_up, v_ffn_w_down, v_ln_g, v_ln_b):
    bl, s, d = x.shape
    t = bl * s
    ff = ffn_w_gate.shape[-1] * N_DEV
    dev = 4 * lax.axis_index("x") + 2 * lax.axis_index("y") + lax.axis_index("c")
    ada_cols = ada_w.shape[-1]
    fox_in = fox_w_in.shape[-1] * N_DEV
    mla_in = mla_w_in.shape[-1]
    mla_in_pad = mla_in + (-mla_in) % LANES

    def t_last(a):
        return jnp.swapaxes(a, -1, -2)

    local = {
        "mla_w_in": mla_w_in[0],
        "mla_w_uq": t_last(mla_w_uq[0]),
        "mla_w_uk": t_last(mla_w_uk[0]),
        "mla_w_uv": t_last(mla_w_uv[0]),
        "mla_w_o": mla_w_o[0],
        "fox_w_in": t_last(fox_w_in[0]),
        "fox_w_o": fox_w_o[0],
    }
    for i in range(DEPTH):
        local.update({f"gate{i}": t_last(ffn_w_gate[i]), f"up{i}": t_last(ffn_w_up[i]), f"down{i}": ffn_w_down[i]})
    groups = [["mla_w_in", "mla_w_uq", "mla_w_uk", "mla_w_uv", "mla_w_o"],
              ["gate0", "up0", "down0"],
              ["fox_w_in", "fox_w_o", "gate1", "up1", "down1"]]
    offsets, rows_of, slot_of, group_of = {}, {}, {}, {}
    group_rows = []
    for gi, names in enumerate(groups):
        rows = 0
        for nm in names:
            rows_of[nm] = local[nm].size // PACK_COLS
            slot_of[nm] = rows_of[nm] + (-rows_of[nm]) % 16
            offsets[nm] = rows
            group_of[nm] = gi
            rows += slot_of[nm]
        group_rows.append(rows)

    def slot(nm, rows):
        pad = [(0, 0)] * rows.ndim
        pad[-2] = (0, slot_of[nm] - rows_of[nm])
        return jnp.pad(rows, pad)

    def landing(block):
        land = lax.empty((N_DEV,) + block.shape, block.dtype)
        return lax.dynamic_update_slice(land, block[None], (dev, 0, 0))

    packed = [jnp.concatenate([slot(nm, _pack_rows(local[nm]).astype(BF16)) for nm in names], axis=0)
              for names in groups]
    gathered = [all_gather(packed[0], "gather_mla_weights"), None, None]
    gather_started = [None] * len(groups)

    def full(nm, cols):
        blk = gathered[group_of[nm]][:, offsets[nm]:offsets[nm] + rows_of[nm], :]
        return blk.reshape(-1, cols)

    w_in = jnp.pad(full("mla_w_in", mla_in), ((0, 0), (0, mla_in_pad - mla_in)))
    wt_uq = full("mla_w_uq", MLA_QR).reshape(MLA_HEADS, MLA_NOPE + MLA_ROPE, MLA_QR)
    wt_uq_n = wt_uq[:, :MLA_NOPE].reshape(MLA_HEADS * MLA_NOPE, MLA_QR)
    wt_uq_r = wt_uq[:, MLA_NOPE:].reshape(MLA_HEADS * MLA_ROPE, MLA_QR)
    wt_uk = full("mla_w_uk", MLA_KVR)
    wt_uv = full("mla_w_uv", MLA_KVR)
    w_mo = full("mla_w_o", d)
    wt_gate, wt_up, w_down = [None] * DEPTH, [None] * DEPTH, [None] * DEPTH

    def arrive(gi, after):
        gathered[gi] = exchange_wait(gather_started[gi], after, f"gather_group{gi}_wait", False)
        for i in range(DEPTH):
            if group_of[f"gate{i}"] == gi:
                wt_gate[i], wt_up[i], w_down[i] = full(f"gate{i}", d), full(f"up{i}", d), full(f"down{i}", d)

    small = jnp.concatenate([c.reshape(-1, LANES), ln_g.reshape(-1, LANES), ln_b.reshape(-1, LANES)], axis=0)
    small_rows = small.shape[0]
    small = jnp.pad(small, ((0, (-small_rows) % 8), (0, 0)))
    small_all = all_gather(small, "gather_small")
    c_rows = bl * d // LANES
    c_all = small_all[:, :c_rows].reshape(N_DEV * bl, d)
    n_ln = DEPTH * 2
    ln_g_all = small_all[:, c_rows:c_rows + n_ln, :].transpose(1, 0, 2).reshape(DEPTH, 2, 1, d)
    ln_b_all = small_all[:, c_rows + n_ln:c_rows + 2 * n_ln, :].transpose(1, 0, 2).reshape(DEPTH, 2, 1, d)

    c_act = silu_rows(c_all, "silu_c")
    ada_b_loc = lax.dynamic_slice_in_dim(ada_b, dev * ada_cols, ada_cols, axis=1)
    mod_cols = [mm([(c_act, ada_w[i])], trans_b=False, out_dtype=F32, name=f"ada_fwd{i}", bias=ada_b_loc[i][None, :])
                for i in range(DEPTH)]
    mod_all = all_gather(jnp.concatenate(mod_cols, axis=0), "gather_mod")
    mod_all = mod_all.reshape(N_DEV, DEPTH, N_DEV * bl, ada_cols).transpose(1, 2, 0, 3).reshape(DEPTH, N_DEV * bl, 6 * d)
    mod_mine = lax.dynamic_slice_in_dim(mod_all, dev * bl, bl, axis=1)
    mods = [mod_mine[i].reshape(bl * 6, 1, d) for i in range(DEPTH)]
    for gi in (1, 2):
        block, _, _ = lax.optimization_barrier((packed[gi], mod_mine, gathered[0]))
        gather_started[gi] = exchange_start(block, landing(block), f"gather_group{gi}_start", False)
        mods[0] = after_token(mods[0], gather_started[gi])

    half_r = MLA_ROPE // 2
    inv_freq = ROPE_THETA ** (-jnp.arange(half_r, dtype=F32) / half_r)
    inv_freq = jnp.tile(inv_freq, LANES // half_r)[None, :]
    sign = jnp.tile(jnp.concatenate([-jnp.ones((half_r,), F32), jnp.ones((half_r,), F32)]), LANES // MLA_ROPE)[None, :]
    cos_t, sin_t = rope_tables(positions.astype(F32).reshape(t, 1), inv_freq, sign, "rope_tables")

    x2d = x.reshape(t, d)
    g_q, g_kv = mla_g_q.reshape(1, MLA_QR), mla_g_kv.reshape(1, MLA_KVR)
    b_f = jnp.pad(fox_b_f.reshape(1, FOX_HEADS), ((0, 0), (0, LANES - FOX_HEADS)))
    mla_scale = (MLA_NOPE + MLA_ROPE) ** -0.5
    fox_scale = FOX_HD ** -0.5
    tq = _attn_tiles(s)
    nk = s // tq

    saved = []
    u = modulate(x2d, mods[0], 0, 1, bl, "modulate0")
    xin = x2d
    for i in range(DEPTH):
        sv = {"u": u, "x_in": xin}
        if i % 2 == 0:
            h_in = mm([(u, w_in)], trans_b=False, out_dtype=F32, name=f"mla_in{i}")
            c_q, c_kv, k_r = mla_latents_forward(h_in, g_q, g_kv, cos_t, sin_t, f"mla_latents{i}")
            q_n = mm([(c_q, wt_uq_n)], trans_b=True, out_dtype=BF16, out_slab=True, name=f"mla_qn{i}")
            q_r_raw = mm([(c_q, wt_uq_r)], trans_b=True, out_dtype=F32, out_slab=True, name=f"mla_qr{i}")
            q_r = rope_slabs(q_r_raw, cos_t, sin_t, BF16, f"mla_qrope{i}")
            k_n = mm([(c_kv, wt_uk)], trans_b=True, out_dtype=BF16, out_slab=True, name=f"mla_kn{i}")
            v_m = mm([(c_kv, wt_uv)], trans_b=True, out_dtype=BF16, out_slab=True, name=f"mla_v{i}")
            ops = (q_n, q_r, k_n, k_r, v_m)
            o, lse, o_delta = attention_forward("mla", ops, bl, mla_scale, f"mla_attn{i}")
            y = mm([(o, w_mo)], trans_b=False, out_dtype=F32, name=f"mla_out{i}")
            sv.update(h_in=h_in, c_q=c_q, c_kv=c_kv, ops=ops, o=o, lse=lse, o_delta=o_delta)
        else:
            arrive(2, u)
            wt_fox = full("fox_w_in", d)
            wt_qkv = wt_fox[:3 * d]
            wt_f = jnp.pad(wt_fox[3 * d:], ((0, LANES - FOX_HEADS), (0, 0)))
            w_fo = full("fox_w_o", d)
            qkv = mm([(u, wt_qkv)], trans_b=True, out_dtype=BF16, out_slab=True, name=f"fox_qkv{i}")
            z = mm([(u, wt_f)], trans_b=True, out_dtype=F32, name=f"fox_z{i}")
            f_tok, f_q = fox_gate_forward(z, b_f, bl, f"fox_gate{i}")
            f_k = f_tok[:, :FOX_HEADS].reshape(bl, nk, tq, FOX_HEADS // 2, 2).transpose(0, 3, 1, 4, 2)
            f_k = jnp.pad(f_k.reshape(bl * FOX_HEADS // 2, nk, 2, tq), ((0, 0), (0, 0), (0, 6), (0, 0)))
            ops = (qkv, f_q, f_k)
            o, lse, o_delta = attention_forward("fox", ops, bl, fox_scale, f"fox_attn{i}")
            y = mm([(o, w_fo)], trans_b=False, out_dtype=F32, name=f"fox_out{i}")
            sv.update(z=z, ops=ops, o=o, lse=lse, o_delta=o_delta)
        x1, r1, u2 = residual_layer_norm(xin, y, mods[i], 2, ln_g_all[i, 0], ln_b_all[i, 0], bl, f"ln_mix{i}",
                                         next_mod=(3, 4))
        if wt_gate[i] is None:
            arrive(group_of[f"gate{i}"], u2)
        a = mm([(u2, wt_gate[i])], trans_b=True, out_dtype=F32, name=f"ffn_gate{i}")
        bb = mm([(u2, wt_up[i])], trans_b=True, out_dtype=F32, name=f"ffn_up{i}")
        h = swiglu_forward(a, bb, f"swiglu{i}")
        y2 = mm([(h, w_down[i])], trans_b=False, out_dtype=F32, name=f"ffn_down{i}")
        sv.update(y=y, r1=r1, u2=u2, a=a, bb=bb, h=h, y2=y2)
        if i + 1 < DEPTH:
            xin, r2, u = residual_layer_norm(x1, y2, mods[i], 5, ln_g_all[i, 1], ln_b_all[i, 1], bl, f"ln_ffn{i}",
                                             next_mod=(0, 1, mods[i + 1]))
        else:
            xin, r2 = residual_layer_norm(x1, y2, mods[i], 5, ln_g_all[i, 1], ln_b_all[i, 1], bl, f"ln_ffn{i}")
        sv.update(r2=r2)
        saved.append(sv)

    loss_cols, d_x = loss_head(xin, loss_target.reshape(t, d), "loss_head")

    grads_full = {}
    dmod = [[None] * 6 for _ in range(DEPTH)]
    dg_ln = [[None, None] for _ in range(DEPTH)]
    db_ln = [[None, None] for _ in range(DEPTH)]
    dg_q = dg_kv = db_f = None
    d_a, du = d_x, None
    scatter_started = [None] * len(groups)

    def scatter_start(gi):
        g = jnp.concatenate(
            [slot(nm, grads_full[nm].reshape(N_DEV, rows_of[nm], PACK_COLS).astype(BF16)) for nm in groups[gi]], axis=1)
        own = lax.dynamic_index_in_dim(g, dev, 0, keepdims=False)
        scatter_started[gi] = exchange_start(g, landing(own), f"scatter_group{gi}_start", True)

    ln_g_bwd = [[ln_g_all[i, k] for k in range(2)] for i in range(DEPTH)]
    for i in reversed(range(DEPTH)):
        sv = saved[i]
        if i + 1 < DEPTH:
            scatter_start(2)
            ln_g_bwd[i][1] = after_token(ln_g_bwd[i][1], scatter_started[2])
        ln2 = (sv["r2"], sv["y2"], ln_g_bwd[i][1], ln_b_all[i, 1], (mods[i], 5))
        if du is None:
            bw = sublayer_backward(d_a, bl, f"bwd_ln_ffn{i}", ln=ln2)
        else:
            bw = sublayer_backward(d_a, bl, f"bwd_ln_ffn{i}", du=du, scale=(mods[i + 1], 1), ln=ln2)
            dmod[i + 1][0], dmod[i + 1][1] = bw["dshift"], bw["dscale"]
        dmod[i][5], dg_ln[i][1], db_ln[i][1] = bw["dgate"], bw["dg"], bw["db"]
        dy2 = bw["dy"]
        dh = mm([(dy2, w_down[i])], trans_b=True, out_dtype=F32, name=f"bwd_ffn_dh{i}")
        da, dbb = swiglu_backward(dh, sv["a"], sv["bb"], f"bwd_swiglu{i}")
        du2 = mm([(da, wt_gate[i]), (dbb, wt_up[i])], trans_b=False, out_dtype=F32, name=f"bwd_ffn_du{i}")
        grads_full[f"down{i}"] = mm_tn(sv["h"], dy2, name=f"bwd_w_down{i}")
        grads_full[f"gate{i}"] = mm_tn(da, sv["u2"], name=f"bwd_w_gate{i}")
        grads_full[f"up{i}"] = mm_tn(dbb, sv["u2"], name=f"bwd_w_up{i}")
        if i == 0:
            scatter_start(1)
            ln_g_bwd[i][0] = after_token(ln_g_bwd[i][0], scatter_started[1])
        bw = sublayer_backward(bw["dx"], bl, f"bwd_ln_mix{i}", du=du2, scale=(mods[i], 4),
                               ln=(sv["r1"], sv["y"], ln_g_bwd[i][0], ln_b_all[i, 0], (mods[i], 2)))
        dmod[i][3], dmod[i][4], dmod[i][2] = bw["dshift"], bw["dscale"], bw["dgate"]
        dg_ln[i][0], db_ln[i][0] = bw["dg"], bw["db"]
        d_a, dy = bw["dx"], bw["dy"]
        o, lse, ops = sv["o"], sv["lse"], sv["ops"]
        if i % 2 == 0:
            do = mm([(dy, w_mo)], trans_b=True, out_dtype=BF16, out_slab=True, name=f"bwd_mla_do{i}")
            grads_full["mla_w_o"] = mm_tn(o, dy, name=f"bwd_w_mla_o{i}")
            dqn, dqr, dkn, dvm, dkr = attention_backward("mla", ops, sv["o_delta"], do, lse, bl, mla_scale,
                                                         f"bwd_mla_attn{i}")
            dqr = rope_slabs(dqr, cos_t, sin_t, F32, f"bwd_mla_qrope{i}", transposed=True)
            dcq = mm([(dqn, wt_uq_n), (dqr, wt_uq_r)], trans_b=False, out_dtype=F32, name=f"bwd_mla_dcq{i}")
            dckv = mm([(dkn, wt_uk), (dvm, wt_uv)], trans_b=False, out_dtype=F32, name=f"bwd_mla_dckv{i}")
            d_uq_n = mm_tn(dqn, sv["c_q"], name=f"bwd_w_uq_n{i}").reshape(MLA_HEADS, MLA_NOPE, MLA_QR)
            d_uq_r = mm_tn(dqr, sv["c_q"], name=f"bwd_w_uq_r{i}").reshape(MLA_HEADS, MLA_ROPE, MLA_QR)
            grads_full["mla_w_uq"] = jnp.concatenate([d_uq_n, d_uq_r], axis=1)
            grads_full["mla_w_uk"] = mm_tn(dkn, sv["c_kv"], name=f"bwd_w_uk{i}")
            grads_full["mla_w_uv"] = mm_tn(dvm, sv["c_kv"], name=f"bwd_w_uv{i}")
            dh_in, dg_q, dg_kv = mla_latents_backward(sv["h_in"], dcq, dckv, dkr, g_q, g_kv, cos_t, sin_t,
                                                      f"bwd_mla_latents{i}")
            du = mm([(dh_in, w_in)], trans_b=True, out_dtype=F32, name=f"bwd_mla_du{i}")
            grads_full["mla_w_in"] = mm_tn(sv["u"], dh_in, name=f"bwd_w_mla_in{i}")[:, :mla_in]
        else:
            do = mm([(dy, w_fo)], trans_b=True, out_dtype=BF16, out_slab=True, name=f"bwd_fox_do{i}")
            grads_full["fox_w_o"] = mm_tn(o, dy, name=f"bwd_w_fox_o{i}")
            dq, dk, dvf, dfk = attention_backward("fox", ops, sv["o_delta"], do, lse, bl, fox_scale, f"bwd_fox_attn{i}")
            df = dfk[:, :, :2, :].reshape(bl, FOX_HEADS // 2, nk, 2, tq).transpose(0, 2, 4, 1, 3).reshape(t, FOX_HEADS)
            df = jnp.pad(df, ((0, 0), (0, LANES - FOX_HEADS)))
            dz, db_f = fox_gate_backward(sv["z"], b_f, df, bl, f"bwd_fox_gate{i}")
            du = mm([(dq, wt_fox[0:d]), (dk, wt_fox[d:2 * d]), (dvf, wt_fox[2 * d:3 * d]), (dz, wt_f)],
                    trans_b=False, out_dtype=F32, name=f"bwd_fox_du{i}")
            u_f = sv["u"]
            grads_full["fox_w_in"] = jnp.concatenate(
                [mm_tn(dq, u_f, name=f"bwd_w_fox_q{i}"), mm_tn(dk, u_f, name=f"bwd_w_fox_k{i}"),
                 mm_tn(dvf, u_f, name=f"bwd_w_fox_v{i}"), mm_tn(dz, u_f, name=f"bwd_w_fox_f{i}")[:FOX_HEADS]], axis=0)
    bw = sublayer_backward(d_a, bl, "bwd_input", du=du, scale=(mods[0], 1), x_in=x2d)
    dmod[0][0], dmod[0][1] = bw["dshift"], bw["dscale"]
    grad_x = bw["dx"].reshape(bl, s, d)

    dmod_rows = jnp.concatenate([r.reshape(bl, d) for layer in dmod for r in layer], axis=0)
    dmod_rows = dmod_rows.reshape(DEPTH, 6, bl, d).transpose(0, 2, 1, 3)
    n_mod = dmod_rows.size // LANES
    ln_parts = [dg_ln[i][k] for i in range(DEPTH) for k in range(2)] + [db_ln[i][k] for i in range(DEPTH) for k in range(2)]
    small_g = jnp.concatenate([dmod_rows.reshape(-1, LANES), dg_q.reshape(-1, LANES), dg_kv.reshape(-1, LANES), db_f]
                              + [p.reshape(-1, LANES) for p in ln_parts] + [loss_cols.reshape(-1, LANES)], axis=0)
    n_small = small_g.shape[0]
    small_g = jnp.pad(small_g, ((0, (-n_small) % 8), (0, 0)))
    small_g_all = all_gather(small_g, "gather_small_grads")
    small_sum = sum_leading(small_g_all, "sum_small_grads")
    per_seq = DEPTH * 6 * d // LANES
    dmod_all = small_g_all[:, :n_mod].reshape(N_DEV, DEPTH, bl, 6 * d).transpose(1, 0, 2, 3)
    dmod_all = dmod_all.reshape(DEPTH, N_DEV * bl, 6 * d)
    o1 = n_mod
    grad_g_q = small_sum[o1:o1 + MLA_QR // LANES].reshape(1, MLA_QR)
    o1 += MLA_QR // LANES
    grad_g_kv = small_sum[o1:o1 + MLA_KVR // LANES].reshape(1, MLA_KVR)
    o1 += MLA_KVR // LANES
    grad_b_f = small_sum[o1:o1 + 1, :FOX_HEADS]
    o1 += 1
    n_ln_rows = DEPTH * 2 * d // LANES
    grad_ln_g_full = small_sum[o1:o1 + n_ln_rows].reshape(DEPTH, 2, d)
    grad_ln_b_full = small_sum[o1 + n_ln_rows:o1 + 2 * n_ln_rows].reshape(DEPTH, 2, d)
    loss = jnp.sum(small_sum[o1 + 2 * n_ln_rows:o1 + 2 * n_ln_rows + d // LANES])
    shard = d // N_DEV
    grad_ln_g = lax.dynamic_slice_in_dim(grad_ln_g_full, dev * shard, shard, axis=2)
    grad_ln_b = lax.dynamic_slice_in_dim(grad_ln_b_full, dev * shard, shard, axis=2)
    by_seq = small_g_all[:, :n_mod].reshape(N_DEV, DEPTH, bl, 6 * d // LANES, LANES).transpose(0, 2, 1, 3, 4)
    grad_ada_b = sum_leading(by_seq.reshape(N_DEV * bl, per_seq, LANES), "sum_ada_b").reshape(DEPTH, 6 * d)
    dmod_cols = lax.dynamic_slice_in_dim(dmod_all, dev * ada_cols, ada_cols, axis=2)
    grad_ada_w = jnp.stack([mm_tn(c_act, dmod_cols[i], name=f"bwd_w_ada{i}") for i in range(DEPTH)])

    scatter_start(0)
    after = bw["dx"]
    g_mine = [None] * len(groups)
    for gi in (2, 1, 0):
        landed = exchange_wait(scatter_started[gi], after, f"scatter_group{gi}_wait", True)
        g_mine[gi] = sum_leading(landed, f"scatter_group{gi}_sum")
        after = g_mine[gi]

    def mine(nm, shape):
        return g_mine[group_of[nm]][offsets[nm]:offsets[nm] + rows_of[nm]].reshape(shape)

    def shard_t(nm, a):
        return t_last(mine(nm, t_last(a).shape))

    grads = {
        "mla_w_in": mine("mla_w_in", mla_w_in[0].shape)[None],
        "mla_g_q": grad_g_q,
        "mla_w_uq": shard_t("mla_w_uq", mla_w_uq[0])[None],
        "mla_g_kv": grad_g_kv,
        "mla_w_uk": shard_t("mla_w_uk", mla_w_uk[0])[None],
        "mla_w_uv": shard_t("mla_w_uv", mla_w_uv[0])[None],
        "mla_w_o": mine("mla_w_o", mla_w_o[0].shape)[None],
        "fox_w_in": shard_t("fox_w_in", fox_w_in[0])[None],
        "fox_b_f": grad_b_f,
        "fox_w_o": mine("fox_w_o", fox_w_o[0].shape)[None],
        "ada_w": grad_ada_w,
        "ada_b": grad_ada_b,
        "ffn_w_gate": jnp.stack([shard_t(f"gate{i}", ffn_w_gate[i]) for i in range(DEPTH)]),
        "ffn_w_up": jnp.stack([shard_t(f"up{i}", ffn_w_up[i]) for i in range(DEPTH)]),
        "ffn_w_down": jnp.stack([mine(f"down{i}", ffn_w_down[i].shape) for i in range(DEPTH)]),
        "ln_g": grad_ln_g,
        "ln_b": grad_ln_b,
    }
    weights = dict(mla_w_in=mla_w_in, mla_g_q=mla_g_q, mla_w_uq=mla_w_uq, mla_g_kv=mla_g_kv, mla_w_uk=mla_w_uk,
                   mla_w_uv=mla_w_uv, mla_w_o=mla_w_o, fox_w_in=fox_w_in, fox_b_f=fox_b_f, fox_w_o=fox_w_o,
                   ada_w=ada_w, ada_b=ada_b, ffn_w_gate=ffn_w_gate, ffn_w_up=ffn_w_up, ffn_w_down=ffn_w_down,
                   ln_g=ln_g, ln_b=ln_b)
    first = dict(mla_w_in=m_mla_w_in, mla_g_q=m_mla_g_q, mla_w_uq=m_mla_w_uq, mla_g_kv=m_mla_g_kv, mla_w_uk=m_mla_w_uk,
                 mla_w_uv=m_mla_w_uv, mla_w_o=m_mla_w_o, fox_w_in=m_fox_w_in, fox_b_f=m_fox_b_f, fox_w_o=m_fox_w_o,
                 ada_w=m_ada_w, ada_b=m_ada_b, ffn_w_gate=m_ffn_w_gate, ffn_w_up=m_ffn_w_up, ffn_w_down=m_ffn_w_down,
                 ln_g=m_ln_g, ln_b=m_ln_b)
    second = dict(mla_w_in=v_mla_w_in, mla_g_q=v_mla_g_q, mla_w_uq=v_mla_w_uq, mla_g_kv=v_mla_g_kv, mla_w_uk=v_mla_w_uk,
                  mla_w_uv=v_mla_w_uv, mla_w_o=v_mla_w_o, fox_w_in=v_fox_w_in, fox_b_f=v_fox_b_f, fox_w_o=v_fox_w_o,
                  ada_w=v_ada_w, ada_b=v_ada_b, ffn_w_gate=v_ffn_w_gate, ffn_w_up=v_ffn_w_up, ffn_w_down=v_ffn_w_down,
                  ln_g=v_ln_g, ln_b=v_ln_b)
    order = list(weights)
    g_out, d_out, m_out, v_out = [], [], [], []
    for nm in order:
        g = grads[nm].reshape(weights[nm].shape)
        delta, new_m, new_v = adamw(weights[nm], g, first[nm], second[nm], f"adamw_{nm}")
        g_out.append(g)
        d_out.append(delta)
        m_out.append(new_m)
        v_out.append(new_v)
    return (loss, grad_x, *g_out, *d_out, *m_out, *v_out)
```

```python
import functools

import jax
import jax.numpy as jnp
from jax import lax
from jax.experimental import pallas as pl
from jax.experimental.pallas import tpu as pltpu

F32 = jnp.float32
BF16 = jnp.bfloat16
LANES = 128
N_DEV = 8
VMEM_LIMIT_BYTES = 56 * 1024 * 1024

DEPTH = 2
MLA_HEADS = 8
MLA_NOPE = 128
MLA_ROPE = 64
MLA_V = 128
MLA_QR = 256
MLA_KVR = 256
ROPE_THETA = 10000.0
FOX_HEADS = 16
FOX_HD = 64
ALPHA = (2.0 * DEPTH) ** 0.25
NORM_EPS = 1e-5
ADAM_LR = 0.001
ADAM_B1 = 0.9
ADAM_B2 = 0.999
ADAM_EPS = 1e-08
ADAM_WD = 0.01
ADAM_STEP = 10

MESH_AXES = ("x", "y", "c")
MESH = pl.DeviceIdType.MESH


def _params(*sem):
    return pltpu.CompilerParams(dimension_semantics=sem, vmem_limit_bytes=VMEM_LIMIT_BYTES)


def _tile(n, cap, mult=LANES):
    if n <= cap:
        return n
    best = None
    for t in range(mult, cap + 1, mult):
        if n % t == 0:
            best = t
    assert best is not None, (n, cap, mult)
    return best


def _dot(a, b, dims):
    return lax.dot_general(a, b, (dims, ((), ())), preferred_element_type=F32)


def _nn(a, b):
    return _dot(a, b, ((1,), (0,)))


def _nt(a, b):
    return _dot(a, b, ((1,), (1,)))


def _tn(a, b):
    return _dot(a, b, ((0,), (0,)))


def _me():
    return lax.axis_index("x"), lax.axis_index("y"), lax.axis_index("c")


def all_gather(x_loc, name):
    r, c = x_loc.shape

    def body(x_ref, out_ref, send_sems, recv_sems, local_sem):
        x, y, cc = _me()
        me, sibling = (x, y, cc), (x, y, 1 - cc)
        chips = [(1 - x, y), (x, 1 - y), (1 - x, 1 - y)]

        def rows(px, py, pc):
            return out_ref.at[4 * px + 2 * py + pc]

        def copy(k, block, to, src=None):
            return pltpu.make_async_remote_copy(
                src_ref=rows(*block) if src is None else src, dst_ref=rows(*block),
                send_sem=send_sems.at[k], recv_sem=recv_sems.at[k], device_id=to, device_id_type=MESH)

        mine = pltpu.make_async_copy(x_ref, rows(*me), local_sem)
        mine.start()
        first = [copy(0, me, sibling, src=x_ref)]
        first += [copy(1 + j, me, (*chip, cc), src=x_ref) for j, chip in enumerate(chips)]
        for cp in first:
            cp.start()
        passed = [copy(4 + j, (*chip, cc), sibling) for j, chip in enumerate(chips)]
        for j, chip in enumerate(chips):
            copy(1 + j, (*chip, cc), me).wait_recv()
            passed[j].start()
        copy(0, sibling, me).wait_recv()
        for j, chip in enumerate(chips):
            copy(4 + j, (*chip, 1 - cc), me).wait_recv()
        for cp in first + passed:
            cp.wait_send()
        mine.wait()

    return pl.pallas_call(
        body, name=name,
        out_shape=jax.ShapeDtypeStruct((N_DEV, r, c), x_loc.dtype),
        in_specs=[pl.BlockSpec(memory_space=pl.ANY)],
        out_specs=pl.BlockSpec(memory_space=pl.ANY),
        scratch_shapes=[pltpu.SemaphoreType.DMA((7,)), pltpu.SemaphoreType.DMA((7,)), pltpu.SemaphoreType.DMA(())],
    )(x_loc)


HBM_SPEC = pl.BlockSpec(memory_space=pltpu.HBM)
SEM_SPEC = pl.BlockSpec(memory_space=pltpu.SEMAPHORE)
N_PEERS = N_DEV - 1


def _peer(k):
    x, y, c = _me()
    return (1 - x if k & 4 else x, 1 - y if k & 2 else y, 1 - c if k & 1 else c)


def _exchange_copies(src_ref, land_ref, send_sems, recv_sems, scatter):
    x, y, c = _me()
    mine = 4 * x + 2 * y + c
    copies = []
    for k in range(1, N_DEV):
        px, py, pc = _peer(k)
        src = src_ref.at[4 * px + 2 * py + pc] if scatter else src_ref
        copies.append(pltpu.make_async_remote_copy(
            src_ref=src, dst_ref=land_ref.at[mine], send_sem=send_sems.at[k - 1], recv_sem=recv_sems.at[k - 1],
            device_id=(px, py, pc), device_id_type=MESH))
    return copies


def exchange_start(src, land, name, scatter):
    def body(src_ref, land_ref, send_sems, recv_sems, src_thru, land_thru, token):
        for cp in _exchange_copies(src_ref, land_ref, send_sems, recv_sems, scatter):
            cp.start()
        token[...] = jnp.zeros_like(token)

    return pl.pallas_call(
        body, name=name,
        out_shape=(pltpu.SemaphoreType.DMA((N_PEERS,)), pltpu.SemaphoreType.DMA((N_PEERS,)),
                   pltpu.HBM(src.shape, src.dtype), pltpu.HBM(land.shape, land.dtype),
                   jax.ShapeDtypeStruct((8, LANES), F32)),
        in_specs=(HBM_SPEC, HBM_SPEC),
        out_specs=(SEM_SPEC, SEM_SPEC, HBM_SPEC, HBM_SPEC, pl.BlockSpec(memory_space=pltpu.VMEM)),
        input_output_aliases={0: 2, 1: 3},
        compiler_params=pltpu.CompilerParams(has_side_effects=pltpu.SideEffectType.DATAFLOW_SIDE_EFFECTING),
    )(pltpu.with_memory_space_constraint(src, pltpu.HBM), pltpu.with_memory_space_constraint(land, pltpu.HBM))


def exchange_wait(started, after, name, scatter):
    send_sems, recv_sems, src_thru, land_thru, _ = started

    def body(src_ref, land_ref, send_sems, recv_sems, after_ref, src_dead, got_ref):
        for cp in _exchange_copies(src_ref, land_ref, send_sems, recv_sems, scatter):
            cp.wait_send()
            cp.wait_recv()

    return pl.pallas_call(
        body, name=name,
        out_shape=(pltpu.HBM(src_thru.shape, src_thru.dtype), pltpu.HBM(land_thru.shape, land_thru.dtype)),
        in_specs=(HBM_SPEC, HBM_SPEC, SEM_SPEC, SEM_SPEC, pl.BlockSpec(memory_space=pl.ANY)),
        out_specs=(HBM_SPEC, HBM_SPEC), input_output_aliases={0: 0, 1: 1},
        compiler_params=pltpu.CompilerParams(has_side_effects=pltpu.SideEffectType.DATAFLOW_SIDE_EFFECTING),
    )(src_thru, land_thru, send_sems, recv_sems, after)[1]


def after_token(small, started):
    return small + started[4][0, 0]


def sum_leading(x, name):
    n, r, c = x.shape
    tr = _tile(r, 512, 16)

    def body(x_ref, o_ref):
        acc = x_ref[0].astype(F32)
        for k in range(1, n):
            acc = acc + x_ref[k].astype(F32)
        o_ref[...] = acc

    return pl.pallas_call(
        body, name=name,
        out_shape=jax.ShapeDtypeStruct((r, c), F32),
        grid=(r // tr,),
        in_specs=[pl.BlockSpec((n, tr, c), lambda i: (0, i, 0))],
        out_specs=pl.BlockSpec((tr, c), lambda i: (i, 0)),
        compiler_params=_params("arbitrary"),
    )(x)


MM_VMEM_BUDGET = 36 * 1024 * 1024
GRID_STEP_AS_BYTES = 1 << 20


def _mm_tiles(m, n, a_row_bytes, b_col_bytes, out_bytes):
    tms = [c for c in (2048, 1024, 512, 256, 128, 64, 32, 16, 8) if m % c == 0] or [m]
    tns = [c for c in range(LANES, min(n, 2048) + 1, LANES) if n % c == 0] or [n]
    best = None
    for tm in tms:
        for tn in tns:
            vmem = 2 * (tm * a_row_bytes + tn * b_col_bytes) + 2 * tm * tn * out_bytes + tm * tn * 4
            if vmem > MM_VMEM_BUDGET:
                continue
            steps = (m // tm) * (n // tn)
            cost = steps * GRID_STEP_AS_BYTES + (m // tm) * n * b_col_bytes + m * a_row_bytes
            if best is None or cost < best[0]:
                best = (cost, tm, tn)
    assert best is not None, (m, n, a_row_bytes, b_col_bytes)
    return best[1], best[2]


def mm(pairs, *, trans_b, out_dtype, name, out_slab=False, bias=None):
    a0 = pairs[0][0]
    m = a0.shape[1] if a0.ndim == 3 else a0.shape[0]
    n = pairs[0][1].shape[0] if trans_b else pairs[0][1].shape[1]
    a_row_bytes = sum((b.shape[1] if trans_b else b.shape[0]) * a.dtype.itemsize for a, b in pairs)
    b_col_bytes = sum((b.shape[1] if trans_b else b.shape[0]) * b.dtype.itemsize for _, b in pairs)
    tm, tn = _mm_tiles(m, n, a_row_bytes, b_col_bytes, jnp.dtype(out_dtype).itemsize)
    slabs = [a.ndim == 3 for a, _ in pairs]
    n_pairs = len(pairs)

    def body(*refs):
        o_ref = refs[-1]
        acc = bias_ref = None
        if bias is not None:
            bias_ref = refs[2 * n_pairs]
        for i in range(n_pairs):
            a_ref, b_ref = refs[2 * i], refs[2 * i + 1]
            if slabs[i]:
                a = jnp.concatenate([a_ref[s].astype(BF16) for s in range(a_ref.shape[0])], axis=1)
            else:
                a = a_ref[...].astype(BF16)
            b = b_ref[...].astype(BF16)
            part = _nt(a, b) if trans_b else _nn(a, b)
            acc = part if acc is None else acc + part
        if bias_ref is not None:
            acc = acc + bias_ref[...]
        if out_slab:
            for s in range(tn // LANES):
                o_ref[s] = acc[:, s * LANES:(s + 1) * LANES].astype(out_dtype)
        else:
            o_ref[...] = acc.astype(out_dtype)

    in_specs, args = [], []
    for (a, b), slab in zip(pairs, slabs):
        if slab:
            in_specs.append(pl.BlockSpec((a.shape[0], tm, LANES), lambda i, j: (0, i, 0)))
        else:
            in_specs.append(pl.BlockSpec((tm, a.shape[1]), lambda i, j: (i, 0)))
        if trans_b:
            in_specs.append(pl.BlockSpec((tn, b.shape[1]), lambda i, j: (j, 0)))
        else:
            in_specs.append(pl.BlockSpec((b.shape[0], tn), lambda i, j: (0, j)))
        args += [a, b]
    if bias is not None:
        in_specs.append(pl.BlockSpec((1, tn), lambda i, j: (0, j)))
        args.append(bias)
    if out_slab:
        out_shape = jax.ShapeDtypeStruct((n // LANES, m, LANES), out_dtype)
        out_spec = pl.BlockSpec((tn // LANES, tm, LANES), lambda i, j: (j, i, 0))
    else:
        out_shape = jax.ShapeDtypeStruct((m, n), out_dtype)
        out_spec = pl.BlockSpec((tm, tn), lambda i, j: (i, j))
    return pl.pallas_call(
        body, name=name, out_shape=out_shape, grid=(m // tm, n // tn),
        in_specs=in_specs, out_specs=out_spec,
        compiler_params=_params("arbitrary", "arbitrary"),
    )(*args)


def mm_tn(a, b, *, name, out_dtype=F32, tk_cap=1536, tn_cap=1024, tm_cap=512):
    slab = a.ndim == 3
    m = a.shape[1] if slab else a.shape[0]
    k = a.shape[0] * LANES if slab else a.shape[1]
    n = b.shape[1]
    tk = _tile(k, tk_cap)
    tn = _tile(n, tn_cap)
    tm = _tile(m, tm_cap, 8)
    n_steps = m // tm

    def body(a_ref, b_ref, o_ref, acc_ref):
        step = pl.program_id(2)

        @pl.when(step == 0)
        def _():
            acc_ref[...] = jnp.zeros_like(acc_ref)

        bb = b_ref[...].astype(BF16)
        if slab:
            for s in range(tk // LANES):
                acc_ref[s * LANES:(s + 1) * LANES, :] += _tn(a_ref[s].astype(BF16), bb)
        else:
            acc_ref[...] += _tn(a_ref[...].astype(BF16), bb)

        @pl.when(step == n_steps - 1)
        def _():
            o_ref[...] = acc_ref[...].astype(out_dtype)

    if slab:
        a_spec = pl.BlockSpec((tk // LANES, tm, LANES), lambda i, j, t: (i, t, 0))
    else:
        a_spec = pl.BlockSpec((tm, tk), lambda i, j, t: (t, i))
    return pl.pallas_call(
        body, name=name, out_shape=jax.ShapeDtypeStruct((k, n), out_dtype), grid=(k // tk, n // tn, n_steps),
        in_specs=[a_spec, pl.BlockSpec((tm, tn), lambda i, j, t: (t, j))],
        out_specs=pl.BlockSpec((tk, tn), lambda i, j, t: (i, j)),
        scratch_shapes=[pltpu.VMEM((tk, tn), F32)],
        compiler_params=_params("arbitrary", "arbitrary", "arbitrary"),
    )(a, b)


def _row_spec(d, k):
    return pl.BlockSpec((1, 1, d), lambda b, i: (6 * b + k, 0, 0))


def modulate(x, mod, k_shift, k_scale, bl, name):
    t, d = x.shape
    s = t // bl
    tm = _tile(s, 512, 8)
    nt = s // tm

    def body(x_ref, sh_ref, sc_ref, o_ref):
        o_ref[...] = (x_ref[...] * (1.0 + sc_ref[0]) + sh_ref[0]).astype(BF16)

    return pl.pallas_call(
        body, name=name, out_shape=jax.ShapeDtypeStruct((t, d), BF16), grid=(bl, nt),
        in_specs=[pl.BlockSpec((tm, d), lambda b, i: (b * nt + i, 0)), _row_spec(d, k_shift), _row_spec(d, k_scale)],
        out_specs=pl.BlockSpec((tm, d), lambda b, i: (b * nt + i, 0)),
        compiler_params=_params("arbitrary", "arbitrary"),
    )(x, mod, mod)


def _layer_norm_stats(r):
    mu = jnp.mean(r, axis=-1, keepdims=True)
    rc = r - mu
    var = jnp.mean(rc * rc, axis=-1, keepdims=True)
    rstd = lax.rsqrt(var + NORM_EPS)
    return rc * rstd, rstd


def residual_layer_norm(x, y, mod, k_gate, g, b, bl, name, next_mod=None):
    t, d = x.shape
    s = t // bl
    tm = _tile(s, 256, 8)
    nt = s // tm
    has_next = next_mod is not None

    def body(*refs):
        x_ref, y_ref, gt_ref, g_ref, b_ref = refs[:5]
        rest = refs[5:]
        if has_next:
            sh_ref, sc_ref, o_ref, r_ref, u_ref = rest
        else:
            o_ref, r_ref = rest
        r = ALPHA * x_ref[...] + (1.0 + gt_ref[0]) * y_ref[...]
        xhat, _ = _layer_norm_stats(r)
        out = xhat * g_ref[...] + b_ref[...]
        o_ref[...] = out
        r_ref[...] = r
        if has_next:
            u_ref[...] = (out * (1.0 + sc_ref[0]) + sh_ref[0]).astype(BF16)

    tok = pl.BlockSpec((tm, d), lambda bb, i: (bb * nt + i, 0))
    vec = pl.BlockSpec((1, d), lambda bb, i: (0, 0))
    in_specs = [tok, tok, _row_spec(d, k_gate), vec, vec]
    args = [x, y, mod, g, b]
    out_shape = [jax.ShapeDtypeStruct((t, d), F32), jax.ShapeDtypeStruct((t, d), F32)]
    out_specs = [tok, tok]
    if has_next:
        in_specs += [_row_spec(d, next_mod[0]), _row_spec(d, next_mod[1])]
        args += [mod if len(next_mod) == 2 else next_mod[2]] * 2
        out_shape.append(jax.ShapeDtypeStruct((t, d), BF16))
        out_specs.append(tok)
    return pl.pallas_call(
        body, name=name, out_shape=out_shape, grid=(bl, nt), in_specs=in_specs, out_specs=out_specs,
        compiler_params=_params("arbitrary", "arbitrary"),
    )(*args)


def loss_head(xo, target, name):
    t, d = xo.shape
    tm = _tile(t, 512, 8)

    def body(x_ref, t_ref, l_ref, dx_ref):
        @pl.when(pl.program_id(0) == 0)
        def _():
            l_ref[...] = jnp.zeros_like(l_ref)

        e = x_ref[...] - t_ref[...]
        l_ref[...] += jnp.sum(e * e, axis=0, keepdims=True) * (0.5 / d)
        dx_ref[...] = e * (1.0 / d)

    tok = pl.BlockSpec((tm, d), lambda i: (i, 0))
    return pl.pallas_call(
        body, name=name,
        out_shape=[jax.ShapeDtypeStruct((1, d), F32), jax.ShapeDtypeStruct((t, d), F32)],
        grid=(t // tm,), in_specs=[tok, tok],
        out_specs=[pl.BlockSpec((1, d), lambda i: (0, 0)), tok],
        compiler_params=_params("arbitrary"),
    )(xo, target)


def sublayer_backward(d_a, bl, name, *, du=None, scale=None, x_in=None, ln=None):
    t, d = d_a.shape
    s = t // bl
    tm = _tile(s, 256, 8)
    nt = s // tm
    has_mod = du is not None
    has_ln = ln is not None
    assert has_mod or has_ln
    assert has_ln or x_in is not None

    def body(*refs):
        refs = list(refs)
        da_ref = refs.pop(0)
        if has_mod:
            du_ref, sc_ref = refs.pop(0), refs.pop(0)
        if has_ln:
            r_ref, y_ref, g_ref, b_ref, gt_ref = (refs.pop(0) for _ in range(5))
        elif has_mod:
            xin_ref = refs.pop(0)
        dx_ref = refs.pop(0)
        if has_ln:
            dy_ref, dg_ref, db_ref, dgt_ref = (refs.pop(0) for _ in range(4))
        if has_mod:
            dsc_ref, dsh_ref = refs.pop(0), refs.pop(0)
        first_tile = pl.program_id(1) == 0
        first_step = jnp.logical_and(pl.program_id(0) == 0, first_tile)

        dout = da_ref[...]
        if has_ln:
            xhat, rstd = _layer_norm_stats(r_ref[...])
        if has_mod:
            duv = du_ref[...]
            dout = dout + duv * (1.0 + sc_ref[0])
            xin = xhat * g_ref[...] + b_ref[...] if has_ln else xin_ref[...]

            @pl.when(first_tile)
            def _():
                dsc_ref[...] = jnp.zeros_like(dsc_ref)
                dsh_ref[...] = jnp.zeros_like(dsh_ref)

            dsc_ref[0] += jnp.sum(duv * xin, axis=0, keepdims=True)
            dsh_ref[0] += jnp.sum(duv, axis=0, keepdims=True)
        if not has_ln:
            dx_ref[...] = dout
            return

        @pl.when(first_step)
        def _():
            dg_ref[...] = jnp.zeros_like(dg_ref)
            db_ref[...] = jnp.zeros_like(db_ref)

        @pl.when(first_tile)
        def _():
            dgt_ref[...] = jnp.zeros_like(dgt_ref)

        dg_ref[...] += jnp.sum(dout * xhat, axis=0, keepdims=True)
        db_ref[...] += jnp.sum(dout, axis=0, keepdims=True)
        dxh = dout * g_ref[...]
        dr = rstd * (dxh - jnp.mean(dxh, axis=-1, keepdims=True) - xhat * jnp.mean(dxh * xhat, axis=-1, keepdims=True))
        dx_ref[...] = ALPHA * dr
        dy_ref[...] = ((1.0 + gt_ref[0]) * dr).astype(BF16)
        dgt_ref[0] += jnp.sum(dr * y_ref[...], axis=0, keepdims=True)

    tok = pl.BlockSpec((tm, d), lambda bb, i: (bb * nt + i, 0))
    vec = pl.BlockSpec((1, d), lambda bb, i: (0, 0))
    seq = pl.BlockSpec((1, 1, d), lambda bb, i: (bb, 0, 0))
    in_specs, args = [tok], [d_a]
    if has_mod:
        in_specs += [tok, _row_spec(d, scale[1])]
        args += [du, scale[0]]
    if has_ln:
        r, y, g, b, gate = ln
        in_specs += [tok, tok, vec, vec, _row_spec(d, gate[1])]
        args += [r, y, g, b, gate[0]]
    elif has_mod:
        in_specs.append(tok)
        args.append(x_in)
    names = ["dx"]
    out_shape, out_specs = [jax.ShapeDtypeStruct((t, d), F32)], [tok]
    if has_ln:
        names += ["dy", "dg", "db", "dgate"]
        out_shape += [jax.ShapeDtypeStruct((t, d), BF16), jax.ShapeDtypeStruct((1, d), F32),
                      jax.ShapeDtypeStruct((1, d), F32), jax.ShapeDtypeStruct((bl, 1, d), F32)]
        out_specs += [tok, vec, vec, seq]
    if has_mod:
        names += ["dscale", "dshift"]
        out_shape += [jax.ShapeDtypeStruct((bl, 1, d), F32)] * 2
        out_specs += [seq, seq]
    outs = pl.pallas_call(
        body, name=name, out_shape=out_shape, grid=(bl, nt), in_specs=in_specs, out_specs=out_specs,
        compiler_params=_params("arbitrary", "arbitrary"),
    )(*args)
    return dict(zip(names, outs))


def _silu(a):
    return a * jax.nn.sigmoid(a)


def silu_rows(a, name):
    def body(a_ref, o_ref):
        o_ref[...] = _silu(a_ref[...]).astype(BF16)

    return pl.pallas_call(body, name=name, out_shape=jax.ShapeDtypeStruct(a.shape, BF16))(a)


def swiglu_forward(a, b, name):
    t, f = a.shape
    tm, tf = _tile(t, 512, 8), _tile(f, 1536)

    def body(a_ref, b_ref, h_ref):
        h_ref[...] = (_silu(a_ref[...]) * b_ref[...]).astype(BF16)

    spec = pl.BlockSpec((tm, tf), lambda i, j: (i, j))
    return pl.pallas_call(
        body, name=name, out_shape=jax.ShapeDtypeStruct((t, f), BF16), grid=(t // tm, f // tf),
        in_specs=[spec, spec], out_specs=spec, compiler_params=_params("arbitrary", "arbitrary"),
    )(a, b)


def swiglu_backward(dh, a, b, name):
    t, f = a.shape
    tm, tf = _tile(t, 512, 8), _tile(f, 1536)

    def body(dh_ref, a_ref, b_ref, da_ref, db_ref):
        av = a_ref[...]
        sig = jax.nn.sigmoid(av)
        dhv = dh_ref[...]
        da_ref[...] = (dhv * b_ref[...] * (sig * (1.0 + av * (1.0 - sig)))).astype(BF16)
        db_ref[...] = (dhv * (av * sig)).astype(BF16)

    spec = pl.BlockSpec((tm, tf), lambda i, j: (i, j))
    return pl.pallas_call(
        body, name=name, out_shape=[jax.ShapeDtypeStruct((t, f), BF16)] * 2, grid=(t // tm, f // tf),
        in_specs=[spec, spec, spec], out_specs=[spec, spec], compiler_params=_params("arbitrary", "arbitrary"),
    )(dh, a, b)


def rope_tables(pos, inv_freq, sign, name):
    t = pos.shape[0]
    tm = _tile(t, 512, 8)

    def body(p_ref, f_ref, s_ref, c_out, s_out):
        ang = p_ref[...] * f_ref[...]
        c_out[...] = jnp.cos(ang)
        s_out[...] = jnp.sin(ang) * s_ref[...]

    vec = pl.BlockSpec((1, LANES), lambda i: (0, 0))
    tab = pl.BlockSpec((tm, LANES), lambda i: (i, 0))
    return pl.pallas_call(
        body, name=name, out_shape=[jax.ShapeDtypeStruct((t, LANES), F32)] * 2, grid=(t // tm,),
        in_specs=[pl.BlockSpec((tm, 1), lambda i: (i, 0)), vec, vec], out_specs=[tab, tab],
        compiler_params=_params("arbitrary"),
    )(pos, inv_freq, sign)


def _rot_half(v):
    lane = lax.broadcasted_iota(jnp.int32, v.shape, v.ndim - 1)
    up = pltpu.roll(v, LANES - MLA_ROPE // 2, v.ndim - 1)
    down = pltpu.roll(v, MLA_ROPE // 2, v.ndim - 1)
    return jnp.where(lane % MLA_ROPE < MLA_ROPE // 2, up, down)


def _rope(v, cos, sin_signed):
    return v * cos + _rot_half(v) * sin_signed


def _rope_transposed(dv, cos, sin_signed):
    return dv * cos + _rot_half(dv * sin_signed)


def rope_slabs(v, cos, sin_signed, out_dtype, name, transposed=False):
    ns, t, _ = v.shape
    tm = _tile(t, 512, 8)
    fn = _rope_transposed if transposed else _rope

    def body(v_ref, c_ref, s_ref, o_ref):
        o_ref[0] = fn(v_ref[0].astype(F32), c_ref[...], s_ref[...]).astype(out_dtype)

    tab = pl.BlockSpec((tm, LANES), lambda j, i: (i, 0))
    spec = pl.BlockSpec((1, tm, LANES), lambda j, i: (j, i, 0))
    return pl.pallas_call(
        body, name=name, out_shape=jax.ShapeDtypeStruct(v.shape, out_dtype), grid=(ns, t // tm),
        in_specs=[spec, tab, tab], out_specs=spec, compiler_params=_params("arbitrary", "arbitrary"),
    )(v, cos, sin_signed)


def _rms(x):
    rinv = lax.rsqrt(jnp.mean(x * x, axis=-1, keepdims=True) + NORM_EPS)
    return x * rinv, rinv


def mla_latents_forward(h_in, g_q, g_kv, cos, sin_signed, name):
    t = h_in.shape[0]
    tm = _tile(t, 512, 8)

    def body(h_ref, gq_ref, gkv_ref, c_ref, s_ref, cq_ref, ckv_ref, kr_ref):
        cq_ref[...] = (_rms(h_ref[:, 0:MLA_QR])[0] * gq_ref[...]).astype(BF16)
        ckv_ref[...] = (_rms(h_ref[:, MLA_QR:MLA_QR + MLA_KVR])[0] * gkv_ref[...]).astype(BF16)
        kr_ref[...] = _rope(h_ref[:, MLA_QR + MLA_KVR:], c_ref[...], s_ref[...]).astype(BF16)

    def tok(w):
        return pl.BlockSpec((tm, w), lambda i: (i, 0))

    def vec(w):
        return pl.BlockSpec((1, w), lambda i: (0, 0))

    return pl.pallas_call(
        body, name=name,
        out_shape=[jax.ShapeDtypeStruct((t, MLA_QR), BF16), jax.ShapeDtypeStruct((t, MLA_KVR), BF16),
                   jax.ShapeDtypeStruct((t, LANES), BF16)],
        grid=(t // tm,),
        in_specs=[tok(h_in.shape[1]), vec(MLA_QR), vec(MLA_KVR), tok(LANES), tok(LANES)],
        out_specs=[tok(MLA_QR), tok(MLA_KVR), tok(LANES)],
        compiler_params=_params("arbitrary"),
    )(h_in, g_q, g_kv, cos, sin_signed)


def mla_latents_backward(h_in, dcq, dckv, dkr, g_q, g_kv, cos, sin_signed, name):
    t, w = h_in.shape
    tm = _tile(t, 512, 8)

    def body(h_ref, dcq_ref, dckv_ref, dkr_ref, gq_ref, gkv_ref, c_ref, s_ref, dh_ref, dgq_ref, dgkv_ref):
        @pl.when(pl.program_id(0) == 0)
        def _():
            dgq_ref[...] = jnp.zeros_like(dgq_ref)
            dgkv_ref[...] = jnp.zeros_like(dgkv_ref)

        def rms_bwd(x, dc, g_ref, dg_ref):
            xn, rinv = _rms(x)
            dg_ref[...] += jnp.sum(dc * xn, axis=0, keepdims=True)
            dxn = dc * g_ref[...]
            return rinv * (dxn - xn * jnp.mean(dxn * xn, axis=-1, keepdims=True))

        dq = rms_bwd(h_ref[:, 0:MLA_QR], dcq_ref[...], gq_ref, dgq_ref)
        dkv = rms_bwd(h_ref[:, MLA_QR:MLA_QR + MLA_KVR], dckv_ref[...], gkv_ref, dgkv_ref)
        dr = _rope_transposed(dkr_ref[...], c_ref[...], s_ref[...])
        dh_ref[...] = jnp.concatenate([dq, dkv, dr], axis=1).astype(BF16)

    def tok(ww):
        return pl.BlockSpec((tm, ww), lambda i: (i, 0))

    def vec(ww):
        return pl.BlockSpec((1, ww), lambda i: (0, 0))

    return pl.pallas_call(
        body, name=name,
        out_shape=[jax.ShapeDtypeStruct((t, w), BF16), jax.ShapeDtypeStruct((1, MLA_QR), F32),
                   jax.ShapeDtypeStruct((1, MLA_KVR), F32)],
        grid=(t // tm,),
        in_specs=[tok(w), tok(MLA_QR), tok(MLA_KVR), tok(LANES), vec(MLA_QR), vec(MLA_KVR), tok(LANES), tok(LANES)],
        out_specs=[tok(w), vec(MLA_QR), vec(MLA_KVR)],
        compiler_params=_params("arbitrary"),
    )(h_in, dcq, dckv, dkr, g_q, g_kv, cos, sin_signed)


def _tri(n, lower):
    r = lax.broadcasted_iota(jnp.int32, (n, n), 0)
    c = lax.broadcasted_iota(jnp.int32, (n, n), 1)
    return jnp.where(r >= c if lower else r <= c, 1.0, 0.0).astype(F32)


def _dot_exact(tri, v):
    hi = v.astype(BF16)
    mid = (v - hi.astype(F32)).astype(BF16)
    lo = (v - hi.astype(F32) - mid.astype(F32)).astype(BF16)
    t = tri.astype(BF16)
    return _nn(t, hi) + _nn(t, mid) + _nn(t, lo)


def fox_gate_forward(z, b_f, bl, name):
    t = z.shape[0]
    s = t // bl
    ch = LANES
    n_ch = s // ch

    def body(z_ref, b_ref, f_ref, fs_ref):
        tri = _tri(ch, True)
        carry = jnp.zeros((1, LANES), F32)
        for k in range(n_ch):
            x = z_ref[k * ch:(k + 1) * ch, :] + b_ref[...]
            logf = jnp.minimum(x, 0.0) - jnp.log(1.0 + jnp.exp(-jnp.abs(x)))
            cs = _dot_exact(tri, logf) + carry
            carry = cs[ch - 1:ch, :]
            f_ref[k * ch:(k + 1) * ch, :] = cs
            for h in range(FOX_HEADS):
                fs_ref[h, k * ch:(k + 1) * ch, :] = jnp.broadcast_to(cs[:, h:h + 1], (ch, LANES))

    return pl.pallas_call(
        body, name=name,
        out_shape=[jax.ShapeDtypeStruct((t, LANES), F32), jax.ShapeDtypeStruct((FOX_HEADS, t, LANES), F32)],
        grid=(bl,),
        in_specs=[pl.BlockSpec((s, LANES), lambda b: (b, 0)), pl.BlockSpec((1, LANES), lambda b: (0, 0))],
        out_specs=[pl.BlockSpec((s, LANES), lambda b: (b, 0)),
                   pl.BlockSpec((FOX_HEADS, s, LANES), lambda b: (0, b, 0))],
        compiler_params=_params("arbitrary"),
    )(z, b_f)


def fox_gate_backward(z, b_f, df, bl, name):
    t = z.shape[0]
    s = t // bl
    ch = LANES
    n_ch = s // ch

    def body(z_ref, b_ref, df_ref, dz_ref, db_ref):
        @pl.when(pl.program_id(0) == 0)
        def _():
            db_ref[...] = jnp.zeros_like(db_ref)

        tri = _tri(ch, False)
        carry = jnp.zeros((1, LANES), F32)
        for k in reversed(range(n_ch)):
            cs = _dot_exact(tri, df_ref[k * ch:(k + 1) * ch, :]) + carry
            carry = cs[0:1, :]
            x = z_ref[k * ch:(k + 1) * ch, :] + b_ref[...]
            dz = cs * (1.0 - jax.nn.sigmoid(x))
            dz_ref[k * ch:(k + 1) * ch, :] = dz
            db_ref[...] += jnp.sum(dz, axis=0, keepdims=True)

    tok = pl.BlockSpec((s, LANES), lambda b: (b, 0))
    vec = pl.BlockSpec((1, LANES), lambda b: (0, 0))
    return pl.pallas_call(
        body, name=name,
        out_shape=[jax.ShapeDtypeStruct((t, LANES), F32), jax.ShapeDtypeStruct((1, LANES), F32)],
        grid=(bl,), in_specs=[tok, vec, tok], out_specs=[tok, vec],
        compiler_params=_params("arbitrary"),
    )(z, b_f, df)


NEG_INF = float("-inf")


def _attn_tiles(s):
    return _tile(s, 512, 8)


def attention_forward(kind, ops, bl, scale, name):
    fox = kind == "fox"
    if fox:
        qkv, fq, fk = ops
        t = qkv.shape[1]
        n_pair = FOX_HEADS // 2
    else:
        qn, qr, kn, kr, v = ops
        t = qn.shape[1]
        n_pair = MLA_HEADS // 2
    s = t // bl
    tq = _attn_tiles(s)
    nq = s // tq
    half = LANES // 2

    def body(*refs):
        if fox:
            q_ref, k_ref, v_ref, fq_ref, fk_ref, o_ref, lse_ref, o32_ref = refs
        else:
            qn_ref, qr_ref, kn_ref, kr_ref, v_ref, o_ref, lse_ref = refs
        i = pl.program_id(2)
        row = lax.broadcasted_iota(jnp.int32, (tq, tq), 0)
        col = lax.broadcasted_iota(jnp.int32, (tq, tq), 1)
        heads = []
        for e in range(2):
            sl = slice(e * half, (e + 1) * half)
            if fox:
                heads.append((sl, q_ref[0, :, sl], None))
            else:
                heads.append((sl, qn_ref[e], qr_ref[0, :, sl]))
        dv = half if fox else LANES

        def wide(stat):
            return jnp.concatenate([stat] * (tq // LANES), axis=1)

        def step(j, carry, masked):
            rows = pl.ds(pl.multiple_of(j * tq, tq), tq)
            new = []
            for e, (sl, qa, qb) in enumerate(heads):
                m, l, acc = carry[e]
                if fox:
                    sc = _nt(qa, k_ref[0, rows, sl]) * scale + wide(fq_ref[e]) - fk_ref[0, j, e:e + 1, :]
                    vv = v_ref[0, rows, sl]
                else:
                    sc = (_nt(qa, kn_ref[e, rows, :]) + _nt(qb, kr_ref[rows, 0:half])) * scale
                    vv = v_ref[e, rows, :]
                if masked:
                    sc = jnp.where(row >= col, sc, NEG_INF)
                m_new = jnp.maximum(m, jnp.max(sc, axis=1, keepdims=True))
                p = jnp.exp(sc - m_new)
                a = jnp.exp(m - m_new)
                l = a * l + jnp.sum(p, axis=1, keepdims=True)
                p_hi = p.astype(BF16)
                acc = a * acc + _nn(p_hi, vv)
                if fox:
                    acc = acc + _nn((p - p_hi.astype(F32)).astype(BF16), vv)
                new.append((m_new, l, acc))
            return tuple(new)

        init = (jnp.full((tq, 1), NEG_INF, F32), jnp.zeros((tq, 1), F32), jnp.zeros((tq, dv), F32))
        carry = step(i, (init, init), True)
        carry = lax.fori_loop(0, i, lambda j, c: step(j, c, False), carry)
        outs = [acc / l for _, l, acc in carry]
        for e, (m, l, _) in enumerate(carry):
            lse_ref[e] = jnp.broadcast_to(m + jnp.log(l), (tq, LANES))
        if fox:
            o32 = jnp.concatenate(outs, axis=1)
            o32_ref[0] = o32
            o_ref[0] = o32.astype(BF16)
        else:
            o_ref[0] = outs[0].astype(BF16)
            o_ref[1] = outs[1].astype(BF16)

    def q_idx(b, g, i):
        return (g, b * nq + i, 0)

    if fox:
        nk = fk.shape[1]
        in_specs = [pl.BlockSpec((1, tq, LANES), q_idx),
                    pl.BlockSpec((1, s, LANES), lambda b, g, i: (n_pair + g, b, 0)),
                    pl.BlockSpec((1, s, LANES), lambda b, g, i: (2 * n_pair + g, b, 0)),
                    pl.BlockSpec((2, tq, LANES), q_idx),
                    pl.BlockSpec((1, nk, 8, tq), lambda b, g, i: (b * n_pair + g, 0, 0, 0))]
        args = [qkv, qkv, qkv, fq, fk]
        o_spec = pl.BlockSpec((1, tq, LANES), q_idx)
    else:
        in_specs = [pl.BlockSpec((2, tq, LANES), q_idx),
                    pl.BlockSpec((1, tq, LANES), q_idx),
                    pl.BlockSpec((2, s, LANES), lambda b, g, i: (g, b, 0)),
                    pl.BlockSpec((s, LANES), lambda b, g, i: (b, 0)),
                    pl.BlockSpec((2, s, LANES), lambda b, g, i: (g, b, 0))]
        args = [qn, qr, kn, kr, v]
        o_spec = pl.BlockSpec((2, tq, LANES), q_idx)
    out_shape = [jax.ShapeDtypeStruct((8, t, LANES), BF16), jax.ShapeDtypeStruct((2 * n_pair, t, LANES), F32)]
    out_specs = [o_spec, pl.BlockSpec((2, tq, LANES), q_idx)]
    if fox:
        out_shape.append(jax.ShapeDtypeStruct((8, t, LANES), F32))
        out_specs.append(o_spec)
    outs = pl.pallas_call(
        body, name=name, out_shape=out_shape, grid=(bl, n_pair, nq), in_specs=in_specs, out_specs=out_specs,
        compiler_params=_params("arbitrary", "arbitrary", "arbitrary"),
    )(*args)
    return (outs[0], outs[1], outs[2] if fox else outs[0])


def attention_backward(kind, ops, o, do, lse, bl, scale, name):
    fox = kind == "fox"
    if fox:
        qkv, fq, fk = ops
        t = qkv.shape[1]
        n_pair = FOX_HEADS // 2
    else:
        qn, qr, kn, kr, v = ops
        t = qn.shape[1]
        n_pair = MLA_HEADS // 2
    s = t // bl
    tq = _attn_tiles(s)
    nq = s // tq
    half = LANES // 2

    def body(*refs):
        if fox:
            (q_ref, k_ref, v_ref, fq_ref, fk_ref, o_ref, do_ref, lse_ref,
             dq_ref, dk_ref, dv_ref, dfk_ref, delta_scr, qt_scr, dot_scr) = refs
        else:
            (qn_ref, qr_ref, kn_ref, kr_ref, v_ref, o_ref, do_ref, lse_ref,
             dqn_ref, dqr_ref, dkn_ref, dv_ref, dkr_ref, delta_scr, qt_scr, qrt_scr, dot_scr) = refs
        g, j = pl.program_id(1), pl.program_id(2)
        row = lax.broadcasted_iota(jnp.int32, (tq, tq), 0)
        col = lax.broadcasted_iota(jnp.int32, (tq, tq), 1)
        krows = pl.ds(pl.multiple_of(j * tq, tq), tq)

        def transposed(v):
            return v.astype(F32).T.astype(BF16)

        def wide(stat):
            return jnp.concatenate([stat] * (tq // LANES), axis=1)

        @pl.when(j == 0)
        def _():
            if fox:
                dq_ref[...] = jnp.zeros_like(dq_ref)
            else:
                dqn_ref[...] = jnp.zeros_like(dqn_ref)
                dqr_ref[...] = jnp.zeros_like(dqr_ref)
            for ii in range(nq):
                rws = slice(ii * tq, (ii + 1) * tq)
                deltas = []
                if fox:
                    prod = do_ref[0, rws, :].astype(F32) * o_ref[0, rws, :].astype(F32)
                    for e in range(2):
                        deltas.append(jnp.sum(prod[:, e * half:(e + 1) * half], axis=1, keepdims=True))
                    qt_scr[ii] = transposed(q_ref[0, rws, :])
                    dot_scr[ii] = transposed(do_ref[0, rws, :])
                else:
                    for e in range(2):
                        prod = do_ref[e, rws, :].astype(F32) * o_ref[e, rws, :].astype(F32)
                        deltas.append(jnp.sum(prod, axis=1, keepdims=True))
                        qt_scr[e, ii] = transposed(qn_ref[e, rws, :])
                        dot_scr[e, ii] = transposed(do_ref[e, rws, :])
                    qrt_scr[ii] = transposed(qr_ref[0, rws, :])
                for e in range(2):
                    delta_scr[e, rws, :] = jnp.broadcast_to(deltas[e], (tq, LANES))

        if fox:
            dfk_ref[...] = jnp.zeros_like(dfk_ref)
        else:
            @pl.when(jnp.logical_and(g == 0, j == 0))
            def _():
                dkr_ref[...] = jnp.zeros_like(dkr_ref)

        heads = []
        for e in range(2):
            sl = slice(e * half, (e + 1) * half)
            if fox:
                heads.append((sl, k_ref[0, :, sl], v_ref[0, :, sl], fk_ref[0, 0, e:e + 1, :]))
            else:
                heads.append((sl, kn_ref[e], v_ref[e], kr_ref[krows, 0:half]))
        dk_w = dv_w = half if fox else LANES

        def step(i, carry, masked):
            rows = pl.ds(pl.multiple_of(i * tq, tq), tq)
            new = []
            for e, (sl, k_e, v_e, x_e) in enumerate(heads):
                dk_acc, dv_acc, last = carry[e]
                if fox:
                    do_i = do_ref[0, rows, sl]
                    sc = _nt(q_ref[0, rows, sl], k_e) * scale + wide(fq_ref[e, rows, :]) - x_e
                else:
                    do_i = do_ref[e, rows, :]
                    sc = (_nt(qn_ref[e, rows, :], k_e) + _nt(qr_ref[0, rows, sl], x_e)) * scale
                if masked:
                    sc = jnp.where(row >= col, sc, NEG_INF)
                p = jnp.exp(sc - wide(lse_ref[e, rows, :]))
                dp = _nt(do_i, v_e)
                ds = p * (dp - wide(delta_scr[e, rows, :]))
                dsb = (ds * scale).astype(BF16)
                if fox:
                    fsl = slice(e * half, (e + 1) * half)
                    dv_acc = dv_acc + _nn(dot_scr[i, fsl, :], p.astype(BF16))
                    dk_acc = dk_acc + _nn(qt_scr[i, fsl, :], dsb)
                    dq_ref[0, rows, sl] += _nn(dsb, k_e)
                    last = last - jnp.sum(ds, axis=0, keepdims=True)
                else:
                    dv_acc = dv_acc + _nn(dot_scr[e, i], p.astype(BF16))
                    dk_acc = dk_acc + _nn(qt_scr[e, i], dsb)
                    dqn_ref[e, rows, :] += _nn(dsb, k_e)
                    dqr_ref[0, rows, sl] += _nn(dsb, x_e)
                    last = last + _nn(qrt_scr[i, e * half:(e + 1) * half, :], dsb)
                new.append((dk_acc, dv_acc, last))
            return tuple(new)

        last0 = jnp.zeros((1, tq), F32) if fox else jnp.zeros((half, tq), F32)
        init = (jnp.zeros((dk_w, tq), F32), jnp.zeros((dv_w, tq), F32), last0)
        carry = step(j, (init, init), True)
        carry = lax.fori_loop(j + 1, nq, lambda i, c: step(i, c, False), carry)
        if fox:
            for e in range(2):
                dfk_ref[0, 0, e:e + 1, :] = carry[e][2]
            dk_ref[0] = jnp.concatenate([carry[0][0], carry[1][0]], axis=0).T.astype(BF16)
            dv_ref[0] = jnp.concatenate([carry[0][1], carry[1][1]], axis=0).T.astype(BF16)
        else:
            for e in range(2):
                dkn_ref[e] = carry[e][0].T.astype(BF16)
                dv_ref[e] = carry[e][1].T.astype(BF16)
            dkr_t = carry[0][2] + carry[1][2]
            dkr_ref[krows, :] += jnp.concatenate([dkr_t, jnp.zeros_like(dkr_t)], axis=0).T

    def whole(b, g, j):
        return (g, b, 0)

    def kblk(b, g, j):
        return (g, b * nq + j, 0)

    if fox:
        in_specs = [pl.BlockSpec((1, s, LANES), whole),
                    pl.BlockSpec((1, tq, LANES), lambda b, g, j: (n_pair + g, b * nq + j, 0)),
                    pl.BlockSpec((1, tq, LANES), lambda b, g, j: (2 * n_pair + g, b * nq + j, 0)),
                    pl.BlockSpec((2, s, LANES), whole),
                    pl.BlockSpec((1, 1, 8, tq), lambda b, g, j: (b * n_pair + g, j, 0, 0)),
                    pl.BlockSpec((1, s, LANES), whole), pl.BlockSpec((1, s, LANES), whole),
                    pl.BlockSpec((2, s, LANES), whole)]
        args = [qkv, qkv, qkv, fq, fk, o, do, lse]
        out_shape = [jax.ShapeDtypeStruct((8, t, LANES), F32), jax.ShapeDtypeStruct((8, t, LANES), BF16),
                     jax.ShapeDtypeStruct((8, t, LANES), BF16), jax.ShapeDtypeStruct(fk.shape, F32)]
        out_specs = [pl.BlockSpec((1, s, LANES), whole), pl.BlockSpec((1, tq, LANES), kblk),
                     pl.BlockSpec((1, tq, LANES), kblk),
                     pl.BlockSpec((1, 1, 8, tq), lambda b, g, j: (b * n_pair + g, j, 0, 0))]
    else:
        pair = pl.BlockSpec((2, s, LANES), whole)
        pair_k = pl.BlockSpec((2, tq, LANES), kblk)
        in_specs = [pair, pl.BlockSpec((1, s, LANES), whole), pair_k,
                    pl.BlockSpec((s, LANES), lambda b, g, j: (b, 0)), pair_k,
                    pair, pair, pair]
        args = [qn, qr, kn, kr, v, o, do, lse]
        out_shape = [jax.ShapeDtypeStruct((8, t, LANES), F32), jax.ShapeDtypeStruct((4, t, LANES), F32),
                     jax.ShapeDtypeStruct((8, t, LANES), BF16), jax.ShapeDtypeStruct((8, t, LANES), BF16),
                     jax.ShapeDtypeStruct((t, LANES), F32)]
        out_specs = [pair, pl.BlockSpec((1, s, LANES), whole), pair_k, pair_k,
                     pl.BlockSpec((s, LANES), lambda b, g, j: (b, 0))]
    t_blocks = pltpu.VMEM((nq, LANES, tq), BF16)
    t_pairs = pltpu.VMEM((2, nq, LANES, tq), BF16)
    scratch = [pltpu.VMEM((2, s, LANES), F32)] + ([t_blocks, t_blocks] if fox else [t_pairs, t_blocks, t_pairs])
    return pl.pallas_call(
        body, name=name, out_shape=out_shape, grid=(bl, n_pair, nq), in_specs=in_specs, out_specs=out_specs,
        scratch_shapes=scratch, compiler_params=_params("arbitrary", "arbitrary", "arbitrary"),
    )(*args)


def adamw(w, g, m, v, name):
    shape = w.shape
    c = shape[-1]
    r = w.size // c
    tr = _tile(r, 512, 8)

    def body(w_ref, g_ref, m_ref, v_ref, d_ref, nm_ref, nv_ref):
        gv = g_ref[...]
        m2 = ADAM_B1 * m_ref[...] + (1.0 - ADAM_B1) * gv
        v2 = ADAM_B2 * v_ref[...] + (1.0 - ADAM_B2) * (gv * gv)
        m_hat = m2 / (1.0 - ADAM_B1 ** ADAM_STEP)
        v_hat = v2 / (1.0 - ADAM_B2 ** ADAM_STEP)
        d_ref[...] = -ADAM_LR * (m_hat / (jnp.sqrt(v_hat) + ADAM_EPS) + ADAM_WD * w_ref[...])
        nm_ref[...] = m2
        nv_ref[...] = v2

    spec = pl.BlockSpec((tr, c), lambda i: (i, 0))
    outs = pl.pallas_call(
        body, name=name, out_shape=[jax.ShapeDtypeStruct((r, c), F32)] * 3, grid=(r // tr,),
        in_specs=[spec] * 4, out_specs=[spec] * 3, compiler_params=_params("arbitrary"),
    )(*(a.reshape(r, c) for a in (w, g, m, v)))
    return tuple(a.reshape(shape) for a in outs)


PACK_COLS = 1024


def _pack_rows(a):
    return a.reshape(-1, PACK_COLS)


def kernel(x, c, positions, mla_w_in, mla_g_q, mla_w_uq, mla_g_kv, mla_w_uk, mla_w_uv, mla_w_o, fox_w_in, fox_b_f, fox_w_o, ada_w, ada_b, ffn_w_gate, ffn_w_up, ffn_w_down, ln_g, ln_b, loss_target, m_mla_w_in, m_mla_g_q, m_mla_w_uq, m_mla_g_kv, m_mla_w_uk, m_mla_w_uv, m_mla_w_o, m_fox_w_in, m_fox_b_f, m_fox_w_o, m_ada_w, m_ada_b, m_ffn_w_gate, m_ffn_w_up, m_ffn_w_down, m_ln_g, m_ln_b, v_mla_w_in, v_mla_g_q, v_mla_w_uq, v_mla_g_kv, v_mla_w_uk, v_mla_w_uv, v_mla_w_o, v_fox_w_in, v_fox_b_f, v_fox_w_o, v_ada_w, v_ada_b, v_ffn_w_gate, v_ffn_w_up, v_ffn_w_down, v_ln_g, v_ln_b):
    bl, s, d = x.shape
    t = bl * s
    ff = ffn_w_gate.shape[-1] * N_DEV
    dev = 4 * lax.axis_index("x") + 2 * lax.axis_index("y") + lax.axis_index("c")
    ada_cols = ada_w.shape[-1]
    fox_in = fox_w_in.shape[-1] * N_DEV
    mla_in = mla_w_in.shape[-1]
    mla_in_pad = mla_in + (-mla_in) % LANES

    def t_last(a):
        return jnp.swapaxes(a, -1, -2)

    local = {
        "mla_w_in": mla_w_in[0],
        "mla_w_uq": t_last(mla_w_uq[0]),
        "mla_w_uk": t_last(mla_w_uk[0]),
        "mla_w_uv": t_last(mla_w_uv[0]),
        "mla_w_o": mla_w_o[0],
        "fox_w_in": t_last(fox_w_in[0]),
        "fox_w_o": fox_w_o[0],
    }
    for i in range(DEPTH):
        local.update({f"gate{i}": t_last(ffn_w_gate[i]), f"up{i}": t_last(ffn_w_up[i]), f"down{i}": ffn_w_down[i]})
    groups = [["mla_w_in", "mla_w_uq", "mla_w_uk", "mla_w_uv", "mla_w_o"],
              ["gate0", "up0", "down0"],
              ["fox_w_in", "fox_w_o"],
              ["gate1", "up1", "down1"]]
    offsets, rows_of, slot_of, group_of = {}, {}, {}, {}
    group_rows = []
    for gi, names in enumerate(groups):
        rows = 0
        for nm in names:
            rows_of[nm] = local[nm].size // PACK_COLS
            slot_of[nm] = rows_of[nm] + (-rows_of[nm]) % 16
            offsets[nm] = rows
            group_of[nm] = gi
            rows += slot_of[nm]
        group_rows.append(rows)

    def slot(nm, rows):
        pad = [(0, 0)] * rows.ndim
        pad[-2] = (0, slot_of[nm] - rows_of[nm])
        return jnp.pad(rows, pad)

    def landing(block):
        land = lax.empty((N_DEV,) + block.shape, block.dtype)
        return lax.dynamic_update_slice(land, block[None], (dev, 0, 0))

    packed = [jnp.concatenate([slot(nm, _pack_rows(local[nm]).astype(BF16)) for nm in names], axis=0)
              for names in groups]
    gathered = [all_gather(packed[0], "gather_mla_weights")] + [None] * (len(groups) - 1)
    gather_started = [None] * len(groups)

    def depart(gi, after):
        block = lax.optimization_barrier((packed[gi], after))[0]
        gather_started[gi] = exchange_start(block, landing(block), f"gather_group{gi}_start", False)
        return gather_started[gi][4]

    def full(nm, cols):
        blk = gathered[group_of[nm]][:, offsets[nm]:offsets[nm] + rows_of[nm], :]
        return blk.reshape(-1, cols)

    w_in = jnp.pad(full("mla_w_in", mla_in), ((0, 0), (0, mla_in_pad - mla_in)))
    wt_uq = full("mla_w_uq", MLA_QR).reshape(MLA_HEADS, MLA_NOPE + MLA_ROPE, MLA_QR)
    wt_uq_n = wt_uq[:, :MLA_NOPE].reshape(MLA_HEADS * MLA_NOPE, MLA_QR)
    wt_uq_r = wt_uq[:, MLA_NOPE:].reshape(MLA_HEADS * MLA_ROPE, MLA_QR)
    wt_uk = full("mla_w_uk", MLA_KVR)
    wt_uv = full("mla_w_uv", MLA_KVR)
    w_mo = full("mla_w_o", d)
    wt_gate, wt_up, w_down = [None] * DEPTH, [None] * DEPTH, [None] * DEPTH

    def arrive(gi, after):
        gathered[gi] = exchange_wait(gather_started[gi], after, f"gather_group{gi}_wait", False)
        if gi + 1 < len(groups):
            gathered[gi] = lax.optimization_barrier((gathered[gi], depart(gi + 1, gathered[gi])))[0]
        for i in range(DEPTH):
            if group_of[f"gate{i}"] == gi:
                wt_gate[i], wt_up[i], w_down[i] = full(f"gate{i}", d), full(f"up{i}", d), full(f"down{i}", d)

    small = jnp.concatenate([c.reshape(-1, LANES), ln_g.reshape(-1, LANES), ln_b.reshape(-1, LANES)], axis=0)
    small_rows = small.shape[0]
    small = jnp.pad(small, ((0, (-small_rows) % 8), (0, 0)))
    small_all = all_gather(small, "gather_small")
    c_rows = bl * d // LANES
    c_all = small_all[:, :c_rows].reshape(N_DEV * bl, d)
    n_ln = DEPTH * 2
    ln_g_all = small_all[:, c_rows:c_rows + n_ln, :].transpose(1, 0, 2).reshape(DEPTH, 2, 1, d)
    ln_b_all = small_all[:, c_rows + n_ln:c_rows + 2 * n_ln, :].transpose(1, 0, 2).reshape(DEPTH, 2, 1, d)

    c_act = silu_rows(c_all, "silu_c")
    ada_b_loc = lax.dynamic_slice_in_dim(ada_b, dev * ada_cols, ada_cols, axis=1)
    mod_cols = [mm([(c_act, ada_w[i])], trans_b=False, out_dtype=F32, name=f"ada_fwd{i}", bias=ada_b_loc[i][None, :])
                for i in range(DEPTH)]
    mod_all = all_gather(jnp.concatenate(mod_cols, axis=0), "gather_mod")
    mod_all = mod_all.reshape(N_DEV, DEPTH, N_DEV * bl, ada_cols).transpose(1, 2, 0, 3).reshape(DEPTH, N_DEV * bl, 6 * d)
    mod_mine = lax.dynamic_slice_in_dim(mod_all, dev * bl, bl, axis=1)
    mods = [mod_mine[i].reshape(bl * 6, 1, d) for i in range(DEPTH)]
    mods[0] = mods[0] + depart(1, (mod_mine, gathered[0]))[0, 0]

    half_r = MLA_ROPE // 2
    inv_freq = ROPE_THETA ** (-jnp.arange(half_r, dtype=F32) / half_r)
    inv_freq = jnp.tile(inv_freq, LANES // half_r)[None, :]
    sign = jnp.tile(jnp.concatenate([-jnp.ones((half_r,), F32), jnp.ones((half_r,), F32)]), LANES // MLA_ROPE)[None, :]
    cos_t, sin_t = rope_tables(positions.astype(F32).reshape(t, 1), inv_freq, sign, "rope_tables")

    x2d = x.reshape(t, d)
    g_q, g_kv = mla_g_q.reshape(1, MLA_QR), mla_g_kv.reshape(1, MLA_KVR)
    b_f = jnp.pad(fox_b_f.reshape(1, FOX_HEADS), ((0, 0), (0, LANES - FOX_HEADS)))
    mla_scale = (MLA_NOPE + MLA_ROPE) ** -0.5
    fox_scale = FOX_HD ** -0.5
    tq = _attn_tiles(s)
    nk = s // tq

    saved = []
    u = modulate(x2d, mods[0], 0, 1, bl, "modulate0")
    xin = x2d
    for i in range(DEPTH):
        sv = {"u": u, "x_in": xin}
        if i % 2 == 0:
            h_in = mm([(u, w_in)], trans_b=False, out_dtype=F32, name=f"mla_in{i}")
            c_q, c_kv, k_r = mla_latents_forward(h_in, g_q, g_kv, cos_t, sin_t, f"mla_latents{i}")
            q_n = mm([(c_q, wt_uq_n)], trans_b=True, out_dtype=BF16, out_slab=True, name=f"mla_qn{i}")
            q_r_raw = mm([(c_q, wt_uq_r)], trans_b=True, out_dtype=F32, out_slab=True, name=f"mla_qr{i}")
            q_r = rope_slabs(q_r_raw, cos_t, sin_t, BF16, f"mla_qrope{i}")
            k_n = mm([(c_kv, wt_uk)], trans_b=True, out_dtype=BF16, out_slab=True, name=f"mla_kn{i}")
            v_m = mm([(c_kv, wt_uv)], trans_b=True, out_dtype=BF16, out_slab=True, name=f"mla_v{i}")
            ops = (q_n, q_r, k_n, k_r, v_m)
            o, lse, o_delta = attention_forward("mla", ops, bl, mla_scale, f"mla_attn{i}")
            y = mm([(o, w_mo)], trans_b=False, out_dtype=F32, name=f"mla_out{i}")
            sv.update(h_in=h_in, c_q=c_q, c_kv=c_kv, ops=ops, o=o, lse=lse, o_delta=o_delta)
        else:
            arrive(2, u)
            wt_fox = full("fox_w_in", d)
            wt_qkv = wt_fox[:3 * d]
            wt_f = jnp.pad(wt_fox[3 * d:], ((0, LANES - FOX_HEADS), (0, 0)))
            w_fo = full("fox_w_o", d)
            qkv = mm([(u, wt_qkv)], trans_b=True, out_dtype=BF16, out_slab=True, name=f"fox_qkv{i}")
            z = mm([(u, wt_f)], trans_b=True, out_dtype=F32, name=f"fox_z{i}")
            f_tok, f_q = fox_gate_forward(z, b_f, bl, f"fox_gate{i}")
            f_k = f_tok[:, :FOX_HEADS].reshape(bl, nk, tq, FOX_HEADS // 2, 2).transpose(0, 3, 1, 4, 2)
            f_k = jnp.pad(f_k.reshape(bl * FOX_HEADS // 2, nk, 2, tq), ((0, 0), (0, 0), (0, 6), (0, 0)))
            ops = (qkv, f_q, f_k)
            o, lse, o_delta = attention_forward("fox", ops, bl, fox_scale, f"fox_attn{i}")
            y = mm([(o, w_fo)], trans_b=False, out_dtype=F32, name=f"fox_out{i}")
            sv.update(z=z, ops=ops, o=o, lse=lse, o_delta=o_delta)
        x1, r1, u2 = residual_layer_norm(xin, y, mods[i], 2, ln_g_all[i, 0], ln_b_all[i, 0], bl, f"ln_mix{i}",
                                         next_mod=(3, 4))
        if wt_gate[i] is None:
            arrive(group_of[f"gate{i}"], u2)
        a = mm([(u2, wt_gate[i])], trans_b=True, out_dtype=F32, name=f"ffn_gate{i}")
        bb = mm([(u2, wt_up[i])], trans_b=True, out_dtype=F32, name=f"ffn_up{i}")
        h = swiglu_forward(a, bb, f"swiglu{i}")
        y2 = mm([(h, w_down[i])], trans_b=False, out_dtype=F32, name=f"ffn_down{i}")
        sv.update(y=y, r1=r1, u2=u2, a=a, bb=bb, h=h, y2=y2)
        if i + 1 < DEPTH:
            xin, r2, u = residual_layer_norm(x1, y2, mods[i], 5, ln_g_all[i, 1], ln_b_all[i, 1], bl, f"ln_ffn{i}",
                                             next_mod=(0, 1, mods[i + 1]))
        else:
            xin, r2 = residual_layer_norm(x1, y2, mods[i], 5, ln_g_all[i, 1], ln_b_all[i, 1], bl, f"ln_ffn{i}")
        sv.update(r2=r2)
        saved.append(sv)

    loss_cols, d_x = loss_head(xin, loss_target.reshape(t, d), "loss_head")

    grads_full = {}
    wgrad = functools.partial(mm_tn, out_dtype=BF16)
    dmod = [[None] * 6 for _ in range(DEPTH)]
    dg_ln = [[None, None] for _ in range(DEPTH)]
    db_ln = [[None, None] for _ in range(DEPTH)]
    dg_q = dg_kv = db_f = None
    d_a, du = d_x, None
    scatter_started = [None] * len(groups)

    def scatter_start(gi):
        g = jnp.concatenate(
            [slot(nm, grads_full[nm].reshape(N_DEV, rows_of[nm], PACK_COLS).astype(BF16)) for nm in groups[gi]], axis=1)
        own = lax.dynamic_index_in_dim(g, dev, 0, keepdims=False)
        scatter_started[gi] = exchange_start(g, landing(own), f"scatter_group{gi}_start", True)

    ln_g_bwd = [[ln_g_all[i, k] for k in range(2)] for i in range(DEPTH)]
    for i in reversed(range(DEPTH)):
        sv = saved[i]
        if i + 1 < DEPTH:
            gi = group_of["fox_w_in"]
            scatter_start(gi)
            ln_g_bwd[i][1] = after_token(ln_g_bwd[i][1], scatter_started[gi])
        ln2 = (sv["r2"], sv["y2"], ln_g_bwd[i][1], ln_b_all[i, 1], (mods[i], 5))
        if du is None:
            bw = sublayer_backward(d_a, bl, f"bwd_ln_ffn{i}", ln=ln2)
        else:
            bw = sublayer_backward(d_a, bl, f"bwd_ln_ffn{i}", du=du, scale=(mods[i + 1], 1), ln=ln2)
            dmod[i + 1][0], dmod[i + 1][1] = bw["dshift"], bw["dscale"]
        dmod[i][5], dg_ln[i][1], db_ln[i][1] = bw["dgate"], bw["dg"], bw["db"]
        dy2 = bw["dy"]
        dh = mm([(dy2, w_down[i])], trans_b=True, out_dtype=F32, name=f"bwd_ffn_dh{i}")
        da, dbb = swiglu_backward(dh, sv["a"], sv["bb"], f"bwd_swiglu{i}")
        du2 = mm([(da, wt_gate[i]), (dbb, wt_up[i])], trans_b=False, out_dtype=F32, name=f"bwd_ffn_du{i}")
        grads_full[f"down{i}"] = wgrad(sv["h"], dy2, name=f"bwd_w_down{i}")
        grads_full[f"gate{i}"] = wgrad(da, sv["u2"], name=f"bwd_w_gate{i}")
        grads_full[f"up{i}"] = wgrad(dbb, sv["u2"], name=f"bwd_w_up{i}")
        gi = group_of[f"gate{i}"]
        scatter_start(gi)
        ln_g_bwd[i][0] = after_token(ln_g_bwd[i][0], scatter_started[gi])
        bw = sublayer_backward(bw["dx"], bl, f"bwd_ln_mix{i}", du=du2, scale=(mods[i], 4),
                               ln=(sv["r1"], sv["y"], ln_g_bwd[i][0], ln_b_all[i, 0], (mods[i], 2)))
        dmod[i][3], dmod[i][4], dmod[i][2] = bw["dshift"], bw["dscale"], bw["dgate"]
        dg_ln[i][0], db_ln[i][0] = bw["dg"], bw["db"]
        d_a, dy = bw["dx"], bw["dy"]
        o, lse, ops = sv["o"], sv["lse"], sv["ops"]
        if i % 2 == 0:
            do = mm([(dy, w_mo)], trans_b=True, out_dtype=BF16, out_slab=True, name=f"bwd_mla_do{i}")
            grads_full["mla_w_o"] = wgrad(o, dy, name=f"bwd_w_mla_o{i}")
            dqn, dqr, dkn, dvm, dkr = attention_backward("mla", ops, sv["o_delta"], do, lse, bl, mla_scale,
                                                         f"bwd_mla_attn{i}")
            dqr = rope_slabs(dqr, cos_t, sin_t, F32, f"bwd_mla_qrope{i}", transposed=True)
            dcq = mm([(dqn, wt_uq_n), (dqr, wt_uq_r)], trans_b=False, out_dtype=F32, name=f"bwd_mla_dcq{i}")
            dckv = mm([(dkn, wt_uk), (dvm, wt_uv)], trans_b=False, out_dtype=F32, name=f"bwd_mla_dckv{i}")
            d_uq_n = wgrad(dqn, sv["c_q"], name=f"bwd_w_uq_n{i}").reshape(MLA_HEADS, MLA_NOPE, MLA_QR)
            d_uq_r = wgrad(dqr, sv["c_q"], name=f"bwd_w_uq_r{i}").reshape(MLA_HEADS, MLA_ROPE, MLA_QR)
            grads_full["mla_w_uq"] = jnp.concatenate([d_uq_n, d_uq_r], axis=1)
            grads_full["mla_w_uk"] = wgrad(dkn, sv["c_kv"], name=f"bwd_w_uk{i}")
            grads_full["mla_w_uv"] = wgrad(dvm, sv["c_kv"], name=f"bwd_w_uv{i}")
            dh_in, dg_q, dg_kv = mla_latents_backward(sv["h_in"], dcq, dckv, dkr, g_q, g_kv, cos_t, sin_t,
                                                      f"bwd_mla_latents{i}")
            du = mm([(dh_in, w_in)], trans_b=True, out_dtype=F32, name=f"bwd_mla_du{i}")
            grads_full["mla_w_in"] = wgrad(sv["u"], dh_in, name=f"bwd_w_mla_in{i}")[:, :mla_in]
        else:
            do = mm([(dy, w_fo)], trans_b=True, out_dtype=BF16, out_slab=True, name=f"bwd_fox_do{i}")
            grads_full["fox_w_o"] = wgrad(o, dy, name=f"bwd_w_fox_o{i}")
            dq, dk, dvf, dfk = attention_backward("fox", ops, sv["o_delta"], do, lse, bl, fox_scale, f"bwd_fox_attn{i}")
            df = dfk[:, :, :2, :].reshape(bl, FOX_HEADS // 2, nk, 2, tq).transpose(0, 2, 4, 1, 3).reshape(t, FOX_HEADS)
            df = jnp.pad(df, ((0, 0), (0, LANES - FOX_HEADS)))
            dz, db_f = fox_gate_backward(sv["z"], b_f, df, bl, f"bwd_fox_gate{i}")
            du = mm([(dq, wt_fox[0:d]), (dk, wt_fox[d:2 * d]), (dvf, wt_fox[2 * d:3 * d]), (dz, wt_f)],
                    trans_b=False, out_dtype=F32, name=f"bwd_fox_du{i}")
            u_f = sv["u"]
            grads_full["fox_w_in"] = jnp.concatenate(
                [wgrad(dq, u_f, name=f"bwd_w_fox_q{i}"), wgrad(dk, u_f, name=f"bwd_w_fox_k{i}"),
                 wgrad(dvf, u_f, name=f"bwd_w_fox_v{i}"), wgrad(dz, u_f, name=f"bwd_w_fox_f{i}")[:FOX_HEADS]], axis=0)
    bw = sublayer_backward(d_a, bl, "bwd_input", du=du, scale=(mods[0], 1), x_in=x2d)
    dmod[0][0], dmod[0][1] = bw["dshift"], bw["dscale"]
    grad_x = bw["dx"].reshape(bl, s, d)

    dmod_rows = jnp.concatenate([r.reshape(bl, d) for layer in dmod for r in layer], axis=0)
    dmod_rows = dmod_rows.reshape(DEPTH, 6, bl, d).transpose(0, 2, 1, 3)
    n_mod = dmod_rows.size // LANES
    ln_parts = [dg_ln[i][k] for i in range(DEPTH) for k in range(2)] + [db_ln[i][k] for i in range(DEPTH) for k in range(2)]
    small_g = jnp.concatenate([dmod_rows.reshape(-1, LANES), dg_q.reshape(-1, LANES), dg_kv.reshape(-1, LANES), db_f]
                              + [p.reshape(-1, LANES) for p in ln_parts] + [loss_cols.reshape(-1, LANES)], axis=0)
    n_small = small_g.shape[0]
    small_g = jnp.pad(small_g, ((0, (-n_small) % 8), (0, 0)))
    small_g_all = all_gather(small_g, "gather_small_grads")
    small_sum = sum_leading(small_g_all, "sum_small_grads")
    per_seq = DEPTH * 6 * d // LANES
    dmod_all = small_g_all[:, :n_mod].reshape(N_DEV, DEPTH, bl, 6 * d).transpose(1, 0, 2, 3)
    dmod_all = dmod_all.reshape(DEPTH, N_DEV * bl, 6 * d)
    o1 = n_mod
    grad_g_q = small_sum[o1:o1 + MLA_QR // LANES].reshape(1, MLA_QR)
    o1 += MLA_QR // LANES
    grad_g_kv = small_sum[o1:o1 + MLA_KVR // LANES].reshape(1, MLA_KVR)
    o1 += MLA_KVR // LANES
    grad_b_f = small_sum[o1:o1 + 1, :FOX_HEADS]
    o1 += 1
    n_ln_rows = DEPTH * 2 * d // LANES
    grad_ln_g_full = small_sum[o1:o1 + n_ln_rows].reshape(DEPTH, 2, d)
    grad_ln_b_full = small_sum[o1 + n_ln_rows:o1 + 2 * n_ln_rows].reshape(DEPTH, 2, d)
    loss = jnp.sum(small_sum[o1 + 2 * n_ln_rows:o1 + 2 * n_ln_rows + d // LANES])
    shard = d // N_DEV
    grad_ln_g = lax.dynamic_slice_in_dim(grad_ln_g_full, dev * shard, shard, axis=2)
    grad_ln_b = lax.dynamic_slice_in_dim(grad_ln_b_full, dev * shard, shard, axis=2)
    by_seq = small_g_all[:, :n_mod].reshape(N_DEV, DEPTH, bl, 6 * d // LANES, LANES).transpose(0, 2, 1, 3, 4)
    grad_ada_b = sum_leading(by_seq.reshape(N_DEV * bl, per_seq, LANES), "sum_ada_b").reshape(DEPTH, 6 * d)
    dmod_cols = lax.dynamic_slice_in_dim(dmod_all, dev * ada_cols, ada_cols, axis=2)
    grad_ada_w = jnp.stack([mm_tn(c_act, dmod_cols[i], name=f"bwd_w_ada{i}") for i in range(DEPTH)])

    scatter_start(0)
    after = bw["dx"]
    g_mine = [None] * len(groups)
    for gi in reversed(range(len(groups))):
        landed = exchange_wait(scatter_started[gi], after, f"scatter_group{gi}_wait", True)
        g_mine[gi] = sum_leading(landed, f"scatter_group{gi}_sum")
        after = g_mine[gi]

    def mine(nm, shape):
        return g_mine[group_of[nm]][offsets[nm]:offsets[nm] + rows_of[nm]].reshape(shape)

    def shard_t(nm, a):
        return t_last(mine(nm, t_last(a).shape))

    grads = {
        "mla_w_in": mine("mla_w_in", mla_w_in[0].shape)[None],
        "mla_g_q": grad_g_q,
        "mla_w_uq": shard_t("mla_w_uq", mla_w_uq[0])[None],
        "mla_g_kv": grad_g_kv,
        "mla_w_uk": shard_t("mla_w_uk", mla_w_uk[0])[None],
        "mla_w_uv": shard_t("mla_w_uv", mla_w_uv[0])[None],
        "mla_w_o": mine("mla_w_o", mla_w_o[0].shape)[None],
        "fox_w_in": shard_t("fox_w_in", fox_w_in[0])[None],
        "fox_b_f": grad_b_f,
        "fox_w_o": mine("fox_w_o", fox_w_o[0].shape)[None],
        "ada_w": grad_ada_w,
        "ada_b": grad_ada_b,
        "ffn_w_gate": jnp.stack([shard_t(f"gate{i}", ffn_w_gate[i]) for i in range(DEPTH)]),
        "ffn_w_up": jnp.stack([shard_t(f"up{i}", ffn_w_up[i]) for i in range(DEPTH)]),
        "ffn_w_down": jnp.stack([mine(f"down{i}", ffn_w_down[i].shape) for i in range(DEPTH)]),
        "ln_g": grad_ln_g,
        "ln_b": grad_ln_b,
    }
    weights = dict(mla_w_in=mla_w_in, mla_g_q=mla_g_q, mla_w_uq=mla_w_uq, mla_g_kv=mla_g_kv, mla_w_uk=mla_w_uk,
                   mla_w_uv=mla_w_uv, mla_w_o=mla_w_o, fox_w_in=fox_w_in, fox_b_f=fox_b_f, fox_w_o=fox_w_o,
                   ada_w=ada_w, ada_b=ada_b, ffn_w_gate=ffn_w_gate, ffn_w_up=ffn_w_up, ffn_w_down=ffn_w_down,
                   ln_g=ln_g, ln_b=ln_b)
    first = dict(mla_w_in=m_mla_w_in, mla_g_q=m_mla_g_q, mla_w_uq=m_mla_w_uq, mla_g_kv=m_mla_g_kv, mla_w_uk=m_mla_w_uk,
                 mla_w_uv=m_mla_w_uv, mla_w_o=m_mla_w_o, fox_w_in=m_fox_w_in, fox_b_f=m_fox_b_f, fox_w_o=m_fox_w_o,
                 ada_w=m_ada_w, ada_b=m_ada_b, ffn_w_gate=m_ffn_w_gate, ffn_w_up=m_ffn_w_up, ffn_w_down=m_ffn_w_down,
                 ln_g=m_ln_g, ln_b=m_ln_b)
    second = dict(mla_w_in=v_mla_w_in, mla_g_q=v_mla_g_q, mla_w_uq=v_mla_w_uq, mla_g_kv=v_mla_g_kv, mla_w_uk=v_mla_w_uk,
                  mla_w_uv=v_mla_w_uv, mla_w_o=v_mla_w_o, fox_w_in=v_fox_w_in, fox_b_f=v_fox_b_f, fox_w_o=v_fox_w_o,
                  ada_w=v_ada_w, ada_b=v_ada_b, ffn_w_gate=v_ffn_w_gate, ffn_w_up=v_ffn_w_up, ffn_w_down=v_ffn_w_down,
                  ln_g=v_ln_g, ln_b=v_ln_b)
    order = list(weights)
    g_out, d_out, m_out, v_out = [], [], [], []
    for nm in order:
        g = grads[nm].reshape(weights[nm].shape)
        delta, new_m, new_v = adamw(weights[nm], g, first[nm], second[nm], f"adamw_{nm}")
        g_out.append(g)
        d_out.append(delta)
        m_out.append(new_m)
        v_out.append(new_v)
    return (loss, grad_x, *g_out, *d_out, *m_out, *v_out)
```

```python
import functools

import jax
import jax.numpy as jnp
from jax import lax
from jax.experimental import pallas as pl
from jax.experimental.pallas import tpu as pltpu

F32 = jnp.float32
BF16 = jnp.bfloat16
LANES = 128
N_DEV = 8
VMEM_LIMIT_BYTES = 56 * 1024 * 1024

DEPTH = 2
MLA_HEADS = 8
MLA_NOPE = 128
MLA_ROPE = 64
MLA_V = 128
MLA_QR = 256
MLA_KVR = 256
ROPE_THETA = 10000.0
FOX_HEADS = 16
FOX_HD = 64
ALPHA = (2.0 * DEPTH) ** 0.25
NORM_EPS = 1e-5
ADAM_LR = 0.001
ADAM_B1 = 0.9
ADAM_B2 = 0.999
ADAM_EPS = 1e-08
ADAM_WD = 0.01
ADAM_STEP = 10

MESH_AXES = ("x", "y", "c")
MESH = pl.DeviceIdType.MESH


def _params(*sem):
    return pltpu.CompilerParams(dimension_semantics=sem, vmem_limit_bytes=VMEM_LIMIT_BYTES)


def _tile(n, cap, mult=LANES):
    if n <= cap:
        return n
    best = None
    for t in range(mult, cap + 1, mult):
        if n % t == 0:
            best = t
    assert best is not None, (n, cap, mult)
    return best


def _dot(a, b, dims):
    return lax.dot_general(a, b, (dims, ((), ())), preferred_element_type=F32)


def _nn(a, b):
    return _dot(a, b, ((1,), (0,)))


def _nt(a, b):
    return _dot(a, b, ((1,), (1,)))


def _tn(a, b):
    return _dot(a, b, ((0,), (0,)))


def _me():
    return lax.axis_index("x"), lax.axis_index("y"), lax.axis_index("c")


def all_gather(x_loc, name):
    r, c = x_loc.shape

    def body(x_ref, out_ref, send_sems, recv_sems, local_sem):
        x, y, cc = _me()
        me, sibling = (x, y, cc), (x, y, 1 - cc)
        chips = [(1 - x, y), (x, 1 - y), (1 - x, 1 - y)]

        def rows(px, py, pc):
            return out_ref.at[4 * px + 2 * py + pc]

        def copy(k, block, to, src=None):
            return pltpu.make_async_remote_copy(
                src_ref=rows(*block) if src is None else src, dst_ref=rows(*block),
                send_sem=send_sems.at[k], recv_sem=recv_sems.at[k], device_id=to, device_id_type=MESH)

        mine = pltpu.make_async_copy(x_ref, rows(*me), local_sem)
        mine.start()
        first = [copy(0, me, sibling, src=x_ref)]
        first += [copy(1 + j, me, (*chip, cc), src=x_ref) for j, chip in enumerate(chips)]
        for cp in first:
            cp.start()
        passed = [copy(4 + j, (*chip, cc), sibling) for j, chip in enumerate(chips)]
        for j, chip in enumerate(chips):
            copy(1 + j, (*chip, cc), me).wait_recv()
            passed[j].start()
        copy(0, sibling, me).wait_recv()
        for j, chip in enumerate(chips):
            copy(4 + j, (*chip, 1 - cc), me).wait_recv()
        for cp in first + passed:
            cp.wait_send()
        mine.wait()

    return pl.pallas_call(
        body, name=name,
        out_shape=jax.ShapeDtypeStruct((N_DEV, r, c), x_loc.dtype),
        in_specs=[pl.BlockSpec(memory_space=pl.ANY)],
        out_specs=pl.BlockSpec(memory_space=pl.ANY),
        scratch_shapes=[pltpu.SemaphoreType.DMA((7,)), pltpu.SemaphoreType.DMA((7,)), pltpu.SemaphoreType.DMA(())],
    )(x_loc)


HBM_SPEC = pl.BlockSpec(memory_space=pltpu.HBM)
SEM_SPEC = pl.BlockSpec(memory_space=pltpu.SEMAPHORE)
N_PEERS = N_DEV - 1


def _peer(k):
    x, y, c = _me()
    return (1 - x if k & 4 else x, 1 - y if k & 2 else y, 1 - c if k & 1 else c)


def _exchange_copies(src_ref, land_ref, send_sems, recv_sems, scatter):
    x, y, c = _me()
    mine = 4 * x + 2 * y + c
    copies = []
    for k in range(1, N_DEV):
        px, py, pc = _peer(k)
        src = src_ref.at[4 * px + 2 * py + pc] if scatter else src_ref
        copies.append(pltpu.make_async_remote_copy(
            src_ref=src, dst_ref=land_ref.at[mine], send_sem=send_sems.at[k - 1], recv_sem=recv_sems.at[k - 1],
            device_id=(px, py, pc), device_id_type=MESH))
    return copies


def exchange_start(src, land, name, scatter):
    def body(src_ref, land_ref, send_sems, recv_sems, src_thru, land_thru, token):
        for cp in _exchange_copies(src_ref, land_ref, send_sems, recv_sems, scatter):
            cp.start()
        token[...] = jnp.zeros_like(token)

    return pl.pallas_call(
        body, name=name,
        out_shape=(pltpu.SemaphoreType.DMA((N_PEERS,)), pltpu.SemaphoreType.DMA((N_PEERS,)),
                   pltpu.HBM(src.shape, src.dtype), pltpu.HBM(land.shape, land.dtype),
                   jax.ShapeDtypeStruct((8, LANES), F32)),
        in_specs=(HBM_SPEC, HBM_SPEC),
        out_specs=(SEM_SPEC, SEM_SPEC, HBM_SPEC, HBM_SPEC, pl.BlockSpec(memory_space=pltpu.VMEM)),
        input_output_aliases={0: 2, 1: 3},
        compiler_params=pltpu.CompilerParams(has_side_effects=pltpu.SideEffectType.DATAFLOW_SIDE_EFFECTING),
    )(pltpu.with_memory_space_constraint(src, pltpu.HBM), pltpu.with_memory_space_constraint(land, pltpu.HBM))


def exchange_wait(started, after, name, scatter):
    send_sems, recv_sems, src_thru, land_thru, _ = started

    def body(src_ref, land_ref, send_sems, recv_sems, after_ref, src_dead, got_ref):
        for cp in _exchange_copies(src_ref, land_ref, send_sems, recv_sems, scatter):
            cp.wait_send()
            cp.wait_recv()

    return pl.pallas_call(
        body, name=name,
        out_shape=(pltpu.HBM(src_thru.shape, src_thru.dtype), pltpu.HBM(land_thru.shape, land_thru.dtype)),
        in_specs=(HBM_SPEC, HBM_SPEC, SEM_SPEC, SEM_SPEC, pl.BlockSpec(memory_space=pl.ANY)),
        out_specs=(HBM_SPEC, HBM_SPEC), input_output_aliases={0: 0, 1: 1},
        compiler_params=pltpu.CompilerParams(has_side_effects=pltpu.SideEffectType.DATAFLOW_SIDE_EFFECTING),
    )(src_thru, land_thru, send_sems, recv_sems, after)[1]


def after_token(small, started):
    return small + started[4][0, 0]


def sum_leading(x, name):
    n, r, c = x.shape
    tr = _tile(r, 512, 16)

    def body(x_ref, o_ref):
        acc = x_ref[0].astype(F32)
        for k in range(1, n):
            acc = acc + x_ref[k].astype(F32)
        o_ref[...] = acc

    return pl.pallas_call(
        body, name=name,
        out_shape=jax.ShapeDtypeStruct((r, c), F32),
        grid=(r // tr,),
        in_specs=[pl.BlockSpec((n, tr, c), lambda i: (0, i, 0))],
        out_specs=pl.BlockSpec((tr, c), lambda i: (i, 0)),
        compiler_params=_params("arbitrary"),
    )(x)


MM_VMEM_BUDGET = 36 * 1024 * 1024
GRID_STEP_AS_BYTES = 1 << 20


def _mm_tiles(m, n, a_row_bytes, b_col_bytes, out_bytes):
    tms = [c for c in (2048, 1024, 512, 256, 128, 64, 32, 16, 8) if m % c == 0] or [m]
    tns = [c for c in range(LANES, min(n, 2048) + 1, LANES) if n % c == 0] or [n]
    best = None
    for tm in tms:
        for tn in tns:
            vmem = 2 * (tm * a_row_bytes + tn * b_col_bytes) + 2 * tm * tn * out_bytes + tm * tn * 4
            if vmem > MM_VMEM_BUDGET:
                continue
            steps = (m // tm) * (n // tn)
            cost = steps * GRID_STEP_AS_BYTES + (m // tm) * n * b_col_bytes + m * a_row_bytes
            if best is None or cost < best[0]:
                best = (cost, tm, tn)
    assert best is not None, (m, n, a_row_bytes, b_col_bytes)
    return best[1], best[2]


def mm(pairs, *, trans_b, out_dtype, name, out_slab=False, bias=None):
    a0 = pairs[0][0]
    m = a0.shape[1] if a0.ndim == 3 else a0.shape[0]
    n = pairs[0][1].shape[0] if trans_b else pairs[0][1].shape[1]
    a_row_bytes = sum((b.shape[1] if trans_b else b.shape[0]) * a.dtype.itemsize for a, b in pairs)
    b_col_bytes = sum((b.shape[1] if trans_b else b.shape[0]) * b.dtype.itemsize for _, b in pairs)
    tm, tn = _mm_tiles(m, n, a_row_bytes, b_col_bytes, jnp.dtype(out_dtype).itemsize)
    slabs = [a.ndim == 3 for a, _ in pairs]
    n_pairs = len(pairs)

    def body(*refs):
        o_ref = refs[-1]
        acc = bias_ref = None
        if bias is not None:
            bias_ref = refs[2 * n_pairs]
        for i in range(n_pairs):
            a_ref, b_ref = refs[2 * i], refs[2 * i + 1]
            if slabs[i]:
                a = jnp.concatenate([a_ref[s].astype(BF16) for s in range(a_ref.shape[0])], axis=1)
            else:
                a = a_ref[...].astype(BF16)
            b = b_ref[...].astype(BF16)
            part = _nt(a, b) if trans_b else _nn(a, b)
            acc = part if acc is None else acc + part
        if bias_ref is not None:
            acc = acc + bias_ref[...]
        if out_slab:
            for s in range(tn // LANES):
                o_ref[s] = acc[:, s * LANES:(s + 1) * LANES].astype(out_dtype)
        else:
            o_ref[...] = acc.astype(out_dtype)

    in_specs, args = [], []
    for (a, b), slab in zip(pairs, slabs):
        if slab:
            in_specs.append(pl.BlockSpec((a.shape[0], tm, LANES), lambda i, j: (0, i, 0)))
        else:
            in_specs.append(pl.BlockSpec((tm, a.shape[1]), lambda i, j: (i, 0)))
        if trans_b:
            in_specs.append(pl.BlockSpec((tn, b.shape[1]), lambda i, j: (j, 0)))
        else:
            in_specs.append(pl.BlockSpec((b.shape[0], tn), lambda i, j: (0, j)))
        args += [a, b]
    if bias is not None:
        in_specs.append(pl.BlockSpec((1, tn), lambda i, j: (0, j)))
        args.append(bias)
    if out_slab:
        out_shape = jax.ShapeDtypeStruct((n // LANES, m, LANES), out_dtype)
        out_spec = pl.BlockSpec((tn // LANES, tm, LANES), lambda i, j: (j, i, 0))
    else:
        out_shape = jax.ShapeDtypeStruct((m, n), out_dtype)
        out_spec = pl.BlockSpec((tm, tn), lambda i, j: (i, j))
    return pl.pallas_call(
        body, name=name, out_shape=out_shape, grid=(m // tm, n // tn),
        in_specs=in_specs, out_specs=out_spec,
        compiler_params=_params("arbitrary", "arbitrary"),
    )(*args)


def mm_tn(a, b, *, name, out_dtype=F32, tk_cap=1536, tn_cap=1024, tm_cap=512):
    slab = a.ndim == 3
    m = a.shape[1] if slab else a.shape[0]
    k = a.shape[0] * LANES if slab else a.shape[1]
    n = b.shape[1]
    tk = _tile(k, tk_cap)
    tn = _tile(n, tn_cap)
    tm = _tile(m, tm_cap, 8)
    n_steps = m // tm

    def body(a_ref, b_ref, o_ref, acc_ref):
        step = pl.program_id(2)

        @pl.when(step == 0)
        def _():
            acc_ref[...] = jnp.zeros_like(acc_ref)

        bb = b_ref[...].astype(BF16)
        if slab:
            for s in range(tk // LANES):
                acc_ref[s * LANES:(s + 1) * LANES, :] += _tn(a_ref[s].astype(BF16), bb)
        else:
            acc_ref[...] += _tn(a_ref[...].astype(BF16), bb)

        @pl.when(step == n_steps - 1)
        def _():
            o_ref[...] = acc_ref[...].astype(out_dtype)

    if slab:
        a_spec = pl.BlockSpec((tk // LANES, tm, LANES), lambda i, j, t: (i, t, 0))
    else:
        a_spec = pl.BlockSpec((tm, tk), lambda i, j, t: (t, i))
    return pl.pallas_call(
        body, name=name, out_shape=jax.ShapeDtypeStruct((k, n), out_dtype), grid=(k // tk, n // tn, n_steps),
        in_specs=[a_spec, pl.BlockSpec((tm, tn), lambda i, j, t: (t, j))],
        out_specs=pl.BlockSpec((tk, tn), lambda i, j, t: (i, j)),
        scratch_shapes=[pltpu.VMEM((tk, tn), F32)],
        compiler_params=_params("arbitrary", "arbitrary", "arbitrary"),
    )(a, b)


def _row_spec(d, k):
    return pl.BlockSpec((1, 1, d), lambda b, i: (6 * b + k, 0, 0))


def modulate(x, mod, k_shift, k_scale, bl, name):
    t, d = x.shape
    s = t // bl
    tm = _tile(s, 512, 8)
    nt = s // tm

    def body(x_ref, sh_ref, sc_ref, o_ref):
        o_ref[...] = (x_ref[...] * (1.0 + sc_ref[0]) + sh_ref[0]).astype(BF16)

    return pl.pallas_call(
        body, name=name, out_shape=jax.ShapeDtypeStruct((t, d), BF16), grid=(bl, nt),
        in_specs=[pl.BlockSpec((tm, d), lambda b, i: (b * nt + i, 0)), _row_spec(d, k_shift), _row_spec(d, k_scale)],
        out_specs=pl.BlockSpec((tm, d), lambda b, i: (b * nt + i, 0)),
        compiler_params=_params("arbitrary", "arbitrary"),
    )(x, mod, mod)


def _layer_norm_stats(r):
    mu = jnp.mean(r, axis=-1, keepdims=True)
    rc = r - mu
    var = jnp.mean(rc * rc, axis=-1, keepdims=True)
    rstd = lax.rsqrt(var + NORM_EPS)
    return rc * rstd, rstd


def residual_layer_norm(x, y, mod, k_gate, g, b, bl, name, next_mod=None):
    t, d = x.shape
    s = t // bl
    tm = _tile(s, 256, 8)
    nt = s // tm
    has_next = next_mod is not None

    def body(*refs):
        x_ref, y_ref, gt_ref, g_ref, b_ref = refs[:5]
        rest = refs[5:]
        if has_next:
            sh_ref, sc_ref, o_ref, r_ref, u_ref = rest
        else:
            o_ref, r_ref = rest
        r = ALPHA * x_ref[...] + (1.0 + gt_ref[0]) * y_ref[...]
        xhat, _ = _layer_norm_stats(r)
        out = xhat * g_ref[...] + b_ref[...]
        o_ref[...] = out
        r_ref[...] = r
        if has_next:
            u_ref[...] = (out * (1.0 + sc_ref[0]) + sh_ref[0]).astype(BF16)

    tok = pl.BlockSpec((tm, d), lambda bb, i: (bb * nt + i, 0))
    vec = pl.BlockSpec((1, d), lambda bb, i: (0, 0))
    in_specs = [tok, tok, _row_spec(d, k_gate), vec, vec]
    args = [x, y, mod, g, b]
    out_shape = [jax.ShapeDtypeStruct((t, d), F32), jax.ShapeDtypeStruct((t, d), F32)]
    out_specs = [tok, tok]
    if has_next:
        in_specs += [_row_spec(d, next_mod[0]), _row_spec(d, next_mod[1])]
        args += [mod if len(next_mod) == 2 else next_mod[2]] * 2
        out_shape.append(jax.ShapeDtypeStruct((t, d), BF16))
        out_specs.append(tok)
    return pl.pallas_call(
        body, name=name, out_shape=out_shape, grid=(bl, nt), in_specs=in_specs, out_specs=out_specs,
        compiler_params=_params("arbitrary", "arbitrary"),
    )(*args)


def loss_head(xo, target, name):
    t, d = xo.shape
    tm = _tile(t, 512, 8)

    def body(x_ref, t_ref, l_ref, dx_ref):
        @pl.when(pl.program_id(0) == 0)
        def _():
            l_ref[...] = jnp.zeros_like(l_ref)

        e = x_ref[...] - t_ref[...]
        l_ref[...] += jnp.sum(e * e, axis=0, keepdims=True) * (0.5 / d)
        dx_ref[...] = e * (1.0 / d)

    tok = pl.BlockSpec((tm, d), lambda i: (i, 0))
    return pl.pallas_call(
        body, name=name,
        out_shape=[jax.ShapeDtypeStruct((1, d), F32), jax.ShapeDtypeStruct((t, d), F32)],
        grid=(t // tm,), in_specs=[tok, tok],
        out_specs=[pl.BlockSpec((1, d), lambda i: (0, 0)), tok],
        compiler_params=_params("arbitrary"),
    )(xo, target)


def sublayer_backward(d_a, bl, name, *, du=None, scale=None, x_in=None, ln=None):
    t, d = d_a.shape
    s = t // bl
    tm = _tile(s, 256, 8)
    nt = s // tm
    has_mod = du is not None
    has_ln = ln is not None
    assert has_mod or has_ln
    assert has_ln or x_in is not None

    def body(*refs):
        refs = list(refs)
        da_ref = refs.pop(0)
        if has_mod:
            du_ref, sc_ref = refs.pop(0), refs.pop(0)
        if has_ln:
            r_ref, y_ref, g_ref, b_ref, gt_ref = (refs.pop(0) for _ in range(5))
        elif has_mod:
            xin_ref = refs.pop(0)
        dx_ref = refs.pop(0)
        if has_ln:
            dy_ref, dg_ref, db_ref, dgt_ref = (refs.pop(0) for _ in range(4))
        if has_mod:
            dsc_ref, dsh_ref = refs.pop(0), refs.pop(0)
        first_tile = pl.program_id(1) == 0
        first_step = jnp.logical_and(pl.program_id(0) == 0, first_tile)

        dout = da_ref[...]
        if has_ln:
            xhat, rstd = _layer_norm_stats(r_ref[...])
        if has_mod:
            duv = du_ref[...]
            dout = dout + duv * (1.0 + sc_ref[0])
            xin = xhat * g_ref[...] + b_ref[...] if has_ln else xin_ref[...]

            @pl.when(first_tile)
            def _():
                dsc_ref[...] = jnp.zeros_like(dsc_ref)
                dsh_ref[...] = jnp.zeros_like(dsh_ref)

            dsc_ref[0] += jnp.sum(duv * xin, axis=0, keepdims=True)
            dsh_ref[0] += jnp.sum(duv, axis=0, keepdims=True)
        if not has_ln:
            dx_ref[...] = dout
            return

        @pl.when(first_step)
        def _():
            dg_ref[...] = jnp.zeros_like(dg_ref)
            db_ref[...] = jnp.zeros_like(db_ref)

        @pl.when(first_tile)
        def _():
            dgt_ref[...] = jnp.zeros_like(dgt_ref)

        dg_ref[...] += jnp.sum(dout * xhat, axis=0, keepdims=True)
        db_ref[...] += jnp.sum(dout, axis=0, keepdims=True)
        dxh = dout * g_ref[...]
        dr = rstd * (dxh - jnp.mean(dxh, axis=-1, keepdims=True) - xhat * jnp.mean(dxh * xhat, axis=-1, keepdims=True))
        dx_ref[...] = ALPHA * dr
        dy_ref[...] = ((1.0 + gt_ref[0]) * dr).astype(BF16)
        dgt_ref[0] += jnp.sum(dr * y_ref[...], axis=0, keepdims=True)

    tok = pl.BlockSpec((tm, d), lambda bb, i: (bb * nt + i, 0))
    vec = pl.BlockSpec((1, d), lambda bb, i: (0, 0))
    seq = pl.BlockSpec((1, 1, d), lambda bb, i: (bb, 0, 0))
    in_specs, args = [tok], [d_a]
    if has_mod:
        in_specs += [tok, _row_spec(d, scale[1])]
        args += [du, scale[0]]
    if has_ln:
        r, y, g, b, gate = ln
        in_specs += [tok, tok, vec, vec, _row_spec(d, gate[1])]
        args += [r, y, g, b, gate[0]]
    elif has_mod:
        in_specs.append(tok)
        args.append(x_in)
    names = ["dx"]
    out_shape, out_specs = [jax.ShapeDtypeStruct((t, d), F32)], [tok]
    if has_ln:
        names += ["dy", "dg", "db", "dgate"]
        out_shape += [jax.ShapeDtypeStruct((t, d), BF16), jax.ShapeDtypeStruct((1, d), F32),
                      jax.ShapeDtypeStruct((1, d), F32), jax.ShapeDtypeStruct((bl, 1, d), F32)]
        out_specs += [tok, vec, vec, seq]
    if has_mod:
        names += ["dscale", "dshift"]
        out_shape += [jax.ShapeDtypeStruct((bl, 1, d), F32)] * 2
        out_specs += [seq, seq]
    outs = pl.pallas_call(
        body, name=name, out_shape=out_shape, grid=(bl, nt), in_specs=in_specs, out_specs=out_specs,
        compiler_params=_params("arbitrary", "arbitrary"),
    )(*args)
    return dict(zip(names, outs))


def _silu(a):
    return a * jax.nn.sigmoid(a)


def silu_rows(a, name):
    def body(a_ref, o_ref):
        o_ref[...] = _silu(a_ref[...]).astype(BF16)

    return pl.pallas_call(body, name=name, out_shape=jax.ShapeDtypeStruct(a.shape, BF16))(a)


def _swiglu_tiles(t, f):
    return _tile(t, 512, 8), _tile(f, 1536)


def swiglu_in(u, wt_gate, wt_up, name):
    t, d = u.shape
    f = wt_gate.shape[0]
    tm, tf = _swiglu_tiles(t, f)

    def body(u_ref, g_ref, w_ref, a_ref, b_ref, h_ref):
        uv = u_ref[...]
        a = _nt(uv, g_ref[...])
        b = _nt(uv, w_ref[...])
        a_ref[...] = a
        b_ref[...] = b
        h_ref[...] = (_silu(a) * b).astype(BF16)

    w_spec = pl.BlockSpec((tf, d), lambda i, j: (j, 0))
    o_spec = pl.BlockSpec((tm, tf), lambda i, j: (i, j))
    return pl.pallas_call(
        body, name=name,
        out_shape=[jax.ShapeDtypeStruct((t, f), F32), jax.ShapeDtypeStruct((t, f), F32), jax.ShapeDtypeStruct((t, f), BF16)],
        grid=(t // tm, f // tf), in_specs=[pl.BlockSpec((tm, d), lambda i, j: (i, 0)), w_spec, w_spec],
        out_specs=[o_spec, o_spec, o_spec], compiler_params=_params("arbitrary", "arbitrary"),
    )(u, wt_gate, wt_up)


def swiglu_out_backward(dy, w_down, a, b, name):
    t, d = dy.shape
    f = w_down.shape[0]
    tm, tf = _swiglu_tiles(t, f)

    def body(dy_ref, w_ref, a_ref, b_ref, da_ref, db_ref):
        dh = _nt(dy_ref[...], w_ref[...])
        av = a_ref[...]
        sig = jax.nn.sigmoid(av)
        da_ref[...] = (dh * b_ref[...] * (sig * (1.0 + av * (1.0 - sig)))).astype(BF16)
        db_ref[...] = (dh * (av * sig)).astype(BF16)

    spec = pl.BlockSpec((tm, tf), lambda i, j: (i, j))
    return pl.pallas_call(
        body, name=name, out_shape=[jax.ShapeDtypeStruct((t, f), BF16)] * 2, grid=(t // tm, f // tf),
        in_specs=[pl.BlockSpec((tm, d), lambda i, j: (i, 0)), pl.BlockSpec((tf, d), lambda i, j: (j, 0)), spec, spec],
        out_specs=[spec, spec], compiler_params=_params("arbitrary", "arbitrary"),
    )(dy, w_down, a, b)


def rope_tables(pos, inv_freq, sign, name):
    t = pos.shape[0]
    tm = _tile(t, 512, 8)

    def body(p_ref, f_ref, s_ref, c_out, s_out):
        ang = p_ref[...] * f_ref[...]
        c_out[...] = jnp.cos(ang)
        s_out[...] = jnp.sin(ang) * s_ref[...]

    vec = pl.BlockSpec((1, LANES), lambda i: (0, 0))
    tab = pl.BlockSpec((tm, LANES), lambda i: (i, 0))
    return pl.pallas_call(
        body, name=name, out_shape=[jax.ShapeDtypeStruct((t, LANES), F32)] * 2, grid=(t // tm,),
        in_specs=[pl.BlockSpec((tm, 1), lambda i: (i, 0)), vec, vec], out_specs=[tab, tab],
        compiler_params=_params("arbitrary"),
    )(pos, inv_freq, sign)


def _rot_half(v):
    lane = lax.broadcasted_iota(jnp.int32, v.shape, v.ndim - 1)
    up = pltpu.roll(v, LANES - MLA_ROPE // 2, v.ndim - 1)
    down = pltpu.roll(v, MLA_ROPE // 2, v.ndim - 1)
    return jnp.where(lane % MLA_ROPE < MLA_ROPE // 2, up, down)


def _rope(v, cos, sin_signed):
    return v * cos + _rot_half(v) * sin_signed


def _rope_transposed(dv, cos, sin_signed):
    return dv * cos + _rot_half(dv * sin_signed)


def rope_slabs(v, cos, sin_signed, out_dtype, name, transposed=False):
    ns, t, _ = v.shape
    tm = _tile(t, 512, 8)
    fn = _rope_transposed if transposed else _rope

    def body(v_ref, c_ref, s_ref, o_ref):
        o_ref[0] = fn(v_ref[0].astype(F32), c_ref[...], s_ref[...]).astype(out_dtype)

    tab = pl.BlockSpec((tm, LANES), lambda j, i: (i, 0))
    spec = pl.BlockSpec((1, tm, LANES), lambda j, i: (j, i, 0))
    return pl.pallas_call(
        body, name=name, out_shape=jax.ShapeDtypeStruct(v.shape, out_dtype), grid=(ns, t // tm),
        in_specs=[spec, tab, tab], out_specs=spec, compiler_params=_params("arbitrary", "arbitrary"),
    )(v, cos, sin_signed)


def _rms(x):
    rinv = lax.rsqrt(jnp.mean(x * x, axis=-1, keepdims=True) + NORM_EPS)
    return x * rinv, rinv


def mla_latents_forward(h_in, g_q, g_kv, cos, sin_signed, name):
    t = h_in.shape[0]
    tm = _tile(t, 512, 8)

    def body(h_ref, gq_ref, gkv_ref, c_ref, s_ref, cq_ref, ckv_ref, kr_ref):
        cq_ref[...] = (_rms(h_ref[:, 0:MLA_QR])[0] * gq_ref[...]).astype(BF16)
        ckv_ref[...] = (_rms(h_ref[:, MLA_QR:MLA_QR + MLA_KVR])[0] * gkv_ref[...]).astype(BF16)
        kr_ref[...] = _rope(h_ref[:, MLA_QR + MLA_KVR:], c_ref[...], s_ref[...]).astype(BF16)

    def tok(w):
        return pl.BlockSpec((tm, w), lambda i: (i, 0))

    def vec(w):
        return pl.BlockSpec((1, w), lambda i: (0, 0))

    return pl.pallas_call(
        body, name=name,
        out_shape=[jax.ShapeDtypeStruct((t, MLA_QR), BF16), jax.ShapeDtypeStruct((t, MLA_KVR), BF16),
                   jax.ShapeDtypeStruct((t, LANES), BF16)],
        grid=(t // tm,),
        in_specs=[tok(h_in.shape[1]), vec(MLA_QR), vec(MLA_KVR), tok(LANES), tok(LANES)],
        out_specs=[tok(MLA_QR), tok(MLA_KVR), tok(LANES)],
        compiler_params=_params("arbitrary"),
    )(h_in, g_q, g_kv, cos, sin_signed)


def mla_latents_backward(h_in, dcq, dckv, dkr, g_q, g_kv, cos, sin_signed, name):
    t, w = h_in.shape
    tm = _tile(t, 512, 8)

    def body(h_ref, dcq_ref, dckv_ref, dkr_ref, gq_ref, gkv_ref, c_ref, s_ref, dh_ref, dgq_ref, dgkv_ref):
        @pl.when(pl.program_id(0) == 0)
        def _():
            dgq_ref[...] = jnp.zeros_like(dgq_ref)
            dgkv_ref[...] = jnp.zeros_like(dgkv_ref)

        def rms_bwd(x, dc, g_ref, dg_ref):
            xn, rinv = _rms(x)
            dg_ref[...] += jnp.sum(dc * xn, axis=0, keepdims=True)
            dxn = dc * g_ref[...]
            return rinv * (dxn - xn * jnp.mean(dxn * xn, axis=-1, keepdims=True))

        dq = rms_bwd(h_ref[:, 0:MLA_QR], dcq_ref[...], gq_ref, dgq_ref)
        dkv = rms_bwd(h_ref[:, MLA_QR:MLA_QR + MLA_KVR], dckv_ref[...], gkv_ref, dgkv_ref)
        dr = _rope_transposed(dkr_ref[...], c_ref[...], s_ref[...])
        dh_ref[...] = jnp.concatenate([dq, dkv, dr], axis=1).astype(BF16)

    def tok(ww):
        return pl.BlockSpec((tm, ww), lambda i: (i, 0))

    def vec(ww):
        return pl.BlockSpec((1, ww), lambda i: (0, 0))

    return pl.pallas_call(
        body, name=name,
        out_shape=[jax.ShapeDtypeStruct((t, w), BF16), jax.ShapeDtypeStruct((1, MLA_QR), F32),
                   jax.ShapeDtypeStruct((1, MLA_KVR), F32)],
        grid=(t // tm,),
        in_specs=[tok(w), tok(MLA_QR), tok(MLA_KVR), tok(LANES), vec(MLA_QR), vec(MLA_KVR), tok(LANES), tok(LANES)],
        out_specs=[tok(w), vec(MLA_QR), vec(MLA_KVR)],
        compiler_params=_params("arbitrary"),
    )(h_in, dcq, dckv, dkr, g_q, g_kv, cos, sin_signed)


def _tri(n, lower):
    r = lax.broadcasted_iota(jnp.int32, (n, n), 0)
    c = lax.broadcasted_iota(jnp.int32, (n, n), 1)
    return jnp.where(r >= c if lower else r <= c, 1.0, 0.0).astype(F32)


def _dot_exact(tri, v):
    hi = v.astype(BF16)
    mid = (v - hi.astype(F32)).astype(BF16)
    lo = (v - hi.astype(F32) - mid.astype(F32)).astype(BF16)
    t = tri.astype(BF16)
    return _nn(t, hi) + _nn(t, mid) + _nn(t, lo)


def fox_gate_forward(z, b_f, bl, name):
    t = z.shape[0]
    s = t // bl
    ch = LANES
    n_ch = s // ch

    def body(z_ref, b_ref, f_ref, fs_ref):
        tri = _tri(ch, True)
        carry = jnp.zeros((1, LANES), F32)
        for k in range(n_ch):
            x = z_ref[k * ch:(k + 1) * ch, :] + b_ref[...]
            logf = jnp.minimum(x, 0.0) - jnp.log(1.0 + jnp.exp(-jnp.abs(x)))
            cs = _dot_exact(tri, logf) + carry
            carry = cs[ch - 1:ch, :]
            f_ref[k * ch:(k + 1) * ch, :] = cs
            for h in range(FOX_HEADS):
                fs_ref[h, k * ch:(k + 1) * ch, :] = jnp.broadcast_to(cs[:, h:h + 1], (ch, LANES))

    return pl.pallas_call(
        body, name=name,
        out_shape=[jax.ShapeDtypeStruct((t, LANES), F32), jax.ShapeDtypeStruct((FOX_HEADS, t, LANES), F32)],
        grid=(bl,),
        in_specs=[pl.BlockSpec((s, LANES), lambda b: (b, 0)), pl.BlockSpec((1, LANES), lambda b: (0, 0))],
        out_specs=[pl.BlockSpec((s, LANES), lambda b: (b, 0)),
                   pl.BlockSpec((FOX_HEADS, s, LANES), lambda b: (0, b, 0))],
        compiler_params=_params("arbitrary"),
    )(z, b_f)


def fox_gate_backward(z, b_f, df, bl, name):
    t = z.shape[0]
    s = t // bl
    ch = LANES
    n_ch = s // ch

    def body(z_ref, b_ref, df_ref, dz_ref, db_ref):
        @pl.when(pl.program_id(0) == 0)
        def _():
            db_ref[...] = jnp.zeros_like(db_ref)

        tri = _tri(ch, False)
        carry = jnp.zeros((1, LANES), F32)
        for k in reversed(range(n_ch)):
            cs = _dot_exact(tri, df_ref[k * ch:(k + 1) * ch, :]) + carry
            carry = cs[0:1, :]
            x = z_ref[k * ch:(k + 1) * ch, :] + b_ref[...]
            dz = cs * (1.0 - jax.nn.sigmoid(x))
            dz_ref[k * ch:(k + 1) * ch, :] = dz
            db_ref[...] += jnp.sum(dz, axis=0, keepdims=True)

    tok = pl.BlockSpec((s, LANES), lambda b: (b, 0))
    vec = pl.BlockSpec((1, LANES), lambda b: (0, 0))
    return pl.pallas_call(
        body, name=name,
        out_shape=[jax.ShapeDtypeStruct((t, LANES), F32), jax.ShapeDtypeStruct((1, LANES), F32)],
        grid=(bl,), in_specs=[tok, vec, tok], out_specs=[tok, vec],
        compiler_params=_params("arbitrary"),
    )(z, b_f, df)


NEG_INF = float("-inf")


def _attn_tiles(s):
    return _tile(s, 512, 8)


def attention_forward(kind, ops, bl, scale, name):
    fox = kind == "fox"
    if fox:
        qkv, fq, fk = ops
        t = qkv.shape[1]
        n_pair = FOX_HEADS // 2
    else:
        qn, qr, kn, kr, v = ops
        t = qn.shape[1]
        n_pair = MLA_HEADS // 2
    s = t // bl
    tq = _attn_tiles(s)
    nq = s // tq
    half = LANES // 2

    def body(*refs):
        if fox:
            q_ref, k_ref, v_ref, fq_ref, fk_ref, o_ref, lse_ref, o32_ref = refs
        else:
            qn_ref, qr_ref, kn_ref, kr_ref, v_ref, o_ref, lse_ref = refs
        i = pl.program_id(2)
        row = lax.broadcasted_iota(jnp.int32, (tq, tq), 0)
        col = lax.broadcasted_iota(jnp.int32, (tq, tq), 1)
        heads = []
        for e in range(2):
            sl = slice(e * half, (e + 1) * half)
            if fox:
                heads.append((sl, q_ref[0, :, sl], None))
            else:
                heads.append((sl, qn_ref[e], qr_ref[0, :, sl]))
        dv = half if fox else LANES

        def wide(stat):
            return jnp.concatenate([stat] * (tq // LANES), axis=1)

        def step(j, carry, masked):
            rows = pl.ds(pl.multiple_of(j * tq, tq), tq)
            new = []
            for e, (sl, qa, qb) in enumerate(heads):
                m, l, acc = carry[e]
                if fox:
                    sc = _nt(qa, k_ref[0, rows, sl]) * scale + wide(fq_ref[e]) - fk_ref[0, j, e:e + 1, :]
                    vv = v_ref[0, rows, sl]
                else:
                    sc = (_nt(qa, kn_ref[e, rows, :]) + _nt(qb, kr_ref[rows, 0:half])) * scale
                    vv = v_ref[e, rows, :]
                if masked:
                    sc = jnp.where(row >= col, sc, NEG_INF)
                m_new = jnp.maximum(m, jnp.max(sc, axis=1, keepdims=True))
                p = jnp.exp(sc - m_new)
                a = jnp.exp(m - m_new)
                l = a * l + jnp.sum(p, axis=1, keepdims=True)
                p_hi = p.astype(BF16)
                acc = a * acc + _nn(p_hi, vv)
                if fox:
                    acc = acc + _nn((p - p_hi.astype(F32)).astype(BF16), vv)
                new.append((m_new, l, acc))
            return tuple(new)

        init = (jnp.full((tq, 1), NEG_INF, F32), jnp.zeros((tq, 1), F32), jnp.zeros((tq, dv), F32))
        carry = step(i, (init, init), True)
        carry = lax.fori_loop(0, i, lambda j, c: step(j, c, False), carry)
        outs = [acc / l for _, l, acc in carry]
        for e, (m, l, _) in enumerate(carry):
            lse_ref[e] = jnp.broadcast_to(m + jnp.log(l), (tq, LANES))
        if fox:
            o32 = jnp.concatenate(outs, axis=1)
            o32_ref[0] = o32
            o_ref[0] = o32.astype(BF16)
        else:
            o_ref[0] = outs[0].astype(BF16)
            o_ref[1] = outs[1].astype(BF16)

    def q_idx(b, g, i):
        return (g, b * nq + i, 0)

    if fox:
        nk = fk.shape[1]
        in_specs = [pl.BlockSpec((1, tq, LANES), q_idx),
                    pl.BlockSpec((1, s, LANES), lambda b, g, i: (n_pair + g, b, 0)),
                    pl.BlockSpec((1, s, LANES), lambda b, g, i: (2 * n_pair + g, b, 0)),
                    pl.BlockSpec((2, tq, LANES), q_idx),
                    pl.BlockSpec((1, nk, 8, tq), lambda b, g, i: (b * n_pair + g, 0, 0, 0))]
        args = [qkv, qkv, qkv, fq, fk]
        o_spec = pl.BlockSpec((1, tq, LANES), q_idx)
    else:
        in_specs = [pl.BlockSpec((2, tq, LANES), q_idx),
                    pl.BlockSpec((1, tq, LANES), q_idx),
                    pl.BlockSpec((2, s, LANES), lambda b, g, i: (g, b, 0)),
                    pl.BlockSpec((s, LANES), lambda b, g, i: (b, 0)),
                    pl.BlockSpec((2, s, LANES), lambda b, g, i: (g, b, 0))]
        args = [qn, qr, kn, kr, v]
        o_spec = pl.BlockSpec((2, tq, LANES), q_idx)
    out_shape = [jax.ShapeDtypeStruct((8, t, LANES), BF16), jax.ShapeDtypeStruct((2 * n_pair, t, LANES), F32)]
    out_specs = [o_spec, pl.BlockSpec((2, tq, LANES), q_idx)]
    if fox:
        out_shape.append(jax.ShapeDtypeStruct((8, t, LANES), F32))
        out_specs.append(o_spec)
    outs = pl.pallas_call(
        body, name=name, out_shape=out_shape, grid=(bl, n_pair, nq), in_specs=in_specs, out_specs=out_specs,
        compiler_params=_params("arbitrary", "arbitrary", "arbitrary"),
    )(*args)
    return (outs[0], outs[1], outs[2] if fox else outs[0])


def attention_backward(kind, ops, o, do, lse, bl, scale, name):
    fox = kind == "fox"
    if fox:
        qkv, fq, fk = ops
        t = qkv.shape[1]
        n_pair = FOX_HEADS // 2
    else:
        qn, qr, kn, kr, v = ops
        t = qn.shape[1]
        n_pair = MLA_HEADS // 2
    s = t // bl
    tq = _attn_tiles(s)
    nq = s // tq
    half = LANES // 2

    def body(*refs):
        if fox:
            (q_ref, k_ref, v_ref, fq_ref, fk_ref, o_ref, do_ref, lse_ref,
             dq_ref, dk_ref, dv_ref, dfk_ref, delta_scr, qt_scr, dot_scr) = refs
        else:
            (qn_ref, qr_ref, kn_ref, kr_ref, v_ref, o_ref, do_ref, lse_ref,
             dqn_ref, dqr_ref, dkn_ref, dv_ref, dkr_ref, delta_scr, qt_scr, qrt_scr, dot_scr) = refs
        g, j = pl.program_id(1), pl.program_id(2)
        row = lax.broadcasted_iota(jnp.int32, (tq, tq), 0)
        col = lax.broadcasted_iota(jnp.int32, (tq, tq), 1)
        krows = pl.ds(pl.multiple_of(j * tq, tq), tq)

        def transposed(v):
            return v.astype(F32).T.astype(BF16)

        def wide(stat):
            return jnp.concatenate([stat] * (tq // LANES), axis=1)

        @pl.when(j == 0)
        def _():
            if fox:
                dq_ref[...] = jnp.zeros_like(dq_ref)
            else:
                dqn_ref[...] = jnp.zeros_like(dqn_ref)
                dqr_ref[...] = jnp.zeros_like(dqr_ref)
            for ii in range(nq):
                rws = slice(ii * tq, (ii + 1) * tq)
                deltas = []
                if fox:
                    prod = do_ref[0, rws, :].astype(F32) * o_ref[0, rws, :].astype(F32)
                    for e in range(2):
                        deltas.append(jnp.sum(prod[:, e * half:(e + 1) * half], axis=1, keepdims=True))
                    qt_scr[ii] = transposed(q_ref[0, rws, :])
                    dot_scr[ii] = transposed(do_ref[0, rws, :])
                else:
                    for e in range(2):
                        prod = do_ref[e, rws, :].astype(F32) * o_ref[e, rws, :].astype(F32)
                        deltas.append(jnp.sum(prod, axis=1, keepdims=True))
                        qt_scr[e, ii] = transposed(qn_ref[e, rws, :])
                        dot_scr[e, ii] = transposed(do_ref[e, rws, :])
                    qrt_scr[ii] = transposed(qr_ref[0, rws, :])
                for e in range(2):
                    delta_scr[e, rws, :] = jnp.broadcast_to(deltas[e], (tq, LANES))

        if fox:
            dfk_ref[...] = jnp.zeros_like(dfk_ref)
        else:
            @pl.when(jnp.logical_and(g == 0, j == 0))
            def _():
                dkr_ref[...] = jnp.zeros_like(dkr_ref)

        heads = []
        for e in range(2):
            sl = slice(e * half, (e + 1) * half)
            if fox:
                heads.append((sl, k_ref[0, :, sl], v_ref[0, :, sl], fk_ref[0, 0, e:e + 1, :]))
            else:
                heads.append((sl, kn_ref[e], v_ref[e], kr_ref[krows, 0:half]))
        dk_w = dv_w = half if fox else LANES

        def step(i, carry, masked):
            rows = pl.ds(pl.multiple_of(i * tq, tq), tq)
            new = []
            for e, (sl, k_e, v_e, x_e) in enumerate(heads):
                dk_acc, dv_acc, last = carry[e]
                if fox:
                    do_i = do_ref[0, rows, sl]
                    sc = _nt(q_ref[0, rows, sl], k_e) * scale + wide(fq_ref[e, rows, :]) - x_e
                else:
                    do_i = do_ref[e, rows, :]
                    sc = (_nt(qn_ref[e, rows, :], k_e) + _nt(qr_ref[0, rows, sl], x_e)) * scale
                if masked:
                    sc = jnp.where(row >= col, sc, NEG_INF)
                p = jnp.exp(sc - wide(lse_ref[e, rows, :]))
                dp = _nt(do_i, v_e)
                ds = p * (dp - wide(delta_scr[e, rows, :]))
                dsb = (ds * scale).astype(BF16)
                if fox:
                    fsl = slice(e * half, (e + 1) * half)
                    dv_acc = dv_acc + _nn(dot_scr[i, fsl, :], p.astype(BF16))
                    dk_acc = dk_acc + _nn(qt_scr[i, fsl, :], dsb)
                    dq_ref[0, rows, sl] += _nn(dsb, k_e)
                    last = last - jnp.sum(ds, axis=0, keepdims=True)
                else:
                    dv_acc = dv_acc + _nn(dot_scr[e, i], p.astype(BF16))
                    dk_acc = dk_acc + _nn(qt_scr[e, i], dsb)
                    dqn_ref[e, rows, :] += _nn(dsb, k_e)
                    dqr_ref[0, rows, sl] += _nn(dsb, x_e)
                    last = last + _nn(qrt_scr[i, e * half:(e + 1) * half, :], dsb)
                new.append((dk_acc, dv_acc, last))
            return tuple(new)

        last0 = jnp.zeros((1, tq), F32) if fox else jnp.zeros((half, tq), F32)
        init = (jnp.zeros((dk_w, tq), F32), jnp.zeros((dv_w, tq), F32), last0)
        carry = step(j, (init, init), True)
        carry = lax.fori_loop(j + 1, nq, lambda i, c: step(i, c, False), carry)
        if fox:
            for e in range(2):
                dfk_ref[0, 0, e:e + 1, :] = carry[e][2]
            dk_ref[0] = jnp.concatenate([carry[0][0], carry[1][0]], axis=0).T.astype(BF16)
            dv_ref[0] = jnp.concatenate([carry[0][1], carry[1][1]], axis=0).T.astype(BF16)
        else:
            for e in range(2):
                dkn_ref[e] = carry[e][0].T.astype(BF16)
                dv_ref[e] = carry[e][1].T.astype(BF16)
            dkr_t = carry[0][2] + carry[1][2]
            dkr_ref[krows, :] += jnp.concatenate([dkr_t, jnp.zeros_like(dkr_t)], axis=0).T

    def whole(b, g, j):
        return (g, b, 0)

    def kblk(b, g, j):
        return (g, b * nq + j, 0)

    if fox:
        in_specs = [pl.BlockSpec((1, s, LANES), whole),
                    pl.BlockSpec((1, tq, LANES), lambda b, g, j: (n_pair + g, b * nq + j, 0)),
                    pl.BlockSpec((1, tq, LANES), lambda b, g, j: (2 * n_pair + g, b * nq + j, 0)),
                    pl.BlockSpec((2, s, LANES), whole),
                    pl.BlockSpec((1, 1, 8, tq), lambda b, g, j: (b * n_pair + g, j, 0, 0)),
                    pl.BlockSpec((1, s, LANES), whole), pl.BlockSpec((1, s, LANES), whole),
                    pl.BlockSpec((2, s, LANES), whole)]
        args = [qkv, qkv, qkv, fq, fk, o, do, lse]
        out_shape = [jax.ShapeDtypeStruct((8, t, LANES), F32), jax.ShapeDtypeStruct((8, t, LANES), BF16),
                     jax.ShapeDtypeStruct((8, t, LANES), BF16), jax.ShapeDtypeStruct(fk.shape, F32)]
        out_specs = [pl.BlockSpec((1, s, LANES), whole), pl.BlockSpec((1, tq, LANES), kblk),
                     pl.BlockSpec((1, tq, LANES), kblk),
                     pl.BlockSpec((1, 1, 8, tq), lambda b, g, j: (b * n_pair + g, j, 0, 0))]
    else:
        pair = pl.BlockSpec((2, s, LANES), whole)
        pair_k = pl.BlockSpec((2, tq, LANES), kblk)
        in_specs = [pair, pl.BlockSpec((1, s, LANES), whole), pair_k,
                    pl.BlockSpec((s, LANES), lambda b, g, j: (b, 0)), pair_k,
                    pair, pair, pair]
        args = [qn, qr, kn, kr, v, o, do, lse]
        out_shape = [jax.ShapeDtypeStruct((8, t, LANES), F32), jax.ShapeDtypeStruct((4, t, LANES), F32),
                     jax.ShapeDtypeStruct((8, t, LANES), BF16), jax.ShapeDtypeStruct((8, t, LANES), BF16),
                     jax.ShapeDtypeStruct((t, LANES), F32)]
        out_specs = [pair, pl.BlockSpec((1, s, LANES), whole), pair_k, pair_k,
                     pl.BlockSpec((s, LANES), lambda b, g, j: (b, 0))]
    t_blocks = pltpu.VMEM((nq, LANES, tq), BF16)
    t_pairs = pltpu.VMEM((2, nq, LANES, tq), BF16)
    scratch = [pltpu.VMEM((2, s, LANES), F32)] + ([t_blocks, t_blocks] if fox else [t_pairs, t_blocks, t_pairs])
    return pl.pallas_call(
        body, name=name, out_shape=out_shape, grid=(bl, n_pair, nq), in_specs=in_specs, out_specs=out_specs,
        scratch_shapes=scratch, compiler_params=_params("arbitrary", "arbitrary", "arbitrary"),
    )(*args)


def adamw(w, g, m, v, name):
    shape = w.shape
    c = shape[-1]
    r = w.size // c
    tr = _tile(r, 512, 8)

    def body(w_ref, g_ref, m_ref, v_ref, d_ref, nm_ref, nv_ref):
        gv = g_ref[...]
        m2 = ADAM_B1 * m_ref[...] + (1.0 - ADAM_B1) * gv
        v2 = ADAM_B2 * v_ref[...] + (1.0 - ADAM_B2) * (gv * gv)
        m_hat = m2 / (1.0 - ADAM_B1 ** ADAM_STEP)
        v_hat = v2 / (1.0 - ADAM_B2 ** ADAM_STEP)
        d_ref[...] = -ADAM_LR * (m_hat / (jnp.sqrt(v_hat) + ADAM_EPS) + ADAM_WD * w_ref[...])
        nm_ref[...] = m2
        nv_ref[...] = v2

    spec = pl.BlockSpec((tr, c), lambda i: (i, 0))
    outs = pl.pallas_call(
        body, name=name, out_shape=[jax.ShapeDtypeStruct((r, c), F32)] * 3, grid=(r // tr,),
        in_specs=[spec] * 4, out_specs=[spec] * 3, compiler_params=_params("arbitrary"),
    )(*(a.reshape(r, c) for a in (w, g, m, v)))
    return tuple(a.reshape(shape) for a in outs)


PACK_COLS = 1024


def _pack_rows(a):
    return a.reshape(-1, PACK_COLS)


def kernel(x, c, positions, mla_w_in, mla_g_q, mla_w_uq, mla_g_kv, mla_w_uk, mla_w_uv, mla_w_o, fox_w_in, fox_b_f, fox_w_o, ada_w, ada_b, ffn_w_gate, ffn_w_up, ffn_w_down, ln_g, ln_b, loss_target, m_mla_w_in, m_mla_g_q, m_mla_w_uq, m_mla_g_kv, m_mla_w_uk, m_mla_w_uv, m_mla_w_o, m_fox_w_in, m_fox_b_f, m_fox_w_o, m_ada_w, m_ada_b, m_ffn_w_gate, m_ffn_w_up, m_ffn_w_down, m_ln_g, m_ln_b, v_mla_w_in, v_mla_g_q, v_mla_w_uq, v_mla_g_kv, v_mla_w_uk, v_mla_w_uv, v_mla_w_o, v_fox_w_in, v_fox_b_f, v_fox_w_o, v_ada_w, v_ada_b, v_ffn_w_gate, v_ffn_w_up, v_ffn_w_down, v_ln_g, v_ln_b):
    bl, s, d = x.shape
    t = bl * s
    ff = ffn_w_gate.shape[-1] * N_DEV
    dev = 4 * lax.axis_index("x") + 2 * lax.axis_index("y") + lax.axis_index("c")
    ada_cols = ada_w.shape[-1]
    fox_in = fox_w_in.shape[-1] * N_DEV
    mla_in = mla_w_in.shape[-1]
    mla_in_pad = mla_in + (-mla_in) % LANES

    def t_last(a):
        return jnp.swapaxes(a, -1, -2)

    local = {
        "mla_w_in": mla_w_in[0],
        "mla_w_uq": t_last(mla_w_uq[0]),
        "mla_w_uk": t_last(mla_w_uk[0]),
        "mla_w_uv": t_last(mla_w_uv[0]),
        "mla_w_o": mla_w_o[0],
        "fox_w_in": t_last(fox_w_in[0]),
        "fox_w_o": fox_w_o[0],
    }
    for i in range(DEPTH):
        local.update({f"gate{i}": t_last(ffn_w_gate[i]), f"up{i}": t_last(ffn_w_up[i]), f"down{i}": ffn_w_down[i]})
    groups = [["mla_w_in", "mla_w_uq", "mla_w_uk", "mla_w_uv", "mla_w_o"],
              ["gate0", "up0", "down0"],
              ["fox_w_in", "fox_w_o"],
              ["gate1", "up1", "down1"]]
    offsets, rows_of, slot_of, group_of = {}, {}, {}, {}
    group_rows = []
    for gi, names in enumerate(groups):
        rows = 0
        for nm in names:
            rows_of[nm] = local[nm].size // PACK_COLS
            slot_of[nm] = rows_of[nm] + (-rows_of[nm]) % 16
            offsets[nm] = rows
            group_of[nm] = gi
            rows += slot_of[nm]
        group_rows.append(rows)

    def slot(nm, rows):
        pad = [(0, 0)] * rows.ndim
        pad[-2] = (0, slot_of[nm] - rows_of[nm])
        return jnp.pad(rows, pad)

    def landing(block):
        land = lax.empty((N_DEV,) + block.shape, block.dtype)
        return lax.dynamic_update_slice(land, block[None], (dev, 0, 0))

    packed = [jnp.concatenate([slot(nm, _pack_rows(local[nm]).astype(BF16)) for nm in names], axis=0)
              for names in groups]
    gathered = [all_gather(packed[0], "gather_mla_weights")] + [None] * (len(groups) - 1)
    gather_started = [None] * len(groups)

    def depart(gi, after):
        block = lax.optimization_barrier((packed[gi], after))[0]
        gather_started[gi] = exchange_start(block, landing(block), f"gather_group{gi}_start", False)
        return gather_started[gi][4]

    def full(nm, cols):
        blk = gathered[group_of[nm]][:, offsets[nm]:offsets[nm] + rows_of[nm], :]
        return blk.reshape(-1, cols)

    w_in = jnp.pad(full("mla_w_in", mla_in), ((0, 0), (0, mla_in_pad - mla_in)))
    wt_uq = full("mla_w_uq", MLA_QR).reshape(MLA_HEADS, MLA_NOPE + MLA_ROPE, MLA_QR)
    wt_uq_n = wt_uq[:, :MLA_NOPE].reshape(MLA_HEADS * MLA_NOPE, MLA_QR)
    wt_uq_r = wt_uq[:, MLA_NOPE:].reshape(MLA_HEADS * MLA_ROPE, MLA_QR)
    wt_uk = full("mla_w_uk", MLA_KVR)
    wt_uv = full("mla_w_uv", MLA_KVR)
    w_mo = full("mla_w_o", d)
    wt_gate, wt_up, w_down = [None] * DEPTH, [None] * DEPTH, [None] * DEPTH

    def arrive(gi, after):
        gathered[gi] = exchange_wait(gather_started[gi], after, f"gather_group{gi}_wait", False)
        if gi + 1 < len(groups):
            gathered[gi] = lax.optimization_barrier((gathered[gi], depart(gi + 1, gathered[gi])))[0]
        for i in range(DEPTH):
            if group_of[f"gate{i}"] == gi:
                wt_gate[i], wt_up[i], w_down[i] = full(f"gate{i}", d), full(f"up{i}", d), full(f"down{i}", d)

    small = jnp.concatenate([c.reshape(-1, LANES), ln_g.reshape(-1, LANES), ln_b.reshape(-1, LANES)], axis=0)
    small_rows = small.shape[0]
    small = jnp.pad(small, ((0, (-small_rows) % 8), (0, 0)))
    small_all = all_gather(small, "gather_small")
    c_rows = bl * d // LANES
    c_all = small_all[:, :c_rows].reshape(N_DEV * bl, d)
    n_ln = DEPTH * 2
    ln_g_all = small_all[:, c_rows:c_rows + n_ln, :].transpose(1, 0, 2).reshape(DEPTH, 2, 1, d)
    ln_b_all = small_all[:, c_rows + n_ln:c_rows + 2 * n_ln, :].transpose(1, 0, 2).reshape(DEPTH, 2, 1, d)

    c_act = silu_rows(c_all, "silu_c")
    ada_b_loc = lax.dynamic_slice_in_dim(ada_b, dev * ada_cols, ada_cols, axis=1)
    mod_cols = [mm([(c_act, ada_w[i])], trans_b=False, out_dtype=F32, name=f"ada_fwd{i}", bias=ada_b_loc[i][None, :])
                for i in range(DEPTH)]
    mod_all = all_gather(jnp.concatenate(mod_cols, axis=0), "gather_mod")
    mod_all = mod_all.reshape(N_DEV, DEPTH, N_DEV * bl, ada_cols).transpose(1, 2, 0, 3).reshape(DEPTH, N_DEV * bl, 6 * d)
    mod_mine = lax.dynamic_slice_in_dim(mod_all, dev * bl, bl, axis=1)
    mods = [mod_mine[i].reshape(bl * 6, 1, d) for i in range(DEPTH)]
    mods[0] = mods[0] + depart(1, (mod_mine, gathered[0]))[0, 0]

    half_r = MLA_ROPE // 2
    inv_freq = ROPE_THETA ** (-jnp.arange(half_r, dtype=F32) / half_r)
    inv_freq = jnp.tile(inv_freq, LANES // half_r)[None, :]
    sign = jnp.tile(jnp.concatenate([-jnp.ones((half_r,), F32), jnp.ones((half_r,), F32)]), LANES // MLA_ROPE)[None, :]
    cos_t, sin_t = rope_tables(positions.astype(F32).reshape(t, 1), inv_freq, sign, "rope_tables")

    x2d = x.reshape(t, d)
    g_q, g_kv = mla_g_q.reshape(1, MLA_QR), mla_g_kv.reshape(1, MLA_KVR)
    b_f = jnp.pad(fox_b_f.reshape(1, FOX_HEADS), ((0, 0), (0, LANES - FOX_HEADS)))
    mla_scale = (MLA_NOPE + MLA_ROPE) ** -0.5
    fox_scale = FOX_HD ** -0.5
    tq = _attn_tiles(s)
    nk = s // tq

    saved = []
    u = modulate(x2d, mods[0], 0, 1, bl, "modulate0")
    xin = x2d
    for i in range(DEPTH):
        sv = {"u": u, "x_in": xin}
        if i % 2 == 0:
            h_in = mm([(u, w_in)], trans_b=False, out_dtype=F32, name=f"mla_in{i}")
            c_q, c_kv, k_r = mla_latents_forward(h_in, g_q, g_kv, cos_t, sin_t, f"mla_latents{i}")
            q_n = mm([(c_q, wt_uq_n)], trans_b=True, out_dtype=BF16, out_slab=True, name=f"mla_qn{i}")
            q_r_raw = mm([(c_q, wt_uq_r)], trans_b=True, out_dtype=F32, out_slab=True, name=f"mla_qr{i}")
            q_r = rope_slabs(q_r_raw, cos_t, sin_t, BF16, f"mla_qrope{i}")
            k_n = mm([(c_kv, wt_uk)], trans_b=True, out_dtype=BF16, out_slab=True, name=f"mla_kn{i}")
            v_m = mm([(c_kv, wt_uv)], trans_b=True, out_dtype=BF16, out_slab=True, name=f"mla_v{i}")
            ops = (q_n, q_r, k_n, k_r, v_m)
            o, lse, o_delta = attention_forward("mla", ops, bl, mla_scale, f"mla_attn{i}")
            y = mm([(o, w_mo)], trans_b=False, out_dtype=F32, name=f"mla_out{i}")
            sv.update(h_in=h_in, c_q=c_q, c_kv=c_kv, ops=ops, o=o, lse=lse, o_delta=o_delta)
        else:
            arrive(2, u)
            wt_fox = full("fox_w_in", d)
            wt_qkv = wt_fox[:3 * d]
            wt_f = jnp.pad(wt_fox[3 * d:], ((0, LANES - FOX_HEADS), (0, 0)))
            w_fo = full("fox_w_o", d)
            qkv = mm([(u, wt_qkv)], trans_b=True, out_dtype=BF16, out_slab=True, name=f"fox_qkv{i}")
            z = mm([(u, wt_f)], trans_b=True, out_dtype=F32, name=f"fox_z{i}")
            f_tok, f_q = fox_gate_forward(z, b_f, bl, f"fox_gate{i}")
            f_k = f_tok[:, :FOX_HEADS].reshape(bl, nk, tq, FOX_HEADS // 2, 2).transpose(0, 3, 1, 4, 2)
            f_k = jnp.pad(f_k.reshape(bl * FOX_HEADS // 2, nk, 2, tq), ((0, 0), (0, 0), (0, 6), (0, 0)))
            ops = (qkv, f_q, f_k)
            o, lse, o_delta = attention_forward("fox", ops, bl, fox_scale, f"fox_attn{i}")
            y = mm([(o, w_fo)], trans_b=False, out_dtype=F32, name=f"fox_out{i}")
            sv.update(z=z, ops=ops, o=o, lse=lse, o_delta=o_delta)
        x1, r1, u2 = residual_layer_norm(xin, y, mods[i], 2, ln_g_all[i, 0], ln_b_all[i, 0], bl, f"ln_mix{i}",
                                         next_mod=(3, 4))
        if wt_gate[i] is None:
            arrive(group_of[f"gate{i}"], u2)
        a, bb, h = swiglu_in(u2, wt_gate[i], wt_up[i], f"ffn_in{i}")
        y2 = mm([(h, w_down[i])], trans_b=False, out_dtype=F32, name=f"ffn_down{i}")
        sv.update(y=y, r1=r1, u2=u2, a=a, bb=bb, h=h, y2=y2)
        if i + 1 < DEPTH:
            xin, r2, u = residual_layer_norm(x1, y2, mods[i], 5, ln_g_all[i, 1], ln_b_all[i, 1], bl, f"ln_ffn{i}",
                                             next_mod=(0, 1, mods[i + 1]))
        else:
            xin, r2 = residual_layer_norm(x1, y2, mods[i], 5, ln_g_all[i, 1], ln_b_all[i, 1], bl, f"ln_ffn{i}")
        sv.update(r2=r2)
        saved.append(sv)

    loss_cols, d_x = loss_head(xin, loss_target.reshape(t, d), "loss_head")

    grads_full = {}
    wgrad = functools.partial(mm_tn, out_dtype=BF16)
    dmod = [[None] * 6 for _ in range(DEPTH)]
    dg_ln = [[None, None] for _ in range(DEPTH)]
    db_ln = [[None, None] for _ in range(DEPTH)]
    dg_q = dg_kv = db_f = None
    d_a, du = d_x, None
    scatter_started = [None] * len(groups)

    def scatter_start(gi):
        g = jnp.concatenate(
            [slot(nm, grads_full[nm].reshape(N_DEV, rows_of[nm], PACK_COLS).astype(BF16)) for nm in groups[gi]], axis=1)
        own = lax.dynamic_index_in_dim(g, dev, 0, keepdims=False)
        scatter_started[gi] = exchange_start(g, landing(own), f"scatter_group{gi}_start", True)

    ln_g_bwd = [[ln_g_all[i, k] for k in range(2)] for i in range(DEPTH)]
    for i in reversed(range(DEPTH)):
        sv = saved[i]
        if i + 1 < DEPTH:
            gi = group_of["fox_w_in"]
            scatter_start(gi)
            ln_g_bwd[i][1] = after_token(ln_g_bwd[i][1], scatter_started[gi])
        ln2 = (sv["r2"], sv["y2"], ln_g_bwd[i][1], ln_b_all[i, 1], (mods[i], 5))
        if du is None:
            bw = sublayer_backward(d_a, bl, f"bwd_ln_ffn{i}", ln=ln2)
        else:
            bw = sublayer_backward(d_a, bl, f"bwd_ln_ffn{i}", du=du, scale=(mods[i + 1], 1), ln=ln2)
            dmod[i + 1][0], dmod[i + 1][1] = bw["dshift"], bw["dscale"]
        dmod[i][5], dg_ln[i][1], db_ln[i][1] = bw["dgate"], bw["dg"], bw["db"]
        dy2 = bw["dy"]
        da, dbb = swiglu_out_backward(dy2, w_down[i], sv["a"], sv["bb"], f"bwd_ffn_act{i}")
        du2 = mm([(da, wt_gate[i]), (dbb, wt_up[i])], trans_b=False, out_dtype=F32, name=f"bwd_ffn_du{i}")
        grads_full[f"down{i}"] = wgrad(sv["h"], dy2, name=f"bwd_w_down{i}")
        grads_full[f"gate{i}"] = wgrad(da, sv["u2"], name=f"bwd_w_gate{i}")
        grads_full[f"up{i}"] = wgrad(dbb, sv["u2"], name=f"bwd_w_up{i}")
        gi = group_of[f"gate{i}"]
        scatter_start(gi)
        ln_g_bwd[i][0] = after_token(ln_g_bwd[i][0], scatter_started[gi])
        bw = sublayer_backward(bw["dx"], bl, f"bwd_ln_mix{i}", du=du2, scale=(mods[i], 4),
                               ln=(sv["r1"], sv["y"], ln_g_bwd[i][0], ln_b_all[i, 0], (mods[i], 2)))
        dmod[i][3], dmod[i][4], dmod[i][2] = bw["dshift"], bw["dscale"], bw["dgate"]
        dg_ln[i][0], db_ln[i][0] = bw["dg"], bw["db"]
        d_a, dy = bw["dx"], bw["dy"]
        o, lse, ops = sv["o"], sv["lse"], sv["ops"]
        if i % 2 == 0:
            do = mm([(dy, w_mo)], trans_b=True, out_dtype=BF16, out_slab=True, name=f"bwd_mla_do{i}")
            grads_full["mla_w_o"] = wgrad(o, dy, name=f"bwd_w_mla_o{i}")
            dqn, dqr, dkn, dvm, dkr = attention_backward("mla", ops, sv["o_delta"], do, lse, bl, mla_scale,
                                                         f"bwd_mla_attn{i}")
            dqr = rope_slabs(dqr, cos_t, sin_t, F32, f"bwd_mla_qrope{i}", transposed=True)
            dcq = mm([(dqn, wt_uq_n), (dqr, wt_uq_r)], trans_b=False, out_dtype=F32, name=f"bwd_mla_dcq{i}")
            dckv = mm([(dkn, wt_uk), (dvm, wt_uv)], trans_b=False, out_dtype=F32, name=f"bwd_mla_dckv{i}")
            d_uq_n = wgrad(dqn, sv["c_q"], name=f"bwd_w_uq_n{i}").reshape(MLA_HEADS, MLA_NOPE, MLA_QR)
            d_uq_r = wgrad(dqr, sv["c_q"], name=f"bwd_w_uq_r{i}").reshape(MLA_HEADS, MLA_ROPE, MLA_QR)
            grads_full["mla_w_uq"] = jnp.concatenate([d_uq_n, d_uq_r], axis=1)
            grads_full["mla_w_uk"] = wgrad(dkn, sv["c_kv"], name=f"bwd_w_uk{i}")
            grads_full["mla_w_uv"] = wgrad(dvm, sv["c_kv"], name=f"bwd_w_uv{i}")
            dh_in, dg_q, dg_kv = mla_latents_backward(sv["h_in"], dcq, dckv, dkr, g_q, g_kv, cos_t, sin_t,
                                                      f"bwd_mla_latents{i}")
            du = mm([(dh_in, w_in)], trans_b=True, out_dtype=F32, name=f"bwd_mla_du{i}")
            grads_full["mla_w_in"] = wgrad(sv["u"], dh_in, name=f"bwd_w_mla_in{i}")[:, :mla_in]
        else:
            do = mm([(dy, w_fo)], trans_b=True, out_dtype=BF16, out_slab=True, name=f"bwd_fox_do{i}")
            grads_full["fox_w_o"] = wgrad(o, dy, name=f"bwd_w_fox_o{i}")
            dq, dk, dvf, dfk = attention_backward("fox", ops, sv["o_delta"], do, lse, bl, fox_scale, f"bwd_fox_attn{i}")
            df = dfk[:, :, :2, :].reshape(bl, FOX_HEADS // 2, nk, 2, tq).transpose(0, 2, 4, 1, 3).reshape(t, FOX_HEADS)
            df = jnp.pad(df, ((0, 0), (0, LANES - FOX_HEADS)))
            dz, db_f = fox_gate_backward(sv["z"], b_f, df, bl, f"bwd_fox_gate{i}")
            du = mm([(dq, wt_fox[0:d]), (dk, wt_fox[d:2 * d]), (dvf, wt_fox[2 * d:3 * d]), (dz, wt_f)],
                    trans_b=False, out_dtype=F32, name=f"bwd_fox_du{i}")
            u_f = sv["u"]
            grads_full["fox_w_in"] = jnp.concatenate(
                [wgrad(dq, u_f, name=f"bwd_w_fox_q{i}"), wgrad(dk, u_f, name=f"bwd_w_fox_k{i}"),
                 wgrad(dvf, u_f, name=f"bwd_w_fox_v{i}"), wgrad(dz, u_f, name=f"bwd_w_fox_f{i}")[:FOX_HEADS]], axis=0)
    bw = sublayer_backward(d_a, bl, "bwd_input", du=du, scale=(mods[0], 1), x_in=x2d)
    dmod[0][0], dmod[0][1] = bw["dshift"], bw["dscale"]
    grad_x = bw["dx"].reshape(bl, s, d)

    dmod_rows = jnp.concatenate([r.reshape(bl, d) for layer in dmod for r in layer], axis=0)
    dmod_rows = dmod_rows.reshape(DEPTH, 6, bl, d).transpose(0, 2, 1, 3)
    n_mod = dmod_rows.size // LANES
    ln_parts = [dg_ln[i][k] for i in range(DEPTH) for k in range(2)] + [db_ln[i][k] for i in range(DEPTH) for k in range(2)]
    small_g = jnp.concatenate([dmod_rows.reshape(-1, LANES), dg_q.reshape(-1, LANES), dg_kv.reshape(-1, LANES), db_f]
                              + [p.reshape(-1, LANES) for p in ln_parts] + [loss_cols.reshape(-1, LANES)], axis=0)
    n_small = small_g.shape[0]
    small_g = jnp.pad(small_g, ((0, (-n_small) % 8), (0, 0)))
    small_g_all = all_gather(small_g, "gather_small_grads")
    small_sum = sum_leading(small_g_all, "sum_small_grads")
    per_seq = DEPTH * 6 * d // LANES
    dmod_all = small_g_all[:, :n_mod].reshape(N_DEV, DEPTH, bl, 6 * d).transpose(1, 0, 2, 3)
    dmod_all = dmod_all.reshape(DEPTH, N_DEV * bl, 6 * d)
    o1 = n_mod
    grad_g_q = small_sum[o1:o1 + MLA_QR // LANES].reshape(1, MLA_QR)
    o1 += MLA_QR // LANES
    grad_g_kv = small_sum[o1:o1 + MLA_KVR // LANES].reshape(1, MLA_KVR)
    o1 += MLA_KVR // LANES
    grad_b_f = small_sum[o1:o1 + 1, :FOX_HEADS]
    o1 += 1
    n_ln_rows = DEPTH * 2 * d // LANES
    grad_ln_g_full = small_sum[o1:o1 + n_ln_rows].reshape(DEPTH, 2, d)
    grad_ln_b_full = small_sum[o1 + n_ln_rows:o1 + 2 * n_ln_rows].reshape(DEPTH, 2, d)
    loss = jnp.sum(small_sum[o1 + 2 * n_ln_rows:o1 + 2 * n_ln_rows + d // LANES])
    shard = d // N_DEV
    grad_ln_g = lax.dynamic_slice_in_dim(grad_ln_g_full, dev * shard, shard, axis=2)
    grad_ln_b = lax.dynamic_slice_in_dim(grad_ln_b_full, dev * shard, shard, axis=2)
    by_seq = small_g_all[:, :n_mod].reshape(N_DEV, DEPTH, bl, 6 * d // LANES, LANES).transpose(0, 2, 1, 3, 4)
    grad_ada_b = sum_leading(by_seq.reshape(N_DEV * bl, per_seq, LANES), "sum_ada_b").reshape(DEPTH, 6 * d)
    dmod_cols = lax.dynamic_slice_in_dim(dmod_all, dev * ada_cols, ada_cols, axis=2)
    grad_ada_w = jnp.stack([mm_tn(c_act, dmod_cols[i], name=f"bwd_w_ada{i}") for i in range(DEPTH)])

    scatter_start(0)
    after = bw["dx"]
    g_mine = [None] * len(groups)
    for gi in reversed(range(len(groups))):
        landed = exchange_wait(scatter_started[gi], after, f"scatter_group{gi}_wait", True)
        g_mine[gi] = sum_leading(landed, f"scatter_group{gi}_sum")
        after = g_mine[gi]

    def mine(nm, shape):
        return g_mine[group_of[nm]][offsets[nm]:offsets[nm] + rows_of[nm]].reshape(shape)

    def shard_t(nm, a):
        return mine(nm, t_last(a).shape)

    transposed = {"mla_w_uq", "mla_w_uk", "mla_w_uv", "fox_w_in", "ffn_w_gate", "ffn_w_up"}
    grads = {
        "mla_w_in": mine("mla_w_in", mla_w_in[0].shape)[None],
        "mla_g_q": grad_g_q,
        "mla_w_uq": shard_t("mla_w_uq", mla_w_uq[0])[None],
        "mla_g_kv": grad_g_kv,
        "mla_w_uk": shard_t("mla_w_uk", mla_w_uk[0])[None],
        "mla_w_uv": shard_t("mla_w_uv", mla_w_uv[0])[None],
        "mla_w_o": mine("mla_w_o", mla_w_o[0].shape)[None],
        "fox_w_in": shard_t("fox_w_in", fox_w_in[0])[None],
        "fox_b_f": grad_b_f,
        "fox_w_o": mine("fox_w_o", fox_w_o[0].shape)[None],
        "ada_w": grad_ada_w,
        "ada_b": grad_ada_b,
        "ffn_w_gate": jnp.stack([shard_t(f"gate{i}", ffn_w_gate[i]) for i in range(DEPTH)]),
        "ffn_w_up": jnp.stack([shard_t(f"up{i}", ffn_w_up[i]) for i in range(DEPTH)]),
        "ffn_w_down": jnp.stack([mine(f"down{i}", ffn_w_down[i].shape) for i in range(DEPTH)]),
        "ln_g": grad_ln_g,
        "ln_b": grad_ln_b,
    }
    weights = dict(mla_w_in=mla_w_in, mla_g_q=mla_g_q, mla_w_uq=mla_w_uq, mla_g_kv=mla_g_kv, mla_w_uk=mla_w_uk,
                   mla_w_uv=mla_w_uv, mla_w_o=mla_w_o, fox_w_in=fox_w_in, fox_b_f=fox_b_f, fox_w_o=fox_w_o,
                   ada_w=ada_w, ada_b=ada_b, ffn_w_gate=ffn_w_gate, ffn_w_up=ffn_w_up, ffn_w_down=ffn_w_down,
                   ln_g=ln_g, ln_b=ln_b)
    first = dict(mla_w_in=m_mla_w_in, mla_g_q=m_mla_g_q, mla_w_uq=m_mla_w_uq, mla_g_kv=m_mla_g_kv, mla_w_uk=m_mla_w_uk,
                 mla_w_uv=m_mla_w_uv, mla_w_o=m_mla_w_o, fox_w_in=m_fox_w_in, fox_b_f=m_fox_b_f, fox_w_o=m_fox_w_o,
                 ada_w=m_ada_w, ada_b=m_ada_b, ffn_w_gate=m_ffn_w_gate, ffn_w_up=m_ffn_w_up, ffn_w_down=m_ffn_w_down,
                 ln_g=m_ln_g, ln_b=m_ln_b)
    second = dict(mla_w_in=v_mla_w_in, mla_g_q=v_mla_g_q, mla_w_uq=v_mla_w_uq, mla_g_kv=v_mla_g_kv, mla_w_uk=v_mla_w_uk,
                  mla_w_uv=v_mla_w_uv, mla_w_o=v_mla_w_o, fox_w_in=v_fox_w_in, fox_b_f=v_fox_b_f, fox_w_o=v_fox_w_o,
                  ada_w=v_ada_w, ada_b=v_ada_b, ffn_w_gate=v_ffn_w_gate, ffn_w_up=v_ffn_w_up, ffn_w_down=v_ffn_w_down,
                  ln_g=v_ln_g, ln_b=v_ln_b)
    order = list(weights)
    g_out, d_out, m_out, v_out = [], [], [], []
    for nm in order:
        lay = t_last if nm in transposed else (lambda a: a)
        w = lay(weights[nm])
        g = grads[nm].reshape(w.shape)
        delta, new_m, new_v = adamw(w, g, lay(first[nm]), lay(second[nm]), f"adamw_{nm}")
        g_out.append(lay(g))
        d_out.append(lay(delta))
        m_out.append(lay(new_m))
        v_out.append(lay(new_v))
    return (loss, grad_x, *g_out, *d_out, *m_out, *v_out)
```

```python
import functools
import math

import jax
import jax.numpy as jnp
from jax import lax
from jax.experimental import pallas as pl
from jax.experimental.pallas import tpu as pltpu

F32 = jnp.float32
BF16 = jnp.bfloat16
LANES = 128
N_DEV = 8
VMEM_LIMIT_BYTES = 56 * 1024 * 1024

DEPTH = 2
MLA_HEADS = 8
MLA_NOPE = 128
MLA_ROPE = 64
MLA_V = 128
MLA_QR = 256
MLA_KVR = 256
ROPE_THETA = 10000.0
FOX_HEADS = 16
FOX_HD = 64
ALPHA = (2.0 * DEPTH) ** 0.25
NORM_EPS = 1e-5
ADAM_LR = 0.001
ADAM_B1 = 0.9
ADAM_B2 = 0.999
ADAM_EPS = 1e-08
ADAM_WD = 0.01
ADAM_STEP = 10

MESH_AXES = ("x", "y", "c")
MESH = pl.DeviceIdType.MESH


def _params(*sem):
    return pltpu.CompilerParams(dimension_semantics=sem, vmem_limit_bytes=VMEM_LIMIT_BYTES)


def _tile(n, cap, mult=LANES):
    if n <= cap:
        return n
    best = None
    for t in range(mult, cap + 1, mult):
        if n % t == 0:
            best = t
    assert best is not None, (n, cap, mult)
    return best


def _dot(a, b, dims):
    return lax.dot_general(a, b, (dims, ((), ())), preferred_element_type=F32)


def _nn(a, b):
    return _dot(a, b, ((1,), (0,)))


def _nt(a, b):
    return _dot(a, b, ((1,), (1,)))


def _tn(a, b):
    return _dot(a, b, ((0,), (0,)))


def _me():
    return lax.axis_index("x"), lax.axis_index("y"), lax.axis_index("c")


def all_gather(x_loc, name):
    r, c = x_loc.shape

    def body(x_ref, out_ref, send_sems, recv_sems, local_sem):
        x, y, cc = _me()
        me, sibling = (x, y, cc), (x, y, 1 - cc)
        chips = [(1 - x, y), (x, 1 - y), (1 - x, 1 - y)]

        def rows(px, py, pc):
            return out_ref.at[4 * px + 2 * py + pc]

        def copy(k, block, to, src=None):
            return pltpu.make_async_remote_copy(
                src_ref=rows(*block) if src is None else src, dst_ref=rows(*block),
                send_sem=send_sems.at[k], recv_sem=recv_sems.at[k], device_id=to, device_id_type=MESH)

        mine = pltpu.make_async_copy(x_ref, rows(*me), local_sem)
        mine.start()
        first = [copy(0, me, sibling, src=x_ref)]
        first += [copy(1 + j, me, (*chip, cc), src=x_ref) for j, chip in enumerate(chips)]
        for cp in first:
            cp.start()
        passed = [copy(4 + j, (*chip, cc), sibling) for j, chip in enumerate(chips)]
        for j, chip in enumerate(chips):
            copy(1 + j, (*chip, cc), me).wait_recv()
            passed[j].start()
        copy(0, sibling, me).wait_recv()
        for j, chip in enumerate(chips):
            copy(4 + j, (*chip, 1 - cc), me).wait_recv()
        for cp in first + passed:
            cp.wait_send()
        mine.wait()

    return pl.pallas_call(
        body, name=name,
        out_shape=jax.ShapeDtypeStruct((N_DEV, r, c), x_loc.dtype),
        in_specs=[pl.BlockSpec(memory_space=pl.ANY)],
        out_specs=pl.BlockSpec(memory_space=pl.ANY),
        scratch_shapes=[pltpu.SemaphoreType.DMA((7,)), pltpu.SemaphoreType.DMA((7,)), pltpu.SemaphoreType.DMA(())],
    )(x_loc)


HBM_SPEC = pl.BlockSpec(memory_space=pltpu.HBM)
SEM_SPEC = pl.BlockSpec(memory_space=pltpu.SEMAPHORE)
N_PEERS = N_DEV - 1


def _peer(k):
    x, y, c = _me()
    return (1 - x if k & 4 else x, 1 - y if k & 2 else y, 1 - c if k & 1 else c)


def _exchange_copies(src_ref, land_ref, send_sems, recv_sems, scatter):
    x, y, c = _me()
    mine = 4 * x + 2 * y + c
    copies = []
    for k in range(1, N_DEV):
        px, py, pc = _peer(k)
        src = src_ref.at[4 * px + 2 * py + pc] if scatter else src_ref
        copies.append(pltpu.make_async_remote_copy(
            src_ref=src, dst_ref=land_ref.at[mine], send_sem=send_sems.at[k - 1], recv_sem=recv_sems.at[k - 1],
            device_id=(px, py, pc), device_id_type=MESH))
    return copies


def exchange_start(src, land, name, scatter):
    def body(src_ref, land_ref, send_sems, recv_sems, src_thru, land_thru, token):
        for cp in _exchange_copies(src_ref, land_ref, send_sems, recv_sems, scatter):
            cp.start()
        token[...] = jnp.zeros_like(token)

    return pl.pallas_call(
        body, name=name,
        out_shape=(pltpu.SemaphoreType.DMA((N_PEERS,)), pltpu.SemaphoreType.DMA((N_PEERS,)),
                   pltpu.HBM(src.shape, src.dtype), pltpu.HBM(land.shape, land.dtype),
                   jax.ShapeDtypeStruct((8, LANES), F32)),
        in_specs=(HBM_SPEC, HBM_SPEC),
        out_specs=(SEM_SPEC, SEM_SPEC, HBM_SPEC, HBM_SPEC, pl.BlockSpec(memory_space=pltpu.VMEM)),
        input_output_aliases={0: 2, 1: 3},
        compiler_params=pltpu.CompilerParams(has_side_effects=pltpu.SideEffectType.DATAFLOW_SIDE_EFFECTING),
    )(pltpu.with_memory_space_constraint(src, pltpu.HBM), pltpu.with_memory_space_constraint(land, pltpu.HBM))


def exchange_wait(started, after, name, scatter):
    send_sems, recv_sems, src_thru, land_thru, _ = started

    def body(src_ref, land_ref, send_sems, recv_sems, after_ref, src_dead, got_ref):
        for cp in _exchange_copies(src_ref, land_ref, send_sems, recv_sems, scatter):
            cp.wait_send()
            cp.wait_recv()

    return pl.pallas_call(
        body, name=name,
        out_shape=(pltpu.HBM(src_thru.shape, src_thru.dtype), pltpu.HBM(land_thru.shape, land_thru.dtype)),
        in_specs=(HBM_SPEC, HBM_SPEC, SEM_SPEC, SEM_SPEC, pl.BlockSpec(memory_space=pl.ANY)),
        out_specs=(HBM_SPEC, HBM_SPEC), input_output_aliases={0: 0, 1: 1},
        compiler_params=pltpu.CompilerParams(has_side_effects=pltpu.SideEffectType.DATAFLOW_SIDE_EFFECTING),
    )(src_thru, land_thru, send_sems, recv_sems, after)[1]


def after_token(small, started):
    return small + started[4][0, 0]


def sum_leading(x, name):
    n, r, c = x.shape
    tr = _tile(r, 512, 16)

    def body(x_ref, o_ref):
        acc = x_ref[0].astype(F32)
        for k in range(1, n):
            acc = acc + x_ref[k].astype(F32)
        o_ref[...] = acc

    return pl.pallas_call(
        body, name=name,
        out_shape=jax.ShapeDtypeStruct((r, c), F32),
        grid=(r // tr,),
        in_specs=[pl.BlockSpec((n, tr, c), lambda i: (0, i, 0))],
        out_specs=pl.BlockSpec((tr, c), lambda i: (i, 0)),
        compiler_params=_params("arbitrary"),
    )(x)


MM_VMEM_BUDGET = 36 * 1024 * 1024
GRID_STEP_AS_BYTES = 1 << 20


def _mm_tiles(m, n, a_row_bytes, b_col_bytes, out_bytes):
    tms = [c for c in (2048, 1024, 512, 256, 128, 64, 32, 16, 8) if m % c == 0] or [m]
    tns = [c for c in range(LANES, min(n, 2048) + 1, LANES) if n % c == 0] or [n]
    best = None
    for tm in tms:
        for tn in tns:
            vmem = 2 * (tm * a_row_bytes + tn * b_col_bytes) + 2 * tm * tn * out_bytes + tm * tn * 4
            if vmem > MM_VMEM_BUDGET:
                continue
            steps = (m // tm) * (n // tn)
            cost = steps * GRID_STEP_AS_BYTES + (m // tm) * n * b_col_bytes + m * a_row_bytes
            if best is None or cost < best[0]:
                best = (cost, tm, tn)
    assert best is not None, (m, n, a_row_bytes, b_col_bytes)
    return best[1], best[2]


def mm(pairs, *, trans_b, out_dtype, name, out_slab=False, bias=None):
    a0 = pairs[0][0]
    m = a0.shape[1] if a0.ndim == 3 else a0.shape[0]
    n = pairs[0][1].shape[0] if trans_b else pairs[0][1].shape[1]
    a_row_bytes = sum((b.shape[1] if trans_b else b.shape[0]) * a.dtype.itemsize for a, b in pairs)
    b_col_bytes = sum((b.shape[1] if trans_b else b.shape[0]) * b.dtype.itemsize for _, b in pairs)
    tm, tn = _mm_tiles(m, n, a_row_bytes, b_col_bytes, jnp.dtype(out_dtype).itemsize)
    slabs = [a.ndim == 3 for a, _ in pairs]
    n_pairs = len(pairs)

    def body(*refs):
        o_ref = refs[-1]
        acc = bias_ref = None
        if bias is not None:
            bias_ref = refs[2 * n_pairs]
        for i in range(n_pairs):
            a_ref, b_ref = refs[2 * i], refs[2 * i + 1]
            if slabs[i]:
                a = jnp.concatenate([a_ref[s].astype(BF16) for s in range(a_ref.shape[0])], axis=1)
            else:
                a = a_ref[...].astype(BF16)
            b = b_ref[...].astype(BF16)
            part = _nt(a, b) if trans_b else _nn(a, b)
            acc = part if acc is None else acc + part
        if bias_ref is not None:
            acc = acc + bias_ref[...]
        if out_slab:
            for s in range(tn // LANES):
                o_ref[s] = acc[:, s * LANES:(s + 1) * LANES].astype(out_dtype)
        else:
            o_ref[...] = acc.astype(out_dtype)

    in_specs, args = [], []
    for (a, b), slab in zip(pairs, slabs):
        if slab:
            in_specs.append(pl.BlockSpec((a.shape[0], tm, LANES), lambda i, j: (0, i, 0)))
        else:
            in_specs.append(pl.BlockSpec((tm, a.shape[1]), lambda i, j: (i, 0)))
        if trans_b:
            in_specs.append(pl.BlockSpec((tn, b.shape[1]), lambda i, j: (j, 0)))
        else:
            in_specs.append(pl.BlockSpec((b.shape[0], tn), lambda i, j: (0, j)))
        args += [a, b]
    if bias is not None:
        in_specs.append(pl.BlockSpec((1, tn), lambda i, j: (0, j)))
        args.append(bias)
    if out_slab:
        out_shape = jax.ShapeDtypeStruct((n // LANES, m, LANES), out_dtype)
        out_spec = pl.BlockSpec((tn // LANES, tm, LANES), lambda i, j: (j, i, 0))
    else:
        out_shape = jax.ShapeDtypeStruct((m, n), out_dtype)
        out_spec = pl.BlockSpec((tm, tn), lambda i, j: (i, j))
    return pl.pallas_call(
        body, name=name, out_shape=out_shape, grid=(m // tm, n // tn),
        in_specs=in_specs, out_specs=out_spec,
        compiler_params=_params("arbitrary", "arbitrary"),
    )(*args)


def mm_tn(a, b, *, name, out_dtype=F32, tk_cap=1536, tn_cap=1024, tm_cap=512):
    slab = a.ndim == 3
    m = a.shape[1] if slab else a.shape[0]
    k = a.shape[0] * LANES if slab else a.shape[1]
    n = b.shape[1]
    tk = _tile(k, tk_cap)
    tn = _tile(n, tn_cap)
    tm = _tile(m, tm_cap, 8)
    n_steps = m // tm

    def body(a_ref, b_ref, o_ref, acc_ref):
        step = pl.program_id(2)

        @pl.when(step == 0)
        def _():
            acc_ref[...] = jnp.zeros_like(acc_ref)

        bb = b_ref[...].astype(BF16)
        if slab:
            for s in range(tk // LANES):
                acc_ref[s * LANES:(s + 1) * LANES, :] += _tn(a_ref[s].astype(BF16), bb)
        else:
            acc_ref[...] += _tn(a_ref[...].astype(BF16), bb)

        @pl.when(step == n_steps - 1)
        def _():
            o_ref[...] = acc_ref[...].astype(out_dtype)

    if slab:
        a_spec = pl.BlockSpec((tk // LANES, tm, LANES), lambda i, j, t: (i, t, 0))
    else:
        a_spec = pl.BlockSpec((tm, tk), lambda i, j, t: (t, i))
    return pl.pallas_call(
        body, name=name, out_shape=jax.ShapeDtypeStruct((k, n), out_dtype), grid=(k // tk, n // tn, n_steps),
        in_specs=[a_spec, pl.BlockSpec((tm, tn), lambda i, j, t: (t, j))],
        out_specs=pl.BlockSpec((tk, tn), lambda i, j, t: (i, j)),
        scratch_shapes=[pltpu.VMEM((tk, tn), F32)],
        compiler_params=_params("arbitrary", "arbitrary", "arbitrary"),
    )(a, b)


def _row_spec(d, k):
    return pl.BlockSpec((1, 1, d), lambda b, i: (6 * b + k, 0, 0))


def modulate(x, mod, k_shift, k_scale, bl, name):
    t, d = x.shape
    s = t // bl
    tm = _tile(s, 512, 8)
    nt = s // tm

    def body(x_ref, sh_ref, sc_ref, o_ref):
        o_ref[...] = (x_ref[...] * (1.0 + sc_ref[0]) + sh_ref[0]).astype(BF16)

    return pl.pallas_call(
        body, name=name, out_shape=jax.ShapeDtypeStruct((t, d), BF16), grid=(bl, nt),
        in_specs=[pl.BlockSpec((tm, d), lambda b, i: (b * nt + i, 0)), _row_spec(d, k_shift), _row_spec(d, k_scale)],
        out_specs=pl.BlockSpec((tm, d), lambda b, i: (b * nt + i, 0)),
        compiler_params=_params("arbitrary", "arbitrary"),
    )(x, mod, mod)


def _layer_norm_stats(r):
    mu = jnp.mean(r, axis=-1, keepdims=True)
    rc = r - mu
    var = jnp.mean(rc * rc, axis=-1, keepdims=True)
    rstd = lax.rsqrt(var + NORM_EPS)
    return rc * rstd, rstd


def residual_layer_norm(x, y, mod, k_gate, g, b, bl, name, next_mod=None):
    t, d = x.shape
    s = t // bl
    tm = _tile(s, 256, 8)
    nt = s // tm
    has_next = next_mod is not None

    def body(*refs):
        x_ref, y_ref, gt_ref, g_ref, b_ref = refs[:5]
        rest = refs[5:]
        if has_next:
            sh_ref, sc_ref, o_ref, r_ref, u_ref = rest
        else:
            o_ref, r_ref = rest
        r = ALPHA * x_ref[...] + (1.0 + gt_ref[0]) * y_ref[...]
        xhat, _ = _layer_norm_stats(r)
        out = xhat * g_ref[...] + b_ref[...]
        o_ref[...] = out
        r_ref[...] = r
        if has_next:
            u_ref[...] = (out * (1.0 + sc_ref[0]) + sh_ref[0]).astype(BF16)

    tok = pl.BlockSpec((tm, d), lambda bb, i: (bb * nt + i, 0))
    vec = pl.BlockSpec((1, d), lambda bb, i: (0, 0))
    in_specs = [tok, tok, _row_spec(d, k_gate), vec, vec]
    args = [x, y, mod, g, b]
    out_shape = [jax.ShapeDtypeStruct((t, d), F32), jax.ShapeDtypeStruct((t, d), F32)]
    out_specs = [tok, tok]
    if has_next:
        in_specs += [_row_spec(d, next_mod[0]), _row_spec(d, next_mod[1])]
        args += [mod if len(next_mod) == 2 else next_mod[2]] * 2
        out_shape.append(jax.ShapeDtypeStruct((t, d), BF16))
        out_specs.append(tok)
    return pl.pallas_call(
        body, name=name, out_shape=out_shape, grid=(bl, nt), in_specs=in_specs, out_specs=out_specs,
        compiler_params=_params("arbitrary", "arbitrary"),
    )(*args)


def loss_head(xo, target, name):
    t, d = xo.shape
    tm = _tile(t, 512, 8)

    def body(x_ref, t_ref, l_ref, dx_ref):
        @pl.when(pl.program_id(0) == 0)
        def _():
            l_ref[...] = jnp.zeros_like(l_ref)

        e = x_ref[...] - t_ref[...]
        l_ref[...] += jnp.sum(e * e, axis=0, keepdims=True) * (0.5 / d)
        dx_ref[...] = e * (1.0 / d)

    tok = pl.BlockSpec((tm, d), lambda i: (i, 0))
    return pl.pallas_call(
        body, name=name,
        out_shape=[jax.ShapeDtypeStruct((1, d), F32), jax.ShapeDtypeStruct((t, d), F32)],
        grid=(t // tm,), in_specs=[tok, tok],
        out_specs=[pl.BlockSpec((1, d), lambda i: (0, 0)), tok],
        compiler_params=_params("arbitrary"),
    )(xo, target)


def sublayer_backward(d_a, bl, name, *, du=None, scale=None, x_in=None, ln=None):
    t, d = d_a.shape
    s = t // bl
    tm = _tile(s, 256, 8)
    nt = s // tm
    has_mod = du is not None
    has_ln = ln is not None
    assert has_mod or has_ln
    assert has_ln or x_in is not None

    def body(*refs):
        refs = list(refs)
        da_ref = refs.pop(0)
        if has_mod:
            du_ref, sc_ref = refs.pop(0), refs.pop(0)
        if has_ln:
            r_ref, y_ref, g_ref, b_ref, gt_ref = (refs.pop(0) for _ in range(5))
        elif has_mod:
            xin_ref = refs.pop(0)
        dx_ref = refs.pop(0)
        if has_ln:
            dy_ref, dg_ref, db_ref, dgt_ref = (refs.pop(0) for _ in range(4))
        if has_mod:
            dsc_ref, dsh_ref = refs.pop(0), refs.pop(0)
        first_tile = pl.program_id(1) == 0
        first_step = jnp.logical_and(pl.program_id(0) == 0, first_tile)

        dout = da_ref[...]
        if has_ln:
            xhat, rstd = _layer_norm_stats(r_ref[...])
        if has_mod:
            duv = du_ref[...]
            dout = dout + duv * (1.0 + sc_ref[0])
            xin = xhat * g_ref[...] + b_ref[...] if has_ln else xin_ref[...]

            @pl.when(first_tile)
            def _():
                dsc_ref[...] = jnp.zeros_like(dsc_ref)
                dsh_ref[...] = jnp.zeros_like(dsh_ref)

            dsc_ref[0] += jnp.sum(duv * xin, axis=0, keepdims=True)
            dsh_ref[0] += jnp.sum(duv, axis=0, keepdims=True)
        if not has_ln:
            dx_ref[...] = dout
            return

        @pl.when(first_step)
        def _():
            dg_ref[...] = jnp.zeros_like(dg_ref)
            db_ref[...] = jnp.zeros_like(db_ref)

        @pl.when(first_tile)
        def _():
            dgt_ref[...] = jnp.zeros_like(dgt_ref)

        dg_ref[...] += jnp.sum(dout * xhat, axis=0, keepdims=True)
        db_ref[...] += jnp.sum(dout, axis=0, keepdims=True)
        dxh = dout * g_ref[...]
        dr = rstd * (dxh - jnp.mean(dxh, axis=-1, keepdims=True) - xhat * jnp.mean(dxh * xhat, axis=-1, keepdims=True))
        dx_ref[...] = ALPHA * dr
        dy_ref[...] = ((1.0 + gt_ref[0]) * dr).astype(BF16)
        dgt_ref[0] += jnp.sum(dr * y_ref[...], axis=0, keepdims=True)

    tok = pl.BlockSpec((tm, d), lambda bb, i: (bb * nt + i, 0))
    vec = pl.BlockSpec((1, d), lambda bb, i: (0, 0))
    seq = pl.BlockSpec((1, 1, d), lambda bb, i: (bb, 0, 0))
    in_specs, args = [tok], [d_a]
    if has_mod:
        in_specs += [tok, _row_spec(d, scale[1])]
        args += [du, scale[0]]
    if has_ln:
        r, y, g, b, gate = ln
        in_specs += [tok, tok, vec, vec, _row_spec(d, gate[1])]
        args += [r, y, g, b, gate[0]]
    elif has_mod:
        in_specs.append(tok)
        args.append(x_in)
    names = ["dx"]
    out_shape, out_specs = [jax.ShapeDtypeStruct((t, d), F32)], [tok]
    if has_ln:
        names += ["dy", "dg", "db", "dgate"]
        out_shape += [jax.ShapeDtypeStruct((t, d), BF16), jax.ShapeDtypeStruct((1, d), F32),
                      jax.ShapeDtypeStruct((1, d), F32), jax.ShapeDtypeStruct((bl, 1, d), F32)]
        out_specs += [tok, vec, vec, seq]
    if has_mod:
        names += ["dscale", "dshift"]
        out_shape += [jax.ShapeDtypeStruct((bl, 1, d), F32)] * 2
        out_specs += [seq, seq]
    outs = pl.pallas_call(
        body, name=name, out_shape=out_shape, grid=(bl, nt), in_specs=in_specs, out_specs=out_specs,
        compiler_params=_params("arbitrary", "arbitrary"),
    )(*args)
    return dict(zip(names, outs))


def _silu(a):
    return a * jax.nn.sigmoid(a)


def silu_rows(a, name):
    def body(a_ref, o_ref):
        o_ref[...] = _silu(a_ref[...]).astype(BF16)

    return pl.pallas_call(body, name=name, out_shape=jax.ShapeDtypeStruct(a.shape, BF16))(a)


def _swiglu_tiles(t, f):
    return _tile(t, 512, 8), _tile(f, 1536)


def swiglu_in(u, wt_gate, wt_up, name):
    t, d = u.shape
    f = wt_gate.shape[0]
    tm, tf = _swiglu_tiles(t, f)

    def body(u_ref, g_ref, w_ref, a_ref, b_ref, h_ref):
        uv = u_ref[...]
        a = _nt(uv, g_ref[...])
        b = _nt(uv, w_ref[...])
        a_ref[...] = a
        b_ref[...] = b
        h_ref[...] = (_silu(a) * b).astype(BF16)

    w_spec = pl.BlockSpec((tf, d), lambda i, j: (j, 0))
    o_spec = pl.BlockSpec((tm, tf), lambda i, j: (i, j))
    return pl.pallas_call(
        body, name=name,
        out_shape=[jax.ShapeDtypeStruct((t, f), F32), jax.ShapeDtypeStruct((t, f), F32), jax.ShapeDtypeStruct((t, f), BF16)],
        grid=(t // tm, f // tf), in_specs=[pl.BlockSpec((tm, d), lambda i, j: (i, 0)), w_spec, w_spec],
        out_specs=[o_spec, o_spec, o_spec], compiler_params=_params("arbitrary", "arbitrary"),
    )(u, wt_gate, wt_up)


def swiglu_out_backward(dy, w_down, a, b, name):
    t, d = dy.shape
    f = w_down.shape[0]
    tm, tf = _swiglu_tiles(t, f)

    def body(dy_ref, w_ref, a_ref, b_ref, da_ref, db_ref):
        dh = _nt(dy_ref[...], w_ref[...])
        av = a_ref[...]
        sig = jax.nn.sigmoid(av)
        da_ref[...] = (dh * b_ref[...] * (sig * (1.0 + av * (1.0 - sig)))).astype(BF16)
        db_ref[...] = (dh * (av * sig)).astype(BF16)

    spec = pl.BlockSpec((tm, tf), lambda i, j: (i, j))
    return pl.pallas_call(
        body, name=name, out_shape=[jax.ShapeDtypeStruct((t, f), BF16)] * 2, grid=(t // tm, f // tf),
        in_specs=[pl.BlockSpec((tm, d), lambda i, j: (i, 0)), pl.BlockSpec((tf, d), lambda i, j: (j, 0)), spec, spec],
        out_specs=[spec, spec], compiler_params=_params("arbitrary", "arbitrary"),
    )(dy, w_down, a, b)


def rope_tables(pos, inv_freq, sign, name):
    t = pos.shape[0]
    tm = _tile(t, 512, 8)

    def body(p_ref, f_ref, s_ref, c_out, s_out):
        ang = p_ref[...] * f_ref[...]
        c_out[...] = jnp.cos(ang)
        s_out[...] = jnp.sin(ang) * s_ref[...]

    vec = pl.BlockSpec((1, LANES), lambda i: (0, 0))
    tab = pl.BlockSpec((tm, LANES), lambda i: (i, 0))
    return pl.pallas_call(
        body, name=name, out_shape=[jax.ShapeDtypeStruct((t, LANES), F32)] * 2, grid=(t // tm,),
        in_specs=[pl.BlockSpec((tm, 1), lambda i: (i, 0)), vec, vec], out_specs=[tab, tab],
        compiler_params=_params("arbitrary"),
    )(pos, inv_freq, sign)


def _rot_half(v):
    lane = lax.broadcasted_iota(jnp.int32, v.shape, v.ndim - 1)
    up = pltpu.roll(v, LANES - MLA_ROPE // 2, v.ndim - 1)
    down = pltpu.roll(v, MLA_ROPE // 2, v.ndim - 1)
    return jnp.where(lane % MLA_ROPE < MLA_ROPE // 2, up, down)


def _rope(v, cos, sin_signed):
    return v * cos + _rot_half(v) * sin_signed


def _rope_transposed(dv, cos, sin_signed):
    return dv * cos + _rot_half(dv * sin_signed)


def rope_slabs(v, cos, sin_signed, out_dtype, name, transposed=False):
    ns, t, _ = v.shape
    tm = _tile(t, 512, 8)
    fn = _rope_transposed if transposed else _rope

    def body(v_ref, c_ref, s_ref, o_ref):
        o_ref[0] = fn(v_ref[0].astype(F32), c_ref[...], s_ref[...]).astype(out_dtype)

    tab = pl.BlockSpec((tm, LANES), lambda j, i: (i, 0))
    spec = pl.BlockSpec((1, tm, LANES), lambda j, i: (j, i, 0))
    return pl.pallas_call(
        body, name=name, out_shape=jax.ShapeDtypeStruct(v.shape, out_dtype), grid=(ns, t // tm),
        in_specs=[spec, tab, tab], out_specs=spec, compiler_params=_params("arbitrary", "arbitrary"),
    )(v, cos, sin_signed)


def _rms(x):
    rinv = lax.rsqrt(jnp.mean(x * x, axis=-1, keepdims=True) + NORM_EPS)
    return x * rinv, rinv


def mla_latents_forward(h_in, g_q, g_kv, cos, sin_signed, name):
    t = h_in.shape[0]
    tm = _tile(t, 512, 8)

    def body(h_ref, gq_ref, gkv_ref, c_ref, s_ref, cq_ref, ckv_ref, kr_ref):
        cq_ref[...] = (_rms(h_ref[:, 0:MLA_QR])[0] * gq_ref[...]).astype(BF16)
        ckv_ref[...] = (_rms(h_ref[:, MLA_QR:MLA_QR + MLA_KVR])[0] * gkv_ref[...]).astype(BF16)
        kr_ref[...] = _rope(h_ref[:, MLA_QR + MLA_KVR:], c_ref[...], s_ref[...]).astype(BF16)

    def tok(w):
        return pl.BlockSpec((tm, w), lambda i: (i, 0))

    def vec(w):
        return pl.BlockSpec((1, w), lambda i: (0, 0))

    return pl.pallas_call(
        body, name=name,
        out_shape=[jax.ShapeDtypeStruct((t, MLA_QR), BF16), jax.ShapeDtypeStruct((t, MLA_KVR), BF16),
                   jax.ShapeDtypeStruct((t, LANES), BF16)],
        grid=(t // tm,),
        in_specs=[tok(h_in.shape[1]), vec(MLA_QR), vec(MLA_KVR), tok(LANES), tok(LANES)],
        out_specs=[tok(MLA_QR), tok(MLA_KVR), tok(LANES)],
        compiler_params=_params("arbitrary"),
    )(h_in, g_q, g_kv, cos, sin_signed)


def mla_latents_backward(h_in, dcq, dckv, dkr, g_q, g_kv, cos, sin_signed, name):
    t, w = h_in.shape
    tm = _tile(t, 512, 8)

    def body(h_ref, dcq_ref, dckv_ref, dkr_ref, gq_ref, gkv_ref, c_ref, s_ref, dh_ref, dgq_ref, dgkv_ref):
        @pl.when(pl.program_id(0) == 0)
        def _():
            dgq_ref[...] = jnp.zeros_like(dgq_ref)
            dgkv_ref[...] = jnp.zeros_like(dgkv_ref)

        def rms_bwd(x, dc, g_ref, dg_ref):
            xn, rinv = _rms(x)
            dg_ref[...] += jnp.sum(dc * xn, axis=0, keepdims=True)
            dxn = dc * g_ref[...]
            return rinv * (dxn - xn * jnp.mean(dxn * xn, axis=-1, keepdims=True))

        dq = rms_bwd(h_ref[:, 0:MLA_QR], dcq_ref[...], gq_ref, dgq_ref)
        dkv = rms_bwd(h_ref[:, MLA_QR:MLA_QR + MLA_KVR], dckv_ref[...], gkv_ref, dgkv_ref)
        dr = _rope_transposed(dkr_ref[...], c_ref[...], s_ref[...])
        dh_ref[...] = jnp.concatenate([dq, dkv, dr], axis=1).astype(BF16)

    def tok(ww):
        return pl.BlockSpec((tm, ww), lambda i: (i, 0))

    def vec(ww):
        return pl.BlockSpec((1, ww), lambda i: (0, 0))

    return pl.pallas_call(
        body, name=name,
        out_shape=[jax.ShapeDtypeStruct((t, w), BF16), jax.ShapeDtypeStruct((1, MLA_QR), F32),
                   jax.ShapeDtypeStruct((1, MLA_KVR), F32)],
        grid=(t // tm,),
        in_specs=[tok(w), tok(MLA_QR), tok(MLA_KVR), tok(LANES), vec(MLA_QR), vec(MLA_KVR), tok(LANES), tok(LANES)],
        out_specs=[tok(w), vec(MLA_QR), vec(MLA_KVR)],
        compiler_params=_params("arbitrary"),
    )(h_in, dcq, dckv, dkr, g_q, g_kv, cos, sin_signed)


def _tri(n, lower):
    r = lax.broadcasted_iota(jnp.int32, (n, n), 0)
    c = lax.broadcasted_iota(jnp.int32, (n, n), 1)
    return jnp.where(r >= c if lower else r <= c, 1.0, 0.0).astype(F32)


def _dot_exact(tri, v):
    hi = v.astype(BF16)
    mid = (v - hi.astype(F32)).astype(BF16)
    lo = (v - hi.astype(F32) - mid.astype(F32)).astype(BF16)
    t = tri.astype(BF16)
    return _nn(t, hi) + _nn(t, mid) + _nn(t, lo)


def fox_gate_forward(z, b_f, bl, name):
    t = z.shape[0]
    s = t // bl
    ch = LANES
    n_ch = s // ch

    def body(z_ref, b_ref, f_ref, fs_ref):
        tri = _tri(ch, True)
        carry = jnp.zeros((1, LANES), F32)
        for k in range(n_ch):
            x = z_ref[k * ch:(k + 1) * ch, :] + b_ref[...]
            logf = jnp.minimum(x, 0.0) - jnp.log(1.0 + jnp.exp(-jnp.abs(x)))
            cs = _dot_exact(tri, logf) + carry
            carry = cs[ch - 1:ch, :]
            f_ref[k * ch:(k + 1) * ch, :] = cs
            for h in range(FOX_HEADS):
                fs_ref[h, k * ch:(k + 1) * ch, :] = jnp.broadcast_to(cs[:, h:h + 1], (ch, LANES))

    return pl.pallas_call(
        body, name=name,
        out_shape=[jax.ShapeDtypeStruct((t, LANES), F32), jax.ShapeDtypeStruct((FOX_HEADS, t, LANES), F32)],
        grid=(bl,),
        in_specs=[pl.BlockSpec((s, LANES), lambda b: (b, 0)), pl.BlockSpec((1, LANES), lambda b: (0, 0))],
        out_specs=[pl.BlockSpec((s, LANES), lambda b: (b, 0)),
                   pl.BlockSpec((FOX_HEADS, s, LANES), lambda b: (0, b, 0))],
        compiler_params=_params("arbitrary"),
    )(z, b_f)


def fox_gate_backward(z, b_f, df, bl, name):
    t = z.shape[0]
    s = t // bl
    ch = LANES
    n_ch = s // ch

    def body(z_ref, b_ref, df_ref, dz_ref, db_ref):
        @pl.when(pl.program_id(0) == 0)
        def _():
            db_ref[...] = jnp.zeros_like(db_ref)

        tri = _tri(ch, False)
        carry = jnp.zeros((1, LANES), F32)
        for k in reversed(range(n_ch)):
            cs = _dot_exact(tri, df_ref[k * ch:(k + 1) * ch, :]) + carry
            carry = cs[0:1, :]
            x = z_ref[k * ch:(k + 1) * ch, :] + b_ref[...]
            dz = cs * (1.0 - jax.nn.sigmoid(x))
            dz_ref[k * ch:(k + 1) * ch, :] = dz
            db_ref[...] += jnp.sum(dz, axis=0, keepdims=True)

    tok = pl.BlockSpec((s, LANES), lambda b: (b, 0))
    vec = pl.BlockSpec((1, LANES), lambda b: (0, 0))
    return pl.pallas_call(
        body, name=name,
        out_shape=[jax.ShapeDtypeStruct((t, LANES), F32), jax.ShapeDtypeStruct((1, LANES), F32)],
        grid=(bl,), in_specs=[tok, vec, tok], out_specs=[tok, vec],
        compiler_params=_params("arbitrary"),
    )(z, b_f, df)


NEG_INF = float("-inf")


def _attn_tiles(s):
    return _tile(s, 512, 8)


def attention_forward(kind, ops, bl, scale, name):
    fox = kind == "fox"
    if fox:
        assert math.frexp(scale)[0] == 0.5, "the FoX scale is folded into bf16 queries: it must be a power of two"
        qkv, fq, fk = ops
        t = qkv.shape[1]
        n_pair = FOX_HEADS // 2
    else:
        qn, qr, kn, kr, v = ops
        t = qn.shape[1]
        n_pair = MLA_HEADS // 2
    s = t // bl
    tq = _attn_tiles(s)
    nq = s // tq
    half = LANES // 2

    def body(*refs):
        if fox:
            q_ref, k_ref, v_ref, fq_ref, fk_ref, o_ref, lse_ref, o32_ref = refs
        else:
            qn_ref, qr_ref, kn_ref, kr_ref, v_ref, o_ref, lse_ref = refs
        i = pl.program_id(2)
        row = lax.broadcasted_iota(jnp.int32, (tq, tq), 0)
        col = lax.broadcasted_iota(jnp.int32, (tq, tq), 1)
        heads = []
        for e in range(2):
            sl = slice(e * half, (e + 1) * half)
            if fox:
                heads.append((sl, q_ref[0, :, sl] * jnp.asarray(scale, BF16), None))
            else:
                heads.append((sl, qn_ref[e], qr_ref[0, :, sl]))
        dv = half if fox else LANES

        def wide(stat):
            return jnp.concatenate([stat] * (tq // LANES), axis=1)

        def step(j, carry, masked):
            rows = pl.ds(pl.multiple_of(j * tq, tq), tq)
            new = []
            for e, (sl, qa, qb) in enumerate(heads):
                m, l, acc = carry[e]
                if fox:
                    sc = _nt(qa, k_ref[0, rows, sl]) + wide(fq_ref[e]) - fk_ref[0, j, e:e + 1, :]
                    vv = v_ref[0, rows, sl]
                else:
                    sc = (_nt(qa, kn_ref[e, rows, :]) + _nt(qb, kr_ref[rows, 0:half])) * scale
                    vv = v_ref[e, rows, :]
                if masked:
                    sc = jnp.where(row >= col, sc, NEG_INF)
                m_new = jnp.maximum(m, jnp.max(sc, axis=1, keepdims=True))
                p = jnp.exp(sc - m_new)
                a = jnp.exp(m - m_new)
                l = a * l + jnp.sum(p, axis=1, keepdims=True)
                p_hi = p.astype(BF16)
                acc = a * acc + _nn(p_hi, vv)
                if fox:
                    acc = acc + _nn((p - p_hi.astype(F32)).astype(BF16), vv)
                new.append((m_new, l, acc))
            return tuple(new)

        init = (jnp.full((tq, 1), NEG_INF, F32), jnp.zeros((tq, 1), F32), jnp.zeros((tq, dv), F32))
        carry = step(i, (init, init), True)
        carry = lax.fori_loop(0, i, lambda j, c: step(j, c, False), carry)
        outs = [acc / l for _, l, acc in carry]
        for e, (m, l, _) in enumerate(carry):
            lse_ref[e] = jnp.broadcast_to(m + jnp.log(l), (tq, LANES))
        if fox:
            o32 = jnp.concatenate(outs, axis=1)
            o32_ref[0] = o32
            o_ref[0] = o32.astype(BF16)
        else:
            o_ref[0] = outs[0].astype(BF16)
            o_ref[1] = outs[1].astype(BF16)

    def q_idx(b, g, i):
        return (g, b * nq + i, 0)

    if fox:
        nk = fk.shape[1]
        in_specs = [pl.BlockSpec((1, tq, LANES), q_idx),
                    pl.BlockSpec((1, s, LANES), lambda b, g, i: (n_pair + g, b, 0)),
                    pl.BlockSpec((1, s, LANES), lambda b, g, i: (2 * n_pair + g, b, 0)),
                    pl.BlockSpec((2, tq, LANES), q_idx),
                    pl.BlockSpec((1, nk, 8, tq), lambda b, g, i: (b * n_pair + g, 0, 0, 0))]
        args = [qkv, qkv, qkv, fq, fk]
        o_spec = pl.BlockSpec((1, tq, LANES), q_idx)
    else:
        in_specs = [pl.BlockSpec((2, tq, LANES), q_idx),
                    pl.BlockSpec((1, tq, LANES), q_idx),
                    pl.BlockSpec((2, s, LANES), lambda b, g, i: (g, b, 0)),
                    pl.BlockSpec((s, LANES), lambda b, g, i: (b, 0)),
                    pl.BlockSpec((2, s, LANES), lambda b, g, i: (g, b, 0))]
        args = [qn, qr, kn, kr, v]
        o_spec = pl.BlockSpec((2, tq, LANES), q_idx)
    out_shape = [jax.ShapeDtypeStruct((8, t, LANES), BF16), jax.ShapeDtypeStruct((2 * n_pair, t, LANES), F32)]
    out_specs = [o_spec, pl.BlockSpec((2, tq, LANES), q_idx)]
    if fox:
        out_shape.append(jax.ShapeDtypeStruct((8, t, LANES), F32))
        out_specs.append(o_spec)
    outs = pl.pallas_call(
        body, name=name, out_shape=out_shape, grid=(bl, n_pair, nq), in_specs=in_specs, out_specs=out_specs,
        compiler_params=_params("arbitrary", "arbitrary", "arbitrary"),
    )(*args)
    return (outs[0], outs[1], outs[2] if fox else outs[0])


def attention_backward(kind, ops, o, do, lse, bl, scale, name):
    fox = kind == "fox"
    if fox:
        qkv, fq, fk = ops
        t = qkv.shape[1]
        n_pair = FOX_HEADS // 2
    else:
        qn, qr, kn, kr, v = ops
        t = qn.shape[1]
        n_pair = MLA_HEADS // 2
    s = t // bl
    tq = _attn_tiles(s)
    nq = s // tq
    half = LANES // 2

    def body(*refs):
        if fox:
            (q_ref, k_ref, v_ref, fq_ref, fk_ref, o_ref, do_ref, lse_ref,
             dq_ref, dk_ref, dv_ref, dfk_ref, delta_scr, qt_scr, dot_scr) = refs
        else:
            (qn_ref, qr_ref, kn_ref, kr_ref, v_ref, o_ref, do_ref, lse_ref,
             dqn_ref, dqr_ref, dkn_ref, dv_ref, dkr_ref, delta_scr, qt_scr, qrt_scr, dot_scr) = refs
        g, j = pl.program_id(1), pl.program_id(2)
        row = lax.broadcasted_iota(jnp.int32, (tq, tq), 0)
        col = lax.broadcasted_iota(jnp.int32, (tq, tq), 1)
        krows = pl.ds(pl.multiple_of(j * tq, tq), tq)

        def transposed(v):
            return v.astype(F32).T.astype(BF16)

        def wide(stat):
            return jnp.concatenate([stat] * (tq // LANES), axis=1)

        @pl.when(j == 0)
        def _():
            if fox:
                dq_ref[...] = jnp.zeros_like(dq_ref)
            else:
                dqn_ref[...] = jnp.zeros_like(dqn_ref)
                dqr_ref[...] = jnp.zeros_like(dqr_ref)
            for ii in range(nq):
                rws = slice(ii * tq, (ii + 1) * tq)
                deltas = []
                if fox:
                    prod = do_ref[0, rws, :].astype(F32) * o_ref[0, rws, :].astype(F32)
                    for e in range(2):
                        deltas.append(jnp.sum(prod[:, e * half:(e + 1) * half], axis=1, keepdims=True))
                    qt_scr[ii] = transposed(q_ref[0, rws, :])
                    dot_scr[ii] = transposed(do_ref[0, rws, :])
                else:
                    for e in range(2):
                        prod = do_ref[e, rws, :].astype(F32) * o_ref[e, rws, :].astype(F32)
                        deltas.append(jnp.sum(prod, axis=1, keepdims=True))
                        qt_scr[e, ii] = transposed(qn_ref[e, rws, :])
                        dot_scr[e, ii] = transposed(do_ref[e, rws, :])
                    qrt_scr[ii] = transposed(qr_ref[0, rws, :])
                for e in range(2):
                    delta_scr[e, rws, :] = jnp.broadcast_to(deltas[e], (tq, LANES))

        if fox:
            dfk_ref[...] = jnp.zeros_like(dfk_ref)
        else:
            @pl.when(jnp.logical_and(g == 0, j == 0))
            def _():
                dkr_ref[...] = jnp.zeros_like(dkr_ref)

        heads = []
        for e in range(2):
            sl = slice(e * half, (e + 1) * half)
            if fox:
                heads.append((sl, k_ref[0, :, sl], v_ref[0, :, sl], fk_ref[0, 0, e:e + 1, :]))
            else:
                heads.append((sl, kn_ref[e], v_ref[e], kr_ref[krows, 0:half]))
        dk_w = dv_w = half if fox else LANES

        def step(i, carry, masked):
            rows = pl.ds(pl.multiple_of(i * tq, tq), tq)
            new = []
            for e, (sl, k_e, v_e, x_e) in enumerate(heads):
                dk_acc, dv_acc, last = carry[e]
                if fox:
                    do_i = do_ref[0, rows, sl]
                    sc = _nt(q_ref[0, rows, sl], k_e) * scale + wide(fq_ref[e, rows, :]) - x_e
                else:
                    do_i = do_ref[e, rows, :]
                    sc = (_nt(qn_ref[e, rows, :], k_e) + _nt(qr_ref[0, rows, sl], x_e)) * scale
                if masked:
                    sc = jnp.where(row >= col, sc, NEG_INF)
                p = jnp.exp(sc - wide(lse_ref[e, rows, :]))
                dp = _nt(do_i, v_e)
                ds = p * (dp - wide(delta_scr[e, rows, :]))
                dsb = (ds * scale).astype(BF16)
                if fox:
                    fsl = slice(e * half, (e + 1) * half)
                    dv_acc = dv_acc + _nn(dot_scr[i, fsl, :], p.astype(BF16))
                    dk_acc = dk_acc + _nn(qt_scr[i, fsl, :], dsb)
                    dq_ref[0, rows, sl] += _nn(dsb, k_e)
                    last = last - jnp.sum(ds, axis=0, keepdims=True)
                else:
                    dv_acc = dv_acc + _nn(dot_scr[e, i], p.astype(BF16))
                    dk_acc = dk_acc + _nn(qt_scr[e, i], dsb)
                    dqn_ref[e, rows, :] += _nn(dsb, k_e)
                    dqr_ref[0, rows, sl] += _nn(dsb, x_e)
                    last = last + _nn(qrt_scr[i, e * half:(e + 1) * half, :], dsb)
                new.append((dk_acc, dv_acc, last))
            return tuple(new)

        last0 = jnp.zeros((1, tq), F32) if fox else jnp.zeros((half, tq), F32)
        init = (jnp.zeros((dk_w, tq), F32), jnp.zeros((dv_w, tq), F32), last0)
        carry = step(j, (init, init), True)
        carry = lax.fori_loop(j + 1, nq, lambda i, c: step(i, c, False), carry)
        if fox:
            for e in range(2):
                dfk_ref[0, 0, e:e + 1, :] = carry[e][2]
            dk_ref[0] = jnp.concatenate([carry[0][0], carry[1][0]], axis=0).T.astype(BF16)
            dv_ref[0] = jnp.concatenate([carry[0][1], carry[1][1]], axis=0).T.astype(BF16)
        else:
            for e in range(2):
                dkn_ref[e] = carry[e][0].T.astype(BF16)
                dv_ref[e] = carry[e][1].T.astype(BF16)
            dkr_t = carry[0][2] + carry[1][2]
            dkr_ref[krows, :] += jnp.concatenate([dkr_t, jnp.zeros_like(dkr_t)], axis=0).T

    def whole(b, g, j):
        return (g, b, 0)

    def kblk(b, g, j):
        return (g, b * nq + j, 0)

    if fox:
        in_specs = [pl.BlockSpec((1, s, LANES), whole),
                    pl.BlockSpec((1, tq, LANES), lambda b, g, j: (n_pair + g, b * nq + j, 0)),
                    pl.BlockSpec((1, tq, LANES), lambda b, g, j: (2 * n_pair + g, b * nq + j, 0)),
                    pl.BlockSpec((2, s, LANES), whole),
                    pl.BlockSpec((1, 1, 8, tq), lambda b, g, j: (b * n_pair + g, j, 0, 0)),
                    pl.BlockSpec((1, s, LANES), whole), pl.BlockSpec((1, s, LANES), whole),
                    pl.BlockSpec((2, s, LANES), whole)]
        args = [qkv, qkv, qkv, fq, fk, o, do, lse]
        out_shape = [jax.ShapeDtypeStruct((8, t, LANES), F32), jax.ShapeDtypeStruct((8, t, LANES), BF16),
                     jax.ShapeDtypeStruct((8, t, LANES), BF16), jax.ShapeDtypeStruct(fk.shape, F32)]
        out_specs = [pl.BlockSpec((1, s, LANES), whole), pl.BlockSpec((1, tq, LANES), kblk),
                     pl.BlockSpec((1, tq, LANES), kblk),
                     pl.BlockSpec((1, 1, 8, tq), lambda b, g, j: (b * n_pair + g, j, 0, 0))]
    else:
        pair = pl.BlockSpec((2, s, LANES), whole)
        pair_k = pl.BlockSpec((2, tq, LANES), kblk)
        in_specs = [pair, pl.BlockSpec((1, s, LANES), whole), pair_k,
                    pl.BlockSpec((s, LANES), lambda b, g, j: (b, 0)), pair_k,
                    pair, pair, pair]
        args = [qn, qr, kn, kr, v, o, do, lse]
        out_shape = [jax.ShapeDtypeStruct((8, t, LANES), F32), jax.ShapeDtypeStruct((4, t, LANES), F32),
                     jax.ShapeDtypeStruct((8, t, LANES), BF16), jax.ShapeDtypeStruct((8, t, LANES), BF16),
                     jax.ShapeDtypeStruct((t, LANES), F32)]
        out_specs = [pair, pl.BlockSpec((1, s, LANES), whole), pair_k, pair_k,
                     pl.BlockSpec((s, LANES), lambda b, g, j: (b, 0))]
    t_blocks = pltpu.VMEM((nq, LANES, tq), BF16)
    t_pairs = pltpu.VMEM((2, nq, LANES, tq), BF16)
    scratch = [pltpu.VMEM((2, s, LANES), F32)] + ([t_blocks, t_blocks] if fox else [t_pairs, t_blocks, t_pairs])
    return pl.pallas_call(
        body, name=name, out_shape=out_shape, grid=(bl, n_pair, nq), in_specs=in_specs, out_specs=out_specs,
        scratch_shapes=scratch, compiler_params=_params("arbitrary", "arbitrary", "arbitrary"),
    )(*args)


def adamw(w, g, m, v, name):
    shape = w.shape
    c = shape[-1]
    r = w.size // c
    tr = _tile(r, 512, 8)

    def body(w_ref, g_ref, m_ref, v_ref, d_ref, nm_ref, nv_ref):
        gv = g_ref[...]
        m2 = ADAM_B1 * m_ref[...] + (1.0 - ADAM_B1) * gv
        v2 = ADAM_B2 * v_ref[...] + (1.0 - ADAM_B2) * (gv * gv)
        m_hat = m2 / (1.0 - ADAM_B1 ** ADAM_STEP)
        v_hat = v2 / (1.0 - ADAM_B2 ** ADAM_STEP)
        d_ref[...] = -ADAM_LR * (m_hat / (jnp.sqrt(v_hat) + ADAM_EPS) + ADAM_WD * w_ref[...])
        nm_ref[...] = m2
        nv_ref[...] = v2

    spec = pl.BlockSpec((tr, c), lambda i: (i, 0))
    outs = pl.pallas_call(
        body, name=name, out_shape=[jax.ShapeDtypeStruct((r, c), F32)] * 3, grid=(r // tr,),
        in_specs=[spec] * 4, out_specs=[spec] * 3, compiler_params=_params("arbitrary"),
    )(*(a.reshape(r, c) for a in (w, g, m, v)))
    return tuple(a.reshape(shape) for a in outs)


PACK_COLS = 1024


def _pack_rows(a):
    return a.reshape(-1, PACK_COLS)


def kernel(x, c, positions, mla_w_in, mla_g_q, mla_w_uq, mla_g_kv, mla_w_uk, mla_w_uv, mla_w_o, fox_w_in, fox_b_f, fox_w_o, ada_w, ada_b, ffn_w_gate, ffn_w_up, ffn_w_down, ln_g, ln_b, loss_target, m_mla_w_in, m_mla_g_q, m_mla_w_uq, m_mla_g_kv, m_mla_w_uk, m_mla_w_uv, m_mla_w_o, m_fox_w_in, m_fox_b_f, m_fox_w_o, m_ada_w, m_ada_b, m_ffn_w_gate, m_ffn_w_up, m_ffn_w_down, m_ln_g, m_ln_b, v_mla_w_in, v_mla_g_q, v_mla_w_uq, v_mla_g_kv, v_mla_w_uk, v_mla_w_uv, v_mla_w_o, v_fox_w_in, v_fox_b_f, v_fox_w_o, v_ada_w, v_ada_b, v_ffn_w_gate, v_ffn_w_up, v_ffn_w_down, v_ln_g, v_ln_b):
    bl, s, d = x.shape
    t = bl * s
    ff = ffn_w_gate.shape[-1] * N_DEV
    dev = 4 * lax.axis_index("x") + 2 * lax.axis_index("y") + lax.axis_index("c")
    ada_cols = ada_w.shape[-1]
    fox_in = fox_w_in.shape[-1] * N_DEV
    mla_in = mla_w_in.shape[-1]
    mla_in_pad = mla_in + (-mla_in) % LANES

    def t_last(a):
        return jnp.swapaxes(a, -1, -2)

    local = {
        "mla_w_in": mla_w_in[0],
        "mla_w_uq": t_last(mla_w_uq[0]),
        "mla_w_uk": t_last(mla_w_uk[0]),
        "mla_w_uv": t_last(mla_w_uv[0]),
        "mla_w_o": mla_w_o[0],
        "fox_w_in": t_last(fox_w_in[0]),
        "fox_w_o": fox_w_o[0],
    }
    for i in range(DEPTH):
        local.update({f"gate{i}": t_last(ffn_w_gate[i]), f"up{i}": t_last(ffn_w_up[i]), f"down{i}": ffn_w_down[i]})
    groups = [["mla_w_in", "mla_w_uq", "mla_w_uk", "mla_w_uv", "mla_w_o"],
              ["gate0", "up0", "down0"],
              ["fox_w_in", "fox_w_o"],
              ["gate1", "up1", "down1"]]
    offsets, rows_of, slot_of, group_of = {}, {}, {}, {}
    group_rows = []
    for gi, names in enumerate(groups):
        rows = 0
        for nm in names:
            rows_of[nm] = local[nm].size // PACK_COLS
            slot_of[nm] = rows_of[nm] + (-rows_of[nm]) % 16
            offsets[nm] = rows
            group_of[nm] = gi
            rows += slot_of[nm]
        group_rows.append(rows)

    def slot(nm, rows):
        pad = [(0, 0)] * rows.ndim
        pad[-2] = (0, slot_of[nm] - rows_of[nm])
        return jnp.pad(rows, pad)

    def landing(block):
        land = lax.empty((N_DEV,) + block.shape, block.dtype)
        return lax.dynamic_update_slice(land, block[None], (dev, 0, 0))

    packed = [jnp.concatenate([slot(nm, _pack_rows(local[nm]).astype(BF16)) for nm in names], axis=0)
              for names in groups]
    gathered = [all_gather(packed[0], "gather_mla_weights")] + [None] * (len(groups) - 1)
    gather_started = [None] * len(groups)

    def depart(gi, after):
        block = lax.optimization_barrier((packed[gi], after))[0]
        gather_started[gi] = exchange_start(block, landing(block), f"gather_group{gi}_start", False)
        return gather_started[gi][4]

    def full(nm, cols):
        blk = gathered[group_of[nm]][:, offsets[nm]:offsets[nm] + rows_of[nm], :]
        return blk.reshape(-1, cols)

    w_in = jnp.pad(full("mla_w_in", mla_in), ((0, 0), (0, mla_in_pad - mla_in)))
    wt_uq = full("mla_w_uq", MLA_QR).reshape(MLA_HEADS, MLA_NOPE + MLA_ROPE, MLA_QR)
    wt_uq_n = wt_uq[:, :MLA_NOPE].reshape(MLA_HEADS * MLA_NOPE, MLA_QR)
    wt_uq_r = wt_uq[:, MLA_NOPE:].reshape(MLA_HEADS * MLA_ROPE, MLA_QR)
    wt_uk = full("mla_w_uk", MLA_KVR)
    wt_uv = full("mla_w_uv", MLA_KVR)
    w_mo = full("mla_w_o", d)
    wt_gate, wt_up, w_down = [None] * DEPTH, [None] * DEPTH, [None] * DEPTH

    def arrive(gi, after):
        gathered[gi] = exchange_wait(gather_started[gi], after, f"gather_group{gi}_wait", False)
        if gi + 1 < len(groups):
            gathered[gi] = lax.optimization_barrier((gathered[gi], depart(gi + 1, gathered[gi])))[0]
        for i in range(DEPTH):
            if group_of[f"gate{i}"] == gi:
                wt_gate[i], wt_up[i], w_down[i] = full(f"gate{i}", d), full(f"up{i}", d), full(f"down{i}", d)

    small = jnp.concatenate([c.reshape(-1, LANES), ln_g.reshape(-1, LANES), ln_b.reshape(-1, LANES)], axis=0)
    small_rows = small.shape[0]
    small = jnp.pad(small, ((0, (-small_rows) % 8), (0, 0)))
    small_all = all_gather(small, "gather_small")
    c_rows = bl * d // LANES
    c_all = small_all[:, :c_rows].reshape(N_DEV * bl, d)
    n_ln = DEPTH * 2
    ln_g_all = small_all[:, c_rows:c_rows + n_ln, :].transpose(1, 0, 2).reshape(DEPTH, 2, 1, d)
    ln_b_all = small_all[:, c_rows + n_ln:c_rows + 2 * n_ln, :].transpose(1, 0, 2).reshape(DEPTH, 2, 1, d)

    c_act = silu_rows(c_all, "silu_c")
    ada_b_loc = lax.dynamic_slice_in_dim(ada_b, dev * ada_cols, ada_cols, axis=1)
    mod_cols = [mm([(c_act, ada_w[i])], trans_b=False, out_dtype=F32, name=f"ada_fwd{i}", bias=ada_b_loc[i][None, :])
                for i in range(DEPTH)]
    mod_all = all_gather(jnp.concatenate(mod_cols, axis=0), "gather_mod")
    mod_all = mod_all.reshape(N_DEV, DEPTH, N_DEV * bl, ada_cols).transpose(1, 2, 0, 3).reshape(DEPTH, N_DEV * bl, 6 * d)
    mod_mine = lax.dynamic_slice_in_dim(mod_all, dev * bl, bl, axis=1)
    mods = [mod_mine[i].reshape(bl * 6, 1, d) for i in range(DEPTH)]
    mods[0] = mods[0] + depart(1, (mod_mine, gathered[0]))[0, 0]

    half_r = MLA_ROPE // 2
    inv_freq = ROPE_THETA ** (-jnp.arange(half_r, dtype=F32) / half_r)
    inv_freq = jnp.tile(inv_freq, LANES // half_r)[None, :]
    sign = jnp.tile(jnp.concatenate([-jnp.ones((half_r,), F32), jnp.ones((half_r,), F32)]), LANES // MLA_ROPE)[None, :]
    cos_t, sin_t = rope_tables(positions.astype(F32).reshape(t, 1), inv_freq, sign, "rope_tables")

    x2d = x.reshape(t, d)
    g_q, g_kv = mla_g_q.reshape(1, MLA_QR), mla_g_kv.reshape(1, MLA_KVR)
    b_f = jnp.pad(fox_b_f.reshape(1, FOX_HEADS), ((0, 0), (0, LANES - FOX_HEADS)))
    mla_scale = (MLA_NOPE + MLA_ROPE) ** -0.5
    fox_scale = FOX_HD ** -0.5
    tq = _attn_tiles(s)
    nk = s // tq

    saved = []
    u = modulate(x2d, mods[0], 0, 1, bl, "modulate0")
    xin = x2d
    for i in range(DEPTH):
        sv = {"u": u, "x_in": xin}
        if i % 2 == 0:
            h_in = mm([(u, w_in)], trans_b=False, out_dtype=F32, name=f"mla_in{i}")
            c_q, c_kv, k_r = mla_latents_forward(h_in, g_q, g_kv, cos_t, sin_t, f"mla_latents{i}")
            q_n = mm([(c_q, wt_uq_n)], trans_b=True, out_dtype=BF16, out_slab=True, name=f"mla_qn{i}")
            q_r_raw = mm([(c_q, wt_uq_r)], trans_b=True, out_dtype=F32, out_slab=True, name=f"mla_qr{i}")
            q_r = rope_slabs(q_r_raw, cos_t, sin_t, BF16, f"mla_qrope{i}")
            k_n = mm([(c_kv, wt_uk)], trans_b=True, out_dtype=BF16, out_slab=True, name=f"mla_kn{i}")
            v_m = mm([(c_kv, wt_uv)], trans_b=True, out_dtype=BF16, out_slab=True, name=f"mla_v{i}")
            ops = (q_n, q_r, k_n, k_r, v_m)
            o, lse, o_delta = attention_forward("mla", ops, bl, mla_scale, f"mla_attn{i}")
            y = mm([(o, w_mo)], trans_b=False, out_dtype=F32, name=f"mla_out{i}")
            sv.update(h_in=h_in, c_q=c_q, c_kv=c_kv, ops=ops, o=o, lse=lse, o_delta=o_delta)
        else:
            arrive(2, u)
            wt_fox = full("fox_w_in", d)
            wt_qkv = wt_fox[:3 * d]
            wt_f = jnp.pad(wt_fox[3 * d:], ((0, LANES - FOX_HEADS), (0, 0)))
            w_fo = full("fox_w_o", d)
            qkv = mm([(u, wt_qkv)], trans_b=True, out_dtype=BF16, out_slab=True, name=f"fox_qkv{i}")
            z = mm([(u, wt_f)], trans_b=True, out_dtype=F32, name=f"fox_z{i}")
            f_tok, f_q = fox_gate_forward(z, b_f, bl, f"fox_gate{i}")
            f_k = f_tok[:, :FOX_HEADS].reshape(bl, nk, tq, FOX_HEADS // 2, 2).transpose(0, 3, 1, 4, 2)
            f_k = jnp.pad(f_k.reshape(bl * FOX_HEADS // 2, nk, 2, tq), ((0, 0), (0, 0), (0, 6), (0, 0)))
            ops = (qkv, f_q, f_k)
            o, lse, o_delta = attention_forward("fox", ops, bl, fox_scale, f"fox_attn{i}")
            y = mm([(o, w_fo)], trans_b=False, out_dtype=F32, name=f"fox_out{i}")
            sv.update(z=z, ops=ops, o=o, lse=lse, o_delta=o_delta)
        x1, r1, u2 = residual_layer_norm(xin, y, mods[i], 2, ln_g_all[i, 0], ln_b_all[i, 0], bl, f"ln_mix{i}",
                                         next_mod=(3, 4))
        if wt_gate[i] is None:
            arrive(group_of[f"gate{i}"], u2)
        a, bb, h = swiglu_in(u2, wt_gate[i], wt_up[i], f"ffn_in{i}")
        y2 = mm([(h, w_down[i])], trans_b=False, out_dtype=F32, name=f"ffn_down{i}")
        sv.update(y=y, r1=r1, u2=u2, a=a, bb=bb, h=h, y2=y2)
        if i + 1 < DEPTH:
            xin, r2, u = residual_layer_norm(x1, y2, mods[i], 5, ln_g_all[i, 1], ln_b_all[i, 1], bl, f"ln_ffn{i}",
                                             next_mod=(0, 1, mods[i + 1]))
        else:
            xin, r2 = residual_layer_norm(x1, y2, mods[i], 5, ln_g_all[i, 1], ln_b_all[i, 1], bl, f"ln_ffn{i}")
        sv.update(r2=r2)
        saved.append(sv)

    loss_cols, d_x = loss_head(xin, loss_target.reshape(t, d), "loss_head")

    grads_full = {}
    wgrad = functools.partial(mm_tn, out_dtype=BF16)
    dmod = [[None] * 6 for _ in range(DEPTH)]
    dg_ln = [[None, None] for _ in range(DEPTH)]
    db_ln = [[None, None] for _ in range(DEPTH)]
    dg_q = dg_kv = db_f = None
    d_a, du = d_x, None
    scatter_started = [None] * len(groups)

    def scatter_start(gi):
        g = jnp.concatenate(
            [slot(nm, grads_full[nm].reshape(N_DEV, rows_of[nm], PACK_COLS).astype(BF16)) for nm in groups[gi]], axis=1)
        own = lax.dynamic_index_in_dim(g, dev, 0, keepdims=False)
        scatter_started[gi] = exchange_start(g, landing(own), f"scatter_group{gi}_start", True)

    ln_g_bwd = [[ln_g_all[i, k] for k in range(2)] for i in range(DEPTH)]
    for i in reversed(range(DEPTH)):
        sv = saved[i]
        if i + 1 < DEPTH:
            gi = group_of["fox_w_in"]
            scatter_start(gi)
            ln_g_bwd[i][1] = after_token(ln_g_bwd[i][1], scatter_started[gi])
        ln2 = (sv["r2"], sv["y2"], ln_g_bwd[i][1], ln_b_all[i, 1], (mods[i], 5))
        if du is None:
            bw = sublayer_backward(d_a, bl, f"bwd_ln_ffn{i}", ln=ln2)
        else:
            bw = sublayer_backward(d_a, bl, f"bwd_ln_ffn{i}", du=du, scale=(mods[i + 1], 1), ln=ln2)
            dmod[i + 1][0], dmod[i + 1][1] = bw["dshift"], bw["dscale"]
        dmod[i][5], dg_ln[i][1], db_ln[i][1] = bw["dgate"], bw["dg"], bw["db"]
        dy2 = bw["dy"]
        da, dbb = swiglu_out_backward(dy2, w_down[i], sv["a"], sv["bb"], f"bwd_ffn_act{i}")
        du2 = mm([(da, wt_gate[i]), (dbb, wt_up[i])], trans_b=False, out_dtype=F32, name=f"bwd_ffn_du{i}")
        grads_full[f"down{i}"] = wgrad(sv["h"], dy2, name=f"bwd_w_down{i}")
        grads_full[f"gate{i}"] = wgrad(da, sv["u2"], name=f"bwd_w_gate{i}")
        grads_full[f"up{i}"] = wgrad(dbb, sv["u2"], name=f"bwd_w_up{i}")
        gi = group_of[f"gate{i}"]
        scatter_start(gi)
        ln_g_bwd[i][0] = after_token(ln_g_bwd[i][0], scatter_started[gi])
        bw = sublayer_backward(bw["dx"], bl, f"bwd_ln_mix{i}", du=du2, scale=(mods[i], 4),
                               ln=(sv["r1"], sv["y"], ln_g_bwd[i][0], ln_b_all[i, 0], (mods[i], 2)))
        dmod[i][3], dmod[i][4], dmod[i][2] = bw["dshift"], bw["dscale"], bw["dgate"]
        dg_ln[i][0], db_ln[i][0] = bw["dg"], bw["db"]
        d_a, dy = bw["dx"], bw["dy"]
        o, lse, ops = sv["o"], sv["lse"], sv["ops"]
        if i % 2 == 0:
            do = mm([(dy, w_mo)], trans_b=True, out_dtype=BF16, out_slab=True, name=f"bwd_mla_do{i}")
            grads_full["mla_w_o"] = wgrad(o, dy, name=f"bwd_w_mla_o{i}")
            dqn, dqr, dkn, dvm, dkr = attention_backward("mla", ops, sv["o_delta"], do, lse, bl, mla_scale,
                                                         f"bwd_mla_attn{i}")
            dqr = rope_slabs(dqr, cos_t, sin_t, F32, f"bwd_mla_qrope{i}", transposed=True)
            dcq = mm([(dqn, wt_uq_n), (dqr, wt_uq_r)], trans_b=False, out_dtype=F32, name=f"bwd_mla_dcq{i}")
            dckv = mm([(dkn, wt_uk), (dvm, wt_uv)], trans_b=False, out_dtype=F32, name=f"bwd_mla_dckv{i}")
            d_uq_n = wgrad(dqn, sv["c_q"], name=f"bwd_w_uq_n{i}").reshape(MLA_HEADS, MLA_NOPE, MLA_QR)
            d_uq_r = wgrad(dqr, sv["c_q"], name=f"bwd_w_uq_r{i}").reshape(MLA_HEADS, MLA_ROPE, MLA_QR)
            grads_full["mla_w_uq"] = jnp.concatenate([d_uq_n, d_uq_r], axis=1)
            grads_full["mla_w_uk"] = wgrad(dkn, sv["c_kv"], name=f"bwd_w_uk{i}")
            grads_full["mla_w_uv"] = wgrad(dvm, sv["c_kv"], name=f"bwd_w_uv{i}")
            dh_in, dg_q, dg_kv = mla_latents_backward(sv["h_in"], dcq, dckv, dkr, g_q, g_kv, cos_t, sin_t,
                                                      f"bwd_mla_latents{i}")
            du = mm([(dh_in, w_in)], trans_b=True, out_dtype=F32, name=f"bwd_mla_du{i}")
            grads_full["mla_w_in"] = wgrad(sv["u"], dh_in, name=f"bwd_w_mla_in{i}")[:, :mla_in]
        else:
            do = mm([(dy, w_fo)], trans_b=True, out_dtype=BF16, out_slab=True, name=f"bwd_fox_do{i}")
            grads_full["fox_w_o"] = wgrad(o, dy, name=f"bwd_w_fox_o{i}")
            dq, dk, dvf, dfk = attention_backward("fox", ops, sv["o_delta"], do, lse, bl, fox_scale, f"bwd_fox_attn{i}")
            df = dfk[:, :, :2, :].reshape(bl, FOX_HEADS // 2, nk, 2, tq).transpose(0, 2, 4, 1, 3).reshape(t, FOX_HEADS)
            df = jnp.pad(df, ((0, 0), (0, LANES - FOX_HEADS)))
            dz, db_f = fox_gate_backward(sv["z"], b_f, df, bl, f"bwd_fox_gate{i}")
            du = mm([(dq, wt_fox[0:d]), (dk, wt_fox[d:2 * d]), (dvf, wt_fox[2 * d:3 * d]), (dz, wt_f)],
                    trans_b=False, out_dtype=F32, name=f"bwd_fox_du{i}")
            u_f = sv["u"]
            grads_full["fox_w_in"] = jnp.concatenate(
                [wgrad(dq, u_f, name=f"bwd_w_fox_q{i}"), wgrad(dk, u_f, name=f"bwd_w_fox_k{i}"),
                 wgrad(dvf, u_f, name=f"bwd_w_fox_v{i}"), wgrad(dz, u_f, name=f"bwd_w_fox_f{i}")[:FOX_HEADS]], axis=0)
    bw = sublayer_backward(d_a, bl, "bwd_input", du=du, scale=(mods[0], 1), x_in=x2d)
    dmod[0][0], dmod[0][1] = bw["dshift"], bw["dscale"]
    grad_x = bw["dx"].reshape(bl, s, d)

    dmod_rows = jnp.concatenate([r.reshape(bl, d) for layer in dmod for r in layer], axis=0)
    dmod_rows = dmod_rows.reshape(DEPTH, 6, bl, d).transpose(0, 2, 1, 3)
    n_mod = dmod_rows.size // LANES
    ln_parts = [dg_ln[i][k] for i in range(DEPTH) for k in range(2)] + [db_ln[i][k] for i in range(DEPTH) for k in range(2)]
    small_g = jnp.concatenate([dmod_rows.reshape(-1, LANES), dg_q.reshape(-1, LANES), dg_kv.reshape(-1, LANES), db_f]
                              + [p.reshape(-1, LANES) for p in ln_parts] + [loss_cols.reshape(-1, LANES)], axis=0)
    n_small = small_g.shape[0]
    small_g = jnp.pad(small_g, ((0, (-n_small) % 8), (0, 0)))
    small_g_all = all_gather(small_g, "gather_small_grads")
    small_sum = sum_leading(small_g_all, "sum_small_grads")
    per_seq = DEPTH * 6 * d // LANES
    dmod_all = small_g_all[:, :n_mod].reshape(N_DEV, DEPTH, bl, 6 * d).transpose(1, 0, 2, 3)
    dmod_all = dmod_all.reshape(DEPTH, N_DEV * bl, 6 * d)
    o1 = n_mod
    grad_g_q = small_sum[o1:o1 + MLA_QR // LANES].reshape(1, MLA_QR)
    o1 += MLA_QR // LANES
    grad_g_kv = small_sum[o1:o1 + MLA_KVR // LANES].reshape(1, MLA_KVR)
    o1 += MLA_KVR // LANES
    grad_b_f = small_sum[o1:o1 + 1, :FOX_HEADS]
    o1 += 1
    n_ln_rows = DEPTH * 2 * d // LANES
    grad_ln_g_full = small_sum[o1:o1 + n_ln_rows].reshape(DEPTH, 2, d)
    grad_ln_b_full = small_sum[o1 + n_ln_rows:o1 + 2 * n_ln_rows].reshape(DEPTH, 2, d)
    loss = jnp.sum(small_sum[o1 + 2 * n_ln_rows:o1 + 2 * n_ln_rows + d // LANES])
    shard = d // N_DEV
    grad_ln_g = lax.dynamic_slice_in_dim(grad_ln_g_full, dev * shard, shard, axis=2)
    grad_ln_b = lax.dynamic_slice_in_dim(grad_ln_b_full, dev * shard, shard, axis=2)
    by_seq = small_g_all[:, :n_mod].reshape(N_DEV, DEPTH, bl, 6 * d // LANES, LANES).transpose(0, 2, 1, 3, 4)
    grad_ada_b = sum_leading(by_seq.reshape(N_DEV * bl, per_seq, LANES), "sum_ada_b").reshape(DEPTH, 6 * d)
    dmod_cols = lax.dynamic_slice_in_dim(dmod_all, dev * ada_cols, ada_cols, axis=2)
    grad_ada_w = jnp.stack([mm_tn(c_act, dmod_cols[i], name=f"bwd_w_ada{i}") for i in range(DEPTH)])

    scatter_start(0)
    g_mine = [None] * len(groups)

    def scatter_arrive(gi, after):
        landed = exchange_wait(scatter_started[gi], after, f"scatter_group{gi}_wait", True)
        g_mine[gi] = sum_leading(landed, f"scatter_group{gi}_sum")
        return g_mine[gi]

    after = bw["dx"]
    for gi in reversed(range(1, len(groups))):
        after = scatter_arrive(gi, after)

    def mine(nm, shape):
        return g_mine[group_of[nm]][offsets[nm]:offsets[nm] + rows_of[nm]].reshape(shape)

    def shard_t(nm, a):
        return mine(nm, t_last(a).shape)

    transposed = {"mla_w_uq", "mla_w_uk", "mla_w_uv", "fox_w_in", "ffn_w_gate", "ffn_w_up"}
    grads = {
        "mla_w_in": lambda: mine("mla_w_in", mla_w_in[0].shape)[None],
        "mla_g_q": lambda: grad_g_q,
        "mla_w_uq": lambda: shard_t("mla_w_uq", mla_w_uq[0])[None],
        "mla_g_kv": lambda: grad_g_kv,
        "mla_w_uk": lambda: shard_t("mla_w_uk", mla_w_uk[0])[None],
        "mla_w_uv": lambda: shard_t("mla_w_uv", mla_w_uv[0])[None],
        "mla_w_o": lambda: mine("mla_w_o", mla_w_o[0].shape)[None],
        "fox_w_in": lambda: shard_t("fox_w_in", fox_w_in[0])[None],
        "fox_b_f": lambda: grad_b_f,
        "fox_w_o": lambda: mine("fox_w_o", fox_w_o[0].shape)[None],
        "ada_w": lambda: grad_ada_w,
        "ada_b": lambda: grad_ada_b,
        "ffn_w_gate": lambda: jnp.stack([shard_t(f"gate{i}", ffn_w_gate[i]) for i in range(DEPTH)]),
        "ffn_w_up": lambda: jnp.stack([shard_t(f"up{i}", ffn_w_up[i]) for i in range(DEPTH)]),
        "ffn_w_down": lambda: jnp.stack([mine(f"down{i}", ffn_w_down[i].shape) for i in range(DEPTH)]),
        "ln_g": lambda: grad_ln_g,
        "ln_b": lambda: grad_ln_b,
    }
    weights = dict(mla_w_in=mla_w_in, mla_g_q=mla_g_q, mla_w_uq=mla_w_uq, mla_g_kv=mla_g_kv, mla_w_uk=mla_w_uk,
                   mla_w_uv=mla_w_uv, mla_w_o=mla_w_o, fox_w_in=fox_w_in, fox_b_f=fox_b_f, fox_w_o=fox_w_o,
                   ada_w=ada_w, ada_b=ada_b, ffn_w_gate=ffn_w_gate, ffn_w_up=ffn_w_up, ffn_w_down=ffn_w_down,
                   ln_g=ln_g, ln_b=ln_b)
    first = dict(mla_w_in=m_mla_w_in, mla_g_q=m_mla_g_q, mla_w_uq=m_mla_w_uq, mla_g_kv=m_mla_g_kv, mla_w_uk=m_mla_w_uk,
                 mla_w_uv=m_mla_w_uv, mla_w_o=m_mla_w_o, fox_w_in=m_fox_w_in, fox_b_f=m_fox_b_f, fox_w_o=m_fox_w_o,
                 ada_w=m_ada_w, ada_b=m_ada_b, ffn_w_gate=m_ffn_w_gate, ffn_w_up=m_ffn_w_up, ffn_w_down=m_ffn_w_down,
                 ln_g=m_ln_g, ln_b=m_ln_b)
    second = dict(mla_w_in=v_mla_w_in, mla_g_q=v_mla_g_q, mla_w_uq=v_mla_w_uq, mla_g_kv=v_mla_g_kv, mla_w_uk=v_mla_w_uk,
                  mla_w_uv=v_mla_w_uv, mla_w_o=v_mla_w_o, fox_w_in=v_fox_w_in, fox_b_f=v_fox_b_f, fox_w_o=v_fox_w_o,
                  ada_w=v_ada_w, ada_b=v_ada_b, ffn_w_gate=v_ffn_w_gate, ffn_w_up=v_ffn_w_up, ffn_w_down=v_ffn_w_down,
                  ln_g=v_ln_g, ln_b=v_ln_b)
    order = list(weights)
    last = [nm for nm in order if group_of.get(nm) == 0]
    updated = {}
    for nm in [nm for nm in order if nm not in last] + last:
        if last and nm == last[0]:
            scatter_arrive(0, after)
        lay = t_last if nm in transposed else (lambda a: a)
        w = lay(weights[nm])
        g = grads[nm]().reshape(w.shape)
        delta, new_m, new_v = adamw(w, g, lay(first[nm]), lay(second[nm]), f"adamw_{nm}")
        updated[nm] = (lay(g), lay(delta), lay(new_m), lay(new_v))
        after = new_v
    return (loss, grad_x, *(updated[nm][k] for k in range(4) for nm in order))
```

```python
import functools
import math

import jax
import jax.numpy as jnp
from jax import lax
from jax.experimental import pallas as pl
from jax.experimental.pallas import tpu as pltpu

F32 = jnp.float32
BF16 = jnp.bfloat16
LANES = 128
N_DEV = 8
VMEM_LIMIT_BYTES = 56 * 1024 * 1024

DEPTH = 2
MLA_HEADS = 8
MLA_NOPE = 128
MLA_ROPE = 64
MLA_V = 128
MLA_QR = 256
MLA_KVR = 256
ROPE_THETA = 10000.0
FOX_HEADS = 16
FOX_HD = 64
ALPHA = (2.0 * DEPTH) ** 0.25
NORM_EPS = 1e-5
ADAM_LR = 0.001
ADAM_B1 = 0.9
ADAM_B2 = 0.999
ADAM_EPS = 1e-08
ADAM_WD = 0.01
ADAM_STEP = 10

MESH_AXES = ("x", "y", "c")
MESH = pl.DeviceIdType.MESH


def _params(*sem):
    return pltpu.CompilerParams(dimension_semantics=sem, vmem_limit_bytes=VMEM_LIMIT_BYTES)


def _tile(n, cap, mult=LANES):
    if n <= cap:
        return n
    best = None
    for t in range(mult, cap + 1, mult):
        if n % t == 0:
            best = t
    assert best is not None, (n, cap, mult)
    return best


def _dot(a, b, dims):
    return lax.dot_general(a, b, (dims, ((), ())), preferred_element_type=F32)


def _nn(a, b):
    return _dot(a, b, ((1,), (0,)))


def _nt(a, b):
    return _dot(a, b, ((1,), (1,)))


def _tn(a, b):
    return _dot(a, b, ((0,), (0,)))


def _me():
    return lax.axis_index("x"), lax.axis_index("y"), lax.axis_index("c")


def all_gather(x_loc, name):
    r, c = x_loc.shape

    def body(x_ref, out_ref, send_sems, recv_sems, local_sem):
        x, y, cc = _me()
        me, sibling = (x, y, cc), (x, y, 1 - cc)
        chips = [(1 - x, y), (x, 1 - y), (1 - x, 1 - y)]

        def rows(px, py, pc):
            return out_ref.at[4 * px + 2 * py + pc]

        def copy(k, block, to, src=None):
            return pltpu.make_async_remote_copy(
                src_ref=rows(*block) if src is None else src, dst_ref=rows(*block),
                send_sem=send_sems.at[k], recv_sem=recv_sems.at[k], device_id=to, device_id_type=MESH)

        mine = pltpu.make_async_copy(x_ref, rows(*me), local_sem)
        mine.start()
        first = [copy(0, me, sibling, src=x_ref)]
        first += [copy(1 + j, me, (*chip, cc), src=x_ref) for j, chip in enumerate(chips)]
        for cp in first:
            cp.start()
        passed = [copy(4 + j, (*chip, cc), sibling) for j, chip in enumerate(chips)]
        for j, chip in enumerate(chips):
            copy(1 + j, (*chip, cc), me).wait_recv()
            passed[j].start()
        copy(0, sibling, me).wait_recv()
        for j, chip in enumerate(chips):
            copy(4 + j, (*chip, 1 - cc), me).wait_recv()
        for cp in first + passed:
            cp.wait_send()
        mine.wait()

    return pl.pallas_call(
        body, name=name,
        out_shape=jax.ShapeDtypeStruct((N_DEV, r, c), x_loc.dtype),
        in_specs=[pl.BlockSpec(memory_space=pl.ANY)],
        out_specs=pl.BlockSpec(memory_space=pl.ANY),
        scratch_shapes=[pltpu.SemaphoreType.DMA((7,)), pltpu.SemaphoreType.DMA((7,)), pltpu.SemaphoreType.DMA(())],
    )(x_loc)


HBM_SPEC = pl.BlockSpec(memory_space=pltpu.HBM)
SEM_SPEC = pl.BlockSpec(memory_space=pltpu.SEMAPHORE)
N_PEERS = N_DEV - 1


def _peer(k):
    x, y, c = _me()
    return (1 - x if k & 4 else x, 1 - y if k & 2 else y, 1 - c if k & 1 else c)


def _exchange_copies(src_ref, land_ref, send_sems, recv_sems, scatter):
    x, y, c = _me()
    mine = 4 * x + 2 * y + c
    copies = []
    for k in range(1, N_DEV):
        px, py, pc = _peer(k)
        src = src_ref.at[4 * px + 2 * py + pc] if scatter else src_ref
        copies.append(pltpu.make_async_remote_copy(
            src_ref=src, dst_ref=land_ref.at[mine], send_sem=send_sems.at[k - 1], recv_sem=recv_sems.at[k - 1],
            device_id=(px, py, pc), device_id_type=MESH))
    return copies


def exchange_start(src, land, name, scatter):
    def body(src_ref, land_ref, send_sems, recv_sems, src_thru, land_thru, token):
        for cp in _exchange_copies(src_ref, land_ref, send_sems, recv_sems, scatter):
            cp.start()
        token[...] = jnp.zeros_like(token)

    return pl.pallas_call(
        body, name=name,
        out_shape=(pltpu.SemaphoreType.DMA((N_PEERS,)), pltpu.SemaphoreType.DMA((N_PEERS,)),
                   pltpu.HBM(src.shape, src.dtype), pltpu.HBM(land.shape, land.dtype),
                   jax.ShapeDtypeStruct((8, LANES), F32)),
        in_specs=(HBM_SPEC, HBM_SPEC),
        out_specs=(SEM_SPEC, SEM_SPEC, HBM_SPEC, HBM_SPEC, pl.BlockSpec(memory_space=pltpu.VMEM)),
        input_output_aliases={0: 2, 1: 3},
        compiler_params=pltpu.CompilerParams(has_side_effects=pltpu.SideEffectType.DATAFLOW_SIDE_EFFECTING),
    )(pltpu.with_memory_space_constraint(src, pltpu.HBM), pltpu.with_memory_space_constraint(land, pltpu.HBM))


def exchange_wait(started, after, name, scatter):
    send_sems, recv_sems, src_thru, land_thru, _ = started

    def body(src_ref, land_ref, send_sems, recv_sems, after_ref, src_dead, got_ref):
        for cp in _exchange_copies(src_ref, land_ref, send_sems, recv_sems, scatter):
            cp.wait_send()
            cp.wait_recv()

    return pl.pallas_call(
        body, name=name,
        out_shape=(pltpu.HBM(src_thru.shape, src_thru.dtype), pltpu.HBM(land_thru.shape, land_thru.dtype)),
        in_specs=(HBM_SPEC, HBM_SPEC, SEM_SPEC, SEM_SPEC, pl.BlockSpec(memory_space=pl.ANY)),
        out_specs=(HBM_SPEC, HBM_SPEC), input_output_aliases={0: 0, 1: 1},
        compiler_params=pltpu.CompilerParams(has_side_effects=pltpu.SideEffectType.DATAFLOW_SIDE_EFFECTING),
    )(src_thru, land_thru, send_sems, recv_sems, after)[1]


def after_token(small, started):
    return small + started[4][0, 0]


def sum_leading(x, name):
    n, r, c = x.shape
    tr = _tile(r, 512, 16)

    def body(x_ref, o_ref):
        acc = x_ref[0].astype(F32)
        for k in range(1, n):
            acc = acc + x_ref[k].astype(F32)
        o_ref[...] = acc

    return pl.pallas_call(
        body, name=name,
        out_shape=jax.ShapeDtypeStruct((r, c), F32),
        grid=(r // tr,),
        in_specs=[pl.BlockSpec((n, tr, c), lambda i: (0, i, 0))],
        out_specs=pl.BlockSpec((tr, c), lambda i: (i, 0)),
        compiler_params=_params("arbitrary"),
    )(x)


MM_VMEM_BUDGET = 36 * 1024 * 1024
GRID_STEP_AS_BYTES = 1 << 20


def _mm_tiles(m, n, a_row_bytes, b_col_bytes, out_bytes):
    tms = [c for c in (2048, 1024, 512, 256, 128, 64, 32, 16, 8) if m % c == 0] or [m]
    tns = [c for c in range(LANES, min(n, 2048) + 1, LANES) if n % c == 0] or [n]
    best = None
    for tm in tms:
        for tn in tns:
            vmem = 2 * (tm * a_row_bytes + tn * b_col_bytes) + 2 * tm * tn * out_bytes + tm * tn * 4
            if vmem > MM_VMEM_BUDGET:
                continue
            steps = (m // tm) * (n // tn)
            cost = steps * GRID_STEP_AS_BYTES + (m // tm) * n * b_col_bytes + m * a_row_bytes
            if best is None or cost < best[0]:
                best = (cost, tm, tn)
    assert best is not None, (m, n, a_row_bytes, b_col_bytes)
    return best[1], best[2]


def mm(pairs, *, trans_b, out_dtype, name, out_slab=False, bias=None):
    a0 = pairs[0][0]
    m = a0.shape[1] if a0.ndim == 3 else a0.shape[0]
    n = pairs[0][1].shape[0] if trans_b else pairs[0][1].shape[1]
    a_row_bytes = sum((b.shape[1] if trans_b else b.shape[0]) * a.dtype.itemsize for a, b in pairs)
    b_col_bytes = sum((b.shape[1] if trans_b else b.shape[0]) * b.dtype.itemsize for _, b in pairs)
    tm, tn = _mm_tiles(m, n, a_row_bytes, b_col_bytes, jnp.dtype(out_dtype).itemsize)
    slabs = [a.ndim == 3 for a, _ in pairs]
    n_pairs = len(pairs)

    def body(*refs):
        o_ref = refs[-1]
        acc = bias_ref = None
        if bias is not None:
            bias_ref = refs[2 * n_pairs]
        for i in range(n_pairs):
            a_ref, b_ref = refs[2 * i], refs[2 * i + 1]
            if slabs[i]:
                a = jnp.concatenate([a_ref[s].astype(BF16) for s in range(a_ref.shape[0])], axis=1)
            else:
                a = a_ref[...].astype(BF16)
            b = b_ref[...].astype(BF16)
            part = _nt(a, b) if trans_b else _nn(a, b)
            acc = part if acc is None else acc + part
        if bias_ref is not None:
            acc = acc + bias_ref[...]
        if out_slab:
            for s in range(tn // LANES):
                o_ref[s] = acc[:, s * LANES:(s + 1) * LANES].astype(out_dtype)
        else:
            o_ref[...] = acc.astype(out_dtype)

    in_specs, args = [], []
    for (a, b), slab in zip(pairs, slabs):
        if slab:
            in_specs.append(pl.BlockSpec((a.shape[0], tm, LANES), lambda i, j: (0, i, 0)))
        else:
            in_specs.append(pl.BlockSpec((tm, a.shape[1]), lambda i, j: (i, 0)))
        if trans_b:
            in_specs.append(pl.BlockSpec((tn, b.shape[1]), lambda i, j: (j, 0)))
        else:
            in_specs.append(pl.BlockSpec((b.shape[0], tn), lambda i, j: (0, j)))
        args += [a, b]
    if bias is not None:
        in_specs.append(pl.BlockSpec((1, tn), lambda i, j: (0, j)))
        args.append(bias)
    if out_slab:
        out_shape = jax.ShapeDtypeStruct((n // LANES, m, LANES), out_dtype)
        out_spec = pl.BlockSpec((tn // LANES, tm, LANES), lambda i, j: (j, i, 0))
    else:
        out_shape = jax.ShapeDtypeStruct((m, n), out_dtype)
        out_spec = pl.BlockSpec((tm, tn), lambda i, j: (i, j))
    return pl.pallas_call(
        body, name=name, out_shape=out_shape, grid=(m // tm, n // tn),
        in_specs=in_specs, out_specs=out_spec,
        compiler_params=_params("arbitrary", "arbitrary"),
    )(*args)


def mm_tn(a, b, *, name, out_dtype=F32, tk_cap=1536, tn_cap=1024, tm_cap=512):
    slab = a.ndim == 3
    m = a.shape[1] if slab else a.shape[0]
    k = a.shape[0] * LANES if slab else a.shape[1]
    n = b.shape[1]
    tk = _tile(k, tk_cap)
    tn = _tile(n, tn_cap)
    tm = _tile(m, tm_cap, 8)
    n_steps = m // tm

    def body(a_ref, b_ref, o_ref, acc_ref):
        step = pl.program_id(2)

        @pl.when(step == 0)
        def _():
            acc_ref[...] = jnp.zeros_like(acc_ref)

        bb = b_ref[...].astype(BF16)
        if slab:
            for s in range(tk // LANES):
                acc_ref[s * LANES:(s + 1) * LANES, :] += _tn(a_ref[s].astype(BF16), bb)
        else:
            acc_ref[...] += _tn(a_ref[...].astype(BF16), bb)

        @pl.when(step == n_steps - 1)
        def _():
            o_ref[...] = acc_ref[...].astype(out_dtype)

    if slab:
        a_spec = pl.BlockSpec((tk // LANES, tm, LANES), lambda i, j, t: (i, t, 0))
    else:
        a_spec = pl.BlockSpec((tm, tk), lambda i, j, t: (t, i))
    return pl.pallas_call(
        body, name=name, out_shape=jax.ShapeDtypeStruct((k, n), out_dtype), grid=(k // tk, n // tn, n_steps),
        in_specs=[a_spec, pl.BlockSpec((tm, tn), lambda i, j, t: (t, j))],
        out_specs=pl.BlockSpec((tk, tn), lambda i, j, t: (i, j)),
        scratch_shapes=[pltpu.VMEM((tk, tn), F32)],
        compiler_params=_params("arbitrary", "arbitrary", "arbitrary"),
    )(a, b)


def _row_spec(d, k):
    return pl.BlockSpec((1, 1, d), lambda b, i: (6 * b + k, 0, 0))


def modulate(x, mod, k_shift, k_scale, bl, name):
    t, d = x.shape
    s = t // bl
    tm = _tile(s, 512, 8)
    nt = s // tm

    def body(x_ref, sh_ref, sc_ref, o_ref):
        o_ref[...] = (x_ref[...] * (1.0 + sc_ref[0]) + sh_ref[0]).astype(BF16)

    return pl.pallas_call(
        body, name=name, out_shape=jax.ShapeDtypeStruct((t, d), BF16), grid=(bl, nt),
        in_specs=[pl.BlockSpec((tm, d), lambda b, i: (b * nt + i, 0)), _row_spec(d, k_shift), _row_spec(d, k_scale)],
        out_specs=pl.BlockSpec((tm, d), lambda b, i: (b * nt + i, 0)),
        compiler_params=_params("arbitrary", "arbitrary"),
    )(x, mod, mod)


def _layer_norm_stats(r):
    mu = jnp.mean(r, axis=-1, keepdims=True)
    rc = r - mu
    var = jnp.mean(rc * rc, axis=-1, keepdims=True)
    rstd = lax.rsqrt(var + NORM_EPS)
    return rc * rstd, rstd


def residual_layer_norm(x, y, mod, k_gate, g, b, bl, name, next_mod=None):
    t, d = x.shape
    s = t // bl
    tm = _tile(s, 256, 8)
    nt = s // tm
    has_next = next_mod is not None

    def body(*refs):
        x_ref, y_ref, gt_ref, g_ref, b_ref = refs[:5]
        rest = refs[5:]
        if has_next:
            sh_ref, sc_ref, o_ref, r_ref, u_ref = rest
        else:
            o_ref, r_ref = rest
        r = ALPHA * x_ref[...] + (1.0 + gt_ref[0]) * y_ref[...]
        xhat, _ = _layer_norm_stats(r)
        out = xhat * g_ref[...] + b_ref[...]
        o_ref[...] = out
        r_ref[...] = r
        if has_next:
            u_ref[...] = (out * (1.0 + sc_ref[0]) + sh_ref[0]).astype(BF16)

    tok = pl.BlockSpec((tm, d), lambda bb, i: (bb * nt + i, 0))
    vec = pl.BlockSpec((1, d), lambda bb, i: (0, 0))
    in_specs = [tok, tok, _row_spec(d, k_gate), vec, vec]
    args = [x, y, mod, g, b]
    out_shape = [jax.ShapeDtypeStruct((t, d), F32), jax.ShapeDtypeStruct((t, d), F32)]
    out_specs = [tok, tok]
    if has_next:
        in_specs += [_row_spec(d, next_mod[0]), _row_spec(d, next_mod[1])]
        args += [mod if len(next_mod) == 2 else next_mod[2]] * 2
        out_shape.append(jax.ShapeDtypeStruct((t, d), BF16))
        out_specs.append(tok)
    return pl.pallas_call(
        body, name=name, out_shape=out_shape, grid=(bl, nt), in_specs=in_specs, out_specs=out_specs,
        compiler_params=_params("arbitrary", "arbitrary"),
    )(*args)


def loss_head(xo, target, name):
    t, d = xo.shape
    tm = _tile(t, 512, 8)

    def body(x_ref, t_ref, l_ref, dx_ref):
        @pl.when(pl.program_id(0) == 0)
        def _():
            l_ref[...] = jnp.zeros_like(l_ref)

        e = x_ref[...] - t_ref[...]
        l_ref[...] += jnp.sum(e * e, axis=0, keepdims=True) * (0.5 / d)
        dx_ref[...] = e * (1.0 / d)

    tok = pl.BlockSpec((tm, d), lambda i: (i, 0))
    return pl.pallas_call(
        body, name=name,
        out_shape=[jax.ShapeDtypeStruct((1, d), F32), jax.ShapeDtypeStruct((t, d), F32)],
        grid=(t // tm,), in_specs=[tok, tok],
        out_specs=[pl.BlockSpec((1, d), lambda i: (0, 0)), tok],
        compiler_params=_params("arbitrary"),
    )(xo, target)


def sublayer_backward(d_a, bl, name, *, du=None, scale=None, x_in=None, ln=None):
    t, d = d_a.shape
    s = t // bl
    tm = _tile(s, 256, 8)
    nt = s // tm
    has_mod = du is not None
    has_ln = ln is not None
    assert has_mod or has_ln
    assert has_ln or x_in is not None

    def body(*refs):
        refs = list(refs)
        da_ref = refs.pop(0)
        if has_mod:
            du_ref, sc_ref = refs.pop(0), refs.pop(0)
        if has_ln:
            r_ref, y_ref, g_ref, b_ref, gt_ref = (refs.pop(0) for _ in range(5))
        elif has_mod:
            xin_ref = refs.pop(0)
        dx_ref = refs.pop(0)
        if has_ln:
            dy_ref, dg_ref, db_ref, dgt_ref = (refs.pop(0) for _ in range(4))
        if has_mod:
            dsc_ref, dsh_ref = refs.pop(0), refs.pop(0)
        first_tile = pl.program_id(1) == 0
        first_step = jnp.logical_and(pl.program_id(0) == 0, first_tile)

        dout = da_ref[...]
        if has_ln:
            xhat, rstd = _layer_norm_stats(r_ref[...])
        if has_mod:
            duv = du_ref[...]
            dout = dout + duv * (1.0 + sc_ref[0])
            xin = xhat * g_ref[...] + b_ref[...] if has_ln else xin_ref[...]

            @pl.when(first_tile)
            def _():
                dsc_ref[...] = jnp.zeros_like(dsc_ref)
                dsh_ref[...] = jnp.zeros_like(dsh_ref)

            dsc_ref[0] += jnp.sum(duv * xin, axis=0, keepdims=True)
            dsh_ref[0] += jnp.sum(duv, axis=0, keepdims=True)
        if not has_ln:
            dx_ref[...] = dout
            return

        @pl.when(first_step)
        def _():
            dg_ref[...] = jnp.zeros_like(dg_ref)
            db_ref[...] = jnp.zeros_like(db_ref)

        @pl.when(first_tile)
        def _():
            dgt_ref[...] = jnp.zeros_like(dgt_ref)

        dg_ref[...] += jnp.sum(dout * xhat, axis=0, keepdims=True)
        db_ref[...] += jnp.sum(dout, axis=0, keepdims=True)
        dxh = dout * g_ref[...]
        dr = rstd * (dxh - jnp.mean(dxh, axis=-1, keepdims=True) - xhat * jnp.mean(dxh * xhat, axis=-1, keepdims=True))
        dx_ref[...] = ALPHA * dr
        dy_ref[...] = ((1.0 + gt_ref[0]) * dr).astype(BF16)
        dgt_ref[0] += jnp.sum(dr * y_ref[...], axis=0, keepdims=True)

    tok = pl.BlockSpec((tm, d), lambda bb, i: (bb * nt + i, 0))
    vec = pl.BlockSpec((1, d), lambda bb, i: (0, 0))
    seq = pl.BlockSpec((1, 1, d), lambda bb, i: (bb, 0, 0))
    in_specs, args = [tok], [d_a]
    if has_mod:
        in_specs += [tok, _row_spec(d, scale[1])]
        args += [du, scale[0]]
    if has_ln:
        r, y, g, b, gate = ln
        in_specs += [tok, tok, vec, vec, _row_spec(d, gate[1])]
        args += [r, y, g, b, gate[0]]
    elif has_mod:
        in_specs.append(tok)
        args.append(x_in)
    names = ["dx"]
    out_shape, out_specs = [jax.ShapeDtypeStruct((t, d), F32)], [tok]
    if has_ln:
        names += ["dy", "dg", "db", "dgate"]
        out_shape += [jax.ShapeDtypeStruct((t, d), BF16), jax.ShapeDtypeStruct((1, d), F32),
                      jax.ShapeDtypeStruct((1, d), F32), jax.ShapeDtypeStruct((bl, 1, d), F32)]
        out_specs += [tok, vec, vec, seq]
    if has_mod:
        names += ["dscale", "dshift"]
        out_shape += [jax.ShapeDtypeStruct((bl, 1, d), F32)] * 2
        out_specs += [seq, seq]
    outs = pl.pallas_call(
        body, name=name, out_shape=out_shape, grid=(bl, nt), in_specs=in_specs, out_specs=out_specs,
        compiler_params=_params("arbitrary", "arbitrary"),
    )(*args)
    return dict(zip(names, outs))


def _silu(a):
    return a * jax.nn.sigmoid(a)


def silu_rows(a, name):
    def body(a_ref, o_ref):
        o_ref[...] = _silu(a_ref[...]).astype(BF16)

    return pl.pallas_call(body, name=name, out_shape=jax.ShapeDtypeStruct(a.shape, BF16))(a)


def _swiglu_tiles(t, f):
    return _tile(t, 512, 8), _tile(f, 1536)


def swiglu_in(u, wt_gate, wt_up, name):
    t, d = u.shape
    f = wt_gate.shape[0]
    tm, tf = _swiglu_tiles(t, f)

    def body(u_ref, g_ref, w_ref, a_ref, b_ref, h_ref):
        uv = u_ref[...]
        a = _nt(uv, g_ref[...])
        b = _nt(uv, w_ref[...])
        a_ref[...] = a
        b_ref[...] = b
        h_ref[...] = (_silu(a) * b).astype(BF16)

    w_spec = pl.BlockSpec((tf, d), lambda i, j: (j, 0))
    o_spec = pl.BlockSpec((tm, tf), lambda i, j: (i, j))
    return pl.pallas_call(
        body, name=name,
        out_shape=[jax.ShapeDtypeStruct((t, f), F32), jax.ShapeDtypeStruct((t, f), F32), jax.ShapeDtypeStruct((t, f), BF16)],
        grid=(t // tm, f // tf), in_specs=[pl.BlockSpec((tm, d), lambda i, j: (i, 0)), w_spec, w_spec],
        out_specs=[o_spec, o_spec, o_spec], compiler_params=_params("arbitrary", "arbitrary"),
    )(u, wt_gate, wt_up)


def swiglu_out_backward(dy, w_down, a, b, name):
    t, d = dy.shape
    f = w_down.shape[0]
    tm, tf = _swiglu_tiles(t, f)

    def body(dy_ref, w_ref, a_ref, b_ref, da_ref, db_ref):
        dh = _nt(dy_ref[...], w_ref[...])
        av = a_ref[...]
        sig = jax.nn.sigmoid(av)
        da_ref[...] = (dh * b_ref[...] * (sig * (1.0 + av * (1.0 - sig)))).astype(BF16)
        db_ref[...] = (dh * (av * sig)).astype(BF16)

    spec = pl.BlockSpec((tm, tf), lambda i, j: (i, j))
    return pl.pallas_call(
        body, name=name, out_shape=[jax.ShapeDtypeStruct((t, f), BF16)] * 2, grid=(t // tm, f // tf),
        in_specs=[pl.BlockSpec((tm, d), lambda i, j: (i, 0)), pl.BlockSpec((tf, d), lambda i, j: (j, 0)), spec, spec],
        out_specs=[spec, spec], compiler_params=_params("arbitrary", "arbitrary"),
    )(dy, w_down, a, b)


def rope_tables(pos, inv_freq, sign, name):
    t = pos.shape[0]
    tm = _tile(t, 512, 8)

    def body(p_ref, f_ref, s_ref, c_out, s_out):
        ang = p_ref[...] * f_ref[...]
        c_out[...] = jnp.cos(ang)
        s_out[...] = jnp.sin(ang) * s_ref[...]

    vec = pl.BlockSpec((1, LANES), lambda i: (0, 0))
    tab = pl.BlockSpec((tm, LANES), lambda i: (i, 0))
    return pl.pallas_call(
        body, name=name, out_shape=[jax.ShapeDtypeStruct((t, LANES), F32)] * 2, grid=(t // tm,),
        in_specs=[pl.BlockSpec((tm, 1), lambda i: (i, 0)), vec, vec], out_specs=[tab, tab],
        compiler_params=_params("arbitrary"),
    )(pos, inv_freq, sign)


def _rot_half(v):
    lane = lax.broadcasted_iota(jnp.int32, v.shape, v.ndim - 1)
    up = pltpu.roll(v, LANES - MLA_ROPE // 2, v.ndim - 1)
    down = pltpu.roll(v, MLA_ROPE // 2, v.ndim - 1)
    return jnp.where(lane % MLA_ROPE < MLA_ROPE // 2, up, down)


def _rope(v, cos, sin_signed):
    return v * cos + _rot_half(v) * sin_signed


def _rope_transposed(dv, cos, sin_signed):
    return dv * cos + _rot_half(dv * sin_signed)


def rope_slabs(v, cos, sin_signed, out_dtype, name, transposed=False):
    ns, t, _ = v.shape
    tm = _tile(t, 512, 8)
    fn = _rope_transposed if transposed else _rope

    def body(v_ref, c_ref, s_ref, o_ref):
        o_ref[0] = fn(v_ref[0].astype(F32), c_ref[...], s_ref[...]).astype(out_dtype)

    tab = pl.BlockSpec((tm, LANES), lambda j, i: (i, 0))
    spec = pl.BlockSpec((1, tm, LANES), lambda j, i: (j, i, 0))
    return pl.pallas_call(
        body, name=name, out_shape=jax.ShapeDtypeStruct(v.shape, out_dtype), grid=(ns, t // tm),
        in_specs=[spec, tab, tab], out_specs=spec, compiler_params=_params("arbitrary", "arbitrary"),
    )(v, cos, sin_signed)


def _rms(x):
    rinv = lax.rsqrt(jnp.mean(x * x, axis=-1, keepdims=True) + NORM_EPS)
    return x * rinv, rinv


def mla_latents_forward(h_in, g_q, g_kv, cos, sin_signed, name):
    t = h_in.shape[0]
    tm = _tile(t, 512, 8)

    def body(h_ref, gq_ref, gkv_ref, c_ref, s_ref, cq_ref, ckv_ref, kr_ref):
        cq_ref[...] = (_rms(h_ref[:, 0:MLA_QR])[0] * gq_ref[...]).astype(BF16)
        ckv_ref[...] = (_rms(h_ref[:, MLA_QR:MLA_QR + MLA_KVR])[0] * gkv_ref[...]).astype(BF16)
        kr_ref[...] = _rope(h_ref[:, MLA_QR + MLA_KVR:], c_ref[...], s_ref[...]).astype(BF16)

    def tok(w):
        return pl.BlockSpec((tm, w), lambda i: (i, 0))

    def vec(w):
        return pl.BlockSpec((1, w), lambda i: (0, 0))

    return pl.pallas_call(
        body, name=name,
        out_shape=[jax.ShapeDtypeStruct((t, MLA_QR), BF16), jax.ShapeDtypeStruct((t, MLA_KVR), BF16),
                   jax.ShapeDtypeStruct((t, LANES), BF16)],
        grid=(t // tm,),
        in_specs=[tok(h_in.shape[1]), vec(MLA_QR), vec(MLA_KVR), tok(LANES), tok(LANES)],
        out_specs=[tok(MLA_QR), tok(MLA_KVR), tok(LANES)],
        compiler_params=_params("arbitrary"),
    )(h_in, g_q, g_kv, cos, sin_signed)


def mla_latents_backward(h_in, dcq, dckv, dkr, g_q, g_kv, cos, sin_signed, name):
    t, w = h_in.shape
    tm = _tile(t, 512, 8)

    def body(h_ref, dcq_ref, dckv_ref, dkr_ref, gq_ref, gkv_ref, c_ref, s_ref, dh_ref, dgq_ref, dgkv_ref):
        @pl.when(pl.program_id(0) == 0)
        def _():
            dgq_ref[...] = jnp.zeros_like(dgq_ref)
            dgkv_ref[...] = jnp.zeros_like(dgkv_ref)

        def rms_bwd(x, dc, g_ref, dg_ref):
            xn, rinv = _rms(x)
            dg_ref[...] += jnp.sum(dc * xn, axis=0, keepdims=True)
            dxn = dc * g_ref[...]
            return rinv * (dxn - xn * jnp.mean(dxn * xn, axis=-1, keepdims=True))

        dq = rms_bwd(h_ref[:, 0:MLA_QR], dcq_ref[...], gq_ref, dgq_ref)
        dkv = rms_bwd(h_ref[:, MLA_QR:MLA_QR + MLA_KVR], dckv_ref[...], gkv_ref, dgkv_ref)
        dr = _rope_transposed(dkr_ref[...], c_ref[...], s_ref[...])
        dh_ref[...] = jnp.concatenate([dq, dkv, dr], axis=1).astype(BF16)

    def tok(ww):
        return pl.BlockSpec((tm, ww), lambda i: (i, 0))

    def vec(ww):
        return pl.BlockSpec((1, ww), lambda i: (0, 0))

    return pl.pallas_call(
        body, name=name,
        out_shape=[jax.ShapeDtypeStruct((t, w), BF16), jax.ShapeDtypeStruct((1, MLA_QR), F32),
                   jax.ShapeDtypeStruct((1, MLA_KVR), F32)],
        grid=(t // tm,),
        in_specs=[tok(w), tok(MLA_QR), tok(MLA_KVR), tok(LANES), vec(MLA_QR), vec(MLA_KVR), tok(LANES), tok(LANES)],
        out_specs=[tok(w), vec(MLA_QR), vec(MLA_KVR)],
        compiler_params=_params("arbitrary"),
    )(h_in, dcq, dckv, dkr, g_q, g_kv, cos, sin_signed)


def _tri(n, lower):
    r = lax.broadcasted_iota(jnp.int32, (n, n), 0)
    c = lax.broadcasted_iota(jnp.int32, (n, n), 1)
    return jnp.where(r >= c if lower else r <= c, 1.0, 0.0).astype(F32)


def _dot_exact(tri, v):
    hi = v.astype(BF16)
    mid = (v - hi.astype(F32)).astype(BF16)
    lo = (v - hi.astype(F32) - mid.astype(F32)).astype(BF16)
    t = tri.astype(BF16)
    return _nn(t, hi) + _nn(t, mid) + _nn(t, lo)


def fox_gate_forward(z, b_f, bl, name):
    t = z.shape[0]
    s = t // bl
    ch = LANES
    n_ch = s // ch

    def body(z_ref, b_ref, f_ref, fs_ref):
        tri = _tri(ch, True)
        carry = jnp.zeros((1, LANES), F32)
        for k in range(n_ch):
            x = z_ref[k * ch:(k + 1) * ch, :] + b_ref[...]
            logf = jnp.minimum(x, 0.0) - jnp.log(1.0 + jnp.exp(-jnp.abs(x)))
            cs = _dot_exact(tri, logf) + carry
            carry = cs[ch - 1:ch, :]
            f_ref[k * ch:(k + 1) * ch, :] = cs
            for h in range(FOX_HEADS):
                fs_ref[h, k * ch:(k + 1) * ch, :] = jnp.broadcast_to(cs[:, h:h + 1], (ch, LANES))

    return pl.pallas_call(
        body, name=name,
        out_shape=[jax.ShapeDtypeStruct((t, LANES), F32), jax.ShapeDtypeStruct((FOX_HEADS, t, LANES), F32)],
        grid=(bl,),
        in_specs=[pl.BlockSpec((s, LANES), lambda b: (b, 0)), pl.BlockSpec((1, LANES), lambda b: (0, 0))],
        out_specs=[pl.BlockSpec((s, LANES), lambda b: (b, 0)),
                   pl.BlockSpec((FOX_HEADS, s, LANES), lambda b: (0, b, 0))],
        compiler_params=_params("arbitrary"),
    )(z, b_f)


def fox_gate_backward(z, b_f, df, bl, name):
    t = z.shape[0]
    s = t // bl
    ch = LANES
    n_ch = s // ch

    def body(z_ref, b_ref, df_ref, dz_ref, db_ref):
        @pl.when(pl.program_id(0) == 0)
        def _():
            db_ref[...] = jnp.zeros_like(db_ref)

        tri = _tri(ch, False)
        carry = jnp.zeros((1, LANES), F32)
        for k in reversed(range(n_ch)):
            cs = _dot_exact(tri, df_ref[k * ch:(k + 1) * ch, :]) + carry
            carry = cs[0:1, :]
            x = z_ref[k * ch:(k + 1) * ch, :] + b_ref[...]
            dz = cs * (1.0 - jax.nn.sigmoid(x))
            dz_ref[k * ch:(k + 1) * ch, :] = dz
            db_ref[...] += jnp.sum(dz, axis=0, keepdims=True)

    tok = pl.BlockSpec((s, LANES), lambda b: (b, 0))
    vec = pl.BlockSpec((1, LANES), lambda b: (0, 0))
    return pl.pallas_call(
        body, name=name,
        out_shape=[jax.ShapeDtypeStruct((t, LANES), F32), jax.ShapeDtypeStruct((1, LANES), F32)],
        grid=(bl,), in_specs=[tok, vec, tok], out_specs=[tok, vec],
        compiler_params=_params("arbitrary"),
    )(z, b_f, df)


NEG_INF = float("-inf")


def _attn_tiles(s):
    return _tile(s, 512, 8)


def attention_forward(kind, ops, bl, scale, name):
    fox = kind == "fox"
    if fox:
        assert math.frexp(scale)[0] == 0.5, "the FoX scale is folded into bf16 queries: it must be a power of two"
        qkv, fq, fk = ops
        t = qkv.shape[1]
        n_pair = FOX_HEADS // 2
    else:
        qn, qr, kn, kr, v = ops
        t = qn.shape[1]
        n_pair = MLA_HEADS // 2
    s = t // bl
    tq = _attn_tiles(s)
    nq = s // tq
    half = LANES // 2

    def body(*refs):
        if fox:
            q_ref, k_ref, v_ref, fq_ref, fk_ref, o_ref, lse_ref, o32_ref = refs
        else:
            qn_ref, qr_ref, kn_ref, kr_ref, v_ref, o_ref, lse_ref = refs
        i = pl.program_id(2)
        row = lax.broadcasted_iota(jnp.int32, (tq, tq), 0)
        col = lax.broadcasted_iota(jnp.int32, (tq, tq), 1)
        heads = []
        for e in range(2):
            sl = slice(e * half, (e + 1) * half)
            if fox:
                heads.append((sl, q_ref[0, :, sl] * jnp.asarray(scale, BF16), None))
            else:
                heads.append((sl, qn_ref[e], qr_ref[0, :, sl]))
        dv = half if fox else LANES

        def wide(stat):
            return jnp.concatenate([stat] * (tq // LANES), axis=1)

        def step(j, carry, masked):
            rows = pl.ds(pl.multiple_of(j * tq, tq), tq)
            new = []
            for e, (sl, qa, qb) in enumerate(heads):
                m, l, acc = carry[e]
                if fox:
                    sc = _nt(qa, k_ref[0, rows, sl]) + wide(fq_ref[e]) - fk_ref[0, j, e:e + 1, :]
                    vv = v_ref[0, rows, sl]
                else:
                    sc = (_nt(qa, kn_ref[e, rows, :]) + _nt(qb, kr_ref[rows, 0:half])) * scale
                    vv = v_ref[e, rows, :]
                if masked:
                    sc = jnp.where(row >= col, sc, NEG_INF)
                m_new = jnp.maximum(m, jnp.max(sc, axis=1, keepdims=True))
                p = jnp.exp(sc - m_new)
                a = jnp.exp(m - m_new)
                l = a * l + jnp.sum(p, axis=1, keepdims=True)
                p_hi = p.astype(BF16)
                acc = a * acc + _nn(p_hi, vv)
                if fox:
                    acc = acc + _nn((p - p_hi.astype(F32)).astype(BF16), vv)
                new.append((m_new, l, acc))
            return tuple(new)

        init = (jnp.full((tq, 1), NEG_INF, F32), jnp.zeros((tq, 1), F32), jnp.zeros((tq, dv), F32))
        carry = step(i, (init, init), True)
        carry = lax.fori_loop(0, i, lambda j, c: step(j, c, False), carry)
        outs = [acc / l for _, l, acc in carry]
        for e, (m, l, _) in enumerate(carry):
            lse_ref[e] = jnp.broadcast_to(m + jnp.log(l), (tq, LANES))
        if fox:
            o32 = jnp.concatenate(outs, axis=1)
            o32_ref[0] = o32
            o_ref[0] = o32.astype(BF16)
        else:
            o_ref[0] = outs[0].astype(BF16)
            o_ref[1] = outs[1].astype(BF16)

    def q_idx(b, g, i):
        return (g, b * nq + i, 0)

    if fox:
        nk = fk.shape[1]
        in_specs = [pl.BlockSpec((1, tq, LANES), q_idx),
                    pl.BlockSpec((1, s, LANES), lambda b, g, i: (n_pair + g, b, 0)),
                    pl.BlockSpec((1, s, LANES), lambda b, g, i: (2 * n_pair + g, b, 0)),
                    pl.BlockSpec((2, tq, LANES), q_idx),
                    pl.BlockSpec((1, nk, 8, tq), lambda b, g, i: (b * n_pair + g, 0, 0, 0))]
        args = [qkv, qkv, qkv, fq, fk]
        o_spec = pl.BlockSpec((1, tq, LANES), q_idx)
    else:
        in_specs = [pl.BlockSpec((2, tq, LANES), q_idx),
                    pl.BlockSpec((1, tq, LANES), q_idx),
                    pl.BlockSpec((2, s, LANES), lambda b, g, i: (g, b, 0)),
                    pl.BlockSpec((s, LANES), lambda b, g, i: (b, 0)),
                    pl.BlockSpec((2, s, LANES), lambda b, g, i: (g, b, 0))]
        args = [qn, qr, kn, kr, v]
        o_spec = pl.BlockSpec((2, tq, LANES), q_idx)
    out_shape = [jax.ShapeDtypeStruct((8, t, LANES), BF16), jax.ShapeDtypeStruct((2 * n_pair, t, LANES), F32)]
    out_specs = [o_spec, pl.BlockSpec((2, tq, LANES), q_idx)]
    if fox:
        out_shape.append(jax.ShapeDtypeStruct((8, t, LANES), F32))
        out_specs.append(o_spec)
    outs = pl.pallas_call(
        body, name=name, out_shape=out_shape, grid=(bl, n_pair, nq), in_specs=in_specs, out_specs=out_specs,
        compiler_params=_params("arbitrary", "arbitrary", "arbitrary"),
    )(*args)
    return (outs[0], outs[1], outs[2] if fox else outs[0])


def attention_backward(kind, ops, o, do, lse, bl, scale, name):
    fox = kind == "fox"
    if fox:
        qkv, fq, fk = ops
        t = qkv.shape[1]
        n_pair = FOX_HEADS // 2
    else:
        qn, qr, kn, kr, v = ops
        t = qn.shape[1]
        n_pair = MLA_HEADS // 2
    s = t // bl
    tq = _attn_tiles(s)
    nq = s // tq
    half = LANES // 2

    def body(*refs):
        if fox:
            (q_ref, k_ref, v_ref, fq_ref, fk_ref, o_ref, do_ref, lse_ref,
             dq_ref, dk_ref, dv_ref, dfk_ref, delta_scr, qt_scr, dot_scr) = refs
        else:
            (qn_ref, qr_ref, kn_ref, kr_ref, v_ref, o_ref, do_ref, lse_ref,
             dqn_ref, dqr_ref, dkn_ref, dv_ref, dkr_ref, delta_scr, qt_scr, qrt_scr, dot_scr) = refs
        g, j = pl.program_id(1), pl.program_id(2)
        row = lax.broadcasted_iota(jnp.int32, (tq, tq), 0)
        col = lax.broadcasted_iota(jnp.int32, (tq, tq), 1)
        krows = pl.ds(pl.multiple_of(j * tq, tq), tq)

        def transposed(v):
            return v.astype(F32).T.astype(BF16)

        def wide(stat):
            return jnp.concatenate([stat] * (tq // LANES), axis=1)

        @pl.when(j == 0)
        def _():
            if fox:
                dq_ref[...] = jnp.zeros_like(dq_ref)
            else:
                dqn_ref[...] = jnp.zeros_like(dqn_ref)
                dqr_ref[...] = jnp.zeros_like(dqr_ref)
            for ii in range(nq):
                rws = slice(ii * tq, (ii + 1) * tq)
                deltas = []
                if fox:
                    prod = do_ref[0, rws, :].astype(F32) * o_ref[0, rws, :].astype(F32)
                    for e in range(2):
                        deltas.append(jnp.sum(prod[:, e * half:(e + 1) * half], axis=1, keepdims=True))
                    qt_scr[ii] = transposed(q_ref[0, rws, :])
                    dot_scr[ii] = transposed(do_ref[0, rws, :])
                else:
                    for e in range(2):
                        prod = do_ref[e, rws, :].astype(F32) * o_ref[e, rws, :].astype(F32)
                        deltas.append(jnp.sum(prod, axis=1, keepdims=True))
                        qt_scr[e, ii] = transposed(qn_ref[e, rws, :])
                        dot_scr[e, ii] = transposed(do_ref[e, rws, :])
                    qrt_scr[ii] = transposed(qr_ref[0, rws, :])
                for e in range(2):
                    delta_scr[e, rws, :] = jnp.broadcast_to(deltas[e], (tq, LANES))

        if fox:
            dfk_ref[...] = jnp.zeros_like(dfk_ref)
        else:
            @pl.when(jnp.logical_and(g == 0, j == 0))
            def _():
                dkr_ref[...] = jnp.zeros_like(dkr_ref)

        heads = []
        for e in range(2):
            sl = slice(e * half, (e + 1) * half)
            if fox:
                heads.append((sl, k_ref[0, :, sl], v_ref[0, :, sl], fk_ref[0, 0, e:e + 1, :]))
            else:
                heads.append((sl, kn_ref[e], v_ref[e], kr_ref[krows, 0:half]))
        dk_w = dv_w = half if fox else LANES

        def step(i, carry, masked):
            rows = pl.ds(pl.multiple_of(i * tq, tq), tq)
            new = []
            for e, (sl, k_e, v_e, x_e) in enumerate(heads):
                dk_acc, dv_acc, last = carry[e]
                if fox:
                    do_i = do_ref[0, rows, sl]
                    sc = _nt(q_ref[0, rows, sl], k_e) * scale + wide(fq_ref[e, rows, :]) - x_e
                else:
                    do_i = do_ref[e, rows, :]
                    sc = (_nt(qn_ref[e, rows, :], k_e) + _nt(qr_ref[0, rows, sl], x_e)) * scale
                if masked:
                    sc = jnp.where(row >= col, sc, NEG_INF)
                p = jnp.exp(sc - wide(lse_ref[e, rows, :]))
                dp = _nt(do_i, v_e)
                ds = p * (dp - wide(delta_scr[e, rows, :]))
                dsb = (ds * scale).astype(BF16)
                if fox:
                    fsl = slice(e * half, (e + 1) * half)
                    dv_acc = dv_acc + _nn(dot_scr[i, fsl, :], p.astype(BF16))
                    dk_acc = dk_acc + _nn(qt_scr[i, fsl, :], dsb)
                    dq_ref[0, rows, sl] += _nn(dsb, k_e)
                    last = last - jnp.sum(ds, axis=0, keepdims=True)
                else:
                    dv_acc = dv_acc + _nn(dot_scr[e, i], p.astype(BF16))
                    dk_acc = dk_acc + _nn(qt_scr[e, i], dsb)
                    dqn_ref[e, rows, :] += _nn(dsb, k_e)
                    dqr_ref[0, rows, sl] += _nn(dsb, x_e)
                    last = last + _nn(qrt_scr[i, e * half:(e + 1) * half, :], dsb)
                new.append((dk_acc, dv_acc, last))
            return tuple(new)

        last0 = jnp.zeros((1, tq), F32) if fox else jnp.zeros((half, tq), F32)
        init = (jnp.zeros((dk_w, tq), F32), jnp.zeros((dv_w, tq), F32), last0)
        carry = step(j, (init, init), True)
        carry = lax.fori_loop(j + 1, nq, lambda i, c: step(i, c, False), carry)
        if fox:
            for e in range(2):
                dfk_ref[0, 0, e:e + 1, :] = carry[e][2]
            dk_ref[0] = jnp.concatenate([carry[0][0], carry[1][0]], axis=0).T.astype(BF16)
            dv_ref[0] = jnp.concatenate([carry[0][1], carry[1][1]], axis=0).T.astype(BF16)
        else:
            for e in range(2):
                dkn_ref[e] = carry[e][0].T.astype(BF16)
                dv_ref[e] = carry[e][1].T.astype(BF16)
            dkr_t = carry[0][2] + carry[1][2]
            dkr_ref[krows, :] += jnp.concatenate([dkr_t, jnp.zeros_like(dkr_t)], axis=0).T

    def whole(b, g, j):
        return (g, b, 0)

    def kblk(b, g, j):
        return (g, b * nq + j, 0)

    if fox:
        in_specs = [pl.BlockSpec((1, s, LANES), whole),
                    pl.BlockSpec((1, tq, LANES), lambda b, g, j: (n_pair + g, b * nq + j, 0)),
                    pl.BlockSpec((1, tq, LANES), lambda b, g, j: (2 * n_pair + g, b * nq + j, 0)),
                    pl.BlockSpec((2, s, LANES), whole),
                    pl.BlockSpec((1, 1, 8, tq), lambda b, g, j: (b * n_pair + g, j, 0, 0)),
                    pl.BlockSpec((1, s, LANES), whole), pl.BlockSpec((1, s, LANES), whole),
                    pl.BlockSpec((2, s, LANES), whole)]
        args = [qkv, qkv, qkv, fq, fk, o, do, lse]
        out_shape = [jax.ShapeDtypeStruct((8, t, LANES), F32), jax.ShapeDtypeStruct((8, t, LANES), BF16),
                     jax.ShapeDtypeStruct((8, t, LANES), BF16), jax.ShapeDtypeStruct(fk.shape, F32)]
        out_specs = [pl.BlockSpec((1, s, LANES), whole), pl.BlockSpec((1, tq, LANES), kblk),
                     pl.BlockSpec((1, tq, LANES), kblk),
                     pl.BlockSpec((1, 1, 8, tq), lambda b, g, j: (b * n_pair + g, j, 0, 0))]
    else:
        pair = pl.BlockSpec((2, s, LANES), whole)
        pair_k = pl.BlockSpec((2, tq, LANES), kblk)
        in_specs = [pair, pl.BlockSpec((1, s, LANES), whole), pair_k,
                    pl.BlockSpec((s, LANES), lambda b, g, j: (b, 0)), pair_k,
                    pair, pair, pair]
        args = [qn, qr, kn, kr, v, o, do, lse]
        out_shape = [jax.ShapeDtypeStruct((8, t, LANES), F32), jax.ShapeDtypeStruct((4, t, LANES), F32),
                     jax.ShapeDtypeStruct((8, t, LANES), BF16), jax.ShapeDtypeStruct((8, t, LANES), BF16),
                     jax.ShapeDtypeStruct((t, LANES), F32)]
        out_specs = [pair, pl.BlockSpec((1, s, LANES), whole), pair_k, pair_k,
                     pl.BlockSpec((s, LANES), lambda b, g, j: (b, 0))]
    t_blocks = pltpu.VMEM((nq, LANES, tq), BF16)
    t_pairs = pltpu.VMEM((2, nq, LANES, tq), BF16)
    scratch = [pltpu.VMEM((2, s, LANES), F32)] + ([t_blocks, t_blocks] if fox else [t_pairs, t_blocks, t_pairs])
    return pl.pallas_call(
        body, name=name, out_shape=out_shape, grid=(bl, n_pair, nq), in_specs=in_specs, out_specs=out_specs,
        scratch_shapes=scratch, compiler_params=_params("arbitrary", "arbitrary", "arbitrary"),
    )(*args)


def adamw(w, g, m, v, name):
    shape = w.shape
    c = shape[-1]
    r = w.size // c
    tr = _tile(r, 512, 8)

    def body(w_ref, g_ref, m_ref, v_ref, d_ref, nm_ref, nv_ref):
        gv = g_ref[...]
        m2 = ADAM_B1 * m_ref[...] + (1.0 - ADAM_B1) * gv
        v2 = ADAM_B2 * v_ref[...] + (1.0 - ADAM_B2) * (gv * gv)
        m_hat = m2 / (1.0 - ADAM_B1 ** ADAM_STEP)
        v_hat = v2 / (1.0 - ADAM_B2 ** ADAM_STEP)
        d_ref[...] = -ADAM_LR * (m_hat / (jnp.sqrt(v_hat) + ADAM_EPS) + ADAM_WD * w_ref[...])
        nm_ref[...] = m2
        nv_ref[...] = v2

    spec = pl.BlockSpec((tr, c), lambda i: (i, 0))
    outs = pl.pallas_call(
        body, name=name, out_shape=[jax.ShapeDtypeStruct((r, c), F32)] * 3, grid=(r // tr,),
        in_specs=[spec] * 4, out_specs=[spec] * 3, compiler_params=_params("arbitrary"),
    )(*(a.reshape(r, c) for a in (w, g, m, v)))
    return tuple(a.reshape(shape) for a in outs)


PACK_COLS = 1024


def _pack_rows(a):
    return a.reshape(-1, PACK_COLS)


def kernel(x, c, positions, mla_w_in, mla_g_q, mla_w_uq, mla_g_kv, mla_w_uk, mla_w_uv, mla_w_o, fox_w_in, fox_b_f, fox_w_o, ada_w, ada_b, ffn_w_gate, ffn_w_up, ffn_w_down, ln_g, ln_b, loss_target, m_mla_w_in, m_mla_g_q, m_mla_w_uq, m_mla_g_kv, m_mla_w_uk, m_mla_w_uv, m_mla_w_o, m_fox_w_in, m_fox_b_f, m_fox_w_o, m_ada_w, m_ada_b, m_ffn_w_gate, m_ffn_w_up, m_ffn_w_down, m_ln_g, m_ln_b, v_mla_w_in, v_mla_g_q, v_mla_w_uq, v_mla_g_kv, v_mla_w_uk, v_mla_w_uv, v_mla_w_o, v_fox_w_in, v_fox_b_f, v_fox_w_o, v_ada_w, v_ada_b, v_ffn_w_gate, v_ffn_w_up, v_ffn_w_down, v_ln_g, v_ln_b):
    bl, s, d = x.shape
    t = bl * s
    ff = ffn_w_gate.shape[-1] * N_DEV
    dev = 4 * lax.axis_index("x") + 2 * lax.axis_index("y") + lax.axis_index("c")
    ada_cols = ada_w.shape[-1]
    fox_in = fox_w_in.shape[-1] * N_DEV
    mla_in = mla_w_in.shape[-1]
    mla_in_pad = mla_in + (-mla_in) % LANES

    def t_last(a):
        return jnp.swapaxes(a, -1, -2)

    local = {
        "mla_w_in": mla_w_in[0],
        "mla_w_uq": t_last(mla_w_uq[0]),
        "mla_w_uk": t_last(mla_w_uk[0]),
        "mla_w_uv": t_last(mla_w_uv[0]),
        "mla_w_o": mla_w_o[0],
        "fox_w_in": t_last(fox_w_in[0]),
        "fox_w_o": fox_w_o[0],
    }
    for i in range(DEPTH):
        local.update({f"gate{i}": t_last(ffn_w_gate[i]), f"up{i}": t_last(ffn_w_up[i]), f"down{i}": ffn_w_down[i]})
    groups = [["mla_w_in", "mla_w_uq", "mla_w_uk", "mla_w_uv", "mla_w_o"],
              ["gate0", "up0", "down0"],
              ["fox_w_in", "fox_w_o"],
              ["gate1", "up1", "down1"]]
    offsets, rows_of, slot_of, group_of = {}, {}, {}, {}
    group_rows = []
    for gi, names in enumerate(groups):
        rows = 0
        for nm in names:
            rows_of[nm] = local[nm].size // PACK_COLS
            slot_of[nm] = rows_of[nm] + (-rows_of[nm]) % 16
            offsets[nm] = rows
            group_of[nm] = gi
            rows += slot_of[nm]
        group_rows.append(rows)

    def slot(nm, rows):
        pad = [(0, 0)] * rows.ndim
        pad[-2] = (0, slot_of[nm] - rows_of[nm])
        return jnp.pad(rows, pad)

    def landing(block):
        land = lax.empty((N_DEV,) + block.shape, block.dtype)
        return lax.dynamic_update_slice(land, block[None], (dev, 0, 0))

    packed = [jnp.concatenate([slot(nm, _pack_rows(local[nm]).astype(BF16)) for nm in names], axis=0)
              for names in groups]
    gathered = [all_gather(packed[0], "gather_mla_weights")] + [None] * (len(groups) - 1)
    gather_started = [None] * len(groups)

    def depart(gi, after):
        block = lax.optimization_barrier((packed[gi], after))[0]
        gather_started[gi] = exchange_start(block, landing(block), f"gather_group{gi}_start", False)
        return gather_started[gi][4]

    def full(nm, cols):
        blk = gathered[group_of[nm]][:, offsets[nm]:offsets[nm] + rows_of[nm], :]
        return blk.reshape(-1, cols)

    w_in = jnp.pad(full("mla_w_in", mla_in), ((0, 0), (0, mla_in_pad - mla_in)))
    wt_uq = full("mla_w_uq", MLA_QR).reshape(MLA_HEADS, MLA_NOPE + MLA_ROPE, MLA_QR)
    wt_uq_n = wt_uq[:, :MLA_NOPE].reshape(MLA_HEADS * MLA_NOPE, MLA_QR)
    wt_uq_r = wt_uq[:, MLA_NOPE:].reshape(MLA_HEADS * MLA_ROPE, MLA_QR)
    wt_uk = full("mla_w_uk", MLA_KVR)
    wt_uv = full("mla_w_uv", MLA_KVR)
    w_mo = full("mla_w_o", d)
    wt_gate, wt_up, w_down = [None] * DEPTH, [None] * DEPTH, [None] * DEPTH

    def arrive(gi, after):
        gathered[gi] = exchange_wait(gather_started[gi], after, f"gather_group{gi}_wait", False)
        if gi + 1 < len(groups):
            gathered[gi] = lax.optimization_barrier((gathered[gi], depart(gi + 1, gathered[gi])))[0]
        for i in range(DEPTH):
            if group_of[f"gate{i}"] == gi:
                wt_gate[i], wt_up[i], w_down[i] = full(f"gate{i}", d), full(f"up{i}", d), full(f"down{i}", d)

    small = jnp.concatenate([c.reshape(-1, LANES), ln_g.reshape(-1, LANES), ln_b.reshape(-1, LANES)], axis=0)
    small_rows = small.shape[0]
    small = jnp.pad(small, ((0, (-small_rows) % 8), (0, 0)))
    small_all = all_gather(small, "gather_small")
    c_rows = bl * d // LANES
    c_all = small_all[:, :c_rows].reshape(N_DEV * bl, d)
    n_ln = DEPTH * 2
    ln_g_all = small_all[:, c_rows:c_rows + n_ln, :].transpose(1, 0, 2).reshape(DEPTH, 2, 1, d)
    ln_b_all = small_all[:, c_rows + n_ln:c_rows + 2 * n_ln, :].transpose(1, 0, 2).reshape(DEPTH, 2, 1, d)

    c_act = silu_rows(c_all, "silu_c")
    ada_b_loc = lax.dynamic_slice_in_dim(ada_b, dev * ada_cols, ada_cols, axis=1)
    mod_cols = [mm([(c_act, ada_w[i])], trans_b=False, out_dtype=F32, name=f"ada_fwd{i}", bias=ada_b_loc[i][None, :])
                for i in range(DEPTH)]
    mod_all = all_gather(jnp.concatenate(mod_cols, axis=0), "gather_mod")
    mod_all = mod_all.reshape(N_DEV, DEPTH, N_DEV * bl, ada_cols).transpose(1, 2, 0, 3).reshape(DEPTH, N_DEV * bl, 6 * d)
    mod_mine = lax.dynamic_slice_in_dim(mod_all, dev * bl, bl, axis=1)
    mods = [mod_mine[i].reshape(bl * 6, 1, d) for i in range(DEPTH)]
    mods[0] = mods[0] + depart(1, (mod_mine, gathered[0]))[0, 0]

    half_r = MLA_ROPE // 2
    inv_freq = ROPE_THETA ** (-jnp.arange(half_r, dtype=F32) / half_r)
    inv_freq = jnp.tile(inv_freq, LANES // half_r)[None, :]
    sign = jnp.tile(jnp.concatenate([-jnp.ones((half_r,), F32), jnp.ones((half_r,), F32)]), LANES // MLA_ROPE)[None, :]
    cos_t, sin_t = rope_tables(positions.astype(F32).reshape(t, 1), inv_freq, sign, "rope_tables")

    x2d = x.reshape(t, d)
    g_q, g_kv = mla_g_q.reshape(1, MLA_QR), mla_g_kv.reshape(1, MLA_KVR)
    b_f = jnp.pad(fox_b_f.reshape(1, FOX_HEADS), ((0, 0), (0, LANES - FOX_HEADS)))
    mla_scale = (MLA_NOPE + MLA_ROPE) ** -0.5
    fox_scale = FOX_HD ** -0.5
    tq = _attn_tiles(s)
    nk = s // tq

    saved = []
    u = modulate(x2d, mods[0], 0, 1, bl, "modulate0")
    xin = x2d
    for i in range(DEPTH):
        sv = {"u": u, "x_in": xin}
        if i % 2 == 0:
            h_in = mm([(u, w_in)], trans_b=False, out_dtype=F32, name=f"mla_in{i}")
            c_q, c_kv, k_r = mla_latents_forward(h_in, g_q, g_kv, cos_t, sin_t, f"mla_latents{i}")
            q_n = mm([(c_q, wt_uq_n)], trans_b=True, out_dtype=BF16, out_slab=True, name=f"mla_qn{i}")
            q_r_raw = mm([(c_q, wt_uq_r)], trans_b=True, out_dtype=F32, out_slab=True, name=f"mla_qr{i}")
            q_r = rope_slabs(q_r_raw, cos_t, sin_t, BF16, f"mla_qrope{i}")
            k_n = mm([(c_kv, wt_uk)], trans_b=True, out_dtype=BF16, out_slab=True, name=f"mla_kn{i}")
            v_m = mm([(c_kv, wt_uv)], trans_b=True, out_dtype=BF16, out_slab=True, name=f"mla_v{i}")
            ops = (q_n, q_r, k_n, k_r, v_m)
            o, lse, o_delta = attention_forward("mla", ops, bl, mla_scale, f"mla_attn{i}")
            y = mm([(o, w_mo)], trans_b=False, out_dtype=F32, name=f"mla_out{i}")
            sv.update(h_in=h_in, c_q=c_q, c_kv=c_kv, ops=ops, o=o, lse=lse, o_delta=o_delta)
        else:
            arrive(2, u)
            wt_fox = full("fox_w_in", d)
            wt_qkv = wt_fox[:3 * d]
            wt_f = jnp.pad(wt_fox[3 * d:], ((0, LANES - FOX_HEADS), (0, 0)))
            w_fo = full("fox_w_o", d)
            qkv = mm([(u, wt_qkv)], trans_b=True, out_dtype=BF16, out_slab=True, name=f"fox_qkv{i}")
            z = mm([(u, wt_f)], trans_b=True, out_dtype=F32, name=f"fox_z{i}")
            f_tok, f_q = fox_gate_forward(z, b_f, bl, f"fox_gate{i}")
            f_k = f_tok[:, :FOX_HEADS].reshape(bl, nk, tq, FOX_HEADS // 2, 2).transpose(0, 3, 1, 4, 2)
            f_k = jnp.pad(f_k.reshape(bl * FOX_HEADS // 2, nk, 2, tq), ((0, 0), (0, 0), (0, 6), (0, 0)))
            ops = (qkv, f_q, f_k)
            o, lse, o_delta = attention_forward("fox", ops, bl, fox_scale, f"fox_attn{i}")
            y = mm([(o, w_fo)], trans_b=False, out_dtype=F32, name=f"fox_out{i}")
            sv.update(z=z, ops=ops, o=o, lse=lse, o_delta=o_delta)
        x1, r1, u2 = residual_layer_norm(xin, y, mods[i], 2, ln_g_all[i, 0], ln_b_all[i, 0], bl, f"ln_mix{i}",
                                         next_mod=(3, 4))
        if wt_gate[i] is None:
            arrive(group_of[f"gate{i}"], u2)
        a, bb, h = swiglu_in(u2, wt_gate[i], wt_up[i], f"ffn_in{i}")
        y2 = mm([(h, w_down[i])], trans_b=False, out_dtype=F32, name=f"ffn_down{i}")
        sv.update(y=y, r1=r1, u2=u2, a=a, bb=bb, h=h, y2=y2)
        if i + 1 < DEPTH:
            xin, r2, u = residual_layer_norm(x1, y2, mods[i], 5, ln_g_all[i, 1], ln_b_all[i, 1], bl, f"ln_ffn{i}",
                                             next_mod=(0, 1, mods[i + 1]))
        else:
            xin, r2 = residual_layer_norm(x1, y2, mods[i], 5, ln_g_all[i, 1], ln_b_all[i, 1], bl, f"ln_ffn{i}")
        sv.update(r2=r2)
        saved.append(sv)

    loss_cols, d_x = loss_head(xin, loss_target.reshape(t, d), "loss_head")

    grads_full = {}
    wgrad = functools.partial(mm_tn, out_dtype=BF16)
    dmod = [[None] * 6 for _ in range(DEPTH)]
    dg_ln = [[None, None] for _ in range(DEPTH)]
    db_ln = [[None, None] for _ in range(DEPTH)]
    dg_q = dg_kv = db_f = None
    d_a, du = d_x, None
    scatter_started = [None] * len(groups)

    def scatter_start(gi, after=None):
        g = jnp.concatenate(
            [slot(nm, grads_full[nm].reshape(N_DEV, rows_of[nm], PACK_COLS).astype(BF16)) for nm in groups[gi]], axis=1)
        if after is not None:
            g = lax.optimization_barrier((g, after))[0]
        own = lax.dynamic_index_in_dim(g, dev, 0, keepdims=False)
        scatter_started[gi] = exchange_start(g, landing(own), f"scatter_group{gi}_start", True)

    ln_g_bwd = [[ln_g_all[i, k] for k in range(2)] for i in range(DEPTH)]
    for i in reversed(range(DEPTH)):
        sv = saved[i]
        if i + 1 < DEPTH:
            gi = group_of["fox_w_in"]
            scatter_start(gi)
            ln_g_bwd[i][1] = after_token(ln_g_bwd[i][1], scatter_started[gi])
        ln2 = (sv["r2"], sv["y2"], ln_g_bwd[i][1], ln_b_all[i, 1], (mods[i], 5))
        if du is None:
            bw = sublayer_backward(d_a, bl, f"bwd_ln_ffn{i}", ln=ln2)
        else:
            bw = sublayer_backward(d_a, bl, f"bwd_ln_ffn{i}", du=du, scale=(mods[i + 1], 1), ln=ln2)
            dmod[i + 1][0], dmod[i + 1][1] = bw["dshift"], bw["dscale"]
        dmod[i][5], dg_ln[i][1], db_ln[i][1] = bw["dgate"], bw["dg"], bw["db"]
        dy2 = bw["dy"]
        da, dbb = swiglu_out_backward(dy2, w_down[i], sv["a"], sv["bb"], f"bwd_ffn_act{i}")
        du2 = mm([(da, wt_gate[i]), (dbb, wt_up[i])], trans_b=False, out_dtype=F32, name=f"bwd_ffn_du{i}")
        grads_full[f"down{i}"] = wgrad(sv["h"], dy2, name=f"bwd_w_down{i}")
        grads_full[f"gate{i}"] = wgrad(da, sv["u2"], name=f"bwd_w_gate{i}")
        grads_full[f"up{i}"] = wgrad(dbb, sv["u2"], name=f"bwd_w_up{i}")
        gi = group_of[f"gate{i}"]
        scatter_start(gi)
        ln_g_bwd[i][0] = after_token(ln_g_bwd[i][0], scatter_started[gi])
        bw = sublayer_backward(bw["dx"], bl, f"bwd_ln_mix{i}", du=du2, scale=(mods[i], 4),
                               ln=(sv["r1"], sv["y"], ln_g_bwd[i][0], ln_b_all[i, 0], (mods[i], 2)))
        dmod[i][3], dmod[i][4], dmod[i][2] = bw["dshift"], bw["dscale"], bw["dgate"]
        dg_ln[i][0], db_ln[i][0] = bw["dg"], bw["db"]
        d_a, dy = bw["dx"], bw["dy"]
        o, lse, ops = sv["o"], sv["lse"], sv["ops"]
        if i % 2 == 0:
            do = mm([(dy, w_mo)], trans_b=True, out_dtype=BF16, out_slab=True, name=f"bwd_mla_do{i}")
            grads_full["mla_w_o"] = wgrad(o, dy, name=f"bwd_w_mla_o{i}")
            dqn, dqr, dkn, dvm, dkr = attention_backward("mla", ops, sv["o_delta"], do, lse, bl, mla_scale,
                                                         f"bwd_mla_attn{i}")
            dqr = rope_slabs(dqr, cos_t, sin_t, F32, f"bwd_mla_qrope{i}", transposed=True)
            dcq = mm([(dqn, wt_uq_n), (dqr, wt_uq_r)], trans_b=False, out_dtype=F32, name=f"bwd_mla_dcq{i}")
            dckv = mm([(dkn, wt_uk), (dvm, wt_uv)], trans_b=False, out_dtype=F32, name=f"bwd_mla_dckv{i}")
            d_uq_n = wgrad(dqn, sv["c_q"], name=f"bwd_w_uq_n{i}").reshape(MLA_HEADS, MLA_NOPE, MLA_QR)
            d_uq_r = wgrad(dqr, sv["c_q"], name=f"bwd_w_uq_r{i}").reshape(MLA_HEADS, MLA_ROPE, MLA_QR)
            grads_full["mla_w_uq"] = jnp.concatenate([d_uq_n, d_uq_r], axis=1)
            grads_full["mla_w_uk"] = wgrad(dkn, sv["c_kv"], name=f"bwd_w_uk{i}")
            grads_full["mla_w_uv"] = wgrad(dvm, sv["c_kv"], name=f"bwd_w_uv{i}")
            dh_in, dg_q, dg_kv = mla_latents_backward(sv["h_in"], dcq, dckv, dkr, g_q, g_kv, cos_t, sin_t,
                                                      f"bwd_mla_latents{i}")
            du = mm([(dh_in, w_in)], trans_b=True, out_dtype=F32, name=f"bwd_mla_du{i}")
            grads_full["mla_w_in"] = wgrad(sv["u"], dh_in, name=f"bwd_w_mla_in{i}")[:, :mla_in]
        else:
            do = mm([(dy, w_fo)], trans_b=True, out_dtype=BF16, out_slab=True, name=f"bwd_fox_do{i}")
            grads_full["fox_w_o"] = wgrad(o, dy, name=f"bwd_w_fox_o{i}")
            dq, dk, dvf, dfk = attention_backward("fox", ops, sv["o_delta"], do, lse, bl, fox_scale, f"bwd_fox_attn{i}")
            df = dfk[:, :, :2, :].reshape(bl, FOX_HEADS // 2, nk, 2, tq).transpose(0, 2, 4, 1, 3).reshape(t, FOX_HEADS)
            df = jnp.pad(df, ((0, 0), (0, LANES - FOX_HEADS)))
            dz, db_f = fox_gate_backward(sv["z"], b_f, df, bl, f"bwd_fox_gate{i}")
            du = mm([(dq, wt_fox[0:d]), (dk, wt_fox[d:2 * d]), (dvf, wt_fox[2 * d:3 * d]), (dz, wt_f)],
                    trans_b=False, out_dtype=F32, name=f"bwd_fox_du{i}")
            u_f = sv["u"]
            grads_full["fox_w_in"] = jnp.concatenate(
                [wgrad(dq, u_f, name=f"bwd_w_fox_q{i}"), wgrad(dk, u_f, name=f"bwd_w_fox_k{i}"),
                 wgrad(dvf, u_f, name=f"bwd_w_fox_v{i}"), wgrad(dz, u_f, name=f"bwd_w_fox_f{i}")[:FOX_HEADS]], axis=0)
    bw = sublayer_backward(d_a, bl, "bwd_input", du=du, scale=(mods[0], 1), x_in=x2d)
    dmod[0][0], dmod[0][1] = bw["dshift"], bw["dscale"]
    grad_x = bw["dx"].reshape(bl, s, d)

    dmod_rows = jnp.concatenate([r.reshape(bl, d) for layer in dmod for r in layer], axis=0)
    dmod_rows = dmod_rows.reshape(DEPTH, 6, bl, d).transpose(0, 2, 1, 3)
    n_mod = dmod_rows.size // LANES
    ln_parts = [dg_ln[i][k] for i in range(DEPTH) for k in range(2)] + [db_ln[i][k] for i in range(DEPTH) for k in range(2)]
    small_g = jnp.concatenate([dmod_rows.reshape(-1, LANES), dg_q.reshape(-1, LANES), dg_kv.reshape(-1, LANES), db_f]
                              + [p.reshape(-1, LANES) for p in ln_parts] + [loss_cols.reshape(-1, LANES)], axis=0)
    n_small = small_g.shape[0]
    small_g = jnp.pad(small_g, ((0, (-n_small) % 8), (0, 0)))
    small_g_all = all_gather(small_g, "gather_small_grads")
    scatter_start(0, after=small_g_all)
    small_g_all = lax.optimization_barrier((small_g_all, scatter_started[0][4]))[0]
    small_sum = sum_leading(small_g_all, "sum_small_grads")
    per_seq = DEPTH * 6 * d // LANES
    dmod_all = small_g_all[:, :n_mod].reshape(N_DEV, DEPTH, bl, 6 * d).transpose(1, 0, 2, 3)
    dmod_all = dmod_all.reshape(DEPTH, N_DEV * bl, 6 * d)
    o1 = n_mod
    grad_g_q = small_sum[o1:o1 + MLA_QR // LANES].reshape(1, MLA_QR)
    o1 += MLA_QR // LANES
    grad_g_kv = small_sum[o1:o1 + MLA_KVR // LANES].reshape(1, MLA_KVR)
    o1 += MLA_KVR // LANES
    grad_b_f = small_sum[o1:o1 + 1, :FOX_HEADS]
    o1 += 1
    n_ln_rows = DEPTH * 2 * d // LANES
    grad_ln_g_full = small_sum[o1:o1 + n_ln_rows].reshape(DEPTH, 2, d)
    grad_ln_b_full = small_sum[o1 + n_ln_rows:o1 + 2 * n_ln_rows].reshape(DEPTH, 2, d)
    loss = jnp.sum(small_sum[o1 + 2 * n_ln_rows:o1 + 2 * n_ln_rows + d // LANES])
    shard = d // N_DEV
    grad_ln_g = lax.dynamic_slice_in_dim(grad_ln_g_full, dev * shard, shard, axis=2)
    grad_ln_b = lax.dynamic_slice_in_dim(grad_ln_b_full, dev * shard, shard, axis=2)
    by_seq = small_g_all[:, :n_mod].reshape(N_DEV, DEPTH, bl, 6 * d // LANES, LANES).transpose(0, 2, 1, 3, 4)
    grad_ada_b = sum_leading(by_seq.reshape(N_DEV * bl, per_seq, LANES), "sum_ada_b").reshape(DEPTH, 6 * d)
    dmod_cols = lax.dynamic_slice_in_dim(dmod_all, dev * ada_cols, ada_cols, axis=2)
    grad_ada_w = jnp.stack([mm_tn(c_act, dmod_cols[i], name=f"bwd_w_ada{i}") for i in range(DEPTH)])

    g_mine = [None] * len(groups)

    def scatter_arrive(gi, after):
        landed = exchange_wait(scatter_started[gi], after, f"scatter_group{gi}_wait", True)
        g_mine[gi] = sum_leading(landed, f"scatter_group{gi}_sum")
        return g_mine[gi]

    after = bw["dx"]
    for gi in reversed(range(1, len(groups))):
        after = scatter_arrive(gi, after)

    def mine(nm, shape):
        return g_mine[group_of[nm]][offsets[nm]:offsets[nm] + rows_of[nm]].reshape(shape)

    def shard_t(nm, a):
        return mine(nm, t_last(a).shape)

    transposed = {"mla_w_uq", "mla_w_uk", "mla_w_uv", "fox_w_in", "ffn_w_gate", "ffn_w_up"}
    grads = {
        "mla_w_in": lambda: mine("mla_w_in", mla_w_in[0].shape)[None],
        "mla_g_q": lambda: grad_g_q,
        "mla_w_uq": lambda: shard_t("mla_w_uq", mla_w_uq[0])[None],
        "mla_g_kv": lambda: grad_g_kv,
        "mla_w_uk": lambda: shard_t("mla_w_uk", mla_w_uk[0])[None],
        "mla_w_uv": lambda: shard_t("mla_w_uv", mla_w_uv[0])[None],
        "mla_w_o": lambda: mine("mla_w_o", mla_w_o[0].shape)[None],
        "fox_w_in": lambda: shard_t("fox_w_in", fox_w_in[0])[None],
        "fox_b_f": lambda: grad_b_f,
        "fox_w_o": lambda: mine("fox_w_o", fox_w_o[0].shape)[None],
        "ada_w": lambda: grad_ada_w,
        "ada_b": lambda: grad_ada_b,
        "ffn_w_gate": lambda: jnp.stack([shard_t(f"gate{i}", ffn_w_gate[i]) for i in range(DEPTH)]),
        "ffn_w_up": lambda: jnp.stack([shard_t(f"up{i}", ffn_w_up[i]) for i in range(DEPTH)]),
        "ffn_w_down": lambda: jnp.stack([mine(f"down{i}", ffn_w_down[i].shape) for i in range(DEPTH)]),
        "ln_g": lambda: grad_ln_g,
        "ln_b": lambda: grad_ln_b,
    }
    weights = dict(mla_w_in=mla_w_in, mla_g_q=mla_g_q, mla_w_uq=mla_w_uq, mla_g_kv=mla_g_kv, mla_w_uk=mla_w_uk,
                   mla_w_uv=mla_w_uv, mla_w_o=mla_w_o, fox_w_in=fox_w_in, fox_b_f=fox_b_f, fox_w_o=fox_w_o,
                   ada_w=ada_w, ada_b=ada_b, ffn_w_gate=ffn_w_gate, ffn_w_up=ffn_w_up, ffn_w_down=ffn_w_down,
                   ln_g=ln_g, ln_b=ln_b)
    first = dict(mla_w_in=m_mla_w_in, mla_g_q=m_mla_g_q, mla_w_uq=m_mla_w_uq, mla_g_kv=m_mla_g_kv, mla_w_uk=m_mla_w_uk,
                 mla_w_uv=m_mla_w_uv, mla_w_o=m_mla_w_o, fox_w_in=m_fox_w_in, fox_b_f=m_fox_b_f, fox_w_o=m_fox_w_o,
                 ada_w=m_ada_w, ada_b=m_ada_b, ffn_w_gate=m_ffn_w_gate, ffn_w_up=m_ffn_w_up, ffn_w_down=m_ffn_w_down,
                 ln_g=m_ln_g, ln_b=m_ln_b)
    second = dict(mla_w_in=v_mla_w_in, mla_g_q=v_mla_g_q, mla_w_uq=v_mla_w_uq, mla_g_kv=v_mla_g_kv, mla_w_uk=v_mla_w_uk,
                  mla_w_uv=v_mla_w_uv, mla_w_o=v_mla_w_o, fox_w_in=v_fox_w_in, fox_b_f=v_fox_b_f, fox_w_o=v_fox_w_o,
                  ada_w=v_ada_w, ada_b=v_ada_b, ffn_w_gate=v_ffn_w_gate, ffn_w_up=v_ffn_w_up, ffn_w_down=v_ffn_w_down,
                  ln_g=v_ln_g, ln_b=v_ln_b)
    order = list(weights)
    last = [nm for nm in order if group_of.get(nm) == 0]
    updated = {}
    for nm in [nm for nm in order if nm not in last] + last:
        if last and nm == last[0]:
            scatter_arrive(0, after)
        lay = t_last if nm in transposed else (lambda a: a)
        w = lay(weights[nm])
        g = grads[nm]().reshape(w.shape)
        delta, new_m, new_v = adamw(w, g, lay(first[nm]), lay(second[nm]), f"adamw_{nm}")
        updated[nm] = (lay(g), lay(delta), lay(new_m), lay(new_v))
        after = new_v
    return (loss, grad_x, *(updated[nm][k] for k in range(4) for nm in order))
```

```python
import functools
import math

import jax
import jax.numpy as jnp
from jax import lax
from jax.experimental import pallas as pl
from jax.experimental.pallas import tpu as pltpu

F32 = jnp.float32
BF16 = jnp.bfloat16
LANES = 128
N_DEV = 8
VMEM_LIMIT_BYTES = 56 * 1024 * 1024

DEPTH = 2
MLA_HEADS = 8
MLA_NOPE = 128
MLA_ROPE = 64
MLA_V = 128
MLA_QR = 256
MLA_KVR = 256
ROPE_THETA = 10000.0
FOX_HEADS = 16
FOX_HD = 64
ALPHA = (2.0 * DEPTH) ** 0.25
NORM_EPS = 1e-5
ADAM_LR = 0.001
ADAM_B1 = 0.9
ADAM_B2 = 0.999
ADAM_EPS = 1e-08
ADAM_WD = 0.01
ADAM_STEP = 10

MESH_AXES = ("x", "y", "c")
MESH = pl.DeviceIdType.MESH


def _params(*sem):
    return pltpu.CompilerParams(dimension_semantics=sem, vmem_limit_bytes=VMEM_LIMIT_BYTES)


def _tile(n, cap, mult=LANES):
    if n <= cap:
        return n
    best = None
    for t in range(mult, cap + 1, mult):
        if n % t == 0:
            best = t
    assert best is not None, (n, cap, mult)
    return best


def _dot(a, b, dims):
    return lax.dot_general(a, b, (dims, ((), ())), preferred_element_type=F32)


def _nn(a, b):
    return _dot(a, b, ((1,), (0,)))


def _nt(a, b):
    return _dot(a, b, ((1,), (1,)))


def _tn(a, b):
    return _dot(a, b, ((0,), (0,)))


def _me():
    return lax.axis_index("x"), lax.axis_index("y"), lax.axis_index("c")


def all_gather(x_loc, name):
    r, c = x_loc.shape

    def body(x_ref, out_ref, send_sems, recv_sems, local_sem):
        x, y, cc = _me()
        me, sibling = (x, y, cc), (x, y, 1 - cc)
        chips = [(1 - x, y), (x, 1 - y), (1 - x, 1 - y)]

        def rows(px, py, pc):
            return out_ref.at[4 * px + 2 * py + pc]

        def copy(k, block, to, src=None):
            return pltpu.make_async_remote_copy(
                src_ref=rows(*block) if src is None else src, dst_ref=rows(*block),
                send_sem=send_sems.at[k], recv_sem=recv_sems.at[k], device_id=to, device_id_type=MESH)

        mine = pltpu.make_async_copy(x_ref, rows(*me), local_sem)
        mine.start()
        first = [copy(0, me, sibling, src=x_ref)]
        first += [copy(1 + j, me, (*chip, cc), src=x_ref) for j, chip in enumerate(chips)]
        for cp in first:
            cp.start()
        passed = [copy(4 + j, (*chip, cc), sibling) for j, chip in enumerate(chips)]
        for j, chip in enumerate(chips):
            copy(1 + j, (*chip, cc), me).wait_recv()
            passed[j].start()
        copy(0, sibling, me).wait_recv()
        for j, chip in enumerate(chips):
            copy(4 + j, (*chip, 1 - cc), me).wait_recv()
        for cp in first + passed:
            cp.wait_send()
        mine.wait()

    return pl.pallas_call(
        body, name=name,
        out_shape=jax.ShapeDtypeStruct((N_DEV, r, c), x_loc.dtype),
        in_specs=[pl.BlockSpec(memory_space=pl.ANY)],
        out_specs=pl.BlockSpec(memory_space=pl.ANY),
        scratch_shapes=[pltpu.SemaphoreType.DMA((7,)), pltpu.SemaphoreType.DMA((7,)), pltpu.SemaphoreType.DMA(())],
    )(x_loc)


HBM_SPEC = pl.BlockSpec(memory_space=pltpu.HBM)
SEM_SPEC = pl.BlockSpec(memory_space=pltpu.SEMAPHORE)
N_PEERS = N_DEV - 1


def _peer(k):
    x, y, c = _me()
    return (1 - x if k & 4 else x, 1 - y if k & 2 else y, 1 - c if k & 1 else c)


def _exchange_copies(src_ref, land_ref, send_sems, recv_sems, scatter):
    x, y, c = _me()
    mine = 4 * x + 2 * y + c
    copies = []
    for k in range(1, N_DEV):
        px, py, pc = _peer(k)
        src = src_ref.at[4 * px + 2 * py + pc] if scatter else src_ref
        copies.append(pltpu.make_async_remote_copy(
            src_ref=src, dst_ref=land_ref.at[mine], send_sem=send_sems.at[k - 1], recv_sem=recv_sems.at[k - 1],
            device_id=(px, py, pc), device_id_type=MESH))
    return copies


def exchange_start(src, land, name, scatter):
    def body(src_ref, land_ref, send_sems, recv_sems, src_thru, land_thru, token):
        for cp in _exchange_copies(src_ref, land_ref, send_sems, recv_sems, scatter):
            cp.start()
        token[...] = jnp.zeros_like(token)

    return pl.pallas_call(
        body, name=name,
        out_shape=(pltpu.SemaphoreType.DMA((N_PEERS,)), pltpu.SemaphoreType.DMA((N_PEERS,)),
                   pltpu.HBM(src.shape, src.dtype), pltpu.HBM(land.shape, land.dtype),
                   jax.ShapeDtypeStruct((8, LANES), F32)),
        in_specs=(HBM_SPEC, HBM_SPEC),
        out_specs=(SEM_SPEC, SEM_SPEC, HBM_SPEC, HBM_SPEC, pl.BlockSpec(memory_space=pltpu.VMEM)),
        input_output_aliases={0: 2, 1: 3},
        compiler_params=pltpu.CompilerParams(has_side_effects=pltpu.SideEffectType.DATAFLOW_SIDE_EFFECTING),
    )(pltpu.with_memory_space_constraint(src, pltpu.HBM), pltpu.with_memory_space_constraint(land, pltpu.HBM))


def exchange_wait(started, after, name, scatter):
    send_sems, recv_sems, src_thru, land_thru, _ = started

    def body(src_ref, land_ref, send_sems, recv_sems, after_ref, src_dead, got_ref):
        for cp in _exchange_copies(src_ref, land_ref, send_sems, recv_sems, scatter):
            cp.wait_send()
            cp.wait_recv()

    return pl.pallas_call(
        body, name=name,
        out_shape=(pltpu.HBM(src_thru.shape, src_thru.dtype), pltpu.HBM(land_thru.shape, land_thru.dtype)),
        in_specs=(HBM_SPEC, HBM_SPEC, SEM_SPEC, SEM_SPEC, pl.BlockSpec(memory_space=pl.ANY)),
        out_specs=(HBM_SPEC, HBM_SPEC), input_output_aliases={0: 0, 1: 1},
        compiler_params=pltpu.CompilerParams(has_side_effects=pltpu.SideEffectType.DATAFLOW_SIDE_EFFECTING),
    )(src_thru, land_thru, send_sems, recv_sems, after)[1]


def after_token(small, started):
    return small + started[4][0, 0]


def sum_leading(x, name):
    n, r, c = x.shape
    tr = _tile(r, 512, 16)

    def body(x_ref, o_ref):
        acc = x_ref[0].astype(F32)
        for k in range(1, n):
            acc = acc + x_ref[k].astype(F32)
        o_ref[...] = acc

    return pl.pallas_call(
        body, name=name,
        out_shape=jax.ShapeDtypeStruct((r, c), F32),
        grid=(r // tr,),
        in_specs=[pl.BlockSpec((n, tr, c), lambda i: (0, i, 0))],
        out_specs=pl.BlockSpec((tr, c), lambda i: (i, 0)),
        compiler_params=_params("arbitrary"),
    )(x)


MM_VMEM_BUDGET = 36 * 1024 * 1024
GRID_STEP_AS_BYTES = 1 << 20


def _mm_tiles(m, n, a_row_bytes, b_col_bytes, out_bytes):
    tms = [c for c in (2048, 1024, 512, 256, 128, 64, 32, 16, 8) if m % c == 0] or [m]
    tns = [c for c in range(LANES, min(n, 2048) + 1, LANES) if n % c == 0] or [n]
    best = None
    for tm in tms:
        for tn in tns:
            vmem = 2 * (tm * a_row_bytes + tn * b_col_bytes) + 2 * tm * tn * out_bytes + tm * tn * 4
            if vmem > MM_VMEM_BUDGET:
                continue
            steps = (m // tm) * (n // tn)
            cost = steps * GRID_STEP_AS_BYTES + (m // tm) * n * b_col_bytes + m * a_row_bytes
            if best is None or cost < best[0]:
                best = (cost, tm, tn)
    assert best is not None, (m, n, a_row_bytes, b_col_bytes)
    return best[1], best[2]


def mm(pairs, *, trans_b, out_dtype, name, out_slab=False, bias=None):
    a0 = pairs[0][0]
    m = a0.shape[1] if a0.ndim == 3 else a0.shape[0]
    n = pairs[0][1].shape[0] if trans_b else pairs[0][1].shape[1]
    a_row_bytes = sum((b.shape[1] if trans_b else b.shape[0]) * a.dtype.itemsize for a, b in pairs)
    b_col_bytes = sum((b.shape[1] if trans_b else b.shape[0]) * b.dtype.itemsize for _, b in pairs)
    tm, tn = _mm_tiles(m, n, a_row_bytes, b_col_bytes, jnp.dtype(out_dtype).itemsize)
    slabs = [a.ndim == 3 for a, _ in pairs]
    n_pairs = len(pairs)

    def body(*refs):
        o_ref = refs[-1]
        acc = bias_ref = None
        if bias is not None:
            bias_ref = refs[2 * n_pairs]
        for i in range(n_pairs):
            a_ref, b_ref = refs[2 * i], refs[2 * i + 1]
            if slabs[i]:
                a = jnp.concatenate([a_ref[s].astype(BF16) for s in range(a_ref.shape[0])], axis=1)
            else:
                a = a_ref[...].astype(BF16)
            b = b_ref[...].astype(BF16)
            part = _nt(a, b) if trans_b else _nn(a, b)
            acc = part if acc is None else acc + part
        if bias_ref is not None:
            acc = acc + bias_ref[...]
        if out_slab:
            for s in range(tn // LANES):
                o_ref[s] = acc[:, s * LANES:(s + 1) * LANES].astype(out_dtype)
        else:
            o_ref[...] = acc.astype(out_dtype)

    in_specs, args = [], []
    for (a, b), slab in zip(pairs, slabs):
        if slab:
            in_specs.append(pl.BlockSpec((a.shape[0], tm, LANES), lambda i, j: (0, i, 0)))
        else:
            in_specs.append(pl.BlockSpec((tm, a.shape[1]), lambda i, j: (i, 0)))
        if trans_b:
            in_specs.append(pl.BlockSpec((tn, b.shape[1]), lambda i, j: (j, 0)))
        else:
            in_specs.append(pl.BlockSpec((b.shape[0], tn), lambda i, j: (0, j)))
        args += [a, b]
    if bias is not None:
        in_specs.append(pl.BlockSpec((1, tn), lambda i, j: (0, j)))
        args.append(bias)
    if out_slab:
        out_shape = jax.ShapeDtypeStruct((n // LANES, m, LANES), out_dtype)
        out_spec = pl.BlockSpec((tn // LANES, tm, LANES), lambda i, j: (j, i, 0))
    else:
        out_shape = jax.ShapeDtypeStruct((m, n), out_dtype)
        out_spec = pl.BlockSpec((tm, tn), lambda i, j: (i, j))
    return pl.pallas_call(
        body, name=name, out_shape=out_shape, grid=(m // tm, n // tn),
        in_specs=in_specs, out_specs=out_spec,
        compiler_params=_params("arbitrary", "arbitrary"),
    )(*args)


def mm_tn(a, b, *, name, out_dtype=F32, tk_cap=1536, tn_cap=1024, tm_cap=512):
    slab = a.ndim == 3
    m = a.shape[1] if slab else a.shape[0]
    k = a.shape[0] * LANES if slab else a.shape[1]
    n = b.shape[1]
    tk = _tile(k, tk_cap)
    tn = _tile(n, tn_cap)
    tm = _tile(m, tm_cap, 8)
    n_steps = m // tm

    def body(a_ref, b_ref, o_ref, acc_ref):
        step = pl.program_id(2)

        @pl.when(step == 0)
        def _():
            acc_ref[...] = jnp.zeros_like(acc_ref)

        bb = b_ref[...].astype(BF16)
        if slab:
            for s in range(tk // LANES):
                acc_ref[s * LANES:(s + 1) * LANES, :] += _tn(a_ref[s].astype(BF16), bb)
        else:
            acc_ref[...] += _tn(a_ref[...].astype(BF16), bb)

        @pl.when(step == n_steps - 1)
        def _():
            o_ref[...] = acc_ref[...].astype(out_dtype)

    if slab:
        a_spec = pl.BlockSpec((tk // LANES, tm, LANES), lambda i, j, t: (i, t, 0))
    else:
        a_spec = pl.BlockSpec((tm, tk), lambda i, j, t: (t, i))
    return pl.pallas_call(
        body, name=name, out_shape=jax.ShapeDtypeStruct((k, n), out_dtype), grid=(k // tk, n // tn, n_steps),
        in_specs=[a_spec, pl.BlockSpec((tm, tn), lambda i, j, t: (t, j))],
        out_specs=pl.BlockSpec((tk, tn), lambda i, j, t: (i, j)),
        scratch_shapes=[pltpu.VMEM((tk, tn), F32)],
        compiler_params=_params("arbitrary", "arbitrary", "arbitrary"),
    )(a, b)


def _row_spec(d, k):
    return pl.BlockSpec((1, 1, d), lambda b, i: (6 * b + k, 0, 0))


def modulate(x, mod, k_shift, k_scale, bl, name):
    t, d = x.shape
    s = t // bl
    tm = _tile(s, 512, 8)
    nt = s // tm

    def body(x_ref, sh_ref, sc_ref, o_ref):
        o_ref[...] = (x_ref[...] * (1.0 + sc_ref[0]) + sh_ref[0]).astype(BF16)

    return pl.pallas_call(
        body, name=name, out_shape=jax.ShapeDtypeStruct((t, d), BF16), grid=(bl, nt),
        in_specs=[pl.BlockSpec((tm, d), lambda b, i: (b * nt + i, 0)), _row_spec(d, k_shift), _row_spec(d, k_scale)],
        out_specs=pl.BlockSpec((tm, d), lambda b, i: (b * nt + i, 0)),
        compiler_params=_params("arbitrary", "arbitrary"),
    )(x, mod, mod)


def _layer_norm_stats(r):
    mu = jnp.mean(r, axis=-1, keepdims=True)
    rc = r - mu
    var = jnp.mean(rc * rc, axis=-1, keepdims=True)
    rstd = lax.rsqrt(var + NORM_EPS)
    return rc * rstd, rstd


def residual_layer_norm(x, y, mod, k_gate, g, b, bl, name, next_mod=None):
    t, d = x.shape
    s = t // bl
    tm = _tile(s, 256, 8)
    nt = s // tm
    has_next = next_mod is not None

    def body(*refs):
        x_ref, y_ref, gt_ref, g_ref, b_ref = refs[:5]
        rest = refs[5:]
        if has_next:
            sh_ref, sc_ref, o_ref, r_ref, u_ref = rest
        else:
            o_ref, r_ref = rest
        r = ALPHA * x_ref[...] + (1.0 + gt_ref[0]) * y_ref[...]
        xhat, _ = _layer_norm_stats(r)
        out = xhat * g_ref[...] + b_ref[...]
        o_ref[...] = out
        r_ref[...] = r
        if has_next:
            u_ref[...] = (out * (1.0 + sc_ref[0]) + sh_ref[0]).astype(BF16)

    tok = pl.BlockSpec((tm, d), lambda bb, i: (bb * nt + i, 0))
    vec = pl.BlockSpec((1, d), lambda bb, i: (0, 0))
    in_specs = [tok, tok, _row_spec(d, k_gate), vec, vec]
    args = [x, y, mod, g, b]
    out_shape = [jax.ShapeDtypeStruct((t, d), F32), jax.ShapeDtypeStruct((t, d), F32)]
    out_specs = [tok, tok]
    if has_next:
        in_specs += [_row_spec(d, next_mod[0]), _row_spec(d, next_mod[1])]
        args += [mod if len(next_mod) == 2 else next_mod[2]] * 2
        out_shape.append(jax.ShapeDtypeStruct((t, d), BF16))
        out_specs.append(tok)
    return pl.pallas_call(
        body, name=name, out_shape=out_shape, grid=(bl, nt), in_specs=in_specs, out_specs=out_specs,
        compiler_params=_params("arbitrary", "arbitrary"),
    )(*args)


def loss_head(xo, target, name):
    t, d = xo.shape
    tm = _tile(t, 512, 8)

    def body(x_ref, t_ref, l_ref, dx_ref):
        @pl.when(pl.program_id(0) == 0)
        def _():
            l_ref[...] = jnp.zeros_like(l_ref)

        e = x_ref[...] - t_ref[...]
        l_ref[...] += jnp.sum(e * e, axis=0, keepdims=True) * (0.5 / d)
        dx_ref[...] = e * (1.0 / d)

    tok = pl.BlockSpec((tm, d), lambda i: (i, 0))
    return pl.pallas_call(
        body, name=name,
        out_shape=[jax.ShapeDtypeStruct((1, d), F32), jax.ShapeDtypeStruct((t, d), F32)],
        grid=(t // tm,), in_specs=[tok, tok],
        out_specs=[pl.BlockSpec((1, d), lambda i: (0, 0)), tok],
        compiler_params=_params("arbitrary"),
    )(xo, target)


def sublayer_backward(d_a, bl, name, *, du=None, scale=None, x_in=None, ln=None):
    t, d = d_a.shape
    s = t // bl
    tm = _tile(s, 256, 8)
    nt = s // tm
    has_mod = du is not None
    has_ln = ln is not None
    assert has_mod or has_ln
    assert has_ln or x_in is not None

    def body(*refs):
        refs = list(refs)
        da_ref = refs.pop(0)
        if has_mod:
            du_ref, sc_ref = refs.pop(0), refs.pop(0)
        if has_ln:
            r_ref, y_ref, g_ref, b_ref, gt_ref = (refs.pop(0) for _ in range(5))
        elif has_mod:
            xin_ref = refs.pop(0)
        dx_ref = refs.pop(0)
        if has_ln:
            dy_ref, dg_ref, db_ref, dgt_ref = (refs.pop(0) for _ in range(4))
        if has_mod:
            dsc_ref, dsh_ref = refs.pop(0), refs.pop(0)
        first_tile = pl.program_id(1) == 0
        first_step = jnp.logical_and(pl.program_id(0) == 0, first_tile)

        dout = da_ref[...]
        if has_ln:
            xhat, rstd = _layer_norm_stats(r_ref[...])
        if has_mod:
            duv = du_ref[...]
            dout = dout + duv * (1.0 + sc_ref[0])
            xin = xhat * g_ref[...] + b_ref[...] if has_ln else xin_ref[...]

            @pl.when(first_tile)
            def _():
                dsc_ref[...] = jnp.zeros_like(dsc_ref)
                dsh_ref[...] = jnp.zeros_like(dsh_ref)

            dsc_ref[0] += jnp.sum(duv * xin, axis=0, keepdims=True)
            dsh_ref[0] += jnp.sum(duv, axis=0, keepdims=True)
        if not has_ln:
            dx_ref[...] = dout
            return

        @pl.when(first_step)
        def _():
            dg_ref[...] = jnp.zeros_like(dg_ref)
            db_ref[...] = jnp.zeros_like(db_ref)

        @pl.when(first_tile)
        def _():
            dgt_ref[...] = jnp.zeros_like(dgt_ref)

        dg_ref[...] += jnp.sum(dout * xhat, axis=0, keepdims=True)
        db_ref[...] += jnp.sum(dout, axis=0, keepdims=True)
        dxh = dout * g_ref[...]
        dr = rstd * (dxh - jnp.mean(dxh, axis=-1, keepdims=True) - xhat * jnp.mean(dxh * xhat, axis=-1, keepdims=True))
        dx_ref[...] = ALPHA * dr
        dy_ref[...] = ((1.0 + gt_ref[0]) * dr).astype(BF16)
        dgt_ref[0] += jnp.sum(dr * y_ref[...], axis=0, keepdims=True)

    tok = pl.BlockSpec((tm, d), lambda bb, i: (bb * nt + i, 0))
    vec = pl.BlockSpec((1, d), lambda bb, i: (0, 0))
    seq = pl.BlockSpec((1, 1, d), lambda bb, i: (bb, 0, 0))
    in_specs, args = [tok], [d_a]
    if has_mod:
        in_specs += [tok, _row_spec(d, scale[1])]
        args += [du, scale[0]]
    if has_ln:
        r, y, g, b, gate = ln
        in_specs += [tok, tok, vec, vec, _row_spec(d, gate[1])]
        args += [r, y, g, b, gate[0]]
    elif has_mod:
        in_specs.append(tok)
        args.append(x_in)
    names = ["dx"]
    out_shape, out_specs = [jax.ShapeDtypeStruct((t, d), F32)], [tok]
    if has_ln:
        names += ["dy", "dg", "db", "dgate"]
        out_shape += [jax.ShapeDtypeStruct((t, d), BF16), jax.ShapeDtypeStruct((1, d), F32),
                      jax.ShapeDtypeStruct((1, d), F32), jax.ShapeDtypeStruct((bl, 1, d), F32)]
        out_specs += [tok, vec, vec, seq]
    if has_mod:
        names += ["dscale", "dshift"]
        out_shape += [jax.ShapeDtypeStruct((bl, 1, d), F32)] * 2
        out_specs += [seq, seq]
    outs = pl.pallas_call(
        body, name=name, out_shape=out_shape, grid=(bl, nt), in_specs=in_specs, out_specs=out_specs,
        compiler_params=_params("arbitrary", "arbitrary"),
    )(*args)
    return dict(zip(names, outs))


def _silu(a):
    return a * jax.nn.sigmoid(a)


def silu_rows(a, name):
    def body(a_ref, o_ref):
        o_ref[...] = _silu(a_ref[...]).astype(BF16)

    return pl.pallas_call(body, name=name, out_shape=jax.ShapeDtypeStruct(a.shape, BF16))(a)


def _swiglu_tiles(t, f):
    return _tile(t, 512, 8), _tile(f, 1536)


def swiglu_in(u, wt_gate, wt_up, name):
    t, d = u.shape
    f = wt_gate.shape[0]
    tm, tf = _swiglu_tiles(t, f)

    def body(u_ref, g_ref, w_ref, a_ref, b_ref, h_ref):
        uv = u_ref[...]
        a = _nt(uv, g_ref[...])
        b = _nt(uv, w_ref[...])
        a_ref[...] = a.astype(BF16)
        b_ref[...] = b.astype(BF16)
        h_ref[...] = (_silu(a) * b).astype(BF16)

    w_spec = pl.BlockSpec((tf, d), lambda i, j: (j, 0))
    o_spec = pl.BlockSpec((tm, tf), lambda i, j: (i, j))
    return pl.pallas_call(
        body, name=name,
        out_shape=[jax.ShapeDtypeStruct((t, f), BF16)] * 3,
        grid=(t // tm, f // tf), in_specs=[pl.BlockSpec((tm, d), lambda i, j: (i, 0)), w_spec, w_spec],
        out_specs=[o_spec, o_spec, o_spec], compiler_params=_params("arbitrary", "arbitrary"),
    )(u, wt_gate, wt_up)


def swiglu_out_backward(dy, w_down, a, b, name):
    t, d = dy.shape
    f = w_down.shape[0]
    tm, tf = _swiglu_tiles(t, f)

    def body(dy_ref, w_ref, a_ref, b_ref, da_ref, db_ref):
        dh = _nt(dy_ref[...], w_ref[...])
        av = a_ref[...].astype(F32)
        sig = jax.nn.sigmoid(av)
        da_ref[...] = (dh * b_ref[...].astype(F32) * (sig * (1.0 + av * (1.0 - sig)))).astype(BF16)
        db_ref[...] = (dh * (av * sig)).astype(BF16)

    spec = pl.BlockSpec((tm, tf), lambda i, j: (i, j))
    return pl.pallas_call(
        body, name=name, out_shape=[jax.ShapeDtypeStruct((t, f), BF16)] * 2, grid=(t // tm, f // tf),
        in_specs=[pl.BlockSpec((tm, d), lambda i, j: (i, 0)), pl.BlockSpec((tf, d), lambda i, j: (j, 0)), spec, spec],
        out_specs=[spec, spec], compiler_params=_params("arbitrary", "arbitrary"),
    )(dy, w_down, a, b)


def rope_tables(pos, inv_freq, sign, name):
    t = pos.shape[0]
    tm = _tile(t, 512, 8)

    def body(p_ref, f_ref, s_ref, c_out, s_out):
        ang = p_ref[...] * f_ref[...]
        c_out[...] = jnp.cos(ang)
        s_out[...] = jnp.sin(ang) * s_ref[...]

    vec = pl.BlockSpec((1, LANES), lambda i: (0, 0))
    tab = pl.BlockSpec((tm, LANES), lambda i: (i, 0))
    return pl.pallas_call(
        body, name=name, out_shape=[jax.ShapeDtypeStruct((t, LANES), F32)] * 2, grid=(t // tm,),
        in_specs=[pl.BlockSpec((tm, 1), lambda i: (i, 0)), vec, vec], out_specs=[tab, tab],
        compiler_params=_params("arbitrary"),
    )(pos, inv_freq, sign)


def _rot_half(v):
    lane = lax.broadcasted_iota(jnp.int32, v.shape, v.ndim - 1)
    up = pltpu.roll(v, LANES - MLA_ROPE // 2, v.ndim - 1)
    down = pltpu.roll(v, MLA_ROPE // 2, v.ndim - 1)
    return jnp.where(lane % MLA_ROPE < MLA_ROPE // 2, up, down)


def _rope(v, cos, sin_signed):
    return v * cos + _rot_half(v) * sin_signed


def _rope_transposed(dv, cos, sin_signed):
    return dv * cos + _rot_half(dv * sin_signed)


def rope_slabs(v, cos, sin_signed, out_dtype, name, transposed=False):
    ns, t, _ = v.shape
    tm = _tile(t, 512, 8)
    fn = _rope_transposed if transposed else _rope

    def body(v_ref, c_ref, s_ref, o_ref):
        o_ref[0] = fn(v_ref[0].astype(F32), c_ref[...], s_ref[...]).astype(out_dtype)

    tab = pl.BlockSpec((tm, LANES), lambda j, i: (i, 0))
    spec = pl.BlockSpec((1, tm, LANES), lambda j, i: (j, i, 0))
    return pl.pallas_call(
        body, name=name, out_shape=jax.ShapeDtypeStruct(v.shape, out_dtype), grid=(ns, t // tm),
        in_specs=[spec, tab, tab], out_specs=spec, compiler_params=_params("arbitrary", "arbitrary"),
    )(v, cos, sin_signed)


def _rms(x):
    rinv = lax.rsqrt(jnp.mean(x * x, axis=-1, keepdims=True) + NORM_EPS)
    return x * rinv, rinv


def mla_latents_forward(h_in, g_q, g_kv, cos, sin_signed, name):
    t = h_in.shape[0]
    tm = _tile(t, 512, 8)

    def body(h_ref, gq_ref, gkv_ref, c_ref, s_ref, cq_ref, ckv_ref, kr_ref):
        cq_ref[...] = (_rms(h_ref[:, 0:MLA_QR])[0] * gq_ref[...]).astype(BF16)
        ckv_ref[...] = (_rms(h_ref[:, MLA_QR:MLA_QR + MLA_KVR])[0] * gkv_ref[...]).astype(BF16)
        kr_ref[...] = _rope(h_ref[:, MLA_QR + MLA_KVR:], c_ref[...], s_ref[...]).astype(BF16)

    def tok(w):
        return pl.BlockSpec((tm, w), lambda i: (i, 0))

    def vec(w):
        return pl.BlockSpec((1, w), lambda i: (0, 0))

    return pl.pallas_call(
        body, name=name,
        out_shape=[jax.ShapeDtypeStruct((t, MLA_QR), BF16), jax.ShapeDtypeStruct((t, MLA_KVR), BF16),
                   jax.ShapeDtypeStruct((t, LANES), BF16)],
        grid=(t // tm,),
        in_specs=[tok(h_in.shape[1]), vec(MLA_QR), vec(MLA_KVR), tok(LANES), tok(LANES)],
        out_specs=[tok(MLA_QR), tok(MLA_KVR), tok(LANES)],
        compiler_params=_params("arbitrary"),
    )(h_in, g_q, g_kv, cos, sin_signed)


def mla_latents_backward(h_in, dcq, dckv, dkr, g_q, g_kv, cos, sin_signed, name):
    t, w = h_in.shape
    tm = _tile(t, 512, 8)

    def body(h_ref, dcq_ref, dckv_ref, dkr_ref, gq_ref, gkv_ref, c_ref, s_ref, dh_ref, dgq_ref, dgkv_ref):
        @pl.when(pl.program_id(0) == 0)
        def _():
            dgq_ref[...] = jnp.zeros_like(dgq_ref)
            dgkv_ref[...] = jnp.zeros_like(dgkv_ref)

        def rms_bwd(x, dc, g_ref, dg_ref):
            xn, rinv = _rms(x)
            dg_ref[...] += jnp.sum(dc * xn, axis=0, keepdims=True)
            dxn = dc * g_ref[...]
            return rinv * (dxn - xn * jnp.mean(dxn * xn, axis=-1, keepdims=True))

        dq = rms_bwd(h_ref[:, 0:MLA_QR], dcq_ref[...], gq_ref, dgq_ref)
        dkv = rms_bwd(h_ref[:, MLA_QR:MLA_QR + MLA_KVR], dckv_ref[...], gkv_ref, dgkv_ref)
        dr = _rope_transposed(dkr_ref[...], c_ref[...], s_ref[...])
        dh_ref[...] = jnp.concatenate([dq, dkv, dr], axis=1).astype(BF16)

    def tok(ww):
        return pl.BlockSpec((tm, ww), lambda i: (i, 0))

    def vec(ww):
        return pl.BlockSpec((1, ww), lambda i: (0, 0))

    return pl.pallas_call(
        body, name=name,
        out_shape=[jax.ShapeDtypeStruct((t, w), BF16), jax.ShapeDtypeStruct((1, MLA_QR), F32),
                   jax.ShapeDtypeStruct((1, MLA_KVR), F32)],
        grid=(t // tm,),
        in_specs=[tok(w), tok(MLA_QR), tok(MLA_KVR), tok(LANES), vec(MLA_QR), vec(MLA_KVR), tok(LANES), tok(LANES)],
        out_specs=[tok(w), vec(MLA_QR), vec(MLA_KVR)],
        compiler_params=_params("arbitrary"),
    )(h_in, dcq, dckv, dkr, g_q, g_kv, cos, sin_signed)


def _tri(n, lower):
    r = lax.broadcasted_iota(jnp.int32, (n, n), 0)
    c = lax.broadcasted_iota(jnp.int32, (n, n), 1)
    return jnp.where(r >= c if lower else r <= c, 1.0, 0.0).astype(F32)


def _dot_exact(tri, v):
    hi = v.astype(BF16)
    mid = (v - hi.astype(F32)).astype(BF16)
    lo = (v - hi.astype(F32) - mid.astype(F32)).astype(BF16)
    t = tri.astype(BF16)
    return _nn(t, hi) + _nn(t, mid) + _nn(t, lo)


def fox_gate_forward(z, b_f, bl, name):
    t = z.shape[0]
    s = t // bl
    ch = LANES
    n_ch = s // ch

    def body(z_ref, b_ref, f_ref, fs_ref):
        tri = _tri(ch, True)
        carry = jnp.zeros((1, LANES), F32)
        for k in range(n_ch):
            x = z_ref[k * ch:(k + 1) * ch, :] + b_ref[...]
            logf = jnp.minimum(x, 0.0) - jnp.log(1.0 + jnp.exp(-jnp.abs(x)))
            cs = _dot_exact(tri, logf) + carry
            carry = cs[ch - 1:ch, :]
            f_ref[k * ch:(k + 1) * ch, :] = cs
            for h in range(FOX_HEADS):
                fs_ref[h, k * ch:(k + 1) * ch, :] = jnp.broadcast_to(cs[:, h:h + 1], (ch, LANES))

    return pl.pallas_call(
        body, name=name,
        out_shape=[jax.ShapeDtypeStruct((t, LANES), F32), jax.ShapeDtypeStruct((FOX_HEADS, t, LANES), F32)],
        grid=(bl,),
        in_specs=[pl.BlockSpec((s, LANES), lambda b: (b, 0)), pl.BlockSpec((1, LANES), lambda b: (0, 0))],
        out_specs=[pl.BlockSpec((s, LANES), lambda b: (b, 0)),
                   pl.BlockSpec((FOX_HEADS, s, LANES), lambda b: (0, b, 0))],
        compiler_params=_params("arbitrary"),
    )(z, b_f)


def fox_gate_backward(z, b_f, df, bl, name):
    t = z.shape[0]
    s = t // bl
    ch = LANES
    n_ch = s // ch

    def body(z_ref, b_ref, df_ref, dz_ref, db_ref):
        @pl.when(pl.program_id(0) == 0)
        def _():
            db_ref[...] = jnp.zeros_like(db_ref)

        tri = _tri(ch, False)
        carry = jnp.zeros((1, LANES), F32)
        for k in reversed(range(n_ch)):
            cs = _dot_exact(tri, df_ref[k * ch:(k + 1) * ch, :]) + carry
            carry = cs[0:1, :]
            x = z_ref[k * ch:(k + 1) * ch, :] + b_ref[...]
            dz = cs * (1.0 - jax.nn.sigmoid(x))
            dz_ref[k * ch:(k + 1) * ch, :] = dz
            db_ref[...] += jnp.sum(dz, axis=0, keepdims=True)

    tok = pl.BlockSpec((s, LANES), lambda b: (b, 0))
    vec = pl.BlockSpec((1, LANES), lambda b: (0, 0))
    return pl.pallas_call(
        body, name=name,
        out_shape=[jax.ShapeDtypeStruct((t, LANES), F32), jax.ShapeDtypeStruct((1, LANES), F32)],
        grid=(bl,), in_specs=[tok, vec, tok], out_specs=[tok, vec],
        compiler_params=_params("arbitrary"),
    )(z, b_f, df)


NEG_INF = float("-inf")


def _attn_tiles(s):
    return _tile(s, 512, 8)


def attention_forward(kind, ops, bl, scale, name):
    fox = kind == "fox"
    if fox:
        assert math.frexp(scale)[0] == 0.5, "the FoX scale is folded into bf16 queries: it must be a power of two"
        qkv, fq, fk = ops
        t = qkv.shape[1]
        n_pair = FOX_HEADS // 2
    else:
        qn, qr, kn, kr, v = ops
        t = qn.shape[1]
        n_pair = MLA_HEADS // 2
    s = t // bl
    tq = _attn_tiles(s)
    nq = s // tq
    half = LANES // 2

    def body(*refs):
        if fox:
            q_ref, k_ref, v_ref, fq_ref, fk_ref, o_ref, lse_ref, o32_ref = refs
        else:
            qn_ref, qr_ref, kn_ref, kr_ref, v_ref, o_ref, lse_ref = refs
        i = pl.program_id(2)
        row = lax.broadcasted_iota(jnp.int32, (tq, tq), 0)
        col = lax.broadcasted_iota(jnp.int32, (tq, tq), 1)
        heads = []
        for e in range(2):
            sl = slice(e * half, (e + 1) * half)
            if fox:
                heads.append((sl, q_ref[0, :, sl] * jnp.asarray(scale, BF16), None))
            else:
                heads.append((sl, qn_ref[e], qr_ref[0, :, sl]))
        dv = half if fox else LANES

        def wide(stat):
            return jnp.concatenate([stat] * (tq // LANES), axis=1)

        def step(j, carry, masked):
            rows = pl.ds(pl.multiple_of(j * tq, tq), tq)
            new = []
            for e, (sl, qa, qb) in enumerate(heads):
                m, l, acc = carry[e]
                if fox:
                    sc = _nt(qa, k_ref[0, rows, sl]) + wide(fq_ref[e]) - fk_ref[0, j, e:e + 1, :]
                    vv = v_ref[0, rows, sl]
                else:
                    sc = (_nt(qa, kn_ref[e, rows, :]) + _nt(qb, kr_ref[rows, 0:half])) * scale
                    vv = v_ref[e, rows, :]
                if masked:
                    sc = jnp.where(row >= col, sc, NEG_INF)
                m_new = jnp.maximum(m, jnp.max(sc, axis=1, keepdims=True))
                p = jnp.exp(sc - m_new)
                a = jnp.exp(m - m_new)
                l = a * l + jnp.sum(p, axis=1, keepdims=True)
                p_hi = p.astype(BF16)
                acc = a * acc + _nn(p_hi, vv)
                if fox:
                    acc = acc + _nn((p - p_hi.astype(F32)).astype(BF16), vv)
                new.append((m_new, l, acc))
            return tuple(new)

        init = (jnp.full((tq, 1), NEG_INF, F32), jnp.zeros((tq, 1), F32), jnp.zeros((tq, dv), F32))
        carry = step(i, (init, init), True)
        carry = lax.fori_loop(0, i, lambda j, c: step(j, c, False), carry)
        outs = [acc / l for _, l, acc in carry]
        for e, (m, l, _) in enumerate(carry):
            lse_ref[e] = jnp.broadcast_to(m + jnp.log(l), (tq, LANES))
        if fox:
            o32 = jnp.concatenate(outs, axis=1)
            o32_ref[0] = o32
            o_ref[0] = o32.astype(BF16)
        else:
            o_ref[0] = outs[0].astype(BF16)
            o_ref[1] = outs[1].astype(BF16)

    def q_idx(b, g, i):
        return (g, b * nq + i, 0)

    if fox:
        nk = fk.shape[1]
        in_specs = [pl.BlockSpec((1, tq, LANES), q_idx),
                    pl.BlockSpec((1, s, LANES), lambda b, g, i: (n_pair + g, b, 0)),
                    pl.BlockSpec((1, s, LANES), lambda b, g, i: (2 * n_pair + g, b, 0)),
                    pl.BlockSpec((2, tq, LANES), q_idx),
                    pl.BlockSpec((1, nk, 8, tq), lambda b, g, i: (b * n_pair + g, 0, 0, 0))]
        args = [qkv, qkv, qkv, fq, fk]
        o_spec = pl.BlockSpec((1, tq, LANES), q_idx)
    else:
        in_specs = [pl.BlockSpec((2, tq, LANES), q_idx),
                    pl.BlockSpec((1, tq, LANES), q_idx),
                    pl.BlockSpec((2, s, LANES), lambda b, g, i: (g, b, 0)),
                    pl.BlockSpec((s, LANES), lambda b, g, i: (b, 0)),
                    pl.BlockSpec((2, s, LANES), lambda b, g, i: (g, b, 0))]
        args = [qn, qr, kn, kr, v]
        o_spec = pl.BlockSpec((2, tq, LANES), q_idx)
    out_shape = [jax.ShapeDtypeStruct((8, t, LANES), BF16), jax.ShapeDtypeStruct((2 * n_pair, t, LANES), F32)]
    out_specs = [o_spec, pl.BlockSpec((2, tq, LANES), q_idx)]
    if fox:
        out_shape.append(jax.ShapeDtypeStruct((8, t, LANES), F32))
        out_specs.append(o_spec)
    outs = pl.pallas_call(
        body, name=name, out_shape=out_shape, grid=(bl, n_pair, nq), in_specs=in_specs, out_specs=out_specs,
        compiler_params=_params("arbitrary", "arbitrary", "arbitrary"),
    )(*args)
    return (outs[0], outs[1], outs[2] if fox else outs[0])


def attention_backward(kind, ops, o, do, lse, bl, scale, name):
    fox = kind == "fox"
    if fox:
        qkv, fq, fk = ops
        t = qkv.shape[1]
        n_pair = FOX_HEADS // 2
    else:
        qn, qr, kn, kr, v = ops
        t = qn.shape[1]
        n_pair = MLA_HEADS // 2
    s = t // bl
    tq = _attn_tiles(s)
    nq = s // tq
    half = LANES // 2

    def body(*refs):
        if fox:
            (q_ref, k_ref, v_ref, fq_ref, fk_ref, o_ref, do_ref, lse_ref,
             dq_ref, dk_ref, dv_ref, dfk_ref, delta_scr, qt_scr, dot_scr) = refs
        else:
            (qn_ref, qr_ref, kn_ref, kr_ref, v_ref, o_ref, do_ref, lse_ref,
             dqn_ref, dqr_ref, dkn_ref, dv_ref, dkr_ref, delta_scr, qt_scr, qrt_scr, dot_scr) = refs
        g, j = pl.program_id(1), pl.program_id(2)
        row = lax.broadcasted_iota(jnp.int32, (tq, tq), 0)
        col = lax.broadcasted_iota(jnp.int32, (tq, tq), 1)
        krows = pl.ds(pl.multiple_of(j * tq, tq), tq)

        def transposed(v):
            return v.astype(F32).T.astype(BF16)

        def wide(stat):
            return jnp.concatenate([stat] * (tq // LANES), axis=1)

        @pl.when(j == 0)
        def _():
            if fox:
                dq_ref[...] = jnp.zeros_like(dq_ref)
            else:
                dqn_ref[...] = jnp.zeros_like(dqn_ref)
                dqr_ref[...] = jnp.zeros_like(dqr_ref)
            for ii in range(nq):
                rws = slice(ii * tq, (ii + 1) * tq)
                deltas = []
                if fox:
                    prod = do_ref[0, rws, :].astype(F32) * o_ref[0, rws, :].astype(F32)
                    for e in range(2):
                        deltas.append(jnp.sum(prod[:, e * half:(e + 1) * half], axis=1, keepdims=True))
                    qt_scr[ii] = transposed(q_ref[0, rws, :])
                    dot_scr[ii] = transposed(do_ref[0, rws, :])
                else:
                    for e in range(2):
                        prod = do_ref[e, rws, :].astype(F32) * o_ref[e, rws, :].astype(F32)
                        deltas.append(jnp.sum(prod, axis=1, keepdims=True))
                        qt_scr[e, ii] = transposed(qn_ref[e, rws, :])
                        dot_scr[e, ii] = transposed(do_ref[e, rws, :])
                    qrt_scr[ii] = transposed(qr_ref[0, rws, :])
                for e in range(2):
                    delta_scr[e, rws, :] = jnp.broadcast_to(deltas[e], (tq, LANES))

        if fox:
            dfk_ref[...] = jnp.zeros_like(dfk_ref)
        else:
            @pl.when(jnp.logical_and(g == 0, j == 0))
            def _():
                dkr_ref[...] = jnp.zeros_like(dkr_ref)

        heads = []
        for e in range(2):
            sl = slice(e * half, (e + 1) * half)
            if fox:
                heads.append((sl, k_ref[0, :, sl], v_ref[0, :, sl], fk_ref[0, 0, e:e + 1, :]))
            else:
                heads.append((sl, kn_ref[e], v_ref[e], kr_ref[krows, 0:half]))
        dk_w = dv_w = half if fox else LANES

        def step(i, carry, masked):
            rows = pl.ds(pl.multiple_of(i * tq, tq), tq)
            new = []
            for e, (sl, k_e, v_e, x_e) in enumerate(heads):
                dk_acc, dv_acc, last = carry[e]
                if fox:
                    do_i = do_ref[0, rows, sl]
                    sc = _nt(q_ref[0, rows, sl], k_e) * scale + wide(fq_ref[e, rows, :]) - x_e
                else:
                    do_i = do_ref[e, rows, :]
                    sc = (_nt(qn_ref[e, rows, :], k_e) + _nt(qr_ref[0, rows, sl], x_e)) * scale
                if masked:
                    sc = jnp.where(row >= col, sc, NEG_INF)
                p = jnp.exp(sc - wide(lse_ref[e, rows, :]))
                dp = _nt(do_i, v_e)
                ds = p * (dp - wide(delta_scr[e, rows, :]))
                dsb = (ds * scale).astype(BF16)
                if fox:
                    fsl = slice(e * half, (e + 1) * half)
                    dv_acc = dv_acc + _nn(dot_scr[i, fsl, :], p.astype(BF16))
                    dk_acc = dk_acc + _nn(qt_scr[i, fsl, :], dsb)
                    dq_ref[0, rows, sl] += _nn(dsb, k_e)
                    last = last - jnp.sum(ds, axis=0, keepdims=True)
                else:
                    dv_acc = dv_acc + _nn(dot_scr[e, i], p.astype(BF16))
                    dk_acc = dk_acc + _nn(qt_scr[e, i], dsb)
                    dqn_ref[e, rows, :] += _nn(dsb, k_e)
                    dqr_ref[0, rows, sl] += _nn(dsb, x_e)
                    last = last + _nn(qrt_scr[i, e * half:(e + 1) * half, :], dsb)
                new.append((dk_acc, dv_acc, last))
            return tuple(new)

        last0 = jnp.zeros((1, tq), F32) if fox else jnp.zeros((half, tq), F32)
        init = (jnp.zeros((dk_w, tq), F32), jnp.zeros((dv_w, tq), F32), last0)
        carry = step(j, (init, init), True)
        carry = lax.fori_loop(j + 1, nq, lambda i, c: step(i, c, False), carry)
        if fox:
            for e in range(2):
                dfk_ref[0, 0, e:e + 1, :] = carry[e][2]
            dk_ref[0] = jnp.concatenate([carry[0][0], carry[1][0]], axis=0).T.astype(BF16)
            dv_ref[0] = jnp.concatenate([carry[0][1], carry[1][1]], axis=0).T.astype(BF16)
        else:
            for e in range(2):
                dkn_ref[e] = carry[e][0].T.astype(BF16)
                dv_ref[e] = carry[e][1].T.astype(BF16)
            dkr_t = carry[0][2] + carry[1][2]
            dkr_ref[krows, :] += jnp.concatenate([dkr_t, jnp.zeros_like(dkr_t)], axis=0).T

    def whole(b, g, j):
        return (g, b, 0)

    def kblk(b, g, j):
        return (g, b * nq + j, 0)

    if fox:
        in_specs = [pl.BlockSpec((1, s, LANES), whole),
                    pl.BlockSpec((1, tq, LANES), lambda b, g, j: (n_pair + g, b * nq + j, 0)),
                    pl.BlockSpec((1, tq, LANES), lambda b, g, j: (2 * n_pair + g, b * nq + j, 0)),
                    pl.BlockSpec((2, s, LANES), whole),
                    pl.BlockSpec((1, 1, 8, tq), lambda b, g, j: (b * n_pair + g, j, 0, 0)),
                    pl.BlockSpec((1, s, LANES), whole), pl.BlockSpec((1, s, LANES), whole),
                    pl.BlockSpec((2, s, LANES), whole)]
        args = [qkv, qkv, qkv, fq, fk, o, do, lse]
        out_shape = [jax.ShapeDtypeStruct((8, t, LANES), F32), jax.ShapeDtypeStruct((8, t, LANES), BF16),
                     jax.ShapeDtypeStruct((8, t, LANES), BF16), jax.ShapeDtypeStruct(fk.shape, F32)]
        out_specs = [pl.BlockSpec((1, s, LANES), whole), pl.BlockSpec((1, tq, LANES), kblk),
                     pl.BlockSpec((1, tq, LANES), kblk),
                     pl.BlockSpec((1, 1, 8, tq), lambda b, g, j: (b * n_pair + g, j, 0, 0))]
    else:
        pair = pl.BlockSpec((2, s, LANES), whole)
        pair_k = pl.BlockSpec((2, tq, LANES), kblk)
        in_specs = [pair, pl.BlockSpec((1, s, LANES), whole), pair_k,
                    pl.BlockSpec((s, LANES), lambda b, g, j: (b, 0)), pair_k,
                    pair, pair, pair]
        args = [qn, qr, kn, kr, v, o, do, lse]
        out_shape = [jax.ShapeDtypeStruct((8, t, LANES), F32), jax.ShapeDtypeStruct((4, t, LANES), F32),
                     jax.ShapeDtypeStruct((8, t, LANES), BF16), jax.ShapeDtypeStruct((8, t, LANES), BF16),
                     jax.ShapeDtypeStruct((t, LANES), F32)]
        out_specs = [pair, pl.BlockSpec((1, s, LANES), whole), pair_k, pair_k,
                     pl.BlockSpec((s, LANES), lambda b, g, j: (b, 0))]
    t_blocks = pltpu.VMEM((nq, LANES, tq), BF16)
    t_pairs = pltpu.VMEM((2, nq, LANES, tq), BF16)
    scratch = [pltpu.VMEM((2, s, LANES), F32)] + ([t_blocks, t_blocks] if fox else [t_pairs, t_blocks, t_pairs])
    return pl.pallas_call(
        body, name=name, out_shape=out_shape, grid=(bl, n_pair, nq), in_specs=in_specs, out_specs=out_specs,
        scratch_shapes=scratch, compiler_params=_params("arbitrary", "arbitrary", "arbitrary"),
    )(*args)


def adamw(w, g, m, v, name):
    shape = w.shape
    c = shape[-1]
    r = w.size // c
    tr = _tile(r, 512, 8)

    def body(w_ref, g_ref, m_ref, v_ref, d_ref, nm_ref, nv_ref):
        gv = g_ref[...]
        m2 = ADAM_B1 * m_ref[...] + (1.0 - ADAM_B1) * gv
        v2 = ADAM_B2 * v_ref[...] + (1.0 - ADAM_B2) * (gv * gv)
        m_hat = m2 / (1.0 - ADAM_B1 ** ADAM_STEP)
        v_hat = v2 / (1.0 - ADAM_B2 ** ADAM_STEP)
        d_ref[...] = -ADAM_LR * (m_hat / (jnp.sqrt(v_hat) + ADAM_EPS) + ADAM_WD * w_ref[...])
        nm_ref[...] = m2
        nv_ref[...] = v2

    spec = pl.BlockSpec((tr, c), lambda i: (i, 0))
    outs = pl.pallas_call(
        body, name=name, out_shape=[jax.ShapeDtypeStruct((r, c), F32)] * 3, grid=(r // tr,),
        in_specs=[spec] * 4, out_specs=[spec] * 3, compiler_params=_params("arbitrary"),
    )(*(a.reshape(r, c) for a in (w, g, m, v)))
    return tuple(a.reshape(shape) for a in outs)


PACK_COLS = 1024


def _pack_rows(a):
    return a.reshape(-1, PACK_COLS)


def kernel(x, c, positions, mla_w_in, mla_g_q, mla_w_uq, mla_g_kv, mla_w_uk, mla_w_uv, mla_w_o, fox_w_in, fox_b_f, fox_w_o, ada_w, ada_b, ffn_w_gate, ffn_w_up, ffn_w_down, ln_g, ln_b, loss_target, m_mla_w_in, m_mla_g_q, m_mla_w_uq, m_mla_g_kv, m_mla_w_uk, m_mla_w_uv, m_mla_w_o, m_fox_w_in, m_fox_b_f, m_fox_w_o, m_ada_w, m_ada_b, m_ffn_w_gate, m_ffn_w_up, m_ffn_w_down, m_ln_g, m_ln_b, v_mla_w_in, v_mla_g_q, v_mla_w_uq, v_mla_g_kv, v_mla_w_uk, v_mla_w_uv, v_mla_w_o, v_fox_w_in, v_fox_b_f, v_fox_w_o, v_ada_w, v_ada_b, v_ffn_w_gate, v_ffn_w_up, v_ffn_w_down, v_ln_g, v_ln_b):
    bl, s, d = x.shape
    t = bl * s
    ff = ffn_w_gate.shape[-1] * N_DEV
    dev = 4 * lax.axis_index("x") + 2 * lax.axis_index("y") + lax.axis_index("c")
    ada_cols = ada_w.shape[-1]
    fox_in = fox_w_in.shape[-1] * N_DEV
    mla_in = mla_w_in.shape[-1]
    mla_in_pad = mla_in + (-mla_in) % LANES

    def t_last(a):
        return jnp.swapaxes(a, -1, -2)

    local = {
        "mla_w_in": mla_w_in[0],
        "mla_w_uq": t_last(mla_w_uq[0]),
        "mla_w_uk": t_last(mla_w_uk[0]),
        "mla_w_uv": t_last(mla_w_uv[0]),
        "mla_w_o": mla_w_o[0],
        "fox_w_in": t_last(fox_w_in[0]),
        "fox_w_o": fox_w_o[0],
    }
    for i in range(DEPTH):
        local.update({f"gate{i}": t_last(ffn_w_gate[i]), f"up{i}": t_last(ffn_w_up[i]), f"down{i}": ffn_w_down[i]})
    groups = [["mla_w_in", "mla_w_uq", "mla_w_uk", "mla_w_uv", "mla_w_o"],
              ["gate0", "up0", "down0"],
              ["fox_w_in", "fox_w_o"],
              ["gate1", "up1", "down1"]]
    offsets, rows_of, slot_of, group_of = {}, {}, {}, {}
    group_rows = []
    for gi, names in enumerate(groups):
        rows = 0
        for nm in names:
            rows_of[nm] = local[nm].size // PACK_COLS
            slot_of[nm] = rows_of[nm] + (-rows_of[nm]) % 16
            offsets[nm] = rows
            group_of[nm] = gi
            rows += slot_of[nm]
        group_rows.append(rows)

    def slot(nm, rows):
        pad = [(0, 0)] * rows.ndim
        pad[-2] = (0, slot_of[nm] - rows_of[nm])
        return jnp.pad(rows, pad)

    def landing(block):
        land = lax.empty((N_DEV,) + block.shape, block.dtype)
        return lax.dynamic_update_slice(land, block[None], (dev, 0, 0))

    packed = [jnp.concatenate([slot(nm, _pack_rows(local[nm]).astype(BF16)) for nm in names], axis=0)
              for names in groups]
    gathered = [all_gather(packed[0], "gather_mla_weights")] + [None] * (len(groups) - 1)
    gather_started = [None] * len(groups)

    def depart(gi, after):
        block = lax.optimization_barrier((packed[gi], after))[0]
        gather_started[gi] = exchange_start(block, landing(block), f"gather_group{gi}_start", False)
        return gather_started[gi][4]

    def full(nm, cols):
        blk = gathered[group_of[nm]][:, offsets[nm]:offsets[nm] + rows_of[nm], :]
        return blk.reshape(-1, cols)

    w_in = jnp.pad(full("mla_w_in", mla_in), ((0, 0), (0, mla_in_pad - mla_in)))
    wt_uq = full("mla_w_uq", MLA_QR).reshape(MLA_HEADS, MLA_NOPE + MLA_ROPE, MLA_QR)
    wt_uq_n = wt_uq[:, :MLA_NOPE].reshape(MLA_HEADS * MLA_NOPE, MLA_QR)
    wt_uq_r = wt_uq[:, MLA_NOPE:].reshape(MLA_HEADS * MLA_ROPE, MLA_QR)
    wt_uk = full("mla_w_uk", MLA_KVR)
    wt_uv = full("mla_w_uv", MLA_KVR)
    w_mo = full("mla_w_o", d)
    wt_gate, wt_up, w_down = [None] * DEPTH, [None] * DEPTH, [None] * DEPTH

    def arrive(gi, after):
        gathered[gi] = exchange_wait(gather_started[gi], after, f"gather_group{gi}_wait", False)
        if gi + 1 < len(groups):
            gathered[gi] = lax.optimization_barrier((gathered[gi], depart(gi + 1, gathered[gi])))[0]
        for i in range(DEPTH):
            if group_of[f"gate{i}"] == gi:
                wt_gate[i], wt_up[i], w_down[i] = full(f"gate{i}", d), full(f"up{i}", d), full(f"down{i}", d)

    small = jnp.concatenate([c.reshape(-1, LANES), ln_g.reshape(-1, LANES), ln_b.reshape(-1, LANES)], axis=0)
    small_rows = small.shape[0]
    small = jnp.pad(small, ((0, (-small_rows) % 8), (0, 0)))
    small_all = all_gather(small, "gather_small")
    c_rows = bl * d // LANES
    c_all = small_all[:, :c_rows].reshape(N_DEV * bl, d)
    n_ln = DEPTH * 2
    ln_g_all = small_all[:, c_rows:c_rows + n_ln, :].transpose(1, 0, 2).reshape(DEPTH, 2, 1, d)
    ln_b_all = small_all[:, c_rows + n_ln:c_rows + 2 * n_ln, :].transpose(1, 0, 2).reshape(DEPTH, 2, 1, d)

    c_act = silu_rows(c_all, "silu_c")
    ada_b_loc = lax.dynamic_slice_in_dim(ada_b, dev * ada_cols, ada_cols, axis=1)
    mod_cols = [mm([(c_act, ada_w[i])], trans_b=False, out_dtype=F32, name=f"ada_fwd{i}", bias=ada_b_loc[i][None, :])
                for i in range(DEPTH)]
    mod_all = all_gather(jnp.concatenate(mod_cols, axis=0), "gather_mod")
    mod_all = mod_all.reshape(N_DEV, DEPTH, N_DEV * bl, ada_cols).transpose(1, 2, 0, 3).reshape(DEPTH, N_DEV * bl, 6 * d)
    mod_mine = lax.dynamic_slice_in_dim(mod_all, dev * bl, bl, axis=1)
    mods = [mod_mine[i].reshape(bl * 6, 1, d) for i in range(DEPTH)]
    mods[0] = mods[0] + depart(1, (mod_mine, gathered[0]))[0, 0]

    half_r = MLA_ROPE // 2
    inv_freq = ROPE_THETA ** (-jnp.arange(half_r, dtype=F32) / half_r)
    inv_freq = jnp.tile(inv_freq, LANES // half_r)[None, :]
    sign = jnp.tile(jnp.concatenate([-jnp.ones((half_r,), F32), jnp.ones((half_r,), F32)]), LANES // MLA_ROPE)[None, :]
    cos_t, sin_t = rope_tables(positions.astype(F32).reshape(t, 1), inv_freq, sign, "rope_tables")

    x2d = x.reshape(t, d)
    g_q, g_kv = mla_g_q.reshape(1, MLA_QR), mla_g_kv.reshape(1, MLA_KVR)
    b_f = jnp.pad(fox_b_f.reshape(1, FOX_HEADS), ((0, 0), (0, LANES - FOX_HEADS)))
    mla_scale = (MLA_NOPE + MLA_ROPE) ** -0.5
    fox_scale = FOX_HD ** -0.5
    tq = _attn_tiles(s)
    nk = s // tq

    saved = []
    u = modulate(x2d, mods[0], 0, 1, bl, "modulate0")
    xin = x2d
    for i in range(DEPTH):
        sv = {"u": u, "x_in": xin}
        if i % 2 == 0:
            h_in = mm([(u, w_in)], trans_b=False, out_dtype=F32, name=f"mla_in{i}")
            c_q, c_kv, k_r = mla_latents_forward(h_in, g_q, g_kv, cos_t, sin_t, f"mla_latents{i}")
            q_n = mm([(c_q, wt_uq_n)], trans_b=True, out_dtype=BF16, out_slab=True, name=f"mla_qn{i}")
            q_r_raw = mm([(c_q, wt_uq_r)], trans_b=True, out_dtype=F32, out_slab=True, name=f"mla_qr{i}")
            q_r = rope_slabs(q_r_raw, cos_t, sin_t, BF16, f"mla_qrope{i}")
            k_n = mm([(c_kv, wt_uk)], trans_b=True, out_dtype=BF16, out_slab=True, name=f"mla_kn{i}")
            v_m = mm([(c_kv, wt_uv)], trans_b=True, out_dtype=BF16, out_slab=True, name=f"mla_v{i}")
            ops = (q_n, q_r, k_n, k_r, v_m)
            o, lse, o_delta = attention_forward("mla", ops, bl, mla_scale, f"mla_attn{i}")
            y = mm([(o, w_mo)], trans_b=False, out_dtype=F32, name=f"mla_out{i}")
            sv.update(h_in=h_in, c_q=c_q, c_kv=c_kv, ops=ops, o=o, lse=lse, o_delta=o_delta)
        else:
            arrive(2, u)
            wt_fox = full("fox_w_in", d)
            wt_qkv = wt_fox[:3 * d]
            wt_f = jnp.pad(wt_fox[3 * d:], ((0, LANES - FOX_HEADS), (0, 0)))
            w_fo = full("fox_w_o", d)
            qkv = mm([(u, wt_qkv)], trans_b=True, out_dtype=BF16, out_slab=True, name=f"fox_qkv{i}")
            z = mm([(u, wt_f)], trans_b=True, out_dtype=F32, name=f"fox_z{i}")
            f_tok, f_q = fox_gate_forward(z, b_f, bl, f"fox_gate{i}")
            f_k = f_tok[:, :FOX_HEADS].reshape(bl, nk, tq, FOX_HEADS // 2, 2).transpose(0, 3, 1, 4, 2)
            f_k = jnp.pad(f_k.reshape(bl * FOX_HEADS // 2, nk, 2, tq), ((0, 0), (0, 0), (0, 6), (0, 0)))
            ops = (qkv, f_q, f_k)
            o, lse, o_delta = attention_forward("fox", ops, bl, fox_scale, f"fox_attn{i}")
            y = mm([(o, w_fo)], trans_b=False, out_dtype=F32, name=f"fox_out{i}")
            sv.update(z=z, ops=ops, o=o, lse=lse, o_delta=o_delta)
        x1, r1, u2 = residual_layer_norm(xin, y, mods[i], 2, ln_g_all[i, 0], ln_b_all[i, 0], bl, f"ln_mix{i}",
                                         next_mod=(3, 4))
        if wt_gate[i] is None:
            arrive(group_of[f"gate{i}"], u2)
        a, bb, h = swiglu_in(u2, wt_gate[i], wt_up[i], f"ffn_in{i}")
        y2 = mm([(h, w_down[i])], trans_b=False, out_dtype=F32, name=f"ffn_down{i}")
        sv.update(y=y, r1=r1, u2=u2, a=a, bb=bb, h=h, y2=y2)
        if i + 1 < DEPTH:
            xin, r2, u = residual_layer_norm(x1, y2, mods[i], 5, ln_g_all[i, 1], ln_b_all[i, 1], bl, f"ln_ffn{i}",
                                             next_mod=(0, 1, mods[i + 1]))
        else:
            xin, r2 = residual_layer_norm(x1, y2, mods[i], 5, ln_g_all[i, 1], ln_b_all[i, 1], bl, f"ln_ffn{i}")
        sv.update(r2=r2)
        saved.append(sv)

    loss_cols, d_x = loss_head(xin, loss_target.reshape(t, d), "loss_head")

    grads_full = {}
    wgrad = functools.partial(mm_tn, out_dtype=BF16)
    dmod = [[None] * 6 for _ in range(DEPTH)]
    dg_ln = [[None, None] for _ in range(DEPTH)]
    db_ln = [[None, None] for _ in range(DEPTH)]
    dg_q = dg_kv = db_f = None
    d_a, du = d_x, None
    scatter_started = [None] * len(groups)

    def scatter_start(gi, after=None):
        g = jnp.concatenate(
            [slot(nm, grads_full[nm].reshape(N_DEV, rows_of[nm], PACK_COLS).astype(BF16)) for nm in groups[gi]], axis=1)
        if after is not None:
            g = lax.optimization_barrier((g, after))[0]
        own = lax.dynamic_index_in_dim(g, dev, 0, keepdims=False)
        scatter_started[gi] = exchange_start(g, landing(own), f"scatter_group{gi}_start", True)

    ln_g_bwd = [[ln_g_all[i, k] for k in range(2)] for i in range(DEPTH)]
    for i in reversed(range(DEPTH)):
        sv = saved[i]
        if i + 1 < DEPTH:
            gi = group_of["fox_w_in"]
            scatter_start(gi)
            ln_g_bwd[i][1] = after_token(ln_g_bwd[i][1], scatter_started[gi])
        ln2 = (sv["r2"], sv["y2"], ln_g_bwd[i][1], ln_b_all[i, 1], (mods[i], 5))
        if du is None:
            bw = sublayer_backward(d_a, bl, f"bwd_ln_ffn{i}", ln=ln2)
        else:
            bw = sublayer_backward(d_a, bl, f"bwd_ln_ffn{i}", du=du, scale=(mods[i + 1], 1), ln=ln2)
            dmod[i + 1][0], dmod[i + 1][1] = bw["dshift"], bw["dscale"]
        dmod[i][5], dg_ln[i][1], db_ln[i][1] = bw["dgate"], bw["dg"], bw["db"]
        dy2 = bw["dy"]
        da, dbb = swiglu_out_backward(dy2, w_down[i], sv["a"], sv["bb"], f"bwd_ffn_act{i}")
        du2 = mm([(da, wt_gate[i]), (dbb, wt_up[i])], trans_b=False, out_dtype=F32, name=f"bwd_ffn_du{i}")
        grads_full[f"down{i}"] = wgrad(sv["h"], dy2, name=f"bwd_w_down{i}")
        grads_full[f"gate{i}"] = wgrad(da, sv["u2"], name=f"bwd_w_gate{i}")
        grads_full[f"up{i}"] = wgrad(dbb, sv["u2"], name=f"bwd_w_up{i}")
        gi = group_of[f"gate{i}"]
        scatter_start(gi)
        ln_g_bwd[i][0] = after_token(ln_g_bwd[i][0], scatter_started[gi])
        bw = sublayer_backward(bw["dx"], bl, f"bwd_ln_mix{i}", du=du2, scale=(mods[i], 4),
                               ln=(sv["r1"], sv["y"], ln_g_bwd[i][0], ln_b_all[i, 0], (mods[i], 2)))
        dmod[i][3], dmod[i][4], dmod[i][2] = bw["dshift"], bw["dscale"], bw["dgate"]
        dg_ln[i][0], db_ln[i][0] = bw["dg"], bw["db"]
        d_a, dy = bw["dx"], bw["dy"]
        o, lse, ops = sv["o"], sv["lse"], sv["ops"]
        if i % 2 == 0:
            do = mm([(dy, w_mo)], trans_b=True, out_dtype=BF16, out_slab=True, name=f"bwd_mla_do{i}")
            grads_full["mla_w_o"] = wgrad(o, dy, name=f"bwd_w_mla_o{i}")
            dqn, dqr, dkn, dvm, dkr = attention_backward("mla", ops, sv["o_delta"], do, lse, bl, mla_scale,
                                                         f"bwd_mla_attn{i}")
            dqr = rope_slabs(dqr, cos_t, sin_t, F32, f"bwd_mla_qrope{i}", transposed=True)
            dcq = mm([(dqn, wt_uq_n), (dqr, wt_uq_r)], trans_b=False, out_dtype=F32, name=f"bwd_mla_dcq{i}")
            dckv = mm([(dkn, wt_uk), (dvm, wt_uv)], trans_b=False, out_dtype=F32, name=f"bwd_mla_dckv{i}")
            d_uq_n = wgrad(dqn, sv["c_q"], name=f"bwd_w_uq_n{i}").reshape(MLA_HEADS, MLA_NOPE, MLA_QR)
            d_uq_r = wgrad(dqr, sv["c_q"], name=f"bwd_w_uq_r{i}").reshape(MLA_HEADS, MLA_ROPE, MLA_QR)
            grads_full["mla_w_uq"] = jnp.concatenate([d_uq_n, d_uq_r], axis=1)
            grads_full["mla_w_uk"] = wgrad(dkn, sv["c_kv"], name=f"bwd_w_uk{i}")
            grads_full["mla_w_uv"] = wgrad(dvm, sv["c_kv"], name=f"bwd_w_uv{i}")
            dh_in, dg_q, dg_kv = mla_latents_backward(sv["h_in"], dcq, dckv, dkr, g_q, g_kv, cos_t, sin_t,
                                                      f"bwd_mla_latents{i}")
            du = mm([(dh_in, w_in)], trans_b=True, out_dtype=F32, name=f"bwd_mla_du{i}")
            grads_full["mla_w_in"] = wgrad(sv["u"], dh_in, name=f"bwd_w_mla_in{i}")[:, :mla_in]
        else:
            do = mm([(dy, w_fo)], trans_b=True, out_dtype=BF16, out_slab=True, name=f"bwd_fox_do{i}")
            grads_full["fox_w_o"] = wgrad(o, dy, name=f"bwd_w_fox_o{i}")
            dq, dk, dvf, dfk = attention_backward("fox", ops, sv["o_delta"], do, lse, bl, fox_scale, f"bwd_fox_attn{i}")
            df = dfk[:, :, :2, :].reshape(bl, FOX_HEADS // 2, nk, 2, tq).transpose(0, 2, 4, 1, 3).reshape(t, FOX_HEADS)
            df = jnp.pad(df, ((0, 0), (0, LANES - FOX_HEADS)))
            dz, db_f = fox_gate_backward(sv["z"], b_f, df, bl, f"bwd_fox_gate{i}")
            du = mm([(dq, wt_fox[0:d]), (dk, wt_fox[d:2 * d]), (dvf, wt_fox[2 * d:3 * d]), (dz, wt_f)],
                    trans_b=False, out_dtype=F32, name=f"bwd_fox_du{i}")
            u_f = sv["u"]
            grads_full["fox_w_in"] = jnp.concatenate(
                [wgrad(dq, u_f, name=f"bwd_w_fox_q{i}"), wgrad(dk, u_f, name=f"bwd_w_fox_k{i}"),
                 wgrad(dvf, u_f, name=f"bwd_w_fox_v{i}"), wgrad(dz, u_f, name=f"bwd_w_fox_f{i}")[:FOX_HEADS]], axis=0)
    bw = sublayer_backward(d_a, bl, "bwd_input", du=du, scale=(mods[0], 1), x_in=x2d)
    dmod[0][0], dmod[0][1] = bw["dshift"], bw["dscale"]
    grad_x = bw["dx"].reshape(bl, s, d)

    dmod_rows = jnp.concatenate([r.reshape(bl, d) for layer in dmod for r in layer], axis=0)
    dmod_rows = dmod_rows.reshape(DEPTH, 6, bl, d).transpose(0, 2, 1, 3)
    n_mod = dmod_rows.size // LANES
    ln_parts = [dg_ln[i][k] for i in range(DEPTH) for k in range(2)] + [db_ln[i][k] for i in range(DEPTH) for k in range(2)]
    small_g = jnp.concatenate([dmod_rows.reshape(-1, LANES), dg_q.reshape(-1, LANES), dg_kv.reshape(-1, LANES), db_f]
                              + [p.reshape(-1, LANES) for p in ln_parts] + [loss_cols.reshape(-1, LANES)], axis=0)
    n_small = small_g.shape[0]
    small_g = jnp.pad(small_g, ((0, (-n_small) % 8), (0, 0)))
    small_g_all = all_gather(small_g, "gather_small_grads")
    scatter_start(0, after=small_g_all)
    small_g_all = lax.optimization_barrier((small_g_all, scatter_started[0][4]))[0]
    small_sum = sum_leading(small_g_all, "sum_small_grads")
    per_seq = DEPTH * 6 * d // LANES
    dmod_all = small_g_all[:, :n_mod].reshape(N_DEV, DEPTH, bl, 6 * d).transpose(1, 0, 2, 3)
    dmod_all = dmod_all.reshape(DEPTH, N_DEV * bl, 6 * d)
    o1 = n_mod
    grad_g_q = small_sum[o1:o1 + MLA_QR // LANES].reshape(1, MLA_QR)
    o1 += MLA_QR // LANES
    grad_g_kv = small_sum[o1:o1 + MLA_KVR // LANES].reshape(1, MLA_KVR)
    o1 += MLA_KVR // LANES
    grad_b_f = small_sum[o1:o1 + 1, :FOX_HEADS]
    o1 += 1
    n_ln_rows = DEPTH * 2 * d // LANES
    grad_ln_g_full = small_sum[o1:o1 + n_ln_rows].reshape(DEPTH, 2, d)
    grad_ln_b_full = small_sum[o1 + n_ln_rows:o1 + 2 * n_ln_rows].reshape(DEPTH, 2, d)
    loss = jnp.sum(small_sum[o1 + 2 * n_ln_rows:o1 + 2 * n_ln_rows + d // LANES])
    shard = d // N_DEV
    grad_ln_g = lax.dynamic_slice_in_dim(grad_ln_g_full, dev * shard, shard, axis=2)
    grad_ln_b = lax.dynamic_slice_in_dim(grad_ln_b_full, dev * shard, shard, axis=2)
    by_seq = small_g_all[:, :n_mod].reshape(N_DEV, DEPTH, bl, 6 * d // LANES, LANES).transpose(0, 2, 1, 3, 4)
    grad_ada_b = sum_leading(by_seq.reshape(N_DEV * bl, per_seq, LANES), "sum_ada_b").reshape(DEPTH, 6 * d)
    dmod_cols = lax.dynamic_slice_in_dim(dmod_all, dev * ada_cols, ada_cols, axis=2)
    grad_ada_w = jnp.stack([mm_tn(c_act, dmod_cols[i], name=f"bwd_w_ada{i}") for i in range(DEPTH)])

    g_mine = [None] * len(groups)

    def scatter_arrive(gi, after):
        landed = exchange_wait(scatter_started[gi], after, f"scatter_group{gi}_wait", True)
        g_mine[gi] = sum_leading(landed, f"scatter_group{gi}_sum")
        return g_mine[gi]

    after = scatter_started[0][4]
    for gi in reversed(range(1, len(groups))):
        after = scatter_arrive(gi, after)

    def mine(nm, shape):
        return g_mine[group_of[nm]][offsets[nm]:offsets[nm] + rows_of[nm]].reshape(shape)

    def shard_t(nm, a):
        return mine(nm, t_last(a).shape)

    transposed = {"mla_w_uq", "mla_w_uk", "mla_w_uv", "fox_w_in", "ffn_w_gate", "ffn_w_up"}
    grads = {
        "mla_w_in": lambda: mine("mla_w_in", mla_w_in[0].shape)[None],
        "mla_g_q": lambda: grad_g_q,
        "mla_w_uq": lambda: shard_t("mla_w_uq", mla_w_uq[0])[None],
        "mla_g_kv": lambda: grad_g_kv,
        "mla_w_uk": lambda: shard_t("mla_w_uk", mla_w_uk[0])[None],
        "mla_w_uv": lambda: shard_t("mla_w_uv", mla_w_uv[0])[None],
        "mla_w_o": lambda: mine("mla_w_o", mla_w_o[0].shape)[None],
        "fox_w_in": lambda: shard_t("fox_w_in", fox_w_in[0])[None],
        "fox_b_f": lambda: grad_b_f,
        "fox_w_o": lambda: mine("fox_w_o", fox_w_o[0].shape)[None],
        "ada_w": lambda: grad_ada_w,
        "ada_b": lambda: grad_ada_b,
        "ffn_w_gate": lambda: jnp.stack([shard_t(f"gate{i}", ffn_w_gate[i]) for i in range(DEPTH)]),
        "ffn_w_up": lambda: jnp.stack([shard_t(f"up{i}", ffn_w_up[i]) for i in range(DEPTH)]),
        "ffn_w_down": lambda: jnp.stack([mine(f"down{i}", ffn_w_down[i].shape) for i in range(DEPTH)]),
        "ln_g": lambda: grad_ln_g,
        "ln_b": lambda: grad_ln_b,
    }
    weights = dict(mla_w_in=mla_w_in, mla_g_q=mla_g_q, mla_w_uq=mla_w_uq, mla_g_kv=mla_g_kv, mla_w_uk=mla_w_uk,
                   mla_w_uv=mla_w_uv, mla_w_o=mla_w_o, fox_w_in=fox_w_in, fox_b_f=fox_b_f, fox_w_o=fox_w_o,
                   ada_w=ada_w, ada_b=ada_b, ffn_w_gate=ffn_w_gate, ffn_w_up=ffn_w_up, ffn_w_down=ffn_w_down,
                   ln_g=ln_g, ln_b=ln_b)
    first = dict(mla_w_in=m_mla_w_in, mla_g_q=m_mla_g_q, mla_w_uq=m_mla_w_uq, mla_g_kv=m_mla_g_kv, mla_w_uk=m_mla_w_uk,
                 mla_w_uv=m_mla_w_uv, mla_w_o=m_mla_w_o, fox_w_in=m_fox_w_in, fox_b_f=m_fox_b_f, fox_w_o=m_fox_w_o,
                 ada_w=m_ada_w, ada_b=m_ada_b, ffn_w_gate=m_ffn_w_gate, ffn_w_up=m_ffn_w_up, ffn_w_down=m_ffn_w_down,
                 ln_g=m_ln_g, ln_b=m_ln_b)
    second = dict(mla_w_in=v_mla_w_in, mla_g_q=v_mla_g_q, mla_w_uq=v_mla_w_uq, mla_g_kv=v_mla_g_kv, mla_w_uk=v_mla_w_uk,
                  mla_w_uv=v_mla_w_uv, mla_w_o=v_mla_w_o, fox_w_in=v_fox_w_in, fox_b_f=v_fox_b_f, fox_w_o=v_fox_w_o,
                  ada_w=v_ada_w, ada_b=v_ada_b, ffn_w_gate=v_ffn_w_gate, ffn_w_up=v_ffn_w_up, ffn_w_down=v_ffn_w_down,
                  ln_g=v_ln_g, ln_b=v_ln_b)
    order = list(weights)
    last = [nm for nm in order if group_of.get(nm) == 0]
    updated = {}
    for nm in [nm for nm in order if nm not in last] + last:
        if last and nm == last[0]:
            scatter_arrive(0, after)
        lay = t_last if nm in transposed else (lambda a: a)
        w = lay(weights[nm])
        g = grads[nm]().reshape(w.shape)
        delta, new_m, new_v = adamw(w, g, lay(first[nm]), lay(second[nm]), f"adamw_{nm}")
        updated[nm] = (lay(g), lay(delta), lay(new_m), lay(new_v))
        after = new_v
    return (loss, grad_x, *(updated[nm][k] for k in range(4) for nm in order))
```

```python
import functools
import math

import jax
import jax.numpy as jnp
from jax import lax
from jax.experimental import pallas as pl
from jax.experimental.pallas import tpu as pltpu

F32 = jnp.float32
BF16 = jnp.bfloat16
LANES = 128
N_DEV = 8
VMEM_LIMIT_BYTES = 56 * 1024 * 1024

DEPTH = 2
MLA_HEADS = 8
MLA_NOPE = 128
MLA_ROPE = 64
MLA_V = 128
MLA_QR = 256
MLA_KVR = 256
ROPE_THETA = 10000.0
FOX_HEADS = 16
FOX_HD = 64
ALPHA = (2.0 * DEPTH) ** 0.25
NORM_EPS = 1e-5
ADAM_LR = 0.001
ADAM_B1 = 0.9
ADAM_B2 = 0.999
ADAM_EPS = 1e-08
ADAM_WD = 0.01
ADAM_STEP = 10

MESH_AXES = ("x", "y", "c")
MESH = pl.DeviceIdType.MESH


def _params(*sem):
    return pltpu.CompilerParams(dimension_semantics=sem, vmem_limit_bytes=VMEM_LIMIT_BYTES)


def _tile(n, cap, mult=LANES):
    if n <= cap:
        return n
    best = None
    for t in range(mult, cap + 1, mult):
        if n % t == 0:
            best = t
    assert best is not None, (n, cap, mult)
    return best


def _dot(a, b, dims):
    return lax.dot_general(a, b, (dims, ((), ())), preferred_element_type=F32)


def _nn(a, b):
    return _dot(a, b, ((1,), (0,)))


def _nt(a, b):
    return _dot(a, b, ((1,), (1,)))


def _tn(a, b):
    return _dot(a, b, ((0,), (0,)))


def _me():
    return lax.axis_index("x"), lax.axis_index("y"), lax.axis_index("c")


def all_gather(x_loc, name):
    r, c = x_loc.shape

    def body(x_ref, out_ref, send_sems, recv_sems, local_sem):
        x, y, cc = _me()
        me, sibling = (x, y, cc), (x, y, 1 - cc)
        chips = [(1 - x, y), (x, 1 - y), (1 - x, 1 - y)]

        def rows(px, py, pc):
            return out_ref.at[4 * px + 2 * py + pc]

        def copy(k, block, to, src=None):
            return pltpu.make_async_remote_copy(
                src_ref=rows(*block) if src is None else src, dst_ref=rows(*block),
                send_sem=send_sems.at[k], recv_sem=recv_sems.at[k], device_id=to, device_id_type=MESH)

        mine = pltpu.make_async_copy(x_ref, rows(*me), local_sem)
        mine.start()
        first = [copy(0, me, sibling, src=x_ref)]
        first += [copy(1 + j, me, (*chip, cc), src=x_ref) for j, chip in enumerate(chips)]
        for cp in first:
            cp.start()
        passed = [copy(4 + j, (*chip, cc), sibling) for j, chip in enumerate(chips)]
        for j, chip in enumerate(chips):
            copy(1 + j, (*chip, cc), me).wait_recv()
            passed[j].start()
        copy(0, sibling, me).wait_recv()
        for j, chip in enumerate(chips):
            copy(4 + j, (*chip, 1 - cc), me).wait_recv()
        for cp in first + passed:
            cp.wait_send()
        mine.wait()

    return pl.pallas_call(
        body, name=name,
        out_shape=jax.ShapeDtypeStruct((N_DEV, r, c), x_loc.dtype),
        in_specs=[pl.BlockSpec(memory_space=pl.ANY)],
        out_specs=pl.BlockSpec(memory_space=pl.ANY),
        scratch_shapes=[pltpu.SemaphoreType.DMA((7,)), pltpu.SemaphoreType.DMA((7,)), pltpu.SemaphoreType.DMA(())],
    )(x_loc)


HBM_SPEC = pl.BlockSpec(memory_space=pltpu.HBM)
SEM_SPEC = pl.BlockSpec(memory_space=pltpu.SEMAPHORE)
N_PEERS = N_DEV - 1


def _peer(k):
    x, y, c = _me()
    return (1 - x if k & 4 else x, 1 - y if k & 2 else y, 1 - c if k & 1 else c)


def _exchange_copies(src_refs, land_refs, send_sems, recv_sems, scatter):
    x, y, c = _me()
    mine = 4 * x + 2 * y + c
    copies = []
    for n, (src_ref, land_ref) in enumerate(zip(src_refs, land_refs)):
        for k in range(1, N_DEV):
            px, py, pc = _peer(k)
            src = src_ref.at[4 * px + 2 * py + pc] if scatter else src_ref
            sem = n * N_PEERS + k - 1
            copies.append(pltpu.make_async_remote_copy(
                src_ref=src, dst_ref=land_ref.at[mine], send_sem=send_sems.at[sem], recv_sem=recv_sems.at[sem],
                device_id=(px, py, pc), device_id_type=MESH))
    return copies


def exchange_start(srcs, lands, name, scatter):
    n = len(srcs)

    def body(*refs):
        send_sems, recv_sems = refs[2 * n], refs[2 * n + 1]
        for cp in _exchange_copies(refs[:n], refs[n:2 * n], send_sems, recv_sems, scatter):
            cp.start()
        token = refs[-1]
        token[...] = jnp.zeros_like(token)

    outs = pl.pallas_call(
        body, name=name,
        out_shape=(pltpu.SemaphoreType.DMA((n * N_PEERS,)), pltpu.SemaphoreType.DMA((n * N_PEERS,)),
                   *(pltpu.HBM(a.shape, a.dtype) for a in (*srcs, *lands)), jax.ShapeDtypeStruct((8, LANES), F32)),
        in_specs=(HBM_SPEC,) * (2 * n),
        out_specs=(SEM_SPEC, SEM_SPEC, *((HBM_SPEC,) * (2 * n)), pl.BlockSpec(memory_space=pltpu.VMEM)),
        input_output_aliases={i: 2 + i for i in range(2 * n)},
        compiler_params=pltpu.CompilerParams(has_side_effects=pltpu.SideEffectType.DATAFLOW_SIDE_EFFECTING),
    )(*(pltpu.with_memory_space_constraint(a, pltpu.HBM) for a in (*srcs, *lands)))
    return outs[0], outs[1], outs[2:2 + n], outs[2 + n:2 + 2 * n], outs[-1]


def exchange_wait(started, after, name, scatter):
    send_sems, recv_sems, srcs, lands, _ = started
    n = len(srcs)

    def body(*refs):
        send_sems, recv_sems = refs[2 * n], refs[2 * n + 1]
        for cp in _exchange_copies(refs[:n], refs[n:2 * n], send_sems, recv_sems, scatter):
            cp.wait_send()
            cp.wait_recv()

    outs = pl.pallas_call(
        body, name=name,
        out_shape=tuple(pltpu.HBM(a.shape, a.dtype) for a in (*srcs, *lands)),
        in_specs=(*((HBM_SPEC,) * (2 * n)), SEM_SPEC, SEM_SPEC, pl.BlockSpec(memory_space=pl.ANY)),
        out_specs=(HBM_SPEC,) * (2 * n), input_output_aliases={i: i for i in range(2 * n)},
        compiler_params=pltpu.CompilerParams(has_side_effects=pltpu.SideEffectType.DATAFLOW_SIDE_EFFECTING),
    )(*srcs, *lands, send_sems, recv_sems, after)
    return outs[n:]


def after_token(small, started):
    return small + started[4][0, 0]


def sum_leading(x, name):
    n, r, c = x.shape
    tr = _tile(r, 512, 16)

    def body(x_ref, o_ref):
        acc = x_ref[0].astype(F32)
        for k in range(1, n):
            acc = acc + x_ref[k].astype(F32)
        o_ref[...] = acc

    return pl.pallas_call(
        body, name=name,
        out_shape=jax.ShapeDtypeStruct((r, c), F32),
        grid=(r // tr,),
        in_specs=[pl.BlockSpec((n, tr, c), lambda i: (0, i, 0))],
        out_specs=pl.BlockSpec((tr, c), lambda i: (i, 0)),
        compiler_params=_params("arbitrary"),
    )(x)


MM_VMEM_BUDGET = 36 * 1024 * 1024
GRID_STEP_AS_BYTES = 1 << 20


def _mm_tiles(m, n, a_row_bytes, b_col_bytes, out_bytes):
    tms = [c for c in (2048, 1024, 512, 256, 128, 64, 32, 16, 8) if m % c == 0] or [m]
    tns = [c for c in range(LANES, min(n, 2048) + 1, LANES) if n % c == 0] or [n]
    best = None
    for tm in tms:
        for tn in tns:
            vmem = 2 * (tm * a_row_bytes + tn * b_col_bytes) + 2 * tm * tn * out_bytes + tm * tn * 4
            if vmem > MM_VMEM_BUDGET:
                continue
            steps = (m // tm) * (n // tn)
            cost = steps * GRID_STEP_AS_BYTES + (m // tm) * n * b_col_bytes + m * a_row_bytes
            if best is None or cost < best[0]:
                best = (cost, tm, tn)
    assert best is not None, (m, n, a_row_bytes, b_col_bytes)
    return best[1], best[2]


def mm(pairs, *, trans_b, out_dtype, name, out_slab=False, bias=None):
    a0 = pairs[0][0]
    m = a0.shape[1] if a0.ndim == 3 else a0.shape[0]
    n = pairs[0][1].shape[0] if trans_b else pairs[0][1].shape[1]
    a_row_bytes = sum((b.shape[1] if trans_b else b.shape[0]) * a.dtype.itemsize for a, b in pairs)
    b_col_bytes = sum((b.shape[1] if trans_b else b.shape[0]) * b.dtype.itemsize for _, b in pairs)
    tm, tn = _mm_tiles(m, n, a_row_bytes, b_col_bytes, jnp.dtype(out_dtype).itemsize)
    slabs = [a.ndim == 3 for a, _ in pairs]
    n_pairs = len(pairs)

    def body(*refs):
        o_ref = refs[-1]
        acc = bias_ref = None
        if bias is not None:
            bias_ref = refs[2 * n_pairs]
        for i in range(n_pairs):
            a_ref, b_ref = refs[2 * i], refs[2 * i + 1]
            if slabs[i]:
                a = jnp.concatenate([a_ref[s].astype(BF16) for s in range(a_ref.shape[0])], axis=1)
            else:
                a = a_ref[...].astype(BF16)
            b = b_ref[...].astype(BF16)
            part = _nt(a, b) if trans_b else _nn(a, b)
            acc = part if acc is None else acc + part
        if bias_ref is not None:
            acc = acc + bias_ref[...]
        if out_slab:
            for s in range(tn // LANES):
                o_ref[s] = acc[:, s * LANES:(s + 1) * LANES].astype(out_dtype)
        else:
            o_ref[...] = acc.astype(out_dtype)

    in_specs, args = [], []
    for (a, b), slab in zip(pairs, slabs):
        if slab:
            in_specs.append(pl.BlockSpec((a.shape[0], tm, LANES), lambda i, j: (0, i, 0)))
        else:
            in_specs.append(pl.BlockSpec((tm, a.shape[1]), lambda i, j: (i, 0)))
        if trans_b:
            in_specs.append(pl.BlockSpec((tn, b.shape[1]), lambda i, j: (j, 0)))
        else:
            in_specs.append(pl.BlockSpec((b.shape[0], tn), lambda i, j: (0, j)))
        args += [a, b]
    if bias is not None:
        in_specs.append(pl.BlockSpec((1, tn), lambda i, j: (0, j)))
        args.append(bias)
    if out_slab:
        out_shape = jax.ShapeDtypeStruct((n // LANES, m, LANES), out_dtype)
        out_spec = pl.BlockSpec((tn // LANES, tm, LANES), lambda i, j: (j, i, 0))
    else:
        out_shape = jax.ShapeDtypeStruct((m, n), out_dtype)
        out_spec = pl.BlockSpec((tm, tn), lambda i, j: (i, j))
    return pl.pallas_call(
        body, name=name, out_shape=out_shape, grid=(m // tm, n // tn),
        in_specs=in_specs, out_specs=out_spec,
        compiler_params=_params("arbitrary", "arbitrary"),
    )(*args)


def mm_tn(a, b, *, name, out_dtype=F32, tk_cap=1536, tn_cap=1024, tm_cap=512):
    slab = a.ndim == 3
    m = a.shape[1] if slab else a.shape[0]
    k = a.shape[0] * LANES if slab else a.shape[1]
    n = b.shape[1]
    tk = _tile(k, tk_cap)
    tn = _tile(n, tn_cap)
    tm = _tile(m, tm_cap, 8)
    n_steps = m // tm

    def body(a_ref, b_ref, o_ref, acc_ref):
        step = pl.program_id(2)

        @pl.when(step == 0)
        def _():
            acc_ref[...] = jnp.zeros_like(acc_ref)

        bb = b_ref[...].astype(BF16)
        if slab:
            for s in range(tk // LANES):
                acc_ref[s * LANES:(s + 1) * LANES, :] += _tn(a_ref[s].astype(BF16), bb)
        else:
            acc_ref[...] += _tn(a_ref[...].astype(BF16), bb)

        @pl.when(step == n_steps - 1)
        def _():
            o_ref[...] = acc_ref[...].astype(out_dtype)

    if slab:
        a_spec = pl.BlockSpec((tk // LANES, tm, LANES), lambda i, j, t: (i, t, 0))
    else:
        a_spec = pl.BlockSpec((tm, tk), lambda i, j, t: (t, i))
    return pl.pallas_call(
        body, name=name, out_shape=jax.ShapeDtypeStruct((k, n), out_dtype), grid=(k // tk, n // tn, n_steps),
        in_specs=[a_spec, pl.BlockSpec((tm, tn), lambda i, j, t: (t, j))],
        out_specs=pl.BlockSpec((tk, tn), lambda i, j, t: (i, j)),
        scratch_shapes=[pltpu.VMEM((tk, tn), F32)],
        compiler_params=_params("arbitrary", "arbitrary", "arbitrary"),
    )(a, b)


def _row_spec(d, k):
    return pl.BlockSpec((1, 1, d), lambda b, i: (6 * b + k, 0, 0))


def modulate(x, mod, k_shift, k_scale, bl, name):
    t, d = x.shape
    s = t // bl
    tm = _tile(s, 512, 8)
    nt = s // tm

    def body(x_ref, sh_ref, sc_ref, o_ref):
        o_ref[...] = (x_ref[...] * (1.0 + sc_ref[0]) + sh_ref[0]).astype(BF16)

    return pl.pallas_call(
        body, name=name, out_shape=jax.ShapeDtypeStruct((t, d), BF16), grid=(bl, nt),
        in_specs=[pl.BlockSpec((tm, d), lambda b, i: (b * nt + i, 0)), _row_spec(d, k_shift), _row_spec(d, k_scale)],
        out_specs=pl.BlockSpec((tm, d), lambda b, i: (b * nt + i, 0)),
        compiler_params=_params("arbitrary", "arbitrary"),
    )(x, mod, mod)


def _layer_norm_stats(r):
    mu = jnp.mean(r, axis=-1, keepdims=True)
    rc = r - mu
    var = jnp.mean(rc * rc, axis=-1, keepdims=True)
    rstd = lax.rsqrt(var + NORM_EPS)
    return rc * rstd, rstd


def residual_layer_norm(x, y, mod, k_gate, g, b, bl, name, next_mod=None):
    t, d = x.shape
    s = t // bl
    tm = _tile(s, 512, 8)
    nt = s // tm
    has_next = next_mod is not None

    def body(*refs):
        x_ref, y_ref, gt_ref, g_ref, b_ref = refs[:5]
        rest = refs[5:]
        if has_next:
            sh_ref, sc_ref, o_ref, r_ref, u_ref = rest
        else:
            o_ref, r_ref = rest
        r = ALPHA * x_ref[...] + (1.0 + gt_ref[0]) * y_ref[...]
        xhat, _ = _layer_norm_stats(r)
        out = xhat * g_ref[...] + b_ref[...]
        o_ref[...] = out
        r_ref[...] = r
        if has_next:
            u_ref[...] = (out * (1.0 + sc_ref[0]) + sh_ref[0]).astype(BF16)

    tok = pl.BlockSpec((tm, d), lambda bb, i: (bb * nt + i, 0))
    vec = pl.BlockSpec((1, d), lambda bb, i: (0, 0))
    in_specs = [tok, tok, _row_spec(d, k_gate), vec, vec]
    args = [x, y, mod, g, b]
    out_shape = [jax.ShapeDtypeStruct((t, d), F32), jax.ShapeDtypeStruct((t, d), F32)]
    out_specs = [tok, tok]
    if has_next:
        in_specs += [_row_spec(d, next_mod[0]), _row_spec(d, next_mod[1])]
        args += [mod if len(next_mod) == 2 else next_mod[2]] * 2
        out_shape.append(jax.ShapeDtypeStruct((t, d), BF16))
        out_specs.append(tok)
    return pl.pallas_call(
        body, name=name, out_shape=out_shape, grid=(bl, nt), in_specs=in_specs, out_specs=out_specs,
        compiler_params=_params("arbitrary", "arbitrary"),
    )(*args)


def loss_head(xo, target, name):
    t, d = xo.shape
    tm = _tile(t, 512, 8)

    def body(x_ref, t_ref, l_ref, dx_ref):
        @pl.when(pl.program_id(0) == 0)
        def _():
            l_ref[...] = jnp.zeros_like(l_ref)

        e = x_ref[...] - t_ref[...]
        l_ref[...] += jnp.sum(e * e, axis=0, keepdims=True) * (0.5 / d)
        dx_ref[...] = e * (1.0 / d)

    tok = pl.BlockSpec((tm, d), lambda i: (i, 0))
    return pl.pallas_call(
        body, name=name,
        out_shape=[jax.ShapeDtypeStruct((1, d), F32), jax.ShapeDtypeStruct((t, d), F32)],
        grid=(t // tm,), in_specs=[tok, tok],
        out_specs=[pl.BlockSpec((1, d), lambda i: (0, 0)), tok],
        compiler_params=_params("arbitrary"),
    )(xo, target)


def sublayer_backward(d_a, bl, name, *, du=None, scale=None, x_in=None, ln=None):
    t, d = d_a.shape
    s = t // bl
    tm = _tile(s, 512, 8)
    nt = s // tm
    has_mod = du is not None
    has_ln = ln is not None
    assert has_mod or has_ln
    assert has_ln or x_in is not None

    def body(*refs):
        refs = list(refs)
        da_ref = refs.pop(0)
        if has_mod:
            du_ref, sc_ref = refs.pop(0), refs.pop(0)
        if has_ln:
            r_ref, y_ref, g_ref, b_ref, gt_ref = (refs.pop(0) for _ in range(5))
        elif has_mod:
            xin_ref = refs.pop(0)
        dx_ref = refs.pop(0)
        if has_ln:
            dy_ref, dg_ref, db_ref, dgt_ref = (refs.pop(0) for _ in range(4))
        if has_mod:
            dsc_ref, dsh_ref = refs.pop(0), refs.pop(0)
        first_tile = pl.program_id(1) == 0
        first_step = jnp.logical_and(pl.program_id(0) == 0, first_tile)

        dout = da_ref[...]
        if has_ln:
            xhat, rstd = _layer_norm_stats(r_ref[...])
        if has_mod:
            duv = du_ref[...]
            dout = dout + duv * (1.0 + sc_ref[0])
            xin = xhat * g_ref[...] + b_ref[...] if has_ln else xin_ref[...]

            @pl.when(first_tile)
            def _():
                dsc_ref[...] = jnp.zeros_like(dsc_ref)
                dsh_ref[...] = jnp.zeros_like(dsh_ref)

            dsc_ref[0] += jnp.sum(duv * xin, axis=0, keepdims=True)
            dsh_ref[0] += jnp.sum(duv, axis=0, keepdims=True)
        if not has_ln:
            dx_ref[...] = dout
            return

        @pl.when(first_step)
        def _():
            dg_ref[...] = jnp.zeros_like(dg_ref)
            db_ref[...] = jnp.zeros_like(db_ref)

        @pl.when(first_tile)
        def _():
            dgt_ref[...] = jnp.zeros_like(dgt_ref)

        dg_ref[...] += jnp.sum(dout * xhat, axis=0, keepdims=True)
        db_ref[...] += jnp.sum(dout, axis=0, keepdims=True)
        dxh = dout * g_ref[...]
        dr = rstd * (dxh - jnp.mean(dxh, axis=-1, keepdims=True) - xhat * jnp.mean(dxh * xhat, axis=-1, keepdims=True))
        dx_ref[...] = ALPHA * dr
        dy_ref[...] = ((1.0 + gt_ref[0]) * dr).astype(BF16)
        dgt_ref[0] += jnp.sum(dr * y_ref[...], axis=0, keepdims=True)

    tok = pl.BlockSpec((tm, d), lambda bb, i: (bb * nt + i, 0))
    vec = pl.BlockSpec((1, d), lambda bb, i: (0, 0))
    seq = pl.BlockSpec((1, 1, d), lambda bb, i: (bb, 0, 0))
    in_specs, args = [tok], [d_a]
    if has_mod:
        in_specs += [tok, _row_spec(d, scale[1])]
        args += [du, scale[0]]
    if has_ln:
        r, y, g, b, gate = ln
        in_specs += [tok, tok, vec, vec, _row_spec(d, gate[1])]
        args += [r, y, g, b, gate[0]]
    elif has_mod:
        in_specs.append(tok)
        args.append(x_in)
    names = ["dx"]
    out_shape, out_specs = [jax.ShapeDtypeStruct((t, d), F32)], [tok]
    if has_ln:
        names += ["dy", "dg", "db", "dgate"]
        out_shape += [jax.ShapeDtypeStruct((t, d), BF16), jax.ShapeDtypeStruct((1, d), F32),
                      jax.ShapeDtypeStruct((1, d), F32), jax.ShapeDtypeStruct((bl, 1, d), F32)]
        out_specs += [tok, vec, vec, seq]
    if has_mod:
        names += ["dscale", "dshift"]
        out_shape += [jax.ShapeDtypeStruct((bl, 1, d), F32)] * 2
        out_specs += [seq, seq]
    outs = pl.pallas_call(
        body, name=name, out_shape=out_shape, grid=(bl, nt), in_specs=in_specs, out_specs=out_specs,
        compiler_params=_params("arbitrary", "arbitrary"),
    )(*args)
    return dict(zip(names, outs))


def _silu(a):
    return a * jax.nn.sigmoid(a)


def silu_rows(a, name):
    def body(a_ref, o_ref):
        o_ref[...] = _silu(a_ref[...]).astype(BF16)

    return pl.pallas_call(body, name=name, out_shape=jax.ShapeDtypeStruct(a.shape, BF16))(a)


def _swiglu_tiles(t, f):
    return _tile(t, 512, 8), _tile(f, 1536)


def swiglu_in(u, wt_gate, wt_up, name):
    t, d = u.shape
    f = wt_gate.shape[0]
    tm, tf = _swiglu_tiles(t, f)

    def body(u_ref, g_ref, w_ref, a_ref, b_ref, h_ref):
        uv = u_ref[...]
        a = _nt(uv, g_ref[...])
        b = _nt(uv, w_ref[...])
        a_ref[...] = a.astype(BF16)
        b_ref[...] = b.astype(BF16)
        h_ref[...] = (_silu(a) * b).astype(BF16)

    w_spec = pl.BlockSpec((tf, d), lambda i, j: (j, 0))
    o_spec = pl.BlockSpec((tm, tf), lambda i, j: (i, j))
    return pl.pallas_call(
        body, name=name,
        out_shape=[jax.ShapeDtypeStruct((t, f), BF16)] * 3,
        grid=(t // tm, f // tf), in_specs=[pl.BlockSpec((tm, d), lambda i, j: (i, 0)), w_spec, w_spec],
        out_specs=[o_spec, o_spec, o_spec], compiler_params=_params("arbitrary", "arbitrary"),
    )(u, wt_gate, wt_up)


def swiglu_out_backward(dy, w_down, a, b, name):
    t, d = dy.shape
    f = w_down.shape[0]
    tm, tf = _swiglu_tiles(t, f)

    def body(dy_ref, w_ref, a_ref, b_ref, da_ref, db_ref):
        dh = _nt(dy_ref[...], w_ref[...])
        av = a_ref[...].astype(F32)
        sig = jax.nn.sigmoid(av)
        da_ref[...] = (dh * b_ref[...].astype(F32) * (sig * (1.0 + av * (1.0 - sig)))).astype(BF16)
        db_ref[...] = (dh * (av * sig)).astype(BF16)

    spec = pl.BlockSpec((tm, tf), lambda i, j: (i, j))
    return pl.pallas_call(
        body, name=name, out_shape=[jax.ShapeDtypeStruct((t, f), BF16)] * 2, grid=(t // tm, f // tf),
        in_specs=[pl.BlockSpec((tm, d), lambda i, j: (i, 0)), pl.BlockSpec((tf, d), lambda i, j: (j, 0)), spec, spec],
        out_specs=[spec, spec], compiler_params=_params("arbitrary", "arbitrary"),
    )(dy, w_down, a, b)


def rope_tables(pos, inv_freq, sign, name):
    t = pos.shape[0]
    tm = _tile(t, 512, 8)

    def body(p_ref, f_ref, s_ref, c_out, s_out):
        ang = p_ref[...] * f_ref[...]
        c_out[...] = jnp.cos(ang)
        s_out[...] = jnp.sin(ang) * s_ref[...]

    vec = pl.BlockSpec((1, LANES), lambda i: (0, 0))
    tab = pl.BlockSpec((tm, LANES), lambda i: (i, 0))
    return pl.pallas_call(
        body, name=name, out_shape=[jax.ShapeDtypeStruct((t, LANES), F32)] * 2, grid=(t // tm,),
        in_specs=[pl.BlockSpec((tm, 1), lambda i: (i, 0)), vec, vec], out_specs=[tab, tab],
        compiler_params=_params("arbitrary"),
    )(pos, inv_freq, sign)


def _rot_half(v):
    lane = lax.broadcasted_iota(jnp.int32, v.shape, v.ndim - 1)
    up = pltpu.roll(v, LANES - MLA_ROPE // 2, v.ndim - 1)
    down = pltpu.roll(v, MLA_ROPE // 2, v.ndim - 1)
    return jnp.where(lane % MLA_ROPE < MLA_ROPE // 2, up, down)


def _rope(v, cos, sin_signed):
    return v * cos + _rot_half(v) * sin_signed


def _rope_transposed(dv, cos, sin_signed):
    return dv * cos + _rot_half(dv * sin_signed)


def rope_slabs(v, cos, sin_signed, out_dtype, name, transposed=False):
    ns, t, _ = v.shape
    tm = _tile(t, 512, 8)
    fn = _rope_transposed if transposed else _rope

    def body(v_ref, c_ref, s_ref, o_ref):
        o_ref[0] = fn(v_ref[0].astype(F32), c_ref[...], s_ref[...]).astype(out_dtype)

    tab = pl.BlockSpec((tm, LANES), lambda j, i: (i, 0))
    spec = pl.BlockSpec((1, tm, LANES), lambda j, i: (j, i, 0))
    return pl.pallas_call(
        body, name=name, out_shape=jax.ShapeDtypeStruct(v.shape, out_dtype), grid=(ns, t // tm),
        in_specs=[spec, tab, tab], out_specs=spec, compiler_params=_params("arbitrary", "arbitrary"),
    )(v, cos, sin_signed)


def _rms(x):
    rinv = lax.rsqrt(jnp.mean(x * x, axis=-1, keepdims=True) + NORM_EPS)
    return x * rinv, rinv


def mla_latents_forward(h_in, g_q, g_kv, cos, sin_signed, name):
    t = h_in.shape[0]
    tm = _tile(t, 512, 8)

    def body(h_ref, gq_ref, gkv_ref, c_ref, s_ref, cq_ref, ckv_ref, kr_ref):
        cq_ref[...] = (_rms(h_ref[:, 0:MLA_QR])[0] * gq_ref[...]).astype(BF16)
        ckv_ref[...] = (_rms(h_ref[:, MLA_QR:MLA_QR + MLA_KVR])[0] * gkv_ref[...]).astype(BF16)
        kr_ref[...] = _rope(h_ref[:, MLA_QR + MLA_KVR:], c_ref[...], s_ref[...]).astype(BF16)

    def tok(w):
        return pl.BlockSpec((tm, w), lambda i: (i, 0))

    def vec(w):
        return pl.BlockSpec((1, w), lambda i: (0, 0))

    return pl.pallas_call(
        body, name=name,
        out_shape=[jax.ShapeDtypeStruct((t, MLA_QR), BF16), jax.ShapeDtypeStruct((t, MLA_KVR), BF16),
                   jax.ShapeDtypeStruct((t, LANES), BF16)],
        grid=(t // tm,),
        in_specs=[tok(h_in.shape[1]), vec(MLA_QR), vec(MLA_KVR), tok(LANES), tok(LANES)],
        out_specs=[tok(MLA_QR), tok(MLA_KVR), tok(LANES)],
        compiler_params=_params("arbitrary"),
    )(h_in, g_q, g_kv, cos, sin_signed)


def mla_latents_backward(h_in, dcq, dckv, dkr, g_q, g_kv, cos, sin_signed, name):
    t, w = h_in.shape
    tm = _tile(t, 512, 8)

    def body(h_ref, dcq_ref, dckv_ref, dkr_ref, gq_ref, gkv_ref, c_ref, s_ref, dh_ref, dgq_ref, dgkv_ref):
        @pl.when(pl.program_id(0) == 0)
        def _():
            dgq_ref[...] = jnp.zeros_like(dgq_ref)
            dgkv_ref[...] = jnp.zeros_like(dgkv_ref)

        def rms_bwd(x, dc, g_ref, dg_ref):
            xn, rinv = _rms(x)
            dg_ref[...] += jnp.sum(dc * xn, axis=0, keepdims=True)
            dxn = dc * g_ref[...]
            return rinv * (dxn - xn * jnp.mean(dxn * xn, axis=-1, keepdims=True))

        dq = rms_bwd(h_ref[:, 0:MLA_QR], dcq_ref[...], gq_ref, dgq_ref)
        dkv = rms_bwd(h_ref[:, MLA_QR:MLA_QR + MLA_KVR], dckv_ref[...], gkv_ref, dgkv_ref)
        dr = _rope_transposed(dkr_ref[...], c_ref[...], s_ref[...])
        dh_ref[...] = jnp.concatenate([dq, dkv, dr], axis=1).astype(BF16)

    def tok(ww):
        return pl.BlockSpec((tm, ww), lambda i: (i, 0))

    def vec(ww):
        return pl.BlockSpec((1, ww), lambda i: (0, 0))

    return pl.pallas_call(
        body, name=name,
        out_shape=[jax.ShapeDtypeStruct((t, w), BF16), jax.ShapeDtypeStruct((1, MLA_QR), F32),
                   jax.ShapeDtypeStruct((1, MLA_KVR), F32)],
        grid=(t // tm,),
        in_specs=[tok(w), tok(MLA_QR), tok(MLA_KVR), tok(LANES), vec(MLA_QR), vec(MLA_KVR), tok(LANES), tok(LANES)],
        out_specs=[tok(w), vec(MLA_QR), vec(MLA_KVR)],
        compiler_params=_params("arbitrary"),
    )(h_in, dcq, dckv, dkr, g_q, g_kv, cos, sin_signed)


def _tri(n, lower):
    r = lax.broadcasted_iota(jnp.int32, (n, n), 0)
    c = lax.broadcasted_iota(jnp.int32, (n, n), 1)
    return jnp.where(r >= c if lower else r <= c, 1.0, 0.0).astype(F32)


def _dot_exact(tri, v):
    hi = v.astype(BF16)
    mid = (v - hi.astype(F32)).astype(BF16)
    lo = (v - hi.astype(F32) - mid.astype(F32)).astype(BF16)
    t = tri.astype(BF16)
    return _nn(t, hi) + _nn(t, mid) + _nn(t, lo)


def fox_gate_forward(z, b_f, bl, name):
    t = z.shape[0]
    s = t // bl
    ch = LANES
    n_ch = s // ch

    def body(z_ref, b_ref, f_ref, fs_ref):
        tri = _tri(ch, True)
        carry = jnp.zeros((1, LANES), F32)
        for k in range(n_ch):
            x = z_ref[k * ch:(k + 1) * ch, :] + b_ref[...]
            logf = jnp.minimum(x, 0.0) - jnp.log(1.0 + jnp.exp(-jnp.abs(x)))
            cs = _dot_exact(tri, logf) + carry
            carry = cs[ch - 1:ch, :]
            f_ref[k * ch:(k + 1) * ch, :] = cs
            for h in range(FOX_HEADS):
                fs_ref[h, k * ch:(k + 1) * ch, :] = jnp.broadcast_to(cs[:, h:h + 1], (ch, LANES))

    return pl.pallas_call(
        body, name=name,
        out_shape=[jax.ShapeDtypeStruct((t, LANES), F32), jax.ShapeDtypeStruct((FOX_HEADS, t, LANES), F32)],
        grid=(bl,),
        in_specs=[pl.BlockSpec((s, LANES), lambda b: (b, 0)), pl.BlockSpec((1, LANES), lambda b: (0, 0))],
        out_specs=[pl.BlockSpec((s, LANES), lambda b: (b, 0)),
                   pl.BlockSpec((FOX_HEADS, s, LANES), lambda b: (0, b, 0))],
        compiler_params=_params("arbitrary"),
    )(z, b_f)


def fox_gate_backward(z, b_f, df, bl, name):
    t = z.shape[0]
    s = t // bl
    ch = LANES
    n_ch = s // ch

    def body(z_ref, b_ref, df_ref, dz_ref, db_ref):
        @pl.when(pl.program_id(0) == 0)
        def _():
            db_ref[...] = jnp.zeros_like(db_ref)

        tri = _tri(ch, False)
        carry = jnp.zeros((1, LANES), F32)
        for k in reversed(range(n_ch)):
            cs = _dot_exact(tri, df_ref[k * ch:(k + 1) * ch, :]) + carry
            carry = cs[0:1, :]
            x = z_ref[k * ch:(k + 1) * ch, :] + b_ref[...]
            dz = cs * (1.0 - jax.nn.sigmoid(x))
            dz_ref[k * ch:(k + 1) * ch, :] = dz
            db_ref[...] += jnp.sum(dz, axis=0, keepdims=True)

    tok = pl.BlockSpec((s, LANES), lambda b: (b, 0))
    vec = pl.BlockSpec((1, LANES), lambda b: (0, 0))
    return pl.pallas_call(
        body, name=name,
        out_shape=[jax.ShapeDtypeStruct((t, LANES), F32), jax.ShapeDtypeStruct((1, LANES), F32)],
        grid=(bl,), in_specs=[tok, vec, tok], out_specs=[tok, vec],
        compiler_params=_params("arbitrary"),
    )(z, b_f, df)


NEG_INF = float("-inf")


def _attn_tiles(s):
    return _tile(s, 512, 8)


def attention_forward(kind, ops, bl, scale, name):
    fox = kind == "fox"
    if fox:
        assert math.frexp(scale)[0] == 0.5, "the FoX scale is folded into bf16 queries: it must be a power of two"
        qkv, fq, fk = ops
        t = qkv.shape[1]
        n_pair = FOX_HEADS // 2
    else:
        qn, qr, kn, kr, v = ops
        t = qn.shape[1]
        n_pair = MLA_HEADS // 2
    s = t // bl
    tq = _attn_tiles(s)
    nq = s // tq
    half = LANES // 2

    def body(*refs):
        if fox:
            q_ref, k_ref, v_ref, fq_ref, fk_ref, o_ref, lse_ref, o32_ref = refs
        else:
            qn_ref, qr_ref, kn_ref, kr_ref, v_ref, o_ref, lse_ref = refs
        i = pl.program_id(2)
        row = lax.broadcasted_iota(jnp.int32, (tq, tq), 0)
        col = lax.broadcasted_iota(jnp.int32, (tq, tq), 1)
        heads = []
        for e in range(2):
            sl = slice(e * half, (e + 1) * half)
            if fox:
                heads.append((sl, q_ref[0, :, sl] * jnp.asarray(scale, BF16), None))
            else:
                heads.append((sl, qn_ref[e], qr_ref[0, :, sl]))
        dv = half if fox else LANES

        def wide(stat):
            return jnp.concatenate([stat] * (tq // LANES), axis=1)

        def step(j, carry, masked):
            rows = pl.ds(pl.multiple_of(j * tq, tq), tq)
            new = []
            for e, (sl, qa, qb) in enumerate(heads):
                m, l, acc = carry[e]
                if fox:
                    sc = _nt(qa, k_ref[0, rows, sl]) + wide(fq_ref[e]) - fk_ref[0, j, e:e + 1, :]
                    vv = v_ref[0, rows, sl]
                else:
                    sc = (_nt(qa, kn_ref[e, rows, :]) + _nt(qb, kr_ref[rows, 0:half])) * scale
                    vv = v_ref[e, rows, :]
                if masked:
                    sc = jnp.where(row >= col, sc, NEG_INF)
                m_new = jnp.maximum(m, jnp.max(sc, axis=1, keepdims=True))
                p = jnp.exp(sc - m_new)
                a = jnp.exp(m - m_new)
                l = a * l + jnp.sum(p, axis=1, keepdims=True)
                p_hi = p.astype(BF16)
                acc = a * acc + _nn(p_hi, vv)
                if fox:
                    acc = acc + _nn((p - p_hi.astype(F32)).astype(BF16), vv)
                new.append((m_new, l, acc))
            return tuple(new)

        init = (jnp.full((tq, 1), NEG_INF, F32), jnp.zeros((tq, 1), F32), jnp.zeros((tq, dv), F32))
        carry = step(i, (init, init), True)
        carry = lax.fori_loop(0, i, lambda j, c: step(j, c, False), carry)
        outs = [acc / l for _, l, acc in carry]
        for e, (m, l, _) in enumerate(carry):
            lse_ref[e] = jnp.broadcast_to(m + jnp.log(l), (tq, LANES))
        if fox:
            o32 = jnp.concatenate(outs, axis=1)
            o32_ref[0] = o32
            o_ref[0] = o32.astype(BF16)
        else:
            o_ref[0] = outs[0].astype(BF16)
            o_ref[1] = outs[1].astype(BF16)

    def q_idx(b, g, i):
        return (g, b * nq + i, 0)

    if fox:
        nk = fk.shape[1]
        in_specs = [pl.BlockSpec((1, tq, LANES), q_idx),
                    pl.BlockSpec((1, s, LANES), lambda b, g, i: (n_pair + g, b, 0)),
                    pl.BlockSpec((1, s, LANES), lambda b, g, i: (2 * n_pair + g, b, 0)),
                    pl.BlockSpec((2, tq, LANES), q_idx),
                    pl.BlockSpec((1, nk, 8, tq), lambda b, g, i: (b * n_pair + g, 0, 0, 0))]
        args = [qkv, qkv, qkv, fq, fk]
        o_spec = pl.BlockSpec((1, tq, LANES), q_idx)
    else:
        in_specs = [pl.BlockSpec((2, tq, LANES), q_idx),
                    pl.BlockSpec((1, tq, LANES), q_idx),
                    pl.BlockSpec((2, s, LANES), lambda b, g, i: (g, b, 0)),
                    pl.BlockSpec((s, LANES), lambda b, g, i: (b, 0)),
                    pl.BlockSpec((2, s, LANES), lambda b, g, i: (g, b, 0))]
        args = [qn, qr, kn, kr, v]
        o_spec = pl.BlockSpec((2, tq, LANES), q_idx)
    out_shape = [jax.ShapeDtypeStruct((8, t, LANES), BF16), jax.ShapeDtypeStruct((2 * n_pair, t, LANES), F32)]
    out_specs = [o_spec, pl.BlockSpec((2, tq, LANES), q_idx)]
    if fox:
        out_shape.append(jax.ShapeDtypeStruct((8, t, LANES), F32))
        out_specs.append(o_spec)
    outs = pl.pallas_call(
        body, name=name, out_shape=out_shape, grid=(bl, n_pair, nq), in_specs=in_specs, out_specs=out_specs,
        compiler_params=_params("arbitrary", "arbitrary", "arbitrary"),
    )(*args)
    return (outs[0], outs[1], outs[2] if fox else outs[0])


def attention_backward(kind, ops, o, do, lse, bl, scale, name):
    fox = kind == "fox"
    if fox:
        qkv, fq, fk = ops
        t = qkv.shape[1]
        n_pair = FOX_HEADS // 2
    else:
        qn, qr, kn, kr, v = ops
        t = qn.shape[1]
        n_pair = MLA_HEADS // 2
    s = t // bl
    tq = _attn_tiles(s)
    nq = s // tq
    half = LANES // 2

    def body(*refs):
        if fox:
            (q_ref, k_ref, v_ref, fq_ref, fk_ref, o_ref, do_ref, lse_ref,
             dq_ref, dk_ref, dv_ref, dfk_ref, delta_scr, qt_scr, dot_scr) = refs
        else:
            (qn_ref, qr_ref, kn_ref, kr_ref, v_ref, o_ref, do_ref, lse_ref,
             dqn_ref, dqr_ref, dkn_ref, dv_ref, dkr_ref, delta_scr, qt_scr, qrt_scr, dot_scr) = refs
        g, j = pl.program_id(1), pl.program_id(2)
        row = lax.broadcasted_iota(jnp.int32, (tq, tq), 0)
        col = lax.broadcasted_iota(jnp.int32, (tq, tq), 1)
        krows = pl.ds(pl.multiple_of(j * tq, tq), tq)

        def transposed(v):
            return v.astype(F32).T.astype(BF16)

        def wide(stat):
            return jnp.concatenate([stat] * (tq // LANES), axis=1)

        @pl.when(j == 0)
        def _():
            if fox:
                dq_ref[...] = jnp.zeros_like(dq_ref)
            else:
                dqn_ref[...] = jnp.zeros_like(dqn_ref)
                dqr_ref[...] = jnp.zeros_like(dqr_ref)
            for ii in range(nq):
                rws = slice(ii * tq, (ii + 1) * tq)
                deltas = []
                if fox:
                    prod = do_ref[0, rws, :].astype(F32) * o_ref[0, rws, :].astype(F32)
                    for e in range(2):
                        deltas.append(jnp.sum(prod[:, e * half:(e + 1) * half], axis=1, keepdims=True))
                    qt_scr[ii] = transposed(q_ref[0, rws, :])
                    dot_scr[ii] = transposed(do_ref[0, rws, :])
                else:
                    for e in range(2):
                        prod = do_ref[e, rws, :].astype(F32) * o_ref[e, rws, :].astype(F32)
                        deltas.append(jnp.sum(prod, axis=1, keepdims=True))
                        qt_scr[e, ii] = transposed(qn_ref[e, rws, :])
                        dot_scr[e, ii] = transposed(do_ref[e, rws, :])
                    qrt_scr[ii] = transposed(qr_ref[0, rws, :])
                for e in range(2):
                    delta_scr[e, rws, :] = jnp.broadcast_to(deltas[e], (tq, LANES))

        if fox:
            dfk_ref[...] = jnp.zeros_like(dfk_ref)
        else:
            @pl.when(jnp.logical_and(g == 0, j == 0))
            def _():
                dkr_ref[...] = jnp.zeros_like(dkr_ref)

        heads = []
        for e in range(2):
            sl = slice(e * half, (e + 1) * half)
            if fox:
                heads.append((sl, k_ref[0, :, sl], v_ref[0, :, sl], fk_ref[0, 0, e:e + 1, :]))
            else:
                heads.append((sl, kn_ref[e], v_ref[e], kr_ref[krows, 0:half]))
        dk_w = dv_w = half if fox else LANES

        def step(i, carry, masked):
            rows = pl.ds(pl.multiple_of(i * tq, tq), tq)
            new = []
            for e, (sl, k_e, v_e, x_e) in enumerate(heads):
                dk_acc, dv_acc, last = carry[e]
                if fox:
                    do_i = do_ref[0, rows, sl]
                    sc = _nt(q_ref[0, rows, sl], k_e) * scale + wide(fq_ref[e, rows, :]) - x_e
                else:
                    do_i = do_ref[e, rows, :]
                    sc = (_nt(qn_ref[e, rows, :], k_e) + _nt(qr_ref[0, rows, sl], x_e)) * scale
                if masked:
                    sc = jnp.where(row >= col, sc, NEG_INF)
                p = jnp.exp(sc - wide(lse_ref[e, rows, :]))
                dp = _nt(do_i, v_e)
                ds = p * (dp - wide(delta_scr[e, rows, :]))
                dsb = (ds * scale).astype(BF16)
                if fox:
                    fsl = slice(e * half, (e + 1) * half)
                    dv_acc = dv_acc + _nn(dot_scr[i, fsl, :], p.astype(BF16))
                    dk_acc = dk_acc + _nn(qt_scr[i, fsl, :], dsb)
                    dq_ref[0, rows, sl] += _nn(dsb, k_e)
                    last = last - jnp.sum(ds, axis=0, keepdims=True)
                else:
                    dv_acc = dv_acc + _nn(dot_scr[e, i], p.astype(BF16))
                    dk_acc = dk_acc + _nn(qt_scr[e, i], dsb)
                    dqn_ref[e, rows, :] += _nn(dsb, k_e)
                    dqr_ref[0, rows, sl] += _nn(dsb, x_e)
                    last = last + _nn(qrt_scr[i, e * half:(e + 1) * half, :], dsb)
                new.append((dk_acc, dv_acc, last))
            return tuple(new)

        last0 = jnp.zeros((1, tq), F32) if fox else jnp.zeros((half, tq), F32)
        init = (jnp.zeros((dk_w, tq), F32), jnp.zeros((dv_w, tq), F32), last0)
        carry = step(j, (init, init), True)
        carry = lax.fori_loop(j + 1, nq, lambda i, c: step(i, c, False), carry)
        if fox:
            for e in range(2):
                dfk_ref[0, 0, e:e + 1, :] = carry[e][2]
            dk_ref[0] = jnp.concatenate([carry[0][0], carry[1][0]], axis=0).T.astype(BF16)
            dv_ref[0] = jnp.concatenate([carry[0][1], carry[1][1]], axis=0).T.astype(BF16)
        else:
            for e in range(2):
                dkn_ref[e] = carry[e][0].T.astype(BF16)
                dv_ref[e] = carry[e][1].T.astype(BF16)
            dkr_t = carry[0][2] + carry[1][2]
            dkr_ref[krows, :] += jnp.concatenate([dkr_t, jnp.zeros_like(dkr_t)], axis=0).T

    def whole(b, g, j):
        return (g, b, 0)

    def kblk(b, g, j):
        return (g, b * nq + j, 0)

    if fox:
        in_specs = [pl.BlockSpec((1, s, LANES), whole),
                    pl.BlockSpec((1, tq, LANES), lambda b, g, j: (n_pair + g, b * nq + j, 0)),
                    pl.BlockSpec((1, tq, LANES), lambda b, g, j: (2 * n_pair + g, b * nq + j, 0)),
                    pl.BlockSpec((2, s, LANES), whole),
                    pl.BlockSpec((1, 1, 8, tq), lambda b, g, j: (b * n_pair + g, j, 0, 0)),
                    pl.BlockSpec((1, s, LANES), whole), pl.BlockSpec((1, s, LANES), whole),
                    pl.BlockSpec((2, s, LANES), whole)]
        args = [qkv, qkv, qkv, fq, fk, o, do, lse]
        out_shape = [jax.ShapeDtypeStruct((8, t, LANES), F32), jax.ShapeDtypeStruct((8, t, LANES), BF16),
                     jax.ShapeDtypeStruct((8, t, LANES), BF16), jax.ShapeDtypeStruct(fk.shape, F32)]
        out_specs = [pl.BlockSpec((1, s, LANES), whole), pl.BlockSpec((1, tq, LANES), kblk),
                     pl.BlockSpec((1, tq, LANES), kblk),
                     pl.BlockSpec((1, 1, 8, tq), lambda b, g, j: (b * n_pair + g, j, 0, 0))]
    else:
        pair = pl.BlockSpec((2, s, LANES), whole)
        pair_k = pl.BlockSpec((2, tq, LANES), kblk)
        in_specs = [pair, pl.BlockSpec((1, s, LANES), whole), pair_k,
                    pl.BlockSpec((s, LANES), lambda b, g, j: (b, 0)), pair_k,
                    pair, pair, pair]
        args = [qn, qr, kn, kr, v, o, do, lse]
        out_shape = [jax.ShapeDtypeStruct((8, t, LANES), F32), jax.ShapeDtypeStruct((4, t, LANES), F32),
                     jax.ShapeDtypeStruct((8, t, LANES), BF16), jax.ShapeDtypeStruct((8, t, LANES), BF16),
                     jax.ShapeDtypeStruct((t, LANES), F32)]
        out_specs = [pair, pl.BlockSpec((1, s, LANES), whole), pair_k, pair_k,
                     pl.BlockSpec((s, LANES), lambda b, g, j: (b, 0))]
    t_blocks = pltpu.VMEM((nq, LANES, tq), BF16)
    t_pairs = pltpu.VMEM((2, nq, LANES, tq), BF16)
    scratch = [pltpu.VMEM((2, s, LANES), F32)] + ([t_blocks, t_blocks] if fox else [t_pairs, t_blocks, t_pairs])
    return pl.pallas_call(
        body, name=name, out_shape=out_shape, grid=(bl, n_pair, nq), in_specs=in_specs, out_specs=out_specs,
        scratch_shapes=scratch, compiler_params=_params("arbitrary", "arbitrary", "arbitrary"),
    )(*args)


def adamw(w, g, m, v, name):
    shape = w.shape
    c = shape[-1]
    r = w.size // c
    tr = _tile(r, 512, 8)

    def body(w_ref, g_ref, m_ref, v_ref, d_ref, nm_ref, nv_ref):
        gv = g_ref[...]
        m2 = ADAM_B1 * m_ref[...] + (1.0 - ADAM_B1) * gv
        v2 = ADAM_B2 * v_ref[...] + (1.0 - ADAM_B2) * (gv * gv)
        m_hat = m2 / (1.0 - ADAM_B1 ** ADAM_STEP)
        v_hat = v2 / (1.0 - ADAM_B2 ** ADAM_STEP)
        d_ref[...] = -ADAM_LR * (m_hat / (jnp.sqrt(v_hat) + ADAM_EPS) + ADAM_WD * w_ref[...])
        nm_ref[...] = m2
        nv_ref[...] = v2

    spec = pl.BlockSpec((tr, c), lambda i: (i, 0))
    outs = pl.pallas_call(
        body, name=name, out_shape=[jax.ShapeDtypeStruct((r, c), F32)] * 3, grid=(r // tr,),
        in_specs=[spec] * 4, out_specs=[spec] * 3, compiler_params=_params("arbitrary"),
    )(*(a.reshape(r, c) for a in (w, g, m, v)))
    return tuple(a.reshape(shape) for a in outs)


PACK_COLS = 1024


def _pack_rows(a):
    return a.reshape(-1, PACK_COLS)


def kernel(x, c, positions, mla_w_in, mla_g_q, mla_w_uq, mla_g_kv, mla_w_uk, mla_w_uv, mla_w_o, fox_w_in, fox_b_f, fox_w_o, ada_w, ada_b, ffn_w_gate, ffn_w_up, ffn_w_down, ln_g, ln_b, loss_target, m_mla_w_in, m_mla_g_q, m_mla_w_uq, m_mla_g_kv, m_mla_w_uk, m_mla_w_uv, m_mla_w_o, m_fox_w_in, m_fox_b_f, m_fox_w_o, m_ada_w, m_ada_b, m_ffn_w_gate, m_ffn_w_up, m_ffn_w_down, m_ln_g, m_ln_b, v_mla_w_in, v_mla_g_q, v_mla_w_uq, v_mla_g_kv, v_mla_w_uk, v_mla_w_uv, v_mla_w_o, v_fox_w_in, v_fox_b_f, v_fox_w_o, v_ada_w, v_ada_b, v_ffn_w_gate, v_ffn_w_up, v_ffn_w_down, v_ln_g, v_ln_b):
    bl, s, d = x.shape
    t = bl * s
    ff = ffn_w_gate.shape[-1] * N_DEV
    dev = 4 * lax.axis_index("x") + 2 * lax.axis_index("y") + lax.axis_index("c")
    ada_cols = ada_w.shape[-1]
    fox_in = fox_w_in.shape[-1] * N_DEV
    mla_in = mla_w_in.shape[-1]
    mla_in_pad = mla_in + (-mla_in) % LANES

    def t_last(a):
        return jnp.swapaxes(a, -1, -2)

    local = {
        "mla_w_in": mla_w_in[0],
        "mla_w_uq": t_last(mla_w_uq[0]),
        "mla_w_uk": t_last(mla_w_uk[0]),
        "mla_w_uv": t_last(mla_w_uv[0]),
        "mla_w_o": mla_w_o[0],
        "fox_w_in": t_last(fox_w_in[0]),
        "fox_w_o": fox_w_o[0],
    }
    for i in range(DEPTH):
        local.update({f"gate{i}": t_last(ffn_w_gate[i]), f"up{i}": t_last(ffn_w_up[i]), f"down{i}": ffn_w_down[i]})
    groups = [["mla_w_in", "mla_w_uq", "mla_w_uk", "mla_w_uv", "mla_w_o"],
              ["gate0", "up0", "down0"],
              ["fox_w_in", "fox_w_o"],
              ["gate1", "up1", "down1"]]
    offsets, rows_of, slot_of, group_of = {}, {}, {}, {}
    group_rows = []
    for gi, names in enumerate(groups):
        rows = 0
        for nm in names:
            rows_of[nm] = local[nm].size // PACK_COLS
            slot_of[nm] = rows_of[nm] + (-rows_of[nm]) % 16
            offsets[nm] = rows
            group_of[nm] = gi
            rows += slot_of[nm]
        group_rows.append(rows)

    def slot(nm, rows):
        pad = [(0, 0)] * rows.ndim
        pad[-2] = (0, slot_of[nm] - rows_of[nm])
        return jnp.pad(rows, pad)

    def landing(block):
        land = lax.empty((N_DEV,) + block.shape, block.dtype)
        return lax.dynamic_update_slice(land, block[None], (dev, 0, 0))

    packed0 = jnp.concatenate([slot(nm, _pack_rows(local[nm]).astype(BF16)) for nm in groups[0]], axis=0)
    gathered0 = all_gather(packed0, "gather_mla_weights")
    gathered = {nm: gathered0[:, offsets[nm]:offsets[nm] + rows_of[nm], :] for nm in groups[0]}
    gather_started = [None] * len(groups)

    def depart(gi, after):
        blocks = lax.optimization_barrier(([_pack_rows(local[nm]).astype(BF16) for nm in groups[gi]], after))[0]
        gather_started[gi] = exchange_start(blocks, [landing(b) for b in blocks], f"gather_group{gi}_start", False)
        return gather_started[gi][4]

    def full(nm, cols):
        return gathered[nm].reshape(-1, cols)

    w_in = jnp.pad(full("mla_w_in", mla_in), ((0, 0), (0, mla_in_pad - mla_in)))
    wt_uq = full("mla_w_uq", MLA_QR).reshape(MLA_HEADS, MLA_NOPE + MLA_ROPE, MLA_QR)
    wt_uq_n = wt_uq[:, :MLA_NOPE].reshape(MLA_HEADS * MLA_NOPE, MLA_QR)
    wt_uq_r = wt_uq[:, MLA_NOPE:].reshape(MLA_HEADS * MLA_ROPE, MLA_QR)
    wt_uk = full("mla_w_uk", MLA_KVR)
    wt_uv = full("mla_w_uv", MLA_KVR)
    w_mo = full("mla_w_o", d)
    wt_gate, wt_up, w_down = [None] * DEPTH, [None] * DEPTH, [None] * DEPTH

    def arrive(gi, after):
        landed = list(exchange_wait(gather_started[gi], after, f"gather_group{gi}_wait", False))
        if gi + 1 < len(groups):
            landed = lax.optimization_barrier((landed, depart(gi + 1, landed)))[0]
        gathered.update(zip(groups[gi], landed))
        for i in range(DEPTH):
            if group_of[f"gate{i}"] == gi:
                wt_gate[i], wt_up[i], w_down[i] = full(f"gate{i}", d), full(f"up{i}", d), full(f"down{i}", d)

    small = jnp.concatenate([c.reshape(-1, LANES), ln_g.reshape(-1, LANES), ln_b.reshape(-1, LANES)], axis=0)
    small_rows = small.shape[0]
    small = jnp.pad(small, ((0, (-small_rows) % 8), (0, 0)))
    small_all = all_gather(small, "gather_small")
    c_rows = bl * d // LANES
    c_all = small_all[:, :c_rows].reshape(N_DEV * bl, d)
    n_ln = DEPTH * 2
    ln_g_all = small_all[:, c_rows:c_rows + n_ln, :].transpose(1, 0, 2).reshape(DEPTH, 2, 1, d)
    ln_b_all = small_all[:, c_rows + n_ln:c_rows + 2 * n_ln, :].transpose(1, 0, 2).reshape(DEPTH, 2, 1, d)

    c_act = silu_rows(c_all, "silu_c")
    ada_b_loc = lax.dynamic_slice_in_dim(ada_b, dev * ada_cols, ada_cols, axis=1)
    mod_cols = [mm([(c_act, ada_w[i])], trans_b=False, out_dtype=F32, name=f"ada_fwd{i}", bias=ada_b_loc[i][None, :])
                for i in range(DEPTH)]
    mod_all = all_gather(jnp.concatenate(mod_cols, axis=0), "gather_mod")
    mod_all = mod_all.reshape(N_DEV, DEPTH, N_DEV * bl, ada_cols).transpose(1, 2, 0, 3).reshape(DEPTH, N_DEV * bl, 6 * d)
    mod_mine = lax.dynamic_slice_in_dim(mod_all, dev * bl, bl, axis=1)
    mods = [mod_mine[i].reshape(bl * 6, 1, d) for i in range(DEPTH)]
    mods[0] = mods[0] + depart(1, (mod_mine, gathered0))[0, 0]

    half_r = MLA_ROPE // 2
    inv_freq = ROPE_THETA ** (-jnp.arange(half_r, dtype=F32) / half_r)
    inv_freq = jnp.tile(inv_freq, LANES // half_r)[None, :]
    sign = jnp.tile(jnp.concatenate([-jnp.ones((half_r,), F32), jnp.ones((half_r,), F32)]), LANES // MLA_ROPE)[None, :]
    cos_t, sin_t = rope_tables(positions.astype(F32).reshape(t, 1), inv_freq, sign, "rope_tables")

    x2d = x.reshape(t, d)
    g_q, g_kv = mla_g_q.reshape(1, MLA_QR), mla_g_kv.reshape(1, MLA_KVR)
    b_f = jnp.pad(fox_b_f.reshape(1, FOX_HEADS), ((0, 0), (0, LANES - FOX_HEADS)))
    mla_scale = (MLA_NOPE + MLA_ROPE) ** -0.5
    fox_scale = FOX_HD ** -0.5
    tq = _attn_tiles(s)
    nk = s // tq

    saved = []
    u = modulate(x2d, mods[0], 0, 1, bl, "modulate0")
    xin = x2d
    for i in range(DEPTH):
        sv = {"u": u, "x_in": xin}
        if i % 2 == 0:
            h_in = mm([(u, w_in)], trans_b=False, out_dtype=F32, name=f"mla_in{i}")
            c_q, c_kv, k_r = mla_latents_forward(h_in, g_q, g_kv, cos_t, sin_t, f"mla_latents{i}")
            q_n = mm([(c_q, wt_uq_n)], trans_b=True, out_dtype=BF16, out_slab=True, name=f"mla_qn{i}")
            q_r_raw = mm([(c_q, wt_uq_r)], trans_b=True, out_dtype=F32, out_slab=True, name=f"mla_qr{i}")
            q_r = rope_slabs(q_r_raw, cos_t, sin_t, BF16, f"mla_qrope{i}")
            k_n = mm([(c_kv, wt_uk)], trans_b=True, out_dtype=BF16, out_slab=True, name=f"mla_kn{i}")
            v_m = mm([(c_kv, wt_uv)], trans_b=True, out_dtype=BF16, out_slab=True, name=f"mla_v{i}")
            ops = (q_n, q_r, k_n, k_r, v_m)
            o, lse, o_delta = attention_forward("mla", ops, bl, mla_scale, f"mla_attn{i}")
            y = mm([(o, w_mo)], trans_b=False, out_dtype=F32, name=f"mla_out{i}")
            sv.update(h_in=h_in, c_q=c_q, c_kv=c_kv, ops=ops, o=o, lse=lse, o_delta=o_delta)
        else:
            arrive(2, u)
            wt_fox = full("fox_w_in", d)
            wt_qkv = wt_fox[:3 * d]
            wt_f = jnp.pad(wt_fox[3 * d:], ((0, LANES - FOX_HEADS), (0, 0)))
            w_fo = full("fox_w_o", d)
            qkv = mm([(u, wt_qkv)], trans_b=True, out_dtype=BF16, out_slab=True, name=f"fox_qkv{i}")
            z = mm([(u, wt_f)], trans_b=True, out_dtype=F32, name=f"fox_z{i}")
            f_tok, f_q = fox_gate_forward(z, b_f, bl, f"fox_gate{i}")
            f_k = f_tok[:, :FOX_HEADS].reshape(bl, nk, tq, FOX_HEADS // 2, 2).transpose(0, 3, 1, 4, 2)
            f_k = jnp.pad(f_k.reshape(bl * FOX_HEADS // 2, nk, 2, tq), ((0, 0), (0, 0), (0, 6), (0, 0)))
            ops = (qkv, f_q, f_k)
            o, lse, o_delta = attention_forward("fox", ops, bl, fox_scale, f"fox_attn{i}")
            y = mm([(o, w_fo)], trans_b=False, out_dtype=F32, name=f"fox_out{i}")
            sv.update(z=z, ops=ops, o=o, lse=lse, o_delta=o_delta)
        x1, r1, u2 = residual_layer_norm(xin, y, mods[i], 2, ln_g_all[i, 0], ln_b_all[i, 0], bl, f"ln_mix{i}",
                                         next_mod=(3, 4))
        if wt_gate[i] is None:
            arrive(group_of[f"gate{i}"], u2)
        a, bb, h = swiglu_in(u2, wt_gate[i], wt_up[i], f"ffn_in{i}")
        y2 = mm([(h, w_down[i])], trans_b=False, out_dtype=F32, name=f"ffn_down{i}")
        sv.update(y=y, r1=r1, u2=u2, a=a, bb=bb, h=h, y2=y2)
        if i + 1 < DEPTH:
            xin, r2, u = residual_layer_norm(x1, y2, mods[i], 5, ln_g_all[i, 1], ln_b_all[i, 1], bl, f"ln_ffn{i}",
                                             next_mod=(0, 1, mods[i + 1]))
        else:
            xin, r2 = residual_layer_norm(x1, y2, mods[i], 5, ln_g_all[i, 1], ln_b_all[i, 1], bl, f"ln_ffn{i}")
        sv.update(r2=r2)
        saved.append(sv)

    loss_cols, d_x = loss_head(xin, loss_target.reshape(t, d), "loss_head")

    grads_full = {}
    wgrad = functools.partial(mm_tn, out_dtype=BF16)
    dmod = [[None] * 6 for _ in range(DEPTH)]
    dg_ln = [[None, None] for _ in range(DEPTH)]
    db_ln = [[None, None] for _ in range(DEPTH)]
    dg_q = dg_kv = db_f = None
    d_a, du = d_x, None
    scatter_started = [None] * len(groups)

    def scatter_start(gi, after=None):
        gs = [grads_full[nm].reshape(N_DEV, rows_of[nm], PACK_COLS).astype(BF16) for nm in groups[gi]]
        if gi == 0:
            gs = [jnp.concatenate([slot(nm, g) for nm, g in zip(groups[gi], gs)], axis=1)]
        if after is not None:
            gs = lax.optimization_barrier((gs, after))[0]
        lands = [landing(lax.dynamic_index_in_dim(g, dev, 0, keepdims=False)) for g in gs]
        scatter_started[gi] = exchange_start(gs, lands, f"scatter_group{gi}_start", True)

    ln_g_bwd = [[ln_g_all[i, k] for k in range(2)] for i in range(DEPTH)]
    for i in reversed(range(DEPTH)):
        sv = saved[i]
        if i + 1 < DEPTH:
            gi = group_of["fox_w_in"]
            scatter_start(gi)
            ln_g_bwd[i][1] = after_token(ln_g_bwd[i][1], scatter_started[gi])
        ln2 = (sv["r2"], sv["y2"], ln_g_bwd[i][1], ln_b_all[i, 1], (mods[i], 5))
        if du is None:
            bw = sublayer_backward(d_a, bl, f"bwd_ln_ffn{i}", ln=ln2)
        else:
            bw = sublayer_backward(d_a, bl, f"bwd_ln_ffn{i}", du=du, scale=(mods[i + 1], 1), ln=ln2)
            dmod[i + 1][0], dmod[i + 1][1] = bw["dshift"], bw["dscale"]
        dmod[i][5], dg_ln[i][1], db_ln[i][1] = bw["dgate"], bw["dg"], bw["db"]
        dy2 = bw["dy"]
        da, dbb = swiglu_out_backward(dy2, w_down[i], sv["a"], sv["bb"], f"bwd_ffn_act{i}")
        du2 = mm([(da, wt_gate[i]), (dbb, wt_up[i])], trans_b=False, out_dtype=F32, name=f"bwd_ffn_du{i}")
        grads_full[f"down{i}"] = wgrad(sv["h"], dy2, name=f"bwd_w_down{i}")
        grads_full[f"gate{i}"] = wgrad(da, sv["u2"], name=f"bwd_w_gate{i}")
        grads_full[f"up{i}"] = wgrad(dbb, sv["u2"], name=f"bwd_w_up{i}")
        gi = group_of[f"gate{i}"]
        scatter_start(gi)
        ln_g_bwd[i][0] = after_token(ln_g_bwd[i][0], scatter_started[gi])
        bw = sublayer_backward(bw["dx"], bl, f"bwd_ln_mix{i}", du=du2, scale=(mods[i], 4),
                               ln=(sv["r1"], sv["y"], ln_g_bwd[i][0], ln_b_all[i, 0], (mods[i], 2)))
        dmod[i][3], dmod[i][4], dmod[i][2] = bw["dshift"], bw["dscale"], bw["dgate"]
        dg_ln[i][0], db_ln[i][0] = bw["dg"], bw["db"]
        d_a, dy = bw["dx"], bw["dy"]
        o, lse, ops = sv["o"], sv["lse"], sv["ops"]
        if i % 2 == 0:
            do = mm([(dy, w_mo)], trans_b=True, out_dtype=BF16, out_slab=True, name=f"bwd_mla_do{i}")
            grads_full["mla_w_o"] = wgrad(o, dy, name=f"bwd_w_mla_o{i}")
            dqn, dqr, dkn, dvm, dkr = attention_backward("mla", ops, sv["o_delta"], do, lse, bl, mla_scale,
                                                         f"bwd_mla_attn{i}")
            dqr = rope_slabs(dqr, cos_t, sin_t, F32, f"bwd_mla_qrope{i}", transposed=True)
            dcq = mm([(dqn, wt_uq_n), (dqr, wt_uq_r)], trans_b=False, out_dtype=F32, name=f"bwd_mla_dcq{i}")
            dckv = mm([(dkn, wt_uk), (dvm, wt_uv)], trans_b=False, out_dtype=F32, name=f"bwd_mla_dckv{i}")
            d_uq_n = wgrad(dqn, sv["c_q"], name=f"bwd_w_uq_n{i}").reshape(MLA_HEADS, MLA_NOPE, MLA_QR)
            d_uq_r = wgrad(dqr, sv["c_q"], name=f"bwd_w_uq_r{i}").reshape(MLA_HEADS, MLA_ROPE, MLA_QR)
            grads_full["mla_w_uq"] = jnp.concatenate([d_uq_n, d_uq_r], axis=1)
            grads_full["mla_w_uk"] = wgrad(dkn, sv["c_kv"], name=f"bwd_w_uk{i}")
            grads_full["mla_w_uv"] = wgrad(dvm, sv["c_kv"], name=f"bwd_w_uv{i}")
            dh_in, dg_q, dg_kv = mla_latents_backward(sv["h_in"], dcq, dckv, dkr, g_q, g_kv, cos_t, sin_t,
                                                      f"bwd_mla_latents{i}")
            du = mm([(dh_in, w_in)], trans_b=True, out_dtype=F32, name=f"bwd_mla_du{i}")
            grads_full["mla_w_in"] = wgrad(sv["u"], dh_in, name=f"bwd_w_mla_in{i}")[:, :mla_in]
        else:
            do = mm([(dy, w_fo)], trans_b=True, out_dtype=BF16, out_slab=True, name=f"bwd_fox_do{i}")
            grads_full["fox_w_o"] = wgrad(o, dy, name=f"bwd_w_fox_o{i}")
            dq, dk, dvf, dfk = attention_backward("fox", ops, sv["o_delta"], do, lse, bl, fox_scale, f"bwd_fox_attn{i}")
            df = dfk[:, :, :2, :].reshape(bl, FOX_HEADS // 2, nk, 2, tq).transpose(0, 2, 4, 1, 3).reshape(t, FOX_HEADS)
            df = jnp.pad(df, ((0, 0), (0, LANES - FOX_HEADS)))
            dz, db_f = fox_gate_backward(sv["z"], b_f, df, bl, f"bwd_fox_gate{i}")
            du = mm([(dq, wt_fox[0:d]), (dk, wt_fox[d:2 * d]), (dvf, wt_fox[2 * d:3 * d]), (dz, wt_f)],
                    trans_b=False, out_dtype=F32, name=f"bwd_fox_du{i}")
            u_f = sv["u"]
            grads_full["fox_w_in"] = jnp.concatenate(
                [wgrad(dq, u_f, name=f"bwd_w_fox_q{i}"), wgrad(dk, u_f, name=f"bwd_w_fox_k{i}"),
                 wgrad(dvf, u_f, name=f"bwd_w_fox_v{i}"), wgrad(dz, u_f, name=f"bwd_w_fox_f{i}")[:FOX_HEADS]], axis=0)
    bw = sublayer_backward(d_a, bl, "bwd_input", du=du, scale=(mods[0], 1), x_in=x2d)
    dmod[0][0], dmod[0][1] = bw["dshift"], bw["dscale"]
    grad_x = bw["dx"].reshape(bl, s, d)

    dmod_rows = jnp.concatenate([r.reshape(bl, d) for layer in dmod for r in layer], axis=0)
    dmod_rows = dmod_rows.reshape(DEPTH, 6, bl, d).transpose(0, 2, 1, 3)
    n_mod = dmod_rows.size // LANES
    ln_parts = [dg_ln[i][k] for i in range(DEPTH) for k in range(2)] + [db_ln[i][k] for i in range(DEPTH) for k in range(2)]
    small_g = jnp.concatenate([dmod_rows.reshape(-1, LANES), dg_q.reshape(-1, LANES), dg_kv.reshape(-1, LANES), db_f]
                              + [p.reshape(-1, LANES) for p in ln_parts] + [loss_cols.reshape(-1, LANES)], axis=0)
    n_small = small_g.shape[0]
    small_g = jnp.pad(small_g, ((0, (-n_small) % 8), (0, 0)))
    small_g_all = all_gather(small_g, "gather_small_grads")
    scatter_start(0, after=small_g_all)
    small_g_all = lax.optimization_barrier((small_g_all, scatter_started[0][4]))[0]
    small_sum = sum_leading(small_g_all, "sum_small_grads")
    per_seq = DEPTH * 6 * d // LANES
    dmod_all = small_g_all[:, :n_mod].reshape(N_DEV, DEPTH, bl, 6 * d).transpose(1, 0, 2, 3)
    dmod_all = dmod_all.reshape(DEPTH, N_DEV * bl, 6 * d)
    o1 = n_mod
    grad_g_q = small_sum[o1:o1 + MLA_QR // LANES].reshape(1, MLA_QR)
    o1 += MLA_QR // LANES
    grad_g_kv = small_sum[o1:o1 + MLA_KVR // LANES].reshape(1, MLA_KVR)
    o1 += MLA_KVR // LANES
    grad_b_f = small_sum[o1:o1 + 1, :FOX_HEADS]
    o1 += 1
    n_ln_rows = DEPTH * 2 * d // LANES
    grad_ln_g_full = small_sum[o1:o1 + n_ln_rows].reshape(DEPTH, 2, d)
    grad_ln_b_full = small_sum[o1 + n_ln_rows:o1 + 2 * n_ln_rows].reshape(DEPTH, 2, d)
    loss = jnp.sum(small_sum[o1 + 2 * n_ln_rows:o1 + 2 * n_ln_rows + d // LANES])
    shard = d // N_DEV
    grad_ln_g = lax.dynamic_slice_in_dim(grad_ln_g_full, dev * shard, shard, axis=2)
    grad_ln_b = lax.dynamic_slice_in_dim(grad_ln_b_full, dev * shard, shard, axis=2)
    by_seq = small_g_all[:, :n_mod].reshape(N_DEV, DEPTH, bl, 6 * d // LANES, LANES).transpose(0, 2, 1, 3, 4)
    grad_ada_b = sum_leading(by_seq.reshape(N_DEV * bl, per_seq, LANES), "sum_ada_b").reshape(DEPTH, 6 * d)
    dmod_cols = lax.dynamic_slice_in_dim(dmod_all, dev * ada_cols, ada_cols, axis=2)
    grad_ada_w = jnp.stack([mm_tn(c_act, dmod_cols[i], name=f"bwd_w_ada{i}") for i in range(DEPTH)])

    g_mine = {}

    def scatter_arrive(gi, after):
        landed = exchange_wait(scatter_started[gi], after, f"scatter_group{gi}_wait", True)
        if gi == 0:
            total = sum_leading(landed[0], f"scatter_group{gi}_sum")
            g_mine.update({nm: total[offsets[nm]:offsets[nm] + rows_of[nm]] for nm in groups[gi]})
            return total
        for nm, land in zip(groups[gi], landed):
            g_mine[nm] = sum_leading(land, f"scatter_sum_{nm}")
        return g_mine[groups[gi][-1]]

    after = scatter_started[0][4]
    for gi in reversed(range(1, len(groups))):
        after = scatter_arrive(gi, after)

    def mine(nm, shape):
        return g_mine[nm].reshape(shape)

    def shard_t(nm, a):
        return mine(nm, t_last(a).shape)

    transposed = {"mla_w_uq", "mla_w_uk", "mla_w_uv", "fox_w_in", "ffn_w_gate", "ffn_w_up"}
    grads = {
        "mla_w_in": lambda: mine("mla_w_in", mla_w_in[0].shape)[None],
        "mla_g_q": lambda: grad_g_q,
        "mla_w_uq": lambda: shard_t("mla_w_uq", mla_w_uq[0])[None],
        "mla_g_kv": lambda: grad_g_kv,
        "mla_w_uk": lambda: shard_t("mla_w_uk", mla_w_uk[0])[None],
        "mla_w_uv": lambda: shard_t("mla_w_uv", mla_w_uv[0])[None],
        "mla_w_o": lambda: mine("mla_w_o", mla_w_o[0].shape)[None],
        "fox_w_in": lambda: shard_t("fox_w_in", fox_w_in[0])[None],
        "fox_b_f": lambda: grad_b_f,
        "fox_w_o": lambda: mine("fox_w_o", fox_w_o[0].shape)[None],
        "ada_w": lambda: grad_ada_w,
        "ada_b": lambda: grad_ada_b,
        "ffn_w_gate": lambda: jnp.stack([shard_t(f"gate{i}", ffn_w_gate[i]) for i in range(DEPTH)]),
        "ffn_w_up": lambda: jnp.stack([shard_t(f"up{i}", ffn_w_up[i]) for i in range(DEPTH)]),
        "ffn_w_down": lambda: jnp.stack([mine(f"down{i}", ffn_w_down[i].shape) for i in range(DEPTH)]),
        "ln_g": lambda: grad_ln_g,
        "ln_b": lambda: grad_ln_b,
    }
    weights = dict(mla_w_in=mla_w_in, mla_g_q=mla_g_q, mla_w_uq=mla_w_uq, mla_g_kv=mla_g_kv, mla_w_uk=mla_w_uk,
                   mla_w_uv=mla_w_uv, mla_w_o=mla_w_o, fox_w_in=fox_w_in, fox_b_f=fox_b_f, fox_w_o=fox_w_o,
                   ada_w=ada_w, ada_b=ada_b, ffn_w_gate=ffn_w_gate, ffn_w_up=ffn_w_up, ffn_w_down=ffn_w_down,
                   ln_g=ln_g, ln_b=ln_b)
    first = dict(mla_w_in=m_mla_w_in, mla_g_q=m_mla_g_q, mla_w_uq=m_mla_w_uq, mla_g_kv=m_mla_g_kv, mla_w_uk=m_mla_w_uk,
                 mla_w_uv=m_mla_w_uv, mla_w_o=m_mla_w_o, fox_w_in=m_fox_w_in, fox_b_f=m_fox_b_f, fox_w_o=m_fox_w_o,
                 ada_w=m_ada_w, ada_b=m_ada_b, ffn_w_gate=m_ffn_w_gate, ffn_w_up=m_ffn_w_up, ffn_w_down=m_ffn_w_down,
                 ln_g=m_ln_g, ln_b=m_ln_b)
    second = dict(mla_w_in=v_mla_w_in, mla_g_q=v_mla_g_q, mla_w_uq=v_mla_w_uq, mla_g_kv=v_mla_g_kv, mla_w_uk=v_mla_w_uk,
                  mla_w_uv=v_mla_w_uv, mla_w_o=v_mla_w_o, fox_w_in=v_fox_w_in, fox_b_f=v_fox_b_f, fox_w_o=v_fox_w_o,
                  ada_w=v_ada_w, ada_b=v_ada_b, ffn_w_gate=v_ffn_w_gate, ffn_w_up=v_ffn_w_up, ffn_w_down=v_ffn_w_down,
                  ln_g=v_ln_g, ln_b=v_ln_b)
    order = list(weights)
    last = [nm for nm in order if group_of.get(nm) == 0]
    updated = {}
    for nm in [nm for nm in order if nm not in last] + last:
        if last and nm == last[0]:
            scatter_arrive(0, after)
        lay = t_last if nm in transposed else (lambda a: a)
        w = lay(weights[nm])
        g = grads[nm]().reshape(w.shape)
        delta, new_m, new_v = adamw(w, g, lay(first[nm]), lay(second[nm]), f"adamw_{nm}")
        updated[nm] = (lay(g), lay(delta), lay(new_m), lay(new_v))
        after = new_v
    return (loss, grad_x, *(updated[nm][k] for k in range(4) for nm in order))
```

```python
import functools
import math

import jax
import jax.numpy as jnp
from jax import lax
from jax.experimental import pallas as pl
from jax.experimental.pallas import tpu as pltpu

F32 = jnp.float32
BF16 = jnp.bfloat16
LANES = 128
N_DEV = 8
VMEM_LIMIT_BYTES = 56 * 1024 * 1024

DEPTH = 2
MLA_HEADS = 8
MLA_NOPE = 128
MLA_ROPE = 64
MLA_V = 128
MLA_QR = 256
MLA_KVR = 256
ROPE_THETA = 10000.0
FOX_HEADS = 16
FOX_HD = 64
ALPHA = (2.0 * DEPTH) ** 0.25
NORM_EPS = 1e-5
ADAM_LR = 0.001
ADAM_B1 = 0.9
ADAM_B2 = 0.999
ADAM_EPS = 1e-08
ADAM_WD = 0.01
ADAM_STEP = 10

MESH_AXES = ("x", "y", "c")
MESH = pl.DeviceIdType.MESH


def _params(*sem):
    return pltpu.CompilerParams(dimension_semantics=sem, vmem_limit_bytes=VMEM_LIMIT_BYTES)


def _tile(n, cap, mult=LANES):
    if n <= cap:
        return n
    best = None
    for t in range(mult, cap + 1, mult):
        if n % t == 0:
            best = t
    assert best is not None, (n, cap, mult)
    return best


def _dot(a, b, dims):
    return lax.dot_general(a, b, (dims, ((), ())), preferred_element_type=F32)


def _nn(a, b):
    return _dot(a, b, ((1,), (0,)))


def _nt(a, b):
    return _dot(a, b, ((1,), (1,)))


def _tn(a, b):
    return _dot(a, b, ((0,), (0,)))


def _me():
    return lax.axis_index("x"), lax.axis_index("y"), lax.axis_index("c")


def all_gather(x_loc, name):
    r, c = x_loc.shape

    def body(x_ref, out_ref, send_sems, recv_sems, local_sem):
        x, y, cc = _me()
        me, sibling = (x, y, cc), (x, y, 1 - cc)
        chips = [(1 - x, y), (x, 1 - y), (1 - x, 1 - y)]

        def rows(px, py, pc):
            return out_ref.at[4 * px + 2 * py + pc]

        def copy(k, block, to, src=None):
            return pltpu.make_async_remote_copy(
                src_ref=rows(*block) if src is None else src, dst_ref=rows(*block),
                send_sem=send_sems.at[k], recv_sem=recv_sems.at[k], device_id=to, device_id_type=MESH)

        mine = pltpu.make_async_copy(x_ref, rows(*me), local_sem)
        mine.start()
        first = [copy(0, me, sibling, src=x_ref)]
        first += [copy(1 + j, me, (*chip, cc), src=x_ref) for j, chip in enumerate(chips)]
        for cp in first:
            cp.start()
        passed = [copy(4 + j, (*chip, cc), sibling) for j, chip in enumerate(chips)]
        for j, chip in enumerate(chips):
            copy(1 + j, (*chip, cc), me).wait_recv()
            passed[j].start()
        copy(0, sibling, me).wait_recv()
        for j, chip in enumerate(chips):
            copy(4 + j, (*chip, 1 - cc), me).wait_recv()
        for cp in first + passed:
            cp.wait_send()
        mine.wait()

    return pl.pallas_call(
        body, name=name,
        out_shape=jax.ShapeDtypeStruct((N_DEV, r, c), x_loc.dtype),
        in_specs=[pl.BlockSpec(memory_space=pl.ANY)],
        out_specs=pl.BlockSpec(memory_space=pl.ANY),
        scratch_shapes=[pltpu.SemaphoreType.DMA((7,)), pltpu.SemaphoreType.DMA((7,)), pltpu.SemaphoreType.DMA(())],
    )(x_loc)


HBM_SPEC = pl.BlockSpec(memory_space=pltpu.HBM)
SEM_SPEC = pl.BlockSpec(memory_space=pltpu.SEMAPHORE)
N_PEERS = N_DEV - 1


def _peer(k):
    x, y, c = _me()
    return (1 - x if k & 4 else x, 1 - y if k & 2 else y, 1 - c if k & 1 else c)


def _exchange_copies(src_refs, land_refs, send_sems, recv_sems, scatter):
    x, y, c = _me()
    mine = 4 * x + 2 * y + c
    copies = []
    for n, (src_ref, land_ref) in enumerate(zip(src_refs, land_refs)):
        for k in range(1, N_DEV):
            px, py, pc = _peer(k)
            src = src_ref.at[4 * px + 2 * py + pc] if scatter else src_ref
            sem = n * N_PEERS + k - 1
            copies.append(pltpu.make_async_remote_copy(
                src_ref=src, dst_ref=land_ref.at[mine], send_sem=send_sems.at[sem], recv_sem=recv_sems.at[sem],
                device_id=(px, py, pc), device_id_type=MESH))
    return copies


def exchange_start(srcs, lands, name, scatter):
    n = len(srcs)

    def body(*refs):
        send_sems, recv_sems = refs[2 * n], refs[2 * n + 1]
        for cp in _exchange_copies(refs[:n], refs[n:2 * n], send_sems, recv_sems, scatter):
            cp.start()
        token = refs[-1]
        token[...] = jnp.zeros_like(token)

    outs = pl.pallas_call(
        body, name=name,
        out_shape=(pltpu.SemaphoreType.DMA((n * N_PEERS,)), pltpu.SemaphoreType.DMA((n * N_PEERS,)),
                   *(pltpu.HBM(a.shape, a.dtype) for a in (*srcs, *lands)), jax.ShapeDtypeStruct((8, LANES), F32)),
        in_specs=(HBM_SPEC,) * (2 * n),
        out_specs=(SEM_SPEC, SEM_SPEC, *((HBM_SPEC,) * (2 * n)), pl.BlockSpec(memory_space=pltpu.VMEM)),
        input_output_aliases={i: 2 + i for i in range(2 * n)},
        compiler_params=pltpu.CompilerParams(has_side_effects=pltpu.SideEffectType.DATAFLOW_SIDE_EFFECTING),
    )(*(pltpu.with_memory_space_constraint(a, pltpu.HBM) for a in (*srcs, *lands)))
    return outs[0], outs[1], outs[2:2 + n], outs[2 + n:2 + 2 * n], outs[-1]


def exchange_wait(started, after, name, scatter):
    send_sems, recv_sems, srcs, lands, _ = started
    n = len(srcs)

    def body(*refs):
        send_sems, recv_sems = refs[2 * n], refs[2 * n + 1]
        for cp in _exchange_copies(refs[:n], refs[n:2 * n], send_sems, recv_sems, scatter):
            cp.wait_send()
            cp.wait_recv()

    outs = pl.pallas_call(
        body, name=name,
        out_shape=tuple(pltpu.HBM(a.shape, a.dtype) for a in (*srcs, *lands)),
        in_specs=(*((HBM_SPEC,) * (2 * n)), SEM_SPEC, SEM_SPEC, pl.BlockSpec(memory_space=pl.ANY)),
        out_specs=(HBM_SPEC,) * (2 * n), input_output_aliases={i: i for i in range(2 * n)},
        compiler_params=pltpu.CompilerParams(has_side_effects=pltpu.SideEffectType.DATAFLOW_SIDE_EFFECTING),
    )(*srcs, *lands, send_sems, recv_sems, after)
    return outs[n:]


def after_token(small, started):
    return small + started[4][0, 0]


def sum_leading(x, name):
    n, r, c = x.shape
    tr = _tile(r, 512, 16)

    def body(x_ref, o_ref):
        acc = x_ref[0].astype(F32)
        for k in range(1, n):
            acc = acc + x_ref[k].astype(F32)
        o_ref[...] = acc

    return pl.pallas_call(
        body, name=name,
        out_shape=jax.ShapeDtypeStruct((r, c), F32),
        grid=(r // tr,),
        in_specs=[pl.BlockSpec((n, tr, c), lambda i: (0, i, 0))],
        out_specs=pl.BlockSpec((tr, c), lambda i: (i, 0)),
        compiler_params=_params("arbitrary"),
    )(x)


MM_VMEM_BUDGET = 36 * 1024 * 1024
GRID_STEP_AS_BYTES = 1 << 20


def _mm_tiles(m, n, a_row_bytes, b_col_bytes, out_bytes):
    tms = [c for c in (2048, 1024, 512, 256, 128, 64, 32, 16, 8) if m % c == 0] or [m]
    tns = [c for c in range(LANES, min(n, 2048) + 1, LANES) if n % c == 0] or [n]
    best = None
    for tm in tms:
        for tn in tns:
            vmem = 2 * (tm * a_row_bytes + tn * b_col_bytes) + 2 * tm * tn * out_bytes + tm * tn * 4
            if vmem > MM_VMEM_BUDGET:
                continue
            steps = (m // tm) * (n // tn)
            cost = steps * GRID_STEP_AS_BYTES + (m // tm) * n * b_col_bytes + m * a_row_bytes
            if best is None or cost < best[0]:
                best = (cost, tm, tn)
    assert best is not None, (m, n, a_row_bytes, b_col_bytes)
    return best[1], best[2]


def mm(pairs, *, trans_b, out_dtype, name, out_slab=False, bias=None):
    a0 = pairs[0][0]
    m = a0.shape[1] if a0.ndim == 3 else a0.shape[0]
    n = pairs[0][1].shape[0] if trans_b else pairs[0][1].shape[1]
    a_row_bytes = sum((b.shape[1] if trans_b else b.shape[0]) * a.dtype.itemsize for a, b in pairs)
    b_col_bytes = sum((b.shape[1] if trans_b else b.shape[0]) * b.dtype.itemsize for _, b in pairs)
    tm, tn = _mm_tiles(m, n, a_row_bytes, b_col_bytes, jnp.dtype(out_dtype).itemsize)
    slabs = [a.ndim == 3 for a, _ in pairs]
    n_pairs = len(pairs)

    def body(*refs):
        o_ref = refs[-1]
        acc = bias_ref = None
        if bias is not None:
            bias_ref = refs[2 * n_pairs]
        for i in range(n_pairs):
            a_ref, b_ref = refs[2 * i], refs[2 * i + 1]
            if slabs[i]:
                a = jnp.concatenate([a_ref[s].astype(BF16) for s in range(a_ref.shape[0])], axis=1)
            else:
                a = a_ref[...].astype(BF16)
            b = b_ref[...].astype(BF16)
            part = _nt(a, b) if trans_b else _nn(a, b)
            acc = part if acc is None else acc + part
        if bias_ref is not None:
            acc = acc + bias_ref[...]
        if out_slab:
            for s in range(tn // LANES):
                o_ref[s] = acc[:, s * LANES:(s + 1) * LANES].astype(out_dtype)
        else:
            o_ref[...] = acc.astype(out_dtype)

    in_specs, args = [], []
    for (a, b), slab in zip(pairs, slabs):
        if slab:
            in_specs.append(pl.BlockSpec((a.shape[0], tm, LANES), lambda i, j: (0, i, 0)))
        else:
            in_specs.append(pl.BlockSpec((tm, a.shape[1]), lambda i, j: (i, 0)))
        if trans_b:
            in_specs.append(pl.BlockSpec((tn, b.shape[1]), lambda i, j: (j, 0)))
        else:
            in_specs.append(pl.BlockSpec((b.shape[0], tn), lambda i, j: (0, j)))
        args += [a, b]
    if bias is not None:
        in_specs.append(pl.BlockSpec((1, tn), lambda i, j: (0, j)))
        args.append(bias)
    if out_slab:
        out_shape = jax.ShapeDtypeStruct((n // LANES, m, LANES), out_dtype)
        out_spec = pl.BlockSpec((tn // LANES, tm, LANES), lambda i, j: (j, i, 0))
    else:
        out_shape = jax.ShapeDtypeStruct((m, n), out_dtype)
        out_spec = pl.BlockSpec((tm, tn), lambda i, j: (i, j))
    return pl.pallas_call(
        body, name=name, out_shape=out_shape, grid=(m // tm, n // tn),
        in_specs=in_specs, out_specs=out_spec,
        compiler_params=_params("arbitrary", "arbitrary"),
    )(*args)


def mm_tn(a, b, *, name, out_dtype=F32, tk_cap=1536, tn_cap=1024, tm_cap=512):
    slab = a.ndim == 3
    m = a.shape[1] if slab else a.shape[0]
    k = a.shape[0] * LANES if slab else a.shape[1]
    n = b.shape[1]
    tk = _tile(k, tk_cap)
    tn = _tile(n, tn_cap)
    tm = _tile(m, tm_cap, 8)
    n_steps = m // tm

    def body(a_ref, b_ref, o_ref, acc_ref):
        step = pl.program_id(2)

        @pl.when(step == 0)
        def _():
            acc_ref[...] = jnp.zeros_like(acc_ref)

        bb = b_ref[...].astype(BF16)
        if slab:
            for s in range(tk // LANES):
                acc_ref[s * LANES:(s + 1) * LANES, :] += _tn(a_ref[s].astype(BF16), bb)
        else:
            acc_ref[...] += _tn(a_ref[...].astype(BF16), bb)

        @pl.when(step == n_steps - 1)
        def _():
            o_ref[...] = acc_ref[...].astype(out_dtype)

    if slab:
        a_spec = pl.BlockSpec((tk // LANES, tm, LANES), lambda i, j, t: (i, t, 0))
    else:
        a_spec = pl.BlockSpec((tm, tk), lambda i, j, t: (t, i))
    return pl.pallas_call(
        body, name=name, out_shape=jax.ShapeDtypeStruct((k, n), out_dtype), grid=(k // tk, n // tn, n_steps),
        in_specs=[a_spec, pl.BlockSpec((tm, tn), lambda i, j, t: (t, j))],
        out_specs=pl.BlockSpec((tk, tn), lambda i, j, t: (i, j)),
        scratch_shapes=[pltpu.VMEM((tk, tn), F32)],
        compiler_params=_params("arbitrary", "arbitrary", "arbitrary"),
    )(a, b)


def _row_spec(d, k):
    return pl.BlockSpec((1, 1, d), lambda b, i: (6 * b + k, 0, 0))


def modulate(x, mod, k_shift, k_scale, bl, name):
    t, d = x.shape
    s = t // bl
    tm = _tile(s, 512, 8)
    nt = s // tm

    def body(x_ref, sh_ref, sc_ref, o_ref):
        o_ref[...] = (x_ref[...] * (1.0 + sc_ref[0]) + sh_ref[0]).astype(BF16)

    return pl.pallas_call(
        body, name=name, out_shape=jax.ShapeDtypeStruct((t, d), BF16), grid=(bl, nt),
        in_specs=[pl.BlockSpec((tm, d), lambda b, i: (b * nt + i, 0)), _row_spec(d, k_shift), _row_spec(d, k_scale)],
        out_specs=pl.BlockSpec((tm, d), lambda b, i: (b * nt + i, 0)),
        compiler_params=_params("arbitrary", "arbitrary"),
    )(x, mod, mod)


def _layer_norm_stats(r):
    mu = jnp.mean(r, axis=-1, keepdims=True)
    rc = r - mu
    var = jnp.mean(rc * rc, axis=-1, keepdims=True)
    rstd = lax.rsqrt(var + NORM_EPS)
    return rc * rstd, rstd


def residual_layer_norm(x, y, mod, k_gate, g, b, bl, name, next_mod=None):
    t, d = x.shape
    s = t // bl
    tm = _tile(s, 512, 8)
    nt = s // tm
    has_next = next_mod is not None

    def body(*refs):
        x_ref, y_ref, gt_ref, g_ref, b_ref = refs[:5]
        rest = refs[5:]
        if has_next:
            sh_ref, sc_ref, o_ref, r_ref, u_ref = rest
        else:
            o_ref, r_ref = rest
        r = ALPHA * x_ref[...] + (1.0 + gt_ref[0]) * y_ref[...]
        xhat, _ = _layer_norm_stats(r)
        out = xhat * g_ref[...] + b_ref[...]
        o_ref[...] = out
        r_ref[...] = r
        if has_next:
            u_ref[...] = (out * (1.0 + sc_ref[0]) + sh_ref[0]).astype(BF16)

    tok = pl.BlockSpec((tm, d), lambda bb, i: (bb * nt + i, 0))
    vec = pl.BlockSpec((1, d), lambda bb, i: (0, 0))
    in_specs = [tok, tok, _row_spec(d, k_gate), vec, vec]
    args = [x, y, mod, g, b]
    out_shape = [jax.ShapeDtypeStruct((t, d), F32), jax.ShapeDtypeStruct((t, d), F32)]
    out_specs = [tok, tok]
    if has_next:
        in_specs += [_row_spec(d, next_mod[0]), _row_spec(d, next_mod[1])]
        args += [mod if len(next_mod) == 2 else next_mod[2]] * 2
        out_shape.append(jax.ShapeDtypeStruct((t, d), BF16))
        out_specs.append(tok)
    return pl.pallas_call(
        body, name=name, out_shape=out_shape, grid=(bl, nt), in_specs=in_specs, out_specs=out_specs,
        compiler_params=_params("arbitrary", "arbitrary"),
    )(*args)


def loss_head(xo, target, name):
    t, d = xo.shape
    tm = _tile(t, 512, 8)

    def body(x_ref, t_ref, l_ref, dx_ref):
        @pl.when(pl.program_id(0) == 0)
        def _():
            l_ref[...] = jnp.zeros_like(l_ref)

        e = x_ref[...] - t_ref[...]
        l_ref[...] += jnp.sum(e * e, axis=0, keepdims=True) * (0.5 / d)
        dx_ref[...] = e * (1.0 / d)

    tok = pl.BlockSpec((tm, d), lambda i: (i, 0))
    return pl.pallas_call(
        body, name=name,
        out_shape=[jax.ShapeDtypeStruct((1, d), F32), jax.ShapeDtypeStruct((t, d), F32)],
        grid=(t // tm,), in_specs=[tok, tok],
        out_specs=[pl.BlockSpec((1, d), lambda i: (0, 0)), tok],
        compiler_params=_params("arbitrary"),
    )(xo, target)


def sublayer_backward(d_a, bl, name, *, du=None, scale=None, x_in=None, ln=None):
    t, d = d_a.shape
    s = t // bl
    tm = _tile(s, 512, 8)
    nt = s // tm
    has_mod = du is not None
    has_ln = ln is not None
    assert has_mod or has_ln
    assert has_ln or x_in is not None

    def body(*refs):
        refs = list(refs)
        da_ref = refs.pop(0)
        if has_mod:
            du_ref, sc_ref = refs.pop(0), refs.pop(0)
        if has_ln:
            r_ref, y_ref, g_ref, b_ref, gt_ref = (refs.pop(0) for _ in range(5))
        elif has_mod:
            xin_ref = refs.pop(0)
        dx_ref = refs.pop(0)
        if has_ln:
            dy_ref, dg_ref, db_ref, dgt_ref = (refs.pop(0) for _ in range(4))
        if has_mod:
            dsc_ref, dsh_ref = refs.pop(0), refs.pop(0)
        first_tile = pl.program_id(1) == 0
        first_step = jnp.logical_and(pl.program_id(0) == 0, first_tile)

        dout = da_ref[...]
        if has_ln:
            xhat, rstd = _layer_norm_stats(r_ref[...])
        if has_mod:
            duv = du_ref[...]
            dout = dout + duv * (1.0 + sc_ref[0])
            xin = xhat * g_ref[...] + b_ref[...] if has_ln else xin_ref[...]

            @pl.when(first_tile)
            def _():
                dsc_ref[...] = jnp.zeros_like(dsc_ref)
                dsh_ref[...] = jnp.zeros_like(dsh_ref)

            dsc_ref[0] += jnp.sum(duv * xin, axis=0, keepdims=True)
            dsh_ref[0] += jnp.sum(duv, axis=0, keepdims=True)
        if not has_ln:
            dx_ref[...] = dout
            return

        @pl.when(first_step)
        def _():
            dg_ref[...] = jnp.zeros_like(dg_ref)
            db_ref[...] = jnp.zeros_like(db_ref)

        @pl.when(first_tile)
        def _():
            dgt_ref[...] = jnp.zeros_like(dgt_ref)

        dg_ref[...] += jnp.sum(dout * xhat, axis=0, keepdims=True)
        db_ref[...] += jnp.sum(dout, axis=0, keepdims=True)
        dxh = dout * g_ref[...]
        dr = rstd * (dxh - jnp.mean(dxh, axis=-1, keepdims=True) - xhat * jnp.mean(dxh * xhat, axis=-1, keepdims=True))
        dx_ref[...] = ALPHA * dr
        dy_ref[...] = ((1.0 + gt_ref[0]) * dr).astype(BF16)
        dgt_ref[0] += jnp.sum(dr * y_ref[...], axis=0, keepdims=True)

    tok = pl.BlockSpec((tm, d), lambda bb, i: (bb * nt + i, 0))
    vec = pl.BlockSpec((1, d), lambda bb, i: (0, 0))
    seq = pl.BlockSpec((1, 1, d), lambda bb, i: (bb, 0, 0))
    in_specs, args = [tok], [d_a]
    if has_mod:
        in_specs += [tok, _row_spec(d, scale[1])]
        args += [du, scale[0]]
    if has_ln:
        r, y, g, b, gate = ln
        in_specs += [tok, tok, vec, vec, _row_spec(d, gate[1])]
        args += [r, y, g, b, gate[0]]
    elif has_mod:
        in_specs.append(tok)
        args.append(x_in)
    names = ["dx"]
    out_shape, out_specs = [jax.ShapeDtypeStruct((t, d), F32)], [tok]
    if has_ln:
        names += ["dy", "dg", "db", "dgate"]
        out_shape += [jax.ShapeDtypeStruct((t, d), BF16), jax.ShapeDtypeStruct((1, d), F32),
                      jax.ShapeDtypeStruct((1, d), F32), jax.ShapeDtypeStruct((bl, 1, d), F32)]
        out_specs += [tok, vec, vec, seq]
    if has_mod:
        names += ["dscale", "dshift"]
        out_shape += [jax.ShapeDtypeStruct((bl, 1, d), F32)] * 2
        out_specs += [seq, seq]
    outs = pl.pallas_call(
        body, name=name, out_shape=out_shape, grid=(bl, nt), in_specs=in_specs, out_specs=out_specs,
        compiler_params=_params("arbitrary", "arbitrary"),
    )(*args)
    return dict(zip(names, outs))


def _silu(a):
    return a * jax.nn.sigmoid(a)


def silu_rows(a, name):
    def body(a_ref, o_ref):
        o_ref[...] = _silu(a_ref[...]).astype(BF16)

    return pl.pallas_call(body, name=name, out_shape=jax.ShapeDtypeStruct(a.shape, BF16))(a)


def _swiglu_tiles(t, f):
    return _tile(t, 512, 8), _tile(f, 1536)


def swiglu_in(u, wt_gate, wt_up, name):
    t, d = u.shape
    f = wt_gate.shape[0]
    tm, tf = _swiglu_tiles(t, f)

    def body(u_ref, g_ref, w_ref, a_ref, b_ref, h_ref):
        uv = u_ref[...]
        a = _nt(uv, g_ref[...])
        b = _nt(uv, w_ref[...])
        a_ref[...] = a.astype(BF16)
        b_ref[...] = b.astype(BF16)
        h_ref[...] = (_silu(a) * b).astype(BF16)

    w_spec = pl.BlockSpec((tf, d), lambda i, j: (j, 0))
    o_spec = pl.BlockSpec((tm, tf), lambda i, j: (i, j))
    return pl.pallas_call(
        body, name=name,
        out_shape=[jax.ShapeDtypeStruct((t, f), BF16)] * 3,
        grid=(t // tm, f // tf), in_specs=[pl.BlockSpec((tm, d), lambda i, j: (i, 0)), w_spec, w_spec],
        out_specs=[o_spec, o_spec, o_spec], compiler_params=_params("arbitrary", "arbitrary"),
    )(u, wt_gate, wt_up)


def swiglu_out_backward(dy, w_down, a, b, name):
    t, d = dy.shape
    f = w_down.shape[0]
    tm, tf = _swiglu_tiles(t, f)

    def body(dy_ref, w_ref, a_ref, b_ref, da_ref, db_ref):
        dh = _nt(dy_ref[...], w_ref[...])
        av = a_ref[...].astype(F32)
        sig = jax.nn.sigmoid(av)
        da_ref[...] = (dh * b_ref[...].astype(F32) * (sig * (1.0 + av * (1.0 - sig)))).astype(BF16)
        db_ref[...] = (dh * (av * sig)).astype(BF16)

    spec = pl.BlockSpec((tm, tf), lambda i, j: (i, j))
    return pl.pallas_call(
        body, name=name, out_shape=[jax.ShapeDtypeStruct((t, f), BF16)] * 2, grid=(t // tm, f // tf),
        in_specs=[pl.BlockSpec((tm, d), lambda i, j: (i, 0)), pl.BlockSpec((tf, d), lambda i, j: (j, 0)), spec, spec],
        out_specs=[spec, spec], compiler_params=_params("arbitrary", "arbitrary"),
    )(dy, w_down, a, b)


def rope_tables(pos, inv_freq, sign, name):
    t = pos.shape[0]
    tm = _tile(t, 512, 8)

    def body(p_ref, f_ref, s_ref, c_out, s_out):
        ang = p_ref[...] * f_ref[...]
        c_out[...] = jnp.cos(ang)
        s_out[...] = jnp.sin(ang) * s_ref[...]

    vec = pl.BlockSpec((1, LANES), lambda i: (0, 0))
    tab = pl.BlockSpec((tm, LANES), lambda i: (i, 0))
    return pl.pallas_call(
        body, name=name, out_shape=[jax.ShapeDtypeStruct((t, LANES), F32)] * 2, grid=(t // tm,),
        in_specs=[pl.BlockSpec((tm, 1), lambda i: (i, 0)), vec, vec], out_specs=[tab, tab],
        compiler_params=_params("arbitrary"),
    )(pos, inv_freq, sign)


def _rot_half(v):
    lane = lax.broadcasted_iota(jnp.int32, v.shape, v.ndim - 1)
    up = pltpu.roll(v, LANES - MLA_ROPE // 2, v.ndim - 1)
    down = pltpu.roll(v, MLA_ROPE // 2, v.ndim - 1)
    return jnp.where(lane % MLA_ROPE < MLA_ROPE // 2, up, down)


def _rope(v, cos, sin_signed):
    return v * cos + _rot_half(v) * sin_signed


def _rope_transposed(dv, cos, sin_signed):
    return dv * cos + _rot_half(dv * sin_signed)


def rope_slabs(v, cos, sin_signed, out_dtype, name, transposed=False):
    ns, t, _ = v.shape
    tm = _tile(t, 512, 8)
    fn = _rope_transposed if transposed else _rope

    def body(v_ref, c_ref, s_ref, o_ref):
        o_ref[0] = fn(v_ref[0].astype(F32), c_ref[...], s_ref[...]).astype(out_dtype)

    tab = pl.BlockSpec((tm, LANES), lambda j, i: (i, 0))
    spec = pl.BlockSpec((1, tm, LANES), lambda j, i: (j, i, 0))
    return pl.pallas_call(
        body, name=name, out_shape=jax.ShapeDtypeStruct(v.shape, out_dtype), grid=(ns, t // tm),
        in_specs=[spec, tab, tab], out_specs=spec, compiler_params=_params("arbitrary", "arbitrary"),
    )(v, cos, sin_signed)


def _rms(x):
    rinv = lax.rsqrt(jnp.mean(x * x, axis=-1, keepdims=True) + NORM_EPS)
    return x * rinv, rinv


def mla_latents_forward(h_in, g_q, g_kv, cos, sin_signed, name):
    t = h_in.shape[0]
    tm = _tile(t, 512, 8)

    def body(h_ref, gq_ref, gkv_ref, c_ref, s_ref, cq_ref, ckv_ref, kr_ref):
        cq_ref[...] = (_rms(h_ref[:, 0:MLA_QR])[0] * gq_ref[...]).astype(BF16)
        ckv_ref[...] = (_rms(h_ref[:, MLA_QR:MLA_QR + MLA_KVR])[0] * gkv_ref[...]).astype(BF16)
        kr_ref[...] = _rope(h_ref[:, MLA_QR + MLA_KVR:], c_ref[...], s_ref[...]).astype(BF16)

    def tok(w):
        return pl.BlockSpec((tm, w), lambda i: (i, 0))

    def vec(w):
        return pl.BlockSpec((1, w), lambda i: (0, 0))

    return pl.pallas_call(
        body, name=name,
        out_shape=[jax.ShapeDtypeStruct((t, MLA_QR), BF16), jax.ShapeDtypeStruct((t, MLA_KVR), BF16),
                   jax.ShapeDtypeStruct((t, LANES), BF16)],
        grid=(t // tm,),
        in_specs=[tok(h_in.shape[1]), vec(MLA_QR), vec(MLA_KVR), tok(LANES), tok(LANES)],
        out_specs=[tok(MLA_QR), tok(MLA_KVR), tok(LANES)],
        compiler_params=_params("arbitrary"),
    )(h_in, g_q, g_kv, cos, sin_signed)


def mla_latents_backward(h_in, dcq, dckv, dkr, g_q, g_kv, cos, sin_signed, name):
    t, w = h_in.shape
    tm = _tile(t, 512, 8)

    def body(h_ref, dcq_ref, dckv_ref, dkr_ref, gq_ref, gkv_ref, c_ref, s_ref, dh_ref, dgq_ref, dgkv_ref):
        @pl.when(pl.program_id(0) == 0)
        def _():
            dgq_ref[...] = jnp.zeros_like(dgq_ref)
            dgkv_ref[...] = jnp.zeros_like(dgkv_ref)

        def rms_bwd(x, dc, g_ref, dg_ref):
            xn, rinv = _rms(x)
            dg_ref[...] += jnp.sum(dc * xn, axis=0, keepdims=True)
            dxn = dc * g_ref[...]
            return rinv * (dxn - xn * jnp.mean(dxn * xn, axis=-1, keepdims=True))

        dq = rms_bwd(h_ref[:, 0:MLA_QR], dcq_ref[...], gq_ref, dgq_ref)
        dkv = rms_bwd(h_ref[:, MLA_QR:MLA_QR + MLA_KVR], dckv_ref[...], gkv_ref, dgkv_ref)
        dr = _rope_transposed(dkr_ref[...], c_ref[...], s_ref[...])
        dh_ref[...] = jnp.concatenate([dq, dkv, dr], axis=1).astype(BF16)

    def tok(ww):
        return pl.BlockSpec((tm, ww), lambda i: (i, 0))

    def vec(ww):
        return pl.BlockSpec((1, ww), lambda i: (0, 0))

    return pl.pallas_call(
        body, name=name,
        out_shape=[jax.ShapeDtypeStruct((t, w), BF16), jax.ShapeDtypeStruct((1, MLA_QR), F32),
                   jax.ShapeDtypeStruct((1, MLA_KVR), F32)],
        grid=(t // tm,),
        in_specs=[tok(w), tok(MLA_QR), tok(MLA_KVR), tok(LANES), vec(MLA_QR), vec(MLA_KVR), tok(LANES), tok(LANES)],
        out_specs=[tok(w), vec(MLA_QR), vec(MLA_KVR)],
        compiler_params=_params("arbitrary"),
    )(h_in, dcq, dckv, dkr, g_q, g_kv, cos, sin_signed)


def _tri(n, lower):
    r = lax.broadcasted_iota(jnp.int32, (n, n), 0)
    c = lax.broadcasted_iota(jnp.int32, (n, n), 1)
    return jnp.where(r >= c if lower else r <= c, 1.0, 0.0).astype(F32)


def _dot_exact(tri, v):
    hi = v.astype(BF16)
    mid = (v - hi.astype(F32)).astype(BF16)
    lo = (v - hi.astype(F32) - mid.astype(F32)).astype(BF16)
    t = tri.astype(BF16)
    return _nn(t, hi) + _nn(t, mid) + _nn(t, lo)


def fox_gate_forward(z, b_f, bl, name):
    t = z.shape[0]
    s = t // bl
    ch = LANES
    n_ch = s // ch

    def body(z_ref, b_ref, f_ref, fs_ref):
        tri = _tri(ch, True)
        carry = jnp.zeros((1, LANES), F32)
        for k in range(n_ch):
            x = z_ref[k * ch:(k + 1) * ch, :] + b_ref[...]
            logf = jnp.minimum(x, 0.0) - jnp.log(1.0 + jnp.exp(-jnp.abs(x)))
            cs = _dot_exact(tri, logf) + carry
            carry = cs[ch - 1:ch, :]
            f_ref[k * ch:(k + 1) * ch, :] = cs
            for h in range(FOX_HEADS):
                fs_ref[h, k * ch:(k + 1) * ch, :] = jnp.broadcast_to(cs[:, h:h + 1], (ch, LANES))

    return pl.pallas_call(
        body, name=name,
        out_shape=[jax.ShapeDtypeStruct((t, LANES), F32), jax.ShapeDtypeStruct((FOX_HEADS, t, LANES), F32)],
        grid=(bl,),
        in_specs=[pl.BlockSpec((s, LANES), lambda b: (b, 0)), pl.BlockSpec((1, LANES), lambda b: (0, 0))],
        out_specs=[pl.BlockSpec((s, LANES), lambda b: (b, 0)),
                   pl.BlockSpec((FOX_HEADS, s, LANES), lambda b: (0, b, 0))],
        compiler_params=_params("arbitrary"),
    )(z, b_f)


def fox_gate_backward(z, b_f, df, bl, name):
    t = z.shape[0]
    s = t // bl
    ch = LANES
    n_ch = s // ch

    def body(z_ref, b_ref, df_ref, dz_ref, db_ref):
        @pl.when(pl.program_id(0) == 0)
        def _():
            db_ref[...] = jnp.zeros_like(db_ref)

        tri = _tri(ch, False)
        carry = jnp.zeros((1, LANES), F32)
        for k in reversed(range(n_ch)):
            cs = _dot_exact(tri, df_ref[k * ch:(k + 1) * ch, :]) + carry
            carry = cs[0:1, :]
            x = z_ref[k * ch:(k + 1) * ch, :] + b_ref[...]
            dz = cs * (1.0 - jax.nn.sigmoid(x))
            dz_ref[k * ch:(k + 1) * ch, :] = dz
            db_ref[...] += jnp.sum(dz, axis=0, keepdims=True)

    tok = pl.BlockSpec((s, LANES), lambda b: (b, 0))
    vec = pl.BlockSpec((1, LANES), lambda b: (0, 0))
    return pl.pallas_call(
        body, name=name,
        out_shape=[jax.ShapeDtypeStruct((t, LANES), F32), jax.ShapeDtypeStruct((1, LANES), F32)],
        grid=(bl,), in_specs=[tok, vec, tok], out_specs=[tok, vec],
        compiler_params=_params("arbitrary"),
    )(z, b_f, df)


NEG_INF = float("-inf")


def _attn_tiles(s):
    return _tile(s, 512, 8)


def attention_forward(kind, ops, bl, scale, name):
    fox = kind == "fox"
    if fox:
        assert math.frexp(scale)[0] == 0.5, "the FoX scale is folded into bf16 queries: it must be a power of two"
        qkv, fq, fk = ops
        t = qkv.shape[1]
        n_pair = FOX_HEADS // 2
    else:
        qn, qr, kn, kr, v = ops
        t = qn.shape[1]
        n_pair = MLA_HEADS // 2
    s = t // bl
    tq = _attn_tiles(s)
    nq = s // tq
    half = LANES // 2

    def body(*refs):
        if fox:
            q_ref, k_ref, v_ref, fq_ref, fk_ref, o_ref, lse_ref, o32_ref = refs
        else:
            qn_ref, qr_ref, kn_ref, kr_ref, v_ref, o_ref, lse_ref = refs
        i = pl.program_id(2)
        row = lax.broadcasted_iota(jnp.int32, (tq, tq), 0)
        col = lax.broadcasted_iota(jnp.int32, (tq, tq), 1)
        heads = []
        for e in range(2):
            sl = slice(e * half, (e + 1) * half)
            if fox:
                heads.append((sl, q_ref[0, :, sl] * jnp.asarray(scale, BF16), None))
            else:
                heads.append((sl, qn_ref[e], qr_ref[0, :, sl]))
        dv = half if fox else LANES

        def wide(stat):
            return jnp.concatenate([stat] * (tq // LANES), axis=1)

        def step(j, carry, masked):
            rows = pl.ds(pl.multiple_of(j * tq, tq), tq)
            new = []
            for e, (sl, qa, qb) in enumerate(heads):
                m, l, acc = carry[e]
                if fox:
                    sc = _nt(qa, k_ref[0, rows, sl]) + wide(fq_ref[e]) - fk_ref[0, j, e:e + 1, :]
                    vv = v_ref[0, rows, sl]
                else:
                    sc = (_nt(qa, kn_ref[e, rows, :]) + _nt(qb, kr_ref[rows, 0:half])) * scale
                    vv = v_ref[e, rows, :]
                if masked:
                    sc = jnp.where(row >= col, sc, NEG_INF)
                m_new = jnp.maximum(m, jnp.max(sc, axis=1, keepdims=True))
                p = jnp.exp(sc - m_new)
                a = jnp.exp(m - m_new)
                l = a * l + jnp.sum(p, axis=1, keepdims=True)
                p_hi = p.astype(BF16)
                acc = a * acc + _nn(p_hi, vv)
                if fox:
                    acc = acc + _nn((p - p_hi.astype(F32)).astype(BF16), vv)
                new.append((m_new, l, acc))
            return tuple(new)

        init = (jnp.full((tq, 1), NEG_INF, F32), jnp.zeros((tq, 1), F32), jnp.zeros((tq, dv), F32))
        carry = step(i, (init, init), True)
        carry = lax.fori_loop(0, i, lambda j, c: step(j, c, False), carry)
        outs = [acc / l for _, l, acc in carry]
        for e, (m, l, _) in enumerate(carry):
            lse_ref[e] = jnp.broadcast_to(m + jnp.log(l), (tq, LANES))
        if fox:
            o32 = jnp.concatenate(outs, axis=1)
            o32_ref[0] = o32
            o_ref[0] = o32.astype(BF16)
        else:
            o_ref[0] = outs[0].astype(BF16)
            o_ref[1] = outs[1].astype(BF16)

    def q_idx(b, g, i):
        return (g, b * nq + i, 0)

    if fox:
        nk = fk.shape[1]
        in_specs = [pl.BlockSpec((1, tq, LANES), q_idx),
                    pl.BlockSpec((1, s, LANES), lambda b, g, i: (n_pair + g, b, 0)),
                    pl.BlockSpec((1, s, LANES), lambda b, g, i: (2 * n_pair + g, b, 0)),
                    pl.BlockSpec((2, tq, LANES), q_idx),
                    pl.BlockSpec((1, nk, 8, tq), lambda b, g, i: (b * n_pair + g, 0, 0, 0))]
        args = [qkv, qkv, qkv, fq, fk]
        o_spec = pl.BlockSpec((1, tq, LANES), q_idx)
    else:
        in_specs = [pl.BlockSpec((2, tq, LANES), q_idx),
                    pl.BlockSpec((1, tq, LANES), q_idx),
                    pl.BlockSpec((2, s, LANES), lambda b, g, i: (g, b, 0)),
                    pl.BlockSpec((s, LANES), lambda b, g, i: (b, 0)),
                    pl.BlockSpec((2, s, LANES), lambda b, g, i: (g, b, 0))]
        args = [qn, qr, kn, kr, v]
        o_spec = pl.BlockSpec((2, tq, LANES), q_idx)
    out_shape = [jax.ShapeDtypeStruct((8, t, LANES), BF16), jax.ShapeDtypeStruct((2 * n_pair, t, LANES), F32)]
    out_specs = [o_spec, pl.BlockSpec((2, tq, LANES), q_idx)]
    if fox:
        out_shape.append(jax.ShapeDtypeStruct((8, t, LANES), F32))
        out_specs.append(o_spec)
    outs = pl.pallas_call(
        body, name=name, out_shape=out_shape, grid=(bl, n_pair, nq), in_specs=in_specs, out_specs=out_specs,
        compiler_params=_params("arbitrary", "arbitrary", "arbitrary"),
    )(*args)
    return (outs[0], outs[1], outs[2] if fox else outs[0])


def attention_backward(kind, ops, o, do, lse, bl, scale, name):
    fox = kind == "fox"
    if fox:
        qkv, fq, fk = ops
        t = qkv.shape[1]
        n_pair = FOX_HEADS // 2
    else:
        qn, qr, kn, kr, v = ops
        t = qn.shape[1]
        n_pair = MLA_HEADS // 2
    s = t // bl
    tq = _attn_tiles(s)
    nq = s // tq
    half = LANES // 2

    def body(*refs):
        if fox:
            (q_ref, k_ref, v_ref, fq_ref, fk_ref, o_ref, do_ref, lse_ref,
             dq_ref, dk_ref, dv_ref, dfk_ref, delta_scr, qt_scr, dot_scr) = refs
        else:
            (qn_ref, qr_ref, kn_ref, kr_ref, v_ref, o_ref, do_ref, lse_ref,
             dqn_ref, dqr_ref, dkn_ref, dv_ref, dkr_ref, delta_scr, qt_scr, qrt_scr, dot_scr) = refs
        g, j = pl.program_id(1), pl.program_id(2)
        row = lax.broadcasted_iota(jnp.int32, (tq, tq), 0)
        col = lax.broadcasted_iota(jnp.int32, (tq, tq), 1)
        krows = pl.ds(pl.multiple_of(j * tq, tq), tq)

        def transposed(v):
            return v.astype(F32).T.astype(BF16)

        def wide(stat):
            return jnp.concatenate([stat] * (tq // LANES), axis=1)

        @pl.when(j == 0)
        def _():
            if fox:
                dq_ref[...] = jnp.zeros_like(dq_ref)
            else:
                dqn_ref[...] = jnp.zeros_like(dqn_ref)
                dqr_ref[...] = jnp.zeros_like(dqr_ref)
            for ii in range(nq):
                rws = slice(ii * tq, (ii + 1) * tq)
                deltas = []
                if fox:
                    prod = do_ref[0, rws, :].astype(F32) * o_ref[0, rws, :].astype(F32)
                    for e in range(2):
                        deltas.append(jnp.sum(prod[:, e * half:(e + 1) * half], axis=1, keepdims=True))
                    qt_scr[ii] = transposed(q_ref[0, rws, :])
                    dot_scr[ii] = transposed(do_ref[0, rws, :])
                else:
                    for e in range(2):
                        prod = do_ref[e, rws, :].astype(F32) * o_ref[e, rws, :].astype(F32)
                        deltas.append(jnp.sum(prod, axis=1, keepdims=True))
                        qt_scr[e, ii] = transposed(qn_ref[e, rws, :])
                        dot_scr[e, ii] = transposed(do_ref[e, rws, :])
                    qrt_scr[ii] = transposed(qr_ref[0, rws, :])
                for e in range(2):
                    delta_scr[e, rws, :] = jnp.broadcast_to(deltas[e], (tq, LANES))

        if fox:
            dfk_ref[...] = jnp.zeros_like(dfk_ref)
        else:
            @pl.when(jnp.logical_and(g == 0, j == 0))
            def _():
                dkr_ref[...] = jnp.zeros_like(dkr_ref)

        heads = []
        for e in range(2):
            sl = slice(e * half, (e + 1) * half)
            if fox:
                heads.append((sl, k_ref[0, :, sl], v_ref[0, :, sl], fk_ref[0, 0, e:e + 1, :]))
            else:
                heads.append((sl, kn_ref[e], v_ref[e], kr_ref[krows, 0:half]))
        dk_w = dv_w = half if fox else LANES

        def step(i, carry, masked):
            rows = pl.ds(pl.multiple_of(i * tq, tq), tq)
            new = []
            for e, (sl, k_e, v_e, x_e) in enumerate(heads):
                dk_acc, dv_acc, last = carry[e]
                if fox:
                    do_i = do_ref[0, rows, sl]
                    sc = _nt(q_ref[0, rows, sl], k_e) * scale + wide(fq_ref[e, rows, :]) - x_e
                else:
                    do_i = do_ref[e, rows, :]
                    sc = (_nt(qn_ref[e, rows, :], k_e) + _nt(qr_ref[0, rows, sl], x_e)) * scale
                if masked:
                    sc = jnp.where(row >= col, sc, NEG_INF)
                p = jnp.exp(sc - wide(lse_ref[e, rows, :]))
                dp = _nt(do_i, v_e)
                ds = p * (dp - wide(delta_scr[e, rows, :]))
                dsb = (ds * scale).astype(BF16)
                if fox:
                    fsl = slice(e * half, (e + 1) * half)
                    dv_acc = dv_acc + _nn(dot_scr[i, fsl, :], p.astype(BF16))
                    dk_acc = dk_acc + _nn(qt_scr[i, fsl, :], dsb)
                    dq_ref[0, rows, sl] += _nn(dsb, k_e)
                    last = last - jnp.sum(ds, axis=0, keepdims=True)
                else:
                    dv_acc = dv_acc + _nn(dot_scr[e, i], p.astype(BF16))
                    dk_acc = dk_acc + _nn(qt_scr[e, i], dsb)
                    dqn_ref[e, rows, :] += _nn(dsb, k_e)
                    dqr_ref[0, rows, sl] += _nn(dsb, x_e)
                    last = last + _nn(qrt_scr[i, e * half:(e + 1) * half, :], dsb)
                new.append((dk_acc, dv_acc, last))
            return tuple(new)

        last0 = jnp.zeros((1, tq), F32) if fox else jnp.zeros((half, tq), F32)
        init = (jnp.zeros((dk_w, tq), F32), jnp.zeros((dv_w, tq), F32), last0)
        carry = step(j, (init, init), True)
        carry = lax.fori_loop(j + 1, nq, lambda i, c: step(i, c, False), carry)
        if fox:
            for e in range(2):
                dfk_ref[0, 0, e:e + 1, :] = carry[e][2]
            dk_ref[0] = jnp.concatenate([carry[0][0], carry[1][0]], axis=0).T.astype(BF16)
            dv_ref[0] = jnp.concatenate([carry[0][1], carry[1][1]], axis=0).T.astype(BF16)
        else:
            for e in range(2):
                dkn_ref[e] = carry[e][0].T.astype(BF16)
                dv_ref[e] = carry[e][1].T.astype(BF16)
            dkr_t = carry[0][2] + carry[1][2]
            dkr_ref[krows, :] += jnp.concatenate([dkr_t, jnp.zeros_like(dkr_t)], axis=0).T

    def whole(b, g, j):
        return (g, b, 0)

    def kblk(b, g, j):
        return (g, b * nq + j, 0)

    if fox:
        in_specs = [pl.BlockSpec((1, s, LANES), whole),
                    pl.BlockSpec((1, tq, LANES), lambda b, g, j: (n_pair + g, b * nq + j, 0)),
                    pl.BlockSpec((1, tq, LANES), lambda b, g, j: (2 * n_pair + g, b * nq + j, 0)),
                    pl.BlockSpec((2, s, LANES), whole),
                    pl.BlockSpec((1, 1, 8, tq), lambda b, g, j: (b * n_pair + g, j, 0, 0)),
                    pl.BlockSpec((1, s, LANES), whole), pl.BlockSpec((1, s, LANES), whole),
                    pl.BlockSpec((2, s, LANES), whole)]
        args = [qkv, qkv, qkv, fq, fk, o, do, lse]
        out_shape = [jax.ShapeDtypeStruct((8, t, LANES), F32), jax.ShapeDtypeStruct((8, t, LANES), BF16),
                     jax.ShapeDtypeStruct((8, t, LANES), BF16), jax.ShapeDtypeStruct(fk.shape, F32)]
        out_specs = [pl.BlockSpec((1, s, LANES), whole), pl.BlockSpec((1, tq, LANES), kblk),
                     pl.BlockSpec((1, tq, LANES), kblk),
                     pl.BlockSpec((1, 1, 8, tq), lambda b, g, j: (b * n_pair + g, j, 0, 0))]
    else:
        pair = pl.BlockSpec((2, s, LANES), whole)
        pair_k = pl.BlockSpec((2, tq, LANES), kblk)
        in_specs = [pair, pl.BlockSpec((1, s, LANES), whole), pair_k,
                    pl.BlockSpec((s, LANES), lambda b, g, j: (b, 0)), pair_k,
                    pair, pair, pair]
        args = [qn, qr, kn, kr, v, o, do, lse]
        out_shape = [jax.ShapeDtypeStruct((8, t, LANES), F32), jax.ShapeDtypeStruct((4, t, LANES), F32),
                     jax.ShapeDtypeStruct((8, t, LANES), BF16), jax.ShapeDtypeStruct((8, t, LANES), BF16),
                     jax.ShapeDtypeStruct((t, LANES), F32)]
        out_specs = [pair, pl.BlockSpec((1, s, LANES), whole), pair_k, pair_k,
                     pl.BlockSpec((s, LANES), lambda b, g, j: (b, 0))]
    t_blocks = pltpu.VMEM((nq, LANES, tq), BF16)
    t_pairs = pltpu.VMEM((2, nq, LANES, tq), BF16)
    scratch = [pltpu.VMEM((2, s, LANES), F32)] + ([t_blocks, t_blocks] if fox else [t_pairs, t_blocks, t_pairs])
    return pl.pallas_call(
        body, name=name, out_shape=out_shape, grid=(bl, n_pair, nq), in_specs=in_specs, out_specs=out_specs,
        scratch_shapes=scratch, compiler_params=_params("arbitrary", "arbitrary", "arbitrary"),
    )(*args)


def adamw(w, g, m, v, name):
    shape = w.shape
    c = shape[-1]
    r = w.size // c
    tr = _tile(r, 512, 8)

    def body(w_ref, g_ref, m_ref, v_ref, d_ref, nm_ref, nv_ref):
        gv = g_ref[...]
        m2 = ADAM_B1 * m_ref[...] + (1.0 - ADAM_B1) * gv
        v2 = ADAM_B2 * v_ref[...] + (1.0 - ADAM_B2) * (gv * gv)
        m_hat = m2 / (1.0 - ADAM_B1 ** ADAM_STEP)
        v_hat = v2 / (1.0 - ADAM_B2 ** ADAM_STEP)
        d_ref[...] = -ADAM_LR * (m_hat / (jnp.sqrt(v_hat) + ADAM_EPS) + ADAM_WD * w_ref[...])
        nm_ref[...] = m2
        nv_ref[...] = v2

    spec = pl.BlockSpec((tr, c), lambda i: (i, 0))
    outs = pl.pallas_call(
        body, name=name, out_shape=[jax.ShapeDtypeStruct((r, c), F32)] * 3, grid=(r // tr,),
        in_specs=[spec] * 4, out_specs=[spec] * 3, compiler_params=_params("arbitrary"),
    )(*(a.reshape(r, c) for a in (w, g, m, v)))
    return tuple(a.reshape(shape) for a in outs)


PACK_COLS = 1024


def _pack_rows(a):
    return a.reshape(-1, PACK_COLS)


def kernel(x, c, positions, mla_w_in, mla_g_q, mla_w_uq, mla_g_kv, mla_w_uk, mla_w_uv, mla_w_o, fox_w_in, fox_b_f, fox_w_o, ada_w, ada_b, ffn_w_gate, ffn_w_up, ffn_w_down, ln_g, ln_b, loss_target, m_mla_w_in, m_mla_g_q, m_mla_w_uq, m_mla_g_kv, m_mla_w_uk, m_mla_w_uv, m_mla_w_o, m_fox_w_in, m_fox_b_f, m_fox_w_o, m_ada_w, m_ada_b, m_ffn_w_gate, m_ffn_w_up, m_ffn_w_down, m_ln_g, m_ln_b, v_mla_w_in, v_mla_g_q, v_mla_w_uq, v_mla_g_kv, v_mla_w_uk, v_mla_w_uv, v_mla_w_o, v_fox_w_in, v_fox_b_f, v_fox_w_o, v_ada_w, v_ada_b, v_ffn_w_gate, v_ffn_w_up, v_ffn_w_down, v_ln_g, v_ln_b):
    bl, s, d = x.shape
    t = bl * s
    ff = ffn_w_gate.shape[-1] * N_DEV
    dev = 4 * lax.axis_index("x") + 2 * lax.axis_index("y") + lax.axis_index("c")
    ada_cols = ada_w.shape[-1]
    fox_in = fox_w_in.shape[-1] * N_DEV
    mla_in = mla_w_in.shape[-1]
    mla_in_pad = mla_in + (-mla_in) % LANES

    def t_last(a):
        return jnp.swapaxes(a, -1, -2)

    local = {
        "mla_w_in": mla_w_in[0],
        "mla_w_uq": t_last(mla_w_uq[0]),
        "mla_w_uk": t_last(mla_w_uk[0]),
        "mla_w_uv": t_last(mla_w_uv[0]),
        "mla_w_o": mla_w_o[0],
        "fox_w_in": t_last(fox_w_in[0]),
        "fox_w_o": fox_w_o[0],
    }
    for i in range(DEPTH):
        local.update({f"gate{i}": t_last(ffn_w_gate[i]), f"up{i}": t_last(ffn_w_up[i]), f"down{i}": ffn_w_down[i]})
    groups = [["mla_w_in", "mla_w_uq", "mla_w_uk", "mla_w_uv", "mla_w_o"],
              ["gate0", "up0", "down0"],
              ["fox_w_in", "fox_w_o"],
              ["gate1", "up1", "down1"]]
    offsets, rows_of, slot_of, group_of = {}, {}, {}, {}
    group_rows = []
    for gi, names in enumerate(groups):
        rows = 0
        for nm in names:
            rows_of[nm] = local[nm].size // PACK_COLS
            slot_of[nm] = rows_of[nm] + (-rows_of[nm]) % 16
            offsets[nm] = rows
            group_of[nm] = gi
            rows += slot_of[nm]
        group_rows.append(rows)

    def slot(nm, rows):
        pad = [(0, 0)] * rows.ndim
        pad[-2] = (0, slot_of[nm] - rows_of[nm])
        return jnp.pad(rows, pad)

    def held_until(block, arrays):
        zero = sum((a.reshape(-1)[0] * 0).astype(F32) for a in jax.tree.leaves(arrays))
        return block + zero.astype(block.dtype)

    def landing(block):
        land = lax.empty((N_DEV,) + block.shape, block.dtype)
        return lax.dynamic_update_slice(land, block[None], (dev, 0, 0))

    packed0 = jnp.concatenate([slot(nm, _pack_rows(local[nm]).astype(BF16)) for nm in groups[0]], axis=0)
    gathered0 = all_gather(packed0, "gather_mla_weights")
    gathered = {nm: gathered0[:, offsets[nm]:offsets[nm] + rows_of[nm], :] for nm in groups[0]}
    gather_started = [None] * len(groups)

    def depart(gi, after):
        blocks = [held_until(_pack_rows(local[nm]).astype(BF16), after) for nm in groups[gi]]
        gather_started[gi] = exchange_start(blocks, [landing(b) for b in blocks], f"gather_group{gi}_start", False)
        return gather_started[gi][4]

    def full(nm, cols):
        return gathered[nm].reshape(-1, cols)

    w_in = jnp.pad(full("mla_w_in", mla_in), ((0, 0), (0, mla_in_pad - mla_in)))
    wt_uq = full("mla_w_uq", MLA_QR).reshape(MLA_HEADS, MLA_NOPE + MLA_ROPE, MLA_QR)
    wt_uq_n = wt_uq[:, :MLA_NOPE].reshape(MLA_HEADS * MLA_NOPE, MLA_QR)
    wt_uq_r = wt_uq[:, MLA_NOPE:].reshape(MLA_HEADS * MLA_ROPE, MLA_QR)
    wt_uk = full("mla_w_uk", MLA_KVR)
    wt_uv = full("mla_w_uv", MLA_KVR)
    w_mo = full("mla_w_o", d)
    wt_gate, wt_up, w_down = [None] * DEPTH, [None] * DEPTH, [None] * DEPTH

    def arrive(gi, after):
        landed = list(exchange_wait(gather_started[gi], after, f"gather_group{gi}_wait", False))
        if gi + 1 < len(groups):
            depart(gi + 1, landed[0])
        gathered.update(zip(groups[gi], landed))
        for i in range(DEPTH):
            if group_of[f"gate{i}"] == gi:
                wt_gate[i], wt_up[i], w_down[i] = full(f"gate{i}", d), full(f"up{i}", d), full(f"down{i}", d)

    small = jnp.concatenate([c.reshape(-1, LANES), ln_g.reshape(-1, LANES), ln_b.reshape(-1, LANES)], axis=0)
    small_rows = small.shape[0]
    small = jnp.pad(small, ((0, (-small_rows) % 8), (0, 0)))
    small_all = all_gather(small, "gather_small")
    c_rows = bl * d // LANES
    c_all = small_all[:, :c_rows].reshape(N_DEV * bl, d)
    n_ln = DEPTH * 2
    ln_g_all = small_all[:, c_rows:c_rows + n_ln, :].transpose(1, 0, 2).reshape(DEPTH, 2, 1, d)
    ln_b_all = small_all[:, c_rows + n_ln:c_rows + 2 * n_ln, :].transpose(1, 0, 2).reshape(DEPTH, 2, 1, d)

    c_act = silu_rows(c_all, "silu_c")
    ada_b_loc = lax.dynamic_slice_in_dim(ada_b, dev * ada_cols, ada_cols, axis=1)
    mod_cols = [mm([(c_act, ada_w[i])], trans_b=False, out_dtype=F32, name=f"ada_fwd{i}", bias=ada_b_loc[i][None, :])
                for i in range(DEPTH)]
    mod_all = all_gather(jnp.concatenate(mod_cols, axis=0), "gather_mod")
    mod_all = mod_all.reshape(N_DEV, DEPTH, N_DEV * bl, ada_cols).transpose(1, 2, 0, 3).reshape(DEPTH, N_DEV * bl, 6 * d)
    mod_mine = lax.dynamic_slice_in_dim(mod_all, dev * bl, bl, axis=1)
    mods = [mod_mine[i].reshape(bl * 6, 1, d) for i in range(DEPTH)]
    mods[0] = mods[0] + depart(1, (mod_mine, gathered0))[0, 0]

    half_r = MLA_ROPE // 2
    inv_freq = ROPE_THETA ** (-jnp.arange(half_r, dtype=F32) / half_r)
    inv_freq = jnp.tile(inv_freq, LANES // half_r)[None, :]
    sign = jnp.tile(jnp.concatenate([-jnp.ones((half_r,), F32), jnp.ones((half_r,), F32)]), LANES // MLA_ROPE)[None, :]
    cos_t, sin_t = rope_tables(positions.astype(F32).reshape(t, 1), inv_freq, sign, "rope_tables")

    x2d = x.reshape(t, d)
    g_q, g_kv = mla_g_q.reshape(1, MLA_QR), mla_g_kv.reshape(1, MLA_KVR)
    b_f = jnp.pad(fox_b_f.reshape(1, FOX_HEADS), ((0, 0), (0, LANES - FOX_HEADS)))
    mla_scale = (MLA_NOPE + MLA_ROPE) ** -0.5
    fox_scale = FOX_HD ** -0.5
    tq = _attn_tiles(s)
    nk = s // tq

    saved = []
    u = modulate(x2d, mods[0], 0, 1, bl, "modulate0")
    xin = x2d
    for i in range(DEPTH):
        sv = {"u": u, "x_in": xin}
        if i % 2 == 0:
            h_in = mm([(u, w_in)], trans_b=False, out_dtype=F32, name=f"mla_in{i}")
            c_q, c_kv, k_r = mla_latents_forward(h_in, g_q, g_kv, cos_t, sin_t, f"mla_latents{i}")
            q_n = mm([(c_q, wt_uq_n)], trans_b=True, out_dtype=BF16, out_slab=True, name=f"mla_qn{i}")
            q_r_raw = mm([(c_q, wt_uq_r)], trans_b=True, out_dtype=F32, out_slab=True, name=f"mla_qr{i}")
            q_r = rope_slabs(q_r_raw, cos_t, sin_t, BF16, f"mla_qrope{i}")
            k_n = mm([(c_kv, wt_uk)], trans_b=True, out_dtype=BF16, out_slab=True, name=f"mla_kn{i}")
            v_m = mm([(c_kv, wt_uv)], trans_b=True, out_dtype=BF16, out_slab=True, name=f"mla_v{i}")
            ops = (q_n, q_r, k_n, k_r, v_m)
            o, lse, o_delta = attention_forward("mla", ops, bl, mla_scale, f"mla_attn{i}")
            y = mm([(o, w_mo)], trans_b=False, out_dtype=F32, name=f"mla_out{i}")
            sv.update(h_in=h_in, c_q=c_q, c_kv=c_kv, ops=ops, o=o, lse=lse, o_delta=o_delta)
        else:
            arrive(2, u)
            wt_fox = full("fox_w_in", d)
            wt_qkv = wt_fox[:3 * d]
            wt_f = jnp.pad(wt_fox[3 * d:], ((0, LANES - FOX_HEADS), (0, 0)))
            w_fo = full("fox_w_o", d)
            qkv = mm([(u, wt_qkv)], trans_b=True, out_dtype=BF16, out_slab=True, name=f"fox_qkv{i}")
            z = mm([(u, wt_f)], trans_b=True, out_dtype=F32, name=f"fox_z{i}")
            f_tok, f_q = fox_gate_forward(z, b_f, bl, f"fox_gate{i}")
            f_k = f_tok[:, :FOX_HEADS].reshape(bl, nk, tq, FOX_HEADS // 2, 2).transpose(0, 3, 1, 4, 2)
            f_k = jnp.pad(f_k.reshape(bl * FOX_HEADS // 2, nk, 2, tq), ((0, 0), (0, 0), (0, 6), (0, 0)))
            ops = (qkv, f_q, f_k)
            o, lse, o_delta = attention_forward("fox", ops, bl, fox_scale, f"fox_attn{i}")
            y = mm([(o, w_fo)], trans_b=False, out_dtype=F32, name=f"fox_out{i}")
            sv.update(z=z, ops=ops, o=o, lse=lse, o_delta=o_delta)
        x1, r1, u2 = residual_layer_norm(xin, y, mods[i], 2, ln_g_all[i, 0], ln_b_all[i, 0], bl, f"ln_mix{i}",
                                         next_mod=(3, 4))
        if wt_gate[i] is None:
            arrive(group_of[f"gate{i}"], u2)
        a, bb, h = swiglu_in(u2, wt_gate[i], wt_up[i], f"ffn_in{i}")
        y2 = mm([(h, w_down[i])], trans_b=False, out_dtype=F32, name=f"ffn_down{i}")
        sv.update(y=y, r1=r1, u2=u2, a=a, bb=bb, h=h, y2=y2)
        if i + 1 < DEPTH:
            xin, r2, u = residual_layer_norm(x1, y2, mods[i], 5, ln_g_all[i, 1], ln_b_all[i, 1], bl, f"ln_ffn{i}",
                                             next_mod=(0, 1, mods[i + 1]))
        else:
            xin, r2 = residual_layer_norm(x1, y2, mods[i], 5, ln_g_all[i, 1], ln_b_all[i, 1], bl, f"ln_ffn{i}")
        sv.update(r2=r2)
        saved.append(sv)

    loss_cols, d_x = loss_head(xin, loss_target.reshape(t, d), "loss_head")

    grads_full = {}
    wgrad = functools.partial(mm_tn, out_dtype=BF16)
    dmod = [[None] * 6 for _ in range(DEPTH)]
    dg_ln = [[None, None] for _ in range(DEPTH)]
    db_ln = [[None, None] for _ in range(DEPTH)]
    dg_q = dg_kv = db_f = None
    d_a, du = d_x, None
    scatter_started = [None] * len(groups)

    def scatter_start(gi, after=None):
        gs = [grads_full[nm].reshape(N_DEV, rows_of[nm], PACK_COLS).astype(BF16) for nm in groups[gi]]
        if gi == 0:
            gs = [jnp.concatenate([slot(nm, g) for nm, g in zip(groups[gi], gs)], axis=1)]
        if after is not None:
            gs = [held_until(g, after) for g in gs]
        lands = [landing(lax.dynamic_index_in_dim(g, dev, 0, keepdims=False)) for g in gs]
        scatter_started[gi] = exchange_start(gs, lands, f"scatter_group{gi}_start", True)

    ln_g_bwd = [[ln_g_all[i, k] for k in range(2)] for i in range(DEPTH)]
    for i in reversed(range(DEPTH)):
        sv = saved[i]
        if i + 1 < DEPTH:
            gi = group_of["fox_w_in"]
            scatter_start(gi)
            ln_g_bwd[i][1] = after_token(ln_g_bwd[i][1], scatter_started[gi])
        ln2 = (sv["r2"], sv["y2"], ln_g_bwd[i][1], ln_b_all[i, 1], (mods[i], 5))
        if du is None:
            bw = sublayer_backward(d_a, bl, f"bwd_ln_ffn{i}", ln=ln2)
        else:
            bw = sublayer_backward(d_a, bl, f"bwd_ln_ffn{i}", du=du, scale=(mods[i + 1], 1), ln=ln2)
            dmod[i + 1][0], dmod[i + 1][1] = bw["dshift"], bw["dscale"]
        dmod[i][5], dg_ln[i][1], db_ln[i][1] = bw["dgate"], bw["dg"], bw["db"]
        dy2 = bw["dy"]
        da, dbb = swiglu_out_backward(dy2, w_down[i], sv["a"], sv["bb"], f"bwd_ffn_act{i}")
        du2 = mm([(da, wt_gate[i]), (dbb, wt_up[i])], trans_b=False, out_dtype=F32, name=f"bwd_ffn_du{i}")
        grads_full[f"down{i}"] = wgrad(sv["h"], dy2, name=f"bwd_w_down{i}")
        grads_full[f"gate{i}"] = wgrad(da, sv["u2"], name=f"bwd_w_gate{i}")
        grads_full[f"up{i}"] = wgrad(dbb, sv["u2"], name=f"bwd_w_up{i}")
        gi = group_of[f"gate{i}"]
        scatter_start(gi)
        ln_g_bwd[i][0] = after_token(ln_g_bwd[i][0], scatter_started[gi])
        bw = sublayer_backward(bw["dx"], bl, f"bwd_ln_mix{i}", du=du2, scale=(mods[i], 4),
                               ln=(sv["r1"], sv["y"], ln_g_bwd[i][0], ln_b_all[i, 0], (mods[i], 2)))
        dmod[i][3], dmod[i][4], dmod[i][2] = bw["dshift"], bw["dscale"], bw["dgate"]
        dg_ln[i][0], db_ln[i][0] = bw["dg"], bw["db"]
        d_a, dy = bw["dx"], bw["dy"]
        o, lse, ops = sv["o"], sv["lse"], sv["ops"]
        if i % 2 == 0:
            do = mm([(dy, w_mo)], trans_b=True, out_dtype=BF16, out_slab=True, name=f"bwd_mla_do{i}")
            grads_full["mla_w_o"] = wgrad(o, dy, name=f"bwd_w_mla_o{i}")
            dqn, dqr, dkn, dvm, dkr = attention_backward("mla", ops, sv["o_delta"], do, lse, bl, mla_scale,
                                                         f"bwd_mla_attn{i}")
            dqr = rope_slabs(dqr, cos_t, sin_t, F32, f"bwd_mla_qrope{i}", transposed=True)
            dcq = mm([(dqn, wt_uq_n), (dqr, wt_uq_r)], trans_b=False, out_dtype=F32, name=f"bwd_mla_dcq{i}")
            dckv = mm([(dkn, wt_uk), (dvm, wt_uv)], trans_b=False, out_dtype=F32, name=f"bwd_mla_dckv{i}")
            d_uq_n = wgrad(dqn, sv["c_q"], name=f"bwd_w_uq_n{i}").reshape(MLA_HEADS, MLA_NOPE, MLA_QR)
            d_uq_r = wgrad(dqr, sv["c_q"], name=f"bwd_w_uq_r{i}").reshape(MLA_HEADS, MLA_ROPE, MLA_QR)
            grads_full["mla_w_uq"] = jnp.concatenate([d_uq_n, d_uq_r], axis=1)
            grads_full["mla_w_uk"] = wgrad(dkn, sv["c_kv"], name=f"bwd_w_uk{i}")
            grads_full["mla_w_uv"] = wgrad(dvm, sv["c_kv"], name=f"bwd_w_uv{i}")
            dh_in, dg_q, dg_kv = mla_latents_backward(sv["h_in"], dcq, dckv, dkr, g_q, g_kv, cos_t, sin_t,
                                                      f"bwd_mla_latents{i}")
            du = mm([(dh_in, w_in)], trans_b=True, out_dtype=F32, name=f"bwd_mla_du{i}")
            grads_full["mla_w_in"] = wgrad(sv["u"], dh_in, name=f"bwd_w_mla_in{i}")[:, :mla_in]
        else:
            do = mm([(dy, w_fo)], trans_b=True, out_dtype=BF16, out_slab=True, name=f"bwd_fox_do{i}")
            grads_full["fox_w_o"] = wgrad(o, dy, name=f"bwd_w_fox_o{i}")
            dq, dk, dvf, dfk = attention_backward("fox", ops, sv["o_delta"], do, lse, bl, fox_scale, f"bwd_fox_attn{i}")
            df = dfk[:, :, :2, :].reshape(bl, FOX_HEADS // 2, nk, 2, tq).transpose(0, 2, 4, 1, 3).reshape(t, FOX_HEADS)
            df = jnp.pad(df, ((0, 0), (0, LANES - FOX_HEADS)))
            dz, db_f = fox_gate_backward(sv["z"], b_f, df, bl, f"bwd_fox_gate{i}")
            du = mm([(dq, wt_fox[0:d]), (dk, wt_fox[d:2 * d]), (dvf, wt_fox[2 * d:3 * d]), (dz, wt_f)],
                    trans_b=False, out_dtype=F32, name=f"bwd_fox_du{i}")
            u_f = sv["u"]
            grads_full["fox_w_in"] = jnp.concatenate(
                [wgrad(dq, u_f, name=f"bwd_w_fox_q{i}"), wgrad(dk, u_f, name=f"bwd_w_fox_k{i}"),
                 wgrad(dvf, u_f, name=f"bwd_w_fox_v{i}"), wgrad(dz, u_f, name=f"bwd_w_fox_f{i}")[:FOX_HEADS]], axis=0)
    bw = sublayer_backward(d_a, bl, "bwd_input", du=du, scale=(mods[0], 1), x_in=x2d)
    dmod[0][0], dmod[0][1] = bw["dshift"], bw["dscale"]
    grad_x = bw["dx"].reshape(bl, s, d)

    dmod_rows = jnp.concatenate([r.reshape(bl, d) for layer in dmod for r in layer], axis=0)
    dmod_rows = dmod_rows.reshape(DEPTH, 6, bl, d).transpose(0, 2, 1, 3)
    n_mod = dmod_rows.size // LANES
    ln_parts = [dg_ln[i][k] for i in range(DEPTH) for k in range(2)] + [db_ln[i][k] for i in range(DEPTH) for k in range(2)]
    small_g = jnp.concatenate([dmod_rows.reshape(-1, LANES), dg_q.reshape(-1, LANES), dg_kv.reshape(-1, LANES), db_f]
                              + [p.reshape(-1, LANES) for p in ln_parts] + [loss_cols.reshape(-1, LANES)], axis=0)
    n_small = small_g.shape[0]
    small_g = jnp.pad(small_g, ((0, (-n_small) % 8), (0, 0)))
    small_g_all = all_gather(small_g, "gather_small_grads")
    scatter_start(0, after=small_g_all)
    small_sum = sum_leading(small_g_all, "sum_small_grads")
    per_seq = DEPTH * 6 * d // LANES
    dmod_all = small_g_all[:, :n_mod].reshape(N_DEV, DEPTH, bl, 6 * d).transpose(1, 0, 2, 3)
    dmod_all = dmod_all.reshape(DEPTH, N_DEV * bl, 6 * d)
    o1 = n_mod
    grad_g_q = small_sum[o1:o1 + MLA_QR // LANES].reshape(1, MLA_QR)
    o1 += MLA_QR // LANES
    grad_g_kv = small_sum[o1:o1 + MLA_KVR // LANES].reshape(1, MLA_KVR)
    o1 += MLA_KVR // LANES
    grad_b_f = small_sum[o1:o1 + 1, :FOX_HEADS]
    o1 += 1
    n_ln_rows = DEPTH * 2 * d // LANES
    grad_ln_g_full = small_sum[o1:o1 + n_ln_rows].reshape(DEPTH, 2, d)
    grad_ln_b_full = small_sum[o1 + n_ln_rows:o1 + 2 * n_ln_rows].reshape(DEPTH, 2, d)
    loss = jnp.sum(small_sum[o1 + 2 * n_ln_rows:o1 + 2 * n_ln_rows + d // LANES])
    shard = d // N_DEV
    grad_ln_g = lax.dynamic_slice_in_dim(grad_ln_g_full, dev * shard, shard, axis=2)
    grad_ln_b = lax.dynamic_slice_in_dim(grad_ln_b_full, dev * shard, shard, axis=2)
    by_seq = small_g_all[:, :n_mod].reshape(N_DEV, DEPTH, bl, 6 * d // LANES, LANES).transpose(0, 2, 1, 3, 4)
    grad_ada_b = sum_leading(by_seq.reshape(N_DEV * bl, per_seq, LANES), "sum_ada_b").reshape(DEPTH, 6 * d)
    dmod_cols = lax.dynamic_slice_in_dim(dmod_all, dev * ada_cols, ada_cols, axis=2)
    grad_ada_w = jnp.stack([mm_tn(c_act, dmod_cols[i], name=f"bwd_w_ada{i}") for i in range(DEPTH)])

    g_mine = {}

    def scatter_arrive(gi, after):
        landed = exchange_wait(scatter_started[gi], after, f"scatter_group{gi}_wait", True)
        if gi == 0:
            total = sum_leading(landed[0], f"scatter_group{gi}_sum")
            g_mine.update({nm: total[offsets[nm]:offsets[nm] + rows_of[nm]] for nm in groups[gi]})
            return total
        for nm, land in zip(groups[gi], landed):
            g_mine[nm] = sum_leading(land, f"scatter_sum_{nm}")
        return g_mine[groups[gi][-1]]

    after = scatter_started[0][4]
    for gi in reversed(range(1, len(groups))):
        after = scatter_arrive(gi, after)

    def mine(nm, shape):
        return g_mine[nm].reshape(shape)

    def shard_t(nm, a):
        return mine(nm, t_last(a).shape)

    transposed = {"mla_w_uq", "mla_w_uk", "mla_w_uv", "fox_w_in", "ffn_w_gate", "ffn_w_up"}
    grads = {
        "mla_w_in": lambda: mine("mla_w_in", mla_w_in[0].shape)[None],
        "mla_g_q": lambda: grad_g_q,
        "mla_w_uq": lambda: shard_t("mla_w_uq", mla_w_uq[0])[None],
        "mla_g_kv": lambda: grad_g_kv,
        "mla_w_uk": lambda: shard_t("mla_w_uk", mla_w_uk[0])[None],
        "mla_w_uv": lambda: shard_t("mla_w_uv", mla_w_uv[0])[None],
        "mla_w_o": lambda: mine("mla_w_o", mla_w_o[0].shape)[None],
        "fox_w_in": lambda: shard_t("fox_w_in", fox_w_in[0])[None],
        "fox_b_f": lambda: grad_b_f,
        "fox_w_o": lambda: mine("fox_w_o", fox_w_o[0].shape)[None],
        "ada_w": lambda: grad_ada_w,
        "ada_b": lambda: grad_ada_b,
        "ffn_w_gate": lambda: jnp.stack([shard_t(f"gate{i}", ffn_w_gate[i]) for i in range(DEPTH)]),
        "ffn_w_up": lambda: jnp.stack([shard_t(f"up{i}", ffn_w_up[i]) for i in range(DEPTH)]),
        "ffn_w_down": lambda: jnp.stack([mine(f"down{i}", ffn_w_down[i].shape) for i in range(DEPTH)]),
        "ln_g": lambda: grad_ln_g,
        "ln_b": lambda: grad_ln_b,
    }
    weights = dict(mla_w_in=mla_w_in, mla_g_q=mla_g_q, mla_w_uq=mla_w_uq, mla_g_kv=mla_g_kv, mla_w_uk=mla_w_uk,
                   mla_w_uv=mla_w_uv, mla_w_o=mla_w_o, fox_w_in=fox_w_in, fox_b_f=fox_b_f, fox_w_o=fox_w_o,
                   ada_w=ada_w, ada_b=ada_b, ffn_w_gate=ffn_w_gate, ffn_w_up=ffn_w_up, ffn_w_down=ffn_w_down,
                   ln_g=ln_g, ln_b=ln_b)
    first = dict(mla_w_in=m_mla_w_in, mla_g_q=m_mla_g_q, mla_w_uq=m_mla_w_uq, mla_g_kv=m_mla_g_kv, mla_w_uk=m_mla_w_uk,
                 mla_w_uv=m_mla_w_uv, mla_w_o=m_mla_w_o, fox_w_in=m_fox_w_in, fox_b_f=m_fox_b_f, fox_w_o=m_fox_w_o,
                 ada_w=m_ada_w, ada_b=m_ada_b, ffn_w_gate=m_ffn_w_gate, ffn_w_up=m_ffn_w_up, ffn_w_down=m_ffn_w_down,
                 ln_g=m_ln_g, ln_b=m_ln_b)
    second = dict(mla_w_in=v_mla_w_in, mla_g_q=v_mla_g_q, mla_w_uq=v_mla_w_uq, mla_g_kv=v_mla_g_kv, mla_w_uk=v_mla_w_uk,
                  mla_w_uv=v_mla_w_uv, mla_w_o=v_mla_w_o, fox_w_in=v_fox_w_in, fox_b_f=v_fox_b_f, fox_w_o=v_fox_w_o,
                  ada_w=v_ada_w, ada_b=v_ada_b, ffn_w_gate=v_ffn_w_gate, ffn_w_up=v_ffn_w_up, ffn_w_down=v_ffn_w_down,
                  ln_g=v_ln_g, ln_b=v_ln_b)
    order = list(weights)
    last = [nm for nm in order if group_of.get(nm) == 0]
    updated = {}
    for nm in [nm for nm in order if nm not in last] + last:
        if last and nm == last[0]:
            scatter_arrive(0, after)
        lay = t_last if nm in transposed else (lambda a: a)
        w = lay(weights[nm])
        g = grads[nm]().reshape(w.shape)
        delta, new_m, new_v = adamw(w, g, lay(first[nm]), lay(second[nm]), f"adamw_{nm}")
        updated[nm] = (lay(g), lay(delta), lay(new_m), lay(new_v))
        after = new_v
    return (loss, grad_x, *(updated[nm][k] for k in range(4) for nm in order))
```

```python
import functools
import math

import jax
import jax.numpy as jnp
from jax import lax
from jax.experimental import pallas as pl
from jax.experimental.pallas import tpu as pltpu

F32 = jnp.float32
BF16 = jnp.bfloat16
LANES = 128
N_DEV = 8
VMEM_LIMIT_BYTES = 56 * 1024 * 1024

DEPTH = 2
MLA_HEADS = 8
MLA_NOPE = 128
MLA_ROPE = 64
MLA_V = 128
MLA_QR = 256
MLA_KVR = 256
ROPE_THETA = 10000.0
FOX_HEADS = 16
FOX_HD = 64
ALPHA = (2.0 * DEPTH) ** 0.25
NORM_EPS = 1e-5
ADAM_LR = 0.001
ADAM_B1 = 0.9
ADAM_B2 = 0.999
ADAM_EPS = 1e-08
ADAM_WD = 0.01
ADAM_STEP = 10

MESH_AXES = ("x", "y", "c")
MESH = pl.DeviceIdType.MESH


def _params(*sem):
    return pltpu.CompilerParams(dimension_semantics=sem, vmem_limit_bytes=VMEM_LIMIT_BYTES)


def _tile(n, cap, mult=LANES):
    if n <= cap:
        return n
    best = None
    for t in range(mult, cap + 1, mult):
        if n % t == 0:
            best = t
    assert best is not None, (n, cap, mult)
    return best


def _dot(a, b, dims):
    return lax.dot_general(a, b, (dims, ((), ())), preferred_element_type=F32)


def _nn(a, b):
    return _dot(a, b, ((1,), (0,)))


def _nt(a, b):
    return _dot(a, b, ((1,), (1,)))


def _tn(a, b):
    return _dot(a, b, ((0,), (0,)))


def _me():
    return lax.axis_index("x"), lax.axis_index("y"), lax.axis_index("c")


def all_gather(x_loc, name):
    r, c = x_loc.shape

    def body(x_ref, out_ref, send_sems, recv_sems, local_sem):
        x, y, cc = _me()
        me, sibling = (x, y, cc), (x, y, 1 - cc)
        chips = [(1 - x, y), (x, 1 - y), (1 - x, 1 - y)]

        def rows(px, py, pc):
            return out_ref.at[4 * px + 2 * py + pc]

        def copy(k, block, to, src=None):
            return pltpu.make_async_remote_copy(
                src_ref=rows(*block) if src is None else src, dst_ref=rows(*block),
                send_sem=send_sems.at[k], recv_sem=recv_sems.at[k], device_id=to, device_id_type=MESH)

        mine = pltpu.make_async_copy(x_ref, rows(*me), local_sem)
        mine.start()
        first = [copy(0, me, sibling, src=x_ref)]
        first += [copy(1 + j, me, (*chip, cc), src=x_ref) for j, chip in enumerate(chips)]
        for cp in first:
            cp.start()
        passed = [copy(4 + j, (*chip, cc), sibling) for j, chip in enumerate(chips)]
        for j, chip in enumerate(chips):
            copy(1 + j, (*chip, cc), me).wait_recv()
            passed[j].start()
        copy(0, sibling, me).wait_recv()
        for j, chip in enumerate(chips):
            copy(4 + j, (*chip, 1 - cc), me).wait_recv()
        for cp in first + passed:
            cp.wait_send()
        mine.wait()

    return pl.pallas_call(
        body, name=name,
        out_shape=jax.ShapeDtypeStruct((N_DEV, r, c), x_loc.dtype),
        in_specs=[pl.BlockSpec(memory_space=pl.ANY)],
        out_specs=pl.BlockSpec(memory_space=pl.ANY),
        scratch_shapes=[pltpu.SemaphoreType.DMA((7,)), pltpu.SemaphoreType.DMA((7,)), pltpu.SemaphoreType.DMA(())],
    )(x_loc)


HBM_SPEC = pl.BlockSpec(memory_space=pltpu.HBM)
SEM_SPEC = pl.BlockSpec(memory_space=pltpu.SEMAPHORE)
N_PEERS = N_DEV - 1


def _peer(k):
    x, y, c = _me()
    return (1 - x if k & 4 else x, 1 - y if k & 2 else y, 1 - c if k & 1 else c)


def _exchange_copies(src_refs, land_refs, send_sems, recv_sems, scatter):
    x, y, c = _me()
    mine = 4 * x + 2 * y + c
    copies = []
    for n, (src_ref, land_ref) in enumerate(zip(src_refs, land_refs)):
        for k in range(1, N_DEV):
            px, py, pc = _peer(k)
            src = src_ref.at[4 * px + 2 * py + pc] if scatter else src_ref
            sem = n * N_PEERS + k - 1
            copies.append(pltpu.make_async_remote_copy(
                src_ref=src, dst_ref=land_ref.at[mine], send_sem=send_sems.at[sem], recv_sem=recv_sems.at[sem],
                device_id=(px, py, pc), device_id_type=MESH))
    return copies


def exchange_start(srcs, lands, name, scatter):
    n = len(srcs)

    def body(*refs):
        send_sems, recv_sems = refs[2 * n], refs[2 * n + 1]
        for cp in _exchange_copies(refs[:n], refs[n:2 * n], send_sems, recv_sems, scatter):
            cp.start()
        token = refs[-1]
        token[...] = jnp.zeros_like(token)

    outs = pl.pallas_call(
        body, name=name,
        out_shape=(pltpu.SemaphoreType.DMA((n * N_PEERS,)), pltpu.SemaphoreType.DMA((n * N_PEERS,)),
                   *(pltpu.HBM(a.shape, a.dtype) for a in (*srcs, *lands)), jax.ShapeDtypeStruct((8, LANES), F32)),
        in_specs=(HBM_SPEC,) * (2 * n),
        out_specs=(SEM_SPEC, SEM_SPEC, *((HBM_SPEC,) * (2 * n)), pl.BlockSpec(memory_space=pltpu.VMEM)),
        input_output_aliases={i: 2 + i for i in range(2 * n)},
        compiler_params=pltpu.CompilerParams(has_side_effects=pltpu.SideEffectType.DATAFLOW_SIDE_EFFECTING),
    )(*(pltpu.with_memory_space_constraint(a, pltpu.HBM) for a in (*srcs, *lands)))
    return outs[0], outs[1], outs[2:2 + n], outs[2 + n:2 + 2 * n], outs[-1]


def exchange_wait(started, after, name, scatter):
    send_sems, recv_sems, srcs, lands, _ = started
    n = len(srcs)

    def body(*refs):
        send_sems, recv_sems = refs[2 * n], refs[2 * n + 1]
        for cp in _exchange_copies(refs[:n], refs[n:2 * n], send_sems, recv_sems, scatter):
            cp.wait_send()
            cp.wait_recv()

    outs = pl.pallas_call(
        body, name=name,
        out_shape=tuple(pltpu.HBM(a.shape, a.dtype) for a in (*srcs, *lands)),
        in_specs=(*((HBM_SPEC,) * (2 * n)), SEM_SPEC, SEM_SPEC, pl.BlockSpec(memory_space=pl.ANY)),
        out_specs=(HBM_SPEC,) * (2 * n), input_output_aliases={i: i for i in range(2 * n)},
        compiler_params=pltpu.CompilerParams(has_side_effects=pltpu.SideEffectType.DATAFLOW_SIDE_EFFECTING),
    )(*srcs, *lands, send_sems, recv_sems, after)
    return outs[n:]


def after_token(small, started):
    return small + started[4][0, 0]


def sum_leading(x, name):
    n, r, c = x.shape
    tr = _tile(r, 512, 16)

    def body(x_ref, o_ref):
        acc = x_ref[0].astype(F32)
        for k in range(1, n):
            acc = acc + x_ref[k].astype(F32)
        o_ref[...] = acc

    return pl.pallas_call(
        body, name=name,
        out_shape=jax.ShapeDtypeStruct((r, c), F32),
        grid=(r // tr,),
        in_specs=[pl.BlockSpec((n, tr, c), lambda i: (0, i, 0))],
        out_specs=pl.BlockSpec((tr, c), lambda i: (i, 0)),
        compiler_params=_params("arbitrary"),
    )(x)


MM_VMEM_BUDGET = 36 * 1024 * 1024
GRID_STEP_AS_BYTES = 1 << 20


def _mm_tiles(m, n, a_row_bytes, b_col_bytes, out_bytes):
    tms = [c for c in (2048, 1024, 512, 256, 128, 64, 32, 16, 8) if m % c == 0] or [m]
    tns = [c for c in range(LANES, min(n, 2048) + 1, LANES) if n % c == 0] or [n]
    best = None
    for tm in tms:
        for tn in tns:
            vmem = 2 * (tm * a_row_bytes + tn * b_col_bytes) + 2 * tm * tn * out_bytes + tm * tn * 4
            if vmem > MM_VMEM_BUDGET:
                continue
            steps = (m // tm) * (n // tn)
            cost = steps * GRID_STEP_AS_BYTES + (m // tm) * n * b_col_bytes + m * a_row_bytes
            if best is None or cost < best[0]:
                best = (cost, tm, tn)
    assert best is not None, (m, n, a_row_bytes, b_col_bytes)
    return best[1], best[2]


def mm(pairs, *, trans_b, out_dtype, name, out_slab=False, bias=None):
    a0 = pairs[0][0]
    m = a0.shape[1] if a0.ndim == 3 else a0.shape[0]
    n = pairs[0][1].shape[0] if trans_b else pairs[0][1].shape[1]
    a_row_bytes = sum((b.shape[1] if trans_b else b.shape[0]) * a.dtype.itemsize for a, b in pairs)
    b_col_bytes = sum((b.shape[1] if trans_b else b.shape[0]) * b.dtype.itemsize for _, b in pairs)
    tm, tn = _mm_tiles(m, n, a_row_bytes, b_col_bytes, jnp.dtype(out_dtype).itemsize)
    slabs = [a.ndim == 3 for a, _ in pairs]
    n_pairs = len(pairs)

    def body(*refs):
        o_ref = refs[-1]
        acc = bias_ref = None
        if bias is not None:
            bias_ref = refs[2 * n_pairs]
        for i in range(n_pairs):
            a_ref, b_ref = refs[2 * i], refs[2 * i + 1]
            if slabs[i]:
                a = jnp.concatenate([a_ref[s].astype(BF16) for s in range(a_ref.shape[0])], axis=1)
            else:
                a = a_ref[...].astype(BF16)
            b = b_ref[...].astype(BF16)
            part = _nt(a, b) if trans_b else _nn(a, b)
            acc = part if acc is None else acc + part
        if bias_ref is not None:
            acc = acc + bias_ref[...]
        if out_slab:
            for s in range(tn // LANES):
                o_ref[s] = acc[:, s * LANES:(s + 1) * LANES].astype(out_dtype)
        else:
            o_ref[...] = acc.astype(out_dtype)

    in_specs, args = [], []
    for (a, b), slab in zip(pairs, slabs):
        if slab:
            in_specs.append(pl.BlockSpec((a.shape[0], tm, LANES), lambda i, j: (0, i, 0)))
        else:
            in_specs.append(pl.BlockSpec((tm, a.shape[1]), lambda i, j: (i, 0)))
        if trans_b:
            in_specs.append(pl.BlockSpec((tn, b.shape[1]), lambda i, j: (j, 0)))
        else:
            in_specs.append(pl.BlockSpec((b.shape[0], tn), lambda i, j: (0, j)))
        args += [a, b]
    if bias is not None:
        in_specs.append(pl.BlockSpec((1, tn), lambda i, j: (0, j)))
        args.append(bias)
    if out_slab:
        out_shape = jax.ShapeDtypeStruct((n // LANES, m, LANES), out_dtype)
        out_spec = pl.BlockSpec((tn // LANES, tm, LANES), lambda i, j: (j, i, 0))
    else:
        out_shape = jax.ShapeDtypeStruct((m, n), out_dtype)
        out_spec = pl.BlockSpec((tm, tn), lambda i, j: (i, j))
    return pl.pallas_call(
        body, name=name, out_shape=out_shape, grid=(m // tm, n // tn),
        in_specs=in_specs, out_specs=out_spec,
        compiler_params=_params("arbitrary", "arbitrary"),
    )(*args)


def mm_tn(a, b, *, name, out_dtype=F32, tk_cap=1536, tn_cap=1024, tm_cap=512):
    slab = a.ndim == 3
    m = a.shape[1] if slab else a.shape[0]
    k = a.shape[0] * LANES if slab else a.shape[1]
    n = b.shape[1]
    tk = _tile(k, tk_cap)
    tn = _tile(n, tn_cap)
    tm = _tile(m, tm_cap, 8)
    n_steps = m // tm

    def body(a_ref, b_ref, o_ref, acc_ref):
        step = pl.program_id(2)

        @pl.when(step == 0)
        def _():
            acc_ref[...] = jnp.zeros_like(acc_ref)

        bb = b_ref[...].astype(BF16)
        if slab:
            for s in range(tk // LANES):
                acc_ref[s * LANES:(s + 1) * LANES, :] += _tn(a_ref[s].astype(BF16), bb)
        else:
            acc_ref[...] += _tn(a_ref[...].astype(BF16), bb)

        @pl.when(step == n_steps - 1)
        def _():
            o_ref[...] = acc_ref[...].astype(out_dtype)

    if slab:
        a_spec = pl.BlockSpec((tk // LANES, tm, LANES), lambda i, j, t: (i, t, 0))
    else:
        a_spec = pl.BlockSpec((tm, tk), lambda i, j, t: (t, i))
    return pl.pallas_call(
        body, name=name, out_shape=jax.ShapeDtypeStruct((k, n), out_dtype), grid=(k // tk, n // tn, n_steps),
        in_specs=[a_spec, pl.BlockSpec((tm, tn), lambda i, j, t: (t, j))],
        out_specs=pl.BlockSpec((tk, tn), lambda i, j, t: (i, j)),
        scratch_shapes=[pltpu.VMEM((tk, tn), F32)],
        compiler_params=_params("arbitrary", "arbitrary", "arbitrary"),
    )(a, b)


def _row_spec(d, k):
    return pl.BlockSpec((1, 1, d), lambda b, i: (6 * b + k, 0, 0))


def modulate(x, mod, k_shift, k_scale, bl, name):
    t, d = x.shape
    s = t // bl
    tm = _tile(s, 512, 8)
    nt = s // tm

    def body(x_ref, sh_ref, sc_ref, o_ref):
        o_ref[...] = (x_ref[...] * (1.0 + sc_ref[0]) + sh_ref[0]).astype(BF16)

    return pl.pallas_call(
        body, name=name, out_shape=jax.ShapeDtypeStruct((t, d), BF16), grid=(bl, nt),
        in_specs=[pl.BlockSpec((tm, d), lambda b, i: (b * nt + i, 0)), _row_spec(d, k_shift), _row_spec(d, k_scale)],
        out_specs=pl.BlockSpec((tm, d), lambda b, i: (b * nt + i, 0)),
        compiler_params=_params("arbitrary", "arbitrary"),
    )(x, mod, mod)


def _layer_norm_stats(r):
    mu = jnp.mean(r, axis=-1, keepdims=True)
    rc = r - mu
    var = jnp.mean(rc * rc, axis=-1, keepdims=True)
    rstd = lax.rsqrt(var + NORM_EPS)
    return rc * rstd, rstd


def residual_layer_norm(x, y, mod, k_gate, g, b, bl, name, next_mod=None):
    t, d = x.shape
    s = t // bl
    tm = _tile(s, 512, 8)
    nt = s // tm
    has_next = next_mod is not None

    def body(*refs):
        x_ref, y_ref, gt_ref, g_ref, b_ref = refs[:5]
        rest = refs[5:]
        if has_next:
            sh_ref, sc_ref, o_ref, r_ref, u_ref = rest
        else:
            o_ref, r_ref = rest
        r = ALPHA * x_ref[...] + (1.0 + gt_ref[0]) * y_ref[...]
        xhat, _ = _layer_norm_stats(r)
        out = xhat * g_ref[...] + b_ref[...]
        o_ref[...] = out
        r_ref[...] = r
        if has_next:
            u_ref[...] = (out * (1.0 + sc_ref[0]) + sh_ref[0]).astype(BF16)

    tok = pl.BlockSpec((tm, d), lambda bb, i: (bb * nt + i, 0))
    vec = pl.BlockSpec((1, d), lambda bb, i: (0, 0))
    in_specs = [tok, tok, _row_spec(d, k_gate), vec, vec]
    args = [x, y, mod, g, b]
    out_shape = [jax.ShapeDtypeStruct((t, d), F32), jax.ShapeDtypeStruct((t, d), F32)]
    out_specs = [tok, tok]
    if has_next:
        in_specs += [_row_spec(d, next_mod[0]), _row_spec(d, next_mod[1])]
        args += [mod if len(next_mod) == 2 else next_mod[2]] * 2
        out_shape.append(jax.ShapeDtypeStruct((t, d), BF16))
        out_specs.append(tok)
    return pl.pallas_call(
        body, name=name, out_shape=out_shape, grid=(bl, nt), in_specs=in_specs, out_specs=out_specs,
        compiler_params=_params("arbitrary", "arbitrary"),
    )(*args)


def loss_head(xo, target, name):
    t, d = xo.shape
    tm = _tile(t, 512, 8)

    def body(x_ref, t_ref, l_ref, dx_ref):
        @pl.when(pl.program_id(0) == 0)
        def _():
            l_ref[...] = jnp.zeros_like(l_ref)

        e = x_ref[...] - t_ref[...]
        l_ref[...] += jnp.sum(e * e, axis=0, keepdims=True) * (0.5 / d)
        dx_ref[...] = e * (1.0 / d)

    tok = pl.BlockSpec((tm, d), lambda i: (i, 0))
    return pl.pallas_call(
        body, name=name,
        out_shape=[jax.ShapeDtypeStruct((1, d), F32), jax.ShapeDtypeStruct((t, d), F32)],
        grid=(t // tm,), in_specs=[tok, tok],
        out_specs=[pl.BlockSpec((1, d), lambda i: (0, 0)), tok],
        compiler_params=_params("arbitrary"),
    )(xo, target)


def sublayer_backward(d_a, bl, name, *, du=None, scale=None, x_in=None, ln=None):
    t, d = d_a.shape
    s = t // bl
    tm = _tile(s, 512, 8)
    nt = s // tm
    has_mod = du is not None
    has_ln = ln is not None
    assert has_mod or has_ln
    assert has_ln or x_in is not None

    def body(*refs):
        refs = list(refs)
        da_ref = refs.pop(0)
        if has_mod:
            du_ref, sc_ref = refs.pop(0), refs.pop(0)
        if has_ln:
            r_ref, y_ref, g_ref, b_ref, gt_ref = (refs.pop(0) for _ in range(5))
        elif has_mod:
            xin_ref = refs.pop(0)
        dx_ref = refs.pop(0)
        if has_ln:
            dy_ref, dg_ref, db_ref, dgt_ref = (refs.pop(0) for _ in range(4))
        if has_mod:
            dsc_ref, dsh_ref = refs.pop(0), refs.pop(0)
        first_tile = pl.program_id(1) == 0
        first_step = jnp.logical_and(pl.program_id(0) == 0, first_tile)

        dout = da_ref[...]
        if has_ln:
            xhat, rstd = _layer_norm_stats(r_ref[...])
        if has_mod:
            duv = du_ref[...]
            dout = dout + duv * (1.0 + sc_ref[0])
            xin = xhat * g_ref[...] + b_ref[...] if has_ln else xin_ref[...]

            @pl.when(first_tile)
            def _():
                dsc_ref[...] = jnp.zeros_like(dsc_ref)
                dsh_ref[...] = jnp.zeros_like(dsh_ref)

            dsc_ref[0] += jnp.sum(duv * xin, axis=0, keepdims=True)
            dsh_ref[0] += jnp.sum(duv, axis=0, keepdims=True)
        if not has_ln:
            dx_ref[...] = dout
            return

        @pl.when(first_step)
        def _():
            dg_ref[...] = jnp.zeros_like(dg_ref)
            db_ref[...] = jnp.zeros_like(db_ref)

        @pl.when(first_tile)
        def _():
            dgt_ref[...] = jnp.zeros_like(dgt_ref)

        dg_ref[...] += jnp.sum(dout * xhat, axis=0, keepdims=True)
        db_ref[...] += jnp.sum(dout, axis=0, keepdims=True)
        dxh = dout * g_ref[...]
        dr = rstd * (dxh - jnp.mean(dxh, axis=-1, keepdims=True) - xhat * jnp.mean(dxh * xhat, axis=-1, keepdims=True))
        dx_ref[...] = ALPHA * dr
        dy_ref[...] = ((1.0 + gt_ref[0]) * dr).astype(BF16)
        dgt_ref[0] += jnp.sum(dr * y_ref[...], axis=0, keepdims=True)

    tok = pl.BlockSpec((tm, d), lambda bb, i: (bb * nt + i, 0))
    vec = pl.BlockSpec((1, d), lambda bb, i: (0, 0))
    seq = pl.BlockSpec((1, 1, d), lambda bb, i: (bb, 0, 0))
    in_specs, args = [tok], [d_a]
    if has_mod:
        in_specs += [tok, _row_spec(d, scale[1])]
        args += [du, scale[0]]
    if has_ln:
        r, y, g, b, gate = ln
        in_specs += [tok, tok, vec, vec, _row_spec(d, gate[1])]
        args += [r, y, g, b, gate[0]]
    elif has_mod:
        in_specs.append(tok)
        args.append(x_in)
    names = ["dx"]
    out_shape, out_specs = [jax.ShapeDtypeStruct((t, d), F32)], [tok]
    if has_ln:
        names += ["dy", "dg", "db", "dgate"]
        out_shape += [jax.ShapeDtypeStruct((t, d), BF16), jax.ShapeDtypeStruct((1, d), F32),
                      jax.ShapeDtypeStruct((1, d), F32), jax.ShapeDtypeStruct((bl, 1, d), F32)]
        out_specs += [tok, vec, vec, seq]
    if has_mod:
        names += ["dscale", "dshift"]
        out_shape += [jax.ShapeDtypeStruct((bl, 1, d), F32)] * 2
        out_specs += [seq, seq]
    outs = pl.pallas_call(
        body, name=name, out_shape=out_shape, grid=(bl, nt), in_specs=in_specs, out_specs=out_specs,
        compiler_params=_params("arbitrary", "arbitrary"),
    )(*args)
    return dict(zip(names, outs))


def _silu(a):
    return a * jax.nn.sigmoid(a)


def silu_rows(a, name):
    def body(a_ref, o_ref):
        o_ref[...] = _silu(a_ref[...]).astype(BF16)

    return pl.pallas_call(body, name=name, out_shape=jax.ShapeDtypeStruct(a.shape, BF16))(a)


def _swiglu_tiles(t, f):
    return _tile(t, 512, 8), _tile(f, 1536)


def swiglu_in(u, wt_gate, wt_up, name):
    t, d = u.shape
    f = wt_gate.shape[0]
    tm, tf = _swiglu_tiles(t, f)

    def body(u_ref, g_ref, w_ref, a_ref, b_ref, h_ref):
        uv = u_ref[...]
        a = _nt(uv, g_ref[...])
        b = _nt(uv, w_ref[...])
        a_ref[...] = a.astype(BF16)
        b_ref[...] = b.astype(BF16)
        h_ref[...] = (_silu(a) * b).astype(BF16)

    w_spec = pl.BlockSpec((tf, d), lambda i, j: (j, 0))
    o_spec = pl.BlockSpec((tm, tf), lambda i, j: (i, j))
    return pl.pallas_call(
        body, name=name,
        out_shape=[jax.ShapeDtypeStruct((t, f), BF16)] * 3,
        grid=(t // tm, f // tf), in_specs=[pl.BlockSpec((tm, d), lambda i, j: (i, 0)), w_spec, w_spec],
        out_specs=[o_spec, o_spec, o_spec], compiler_params=_params("arbitrary", "arbitrary"),
    )(u, wt_gate, wt_up)


def swiglu_out_backward(dy, w_down, a, b, name):
    t, d = dy.shape
    f = w_down.shape[0]
    tm, tf = _swiglu_tiles(t, f)

    def body(dy_ref, w_ref, a_ref, b_ref, da_ref, db_ref):
        dh = _nt(dy_ref[...], w_ref[...])
        av = a_ref[...].astype(F32)
        sig = jax.nn.sigmoid(av)
        da_ref[...] = (dh * b_ref[...].astype(F32) * (sig * (1.0 + av * (1.0 - sig)))).astype(BF16)
        db_ref[...] = (dh * (av * sig)).astype(BF16)

    spec = pl.BlockSpec((tm, tf), lambda i, j: (i, j))
    return pl.pallas_call(
        body, name=name, out_shape=[jax.ShapeDtypeStruct((t, f), BF16)] * 2, grid=(t // tm, f // tf),
        in_specs=[pl.BlockSpec((tm, d), lambda i, j: (i, 0)), pl.BlockSpec((tf, d), lambda i, j: (j, 0)), spec, spec],
        out_specs=[spec, spec], compiler_params=_params("arbitrary", "arbitrary"),
    )(dy, w_down, a, b)


def rope_tables(pos, inv_freq, sign, name):
    t = pos.shape[0]
    tm = _tile(t, 512, 8)

    def body(p_ref, f_ref, s_ref, c_out, s_out):
        ang = p_ref[...] * f_ref[...]
        c_out[...] = jnp.cos(ang)
        s_out[...] = jnp.sin(ang) * s_ref[...]

    vec = pl.BlockSpec((1, LANES), lambda i: (0, 0))
    tab = pl.BlockSpec((tm, LANES), lambda i: (i, 0))
    return pl.pallas_call(
        body, name=name, out_shape=[jax.ShapeDtypeStruct((t, LANES), F32)] * 2, grid=(t // tm,),
        in_specs=[pl.BlockSpec((tm, 1), lambda i: (i, 0)), vec, vec], out_specs=[tab, tab],
        compiler_params=_params("arbitrary"),
    )(pos, inv_freq, sign)


def _rot_half(v):
    lane = lax.broadcasted_iota(jnp.int32, v.shape, v.ndim - 1)
    up = pltpu.roll(v, LANES - MLA_ROPE // 2, v.ndim - 1)
    down = pltpu.roll(v, MLA_ROPE // 2, v.ndim - 1)
    return jnp.where(lane % MLA_ROPE < MLA_ROPE // 2, up, down)


def _rope(v, cos, sin_signed):
    return v * cos + _rot_half(v) * sin_signed


def _rope_transposed(dv, cos, sin_signed):
    return dv * cos + _rot_half(dv * sin_signed)


def rope_slabs(v, cos, sin_signed, out_dtype, name, transposed=False):
    ns, t, _ = v.shape
    tm = _tile(t, 512, 8)
    fn = _rope_transposed if transposed else _rope

    def body(v_ref, c_ref, s_ref, o_ref):
        o_ref[0] = fn(v_ref[0].astype(F32), c_ref[...], s_ref[...]).astype(out_dtype)

    tab = pl.BlockSpec((tm, LANES), lambda j, i: (i, 0))
    spec = pl.BlockSpec((1, tm, LANES), lambda j, i: (j, i, 0))
    return pl.pallas_call(
        body, name=name, out_shape=jax.ShapeDtypeStruct(v.shape, out_dtype), grid=(ns, t // tm),
        in_specs=[spec, tab, tab], out_specs=spec, compiler_params=_params("arbitrary", "arbitrary"),
    )(v, cos, sin_signed)


def _rms(x):
    rinv = lax.rsqrt(jnp.mean(x * x, axis=-1, keepdims=True) + NORM_EPS)
    return x * rinv, rinv


def mla_latents_forward(h_in, g_q, g_kv, cos, sin_signed, name):
    t = h_in.shape[0]
    tm = _tile(t, 512, 8)

    def body(h_ref, gq_ref, gkv_ref, c_ref, s_ref, cq_ref, ckv_ref, kr_ref):
        cq_ref[...] = (_rms(h_ref[:, 0:MLA_QR])[0] * gq_ref[...]).astype(BF16)
        ckv_ref[...] = (_rms(h_ref[:, MLA_QR:MLA_QR + MLA_KVR])[0] * gkv_ref[...]).astype(BF16)
        kr_ref[...] = _rope(h_ref[:, MLA_QR + MLA_KVR:], c_ref[...], s_ref[...]).astype(BF16)

    def tok(w):
        return pl.BlockSpec((tm, w), lambda i: (i, 0))

    def vec(w):
        return pl.BlockSpec((1, w), lambda i: (0, 0))

    return pl.pallas_call(
        body, name=name,
        out_shape=[jax.ShapeDtypeStruct((t, MLA_QR), BF16), jax.ShapeDtypeStruct((t, MLA_KVR), BF16),
                   jax.ShapeDtypeStruct((t, LANES), BF16)],
        grid=(t // tm,),
        in_specs=[tok(h_in.shape[1]), vec(MLA_QR), vec(MLA_KVR), tok(LANES), tok(LANES)],
        out_specs=[tok(MLA_QR), tok(MLA_KVR), tok(LANES)],
        compiler_params=_params("arbitrary"),
    )(h_in, g_q, g_kv, cos, sin_signed)


def mla_latents_backward(h_in, dcq, dckv, dkr, g_q, g_kv, cos, sin_signed, name):
    t, w = h_in.shape
    tm = _tile(t, 512, 8)

    def body(h_ref, dcq_ref, dckv_ref, dkr_ref, gq_ref, gkv_ref, c_ref, s_ref, dh_ref, dgq_ref, dgkv_ref):
        @pl.when(pl.program_id(0) == 0)
        def _():
            dgq_ref[...] = jnp.zeros_like(dgq_ref)
            dgkv_ref[...] = jnp.zeros_like(dgkv_ref)

        def rms_bwd(x, dc, g_ref, dg_ref):
            xn, rinv = _rms(x)
            dg_ref[...] += jnp.sum(dc * xn, axis=0, keepdims=True)
            dxn = dc * g_ref[...]
            return rinv * (dxn - xn * jnp.mean(dxn * xn, axis=-1, keepdims=True))

        dq = rms_bwd(h_ref[:, 0:MLA_QR], dcq_ref[...], gq_ref, dgq_ref)
        dkv = rms_bwd(h_ref[:, MLA_QR:MLA_QR + MLA_KVR], dckv_ref[...], gkv_ref, dgkv_ref)
        dr = _rope_transposed(dkr_ref[...], c_ref[...], s_ref[...])
        dh_ref[...] = jnp.concatenate([dq, dkv, dr], axis=1).astype(BF16)

    def tok(ww):
        return pl.BlockSpec((tm, ww), lambda i: (i, 0))

    def vec(ww):
        return pl.BlockSpec((1, ww), lambda i: (0, 0))

    return pl.pallas_call(
        body, name=name,
        out_shape=[jax.ShapeDtypeStruct((t, w), BF16), jax.ShapeDtypeStruct((1, MLA_QR), F32),
                   jax.ShapeDtypeStruct((1, MLA_KVR), F32)],
        grid=(t // tm,),
        in_specs=[tok(w), tok(MLA_QR), tok(MLA_KVR), tok(LANES), vec(MLA_QR), vec(MLA_KVR), tok(LANES), tok(LANES)],
        out_specs=[tok(w), vec(MLA_QR), vec(MLA_KVR)],
        compiler_params=_params("arbitrary"),
    )(h_in, dcq, dckv, dkr, g_q, g_kv, cos, sin_signed)


def _tri(n, lower):
    r = lax.broadcasted_iota(jnp.int32, (n, n), 0)
    c = lax.broadcasted_iota(jnp.int32, (n, n), 1)
    return jnp.where(r >= c if lower else r <= c, 1.0, 0.0).astype(F32)


def _dot_exact(tri, v):
    hi = v.astype(BF16)
    mid = (v - hi.astype(F32)).astype(BF16)
    lo = (v - hi.astype(F32) - mid.astype(F32)).astype(BF16)
    t = tri.astype(BF16)
    return _nn(t, hi) + _nn(t, mid) + _nn(t, lo)


def fox_gate_forward(z, b_f, bl, name):
    t = z.shape[0]
    s = t // bl
    ch = LANES
    n_ch = s // ch

    def body(z_ref, b_ref, f_ref, fs_ref):
        tri = _tri(ch, True)
        carry = jnp.zeros((1, LANES), F32)
        for k in range(n_ch):
            x = z_ref[k * ch:(k + 1) * ch, :] + b_ref[...]
            logf = jnp.minimum(x, 0.0) - jnp.log(1.0 + jnp.exp(-jnp.abs(x)))
            cs = _dot_exact(tri, logf) + carry
            carry = cs[ch - 1:ch, :]
            f_ref[k * ch:(k + 1) * ch, :] = cs
            for h in range(FOX_HEADS):
                fs_ref[h, k * ch:(k + 1) * ch, :] = jnp.broadcast_to(cs[:, h:h + 1], (ch, LANES))

    return pl.pallas_call(
        body, name=name,
        out_shape=[jax.ShapeDtypeStruct((t, LANES), F32), jax.ShapeDtypeStruct((FOX_HEADS, t, LANES), F32)],
        grid=(bl,),
        in_specs=[pl.BlockSpec((s, LANES), lambda b: (b, 0)), pl.BlockSpec((1, LANES), lambda b: (0, 0))],
        out_specs=[pl.BlockSpec((s, LANES), lambda b: (b, 0)),
                   pl.BlockSpec((FOX_HEADS, s, LANES), lambda b: (0, b, 0))],
        compiler_params=_params("arbitrary"),
    )(z, b_f)


def fox_gate_backward(z, b_f, df, bl, name):
    t = z.shape[0]
    s = t // bl
    ch = LANES
    n_ch = s // ch

    def body(z_ref, b_ref, df_ref, dz_ref, db_ref):
        @pl.when(pl.program_id(0) == 0)
        def _():
            db_ref[...] = jnp.zeros_like(db_ref)

        tri = _tri(ch, False)
        carry = jnp.zeros((1, LANES), F32)
        for k in reversed(range(n_ch)):
            cs = _dot_exact(tri, df_ref[k * ch:(k + 1) * ch, :]) + carry
            carry = cs[0:1, :]
            x = z_ref[k * ch:(k + 1) * ch, :] + b_ref[...]
            dz = cs * (1.0 - jax.nn.sigmoid(x))
            dz_ref[k * ch:(k + 1) * ch, :] = dz
            db_ref[...] += jnp.sum(dz, axis=0, keepdims=True)

    tok = pl.BlockSpec((s, LANES), lambda b: (b, 0))
    vec = pl.BlockSpec((1, LANES), lambda b: (0, 0))
    return pl.pallas_call(
        body, name=name,
        out_shape=[jax.ShapeDtypeStruct((t, LANES), F32), jax.ShapeDtypeStruct((1, LANES), F32)],
        grid=(bl,), in_specs=[tok, vec, tok], out_specs=[tok, vec],
        compiler_params=_params("arbitrary"),
    )(z, b_f, df)


NEG_INF = float("-inf")


def _attn_tiles(s):
    return _tile(s, 512, 8)


def attention_forward(kind, ops, bl, scale, name):
    fox = kind == "fox"
    if fox:
        assert math.frexp(scale)[0] == 0.5, "the FoX scale is folded into bf16 queries: it must be a power of two"
        qkv, fq, fk = ops
        t = qkv.shape[1]
        n_pair = FOX_HEADS // 2
    else:
        qn, qr, kn, kr, v = ops
        t = qn.shape[1]
        n_pair = MLA_HEADS // 2
    s = t // bl
    tq = _attn_tiles(s)
    nq = s // tq
    half = LANES // 2

    def body(*refs):
        if fox:
            q_ref, k_ref, v_ref, fq_ref, fk_ref, o_ref, lse_ref, o32_ref = refs
        else:
            qn_ref, qr_ref, kn_ref, kr_ref, v_ref, o_ref, lse_ref = refs
        i = pl.program_id(2)
        row = lax.broadcasted_iota(jnp.int32, (tq, tq), 0)
        col = lax.broadcasted_iota(jnp.int32, (tq, tq), 1)
        heads = []
        for e in range(2):
            sl = slice(e * half, (e + 1) * half)
            if fox:
                heads.append((sl, q_ref[0, :, sl] * jnp.asarray(scale, BF16), None))
            else:
                heads.append((sl, qn_ref[e], qr_ref[0, :, sl]))
        dv = half if fox else LANES

        def wide(stat):
            return jnp.concatenate([stat] * (tq // LANES), axis=1)

        def step(j, carry, masked):
            rows = pl.ds(pl.multiple_of(j * tq, tq), tq)
            new = []
            for e, (sl, qa, qb) in enumerate(heads):
                m, l, acc = carry[e]
                if fox:
                    sc = _nt(qa, k_ref[0, rows, sl]) + wide(fq_ref[e]) - fk_ref[0, j, e:e + 1, :]
                    vv = v_ref[0, rows, sl]
                else:
                    sc = (_nt(qa, kn_ref[e, rows, :]) + _nt(qb, kr_ref[rows, 0:half])) * scale
                    vv = v_ref[e, rows, :]
                if masked:
                    sc = jnp.where(row >= col, sc, NEG_INF)
                m_new = jnp.maximum(m, jnp.max(sc, axis=1, keepdims=True))
                p = jnp.exp(sc - m_new)
                a = jnp.exp(m - m_new)
                l = a * l + jnp.sum(p, axis=1, keepdims=True)
                p_hi = p.astype(BF16)
                acc = a * acc + _nn(p_hi, vv)
                if fox:
                    acc = acc + _nn((p - p_hi.astype(F32)).astype(BF16), vv)
                new.append((m_new, l, acc))
            return tuple(new)

        init = (jnp.full((tq, 1), NEG_INF, F32), jnp.zeros((tq, 1), F32), jnp.zeros((tq, dv), F32))
        carry = step(i, (init, init), True)
        carry = lax.fori_loop(0, i, lambda j, c: step(j, c, False), carry)
        outs = [acc / l for _, l, acc in carry]
        for e, (m, l, _) in enumerate(carry):
            lse_ref[e] = jnp.broadcast_to(m + jnp.log(l), (tq, LANES))
        if fox:
            o32 = jnp.concatenate(outs, axis=1)
            o32_ref[0] = o32
            o_ref[0] = o32.astype(BF16)
        else:
            o_ref[0] = outs[0].astype(BF16)
            o_ref[1] = outs[1].astype(BF16)

    def q_idx(b, g, i):
        return (g, b * nq + i, 0)

    if fox:
        nk = fk.shape[1]
        in_specs = [pl.BlockSpec((1, tq, LANES), q_idx),
                    pl.BlockSpec((1, s, LANES), lambda b, g, i: (n_pair + g, b, 0)),
                    pl.BlockSpec((1, s, LANES), lambda b, g, i: (2 * n_pair + g, b, 0)),
                    pl.BlockSpec((2, tq, LANES), q_idx),
                    pl.BlockSpec((1, nk, 8, tq), lambda b, g, i: (b * n_pair + g, 0, 0, 0))]
        args = [qkv, qkv, qkv, fq, fk]
        o_spec = pl.BlockSpec((1, tq, LANES), q_idx)
    else:
        in_specs = [pl.BlockSpec((2, tq, LANES), q_idx),
                    pl.BlockSpec((1, tq, LANES), q_idx),
                    pl.BlockSpec((2, s, LANES), lambda b, g, i: (g, b, 0)),
                    pl.BlockSpec((s, LANES), lambda b, g, i: (b, 0)),
                    pl.BlockSpec((2, s, LANES), lambda b, g, i: (g, b, 0))]
        args = [qn, qr, kn, kr, v]
        o_spec = pl.BlockSpec((2, tq, LANES), q_idx)
    out_shape = [jax.ShapeDtypeStruct((8, t, LANES), BF16), jax.ShapeDtypeStruct((2 * n_pair, t, LANES), F32)]
    out_specs = [o_spec, pl.BlockSpec((2, tq, LANES), q_idx)]
    if fox:
        out_shape.append(jax.ShapeDtypeStruct((8, t, LANES), F32))
        out_specs.append(o_spec)
    outs = pl.pallas_call(
        body, name=name, out_shape=out_shape, grid=(bl, n_pair, nq), in_specs=in_specs, out_specs=out_specs,
        compiler_params=_params("arbitrary", "arbitrary", "arbitrary"),
    )(*args)
    return (outs[0], outs[1], outs[2] if fox else outs[0])


def attention_backward(kind, ops, o, do, lse, bl, scale, name):
    fox = kind == "fox"
    if fox:
        qkv, fq, fk = ops
        t = qkv.shape[1]
        n_pair = FOX_HEADS // 2
    else:
        qn, qr, kn, kr, v = ops
        t = qn.shape[1]
        n_pair = MLA_HEADS // 2
    s = t // bl
    tq = _attn_tiles(s)
    nq = s // tq
    half = LANES // 2

    def body(*refs):
        if fox:
            (q_ref, k_ref, v_ref, fq_ref, fk_ref, o_ref, do_ref, lse_ref,
             dq_ref, dk_ref, dv_ref, dfk_ref, delta_scr, qt_scr, dot_scr) = refs
        else:
            (qn_ref, qr_ref, kn_ref, kr_ref, v_ref, o_ref, do_ref, lse_ref,
             dqn_ref, dqr_ref, dkn_ref, dv_ref, dkr_ref, delta_scr, qt_scr, qrt_scr, dot_scr) = refs
        g, j = pl.program_id(1), pl.program_id(2)
        row = lax.broadcasted_iota(jnp.int32, (tq, tq), 0)
        col = lax.broadcasted_iota(jnp.int32, (tq, tq), 1)
        krows = pl.ds(pl.multiple_of(j * tq, tq), tq)

        def transposed(v):
            return v.astype(F32).T.astype(BF16)

        def wide(stat):
            return jnp.concatenate([stat] * (tq // LANES), axis=1)

        @pl.when(j == 0)
        def _():
            if fox:
                dq_ref[...] = jnp.zeros_like(dq_ref)
            else:
                dqn_ref[...] = jnp.zeros_like(dqn_ref)
                dqr_ref[...] = jnp.zeros_like(dqr_ref)
            for ii in range(nq):
                rws = slice(ii * tq, (ii + 1) * tq)
                deltas = []
                if fox:
                    prod = do_ref[0, rws, :].astype(F32) * o_ref[0, rws, :].astype(F32)
                    for e in range(2):
                        deltas.append(jnp.sum(prod[:, e * half:(e + 1) * half], axis=1, keepdims=True))
                    qt_scr[ii] = transposed(q_ref[0, rws, :])
                    dot_scr[ii] = transposed(do_ref[0, rws, :])
                else:
                    for e in range(2):
                        prod = do_ref[e, rws, :].astype(F32) * o_ref[e, rws, :].astype(F32)
                        deltas.append(jnp.sum(prod, axis=1, keepdims=True))
                        qt_scr[e, ii] = transposed(qn_ref[e, rws, :])
                        dot_scr[e, ii] = transposed(do_ref[e, rws, :])
                    qrt_scr[ii] = transposed(qr_ref[0, rws, :])
                for e in range(2):
                    delta_scr[e, rws, :] = jnp.broadcast_to(deltas[e], (tq, LANES))

        if fox:
            dfk_ref[...] = jnp.zeros_like(dfk_ref)
        else:
            @pl.when(jnp.logical_and(g == 0, j == 0))
            def _():
                dkr_ref[...] = jnp.zeros_like(dkr_ref)

        heads = []
        for e in range(2):
            sl = slice(e * half, (e + 1) * half)
            if fox:
                heads.append((sl, k_ref[0, :, sl], v_ref[0, :, sl], fk_ref[0, 0, e:e + 1, :]))
            else:
                heads.append((sl, kn_ref[e], v_ref[e], kr_ref[krows, 0:half]))
        dk_w = dv_w = half if fox else LANES

        def step(i, carry, masked):
            rows = pl.ds(pl.multiple_of(i * tq, tq), tq)
            new = []
            for e, (sl, k_e, v_e, x_e) in enumerate(heads):
                dk_acc, dv_acc, last = carry[e]
                if fox:
                    do_i = do_ref[0, rows, sl]
                    sc = _nt(q_ref[0, rows, sl], k_e) * scale + wide(fq_ref[e, rows, :]) - x_e
                else:
                    do_i = do_ref[e, rows, :]
                    sc = (_nt(qn_ref[e, rows, :], k_e) + _nt(qr_ref[0, rows, sl], x_e)) * scale
                if masked:
                    sc = jnp.where(row >= col, sc, NEG_INF)
                p = jnp.exp(sc - wide(lse_ref[e, rows, :]))
                dp = _nt(do_i, v_e)
                ds = p * (dp - wide(delta_scr[e, rows, :]))
                dsb = (ds * scale).astype(BF16)
                if fox:
                    fsl = slice(e * half, (e + 1) * half)
                    dv_acc = dv_acc + _nn(dot_scr[i, fsl, :], p.astype(BF16))
                    dk_acc = dk_acc + _nn(qt_scr[i, fsl, :], dsb)
                    dq_ref[0, rows, sl] += _nn(dsb, k_e)
                    last = last - jnp.sum(ds, axis=0, keepdims=True)
                else:
                    dv_acc = dv_acc + _nn(dot_scr[e, i], p.astype(BF16))
                    dk_acc = dk_acc + _nn(qt_scr[e, i], dsb)
                    dqn_ref[e, rows, :] += _nn(dsb, k_e)
                    dqr_ref[0, rows, sl] += _nn(dsb, x_e)
                    last = last + _nn(qrt_scr[i, e * half:(e + 1) * half, :], dsb)
                new.append((dk_acc, dv_acc, last))
            return tuple(new)

        last0 = jnp.zeros((1, tq), F32) if fox else jnp.zeros((half, tq), F32)
        init = (jnp.zeros((dk_w, tq), F32), jnp.zeros((dv_w, tq), F32), last0)
        carry = step(j, (init, init), True)
        carry = lax.fori_loop(j + 1, nq, lambda i, c: step(i, c, False), carry)
        if fox:
            for e in range(2):
                dfk_ref[0, 0, e:e + 1, :] = carry[e][2]
            dk_ref[0] = jnp.concatenate([carry[0][0], carry[1][0]], axis=0).T.astype(BF16)
            dv_ref[0] = jnp.concatenate([carry[0][1], carry[1][1]], axis=0).T.astype(BF16)
        else:
            for e in range(2):
                dkn_ref[e] = carry[e][0].T.astype(BF16)
                dv_ref[e] = carry[e][1].T.astype(BF16)
            dkr_t = carry[0][2] + carry[1][2]
            dkr_ref[krows, :] += jnp.concatenate([dkr_t, jnp.zeros_like(dkr_t)], axis=0).T

    def whole(b, g, j):
        return (g, b, 0)

    def kblk(b, g, j):
        return (g, b * nq + j, 0)

    if fox:
        in_specs = [pl.BlockSpec((1, s, LANES), whole),
                    pl.BlockSpec((1, tq, LANES), lambda b, g, j: (n_pair + g, b * nq + j, 0)),
                    pl.BlockSpec((1, tq, LANES), lambda b, g, j: (2 * n_pair + g, b * nq + j, 0)),
                    pl.BlockSpec((2, s, LANES), whole),
                    pl.BlockSpec((1, 1, 8, tq), lambda b, g, j: (b * n_pair + g, j, 0, 0)),
                    pl.BlockSpec((1, s, LANES), whole), pl.BlockSpec((1, s, LANES), whole),
                    pl.BlockSpec((2, s, LANES), whole)]
        args = [qkv, qkv, qkv, fq, fk, o, do, lse]
        out_shape = [jax.ShapeDtypeStruct((8, t, LANES), F32), jax.ShapeDtypeStruct((8, t, LANES), BF16),
                     jax.ShapeDtypeStruct((8, t, LANES), BF16), jax.ShapeDtypeStruct(fk.shape, F32)]
        out_specs = [pl.BlockSpec((1, s, LANES), whole), pl.BlockSpec((1, tq, LANES), kblk),
                     pl.BlockSpec((1, tq, LANES), kblk),
                     pl.BlockSpec((1, 1, 8, tq), lambda b, g, j: (b * n_pair + g, j, 0, 0))]
    else:
        pair = pl.BlockSpec((2, s, LANES), whole)
        pair_k = pl.BlockSpec((2, tq, LANES), kblk)
        in_specs = [pair, pl.BlockSpec((1, s, LANES), whole), pair_k,
                    pl.BlockSpec((s, LANES), lambda b, g, j: (b, 0)), pair_k,
                    pair, pair, pair]
        args = [qn, qr, kn, kr, v, o, do, lse]
        out_shape = [jax.ShapeDtypeStruct((8, t, LANES), F32), jax.ShapeDtypeStruct((4, t, LANES), F32),
                     jax.ShapeDtypeStruct((8, t, LANES), BF16), jax.ShapeDtypeStruct((8, t, LANES), BF16),
                     jax.ShapeDtypeStruct((t, LANES), F32)]
        out_specs = [pair, pl.BlockSpec((1, s, LANES), whole), pair_k, pair_k,
                     pl.BlockSpec((s, LANES), lambda b, g, j: (b, 0))]
    t_blocks = pltpu.VMEM((nq, LANES, tq), BF16)
    t_pairs = pltpu.VMEM((2, nq, LANES, tq), BF16)
    scratch = [pltpu.VMEM((2, s, LANES), F32)] + ([t_blocks, t_blocks] if fox else [t_pairs, t_blocks, t_pairs])
    return pl.pallas_call(
        body, name=name, out_shape=out_shape, grid=(bl, n_pair, nq), in_specs=in_specs, out_specs=out_specs,
        scratch_shapes=scratch, compiler_params=_params("arbitrary", "arbitrary", "arbitrary"),
    )(*args)


def adamw(w, g, m, v, name):
    shape = w.shape
    c = shape[-1]
    r = w.size // c
    tr = _tile(r, 512, 8)

    def body(w_ref, g_ref, m_ref, v_ref, d_ref, nm_ref, nv_ref):
        gv = g_ref[...]
        m2 = ADAM_B1 * m_ref[...] + (1.0 - ADAM_B1) * gv
        v2 = ADAM_B2 * v_ref[...] + (1.0 - ADAM_B2) * (gv * gv)
        m_hat = m2 / (1.0 - ADAM_B1 ** ADAM_STEP)
        v_hat = v2 / (1.0 - ADAM_B2 ** ADAM_STEP)
        d_ref[...] = -ADAM_LR * (m_hat / (jnp.sqrt(v_hat) + ADAM_EPS) + ADAM_WD * w_ref[...])
        nm_ref[...] = m2
        nv_ref[...] = v2

    spec = pl.BlockSpec((tr, c), lambda i: (i, 0))
    outs = pl.pallas_call(
        body, name=name, out_shape=[jax.ShapeDtypeStruct((r, c), F32)] * 3, grid=(r // tr,),
        in_specs=[spec] * 4, out_specs=[spec] * 3, compiler_params=_params("arbitrary"),
    )(*(a.reshape(r, c) for a in (w, g, m, v)))
    return tuple(a.reshape(shape) for a in outs)


PACK_COLS = 1024


def _pack_rows(a):
    return a.reshape(-1, PACK_COLS)


def kernel(x, c, positions, mla_w_in, mla_g_q, mla_w_uq, mla_g_kv, mla_w_uk, mla_w_uv, mla_w_o, fox_w_in, fox_b_f, fox_w_o, ada_w, ada_b, ffn_w_gate, ffn_w_up, ffn_w_down, ln_g, ln_b, loss_target, m_mla_w_in, m_mla_g_q, m_mla_w_uq, m_mla_g_kv, m_mla_w_uk, m_mla_w_uv, m_mla_w_o, m_fox_w_in, m_fox_b_f, m_fox_w_o, m_ada_w, m_ada_b, m_ffn_w_gate, m_ffn_w_up, m_ffn_w_down, m_ln_g, m_ln_b, v_mla_w_in, v_mla_g_q, v_mla_w_uq, v_mla_g_kv, v_mla_w_uk, v_mla_w_uv, v_mla_w_o, v_fox_w_in, v_fox_b_f, v_fox_w_o, v_ada_w, v_ada_b, v_ffn_w_gate, v_ffn_w_up, v_ffn_w_down, v_ln_g, v_ln_b):
    bl, s, d = x.shape
    t = bl * s
    ff = ffn_w_gate.shape[-1] * N_DEV
    dev = 4 * lax.axis_index("x") + 2 * lax.axis_index("y") + lax.axis_index("c")
    ada_cols = ada_w.shape[-1]
    fox_in = fox_w_in.shape[-1] * N_DEV
    mla_in = mla_w_in.shape[-1]
    mla_in_pad = mla_in + (-mla_in) % LANES

    def t_last(a):
        return jnp.swapaxes(a, -1, -2)

    local = {
        "mla_w_in": mla_w_in[0],
        "mla_w_uq": t_last(mla_w_uq[0]),
        "mla_w_uk": t_last(mla_w_uk[0]),
        "mla_w_uv": t_last(mla_w_uv[0]),
        "mla_w_o": mla_w_o[0],
        "fox_w_in": t_last(fox_w_in[0]),
        "fox_w_o": fox_w_o[0],
    }
    for i in range(DEPTH):
        local.update({f"gate{i}": t_last(ffn_w_gate[i]), f"up{i}": t_last(ffn_w_up[i]), f"down{i}": ffn_w_down[i]})
    groups = [["mla_w_in", "mla_w_uq", "mla_w_uk", "mla_w_uv", "mla_w_o"],
              ["gate0", "up0", "down0"],
              ["fox_w_in", "fox_w_o"],
              ["gate1", "up1", "down1"]]
    offsets, rows_of, slot_of, group_of = {}, {}, {}, {}
    group_rows = []
    for gi, names in enumerate(groups):
        rows = 0
        for nm in names:
            rows_of[nm] = local[nm].size // PACK_COLS
            slot_of[nm] = rows_of[nm] + (-rows_of[nm]) % 16
            offsets[nm] = rows
            group_of[nm] = gi
            rows += slot_of[nm]
        group_rows.append(rows)

    def slot(nm, rows):
        pad = [(0, 0)] * rows.ndim
        pad[-2] = (0, slot_of[nm] - rows_of[nm])
        return jnp.pad(rows, pad)

    def held_until(block, arrays):
        zero = sum((a.reshape(-1)[0] * 0).astype(F32) for a in jax.tree.leaves(arrays))
        return block + zero.astype(block.dtype)

    def landing(block):
        land = lax.empty((N_DEV,) + block.shape, block.dtype)
        return lax.dynamic_update_slice(land, block[None], (dev, 0, 0))

    packed0 = jnp.concatenate([slot(nm, _pack_rows(local[nm]).astype(BF16)) for nm in groups[0]], axis=0)
    gathered0 = all_gather(packed0, "gather_mla_weights")
    gathered = {nm: gathered0[:, offsets[nm]:offsets[nm] + rows_of[nm], :] for nm in groups[0]}
    gather_started = [None] * len(groups)

    def depart(gi, after):
        blocks = [held_until(_pack_rows(local[nm]).astype(BF16), after) for nm in groups[gi]]
        gather_started[gi] = exchange_start(blocks, [landing(b) for b in blocks], f"gather_group{gi}_start", False)
        return gather_started[gi][4]

    def full(nm, cols):
        return gathered[nm].reshape(-1, cols)

    w_in = jnp.pad(full("mla_w_in", mla_in), ((0, 0), (0, mla_in_pad - mla_in)))
    wt_uq = full("mla_w_uq", MLA_QR).reshape(MLA_HEADS, MLA_NOPE + MLA_ROPE, MLA_QR)
    wt_uq_n = wt_uq[:, :MLA_NOPE].reshape(MLA_HEADS * MLA_NOPE, MLA_QR)
    wt_uq_r = wt_uq[:, MLA_NOPE:].reshape(MLA_HEADS * MLA_ROPE, MLA_QR)
    wt_uk = full("mla_w_uk", MLA_KVR)
    wt_uv = full("mla_w_uv", MLA_KVR)
    w_mo = full("mla_w_o", d)
    wt_gate, wt_up, w_down = [None] * DEPTH, [None] * DEPTH, [None] * DEPTH

    def arrive(gi, after):
        if gi + 1 < len(groups):
            after = depart(gi + 1, after)
        landed = list(exchange_wait(gather_started[gi], after, f"gather_group{gi}_wait", False))
        gathered.update(zip(groups[gi], landed))
        for i in range(DEPTH):
            if group_of[f"gate{i}"] == gi:
                wt_gate[i], wt_up[i], w_down[i] = full(f"gate{i}", d), full(f"up{i}", d), full(f"down{i}", d)

    small = jnp.concatenate([c.reshape(-1, LANES), ln_g.reshape(-1, LANES), ln_b.reshape(-1, LANES)], axis=0)
    small_rows = small.shape[0]
    small = jnp.pad(small, ((0, (-small_rows) % 8), (0, 0)))
    small_all = all_gather(small, "gather_small")
    c_rows = bl * d // LANES
    c_all = small_all[:, :c_rows].reshape(N_DEV * bl, d)
    n_ln = DEPTH * 2
    ln_g_all = small_all[:, c_rows:c_rows + n_ln, :].transpose(1, 0, 2).reshape(DEPTH, 2, 1, d)
    ln_b_all = small_all[:, c_rows + n_ln:c_rows + 2 * n_ln, :].transpose(1, 0, 2).reshape(DEPTH, 2, 1, d)

    c_act = silu_rows(c_all, "silu_c")
    ada_b_loc = lax.dynamic_slice_in_dim(ada_b, dev * ada_cols, ada_cols, axis=1)
    mod_cols = [mm([(c_act, ada_w[i])], trans_b=False, out_dtype=F32, name=f"ada_fwd{i}", bias=ada_b_loc[i][None, :])
                for i in range(DEPTH)]
    mod_all = all_gather(jnp.concatenate(mod_cols, axis=0), "gather_mod")
    mod_all = mod_all.reshape(N_DEV, DEPTH, N_DEV * bl, ada_cols).transpose(1, 2, 0, 3).reshape(DEPTH, N_DEV * bl, 6 * d)
    mod_mine = lax.dynamic_slice_in_dim(mod_all, dev * bl, bl, axis=1)
    mods = [mod_mine[i].reshape(bl * 6, 1, d) for i in range(DEPTH)]
    mods[0] = mods[0] + depart(1, (mod_mine, gathered0))[0, 0]

    half_r = MLA_ROPE // 2
    inv_freq = ROPE_THETA ** (-jnp.arange(half_r, dtype=F32) / half_r)
    inv_freq = jnp.tile(inv_freq, LANES // half_r)[None, :]
    sign = jnp.tile(jnp.concatenate([-jnp.ones((half_r,), F32), jnp.ones((half_r,), F32)]), LANES // MLA_ROPE)[None, :]
    cos_t, sin_t = rope_tables(positions.astype(F32).reshape(t, 1), inv_freq, sign, "rope_tables")

    x2d = x.reshape(t, d)
    g_q, g_kv = mla_g_q.reshape(1, MLA_QR), mla_g_kv.reshape(1, MLA_KVR)
    b_f = jnp.pad(fox_b_f.reshape(1, FOX_HEADS), ((0, 0), (0, LANES - FOX_HEADS)))
    mla_scale = (MLA_NOPE + MLA_ROPE) ** -0.5
    fox_scale = FOX_HD ** -0.5
    tq = _attn_tiles(s)
    nk = s // tq

    saved = []
    u = modulate(x2d, mods[0], 0, 1, bl, "modulate0")
    xin = x2d
    for i in range(DEPTH):
        sv = {"u": u, "x_in": xin}
        if i % 2 == 0:
            h_in = mm([(u, w_in)], trans_b=False, out_dtype=F32, name=f"mla_in{i}")
            c_q, c_kv, k_r = mla_latents_forward(h_in, g_q, g_kv, cos_t, sin_t, f"mla_latents{i}")
            q_n = mm([(c_q, wt_uq_n)], trans_b=True, out_dtype=BF16, out_slab=True, name=f"mla_qn{i}")
            q_r_raw = mm([(c_q, wt_uq_r)], trans_b=True, out_dtype=F32, out_slab=True, name=f"mla_qr{i}")
            q_r = rope_slabs(q_r_raw, cos_t, sin_t, BF16, f"mla_qrope{i}")
            k_n = mm([(c_kv, wt_uk)], trans_b=True, out_dtype=BF16, out_slab=True, name=f"mla_kn{i}")
            v_m = mm([(c_kv, wt_uv)], trans_b=True, out_dtype=BF16, out_slab=True, name=f"mla_v{i}")
            ops = (q_n, q_r, k_n, k_r, v_m)
            o, lse, o_delta = attention_forward("mla", ops, bl, mla_scale, f"mla_attn{i}")
            y = mm([(o, w_mo)], trans_b=False, out_dtype=F32, name=f"mla_out{i}")
            sv.update(h_in=h_in, c_q=c_q, c_kv=c_kv, ops=ops, o=o, lse=lse, o_delta=o_delta)
        else:
            arrive(2, u)
            wt_fox = full("fox_w_in", d)
            wt_qkv = wt_fox[:3 * d]
            wt_f = jnp.pad(wt_fox[3 * d:], ((0, LANES - FOX_HEADS), (0, 0)))
            w_fo = full("fox_w_o", d)
            qkv = mm([(u, wt_qkv)], trans_b=True, out_dtype=BF16, out_slab=True, name=f"fox_qkv{i}")
            z = mm([(u, wt_f)], trans_b=True, out_dtype=F32, name=f"fox_z{i}")
            f_tok, f_q = fox_gate_forward(z, b_f, bl, f"fox_gate{i}")
            f_k = f_tok[:, :FOX_HEADS].reshape(bl, nk, tq, FOX_HEADS // 2, 2).transpose(0, 3, 1, 4, 2)
            f_k = jnp.pad(f_k.reshape(bl * FOX_HEADS // 2, nk, 2, tq), ((0, 0), (0, 0), (0, 6), (0, 0)))
            ops = (qkv, f_q, f_k)
            o, lse, o_delta = attention_forward("fox", ops, bl, fox_scale, f"fox_attn{i}")
            y = mm([(o, w_fo)], trans_b=False, out_dtype=F32, name=f"fox_out{i}")
            sv.update(z=z, ops=ops, o=o, lse=lse, o_delta=o_delta)
        x1, r1, u2 = residual_layer_norm(xin, y, mods[i], 2, ln_g_all[i, 0], ln_b_all[i, 0], bl, f"ln_mix{i}",
                                         next_mod=(3, 4))
        if wt_gate[i] is None:
            arrive(group_of[f"gate{i}"], u2)
        a, bb, h = swiglu_in(u2, wt_gate[i], wt_up[i], f"ffn_in{i}")
        y2 = mm([(h, w_down[i])], trans_b=False, out_dtype=F32, name=f"ffn_down{i}")
        sv.update(y=y, r1=r1, u2=u2, a=a, bb=bb, h=h, y2=y2)
        if i + 1 < DEPTH:
            xin, r2, u = residual_layer_norm(x1, y2, mods[i], 5, ln_g_all[i, 1], ln_b_all[i, 1], bl, f"ln_ffn{i}",
                                             next_mod=(0, 1, mods[i + 1]))
        else:
            xin, r2 = residual_layer_norm(x1, y2, mods[i], 5, ln_g_all[i, 1], ln_b_all[i, 1], bl, f"ln_ffn{i}")
        sv.update(r2=r2)
        saved.append(sv)

    loss_cols, d_x = loss_head(xin, loss_target.reshape(t, d), "loss_head")

    grads_full = {}
    wgrad = functools.partial(mm_tn, out_dtype=BF16)
    dmod = [[None] * 6 for _ in range(DEPTH)]
    dg_ln = [[None, None] for _ in range(DEPTH)]
    db_ln = [[None, None] for _ in range(DEPTH)]
    dg_q = dg_kv = db_f = None
    d_a, du = d_x, None
    scatter_started = [None] * len(groups)

    def scatter_start(gi, after=None):
        gs = [grads_full[nm].reshape(N_DEV, rows_of[nm], PACK_COLS).astype(BF16) for nm in groups[gi]]
        if gi == 0:
            gs = [jnp.concatenate([slot(nm, g) for nm, g in zip(groups[gi], gs)], axis=1)]
        if after is not None:
            gs = [held_until(g, after) for g in gs]
        lands = [landing(lax.dynamic_index_in_dim(g, dev, 0, keepdims=False)) for g in gs]
        scatter_started[gi] = exchange_start(gs, lands, f"scatter_group{gi}_start", True)

    ln_g_bwd = [[ln_g_all[i, k] for k in range(2)] for i in range(DEPTH)]
    for i in reversed(range(DEPTH)):
        sv = saved[i]
        if i + 1 < DEPTH:
            gi = group_of["fox_w_in"]
            scatter_start(gi)
            ln_g_bwd[i][1] = after_token(ln_g_bwd[i][1], scatter_started[gi])
        ln2 = (sv["r2"], sv["y2"], ln_g_bwd[i][1], ln_b_all[i, 1], (mods[i], 5))
        if du is None:
            bw = sublayer_backward(d_a, bl, f"bwd_ln_ffn{i}", ln=ln2)
        else:
            bw = sublayer_backward(d_a, bl, f"bwd_ln_ffn{i}", du=du, scale=(mods[i + 1], 1), ln=ln2)
            dmod[i + 1][0], dmod[i + 1][1] = bw["dshift"], bw["dscale"]
        dmod[i][5], dg_ln[i][1], db_ln[i][1] = bw["dgate"], bw["dg"], bw["db"]
        dy2 = bw["dy"]
        da, dbb = swiglu_out_backward(dy2, w_down[i], sv["a"], sv["bb"], f"bwd_ffn_act{i}")
        du2 = mm([(da, wt_gate[i]), (dbb, wt_up[i])], trans_b=False, out_dtype=F32, name=f"bwd_ffn_du{i}")
        grads_full[f"down{i}"] = wgrad(sv["h"], dy2, name=f"bwd_w_down{i}")
        grads_full[f"gate{i}"] = wgrad(da, sv["u2"], name=f"bwd_w_gate{i}")
        grads_full[f"up{i}"] = wgrad(dbb, sv["u2"], name=f"bwd_w_up{i}")
        gi = group_of[f"gate{i}"]
        scatter_start(gi)
        ln_g_bwd[i][0] = after_token(ln_g_bwd[i][0], scatter_started[gi])
        bw = sublayer_backward(bw["dx"], bl, f"bwd_ln_mix{i}", du=du2, scale=(mods[i], 4),
                               ln=(sv["r1"], sv["y"], ln_g_bwd[i][0], ln_b_all[i, 0], (mods[i], 2)))
        dmod[i][3], dmod[i][4], dmod[i][2] = bw["dshift"], bw["dscale"], bw["dgate"]
        dg_ln[i][0], db_ln[i][0] = bw["dg"], bw["db"]
        d_a, dy = bw["dx"], bw["dy"]
        o, lse, ops = sv["o"], sv["lse"], sv["ops"]
        if i % 2 == 0:
            do = mm([(dy, w_mo)], trans_b=True, out_dtype=BF16, out_slab=True, name=f"bwd_mla_do{i}")
            grads_full["mla_w_o"] = wgrad(o, dy, name=f"bwd_w_mla_o{i}")
            dqn, dqr, dkn, dvm, dkr = attention_backward("mla", ops, sv["o_delta"], do, lse, bl, mla_scale,
                                                         f"bwd_mla_attn{i}")
            dqr = rope_slabs(dqr, cos_t, sin_t, F32, f"bwd_mla_qrope{i}", transposed=True)
            dcq = mm([(dqn, wt_uq_n), (dqr, wt_uq_r)], trans_b=False, out_dtype=F32, name=f"bwd_mla_dcq{i}")
            dckv = mm([(dkn, wt_uk), (dvm, wt_uv)], trans_b=False, out_dtype=F32, name=f"bwd_mla_dckv{i}")
            d_uq_n = wgrad(dqn, sv["c_q"], name=f"bwd_w_uq_n{i}").reshape(MLA_HEADS, MLA_NOPE, MLA_QR)
            d_uq_r = wgrad(dqr, sv["c_q"], name=f"bwd_w_uq_r{i}").reshape(MLA_HEADS, MLA_ROPE, MLA_QR)
            grads_full["mla_w_uq"] = jnp.concatenate([d_uq_n, d_uq_r], axis=1)
            grads_full["mla_w_uk"] = wgrad(dkn, sv["c_kv"], name=f"bwd_w_uk{i}")
            grads_full["mla_w_uv"] = wgrad(dvm, sv["c_kv"], name=f"bwd_w_uv{i}")
            dh_in, dg_q, dg_kv = mla_latents_backward(sv["h_in"], dcq, dckv, dkr, g_q, g_kv, cos_t, sin_t,
                                                      f"bwd_mla_latents{i}")
            du = mm([(dh_in, w_in)], trans_b=True, out_dtype=F32, name=f"bwd_mla_du{i}")
            grads_full["mla_w_in"] = wgrad(sv["u"], dh_in, name=f"bwd_w_mla_in{i}")[:, :mla_in]
        else:
            do = mm([(dy, w_fo)], trans_b=True, out_dtype=BF16, out_slab=True, name=f"bwd_fox_do{i}")
            grads_full["fox_w_o"] = wgrad(o, dy, name=f"bwd_w_fox_o{i}")
            dq, dk, dvf, dfk = attention_backward("fox", ops, sv["o_delta"], do, lse, bl, fox_scale, f"bwd_fox_attn{i}")
            df = dfk[:, :, :2, :].reshape(bl, FOX_HEADS // 2, nk, 2, tq).transpose(0, 2, 4, 1, 3).reshape(t, FOX_HEADS)
            df = jnp.pad(df, ((0, 0), (0, LANES - FOX_HEADS)))
            dz, db_f = fox_gate_backward(sv["z"], b_f, df, bl, f"bwd_fox_gate{i}")
            du = mm([(dq, wt_fox[0:d]), (dk, wt_fox[d:2 * d]), (dvf, wt_fox[2 * d:3 * d]), (dz, wt_f)],
                    trans_b=False, out_dtype=F32, name=f"bwd_fox_du{i}")
            u_f = sv["u"]
            grads_full["fox_w_in"] = jnp.concatenate(
                [wgrad(dq, u_f, name=f"bwd_w_fox_q{i}"), wgrad(dk, u_f, name=f"bwd_w_fox_k{i}"),
                 wgrad(dvf, u_f, name=f"bwd_w_fox_v{i}"), wgrad(dz, u_f, name=f"bwd_w_fox_f{i}")[:FOX_HEADS]], axis=0)
    scatter_start(0)
    bw = sublayer_backward(d_a, bl, "bwd_input", du=du, scale=(after_token(mods[0], scatter_started[0]), 1), x_in=x2d)
    dmod[0][0], dmod[0][1] = bw["dshift"], bw["dscale"]
    grad_x = bw["dx"].reshape(bl, s, d)

    dmod_rows = jnp.concatenate([r.reshape(bl, d) for layer in dmod for r in layer], axis=0)
    dmod_rows = dmod_rows.reshape(DEPTH, 6, bl, d).transpose(0, 2, 1, 3)
    n_mod = dmod_rows.size // LANES
    ln_parts = [dg_ln[i][k] for i in range(DEPTH) for k in range(2)] + [db_ln[i][k] for i in range(DEPTH) for k in range(2)]
    small_g = jnp.concatenate([dmod_rows.reshape(-1, LANES), dg_q.reshape(-1, LANES), dg_kv.reshape(-1, LANES), db_f]
                              + [p.reshape(-1, LANES) for p in ln_parts] + [loss_cols.reshape(-1, LANES)], axis=0)
    n_small = small_g.shape[0]
    small_g = jnp.pad(small_g, ((0, (-n_small) % 8), (0, 0)))
    small_g_all = all_gather(small_g, "gather_small_grads")
    small_sum = sum_leading(small_g_all, "sum_small_grads")
    per_seq = DEPTH * 6 * d // LANES
    dmod_all = small_g_all[:, :n_mod].reshape(N_DEV, DEPTH, bl, 6 * d).transpose(1, 0, 2, 3)
    dmod_all = dmod_all.reshape(DEPTH, N_DEV * bl, 6 * d)
    o1 = n_mod
    grad_g_q = small_sum[o1:o1 + MLA_QR // LANES].reshape(1, MLA_QR)
    o1 += MLA_QR // LANES
    grad_g_kv = small_sum[o1:o1 + MLA_KVR // LANES].reshape(1, MLA_KVR)
    o1 += MLA_KVR // LANES
    grad_b_f = small_sum[o1:o1 + 1, :FOX_HEADS]
    o1 += 1
    n_ln_rows = DEPTH * 2 * d // LANES
    grad_ln_g_full = small_sum[o1:o1 + n_ln_rows].reshape(DEPTH, 2, d)
    grad_ln_b_full = small_sum[o1 + n_ln_rows:o1 + 2 * n_ln_rows].reshape(DEPTH, 2, d)
    loss = jnp.sum(small_sum[o1 + 2 * n_ln_rows:o1 + 2 * n_ln_rows + d // LANES])
    shard = d // N_DEV
    grad_ln_g = lax.dynamic_slice_in_dim(grad_ln_g_full, dev * shard, shard, axis=2)
    grad_ln_b = lax.dynamic_slice_in_dim(grad_ln_b_full, dev * shard, shard, axis=2)
    by_seq = small_g_all[:, :n_mod].reshape(N_DEV, DEPTH, bl, 6 * d // LANES, LANES).transpose(0, 2, 1, 3, 4)
    grad_ada_b = sum_leading(by_seq.reshape(N_DEV * bl, per_seq, LANES), "sum_ada_b").reshape(DEPTH, 6 * d)
    dmod_cols = lax.dynamic_slice_in_dim(dmod_all, dev * ada_cols, ada_cols, axis=2)
    grad_ada_w = jnp.stack([mm_tn(c_act, dmod_cols[i], name=f"bwd_w_ada{i}") for i in range(DEPTH)])

    g_mine = {}

    def scatter_arrive(gi, after):
        landed = exchange_wait(scatter_started[gi], after, f"scatter_group{gi}_wait", True)
        if gi == 0:
            total = sum_leading(landed[0], f"scatter_group{gi}_sum")
            g_mine.update({nm: total[offsets[nm]:offsets[nm] + rows_of[nm]] for nm in groups[gi]})
            return total
        for nm, land in zip(groups[gi], landed):
            g_mine[nm] = sum_leading(land, f"scatter_sum_{nm}")
        return g_mine[groups[gi][-1]]

    after = scatter_started[0][4]
    for gi in reversed(range(1, len(groups))):
        after = scatter_arrive(gi, after)

    def mine(nm, shape):
        return g_mine[nm].reshape(shape)

    def shard_t(nm, a):
        return mine(nm, t_last(a).shape)

    transposed = {"mla_w_uq", "mla_w_uk", "mla_w_uv", "fox_w_in", "ffn_w_gate", "ffn_w_up"}
    grads = {
        "mla_w_in": lambda: mine("mla_w_in", mla_w_in[0].shape)[None],
        "mla_g_q": lambda: grad_g_q,
        "mla_w_uq": lambda: shard_t("mla_w_uq", mla_w_uq[0])[None],
        "mla_g_kv": lambda: grad_g_kv,
        "mla_w_uk": lambda: shard_t("mla_w_uk", mla_w_uk[0])[None],
        "mla_w_uv": lambda: shard_t("mla_w_uv", mla_w_uv[0])[None],
        "mla_w_o": lambda: mine("mla_w_o", mla_w_o[0].shape)[None],
        "fox_w_in": lambda: shard_t("fox_w_in", fox_w_in[0])[None],
        "fox_b_f": lambda: grad_b_f,
        "fox_w_o": lambda: mine("fox_w_o", fox_w_o[0].shape)[None],
        "ada_w": lambda: grad_ada_w,
        "ada_b": lambda: grad_ada_b,
        "ffn_w_gate": lambda: jnp.stack([shard_t(f"gate{i}", ffn_w_gate[i]) for i in range(DEPTH)]),
        "ffn_w_up": lambda: jnp.stack([shard_t(f"up{i}", ffn_w_up[i]) for i in range(DEPTH)]),
        "ffn_w_down": lambda: jnp.stack([mine(f"down{i}", ffn_w_down[i].shape) for i in range(DEPTH)]),
        "ln_g": lambda: grad_ln_g,
        "ln_b": lambda: grad_ln_b,
    }
    weights = dict(mla_w_in=mla_w_in, mla_g_q=mla_g_q, mla_w_uq=mla_w_uq, mla_g_kv=mla_g_kv, mla_w_uk=mla_w_uk,
                   mla_w_uv=mla_w_uv, mla_w_o=mla_w_o, fox_w_in=fox_w_in, fox_b_f=fox_b_f, fox_w_o=fox_w_o,
                   ada_w=ada_w, ada_b=ada_b, ffn_w_gate=ffn_w_gate, ffn_w_up=ffn_w_up, ffn_w_down=ffn_w_down,
                   ln_g=ln_g, ln_b=ln_b)
    first = dict(mla_w_in=m_mla_w_in, mla_g_q=m_mla_g_q, mla_w_uq=m_mla_w_uq, mla_g_kv=m_mla_g_kv, mla_w_uk=m_mla_w_uk,
                 mla_w_uv=m_mla_w_uv, mla_w_o=m_mla_w_o, fox_w_in=m_fox_w_in, fox_b_f=m_fox_b_f, fox_w_o=m_fox_w_o,
                 ada_w=m_ada_w, ada_b=m_ada_b, ffn_w_gate=m_ffn_w_gate, ffn_w_up=m_ffn_w_up, ffn_w_down=m_ffn_w_down,
                 ln_g=m_ln_g, ln_b=m_ln_b)
    second = dict(mla_w_in=v_mla_w_in, mla_g_q=v_mla_g_q, mla_w_uq=v_mla_w_uq, mla_g_kv=v_mla_g_kv, mla_w_uk=v_mla_w_uk,
                  mla_w_uv=v_mla_w_uv, mla_w_o=v_mla_w_o, fox_w_in=v_fox_w_in, fox_b_f=v_fox_b_f, fox_w_o=v_fox_w_o,
                  ada_w=v_ada_w, ada_b=v_ada_b, ffn_w_gate=v_ffn_w_gate, ffn_w_up=v_ffn_w_up, ffn_w_down=v_ffn_w_down,
                  ln_g=v_ln_g, ln_b=v_ln_b)
    order = list(weights)
    last = [nm for nm in order if group_of.get(nm) == 0]
    updated = {}
    for nm in [nm for nm in order if nm not in last] + last:
        if last and nm == last[0]:
            scatter_arrive(0, after)
        lay = t_last if nm in transposed else (lambda a: a)
        w = lay(weights[nm])
        g = grads[nm]().reshape(w.shape)
        delta, new_m, new_v = adamw(w, g, lay(first[nm]), lay(second[nm]), f"adamw_{nm}")
        updated[nm] = (lay(g), lay(delta), lay(new_m), lay(new_v))
        after = new_v
    return (loss, grad_x, *(updated[nm][k] for k in range(4) for nm in order))
```

```python
import functools
import math

import jax
import jax.numpy as jnp
from jax import lax
from jax.experimental import pallas as pl
from jax.experimental.pallas import tpu as pltpu

F32 = jnp.float32
BF16 = jnp.bfloat16
LANES = 128
N_DEV = 8
VMEM_LIMIT_BYTES = 56 * 1024 * 1024

DEPTH = 2
MLA_HEADS = 8
MLA_NOPE = 128
MLA_ROPE = 64
MLA_V = 128
MLA_QR = 256
MLA_KVR = 256
ROPE_THETA = 10000.0
FOX_HEADS = 16
FOX_HD = 64
ALPHA = (2.0 * DEPTH) ** 0.25
NORM_EPS = 1e-5
ADAM_LR = 0.001
ADAM_B1 = 0.9
ADAM_B2 = 0.999
ADAM_EPS = 1e-08
ADAM_WD = 0.01
ADAM_STEP = 10

MESH_AXES = ("x", "y", "c")
MESH = pl.DeviceIdType.MESH


def _params(*sem):
    return pltpu.CompilerParams(dimension_semantics=sem, vmem_limit_bytes=VMEM_LIMIT_BYTES)


def _tile(n, cap, mult=LANES):
    if n <= cap:
        return n
    best = None
    for t in range(mult, cap + 1, mult):
        if n % t == 0:
            best = t
    assert best is not None, (n, cap, mult)
    return best


def _dot(a, b, dims):
    return lax.dot_general(a, b, (dims, ((), ())), preferred_element_type=F32)


def _nn(a, b):
    return _dot(a, b, ((1,), (0,)))


def _nt(a, b):
    return _dot(a, b, ((1,), (1,)))


def _tn(a, b):
    return _dot(a, b, ((0,), (0,)))


def _me():
    return lax.axis_index("x"), lax.axis_index("y"), lax.axis_index("c")


def all_gather(x_loc, name):
    r, c = x_loc.shape

    def body(x_ref, out_ref, send_sems, recv_sems, local_sem):
        x, y, cc = _me()
        me, sibling = (x, y, cc), (x, y, 1 - cc)
        chips = [(1 - x, y), (x, 1 - y), (1 - x, 1 - y)]

        def rows(px, py, pc):
            return out_ref.at[4 * px + 2 * py + pc]

        def copy(k, block, to, src=None):
            return pltpu.make_async_remote_copy(
                src_ref=rows(*block) if src is None else src, dst_ref=rows(*block),
                send_sem=send_sems.at[k], recv_sem=recv_sems.at[k], device_id=to, device_id_type=MESH)

        mine = pltpu.make_async_copy(x_ref, rows(*me), local_sem)
        mine.start()
        first = [copy(0, me, sibling, src=x_ref)]
        first += [copy(1 + j, me, (*chip, cc), src=x_ref) for j, chip in enumerate(chips)]
        for cp in first:
            cp.start()
        passed = [copy(4 + j, (*chip, cc), sibling) for j, chip in enumerate(chips)]
        for j, chip in enumerate(chips):
            copy(1 + j, (*chip, cc), me).wait_recv()
            passed[j].start()
        copy(0, sibling, me).wait_recv()
        for j, chip in enumerate(chips):
            copy(4 + j, (*chip, 1 - cc), me).wait_recv()
        for cp in first + passed:
            cp.wait_send()
        mine.wait()

    return pl.pallas_call(
        body, name=name,
        out_shape=jax.ShapeDtypeStruct((N_DEV, r, c), x_loc.dtype),
        in_specs=[pl.BlockSpec(memory_space=pl.ANY)],
        out_specs=pl.BlockSpec(memory_space=pl.ANY),
        scratch_shapes=[pltpu.SemaphoreType.DMA((7,)), pltpu.SemaphoreType.DMA((7,)), pltpu.SemaphoreType.DMA(())],
    )(x_loc)


HBM_SPEC = pl.BlockSpec(memory_space=pltpu.HBM)
SEM_SPEC = pl.BlockSpec(memory_space=pltpu.SEMAPHORE)
N_PEERS = N_DEV - 1


def _peer(k):
    x, y, c = _me()
    return (1 - x if k & 4 else x, 1 - y if k & 2 else y, 1 - c if k & 1 else c)


def _exchange_copies(src_refs, land_refs, send_sems, recv_sems, scatter):
    x, y, c = _me()
    mine = 4 * x + 2 * y + c
    copies = []
    for n, (src_ref, land_ref) in enumerate(zip(src_refs, land_refs)):
        for k in range(1, N_DEV):
            px, py, pc = _peer(k)
            src = src_ref.at[4 * px + 2 * py + pc] if scatter else src_ref
            sem = n * N_PEERS + k - 1
            copies.append(pltpu.make_async_remote_copy(
                src_ref=src, dst_ref=land_ref.at[mine], send_sem=send_sems.at[sem], recv_sem=recv_sems.at[sem],
                device_id=(px, py, pc), device_id_type=MESH))
    return copies


def exchange_start(srcs, lands, name, scatter):
    n = len(srcs)

    def body(*refs):
        send_sems, recv_sems = refs[2 * n], refs[2 * n + 1]
        for cp in _exchange_copies(refs[:n], refs[n:2 * n], send_sems, recv_sems, scatter):
            cp.start()
        token = refs[-1]
        token[...] = jnp.zeros_like(token)

    outs = pl.pallas_call(
        body, name=name,
        out_shape=(pltpu.SemaphoreType.DMA((n * N_PEERS,)), pltpu.SemaphoreType.DMA((n * N_PEERS,)),
                   *(pltpu.HBM(a.shape, a.dtype) for a in (*srcs, *lands)), jax.ShapeDtypeStruct((8, LANES), F32)),
        in_specs=(HBM_SPEC,) * (2 * n),
        out_specs=(SEM_SPEC, SEM_SPEC, *((HBM_SPEC,) * (2 * n)), pl.BlockSpec(memory_space=pltpu.VMEM)),
        input_output_aliases={i: 2 + i for i in range(2 * n)},
        compiler_params=pltpu.CompilerParams(has_side_effects=pltpu.SideEffectType.DATAFLOW_SIDE_EFFECTING),
    )(*(pltpu.with_memory_space_constraint(a, pltpu.HBM) for a in (*srcs, *lands)))
    return outs[0], outs[1], outs[2:2 + n], outs[2 + n:2 + 2 * n], outs[-1]


def exchange_wait(started, after, name, scatter):
    send_sems, recv_sems, srcs, lands, _ = started
    n = len(srcs)

    def body(*refs):
        send_sems, recv_sems = refs[2 * n], refs[2 * n + 1]
        for cp in _exchange_copies(refs[:n], refs[n:2 * n], send_sems, recv_sems, scatter):
            cp.wait_send()
            cp.wait_recv()

    outs = pl.pallas_call(
        body, name=name,
        out_shape=tuple(pltpu.HBM(a.shape, a.dtype) for a in (*srcs, *lands)),
        in_specs=(*((HBM_SPEC,) * (2 * n)), SEM_SPEC, SEM_SPEC, pl.BlockSpec(memory_space=pl.ANY)),
        out_specs=(HBM_SPEC,) * (2 * n), input_output_aliases={i: i for i in range(2 * n)},
        compiler_params=pltpu.CompilerParams(has_side_effects=pltpu.SideEffectType.DATAFLOW_SIDE_EFFECTING),
    )(*srcs, *lands, send_sems, recv_sems, after)
    return outs[n:]


def after_token(small, started):
    return small + started[4][0, 0]


def sum_leading(x, name):
    n, r, c = x.shape
    tr = _tile(r, 512, 16)

    def body(x_ref, o_ref):
        acc = x_ref[0].astype(F32)
        for k in range(1, n):
            acc = acc + x_ref[k].astype(F32)
        o_ref[...] = acc

    return pl.pallas_call(
        body, name=name,
        out_shape=jax.ShapeDtypeStruct((r, c), F32),
        grid=(r // tr,),
        in_specs=[pl.BlockSpec((n, tr, c), lambda i: (0, i, 0))],
        out_specs=pl.BlockSpec((tr, c), lambda i: (i, 0)),
        compiler_params=_params("arbitrary"),
    )(x)


MM_VMEM_BUDGET = 36 * 1024 * 1024
GRID_STEP_AS_BYTES = 1 << 20


def _mm_tiles(m, n, a_row_bytes, b_col_bytes, out_bytes):
    tms = [c for c in (2048, 1024, 512, 256, 128, 64, 32, 16, 8) if m % c == 0] or [m]
    tns = [c for c in range(LANES, min(n, 2048) + 1, LANES) if n % c == 0] or [n]
    best = None
    for tm in tms:
        for tn in tns:
            vmem = 2 * (tm * a_row_bytes + tn * b_col_bytes) + 2 * tm * tn * out_bytes + tm * tn * 4
            if vmem > MM_VMEM_BUDGET:
                continue
            steps = (m // tm) * (n // tn)
            cost = steps * GRID_STEP_AS_BYTES + (m // tm) * n * b_col_bytes + m * a_row_bytes
            if best is None or cost < best[0]:
                best = (cost, tm, tn)
    assert best is not None, (m, n, a_row_bytes, b_col_bytes)
    return best[1], best[2]


def mm(pairs, *, trans_b, out_dtype, name, out_slab=False, bias=None):
    a0 = pairs[0][0]
    m = a0.shape[1] if a0.ndim == 3 else a0.shape[0]
    n = pairs[0][1].shape[0] if trans_b else pairs[0][1].shape[1]
    a_row_bytes = sum((b.shape[1] if trans_b else b.shape[0]) * a.dtype.itemsize for a, b in pairs)
    b_col_bytes = sum((b.shape[1] if trans_b else b.shape[0]) * b.dtype.itemsize for _, b in pairs)
    tm, tn = _mm_tiles(m, n, a_row_bytes, b_col_bytes, jnp.dtype(out_dtype).itemsize)
    slabs = [a.ndim == 3 for a, _ in pairs]
    n_pairs = len(pairs)

    def body(*refs):
        o_ref = refs[-1]
        acc = bias_ref = None
        if bias is not None:
            bias_ref = refs[2 * n_pairs]
        for i in range(n_pairs):
            a_ref, b_ref = refs[2 * i], refs[2 * i + 1]
            if slabs[i]:
                a = jnp.concatenate([a_ref[s].astype(BF16) for s in range(a_ref.shape[0])], axis=1)
            else:
                a = a_ref[...].astype(BF16)
            b = b_ref[...].astype(BF16)
            part = _nt(a, b) if trans_b else _nn(a, b)
            acc = part if acc is None else acc + part
        if bias_ref is not None:
            acc = acc + bias_ref[...]
        if out_slab:
            for s in range(tn // LANES):
                o_ref[s] = acc[:, s * LANES:(s + 1) * LANES].astype(out_dtype)
        else:
            o_ref[...] = acc.astype(out_dtype)

    in_specs, args = [], []
    for (a, b), slab in zip(pairs, slabs):
        if slab:
            in_specs.append(pl.BlockSpec((a.shape[0], tm, LANES), lambda i, j: (0, i, 0)))
        else:
            in_specs.append(pl.BlockSpec((tm, a.shape[1]), lambda i, j: (i, 0)))
        if trans_b:
            in_specs.append(pl.BlockSpec((tn, b.shape[1]), lambda i, j: (j, 0)))
        else:
            in_specs.append(pl.BlockSpec((b.shape[0], tn), lambda i, j: (0, j)))
        args += [a, b]
    if bias is not None:
        in_specs.append(pl.BlockSpec((1, tn), lambda i, j: (0, j)))
        args.append(bias)
    if out_slab:
        out_shape = jax.ShapeDtypeStruct((n // LANES, m, LANES), out_dtype)
        out_spec = pl.BlockSpec((tn // LANES, tm, LANES), lambda i, j: (j, i, 0))
    else:
        out_shape = jax.ShapeDtypeStruct((m, n), out_dtype)
        out_spec = pl.BlockSpec((tm, tn), lambda i, j: (i, j))
    return pl.pallas_call(
        body, name=name, out_shape=out_shape, grid=(m // tm, n // tn),
        in_specs=in_specs, out_specs=out_spec,
        compiler_params=_params("arbitrary", "arbitrary"),
    )(*args)


def mm_tn(a, b, *, name, out_dtype=F32, tk_cap=1536, tn_cap=1024, tm_cap=512):
    slab = a.ndim == 3
    m = a.shape[1] if slab else a.shape[0]
    k = a.shape[0] * LANES if slab else a.shape[1]
    n = b.shape[1]
    tk = _tile(k, tk_cap)
    tn = _tile(n, tn_cap)
    tm = _tile(m, tm_cap, 8)
    n_steps = m // tm

    def body(a_ref, b_ref, o_ref, acc_ref):
        step = pl.program_id(2)

        @pl.when(step == 0)
        def _():
            acc_ref[...] = jnp.zeros_like(acc_ref)

        bb = b_ref[...].astype(BF16)
        if slab:
            for s in range(tk // LANES):
                acc_ref[s * LANES:(s + 1) * LANES, :] += _tn(a_ref[s].astype(BF16), bb)
        else:
            acc_ref[...] += _tn(a_ref[...].astype(BF16), bb)

        @pl.when(step == n_steps - 1)
        def _():
            o_ref[...] = acc_ref[...].astype(out_dtype)

    if slab:
        a_spec = pl.BlockSpec((tk // LANES, tm, LANES), lambda i, j, t: (i, t, 0))
    else:
        a_spec = pl.BlockSpec((tm, tk), lambda i, j, t: (t, i))
    return pl.pallas_call(
        body, name=name, out_shape=jax.ShapeDtypeStruct((k, n), out_dtype), grid=(k // tk, n // tn, n_steps),
        in_specs=[a_spec, pl.BlockSpec((tm, tn), lambda i, j, t: (t, j))],
        out_specs=pl.BlockSpec((tk, tn), lambda i, j, t: (i, j)),
        scratch_shapes=[pltpu.VMEM((tk, tn), F32)],
        compiler_params=_params("arbitrary", "arbitrary", "arbitrary"),
    )(a, b)


def _row_spec(d, k):
    return pl.BlockSpec((1, 1, d), lambda b, i: (6 * b + k, 0, 0))


def modulate(x, mod, k_shift, k_scale, bl, name):
    t, d = x.shape
    s = t // bl
    tm = _tile(s, 512, 8)
    nt = s // tm

    def body(x_ref, sh_ref, sc_ref, o_ref):
        o_ref[...] = (x_ref[...] * (1.0 + sc_ref[0]) + sh_ref[0]).astype(BF16)

    return pl.pallas_call(
        body, name=name, out_shape=jax.ShapeDtypeStruct((t, d), BF16), grid=(bl, nt),
        in_specs=[pl.BlockSpec((tm, d), lambda b, i: (b * nt + i, 0)), _row_spec(d, k_shift), _row_spec(d, k_scale)],
        out_specs=pl.BlockSpec((tm, d), lambda b, i: (b * nt + i, 0)),
        compiler_params=_params("arbitrary", "arbitrary"),
    )(x, mod, mod)


def _layer_norm_stats(r):
    mu = jnp.mean(r, axis=-1, keepdims=True)
    rc = r - mu
    var = jnp.mean(rc * rc, axis=-1, keepdims=True)
    rstd = lax.rsqrt(var + NORM_EPS)
    return rc * rstd, rstd


def residual_layer_norm(x, y, mod, k_gate, g, b, bl, name, next_mod=None):
    t, d = x.shape
    s = t // bl
    tm = _tile(s, 512, 8)
    nt = s // tm
    has_next = next_mod is not None

    def body(*refs):
        x_ref, y_ref, gt_ref, g_ref, b_ref = refs[:5]
        rest = refs[5:]
        if has_next:
            sh_ref, sc_ref, o_ref, r_ref, u_ref = rest
        else:
            o_ref, r_ref = rest
        r = ALPHA * x_ref[...] + (1.0 + gt_ref[0]) * y_ref[...]
        xhat, _ = _layer_norm_stats(r)
        out = xhat * g_ref[...] + b_ref[...]
        o_ref[...] = out
        r_ref[...] = r
        if has_next:
            u_ref[...] = (out * (1.0 + sc_ref[0]) + sh_ref[0]).astype(BF16)

    tok = pl.BlockSpec((tm, d), lambda bb, i: (bb * nt + i, 0))
    vec = pl.BlockSpec((1, d), lambda bb, i: (0, 0))
    in_specs = [tok, tok, _row_spec(d, k_gate), vec, vec]
    args = [x, y, mod, g, b]
    out_shape = [jax.ShapeDtypeStruct((t, d), F32), jax.ShapeDtypeStruct((t, d), F32)]
    out_specs = [tok, tok]
    if has_next:
        in_specs += [_row_spec(d, next_mod[0]), _row_spec(d, next_mod[1])]
        args += [mod if len(next_mod) == 2 else next_mod[2]] * 2
        out_shape.append(jax.ShapeDtypeStruct((t, d), BF16))
        out_specs.append(tok)
    return pl.pallas_call(
        body, name=name, out_shape=out_shape, grid=(bl, nt), in_specs=in_specs, out_specs=out_specs,
        compiler_params=_params("arbitrary", "arbitrary"),
    )(*args)


def loss_head(xo, target, name):
    t, d = xo.shape
    tm = _tile(t, 512, 8)

    def body(x_ref, t_ref, l_ref, dx_ref):
        @pl.when(pl.program_id(0) == 0)
        def _():
            l_ref[...] = jnp.zeros_like(l_ref)

        e = x_ref[...] - t_ref[...]
        l_ref[...] += jnp.sum(e * e, axis=0, keepdims=True) * (0.5 / d)
        dx_ref[...] = e * (1.0 / d)

    tok = pl.BlockSpec((tm, d), lambda i: (i, 0))
    return pl.pallas_call(
        body, name=name,
        out_shape=[jax.ShapeDtypeStruct((1, d), F32), jax.ShapeDtypeStruct((t, d), F32)],
        grid=(t // tm,), in_specs=[tok, tok],
        out_specs=[pl.BlockSpec((1, d), lambda i: (0, 0)), tok],
        compiler_params=_params("arbitrary"),
    )(xo, target)


def sublayer_backward(d_a, bl, name, *, du=None, scale=None, x_in=None, ln=None):
    t, d = d_a.shape
    s = t // bl
    tm = _tile(s, 512, 8)
    nt = s // tm
    has_mod = du is not None
    has_ln = ln is not None
    assert has_mod or has_ln
    assert has_ln or x_in is not None

    def body(*refs):
        refs = list(refs)
        da_ref = refs.pop(0)
        if has_mod:
            du_ref, sc_ref = refs.pop(0), refs.pop(0)
        if has_ln:
            r_ref, y_ref, g_ref, b_ref, gt_ref = (refs.pop(0) for _ in range(5))
        elif has_mod:
            xin_ref = refs.pop(0)
        dx_ref = refs.pop(0)
        if has_ln:
            dy_ref, dg_ref, db_ref, dgt_ref = (refs.pop(0) for _ in range(4))
        if has_mod:
            dsc_ref, dsh_ref = refs.pop(0), refs.pop(0)
        first_tile = pl.program_id(1) == 0
        first_step = jnp.logical_and(pl.program_id(0) == 0, first_tile)

        dout = da_ref[...]
        if has_ln:
            xhat, rstd = _layer_norm_stats(r_ref[...])
        if has_mod:
            duv = du_ref[...]
            dout = dout + duv * (1.0 + sc_ref[0])
            xin = xhat * g_ref[...] + b_ref[...] if has_ln else xin_ref[...]

            @pl.when(first_tile)
            def _():
                dsc_ref[...] = jnp.zeros_like(dsc_ref)
                dsh_ref[...] = jnp.zeros_like(dsh_ref)

            dsc_ref[0] += jnp.sum(duv * xin, axis=0, keepdims=True)
            dsh_ref[0] += jnp.sum(duv, axis=0, keepdims=True)
        if not has_ln:
            dx_ref[...] = dout
            return

        @pl.when(first_step)
        def _():
            dg_ref[...] = jnp.zeros_like(dg_ref)
            db_ref[...] = jnp.zeros_like(db_ref)

        @pl.when(first_tile)
        def _():
            dgt_ref[...] = jnp.zeros_like(dgt_ref)

        dg_ref[...] += jnp.sum(dout * xhat, axis=0, keepdims=True)
        db_ref[...] += jnp.sum(dout, axis=0, keepdims=True)
        dxh = dout * g_ref[...]
        dr = rstd * (dxh - jnp.mean(dxh, axis=-1, keepdims=True) - xhat * jnp.mean(dxh * xhat, axis=-1, keepdims=True))
        dx_ref[...] = ALPHA * dr
        dy_ref[...] = ((1.0 + gt_ref[0]) * dr).astype(BF16)
        dgt_ref[0] += jnp.sum(dr * y_ref[...], axis=0, keepdims=True)

    tok = pl.BlockSpec((tm, d), lambda bb, i: (bb * nt + i, 0))
    vec = pl.BlockSpec((1, d), lambda bb, i: (0, 0))
    seq = pl.BlockSpec((1, 1, d), lambda bb, i: (bb, 0, 0))
    in_specs, args = [tok], [d_a]
    if has_mod:
        in_specs += [tok, _row_spec(d, scale[1])]
        args += [du, scale[0]]
    if has_ln:
        r, y, g, b, gate = ln
        in_specs += [tok, tok, vec, vec, _row_spec(d, gate[1])]
        args += [r, y, g, b, gate[0]]
    elif has_mod:
        in_specs.append(tok)
        args.append(x_in)
    names = ["dx"]
    out_shape, out_specs = [jax.ShapeDtypeStruct((t, d), F32)], [tok]
    if has_ln:
        names += ["dy", "dg", "db", "dgate"]
        out_shape += [jax.ShapeDtypeStruct((t, d), BF16), jax.ShapeDtypeStruct((1, d), F32),
                      jax.ShapeDtypeStruct((1, d), F32), jax.ShapeDtypeStruct((bl, 1, d), F32)]
        out_specs += [tok, vec, vec, seq]
    if has_mod:
        names += ["dscale", "dshift"]
        out_shape += [jax.ShapeDtypeStruct((bl, 1, d), F32)] * 2
        out_specs += [seq, seq]
    outs = pl.pallas_call(
        body, name=name, out_shape=out_shape, grid=(bl, nt), in_specs=in_specs, out_specs=out_specs,
        compiler_params=_params("arbitrary", "arbitrary"),
    )(*args)
    return dict(zip(names, outs))


def _silu(a):
    return a * jax.nn.sigmoid(a)


def silu_rows(a, name):
    def body(a_ref, o_ref):
        o_ref[...] = _silu(a_ref[...]).astype(BF16)

    return pl.pallas_call(body, name=name, out_shape=jax.ShapeDtypeStruct(a.shape, BF16))(a)


def _swiglu_tiles(t, f):
    return _tile(t, 512, 8), _tile(f, 1536)


def swiglu_in(u, wt_gate, wt_up, name):
    t, d = u.shape
    f = wt_gate.shape[0]
    tm, tf = _swiglu_tiles(t, f)

    def body(u_ref, g_ref, w_ref, a_ref, b_ref, h_ref):
        uv = u_ref[...]
        a = _nt(uv, g_ref[...])
        b = _nt(uv, w_ref[...])
        a_ref[...] = a.astype(BF16)
        b_ref[...] = b.astype(BF16)
        h_ref[...] = (_silu(a) * b).astype(BF16)

    w_spec = pl.BlockSpec((tf, d), lambda i, j: (j, 0))
    o_spec = pl.BlockSpec((tm, tf), lambda i, j: (i, j))
    return pl.pallas_call(
        body, name=name,
        out_shape=[jax.ShapeDtypeStruct((t, f), BF16)] * 3,
        grid=(t // tm, f // tf), in_specs=[pl.BlockSpec((tm, d), lambda i, j: (i, 0)), w_spec, w_spec],
        out_specs=[o_spec, o_spec, o_spec], compiler_params=_params("arbitrary", "arbitrary"),
    )(u, wt_gate, wt_up)


def swiglu_out_backward(dy, w_down, a, b, name):
    t, d = dy.shape
    f = w_down.shape[0]
    tm, tf = _swiglu_tiles(t, f)

    def body(dy_ref, w_ref, a_ref, b_ref, da_ref, db_ref):
        dh = _nt(dy_ref[...], w_ref[...])
        av = a_ref[...].astype(F32)
        sig = jax.nn.sigmoid(av)
        da_ref[...] = (dh * b_ref[...].astype(F32) * (sig * (1.0 + av * (1.0 - sig)))).astype(BF16)
        db_ref[...] = (dh * (av * sig)).astype(BF16)

    spec = pl.BlockSpec((tm, tf), lambda i, j: (i, j))
    return pl.pallas_call(
        body, name=name, out_shape=[jax.ShapeDtypeStruct((t, f), BF16)] * 2, grid=(t // tm, f // tf),
        in_specs=[pl.BlockSpec((tm, d), lambda i, j: (i, 0)), pl.BlockSpec((tf, d), lambda i, j: (j, 0)), spec, spec],
        out_specs=[spec, spec], compiler_params=_params("arbitrary", "arbitrary"),
    )(dy, w_down, a, b)


def rope_tables(pos, inv_freq, sign, name):
    t = pos.shape[0]
    tm = _tile(t, 512, 8)

    def body(p_ref, f_ref, s_ref, c_out, s_out):
        ang = p_ref[...] * f_ref[...]
        c_out[...] = jnp.cos(ang)
        s_out[...] = jnp.sin(ang) * s_ref[...]

    vec = pl.BlockSpec((1, LANES), lambda i: (0, 0))
    tab = pl.BlockSpec((tm, LANES), lambda i: (i, 0))
    return pl.pallas_call(
        body, name=name, out_shape=[jax.ShapeDtypeStruct((t, LANES), F32)] * 2, grid=(t // tm,),
        in_specs=[pl.BlockSpec((tm, 1), lambda i: (i, 0)), vec, vec], out_specs=[tab, tab],
        compiler_params=_params("arbitrary"),
    )(pos, inv_freq, sign)


def _rot_half(v):
    lane = lax.broadcasted_iota(jnp.int32, v.shape, v.ndim - 1)
    up = pltpu.roll(v, LANES - MLA_ROPE // 2, v.ndim - 1)
    down = pltpu.roll(v, MLA_ROPE // 2, v.ndim - 1)
    return jnp.where(lane % MLA_ROPE < MLA_ROPE // 2, up, down)


def _rope(v, cos, sin_signed):
    return v * cos + _rot_half(v) * sin_signed


def _rope_transposed(dv, cos, sin_signed):
    return dv * cos + _rot_half(dv * sin_signed)


def rope_slabs(v, cos, sin_signed, out_dtype, name, transposed=False):
    ns, t, _ = v.shape
    tm = _tile(t, 1024, 8)
    fn = _rope_transposed if transposed else _rope

    def body(v_ref, c_ref, s_ref, o_ref):
        for j in range(ns):
            o_ref[j] = fn(v_ref[j].astype(F32), c_ref[...], s_ref[...]).astype(out_dtype)

    tab = pl.BlockSpec((tm, LANES), lambda i: (i, 0))
    spec = pl.BlockSpec((ns, tm, LANES), lambda i: (0, i, 0))
    return pl.pallas_call(
        body, name=name, out_shape=jax.ShapeDtypeStruct(v.shape, out_dtype), grid=(t // tm,),
        in_specs=[spec, tab, tab], out_specs=spec, compiler_params=_params("arbitrary"),
    )(v, cos, sin_signed)


def _rms(x):
    rinv = lax.rsqrt(jnp.mean(x * x, axis=-1, keepdims=True) + NORM_EPS)
    return x * rinv, rinv


def mla_latents_forward(h_in, g_q, g_kv, cos, sin_signed, name):
    t = h_in.shape[0]
    tm = _tile(t, 512, 8)

    def body(h_ref, gq_ref, gkv_ref, c_ref, s_ref, cq_ref, ckv_ref, kr_ref):
        cq_ref[...] = (_rms(h_ref[:, 0:MLA_QR])[0] * gq_ref[...]).astype(BF16)
        ckv_ref[...] = (_rms(h_ref[:, MLA_QR:MLA_QR + MLA_KVR])[0] * gkv_ref[...]).astype(BF16)
        kr_ref[...] = _rope(h_ref[:, MLA_QR + MLA_KVR:], c_ref[...], s_ref[...]).astype(BF16)

    def tok(w):
        return pl.BlockSpec((tm, w), lambda i: (i, 0))

    def vec(w):
        return pl.BlockSpec((1, w), lambda i: (0, 0))

    return pl.pallas_call(
        body, name=name,
        out_shape=[jax.ShapeDtypeStruct((t, MLA_QR), BF16), jax.ShapeDtypeStruct((t, MLA_KVR), BF16),
                   jax.ShapeDtypeStruct((t, LANES), BF16)],
        grid=(t // tm,),
        in_specs=[tok(h_in.shape[1]), vec(MLA_QR), vec(MLA_KVR), tok(LANES), tok(LANES)],
        out_specs=[tok(MLA_QR), tok(MLA_KVR), tok(LANES)],
        compiler_params=_params("arbitrary"),
    )(h_in, g_q, g_kv, cos, sin_signed)


def mla_latents_backward(h_in, dcq, dckv, dkr, g_q, g_kv, cos, sin_signed, name):
    t, w = h_in.shape
    tm = _tile(t, 512, 8)

    def body(h_ref, dcq_ref, dckv_ref, dkr_ref, gq_ref, gkv_ref, c_ref, s_ref, dh_ref, dgq_ref, dgkv_ref):
        @pl.when(pl.program_id(0) == 0)
        def _():
            dgq_ref[...] = jnp.zeros_like(dgq_ref)
            dgkv_ref[...] = jnp.zeros_like(dgkv_ref)

        def rms_bwd(x, dc, g_ref, dg_ref):
            xn, rinv = _rms(x)
            dg_ref[...] += jnp.sum(dc * xn, axis=0, keepdims=True)
            dxn = dc * g_ref[...]
            return rinv * (dxn - xn * jnp.mean(dxn * xn, axis=-1, keepdims=True))

        dq = rms_bwd(h_ref[:, 0:MLA_QR], dcq_ref[...], gq_ref, dgq_ref)
        dkv = rms_bwd(h_ref[:, MLA_QR:MLA_QR + MLA_KVR], dckv_ref[...], gkv_ref, dgkv_ref)
        dr = _rope_transposed(dkr_ref[...], c_ref[...], s_ref[...])
        dh_ref[...] = jnp.concatenate([dq, dkv, dr], axis=1).astype(BF16)

    def tok(ww):
        return pl.BlockSpec((tm, ww), lambda i: (i, 0))

    def vec(ww):
        return pl.BlockSpec((1, ww), lambda i: (0, 0))

    return pl.pallas_call(
        body, name=name,
        out_shape=[jax.ShapeDtypeStruct((t, w), BF16), jax.ShapeDtypeStruct((1, MLA_QR), F32),
                   jax.ShapeDtypeStruct((1, MLA_KVR), F32)],
        grid=(t // tm,),
        in_specs=[tok(w), tok(MLA_QR), tok(MLA_KVR), tok(LANES), vec(MLA_QR), vec(MLA_KVR), tok(LANES), tok(LANES)],
        out_specs=[tok(w), vec(MLA_QR), vec(MLA_KVR)],
        compiler_params=_params("arbitrary"),
    )(h_in, dcq, dckv, dkr, g_q, g_kv, cos, sin_signed)


def _tri(n, lower):
    r = lax.broadcasted_iota(jnp.int32, (n, n), 0)
    c = lax.broadcasted_iota(jnp.int32, (n, n), 1)
    return jnp.where(r >= c if lower else r <= c, 1.0, 0.0).astype(F32)


def _dot_exact(tri, v):
    hi = v.astype(BF16)
    mid = (v - hi.astype(F32)).astype(BF16)
    lo = (v - hi.astype(F32) - mid.astype(F32)).astype(BF16)
    t = tri.astype(BF16)
    return _nn(t, hi) + _nn(t, mid) + _nn(t, lo)


def fox_gate_forward(z, b_f, bl, name):
    t = z.shape[0]
    s = t // bl
    ch = LANES
    n_ch = s // ch

    def body(z_ref, b_ref, f_ref, fs_ref):
        tri = _tri(ch, True)
        carry = jnp.zeros((1, LANES), F32)
        for k in range(n_ch):
            x = z_ref[k * ch:(k + 1) * ch, :] + b_ref[...]
            logf = jnp.minimum(x, 0.0) - jnp.log(1.0 + jnp.exp(-jnp.abs(x)))
            cs = _dot_exact(tri, logf) + carry
            carry = cs[ch - 1:ch, :]
            f_ref[k * ch:(k + 1) * ch, :] = cs
            for h in range(FOX_HEADS):
                fs_ref[h, k * ch:(k + 1) * ch, :] = jnp.broadcast_to(cs[:, h:h + 1], (ch, LANES))

    return pl.pallas_call(
        body, name=name,
        out_shape=[jax.ShapeDtypeStruct((t, LANES), F32), jax.ShapeDtypeStruct((FOX_HEADS, t, LANES), F32)],
        grid=(bl,),
        in_specs=[pl.BlockSpec((s, LANES), lambda b: (b, 0)), pl.BlockSpec((1, LANES), lambda b: (0, 0))],
        out_specs=[pl.BlockSpec((s, LANES), lambda b: (b, 0)),
                   pl.BlockSpec((FOX_HEADS, s, LANES), lambda b: (0, b, 0))],
        compiler_params=_params("arbitrary"),
    )(z, b_f)


def fox_gate_backward(z, b_f, df, bl, name):
    t = z.shape[0]
    s = t // bl
    ch = LANES
    n_ch = s // ch

    def body(z_ref, b_ref, df_ref, dz_ref, db_ref):
        @pl.when(pl.program_id(0) == 0)
        def _():
            db_ref[...] = jnp.zeros_like(db_ref)

        tri = _tri(ch, False)
        carry = jnp.zeros((1, LANES), F32)
        for k in reversed(range(n_ch)):
            cs = _dot_exact(tri, df_ref[k * ch:(k + 1) * ch, :]) + carry
            carry = cs[0:1, :]
            x = z_ref[k * ch:(k + 1) * ch, :] + b_ref[...]
            dz = cs * (1.0 - jax.nn.sigmoid(x))
            dz_ref[k * ch:(k + 1) * ch, :] = dz
            db_ref[...] += jnp.sum(dz, axis=0, keepdims=True)

    tok = pl.BlockSpec((s, LANES), lambda b: (b, 0))
    vec = pl.BlockSpec((1, LANES), lambda b: (0, 0))
    return pl.pallas_call(
        body, name=name,
        out_shape=[jax.ShapeDtypeStruct((t, LANES), F32), jax.ShapeDtypeStruct((1, LANES), F32)],
        grid=(bl,), in_specs=[tok, vec, tok], out_specs=[tok, vec],
        compiler_params=_params("arbitrary"),
    )(z, b_f, df)


NEG_INF = float("-inf")


def _attn_tiles(s):
    return _tile(s, 512, 8)


def attention_forward(kind, ops, bl, scale, name):
    fox = kind == "fox"
    if fox:
        assert math.frexp(scale)[0] == 0.5, "the FoX scale is folded into bf16 queries: it must be a power of two"
        qkv, fq, fk = ops
        t = qkv.shape[1]
        n_pair = FOX_HEADS // 2
    else:
        qn, qr, kn, kr, v = ops
        t = qn.shape[1]
        n_pair = MLA_HEADS // 2
    s = t // bl
    tq = _attn_tiles(s)
    nq = s // tq
    half = LANES // 2

    def body(*refs):
        if fox:
            q_ref, k_ref, v_ref, fq_ref, fk_ref, o_ref, lse_ref, o32_ref = refs
        else:
            qn_ref, qr_ref, kn_ref, kr_ref, v_ref, o_ref, lse_ref = refs
        i = pl.program_id(2)
        row = lax.broadcasted_iota(jnp.int32, (tq, tq), 0)
        col = lax.broadcasted_iota(jnp.int32, (tq, tq), 1)
        heads = []
        for e in range(2):
            sl = slice(e * half, (e + 1) * half)
            if fox:
                heads.append((sl, q_ref[0, :, sl] * jnp.asarray(scale, BF16), None))
            else:
                heads.append((sl, qn_ref[e], qr_ref[0, :, sl]))
        dv = half if fox else LANES

        def wide(stat):
            return jnp.concatenate([stat] * (tq // LANES), axis=1)

        def step(j, carry, masked):
            rows = pl.ds(pl.multiple_of(j * tq, tq), tq)
            new = []
            for e, (sl, qa, qb) in enumerate(heads):
                m, l, acc = carry[e]
                if fox:
                    sc = _nt(qa, k_ref[0, rows, sl]) + wide(fq_ref[e]) - fk_ref[0, j, e:e + 1, :]
                    vv = v_ref[0, rows, sl]
                else:
                    sc = (_nt(qa, kn_ref[e, rows, :]) + _nt(qb, kr_ref[rows, 0:half])) * scale
                    vv = v_ref[e, rows, :]
                if masked:
                    sc = jnp.where(row >= col, sc, NEG_INF)
                m_new = jnp.maximum(m, jnp.max(sc, axis=1, keepdims=True))
                p = jnp.exp(sc - m_new)
                a = jnp.exp(m - m_new)
                p_hi = p.astype(BF16)
                if fox:
                    vv = jnp.concatenate([vv, ones], axis=1)
                    acc = a * acc + _nn(p_hi, vv) + _nn((p - p_hi.astype(F32)).astype(BF16), vv)
                else:
                    l = a * l + jnp.sum(p, axis=1, keepdims=True)
                    acc = a * acc + _nn(p_hi, vv)
                new.append((m_new, l, acc))
            return tuple(new)

        ones = jnp.ones((tq, half), BF16)
        acc_w = LANES if fox else dv
        init = (jnp.full((tq, 1), NEG_INF, F32), jnp.zeros((tq, 1), F32), jnp.zeros((tq, acc_w), F32))
        carry = step(i, (init, init), True)
        carry = lax.fori_loop(0, i, lambda j, c: step(j, c, False), carry)
        if fox:
            carry = [(m, acc[:, dv:dv + 1], acc[:, :dv]) for m, _, acc in carry]
        outs = [acc / l for _, l, acc in carry]
        for e, (m, l, _) in enumerate(carry):
            lse_ref[e] = jnp.broadcast_to(m + jnp.log(l), (tq, LANES))
        if fox:
            o32 = jnp.concatenate(outs, axis=1)
            o32_ref[0] = o32
            o_ref[0] = o32.astype(BF16)
        else:
            o_ref[0] = outs[0].astype(BF16)
            o_ref[1] = outs[1].astype(BF16)

    def q_idx(b, g, i):
        return (g, b * nq + i, 0)

    if fox:
        nk = fk.shape[1]
        in_specs = [pl.BlockSpec((1, tq, LANES), q_idx),
                    pl.BlockSpec((1, s, LANES), lambda b, g, i: (n_pair + g, b, 0)),
                    pl.BlockSpec((1, s, LANES), lambda b, g, i: (2 * n_pair + g, b, 0)),
                    pl.BlockSpec((2, tq, LANES), q_idx),
                    pl.BlockSpec((1, nk, 8, tq), lambda b, g, i: (b * n_pair + g, 0, 0, 0))]
        args = [qkv, qkv, qkv, fq, fk]
        o_spec = pl.BlockSpec((1, tq, LANES), q_idx)
    else:
        in_specs = [pl.BlockSpec((2, tq, LANES), q_idx),
                    pl.BlockSpec((1, tq, LANES), q_idx),
                    pl.BlockSpec((2, s, LANES), lambda b, g, i: (g, b, 0)),
                    pl.BlockSpec((s, LANES), lambda b, g, i: (b, 0)),
                    pl.BlockSpec((2, s, LANES), lambda b, g, i: (g, b, 0))]
        args = [qn, qr, kn, kr, v]
        o_spec = pl.BlockSpec((2, tq, LANES), q_idx)
    out_shape = [jax.ShapeDtypeStruct((8, t, LANES), BF16), jax.ShapeDtypeStruct((2 * n_pair, t, LANES), F32)]
    out_specs = [o_spec, pl.BlockSpec((2, tq, LANES), q_idx)]
    if fox:
        out_shape.append(jax.ShapeDtypeStruct((8, t, LANES), F32))
        out_specs.append(o_spec)
    outs = pl.pallas_call(
        body, name=name, out_shape=out_shape, grid=(bl, n_pair, nq), in_specs=in_specs, out_specs=out_specs,
        compiler_params=_params("arbitrary", "arbitrary", "arbitrary"),
    )(*args)
    return (outs[0], outs[1], outs[2] if fox else outs[0])


def attention_backward(kind, ops, o, do, lse, bl, scale, name):
    fox = kind == "fox"
    if fox:
        qkv, fq, fk = ops
        t = qkv.shape[1]
        n_pair = FOX_HEADS // 2
    else:
        qn, qr, kn, kr, v = ops
        t = qn.shape[1]
        n_pair = MLA_HEADS // 2
    s = t // bl
    tq = _attn_tiles(s)
    nq = s // tq
    half = LANES // 2

    def body(*refs):
        if fox:
            (q_ref, k_ref, v_ref, fq_ref, fk_ref, o_ref, do_ref, lse_ref,
             dq_ref, dk_ref, dv_ref, dfk_ref, delta_scr, qt_scr, dot_scr) = refs
        else:
            (qn_ref, qr_ref, kn_ref, kr_ref, v_ref, o_ref, do_ref, lse_ref,
             dqn_ref, dqr_ref, dkn_ref, dv_ref, dkr_ref, delta_scr, qt_scr, qrt_scr, dot_scr) = refs
        g, j = pl.program_id(1), pl.program_id(2)
        row = lax.broadcasted_iota(jnp.int32, (tq, tq), 0)
        col = lax.broadcasted_iota(jnp.int32, (tq, tq), 1)
        krows = pl.ds(pl.multiple_of(j * tq, tq), tq)

        def transposed(v):
            return v.astype(F32).T.astype(BF16)

        def wide(stat):
            return jnp.concatenate([stat] * (tq // LANES), axis=1)

        @pl.when(j == 0)
        def _():
            if fox:
                dq_ref[...] = jnp.zeros_like(dq_ref)
            else:
                dqn_ref[...] = jnp.zeros_like(dqn_ref)
                dqr_ref[...] = jnp.zeros_like(dqr_ref)
            for ii in range(nq):
                rws = slice(ii * tq, (ii + 1) * tq)
                deltas = []
                if fox:
                    prod = do_ref[0, rws, :].astype(F32) * o_ref[0, rws, :].astype(F32)
                    for e in range(2):
                        deltas.append(jnp.sum(prod[:, e * half:(e + 1) * half], axis=1, keepdims=True))
                    qt_scr[ii] = transposed(q_ref[0, rws, :])
                    dot_scr[ii] = transposed(do_ref[0, rws, :])
                else:
                    for e in range(2):
                        prod = do_ref[e, rws, :].astype(F32) * o_ref[e, rws, :].astype(F32)
                        deltas.append(jnp.sum(prod, axis=1, keepdims=True))
                        qt_scr[e, ii] = transposed(qn_ref[e, rws, :])
                        dot_scr[e, ii] = transposed(do_ref[e, rws, :])
                    qrt_scr[ii] = transposed(qr_ref[0, rws, :])
                for e in range(2):
                    delta_scr[e, rws, :] = jnp.broadcast_to(deltas[e], (tq, LANES))

        if fox:
            dfk_ref[...] = jnp.zeros_like(dfk_ref)
        else:
            @pl.when(jnp.logical_and(g == 0, j == 0))
            def _():
                dkr_ref[...] = jnp.zeros_like(dkr_ref)

        heads = []
        for e in range(2):
            sl = slice(e * half, (e + 1) * half)
            if fox:
                heads.append((sl, k_ref[0, :, sl], v_ref[0, :, sl], fk_ref[0, 0, e:e + 1, :]))
            else:
                heads.append((sl, kn_ref[e], v_ref[e], kr_ref[krows, 0:half]))
        dk_w = dv_w = half if fox else LANES

        def step(i, carry, masked):
            rows = pl.ds(pl.multiple_of(i * tq, tq), tq)
            new = []
            for e, (sl, k_e, v_e, x_e) in enumerate(heads):
                dk_acc, dv_acc, last = carry[e]
                if fox:
                    do_i = do_ref[0, rows, sl]
                    sc = _nt(q_ref[0, rows, sl], k_e) * scale + wide(fq_ref[e, rows, :]) - x_e
                else:
                    do_i = do_ref[e, rows, :]
                    sc = (_nt(qn_ref[e, rows, :], k_e) + _nt(qr_ref[0, rows, sl], x_e)) * scale
                if masked:
                    sc = jnp.where(row >= col, sc, NEG_INF)
                p = jnp.exp(sc - wide(lse_ref[e, rows, :]))
                dp = _nt(do_i, v_e)
                ds = p * (dp - wide(delta_scr[e, rows, :]))
                dsb = (ds * scale).astype(BF16)
                if fox:
                    fsl = slice(e * half, (e + 1) * half)
                    dv_acc = dv_acc + _nn(dot_scr[i, fsl, :], p.astype(BF16))
                    dk_acc = dk_acc + _nn(qt_scr[i, fsl, :], dsb)
                    dq_ref[0, rows, sl] += _nn(dsb, k_e)
                    last = last - jnp.sum(ds, axis=0, keepdims=True)
                else:
                    dv_acc = dv_acc + _nn(dot_scr[e, i], p.astype(BF16))
                    dk_acc = dk_acc + _nn(qt_scr[e, i], dsb)
                    dqn_ref[e, rows, :] += _nn(dsb, k_e)
                    dqr_ref[0, rows, sl] += _nn(dsb, x_e)
                    last = last + _nn(qrt_scr[i, e * half:(e + 1) * half, :], dsb)
                new.append((dk_acc, dv_acc, last))
            return tuple(new)

        last0 = jnp.zeros((1, tq), F32) if fox else jnp.zeros((half, tq), F32)
        init = (jnp.zeros((dk_w, tq), F32), jnp.zeros((dv_w, tq), F32), last0)
        carry = step(j, (init, init), True)
        carry = lax.fori_loop(j + 1, nq, lambda i, c: step(i, c, False), carry)
        if fox:
            for e in range(2):
                dfk_ref[0, 0, e:e + 1, :] = carry[e][2]
            dk_ref[0] = jnp.concatenate([carry[0][0], carry[1][0]], axis=0).T.astype(BF16)
            dv_ref[0] = jnp.concatenate([carry[0][1], carry[1][1]], axis=0).T.astype(BF16)
        else:
            for e in range(2):
                dkn_ref[e] = carry[e][0].T.astype(BF16)
                dv_ref[e] = carry[e][1].T.astype(BF16)
            dkr_t = carry[0][2] + carry[1][2]
            dkr_ref[krows, :] += jnp.concatenate([dkr_t, jnp.zeros_like(dkr_t)], axis=0).T

    def whole(b, g, j):
        return (g, b, 0)

    def kblk(b, g, j):
        return (g, b * nq + j, 0)

    if fox:
        in_specs = [pl.BlockSpec((1, s, LANES), whole),
                    pl.BlockSpec((1, tq, LANES), lambda b, g, j: (n_pair + g, b * nq + j, 0)),
                    pl.BlockSpec((1, tq, LANES), lambda b, g, j: (2 * n_pair + g, b * nq + j, 0)),
                    pl.BlockSpec((2, s, LANES), whole),
                    pl.BlockSpec((1, 1, 8, tq), lambda b, g, j: (b * n_pair + g, j, 0, 0)),
                    pl.BlockSpec((1, s, LANES), whole), pl.BlockSpec((1, s, LANES), whole),
                    pl.BlockSpec((2, s, LANES), whole)]
        args = [qkv, qkv, qkv, fq, fk, o, do, lse]
        out_shape = [jax.ShapeDtypeStruct((8, t, LANES), F32), jax.ShapeDtypeStruct((8, t, LANES), BF16),
                     jax.ShapeDtypeStruct((8, t, LANES), BF16), jax.ShapeDtypeStruct(fk.shape, F32)]
        out_specs = [pl.BlockSpec((1, s, LANES), whole), pl.BlockSpec((1, tq, LANES), kblk),
                     pl.BlockSpec((1, tq, LANES), kblk),
                     pl.BlockSpec((1, 1, 8, tq), lambda b, g, j: (b * n_pair + g, j, 0, 0))]
    else:
        pair = pl.BlockSpec((2, s, LANES), whole)
        pair_k = pl.BlockSpec((2, tq, LANES), kblk)
        in_specs = [pair, pl.BlockSpec((1, s, LANES), whole), pair_k,
                    pl.BlockSpec((s, LANES), lambda b, g, j: (b, 0)), pair_k,
                    pair, pair, pair]
        args = [qn, qr, kn, kr, v, o, do, lse]
        out_shape = [jax.ShapeDtypeStruct((8, t, LANES), F32), jax.ShapeDtypeStruct((4, t, LANES), F32),
                     jax.ShapeDtypeStruct((8, t, LANES), BF16), jax.ShapeDtypeStruct((8, t, LANES), BF16),
                     jax.ShapeDtypeStruct((t, LANES), F32)]
        out_specs = [pair, pl.BlockSpec((1, s, LANES), whole), pair_k, pair_k,
                     pl.BlockSpec((s, LANES), lambda b, g, j: (b, 0))]
    t_blocks = pltpu.VMEM((nq, LANES, tq), BF16)
    t_pairs = pltpu.VMEM((2, nq, LANES, tq), BF16)
    scratch = [pltpu.VMEM((2, s, LANES), F32)] + ([t_blocks, t_blocks] if fox else [t_pairs, t_blocks, t_pairs])
    return pl.pallas_call(
        body, name=name, out_shape=out_shape, grid=(bl, n_pair, nq), in_specs=in_specs, out_specs=out_specs,
        scratch_shapes=scratch, compiler_params=_params("arbitrary", "arbitrary", "arbitrary"),
    )(*args)


def adamw(w, g, m, v, name):
    shape = w.shape
    c = shape[-1]
    r = w.size // c
    tr = _tile(r, 512, 8)

    def body(w_ref, g_ref, m_ref, v_ref, d_ref, nm_ref, nv_ref):
        gv = g_ref[...]
        m2 = ADAM_B1 * m_ref[...] + (1.0 - ADAM_B1) * gv
        v2 = ADAM_B2 * v_ref[...] + (1.0 - ADAM_B2) * (gv * gv)
        m_hat = m2 / (1.0 - ADAM_B1 ** ADAM_STEP)
        v_hat = v2 / (1.0 - ADAM_B2 ** ADAM_STEP)
        d_ref[...] = -ADAM_LR * (m_hat / (jnp.sqrt(v_hat) + ADAM_EPS) + ADAM_WD * w_ref[...])
        nm_ref[...] = m2
        nv_ref[...] = v2

    spec = pl.BlockSpec((tr, c), lambda i: (i, 0))
    outs = pl.pallas_call(
        body, name=name, out_shape=[jax.ShapeDtypeStruct((r, c), F32)] * 3, grid=(r // tr,),
        in_specs=[spec] * 4, out_specs=[spec] * 3, compiler_params=_params("arbitrary"),
    )(*(a.reshape(r, c) for a in (w, g, m, v)))
    return tuple(a.reshape(shape) for a in outs)


PACK_COLS = 1024


def _pack_rows(a):
    return a.reshape(-1, PACK_COLS)


def kernel(x, c, positions, mla_w_in, mla_g_q, mla_w_uq, mla_g_kv, mla_w_uk, mla_w_uv, mla_w_o, fox_w_in, fox_b_f, fox_w_o, ada_w, ada_b, ffn_w_gate, ffn_w_up, ffn_w_down, ln_g, ln_b, loss_target, m_mla_w_in, m_mla_g_q, m_mla_w_uq, m_mla_g_kv, m_mla_w_uk, m_mla_w_uv, m_mla_w_o, m_fox_w_in, m_fox_b_f, m_fox_w_o, m_ada_w, m_ada_b, m_ffn_w_gate, m_ffn_w_up, m_ffn_w_down, m_ln_g, m_ln_b, v_mla_w_in, v_mla_g_q, v_mla_w_uq, v_mla_g_kv, v_mla_w_uk, v_mla_w_uv, v_mla_w_o, v_fox_w_in, v_fox_b_f, v_fox_w_o, v_ada_w, v_ada_b, v_ffn_w_gate, v_ffn_w_up, v_ffn_w_down, v_ln_g, v_ln_b):
    bl, s, d = x.shape
    t = bl * s
    ff = ffn_w_gate.shape[-1] * N_DEV
    dev = 4 * lax.axis_index("x") + 2 * lax.axis_index("y") + lax.axis_index("c")
    ada_cols = ada_w.shape[-1]
    fox_in = fox_w_in.shape[-1] * N_DEV
    mla_in = mla_w_in.shape[-1]
    mla_in_pad = mla_in + (-mla_in) % LANES

    def t_last(a):
        return jnp.swapaxes(a, -1, -2)

    local = {
        "mla_w_in": mla_w_in[0],
        "mla_w_uq": t_last(mla_w_uq[0]),
        "mla_w_uk": t_last(mla_w_uk[0]),
        "mla_w_uv": t_last(mla_w_uv[0]),
        "mla_w_o": mla_w_o[0],
        "fox_w_in": t_last(fox_w_in[0]),
        "fox_w_o": fox_w_o[0],
    }
    for i in range(DEPTH):
        local.update({f"gate{i}": t_last(ffn_w_gate[i]), f"up{i}": t_last(ffn_w_up[i]), f"down{i}": ffn_w_down[i]})
    groups = [["mla_w_in", "mla_w_uq", "mla_w_uk", "mla_w_uv", "mla_w_o"],
              ["gate0", "up0", "down0"],
              ["fox_w_in", "fox_w_o"],
              ["gate1", "up1", "down1"]]
    offsets, rows_of, slot_of, group_of = {}, {}, {}, {}
    group_rows = []
    for gi, names in enumerate(groups):
        rows = 0
        for nm in names:
            rows_of[nm] = local[nm].size // PACK_COLS
            slot_of[nm] = rows_of[nm] + (-rows_of[nm]) % 16
            offsets[nm] = rows
            group_of[nm] = gi
            rows += slot_of[nm]
        group_rows.append(rows)

    def slot(nm, rows):
        pad = [(0, 0)] * rows.ndim
        pad[-2] = (0, slot_of[nm] - rows_of[nm])
        return jnp.pad(rows, pad)

    def held_until(block, arrays):
        zero = sum((a.reshape(-1)[0] * 0).astype(F32) for a in jax.tree.leaves(arrays))
        return block + zero.astype(block.dtype)

    def landing(block):
        land = lax.empty((N_DEV,) + block.shape, block.dtype)
        return lax.dynamic_update_slice(land, block[None], (dev, 0, 0))

    packed0 = jnp.concatenate([slot(nm, _pack_rows(local[nm]).astype(BF16)) for nm in groups[0]], axis=0)
    gathered0 = all_gather(packed0, "gather_mla_weights")
    gathered = {nm: gathered0[:, offsets[nm]:offsets[nm] + rows_of[nm], :] for nm in groups[0]}
    gather_started = [None] * len(groups)

    def depart(gi, after):
        blocks = [held_until(_pack_rows(local[nm]).astype(BF16), after) for nm in groups[gi]]
        gather_started[gi] = exchange_start(blocks, [landing(b) for b in blocks], f"gather_group{gi}_start", False)
        return gather_started[gi][4]

    def full(nm, cols):
        return gathered[nm].reshape(-1, cols)

    w_in = jnp.pad(full("mla_w_in", mla_in), ((0, 0), (0, mla_in_pad - mla_in)))
    wt_uq = full("mla_w_uq", MLA_QR).reshape(MLA_HEADS, MLA_NOPE + MLA_ROPE, MLA_QR)
    wt_uq_n = wt_uq[:, :MLA_NOPE].reshape(MLA_HEADS * MLA_NOPE, MLA_QR)
    wt_uq_r = wt_uq[:, MLA_NOPE:].reshape(MLA_HEADS * MLA_ROPE, MLA_QR)
    wt_uk = full("mla_w_uk", MLA_KVR)
    wt_uv = full("mla_w_uv", MLA_KVR)
    w_mo = full("mla_w_o", d)
    wt_gate, wt_up, w_down = [None] * DEPTH, [None] * DEPTH, [None] * DEPTH

    def arrive(gi, after):
        if gi + 1 < len(groups):
            after = depart(gi + 1, after)
        landed = list(exchange_wait(gather_started[gi], after, f"gather_group{gi}_wait", False))
        gathered.update(zip(groups[gi], landed))
        for i in range(DEPTH):
            if group_of[f"gate{i}"] == gi:
                wt_gate[i], wt_up[i], w_down[i] = full(f"gate{i}", d), full(f"up{i}", d), full(f"down{i}", d)

    small = jnp.concatenate([c.reshape(-1, LANES), ln_g.reshape(-1, LANES), ln_b.reshape(-1, LANES)], axis=0)
    small_rows = small.shape[0]
    small = jnp.pad(small, ((0, (-small_rows) % 8), (0, 0)))
    small_all = all_gather(small, "gather_small")
    c_rows = bl * d // LANES
    c_all = small_all[:, :c_rows].reshape(N_DEV * bl, d)
    n_ln = DEPTH * 2
    ln_g_all = small_all[:, c_rows:c_rows + n_ln, :].transpose(1, 0, 2).reshape(DEPTH, 2, 1, d)
    ln_b_all = small_all[:, c_rows + n_ln:c_rows + 2 * n_ln, :].transpose(1, 0, 2).reshape(DEPTH, 2, 1, d)

    c_act = silu_rows(c_all, "silu_c")
    ada_b_loc = lax.dynamic_slice_in_dim(ada_b, dev * ada_cols, ada_cols, axis=1)
    mod_cols = [mm([(c_act, ada_w[i])], trans_b=False, out_dtype=F32, name=f"ada_fwd{i}", bias=ada_b_loc[i][None, :])
                for i in range(DEPTH)]
    mod_all = all_gather(jnp.concatenate(mod_cols, axis=0), "gather_mod")
    mod_all = mod_all.reshape(N_DEV, DEPTH, N_DEV * bl, ada_cols).transpose(1, 2, 0, 3).reshape(DEPTH, N_DEV * bl, 6 * d)
    mod_mine = lax.dynamic_slice_in_dim(mod_all, dev * bl, bl, axis=1)
    mods = [mod_mine[i].reshape(bl * 6, 1, d) for i in range(DEPTH)]
    mods[0] = mods[0] + depart(1, (mod_mine, gathered0))[0, 0]

    half_r = MLA_ROPE // 2
    inv_freq = ROPE_THETA ** (-jnp.arange(half_r, dtype=F32) / half_r)
    inv_freq = jnp.tile(inv_freq, LANES // half_r)[None, :]
    sign = jnp.tile(jnp.concatenate([-jnp.ones((half_r,), F32), jnp.ones((half_r,), F32)]), LANES // MLA_ROPE)[None, :]
    cos_t, sin_t = rope_tables(positions.astype(F32).reshape(t, 1), inv_freq, sign, "rope_tables")

    x2d = x.reshape(t, d)
    g_q, g_kv = mla_g_q.reshape(1, MLA_QR), mla_g_kv.reshape(1, MLA_KVR)
    b_f = jnp.pad(fox_b_f.reshape(1, FOX_HEADS), ((0, 0), (0, LANES - FOX_HEADS)))
    mla_scale = (MLA_NOPE + MLA_ROPE) ** -0.5
    fox_scale = FOX_HD ** -0.5
    tq = _attn_tiles(s)
    nk = s // tq

    saved = []
    u = modulate(x2d, mods[0], 0, 1, bl, "modulate0")
    xin = x2d
    for i in range(DEPTH):
        sv = {"u": u, "x_in": xin}
        if i % 2 == 0:
            h_in = mm([(u, w_in)], trans_b=False, out_dtype=F32, name=f"mla_in{i}")
            c_q, c_kv, k_r = mla_latents_forward(h_in, g_q, g_kv, cos_t, sin_t, f"mla_latents{i}")
            q_n = mm([(c_q, wt_uq_n)], trans_b=True, out_dtype=BF16, out_slab=True, name=f"mla_qn{i}")
            q_r_raw = mm([(c_q, wt_uq_r)], trans_b=True, out_dtype=F32, out_slab=True, name=f"mla_qr{i}")
            q_r = rope_slabs(q_r_raw, cos_t, sin_t, BF16, f"mla_qrope{i}")
            k_n = mm([(c_kv, wt_uk)], trans_b=True, out_dtype=BF16, out_slab=True, name=f"mla_kn{i}")
            v_m = mm([(c_kv, wt_uv)], trans_b=True, out_dtype=BF16, out_slab=True, name=f"mla_v{i}")
            ops = (q_n, q_r, k_n, k_r, v_m)
            o, lse, o_delta = attention_forward("mla", ops, bl, mla_scale, f"mla_attn{i}")
            y = mm([(o, w_mo)], trans_b=False, out_dtype=F32, name=f"mla_out{i}")
            sv.update(h_in=h_in, c_q=c_q, c_kv=c_kv, ops=ops, o=o, lse=lse, o_delta=o_delta)
        else:
            arrive(2, u)
            wt_fox = full("fox_w_in", d)
            wt_qkv = wt_fox[:3 * d]
            wt_f = jnp.pad(wt_fox[3 * d:], ((0, LANES - FOX_HEADS), (0, 0)))
            w_fo = full("fox_w_o", d)
            qkv = mm([(u, wt_qkv)], trans_b=True, out_dtype=BF16, out_slab=True, name=f"fox_qkv{i}")
            z = mm([(u, wt_f)], trans_b=True, out_dtype=F32, name=f"fox_z{i}")
            f_tok, f_q = fox_gate_forward(z, b_f, bl, f"fox_gate{i}")
            f_k = f_tok[:, :FOX_HEADS].reshape(bl, nk, tq, FOX_HEADS // 2, 2).transpose(0, 3, 1, 4, 2)
            f_k = jnp.pad(f_k.reshape(bl * FOX_HEADS // 2, nk, 2, tq), ((0, 0), (0, 0), (0, 6), (0, 0)))
            ops = (qkv, f_q, f_k)
            o, lse, o_delta = attention_forward("fox", ops, bl, fox_scale, f"fox_attn{i}")
            y = mm([(o, w_fo)], trans_b=False, out_dtype=F32, name=f"fox_out{i}")
            sv.update(z=z, ops=ops, o=o, lse=lse, o_delta=o_delta)
        x1, r1, u2 = residual_layer_norm(xin, y, mods[i], 2, ln_g_all[i, 0], ln_b_all[i, 0], bl, f"ln_mix{i}",
                                         next_mod=(3, 4))
        if wt_gate[i] is None:
            arrive(group_of[f"gate{i}"], u2)
        a, bb, h = swiglu_in(u2, wt_gate[i], wt_up[i], f"ffn_in{i}")
        y2 = mm([(h, w_down[i])], trans_b=False, out_dtype=F32, name=f"ffn_down{i}")
        sv.update(y=y, r1=r1, u2=u2, a=a, bb=bb, h=h, y2=y2)
        if i + 1 < DEPTH:
            xin, r2, u = residual_layer_norm(x1, y2, mods[i], 5, ln_g_all[i, 1], ln_b_all[i, 1], bl, f"ln_ffn{i}",
                                             next_mod=(0, 1, mods[i + 1]))
        else:
            xin, r2 = residual_layer_norm(x1, y2, mods[i], 5, ln_g_all[i, 1], ln_b_all[i, 1], bl, f"ln_ffn{i}")
        sv.update(r2=r2)
        saved.append(sv)

    loss_cols, d_x = loss_head(xin, loss_target.reshape(t, d), "loss_head")

    grads_full = {}
    wgrad = functools.partial(mm_tn, out_dtype=BF16)
    dmod = [[None] * 6 for _ in range(DEPTH)]
    dg_ln = [[None, None] for _ in range(DEPTH)]
    db_ln = [[None, None] for _ in range(DEPTH)]
    dg_q = dg_kv = db_f = None
    d_a, du = d_x, None
    scatter_started = [None] * len(groups)

    def scatter_start(gi, after=None):
        gs = [grads_full[nm].reshape(N_DEV, rows_of[nm], PACK_COLS).astype(BF16) for nm in groups[gi]]
        if gi == 0:
            gs = [jnp.concatenate([slot(nm, g) for nm, g in zip(groups[gi], gs)], axis=1)]
        if after is not None:
            gs = [held_until(g, after) for g in gs]
        lands = [landing(lax.dynamic_index_in_dim(g, dev, 0, keepdims=False)) for g in gs]
        scatter_started[gi] = exchange_start(gs, lands, f"scatter_group{gi}_start", True)

    ln_g_bwd = [[ln_g_all[i, k] for k in range(2)] for i in range(DEPTH)]
    for i in reversed(range(DEPTH)):
        sv = saved[i]
        if i + 1 < DEPTH:
            gi = group_of["fox_w_in"]
            scatter_start(gi)
            ln_g_bwd[i][1] = after_token(ln_g_bwd[i][1], scatter_started[gi])
        ln2 = (sv["r2"], sv["y2"], ln_g_bwd[i][1], ln_b_all[i, 1], (mods[i], 5))
        if du is None:
            bw = sublayer_backward(d_a, bl, f"bwd_ln_ffn{i}", ln=ln2)
        else:
            bw = sublayer_backward(d_a, bl, f"bwd_ln_ffn{i}", du=du, scale=(mods[i + 1], 1), ln=ln2)
            dmod[i + 1][0], dmod[i + 1][1] = bw["dshift"], bw["dscale"]
        dmod[i][5], dg_ln[i][1], db_ln[i][1] = bw["dgate"], bw["dg"], bw["db"]
        dy2 = bw["dy"]
        da, dbb = swiglu_out_backward(dy2, w_down[i], sv["a"], sv["bb"], f"bwd_ffn_act{i}")
        du2 = mm([(da, wt_gate[i]), (dbb, wt_up[i])], trans_b=False, out_dtype=F32, name=f"bwd_ffn_du{i}")
        grads_full[f"down{i}"] = wgrad(sv["h"], dy2, name=f"bwd_w_down{i}")
        grads_full[f"gate{i}"] = wgrad(da, sv["u2"], name=f"bwd_w_gate{i}")
        grads_full[f"up{i}"] = wgrad(dbb, sv["u2"], name=f"bwd_w_up{i}")
        gi = group_of[f"gate{i}"]
        scatter_start(gi)
        ln_g_bwd[i][0] = after_token(ln_g_bwd[i][0], scatter_started[gi])
        bw = sublayer_backward(bw["dx"], bl, f"bwd_ln_mix{i}", du=du2, scale=(mods[i], 4),
                               ln=(sv["r1"], sv["y"], ln_g_bwd[i][0], ln_b_all[i, 0], (mods[i], 2)))
        dmod[i][3], dmod[i][4], dmod[i][2] = bw["dshift"], bw["dscale"], bw["dgate"]
        dg_ln[i][0], db_ln[i][0] = bw["dg"], bw["db"]
        d_a, dy = bw["dx"], bw["dy"]
        o, lse, ops = sv["o"], sv["lse"], sv["ops"]
        if i % 2 == 0:
            do = mm([(dy, w_mo)], trans_b=True, out_dtype=BF16, out_slab=True, name=f"bwd_mla_do{i}")
            grads_full["mla_w_o"] = wgrad(o, dy, name=f"bwd_w_mla_o{i}")
            dqn, dqr, dkn, dvm, dkr = attention_backward("mla", ops, sv["o_delta"], do, lse, bl, mla_scale,
                                                         f"bwd_mla_attn{i}")
            dqr = rope_slabs(dqr, cos_t, sin_t, F32, f"bwd_mla_qrope{i}", transposed=True)
            dcq = mm([(dqn, wt_uq_n), (dqr, wt_uq_r)], trans_b=False, out_dtype=F32, name=f"bwd_mla_dcq{i}")
            dckv = mm([(dkn, wt_uk), (dvm, wt_uv)], trans_b=False, out_dtype=F32, name=f"bwd_mla_dckv{i}")
            d_uq_n = wgrad(dqn, sv["c_q"], name=f"bwd_w_uq_n{i}").reshape(MLA_HEADS, MLA_NOPE, MLA_QR)
            d_uq_r = wgrad(dqr, sv["c_q"], name=f"bwd_w_uq_r{i}").reshape(MLA_HEADS, MLA_ROPE, MLA_QR)
            grads_full["mla_w_uq"] = jnp.concatenate([d_uq_n, d_uq_r], axis=1)
            grads_full["mla_w_uk"] = wgrad(dkn, sv["c_kv"], name=f"bwd_w_uk{i}")
            grads_full["mla_w_uv"] = wgrad(dvm, sv["c_kv"], name=f"bwd_w_uv{i}")
            dh_in, dg_q, dg_kv = mla_latents_backward(sv["h_in"], dcq, dckv, dkr, g_q, g_kv, cos_t, sin_t,
                                                      f"bwd_mla_latents{i}")
            du = mm([(dh_in, w_in)], trans_b=True, out_dtype=F32, name=f"bwd_mla_du{i}")
            grads_full["mla_w_in"] = wgrad(sv["u"], dh_in, name=f"bwd_w_mla_in{i}")[:, :mla_in]
        else:
            do = mm([(dy, w_fo)], trans_b=True, out_dtype=BF16, out_slab=True, name=f"bwd_fox_do{i}")
            grads_full["fox_w_o"] = wgrad(o, dy, name=f"bwd_w_fox_o{i}")
            dq, dk, dvf, dfk = attention_backward("fox", ops, sv["o_delta"], do, lse, bl, fox_scale, f"bwd_fox_attn{i}")
            df = dfk[:, :, :2, :].reshape(bl, FOX_HEADS // 2, nk, 2, tq).transpose(0, 2, 4, 1, 3).reshape(t, FOX_HEADS)
            df = jnp.pad(df, ((0, 0), (0, LANES - FOX_HEADS)))
            dz, db_f = fox_gate_backward(sv["z"], b_f, df, bl, f"bwd_fox_gate{i}")
            du = mm([(dq, wt_fox[0:d]), (dk, wt_fox[d:2 * d]), (dvf, wt_fox[2 * d:3 * d]), (dz, wt_f)],
                    trans_b=False, out_dtype=F32, name=f"bwd_fox_du{i}")
            u_f = sv["u"]
            grads_full["fox_w_in"] = jnp.concatenate(
                [wgrad(dq, u_f, name=f"bwd_w_fox_q{i}"), wgrad(dk, u_f, name=f"bwd_w_fox_k{i}"),
                 wgrad(dvf, u_f, name=f"bwd_w_fox_v{i}"), wgrad(dz, u_f, name=f"bwd_w_fox_f{i}")[:FOX_HEADS]], axis=0)
    scatter_start(0)
    bw = sublayer_backward(d_a, bl, "bwd_input", du=du, scale=(after_token(mods[0], scatter_started[0]), 1), x_in=x2d)
    dmod[0][0], dmod[0][1] = bw["dshift"], bw["dscale"]
    grad_x = bw["dx"].reshape(bl, s, d)

    dmod_rows = jnp.concatenate([r.reshape(bl, d) for layer in dmod for r in layer], axis=0)
    dmod_rows = dmod_rows.reshape(DEPTH, 6, bl, d).transpose(0, 2, 1, 3)
    n_mod = dmod_rows.size // LANES
    ln_parts = [dg_ln[i][k] for i in range(DEPTH) for k in range(2)] + [db_ln[i][k] for i in range(DEPTH) for k in range(2)]
    small_g = jnp.concatenate([dmod_rows.reshape(-1, LANES), dg_q.reshape(-1, LANES), dg_kv.reshape(-1, LANES), db_f]
                              + [p.reshape(-1, LANES) for p in ln_parts] + [loss_cols.reshape(-1, LANES)], axis=0)
    n_small = small_g.shape[0]
    small_g = jnp.pad(small_g, ((0, (-n_small) % 8), (0, 0)))
    small_g_all = all_gather(small_g, "gather_small_grads")
    small_sum = sum_leading(small_g_all, "sum_small_grads")
    per_seq = DEPTH * 6 * d // LANES
    dmod_all = small_g_all[:, :n_mod].reshape(N_DEV, DEPTH, bl, 6 * d).transpose(1, 0, 2, 3)
    dmod_all = dmod_all.reshape(DEPTH, N_DEV * bl, 6 * d)
    o1 = n_mod
    grad_g_q = small_sum[o1:o1 + MLA_QR // LANES].reshape(1, MLA_QR)
    o1 += MLA_QR // LANES
    grad_g_kv = small_sum[o1:o1 + MLA_KVR // LANES].reshape(1, MLA_KVR)
    o1 += MLA_KVR // LANES
    grad_b_f = small_sum[o1:o1 + 1, :FOX_HEADS]
    o1 += 1
    n_ln_rows = DEPTH * 2 * d // LANES
    grad_ln_g_full = small_sum[o1:o1 + n_ln_rows].reshape(DEPTH, 2, d)
    grad_ln_b_full = small_sum[o1 + n_ln_rows:o1 + 2 * n_ln_rows].reshape(DEPTH, 2, d)
    loss = jnp.sum(small_sum[o1 + 2 * n_ln_rows:o1 + 2 * n_ln_rows + d // LANES])
    shard = d // N_DEV
    grad_ln_g = lax.dynamic_slice_in_dim(grad_ln_g_full, dev * shard, shard, axis=2)
    grad_ln_b = lax.dynamic_slice_in_dim(grad_ln_b_full, dev * shard, shard, axis=2)
    by_seq = small_g_all[:, :n_mod].reshape(N_DEV, DEPTH, bl, 6 * d // LANES, LANES).transpose(0, 2, 1, 3, 4)
    grad_ada_b = sum_leading(by_seq.reshape(N_DEV * bl, per_seq, LANES), "sum_ada_b").reshape(DEPTH, 6 * d)
    dmod_cols = lax.dynamic_slice_in_dim(dmod_all, dev * ada_cols, ada_cols, axis=2)
    grad_ada_w = jnp.stack([mm_tn(c_act, dmod_cols[i], name=f"bwd_w_ada{i}") for i in range(DEPTH)])

    g_mine = {}

    def scatter_arrive(gi, after):
        landed = exchange_wait(scatter_started[gi], after, f"scatter_group{gi}_wait", True)
        if gi == 0:
            total = sum_leading(landed[0], f"scatter_group{gi}_sum")
            g_mine.update({nm: total[offsets[nm]:offsets[nm] + rows_of[nm]] for nm in groups[gi]})
            return total
        for nm, land in zip(groups[gi], landed):
            g_mine[nm] = sum_leading(land, f"scatter_sum_{nm}")
        return g_mine[groups[gi][-1]]

    after = scatter_started[0][4]
    for gi in reversed(range(1, len(groups))):
        after = scatter_arrive(gi, after)

    def mine(nm, shape):
        return g_mine[nm].reshape(shape)

    def shard_t(nm, a):
        return mine(nm, t_last(a).shape)

    transposed = {"mla_w_uq", "mla_w_uk", "mla_w_uv", "fox_w_in", "ffn_w_gate", "ffn_w_up"}
    grads = {
        "mla_w_in": lambda: mine("mla_w_in", mla_w_in[0].shape)[None],
        "mla_g_q": lambda: grad_g_q,
        "mla_w_uq": lambda: shard_t("mla_w_uq", mla_w_uq[0])[None],
        "mla_g_kv": lambda: grad_g_kv,
        "mla_w_uk": lambda: shard_t("mla_w_uk", mla_w_uk[0])[None],
        "mla_w_uv": lambda: shard_t("mla_w_uv", mla_w_uv[0])[None],
        "mla_w_o": lambda: mine("mla_w_o", mla_w_o[0].shape)[None],
        "fox_w_in": lambda: shard_t("fox_w_in", fox_w_in[0])[None],
        "fox_b_f": lambda: grad_b_f,
        "fox_w_o": lambda: mine("fox_w_o", fox_w_o[0].shape)[None],
        "ada_w": lambda: grad_ada_w,
        "ada_b": lambda: grad_ada_b,
        "ffn_w_gate": lambda: jnp.stack([shard_t(f"gate{i}", ffn_w_gate[i]) for i in range(DEPTH)]),
        "ffn_w_up": lambda: jnp.stack([shard_t(f"up{i}", ffn_w_up[i]) for i in range(DEPTH)]),
        "ffn_w_down": lambda: jnp.stack([mine(f"down{i}", ffn_w_down[i].shape) for i in range(DEPTH)]),
        "ln_g": lambda: grad_ln_g,
        "ln_b": lambda: grad_ln_b,
    }
    weights = dict(mla_w_in=mla_w_in, mla_g_q=mla_g_q, mla_w_uq=mla_w_uq, mla_g_kv=mla_g_kv, mla_w_uk=mla_w_uk,
                   mla_w_uv=mla_w_uv, mla_w_o=mla_w_o, fox_w_in=fox_w_in, fox_b_f=fox_b_f, fox_w_o=fox_w_o,
                   ada_w=ada_w, ada_b=ada_b, ffn_w_gate=ffn_w_gate, ffn_w_up=ffn_w_up, ffn_w_down=ffn_w_down,
                   ln_g=ln_g, ln_b=ln_b)
    first = dict(mla_w_in=m_mla_w_in, mla_g_q=m_mla_g_q, mla_w_uq=m_mla_w_uq, mla_g_kv=m_mla_g_kv, mla_w_uk=m_mla_w_uk,
                 mla_w_uv=m_mla_w_uv, mla_w_o=m_mla_w_o, fox_w_in=m_fox_w_in, fox_b_f=m_fox_b_f, fox_w_o=m_fox_w_o,
                 ada_w=m_ada_w, ada_b=m_ada_b, ffn_w_gate=m_ffn_w_gate, ffn_w_up=m_ffn_w_up, ffn_w_down=m_ffn_w_down,
                 ln_g=m_ln_g, ln_b=m_ln_b)
    second = dict(mla_w_in=v_mla_w_in, mla_g_q=v_mla_g_q, mla_w_uq=v_mla_w_uq, mla_g_kv=v_mla_g_kv, mla_w_uk=v_mla_w_uk,
                  mla_w_uv=v_mla_w_uv, mla_w_o=v_mla_w_o, fox_w_in=v_fox_w_in, fox_b_f=v_fox_b_f, fox_w_o=v_fox_w_o,
                  ada_w=v_ada_w, ada_b=v_ada_b, ffn_w_gate=v_ffn_w_gate, ffn_w_up=v_ffn_w_up, ffn_w_down=v_ffn_w_down,
                  ln_g=v_ln_g, ln_b=v_ln_b)
    order = list(weights)
    last = [nm for nm in order if group_of.get(nm) == 0]
    updated = {}
    for nm in [nm for nm in order if nm not in last] + last:
        if last and nm == last[0]:
            scatter_arrive(0, after)
        lay = t_last if nm in transposed else (lambda a: a)
        w = lay(weights[nm])
        g = grads[nm]().reshape(w.shape)
        delta, new_m, new_v = adamw(w, g, lay(first[nm]), lay(second[nm]), f"adamw_{nm}")
        updated[nm] = (lay(g), lay(delta), lay(new_m), lay(new_v))
        after = new_v
    return (loss, grad_x, *(updated[nm][k] for k in range(4) for nm in order))
```

```python
import functools
import math

import jax
import jax.numpy as jnp
from jax import lax
from jax.experimental import pallas as pl
from jax.experimental.pallas import tpu as pltpu

F32 = jnp.float32
BF16 = jnp.bfloat16
LANES = 128
N_DEV = 8
VMEM_LIMIT_BYTES = 56 * 1024 * 1024

DEPTH = 2
MLA_HEADS = 8
MLA_NOPE = 128
MLA_ROPE = 64
MLA_V = 128
MLA_QR = 256
MLA_KVR = 256
ROPE_THETA = 10000.0
FOX_HEADS = 16
FOX_HD = 64
ALPHA = (2.0 * DEPTH) ** 0.25
NORM_EPS = 1e-5
ADAM_LR = 0.001
ADAM_B1 = 0.9
ADAM_B2 = 0.999
ADAM_EPS = 1e-08
ADAM_WD = 0.01
ADAM_STEP = 10

MESH_AXES = ("x", "y", "c")
MESH = pl.DeviceIdType.MESH


def _params(*sem):
    return pltpu.CompilerParams(dimension_semantics=sem, vmem_limit_bytes=VMEM_LIMIT_BYTES)


def _tile(n, cap, mult=LANES):
    if n <= cap:
        return n
    best = None
    for t in range(mult, cap + 1, mult):
        if n % t == 0:
            best = t
    assert best is not None, (n, cap, mult)
    return best


def _dot(a, b, dims):
    return lax.dot_general(a, b, (dims, ((), ())), preferred_element_type=F32)


def _nn(a, b):
    return _dot(a, b, ((1,), (0,)))


def _nt(a, b):
    return _dot(a, b, ((1,), (1,)))


def _tn(a, b):
    return _dot(a, b, ((0,), (0,)))


def _me():
    return lax.axis_index("x"), lax.axis_index("y"), lax.axis_index("c")


def all_gather(x_loc, name):
    r, c = x_loc.shape

    def body(x_ref, out_ref, send_sems, recv_sems, local_sem):
        x, y, cc = _me()
        me, sibling = (x, y, cc), (x, y, 1 - cc)
        chips = [(1 - x, y), (x, 1 - y), (1 - x, 1 - y)]

        def rows(px, py, pc):
            return out_ref.at[4 * px + 2 * py + pc]

        def copy(k, block, to, src=None):
            return pltpu.make_async_remote_copy(
                src_ref=rows(*block) if src is None else src, dst_ref=rows(*block),
                send_sem=send_sems.at[k], recv_sem=recv_sems.at[k], device_id=to, device_id_type=MESH)

        mine = pltpu.make_async_copy(x_ref, rows(*me), local_sem)
        mine.start()
        first = [copy(0, me, sibling, src=x_ref)]
        first += [copy(1 + j, me, (*chip, cc), src=x_ref) for j, chip in enumerate(chips)]
        for cp in first:
            cp.start()
        passed = [copy(4 + j, (*chip, cc), sibling) for j, chip in enumerate(chips)]
        for j, chip in enumerate(chips):
            copy(1 + j, (*chip, cc), me).wait_recv()
            passed[j].start()
        copy(0, sibling, me).wait_recv()
        for j, chip in enumerate(chips):
            copy(4 + j, (*chip, 1 - cc), me).wait_recv()
        for cp in first + passed:
            cp.wait_send()
        mine.wait()

    return pl.pallas_call(
        body, name=name,
        out_shape=jax.ShapeDtypeStruct((N_DEV, r, c), x_loc.dtype),
        in_specs=[pl.BlockSpec(memory_space=pl.ANY)],
        out_specs=pl.BlockSpec(memory_space=pl.ANY),
        scratch_shapes=[pltpu.SemaphoreType.DMA((7,)), pltpu.SemaphoreType.DMA((7,)), pltpu.SemaphoreType.DMA(())],
    )(x_loc)


HBM_SPEC = pl.BlockSpec(memory_space=pltpu.HBM)
SEM_SPEC = pl.BlockSpec(memory_space=pltpu.SEMAPHORE)
N_PEERS = N_DEV - 1


def _peer(k):
    x, y, c = _me()
    return (1 - x if k & 4 else x, 1 - y if k & 2 else y, 1 - c if k & 1 else c)


def _exchange_copies(src_refs, land_refs, send_sems, recv_sems, scatter):
    x, y, c = _me()
    mine = 4 * x + 2 * y + c
    copies = []
    for n, (src_ref, land_ref) in enumerate(zip(src_refs, land_refs)):
        for k in range(1, N_DEV):
            px, py, pc = _peer(k)
            src = src_ref.at[4 * px + 2 * py + pc] if scatter else src_ref
            sem = n * N_PEERS + k - 1
            copies.append(pltpu.make_async_remote_copy(
                src_ref=src, dst_ref=land_ref.at[mine], send_sem=send_sems.at[sem], recv_sem=recv_sems.at[sem],
                device_id=(px, py, pc), device_id_type=MESH))
    return copies


def exchange_start(srcs, lands, name, scatter):
    n = len(srcs)

    def body(*refs):
        send_sems, recv_sems = refs[2 * n], refs[2 * n + 1]
        for cp in _exchange_copies(refs[:n], refs[n:2 * n], send_sems, recv_sems, scatter):
            cp.start()
        token = refs[-1]
        token[...] = jnp.zeros_like(token)

    outs = pl.pallas_call(
        body, name=name,
        out_shape=(pltpu.SemaphoreType.DMA((n * N_PEERS,)), pltpu.SemaphoreType.DMA((n * N_PEERS,)),
                   *(pltpu.HBM(a.shape, a.dtype) for a in (*srcs, *lands)), jax.ShapeDtypeStruct((8, LANES), F32)),
        in_specs=(HBM_SPEC,) * (2 * n),
        out_specs=(SEM_SPEC, SEM_SPEC, *((HBM_SPEC,) * (2 * n)), pl.BlockSpec(memory_space=pltpu.VMEM)),
        input_output_aliases={i: 2 + i for i in range(2 * n)},
        compiler_params=pltpu.CompilerParams(has_side_effects=pltpu.SideEffectType.DATAFLOW_SIDE_EFFECTING),
    )(*(pltpu.with_memory_space_constraint(a, pltpu.HBM) for a in (*srcs, *lands)))
    return outs[0], outs[1], outs[2:2 + n], outs[2 + n:2 + 2 * n], outs[-1]


def exchange_wait(started, after, name, scatter):
    send_sems, recv_sems, srcs, lands, _ = started
    n = len(srcs)

    def body(*refs):
        send_sems, recv_sems = refs[2 * n], refs[2 * n + 1]
        for cp in _exchange_copies(refs[:n], refs[n:2 * n], send_sems, recv_sems, scatter):
            cp.wait_send()
            cp.wait_recv()

    outs = pl.pallas_call(
        body, name=name,
        out_shape=tuple(pltpu.HBM(a.shape, a.dtype) for a in (*srcs, *lands)),
        in_specs=(*((HBM_SPEC,) * (2 * n)), SEM_SPEC, SEM_SPEC, pl.BlockSpec(memory_space=pl.ANY)),
        out_specs=(HBM_SPEC,) * (2 * n), input_output_aliases={i: i for i in range(2 * n)},
        compiler_params=pltpu.CompilerParams(has_side_effects=pltpu.SideEffectType.DATAFLOW_SIDE_EFFECTING),
    )(*srcs, *lands, send_sems, recv_sems, after)
    return outs[n:]


def after_token(small, started):
    return small + started[4][0, 0]


def sum_leading(x, name):
    n, r, c = x.shape
    tr = _tile(r, 512, 16)

    def body(x_ref, o_ref):
        acc = x_ref[0].astype(F32)
        for k in range(1, n):
            acc = acc + x_ref[k].astype(F32)
        o_ref[...] = acc

    return pl.pallas_call(
        body, name=name,
        out_shape=jax.ShapeDtypeStruct((r, c), F32),
        grid=(r // tr,),
        in_specs=[pl.BlockSpec((n, tr, c), lambda i: (0, i, 0))],
        out_specs=pl.BlockSpec((tr, c), lambda i: (i, 0)),
        compiler_params=_params("arbitrary"),
    )(x)


MM_VMEM_BUDGET = 36 * 1024 * 1024
GRID_STEP_AS_BYTES = 1 << 20


def _mm_tiles(m, n, a_row_bytes, b_col_bytes, out_bytes):
    tms = [c for c in (2048, 1024, 512, 256, 128, 64, 32, 16, 8) if m % c == 0] or [m]
    tns = [c for c in range(LANES, min(n, 2048) + 1, LANES) if n % c == 0] or [n]
    best = None
    for tm in tms:
        for tn in tns:
            vmem = 2 * (tm * a_row_bytes + tn * b_col_bytes) + 2 * tm * tn * out_bytes + tm * tn * 4
            if vmem > MM_VMEM_BUDGET:
                continue
            steps = (m // tm) * (n // tn)
            cost = steps * GRID_STEP_AS_BYTES + (m // tm) * n * b_col_bytes + m * a_row_bytes
            if best is None or cost < best[0]:
                best = (cost, tm, tn)
    assert best is not None, (m, n, a_row_bytes, b_col_bytes)
    return best[1], best[2]


def mm(pairs, *, trans_b, out_dtype, name, out_slab=False, bias=None):
    a0 = pairs[0][0]
    m = a0.shape[1] if a0.ndim == 3 else a0.shape[0]
    n = pairs[0][1].shape[0] if trans_b else pairs[0][1].shape[1]
    a_row_bytes = sum((b.shape[1] if trans_b else b.shape[0]) * a.dtype.itemsize for a, b in pairs)
    b_col_bytes = sum((b.shape[1] if trans_b else b.shape[0]) * b.dtype.itemsize for _, b in pairs)
    tm, tn = _mm_tiles(m, n, a_row_bytes, b_col_bytes, jnp.dtype(out_dtype).itemsize)
    slabs = [a.ndim == 3 for a, _ in pairs]
    n_pairs = len(pairs)

    def body(*refs):
        o_ref = refs[-1]
        acc = bias_ref = None
        if bias is not None:
            bias_ref = refs[2 * n_pairs]
        for i in range(n_pairs):
            a_ref, b_ref = refs[2 * i], refs[2 * i + 1]
            if slabs[i]:
                a = jnp.concatenate([a_ref[s].astype(BF16) for s in range(a_ref.shape[0])], axis=1)
            else:
                a = a_ref[...].astype(BF16)
            b = b_ref[...].astype(BF16)
            part = _nt(a, b) if trans_b else _nn(a, b)
            acc = part if acc is None else acc + part
        if bias_ref is not None:
            acc = acc + bias_ref[...]
        if out_slab:
            for s in range(tn // LANES):
                o_ref[s] = acc[:, s * LANES:(s + 1) * LANES].astype(out_dtype)
        else:
            o_ref[...] = acc.astype(out_dtype)

    in_specs, args = [], []
    for (a, b), slab in zip(pairs, slabs):
        if slab:
            in_specs.append(pl.BlockSpec((a.shape[0], tm, LANES), lambda i, j: (0, i, 0)))
        else:
            in_specs.append(pl.BlockSpec((tm, a.shape[1]), lambda i, j: (i, 0)))
        if trans_b:
            in_specs.append(pl.BlockSpec((tn, b.shape[1]), lambda i, j: (j, 0)))
        else:
            in_specs.append(pl.BlockSpec((b.shape[0], tn), lambda i, j: (0, j)))
        args += [a, b]
    if bias is not None:
        in_specs.append(pl.BlockSpec((1, tn), lambda i, j: (0, j)))
        args.append(bias)
    if out_slab:
        out_shape = jax.ShapeDtypeStruct((n // LANES, m, LANES), out_dtype)
        out_spec = pl.BlockSpec((tn // LANES, tm, LANES), lambda i, j: (j, i, 0))
    else:
        out_shape = jax.ShapeDtypeStruct((m, n), out_dtype)
        out_spec = pl.BlockSpec((tm, tn), lambda i, j: (i, j))
    return pl.pallas_call(
        body, name=name, out_shape=out_shape, grid=(m // tm, n // tn),
        in_specs=in_specs, out_specs=out_spec,
        compiler_params=_params("arbitrary", "arbitrary"),
    )(*args)


def mm_tn(a, b, *, name, out_dtype=F32, tk_cap=1536, tn_cap=1024, tm_cap=512):
    slab = a.ndim == 3
    m = a.shape[1] if slab else a.shape[0]
    k = a.shape[0] * LANES if slab else a.shape[1]
    n = b.shape[1]
    tk = _tile(k, tk_cap)
    tn = _tile(n, tn_cap)
    tm = _tile(m, tm_cap, 8)
    n_steps = m // tm

    def body(a_ref, b_ref, o_ref, acc_ref):
        step = pl.program_id(2)

        @pl.when(step == 0)
        def _():
            acc_ref[...] = jnp.zeros_like(acc_ref)

        bb = b_ref[...].astype(BF16)
        if slab:
            for s in range(tk // LANES):
                acc_ref[s * LANES:(s + 1) * LANES, :] += _tn(a_ref[s].astype(BF16), bb)
        else:
            acc_ref[...] += _tn(a_ref[...].astype(BF16), bb)

        @pl.when(step == n_steps - 1)
        def _():
            o_ref[...] = acc_ref[...].astype(out_dtype)

    if slab:
        a_spec = pl.BlockSpec((tk // LANES, tm, LANES), lambda i, j, t: (i, t, 0))
    else:
        a_spec = pl.BlockSpec((tm, tk), lambda i, j, t: (t, i))
    return pl.pallas_call(
        body, name=name, out_shape=jax.ShapeDtypeStruct((k, n), out_dtype), grid=(k // tk, n // tn, n_steps),
        in_specs=[a_spec, pl.BlockSpec((tm, tn), lambda i, j, t: (t, j))],
        out_specs=pl.BlockSpec((tk, tn), lambda i, j, t: (i, j)),
        scratch_shapes=[pltpu.VMEM((tk, tn), F32)],
        compiler_params=_params("arbitrary", "arbitrary", "arbitrary"),
    )(a, b)


def _row_spec(d, k):
    return pl.BlockSpec((1, 1, d), lambda b, i: (6 * b + k, 0, 0))


def modulate(x, mod, k_shift, k_scale, bl, name):
    t, d = x.shape
    s = t // bl
    tm = _tile(s, 512, 8)
    nt = s // tm

    def body(x_ref, sh_ref, sc_ref, o_ref):
        o_ref[...] = (x_ref[...] * (1.0 + sc_ref[0]) + sh_ref[0]).astype(BF16)

    return pl.pallas_call(
        body, name=name, out_shape=jax.ShapeDtypeStruct((t, d), BF16), grid=(bl, nt),
        in_specs=[pl.BlockSpec((tm, d), lambda b, i: (b * nt + i, 0)), _row_spec(d, k_shift), _row_spec(d, k_scale)],
        out_specs=pl.BlockSpec((tm, d), lambda b, i: (b * nt + i, 0)),
        compiler_params=_params("arbitrary", "arbitrary"),
    )(x, mod, mod)


def _layer_norm_stats(r):
    mu = jnp.mean(r, axis=-1, keepdims=True)
    rc = r - mu
    var = jnp.mean(rc * rc, axis=-1, keepdims=True)
    rstd = lax.rsqrt(var + NORM_EPS)
    return rc * rstd, rstd


def residual_layer_norm(x, y, mod, k_gate, g, b, bl, name, next_mod=None):
    t, d = x.shape
    s = t // bl
    tm = _tile(s, 512, 8)
    nt = s // tm
    has_next = next_mod is not None

    def body(*refs):
        x_ref, y_ref, gt_ref, g_ref, b_ref = refs[:5]
        rest = refs[5:]
        if has_next:
            sh_ref, sc_ref, o_ref, r_ref, u_ref = rest
        else:
            o_ref, r_ref = rest
        r = ALPHA * x_ref[...] + (1.0 + gt_ref[0]) * y_ref[...]
        xhat, _ = _layer_norm_stats(r)
        out = xhat * g_ref[...] + b_ref[...]
        o_ref[...] = out
        r_ref[...] = r
        if has_next:
            u_ref[...] = (out * (1.0 + sc_ref[0]) + sh_ref[0]).astype(BF16)

    tok = pl.BlockSpec((tm, d), lambda bb, i: (bb * nt + i, 0))
    vec = pl.BlockSpec((1, d), lambda bb, i: (0, 0))
    in_specs = [tok, tok, _row_spec(d, k_gate), vec, vec]
    args = [x, y, mod, g, b]
    out_shape = [jax.ShapeDtypeStruct((t, d), F32), jax.ShapeDtypeStruct((t, d), F32)]
    out_specs = [tok, tok]
    if has_next:
        in_specs += [_row_spec(d, next_mod[0]), _row_spec(d, next_mod[1])]
        args += [mod if len(next_mod) == 2 else next_mod[2]] * 2
        out_shape.append(jax.ShapeDtypeStruct((t, d), BF16))
        out_specs.append(tok)
    return pl.pallas_call(
        body, name=name, out_shape=out_shape, grid=(bl, nt), in_specs=in_specs, out_specs=out_specs,
        compiler_params=_params("arbitrary", "arbitrary"),
    )(*args)


def loss_head(xo, target, name):
    t, d = xo.shape
    tm = _tile(t, 512, 8)

    def body(x_ref, t_ref, l_ref, dx_ref):
        @pl.when(pl.program_id(0) == 0)
        def _():
            l_ref[...] = jnp.zeros_like(l_ref)

        e = x_ref[...] - t_ref[...]
        l_ref[...] += jnp.sum(e * e, axis=0, keepdims=True) * (0.5 / d)
        dx_ref[...] = e * (1.0 / d)

    tok = pl.BlockSpec((tm, d), lambda i: (i, 0))
    return pl.pallas_call(
        body, name=name,
        out_shape=[jax.ShapeDtypeStruct((1, d), F32), jax.ShapeDtypeStruct((t, d), F32)],
        grid=(t // tm,), in_specs=[tok, tok],
        out_specs=[pl.BlockSpec((1, d), lambda i: (0, 0)), tok],
        compiler_params=_params("arbitrary"),
    )(xo, target)


def sublayer_backward(d_a, bl, name, *, du=None, scale=None, x_in=None, ln=None):
    t, d = d_a.shape
    s = t // bl
    tm = _tile(s, 512, 8)
    nt = s // tm
    has_mod = du is not None
    has_ln = ln is not None
    assert has_mod or has_ln
    assert has_ln or x_in is not None

    def body(*refs):
        refs = list(refs)
        da_ref = refs.pop(0)
        if has_mod:
            du_ref, sc_ref = refs.pop(0), refs.pop(0)
        if has_ln:
            r_ref, y_ref, g_ref, b_ref, gt_ref = (refs.pop(0) for _ in range(5))
        elif has_mod:
            xin_ref = refs.pop(0)
        dx_ref = refs.pop(0)
        if has_ln:
            dy_ref, dg_ref, db_ref, dgt_ref = (refs.pop(0) for _ in range(4))
        if has_mod:
            dsc_ref, dsh_ref = refs.pop(0), refs.pop(0)
        first_tile = pl.program_id(1) == 0
        first_step = jnp.logical_and(pl.program_id(0) == 0, first_tile)

        dout = da_ref[...]
        if has_ln:
            xhat, rstd = _layer_norm_stats(r_ref[...])
        if has_mod:
            duv = du_ref[...]
            dout = dout + duv * (1.0 + sc_ref[0])
            xin = xhat * g_ref[...] + b_ref[...] if has_ln else xin_ref[...]

            @pl.when(first_tile)
            def _():
                dsc_ref[...] = jnp.zeros_like(dsc_ref)
                dsh_ref[...] = jnp.zeros_like(dsh_ref)

            dsc_ref[0] += jnp.sum(duv * xin, axis=0, keepdims=True)
            dsh_ref[0] += jnp.sum(duv, axis=0, keepdims=True)
        if not has_ln:
            dx_ref[...] = dout
            return

        @pl.when(first_step)
        def _():
            dg_ref[...] = jnp.zeros_like(dg_ref)
            db_ref[...] = jnp.zeros_like(db_ref)

        @pl.when(first_tile)
        def _():
            dgt_ref[...] = jnp.zeros_like(dgt_ref)

        dg_ref[...] += jnp.sum(dout * xhat, axis=0, keepdims=True)
        db_ref[...] += jnp.sum(dout, axis=0, keepdims=True)
        dxh = dout * g_ref[...]
        dr = rstd * (dxh - jnp.mean(dxh, axis=-1, keepdims=True) - xhat * jnp.mean(dxh * xhat, axis=-1, keepdims=True))
        dx_ref[...] = ALPHA * dr
        dy_ref[...] = ((1.0 + gt_ref[0]) * dr).astype(BF16)
        dgt_ref[0] += jnp.sum(dr * y_ref[...], axis=0, keepdims=True)

    tok = pl.BlockSpec((tm, d), lambda bb, i: (bb * nt + i, 0))
    vec = pl.BlockSpec((1, d), lambda bb, i: (0, 0))
    seq = pl.BlockSpec((1, 1, d), lambda bb, i: (bb, 0, 0))
    in_specs, args = [tok], [d_a]
    if has_mod:
        in_specs += [tok, _row_spec(d, scale[1])]
        args += [du, scale[0]]
    if has_ln:
        r, y, g, b, gate = ln
        in_specs += [tok, tok, vec, vec, _row_spec(d, gate[1])]
        args += [r, y, g, b, gate[0]]
    elif has_mod:
        in_specs.append(tok)
        args.append(x_in)
    names = ["dx"]
    out_shape, out_specs = [jax.ShapeDtypeStruct((t, d), F32)], [tok]
    if has_ln:
        names += ["dy", "dg", "db", "dgate"]
        out_shape += [jax.ShapeDtypeStruct((t, d), BF16), jax.ShapeDtypeStruct((1, d), F32),
                      jax.ShapeDtypeStruct((1, d), F32), jax.ShapeDtypeStruct((bl, 1, d), F32)]
        out_specs += [tok, vec, vec, seq]
    if has_mod:
        names += ["dscale", "dshift"]
        out_shape += [jax.ShapeDtypeStruct((bl, 1, d), F32)] * 2
        out_specs += [seq, seq]
    outs = pl.pallas_call(
        body, name=name, out_shape=out_shape, grid=(bl, nt), in_specs=in_specs, out_specs=out_specs,
        compiler_params=_params("arbitrary", "arbitrary"),
    )(*args)
    return dict(zip(names, outs))


def _silu(a):
    return a * jax.nn.sigmoid(a)


def silu_rows(a, name):
    def body(a_ref, o_ref):
        o_ref[...] = _silu(a_ref[...]).astype(BF16)

    return pl.pallas_call(body, name=name, out_shape=jax.ShapeDtypeStruct(a.shape, BF16))(a)


def _swiglu_tiles(t, f):
    return _tile(t, 512, 8), _tile(f, 1536)


def swiglu_in(u, wt_gate, wt_up, name):
    t, d = u.shape
    f = wt_gate.shape[0]
    tm, tf = _swiglu_tiles(t, f)

    def body(u_ref, g_ref, w_ref, a_ref, b_ref, h_ref):
        uv = u_ref[...]
        a = _nt(uv, g_ref[...])
        b = _nt(uv, w_ref[...])
        a_ref[...] = a.astype(BF16)
        b_ref[...] = b.astype(BF16)
        h_ref[...] = (_silu(a) * b).astype(BF16)

    w_spec = pl.BlockSpec((tf, d), lambda i, j: (j, 0))
    o_spec = pl.BlockSpec((tm, tf), lambda i, j: (i, j))
    return pl.pallas_call(
        body, name=name,
        out_shape=[jax.ShapeDtypeStruct((t, f), BF16)] * 3,
        grid=(t // tm, f // tf), in_specs=[pl.BlockSpec((tm, d), lambda i, j: (i, 0)), w_spec, w_spec],
        out_specs=[o_spec, o_spec, o_spec], compiler_params=_params("arbitrary", "arbitrary"),
    )(u, wt_gate, wt_up)


def swiglu_out_backward(dy, w_down, a, b, name):
    t, d = dy.shape
    f = w_down.shape[0]
    tm, tf = _swiglu_tiles(t, f)

    def body(dy_ref, w_ref, a_ref, b_ref, da_ref, db_ref):
        dh = _nt(dy_ref[...], w_ref[...])
        av = a_ref[...].astype(F32)
        sig = jax.nn.sigmoid(av)
        da_ref[...] = (dh * b_ref[...].astype(F32) * (sig * (1.0 + av * (1.0 - sig)))).astype(BF16)
        db_ref[...] = (dh * (av * sig)).astype(BF16)

    spec = pl.BlockSpec((tm, tf), lambda i, j: (i, j))
    return pl.pallas_call(
        body, name=name, out_shape=[jax.ShapeDtypeStruct((t, f), BF16)] * 2, grid=(t // tm, f // tf),
        in_specs=[pl.BlockSpec((tm, d), lambda i, j: (i, 0)), pl.BlockSpec((tf, d), lambda i, j: (j, 0)), spec, spec],
        out_specs=[spec, spec], compiler_params=_params("arbitrary", "arbitrary"),
    )(dy, w_down, a, b)


def rope_tables(pos, inv_freq, sign, name):
    t = pos.shape[0]
    tm = _tile(t, 512, 8)

    def body(p_ref, f_ref, s_ref, c_out, s_out):
        ang = p_ref[...] * f_ref[...]
        c_out[...] = jnp.cos(ang)
        s_out[...] = jnp.sin(ang) * s_ref[...]

    vec = pl.BlockSpec((1, LANES), lambda i: (0, 0))
    tab = pl.BlockSpec((tm, LANES), lambda i: (i, 0))
    return pl.pallas_call(
        body, name=name, out_shape=[jax.ShapeDtypeStruct((t, LANES), F32)] * 2, grid=(t // tm,),
        in_specs=[pl.BlockSpec((tm, 1), lambda i: (i, 0)), vec, vec], out_specs=[tab, tab],
        compiler_params=_params("arbitrary"),
    )(pos, inv_freq, sign)


def _rot_half(v):
    lane = lax.broadcasted_iota(jnp.int32, v.shape, v.ndim - 1)
    up = pltpu.roll(v, LANES - MLA_ROPE // 2, v.ndim - 1)
    down = pltpu.roll(v, MLA_ROPE // 2, v.ndim - 1)
    return jnp.where(lane % MLA_ROPE < MLA_ROPE // 2, up, down)


def _rope(v, cos, sin_signed):
    return v * cos + _rot_half(v) * sin_signed


def _rope_transposed(dv, cos, sin_signed):
    return dv * cos + _rot_half(dv * sin_signed)


def rope_slabs(v, cos, sin_signed, out_dtype, name, transposed=False):
    ns, t, _ = v.shape
    tm = _tile(t, 1024, 8)
    fn = _rope_transposed if transposed else _rope

    def body(v_ref, c_ref, s_ref, o_ref):
        for j in range(ns):
            o_ref[j] = fn(v_ref[j].astype(F32), c_ref[...], s_ref[...]).astype(out_dtype)

    tab = pl.BlockSpec((tm, LANES), lambda i: (i, 0))
    spec = pl.BlockSpec((ns, tm, LANES), lambda i: (0, i, 0))
    return pl.pallas_call(
        body, name=name, out_shape=jax.ShapeDtypeStruct(v.shape, out_dtype), grid=(t // tm,),
        in_specs=[spec, tab, tab], out_specs=spec, compiler_params=_params("arbitrary"),
    )(v, cos, sin_signed)


def _rms(x):
    rinv = lax.rsqrt(jnp.mean(x * x, axis=-1, keepdims=True) + NORM_EPS)
    return x * rinv, rinv


def mla_latents_forward(h_in, g_q, g_kv, cos, sin_signed, name):
    t = h_in.shape[0]
    tm = _tile(t, 512, 8)

    def body(h_ref, gq_ref, gkv_ref, c_ref, s_ref, cq_ref, ckv_ref, kr_ref):
        cq_ref[...] = (_rms(h_ref[:, 0:MLA_QR])[0] * gq_ref[...]).astype(BF16)
        ckv_ref[...] = (_rms(h_ref[:, MLA_QR:MLA_QR + MLA_KVR])[0] * gkv_ref[...]).astype(BF16)
        kr_ref[...] = _rope(h_ref[:, MLA_QR + MLA_KVR:], c_ref[...], s_ref[...]).astype(BF16)

    def tok(w):
        return pl.BlockSpec((tm, w), lambda i: (i, 0))

    def vec(w):
        return pl.BlockSpec((1, w), lambda i: (0, 0))

    return pl.pallas_call(
        body, name=name,
        out_shape=[jax.ShapeDtypeStruct((t, MLA_QR), BF16), jax.ShapeDtypeStruct((t, MLA_KVR), BF16),
                   jax.ShapeDtypeStruct((t, LANES), BF16)],
        grid=(t // tm,),
        in_specs=[tok(h_in.shape[1]), vec(MLA_QR), vec(MLA_KVR), tok(LANES), tok(LANES)],
        out_specs=[tok(MLA_QR), tok(MLA_KVR), tok(LANES)],
        compiler_params=_params("arbitrary"),
    )(h_in, g_q, g_kv, cos, sin_signed)


def mla_latents_backward(h_in, dcq, dckv, dkr, g_q, g_kv, cos, sin_signed, name):
    t, w = h_in.shape
    tm = _tile(t, 512, 8)

    def body(h_ref, dcq_ref, dckv_ref, dkr_ref, gq_ref, gkv_ref, c_ref, s_ref, dh_ref, dgq_ref, dgkv_ref):
        @pl.when(pl.program_id(0) == 0)
        def _():
            dgq_ref[...] = jnp.zeros_like(dgq_ref)
            dgkv_ref[...] = jnp.zeros_like(dgkv_ref)

        def rms_bwd(x, dc, g_ref, dg_ref):
            xn, rinv = _rms(x)
            dg_ref[...] += jnp.sum(dc * xn, axis=0, keepdims=True)
            dxn = dc * g_ref[...]
            return rinv * (dxn - xn * jnp.mean(dxn * xn, axis=-1, keepdims=True))

        dq = rms_bwd(h_ref[:, 0:MLA_QR], dcq_ref[...], gq_ref, dgq_ref)
        dkv = rms_bwd(h_ref[:, MLA_QR:MLA_QR + MLA_KVR], dckv_ref[...], gkv_ref, dgkv_ref)
        dr = _rope_transposed(dkr_ref[...], c_ref[...], s_ref[...])
        dh_ref[...] = jnp.concatenate([dq, dkv, dr], axis=1).astype(BF16)

    def tok(ww):
        return pl.BlockSpec((tm, ww), lambda i: (i, 0))

    def vec(ww):
        return pl.BlockSpec((1, ww), lambda i: (0, 0))

    return pl.pallas_call(
        body, name=name,
        out_shape=[jax.ShapeDtypeStruct((t, w), BF16), jax.ShapeDtypeStruct((1, MLA_QR), F32),
                   jax.ShapeDtypeStruct((1, MLA_KVR), F32)],
        grid=(t // tm,),
        in_specs=[tok(w), tok(MLA_QR), tok(MLA_KVR), tok(LANES), vec(MLA_QR), vec(MLA_KVR), tok(LANES), tok(LANES)],
        out_specs=[tok(w), vec(MLA_QR), vec(MLA_KVR)],
        compiler_params=_params("arbitrary"),
    )(h_in, dcq, dckv, dkr, g_q, g_kv, cos, sin_signed)


def _tri(n, lower):
    r = lax.broadcasted_iota(jnp.int32, (n, n), 0)
    c = lax.broadcasted_iota(jnp.int32, (n, n), 1)
    return jnp.where(r >= c if lower else r <= c, 1.0, 0.0).astype(F32)


def _dot_exact(tri, v):
    hi = v.astype(BF16)
    mid = (v - hi.astype(F32)).astype(BF16)
    lo = (v - hi.astype(F32) - mid.astype(F32)).astype(BF16)
    t = tri.astype(BF16)
    return _nn(t, hi) + _nn(t, mid) + _nn(t, lo)


def fox_gate_forward(z, b_f, bl, name):
    t = z.shape[0]
    s = t // bl
    ch = LANES
    n_ch = s // ch

    def body(z_ref, b_ref, f_ref, fs_ref):
        tri = _tri(ch, True)
        carry = jnp.zeros((1, LANES), F32)
        for k in range(n_ch):
            x = z_ref[k * ch:(k + 1) * ch, :] + b_ref[...]
            logf = jnp.minimum(x, 0.0) - jnp.log(1.0 + jnp.exp(-jnp.abs(x)))
            cs = _dot_exact(tri, logf) + carry
            carry = cs[ch - 1:ch, :]
            f_ref[k * ch:(k + 1) * ch, :] = cs
            for h in range(FOX_HEADS):
                fs_ref[h, k * ch:(k + 1) * ch, :] = jnp.broadcast_to(cs[:, h:h + 1], (ch, LANES))

    return pl.pallas_call(
        body, name=name,
        out_shape=[jax.ShapeDtypeStruct((t, LANES), F32), jax.ShapeDtypeStruct((FOX_HEADS, t, LANES), F32)],
        grid=(bl,),
        in_specs=[pl.BlockSpec((s, LANES), lambda b: (b, 0)), pl.BlockSpec((1, LANES), lambda b: (0, 0))],
        out_specs=[pl.BlockSpec((s, LANES), lambda b: (b, 0)),
                   pl.BlockSpec((FOX_HEADS, s, LANES), lambda b: (0, b, 0))],
        compiler_params=_params("arbitrary"),
    )(z, b_f)


def fox_gate_backward(z, b_f, df, bl, name):
    t = z.shape[0]
    s = t // bl
    ch = LANES
    n_ch = s // ch

    def body(z_ref, b_ref, df_ref, dz_ref, db_ref):
        @pl.when(pl.program_id(0) == 0)
        def _():
            db_ref[...] = jnp.zeros_like(db_ref)

        tri = _tri(ch, False)
        carry = jnp.zeros((1, LANES), F32)
        for k in reversed(range(n_ch)):
            cs = _dot_exact(tri, df_ref[k * ch:(k + 1) * ch, :]) + carry
            carry = cs[0:1, :]
            x = z_ref[k * ch:(k + 1) * ch, :] + b_ref[...]
            dz = cs * (1.0 - jax.nn.sigmoid(x))
            dz_ref[k * ch:(k + 1) * ch, :] = dz
            db_ref[...] += jnp.sum(dz, axis=0, keepdims=True)

    tok = pl.BlockSpec((s, LANES), lambda b: (b, 0))
    vec = pl.BlockSpec((1, LANES), lambda b: (0, 0))
    return pl.pallas_call(
        body, name=name,
        out_shape=[jax.ShapeDtypeStruct((t, LANES), F32), jax.ShapeDtypeStruct((1, LANES), F32)],
        grid=(bl,), in_specs=[tok, vec, tok], out_specs=[tok, vec],
        compiler_params=_params("arbitrary"),
    )(z, b_f, df)


NEG_INF = float("-inf")


def _attn_tiles(s):
    return _tile(s, 512, 8)


def attention_forward(kind, ops, bl, scale, name):
    fox = kind == "fox"
    if fox:
        assert math.frexp(scale)[0] == 0.5, "the FoX scale is folded into bf16 queries: it must be a power of two"
        qkv, fq, fk = ops
        t = qkv.shape[1]
        n_pair = FOX_HEADS // 2
    else:
        qn, qr, kn, kr, v = ops
        t = qn.shape[1]
        n_pair = MLA_HEADS // 2
    s = t // bl
    tq = _attn_tiles(s)
    nq = s // tq
    half = LANES // 2

    def body(*refs):
        if fox:
            q_ref, k_ref, v_ref, fq_ref, fk_ref, o_ref, lse_ref, o32_ref = refs
        else:
            qn_ref, qr_ref, kn_ref, kr_ref, v_ref, o_ref, lse_ref = refs
        i = pl.program_id(2)
        row = lax.broadcasted_iota(jnp.int32, (tq, tq), 0)
        col = lax.broadcasted_iota(jnp.int32, (tq, tq), 1)
        heads = []
        for e in range(2):
            sl = slice(e * half, (e + 1) * half)
            if fox:
                heads.append((sl, q_ref[0, :, sl] * jnp.asarray(scale, BF16), None))
            else:
                heads.append((sl, jnp.concatenate([qn_ref[e], qr_ref[0, :, sl], jnp.zeros((tq, half), BF16)], axis=1),
                              None))
        dv = half if fox else LANES

        def wide(stat):
            return jnp.concatenate([stat] * (tq // LANES), axis=1)

        def step(j, carry, masked):
            rows = pl.ds(pl.multiple_of(j * tq, tq), tq)
            new = []
            for e, (sl, qa, qb) in enumerate(heads):
                m, l, acc = carry[e]
                if fox:
                    sc = _nt(qa, k_ref[0, rows, sl]) + wide(fq_ref[e]) - fk_ref[0, j, e:e + 1, :]
                    vv = v_ref[0, rows, sl]
                else:
                    k_cat = jnp.concatenate([kn_ref[e, rows, :], kr_ref[rows, :]], axis=1)
                    sc = _nt(qa, k_cat) * scale
                    vv = v_ref[e, rows, :]
                if masked:
                    sc = jnp.where(row >= col, sc, NEG_INF)
                m_new = jnp.maximum(m, jnp.max(sc, axis=1, keepdims=True))
                p = jnp.exp(sc - m_new)
                a = jnp.exp(m - m_new)
                p_hi = p.astype(BF16)
                if fox:
                    vv = jnp.concatenate([vv, ones], axis=1)
                    acc = a * acc + _nn(p_hi, vv) + _nn((p - p_hi.astype(F32)).astype(BF16), vv)
                else:
                    l = a * l + jnp.sum(p, axis=1, keepdims=True)
                    acc = a * acc + _nn(p_hi, vv)
                new.append((m_new, l, acc))
            return tuple(new)

        ones = jnp.ones((tq, half), BF16)
        acc_w = LANES if fox else dv
        init = (jnp.full((tq, 1), NEG_INF, F32), jnp.zeros((tq, 1), F32), jnp.zeros((tq, acc_w), F32))
        carry = step(i, (init, init), True)
        carry = lax.fori_loop(0, i, lambda j, c: step(j, c, False), carry)
        if fox:
            carry = [(m, acc[:, dv:dv + 1], acc[:, :dv]) for m, _, acc in carry]
        outs = [acc / l for _, l, acc in carry]
        for e, (m, l, _) in enumerate(carry):
            lse_ref[e] = jnp.broadcast_to(m + jnp.log(l), (tq, LANES))
        if fox:
            o32 = jnp.concatenate(outs, axis=1)
            o32_ref[0] = o32
            o_ref[0] = o32.astype(BF16)
        else:
            o_ref[0] = outs[0].astype(BF16)
            o_ref[1] = outs[1].astype(BF16)

    def q_idx(b, g, i):
        return (g, b * nq + i, 0)

    if fox:
        nk = fk.shape[1]
        in_specs = [pl.BlockSpec((1, tq, LANES), q_idx),
                    pl.BlockSpec((1, s, LANES), lambda b, g, i: (n_pair + g, b, 0)),
                    pl.BlockSpec((1, s, LANES), lambda b, g, i: (2 * n_pair + g, b, 0)),
                    pl.BlockSpec((2, tq, LANES), q_idx),
                    pl.BlockSpec((1, nk, 8, tq), lambda b, g, i: (b * n_pair + g, 0, 0, 0))]
        args = [qkv, qkv, qkv, fq, fk]
        o_spec = pl.BlockSpec((1, tq, LANES), q_idx)
    else:
        in_specs = [pl.BlockSpec((2, tq, LANES), q_idx),
                    pl.BlockSpec((1, tq, LANES), q_idx),
                    pl.BlockSpec((2, s, LANES), lambda b, g, i: (g, b, 0)),
                    pl.BlockSpec((s, LANES), lambda b, g, i: (b, 0)),
                    pl.BlockSpec((2, s, LANES), lambda b, g, i: (g, b, 0))]
        args = [qn, qr, kn, kr, v]
        o_spec = pl.BlockSpec((2, tq, LANES), q_idx)
    out_shape = [jax.ShapeDtypeStruct((8, t, LANES), BF16), jax.ShapeDtypeStruct((2 * n_pair, t, LANES), F32)]
    out_specs = [o_spec, pl.BlockSpec((2, tq, LANES), q_idx)]
    if fox:
        out_shape.append(jax.ShapeDtypeStruct((8, t, LANES), F32))
        out_specs.append(o_spec)
    outs = pl.pallas_call(
        body, name=name, out_shape=out_shape, grid=(bl, n_pair, nq), in_specs=in_specs, out_specs=out_specs,
        compiler_params=_params("arbitrary", "arbitrary", "arbitrary"),
    )(*args)
    return (outs[0], outs[1], outs[2] if fox else outs[0])


def attention_backward(kind, ops, o, do, lse, bl, scale, name):
    fox = kind == "fox"
    if fox:
        qkv, fq, fk = ops
        t = qkv.shape[1]
        n_pair = FOX_HEADS // 2
    else:
        qn, qr, kn, kr, v = ops
        t = qn.shape[1]
        n_pair = MLA_HEADS // 2
    s = t // bl
    tq = _attn_tiles(s)
    nq = s // tq
    half = LANES // 2

    def body(*refs):
        if fox:
            (q_ref, k_ref, v_ref, fq_ref, fk_ref, o_ref, do_ref, lse_ref,
             dq_ref, dk_ref, dv_ref, dfk_ref, delta_scr, qt_scr, dot_scr) = refs
        else:
            (qn_ref, qr_ref, kn_ref, kr_ref, v_ref, o_ref, do_ref, lse_ref,
             dqn_ref, dqr_ref, dkn_ref, dv_ref, dkr_ref, delta_scr, qt_scr, qrt_scr, dot_scr) = refs
        g, j = pl.program_id(1), pl.program_id(2)
        row = lax.broadcasted_iota(jnp.int32, (tq, tq), 0)
        col = lax.broadcasted_iota(jnp.int32, (tq, tq), 1)
        krows = pl.ds(pl.multiple_of(j * tq, tq), tq)

        def transposed(v):
            return v.astype(F32).T.astype(BF16)

        def wide(stat):
            return jnp.concatenate([stat] * (tq // LANES), axis=1)

        @pl.when(j == 0)
        def _():
            if fox:
                dq_ref[...] = jnp.zeros_like(dq_ref)
            else:
                dqn_ref[...] = jnp.zeros_like(dqn_ref)
                dqr_ref[...] = jnp.zeros_like(dqr_ref)
            for ii in range(nq):
                rws = slice(ii * tq, (ii + 1) * tq)
                deltas = []
                if fox:
                    prod = do_ref[0, rws, :].astype(F32) * o_ref[0, rws, :].astype(F32)
                    for e in range(2):
                        deltas.append(jnp.sum(prod[:, e * half:(e + 1) * half], axis=1, keepdims=True))
                    qt_scr[ii] = transposed(q_ref[0, rws, :])
                    dot_scr[ii] = transposed(do_ref[0, rws, :])
                else:
                    for e in range(2):
                        prod = do_ref[e, rws, :].astype(F32) * o_ref[e, rws, :].astype(F32)
                        deltas.append(jnp.sum(prod, axis=1, keepdims=True))
                        qt_scr[e, ii] = transposed(qn_ref[e, rws, :])
                        dot_scr[e, ii] = transposed(do_ref[e, rws, :])
                    qrt_scr[ii] = transposed(qr_ref[0, rws, :])
                for e in range(2):
                    delta_scr[e, rws, :] = jnp.broadcast_to(deltas[e], (tq, LANES))

        if fox:
            dfk_ref[...] = jnp.zeros_like(dfk_ref)
        else:
            @pl.when(jnp.logical_and(g == 0, j == 0))
            def _():
                dkr_ref[...] = jnp.zeros_like(dkr_ref)

        heads = []
        for e in range(2):
            sl = slice(e * half, (e + 1) * half)
            if fox:
                heads.append((sl, k_ref[0, :, sl], v_ref[0, :, sl], fk_ref[0, 0, e:e + 1, :]))
            else:
                heads.append((sl, jnp.concatenate([kn_ref[e], kr_ref[krows, :]], axis=1), v_ref[e], None))
        dk_w = dv_w = half if fox else LANES

        def step(i, carry, masked):
            rows = pl.ds(pl.multiple_of(i * tq, tq), tq)
            new = []
            for e, (sl, k_e, v_e, x_e) in enumerate(heads):
                dk_acc, dv_acc, last = carry[e]
                if fox:
                    do_i = do_ref[0, rows, sl]
                    sc = _nt(q_ref[0, rows, sl], k_e) * scale + wide(fq_ref[e, rows, :]) - x_e
                else:
                    do_i = do_ref[e, rows, :]
                    q_cat = jnp.concatenate([qn_ref[e, rows, :], qr_ref[0, rows, sl], jnp.zeros((tq, half), BF16)], axis=1)
                    sc = _nt(q_cat, k_e) * scale
                if masked:
                    sc = jnp.where(row >= col, sc, NEG_INF)
                p = jnp.exp(sc - wide(lse_ref[e, rows, :]))
                dp = _nt(do_i, v_e)
                ds = p * (dp - wide(delta_scr[e, rows, :]))
                dsb = (ds * scale).astype(BF16)
                if fox:
                    fsl = slice(e * half, (e + 1) * half)
                    dv_acc = dv_acc + _nn(dot_scr[i, fsl, :], p.astype(BF16))
                    dk_acc = dk_acc + _nn(qt_scr[i, fsl, :], dsb)
                    dq_ref[0, rows, sl] += _nn(dsb, k_e)
                    last = last - jnp.sum(ds, axis=0, keepdims=True)
                else:
                    dv_acc = dv_acc + _nn(dot_scr[e, i], p.astype(BF16))
                    dk_acc = dk_acc + _nn(qt_scr[e, i], dsb)
                    dq_cat = _nn(dsb, k_e)
                    dqn_ref[e, rows, :] += dq_cat[:, :LANES]
                    dqr_ref[0, rows, sl] += dq_cat[:, LANES:LANES + half]
                    last = last + _nn(qrt_scr[i, e * half:(e + 1) * half, :], dsb)
                new.append((dk_acc, dv_acc, last))
            return tuple(new)

        last0 = jnp.zeros((1, tq), F32) if fox else jnp.zeros((half, tq), F32)
        init = (jnp.zeros((dk_w, tq), F32), jnp.zeros((dv_w, tq), F32), last0)
        carry = step(j, (init, init), True)
        carry = lax.fori_loop(j + 1, nq, lambda i, c: step(i, c, False), carry)
        if fox:
            for e in range(2):
                dfk_ref[0, 0, e:e + 1, :] = carry[e][2]
            dk_ref[0] = jnp.concatenate([carry[0][0], carry[1][0]], axis=0).T.astype(BF16)
            dv_ref[0] = jnp.concatenate([carry[0][1], carry[1][1]], axis=0).T.astype(BF16)
        else:
            for e in range(2):
                dkn_ref[e] = carry[e][0].T.astype(BF16)
                dv_ref[e] = carry[e][1].T.astype(BF16)
            dkr_t = carry[0][2] + carry[1][2]
            dkr_ref[krows, :] += jnp.concatenate([dkr_t, jnp.zeros_like(dkr_t)], axis=0).T

    def whole(b, g, j):
        return (g, b, 0)

    def kblk(b, g, j):
        return (g, b * nq + j, 0)

    if fox:
        in_specs = [pl.BlockSpec((1, s, LANES), whole),
                    pl.BlockSpec((1, tq, LANES), lambda b, g, j: (n_pair + g, b * nq + j, 0)),
                    pl.BlockSpec((1, tq, LANES), lambda b, g, j: (2 * n_pair + g, b * nq + j, 0)),
                    pl.BlockSpec((2, s, LANES), whole),
                    pl.BlockSpec((1, 1, 8, tq), lambda b, g, j: (b * n_pair + g, j, 0, 0)),
                    pl.BlockSpec((1, s, LANES), whole), pl.BlockSpec((1, s, LANES), whole),
                    pl.BlockSpec((2, s, LANES), whole)]
        args = [qkv, qkv, qkv, fq, fk, o, do, lse]
        out_shape = [jax.ShapeDtypeStruct((8, t, LANES), F32), jax.ShapeDtypeStruct((8, t, LANES), BF16),
                     jax.ShapeDtypeStruct((8, t, LANES), BF16), jax.ShapeDtypeStruct(fk.shape, F32)]
        out_specs = [pl.BlockSpec((1, s, LANES), whole), pl.BlockSpec((1, tq, LANES), kblk),
                     pl.BlockSpec((1, tq, LANES), kblk),
                     pl.BlockSpec((1, 1, 8, tq), lambda b, g, j: (b * n_pair + g, j, 0, 0))]
    else:
        pair = pl.BlockSpec((2, s, LANES), whole)
        pair_k = pl.BlockSpec((2, tq, LANES), kblk)
        in_specs = [pair, pl.BlockSpec((1, s, LANES), whole), pair_k,
                    pl.BlockSpec((s, LANES), lambda b, g, j: (b, 0)), pair_k,
                    pair, pair, pair]
        args = [qn, qr, kn, kr, v, o, do, lse]
        out_shape = [jax.ShapeDtypeStruct((8, t, LANES), F32), jax.ShapeDtypeStruct((4, t, LANES), F32),
                     jax.ShapeDtypeStruct((8, t, LANES), BF16), jax.ShapeDtypeStruct((8, t, LANES), BF16),
                     jax.ShapeDtypeStruct((t, LANES), F32)]
        out_specs = [pair, pl.BlockSpec((1, s, LANES), whole), pair_k, pair_k,
                     pl.BlockSpec((s, LANES), lambda b, g, j: (b, 0))]
    t_blocks = pltpu.VMEM((nq, LANES, tq), BF16)
    t_pairs = pltpu.VMEM((2, nq, LANES, tq), BF16)
    scratch = [pltpu.VMEM((2, s, LANES), F32)] + ([t_blocks, t_blocks] if fox else [t_pairs, t_blocks, t_pairs])
    return pl.pallas_call(
        body, name=name, out_shape=out_shape, grid=(bl, n_pair, nq), in_specs=in_specs, out_specs=out_specs,
        scratch_shapes=scratch, compiler_params=_params("arbitrary", "arbitrary", "arbitrary"),
    )(*args)


def adamw(w, g, m, v, name):
    shape = w.shape
    c = shape[-1]
    r = w.size // c
    tr = _tile(r, 512, 8)

    def body(w_ref, g_ref, m_ref, v_ref, d_ref, nm_ref, nv_ref):
        gv = g_ref[...]
        m2 = ADAM_B1 * m_ref[...] + (1.0 - ADAM_B1) * gv
        v2 = ADAM_B2 * v_ref[...] + (1.0 - ADAM_B2) * (gv * gv)
        m_hat = m2 / (1.0 - ADAM_B1 ** ADAM_STEP)
        v_hat = v2 / (1.0 - ADAM_B2 ** ADAM_STEP)
        d_ref[...] = -ADAM_LR * (m_hat / (jnp.sqrt(v_hat) + ADAM_EPS) + ADAM_WD * w_ref[...])
        nm_ref[...] = m2
        nv_ref[...] = v2

    spec = pl.BlockSpec((tr, c), lambda i: (i, 0))
    outs = pl.pallas_call(
        body, name=name, out_shape=[jax.ShapeDtypeStruct((r, c), F32)] * 3, grid=(r // tr,),
        in_specs=[spec] * 4, out_specs=[spec] * 3, compiler_params=_params("arbitrary"),
    )(*(a.reshape(r, c) for a in (w, g, m, v)))
    return tuple(a.reshape(shape) for a in outs)


PACK_COLS = 1024


def _pack_rows(a):
    return a.reshape(-1, PACK_COLS)


def kernel(x, c, positions, mla_w_in, mla_g_q, mla_w_uq, mla_g_kv, mla_w_uk, mla_w_uv, mla_w_o, fox_w_in, fox_b_f, fox_w_o, ada_w, ada_b, ffn_w_gate, ffn_w_up, ffn_w_down, ln_g, ln_b, loss_target, m_mla_w_in, m_mla_g_q, m_mla_w_uq, m_mla_g_kv, m_mla_w_uk, m_mla_w_uv, m_mla_w_o, m_fox_w_in, m_fox_b_f, m_fox_w_o, m_ada_w, m_ada_b, m_ffn_w_gate, m_ffn_w_up, m_ffn_w_down, m_ln_g, m_ln_b, v_mla_w_in, v_mla_g_q, v_mla_w_uq, v_mla_g_kv, v_mla_w_uk, v_mla_w_uv, v_mla_w_o, v_fox_w_in, v_fox_b_f, v_fox_w_o, v_ada_w, v_ada_b, v_ffn_w_gate, v_ffn_w_up, v_ffn_w_down, v_ln_g, v_ln_b):
    bl, s, d = x.shape
    t = bl * s
    ff = ffn_w_gate.shape[-1] * N_DEV
    dev = 4 * lax.axis_index("x") + 2 * lax.axis_index("y") + lax.axis_index("c")
    ada_cols = ada_w.shape[-1]
    fox_in = fox_w_in.shape[-1] * N_DEV
    mla_in = mla_w_in.shape[-1]
    mla_in_pad = mla_in + (-mla_in) % LANES

    def t_last(a):
        return jnp.swapaxes(a, -1, -2)

    local = {
        "mla_w_in": mla_w_in[0],
        "mla_w_uq": t_last(mla_w_uq[0]),
        "mla_w_uk": t_last(mla_w_uk[0]),
        "mla_w_uv": t_last(mla_w_uv[0]),
        "mla_w_o": mla_w_o[0],
        "fox_w_in": t_last(fox_w_in[0]),
        "fox_w_o": fox_w_o[0],
    }
    for i in range(DEPTH):
        local.update({f"gate{i}": t_last(ffn_w_gate[i]), f"up{i}": t_last(ffn_w_up[i]), f"down{i}": ffn_w_down[i]})
    groups = [["mla_w_in", "mla_w_uq", "mla_w_uk", "mla_w_uv", "mla_w_o"],
              ["gate0", "up0", "down0"],
              ["fox_w_in", "fox_w_o"],
              ["gate1", "up1", "down1"]]
    offsets, rows_of, slot_of, group_of = {}, {}, {}, {}
    group_rows = []
    for gi, names in enumerate(groups):
        rows = 0
        for nm in names:
            rows_of[nm] = local[nm].size // PACK_COLS
            slot_of[nm] = rows_of[nm] + (-rows_of[nm]) % 16
            offsets[nm] = rows
            group_of[nm] = gi
            rows += slot_of[nm]
        group_rows.append(rows)

    def slot(nm, rows):
        pad = [(0, 0)] * rows.ndim
        pad[-2] = (0, slot_of[nm] - rows_of[nm])
        return jnp.pad(rows, pad)

    def held_until(block, arrays):
        zero = sum((a.reshape(-1)[0] * 0).astype(F32) for a in jax.tree.leaves(arrays))
        return block + zero.astype(block.dtype)

    def landing(block):
        land = lax.empty((N_DEV,) + block.shape, block.dtype)
        return lax.dynamic_update_slice(land, block[None], (dev, 0, 0))

    packed0 = jnp.concatenate([slot(nm, _pack_rows(local[nm]).astype(BF16)) for nm in groups[0]], axis=0)
    gathered0 = all_gather(packed0, "gather_mla_weights")
    gathered = {nm: gathered0[:, offsets[nm]:offsets[nm] + rows_of[nm], :] for nm in groups[0]}
    gather_started = [None] * len(groups)

    def depart(gi, after):
        blocks = [held_until(_pack_rows(local[nm]).astype(BF16), after) for nm in groups[gi]]
        gather_started[gi] = exchange_start(blocks, [landing(b) for b in blocks], f"gather_group{gi}_start", False)
        return gather_started[gi][4]

    def full(nm, cols):
        return gathered[nm].reshape(-1, cols)

    w_in = jnp.pad(full("mla_w_in", mla_in), ((0, 0), (0, mla_in_pad - mla_in)))
    wt_uq = full("mla_w_uq", MLA_QR).reshape(MLA_HEADS, MLA_NOPE + MLA_ROPE, MLA_QR)
    wt_uq_n = wt_uq[:, :MLA_NOPE].reshape(MLA_HEADS * MLA_NOPE, MLA_QR)
    wt_uq_r = wt_uq[:, MLA_NOPE:].reshape(MLA_HEADS * MLA_ROPE, MLA_QR)
    wt_uk = full("mla_w_uk", MLA_KVR)
    wt_uv = full("mla_w_uv", MLA_KVR)
    w_mo = full("mla_w_o", d)
    wt_gate, wt_up, w_down = [None] * DEPTH, [None] * DEPTH, [None] * DEPTH

    def arrive(gi, after):
        if gi + 1 < len(groups):
            after = depart(gi + 1, after)
        landed = list(exchange_wait(gather_started[gi], after, f"gather_group{gi}_wait", False))
        gathered.update(zip(groups[gi], landed))
        for i in range(DEPTH):
            if group_of[f"gate{i}"] == gi:
                wt_gate[i], wt_up[i], w_down[i] = full(f"gate{i}", d), full(f"up{i}", d), full(f"down{i}", d)

    small = jnp.concatenate([c.reshape(-1, LANES), ln_g.reshape(-1, LANES), ln_b.reshape(-1, LANES)], axis=0)
    small_rows = small.shape[0]
    small = jnp.pad(small, ((0, (-small_rows) % 8), (0, 0)))
    small_all = all_gather(small, "gather_small")
    c_rows = bl * d // LANES
    c_all = small_all[:, :c_rows].reshape(N_DEV * bl, d)
    n_ln = DEPTH * 2
    ln_g_all = small_all[:, c_rows:c_rows + n_ln, :].transpose(1, 0, 2).reshape(DEPTH, 2, 1, d)
    ln_b_all = small_all[:, c_rows + n_ln:c_rows + 2 * n_ln, :].transpose(1, 0, 2).reshape(DEPTH, 2, 1, d)

    c_act = silu_rows(c_all, "silu_c")
    ada_b_loc = lax.dynamic_slice_in_dim(ada_b, dev * ada_cols, ada_cols, axis=1)
    mod_cols = [mm([(c_act, ada_w[i])], trans_b=False, out_dtype=F32, name=f"ada_fwd{i}", bias=ada_b_loc[i][None, :])
                for i in range(DEPTH)]
    mod_all = all_gather(jnp.concatenate(mod_cols, axis=0), "gather_mod")
    mod_all = mod_all.reshape(N_DEV, DEPTH, N_DEV * bl, ada_cols).transpose(1, 2, 0, 3).reshape(DEPTH, N_DEV * bl, 6 * d)
    mod_mine = lax.dynamic_slice_in_dim(mod_all, dev * bl, bl, axis=1)
    mods = [mod_mine[i].reshape(bl * 6, 1, d) for i in range(DEPTH)]
    mods[0] = mods[0] + depart(1, (mod_mine, gathered0))[0, 0]

    half_r = MLA_ROPE // 2
    inv_freq = ROPE_THETA ** (-jnp.arange(half_r, dtype=F32) / half_r)
    inv_freq = jnp.tile(inv_freq, LANES // half_r)[None, :]
    sign = jnp.tile(jnp.concatenate([-jnp.ones((half_r,), F32), jnp.ones((half_r,), F32)]), LANES // MLA_ROPE)[None, :]
    cos_t, sin_t = rope_tables(positions.astype(F32).reshape(t, 1), inv_freq, sign, "rope_tables")

    x2d = x.reshape(t, d)
    g_q, g_kv = mla_g_q.reshape(1, MLA_QR), mla_g_kv.reshape(1, MLA_KVR)
    b_f = jnp.pad(fox_b_f.reshape(1, FOX_HEADS), ((0, 0), (0, LANES - FOX_HEADS)))
    mla_scale = (MLA_NOPE + MLA_ROPE) ** -0.5
    fox_scale = FOX_HD ** -0.5
    tq = _attn_tiles(s)
    nk = s // tq

    saved = []
    u = modulate(x2d, mods[0], 0, 1, bl, "modulate0")
    xin = x2d
    for i in range(DEPTH):
        sv = {"u": u, "x_in": xin}
        if i % 2 == 0:
            h_in = mm([(u, w_in)], trans_b=False, out_dtype=F32, name=f"mla_in{i}")
            c_q, c_kv, k_r = mla_latents_forward(h_in, g_q, g_kv, cos_t, sin_t, f"mla_latents{i}")
            q_n = mm([(c_q, wt_uq_n)], trans_b=True, out_dtype=BF16, out_slab=True, name=f"mla_qn{i}")
            q_r_raw = mm([(c_q, wt_uq_r)], trans_b=True, out_dtype=F32, out_slab=True, name=f"mla_qr{i}")
            q_r = rope_slabs(q_r_raw, cos_t, sin_t, BF16, f"mla_qrope{i}")
            k_n = mm([(c_kv, wt_uk)], trans_b=True, out_dtype=BF16, out_slab=True, name=f"mla_kn{i}")
            v_m = mm([(c_kv, wt_uv)], trans_b=True, out_dtype=BF16, out_slab=True, name=f"mla_v{i}")
            ops = (q_n, q_r, k_n, k_r, v_m)
            o, lse, o_delta = attention_forward("mla", ops, bl, mla_scale, f"mla_attn{i}")
            y = mm([(o, w_mo)], trans_b=False, out_dtype=F32, name=f"mla_out{i}")
            sv.update(h_in=h_in, c_q=c_q, c_kv=c_kv, ops=ops, o=o, lse=lse, o_delta=o_delta)
        else:
            arrive(2, u)
            wt_fox = full("fox_w_in", d)
            wt_qkv = wt_fox[:3 * d]
            wt_f = jnp.pad(wt_fox[3 * d:], ((0, LANES - FOX_HEADS), (0, 0)))
            w_fo = full("fox_w_o", d)
            qkv = mm([(u, wt_qkv)], trans_b=True, out_dtype=BF16, out_slab=True, name=f"fox_qkv{i}")
            z = mm([(u, wt_f)], trans_b=True, out_dtype=F32, name=f"fox_z{i}")
            f_tok, f_q = fox_gate_forward(z, b_f, bl, f"fox_gate{i}")
            f_k = f_tok[:, :FOX_HEADS].reshape(bl, nk, tq, FOX_HEADS // 2, 2).transpose(0, 3, 1, 4, 2)
            f_k = jnp.pad(f_k.reshape(bl * FOX_HEADS // 2, nk, 2, tq), ((0, 0), (0, 0), (0, 6), (0, 0)))
            ops = (qkv, f_q, f_k)
            o, lse, o_delta = attention_forward("fox", ops, bl, fox_scale, f"fox_attn{i}")
            y = mm([(o, w_fo)], trans_b=False, out_dtype=F32, name=f"fox_out{i}")
            sv.update(z=z, ops=ops, o=o, lse=lse, o_delta=o_delta)
        x1, r1, u2 = residual_layer_norm(xin, y, mods[i], 2, ln_g_all[i, 0], ln_b_all[i, 0], bl, f"ln_mix{i}",
                                         next_mod=(3, 4))
        if wt_gate[i] is None:
            arrive(group_of[f"gate{i}"], u2)
        a, bb, h = swiglu_in(u2, wt_gate[i], wt_up[i], f"ffn_in{i}")
        y2 = mm([(h, w_down[i])], trans_b=False, out_dtype=F32, name=f"ffn_down{i}")
        sv.update(y=y, r1=r1, u2=u2, a=a, bb=bb, h=h, y2=y2)
        if i + 1 < DEPTH:
            xin, r2, u = residual_layer_norm(x1, y2, mods[i], 5, ln_g_all[i, 1], ln_b_all[i, 1], bl, f"ln_ffn{i}",
                                             next_mod=(0, 1, mods[i + 1]))
        else:
            xin, r2 = residual_layer_norm(x1, y2, mods[i], 5, ln_g_all[i, 1], ln_b_all[i, 1], bl, f"ln_ffn{i}")
        sv.update(r2=r2)
        saved.append(sv)

    loss_cols, d_x = loss_head(xin, loss_target.reshape(t, d), "loss_head")

    grads_full = {}
    wgrad = functools.partial(mm_tn, out_dtype=BF16)
    dmod = [[None] * 6 for _ in range(DEPTH)]
    dg_ln = [[None, None] for _ in range(DEPTH)]
    db_ln = [[None, None] for _ in range(DEPTH)]
    dg_q = dg_kv = db_f = None
    d_a, du = d_x, None
    scatter_started = [None] * len(groups)

    def scatter_start(gi, after=None):
        gs = [grads_full[nm].reshape(N_DEV, rows_of[nm], PACK_COLS).astype(BF16) for nm in groups[gi]]
        if gi == 0:
            gs = [jnp.concatenate([slot(nm, g) for nm, g in zip(groups[gi], gs)], axis=1)]
        if after is not None:
            gs = [held_until(g, after) for g in gs]
        lands = [landing(lax.dynamic_index_in_dim(g, dev, 0, keepdims=False)) for g in gs]
        scatter_started[gi] = exchange_start(gs, lands, f"scatter_group{gi}_start", True)

    ln_g_bwd = [[ln_g_all[i, k] for k in range(2)] for i in range(DEPTH)]
    for i in reversed(range(DEPTH)):
        sv = saved[i]
        if i + 1 < DEPTH:
            gi = group_of["fox_w_in"]
            scatter_start(gi)
            ln_g_bwd[i][1] = after_token(ln_g_bwd[i][1], scatter_started[gi])
        ln2 = (sv["r2"], sv["y2"], ln_g_bwd[i][1], ln_b_all[i, 1], (mods[i], 5))
        if du is None:
            bw = sublayer_backward(d_a, bl, f"bwd_ln_ffn{i}", ln=ln2)
        else:
            bw = sublayer_backward(d_a, bl, f"bwd_ln_ffn{i}", du=du, scale=(mods[i + 1], 1), ln=ln2)
            dmod[i + 1][0], dmod[i + 1][1] = bw["dshift"], bw["dscale"]
        dmod[i][5], dg_ln[i][1], db_ln[i][1] = bw["dgate"], bw["dg"], bw["db"]
        dy2 = bw["dy"]
        da, dbb = swiglu_out_backward(dy2, w_down[i], sv["a"], sv["bb"], f"bwd_ffn_act{i}")
        du2 = mm([(da, wt_gate[i]), (dbb, wt_up[i])], trans_b=False, out_dtype=F32, name=f"bwd_ffn_du{i}")
        grads_full[f"down{i}"] = wgrad(sv["h"], dy2, name=f"bwd_w_down{i}")
        grads_full[f"gate{i}"] = wgrad(da, sv["u2"], name=f"bwd_w_gate{i}")
        grads_full[f"up{i}"] = wgrad(dbb, sv["u2"], name=f"bwd_w_up{i}")
        gi = group_of[f"gate{i}"]
        scatter_start(gi)
        ln_g_bwd[i][0] = after_token(ln_g_bwd[i][0], scatter_started[gi])
        bw = sublayer_backward(bw["dx"], bl, f"bwd_ln_mix{i}", du=du2, scale=(mods[i], 4),
                               ln=(sv["r1"], sv["y"], ln_g_bwd[i][0], ln_b_all[i, 0], (mods[i], 2)))
        dmod[i][3], dmod[i][4], dmod[i][2] = bw["dshift"], bw["dscale"], bw["dgate"]
        dg_ln[i][0], db_ln[i][0] = bw["dg"], bw["db"]
        d_a, dy = bw["dx"], bw["dy"]
        o, lse, ops = sv["o"], sv["lse"], sv["ops"]
        if i % 2 == 0:
            do = mm([(dy, w_mo)], trans_b=True, out_dtype=BF16, out_slab=True, name=f"bwd_mla_do{i}")
            grads_full["mla_w_o"] = wgrad(o, dy, name=f"bwd_w_mla_o{i}")
            dqn, dqr, dkn, dvm, dkr = attention_backward("mla", ops, sv["o_delta"], do, lse, bl, mla_scale,
                                                         f"bwd_mla_attn{i}")
            dqr = rope_slabs(dqr, cos_t, sin_t, F32, f"bwd_mla_qrope{i}", transposed=True)
            dcq = mm([(dqn, wt_uq_n), (dqr, wt_uq_r)], trans_b=False, out_dtype=F32, name=f"bwd_mla_dcq{i}")
            dckv = mm([(dkn, wt_uk), (dvm, wt_uv)], trans_b=False, out_dtype=F32, name=f"bwd_mla_dckv{i}")
            d_uq_n = wgrad(dqn, sv["c_q"], name=f"bwd_w_uq_n{i}").reshape(MLA_HEADS, MLA_NOPE, MLA_QR)
            d_uq_r = wgrad(dqr, sv["c_q"], name=f"bwd_w_uq_r{i}").reshape(MLA_HEADS, MLA_ROPE, MLA_QR)
            grads_full["mla_w_uq"] = jnp.concatenate([d_uq_n, d_uq_r], axis=1)
            grads_full["mla_w_uk"] = wgrad(dkn, sv["c_kv"], name=f"bwd_w_uk{i}")
            grads_full["mla_w_uv"] = wgrad(dvm, sv["c_kv"], name=f"bwd_w_uv{i}")
            dh_in, dg_q, dg_kv = mla_latents_backward(sv["h_in"], dcq, dckv, dkr, g_q, g_kv, cos_t, sin_t,
                                                      f"bwd_mla_latents{i}")
            du = mm([(dh_in, w_in)], trans_b=True, out_dtype=F32, name=f"bwd_mla_du{i}")
            grads_full["mla_w_in"] = wgrad(sv["u"], dh_in, name=f"bwd_w_mla_in{i}")[:, :mla_in]
        else:
            do = mm([(dy, w_fo)], trans_b=True, out_dtype=BF16, out_slab=True, name=f"bwd_fox_do{i}")
            grads_full["fox_w_o"] = wgrad(o, dy, name=f"bwd_w_fox_o{i}")
            dq, dk, dvf, dfk = attention_backward("fox", ops, sv["o_delta"], do, lse, bl, fox_scale, f"bwd_fox_attn{i}")
            df = dfk[:, :, :2, :].reshape(bl, FOX_HEADS // 2, nk, 2, tq).transpose(0, 2, 4, 1, 3).reshape(t, FOX_HEADS)
            df = jnp.pad(df, ((0, 0), (0, LANES - FOX_HEADS)))
            dz, db_f = fox_gate_backward(sv["z"], b_f, df, bl, f"bwd_fox_gate{i}")
            du = mm([(dq, wt_fox[0:d]), (dk, wt_fox[d:2 * d]), (dvf, wt_fox[2 * d:3 * d]), (dz, wt_f)],
                    trans_b=False, out_dtype=F32, name=f"bwd_fox_du{i}")
            u_f = sv["u"]
            grads_full["fox_w_in"] = jnp.concatenate(
                [wgrad(dq, u_f, name=f"bwd_w_fox_q{i}"), wgrad(dk, u_f, name=f"bwd_w_fox_k{i}"),
                 wgrad(dvf, u_f, name=f"bwd_w_fox_v{i}"), wgrad(dz, u_f, name=f"bwd_w_fox_f{i}")[:FOX_HEADS]], axis=0)
    scatter_start(0)
    bw = sublayer_backward(d_a, bl, "bwd_input", du=du, scale=(after_token(mods[0], scatter_started[0]), 1), x_in=x2d)
    dmod[0][0], dmod[0][1] = bw["dshift"], bw["dscale"]
    grad_x = bw["dx"].reshape(bl, s, d)

    dmod_rows = jnp.concatenate([r.reshape(bl, d) for layer in dmod for r in layer], axis=0)
    dmod_rows = dmod_rows.reshape(DEPTH, 6, bl, d).transpose(0, 2, 1, 3)
    n_mod = dmod_rows.size // LANES
    ln_parts = [dg_ln[i][k] for i in range(DEPTH) for k in range(2)] + [db_ln[i][k] for i in range(DEPTH) for k in range(2)]
    small_g = jnp.concatenate([dmod_rows.reshape(-1, LANES), dg_q.reshape(-1, LANES), dg_kv.reshape(-1, LANES), db_f]
                              + [p.reshape(-1, LANES) for p in ln_parts] + [loss_cols.reshape(-1, LANES)], axis=0)
    n_small = small_g.shape[0]
    small_g = jnp.pad(small_g, ((0, (-n_small) % 8), (0, 0)))
    small_g_all = all_gather(small_g, "gather_small_grads")
    small_sum = sum_leading(small_g_all, "sum_small_grads")
    per_seq = DEPTH * 6 * d // LANES
    dmod_all = small_g_all[:, :n_mod].reshape(N_DEV, DEPTH, bl, 6 * d).transpose(1, 0, 2, 3)
    dmod_all = dmod_all.reshape(DEPTH, N_DEV * bl, 6 * d)
    o1 = n_mod
    grad_g_q = small_sum[o1:o1 + MLA_QR // LANES].reshape(1, MLA_QR)
    o1 += MLA_QR // LANES
    grad_g_kv = small_sum[o1:o1 + MLA_KVR // LANES].reshape(1, MLA_KVR)
    o1 += MLA_KVR // LANES
    grad_b_f = small_sum[o1:o1 + 1, :FOX_HEADS]
    o1 += 1
    n_ln_rows = DEPTH * 2 * d // LANES
    grad_ln_g_full = small_sum[o1:o1 + n_ln_rows].reshape(DEPTH, 2, d)
    grad_ln_b_full = small_sum[o1 + n_ln_rows:o1 + 2 * n_ln_rows].reshape(DEPTH, 2, d)
    loss = jnp.sum(small_sum[o1 + 2 * n_ln_rows:o1 + 2 * n_ln_rows + d // LANES])
    shard = d // N_DEV
    grad_ln_g = lax.dynamic_slice_in_dim(grad_ln_g_full, dev * shard, shard, axis=2)
    grad_ln_b = lax.dynamic_slice_in_dim(grad_ln_b_full, dev * shard, shard, axis=2)
    by_seq = small_g_all[:, :n_mod].reshape(N_DEV, DEPTH, bl, 6 * d // LANES, LANES).transpose(0, 2, 1, 3, 4)
    grad_ada_b = sum_leading(by_seq.reshape(N_DEV * bl, per_seq, LANES), "sum_ada_b").reshape(DEPTH, 6 * d)
    dmod_cols = lax.dynamic_slice_in_dim(dmod_all, dev * ada_cols, ada_cols, axis=2)
    grad_ada_w = jnp.stack([mm_tn(c_act, dmod_cols[i], name=f"bwd_w_ada{i}") for i in range(DEPTH)])

    g_mine = {}

    def scatter_arrive(gi, after):
        landed = exchange_wait(scatter_started[gi], after, f"scatter_group{gi}_wait", True)
        if gi == 0:
            total = sum_leading(landed[0], f"scatter_group{gi}_sum")
            g_mine.update({nm: total[offsets[nm]:offsets[nm] + rows_of[nm]] for nm in groups[gi]})
            return total
        for nm, land in zip(groups[gi], landed):
            g_mine[nm] = sum_leading(land, f"scatter_sum_{nm}")
        return g_mine[groups[gi][-1]]

    after = scatter_started[0][4]
    for gi in reversed(range(1, len(groups))):
        after = scatter_arrive(gi, after)

    def mine(nm, shape):
        return g_mine[nm].reshape(shape)

    def shard_t(nm, a):
        return mine(nm, t_last(a).shape)

    transposed = {"mla_w_uq", "mla_w_uk", "mla_w_uv", "fox_w_in", "ffn_w_gate", "ffn_w_up"}
    grads = {
        "mla_w_in": lambda: mine("mla_w_in", mla_w_in[0].shape)[None],
        "mla_g_q": lambda: grad_g_q,
        "mla_w_uq": lambda: shard_t("mla_w_uq", mla_w_uq[0])[None],
        "mla_g_kv": lambda: grad_g_kv,
        "mla_w_uk": lambda: shard_t("mla_w_uk", mla_w_uk[0])[None],
        "mla_w_uv": lambda: shard_t("mla_w_uv", mla_w_uv[0])[None],
        "mla_w_o": lambda: mine("mla_w_o", mla_w_o[0].shape)[None],
        "fox_w_in": lambda: shard_t("fox_w_in", fox_w_in[0])[None],
        "fox_b_f": lambda: grad_b_f,
        "fox_w_o": lambda: mine("fox_w_o", fox_w_o[0].shape)[None],
        "ada_w": lambda: grad_ada_w,
        "ada_b": lambda: grad_ada_b,
        "ffn_w_gate": lambda: jnp.stack([shard_t(f"gate{i}", ffn_w_gate[i]) for i in range(DEPTH)]),
        "ffn_w_up": lambda: jnp.stack([shard_t(f"up{i}", ffn_w_up[i]) for i in range(DEPTH)]),
        "ffn_w_down": lambda: jnp.stack([mine(f"down{i}", ffn_w_down[i].shape) for i in range(DEPTH)]),
        "ln_g": lambda: grad_ln_g,
        "ln_b": lambda: grad_ln_b,
    }
    weights = dict(mla_w_in=mla_w_in, mla_g_q=mla_g_q, mla_w_uq=mla_w_uq, mla_g_kv=mla_g_kv, mla_w_uk=mla_w_uk,
                   mla_w_uv=mla_w_uv, mla_w_o=mla_w_o, fox_w_in=fox_w_in, fox_b_f=fox_b_f, fox_w_o=fox_w_o,
                   ada_w=ada_w, ada_b=ada_b, ffn_w_gate=ffn_w_gate, ffn_w_up=ffn_w_up, ffn_w_down=ffn_w_down,
                   ln_g=ln_g, ln_b=ln_b)
    first = dict(mla_w_in=m_mla_w_in, mla_g_q=m_mla_g_q, mla_w_uq=m_mla_w_uq, mla_g_kv=m_mla_g_kv, mla_w_uk=m_mla_w_uk,
                 mla_w_uv=m_mla_w_uv, mla_w_o=m_mla_w_o, fox_w_in=m_fox_w_in, fox_b_f=m_fox_b_f, fox_w_o=m_fox_w_o,
                 ada_w=m_ada_w, ada_b=m_ada_b, ffn_w_gate=m_ffn_w_gate, ffn_w_up=m_ffn_w_up, ffn_w_down=m_ffn_w_down,
                 ln_g=m_ln_g, ln_b=m_ln_b)
    second = dict(mla_w_in=v_mla_w_in, mla_g_q=v_mla_g_q, mla_w_uq=v_mla_w_uq, mla_g_kv=v_mla_g_kv, mla_w_uk=v_mla_w_uk,
                  mla_w_uv=v_mla_w_uv, mla_w_o=v_mla_w_o, fox_w_in=v_fox_w_in, fox_b_f=v_fox_b_f, fox_w_o=v_fox_w_o,
                  ada_w=v_ada_w, ada_b=v_ada_b, ffn_w_gate=v_ffn_w_gate, ffn_w_up=v_ffn_w_up, ffn_w_down=v_ffn_w_down,
                  ln_g=v_ln_g, ln_b=v_ln_b)
    order = list(weights)
    last = [nm for nm in order if group_of.get(nm) == 0]
    updated = {}
    for nm in [nm for nm in order if nm not in last] + last:
        if last and nm == last[0]:
            scatter_arrive(0, after)
        lay = t_last if nm in transposed else (lambda a: a)
        w = lay(weights[nm])
        g = grads[nm]().reshape(w.shape)
        delta, new_m, new_v = adamw(w, g, lay(first[nm]), lay(second[nm]), f"adamw_{nm}")
        updated[nm] = (lay(g), lay(delta), lay(new_m), lay(new_v))
        after = new_v
    return (loss, grad_x, *(updated[nm][k] for k in range(4) for nm in order))
```

```python
import functools
import math

import jax
import jax.numpy as jnp
from jax import lax
from jax.experimental import pallas as pl
from jax.experimental.pallas import tpu as pltpu

F32 = jnp.float32
BF16 = jnp.bfloat16
LANES = 128
N_DEV = 8
VMEM_LIMIT_BYTES = 56 * 1024 * 1024

DEPTH = 2
MLA_HEADS = 8
MLA_NOPE = 128
MLA_ROPE = 64
MLA_V = 128
MLA_QR = 256
MLA_KVR = 256
ROPE_THETA = 10000.0
FOX_HEADS = 16
FOX_HD = 64
ALPHA = (2.0 * DEPTH) ** 0.25
NORM_EPS = 1e-5
ADAM_LR = 0.001
ADAM_B1 = 0.9
ADAM_B2 = 0.999
ADAM_EPS = 1e-08
ADAM_WD = 0.01
ADAM_STEP = 10

MESH = pl.DeviceIdType.MESH


def _params(*sem):
    return pltpu.CompilerParams(dimension_semantics=sem, vmem_limit_bytes=VMEM_LIMIT_BYTES)


def _tile(n, cap, mult=LANES):
    if n <= cap:
        return n
    best = None
    for t in range(mult, cap + 1, mult):
        if n % t == 0:
            best = t
    assert best is not None, (n, cap, mult)
    return best


def _dot(a, b, dims):
    return lax.dot_general(a, b, (dims, ((), ())), preferred_element_type=F32)


def _nn(a, b):
    return _dot(a, b, ((1,), (0,)))


def _nt(a, b):
    return _dot(a, b, ((1,), (1,)))


def _tn(a, b):
    return _dot(a, b, ((0,), (0,)))


def _me():
    return lax.axis_index("x"), lax.axis_index("y"), lax.axis_index("c")


def all_gather(x_loc, name):
    r, c = x_loc.shape

    def body(x_ref, out_ref, send_sems, recv_sems, local_sem):
        x, y, cc = _me()
        me, sibling = (x, y, cc), (x, y, 1 - cc)
        chips = [(1 - x, y), (x, 1 - y), (1 - x, 1 - y)]

        def rows(px, py, pc):
            return out_ref.at[4 * px + 2 * py + pc]

        def copy(k, block, to, src=None):
            return pltpu.make_async_remote_copy(
                src_ref=rows(*block) if src is None else src, dst_ref=rows(*block),
                send_sem=send_sems.at[k], recv_sem=recv_sems.at[k], device_id=to, device_id_type=MESH)

        mine = pltpu.make_async_copy(x_ref, rows(*me), local_sem)
        mine.start()
        first = [copy(0, me, sibling, src=x_ref)]
        first += [copy(1 + j, me, (*chip, cc), src=x_ref) for j, chip in enumerate(chips)]
        for cp in first:
            cp.start()
        passed = [copy(4 + j, (*chip, cc), sibling) for j, chip in enumerate(chips)]
        for j, chip in enumerate(chips):
            copy(1 + j, (*chip, cc), me).wait_recv()
            passed[j].start()
        copy(0, sibling, me).wait_recv()
        for j, chip in enumerate(chips):
            copy(4 + j, (*chip, 1 - cc), me).wait_recv()
        for cp in first + passed:
            cp.wait_send()
        mine.wait()

    return pl.pallas_call(
        body, name=name,
        out_shape=jax.ShapeDtypeStruct((N_DEV, r, c), x_loc.dtype),
        in_specs=[pl.BlockSpec(memory_space=pl.ANY)],
        out_specs=pl.BlockSpec(memory_space=pl.ANY),
        scratch_shapes=[pltpu.SemaphoreType.DMA((7,)), pltpu.SemaphoreType.DMA((7,)), pltpu.SemaphoreType.DMA(())],
    )(x_loc)


HBM_SPEC = pl.BlockSpec(memory_space=pltpu.HBM)
SEM_SPEC = pl.BlockSpec(memory_space=pltpu.SEMAPHORE)
N_PEERS = N_DEV - 1


def _peer(k):
    x, y, c = _me()
    return (1 - x if k & 4 else x, 1 - y if k & 2 else y, 1 - c if k & 1 else c)


def _exchange_copies(src_refs, land_refs, send_sems, recv_sems, scatter):
    x, y, c = _me()
    mine = 4 * x + 2 * y + c
    copies = []
    for n, (src_ref, land_ref) in enumerate(zip(src_refs, land_refs)):
        for k in range(1, N_DEV):
            px, py, pc = _peer(k)
            src = src_ref.at[4 * px + 2 * py + pc] if scatter else src_ref
            sem = n * N_PEERS + k - 1
            copies.append(pltpu.make_async_remote_copy(
                src_ref=src, dst_ref=land_ref.at[mine], send_sem=send_sems.at[sem], recv_sem=recv_sems.at[sem],
                device_id=(px, py, pc), device_id_type=MESH))
    return copies


def exchange_start(srcs, lands, name, scatter):
    n = len(srcs)

    def body(*refs):
        send_sems, recv_sems = refs[2 * n], refs[2 * n + 1]
        for cp in _exchange_copies(refs[:n], refs[n:2 * n], send_sems, recv_sems, scatter):
            cp.start()
        token = refs[-1]
        token[...] = jnp.zeros_like(token)

    outs = pl.pallas_call(
        body, name=name,
        out_shape=(pltpu.SemaphoreType.DMA((n * N_PEERS,)), pltpu.SemaphoreType.DMA((n * N_PEERS,)),
                   *(pltpu.HBM(a.shape, a.dtype) for a in (*srcs, *lands)), jax.ShapeDtypeStruct((8, LANES), F32)),
        in_specs=(HBM_SPEC,) * (2 * n),
        out_specs=(SEM_SPEC, SEM_SPEC, *((HBM_SPEC,) * (2 * n)), pl.BlockSpec(memory_space=pltpu.VMEM)),
        input_output_aliases={i: 2 + i for i in range(2 * n)},
        compiler_params=pltpu.CompilerParams(has_side_effects=pltpu.SideEffectType.DATAFLOW_SIDE_EFFECTING),
    )(*(pltpu.with_memory_space_constraint(a, pltpu.HBM) for a in (*srcs, *lands)))
    return outs[0], outs[1], outs[2:2 + n], outs[2 + n:2 + 2 * n], outs[-1]


def exchange_wait(started, after, name, scatter):
    send_sems, recv_sems, srcs, lands, _ = started
    n = len(srcs)

    def body(*refs):
        send_sems, recv_sems = refs[2 * n], refs[2 * n + 1]
        for cp in _exchange_copies(refs[:n], refs[n:2 * n], send_sems, recv_sems, scatter):
            cp.wait_send()
            cp.wait_recv()

    outs = pl.pallas_call(
        body, name=name,
        out_shape=tuple(pltpu.HBM(a.shape, a.dtype) for a in (*srcs, *lands)),
        in_specs=(*((HBM_SPEC,) * (2 * n)), SEM_SPEC, SEM_SPEC, pl.BlockSpec(memory_space=pl.ANY)),
        out_specs=(HBM_SPEC,) * (2 * n), input_output_aliases={i: i for i in range(2 * n)},
        compiler_params=pltpu.CompilerParams(has_side_effects=pltpu.SideEffectType.DATAFLOW_SIDE_EFFECTING),
    )(*srcs, *lands, send_sems, recv_sems, after)
    return outs[n:]


def after_token(small, started):
    return small + started[4][0, 0]


def sum_leading(x, name):
    n, r, c = x.shape
    tr = _tile(r, 512, 16)

    def body(x_ref, o_ref):
        acc = x_ref[0].astype(F32)
        for k in range(1, n):
            acc = acc + x_ref[k].astype(F32)
        o_ref[...] = acc

    return pl.pallas_call(
        body, name=name,
        out_shape=jax.ShapeDtypeStruct((r, c), F32),
        grid=(r // tr,),
        in_specs=[pl.BlockSpec((n, tr, c), lambda i: (0, i, 0))],
        out_specs=pl.BlockSpec((tr, c), lambda i: (i, 0)),
        compiler_params=_params("arbitrary"),
    )(x)


MM_VMEM_BUDGET = 36 * 1024 * 1024
GRID_STEP_AS_BYTES = 1 << 20


def _mm_tiles(m, n, a_row_bytes, b_col_bytes, out_bytes):
    tms = [c for c in (2048, 1024, 512, 256, 128, 64, 32, 16, 8) if m % c == 0] or [m]
    tns = [c for c in range(LANES, min(n, 2048) + 1, LANES) if n % c == 0] or [n]
    best = None
    for tm in tms:
        for tn in tns:
            vmem = 2 * (tm * a_row_bytes + tn * b_col_bytes) + 2 * tm * tn * out_bytes + tm * tn * 4
            if vmem > MM_VMEM_BUDGET:
                continue
            steps = (m // tm) * (n // tn)
            cost = steps * GRID_STEP_AS_BYTES + (m // tm) * n * b_col_bytes + m * a_row_bytes
            if best is None or cost < best[0]:
                best = (cost, tm, tn)
    assert best is not None, (m, n, a_row_bytes, b_col_bytes)
    return best[1], best[2]


def mm(pairs, *, trans_b, out_dtype, name, out_slab=False, bias=None):
    a0 = pairs[0][0]
    m = a0.shape[1] if a0.ndim == 3 else a0.shape[0]
    n = pairs[0][1].shape[0] if trans_b else pairs[0][1].shape[1]
    a_row_bytes = sum((b.shape[1] if trans_b else b.shape[0]) * a.dtype.itemsize for a, b in pairs)
    b_col_bytes = sum((b.shape[1] if trans_b else b.shape[0]) * b.dtype.itemsize for _, b in pairs)
    tm, tn = _mm_tiles(m, n, a_row_bytes, b_col_bytes, jnp.dtype(out_dtype).itemsize)
    slabs = [a.ndim == 3 for a, _ in pairs]
    n_pairs = len(pairs)

    def body(*refs):
        o_ref = refs[-1]
        acc = bias_ref = None
        if bias is not None:
            bias_ref = refs[2 * n_pairs]
        for i in range(n_pairs):
            a_ref, b_ref = refs[2 * i], refs[2 * i + 1]
            if slabs[i]:
                a = jnp.concatenate([a_ref[s].astype(BF16) for s in range(a_ref.shape[0])], axis=1)
            else:
                a = a_ref[...].astype(BF16)
            b = b_ref[...].astype(BF16)
            part = _nt(a, b) if trans_b else _nn(a, b)
            acc = part if acc is None else acc + part
        if bias_ref is not None:
            acc = acc + bias_ref[...]
        if out_slab:
            for s in range(tn // LANES):
                o_ref[s] = acc[:, s * LANES:(s + 1) * LANES].astype(out_dtype)
        else:
            o_ref[...] = acc.astype(out_dtype)

    in_specs, args = [], []
    for (a, b), slab in zip(pairs, slabs):
        if slab:
            in_specs.append(pl.BlockSpec((a.shape[0], tm, LANES), lambda i, j: (0, i, 0)))
        else:
            in_specs.append(pl.BlockSpec((tm, a.shape[1]), lambda i, j: (i, 0)))
        if trans_b:
            in_specs.append(pl.BlockSpec((tn, b.shape[1]), lambda i, j: (j, 0)))
        else:
            in_specs.append(pl.BlockSpec((b.shape[0], tn), lambda i, j: (0, j)))
        args += [a, b]
    if bias is not None:
        in_specs.append(pl.BlockSpec((1, tn), lambda i, j: (0, j)))
        args.append(bias)
    if out_slab:
        out_shape = jax.ShapeDtypeStruct((n // LANES, m, LANES), out_dtype)
        out_spec = pl.BlockSpec((tn // LANES, tm, LANES), lambda i, j: (j, i, 0))
    else:
        out_shape = jax.ShapeDtypeStruct((m, n), out_dtype)
        out_spec = pl.BlockSpec((tm, tn), lambda i, j: (i, j))
    return pl.pallas_call(
        body, name=name, out_shape=out_shape, grid=(m // tm, n // tn),
        in_specs=in_specs, out_specs=out_spec,
        compiler_params=_params("arbitrary", "arbitrary"),
    )(*args)


def mm_tn(a, b, *, name, out_dtype=F32, tk_cap=1536, tn_cap=1024, tm_cap=512):
    slab = a.ndim == 3
    m = a.shape[1] if slab else a.shape[0]
    k = a.shape[0] * LANES if slab else a.shape[1]
    n = b.shape[1]
    tk = _tile(k, tk_cap)
    tn = _tile(n, tn_cap)
    tm = _tile(m, tm_cap, 8)
    n_steps = m // tm

    def body(a_ref, b_ref, o_ref, acc_ref):
        step = pl.program_id(2)

        @pl.when(step == 0)
        def _():
            acc_ref[...] = jnp.zeros_like(acc_ref)

        bb = b_ref[...].astype(BF16)
        if slab:
            for s in range(tk // LANES):
                acc_ref[s * LANES:(s + 1) * LANES, :] += _tn(a_ref[s].astype(BF16), bb)
        else:
            acc_ref[...] += _tn(a_ref[...].astype(BF16), bb)

        @pl.when(step == n_steps - 1)
        def _():
            o_ref[...] = acc_ref[...].astype(out_dtype)

    if slab:
        a_spec = pl.BlockSpec((tk // LANES, tm, LANES), lambda i, j, t: (i, t, 0))
    else:
        a_spec = pl.BlockSpec((tm, tk), lambda i, j, t: (t, i))
    return pl.pallas_call(
        body, name=name, out_shape=jax.ShapeDtypeStruct((k, n), out_dtype), grid=(k // tk, n // tn, n_steps),
        in_specs=[a_spec, pl.BlockSpec((tm, tn), lambda i, j, t: (t, j))],
        out_specs=pl.BlockSpec((tk, tn), lambda i, j, t: (i, j)),
        scratch_shapes=[pltpu.VMEM((tk, tn), F32)],
        compiler_params=_params("arbitrary", "arbitrary", "arbitrary"),
    )(a, b)


def _row_spec(d, k):
    return pl.BlockSpec((1, 1, d), lambda b, i: (6 * b + k, 0, 0))


def modulate(x, mod, k_shift, k_scale, bl, name):
    t, d = x.shape
    s = t // bl
    tm = _tile(s, 512, 8)
    nt = s // tm

    def body(x_ref, sh_ref, sc_ref, o_ref):
        o_ref[...] = (x_ref[...] * (1.0 + sc_ref[0]) + sh_ref[0]).astype(BF16)

    return pl.pallas_call(
        body, name=name, out_shape=jax.ShapeDtypeStruct((t, d), BF16), grid=(bl, nt),
        in_specs=[pl.BlockSpec((tm, d), lambda b, i: (b * nt + i, 0)), _row_spec(d, k_shift), _row_spec(d, k_scale)],
        out_specs=pl.BlockSpec((tm, d), lambda b, i: (b * nt + i, 0)),
        compiler_params=_params("arbitrary", "arbitrary"),
    )(x, mod, mod)


def _layer_norm_stats(r):
    mu = jnp.mean(r, axis=-1, keepdims=True)
    rc = r - mu
    var = jnp.mean(rc * rc, axis=-1, keepdims=True)
    rstd = lax.rsqrt(var + NORM_EPS)
    return rc * rstd, rstd


def residual_layer_norm(x, y, mod, k_gate, g, b, bl, name, next_mod=None):
    t, d = x.shape
    s = t // bl
    tm = _tile(s, 512, 8)
    nt = s // tm
    has_next = next_mod is not None

    def body(*refs):
        x_ref, y_ref, gt_ref, g_ref, b_ref = refs[:5]
        rest = refs[5:]
        if has_next:
            sh_ref, sc_ref, o_ref, r_ref, u_ref = rest
        else:
            o_ref, r_ref = rest
        r = ALPHA * x_ref[...] + (1.0 + gt_ref[0]) * y_ref[...]
        xhat, _ = _layer_norm_stats(r)
        out = xhat * g_ref[...] + b_ref[...]
        o_ref[...] = out
        r_ref[...] = r
        if has_next:
            u_ref[...] = (out * (1.0 + sc_ref[0]) + sh_ref[0]).astype(BF16)

    tok = pl.BlockSpec((tm, d), lambda bb, i: (bb * nt + i, 0))
    vec = pl.BlockSpec((1, d), lambda bb, i: (0, 0))
    in_specs = [tok, tok, _row_spec(d, k_gate), vec, vec]
    args = [x, y, mod, g, b]
    out_shape = [jax.ShapeDtypeStruct((t, d), F32), jax.ShapeDtypeStruct((t, d), F32)]
    out_specs = [tok, tok]
    if has_next:
        in_specs += [_row_spec(d, next_mod[0]), _row_spec(d, next_mod[1])]
        args += [mod if len(next_mod) == 2 else next_mod[2]] * 2
        out_shape.append(jax.ShapeDtypeStruct((t, d), BF16))
        out_specs.append(tok)
    return pl.pallas_call(
        body, name=name, out_shape=out_shape, grid=(bl, nt), in_specs=in_specs, out_specs=out_specs,
        compiler_params=_params("arbitrary", "arbitrary"),
    )(*args)


def loss_head(xo, target, name):
    t, d = xo.shape
    tm = _tile(t, 512, 8)

    def body(x_ref, t_ref, l_ref, dx_ref):
        @pl.when(pl.program_id(0) == 0)
        def _():
            l_ref[...] = jnp.zeros_like(l_ref)

        e = x_ref[...] - t_ref[...]
        l_ref[...] += jnp.sum(e * e, axis=0, keepdims=True) * (0.5 / d)
        dx_ref[...] = e * (1.0 / d)

    tok = pl.BlockSpec((tm, d), lambda i: (i, 0))
    return pl.pallas_call(
        body, name=name,
        out_shape=[jax.ShapeDtypeStruct((1, d), F32), jax.ShapeDtypeStruct((t, d), F32)],
        grid=(t // tm,), in_specs=[tok, tok],
        out_specs=[pl.BlockSpec((1, d), lambda i: (0, 0)), tok],
        compiler_params=_params("arbitrary"),
    )(xo, target)


def sublayer_backward(d_a, bl, name, *, du=None, scale=None, x_in=None, ln=None):
    t, d = d_a.shape
    s = t // bl
    tm = _tile(s, 512, 8)
    nt = s // tm
    has_mod = du is not None
    has_ln = ln is not None
    assert has_mod or has_ln
    assert has_ln or x_in is not None

    def body(*refs):
        refs = list(refs)
        da_ref = refs.pop(0)
        if has_mod:
            du_ref, sc_ref = refs.pop(0), refs.pop(0)
        if has_ln:
            r_ref, y_ref, g_ref, b_ref, gt_ref = (refs.pop(0) for _ in range(5))
        elif has_mod:
            xin_ref = refs.pop(0)
        dx_ref = refs.pop(0)
        if has_ln:
            dy_ref, dg_ref, db_ref, dgt_ref = (refs.pop(0) for _ in range(4))
        if has_mod:
            dsc_ref, dsh_ref = refs.pop(0), refs.pop(0)
        first_tile = pl.program_id(1) == 0
        first_step = jnp.logical_and(pl.program_id(0) == 0, first_tile)

        dout = da_ref[...]
        if has_ln:
            xhat, rstd = _layer_norm_stats(r_ref[...])
        if has_mod:
            duv = du_ref[...]
            dout = dout + duv * (1.0 + sc_ref[0])
            xin = xhat * g_ref[...] + b_ref[...] if has_ln else xin_ref[...]

            @pl.when(first_tile)
            def _():
                dsc_ref[...] = jnp.zeros_like(dsc_ref)
                dsh_ref[...] = jnp.zeros_like(dsh_ref)

            dsc_ref[0] += jnp.sum(duv * xin, axis=0, keepdims=True)
            dsh_ref[0] += jnp.sum(duv, axis=0, keepdims=True)
        if not has_ln:
            dx_ref[...] = dout
            return

        @pl.when(first_step)
        def _():
            dg_ref[...] = jnp.zeros_like(dg_ref)
            db_ref[...] = jnp.zeros_like(db_ref)

        @pl.when(first_tile)
        def _():
            dgt_ref[...] = jnp.zeros_like(dgt_ref)

        dg_ref[...] += jnp.sum(dout * xhat, axis=0, keepdims=True)
        db_ref[...] += jnp.sum(dout, axis=0, keepdims=True)
        dxh = dout * g_ref[...]
        dr = rstd * (dxh - jnp.mean(dxh, axis=-1, keepdims=True) - xhat * jnp.mean(dxh * xhat, axis=-1, keepdims=True))
        dx_ref[...] = ALPHA * dr
        dy_ref[...] = ((1.0 + gt_ref[0]) * dr).astype(BF16)
        dgt_ref[0] += jnp.sum(dr * y_ref[...], axis=0, keepdims=True)

    tok = pl.BlockSpec((tm, d), lambda bb, i: (bb * nt + i, 0))
    vec = pl.BlockSpec((1, d), lambda bb, i: (0, 0))
    seq = pl.BlockSpec((1, 1, d), lambda bb, i: (bb, 0, 0))
    in_specs, args = [tok], [d_a]
    if has_mod:
        in_specs += [tok, _row_spec(d, scale[1])]
        args += [du, scale[0]]
    if has_ln:
        r, y, g, b, gate = ln
        in_specs += [tok, tok, vec, vec, _row_spec(d, gate[1])]
        args += [r, y, g, b, gate[0]]
    elif has_mod:
        in_specs.append(tok)
        args.append(x_in)
    names = ["dx"]
    out_shape, out_specs = [jax.ShapeDtypeStruct((t, d), F32)], [tok]
    if has_ln:
        names += ["dy", "dg", "db", "dgate"]
        out_shape += [jax.ShapeDtypeStruct((t, d), BF16), jax.ShapeDtypeStruct((1, d), F32),
                      jax.ShapeDtypeStruct((1, d), F32), jax.ShapeDtypeStruct((bl, 1, d), F32)]
        out_specs += [tok, vec, vec, seq]
    if has_mod:
        names += ["dscale", "dshift"]
        out_shape += [jax.ShapeDtypeStruct((bl, 1, d), F32)] * 2
        out_specs += [seq, seq]
    outs = pl.pallas_call(
        body, name=name, out_shape=out_shape, grid=(bl, nt), in_specs=in_specs, out_specs=out_specs,
        compiler_params=_params("arbitrary", "arbitrary"),
    )(*args)
    return dict(zip(names, outs))


def _silu(a):
    return a * jax.nn.sigmoid(a)


def silu_rows(a, name):
    def body(a_ref, o_ref):
        o_ref[...] = _silu(a_ref[...]).astype(BF16)

    return pl.pallas_call(body, name=name, out_shape=jax.ShapeDtypeStruct(a.shape, BF16))(a)


def _swiglu_tiles(t, f):
    return _tile(t, 512, 8), _tile(f, 1536)


def swiglu_in(u, wt_gate, wt_up, name):
    t, d = u.shape
    f = wt_gate.shape[0]
    tm, tf = _swiglu_tiles(t, f)

    def body(u_ref, g_ref, w_ref, a_ref, b_ref, h_ref):
        uv = u_ref[...]
        a = _nt(uv, g_ref[...])
        b = _nt(uv, w_ref[...])
        a_ref[...] = a.astype(BF16)
        b_ref[...] = b.astype(BF16)
        h_ref[...] = (_silu(a) * b).astype(BF16)

    w_spec = pl.BlockSpec((tf, d), lambda i, j: (j, 0))
    o_spec = pl.BlockSpec((tm, tf), lambda i, j: (i, j))
    return pl.pallas_call(
        body, name=name,
        out_shape=[jax.ShapeDtypeStruct((t, f), BF16)] * 3,
        grid=(t // tm, f // tf), in_specs=[pl.BlockSpec((tm, d), lambda i, j: (i, 0)), w_spec, w_spec],
        out_specs=[o_spec, o_spec, o_spec], compiler_params=_params("arbitrary", "arbitrary"),
    )(u, wt_gate, wt_up)


def swiglu_out_backward(dy, w_down, a, b, name):
    t, d = dy.shape
    f = w_down.shape[0]
    tm, tf = _swiglu_tiles(t, f)

    def body(dy_ref, w_ref, a_ref, b_ref, da_ref, db_ref):
        dh = _nt(dy_ref[...], w_ref[...])
        av = a_ref[...].astype(F32)
        sig = jax.nn.sigmoid(av)
        da_ref[...] = (dh * b_ref[...].astype(F32) * (sig * (1.0 + av * (1.0 - sig)))).astype(BF16)
        db_ref[...] = (dh * (av * sig)).astype(BF16)

    spec = pl.BlockSpec((tm, tf), lambda i, j: (i, j))
    return pl.pallas_call(
        body, name=name, out_shape=[jax.ShapeDtypeStruct((t, f), BF16)] * 2, grid=(t // tm, f // tf),
        in_specs=[pl.BlockSpec((tm, d), lambda i, j: (i, 0)), pl.BlockSpec((tf, d), lambda i, j: (j, 0)), spec, spec],
        out_specs=[spec, spec], compiler_params=_params("arbitrary", "arbitrary"),
    )(dy, w_down, a, b)


def rope_tables(pos, inv_freq, sign, name):
    t = pos.shape[0]
    tm = _tile(t, 512, 8)

    def body(p_ref, f_ref, s_ref, c_out, s_out):
        ang = p_ref[...] * f_ref[...]
        c_out[...] = jnp.cos(ang)
        s_out[...] = jnp.sin(ang) * s_ref[...]

    vec = pl.BlockSpec((1, LANES), lambda i: (0, 0))
    tab = pl.BlockSpec((tm, LANES), lambda i: (i, 0))
    return pl.pallas_call(
        body, name=name, out_shape=[jax.ShapeDtypeStruct((t, LANES), F32)] * 2, grid=(t // tm,),
        in_specs=[pl.BlockSpec((tm, 1), lambda i: (i, 0)), vec, vec], out_specs=[tab, tab],
        compiler_params=_params("arbitrary"),
    )(pos, inv_freq, sign)


def _rot_half(v):
    lane = lax.broadcasted_iota(jnp.int32, v.shape, v.ndim - 1)
    up = pltpu.roll(v, LANES - MLA_ROPE // 2, v.ndim - 1)
    down = pltpu.roll(v, MLA_ROPE // 2, v.ndim - 1)
    return jnp.where(lane % MLA_ROPE < MLA_ROPE // 2, up, down)


def _rope(v, cos, sin_signed):
    return v * cos + _rot_half(v) * sin_signed


def _rope_transposed(dv, cos, sin_signed):
    return dv * cos + _rot_half(dv * sin_signed)


def rope_slabs(v, cos, sin_signed, out_dtype, name, transposed=False):
    ns, t, _ = v.shape
    tm = _tile(t, 1024, 8)
    fn = _rope_transposed if transposed else _rope

    def body(v_ref, c_ref, s_ref, o_ref):
        for j in range(ns):
            o_ref[j] = fn(v_ref[j].astype(F32), c_ref[...], s_ref[...]).astype(out_dtype)

    tab = pl.BlockSpec((tm, LANES), lambda i: (i, 0))
    spec = pl.BlockSpec((ns, tm, LANES), lambda i: (0, i, 0))
    return pl.pallas_call(
        body, name=name, out_shape=jax.ShapeDtypeStruct(v.shape, out_dtype), grid=(t // tm,),
        in_specs=[spec, tab, tab], out_specs=spec, compiler_params=_params("arbitrary"),
    )(v, cos, sin_signed)


def _rms(x):
    rinv = lax.rsqrt(jnp.mean(x * x, axis=-1, keepdims=True) + NORM_EPS)
    return x * rinv, rinv


def mla_latents_forward(h_in, g_q, g_kv, cos, sin_signed, name):
    t = h_in.shape[0]
    tm = _tile(t, 512, 8)

    def body(h_ref, gq_ref, gkv_ref, c_ref, s_ref, cq_ref, ckv_ref, kr_ref):
        cq_ref[...] = (_rms(h_ref[:, 0:MLA_QR])[0] * gq_ref[...]).astype(BF16)
        ckv_ref[...] = (_rms(h_ref[:, MLA_QR:MLA_QR + MLA_KVR])[0] * gkv_ref[...]).astype(BF16)
        kr_ref[...] = _rope(h_ref[:, MLA_QR + MLA_KVR:], c_ref[...], s_ref[...]).astype(BF16)

    def tok(w):
        return pl.BlockSpec((tm, w), lambda i: (i, 0))

    def vec(w):
        return pl.BlockSpec((1, w), lambda i: (0, 0))

    return pl.pallas_call(
        body, name=name,
        out_shape=[jax.ShapeDtypeStruct((t, MLA_QR), BF16), jax.ShapeDtypeStruct((t, MLA_KVR), BF16),
                   jax.ShapeDtypeStruct((t, LANES), BF16)],
        grid=(t // tm,),
        in_specs=[tok(h_in.shape[1]), vec(MLA_QR), vec(MLA_KVR), tok(LANES), tok(LANES)],
        out_specs=[tok(MLA_QR), tok(MLA_KVR), tok(LANES)],
        compiler_params=_params("arbitrary"),
    )(h_in, g_q, g_kv, cos, sin_signed)


def mla_latents_backward(h_in, dcq, dckv, dkr, g_q, g_kv, cos, sin_signed, name):
    t, w = h_in.shape
    tm = _tile(t, 512, 8)

    def body(h_ref, dcq_ref, dckv_ref, dkr_ref, gq_ref, gkv_ref, c_ref, s_ref, dh_ref, dgq_ref, dgkv_ref):
        @pl.when(pl.program_id(0) == 0)
        def _():
            dgq_ref[...] = jnp.zeros_like(dgq_ref)
            dgkv_ref[...] = jnp.zeros_like(dgkv_ref)

        def rms_bwd(x, dc, g_ref, dg_ref):
            xn, rinv = _rms(x)
            dg_ref[...] += jnp.sum(dc * xn, axis=0, keepdims=True)
            dxn = dc * g_ref[...]
            return rinv * (dxn - xn * jnp.mean(dxn * xn, axis=-1, keepdims=True))

        dq = rms_bwd(h_ref[:, 0:MLA_QR], dcq_ref[...], gq_ref, dgq_ref)
        dkv = rms_bwd(h_ref[:, MLA_QR:MLA_QR + MLA_KVR], dckv_ref[...], gkv_ref, dgkv_ref)
        dr = _rope_transposed(dkr_ref[...], c_ref[...], s_ref[...])
        dh_ref[...] = jnp.concatenate([dq, dkv, dr], axis=1).astype(BF16)

    def tok(ww):
        return pl.BlockSpec((tm, ww), lambda i: (i, 0))

    def vec(ww):
        return pl.BlockSpec((1, ww), lambda i: (0, 0))

    return pl.pallas_call(
        body, name=name,
        out_shape=[jax.ShapeDtypeStruct((t, w), BF16), jax.ShapeDtypeStruct((1, MLA_QR), F32),
                   jax.ShapeDtypeStruct((1, MLA_KVR), F32)],
        grid=(t // tm,),
        in_specs=[tok(w), tok(MLA_QR), tok(MLA_KVR), tok(LANES), vec(MLA_QR), vec(MLA_KVR), tok(LANES), tok(LANES)],
        out_specs=[tok(w), vec(MLA_QR), vec(MLA_KVR)],
        compiler_params=_params("arbitrary"),
    )(h_in, dcq, dckv, dkr, g_q, g_kv, cos, sin_signed)


def _tri(n, lower):
    r = lax.broadcasted_iota(jnp.int32, (n, n), 0)
    c = lax.broadcasted_iota(jnp.int32, (n, n), 1)
    return jnp.where(r >= c if lower else r <= c, 1.0, 0.0).astype(F32)


def _dot_exact(tri, v):
    hi = v.astype(BF16)
    mid = (v - hi.astype(F32)).astype(BF16)
    lo = (v - hi.astype(F32) - mid.astype(F32)).astype(BF16)
    t = tri.astype(BF16)
    return _nn(t, hi) + _nn(t, mid) + _nn(t, lo)


def fox_gate_forward(z, b_f, bl, name):
    t = z.shape[0]
    s = t // bl
    ch = LANES
    n_ch = s // ch

    def body(z_ref, b_ref, f_ref, fs_ref):
        tri = _tri(ch, True)
        carry = jnp.zeros((1, LANES), F32)
        for k in range(n_ch):
            x = z_ref[k * ch:(k + 1) * ch, :] + b_ref[...]
            logf = jnp.minimum(x, 0.0) - jnp.log(1.0 + jnp.exp(-jnp.abs(x)))
            cs = _dot_exact(tri, logf) + carry
            carry = cs[ch - 1:ch, :]
            f_ref[k * ch:(k + 1) * ch, :] = cs
            for h in range(FOX_HEADS):
                fs_ref[h, k * ch:(k + 1) * ch, :] = jnp.broadcast_to(cs[:, h:h + 1], (ch, LANES))

    return pl.pallas_call(
        body, name=name,
        out_shape=[jax.ShapeDtypeStruct((t, LANES), F32), jax.ShapeDtypeStruct((FOX_HEADS, t, LANES), F32)],
        grid=(bl,),
        in_specs=[pl.BlockSpec((s, LANES), lambda b: (b, 0)), pl.BlockSpec((1, LANES), lambda b: (0, 0))],
        out_specs=[pl.BlockSpec((s, LANES), lambda b: (b, 0)),
                   pl.BlockSpec((FOX_HEADS, s, LANES), lambda b: (0, b, 0))],
        compiler_params=_params("arbitrary"),
    )(z, b_f)


def fox_gate_backward(z, b_f, df, bl, name):
    t = z.shape[0]
    s = t // bl
    ch = LANES
    n_ch = s // ch

    def body(z_ref, b_ref, df_ref, dz_ref, db_ref):
        @pl.when(pl.program_id(0) == 0)
        def _():
            db_ref[...] = jnp.zeros_like(db_ref)

        tri = _tri(ch, False)
        carry = jnp.zeros((1, LANES), F32)
        for k in reversed(range(n_ch)):
            cs = _dot_exact(tri, df_ref[k * ch:(k + 1) * ch, :]) + carry
            carry = cs[0:1, :]
            x = z_ref[k * ch:(k + 1) * ch, :] + b_ref[...]
            dz = cs * (1.0 - jax.nn.sigmoid(x))
            dz_ref[k * ch:(k + 1) * ch, :] = dz
            db_ref[...] += jnp.sum(dz, axis=0, keepdims=True)

    tok = pl.BlockSpec((s, LANES), lambda b: (b, 0))
    vec = pl.BlockSpec((1, LANES), lambda b: (0, 0))
    return pl.pallas_call(
        body, name=name,
        out_shape=[jax.ShapeDtypeStruct((t, LANES), F32), jax.ShapeDtypeStruct((1, LANES), F32)],
        grid=(bl,), in_specs=[tok, vec, tok], out_specs=[tok, vec],
        compiler_params=_params("arbitrary"),
    )(z, b_f, df)


NEG_INF = float("-inf")


def _attn_tiles(s):
    return _tile(s, 512, 8)


def attention_forward(kind, ops, bl, scale, name):
    fox = kind == "fox"
    if fox:
        assert math.frexp(scale)[0] == 0.5, "the FoX scale is folded into bf16 queries: it must be a power of two"
        qkv, fq, fk = ops
        t = qkv.shape[1]
        n_pair = FOX_HEADS // 2
    else:
        qn, qr, kn, kr, v = ops
        t = qn.shape[1]
        n_pair = MLA_HEADS // 2
    s = t // bl
    tq = _attn_tiles(s)
    nq = s // tq
    half = LANES // 2

    def body(*refs):
        if fox:
            q_ref, k_ref, v_ref, fq_ref, fk_ref, o_ref, lse_ref, o32_ref = refs
        else:
            qn_ref, qr_ref, kn_ref, kr_ref, v_ref, o_ref, lse_ref = refs
        i = pl.program_id(2)
        row = lax.broadcasted_iota(jnp.int32, (tq, tq), 0)
        col = lax.broadcasted_iota(jnp.int32, (tq, tq), 1)
        heads = []
        for e in range(2):
            sl = slice(e * half, (e + 1) * half)
            if fox:
                heads.append((sl, q_ref[0, :, sl] * jnp.asarray(scale, BF16), None))
            else:
                heads.append((sl, jnp.concatenate([qn_ref[e], qr_ref[0, :, sl], jnp.zeros((tq, half), BF16)], axis=1),
                              None))
        dv = half if fox else LANES

        def wide(stat):
            return jnp.concatenate([stat] * (tq // LANES), axis=1)

        def step(j, carry, masked):
            rows = pl.ds(pl.multiple_of(j * tq, tq), tq)
            new = []
            for e, (sl, qa, qb) in enumerate(heads):
                m, l, acc = carry[e]
                if fox:
                    sc = _nt(qa, k_ref[0, rows, sl]) + wide(fq_ref[e]) - fk_ref[0, j, e:e + 1, :]
                    vv = v_ref[0, rows, sl]
                else:
                    k_cat = jnp.concatenate([kn_ref[e, rows, :], kr_ref[rows, :]], axis=1)
                    sc = _nt(qa, k_cat) * scale
                    vv = v_ref[e, rows, :]
                if masked:
                    sc = jnp.where(row >= col, sc, NEG_INF)
                m_new = jnp.maximum(m, jnp.max(sc, axis=1, keepdims=True))
                p = jnp.exp(sc - m_new)
                a = jnp.exp(m - m_new)
                p_hi = p.astype(BF16)
                if fox:
                    vv = jnp.concatenate([vv, ones], axis=1)
                    acc = a * acc + _nn(p_hi, vv) + _nn((p - p_hi.astype(F32)).astype(BF16), vv)
                else:
                    l = a * l + jnp.sum(p, axis=1, keepdims=True)
                    acc = a * acc + _nn(p_hi, vv)
                new.append((m_new, l, acc))
            return tuple(new)

        ones = jnp.ones((tq, half), BF16)
        acc_w = LANES if fox else dv
        init = (jnp.full((tq, 1), NEG_INF, F32), jnp.zeros((tq, 1), F32), jnp.zeros((tq, acc_w), F32))
        carry = step(i, (init, init), True)
        carry = lax.fori_loop(0, i, lambda j, c: step(j, c, False), carry)
        if fox:
            carry = [(m, acc[:, dv:dv + 1], acc[:, :dv]) for m, _, acc in carry]
        outs = [acc / l for _, l, acc in carry]
        for e, (m, l, _) in enumerate(carry):
            lse_ref[e] = jnp.broadcast_to(m + jnp.log(l), (tq, LANES))
        if fox:
            o32 = jnp.concatenate(outs, axis=1)
            o32_ref[0] = o32
            o_ref[0] = o32.astype(BF16)
        else:
            o_ref[0] = outs[0].astype(BF16)
            o_ref[1] = outs[1].astype(BF16)

    def q_idx(b, g, i):
        return (g, b * nq + i, 0)

    if fox:
        nk = fk.shape[1]
        in_specs = [pl.BlockSpec((1, tq, LANES), q_idx),
                    pl.BlockSpec((1, s, LANES), lambda b, g, i: (n_pair + g, b, 0)),
                    pl.BlockSpec((1, s, LANES), lambda b, g, i: (2 * n_pair + g, b, 0)),
                    pl.BlockSpec((2, tq, LANES), q_idx),
                    pl.BlockSpec((1, nk, 8, tq), lambda b, g, i: (b * n_pair + g, 0, 0, 0))]
        args = [qkv, qkv, qkv, fq, fk]
        o_spec = pl.BlockSpec((1, tq, LANES), q_idx)
    else:
        in_specs = [pl.BlockSpec((2, tq, LANES), q_idx),
                    pl.BlockSpec((1, tq, LANES), q_idx),
                    pl.BlockSpec((2, s, LANES), lambda b, g, i: (g, b, 0)),
                    pl.BlockSpec((s, LANES), lambda b, g, i: (b, 0)),
                    pl.BlockSpec((2, s, LANES), lambda b, g, i: (g, b, 0))]
        args = [qn, qr, kn, kr, v]
        o_spec = pl.BlockSpec((2, tq, LANES), q_idx)
    out_shape = [jax.ShapeDtypeStruct((8, t, LANES), BF16), jax.ShapeDtypeStruct((2 * n_pair, t, LANES), F32)]
    out_specs = [o_spec, pl.BlockSpec((2, tq, LANES), q_idx)]
    if fox:
        out_shape.append(jax.ShapeDtypeStruct((8, t, LANES), F32))
        out_specs.append(o_spec)
    outs = pl.pallas_call(
        body, name=name, out_shape=out_shape, grid=(bl, n_pair, nq), in_specs=in_specs, out_specs=out_specs,
        compiler_params=_params("arbitrary", "arbitrary", "arbitrary"),
    )(*args)
    return (outs[0], outs[1], outs[2] if fox else outs[0])


def attention_backward(kind, ops, o, do, lse, bl, scale, name):
    fox = kind == "fox"
    if fox:
        assert math.frexp(scale)[0] == 0.5, "the FoX scale is folded into bf16 queries: it must be a power of two"
        qkv, fq, fk = ops
        t = qkv.shape[1]
        n_pair = FOX_HEADS // 2
    else:
        qn, qr, kn, kr, v = ops
        t = qn.shape[1]
        n_pair = MLA_HEADS // 2
    s = t // bl
    tq = _attn_tiles(s)
    nq = s // tq
    half = LANES // 2

    def body(*refs):
        if fox:
            (q_ref, k_ref, v_ref, fq_ref, fk_ref, o_ref, do_ref, lse_ref,
             dq_ref, dk_ref, dv_ref, dfk_ref, delta_scr, qt_scr, dot_scr) = refs
        else:
            (qn_ref, qr_ref, kn_ref, kr_ref, v_ref, o_ref, do_ref, lse_ref,
             dqn_ref, dqr_ref, dkn_ref, dv_ref, dkr_ref, delta_scr, qt_scr, qrt_scr, dot_scr) = refs
        g, j = pl.program_id(1), pl.program_id(2)
        row = lax.broadcasted_iota(jnp.int32, (tq, tq), 0)
        col = lax.broadcasted_iota(jnp.int32, (tq, tq), 1)
        krows = pl.ds(pl.multiple_of(j * tq, tq), tq)
        q_scale = jnp.asarray(scale, BF16)

        def transposed(v):
            return v.astype(F32).T.astype(BF16)

        def wide(stat):
            return jnp.concatenate([stat] * (tq // LANES), axis=1)

        @pl.when(j == 0)
        def _():
            if fox:
                dq_ref[...] = jnp.zeros_like(dq_ref)
            else:
                dqn_ref[...] = jnp.zeros_like(dqn_ref)
                dqr_ref[...] = jnp.zeros_like(dqr_ref)
            for ii in range(nq):
                rws = slice(ii * tq, (ii + 1) * tq)
                deltas = []
                if fox:
                    prod = do_ref[0, rws, :].astype(F32) * o_ref[0, rws, :].astype(F32)
                    for e in range(2):
                        deltas.append(jnp.sum(prod[:, e * half:(e + 1) * half], axis=1, keepdims=True))
                    qt_scr[ii] = transposed(q_ref[0, rws, :] * q_scale)
                    dot_scr[ii] = transposed(do_ref[0, rws, :])
                else:
                    for e in range(2):
                        prod = do_ref[e, rws, :].astype(F32) * o_ref[e, rws, :].astype(F32)
                        deltas.append(jnp.sum(prod, axis=1, keepdims=True))
                        qt_scr[e, ii] = transposed(qn_ref[e, rws, :])
                        dot_scr[e, ii] = transposed(do_ref[e, rws, :])
                    qrt_scr[ii] = transposed(qr_ref[0, rws, :])
                for e in range(2):
                    delta_scr[e, rws, :] = jnp.broadcast_to(deltas[e], (tq, LANES))

        if fox:
            dfk_ref[...] = jnp.zeros_like(dfk_ref)
        else:
            @pl.when(jnp.logical_and(g == 0, j == 0))
            def _():
                dkr_ref[...] = jnp.zeros_like(dkr_ref)

        heads = []
        for e in range(2):
            sl = slice(e * half, (e + 1) * half)
            if fox:
                heads.append((sl, k_ref[0, :, sl], v_ref[0, :, sl], fk_ref[0, 0, e:e + 1, :]))
            else:
                heads.append((sl, jnp.concatenate([kn_ref[e], kr_ref[krows, :]], axis=1), v_ref[e], None))
        dk_w = dv_w = half if fox else LANES

        def step(i, carry, masked):
            rows = pl.ds(pl.multiple_of(i * tq, tq), tq)
            new = []
            for e, (sl, k_e, v_e, x_e) in enumerate(heads):
                dk_acc, dv_acc, last = carry[e]
                if fox:
                    do_i = do_ref[0, rows, sl]
                    sc = _nt(q_ref[0, rows, sl] * q_scale, k_e) + wide(fq_ref[e, rows, :]) - x_e
                else:
                    do_i = do_ref[e, rows, :]
                    q_cat = jnp.concatenate([qn_ref[e, rows, :], qr_ref[0, rows, sl], jnp.zeros((tq, half), BF16)], axis=1)
                    sc = _nt(q_cat, k_e) * scale
                if masked:
                    sc = jnp.where(row >= col, sc, NEG_INF)
                p = jnp.exp(sc - wide(lse_ref[e, rows, :]))
                dp = _nt(do_i, v_e)
                ds = p * (dp - wide(delta_scr[e, rows, :]))
                dsb = ds.astype(BF16) if fox else (ds * scale).astype(BF16)
                if fox:
                    fsl = slice(e * half, (e + 1) * half)
                    dv_acc = dv_acc + _nn(dot_scr[i, fsl, :], p.astype(BF16))
                    dk_acc = dk_acc + _nn(qt_scr[i, fsl, :], dsb)
                    dq_ref[0, rows, sl] += _nn(dsb, k_e) * scale
                    last = last - jnp.sum(ds, axis=0, keepdims=True)
                else:
                    dv_acc = dv_acc + _nn(dot_scr[e, i], p.astype(BF16))
                    dk_acc = dk_acc + _nn(qt_scr[e, i], dsb)
                    dq_cat = _nn(dsb, k_e)
                    dqn_ref[e, rows, :] += dq_cat[:, :LANES]
                    dqr_ref[0, rows, sl] += dq_cat[:, LANES:LANES + half]
                    last = last + _nn(qrt_scr[i, e * half:(e + 1) * half, :], dsb)
                new.append((dk_acc, dv_acc, last))
            return tuple(new)

        last0 = jnp.zeros((1, tq), F32) if fox else jnp.zeros((half, tq), F32)
        init = (jnp.zeros((dk_w, tq), F32), jnp.zeros((dv_w, tq), F32), last0)
        carry = step(j, (init, init), True)
        carry = lax.fori_loop(j + 1, nq, lambda i, c: step(i, c, False), carry)
        if fox:
            for e in range(2):
                dfk_ref[0, 0, e:e + 1, :] = carry[e][2]
            dk_ref[0] = jnp.concatenate([carry[0][0], carry[1][0]], axis=0).T.astype(BF16)
            dv_ref[0] = jnp.concatenate([carry[0][1], carry[1][1]], axis=0).T.astype(BF16)
        else:
            for e in range(2):
                dkn_ref[e] = carry[e][0].T.astype(BF16)
                dv_ref[e] = carry[e][1].T.astype(BF16)
            dkr_t = carry[0][2] + carry[1][2]
            dkr_ref[krows, :] += jnp.concatenate([dkr_t, jnp.zeros_like(dkr_t)], axis=0).T

    def whole(b, g, j):
        return (g, b, 0)

    def kblk(b, g, j):
        return (g, b * nq + j, 0)

    if fox:
        in_specs = [pl.BlockSpec((1, s, LANES), whole),
                    pl.BlockSpec((1, tq, LANES), lambda b, g, j: (n_pair + g, b * nq + j, 0)),
                    pl.BlockSpec((1, tq, LANES), lambda b, g, j: (2 * n_pair + g, b * nq + j, 0)),
                    pl.BlockSpec((2, s, LANES), whole),
                    pl.BlockSpec((1, 1, 8, tq), lambda b, g, j: (b * n_pair + g, j, 0, 0)),
                    pl.BlockSpec((1, s, LANES), whole), pl.BlockSpec((1, s, LANES), whole),
                    pl.BlockSpec((2, s, LANES), whole)]
        args = [qkv, qkv, qkv, fq, fk, o, do, lse]
        out_shape = [jax.ShapeDtypeStruct((8, t, LANES), F32), jax.ShapeDtypeStruct((8, t, LANES), BF16),
                     jax.ShapeDtypeStruct((8, t, LANES), BF16), jax.ShapeDtypeStruct(fk.shape, F32)]
        out_specs = [pl.BlockSpec((1, s, LANES), whole), pl.BlockSpec((1, tq, LANES), kblk),
                     pl.BlockSpec((1, tq, LANES), kblk),
                     pl.BlockSpec((1, 1, 8, tq), lambda b, g, j: (b * n_pair + g, j, 0, 0))]
    else:
        pair = pl.BlockSpec((2, s, LANES), whole)
        pair_k = pl.BlockSpec((2, tq, LANES), kblk)
        in_specs = [pair, pl.BlockSpec((1, s, LANES), whole), pair_k,
                    pl.BlockSpec((s, LANES), lambda b, g, j: (b, 0)), pair_k,
                    pair, pair, pair]
        args = [qn, qr, kn, kr, v, o, do, lse]
        out_shape = [jax.ShapeDtypeStruct((8, t, LANES), F32), jax.ShapeDtypeStruct((4, t, LANES), F32),
                     jax.ShapeDtypeStruct((8, t, LANES), BF16), jax.ShapeDtypeStruct((8, t, LANES), BF16),
                     jax.ShapeDtypeStruct((t, LANES), F32)]
        out_specs = [pair, pl.BlockSpec((1, s, LANES), whole), pair_k, pair_k,
                     pl.BlockSpec((s, LANES), lambda b, g, j: (b, 0))]
    t_blocks = pltpu.VMEM((nq, LANES, tq), BF16)
    t_pairs = pltpu.VMEM((2, nq, LANES, tq), BF16)
    scratch = [pltpu.VMEM((2, s, LANES), F32)] + ([t_blocks, t_blocks] if fox else [t_pairs, t_blocks, t_pairs])
    return pl.pallas_call(
        body, name=name, out_shape=out_shape, grid=(bl, n_pair, nq), in_specs=in_specs, out_specs=out_specs,
        scratch_shapes=scratch, compiler_params=_params("arbitrary", "arbitrary", "arbitrary"),
    )(*args)


def adamw(w, g, m, v, name):
    shape = w.shape
    c = shape[-1]
    r = w.size // c
    tr = _tile(r, 512, 8)

    def body(w_ref, g_ref, m_ref, v_ref, d_ref, nm_ref, nv_ref):
        gv = g_ref[...]
        m2 = ADAM_B1 * m_ref[...] + (1.0 - ADAM_B1) * gv
        v2 = ADAM_B2 * v_ref[...] + (1.0 - ADAM_B2) * (gv * gv)
        m_hat = m2 / (1.0 - ADAM_B1 ** ADAM_STEP)
        v_hat = v2 / (1.0 - ADAM_B2 ** ADAM_STEP)
        d_ref[...] = -ADAM_LR * (m_hat / (jnp.sqrt(v_hat) + ADAM_EPS) + ADAM_WD * w_ref[...])
        nm_ref[...] = m2
        nv_ref[...] = v2

    spec = pl.BlockSpec((tr, c), lambda i: (i, 0))
    outs = pl.pallas_call(
        body, name=name, out_shape=[jax.ShapeDtypeStruct((r, c), F32)] * 3, grid=(r // tr,),
        in_specs=[spec] * 4, out_specs=[spec] * 3, compiler_params=_params("arbitrary"),
    )(*(a.reshape(r, c) for a in (w, g, m, v)))
    return tuple(a.reshape(shape) for a in outs)


PACK_COLS = 1024


def _pack_rows(a):
    return a.reshape(-1, PACK_COLS)


def kernel(x, c, positions, mla_w_in, mla_g_q, mla_w_uq, mla_g_kv, mla_w_uk, mla_w_uv, mla_w_o, fox_w_in, fox_b_f, fox_w_o, ada_w, ada_b, ffn_w_gate, ffn_w_up, ffn_w_down, ln_g, ln_b, loss_target, m_mla_w_in, m_mla_g_q, m_mla_w_uq, m_mla_g_kv, m_mla_w_uk, m_mla_w_uv, m_mla_w_o, m_fox_w_in, m_fox_b_f, m_fox_w_o, m_ada_w, m_ada_b, m_ffn_w_gate, m_ffn_w_up, m_ffn_w_down, m_ln_g, m_ln_b, v_mla_w_in, v_mla_g_q, v_mla_w_uq, v_mla_g_kv, v_mla_w_uk, v_mla_w_uv, v_mla_w_o, v_fox_w_in, v_fox_b_f, v_fox_w_o, v_ada_w, v_ada_b, v_ffn_w_gate, v_ffn_w_up, v_ffn_w_down, v_ln_g, v_ln_b):
    bl, s, d = x.shape
    t = bl * s
    ff = ffn_w_gate.shape[-1] * N_DEV
    dev = 4 * lax.axis_index("x") + 2 * lax.axis_index("y") + lax.axis_index("c")
    ada_cols = ada_w.shape[-1]
    mla_in = mla_w_in.shape[-1]
    mla_in_pad = mla_in + (-mla_in) % LANES

    def t_last(a):
        return jnp.swapaxes(a, -1, -2)

    local = {
        "mla_w_in": mla_w_in[0],
        "mla_w_uq": t_last(mla_w_uq[0]),
        "mla_w_uk": t_last(mla_w_uk[0]),
        "mla_w_uv": t_last(mla_w_uv[0]),
        "mla_w_o": mla_w_o[0],
        "fox_w_in": t_last(fox_w_in[0]),
        "fox_w_o": fox_w_o[0],
    }
    for i in range(DEPTH):
        local.update({f"gate{i}": t_last(ffn_w_gate[i]), f"up{i}": t_last(ffn_w_up[i]), f"down{i}": ffn_w_down[i]})
    groups = [["mla_w_in", "mla_w_uq", "mla_w_uk", "mla_w_uv", "mla_w_o"],
              ["gate0", "up0", "down0"],
              ["fox_w_in", "fox_w_o"],
              ["gate1", "up1", "down1"]]
    offsets, rows_of, slot_of, group_of = {}, {}, {}, {}
    group_rows = []
    for gi, names in enumerate(groups):
        rows = 0
        for nm in names:
            rows_of[nm] = local[nm].size // PACK_COLS
            slot_of[nm] = rows_of[nm] + (-rows_of[nm]) % 16
            offsets[nm] = rows
            group_of[nm] = gi
            rows += slot_of[nm]
        group_rows.append(rows)

    def slot(nm, rows):
        pad = [(0, 0)] * rows.ndim
        pad[-2] = (0, slot_of[nm] - rows_of[nm])
        return jnp.pad(rows, pad)

    def held_until(block, arrays):
        zero = sum((a.reshape(-1)[0] * 0).astype(F32) for a in jax.tree.leaves(arrays))
        return block + zero.astype(block.dtype)

    def landing(block):
        land = lax.empty((N_DEV,) + block.shape, block.dtype)
        return lax.dynamic_update_slice(land, block[None], (dev, 0, 0))

    packed0 = jnp.concatenate([slot(nm, _pack_rows(local[nm]).astype(BF16)) for nm in groups[0]], axis=0)
    gathered0 = all_gather(packed0, "gather_mla_weights")
    gathered = {nm: gathered0[:, offsets[nm]:offsets[nm] + rows_of[nm], :] for nm in groups[0]}
    gather_started = [None] * len(groups)

    def depart(gi, after):
        blocks = [held_until(_pack_rows(local[nm]).astype(BF16), after) for nm in groups[gi]]
        gather_started[gi] = exchange_start(blocks, [landing(b) for b in blocks], f"gather_group{gi}_start", False)
        return gather_started[gi][4]

    def full(nm, cols):
        return gathered[nm].reshape(-1, cols)

    w_in = jnp.pad(full("mla_w_in", mla_in), ((0, 0), (0, mla_in_pad - mla_in)))
    wt_uq = full("mla_w_uq", MLA_QR).reshape(MLA_HEADS, MLA_NOPE + MLA_ROPE, MLA_QR)
    wt_uq_n = wt_uq[:, :MLA_NOPE].reshape(MLA_HEADS * MLA_NOPE, MLA_QR)
    wt_uq_r = wt_uq[:, MLA_NOPE:].reshape(MLA_HEADS * MLA_ROPE, MLA_QR)
    wt_uk = full("mla_w_uk", MLA_KVR)
    wt_uv = full("mla_w_uv", MLA_KVR)
    w_mo = full("mla_w_o", d)
    wt_gate, wt_up, w_down = [None] * DEPTH, [None] * DEPTH, [None] * DEPTH

    def arrive(gi, after):
        if gi + 1 < len(groups):
            after = depart(gi + 1, after)
        landed = list(exchange_wait(gather_started[gi], after, f"gather_group{gi}_wait", False))
        gathered.update(zip(groups[gi], landed))
        for i in range(DEPTH):
            if group_of[f"gate{i}"] == gi:
                wt_gate[i], wt_up[i], w_down[i] = full(f"gate{i}", d), full(f"up{i}", d), full(f"down{i}", d)

    small = jnp.concatenate([c.reshape(-1, LANES), ln_g.reshape(-1, LANES), ln_b.reshape(-1, LANES)], axis=0)
    small_rows = small.shape[0]
    small = jnp.pad(small, ((0, (-small_rows) % 8), (0, 0)))
    small_all = all_gather(small, "gather_small")
    c_rows = bl * d // LANES
    c_all = small_all[:, :c_rows].reshape(N_DEV * bl, d)
    n_ln = DEPTH * 2
    ln_g_all = small_all[:, c_rows:c_rows + n_ln, :].transpose(1, 0, 2).reshape(DEPTH, 2, 1, d)
    ln_b_all = small_all[:, c_rows + n_ln:c_rows + 2 * n_ln, :].transpose(1, 0, 2).reshape(DEPTH, 2, 1, d)

    c_act = silu_rows(c_all, "silu_c")
    ada_b_loc = lax.dynamic_slice_in_dim(ada_b, dev * ada_cols, ada_cols, axis=1)
    mod_cols = [mm([(c_act, ada_w[i])], trans_b=False, out_dtype=F32, name=f"ada_fwd{i}", bias=ada_b_loc[i][None, :])
                for i in range(DEPTH)]
    mod_all = all_gather(jnp.concatenate(mod_cols, axis=0), "gather_mod")
    mod_all = mod_all.reshape(N_DEV, DEPTH, N_DEV * bl, ada_cols).transpose(1, 2, 0, 3).reshape(DEPTH, N_DEV * bl, 6 * d)
    mod_mine = lax.dynamic_slice_in_dim(mod_all, dev * bl, bl, axis=1)
    mods = [mod_mine[i].reshape(bl * 6, 1, d) for i in range(DEPTH)]
    mods[0] = mods[0] + depart(1, (mod_mine, gathered0))[0, 0]

    half_r = MLA_ROPE // 2
    inv_freq = ROPE_THETA ** (-jnp.arange(half_r, dtype=F32) / half_r)
    inv_freq = jnp.tile(inv_freq, LANES // half_r)[None, :]
    sign = jnp.tile(jnp.concatenate([-jnp.ones((half_r,), F32), jnp.ones((half_r,), F32)]), LANES // MLA_ROPE)[None, :]
    cos_t, sin_t = rope_tables(positions.astype(F32).reshape(t, 1), inv_freq, sign, "rope_tables")

    x2d = x.reshape(t, d)
    g_q, g_kv = mla_g_q.reshape(1, MLA_QR), mla_g_kv.reshape(1, MLA_KVR)
    b_f = jnp.pad(fox_b_f.reshape(1, FOX_HEADS), ((0, 0), (0, LANES - FOX_HEADS)))
    mla_scale = (MLA_NOPE + MLA_ROPE) ** -0.5
    fox_scale = FOX_HD ** -0.5
    tq = _attn_tiles(s)
    nk = s // tq

    saved = []
    u = modulate(x2d, mods[0], 0, 1, bl, "modulate0")
    xin = x2d
    for i in range(DEPTH):
        sv = {"u": u, "x_in": xin}
        if i % 2 == 0:
            h_in = mm([(u, w_in)], trans_b=False, out_dtype=F32, name=f"mla_in{i}")
            c_q, c_kv, k_r = mla_latents_forward(h_in, g_q, g_kv, cos_t, sin_t, f"mla_latents{i}")
            q_n = mm([(c_q, wt_uq_n)], trans_b=True, out_dtype=BF16, out_slab=True, name=f"mla_qn{i}")
            q_r_raw = mm([(c_q, wt_uq_r)], trans_b=True, out_dtype=F32, out_slab=True, name=f"mla_qr{i}")
            q_r = rope_slabs(q_r_raw, cos_t, sin_t, BF16, f"mla_qrope{i}")
            k_n = mm([(c_kv, wt_uk)], trans_b=True, out_dtype=BF16, out_slab=True, name=f"mla_kn{i}")
            v_m = mm([(c_kv, wt_uv)], trans_b=True, out_dtype=BF16, out_slab=True, name=f"mla_v{i}")
            ops = (q_n, q_r, k_n, k_r, v_m)
            o, lse, o_delta = attention_forward("mla", ops, bl, mla_scale, f"mla_attn{i}")
            y = mm([(o, w_mo)], trans_b=False, out_dtype=F32, name=f"mla_out{i}")
            sv.update(h_in=h_in, c_q=c_q, c_kv=c_kv, ops=ops, o=o, lse=lse, o_delta=o_delta)
        else:
            arrive(2, u)
            wt_fox = full("fox_w_in", d)
            wt_qkv = wt_fox[:3 * d]
            wt_f = jnp.pad(wt_fox[3 * d:], ((0, LANES - FOX_HEADS), (0, 0)))
            w_fo = full("fox_w_o", d)
            qkv = mm([(u, wt_qkv)], trans_b=True, out_dtype=BF16, out_slab=True, name=f"fox_qkv{i}")
            z = mm([(u, wt_f)], trans_b=True, out_dtype=F32, name=f"fox_z{i}")
            f_tok, f_q = fox_gate_forward(z, b_f, bl, f"fox_gate{i}")
            f_k = f_tok[:, :FOX_HEADS].reshape(bl, nk, tq, FOX_HEADS // 2, 2).transpose(0, 3, 1, 4, 2)
            f_k = jnp.pad(f_k.reshape(bl * FOX_HEADS // 2, nk, 2, tq), ((0, 0), (0, 0), (0, 6), (0, 0)))
            ops = (qkv, f_q, f_k)
            o, lse, o_delta = attention_forward("fox", ops, bl, fox_scale, f"fox_attn{i}")
            y = mm([(o, w_fo)], trans_b=False, out_dtype=F32, name=f"fox_out{i}")
            sv.update(z=z, ops=ops, o=o, lse=lse, o_delta=o_delta)
        x1, r1, u2 = residual_layer_norm(xin, y, mods[i], 2, ln_g_all[i, 0], ln_b_all[i, 0], bl, f"ln_mix{i}",
                                         next_mod=(3, 4))
        if wt_gate[i] is None:
            arrive(group_of[f"gate{i}"], u2)
        a, bb, h = swiglu_in(u2, wt_gate[i], wt_up[i], f"ffn_in{i}")
        y2 = mm([(h, w_down[i])], trans_b=False, out_dtype=F32, name=f"ffn_down{i}")
        sv.update(y=y, r1=r1, u2=u2, a=a, bb=bb, h=h, y2=y2)
        if i + 1 < DEPTH:
            xin, r2, u = residual_layer_norm(x1, y2, mods[i], 5, ln_g_all[i, 1], ln_b_all[i, 1], bl, f"ln_ffn{i}",
                                             next_mod=(0, 1, mods[i + 1]))
        else:
            xin, r2 = residual_layer_norm(x1, y2, mods[i], 5, ln_g_all[i, 1], ln_b_all[i, 1], bl, f"ln_ffn{i}")
        sv.update(r2=r2)
        saved.append(sv)

    loss_cols, d_x = loss_head(xin, loss_target.reshape(t, d), "loss_head")

    grads_full = {}
    wgrad = functools.partial(mm_tn, out_dtype=BF16)
    dmod = [[None] * 6 for _ in range(DEPTH)]
    dg_ln = [[None, None] for _ in range(DEPTH)]
    db_ln = [[None, None] for _ in range(DEPTH)]
    dg_q = dg_kv = db_f = None
    d_a, du = d_x, None
    scatter_started = [None] * len(groups)

    def scatter_start(gi, after=None):
        gs = [grads_full[nm].reshape(N_DEV, rows_of[nm], PACK_COLS).astype(BF16) for nm in groups[gi]]
        if gi == 0:
            gs = [jnp.concatenate([slot(nm, g) for nm, g in zip(groups[gi], gs)], axis=1)]
        if after is not None:
            gs = [held_until(g, after) for g in gs]
        lands = [landing(lax.dynamic_index_in_dim(g, dev, 0, keepdims=False)) for g in gs]
        scatter_started[gi] = exchange_start(gs, lands, f"scatter_group{gi}_start", True)

    ln_g_bwd = [[ln_g_all[i, k] for k in range(2)] for i in range(DEPTH)]
    for i in reversed(range(DEPTH)):
        sv = saved[i]
        if i + 1 < DEPTH:
            gi = group_of["fox_w_in"]
            scatter_start(gi)
            ln_g_bwd[i][1] = after_token(ln_g_bwd[i][1], scatter_started[gi])
        ln2 = (sv["r2"], sv["y2"], ln_g_bwd[i][1], ln_b_all[i, 1], (mods[i], 5))
        if du is None:
            bw = sublayer_backward(d_a, bl, f"bwd_ln_ffn{i}", ln=ln2)
        else:
            bw = sublayer_backward(d_a, bl, f"bwd_ln_ffn{i}", du=du, scale=(mods[i + 1], 1), ln=ln2)
            dmod[i + 1][0], dmod[i + 1][1] = bw["dshift"], bw["dscale"]
        dmod[i][5], dg_ln[i][1], db_ln[i][1] = bw["dgate"], bw["dg"], bw["db"]
        dy2 = bw["dy"]
        da, dbb = swiglu_out_backward(dy2, w_down[i], sv["a"], sv["bb"], f"bwd_ffn_act{i}")
        du2 = mm([(da, wt_gate[i]), (dbb, wt_up[i])], trans_b=False, out_dtype=F32, name=f"bwd_ffn_du{i}")
        grads_full[f"down{i}"] = wgrad(sv["h"], dy2, name=f"bwd_w_down{i}")
        grads_full[f"gate{i}"] = wgrad(da, sv["u2"], name=f"bwd_w_gate{i}")
        grads_full[f"up{i}"] = wgrad(dbb, sv["u2"], name=f"bwd_w_up{i}")
        gi = group_of[f"gate{i}"]
        scatter_start(gi)
        ln_g_bwd[i][0] = after_token(ln_g_bwd[i][0], scatter_started[gi])
        bw = sublayer_backward(bw["dx"], bl, f"bwd_ln_mix{i}", du=du2, scale=(mods[i], 4),
                               ln=(sv["r1"], sv["y"], ln_g_bwd[i][0], ln_b_all[i, 0], (mods[i], 2)))
        dmod[i][3], dmod[i][4], dmod[i][2] = bw["dshift"], bw["dscale"], bw["dgate"]
        dg_ln[i][0], db_ln[i][0] = bw["dg"], bw["db"]
        d_a, dy = bw["dx"], bw["dy"]
        o, lse, ops = sv["o"], sv["lse"], sv["ops"]
        if i % 2 == 0:
            do = mm([(dy, w_mo)], trans_b=True, out_dtype=BF16, out_slab=True, name=f"bwd_mla_do{i}")
            grads_full["mla_w_o"] = wgrad(o, dy, name=f"bwd_w_mla_o{i}")
            dqn, dqr, dkn, dvm, dkr = attention_backward("mla", ops, sv["o_delta"], do, lse, bl, mla_scale,
                                                         f"bwd_mla_attn{i}")
            dqr = rope_slabs(dqr, cos_t, sin_t, F32, f"bwd_mla_qrope{i}", transposed=True)
            dcq = mm([(dqn, wt_uq_n), (dqr, wt_uq_r)], trans_b=False, out_dtype=F32, name=f"bwd_mla_dcq{i}")
            dckv = mm([(dkn, wt_uk), (dvm, wt_uv)], trans_b=False, out_dtype=F32, name=f"bwd_mla_dckv{i}")
            d_uq_n = wgrad(dqn, sv["c_q"], name=f"bwd_w_uq_n{i}").reshape(MLA_HEADS, MLA_NOPE, MLA_QR)
            d_uq_r = wgrad(dqr, sv["c_q"], name=f"bwd_w_uq_r{i}").reshape(MLA_HEADS, MLA_ROPE, MLA_QR)
            grads_full["mla_w_uq"] = jnp.concatenate([d_uq_n, d_uq_r], axis=1)
            grads_full["mla_w_uk"] = wgrad(dkn, sv["c_kv"], name=f"bwd_w_uk{i}")
            grads_full["mla_w_uv"] = wgrad(dvm, sv["c_kv"], name=f"bwd_w_uv{i}")
            dh_in, dg_q, dg_kv = mla_latents_backward(sv["h_in"], dcq, dckv, dkr, g_q, g_kv, cos_t, sin_t,
                                                      f"bwd_mla_latents{i}")
            du = mm([(dh_in, w_in)], trans_b=True, out_dtype=F32, name=f"bwd_mla_du{i}")
            grads_full["mla_w_in"] = wgrad(sv["u"], dh_in, name=f"bwd_w_mla_in{i}")[:, :mla_in]
        else:
            do = mm([(dy, w_fo)], trans_b=True, out_dtype=BF16, out_slab=True, name=f"bwd_fox_do{i}")
            grads_full["fox_w_o"] = wgrad(o, dy, name=f"bwd_w_fox_o{i}")
            dq, dk, dvf, dfk = attention_backward("fox", ops, sv["o_delta"], do, lse, bl, fox_scale, f"bwd_fox_attn{i}")
            df = dfk[:, :, :2, :].reshape(bl, FOX_HEADS // 2, nk, 2, tq).transpose(0, 2, 4, 1, 3).reshape(t, FOX_HEADS)
            df = jnp.pad(df, ((0, 0), (0, LANES - FOX_HEADS)))
            dz, db_f = fox_gate_backward(sv["z"], b_f, df, bl, f"bwd_fox_gate{i}")
            du = mm([(dq, wt_fox[0:d]), (dk, wt_fox[d:2 * d]), (dvf, wt_fox[2 * d:3 * d]), (dz, wt_f)],
                    trans_b=False, out_dtype=F32, name=f"bwd_fox_du{i}")
            u_f = sv["u"]
            grads_full["fox_w_in"] = jnp.concatenate(
                [wgrad(dq, u_f, name=f"bwd_w_fox_q{i}"), wgrad(dk, u_f, name=f"bwd_w_fox_k{i}"),
                 wgrad(dvf, u_f, name=f"bwd_w_fox_v{i}"), wgrad(dz, u_f, name=f"bwd_w_fox_f{i}")[:FOX_HEADS]], axis=0)
    scatter_start(0)
    bw = sublayer_backward(d_a, bl, "bwd_input", du=du, scale=(after_token(mods[0], scatter_started[0]), 1), x_in=x2d)
    dmod[0][0], dmod[0][1] = bw["dshift"], bw["dscale"]
    grad_x = bw["dx"].reshape(bl, s, d)

    dmod_rows = jnp.concatenate([r.reshape(bl, d) for layer in dmod for r in layer], axis=0)
    dmod_rows = dmod_rows.reshape(DEPTH, 6, bl, d).transpose(0, 2, 1, 3)
    n_mod = dmod_rows.size // LANES
    ln_parts = [dg_ln[i][k] for i in range(DEPTH) for k in range(2)] + [db_ln[i][k] for i in range(DEPTH) for k in range(2)]
    small_g = jnp.concatenate([dmod_rows.reshape(-1, LANES), dg_q.reshape(-1, LANES), dg_kv.reshape(-1, LANES), db_f]
                              + [p.reshape(-1, LANES) for p in ln_parts] + [loss_cols.reshape(-1, LANES)], axis=0)
    n_small = small_g.shape[0]
    small_g = jnp.pad(small_g, ((0, (-n_small) % 8), (0, 0)))
    small_g_all = all_gather(small_g, "gather_small_grads")
    small_sum = sum_leading(small_g_all, "sum_small_grads")
    per_seq = DEPTH * 6 * d // LANES
    dmod_all = small_g_all[:, :n_mod].reshape(N_DEV, DEPTH, bl, 6 * d).transpose(1, 0, 2, 3)
    dmod_all = dmod_all.reshape(DEPTH, N_DEV * bl, 6 * d)
    o1 = n_mod
    grad_g_q = small_sum[o1:o1 + MLA_QR // LANES].reshape(1, MLA_QR)
    o1 += MLA_QR // LANES
    grad_g_kv = small_sum[o1:o1 + MLA_KVR // LANES].reshape(1, MLA_KVR)
    o1 += MLA_KVR // LANES
    grad_b_f = small_sum[o1:o1 + 1, :FOX_HEADS]
    o1 += 1
    n_ln_rows = DEPTH * 2 * d // LANES
    grad_ln_g_full = small_sum[o1:o1 + n_ln_rows].reshape(DEPTH, 2, d)
    grad_ln_b_full = small_sum[o1 + n_ln_rows:o1 + 2 * n_ln_rows].reshape(DEPTH, 2, d)
    loss = jnp.sum(small_sum[o1 + 2 * n_ln_rows:o1 + 2 * n_ln_rows + d // LANES])
    shard = d // N_DEV
    grad_ln_g = lax.dynamic_slice_in_dim(grad_ln_g_full, dev * shard, shard, axis=2)
    grad_ln_b = lax.dynamic_slice_in_dim(grad_ln_b_full, dev * shard, shard, axis=2)
    by_seq = small_g_all[:, :n_mod].reshape(N_DEV, DEPTH, bl, 6 * d // LANES, LANES).transpose(0, 2, 1, 3, 4)
    grad_ada_b = sum_leading(by_seq.reshape(N_DEV * bl, per_seq, LANES), "sum_ada_b").reshape(DEPTH, 6 * d)
    dmod_cols = lax.dynamic_slice_in_dim(dmod_all, dev * ada_cols, ada_cols, axis=2)
    grad_ada_w = jnp.stack([mm_tn(c_act, dmod_cols[i], name=f"bwd_w_ada{i}") for i in range(DEPTH)])

    g_mine = {}

    def scatter_arrive(gi, after):
        landed = exchange_wait(scatter_started[gi], after, f"scatter_group{gi}_wait", True)
        if gi == 0:
            total = sum_leading(landed[0], f"scatter_group{gi}_sum")
            g_mine.update({nm: total[offsets[nm]:offsets[nm] + rows_of[nm]] for nm in groups[gi]})
            return total
        for nm, land in zip(groups[gi], landed):
            g_mine[nm] = sum_leading(land, f"scatter_sum_{nm}")
        return g_mine[groups[gi][-1]]

    after = scatter_started[0][4]
    for gi in reversed(range(1, len(groups))):
        after = scatter_arrive(gi, after)

    def mine(nm, shape):
        return g_mine[nm].reshape(shape)

    def shard_t(nm, a):
        return mine(nm, t_last(a).shape)

    transposed = {"mla_w_uq", "mla_w_uk", "mla_w_uv", "fox_w_in", "ffn_w_gate", "ffn_w_up"}
    grads = {
        "mla_w_in": lambda: mine("mla_w_in", mla_w_in[0].shape)[None],
        "mla_g_q": lambda: grad_g_q,
        "mla_w_uq": lambda: shard_t("mla_w_uq", mla_w_uq[0])[None],
        "mla_g_kv": lambda: grad_g_kv,
        "mla_w_uk": lambda: shard_t("mla_w_uk", mla_w_uk[0])[None],
        "mla_w_uv": lambda: shard_t("mla_w_uv", mla_w_uv[0])[None],
        "mla_w_o": lambda: mine("mla_w_o", mla_w_o[0].shape)[None],
        "fox_w_in": lambda: shard_t("fox_w_in", fox_w_in[0])[None],
        "fox_b_f": lambda: grad_b_f,
        "fox_w_o": lambda: mine("fox_w_o", fox_w_o[0].shape)[None],
        "ada_w": lambda: grad_ada_w,
        "ada_b": lambda: grad_ada_b,
        "ffn_w_gate": lambda: jnp.stack([shard_t(f"gate{i}", ffn_w_gate[i]) for i in range(DEPTH)]),
        "ffn_w_up": lambda: jnp.stack([shard_t(f"up{i}", ffn_w_up[i]) for i in range(DEPTH)]),
        "ffn_w_down": lambda: jnp.stack([mine(f"down{i}", ffn_w_down[i].shape) for i in range(DEPTH)]),
        "ln_g": lambda: grad_ln_g,
        "ln_b": lambda: grad_ln_b,
    }
    weights = dict(mla_w_in=mla_w_in, mla_g_q=mla_g_q, mla_w_uq=mla_w_uq, mla_g_kv=mla_g_kv, mla_w_uk=mla_w_uk,
                   mla_w_uv=mla_w_uv, mla_w_o=mla_w_o, fox_w_in=fox_w_in, fox_b_f=fox_b_f, fox_w_o=fox_w_o,
                   ada_w=ada_w, ada_b=ada_b, ffn_w_gate=ffn_w_gate, ffn_w_up=ffn_w_up, ffn_w_down=ffn_w_down,
                   ln_g=ln_g, ln_b=ln_b)
    first = dict(mla_w_in=m_mla_w_in, mla_g_q=m_mla_g_q, mla_w_uq=m_mla_w_uq, mla_g_kv=m_mla_g_kv, mla_w_uk=m_mla_w_uk,
                 mla_w_uv=m_mla_w_uv, mla_w_o=m_mla_w_o, fox_w_in=m_fox_w_in, fox_b_f=m_fox_b_f, fox_w_o=m_fox_w_o,
                 ada_w=m_ada_w, ada_b=m_ada_b, ffn_w_gate=m_ffn_w_gate, ffn_w_up=m_ffn_w_up, ffn_w_down=m_ffn_w_down,
                 ln_g=m_ln_g, ln_b=m_ln_b)
    second = dict(mla_w_in=v_mla_w_in, mla_g_q=v_mla_g_q, mla_w_uq=v_mla_w_uq, mla_g_kv=v_mla_g_kv, mla_w_uk=v_mla_w_uk,
                  mla_w_uv=v_mla_w_uv, mla_w_o=v_mla_w_o, fox_w_in=v_fox_w_in, fox_b_f=v_fox_b_f, fox_w_o=v_fox_w_o,
                  ada_w=v_ada_w, ada_b=v_ada_b, ffn_w_gate=v_ffn_w_gate, ffn_w_up=v_ffn_w_up, ffn_w_down=v_ffn_w_down,
                  ln_g=v_ln_g, ln_b=v_ln_b)
    order = list(weights)
    last = [nm for nm in order if group_of.get(nm) == 0]
    updated = {}
    for nm in [nm for nm in order if nm not in last] + last:
        if last and nm == last[0]:
            scatter_arrive(0, after)
        lay = t_last if nm in transposed else (lambda a: a)
        w = lay(weights[nm])
        g = grads[nm]().reshape(w.shape)
        delta, new_m, new_v = adamw(w, g, lay(first[nm]), lay(second[nm]), f"adamw_{nm}")
        updated[nm] = (lay(g), lay(delta), lay(new_m), lay(new_v))
        after = new_v
    return (loss, grad_x, *(updated[nm][k] for k in range(4) for nm in order))
```

```python
import functools
import math

import jax
import jax.numpy as jnp
from jax import lax
from jax.experimental import pallas as pl
from jax.experimental.pallas import tpu as pltpu

F32 = jnp.float32
BF16 = jnp.bfloat16
LANES = 128
N_DEV = 8
VMEM_LIMIT_BYTES = 56 * 1024 * 1024

DEPTH = 2
MLA_HEADS = 8
MLA_NOPE = 128
MLA_ROPE = 64
MLA_V = 128
MLA_QR = 256
MLA_KVR = 256
ROPE_THETA = 10000.0
FOX_HEADS = 16
FOX_HD = 64
ALPHA = (2.0 * DEPTH) ** 0.25
NORM_EPS = 1e-5
ADAM_LR = 0.001
ADAM_B1 = 0.9
ADAM_B2 = 0.999
ADAM_EPS = 1e-08
ADAM_WD = 0.01
ADAM_STEP = 10

MESH = pl.DeviceIdType.MESH


def _params(*sem):
    return pltpu.CompilerParams(dimension_semantics=sem, vmem_limit_bytes=VMEM_LIMIT_BYTES)


def _tile(n, cap, mult=LANES):
    if n <= cap:
        return n
    best = None
    for t in range(mult, cap + 1, mult):
        if n % t == 0:
            best = t
    assert best is not None, (n, cap, mult)
    return best


def _dot(a, b, dims):
    return lax.dot_general(a, b, (dims, ((), ())), preferred_element_type=F32)


def _nn(a, b):
    return _dot(a, b, ((1,), (0,)))


def _nt(a, b):
    return _dot(a, b, ((1,), (1,)))


def _tn(a, b):
    return _dot(a, b, ((0,), (0,)))


def _me():
    return lax.axis_index("x"), lax.axis_index("y"), lax.axis_index("c")


def all_gather(x_loc, name):
    r, c = x_loc.shape

    def body(x_ref, out_ref, send_sems, recv_sems, local_sem):
        x, y, cc = _me()
        me, sibling = (x, y, cc), (x, y, 1 - cc)
        chips = [(1 - x, y), (x, 1 - y), (1 - x, 1 - y)]

        def rows(px, py, pc):
            return out_ref.at[4 * px + 2 * py + pc]

        def copy(k, block, to, src=None):
            return pltpu.make_async_remote_copy(
                src_ref=rows(*block) if src is None else src, dst_ref=rows(*block),
                send_sem=send_sems.at[k], recv_sem=recv_sems.at[k], device_id=to, device_id_type=MESH)

        mine = pltpu.make_async_copy(x_ref, rows(*me), local_sem)
        mine.start()
        first = [copy(0, me, sibling, src=x_ref)]
        first += [copy(1 + j, me, (*chip, cc), src=x_ref) for j, chip in enumerate(chips)]
        for cp in first:
            cp.start()
        passed = [copy(4 + j, (*chip, cc), sibling) for j, chip in enumerate(chips)]
        for j, chip in enumerate(chips):
            copy(1 + j, (*chip, cc), me).wait_recv()
            passed[j].start()
        copy(0, sibling, me).wait_recv()
        for j, chip in enumerate(chips):
            copy(4 + j, (*chip, 1 - cc), me).wait_recv()
        for cp in first + passed:
            cp.wait_send()
        mine.wait()

    return pl.pallas_call(
        body, name=name,
        out_shape=jax.ShapeDtypeStruct((N_DEV, r, c), x_loc.dtype),
        in_specs=[pl.BlockSpec(memory_space=pl.ANY)],
        out_specs=pl.BlockSpec(memory_space=pl.ANY),
        scratch_shapes=[pltpu.SemaphoreType.DMA((7,)), pltpu.SemaphoreType.DMA((7,)), pltpu.SemaphoreType.DMA(())],
    )(x_loc)


HBM_SPEC = pl.BlockSpec(memory_space=pltpu.HBM)
SEM_SPEC = pl.BlockSpec(memory_space=pltpu.SEMAPHORE)
N_PEERS = N_DEV - 1


def _peer(k):
    x, y, c = _me()
    return (1 - x if k & 4 else x, 1 - y if k & 2 else y, 1 - c if k & 1 else c)


def _exchange_copies(src_refs, land_refs, send_sems, recv_sems, scatter):
    x, y, c = _me()
    mine = 4 * x + 2 * y + c
    copies = []
    for n, (src_ref, land_ref) in enumerate(zip(src_refs, land_refs)):
        for k in range(1, N_DEV):
            px, py, pc = _peer(k)
            src = src_ref.at[4 * px + 2 * py + pc] if scatter else src_ref
            sem = n * N_PEERS + k - 1
            copies.append(pltpu.make_async_remote_copy(
                src_ref=src, dst_ref=land_ref.at[mine], send_sem=send_sems.at[sem], recv_sem=recv_sems.at[sem],
                device_id=(px, py, pc), device_id_type=MESH))
    return copies


def exchange_start(srcs, lands, name, scatter):
    n = len(srcs)

    def body(*refs):
        send_sems, recv_sems = refs[2 * n], refs[2 * n + 1]
        for cp in _exchange_copies(refs[:n], refs[n:2 * n], send_sems, recv_sems, scatter):
            cp.start()
        token = refs[-1]
        token[...] = jnp.zeros_like(token)

    outs = pl.pallas_call(
        body, name=name,
        out_shape=(pltpu.SemaphoreType.DMA((n * N_PEERS,)), pltpu.SemaphoreType.DMA((n * N_PEERS,)),
                   *(pltpu.HBM(a.shape, a.dtype) for a in (*srcs, *lands)), jax.ShapeDtypeStruct((8, LANES), F32)),
        in_specs=(HBM_SPEC,) * (2 * n),
        out_specs=(SEM_SPEC, SEM_SPEC, *((HBM_SPEC,) * (2 * n)), pl.BlockSpec(memory_space=pltpu.VMEM)),
        input_output_aliases={i: 2 + i for i in range(2 * n)},
        compiler_params=pltpu.CompilerParams(has_side_effects=pltpu.SideEffectType.DATAFLOW_SIDE_EFFECTING),
    )(*(pltpu.with_memory_space_constraint(a, pltpu.HBM) for a in (*srcs, *lands)))
    return outs[0], outs[1], outs[2:2 + n], outs[2 + n:2 + 2 * n], outs[-1]


def exchange_wait(started, after, name, scatter):
    send_sems, recv_sems, srcs, lands, _ = started
    n = len(srcs)

    def body(*refs):
        send_sems, recv_sems = refs[2 * n], refs[2 * n + 1]
        for cp in _exchange_copies(refs[:n], refs[n:2 * n], send_sems, recv_sems, scatter):
            cp.wait_send()
            cp.wait_recv()

    outs = pl.pallas_call(
        body, name=name,
        out_shape=tuple(pltpu.HBM(a.shape, a.dtype) for a in (*srcs, *lands)),
        in_specs=(*((HBM_SPEC,) * (2 * n)), SEM_SPEC, SEM_SPEC, pl.BlockSpec(memory_space=pl.ANY)),
        out_specs=(HBM_SPEC,) * (2 * n), input_output_aliases={i: i for i in range(2 * n)},
        compiler_params=pltpu.CompilerParams(has_side_effects=pltpu.SideEffectType.DATAFLOW_SIDE_EFFECTING),
    )(*srcs, *lands, send_sems, recv_sems, after)
    return outs[n:]


def after_token(small, started):
    return small + started[4][0, 0]


def sum_leading(x, name):
    n, r, c = x.shape
    tr = _tile(r, 512, 16)

    def body(x_ref, o_ref):
        acc = x_ref[0].astype(F32)
        for k in range(1, n):
            acc = acc + x_ref[k].astype(F32)
        o_ref[...] = acc

    return pl.pallas_call(
        body, name=name,
        out_shape=jax.ShapeDtypeStruct((r, c), F32),
        grid=(r // tr,),
        in_specs=[pl.BlockSpec((n, tr, c), lambda i: (0, i, 0))],
        out_specs=pl.BlockSpec((tr, c), lambda i: (i, 0)),
        compiler_params=_params("arbitrary"),
    )(x)


MM_VMEM_BUDGET = 36 * 1024 * 1024
GRID_STEP_AS_BYTES = 1 << 20


def _mm_tiles(m, n, a_row_bytes, b_col_bytes, out_bytes):
    tms = [c for c in (2048, 1024, 512, 256, 128, 64, 32, 16, 8) if m % c == 0] or [m]
    tns = [c for c in range(LANES, min(n, 2048) + 1, LANES) if n % c == 0] or [n]
    best = None
    for tm in tms:
        for tn in tns:
            vmem = 2 * (tm * a_row_bytes + tn * b_col_bytes) + 2 * tm * tn * out_bytes + tm * tn * 4
            if vmem > MM_VMEM_BUDGET:
                continue
            steps = (m // tm) * (n // tn)
            cost = steps * GRID_STEP_AS_BYTES + (m // tm) * n * b_col_bytes + m * a_row_bytes
            if best is None or cost < best[0]:
                best = (cost, tm, tn)
    assert best is not None, (m, n, a_row_bytes, b_col_bytes)
    return best[1], best[2]


def mm(pairs, *, trans_b, out_dtype, name, out_slab=False, bias=None):
    a0 = pairs[0][0]
    m = a0.shape[1] if a0.ndim == 3 else a0.shape[0]
    n = pairs[0][1].shape[0] if trans_b else pairs[0][1].shape[1]
    a_row_bytes = sum((b.shape[1] if trans_b else b.shape[0]) * a.dtype.itemsize for a, b in pairs)
    b_col_bytes = sum((b.shape[1] if trans_b else b.shape[0]) * b.dtype.itemsize for _, b in pairs)
    tm, tn = _mm_tiles(m, n, a_row_bytes, b_col_bytes, jnp.dtype(out_dtype).itemsize)
    slabs = [a.ndim == 3 for a, _ in pairs]
    n_pairs = len(pairs)

    def body(*refs):
        o_ref = refs[-1]
        acc = bias_ref = None
        if bias is not None:
            bias_ref = refs[2 * n_pairs]
        for i in range(n_pairs):
            a_ref, b_ref = refs[2 * i], refs[2 * i + 1]
            if slabs[i]:
                a = jnp.concatenate([a_ref[s].astype(BF16) for s in range(a_ref.shape[0])], axis=1)
            else:
                a = a_ref[...].astype(BF16)
            b = b_ref[...].astype(BF16)
            part = _nt(a, b) if trans_b else _nn(a, b)
            acc = part if acc is None else acc + part
        if bias_ref is not None:
            acc = acc + bias_ref[...]
        if out_slab:
            for s in range(tn // LANES):
                o_ref[s] = acc[:, s * LANES:(s + 1) * LANES].astype(out_dtype)
        else:
            o_ref[...] = acc.astype(out_dtype)

    in_specs, args = [], []
    for (a, b), slab in zip(pairs, slabs):
        if slab:
            in_specs.append(pl.BlockSpec((a.shape[0], tm, LANES), lambda i, j: (0, i, 0)))
        else:
            in_specs.append(pl.BlockSpec((tm, a.shape[1]), lambda i, j: (i, 0)))
        if trans_b:
            in_specs.append(pl.BlockSpec((tn, b.shape[1]), lambda i, j: (j, 0)))
        else:
            in_specs.append(pl.BlockSpec((b.shape[0], tn), lambda i, j: (0, j)))
        args += [a, b]
    if bias is not None:
        in_specs.append(pl.BlockSpec((1, tn), lambda i, j: (0, j)))
        args.append(bias)
    if out_slab:
        out_shape = jax.ShapeDtypeStruct((n // LANES, m, LANES), out_dtype)
        out_spec = pl.BlockSpec((tn // LANES, tm, LANES), lambda i, j: (j, i, 0))
    else:
        out_shape = jax.ShapeDtypeStruct((m, n), out_dtype)
        out_spec = pl.BlockSpec((tm, tn), lambda i, j: (i, j))
    return pl.pallas_call(
        body, name=name, out_shape=out_shape, grid=(m // tm, n // tn),
        in_specs=in_specs, out_specs=out_spec,
        compiler_params=_params("arbitrary", "arbitrary"),
    )(*args)


def mm_tn(a, b, *, name, out_dtype=F32, tk_cap=1536, tn_cap=1024, tm_cap=512):
    slab = a.ndim == 3
    m = a.shape[1] if slab else a.shape[0]
    k = a.shape[0] * LANES if slab else a.shape[1]
    n = b.shape[1]
    tk = _tile(k, tk_cap)
    tn = _tile(n, tn_cap)
    tm = _tile(m, tm_cap, 8)
    n_steps = m // tm

    def body(a_ref, b_ref, o_ref, acc_ref):
        step = pl.program_id(2)

        @pl.when(step == 0)
        def _():
            acc_ref[...] = jnp.zeros_like(acc_ref)

        bb = b_ref[...].astype(BF16)
        if slab:
            for s in range(tk // LANES):
                acc_ref[s * LANES:(s + 1) * LANES, :] += _tn(a_ref[s].astype(BF16), bb)
        else:
            acc_ref[...] += _tn(a_ref[...].astype(BF16), bb)

        @pl.when(step == n_steps - 1)
        def _():
            o_ref[...] = acc_ref[...].astype(out_dtype)

    if slab:
        a_spec = pl.BlockSpec((tk // LANES, tm, LANES), lambda i, j, t: (i, t, 0))
    else:
        a_spec = pl.BlockSpec((tm, tk), lambda i, j, t: (t, i))
    return pl.pallas_call(
        body, name=name, out_shape=jax.ShapeDtypeStruct((k, n), out_dtype), grid=(k // tk, n // tn, n_steps),
        in_specs=[a_spec, pl.BlockSpec((tm, tn), lambda i, j, t: (t, j))],
        out_specs=pl.BlockSpec((tk, tn), lambda i, j, t: (i, j)),
        scratch_shapes=[pltpu.VMEM((tk, tn), F32)],
        compiler_params=_params("arbitrary", "arbitrary", "arbitrary"),
    )(a, b)


def _row_spec(d, k):
    return pl.BlockSpec((1, 1, d), lambda b, i: (6 * b + k, 0, 0))


def modulate(x, mod, k_shift, k_scale, bl, name):
    t, d = x.shape
    s = t // bl
    tm = _tile(s, 512, 8)
    nt = s // tm

    def body(x_ref, sh_ref, sc_ref, o_ref):
        o_ref[...] = (x_ref[...] * (1.0 + sc_ref[0]) + sh_ref[0]).astype(BF16)

    return pl.pallas_call(
        body, name=name, out_shape=jax.ShapeDtypeStruct((t, d), BF16), grid=(bl, nt),
        in_specs=[pl.BlockSpec((tm, d), lambda b, i: (b * nt + i, 0)), _row_spec(d, k_shift), _row_spec(d, k_scale)],
        out_specs=pl.BlockSpec((tm, d), lambda b, i: (b * nt + i, 0)),
        compiler_params=_params("arbitrary", "arbitrary"),
    )(x, mod, mod)


def _layer_norm_stats(r):
    mu = jnp.mean(r, axis=-1, keepdims=True)
    rc = r - mu
    var = jnp.mean(rc * rc, axis=-1, keepdims=True)
    rstd = lax.rsqrt(var + NORM_EPS)
    return rc * rstd, rstd


def residual_layer_norm(x, y, mod, k_gate, g, b, bl, name, next_mod=None):
    t, d = x.shape
    s = t // bl
    tm = _tile(s, 512, 8)
    nt = s // tm
    has_next = next_mod is not None

    def body(*refs):
        x_ref, y_ref, gt_ref, g_ref, b_ref = refs[:5]
        rest = refs[5:]
        if has_next:
            sh_ref, sc_ref, o_ref, r_ref, u_ref = rest
        else:
            o_ref, r_ref = rest
        r = ALPHA * x_ref[...] + (1.0 + gt_ref[0]) * y_ref[...]
        xhat, _ = _layer_norm_stats(r)
        out = xhat * g_ref[...] + b_ref[...]
        o_ref[...] = out
        r_ref[...] = r
        if has_next:
            u_ref[...] = (out * (1.0 + sc_ref[0]) + sh_ref[0]).astype(BF16)

    tok = pl.BlockSpec((tm, d), lambda bb, i: (bb * nt + i, 0))
    vec = pl.BlockSpec((1, d), lambda bb, i: (0, 0))
    in_specs = [tok, tok, _row_spec(d, k_gate), vec, vec]
    args = [x, y, mod, g, b]
    out_shape = [jax.ShapeDtypeStruct((t, d), F32), jax.ShapeDtypeStruct((t, d), F32)]
    out_specs = [tok, tok]
    if has_next:
        in_specs += [_row_spec(d, next_mod[0]), _row_spec(d, next_mod[1])]
        args += [mod if len(next_mod) == 2 else next_mod[2]] * 2
        out_shape.append(jax.ShapeDtypeStruct((t, d), BF16))
        out_specs.append(tok)
    return pl.pallas_call(
        body, name=name, out_shape=out_shape, grid=(bl, nt), in_specs=in_specs, out_specs=out_specs,
        compiler_params=_params("arbitrary", "arbitrary"),
    )(*args)


def loss_head(xo, target, name):
    t, d = xo.shape
    tm = _tile(t, 512, 8)

    def body(x_ref, t_ref, l_ref, dx_ref):
        @pl.when(pl.program_id(0) == 0)
        def _():
            l_ref[...] = jnp.zeros_like(l_ref)

        e = x_ref[...] - t_ref[...]
        l_ref[...] += jnp.sum(e * e, axis=0, keepdims=True) * (0.5 / d)
        dx_ref[...] = e * (1.0 / d)

    tok = pl.BlockSpec((tm, d), lambda i: (i, 0))
    return pl.pallas_call(
        body, name=name,
        out_shape=[jax.ShapeDtypeStruct((1, d), F32), jax.ShapeDtypeStruct((t, d), F32)],
        grid=(t // tm,), in_specs=[tok, tok],
        out_specs=[pl.BlockSpec((1, d), lambda i: (0, 0)), tok],
        compiler_params=_params("arbitrary"),
    )(xo, target)


def sublayer_backward(d_a, bl, name, *, du=None, scale=None, x_in=None, ln=None):
    t, d = d_a.shape
    s = t // bl
    tm = _tile(s, 512, 8)
    nt = s // tm
    has_mod = du is not None
    has_ln = ln is not None
    assert has_mod or has_ln
    assert has_ln or x_in is not None

    def body(*refs):
        refs = list(refs)
        da_ref = refs.pop(0)
        if has_mod:
            du_ref, sc_ref = refs.pop(0), refs.pop(0)
        if has_ln:
            r_ref, y_ref, g_ref, b_ref, gt_ref = (refs.pop(0) for _ in range(5))
        elif has_mod:
            xin_ref = refs.pop(0)
        dx_ref = refs.pop(0)
        if has_ln:
            dy_ref, dg_ref, db_ref, dgt_ref = (refs.pop(0) for _ in range(4))
        if has_mod:
            dsc_ref, dsh_ref = refs.pop(0), refs.pop(0)
        first_tile = pl.program_id(1) == 0
        first_step = jnp.logical_and(pl.program_id(0) == 0, first_tile)

        dout = da_ref[...]
        if has_ln:
            xhat, rstd = _layer_norm_stats(r_ref[...])
        if has_mod:
            duv = du_ref[...]
            dout = dout + duv * (1.0 + sc_ref[0])
            xin = xhat * g_ref[...] + b_ref[...] if has_ln else xin_ref[...]

            @pl.when(first_tile)
            def _():
                dsc_ref[...] = jnp.zeros_like(dsc_ref)
                dsh_ref[...] = jnp.zeros_like(dsh_ref)

            dsc_ref[0] += jnp.sum(duv * xin, axis=0, keepdims=True)
            dsh_ref[0] += jnp.sum(duv, axis=0, keepdims=True)
        if not has_ln:
            dx_ref[...] = dout
            return

        @pl.when(first_step)
        def _():
            dg_ref[...] = jnp.zeros_like(dg_ref)
            db_ref[...] = jnp.zeros_like(db_ref)

        @pl.when(first_tile)
        def _():
            dgt_ref[...] = jnp.zeros_like(dgt_ref)

        dg_ref[...] += jnp.sum(dout * xhat, axis=0, keepdims=True)
        db_ref[...] += jnp.sum(dout, axis=0, keepdims=True)
        dxh = dout * g_ref[...]
        dr = rstd * (dxh - jnp.mean(dxh, axis=-1, keepdims=True) - xhat * jnp.mean(dxh * xhat, axis=-1, keepdims=True))
        dx_ref[...] = ALPHA * dr
        dy_ref[...] = ((1.0 + gt_ref[0]) * dr).astype(BF16)
        dgt_ref[0] += jnp.sum(dr * y_ref[...], axis=0, keepdims=True)

    tok = pl.BlockSpec((tm, d), lambda bb, i: (bb * nt + i, 0))
    vec = pl.BlockSpec((1, d), lambda bb, i: (0, 0))
    seq = pl.BlockSpec((1, 1, d), lambda bb, i: (bb, 0, 0))
    in_specs, args = [tok], [d_a]
    if has_mod:
        in_specs += [tok, _row_spec(d, scale[1])]
        args += [du, scale[0]]
    if has_ln:
        r, y, g, b, gate = ln
        in_specs += [tok, tok, vec, vec, _row_spec(d, gate[1])]
        args += [r, y, g, b, gate[0]]
    elif has_mod:
        in_specs.append(tok)
        args.append(x_in)
    names = ["dx"]
    out_shape, out_specs = [jax.ShapeDtypeStruct((t, d), F32)], [tok]
    if has_ln:
        names += ["dy", "dg", "db", "dgate"]
        out_shape += [jax.ShapeDtypeStruct((t, d), BF16), jax.ShapeDtypeStruct((1, d), F32),
                      jax.ShapeDtypeStruct((1, d), F32), jax.ShapeDtypeStruct((bl, 1, d), F32)]
        out_specs += [tok, vec, vec, seq]
    if has_mod:
        names += ["dscale", "dshift"]
        out_shape += [jax.ShapeDtypeStruct((bl, 1, d), F32)] * 2
        out_specs += [seq, seq]
    outs = pl.pallas_call(
        body, name=name, out_shape=out_shape, grid=(bl, nt), in_specs=in_specs, out_specs=out_specs,
        compiler_params=_params("arbitrary", "arbitrary"),
    )(*args)
    return dict(zip(names, outs))


def _silu(a):
    return a * jax.nn.sigmoid(a)


def silu_rows(a, name):
    def body(a_ref, o_ref):
        o_ref[...] = _silu(a_ref[...]).astype(BF16)

    return pl.pallas_call(body, name=name, out_shape=jax.ShapeDtypeStruct(a.shape, BF16))(a)


def _swiglu_tiles(t, f):
    return _tile(t, 512, 8), _tile(f, 1536)


def swiglu_in(u, wt_gate, wt_up, name):
    t, d = u.shape
    f = wt_gate.shape[0]
    tm, tf = _swiglu_tiles(t, f)

    def body(u_ref, g_ref, w_ref, a_ref, b_ref, h_ref):
        uv = u_ref[...]
        a = _nt(uv, g_ref[...])
        b = _nt(uv, w_ref[...])
        a_ref[...] = a.astype(BF16)
        b_ref[...] = b.astype(BF16)
        h_ref[...] = (_silu(a) * b).astype(BF16)

    w_spec = pl.BlockSpec((tf, d), lambda i, j: (j, 0))
    o_spec = pl.BlockSpec((tm, tf), lambda i, j: (i, j))
    return pl.pallas_call(
        body, name=name,
        out_shape=[jax.ShapeDtypeStruct((t, f), BF16)] * 3,
        grid=(t // tm, f // tf), in_specs=[pl.BlockSpec((tm, d), lambda i, j: (i, 0)), w_spec, w_spec],
        out_specs=[o_spec, o_spec, o_spec], compiler_params=_params("arbitrary", "arbitrary"),
    )(u, wt_gate, wt_up)


def swiglu_out_backward(dy, w_down, a, b, name):
    t, d = dy.shape
    f = w_down.shape[0]
    tm, tf = _swiglu_tiles(t, f)

    def body(dy_ref, w_ref, a_ref, b_ref, da_ref, db_ref):
        dh = _nt(dy_ref[...], w_ref[...])
        av = a_ref[...].astype(F32)
        sig = jax.nn.sigmoid(av)
        da_ref[...] = (dh * b_ref[...].astype(F32) * (sig * (1.0 + av * (1.0 - sig)))).astype(BF16)
        db_ref[...] = (dh * (av * sig)).astype(BF16)

    spec = pl.BlockSpec((tm, tf), lambda i, j: (i, j))
    return pl.pallas_call(
        body, name=name, out_shape=[jax.ShapeDtypeStruct((t, f), BF16)] * 2, grid=(t // tm, f // tf),
        in_specs=[pl.BlockSpec((tm, d), lambda i, j: (i, 0)), pl.BlockSpec((tf, d), lambda i, j: (j, 0)), spec, spec],
        out_specs=[spec, spec], compiler_params=_params("arbitrary", "arbitrary"),
    )(dy, w_down, a, b)


def rope_tables(pos, inv_freq, sign, name):
    t = pos.shape[0]
    tm = _tile(t, 512, 8)

    def body(p_ref, f_ref, s_ref, c_out, s_out):
        ang = p_ref[...] * f_ref[...]
        c_out[...] = jnp.cos(ang)
        s_out[...] = jnp.sin(ang) * s_ref[...]

    vec = pl.BlockSpec((1, LANES), lambda i: (0, 0))
    tab = pl.BlockSpec((tm, LANES), lambda i: (i, 0))
    return pl.pallas_call(
        body, name=name, out_shape=[jax.ShapeDtypeStruct((t, LANES), F32)] * 2, grid=(t // tm,),
        in_specs=[pl.BlockSpec((tm, 1), lambda i: (i, 0)), vec, vec], out_specs=[tab, tab],
        compiler_params=_params("arbitrary"),
    )(pos, inv_freq, sign)


def _rot_half(v):
    lane = lax.broadcasted_iota(jnp.int32, v.shape, v.ndim - 1)
    up = pltpu.roll(v, LANES - MLA_ROPE // 2, v.ndim - 1)
    down = pltpu.roll(v, MLA_ROPE // 2, v.ndim - 1)
    return jnp.where(lane % MLA_ROPE < MLA_ROPE // 2, up, down)


def _rope(v, cos, sin_signed):
    return v * cos + _rot_half(v) * sin_signed


def _rope_transposed(dv, cos, sin_signed):
    return dv * cos + _rot_half(dv * sin_signed)


def rope_slabs(v, cos, sin_signed, out_dtype, name, transposed=False):
    ns, t, _ = v.shape
    tm = _tile(t, 1024, 8)
    fn = _rope_transposed if transposed else _rope

    def body(v_ref, c_ref, s_ref, o_ref):
        for j in range(ns):
            o_ref[j] = fn(v_ref[j].astype(F32), c_ref[...], s_ref[...]).astype(out_dtype)

    tab = pl.BlockSpec((tm, LANES), lambda i: (i, 0))
    spec = pl.BlockSpec((ns, tm, LANES), lambda i: (0, i, 0))
    return pl.pallas_call(
        body, name=name, out_shape=jax.ShapeDtypeStruct(v.shape, out_dtype), grid=(t // tm,),
        in_specs=[spec, tab, tab], out_specs=spec, compiler_params=_params("arbitrary"),
    )(v, cos, sin_signed)


def _rms(x):
    rinv = lax.rsqrt(jnp.mean(x * x, axis=-1, keepdims=True) + NORM_EPS)
    return x * rinv, rinv


def mla_latents_forward(h_in, g_q, g_kv, cos, sin_signed, name):
    t = h_in.shape[0]
    tm = _tile(t, 512, 8)

    def body(h_ref, gq_ref, gkv_ref, c_ref, s_ref, cq_ref, ckv_ref, kr_ref):
        cq_ref[...] = (_rms(h_ref[:, 0:MLA_QR])[0] * gq_ref[...]).astype(BF16)
        ckv_ref[...] = (_rms(h_ref[:, MLA_QR:MLA_QR + MLA_KVR])[0] * gkv_ref[...]).astype(BF16)
        kr_ref[...] = _rope(h_ref[:, MLA_QR + MLA_KVR:], c_ref[...], s_ref[...]).astype(BF16)

    def tok(w):
        return pl.BlockSpec((tm, w), lambda i: (i, 0))

    def vec(w):
        return pl.BlockSpec((1, w), lambda i: (0, 0))

    return pl.pallas_call(
        body, name=name,
        out_shape=[jax.ShapeDtypeStruct((t, MLA_QR), BF16), jax.ShapeDtypeStruct((t, MLA_KVR), BF16),
                   jax.ShapeDtypeStruct((t, LANES), BF16)],
        grid=(t // tm,),
        in_specs=[tok(h_in.shape[1]), vec(MLA_QR), vec(MLA_KVR), tok(LANES), tok(LANES)],
        out_specs=[tok(MLA_QR), tok(MLA_KVR), tok(LANES)],
        compiler_params=_params("arbitrary"),
    )(h_in, g_q, g_kv, cos, sin_signed)


def mla_latents_backward(h_in, dcq, dckv, dkr, g_q, g_kv, cos, sin_signed, name):
    t, w = h_in.shape
    tm = _tile(t, 512, 8)

    def body(h_ref, dcq_ref, dckv_ref, dkr_ref, gq_ref, gkv_ref, c_ref, s_ref, dh_ref, dgq_ref, dgkv_ref):
        @pl.when(pl.program_id(0) == 0)
        def _():
            dgq_ref[...] = jnp.zeros_like(dgq_ref)
            dgkv_ref[...] = jnp.zeros_like(dgkv_ref)

        def rms_bwd(x, dc, g_ref, dg_ref):
            xn, rinv = _rms(x)
            dg_ref[...] += jnp.sum(dc * xn, axis=0, keepdims=True)
            dxn = dc * g_ref[...]
            return rinv * (dxn - xn * jnp.mean(dxn * xn, axis=-1, keepdims=True))

        dq = rms_bwd(h_ref[:, 0:MLA_QR], dcq_ref[...], gq_ref, dgq_ref)
        dkv = rms_bwd(h_ref[:, MLA_QR:MLA_QR + MLA_KVR], dckv_ref[...], gkv_ref, dgkv_ref)
        dr = _rope_transposed(dkr_ref[...], c_ref[...], s_ref[...])
        dh_ref[...] = jnp.concatenate([dq, dkv, dr], axis=1).astype(BF16)

    def tok(ww):
        return pl.BlockSpec((tm, ww), lambda i: (i, 0))

    def vec(ww):
        return pl.BlockSpec((1, ww), lambda i: (0, 0))

    return pl.pallas_call(
        body, name=name,
        out_shape=[jax.ShapeDtypeStruct((t, w), BF16), jax.ShapeDtypeStruct((1, MLA_QR), F32),
                   jax.ShapeDtypeStruct((1, MLA_KVR), F32)],
        grid=(t // tm,),
        in_specs=[tok(w), tok(MLA_QR), tok(MLA_KVR), tok(LANES), vec(MLA_QR), vec(MLA_KVR), tok(LANES), tok(LANES)],
        out_specs=[tok(w), vec(MLA_QR), vec(MLA_KVR)],
        compiler_params=_params("arbitrary"),
    )(h_in, dcq, dckv, dkr, g_q, g_kv, cos, sin_signed)


def _tri(n, lower):
    r = lax.broadcasted_iota(jnp.int32, (n, n), 0)
    c = lax.broadcasted_iota(jnp.int32, (n, n), 1)
    return jnp.where(r >= c if lower else r <= c, 1.0, 0.0).astype(F32)


def _dot_exact(tri, v):
    hi = v.astype(BF16)
    mid = (v - hi.astype(F32)).astype(BF16)
    lo = (v - hi.astype(F32) - mid.astype(F32)).astype(BF16)
    t = tri.astype(BF16)
    return _nn(t, hi) + _nn(t, mid) + _nn(t, lo)


def fox_gate_forward(z, b_f, bl, name):
    t = z.shape[0]
    s = t // bl
    ch = LANES
    n_ch = s // ch

    def body(z_ref, b_ref, f_ref, fs_ref):
        tri = _tri(ch, True)
        carry = jnp.zeros((1, LANES), F32)
        for k in range(n_ch):
            x = z_ref[k * ch:(k + 1) * ch, :] + b_ref[...]
            logf = jnp.minimum(x, 0.0) - jnp.log(1.0 + jnp.exp(-jnp.abs(x)))
            cs = _dot_exact(tri, logf) + carry
            carry = cs[ch - 1:ch, :]
            f_ref[k * ch:(k + 1) * ch, :] = cs
            for h in range(FOX_HEADS):
                fs_ref[h, k * ch:(k + 1) * ch, :] = jnp.broadcast_to(cs[:, h:h + 1], (ch, LANES))

    return pl.pallas_call(
        body, name=name,
        out_shape=[jax.ShapeDtypeStruct((t, LANES), F32), jax.ShapeDtypeStruct((FOX_HEADS, t, LANES), F32)],
        grid=(bl,),
        in_specs=[pl.BlockSpec((s, LANES), lambda b: (b, 0)), pl.BlockSpec((1, LANES), lambda b: (0, 0))],
        out_specs=[pl.BlockSpec((s, LANES), lambda b: (b, 0)),
                   pl.BlockSpec((FOX_HEADS, s, LANES), lambda b: (0, b, 0))],
        compiler_params=_params("arbitrary"),
    )(z, b_f)


def fox_gate_backward(z, b_f, df, bl, name):
    t = z.shape[0]
    s = t // bl
    ch = LANES
    n_ch = s // ch

    def body(z_ref, b_ref, df_ref, dz_ref, db_ref):
        @pl.when(pl.program_id(0) == 0)
        def _():
            db_ref[...] = jnp.zeros_like(db_ref)

        tri = _tri(ch, False)
        carry = jnp.zeros((1, LANES), F32)
        for k in reversed(range(n_ch)):
            cs = _dot_exact(tri, df_ref[k * ch:(k + 1) * ch, :]) + carry
            carry = cs[0:1, :]
            x = z_ref[k * ch:(k + 1) * ch, :] + b_ref[...]
            dz = cs * (1.0 - jax.nn.sigmoid(x))
            dz_ref[k * ch:(k + 1) * ch, :] = dz
            db_ref[...] += jnp.sum(dz, axis=0, keepdims=True)

    tok = pl.BlockSpec((s, LANES), lambda b: (b, 0))
    vec = pl.BlockSpec((1, LANES), lambda b: (0, 0))
    return pl.pallas_call(
        body, name=name,
        out_shape=[jax.ShapeDtypeStruct((t, LANES), F32), jax.ShapeDtypeStruct((1, LANES), F32)],
        grid=(bl,), in_specs=[tok, vec, tok], out_specs=[tok, vec],
        compiler_params=_params("arbitrary"),
    )(z, b_f, df)


NEG_INF = float("-inf")


def _attn_tiles(s):
    return _tile(s, 1024, 8)


def attention_forward(kind, ops, bl, scale, name):
    fox = kind == "fox"
    if fox:
        assert math.frexp(scale)[0] == 0.5, "the FoX scale is folded into bf16 queries: it must be a power of two"
        qkv, fq, fk = ops
        t = qkv.shape[1]
        n_pair = FOX_HEADS // 2
    else:
        qn, qr, kn, kr, v = ops
        t = qn.shape[1]
        n_pair = MLA_HEADS // 2
    s = t // bl
    tq = _attn_tiles(s)
    nq = s // tq
    half = LANES // 2

    def body(*refs):
        if fox:
            q_ref, k_ref, v_ref, fq_ref, fk_ref, o_ref, lse_ref, o32_ref = refs
        else:
            qn_ref, qr_ref, kn_ref, kr_ref, v_ref, o_ref, lse_ref = refs
        i = pl.program_id(2)
        row = lax.broadcasted_iota(jnp.int32, (tq, tq), 0)
        col = lax.broadcasted_iota(jnp.int32, (tq, tq), 1)
        heads = []
        for e in range(2):
            sl = slice(e * half, (e + 1) * half)
            if fox:
                heads.append((sl, q_ref[0, :, sl] * jnp.asarray(scale, BF16), None))
            else:
                heads.append((sl, jnp.concatenate([qn_ref[e], qr_ref[0, :, sl], jnp.zeros((tq, half), BF16)], axis=1),
                              None))
        dv = half if fox else LANES

        def wide(stat):
            return jnp.concatenate([stat] * (tq // LANES), axis=1)

        def step(j, carry, masked):
            rows = pl.ds(pl.multiple_of(j * tq, tq), tq)
            new = []
            for e, (sl, qa, qb) in enumerate(heads):
                m, l, acc = carry[e]
                if fox:
                    sc = _nt(qa, k_ref[0, rows, sl]) + wide(fq_ref[e]) - fk_ref[0, j, e:e + 1, :]
                    vv = v_ref[0, rows, sl]
                else:
                    k_cat = jnp.concatenate([kn_ref[e, rows, :], kr_ref[rows, :]], axis=1)
                    sc = _nt(qa, k_cat) * scale
                    vv = v_ref[e, rows, :]
                if masked:
                    sc = jnp.where(row >= col, sc, NEG_INF)
                m_new = jnp.maximum(m, jnp.max(sc, axis=1, keepdims=True))
                p = jnp.exp(sc - m_new)
                a = jnp.exp(m - m_new)
                p_hi = p.astype(BF16)
                if fox:
                    vv = jnp.concatenate([vv, ones], axis=1)
                    acc = a * acc + _nn(p_hi, vv) + _nn((p - p_hi.astype(F32)).astype(BF16), vv)
                else:
                    l = a * l + jnp.sum(p, axis=1, keepdims=True)
                    acc = a * acc + _nn(p_hi, vv)
                new.append((m_new, l, acc))
            return tuple(new)

        ones = jnp.ones((tq, half), BF16)
        acc_w = LANES if fox else dv
        init = (jnp.full((tq, 1), NEG_INF, F32), jnp.zeros((tq, 1), F32), jnp.zeros((tq, acc_w), F32))
        carry = step(i, (init, init), True)
        carry = lax.fori_loop(0, i, lambda j, c: step(j, c, False), carry)
        if fox:
            carry = [(m, acc[:, dv:dv + 1], acc[:, :dv]) for m, _, acc in carry]
        outs = [acc / l for _, l, acc in carry]
        for e, (m, l, _) in enumerate(carry):
            lse_ref[e] = jnp.broadcast_to(m + jnp.log(l), (tq, LANES))
        if fox:
            o32 = jnp.concatenate(outs, axis=1)
            o32_ref[0] = o32
            o_ref[0] = o32.astype(BF16)
        else:
            o_ref[0] = outs[0].astype(BF16)
            o_ref[1] = outs[1].astype(BF16)

    def q_idx(b, g, i):
        return (g, b * nq + i, 0)

    if fox:
        nk = fk.shape[1]
        in_specs = [pl.BlockSpec((1, tq, LANES), q_idx),
                    pl.BlockSpec((1, s, LANES), lambda b, g, i: (n_pair + g, b, 0)),
                    pl.BlockSpec((1, s, LANES), lambda b, g, i: (2 * n_pair + g, b, 0)),
                    pl.BlockSpec((2, tq, LANES), q_idx),
                    pl.BlockSpec((1, nk, 8, tq), lambda b, g, i: (b * n_pair + g, 0, 0, 0))]
        args = [qkv, qkv, qkv, fq, fk]
        o_spec = pl.BlockSpec((1, tq, LANES), q_idx)
    else:
        in_specs = [pl.BlockSpec((2, tq, LANES), q_idx),
                    pl.BlockSpec((1, tq, LANES), q_idx),
                    pl.BlockSpec((2, s, LANES), lambda b, g, i: (g, b, 0)),
                    pl.BlockSpec((s, LANES), lambda b, g, i: (b, 0)),
                    pl.BlockSpec((2, s, LANES), lambda b, g, i: (g, b, 0))]
        args = [qn, qr, kn, kr, v]
        o_spec = pl.BlockSpec((2, tq, LANES), q_idx)
    out_shape = [jax.ShapeDtypeStruct((8, t, LANES), BF16), jax.ShapeDtypeStruct((2 * n_pair, t, LANES), F32)]
    out_specs = [o_spec, pl.BlockSpec((2, tq, LANES), q_idx)]
    if fox:
        out_shape.append(jax.ShapeDtypeStruct((8, t, LANES), F32))
        out_specs.append(o_spec)
    outs = pl.pallas_call(
        body, name=name, out_shape=out_shape, grid=(bl, n_pair, nq), in_specs=in_specs, out_specs=out_specs,
        compiler_params=_params("arbitrary", "arbitrary", "arbitrary"),
    )(*args)
    return (outs[0], outs[1], outs[2] if fox else outs[0])


def attention_backward(kind, ops, o, do, lse, bl, scale, name):
    fox = kind == "fox"
    if fox:
        assert math.frexp(scale)[0] == 0.5, "the FoX scale is folded into bf16 queries: it must be a power of two"
        qkv, fq, fk = ops
        t = qkv.shape[1]
        n_pair = FOX_HEADS // 2
    else:
        qn, qr, kn, kr, v = ops
        t = qn.shape[1]
        n_pair = MLA_HEADS // 2
    s = t // bl
    tq = _attn_tiles(s)
    nq = s // tq
    half = LANES // 2

    def body(*refs):
        if fox:
            (q_ref, k_ref, v_ref, fq_ref, fk_ref, o_ref, do_ref, lse_ref,
             dq_ref, dk_ref, dv_ref, dfk_ref, delta_scr, qt_scr, dot_scr) = refs
        else:
            (qn_ref, qr_ref, kn_ref, kr_ref, v_ref, o_ref, do_ref, lse_ref,
             dqn_ref, dqr_ref, dkn_ref, dv_ref, dkr_ref, delta_scr, qt_scr, qrt_scr, dot_scr) = refs
        g, j = pl.program_id(1), pl.program_id(2)
        row = lax.broadcasted_iota(jnp.int32, (tq, tq), 0)
        col = lax.broadcasted_iota(jnp.int32, (tq, tq), 1)
        krows = pl.ds(pl.multiple_of(j * tq, tq), tq)
        q_scale = jnp.asarray(scale, BF16)

        def transposed(v):
            return v.astype(F32).T.astype(BF16)

        def wide(stat):
            return jnp.concatenate([stat] * (tq // LANES), axis=1)

        @pl.when(j == 0)
        def _():
            if fox:
                dq_ref[...] = jnp.zeros_like(dq_ref)
            else:
                dqn_ref[...] = jnp.zeros_like(dqn_ref)
                dqr_ref[...] = jnp.zeros_like(dqr_ref)
            for ii in range(nq):
                rws = slice(ii * tq, (ii + 1) * tq)
                deltas = []
                if fox:
                    prod = do_ref[0, rws, :].astype(F32) * o_ref[0, rws, :].astype(F32)
                    for e in range(2):
                        deltas.append(jnp.sum(prod[:, e * half:(e + 1) * half], axis=1, keepdims=True))
                    qt_scr[ii] = transposed(q_ref[0, rws, :] * q_scale)
                    dot_scr[ii] = transposed(do_ref[0, rws, :])
                else:
                    for e in range(2):
                        prod = do_ref[e, rws, :].astype(F32) * o_ref[e, rws, :].astype(F32)
                        deltas.append(jnp.sum(prod, axis=1, keepdims=True))
                        qt_scr[e, ii] = transposed(qn_ref[e, rws, :])
                        dot_scr[e, ii] = transposed(do_ref[e, rws, :])
                    qrt_scr[ii] = transposed(qr_ref[0, rws, :])
                for e in range(2):
                    delta_scr[e, rws, :] = jnp.broadcast_to(deltas[e], (tq, LANES))

        if fox:
            dfk_ref[...] = jnp.zeros_like(dfk_ref)
        else:
            @pl.when(jnp.logical_and(g == 0, j == 0))
            def _():
                dkr_ref[...] = jnp.zeros_like(dkr_ref)

        heads = []
        for e in range(2):
            sl = slice(e * half, (e + 1) * half)
            if fox:
                heads.append((sl, k_ref[0, :, sl], v_ref[0, :, sl], fk_ref[0, 0, e:e + 1, :]))
            else:
                heads.append((sl, jnp.concatenate([kn_ref[e], kr_ref[krows, :]], axis=1), v_ref[e], None))
        dk_w = dv_w = half if fox else LANES

        def step(i, carry, masked):
            rows = pl.ds(pl.multiple_of(i * tq, tq), tq)
            new = []
            for e, (sl, k_e, v_e, x_e) in enumerate(heads):
                dk_acc, dv_acc, last = carry[e]
                if fox:
                    do_i = do_ref[0, rows, sl]
                    sc = _nt(q_ref[0, rows, sl] * q_scale, k_e) + wide(fq_ref[e, rows, :]) - x_e
                else:
                    do_i = do_ref[e, rows, :]
                    q_cat = jnp.concatenate([qn_ref[e, rows, :], qr_ref[0, rows, sl], jnp.zeros((tq, half), BF16)], axis=1)
                    sc = _nt(q_cat, k_e) * scale
                if masked:
                    sc = jnp.where(row >= col, sc, NEG_INF)
                p = jnp.exp(sc - wide(lse_ref[e, rows, :]))
                dp = _nt(do_i, v_e)
                ds = p * (dp - wide(delta_scr[e, rows, :]))
                dsb = ds.astype(BF16) if fox else (ds * scale).astype(BF16)
                if fox:
                    fsl = slice(e * half, (e + 1) * half)
                    dv_acc = dv_acc + _nn(dot_scr[i, fsl, :], p.astype(BF16))
                    dk_acc = dk_acc + _nn(qt_scr[i, fsl, :], dsb)
                    dq_ref[0, rows, sl] += _nn(dsb, k_e) * scale
                    last = last - jnp.sum(ds, axis=0, keepdims=True)
                else:
                    dv_acc = dv_acc + _nn(dot_scr[e, i], p.astype(BF16))
                    dk_acc = dk_acc + _nn(qt_scr[e, i], dsb)
                    dq_cat = _nn(dsb, k_e)
                    dqn_ref[e, rows, :] += dq_cat[:, :LANES]
                    dqr_ref[0, rows, sl] += dq_cat[:, LANES:LANES + half]
                    last = last + _nn(qrt_scr[i, e * half:(e + 1) * half, :], dsb)
                new.append((dk_acc, dv_acc, last))
            return tuple(new)

        last0 = jnp.zeros((1, tq), F32) if fox else jnp.zeros((half, tq), F32)
        init = (jnp.zeros((dk_w, tq), F32), jnp.zeros((dv_w, tq), F32), last0)
        carry = step(j, (init, init), True)
        carry = lax.fori_loop(j + 1, nq, lambda i, c: step(i, c, False), carry)
        if fox:
            for e in range(2):
                dfk_ref[0, 0, e:e + 1, :] = carry[e][2]
            dk_ref[0] = jnp.concatenate([carry[0][0], carry[1][0]], axis=0).T.astype(BF16)
            dv_ref[0] = jnp.concatenate([carry[0][1], carry[1][1]], axis=0).T.astype(BF16)
        else:
            for e in range(2):
                dkn_ref[e] = carry[e][0].T.astype(BF16)
                dv_ref[e] = carry[e][1].T.astype(BF16)
            dkr_t = carry[0][2] + carry[1][2]
            dkr_ref[krows, :] += jnp.concatenate([dkr_t, jnp.zeros_like(dkr_t)], axis=0).T

    def whole(b, g, j):
        return (g, b, 0)

    def kblk(b, g, j):
        return (g, b * nq + j, 0)

    if fox:
        in_specs = [pl.BlockSpec((1, s, LANES), whole),
                    pl.BlockSpec((1, tq, LANES), lambda b, g, j: (n_pair + g, b * nq + j, 0)),
                    pl.BlockSpec((1, tq, LANES), lambda b, g, j: (2 * n_pair + g, b * nq + j, 0)),
                    pl.BlockSpec((2, s, LANES), whole),
                    pl.BlockSpec((1, 1, 8, tq), lambda b, g, j: (b * n_pair + g, j, 0, 0)),
                    pl.BlockSpec((1, s, LANES), whole), pl.BlockSpec((1, s, LANES), whole),
                    pl.BlockSpec((2, s, LANES), whole)]
        args = [qkv, qkv, qkv, fq, fk, o, do, lse]
        out_shape = [jax.ShapeDtypeStruct((8, t, LANES), F32), jax.ShapeDtypeStruct((8, t, LANES), BF16),
                     jax.ShapeDtypeStruct((8, t, LANES), BF16), jax.ShapeDtypeStruct(fk.shape, F32)]
        out_specs = [pl.BlockSpec((1, s, LANES), whole), pl.BlockSpec((1, tq, LANES), kblk),
                     pl.BlockSpec((1, tq, LANES), kblk),
                     pl.BlockSpec((1, 1, 8, tq), lambda b, g, j: (b * n_pair + g, j, 0, 0))]
    else:
        pair = pl.BlockSpec((2, s, LANES), whole)
        pair_k = pl.BlockSpec((2, tq, LANES), kblk)
        in_specs = [pair, pl.BlockSpec((1, s, LANES), whole), pair_k,
                    pl.BlockSpec((s, LANES), lambda b, g, j: (b, 0)), pair_k,
                    pair, pair, pair]
        args = [qn, qr, kn, kr, v, o, do, lse]
        out_shape = [jax.ShapeDtypeStruct((8, t, LANES), F32), jax.ShapeDtypeStruct((4, t, LANES), F32),
                     jax.ShapeDtypeStruct((8, t, LANES), BF16), jax.ShapeDtypeStruct((8, t, LANES), BF16),
                     jax.ShapeDtypeStruct((t, LANES), F32)]
        out_specs = [pair, pl.BlockSpec((1, s, LANES), whole), pair_k, pair_k,
                     pl.BlockSpec((s, LANES), lambda b, g, j: (b, 0))]
    t_blocks = pltpu.VMEM((nq, LANES, tq), BF16)
    t_pairs = pltpu.VMEM((2, nq, LANES, tq), BF16)
    scratch = [pltpu.VMEM((2, s, LANES), F32)] + ([t_blocks, t_blocks] if fox else [t_pairs, t_blocks, t_pairs])
    return pl.pallas_call(
        body, name=name, out_shape=out_shape, grid=(bl, n_pair, nq), in_specs=in_specs, out_specs=out_specs,
        scratch_shapes=scratch, compiler_params=_params("arbitrary", "arbitrary", "arbitrary"),
    )(*args)


def adamw(w, g, m, v, name):
    shape = w.shape
    c = shape[-1]
    r = w.size // c
    tr = _tile(r, 512, 8)

    def body(w_ref, g_ref, m_ref, v_ref, d_ref, nm_ref, nv_ref):
        gv = g_ref[...]
        m2 = ADAM_B1 * m_ref[...] + (1.0 - ADAM_B1) * gv
        v2 = ADAM_B2 * v_ref[...] + (1.0 - ADAM_B2) * (gv * gv)
        m_hat = m2 / (1.0 - ADAM_B1 ** ADAM_STEP)
        v_hat = v2 / (1.0 - ADAM_B2 ** ADAM_STEP)
        d_ref[...] = -ADAM_LR * (m_hat / (jnp.sqrt(v_hat) + ADAM_EPS) + ADAM_WD * w_ref[...])
        nm_ref[...] = m2
        nv_ref[...] = v2

    spec = pl.BlockSpec((tr, c), lambda i: (i, 0))
    outs = pl.pallas_call(
        body, name=name, out_shape=[jax.ShapeDtypeStruct((r, c), F32)] * 3, grid=(r // tr,),
        in_specs=[spec] * 4, out_specs=[spec] * 3, compiler_params=_params("arbitrary"),
    )(*(a.reshape(r, c) for a in (w, g, m, v)))
    return tuple(a.reshape(shape) for a in outs)


PACK_COLS = 1024


def _pack_rows(a):
    return a.reshape(-1, PACK_COLS)


def kernel(x, c, positions, mla_w_in, mla_g_q, mla_w_uq, mla_g_kv, mla_w_uk, mla_w_uv, mla_w_o, fox_w_in, fox_b_f, fox_w_o, ada_w, ada_b, ffn_w_gate, ffn_w_up, ffn_w_down, ln_g, ln_b, loss_target, m_mla_w_in, m_mla_g_q, m_mla_w_uq, m_mla_g_kv, m_mla_w_uk, m_mla_w_uv, m_mla_w_o, m_fox_w_in, m_fox_b_f, m_fox_w_o, m_ada_w, m_ada_b, m_ffn_w_gate, m_ffn_w_up, m_ffn_w_down, m_ln_g, m_ln_b, v_mla_w_in, v_mla_g_q, v_mla_w_uq, v_mla_g_kv, v_mla_w_uk, v_mla_w_uv, v_mla_w_o, v_fox_w_in, v_fox_b_f, v_fox_w_o, v_ada_w, v_ada_b, v_ffn_w_gate, v_ffn_w_up, v_ffn_w_down, v_ln_g, v_ln_b):
    bl, s, d = x.shape
    t = bl * s
    ff = ffn_w_gate.shape[-1] * N_DEV
    dev = 4 * lax.axis_index("x") + 2 * lax.axis_index("y") + lax.axis_index("c")
    ada_cols = ada_w.shape[-1]
    mla_in = mla_w_in.shape[-1]
    mla_in_pad = mla_in + (-mla_in) % LANES

    def t_last(a):
        return jnp.swapaxes(a, -1, -2)

    local = {
        "mla_w_in": mla_w_in[0],
        "mla_w_uq": t_last(mla_w_uq[0]),
        "mla_w_uk": t_last(mla_w_uk[0]),
        "mla_w_uv": t_last(mla_w_uv[0]),
        "mla_w_o": mla_w_o[0],
        "fox_w_in": t_last(fox_w_in[0]),
        "fox_w_o": fox_w_o[0],
    }
    for i in range(DEPTH):
        local.update({f"gate{i}": t_last(ffn_w_gate[i]), f"up{i}": t_last(ffn_w_up[i]), f"down{i}": ffn_w_down[i]})
    groups = [["mla_w_in", "mla_w_uq", "mla_w_uk", "mla_w_uv", "mla_w_o"],
              ["gate0", "up0", "down0"],
              ["fox_w_in", "fox_w_o"],
              ["gate1", "up1", "down1"]]
    offsets, rows_of, slot_of, group_of = {}, {}, {}, {}
    group_rows = []
    for gi, names in enumerate(groups):
        rows = 0
        for nm in names:
            rows_of[nm] = local[nm].size // PACK_COLS
            slot_of[nm] = rows_of[nm] + (-rows_of[nm]) % 16
            offsets[nm] = rows
            group_of[nm] = gi
            rows += slot_of[nm]
        group_rows.append(rows)

    def slot(nm, rows):
        pad = [(0, 0)] * rows.ndim
        pad[-2] = (0, slot_of[nm] - rows_of[nm])
        return jnp.pad(rows, pad)

    def held_until(block, arrays):
        zero = sum((a.reshape(-1)[0] * 0).astype(F32) for a in jax.tree.leaves(arrays))
        return block + zero.astype(block.dtype)

    def landing(block):
        land = lax.empty((N_DEV,) + block.shape, block.dtype)
        return lax.dynamic_update_slice(land, block[None], (dev, 0, 0))

    packed0 = jnp.concatenate([slot(nm, _pack_rows(local[nm]).astype(BF16)) for nm in groups[0]], axis=0)
    gathered0 = all_gather(packed0, "gather_mla_weights")
    gathered = {nm: gathered0[:, offsets[nm]:offsets[nm] + rows_of[nm], :] for nm in groups[0]}
    gather_started = [None] * len(groups)

    def depart(gi, after):
        blocks = [held_until(_pack_rows(local[nm]).astype(BF16), after) for nm in groups[gi]]
        gather_started[gi] = exchange_start(blocks, [landing(b) for b in blocks], f"gather_group{gi}_start", False)
        return gather_started[gi][4]

    def full(nm, cols):
        return gathered[nm].reshape(-1, cols)

    w_in = jnp.pad(full("mla_w_in", mla_in), ((0, 0), (0, mla_in_pad - mla_in)))
    wt_uq = full("mla_w_uq", MLA_QR).reshape(MLA_HEADS, MLA_NOPE + MLA_ROPE, MLA_QR)
    wt_uq_n = wt_uq[:, :MLA_NOPE].reshape(MLA_HEADS * MLA_NOPE, MLA_QR)
    wt_uq_r = wt_uq[:, MLA_NOPE:].reshape(MLA_HEADS * MLA_ROPE, MLA_QR)
    wt_uk = full("mla_w_uk", MLA_KVR)
    wt_uv = full("mla_w_uv", MLA_KVR)
    w_mo = full("mla_w_o", d)
    wt_gate, wt_up, w_down = [None] * DEPTH, [None] * DEPTH, [None] * DEPTH

    def arrive(gi, after):
        if gi + 1 < len(groups):
            after = depart(gi + 1, after)
        landed = list(exchange_wait(gather_started[gi], after, f"gather_group{gi}_wait", False))
        gathered.update(zip(groups[gi], landed))
        for i in range(DEPTH):
            if group_of[f"gate{i}"] == gi:
                wt_gate[i], wt_up[i], w_down[i] = full(f"gate{i}", d), full(f"up{i}", d), full(f"down{i}", d)

    small = jnp.concatenate([c.reshape(-1, LANES), ln_g.reshape(-1, LANES), ln_b.reshape(-1, LANES)], axis=0)
    small_rows = small.shape[0]
    small = jnp.pad(small, ((0, (-small_rows) % 8), (0, 0)))
    small_all = all_gather(small, "gather_small")
    c_rows = bl * d // LANES
    c_all = small_all[:, :c_rows].reshape(N_DEV * bl, d)
    n_ln = DEPTH * 2
    ln_g_all = small_all[:, c_rows:c_rows + n_ln, :].transpose(1, 0, 2).reshape(DEPTH, 2, 1, d)
    ln_b_all = small_all[:, c_rows + n_ln:c_rows + 2 * n_ln, :].transpose(1, 0, 2).reshape(DEPTH, 2, 1, d)

    c_act = silu_rows(c_all, "silu_c")
    ada_b_loc = lax.dynamic_slice_in_dim(ada_b, dev * ada_cols, ada_cols, axis=1)
    mod_cols = [mm([(c_act, ada_w[i])], trans_b=False, out_dtype=F32, name=f"ada_fwd{i}", bias=ada_b_loc[i][None, :])
                for i in range(DEPTH)]
    mod_all = all_gather(jnp.concatenate(mod_cols, axis=0), "gather_mod")
    mod_all = mod_all.reshape(N_DEV, DEPTH, N_DEV * bl, ada_cols).transpose(1, 2, 0, 3).reshape(DEPTH, N_DEV * bl, 6 * d)
    mod_mine = lax.dynamic_slice_in_dim(mod_all, dev * bl, bl, axis=1)
    mods = [mod_mine[i].reshape(bl * 6, 1, d) for i in range(DEPTH)]
    mods[0] = mods[0] + depart(1, (mod_mine, gathered0))[0, 0]

    half_r = MLA_ROPE // 2
    inv_freq = ROPE_THETA ** (-jnp.arange(half_r, dtype=F32) / half_r)
    inv_freq = jnp.tile(inv_freq, LANES // half_r)[None, :]
    sign = jnp.tile(jnp.concatenate([-jnp.ones((half_r,), F32), jnp.ones((half_r,), F32)]), LANES // MLA_ROPE)[None, :]
    cos_t, sin_t = rope_tables(positions.astype(F32).reshape(t, 1), inv_freq, sign, "rope_tables")

    x2d = x.reshape(t, d)
    g_q, g_kv = mla_g_q.reshape(1, MLA_QR), mla_g_kv.reshape(1, MLA_KVR)
    b_f = jnp.pad(fox_b_f.reshape(1, FOX_HEADS), ((0, 0), (0, LANES - FOX_HEADS)))
    mla_scale = (MLA_NOPE + MLA_ROPE) ** -0.5
    fox_scale = FOX_HD ** -0.5
    tq = _attn_tiles(s)
    nk = s // tq

    saved = []
    u = modulate(x2d, mods[0], 0, 1, bl, "modulate0")
    xin = x2d
    for i in range(DEPTH):
        sv = {"u": u, "x_in": xin}
        if i % 2 == 0:
            h_in = mm([(u, w_in)], trans_b=False, out_dtype=F32, name=f"mla_in{i}")
            c_q, c_kv, k_r = mla_latents_forward(h_in, g_q, g_kv, cos_t, sin_t, f"mla_latents{i}")
            q_n = mm([(c_q, wt_uq_n)], trans_b=True, out_dtype=BF16, out_slab=True, name=f"mla_qn{i}")
            q_r_raw = mm([(c_q, wt_uq_r)], trans_b=True, out_dtype=F32, out_slab=True, name=f"mla_qr{i}")
            q_r = rope_slabs(q_r_raw, cos_t, sin_t, BF16, f"mla_qrope{i}")
            k_n = mm([(c_kv, wt_uk)], trans_b=True, out_dtype=BF16, out_slab=True, name=f"mla_kn{i}")
            v_m = mm([(c_kv, wt_uv)], trans_b=True, out_dtype=BF16, out_slab=True, name=f"mla_v{i}")
            ops = (q_n, q_r, k_n, k_r, v_m)
            o, lse, o_delta = attention_forward("mla", ops, bl, mla_scale, f"mla_attn{i}")
            y = mm([(o, w_mo)], trans_b=False, out_dtype=F32, name=f"mla_out{i}")
            sv.update(h_in=h_in, c_q=c_q, c_kv=c_kv, ops=ops, o=o, lse=lse, o_delta=o_delta)
        else:
            arrive(2, u)
            wt_fox = full("fox_w_in", d)
            wt_qkv = wt_fox[:3 * d]
            wt_f = jnp.pad(wt_fox[3 * d:], ((0, LANES - FOX_HEADS), (0, 0)))
            w_fo = full("fox_w_o", d)
            qkv = mm([(u, wt_qkv)], trans_b=True, out_dtype=BF16, out_slab=True, name=f"fox_qkv{i}")
            z = mm([(u, wt_f)], trans_b=True, out_dtype=F32, name=f"fox_z{i}")
            f_tok, f_q = fox_gate_forward(z, b_f, bl, f"fox_gate{i}")
            f_k = f_tok[:, :FOX_HEADS].reshape(bl, nk, tq, FOX_HEADS // 2, 2).transpose(0, 3, 1, 4, 2)
            f_k = jnp.pad(f_k.reshape(bl * FOX_HEADS // 2, nk, 2, tq), ((0, 0), (0, 0), (0, 6), (0, 0)))
            ops = (qkv, f_q, f_k)
            o, lse, o_delta = attention_forward("fox", ops, bl, fox_scale, f"fox_attn{i}")
            y = mm([(o, w_fo)], trans_b=False, out_dtype=F32, name=f"fox_out{i}")
            sv.update(z=z, ops=ops, o=o, lse=lse, o_delta=o_delta)
        x1, r1, u2 = residual_layer_norm(xin, y, mods[i], 2, ln_g_all[i, 0], ln_b_all[i, 0], bl, f"ln_mix{i}",
                                         next_mod=(3, 4))
        if wt_gate[i] is None:
            arrive(group_of[f"gate{i}"], u2)
        a, bb, h = swiglu_in(u2, wt_gate[i], wt_up[i], f"ffn_in{i}")
        y2 = mm([(h, w_down[i])], trans_b=False, out_dtype=F32, name=f"ffn_down{i}")
        sv.update(y=y, r1=r1, u2=u2, a=a, bb=bb, h=h, y2=y2)
        if i + 1 < DEPTH:
            xin, r2, u = residual_layer_norm(x1, y2, mods[i], 5, ln_g_all[i, 1], ln_b_all[i, 1], bl, f"ln_ffn{i}",
                                             next_mod=(0, 1, mods[i + 1]))
        else:
            xin, r2 = residual_layer_norm(x1, y2, mods[i], 5, ln_g_all[i, 1], ln_b_all[i, 1], bl, f"ln_ffn{i}")
        sv.update(r2=r2)
        saved.append(sv)

    loss_cols, d_x = loss_head(xin, loss_target.reshape(t, d), "loss_head")

    grads_full = {}
    wgrad = functools.partial(mm_tn, out_dtype=BF16)
    dmod = [[None] * 6 for _ in range(DEPTH)]
    dg_ln = [[None, None] for _ in range(DEPTH)]
    db_ln = [[None, None] for _ in range(DEPTH)]
    dg_q = dg_kv = db_f = None
    d_a, du = d_x, None
    scatter_started = [None] * len(groups)

    def scatter_start(gi, after=None):
        gs = [grads_full[nm].reshape(N_DEV, rows_of[nm], PACK_COLS).astype(BF16) for nm in groups[gi]]
        if gi == 0:
            gs = [jnp.concatenate([slot(nm, g) for nm, g in zip(groups[gi], gs)], axis=1)]
        if after is not None:
            gs = [held_until(g, after) for g in gs]
        lands = [landing(lax.dynamic_index_in_dim(g, dev, 0, keepdims=False)) for g in gs]
        scatter_started[gi] = exchange_start(gs, lands, f"scatter_group{gi}_start", True)

    ln_g_bwd = [[ln_g_all[i, k] for k in range(2)] for i in range(DEPTH)]
    for i in reversed(range(DEPTH)):
        sv = saved[i]
        if i + 1 < DEPTH:
            gi = group_of["fox_w_in"]
            scatter_start(gi)
            ln_g_bwd[i][1] = after_token(ln_g_bwd[i][1], scatter_started[gi])
        ln2 = (sv["r2"], sv["y2"], ln_g_bwd[i][1], ln_b_all[i, 1], (mods[i], 5))
        if du is None:
            bw = sublayer_backward(d_a, bl, f"bwd_ln_ffn{i}", ln=ln2)
        else:
            bw = sublayer_backward(d_a, bl, f"bwd_ln_ffn{i}", du=du, scale=(mods[i + 1], 1), ln=ln2)
            dmod[i + 1][0], dmod[i + 1][1] = bw["dshift"], bw["dscale"]
        dmod[i][5], dg_ln[i][1], db_ln[i][1] = bw["dgate"], bw["dg"], bw["db"]
        dy2 = bw["dy"]
        da, dbb = swiglu_out_backward(dy2, w_down[i], sv["a"], sv["bb"], f"bwd_ffn_act{i}")
        du2 = mm([(da, wt_gate[i]), (dbb, wt_up[i])], trans_b=False, out_dtype=F32, name=f"bwd_ffn_du{i}")
        grads_full[f"down{i}"] = wgrad(sv["h"], dy2, name=f"bwd_w_down{i}")
        grads_full[f"gate{i}"] = wgrad(da, sv["u2"], name=f"bwd_w_gate{i}")
        grads_full[f"up{i}"] = wgrad(dbb, sv["u2"], name=f"bwd_w_up{i}")
        gi = group_of[f"gate{i}"]
        scatter_start(gi)
        ln_g_bwd[i][0] = after_token(ln_g_bwd[i][0], scatter_started[gi])
        bw = sublayer_backward(bw["dx"], bl, f"bwd_ln_mix{i}", du=du2, scale=(mods[i], 4),
                               ln=(sv["r1"], sv["y"], ln_g_bwd[i][0], ln_b_all[i, 0], (mods[i], 2)))
        dmod[i][3], dmod[i][4], dmod[i][2] = bw["dshift"], bw["dscale"], bw["dgate"]
        dg_ln[i][0], db_ln[i][0] = bw["dg"], bw["db"]
        d_a, dy = bw["dx"], bw["dy"]
        o, lse, ops = sv["o"], sv["lse"], sv["ops"]
        if i % 2 == 0:
            do = mm([(dy, w_mo)], trans_b=True, out_dtype=BF16, out_slab=True, name=f"bwd_mla_do{i}")
            grads_full["mla_w_o"] = wgrad(o, dy, name=f"bwd_w_mla_o{i}")
            dqn, dqr, dkn, dvm, dkr = attention_backward("mla", ops, sv["o_delta"], do, lse, bl, mla_scale,
                                                         f"bwd_mla_attn{i}")
            dqr = rope_slabs(dqr, cos_t, sin_t, F32, f"bwd_mla_qrope{i}", transposed=True)
            dcq = mm([(dqn, wt_uq_n), (dqr, wt_uq_r)], trans_b=False, out_dtype=F32, name=f"bwd_mla_dcq{i}")
            dckv = mm([(dkn, wt_uk), (dvm, wt_uv)], trans_b=False, out_dtype=F32, name=f"bwd_mla_dckv{i}")
            d_uq_n = wgrad(dqn, sv["c_q"], name=f"bwd_w_uq_n{i}").reshape(MLA_HEADS, MLA_NOPE, MLA_QR)
            d_uq_r = wgrad(dqr, sv["c_q"], name=f"bwd_w_uq_r{i}").reshape(MLA_HEADS, MLA_ROPE, MLA_QR)
            grads_full["mla_w_uq"] = jnp.concatenate([d_uq_n, d_uq_r], axis=1)
            grads_full["mla_w_uk"] = wgrad(dkn, sv["c_kv"], name=f"bwd_w_uk{i}")
            grads_full["mla_w_uv"] = wgrad(dvm, sv["c_kv"], name=f"bwd_w_uv{i}")
            dh_in, dg_q, dg_kv = mla_latents_backward(sv["h_in"], dcq, dckv, dkr, g_q, g_kv, cos_t, sin_t,
                                                      f"bwd_mla_latents{i}")
            du = mm([(dh_in, w_in)], trans_b=True, out_dtype=F32, name=f"bwd_mla_du{i}")
            grads_full["mla_w_in"] = wgrad(sv["u"], dh_in, name=f"bwd_w_mla_in{i}")[:, :mla_in]
        else:
            do = mm([(dy, w_fo)], trans_b=True, out_dtype=BF16, out_slab=True, name=f"bwd_fox_do{i}")
            grads_full["fox_w_o"] = wgrad(o, dy, name=f"bwd_w_fox_o{i}")
            dq, dk, dvf, dfk = attention_backward("fox", ops, sv["o_delta"], do, lse, bl, fox_scale, f"bwd_fox_attn{i}")
            df = dfk[:, :, :2, :].reshape(bl, FOX_HEADS // 2, nk, 2, tq).transpose(0, 2, 4, 1, 3).reshape(t, FOX_HEADS)
            df = jnp.pad(df, ((0, 0), (0, LANES - FOX_HEADS)))
            dz, db_f = fox_gate_backward(sv["z"], b_f, df, bl, f"bwd_fox_gate{i}")
            du = mm([(dq, wt_fox[0:d]), (dk, wt_fox[d:2 * d]), (dvf, wt_fox[2 * d:3 * d]), (dz, wt_f)],
                    trans_b=False, out_dtype=F32, name=f"bwd_fox_du{i}")
            u_f = sv["u"]
            grads_full["fox_w_in"] = jnp.concatenate(
                [wgrad(dq, u_f, name=f"bwd_w_fox_q{i}"), wgrad(dk, u_f, name=f"bwd_w_fox_k{i}"),
                 wgrad(dvf, u_f, name=f"bwd_w_fox_v{i}"), wgrad(dz, u_f, name=f"bwd_w_fox_f{i}")[:FOX_HEADS]], axis=0)
    scatter_start(0)
    bw = sublayer_backward(d_a, bl, "bwd_input", du=du, scale=(after_token(mods[0], scatter_started[0]), 1), x_in=x2d)
    dmod[0][0], dmod[0][1] = bw["dshift"], bw["dscale"]
    grad_x = bw["dx"].reshape(bl, s, d)

    dmod_rows = jnp.concatenate([r.reshape(bl, d) for layer in dmod for r in layer], axis=0)
    dmod_rows = dmod_rows.reshape(DEPTH, 6, bl, d).transpose(0, 2, 1, 3)
    n_mod = dmod_rows.size // LANES
    ln_parts = [dg_ln[i][k] for i in range(DEPTH) for k in range(2)] + [db_ln[i][k] for i in range(DEPTH) for k in range(2)]
    small_g = jnp.concatenate([dmod_rows.reshape(-1, LANES), dg_q.reshape(-1, LANES), dg_kv.reshape(-1, LANES), db_f]
                              + [p.reshape(-1, LANES) for p in ln_parts] + [loss_cols.reshape(-1, LANES)], axis=0)
    n_small = small_g.shape[0]
    small_g = jnp.pad(small_g, ((0, (-n_small) % 8), (0, 0)))
    small_g_all = all_gather(small_g, "gather_small_grads")
    small_sum = sum_leading(small_g_all, "sum_small_grads")
    per_seq = DEPTH * 6 * d // LANES
    dmod_all = small_g_all[:, :n_mod].reshape(N_DEV, DEPTH, bl, 6 * d).transpose(1, 0, 2, 3)
    dmod_all = dmod_all.reshape(DEPTH, N_DEV * bl, 6 * d)
    o1 = n_mod
    grad_g_q = small_sum[o1:o1 + MLA_QR // LANES].reshape(1, MLA_QR)
    o1 += MLA_QR // LANES
    grad_g_kv = small_sum[o1:o1 + MLA_KVR // LANES].reshape(1, MLA_KVR)
    o1 += MLA_KVR // LANES
    grad_b_f = small_sum[o1:o1 + 1, :FOX_HEADS]
    o1 += 1
    n_ln_rows = DEPTH * 2 * d // LANES
    grad_ln_g_full = small_sum[o1:o1 + n_ln_rows].reshape(DEPTH, 2, d)
    grad_ln_b_full = small_sum[o1 + n_ln_rows:o1 + 2 * n_ln_rows].reshape(DEPTH, 2, d)
    loss = jnp.sum(small_sum[o1 + 2 * n_ln_rows:o1 + 2 * n_ln_rows + d // LANES])
    shard = d // N_DEV
    grad_ln_g = lax.dynamic_slice_in_dim(grad_ln_g_full, dev * shard, shard, axis=2)
    grad_ln_b = lax.dynamic_slice_in_dim(grad_ln_b_full, dev * shard, shard, axis=2)
    by_seq = small_g_all[:, :n_mod].reshape(N_DEV, DEPTH, bl, 6 * d // LANES, LANES).transpose(0, 2, 1, 3, 4)
    grad_ada_b = sum_leading(by_seq.reshape(N_DEV * bl, per_seq, LANES), "sum_ada_b").reshape(DEPTH, 6 * d)
    dmod_cols = lax.dynamic_slice_in_dim(dmod_all, dev * ada_cols, ada_cols, axis=2)
    grad_ada_w = jnp.stack([mm_tn(c_act, dmod_cols[i], name=f"bwd_w_ada{i}") for i in range(DEPTH)])

    g_mine = {}

    def scatter_arrive(gi, after):
        landed = exchange_wait(scatter_started[gi], after, f"scatter_group{gi}_wait", True)
        if gi == 0:
            total = sum_leading(landed[0], f"scatter_group{gi}_sum")
            g_mine.update({nm: total[offsets[nm]:offsets[nm] + rows_of[nm]] for nm in groups[gi]})
            return total
        for nm, land in zip(groups[gi], landed):
            g_mine[nm] = sum_leading(land, f"scatter_sum_{nm}")
        return g_mine[groups[gi][-1]]

    after = scatter_started[0][4]
    for gi in reversed(range(1, len(groups))):
        after = scatter_arrive(gi, after)

    def mine(nm, shape):
        return g_mine[nm].reshape(shape)

    def shard_t(nm, a):
        return mine(nm, t_last(a).shape)

    transposed = {"mla_w_uq", "mla_w_uk", "mla_w_uv", "fox_w_in", "ffn_w_gate", "ffn_w_up"}
    grads = {
        "mla_w_in": lambda: mine("mla_w_in", mla_w_in[0].shape)[None],
        "mla_g_q": lambda: grad_g_q,
        "mla_w_uq": lambda: shard_t("mla_w_uq", mla_w_uq[0])[None],
        "mla_g_kv": lambda: grad_g_kv,
        "mla_w_uk": lambda: shard_t("mla_w_uk", mla_w_uk[0])[None],
        "mla_w_uv": lambda: shard_t("mla_w_uv", mla_w_uv[0])[None],
        "mla_w_o": lambda: mine("mla_w_o", mla_w_o[0].shape)[None],
        "fox_w_in": lambda: shard_t("fox_w_in", fox_w_in[0])[None],
        "fox_b_f": lambda: grad_b_f,
        "fox_w_o": lambda: mine("fox_w_o", fox_w_o[0].shape)[None],
        "ada_w": lambda: grad_ada_w,
        "ada_b": lambda: grad_ada_b,
        "ffn_w_gate": lambda: jnp.stack([shard_t(f"gate{i}", ffn_w_gate[i]) for i in range(DEPTH)]),
        "ffn_w_up": lambda: jnp.stack([shard_t(f"up{i}", ffn_w_up[i]) for i in range(DEPTH)]),
        "ffn_w_down": lambda: jnp.stack([mine(f"down{i}", ffn_w_down[i].shape) for i in range(DEPTH)]),
        "ln_g": lambda: grad_ln_g,
        "ln_b": lambda: grad_ln_b,
    }
    weights = dict(mla_w_in=mla_w_in, mla_g_q=mla_g_q, mla_w_uq=mla_w_uq, mla_g_kv=mla_g_kv, mla_w_uk=mla_w_uk,
                   mla_w_uv=mla_w_uv, mla_w_o=mla_w_o, fox_w_in=fox_w_in, fox_b_f=fox_b_f, fox_w_o=fox_w_o,
                   ada_w=ada_w, ada_b=ada_b, ffn_w_gate=ffn_w_gate, ffn_w_up=ffn_w_up, ffn_w_down=ffn_w_down,
                   ln_g=ln_g, ln_b=ln_b)
    first = dict(mla_w_in=m_mla_w_in, mla_g_q=m_mla_g_q, mla_w_uq=m_mla_w_uq, mla_g_kv=m_mla_g_kv, mla_w_uk=m_mla_w_uk,
                 mla_w_uv=m_mla_w_uv, mla_w_o=m_mla_w_o, fox_w_in=m_fox_w_in, fox_b_f=m_fox_b_f, fox_w_o=m_fox_w_o,
                 ada_w=m_ada_w, ada_b=m_ada_b, ffn_w_gate=m_ffn_w_gate, ffn_w_up=m_ffn_w_up, ffn_w_down=m_ffn_w_down,
                 ln_g=m_ln_g, ln_b=m_ln_b)
    second = dict(mla_w_in=v_mla_w_in, mla_g_q=v_mla_g_q, mla_w_uq=v_mla_w_uq, mla_g_kv=v_mla_g_kv, mla_w_uk=v_mla_w_uk,
                  mla_w_uv=v_mla_w_uv, mla_w_o=v_mla_w_o, fox_w_in=v_fox_w_in, fox_b_f=v_fox_b_f, fox_w_o=v_fox_w_o,
                  ada_w=v_ada_w, ada_b=v_ada_b, ffn_w_gate=v_ffn_w_gate, ffn_w_up=v_ffn_w_up, ffn_w_down=v_ffn_w_down,
                  ln_g=v_ln_g, ln_b=v_ln_b)
    order = list(weights)
    last = [nm for nm in order if group_of.get(nm) == 0]
    updated = {}
    for nm in [nm for nm in order if nm not in last] + last:
        if last and nm == last[0]:
            scatter_arrive(0, after)
        lay = t_last if nm in transposed else (lambda a: a)
        w = lay(weights[nm])
        g = grads[nm]().reshape(w.shape)
        delta, new_m, new_v = adamw(w, g, lay(first[nm]), lay(second[nm]), f"adamw_{nm}")
        updated[nm] = (lay(g), lay(delta), lay(new_m), lay(new_v))
        after = new_v
    return (loss, grad_x, *(updated[nm][k] for k in range(4) for nm in order))
```

```python
import functools
import math

import jax
import jax.numpy as jnp
from jax import lax
from jax.experimental import pallas as pl
from jax.experimental.pallas import tpu as pltpu

F32 = jnp.float32
BF16 = jnp.bfloat16
LANES = 128
N_DEV = 8
VMEM_LIMIT_BYTES = 56 * 1024 * 1024

DEPTH = 2
MLA_HEADS = 8
MLA_NOPE = 128
MLA_ROPE = 64
MLA_V = 128
MLA_QR = 256
MLA_KVR = 256
ROPE_THETA = 10000.0
FOX_HEADS = 16
FOX_HD = 64
ALPHA = (2.0 * DEPTH) ** 0.25
NORM_EPS = 1e-5
ADAM_LR = 0.001
ADAM_B1 = 0.9
ADAM_B2 = 0.999
ADAM_EPS = 1e-08
ADAM_WD = 0.01
ADAM_STEP = 10

MESH = pl.DeviceIdType.MESH


def _params(*sem):
    return pltpu.CompilerParams(dimension_semantics=sem, vmem_limit_bytes=VMEM_LIMIT_BYTES)


def _tile(n, cap, mult=LANES):
    if n <= cap:
        return n
    best = None
    for t in range(mult, cap + 1, mult):
        if n % t == 0:
            best = t
    assert best is not None, (n, cap, mult)
    return best


def _dot(a, b, dims):
    return lax.dot_general(a, b, (dims, ((), ())), preferred_element_type=F32)


def _nn(a, b):
    return _dot(a, b, ((1,), (0,)))


def _nt(a, b):
    return _dot(a, b, ((1,), (1,)))


def _tn(a, b):
    return _dot(a, b, ((0,), (0,)))


def _me():
    return lax.axis_index("x"), lax.axis_index("y"), lax.axis_index("c")


def all_gather(x_loc, name):
    r, c = x_loc.shape

    def body(x_ref, out_ref, send_sems, recv_sems, local_sem):
        x, y, cc = _me()
        me, sibling = (x, y, cc), (x, y, 1 - cc)
        chips = [(1 - x, y), (x, 1 - y), (1 - x, 1 - y)]

        def rows(px, py, pc):
            return out_ref.at[4 * px + 2 * py + pc]

        def copy(k, block, to, src=None):
            return pltpu.make_async_remote_copy(
                src_ref=rows(*block) if src is None else src, dst_ref=rows(*block),
                send_sem=send_sems.at[k], recv_sem=recv_sems.at[k], device_id=to, device_id_type=MESH)

        mine = pltpu.make_async_copy(x_ref, rows(*me), local_sem)
        mine.start()
        first = [copy(0, me, sibling, src=x_ref)]
        first += [copy(1 + j, me, (*chip, cc), src=x_ref) for j, chip in enumerate(chips)]
        for cp in first:
            cp.start()
        passed = [copy(4 + j, (*chip, cc), sibling) for j, chip in enumerate(chips)]
        for j, chip in enumerate(chips):
            copy(1 + j, (*chip, cc), me).wait_recv()
            passed[j].start()
        copy(0, sibling, me).wait_recv()
        for j, chip in enumerate(chips):
            copy(4 + j, (*chip, 1 - cc), me).wait_recv()
        for cp in first + passed:
            cp.wait_send()
        mine.wait()

    return pl.pallas_call(
        body, name=name,
        out_shape=jax.ShapeDtypeStruct((N_DEV, r, c), x_loc.dtype),
        in_specs=[pl.BlockSpec(memory_space=pl.ANY)],
        out_specs=pl.BlockSpec(memory_space=pl.ANY),
        scratch_shapes=[pltpu.SemaphoreType.DMA((7,)), pltpu.SemaphoreType.DMA((7,)), pltpu.SemaphoreType.DMA(())],
    )(x_loc)


HBM_SPEC = pl.BlockSpec(memory_space=pltpu.HBM)
SEM_SPEC = pl.BlockSpec(memory_space=pltpu.SEMAPHORE)
N_PEERS = N_DEV - 1


def _peer(k):
    x, y, c = _me()
    return (1 - x if k & 4 else x, 1 - y if k & 2 else y, 1 - c if k & 1 else c)


def _exchange_copies(src_refs, land_refs, send_sems, recv_sems, scatter):
    x, y, c = _me()
    mine = 4 * x + 2 * y + c
    copies = []
    for n, (src_ref, land_ref) in enumerate(zip(src_refs, land_refs)):
        for k in range(1, N_DEV):
            px, py, pc = _peer(k)
            src = src_ref.at[4 * px + 2 * py + pc] if scatter else src_ref
            sem = n * N_PEERS + k - 1
            copies.append(pltpu.make_async_remote_copy(
                src_ref=src, dst_ref=land_ref.at[mine], send_sem=send_sems.at[sem], recv_sem=recv_sems.at[sem],
                device_id=(px, py, pc), device_id_type=MESH))
    return copies


def exchange_start(srcs, lands, name, scatter):
    n = len(srcs)

    def body(*refs):
        send_sems, recv_sems = refs[2 * n], refs[2 * n + 1]
        for cp in _exchange_copies(refs[:n], refs[n:2 * n], send_sems, recv_sems, scatter):
            cp.start()
        token = refs[-1]
        token[...] = jnp.zeros_like(token)

    outs = pl.pallas_call(
        body, name=name,
        out_shape=(pltpu.SemaphoreType.DMA((n * N_PEERS,)), pltpu.SemaphoreType.DMA((n * N_PEERS,)),
                   *(pltpu.HBM(a.shape, a.dtype) for a in (*srcs, *lands)), jax.ShapeDtypeStruct((8, LANES), F32)),
        in_specs=(HBM_SPEC,) * (2 * n),
        out_specs=(SEM_SPEC, SEM_SPEC, *((HBM_SPEC,) * (2 * n)), pl.BlockSpec(memory_space=pltpu.VMEM)),
        input_output_aliases={i: 2 + i for i in range(2 * n)},
        compiler_params=pltpu.CompilerParams(has_side_effects=pltpu.SideEffectType.DATAFLOW_SIDE_EFFECTING),
    )(*(pltpu.with_memory_space_constraint(a, pltpu.HBM) for a in (*srcs, *lands)))
    return outs[0], outs[1], outs[2:2 + n], outs[2 + n:2 + 2 * n], outs[-1]


def exchange_wait(started, after, name, scatter):
    send_sems, recv_sems, srcs, lands, _ = started
    n = len(srcs)

    def body(*refs):
        send_sems, recv_sems = refs[2 * n], refs[2 * n + 1]
        for cp in _exchange_copies(refs[:n], refs[n:2 * n], send_sems, recv_sems, scatter):
            cp.wait_send()
            cp.wait_recv()

    outs = pl.pallas_call(
        body, name=name,
        out_shape=tuple(pltpu.HBM(a.shape, a.dtype) for a in (*srcs, *lands)),
        in_specs=(*((HBM_SPEC,) * (2 * n)), SEM_SPEC, SEM_SPEC, pl.BlockSpec(memory_space=pl.ANY)),
        out_specs=(HBM_SPEC,) * (2 * n), input_output_aliases={i: i for i in range(2 * n)},
        compiler_params=pltpu.CompilerParams(has_side_effects=pltpu.SideEffectType.DATAFLOW_SIDE_EFFECTING),
    )(*srcs, *lands, send_sems, recv_sems, after)
    return outs[n:]


def after_token(small, started):
    return small + started[4][0, 0]


def sum_leading(x, name):
    n, r, c = x.shape
    tr = _tile(r, 512, 16)

    def body(x_ref, o_ref):
        acc = x_ref[0].astype(F32)
        for k in range(1, n):
            acc = acc + x_ref[k].astype(F32)
        o_ref[...] = acc

    return pl.pallas_call(
        body, name=name,
        out_shape=jax.ShapeDtypeStruct((r, c), F32),
        grid=(r // tr,),
        in_specs=[pl.BlockSpec((n, tr, c), lambda i: (0, i, 0))],
        out_specs=pl.BlockSpec((tr, c), lambda i: (i, 0)),
        compiler_params=_params("arbitrary"),
    )(x)


MM_VMEM_BUDGET = 36 * 1024 * 1024
GRID_STEP_AS_BYTES = 1 << 20


def _mm_tiles(m, n, a_row_bytes, b_col_bytes, out_bytes):
    tms = [c for c in (2048, 1024, 512, 256, 128, 64, 32, 16, 8) if m % c == 0] or [m]
    tns = [c for c in range(LANES, min(n, 2048) + 1, LANES) if n % c == 0] or [n]
    best = None
    for tm in tms:
        for tn in tns:
            vmem = 2 * (tm * a_row_bytes + tn * b_col_bytes) + 2 * tm * tn * out_bytes + tm * tn * 4
            if vmem > MM_VMEM_BUDGET:
                continue
            steps = (m // tm) * (n // tn)
            cost = steps * GRID_STEP_AS_BYTES + (m // tm) * n * b_col_bytes + m * a_row_bytes
            if best is None or cost < best[0]:
                best = (cost, tm, tn)
    assert best is not None, (m, n, a_row_bytes, b_col_bytes)
    return best[1], best[2]


def mm(pairs, *, trans_b, out_dtype, name, out_slab=False, bias=None):
    a0 = pairs[0][0]
    m = a0.shape[1] if a0.ndim == 3 else a0.shape[0]
    n = pairs[0][1].shape[0] if trans_b else pairs[0][1].shape[1]
    a_row_bytes = sum((b.shape[1] if trans_b else b.shape[0]) * a.dtype.itemsize for a, b in pairs)
    b_col_bytes = sum((b.shape[1] if trans_b else b.shape[0]) * b.dtype.itemsize for _, b in pairs)
    tm, tn = _mm_tiles(m, n, a_row_bytes, b_col_bytes, jnp.dtype(out_dtype).itemsize)
    slabs = [a.ndim == 3 for a, _ in pairs]
    n_pairs = len(pairs)

    def body(*refs):
        o_ref = refs[-1]
        acc = bias_ref = None
        if bias is not None:
            bias_ref = refs[2 * n_pairs]
        for i in range(n_pairs):
            a_ref, b_ref = refs[2 * i], refs[2 * i + 1]
            if slabs[i]:
                a = jnp.concatenate([a_ref[s].astype(BF16) for s in range(a_ref.shape[0])], axis=1)
            else:
                a = a_ref[...].astype(BF16)
            b = b_ref[...].astype(BF16)
            part = _nt(a, b) if trans_b else _nn(a, b)
            acc = part if acc is None else acc + part
        if bias_ref is not None:
            acc = acc + bias_ref[...]
        if out_slab:
            for s in range(tn // LANES):
                o_ref[s] = acc[:, s * LANES:(s + 1) * LANES].astype(out_dtype)
        else:
            o_ref[...] = acc.astype(out_dtype)

    in_specs, args = [], []
    for (a, b), slab in zip(pairs, slabs):
        if slab:
            in_specs.append(pl.BlockSpec((a.shape[0], tm, LANES), lambda i, j: (0, i, 0)))
        else:
            in_specs.append(pl.BlockSpec((tm, a.shape[1]), lambda i, j: (i, 0)))
        if trans_b:
            in_specs.append(pl.BlockSpec((tn, b.shape[1]), lambda i, j: (j, 0)))
        else:
            in_specs.append(pl.BlockSpec((b.shape[0], tn), lambda i, j: (0, j)))
        args += [a, b]
    if bias is not None:
        in_specs.append(pl.BlockSpec((1, tn), lambda i, j: (0, j)))
        args.append(bias)
    if out_slab:
        out_shape = jax.ShapeDtypeStruct((n // LANES, m, LANES), out_dtype)
        out_spec = pl.BlockSpec((tn // LANES, tm, LANES), lambda i, j: (j, i, 0))
    else:
        out_shape = jax.ShapeDtypeStruct((m, n), out_dtype)
        out_spec = pl.BlockSpec((tm, tn), lambda i, j: (i, j))
    return pl.pallas_call(
        body, name=name, out_shape=out_shape, grid=(m // tm, n // tn),
        in_specs=in_specs, out_specs=out_spec,
        compiler_params=_params("arbitrary", "arbitrary"),
    )(*args)


def mm_tn(a, b, *, name, out_dtype=F32, tk_cap=1536, tn_cap=1024, tm_cap=512):
    slab = a.ndim == 3
    m = a.shape[1] if slab else a.shape[0]
    k = a.shape[0] * LANES if slab else a.shape[1]
    n = b.shape[1]
    tk = _tile(k, tk_cap)
    tn = _tile(n, tn_cap)
    tm = _tile(m, tm_cap, 8)
    n_steps = m // tm

    def body(a_ref, b_ref, o_ref, acc_ref):
        step = pl.program_id(2)

        @pl.when(step == 0)
        def _():
            acc_ref[...] = jnp.zeros_like(acc_ref)

        bb = b_ref[...].astype(BF16)
        if slab:
            for s in range(tk // LANES):
                acc_ref[s * LANES:(s + 1) * LANES, :] += _tn(a_ref[s].astype(BF16), bb)
        else:
            acc_ref[...] += _tn(a_ref[...].astype(BF16), bb)

        @pl.when(step == n_steps - 1)
        def _():
            o_ref[...] = acc_ref[...].astype(out_dtype)

    if slab:
        a_spec = pl.BlockSpec((tk // LANES, tm, LANES), lambda i, j, t: (i, t, 0))
    else:
        a_spec = pl.BlockSpec((tm, tk), lambda i, j, t: (t, i))
    return pl.pallas_call(
        body, name=name, out_shape=jax.ShapeDtypeStruct((k, n), out_dtype), grid=(k // tk, n // tn, n_steps),
        in_specs=[a_spec, pl.BlockSpec((tm, tn), lambda i, j, t: (t, j))],
        out_specs=pl.BlockSpec((tk, tn), lambda i, j, t: (i, j)),
        scratch_shapes=[pltpu.VMEM((tk, tn), F32)],
        compiler_params=_params("arbitrary", "arbitrary", "arbitrary"),
    )(a, b)


def _row_spec(d, k):
    return pl.BlockSpec((1, 1, d), lambda b, i: (6 * b + k, 0, 0))


def modulate(x, mod, k_shift, k_scale, bl, name):
    t, d = x.shape
    s = t // bl
    tm = _tile(s, 512, 8)
    nt = s // tm

    def body(x_ref, sh_ref, sc_ref, o_ref):
        o_ref[...] = (x_ref[...] * (1.0 + sc_ref[0]) + sh_ref[0]).astype(BF16)

    return pl.pallas_call(
        body, name=name, out_shape=jax.ShapeDtypeStruct((t, d), BF16), grid=(bl, nt),
        in_specs=[pl.BlockSpec((tm, d), lambda b, i: (b * nt + i, 0)), _row_spec(d, k_shift), _row_spec(d, k_scale)],
        out_specs=pl.BlockSpec((tm, d), lambda b, i: (b * nt + i, 0)),
        compiler_params=_params("arbitrary", "arbitrary"),
    )(x, mod, mod)


def _layer_norm_stats(r):
    mu = jnp.mean(r, axis=-1, keepdims=True)
    rc = r - mu
    var = jnp.mean(rc * rc, axis=-1, keepdims=True)
    rstd = lax.rsqrt(var + NORM_EPS)
    return rc * rstd, rstd


def residual_layer_norm(x, y, mod, k_gate, g, b, bl, name, next_mod=None):
    t, d = x.shape
    s = t // bl
    tm = _tile(s, 512, 8)
    nt = s // tm
    has_next = next_mod is not None

    def body(*refs):
        x_ref, y_ref, gt_ref, g_ref, b_ref = refs[:5]
        rest = refs[5:]
        if has_next:
            sh_ref, sc_ref, o_ref, r_ref, u_ref = rest
        else:
            o_ref, r_ref = rest
        r = ALPHA * x_ref[...] + (1.0 + gt_ref[0]) * y_ref[...]
        xhat, _ = _layer_norm_stats(r)
        out = xhat * g_ref[...] + b_ref[...]
        o_ref[...] = out
        r_ref[...] = r
        if has_next:
            u_ref[...] = (out * (1.0 + sc_ref[0]) + sh_ref[0]).astype(BF16)

    tok = pl.BlockSpec((tm, d), lambda bb, i: (bb * nt + i, 0))
    vec = pl.BlockSpec((1, d), lambda bb, i: (0, 0))
    in_specs = [tok, tok, _row_spec(d, k_gate), vec, vec]
    args = [x, y, mod, g, b]
    out_shape = [jax.ShapeDtypeStruct((t, d), F32), jax.ShapeDtypeStruct((t, d), F32)]
    out_specs = [tok, tok]
    if has_next:
        in_specs += [_row_spec(d, next_mod[0]), _row_spec(d, next_mod[1])]
        args += [mod if len(next_mod) == 2 else next_mod[2]] * 2
        out_shape.append(jax.ShapeDtypeStruct((t, d), BF16))
        out_specs.append(tok)
    return pl.pallas_call(
        body, name=name, out_shape=out_shape, grid=(bl, nt), in_specs=in_specs, out_specs=out_specs,
        compiler_params=_params("arbitrary", "arbitrary"),
    )(*args)


def loss_head(xo, target, name):
    t, d = xo.shape
    tm = _tile(t, 512, 8)

    def body(x_ref, t_ref, l_ref, dx_ref):
        @pl.when(pl.program_id(0) == 0)
        def _():
            l_ref[...] = jnp.zeros_like(l_ref)

        e = x_ref[...] - t_ref[...]
        l_ref[...] += jnp.sum(e * e, axis=0, keepdims=True) * (0.5 / d)
        dx_ref[...] = e * (1.0 / d)

    tok = pl.BlockSpec((tm, d), lambda i: (i, 0))
    return pl.pallas_call(
        body, name=name,
        out_shape=[jax.ShapeDtypeStruct((1, d), F32), jax.ShapeDtypeStruct((t, d), F32)],
        grid=(t // tm,), in_specs=[tok, tok],
        out_specs=[pl.BlockSpec((1, d), lambda i: (0, 0)), tok],
        compiler_params=_params("arbitrary"),
    )(xo, target)


def sublayer_backward(d_a, bl, name, *, du=None, scale=None, x_in=None, ln=None):
    t, d = d_a.shape
    s = t // bl
    tm = _tile(s, 512, 8)
    nt = s // tm
    has_mod = du is not None
    has_ln = ln is not None
    assert has_mod or has_ln
    assert has_ln or x_in is not None

    def body(*refs):
        refs = list(refs)
        da_ref = refs.pop(0)
        if has_mod:
            du_ref, sc_ref = refs.pop(0), refs.pop(0)
        if has_ln:
            r_ref, y_ref, g_ref, b_ref, gt_ref = (refs.pop(0) for _ in range(5))
        elif has_mod:
            xin_ref = refs.pop(0)
        dx_ref = refs.pop(0)
        if has_ln:
            dy_ref, dg_ref, db_ref, dgt_ref = (refs.pop(0) for _ in range(4))
        if has_mod:
            dsc_ref, dsh_ref = refs.pop(0), refs.pop(0)
        first_tile = pl.program_id(1) == 0
        first_step = jnp.logical_and(pl.program_id(0) == 0, first_tile)

        dout = da_ref[...]
        if has_ln:
            xhat, rstd = _layer_norm_stats(r_ref[...])
        if has_mod:
            duv = du_ref[...]
            dout = dout + duv * (1.0 + sc_ref[0])
            xin = xhat * g_ref[...] + b_ref[...] if has_ln else xin_ref[...]

            @pl.when(first_tile)
            def _():
                dsc_ref[...] = jnp.zeros_like(dsc_ref)
                dsh_ref[...] = jnp.zeros_like(dsh_ref)

            dsc_ref[0] += jnp.sum(duv * xin, axis=0, keepdims=True)
            dsh_ref[0] += jnp.sum(duv, axis=0, keepdims=True)
        if not has_ln:
            dx_ref[...] = dout
            return

        @pl.when(first_step)
        def _():
            dg_ref[...] = jnp.zeros_like(dg_ref)
            db_ref[...] = jnp.zeros_like(db_ref)

        @pl.when(first_tile)
        def _():
            dgt_ref[...] = jnp.zeros_like(dgt_ref)

        dg_ref[...] += jnp.sum(dout * xhat, axis=0, keepdims=True)
        db_ref[...] += jnp.sum(dout, axis=0, keepdims=True)
        dxh = dout * g_ref[...]
        dr = rstd * (dxh - jnp.mean(dxh, axis=-1, keepdims=True) - xhat * jnp.mean(dxh * xhat, axis=-1, keepdims=True))
        dx_ref[...] = ALPHA * dr
        dy_ref[...] = ((1.0 + gt_ref[0]) * dr).astype(BF16)
        dgt_ref[0] += jnp.sum(dr * y_ref[...], axis=0, keepdims=True)

    tok = pl.BlockSpec((tm, d), lambda bb, i: (bb * nt + i, 0))
    vec = pl.BlockSpec((1, d), lambda bb, i: (0, 0))
    seq = pl.BlockSpec((1, 1, d), lambda bb, i: (bb, 0, 0))
    in_specs, args = [tok], [d_a]
    if has_mod:
        in_specs += [tok, _row_spec(d, scale[1])]
        args += [du, scale[0]]
    if has_ln:
        r, y, g, b, gate = ln
        in_specs += [tok, tok, vec, vec, _row_spec(d, gate[1])]
        args += [r, y, g, b, gate[0]]
    elif has_mod:
        in_specs.append(tok)
        args.append(x_in)
    names = ["dx"]
    out_shape, out_specs = [jax.ShapeDtypeStruct((t, d), F32)], [tok]
    if has_ln:
        names += ["dy", "dg", "db", "dgate"]
        out_shape += [jax.ShapeDtypeStruct((t, d), BF16), jax.ShapeDtypeStruct((1, d), F32),
                      jax.ShapeDtypeStruct((1, d), F32), jax.ShapeDtypeStruct((bl, 1, d), F32)]
        out_specs += [tok, vec, vec, seq]
    if has_mod:
        names += ["dscale", "dshift"]
        out_shape += [jax.ShapeDtypeStruct((bl, 1, d), F32)] * 2
        out_specs += [seq, seq]
    outs = pl.pallas_call(
        body, name=name, out_shape=out_shape, grid=(bl, nt), in_specs=in_specs, out_specs=out_specs,
        compiler_params=_params("arbitrary", "arbitrary"),
    )(*args)
    return dict(zip(names, outs))


def _silu(a):
    return a * jax.nn.sigmoid(a)


def silu_rows(a, name):
    def body(a_ref, o_ref):
        o_ref[...] = _silu(a_ref[...]).astype(BF16)

    return pl.pallas_call(body, name=name, out_shape=jax.ShapeDtypeStruct(a.shape, BF16))(a)


def _swiglu_tiles(t, f):
    return _tile(t, 1024, 8), _tile(f, 1536)


def swiglu_in(u, wt_gate, wt_up, name):
    t, d = u.shape
    f = wt_gate.shape[0]
    tm, tf = _swiglu_tiles(t, f)

    def body(u_ref, g_ref, w_ref, a_ref, b_ref, h_ref):
        uv = u_ref[...]
        a = _nt(uv, g_ref[...])
        b = _nt(uv, w_ref[...])
        a_ref[...] = a.astype(BF16)
        b_ref[...] = b.astype(BF16)
        h_ref[...] = (_silu(a) * b).astype(BF16)

    w_spec = pl.BlockSpec((tf, d), lambda i, j: (j, 0))
    o_spec = pl.BlockSpec((tm, tf), lambda i, j: (i, j))
    return pl.pallas_call(
        body, name=name,
        out_shape=[jax.ShapeDtypeStruct((t, f), BF16)] * 3,
        grid=(t // tm, f // tf), in_specs=[pl.BlockSpec((tm, d), lambda i, j: (i, 0)), w_spec, w_spec],
        out_specs=[o_spec, o_spec, o_spec], compiler_params=_params("arbitrary", "arbitrary"),
    )(u, wt_gate, wt_up)


def swiglu_out_backward(dy, w_down, a, b, name):
    t, d = dy.shape
    f = w_down.shape[0]
    tm, tf = _swiglu_tiles(t, f)

    def body(dy_ref, w_ref, a_ref, b_ref, da_ref, db_ref):
        dh = _nt(dy_ref[...], w_ref[...])
        av = a_ref[...].astype(F32)
        sig = jax.nn.sigmoid(av)
        da_ref[...] = (dh * b_ref[...].astype(F32) * (sig * (1.0 + av * (1.0 - sig)))).astype(BF16)
        db_ref[...] = (dh * (av * sig)).astype(BF16)

    spec = pl.BlockSpec((tm, tf), lambda i, j: (i, j))
    return pl.pallas_call(
        body, name=name, out_shape=[jax.ShapeDtypeStruct((t, f), BF16)] * 2, grid=(t // tm, f // tf),
        in_specs=[pl.BlockSpec((tm, d), lambda i, j: (i, 0)), pl.BlockSpec((tf, d), lambda i, j: (j, 0)), spec, spec],
        out_specs=[spec, spec], compiler_params=_params("arbitrary", "arbitrary"),
    )(dy, w_down, a, b)


def rope_tables(pos, inv_freq, sign, name):
    t = pos.shape[0]
    tm = _tile(t, 512, 8)

    def body(p_ref, f_ref, s_ref, c_out, s_out):
        ang = p_ref[...] * f_ref[...]
        c_out[...] = jnp.cos(ang)
        s_out[...] = jnp.sin(ang) * s_ref[...]

    vec = pl.BlockSpec((1, LANES), lambda i: (0, 0))
    tab = pl.BlockSpec((tm, LANES), lambda i: (i, 0))
    return pl.pallas_call(
        body, name=name, out_shape=[jax.ShapeDtypeStruct((t, LANES), F32)] * 2, grid=(t // tm,),
        in_specs=[pl.BlockSpec((tm, 1), lambda i: (i, 0)), vec, vec], out_specs=[tab, tab],
        compiler_params=_params("arbitrary"),
    )(pos, inv_freq, sign)


def _rot_half(v):
    lane = lax.broadcasted_iota(jnp.int32, v.shape, v.ndim - 1)
    up = pltpu.roll(v, LANES - MLA_ROPE // 2, v.ndim - 1)
    down = pltpu.roll(v, MLA_ROPE // 2, v.ndim - 1)
    return jnp.where(lane % MLA_ROPE < MLA_ROPE // 2, up, down)


def _rope(v, cos, sin_signed):
    return v * cos + _rot_half(v) * sin_signed


def _rope_transposed(dv, cos, sin_signed):
    return dv * cos + _rot_half(dv * sin_signed)


def rope_slabs(v, cos, sin_signed, out_dtype, name, transposed=False):
    ns, t, _ = v.shape
    tm = _tile(t, 1024, 8)
    fn = _rope_transposed if transposed else _rope

    def body(v_ref, c_ref, s_ref, o_ref):
        for j in range(ns):
            o_ref[j] = fn(v_ref[j].astype(F32), c_ref[...], s_ref[...]).astype(out_dtype)

    tab = pl.BlockSpec((tm, LANES), lambda i: (i, 0))
    spec = pl.BlockSpec((ns, tm, LANES), lambda i: (0, i, 0))
    return pl.pallas_call(
        body, name=name, out_shape=jax.ShapeDtypeStruct(v.shape, out_dtype), grid=(t // tm,),
        in_specs=[spec, tab, tab], out_specs=spec, compiler_params=_params("arbitrary"),
    )(v, cos, sin_signed)


def _rms(x):
    rinv = lax.rsqrt(jnp.mean(x * x, axis=-1, keepdims=True) + NORM_EPS)
    return x * rinv, rinv


def mla_latents_forward(h_in, g_q, g_kv, cos, sin_signed, name):
    t = h_in.shape[0]
    tm = _tile(t, 512, 8)

    def body(h_ref, gq_ref, gkv_ref, c_ref, s_ref, cq_ref, ckv_ref, kr_ref):
        cq_ref[...] = (_rms(h_ref[:, 0:MLA_QR])[0] * gq_ref[...]).astype(BF16)
        ckv_ref[...] = (_rms(h_ref[:, MLA_QR:MLA_QR + MLA_KVR])[0] * gkv_ref[...]).astype(BF16)
        kr_ref[...] = _rope(h_ref[:, MLA_QR + MLA_KVR:], c_ref[...], s_ref[...]).astype(BF16)

    def tok(w):
        return pl.BlockSpec((tm, w), lambda i: (i, 0))

    def vec(w):
        return pl.BlockSpec((1, w), lambda i: (0, 0))

    return pl.pallas_call(
        body, name=name,
        out_shape=[jax.ShapeDtypeStruct((t, MLA_QR), BF16), jax.ShapeDtypeStruct((t, MLA_KVR), BF16),
                   jax.ShapeDtypeStruct((t, LANES), BF16)],
        grid=(t // tm,),
        in_specs=[tok(h_in.shape[1]), vec(MLA_QR), vec(MLA_KVR), tok(LANES), tok(LANES)],
        out_specs=[tok(MLA_QR), tok(MLA_KVR), tok(LANES)],
        compiler_params=_params("arbitrary"),
    )(h_in, g_q, g_kv, cos, sin_signed)


def mla_latents_backward(h_in, dcq, dckv, dkr, g_q, g_kv, cos, sin_signed, name):
    t, w = h_in.shape
    tm = _tile(t, 512, 8)

    def body(h_ref, dcq_ref, dckv_ref, dkr_ref, gq_ref, gkv_ref, c_ref, s_ref, dh_ref, dgq_ref, dgkv_ref):
        @pl.when(pl.program_id(0) == 0)
        def _():
            dgq_ref[...] = jnp.zeros_like(dgq_ref)
            dgkv_ref[...] = jnp.zeros_like(dgkv_ref)

        def rms_bwd(x, dc, g_ref, dg_ref):
            xn, rinv = _rms(x)
            dg_ref[...] += jnp.sum(dc * xn, axis=0, keepdims=True)
            dxn = dc * g_ref[...]
            return rinv * (dxn - xn * jnp.mean(dxn * xn, axis=-1, keepdims=True))

        dq = rms_bwd(h_ref[:, 0:MLA_QR], dcq_ref[...], gq_ref, dgq_ref)
        dkv = rms_bwd(h_ref[:, MLA_QR:MLA_QR + MLA_KVR], dckv_ref[...], gkv_ref, dgkv_ref)
        dr = _rope_transposed(dkr_ref[...], c_ref[...], s_ref[...])
        dh_ref[...] = jnp.concatenate([dq, dkv, dr], axis=1).astype(BF16)

    def tok(ww):
        return pl.BlockSpec((tm, ww), lambda i: (i, 0))

    def vec(ww):
        return pl.BlockSpec((1, ww), lambda i: (0, 0))

    return pl.pallas_call(
        body, name=name,
        out_shape=[jax.ShapeDtypeStruct((t, w), BF16), jax.ShapeDtypeStruct((1, MLA_QR), F32),
                   jax.ShapeDtypeStruct((1, MLA_KVR), F32)],
        grid=(t // tm,),
        in_specs=[tok(w), tok(MLA_QR), tok(MLA_KVR), tok(LANES), vec(MLA_QR), vec(MLA_KVR), tok(LANES), tok(LANES)],
        out_specs=[tok(w), vec(MLA_QR), vec(MLA_KVR)],
        compiler_params=_params("arbitrary"),
    )(h_in, dcq, dckv, dkr, g_q, g_kv, cos, sin_signed)


def _tri(n, lower):
    r = lax.broadcasted_iota(jnp.int32, (n, n), 0)
    c = lax.broadcasted_iota(jnp.int32, (n, n), 1)
    return jnp.where(r >= c if lower else r <= c, 1.0, 0.0).astype(F32)


def _dot_exact(tri, v):
    hi = v.astype(BF16)
    mid = (v - hi.astype(F32)).astype(BF16)
    lo = (v - hi.astype(F32) - mid.astype(F32)).astype(BF16)
    t = tri.astype(BF16)
    return _nn(t, hi) + _nn(t, mid) + _nn(t, lo)


def fox_gate_forward(z, b_f, bl, name):
    t = z.shape[0]
    s = t // bl
    ch = LANES
    n_ch = s // ch

    def body(z_ref, b_ref, f_ref, fs_ref):
        tri = _tri(ch, True)
        carry = jnp.zeros((1, LANES), F32)
        for k in range(n_ch):
            x = z_ref[k * ch:(k + 1) * ch, :] + b_ref[...]
            logf = jnp.minimum(x, 0.0) - jnp.log(1.0 + jnp.exp(-jnp.abs(x)))
            cs = _dot_exact(tri, logf) + carry
            carry = cs[ch - 1:ch, :]
            f_ref[k * ch:(k + 1) * ch, :] = cs
            for h in range(FOX_HEADS):
                fs_ref[h, k * ch:(k + 1) * ch, :] = jnp.broadcast_to(cs[:, h:h + 1], (ch, LANES))

    return pl.pallas_call(
        body, name=name,
        out_shape=[jax.ShapeDtypeStruct((t, LANES), F32), jax.ShapeDtypeStruct((FOX_HEADS, t, LANES), F32)],
        grid=(bl,),
        in_specs=[pl.BlockSpec((s, LANES), lambda b: (b, 0)), pl.BlockSpec((1, LANES), lambda b: (0, 0))],
        out_specs=[pl.BlockSpec((s, LANES), lambda b: (b, 0)),
                   pl.BlockSpec((FOX_HEADS, s, LANES), lambda b: (0, b, 0))],
        compiler_params=_params("arbitrary"),
    )(z, b_f)


def fox_gate_backward(z, b_f, df, bl, name):
    t = z.shape[0]
    s = t // bl
    ch = LANES
    n_ch = s // ch

    def body(z_ref, b_ref, df_ref, dz_ref, db_ref):
        @pl.when(pl.program_id(0) == 0)
        def _():
            db_ref[...] = jnp.zeros_like(db_ref)

        tri = _tri(ch, False)
        carry = jnp.zeros((1, LANES), F32)
        for k in reversed(range(n_ch)):
            cs = _dot_exact(tri, df_ref[k * ch:(k + 1) * ch, :]) + carry
            carry = cs[0:1, :]
            x = z_ref[k * ch:(k + 1) * ch, :] + b_ref[...]
            dz = cs * (1.0 - jax.nn.sigmoid(x))
            dz_ref[k * ch:(k + 1) * ch, :] = dz
            db_ref[...] += jnp.sum(dz, axis=0, keepdims=True)

    tok = pl.BlockSpec((s, LANES), lambda b: (b, 0))
    vec = pl.BlockSpec((1, LANES), lambda b: (0, 0))
    return pl.pallas_call(
        body, name=name,
        out_shape=[jax.ShapeDtypeStruct((t, LANES), F32), jax.ShapeDtypeStruct((1, LANES), F32)],
        grid=(bl,), in_specs=[tok, vec, tok], out_specs=[tok, vec],
        compiler_params=_params("arbitrary"),
    )(z, b_f, df)


NEG_INF = float("-inf")


def _attn_tiles(s):
    return _tile(s, 1024, 8)


def attention_forward(kind, ops, bl, scale, name):
    fox = kind == "fox"
    if fox:
        assert math.frexp(scale)[0] == 0.5, "the FoX scale is folded into bf16 queries: it must be a power of two"
        qkv, fq, fk = ops
        t = qkv.shape[1]
        n_pair = FOX_HEADS // 2
    else:
        qn, qr, kn, kr, v = ops
        t = qn.shape[1]
        n_pair = MLA_HEADS // 2
    s = t // bl
    tq = _attn_tiles(s)
    nq = s // tq
    half = LANES // 2

    def body(*refs):
        if fox:
            q_ref, k_ref, v_ref, fq_ref, fk_ref, o_ref, lse_ref, o32_ref = refs
        else:
            qn_ref, qr_ref, kn_ref, kr_ref, v_ref, o_ref, lse_ref = refs
        i = pl.program_id(2)
        row = lax.broadcasted_iota(jnp.int32, (tq, tq), 0)
        col = lax.broadcasted_iota(jnp.int32, (tq, tq), 1)
        heads = []
        for e in range(2):
            sl = slice(e * half, (e + 1) * half)
            if fox:
                heads.append((sl, q_ref[0, :, sl] * jnp.asarray(scale, BF16), None))
            else:
                heads.append((sl, jnp.concatenate([qn_ref[e], qr_ref[0, :, sl], jnp.zeros((tq, half), BF16)], axis=1),
                              None))
        dv = half if fox else LANES

        def wide(stat):
            return jnp.concatenate([stat] * (tq // LANES), axis=1)

        def step(j, carry, masked):
            rows = pl.ds(pl.multiple_of(j * tq, tq), tq)
            new = []
            for e, (sl, qa, qb) in enumerate(heads):
                m, l, acc = carry[e]
                if fox:
                    sc = _nt(qa, k_ref[0, rows, sl]) + wide(fq_ref[e]) - fk_ref[0, j, e:e + 1, :]
                    vv = v_ref[0, rows, sl]
                else:
                    k_cat = jnp.concatenate([kn_ref[e, rows, :], kr_ref[rows, :]], axis=1)
                    sc = _nt(qa, k_cat) * scale
                    vv = v_ref[e, rows, :]
                if masked:
                    sc = jnp.where(row >= col, sc, NEG_INF)
                m_new = jnp.maximum(m, jnp.max(sc, axis=1, keepdims=True))
                p = jnp.exp(sc - m_new)
                a = jnp.exp(m - m_new)
                p_hi = p.astype(BF16)
                if fox:
                    vv = jnp.concatenate([vv, ones], axis=1)
                    acc = a * acc + _nn(p_hi, vv) + _nn((p - p_hi.astype(F32)).astype(BF16), vv)
                else:
                    l = a * l + jnp.sum(p, axis=1, keepdims=True)
                    acc = a * acc + _nn(p_hi, vv)
                new.append((m_new, l, acc))
            return tuple(new)

        ones = jnp.ones((tq, half), BF16)
        acc_w = LANES if fox else dv
        init = (jnp.full((tq, 1), NEG_INF, F32), jnp.zeros((tq, 1), F32), jnp.zeros((tq, acc_w), F32))
        carry = step(i, (init, init), True)
        carry = lax.fori_loop(0, i, lambda j, c: step(j, c, False), carry)
        if fox:
            carry = [(m, acc[:, dv:dv + 1], acc[:, :dv]) for m, _, acc in carry]
        outs = [acc / l for _, l, acc in carry]
        for e, (m, l, _) in enumerate(carry):
            lse_ref[e] = jnp.broadcast_to(m + jnp.log(l), (tq, LANES))
        if fox:
            o32 = jnp.concatenate(outs, axis=1)
            o32_ref[0] = o32
            o_ref[0] = o32.astype(BF16)
        else:
            o_ref[0] = outs[0].astype(BF16)
            o_ref[1] = outs[1].astype(BF16)

    def q_idx(b, g, i):
        return (g, b * nq + i, 0)

    if fox:
        nk = fk.shape[1]
        in_specs = [pl.BlockSpec((1, tq, LANES), q_idx),
                    pl.BlockSpec((1, s, LANES), lambda b, g, i: (n_pair + g, b, 0)),
                    pl.BlockSpec((1, s, LANES), lambda b, g, i: (2 * n_pair + g, b, 0)),
                    pl.BlockSpec((2, tq, LANES), q_idx),
                    pl.BlockSpec((1, nk, 8, tq), lambda b, g, i: (b * n_pair + g, 0, 0, 0))]
        args = [qkv, qkv, qkv, fq, fk]
        o_spec = pl.BlockSpec((1, tq, LANES), q_idx)
    else:
        in_specs = [pl.BlockSpec((2, tq, LANES), q_idx),
                    pl.BlockSpec((1, tq, LANES), q_idx),
                    pl.BlockSpec((2, s, LANES), lambda b, g, i: (g, b, 0)),
                    pl.BlockSpec((s, LANES), lambda b, g, i: (b, 0)),
                    pl.BlockSpec((2, s, LANES), lambda b, g, i: (g, b, 0))]
        args = [qn, qr, kn, kr, v]
        o_spec = pl.BlockSpec((2, tq, LANES), q_idx)
    out_shape = [jax.ShapeDtypeStruct((8, t, LANES), BF16), jax.ShapeDtypeStruct((2 * n_pair, t, LANES), F32)]
    out_specs = [o_spec, pl.BlockSpec((2, tq, LANES), q_idx)]
    if fox:
        out_shape.append(jax.ShapeDtypeStruct((8, t, LANES), F32))
        out_specs.append(o_spec)
    outs = pl.pallas_call(
        body, name=name, out_shape=out_shape, grid=(bl, n_pair, nq), in_specs=in_specs, out_specs=out_specs,
        compiler_params=_params("arbitrary", "arbitrary", "arbitrary"),
    )(*args)
    return (outs[0], outs[1], outs[2] if fox else outs[0])


def attention_backward(kind, ops, o, do, lse, bl, scale, name):
    fox = kind == "fox"
    if fox:
        assert math.frexp(scale)[0] == 0.5, "the FoX scale is folded into bf16 queries: it must be a power of two"
        qkv, fq, fk = ops
        t = qkv.shape[1]
        n_pair = FOX_HEADS // 2
    else:
        qn, qr, kn, kr, v = ops
        t = qn.shape[1]
        n_pair = MLA_HEADS // 2
    s = t // bl
    tq = _attn_tiles(s)
    nq = s // tq
    half = LANES // 2

    def body(*refs):
        if fox:
            (q_ref, k_ref, v_ref, fq_ref, fk_ref, o_ref, do_ref, lse_ref,
             dq_ref, dk_ref, dv_ref, dfk_ref, delta_scr, qt_scr, dot_scr) = refs
        else:
            (qn_ref, qr_ref, kn_ref, kr_ref, v_ref, o_ref, do_ref, lse_ref,
             dqn_ref, dqr_ref, dkn_ref, dv_ref, dkr_ref, delta_scr, qt_scr, qrt_scr, dot_scr) = refs
        g, j = pl.program_id(1), pl.program_id(2)
        row = lax.broadcasted_iota(jnp.int32, (tq, tq), 0)
        col = lax.broadcasted_iota(jnp.int32, (tq, tq), 1)
        krows = pl.ds(pl.multiple_of(j * tq, tq), tq)
        q_scale = jnp.asarray(scale, BF16)

        def transposed(v):
            return v.astype(F32).T.astype(BF16)

        def wide(stat):
            return jnp.concatenate([stat] * (tq // LANES), axis=1)

        @pl.when(j == 0)
        def _():
            if fox:
                dq_ref[...] = jnp.zeros_like(dq_ref)
            else:
                dqn_ref[...] = jnp.zeros_like(dqn_ref)
                dqr_ref[...] = jnp.zeros_like(dqr_ref)
            for ii in range(nq):
                rws = slice(ii * tq, (ii + 1) * tq)
                deltas = []
                if fox:
                    prod = do_ref[0, rws, :].astype(F32) * o_ref[0, rws, :].astype(F32)
                    for e in range(2):
                        deltas.append(jnp.sum(prod[:, e * half:(e + 1) * half], axis=1, keepdims=True))
                    qt_scr[ii] = transposed(q_ref[0, rws, :] * q_scale)
                    dot_scr[ii] = transposed(do_ref[0, rws, :])
                else:
                    for e in range(2):
                        prod = do_ref[e, rws, :].astype(F32) * o_ref[e, rws, :].astype(F32)
                        deltas.append(jnp.sum(prod, axis=1, keepdims=True))
                        qt_scr[e, ii] = transposed(qn_ref[e, rws, :])
                        dot_scr[e, ii] = transposed(do_ref[e, rws, :])
                    qrt_scr[ii] = transposed(qr_ref[0, rws, :])
                for e in range(2):
                    delta_scr[e, rws, :] = jnp.broadcast_to(deltas[e], (tq, LANES))

        if fox:
            dfk_ref[...] = jnp.zeros_like(dfk_ref)
        else:
            @pl.when(jnp.logical_and(g == 0, j == 0))
            def _():
                dkr_ref[...] = jnp.zeros_like(dkr_ref)

        heads = []
        for e in range(2):
            sl = slice(e * half, (e + 1) * half)
            if fox:
                heads.append((sl, k_ref[0, :, sl], v_ref[0, :, sl], fk_ref[0, 0, e:e + 1, :]))
            else:
                heads.append((sl, jnp.concatenate([kn_ref[e], kr_ref[krows, :]], axis=1), v_ref[e], None))
        dk_w = dv_w = half if fox else LANES

        def step(i, carry, masked):
            rows = pl.ds(pl.multiple_of(i * tq, tq), tq)
            new = []
            for e, (sl, k_e, v_e, x_e) in enumerate(heads):
                dk_acc, dv_acc, last = carry[e]
                if fox:
                    do_i = do_ref[0, rows, sl]
                    sc = _nt(q_ref[0, rows, sl] * q_scale, k_e) + wide(fq_ref[e, rows, :]) - x_e
                else:
                    do_i = do_ref[e, rows, :]
                    q_cat = jnp.concatenate([qn_ref[e, rows, :], qr_ref[0, rows, sl], jnp.zeros((tq, half), BF16)], axis=1)
                    sc = _nt(q_cat, k_e) * scale
                if masked:
                    sc = jnp.where(row >= col, sc, NEG_INF)
                p = jnp.exp(sc - wide(lse_ref[e, rows, :]))
                dp = _nt(do_i, v_e)
                ds = p * (dp - wide(delta_scr[e, rows, :]))
                dsb = ds.astype(BF16) if fox else (ds * scale).astype(BF16)
                if fox:
                    fsl = slice(e * half, (e + 1) * half)
                    dv_acc = dv_acc + _nn(dot_scr[i, fsl, :], p.astype(BF16))
                    dk_acc = dk_acc + _nn(qt_scr[i, fsl, :], dsb)
                    dq_ref[0, rows, sl] += _nn(dsb, k_e) * scale
                    last = last - jnp.sum(ds, axis=0, keepdims=True)
                else:
                    dv_acc = dv_acc + _nn(dot_scr[e, i], p.astype(BF16))
                    dk_acc = dk_acc + _nn(qt_scr[e, i], dsb)
                    dq_cat = _nn(dsb, k_e)
                    dqn_ref[e, rows, :] += dq_cat[:, :LANES]
                    dqr_ref[0, rows, sl] += dq_cat[:, LANES:LANES + half]
                    last = last + _nn(qrt_scr[i, e * half:(e + 1) * half, :], dsb)
                new.append((dk_acc, dv_acc, last))
            return tuple(new)

        last0 = jnp.zeros((1, tq), F32) if fox else jnp.zeros((half, tq), F32)
        init = (jnp.zeros((dk_w, tq), F32), jnp.zeros((dv_w, tq), F32), last0)
        carry = step(j, (init, init), True)
        carry = lax.fori_loop(j + 1, nq, lambda i, c: step(i, c, False), carry)
        if fox:
            for e in range(2):
                dfk_ref[0, 0, e:e + 1, :] = carry[e][2]
            dk_ref[0] = jnp.concatenate([carry[0][0], carry[1][0]], axis=0).T.astype(BF16)
            dv_ref[0] = jnp.concatenate([carry[0][1], carry[1][1]], axis=0).T.astype(BF16)
        else:
            for e in range(2):
                dkn_ref[e] = carry[e][0].T.astype(BF16)
                dv_ref[e] = carry[e][1].T.astype(BF16)
            dkr_t = carry[0][2] + carry[1][2]
            dkr_ref[krows, :] += jnp.concatenate([dkr_t, jnp.zeros_like(dkr_t)], axis=0).T

    def whole(b, g, j):
        return (g, b, 0)

    def kblk(b, g, j):
        return (g, b * nq + j, 0)

    if fox:
        in_specs = [pl.BlockSpec((1, s, LANES), whole),
                    pl.BlockSpec((1, tq, LANES), lambda b, g, j: (n_pair + g, b * nq + j, 0)),
                    pl.BlockSpec((1, tq, LANES), lambda b, g, j: (2 * n_pair + g, b * nq + j, 0)),
                    pl.BlockSpec((2, s, LANES), whole),
                    pl.BlockSpec((1, 1, 8, tq), lambda b, g, j: (b * n_pair + g, j, 0, 0)),
                    pl.BlockSpec((1, s, LANES), whole), pl.BlockSpec((1, s, LANES), whole),
                    pl.BlockSpec((2, s, LANES), whole)]
        args = [qkv, qkv, qkv, fq, fk, o, do, lse]
        out_shape = [jax.ShapeDtypeStruct((8, t, LANES), F32), jax.ShapeDtypeStruct((8, t, LANES), BF16),
                     jax.ShapeDtypeStruct((8, t, LANES), BF16), jax.ShapeDtypeStruct(fk.shape, F32)]
        out_specs = [pl.BlockSpec((1, s, LANES), whole), pl.BlockSpec((1, tq, LANES), kblk),
                     pl.BlockSpec((1, tq, LANES), kblk),
                     pl.BlockSpec((1, 1, 8, tq), lambda b, g, j: (b * n_pair + g, j, 0, 0))]
    else:
        pair = pl.BlockSpec((2, s, LANES), whole)
        pair_k = pl.BlockSpec((2, tq, LANES), kblk)
        in_specs = [pair, pl.BlockSpec((1, s, LANES), whole), pair_k,
                    pl.BlockSpec((s, LANES), lambda b, g, j: (b, 0)), pair_k,
                    pair, pair, pair]
        args = [qn, qr, kn, kr, v, o, do, lse]
        out_shape = [jax.ShapeDtypeStruct((8, t, LANES), F32), jax.ShapeDtypeStruct((4, t, LANES), F32),
                     jax.ShapeDtypeStruct((8, t, LANES), BF16), jax.ShapeDtypeStruct((8, t, LANES), BF16),
                     jax.ShapeDtypeStruct((t, LANES), F32)]
        out_specs = [pair, pl.BlockSpec((1, s, LANES), whole), pair_k, pair_k,
                     pl.BlockSpec((s, LANES), lambda b, g, j: (b, 0))]
    t_blocks = pltpu.VMEM((nq, LANES, tq), BF16)
    t_pairs = pltpu.VMEM((2, nq, LANES, tq), BF16)
    scratch = [pltpu.VMEM((2, s, LANES), F32)] + ([t_blocks, t_blocks] if fox else [t_pairs, t_blocks, t_pairs])
    return pl.pallas_call(
        body, name=name, out_shape=out_shape, grid=(bl, n_pair, nq), in_specs=in_specs, out_specs=out_specs,
        scratch_shapes=scratch, compiler_params=_params("arbitrary", "arbitrary", "arbitrary"),
    )(*args)


def adamw(w, g, m, v, name):
    shape = w.shape
    c = shape[-1]
    r = w.size // c
    tr = _tile(r, 512, 8)

    def body(w_ref, g_ref, m_ref, v_ref, d_ref, nm_ref, nv_ref):
        gv = g_ref[...]
        m2 = ADAM_B1 * m_ref[...] + (1.0 - ADAM_B1) * gv
        v2 = ADAM_B2 * v_ref[...] + (1.0 - ADAM_B2) * (gv * gv)
        m_hat = m2 / (1.0 - ADAM_B1 ** ADAM_STEP)
        v_hat = v2 / (1.0 - ADAM_B2 ** ADAM_STEP)
        d_ref[...] = -ADAM_LR * (m_hat / (jnp.sqrt(v_hat) + ADAM_EPS) + ADAM_WD * w_ref[...])
        nm_ref[...] = m2
        nv_ref[...] = v2

    spec = pl.BlockSpec((tr, c), lambda i: (i, 0))
    outs = pl.pallas_call(
        body, name=name, out_shape=[jax.ShapeDtypeStruct((r, c), F32)] * 3, grid=(r // tr,),
        in_specs=[spec] * 4, out_specs=[spec] * 3, compiler_params=_params("arbitrary"),
    )(*(a.reshape(r, c) for a in (w, g, m, v)))
    return tuple(a.reshape(shape) for a in outs)


PACK_COLS = 1024


def _pack_rows(a):
    return a.reshape(-1, PACK_COLS)


def kernel(x, c, positions, mla_w_in, mla_g_q, mla_w_uq, mla_g_kv, mla_w_uk, mla_w_uv, mla_w_o, fox_w_in, fox_b_f, fox_w_o, ada_w, ada_b, ffn_w_gate, ffn_w_up, ffn_w_down, ln_g, ln_b, loss_target, m_mla_w_in, m_mla_g_q, m_mla_w_uq, m_mla_g_kv, m_mla_w_uk, m_mla_w_uv, m_mla_w_o, m_fox_w_in, m_fox_b_f, m_fox_w_o, m_ada_w, m_ada_b, m_ffn_w_gate, m_ffn_w_up, m_ffn_w_down, m_ln_g, m_ln_b, v_mla_w_in, v_mla_g_q, v_mla_w_uq, v_mla_g_kv, v_mla_w_uk, v_mla_w_uv, v_mla_w_o, v_fox_w_in, v_fox_b_f, v_fox_w_o, v_ada_w, v_ada_b, v_ffn_w_gate, v_ffn_w_up, v_ffn_w_down, v_ln_g, v_ln_b):
    bl, s, d = x.shape
    t = bl * s
    ff = ffn_w_gate.shape[-1] * N_DEV
    dev = 4 * lax.axis_index("x") + 2 * lax.axis_index("y") + lax.axis_index("c")
    ada_cols = ada_w.shape[-1]
    mla_in = mla_w_in.shape[-1]
    mla_in_pad = mla_in + (-mla_in) % LANES

    def t_last(a):
        return jnp.swapaxes(a, -1, -2)

    local = {
        "mla_w_in": mla_w_in[0],
        "mla_w_uq": t_last(mla_w_uq[0]),
        "mla_w_uk": t_last(mla_w_uk[0]),
        "mla_w_uv": t_last(mla_w_uv[0]),
        "mla_w_o": mla_w_o[0],
        "fox_w_in": t_last(fox_w_in[0]),
        "fox_w_o": fox_w_o[0],
    }
    for i in range(DEPTH):
        local.update({f"gate{i}": t_last(ffn_w_gate[i]), f"up{i}": t_last(ffn_w_up[i]), f"down{i}": ffn_w_down[i]})
    groups = [["mla_w_in", "mla_w_uq", "mla_w_uk", "mla_w_uv", "mla_w_o"],
              ["gate0", "up0", "down0"],
              ["fox_w_in", "fox_w_o"],
              ["gate1", "up1", "down1"]]
    offsets, rows_of, slot_of, group_of = {}, {}, {}, {}
    group_rows = []
    for gi, names in enumerate(groups):
        rows = 0
        for nm in names:
            rows_of[nm] = local[nm].size // PACK_COLS
            slot_of[nm] = rows_of[nm] + (-rows_of[nm]) % 16
            offsets[nm] = rows
            group_of[nm] = gi
            rows += slot_of[nm]
        group_rows.append(rows)

    def slot(nm, rows):
        pad = [(0, 0)] * rows.ndim
        pad[-2] = (0, slot_of[nm] - rows_of[nm])
        return jnp.pad(rows, pad)

    def held_until(block, arrays):
        zero = sum((a.reshape(-1)[0] * 0).astype(F32) for a in jax.tree.leaves(arrays))
        return block + zero.astype(block.dtype)

    def landing(block):
        land = lax.empty((N_DEV,) + block.shape, block.dtype)
        return lax.dynamic_update_slice(land, block[None], (dev, 0, 0))

    packed0 = jnp.concatenate([slot(nm, _pack_rows(local[nm]).astype(BF16)) for nm in groups[0]], axis=0)
    gathered0 = all_gather(packed0, "gather_mla_weights")
    gathered = {nm: gathered0[:, offsets[nm]:offsets[nm] + rows_of[nm], :] for nm in groups[0]}
    gather_started = [None] * len(groups)

    def depart(gi, after):
        blocks = [held_until(_pack_rows(local[nm]).astype(BF16), after) for nm in groups[gi]]
        gather_started[gi] = exchange_start(blocks, [landing(b) for b in blocks], f"gather_group{gi}_start", False)
        return gather_started[gi][4]

    def full(nm, cols):
        return gathered[nm].reshape(-1, cols)

    w_in = jnp.pad(full("mla_w_in", mla_in), ((0, 0), (0, mla_in_pad - mla_in)))
    wt_uq = full("mla_w_uq", MLA_QR).reshape(MLA_HEADS, MLA_NOPE + MLA_ROPE, MLA_QR)
    wt_uq_n = wt_uq[:, :MLA_NOPE].reshape(MLA_HEADS * MLA_NOPE, MLA_QR)
    wt_uq_r = wt_uq[:, MLA_NOPE:].reshape(MLA_HEADS * MLA_ROPE, MLA_QR)
    wt_uk = full("mla_w_uk", MLA_KVR)
    wt_uv = full("mla_w_uv", MLA_KVR)
    w_mo = full("mla_w_o", d)
    wt_gate, wt_up, w_down = [None] * DEPTH, [None] * DEPTH, [None] * DEPTH

    def arrive(gi, after):
        if gi + 1 < len(groups):
            after = depart(gi + 1, after)
        landed = list(exchange_wait(gather_started[gi], after, f"gather_group{gi}_wait", False))
        gathered.update(zip(groups[gi], landed))
        for i in range(DEPTH):
            if group_of[f"gate{i}"] == gi:
                wt_gate[i], wt_up[i], w_down[i] = full(f"gate{i}", d), full(f"up{i}", d), full(f"down{i}", d)

    small = jnp.concatenate([c.reshape(-1, LANES), ln_g.reshape(-1, LANES), ln_b.reshape(-1, LANES)], axis=0)
    small_rows = small.shape[0]
    small = jnp.pad(small, ((0, (-small_rows) % 8), (0, 0)))
    small_all = all_gather(small, "gather_small")
    c_rows = bl * d // LANES
    c_all = small_all[:, :c_rows].reshape(N_DEV * bl, d)
    n_ln = DEPTH * 2
    ln_g_all = small_all[:, c_rows:c_rows + n_ln, :].transpose(1, 0, 2).reshape(DEPTH, 2, 1, d)
    ln_b_all = small_all[:, c_rows + n_ln:c_rows + 2 * n_ln, :].transpose(1, 0, 2).reshape(DEPTH, 2, 1, d)

    c_act = silu_rows(c_all, "silu_c")
    ada_b_loc = lax.dynamic_slice_in_dim(ada_b, dev * ada_cols, ada_cols, axis=1)
    mod_cols = [mm([(c_act, ada_w[i])], trans_b=False, out_dtype=F32, name=f"ada_fwd{i}", bias=ada_b_loc[i][None, :])
                for i in range(DEPTH)]
    mod_all = all_gather(jnp.concatenate(mod_cols, axis=0), "gather_mod")
    mod_all = mod_all.reshape(N_DEV, DEPTH, N_DEV * bl, ada_cols).transpose(1, 2, 0, 3).reshape(DEPTH, N_DEV * bl, 6 * d)
    mod_mine = lax.dynamic_slice_in_dim(mod_all, dev * bl, bl, axis=1)
    mods = [mod_mine[i].reshape(bl * 6, 1, d) for i in range(DEPTH)]
    mods[0] = mods[0] + depart(1, (mod_mine, gathered0))[0, 0]

    half_r = MLA_ROPE // 2
    inv_freq = ROPE_THETA ** (-jnp.arange(half_r, dtype=F32) / half_r)
    inv_freq = jnp.tile(inv_freq, LANES // half_r)[None, :]
    sign = jnp.tile(jnp.concatenate([-jnp.ones((half_r,), F32), jnp.ones((half_r,), F32)]), LANES // MLA_ROPE)[None, :]
    cos_t, sin_t = rope_tables(positions.astype(F32).reshape(t, 1), inv_freq, sign, "rope_tables")

    x2d = x.reshape(t, d)
    g_q, g_kv = mla_g_q.reshape(1, MLA_QR), mla_g_kv.reshape(1, MLA_KVR)
    b_f = jnp.pad(fox_b_f.reshape(1, FOX_HEADS), ((0, 0), (0, LANES - FOX_HEADS)))
    mla_scale = (MLA_NOPE + MLA_ROPE) ** -0.5
    fox_scale = FOX_HD ** -0.5
    tq = _attn_tiles(s)
    nk = s // tq

    saved = []
    u = modulate(x2d, mods[0], 0, 1, bl, "modulate0")
    xin = x2d
    for i in range(DEPTH):
        sv = {"u": u, "x_in": xin}
        if i % 2 == 0:
            h_in = mm([(u, w_in)], trans_b=False, out_dtype=F32, name=f"mla_in{i}")
            c_q, c_kv, k_r = mla_latents_forward(h_in, g_q, g_kv, cos_t, sin_t, f"mla_latents{i}")
            q_n = mm([(c_q, wt_uq_n)], trans_b=True, out_dtype=BF16, out_slab=True, name=f"mla_qn{i}")
            q_r_raw = mm([(c_q, wt_uq_r)], trans_b=True, out_dtype=F32, out_slab=True, name=f"mla_qr{i}")
            q_r = rope_slabs(q_r_raw, cos_t, sin_t, BF16, f"mla_qrope{i}")
            k_n = mm([(c_kv, wt_uk)], trans_b=True, out_dtype=BF16, out_slab=True, name=f"mla_kn{i}")
            v_m = mm([(c_kv, wt_uv)], trans_b=True, out_dtype=BF16, out_slab=True, name=f"mla_v{i}")
            ops = (q_n, q_r, k_n, k_r, v_m)
            o, lse, o_delta = attention_forward("mla", ops, bl, mla_scale, f"mla_attn{i}")
            y = mm([(o, w_mo)], trans_b=False, out_dtype=F32, name=f"mla_out{i}")
            sv.update(h_in=h_in, c_q=c_q, c_kv=c_kv, ops=ops, o=o, lse=lse, o_delta=o_delta)
        else:
            arrive(2, u)
            wt_fox = full("fox_w_in", d)
            wt_qkv = wt_fox[:3 * d]
            wt_f = jnp.pad(wt_fox[3 * d:], ((0, LANES - FOX_HEADS), (0, 0)))
            w_fo = full("fox_w_o", d)
            qkv = mm([(u, wt_qkv)], trans_b=True, out_dtype=BF16, out_slab=True, name=f"fox_qkv{i}")
            z = mm([(u, wt_f)], trans_b=True, out_dtype=F32, name=f"fox_z{i}")
            f_tok, f_q = fox_gate_forward(z, b_f, bl, f"fox_gate{i}")
            f_k = f_tok[:, :FOX_HEADS].reshape(bl, nk, tq, FOX_HEADS // 2, 2).transpose(0, 3, 1, 4, 2)
            f_k = jnp.pad(f_k.reshape(bl * FOX_HEADS // 2, nk, 2, tq), ((0, 0), (0, 0), (0, 6), (0, 0)))
            ops = (qkv, f_q, f_k)
            o, lse, o_delta = attention_forward("fox", ops, bl, fox_scale, f"fox_attn{i}")
            y = mm([(o, w_fo)], trans_b=False, out_dtype=F32, name=f"fox_out{i}")
            sv.update(z=z, ops=ops, o=o, lse=lse, o_delta=o_delta)
        x1, r1, u2 = residual_layer_norm(xin, y, mods[i], 2, ln_g_all[i, 0], ln_b_all[i, 0], bl, f"ln_mix{i}",
                                         next_mod=(3, 4))
        if wt_gate[i] is None:
            arrive(group_of[f"gate{i}"], u2)
        a, bb, h = swiglu_in(u2, wt_gate[i], wt_up[i], f"ffn_in{i}")
        y2 = mm([(h, w_down[i])], trans_b=False, out_dtype=F32, name=f"ffn_down{i}")
        sv.update(y=y, r1=r1, u2=u2, a=a, bb=bb, h=h, y2=y2)
        if i + 1 < DEPTH:
            xin, r2, u = residual_layer_norm(x1, y2, mods[i], 5, ln_g_all[i, 1], ln_b_all[i, 1], bl, f"ln_ffn{i}",
                                             next_mod=(0, 1, mods[i + 1]))
        else:
            xin, r2 = residual_layer_norm(x1, y2, mods[i], 5, ln_g_all[i, 1], ln_b_all[i, 1], bl, f"ln_ffn{i}")
        sv.update(r2=r2)
        saved.append(sv)

    loss_cols, d_x = loss_head(xin, loss_target.reshape(t, d), "loss_head")

    grads_full = {}
    wgrad = functools.partial(mm_tn, out_dtype=BF16)
    dmod = [[None] * 6 for _ in range(DEPTH)]
    dg_ln = [[None, None] for _ in range(DEPTH)]
    db_ln = [[None, None] for _ in range(DEPTH)]
    dg_q = dg_kv = db_f = None
    d_a, du = d_x, None
    scatter_started = [None] * len(groups)

    def scatter_start(gi, after=None):
        gs = [grads_full[nm].reshape(N_DEV, rows_of[nm], PACK_COLS).astype(BF16) for nm in groups[gi]]
        if gi == 0:
            gs = [jnp.concatenate([slot(nm, g) for nm, g in zip(groups[gi], gs)], axis=1)]
        if after is not None:
            gs = [held_until(g, after) for g in gs]
        lands = [landing(lax.dynamic_index_in_dim(g, dev, 0, keepdims=False)) for g in gs]
        scatter_started[gi] = exchange_start(gs, lands, f"scatter_group{gi}_start", True)

    ln_g_bwd = [[ln_g_all[i, k] for k in range(2)] for i in range(DEPTH)]
    for i in reversed(range(DEPTH)):
        sv = saved[i]
        if i + 1 < DEPTH:
            gi = group_of["fox_w_in"]
            scatter_start(gi)
            ln_g_bwd[i][1] = after_token(ln_g_bwd[i][1], scatter_started[gi])
        ln2 = (sv["r2"], sv["y2"], ln_g_bwd[i][1], ln_b_all[i, 1], (mods[i], 5))
        if du is None:
            bw = sublayer_backward(d_a, bl, f"bwd_ln_ffn{i}", ln=ln2)
        else:
            bw = sublayer_backward(d_a, bl, f"bwd_ln_ffn{i}", du=du, scale=(mods[i + 1], 1), ln=ln2)
            dmod[i + 1][0], dmod[i + 1][1] = bw["dshift"], bw["dscale"]
        dmod[i][5], dg_ln[i][1], db_ln[i][1] = bw["dgate"], bw["dg"], bw["db"]
        dy2 = bw["dy"]
        da, dbb = swiglu_out_backward(dy2, w_down[i], sv["a"], sv["bb"], f"bwd_ffn_act{i}")
        du2 = mm([(da, wt_gate[i]), (dbb, wt_up[i])], trans_b=False, out_dtype=F32, name=f"bwd_ffn_du{i}")
        grads_full[f"down{i}"] = wgrad(sv["h"], dy2, name=f"bwd_w_down{i}")
        grads_full[f"gate{i}"] = wgrad(da, sv["u2"], name=f"bwd_w_gate{i}")
        grads_full[f"up{i}"] = wgrad(dbb, sv["u2"], name=f"bwd_w_up{i}")
        gi = group_of[f"gate{i}"]
        scatter_start(gi)
        ln_g_bwd[i][0] = after_token(ln_g_bwd[i][0], scatter_started[gi])
        bw = sublayer_backward(bw["dx"], bl, f"bwd_ln_mix{i}", du=du2, scale=(mods[i], 4),
                               ln=(sv["r1"], sv["y"], ln_g_bwd[i][0], ln_b_all[i, 0], (mods[i], 2)))
        dmod[i][3], dmod[i][4], dmod[i][2] = bw["dshift"], bw["dscale"], bw["dgate"]
        dg_ln[i][0], db_ln[i][0] = bw["dg"], bw["db"]
        d_a, dy = bw["dx"], bw["dy"]
        o, lse, ops = sv["o"], sv["lse"], sv["ops"]
        if i % 2 == 0:
            do = mm([(dy, w_mo)], trans_b=True, out_dtype=BF16, out_slab=True, name=f"bwd_mla_do{i}")
            grads_full["mla_w_o"] = wgrad(o, dy, name=f"bwd_w_mla_o{i}")
            dqn, dqr, dkn, dvm, dkr = attention_backward("mla", ops, sv["o_delta"], do, lse, bl, mla_scale,
                                                         f"bwd_mla_attn{i}")
            dqr = rope_slabs(dqr, cos_t, sin_t, F32, f"bwd_mla_qrope{i}", transposed=True)
            dcq = mm([(dqn, wt_uq_n), (dqr, wt_uq_r)], trans_b=False, out_dtype=F32, name=f"bwd_mla_dcq{i}")
            dckv = mm([(dkn, wt_uk), (dvm, wt_uv)], trans_b=False, out_dtype=F32, name=f"bwd_mla_dckv{i}")
            d_uq_n = wgrad(dqn, sv["c_q"], name=f"bwd_w_uq_n{i}").reshape(MLA_HEADS, MLA_NOPE, MLA_QR)
            d_uq_r = wgrad(dqr, sv["c_q"], name=f"bwd_w_uq_r{i}").reshape(MLA_HEADS, MLA_ROPE, MLA_QR)
            grads_full["mla_w_uq"] = jnp.concatenate([d_uq_n, d_uq_r], axis=1)
            grads_full["mla_w_uk"] = wgrad(dkn, sv["c_kv"], name=f"bwd_w_uk{i}")
            grads_full["mla_w_uv"] = wgrad(dvm, sv["c_kv"], name=f"bwd_w_uv{i}")
            dh_in, dg_q, dg_kv = mla_latents_backward(sv["h_in"], dcq, dckv, dkr, g_q, g_kv, cos_t, sin_t,
                                                      f"bwd_mla_latents{i}")
            du = mm([(dh_in, w_in)], trans_b=True, out_dtype=F32, name=f"bwd_mla_du{i}")
            grads_full["mla_w_in"] = wgrad(sv["u"], dh_in, name=f"bwd_w_mla_in{i}")[:, :mla_in]
        else:
            do = mm([(dy, w_fo)], trans_b=True, out_dtype=BF16, out_slab=True, name=f"bwd_fox_do{i}")
            grads_full["fox_w_o"] = wgrad(o, dy, name=f"bwd_w_fox_o{i}")
            dq, dk, dvf, dfk = attention_backward("fox", ops, sv["o_delta"], do, lse, bl, fox_scale, f"bwd_fox_attn{i}")
            df = dfk[:, :, :2, :].reshape(bl, FOX_HEADS // 2, nk, 2, tq).transpose(0, 2, 4, 1, 3).reshape(t, FOX_HEADS)
            df = jnp.pad(df, ((0, 0), (0, LANES - FOX_HEADS)))
            dz, db_f = fox_gate_backward(sv["z"], b_f, df, bl, f"bwd_fox_gate{i}")
            du = mm([(dq, wt_fox[0:d]), (dk, wt_fox[d:2 * d]), (dvf, wt_fox[2 * d:3 * d]), (dz, wt_f)],
                    trans_b=False, out_dtype=F32, name=f"bwd_fox_du{i}")
            u_f = sv["u"]
            grads_full["fox_w_in"] = jnp.concatenate(
                [wgrad(dq, u_f, name=f"bwd_w_fox_q{i}"), wgrad(dk, u_f, name=f"bwd_w_fox_k{i}"),
                 wgrad(dvf, u_f, name=f"bwd_w_fox_v{i}"), wgrad(dz, u_f, name=f"bwd_w_fox_f{i}")[:FOX_HEADS]], axis=0)
    scatter_start(0)
    bw = sublayer_backward(d_a, bl, "bwd_input", du=du, scale=(after_token(mods[0], scatter_started[0]), 1), x_in=x2d)
    dmod[0][0], dmod[0][1] = bw["dshift"], bw["dscale"]
    grad_x = bw["dx"].reshape(bl, s, d)

    dmod_rows = jnp.concatenate([r.reshape(bl, d) for layer in dmod for r in layer], axis=0)
    dmod_rows = dmod_rows.reshape(DEPTH, 6, bl, d).transpose(0, 2, 1, 3)
    n_mod = dmod_rows.size // LANES
    ln_parts = [dg_ln[i][k] for i in range(DEPTH) for k in range(2)] + [db_ln[i][k] for i in range(DEPTH) for k in range(2)]
    small_g = jnp.concatenate([dmod_rows.reshape(-1, LANES), dg_q.reshape(-1, LANES), dg_kv.reshape(-1, LANES), db_f]
                              + [p.reshape(-1, LANES) for p in ln_parts] + [loss_cols.reshape(-1, LANES)], axis=0)
    n_small = small_g.shape[0]
    small_g = jnp.pad(small_g, ((0, (-n_small) % 8), (0, 0)))
    small_g_all = all_gather(small_g, "gather_small_grads")
    small_sum = sum_leading(small_g_all, "sum_small_grads")
    per_seq = DEPTH * 6 * d // LANES
    dmod_all = small_g_all[:, :n_mod].reshape(N_DEV, DEPTH, bl, 6 * d).transpose(1, 0, 2, 3)
    dmod_all = dmod_all.reshape(DEPTH, N_DEV * bl, 6 * d)
    o1 = n_mod
    grad_g_q = small_sum[o1:o1 + MLA_QR // LANES].reshape(1, MLA_QR)
    o1 += MLA_QR // LANES
    grad_g_kv = small_sum[o1:o1 + MLA_KVR // LANES].reshape(1, MLA_KVR)
    o1 += MLA_KVR // LANES
    grad_b_f = small_sum[o1:o1 + 1, :FOX_HEADS]
    o1 += 1
    n_ln_rows = DEPTH * 2 * d // LANES
    grad_ln_g_full = small_sum[o1:o1 + n_ln_rows].reshape(DEPTH, 2, d)
    grad_ln_b_full = small_sum[o1 + n_ln_rows:o1 + 2 * n_ln_rows].reshape(DEPTH, 2, d)
    loss = jnp.sum(small_sum[o1 + 2 * n_ln_rows:o1 + 2 * n_ln_rows + d // LANES])
    shard = d // N_DEV
    grad_ln_g = lax.dynamic_slice_in_dim(grad_ln_g_full, dev * shard, shard, axis=2)
    grad_ln_b = lax.dynamic_slice_in_dim(grad_ln_b_full, dev * shard, shard, axis=2)
    by_seq = small_g_all[:, :n_mod].reshape(N_DEV, DEPTH, bl, 6 * d // LANES, LANES).transpose(0, 2, 1, 3, 4)
    grad_ada_b = sum_leading(by_seq.reshape(N_DEV * bl, per_seq, LANES), "sum_ada_b").reshape(DEPTH, 6 * d)
    dmod_cols = lax.dynamic_slice_in_dim(dmod_all, dev * ada_cols, ada_cols, axis=2)
    grad_ada_w = jnp.stack([mm_tn(c_act, dmod_cols[i], name=f"bwd_w_ada{i}") for i in range(DEPTH)])

    g_mine = {}

    def scatter_arrive(gi, after):
        landed = exchange_wait(scatter_started[gi], after, f"scatter_group{gi}_wait", True)
        if gi == 0:
            total = sum_leading(landed[0], f"scatter_group{gi}_sum")
            g_mine.update({nm: total[offsets[nm]:offsets[nm] + rows_of[nm]] for nm in groups[gi]})
            return total
        for nm, land in zip(groups[gi], landed):
            g_mine[nm] = sum_leading(land, f"scatter_sum_{nm}")
        return g_mine[groups[gi][-1]]

    after = scatter_started[0][4]
    for gi in reversed(range(1, len(groups))):
        after = scatter_arrive(gi, after)

    def mine(nm, shape):
        return g_mine[nm].reshape(shape)

    def shard_t(nm, a):
        return mine(nm, t_last(a).shape)

    transposed = {"mla_w_uq", "mla_w_uk", "mla_w_uv", "fox_w_in", "ffn_w_gate", "ffn_w_up"}
    grads = {
        "mla_w_in": lambda: mine("mla_w_in", mla_w_in[0].shape)[None],
        "mla_g_q": lambda: grad_g_q,
        "mla_w_uq": lambda: shard_t("mla_w_uq", mla_w_uq[0])[None],
        "mla_g_kv": lambda: grad_g_kv,
        "mla_w_uk": lambda: shard_t("mla_w_uk", mla_w_uk[0])[None],
        "mla_w_uv": lambda: shard_t("mla_w_uv", mla_w_uv[0])[None],
        "mla_w_o": lambda: mine("mla_w_o", mla_w_o[0].shape)[None],
        "fox_w_in": lambda: shard_t("fox_w_in", fox_w_in[0])[None],
        "fox_b_f": lambda: grad_b_f,
        "fox_w_o": lambda: mine("fox_w_o", fox_w_o[0].shape)[None],
        "ada_w": lambda: grad_ada_w,
        "ada_b": lambda: grad_ada_b,
        "ffn_w_gate": lambda: jnp.stack([shard_t(f"gate{i}", ffn_w_gate[i]) for i in range(DEPTH)]),
        "ffn_w_up": lambda: jnp.stack([shard_t(f"up{i}", ffn_w_up[i]) for i in range(DEPTH)]),
        "ffn_w_down": lambda: jnp.stack([mine(f"down{i}", ffn_w_down[i].shape) for i in range(DEPTH)]),
        "ln_g": lambda: grad_ln_g,
        "ln_b": lambda: grad_ln_b,
    }
    weights = dict(mla_w_in=mla_w_in, mla_g_q=mla_g_q, mla_w_uq=mla_w_uq, mla_g_kv=mla_g_kv, mla_w_uk=mla_w_uk,
                   mla_w_uv=mla_w_uv, mla_w_o=mla_w_o, fox_w_in=fox_w_in, fox_b_f=fox_b_f, fox_w_o=fox_w_o,
                   ada_w=ada_w, ada_b=ada_b, ffn_w_gate=ffn_w_gate, ffn_w_up=ffn_w_up, ffn_w_down=ffn_w_down,
                   ln_g=ln_g, ln_b=ln_b)
    first = dict(mla_w_in=m_mla_w_in, mla_g_q=m_mla_g_q, mla_w_uq=m_mla_w_uq, mla_g_kv=m_mla_g_kv, mla_w_uk=m_mla_w_uk,
                 mla_w_uv=m_mla_w_uv, mla_w_o=m_mla_w_o, fox_w_in=m_fox_w_in, fox_b_f=m_fox_b_f, fox_w_o=m_fox_w_o,
                 ada_w=m_ada_w, ada_b=m_ada_b, ffn_w_gate=m_ffn_w_gate, ffn_w_up=m_ffn_w_up, ffn_w_down=m_ffn_w_down,
                 ln_g=m_ln_g, ln_b=m_ln_b)
    second = dict(mla_w_in=v_mla_w_in, mla_g_q=v_mla_g_q, mla_w_uq=v_mla_w_uq, mla_g_kv=v_mla_g_kv, mla_w_uk=v_mla_w_uk,
                  mla_w_uv=v_mla_w_uv, mla_w_o=v_mla_w_o, fox_w_in=v_fox_w_in, fox_b_f=v_fox_b_f, fox_w_o=v_fox_w_o,
                  ada_w=v_ada_w, ada_b=v_ada_b, ffn_w_gate=v_ffn_w_gate, ffn_w_up=v_ffn_w_up, ffn_w_down=v_ffn_w_down,
                  ln_g=v_ln_g, ln_b=v_ln_b)
    order = list(weights)
    last = [nm for nm in order if group_of.get(nm) == 0]
    updated = {}
    for nm in [nm for nm in order if nm not in last] + last:
        if last and nm == last[0]:
            scatter_arrive(0, after)
        lay = t_last if nm in transposed else (lambda a: a)
        w = lay(weights[nm])
        g = grads[nm]().reshape(w.shape)
        delta, new_m, new_v = adamw(w, g, lay(first[nm]), lay(second[nm]), f"adamw_{nm}")
        updated[nm] = (lay(g), lay(delta), lay(new_m), lay(new_v))
        after = new_v
    return (loss, grad_x, *(updated[nm][k] for k in range(4) for nm in order))
```

```python
import functools
import math

import jax
import jax.numpy as jnp
from jax import lax
from jax.experimental import pallas as pl
from jax.experimental.pallas import tpu as pltpu

F32 = jnp.float32
BF16 = jnp.bfloat16
LANES = 128
N_DEV = 8
VMEM_LIMIT_BYTES = 56 * 1024 * 1024

DEPTH = 2
MLA_HEADS = 8
MLA_NOPE = 128
MLA_ROPE = 64
MLA_V = 128
MLA_QR = 256
MLA_KVR = 256
ROPE_THETA = 10000.0
FOX_HEADS = 16
FOX_HD = 64
ALPHA = (2.0 * DEPTH) ** 0.25
NORM_EPS = 1e-5
ADAM_LR = 0.001
ADAM_B1 = 0.9
ADAM_B2 = 0.999
ADAM_EPS = 1e-08
ADAM_WD = 0.01
ADAM_STEP = 10

MESH = pl.DeviceIdType.MESH


def _params(*sem):
    return pltpu.CompilerParams(dimension_semantics=sem, vmem_limit_bytes=VMEM_LIMIT_BYTES)


def _tile(n, cap, mult=LANES):
    if n <= cap:
        return n
    best = None
    for t in range(mult, cap + 1, mult):
        if n % t == 0:
            best = t
    assert best is not None, (n, cap, mult)
    return best


def _dot(a, b, dims):
    return lax.dot_general(a, b, (dims, ((), ())), preferred_element_type=F32)


def _nn(a, b):
    return _dot(a, b, ((1,), (0,)))


def _nt(a, b):
    return _dot(a, b, ((1,), (1,)))


def _tn(a, b):
    return _dot(a, b, ((0,), (0,)))


def _me():
    return lax.axis_index("x"), lax.axis_index("y"), lax.axis_index("c")


def all_gather(x_loc, name):
    r, c = x_loc.shape

    def body(x_ref, out_ref, send_sems, recv_sems, local_sem):
        x, y, cc = _me()
        me, sibling = (x, y, cc), (x, y, 1 - cc)
        chips = [(1 - x, y), (x, 1 - y), (1 - x, 1 - y)]

        def rows(px, py, pc):
            return out_ref.at[4 * px + 2 * py + pc]

        def copy(k, block, to, src=None):
            return pltpu.make_async_remote_copy(
                src_ref=rows(*block) if src is None else src, dst_ref=rows(*block),
                send_sem=send_sems.at[k], recv_sem=recv_sems.at[k], device_id=to, device_id_type=MESH)

        mine = pltpu.make_async_copy(x_ref, rows(*me), local_sem)
        mine.start()
        first = [copy(0, me, sibling, src=x_ref)]
        first += [copy(1 + j, me, (*chip, cc), src=x_ref) for j, chip in enumerate(chips)]
        for cp in first:
            cp.start()
        passed = [copy(4 + j, (*chip, cc), sibling) for j, chip in enumerate(chips)]
        for j, chip in enumerate(chips):
            copy(1 + j, (*chip, cc), me).wait_recv()
            passed[j].start()
        copy(0, sibling, me).wait_recv()
        for j, chip in enumerate(chips):
            copy(4 + j, (*chip, 1 - cc), me).wait_recv()
        for cp in first + passed:
            cp.wait_send()
        mine.wait()

    return pl.pallas_call(
        body, name=name,
        out_shape=jax.ShapeDtypeStruct((N_DEV, r, c), x_loc.dtype),
        in_specs=[pl.BlockSpec(memory_space=pl.ANY)],
        out_specs=pl.BlockSpec(memory_space=pl.ANY),
        scratch_shapes=[pltpu.SemaphoreType.DMA((7,)), pltpu.SemaphoreType.DMA((7,)), pltpu.SemaphoreType.DMA(())],
    )(x_loc)


HBM_SPEC = pl.BlockSpec(memory_space=pltpu.HBM)
SEM_SPEC = pl.BlockSpec(memory_space=pltpu.SEMAPHORE)
N_PEERS = N_DEV - 1


def _peer(k):
    x, y, c = _me()
    return (1 - x if k & 4 else x, 1 - y if k & 2 else y, 1 - c if k & 1 else c)


def _exchange_copies(src_refs, land_refs, send_sems, recv_sems, scatter):
    x, y, c = _me()
    mine = 4 * x + 2 * y + c
    copies = []
    for n, (src_ref, land_ref) in enumerate(zip(src_refs, land_refs)):
        for k in range(1, N_DEV):
            px, py, pc = _peer(k)
            src = src_ref.at[4 * px + 2 * py + pc] if scatter else src_ref
            sem = n * N_PEERS + k - 1
            copies.append(pltpu.make_async_remote_copy(
                src_ref=src, dst_ref=land_ref.at[mine], send_sem=send_sems.at[sem], recv_sem=recv_sems.at[sem],
                device_id=(px, py, pc), device_id_type=MESH))
    return copies


def exchange_start(srcs, lands, name, scatter):
    n = len(srcs)

    def body(*refs):
        send_sems, recv_sems = refs[2 * n], refs[2 * n + 1]
        for cp in _exchange_copies(refs[:n], refs[n:2 * n], send_sems, recv_sems, scatter):
            cp.start()
        token = refs[-1]
        token[...] = jnp.zeros_like(token)

    outs = pl.pallas_call(
        body, name=name,
        out_shape=(pltpu.SemaphoreType.DMA((n * N_PEERS,)), pltpu.SemaphoreType.DMA((n * N_PEERS,)),
                   *(pltpu.HBM(a.shape, a.dtype) for a in (*srcs, *lands)), jax.ShapeDtypeStruct((8, LANES), F32)),
        in_specs=(HBM_SPEC,) * (2 * n),
        out_specs=(SEM_SPEC, SEM_SPEC, *((HBM_SPEC,) * (2 * n)), pl.BlockSpec(memory_space=pltpu.VMEM)),
        input_output_aliases={i: 2 + i for i in range(2 * n)},
        compiler_params=pltpu.CompilerParams(has_side_effects=pltpu.SideEffectType.DATAFLOW_SIDE_EFFECTING),
    )(*(pltpu.with_memory_space_constraint(a, pltpu.HBM) for a in (*srcs, *lands)))
    return outs[0], outs[1], outs[2:2 + n], outs[2 + n:2 + 2 * n], outs[-1]


def exchange_wait(started, after, name, scatter):
    send_sems, recv_sems, srcs, lands, _ = started
    n = len(srcs)

    def body(*refs):
        send_sems, recv_sems = refs[2 * n], refs[2 * n + 1]
        for cp in _exchange_copies(refs[:n], refs[n:2 * n], send_sems, recv_sems, scatter):
            cp.wait_send()
            cp.wait_recv()

    outs = pl.pallas_call(
        body, name=name,
        out_shape=tuple(pltpu.HBM(a.shape, a.dtype) for a in (*srcs, *lands)),
        in_specs=(*((HBM_SPEC,) * (2 * n)), SEM_SPEC, SEM_SPEC, pl.BlockSpec(memory_space=pl.ANY)),
        out_specs=(HBM_SPEC,) * (2 * n), input_output_aliases={i: i for i in range(2 * n)},
        compiler_params=pltpu.CompilerParams(has_side_effects=pltpu.SideEffectType.DATAFLOW_SIDE_EFFECTING),
    )(*srcs, *lands, send_sems, recv_sems, after)
    return outs[n:]


def after_token(small, started):
    return small + started[4][0, 0]


def sum_leading(x, name):
    n, r, c = x.shape
    tr = _tile(r, 512, 16)

    def body(x_ref, o_ref):
        acc = x_ref[0].astype(F32)
        for k in range(1, n):
            acc = acc + x_ref[k].astype(F32)
        o_ref[...] = acc

    return pl.pallas_call(
        body, name=name,
        out_shape=jax.ShapeDtypeStruct((r, c), F32),
        grid=(r // tr,),
        in_specs=[pl.BlockSpec((n, tr, c), lambda i: (0, i, 0))],
        out_specs=pl.BlockSpec((tr, c), lambda i: (i, 0)),
        compiler_params=_params("arbitrary"),
    )(x)


MM_VMEM_BUDGET = 36 * 1024 * 1024
GRID_STEP_AS_BYTES = 1 << 20


def _mm_tiles(m, n, a_row_bytes, b_col_bytes, out_bytes):
    tms = [c for c in (2048, 1024, 512, 256, 128, 64, 32, 16, 8) if m % c == 0] or [m]
    tns = [c for c in range(LANES, min(n, 2048) + 1, LANES) if n % c == 0] or [n]
    best = None
    for tm in tms:
        for tn in tns:
            vmem = 2 * (tm * a_row_bytes + tn * b_col_bytes) + 2 * tm * tn * out_bytes + tm * tn * 4
            if vmem > MM_VMEM_BUDGET:
                continue
            steps = (m // tm) * (n // tn)
            cost = steps * GRID_STEP_AS_BYTES + (m // tm) * n * b_col_bytes + m * a_row_bytes
            if best is None or cost < best[0]:
                best = (cost, tm, tn)
    assert best is not None, (m, n, a_row_bytes, b_col_bytes)
    return best[1], best[2]


def mm(pairs, *, trans_b, out_dtype, name, out_slab=False, bias=None):
    a0 = pairs[0][0]
    m = a0.shape[1] if a0.ndim == 3 else a0.shape[0]
    n = pairs[0][1].shape[0] if trans_b else pairs[0][1].shape[1]
    a_row_bytes = sum((b.shape[1] if trans_b else b.shape[0]) * a.dtype.itemsize for a, b in pairs)
    b_col_bytes = sum((b.shape[1] if trans_b else b.shape[0]) * b.dtype.itemsize for _, b in pairs)
    tm, tn = _mm_tiles(m, n, a_row_bytes, b_col_bytes, jnp.dtype(out_dtype).itemsize)
    slabs = [a.ndim == 3 for a, _ in pairs]
    n_pairs = len(pairs)

    def body(*refs):
        o_ref = refs[-1]
        acc = bias_ref = None
        if bias is not None:
            bias_ref = refs[2 * n_pairs]
        for i in range(n_pairs):
            a_ref, b_ref = refs[2 * i], refs[2 * i + 1]
            if slabs[i]:
                a = jnp.concatenate([a_ref[s].astype(BF16) for s in range(a_ref.shape[0])], axis=1)
            else:
                a = a_ref[...].astype(BF16)
            b = b_ref[...].astype(BF16)
            part = _nt(a, b) if trans_b else _nn(a, b)
            acc = part if acc is None else acc + part
        if bias_ref is not None:
            acc = acc + bias_ref[...]
        if out_slab:
            for s in range(tn // LANES):
                o_ref[s] = acc[:, s * LANES:(s + 1) * LANES].astype(out_dtype)
        else:
            o_ref[...] = acc.astype(out_dtype)

    in_specs, args = [], []
    for (a, b), slab in zip(pairs, slabs):
        if slab:
            in_specs.append(pl.BlockSpec((a.shape[0], tm, LANES), lambda i, j: (0, i, 0)))
        else:
            in_specs.append(pl.BlockSpec((tm, a.shape[1]), lambda i, j: (i, 0)))
        if trans_b:
            in_specs.append(pl.BlockSpec((tn, b.shape[1]), lambda i, j: (j, 0)))
        else:
            in_specs.append(pl.BlockSpec((b.shape[0], tn), lambda i, j: (0, j)))
        args += [a, b]
    if bias is not None:
        in_specs.append(pl.BlockSpec((1, tn), lambda i, j: (0, j)))
        args.append(bias)
    if out_slab:
        out_shape = jax.ShapeDtypeStruct((n // LANES, m, LANES), out_dtype)
        out_spec = pl.BlockSpec((tn // LANES, tm, LANES), lambda i, j: (j, i, 0))
    else:
        out_shape = jax.ShapeDtypeStruct((m, n), out_dtype)
        out_spec = pl.BlockSpec((tm, tn), lambda i, j: (i, j))
    return pl.pallas_call(
        body, name=name, out_shape=out_shape, grid=(m // tm, n // tn),
        in_specs=in_specs, out_specs=out_spec,
        compiler_params=_params("arbitrary", "arbitrary"),
    )(*args)


def mm_tn(a, b, *, name, out_dtype=F32, tk_cap=1536, tn_cap=1024, tm_cap=1024):
    slab = a.ndim == 3
    m = a.shape[1] if slab else a.shape[0]
    k = a.shape[0] * LANES if slab else a.shape[1]
    n = b.shape[1]
    tk = _tile(k, tk_cap)
    tn = _tile(n, tn_cap)
    tm = _tile(m, tm_cap, 8)
    n_steps = m // tm

    def body(a_ref, b_ref, o_ref, acc_ref):
        step = pl.program_id(2)

        @pl.when(step == 0)
        def _():
            acc_ref[...] = jnp.zeros_like(acc_ref)

        bb = b_ref[...].astype(BF16)
        if slab:
            for s in range(tk // LANES):
                acc_ref[s * LANES:(s + 1) * LANES, :] += _tn(a_ref[s].astype(BF16), bb)
        else:
            acc_ref[...] += _tn(a_ref[...].astype(BF16), bb)

        @pl.when(step == n_steps - 1)
        def _():
            o_ref[...] = acc_ref[...].astype(out_dtype)

    if slab:
        a_spec = pl.BlockSpec((tk // LANES, tm, LANES), lambda i, j, t: (i, t, 0))
    else:
        a_spec = pl.BlockSpec((tm, tk), lambda i, j, t: (t, i))
    return pl.pallas_call(
        body, name=name, out_shape=jax.ShapeDtypeStruct((k, n), out_dtype), grid=(k // tk, n // tn, n_steps),
        in_specs=[a_spec, pl.BlockSpec((tm, tn), lambda i, j, t: (t, j))],
        out_specs=pl.BlockSpec((tk, tn), lambda i, j, t: (i, j)),
        scratch_shapes=[pltpu.VMEM((tk, tn), F32)],
        compiler_params=_params("arbitrary", "arbitrary", "arbitrary"),
    )(a, b)


def _row_spec(d, k):
    return pl.BlockSpec((1, 1, d), lambda b, i: (6 * b + k, 0, 0))


def modulate(x, mod, k_shift, k_scale, bl, name):
    t, d = x.shape
    s = t // bl
    tm = _tile(s, 512, 8)
    nt = s // tm

    def body(x_ref, sh_ref, sc_ref, o_ref):
        o_ref[...] = (x_ref[...] * (1.0 + sc_ref[0]) + sh_ref[0]).astype(BF16)

    return pl.pallas_call(
        body, name=name, out_shape=jax.ShapeDtypeStruct((t, d), BF16), grid=(bl, nt),
        in_specs=[pl.BlockSpec((tm, d), lambda b, i: (b * nt + i, 0)), _row_spec(d, k_shift), _row_spec(d, k_scale)],
        out_specs=pl.BlockSpec((tm, d), lambda b, i: (b * nt + i, 0)),
        compiler_params=_params("arbitrary", "arbitrary"),
    )(x, mod, mod)


def _layer_norm_stats(r):
    mu = jnp.mean(r, axis=-1, keepdims=True)
    rc = r - mu
    var = jnp.mean(rc * rc, axis=-1, keepdims=True)
    rstd = lax.rsqrt(var + NORM_EPS)
    return rc * rstd, rstd


def residual_layer_norm(x, y, mod, k_gate, g, b, bl, name, next_mod=None):
    t, d = x.shape
    s = t // bl
    tm = _tile(s, 512, 8)
    nt = s // tm
    has_next = next_mod is not None

    def body(*refs):
        x_ref, y_ref, gt_ref, g_ref, b_ref = refs[:5]
        rest = refs[5:]
        if has_next:
            sh_ref, sc_ref, o_ref, r_ref, u_ref = rest
        else:
            o_ref, r_ref = rest
        r = ALPHA * x_ref[...] + (1.0 + gt_ref[0]) * y_ref[...]
        xhat, _ = _layer_norm_stats(r)
        out = xhat * g_ref[...] + b_ref[...]
        o_ref[...] = out
        r_ref[...] = r
        if has_next:
            u_ref[...] = (out * (1.0 + sc_ref[0]) + sh_ref[0]).astype(BF16)

    tok = pl.BlockSpec((tm, d), lambda bb, i: (bb * nt + i, 0))
    vec = pl.BlockSpec((1, d), lambda bb, i: (0, 0))
    in_specs = [tok, tok, _row_spec(d, k_gate), vec, vec]
    args = [x, y, mod, g, b]
    out_shape = [jax.ShapeDtypeStruct((t, d), F32), jax.ShapeDtypeStruct((t, d), F32)]
    out_specs = [tok, tok]
    if has_next:
        in_specs += [_row_spec(d, next_mod[0]), _row_spec(d, next_mod[1])]
        args += [mod if len(next_mod) == 2 else next_mod[2]] * 2
        out_shape.append(jax.ShapeDtypeStruct((t, d), BF16))
        out_specs.append(tok)
    return pl.pallas_call(
        body, name=name, out_shape=out_shape, grid=(bl, nt), in_specs=in_specs, out_specs=out_specs,
        compiler_params=_params("arbitrary", "arbitrary"),
    )(*args)


def loss_head(xo, target, name):
    t, d = xo.shape
    tm = _tile(t, 512, 8)

    def body(x_ref, t_ref, l_ref, dx_ref):
        @pl.when(pl.program_id(0) == 0)
        def _():
            l_ref[...] = jnp.zeros_like(l_ref)

        e = x_ref[...] - t_ref[...]
        l_ref[...] += jnp.sum(e * e, axis=0, keepdims=True) * (0.5 / d)
        dx_ref[...] = e * (1.0 / d)

    tok = pl.BlockSpec((tm, d), lambda i: (i, 0))
    return pl.pallas_call(
        body, name=name,
        out_shape=[jax.ShapeDtypeStruct((1, d), F32), jax.ShapeDtypeStruct((t, d), F32)],
        grid=(t // tm,), in_specs=[tok, tok],
        out_specs=[pl.BlockSpec((1, d), lambda i: (0, 0)), tok],
        compiler_params=_params("arbitrary"),
    )(xo, target)


def sublayer_backward(d_a, bl, name, *, du=None, scale=None, x_in=None, ln=None):
    t, d = d_a.shape
    s = t // bl
    tm = _tile(s, 512, 8)
    nt = s // tm
    has_mod = du is not None
    has_ln = ln is not None
    assert has_mod or has_ln
    assert has_ln or x_in is not None

    def body(*refs):
        refs = list(refs)
        da_ref = refs.pop(0)
        if has_mod:
            du_ref, sc_ref = refs.pop(0), refs.pop(0)
        if has_ln:
            r_ref, y_ref, g_ref, b_ref, gt_ref = (refs.pop(0) for _ in range(5))
        elif has_mod:
            xin_ref = refs.pop(0)
        dx_ref = refs.pop(0)
        if has_ln:
            dy_ref, dg_ref, db_ref, dgt_ref = (refs.pop(0) for _ in range(4))
        if has_mod:
            dsc_ref, dsh_ref = refs.pop(0), refs.pop(0)
        first_tile = pl.program_id(1) == 0
        first_step = jnp.logical_and(pl.program_id(0) == 0, first_tile)

        dout = da_ref[...]
        if has_ln:
            xhat, rstd = _layer_norm_stats(r_ref[...])
        if has_mod:
            duv = du_ref[...]
            dout = dout + duv * (1.0 + sc_ref[0])
            xin = xhat * g_ref[...] + b_ref[...] if has_ln else xin_ref[...]

            @pl.when(first_tile)
            def _():
                dsc_ref[...] = jnp.zeros_like(dsc_ref)
                dsh_ref[...] = jnp.zeros_like(dsh_ref)

            dsc_ref[0] += jnp.sum(duv * xin, axis=0, keepdims=True)
            dsh_ref[0] += jnp.sum(duv, axis=0, keepdims=True)
        if not has_ln:
            dx_ref[...] = dout
            return

        @pl.when(first_step)
        def _():
            dg_ref[...] = jnp.zeros_like(dg_ref)
            db_ref[...] = jnp.zeros_like(db_ref)

        @pl.when(first_tile)
        def _():
            dgt_ref[...] = jnp.zeros_like(dgt_ref)

        dg_ref[...] += jnp.sum(dout * xhat, axis=0, keepdims=True)
        db_ref[...] += jnp.sum(dout, axis=0, keepdims=True)
        dxh = dout * g_ref[...]
        dr = rstd * (dxh - jnp.mean(dxh, axis=-1, keepdims=True) - xhat * jnp.mean(dxh * xhat, axis=-1, keepdims=True))
        dx_ref[...] = ALPHA * dr
        dy_ref[...] = ((1.0 + gt_ref[0]) * dr).astype(BF16)
        dgt_ref[0] += jnp.sum(dr * y_ref[...], axis=0, keepdims=True)

    tok = pl.BlockSpec((tm, d), lambda bb, i: (bb * nt + i, 0))
    vec = pl.BlockSpec((1, d), lambda bb, i: (0, 0))
    seq = pl.BlockSpec((1, 1, d), lambda bb, i: (bb, 0, 0))
    in_specs, args = [tok], [d_a]
    if has_mod:
        in_specs += [tok, _row_spec(d, scale[1])]
        args += [du, scale[0]]
    if has_ln:
        r, y, g, b, gate = ln
        in_specs += [tok, tok, vec, vec, _row_spec(d, gate[1])]
        args += [r, y, g, b, gate[0]]
    elif has_mod:
        in_specs.append(tok)
        args.append(x_in)
    names = ["dx"]
    out_shape, out_specs = [jax.ShapeDtypeStruct((t, d), F32)], [tok]
    if has_ln:
        names += ["dy", "dg", "db", "dgate"]
        out_shape += [jax.ShapeDtypeStruct((t, d), BF16), jax.ShapeDtypeStruct((1, d), F32),
                      jax.ShapeDtypeStruct((1, d), F32), jax.ShapeDtypeStruct((bl, 1, d), F32)]
        out_specs += [tok, vec, vec, seq]
    if has_mod:
        names += ["dscale", "dshift"]
        out_shape += [jax.ShapeDtypeStruct((bl, 1, d), F32)] * 2
        out_specs += [seq, seq]
    outs = pl.pallas_call(
        body, name=name, out_shape=out_shape, grid=(bl, nt), in_specs=in_specs, out_specs=out_specs,
        compiler_params=_params("arbitrary", "arbitrary"),
    )(*args)
    return dict(zip(names, outs))


def _silu(a):
    return a * jax.nn.sigmoid(a)


def silu_rows(a, name):
    def body(a_ref, o_ref):
        o_ref[...] = _silu(a_ref[...]).astype(BF16)

    return pl.pallas_call(body, name=name, out_shape=jax.ShapeDtypeStruct(a.shape, BF16))(a)


def _swiglu_tiles(t, f):
    return _tile(t, 1024, 8), _tile(f, 1536)


def swiglu_in(u, wt_gate, wt_up, name):
    t, d = u.shape
    f = wt_gate.shape[0]
    tm, tf = _swiglu_tiles(t, f)

    def body(u_ref, g_ref, w_ref, a_ref, b_ref, h_ref):
        uv = u_ref[...]
        a = _nt(uv, g_ref[...])
        b = _nt(uv, w_ref[...])
        a_ref[...] = a.astype(BF16)
        b_ref[...] = b.astype(BF16)
        h_ref[...] = (_silu(a) * b).astype(BF16)

    w_spec = pl.BlockSpec((tf, d), lambda i, j: (j, 0))
    o_spec = pl.BlockSpec((tm, tf), lambda i, j: (i, j))
    return pl.pallas_call(
        body, name=name,
        out_shape=[jax.ShapeDtypeStruct((t, f), BF16)] * 3,
        grid=(t // tm, f // tf), in_specs=[pl.BlockSpec((tm, d), lambda i, j: (i, 0)), w_spec, w_spec],
        out_specs=[o_spec, o_spec, o_spec], compiler_params=_params("arbitrary", "arbitrary"),
    )(u, wt_gate, wt_up)


def swiglu_out_backward(dy, w_down, a, b, name):
    t, d = dy.shape
    f = w_down.shape[0]
    tm, tf = _swiglu_tiles(t, f)

    def body(dy_ref, w_ref, a_ref, b_ref, da_ref, db_ref):
        dh = _nt(dy_ref[...], w_ref[...])
        av = a_ref[...].astype(F32)
        sig = jax.nn.sigmoid(av)
        da_ref[...] = (dh * b_ref[...].astype(F32) * (sig * (1.0 + av * (1.0 - sig)))).astype(BF16)
        db_ref[...] = (dh * (av * sig)).astype(BF16)

    spec = pl.BlockSpec((tm, tf), lambda i, j: (i, j))
    return pl.pallas_call(
        body, name=name, out_shape=[jax.ShapeDtypeStruct((t, f), BF16)] * 2, grid=(t // tm, f // tf),
        in_specs=[pl.BlockSpec((tm, d), lambda i, j: (i, 0)), pl.BlockSpec((tf, d), lambda i, j: (j, 0)), spec, spec],
        out_specs=[spec, spec], compiler_params=_params("arbitrary", "arbitrary"),
    )(dy, w_down, a, b)


def rope_tables(pos, inv_freq, sign, name):
    t = pos.shape[0]
    tm = _tile(t, 512, 8)

    def body(p_ref, f_ref, s_ref, c_out, s_out):
        ang = p_ref[...] * f_ref[...]
        c_out[...] = jnp.cos(ang)
        s_out[...] = jnp.sin(ang) * s_ref[...]

    vec = pl.BlockSpec((1, LANES), lambda i: (0, 0))
    tab = pl.BlockSpec((tm, LANES), lambda i: (i, 0))
    return pl.pallas_call(
        body, name=name, out_shape=[jax.ShapeDtypeStruct((t, LANES), F32)] * 2, grid=(t // tm,),
        in_specs=[pl.BlockSpec((tm, 1), lambda i: (i, 0)), vec, vec], out_specs=[tab, tab],
        compiler_params=_params("arbitrary"),
    )(pos, inv_freq, sign)


def _rot_half(v):
    lane = lax.broadcasted_iota(jnp.int32, v.shape, v.ndim - 1)
    up = pltpu.roll(v, LANES - MLA_ROPE // 2, v.ndim - 1)
    down = pltpu.roll(v, MLA_ROPE // 2, v.ndim - 1)
    return jnp.where(lane % MLA_ROPE < MLA_ROPE // 2, up, down)


def _rope(v, cos, sin_signed):
    return v * cos + _rot_half(v) * sin_signed


def _rope_transposed(dv, cos, sin_signed):
    return dv * cos + _rot_half(dv * sin_signed)


def rope_slabs(v, cos, sin_signed, out_dtype, name, transposed=False):
    ns, t, _ = v.shape
    tm = _tile(t, 1024, 8)
    fn = _rope_transposed if transposed else _rope

    def body(v_ref, c_ref, s_ref, o_ref):
        for j in range(ns):
            o_ref[j] = fn(v_ref[j].astype(F32), c_ref[...], s_ref[...]).astype(out_dtype)

    tab = pl.BlockSpec((tm, LANES), lambda i: (i, 0))
    spec = pl.BlockSpec((ns, tm, LANES), lambda i: (0, i, 0))
    return pl.pallas_call(
        body, name=name, out_shape=jax.ShapeDtypeStruct(v.shape, out_dtype), grid=(t // tm,),
        in_specs=[spec, tab, tab], out_specs=spec, compiler_params=_params("arbitrary"),
    )(v, cos, sin_signed)


def _rms(x):
    rinv = lax.rsqrt(jnp.mean(x * x, axis=-1, keepdims=True) + NORM_EPS)
    return x * rinv, rinv


def mla_latents_forward(h_in, g_q, g_kv, cos, sin_signed, name):
    t = h_in.shape[0]
    tm = _tile(t, 512, 8)

    def body(h_ref, gq_ref, gkv_ref, c_ref, s_ref, cq_ref, ckv_ref, kr_ref):
        cq_ref[...] = (_rms(h_ref[:, 0:MLA_QR])[0] * gq_ref[...]).astype(BF16)
        ckv_ref[...] = (_rms(h_ref[:, MLA_QR:MLA_QR + MLA_KVR])[0] * gkv_ref[...]).astype(BF16)
        kr_ref[...] = _rope(h_ref[:, MLA_QR + MLA_KVR:], c_ref[...], s_ref[...]).astype(BF16)

    def tok(w):
        return pl.BlockSpec((tm, w), lambda i: (i, 0))

    def vec(w):
        return pl.BlockSpec((1, w), lambda i: (0, 0))

    return pl.pallas_call(
        body, name=name,
        out_shape=[jax.ShapeDtypeStruct((t, MLA_QR), BF16), jax.ShapeDtypeStruct((t, MLA_KVR), BF16),
                   jax.ShapeDtypeStruct((t, LANES), BF16)],
        grid=(t // tm,),
        in_specs=[tok(h_in.shape[1]), vec(MLA_QR), vec(MLA_KVR), tok(LANES), tok(LANES)],
        out_specs=[tok(MLA_QR), tok(MLA_KVR), tok(LANES)],
        compiler_params=_params("arbitrary"),
    )(h_in, g_q, g_kv, cos, sin_signed)


def mla_latents_backward(h_in, dcq, dckv, dkr, g_q, g_kv, cos, sin_signed, name):
    t, w = h_in.shape
    tm = _tile(t, 512, 8)

    def body(h_ref, dcq_ref, dckv_ref, dkr_ref, gq_ref, gkv_ref, c_ref, s_ref, dh_ref, dgq_ref, dgkv_ref):
        @pl.when(pl.program_id(0) == 0)
        def _():
            dgq_ref[...] = jnp.zeros_like(dgq_ref)
            dgkv_ref[...] = jnp.zeros_like(dgkv_ref)

        def rms_bwd(x, dc, g_ref, dg_ref):
            xn, rinv = _rms(x)
            dg_ref[...] += jnp.sum(dc * xn, axis=0, keepdims=True)
            dxn = dc * g_ref[...]
            return rinv * (dxn - xn * jnp.mean(dxn * xn, axis=-1, keepdims=True))

        dq = rms_bwd(h_ref[:, 0:MLA_QR], dcq_ref[...], gq_ref, dgq_ref)
        dkv = rms_bwd(h_ref[:, MLA_QR:MLA_QR + MLA_KVR], dckv_ref[...], gkv_ref, dgkv_ref)
        dr = _rope_transposed(dkr_ref[...], c_ref[...], s_ref[...])
        dh_ref[...] = jnp.concatenate([dq, dkv, dr], axis=1).astype(BF16)

    def tok(ww):
        return pl.BlockSpec((tm, ww), lambda i: (i, 0))

    def vec(ww):
        return pl.BlockSpec((1, ww), lambda i: (0, 0))

    return pl.pallas_call(
        body, name=name,
        out_shape=[jax.ShapeDtypeStruct((t, w), BF16), jax.ShapeDtypeStruct((1, MLA_QR), F32),
                   jax.ShapeDtypeStruct((1, MLA_KVR), F32)],
        grid=(t // tm,),
        in_specs=[tok(w), tok(MLA_QR), tok(MLA_KVR), tok(LANES), vec(MLA_QR), vec(MLA_KVR), tok(LANES), tok(LANES)],
        out_specs=[tok(w), vec(MLA_QR), vec(MLA_KVR)],
        compiler_params=_params("arbitrary"),
    )(h_in, dcq, dckv, dkr, g_q, g_kv, cos, sin_signed)


def _tri(n, lower):
    r = lax.broadcasted_iota(jnp.int32, (n, n), 0)
    c = lax.broadcasted_iota(jnp.int32, (n, n), 1)
    return jnp.where(r >= c if lower else r <= c, 1.0, 0.0).astype(F32)


def _dot_exact(tri, v):
    hi = v.astype(BF16)
    mid = (v - hi.astype(F32)).astype(BF16)
    lo = (v - hi.astype(F32) - mid.astype(F32)).astype(BF16)
    t = tri.astype(BF16)
    return _nn(t, hi) + _nn(t, mid) + _nn(t, lo)


def fox_gate_forward(z, b_f, bl, name):
    t = z.shape[0]
    s = t // bl
    ch = LANES
    n_ch = s // ch

    def body(z_ref, b_ref, f_ref, fs_ref):
        tri = _tri(ch, True)
        carry = jnp.zeros((1, LANES), F32)
        for k in range(n_ch):
            x = z_ref[k * ch:(k + 1) * ch, :] + b_ref[...]
            logf = jnp.minimum(x, 0.0) - jnp.log(1.0 + jnp.exp(-jnp.abs(x)))
            cs = _dot_exact(tri, logf) + carry
            carry = cs[ch - 1:ch, :]
            f_ref[k * ch:(k + 1) * ch, :] = cs
            for h in range(FOX_HEADS):
                fs_ref[h, k * ch:(k + 1) * ch, :] = jnp.broadcast_to(cs[:, h:h + 1], (ch, LANES))

    return pl.pallas_call(
        body, name=name,
        out_shape=[jax.ShapeDtypeStruct((t, LANES), F32), jax.ShapeDtypeStruct((FOX_HEADS, t, LANES), F32)],
        grid=(bl,),
        in_specs=[pl.BlockSpec((s, LANES), lambda b: (b, 0)), pl.BlockSpec((1, LANES), lambda b: (0, 0))],
        out_specs=[pl.BlockSpec((s, LANES), lambda b: (b, 0)),
                   pl.BlockSpec((FOX_HEADS, s, LANES), lambda b: (0, b, 0))],
        compiler_params=_params("arbitrary"),
    )(z, b_f)


def fox_gate_backward(z, b_f, df, bl, name):
    t = z.shape[0]
    s = t // bl
    ch = LANES
    n_ch = s // ch

    def body(z_ref, b_ref, df_ref, dz_ref, db_ref):
        @pl.when(pl.program_id(0) == 0)
        def _():
            db_ref[...] = jnp.zeros_like(db_ref)

        tri = _tri(ch, False)
        carry = jnp.zeros((1, LANES), F32)
        for k in reversed(range(n_ch)):
            cs = _dot_exact(tri, df_ref[k * ch:(k + 1) * ch, :]) + carry
            carry = cs[0:1, :]
            x = z_ref[k * ch:(k + 1) * ch, :] + b_ref[...]
            dz = cs * (1.0 - jax.nn.sigmoid(x))
            dz_ref[k * ch:(k + 1) * ch, :] = dz
            db_ref[...] += jnp.sum(dz, axis=0, keepdims=True)

    tok = pl.BlockSpec((s, LANES), lambda b: (b, 0))
    vec = pl.BlockSpec((1, LANES), lambda b: (0, 0))
    return pl.pallas_call(
        body, name=name,
        out_shape=[jax.ShapeDtypeStruct((t, LANES), F32), jax.ShapeDtypeStruct((1, LANES), F32)],
        grid=(bl,), in_specs=[tok, vec, tok], out_specs=[tok, vec],
        compiler_params=_params("arbitrary"),
    )(z, b_f, df)


NEG_INF = float("-inf")


def _attn_tiles(s):
    return _tile(s, 1024, 8)


def attention_forward(kind, ops, bl, scale, name):
    fox = kind == "fox"
    if fox:
        assert math.frexp(scale)[0] == 0.5, "the FoX scale is folded into bf16 queries: it must be a power of two"
        qkv, fq, fk = ops
        t = qkv.shape[1]
        n_pair = FOX_HEADS // 2
    else:
        qn, qr, kn, kr, v = ops
        t = qn.shape[1]
        n_pair = MLA_HEADS // 2
    s = t // bl
    tq = _attn_tiles(s)
    nq = s // tq
    half = LANES // 2

    def body(*refs):
        if fox:
            q_ref, k_ref, v_ref, fq_ref, fk_ref, o_ref, lse_ref, o32_ref = refs
        else:
            qn_ref, qr_ref, kn_ref, kr_ref, v_ref, o_ref, lse_ref = refs
        i = pl.program_id(2)
        row = lax.broadcasted_iota(jnp.int32, (tq, tq), 0)
        col = lax.broadcasted_iota(jnp.int32, (tq, tq), 1)
        heads = []
        for e in range(2):
            sl = slice(e * half, (e + 1) * half)
            if fox:
                heads.append((sl, q_ref[0, :, sl] * jnp.asarray(scale, BF16), None))
            else:
                heads.append((sl, jnp.concatenate([qn_ref[e], qr_ref[0, :, sl], jnp.zeros((tq, half), BF16)], axis=1),
                              None))
        dv = half if fox else LANES

        def wide(stat):
            return jnp.concatenate([stat] * (tq // LANES), axis=1)

        def step(j, carry, masked):
            rows = pl.ds(pl.multiple_of(j * tq, tq), tq)
            new = []
            for e, (sl, qa, qb) in enumerate(heads):
                m, l, acc = carry[e]
                if fox:
                    sc = _nt(qa, k_ref[0, rows, sl]) + wide(fq_ref[e]) - fk_ref[0, j, e:e + 1, :]
                    vv = v_ref[0, rows, sl]
                else:
                    k_cat = jnp.concatenate([kn_ref[e, rows, :], kr_ref[rows, :]], axis=1)
                    sc = _nt(qa, k_cat) * scale
                    vv = v_ref[e, rows, :]
                if masked:
                    sc = jnp.where(row >= col, sc, NEG_INF)
                m_new = jnp.maximum(m, jnp.max(sc, axis=1, keepdims=True))
                p = jnp.exp(sc - m_new)
                a = jnp.exp(m - m_new)
                p_hi = p.astype(BF16)
                if fox:
                    vv = jnp.concatenate([vv, ones], axis=1)
                    acc = a * acc + _nn(p_hi, vv) + _nn((p - p_hi.astype(F32)).astype(BF16), vv)
                else:
                    l = a * l + jnp.sum(p, axis=1, keepdims=True)
                    acc = a * acc + _nn(p_hi, vv)
                new.append((m_new, l, acc))
            return tuple(new)

        ones = jnp.ones((tq, half), BF16)
        acc_w = LANES if fox else dv
        init = (jnp.full((tq, 1), NEG_INF, F32), jnp.zeros((tq, 1), F32), jnp.zeros((tq, acc_w), F32))
        carry = step(i, (init, init), True)
        carry = lax.fori_loop(0, i, lambda j, c: step(j, c, False), carry)
        if fox:
            carry = [(m, acc[:, dv:dv + 1], acc[:, :dv]) for m, _, acc in carry]
        outs = [acc / l for _, l, acc in carry]
        for e, (m, l, _) in enumerate(carry):
            lse_ref[e] = jnp.broadcast_to(m + jnp.log(l), (tq, LANES))
        if fox:
            o32 = jnp.concatenate(outs, axis=1)
            o32_ref[0] = o32
            o_ref[0] = o32.astype(BF16)
        else:
            o_ref[0] = outs[0].astype(BF16)
            o_ref[1] = outs[1].astype(BF16)

    def q_idx(b, g, i):
        return (g, b * nq + i, 0)

    if fox:
        nk = fk.shape[1]
        in_specs = [pl.BlockSpec((1, tq, LANES), q_idx),
                    pl.BlockSpec((1, s, LANES), lambda b, g, i: (n_pair + g, b, 0)),
                    pl.BlockSpec((1, s, LANES), lambda b, g, i: (2 * n_pair + g, b, 0)),
                    pl.BlockSpec((2, tq, LANES), q_idx),
                    pl.BlockSpec((1, nk, 8, tq), lambda b, g, i: (b * n_pair + g, 0, 0, 0))]
        args = [qkv, qkv, qkv, fq, fk]
        o_spec = pl.BlockSpec((1, tq, LANES), q_idx)
    else:
        in_specs = [pl.BlockSpec((2, tq, LANES), q_idx),
                    pl.BlockSpec((1, tq, LANES), q_idx),
                    pl.BlockSpec((2, s, LANES), lambda b, g, i: (g, b, 0)),
                    pl.BlockSpec((s, LANES), lambda b, g, i: (b, 0)),
                    pl.BlockSpec((2, s, LANES), lambda b, g, i: (g, b, 0))]
        args = [qn, qr, kn, kr, v]
        o_spec = pl.BlockSpec((2, tq, LANES), q_idx)
    out_shape = [jax.ShapeDtypeStruct((8, t, LANES), BF16), jax.ShapeDtypeStruct((2 * n_pair, t, LANES), F32)]
    out_specs = [o_spec, pl.BlockSpec((2, tq, LANES), q_idx)]
    if fox:
        out_shape.append(jax.ShapeDtypeStruct((8, t, LANES), F32))
        out_specs.append(o_spec)
    outs = pl.pallas_call(
        body, name=name, out_shape=out_shape, grid=(bl, n_pair, nq), in_specs=in_specs, out_specs=out_specs,
        compiler_params=_params("arbitrary", "arbitrary", "arbitrary"),
    )(*args)
    return (outs[0], outs[1], outs[2] if fox else outs[0])


def attention_backward(kind, ops, o, do, lse, bl, scale, name):
    fox = kind == "fox"
    if fox:
        assert math.frexp(scale)[0] == 0.5, "the FoX scale is folded into bf16 queries: it must be a power of two"
        qkv, fq, fk = ops
        t = qkv.shape[1]
        n_pair = FOX_HEADS // 2
    else:
        qn, qr, kn, kr, v = ops
        t = qn.shape[1]
        n_pair = MLA_HEADS // 2
    s = t // bl
    tq = _attn_tiles(s)
    nq = s // tq
    half = LANES // 2

    def body(*refs):
        if fox:
            (q_ref, k_ref, v_ref, fq_ref, fk_ref, o_ref, do_ref, lse_ref,
             dq_ref, dk_ref, dv_ref, dfk_ref, delta_scr, qt_scr, dot_scr) = refs
        else:
            (qn_ref, qr_ref, kn_ref, kr_ref, v_ref, o_ref, do_ref, lse_ref,
             dqn_ref, dqr_ref, dkn_ref, dv_ref, dkr_ref, delta_scr, qt_scr, qrt_scr, dot_scr) = refs
        g, j = pl.program_id(1), pl.program_id(2)
        row = lax.broadcasted_iota(jnp.int32, (tq, tq), 0)
        col = lax.broadcasted_iota(jnp.int32, (tq, tq), 1)
        krows = pl.ds(pl.multiple_of(j * tq, tq), tq)
        q_scale = jnp.asarray(scale, BF16)

        def transposed(v):
            return v.astype(F32).T.astype(BF16)

        def wide(stat):
            return jnp.concatenate([stat] * (tq // LANES), axis=1)

        @pl.when(j == 0)
        def _():
            if fox:
                dq_ref[...] = jnp.zeros_like(dq_ref)
            else:
                dqn_ref[...] = jnp.zeros_like(dqn_ref)
                dqr_ref[...] = jnp.zeros_like(dqr_ref)
            for ii in range(nq):
                rws = slice(ii * tq, (ii + 1) * tq)
                deltas = []
                if fox:
                    prod = do_ref[0, rws, :].astype(F32) * o_ref[0, rws, :].astype(F32)
                    for e in range(2):
                        deltas.append(jnp.sum(prod[:, e * half:(e + 1) * half], axis=1, keepdims=True))
                    qt_scr[ii] = transposed(q_ref[0, rws, :] * q_scale)
                    dot_scr[ii] = transposed(do_ref[0, rws, :])
                else:
                    for e in range(2):
                        prod = do_ref[e, rws, :].astype(F32) * o_ref[e, rws, :].astype(F32)
                        deltas.append(jnp.sum(prod, axis=1, keepdims=True))
                        qt_scr[e, ii] = transposed(qn_ref[e, rws, :])
                        dot_scr[e, ii] = transposed(do_ref[e, rws, :])
                    qrt_scr[ii] = transposed(qr_ref[0, rws, :])
                for e in range(2):
                    delta_scr[e, rws, :] = jnp.broadcast_to(deltas[e], (tq, LANES))

        if fox:
            dfk_ref[...] = jnp.zeros_like(dfk_ref)
        else:
            @pl.when(jnp.logical_and(g == 0, j == 0))
            def _():
                dkr_ref[...] = jnp.zeros_like(dkr_ref)

        heads = []
        for e in range(2):
            sl = slice(e * half, (e + 1) * half)
            if fox:
                heads.append((sl, k_ref[0, :, sl], v_ref[0, :, sl], fk_ref[0, 0, e:e + 1, :]))
            else:
                heads.append((sl, jnp.concatenate([kn_ref[e], kr_ref[krows, :]], axis=1), v_ref[e], None))
        dk_w = dv_w = half if fox else LANES

        def step(i, carry, masked):
            rows = pl.ds(pl.multiple_of(i * tq, tq), tq)
            new = []
            for e, (sl, k_e, v_e, x_e) in enumerate(heads):
                dk_acc, dv_acc, last = carry[e]
                if fox:
                    do_i = do_ref[0, rows, sl]
                    sc = _nt(q_ref[0, rows, sl] * q_scale, k_e) + wide(fq_ref[e, rows, :]) - x_e
                else:
                    do_i = do_ref[e, rows, :]
                    q_cat = jnp.concatenate([qn_ref[e, rows, :], qr_ref[0, rows, sl], jnp.zeros((tq, half), BF16)], axis=1)
                    sc = _nt(q_cat, k_e) * scale
                if masked:
                    sc = jnp.where(row >= col, sc, NEG_INF)
                p = jnp.exp(sc - wide(lse_ref[e, rows, :]))
                dp = _nt(do_i, v_e)
                ds = p * (dp - wide(delta_scr[e, rows, :]))
                dsb = ds.astype(BF16) if fox else (ds * scale).astype(BF16)
                if fox:
                    fsl = slice(e * half, (e + 1) * half)
                    dv_acc = dv_acc + _nn(dot_scr[i, fsl, :], p.astype(BF16))
                    dk_acc = dk_acc + _nn(qt_scr[i, fsl, :], dsb)
                    dq_ref[0, rows, sl] += _nn(dsb, k_e) * scale
                    last = last - jnp.sum(ds, axis=0, keepdims=True)
                else:
                    dv_acc = dv_acc + _nn(dot_scr[e, i], p.astype(BF16))
                    dk_acc = dk_acc + _nn(qt_scr[e, i], dsb)
                    dq_cat = _nn(dsb, k_e)
                    dqn_ref[e, rows, :] += dq_cat[:, :LANES]
                    dqr_ref[0, rows, sl] += dq_cat[:, LANES:LANES + half]
                    last = last + _nn(qrt_scr[i, e * half:(e + 1) * half, :], dsb)
                new.append((dk_acc, dv_acc, last))
            return tuple(new)

        last0 = jnp.zeros((1, tq), F32) if fox else jnp.zeros((half, tq), F32)
        init = (jnp.zeros((dk_w, tq), F32), jnp.zeros((dv_w, tq), F32), last0)
        carry = step(j, (init, init), True)
        carry = lax.fori_loop(j + 1, nq, lambda i, c: step(i, c, False), carry)
        if fox:
            for e in range(2):
                dfk_ref[0, 0, e:e + 1, :] = carry[e][2]
            dk_ref[0] = jnp.concatenate([carry[0][0], carry[1][0]], axis=0).T.astype(BF16)
            dv_ref[0] = jnp.concatenate([carry[0][1], carry[1][1]], axis=0).T.astype(BF16)
        else:
            for e in range(2):
                dkn_ref[e] = carry[e][0].T.astype(BF16)
                dv_ref[e] = carry[e][1].T.astype(BF16)
            dkr_t = carry[0][2] + carry[1][2]
            dkr_ref[krows, :] += jnp.concatenate([dkr_t, jnp.zeros_like(dkr_t)], axis=0).T

    def whole(b, g, j):
        return (g, b, 0)

    def kblk(b, g, j):
        return (g, b * nq + j, 0)

    if fox:
        in_specs = [pl.BlockSpec((1, s, LANES), whole),
                    pl.BlockSpec((1, tq, LANES), lambda b, g, j: (n_pair + g, b * nq + j, 0)),
                    pl.BlockSpec((1, tq, LANES), lambda b, g, j: (2 * n_pair + g, b * nq + j, 0)),
                    pl.BlockSpec((2, s, LANES), whole),
                    pl.BlockSpec((1, 1, 8, tq), lambda b, g, j: (b * n_pair + g, j, 0, 0)),
                    pl.BlockSpec((1, s, LANES), whole), pl.BlockSpec((1, s, LANES), whole),
                    pl.BlockSpec((2, s, LANES), whole)]
        args = [qkv, qkv, qkv, fq, fk, o, do, lse]
        out_shape = [jax.ShapeDtypeStruct((8, t, LANES), F32), jax.ShapeDtypeStruct((8, t, LANES), BF16),
                     jax.ShapeDtypeStruct((8, t, LANES), BF16), jax.ShapeDtypeStruct(fk.shape, F32)]
        out_specs = [pl.BlockSpec((1, s, LANES), whole), pl.BlockSpec((1, tq, LANES), kblk),
                     pl.BlockSpec((1, tq, LANES), kblk),
                     pl.BlockSpec((1, 1, 8, tq), lambda b, g, j: (b * n_pair + g, j, 0, 0))]
    else:
        pair = pl.BlockSpec((2, s, LANES), whole)
        pair_k = pl.BlockSpec((2, tq, LANES), kblk)
        in_specs = [pair, pl.BlockSpec((1, s, LANES), whole), pair_k,
                    pl.BlockSpec((s, LANES), lambda b, g, j: (b, 0)), pair_k,
                    pair, pair, pair]
        args = [qn, qr, kn, kr, v, o, do, lse]
        out_shape = [jax.ShapeDtypeStruct((8, t, LANES), F32), jax.ShapeDtypeStruct((4, t, LANES), F32),
                     jax.ShapeDtypeStruct((8, t, LANES), BF16), jax.ShapeDtypeStruct((8, t, LANES), BF16),
                     jax.ShapeDtypeStruct((t, LANES), F32)]
        out_specs = [pair, pl.BlockSpec((1, s, LANES), whole), pair_k, pair_k,
                     pl.BlockSpec((s, LANES), lambda b, g, j: (b, 0))]
    t_blocks = pltpu.VMEM((nq, LANES, tq), BF16)
    t_pairs = pltpu.VMEM((2, nq, LANES, tq), BF16)
    scratch = [pltpu.VMEM((2, s, LANES), F32)] + ([t_blocks, t_blocks] if fox else [t_pairs, t_blocks, t_pairs])
    return pl.pallas_call(
        body, name=name, out_shape=out_shape, grid=(bl, n_pair, nq), in_specs=in_specs, out_specs=out_specs,
        scratch_shapes=scratch, compiler_params=_params("arbitrary", "arbitrary", "arbitrary"),
    )(*args)


def adamw(w, g, m, v, name):
    shape = w.shape
    c = shape[-1]
    r = w.size // c
    tr = _tile(r, 512, 8)

    def body(w_ref, g_ref, m_ref, v_ref, d_ref, nm_ref, nv_ref):
        gv = g_ref[...]
        m2 = ADAM_B1 * m_ref[...] + (1.0 - ADAM_B1) * gv
        v2 = ADAM_B2 * v_ref[...] + (1.0 - ADAM_B2) * (gv * gv)
        m_hat = m2 / (1.0 - ADAM_B1 ** ADAM_STEP)
        v_hat = v2 / (1.0 - ADAM_B2 ** ADAM_STEP)
        d_ref[...] = -ADAM_LR * (m_hat / (jnp.sqrt(v_hat) + ADAM_EPS) + ADAM_WD * w_ref[...])
        nm_ref[...] = m2
        nv_ref[...] = v2

    spec = pl.BlockSpec((tr, c), lambda i: (i, 0))
    outs = pl.pallas_call(
        body, name=name, out_shape=[jax.ShapeDtypeStruct((r, c), F32)] * 3, grid=(r // tr,),
        in_specs=[spec] * 4, out_specs=[spec] * 3, compiler_params=_params("arbitrary"),
    )(*(a.reshape(r, c) for a in (w, g, m, v)))
    return tuple(a.reshape(shape) for a in outs)


PACK_COLS = 1024


def _pack_rows(a):
    return a.reshape(-1, PACK_COLS)


def kernel(x, c, positions, mla_w_in, mla_g_q, mla_w_uq, mla_g_kv, mla_w_uk, mla_w_uv, mla_w_o, fox_w_in, fox_b_f, fox_w_o, ada_w, ada_b, ffn_w_gate, ffn_w_up, ffn_w_down, ln_g, ln_b, loss_target, m_mla_w_in, m_mla_g_q, m_mla_w_uq, m_mla_g_kv, m_mla_w_uk, m_mla_w_uv, m_mla_w_o, m_fox_w_in, m_fox_b_f, m_fox_w_o, m_ada_w, m_ada_b, m_ffn_w_gate, m_ffn_w_up, m_ffn_w_down, m_ln_g, m_ln_b, v_mla_w_in, v_mla_g_q, v_mla_w_uq, v_mla_g_kv, v_mla_w_uk, v_mla_w_uv, v_mla_w_o, v_fox_w_in, v_fox_b_f, v_fox_w_o, v_ada_w, v_ada_b, v_ffn_w_gate, v_ffn_w_up, v_ffn_w_down, v_ln_g, v_ln_b):
    bl, s, d = x.shape
    t = bl * s
    ff = ffn_w_gate.shape[-1] * N_DEV
    dev = 4 * lax.axis_index("x") + 2 * lax.axis_index("y") + lax.axis_index("c")
    ada_cols = ada_w.shape[-1]
    mla_in = mla_w_in.shape[-1]
    mla_in_pad = mla_in + (-mla_in) % LANES

    def t_last(a):
        return jnp.swapaxes(a, -1, -2)

    local = {
        "mla_w_in": mla_w_in[0],
        "mla_w_uq": t_last(mla_w_uq[0]),
        "mla_w_uk": t_last(mla_w_uk[0]),
        "mla_w_uv": t_last(mla_w_uv[0]),
        "mla_w_o": mla_w_o[0],
        "fox_w_in": t_last(fox_w_in[0]),
        "fox_w_o": fox_w_o[0],
    }
    for i in range(DEPTH):
        local.update({f"gate{i}": t_last(ffn_w_gate[i]), f"up{i}": t_last(ffn_w_up[i]), f"down{i}": ffn_w_down[i]})
    groups = [["mla_w_in", "mla_w_uq", "mla_w_uk", "mla_w_uv", "mla_w_o"],
              ["gate0", "up0", "down0"],
              ["fox_w_in", "fox_w_o"],
              ["gate1", "up1", "down1"]]
    offsets, rows_of, slot_of, group_of = {}, {}, {}, {}
    group_rows = []
    for gi, names in enumerate(groups):
        rows = 0
        for nm in names:
            rows_of[nm] = local[nm].size // PACK_COLS
            slot_of[nm] = rows_of[nm] + (-rows_of[nm]) % 16
            offsets[nm] = rows
            group_of[nm] = gi
            rows += slot_of[nm]
        group_rows.append(rows)

    def slot(nm, rows):
        pad = [(0, 0)] * rows.ndim
        pad[-2] = (0, slot_of[nm] - rows_of[nm])
        return jnp.pad(rows, pad)

    def held_until(block, arrays):
        zero = sum((a.reshape(-1)[0] * 0).astype(F32) for a in jax.tree.leaves(arrays))
        return block + zero.astype(block.dtype)

    def landing(block):
        land = lax.empty((N_DEV,) + block.shape, block.dtype)
        return lax.dynamic_update_slice(land, block[None], (dev, 0, 0))

    packed0 = jnp.concatenate([slot(nm, _pack_rows(local[nm]).astype(BF16)) for nm in groups[0]], axis=0)
    gathered0 = all_gather(packed0, "gather_mla_weights")
    gathered = {nm: gathered0[:, offsets[nm]:offsets[nm] + rows_of[nm], :] for nm in groups[0]}
    gather_started = [None] * len(groups)

    def depart(gi, after):
        blocks = [held_until(_pack_rows(local[nm]).astype(BF16), after) for nm in groups[gi]]
        gather_started[gi] = exchange_start(blocks, [landing(b) for b in blocks], f"gather_group{gi}_start", False)
        return gather_started[gi][4]

    def full(nm, cols):
        return gathered[nm].reshape(-1, cols)

    w_in = jnp.pad(full("mla_w_in", mla_in), ((0, 0), (0, mla_in_pad - mla_in)))
    wt_uq = full("mla_w_uq", MLA_QR).reshape(MLA_HEADS, MLA_NOPE + MLA_ROPE, MLA_QR)
    wt_uq_n = wt_uq[:, :MLA_NOPE].reshape(MLA_HEADS * MLA_NOPE, MLA_QR)
    wt_uq_r = wt_uq[:, MLA_NOPE:].reshape(MLA_HEADS * MLA_ROPE, MLA_QR)
    wt_uk = full("mla_w_uk", MLA_KVR)
    wt_uv = full("mla_w_uv", MLA_KVR)
    w_mo = full("mla_w_o", d)
    wt_gate, wt_up, w_down = [None] * DEPTH, [None] * DEPTH, [None] * DEPTH

    def arrive(gi, after):
        if gi + 1 < len(groups):
            after = depart(gi + 1, after)
        landed = list(exchange_wait(gather_started[gi], after, f"gather_group{gi}_wait", False))
        gathered.update(zip(groups[gi], landed))
        for i in range(DEPTH):
            if group_of[f"gate{i}"] == gi:
                wt_gate[i], wt_up[i], w_down[i] = full(f"gate{i}", d), full(f"up{i}", d), full(f"down{i}", d)

    small = jnp.concatenate([c.reshape(-1, LANES), ln_g.reshape(-1, LANES), ln_b.reshape(-1, LANES)], axis=0)
    small_rows = small.shape[0]
    small = jnp.pad(small, ((0, (-small_rows) % 8), (0, 0)))
    small_all = all_gather(small, "gather_small")
    c_rows = bl * d // LANES
    c_all = small_all[:, :c_rows].reshape(N_DEV * bl, d)
    n_ln = DEPTH * 2
    ln_g_all = small_all[:, c_rows:c_rows + n_ln, :].transpose(1, 0, 2).reshape(DEPTH, 2, 1, d)
    ln_b_all = small_all[:, c_rows + n_ln:c_rows + 2 * n_ln, :].transpose(1, 0, 2).reshape(DEPTH, 2, 1, d)

    c_act = silu_rows(c_all, "silu_c")
    ada_b_loc = lax.dynamic_slice_in_dim(ada_b, dev * ada_cols, ada_cols, axis=1)
    mod_cols = [mm([(c_act, ada_w[i])], trans_b=False, out_dtype=F32, name=f"ada_fwd{i}", bias=ada_b_loc[i][None, :])
                for i in range(DEPTH)]
    mod_all = all_gather(jnp.concatenate(mod_cols, axis=0), "gather_mod")
    mod_all = mod_all.reshape(N_DEV, DEPTH, N_DEV * bl, ada_cols).transpose(1, 2, 0, 3).reshape(DEPTH, N_DEV * bl, 6 * d)
    mod_mine = lax.dynamic_slice_in_dim(mod_all, dev * bl, bl, axis=1)
    mods = [mod_mine[i].reshape(bl * 6, 1, d) for i in range(DEPTH)]
    mods[0] = mods[0] + depart(1, (mod_mine, gathered0))[0, 0]

    half_r = MLA_ROPE // 2
    inv_freq = ROPE_THETA ** (-jnp.arange(half_r, dtype=F32) / half_r)
    inv_freq = jnp.tile(inv_freq, LANES // half_r)[None, :]
    sign = jnp.tile(jnp.concatenate([-jnp.ones((half_r,), F32), jnp.ones((half_r,), F32)]), LANES // MLA_ROPE)[None, :]
    cos_t, sin_t = rope_tables(positions.astype(F32).reshape(t, 1), inv_freq, sign, "rope_tables")

    x2d = x.reshape(t, d)
    g_q, g_kv = mla_g_q.reshape(1, MLA_QR), mla_g_kv.reshape(1, MLA_KVR)
    b_f = jnp.pad(fox_b_f.reshape(1, FOX_HEADS), ((0, 0), (0, LANES - FOX_HEADS)))
    mla_scale = (MLA_NOPE + MLA_ROPE) ** -0.5
    fox_scale = FOX_HD ** -0.5
    tq = _attn_tiles(s)
    nk = s // tq

    saved = []
    u = modulate(x2d, mods[0], 0, 1, bl, "modulate0")
    xin = x2d
    for i in range(DEPTH):
        sv = {"u": u, "x_in": xin}
        if i % 2 == 0:
            h_in = mm([(u, w_in)], trans_b=False, out_dtype=F32, name=f"mla_in{i}")
            c_q, c_kv, k_r = mla_latents_forward(h_in, g_q, g_kv, cos_t, sin_t, f"mla_latents{i}")
            q_n = mm([(c_q, wt_uq_n)], trans_b=True, out_dtype=BF16, out_slab=True, name=f"mla_qn{i}")
            q_r_raw = mm([(c_q, wt_uq_r)], trans_b=True, out_dtype=F32, out_slab=True, name=f"mla_qr{i}")
            q_r = rope_slabs(q_r_raw, cos_t, sin_t, BF16, f"mla_qrope{i}")
            k_n = mm([(c_kv, wt_uk)], trans_b=True, out_dtype=BF16, out_slab=True, name=f"mla_kn{i}")
            v_m = mm([(c_kv, wt_uv)], trans_b=True, out_dtype=BF16, out_slab=True, name=f"mla_v{i}")
            ops = (q_n, q_r, k_n, k_r, v_m)
            o, lse, o_delta = attention_forward("mla", ops, bl, mla_scale, f"mla_attn{i}")
            y = mm([(o, w_mo)], trans_b=False, out_dtype=F32, name=f"mla_out{i}")
            sv.update(h_in=h_in, c_q=c_q, c_kv=c_kv, ops=ops, o=o, lse=lse, o_delta=o_delta)
        else:
            arrive(2, u)
            wt_fox = full("fox_w_in", d)
            wt_qkv = wt_fox[:3 * d]
            wt_f = jnp.pad(wt_fox[3 * d:], ((0, LANES - FOX_HEADS), (0, 0)))
            w_fo = full("fox_w_o", d)
            qkv = mm([(u, wt_qkv)], trans_b=True, out_dtype=BF16, out_slab=True, name=f"fox_qkv{i}")
            z = mm([(u, wt_f)], trans_b=True, out_dtype=F32, name=f"fox_z{i}")
            f_tok, f_q = fox_gate_forward(z, b_f, bl, f"fox_gate{i}")
            f_k = f_tok[:, :FOX_HEADS].reshape(bl, nk, tq, FOX_HEADS // 2, 2).transpose(0, 3, 1, 4, 2)
            f_k = jnp.pad(f_k.reshape(bl * FOX_HEADS // 2, nk, 2, tq), ((0, 0), (0, 0), (0, 6), (0, 0)))
            ops = (qkv, f_q, f_k)
            o, lse, o_delta = attention_forward("fox", ops, bl, fox_scale, f"fox_attn{i}")
            y = mm([(o, w_fo)], trans_b=False, out_dtype=F32, name=f"fox_out{i}")
            sv.update(z=z, ops=ops, o=o, lse=lse, o_delta=o_delta)
        x1, r1, u2 = residual_layer_norm(xin, y, mods[i], 2, ln_g_all[i, 0], ln_b_all[i, 0], bl, f"ln_mix{i}",
                                         next_mod=(3, 4))
        if wt_gate[i] is None:
            arrive(group_of[f"gate{i}"], u2)
        a, bb, h = swiglu_in(u2, wt_gate[i], wt_up[i], f"ffn_in{i}")
        y2 = mm([(h, w_down[i])], trans_b=False, out_dtype=F32, name=f"ffn_down{i}")
        sv.update(y=y, r1=r1, u2=u2, a=a, bb=bb, h=h, y2=y2)
        if i + 1 < DEPTH:
            xin, r2, u = residual_layer_norm(x1, y2, mods[i], 5, ln_g_all[i, 1], ln_b_all[i, 1], bl, f"ln_ffn{i}",
                                             next_mod=(0, 1, mods[i + 1]))
        else:
            xin, r2 = residual_layer_norm(x1, y2, mods[i], 5, ln_g_all[i, 1], ln_b_all[i, 1], bl, f"ln_ffn{i}")
        sv.update(r2=r2)
        saved.append(sv)

    loss_cols, d_x = loss_head(xin, loss_target.reshape(t, d), "loss_head")

    grads_full = {}
    wgrad = functools.partial(mm_tn, out_dtype=BF16)
    dmod = [[None] * 6 for _ in range(DEPTH)]
    dg_ln = [[None, None] for _ in range(DEPTH)]
    db_ln = [[None, None] for _ in range(DEPTH)]
    dg_q = dg_kv = db_f = None
    d_a, du = d_x, None
    scatter_started = [None] * len(groups)

    def scatter_start(gi, after=None):
        gs = [grads_full[nm].reshape(N_DEV, rows_of[nm], PACK_COLS).astype(BF16) for nm in groups[gi]]
        if gi == 0:
            gs = [jnp.concatenate([slot(nm, g) for nm, g in zip(groups[gi], gs)], axis=1)]
        if after is not None:
            gs = [held_until(g, after) for g in gs]
        lands = [landing(lax.dynamic_index_in_dim(g, dev, 0, keepdims=False)) for g in gs]
        scatter_started[gi] = exchange_start(gs, lands, f"scatter_group{gi}_start", True)

    ln_g_bwd = [[ln_g_all[i, k] for k in range(2)] for i in range(DEPTH)]
    for i in reversed(range(DEPTH)):
        sv = saved[i]
        if i + 1 < DEPTH:
            gi = group_of["fox_w_in"]
            scatter_start(gi)
            ln_g_bwd[i][1] = after_token(ln_g_bwd[i][1], scatter_started[gi])
        ln2 = (sv["r2"], sv["y2"], ln_g_bwd[i][1], ln_b_all[i, 1], (mods[i], 5))
        if du is None:
            bw = sublayer_backward(d_a, bl, f"bwd_ln_ffn{i}", ln=ln2)
        else:
            bw = sublayer_backward(d_a, bl, f"bwd_ln_ffn{i}", du=du, scale=(mods[i + 1], 1), ln=ln2)
            dmod[i + 1][0], dmod[i + 1][1] = bw["dshift"], bw["dscale"]
        dmod[i][5], dg_ln[i][1], db_ln[i][1] = bw["dgate"], bw["dg"], bw["db"]
        dy2 = bw["dy"]
        da, dbb = swiglu_out_backward(dy2, w_down[i], sv["a"], sv["bb"], f"bwd_ffn_act{i}")
        du2 = mm([(da, wt_gate[i]), (dbb, wt_up[i])], trans_b=False, out_dtype=F32, name=f"bwd_ffn_du{i}")
        grads_full[f"down{i}"] = wgrad(sv["h"], dy2, name=f"bwd_w_down{i}")
        grads_full[f"gate{i}"] = wgrad(da, sv["u2"], name=f"bwd_w_gate{i}")
        grads_full[f"up{i}"] = wgrad(dbb, sv["u2"], name=f"bwd_w_up{i}")
        gi = group_of[f"gate{i}"]
        scatter_start(gi)
        ln_g_bwd[i][0] = after_token(ln_g_bwd[i][0], scatter_started[gi])
        bw = sublayer_backward(bw["dx"], bl, f"bwd_ln_mix{i}", du=du2, scale=(mods[i], 4),
                               ln=(sv["r1"], sv["y"], ln_g_bwd[i][0], ln_b_all[i, 0], (mods[i], 2)))
        dmod[i][3], dmod[i][4], dmod[i][2] = bw["dshift"], bw["dscale"], bw["dgate"]
        dg_ln[i][0], db_ln[i][0] = bw["dg"], bw["db"]
        d_a, dy = bw["dx"], bw["dy"]
        o, lse, ops = sv["o"], sv["lse"], sv["ops"]
        if i % 2 == 0:
            do = mm([(dy, w_mo)], trans_b=True, out_dtype=BF16, out_slab=True, name=f"bwd_mla_do{i}")
            grads_full["mla_w_o"] = wgrad(o, dy, name=f"bwd_w_mla_o{i}")
            dqn, dqr, dkn, dvm, dkr = attention_backward("mla", ops, sv["o_delta"], do, lse, bl, mla_scale,
                                                         f"bwd_mla_attn{i}")
            dqr = rope_slabs(dqr, cos_t, sin_t, F32, f"bwd_mla_qrope{i}", transposed=True)
            dcq = mm([(dqn, wt_uq_n), (dqr, wt_uq_r)], trans_b=False, out_dtype=F32, name=f"bwd_mla_dcq{i}")
            dckv = mm([(dkn, wt_uk), (dvm, wt_uv)], trans_b=False, out_dtype=F32, name=f"bwd_mla_dckv{i}")
            d_uq_n = wgrad(dqn, sv["c_q"], name=f"bwd_w_uq_n{i}").reshape(MLA_HEADS, MLA_NOPE, MLA_QR)
            d_uq_r = wgrad(dqr, sv["c_q"], name=f"bwd_w_uq_r{i}").reshape(MLA_HEADS, MLA_ROPE, MLA_QR)
            grads_full["mla_w_uq"] = jnp.concatenate([d_uq_n, d_uq_r], axis=1)
            grads_full["mla_w_uk"] = wgrad(dkn, sv["c_kv"], name=f"bwd_w_uk{i}")
            grads_full["mla_w_uv"] = wgrad(dvm, sv["c_kv"], name=f"bwd_w_uv{i}")
            dh_in, dg_q, dg_kv = mla_latents_backward(sv["h_in"], dcq, dckv, dkr, g_q, g_kv, cos_t, sin_t,
                                                      f"bwd_mla_latents{i}")
            du = mm([(dh_in, w_in)], trans_b=True, out_dtype=F32, name=f"bwd_mla_du{i}")
            grads_full["mla_w_in"] = wgrad(sv["u"], dh_in, name=f"bwd_w_mla_in{i}")[:, :mla_in]
        else:
            do = mm([(dy, w_fo)], trans_b=True, out_dtype=BF16, out_slab=True, name=f"bwd_fox_do{i}")
            grads_full["fox_w_o"] = wgrad(o, dy, name=f"bwd_w_fox_o{i}")
            dq, dk, dvf, dfk = attention_backward("fox", ops, sv["o_delta"], do, lse, bl, fox_scale, f"bwd_fox_attn{i}")
            df = dfk[:, :, :2, :].reshape(bl, FOX_HEADS // 2, nk, 2, tq).transpose(0, 2, 4, 1, 3).reshape(t, FOX_HEADS)
            df = jnp.pad(df, ((0, 0), (0, LANES - FOX_HEADS)))
            dz, db_f = fox_gate_backward(sv["z"], b_f, df, bl, f"bwd_fox_gate{i}")
            du = mm([(dq, wt_fox[0:d]), (dk, wt_fox[d:2 * d]), (dvf, wt_fox[2 * d:3 * d]), (dz, wt_f)],
                    trans_b=False, out_dtype=F32, name=f"bwd_fox_du{i}")
            u_f = sv["u"]
            grads_full["fox_w_in"] = jnp.concatenate(
                [wgrad(dq, u_f, name=f"bwd_w_fox_q{i}"), wgrad(dk, u_f, name=f"bwd_w_fox_k{i}"),
                 wgrad(dvf, u_f, name=f"bwd_w_fox_v{i}"), wgrad(dz, u_f, name=f"bwd_w_fox_f{i}")[:FOX_HEADS]], axis=0)
    scatter_start(0)
    bw = sublayer_backward(d_a, bl, "bwd_input", du=du, scale=(after_token(mods[0], scatter_started[0]), 1), x_in=x2d)
    dmod[0][0], dmod[0][1] = bw["dshift"], bw["dscale"]
    grad_x = bw["dx"].reshape(bl, s, d)

    dmod_rows = jnp.concatenate([r.reshape(bl, d) for layer in dmod for r in layer], axis=0)
    dmod_rows = dmod_rows.reshape(DEPTH, 6, bl, d).transpose(0, 2, 1, 3)
    n_mod = dmod_rows.size // LANES
    ln_parts = [dg_ln[i][k] for i in range(DEPTH) for k in range(2)] + [db_ln[i][k] for i in range(DEPTH) for k in range(2)]
    small_g = jnp.concatenate([dmod_rows.reshape(-1, LANES), dg_q.reshape(-1, LANES), dg_kv.reshape(-1, LANES), db_f]
                              + [p.reshape(-1, LANES) for p in ln_parts] + [loss_cols.reshape(-1, LANES)], axis=0)
    n_small = small_g.shape[0]
    small_g = jnp.pad(small_g, ((0, (-n_small) % 8), (0, 0)))
    small_g_all = all_gather(small_g, "gather_small_grads")
    small_sum = sum_leading(small_g_all, "sum_small_grads")
    per_seq = DEPTH * 6 * d // LANES
    dmod_all = small_g_all[:, :n_mod].reshape(N_DEV, DEPTH, bl, 6 * d).transpose(1, 0, 2, 3)
    dmod_all = dmod_all.reshape(DEPTH, N_DEV * bl, 6 * d)
    o1 = n_mod
    grad_g_q = small_sum[o1:o1 + MLA_QR // LANES].reshape(1, MLA_QR)
    o1 += MLA_QR // LANES
    grad_g_kv = small_sum[o1:o1 + MLA_KVR // LANES].reshape(1, MLA_KVR)
    o1 += MLA_KVR // LANES
    grad_b_f = small_sum[o1:o1 + 1, :FOX_HEADS]
    o1 += 1
    n_ln_rows = DEPTH * 2 * d // LANES
    grad_ln_g_full = small_sum[o1:o1 + n_ln_rows].reshape(DEPTH, 2, d)
    grad_ln_b_full = small_sum[o1 + n_ln_rows:o1 + 2 * n_ln_rows].reshape(DEPTH, 2, d)
    loss = jnp.sum(small_sum[o1 + 2 * n_ln_rows:o1 + 2 * n_ln_rows + d // LANES])
    shard = d // N_DEV
    grad_ln_g = lax.dynamic_slice_in_dim(grad_ln_g_full, dev * shard, shard, axis=2)
    grad_ln_b = lax.dynamic_slice_in_dim(grad_ln_b_full, dev * shard, shard, axis=2)
    by_seq = small_g_all[:, :n_mod].reshape(N_DEV, DEPTH, bl, 6 * d // LANES, LANES).transpose(0, 2, 1, 3, 4)
    grad_ada_b = sum_leading(by_seq.reshape(N_DEV * bl, per_seq, LANES), "sum_ada_b").reshape(DEPTH, 6 * d)
    dmod_cols = lax.dynamic_slice_in_dim(dmod_all, dev * ada_cols, ada_cols, axis=2)
    grad_ada_w = jnp.stack([mm_tn(c_act, dmod_cols[i], name=f"bwd_w_ada{i}") for i in range(DEPTH)])

    g_mine = {}

    def scatter_arrive(gi, after):
        landed = exchange_wait(scatter_started[gi], after, f"scatter_group{gi}_wait", True)
        if gi == 0:
            total = sum_leading(landed[0], f"scatter_group{gi}_sum")
            g_mine.update({nm: total[offsets[nm]:offsets[nm] + rows_of[nm]] for nm in groups[gi]})
            return total
        for nm, land in zip(groups[gi], landed):
            g_mine[nm] = sum_leading(land, f"scatter_sum_{nm}")
        return g_mine[groups[gi][-1]]

    after = scatter_started[0][4]
    for gi in reversed(range(1, len(groups))):
        after = scatter_arrive(gi, after)

    def mine(nm, shape):
        return g_mine[nm].reshape(shape)

    def shard_t(nm, a):
        return mine(nm, t_last(a).shape)

    transposed = {"mla_w_uq", "mla_w_uk", "mla_w_uv", "fox_w_in", "ffn_w_gate", "ffn_w_up"}
    grads = {
        "mla_w_in": lambda: mine("mla_w_in", mla_w_in[0].shape)[None],
        "mla_g_q": lambda: grad_g_q,
        "mla_w_uq": lambda: shard_t("mla_w_uq", mla_w_uq[0])[None],
        "mla_g_kv": lambda: grad_g_kv,
        "mla_w_uk": lambda: shard_t("mla_w_uk", mla_w_uk[0])[None],
        "mla_w_uv": lambda: shard_t("mla_w_uv", mla_w_uv[0])[None],
        "mla_w_o": lambda: mine("mla_w_o", mla_w_o[0].shape)[None],
        "fox_w_in": lambda: shard_t("fox_w_in", fox_w_in[0])[None],
        "fox_b_f": lambda: grad_b_f,
        "fox_w_o": lambda: mine("fox_w_o", fox_w_o[0].shape)[None],
        "ada_w": lambda: grad_ada_w,
        "ada_b": lambda: grad_ada_b,
        "ffn_w_gate": lambda: jnp.stack([shard_t(f"gate{i}", ffn_w_gate[i]) for i in range(DEPTH)]),
        "ffn_w_up": lambda: jnp.stack([shard_t(f"up{i}", ffn_w_up[i]) for i in range(DEPTH)]),
        "ffn_w_down": lambda: jnp.stack([mine(f"down{i}", ffn_w_down[i].shape) for i in range(DEPTH)]),
        "ln_g": lambda: grad_ln_g,
        "ln_b": lambda: grad_ln_b,
    }
    weights = dict(mla_w_in=mla_w_in, mla_g_q=mla_g_q, mla_w_uq=mla_w_uq, mla_g_kv=mla_g_kv, mla_w_uk=mla_w_uk,
                   mla_w_uv=mla_w_uv, mla_w_o=mla_w_o, fox_w_in=fox_w_in, fox_b_f=fox_b_f, fox_w_o=fox_w_o,
                   ada_w=ada_w, ada_b=ada_b, ffn_w_gate=ffn_w_gate, ffn_w_up=ffn_w_up, ffn_w_down=ffn_w_down,
                   ln_g=ln_g, ln_b=ln_b)
    first = dict(mla_w_in=m_mla_w_in, mla_g_q=m_mla_g_q, mla_w_uq=m_mla_w_uq, mla_g_kv=m_mla_g_kv, mla_w_uk=m_mla_w_uk,
                 mla_w_uv=m_mla_w_uv, mla_w_o=m_mla_w_o, fox_w_in=m_fox_w_in, fox_b_f=m_fox_b_f, fox_w_o=m_fox_w_o,
                 ada_w=m_ada_w, ada_b=m_ada_b, ffn_w_gate=m_ffn_w_gate, ffn_w_up=m_ffn_w_up, ffn_w_down=m_ffn_w_down,
                 ln_g=m_ln_g, ln_b=m_ln_b)
    second = dict(mla_w_in=v_mla_w_in, mla_g_q=v_mla_g_q, mla_w_uq=v_mla_w_uq, mla_g_kv=v_mla_g_kv, mla_w_uk=v_mla_w_uk,
                  mla_w_uv=v_mla_w_uv, mla_w_o=v_mla_w_o, fox_w_in=v_fox_w_in, fox_b_f=v_fox_b_f, fox_w_o=v_fox_w_o,
                  ada_w=v_ada_w, ada_b=v_ada_b, ffn_w_gate=v_ffn_w_gate, ffn_w_up=v_ffn_w_up, ffn_w_down=v_ffn_w_down,
                  ln_g=v_ln_g, ln_b=v_ln_b)
    order = list(weights)
    last = [nm for nm in order if group_of.get(nm) == 0]
    updated = {}
    for nm in [nm for nm in order if nm not in last] + last:
        if last and nm == last[0]:
            scatter_arrive(0, after)
        lay = t_last if nm in transposed else (lambda a: a)
        w = lay(weights[nm])
        g = grads[nm]().reshape(w.shape)
        delta, new_m, new_v = adamw(w, g, lay(first[nm]), lay(second[nm]), f"adamw_{nm}")
        updated[nm] = (lay(g), lay(delta), lay(new_m), lay(new_v))
        after = new_v
    return (loss, grad_x, *(updated[nm][k] for k in range(4) for nm in order))
```

```python
import functools
import math

import jax
import jax.numpy as jnp
from jax import lax
from jax.experimental import pallas as pl
from jax.experimental.pallas import tpu as pltpu

F32 = jnp.float32
BF16 = jnp.bfloat16
LANES = 128
N_DEV = 8
VMEM_LIMIT_BYTES = 56 * 1024 * 1024

DEPTH = 2
MLA_HEADS = 8
MLA_NOPE = 128
MLA_ROPE = 64
MLA_V = 128
MLA_QR = 256
MLA_KVR = 256
ROPE_THETA = 10000.0
FOX_HEADS = 16
FOX_HD = 64
ALPHA = (2.0 * DEPTH) ** 0.25
NORM_EPS = 1e-5
ADAM_LR = 0.001
ADAM_B1 = 0.9
ADAM_B2 = 0.999
ADAM_EPS = 1e-08
ADAM_WD = 0.01
ADAM_STEP = 10

MESH = pl.DeviceIdType.MESH


def _params(*sem):
    return pltpu.CompilerParams(dimension_semantics=sem, vmem_limit_bytes=VMEM_LIMIT_BYTES)


def _tile(n, cap, mult=LANES):
    if n <= cap:
        return n
    best = None
    for t in range(mult, cap + 1, mult):
        if n % t == 0:
            best = t
    assert best is not None, (n, cap, mult)
    return best


def _dot(a, b, dims):
    return lax.dot_general(a, b, (dims, ((), ())), preferred_element_type=F32)


def _nn(a, b):
    return _dot(a, b, ((1,), (0,)))


def _nt(a, b):
    return _dot(a, b, ((1,), (1,)))


def _tn(a, b):
    return _dot(a, b, ((0,), (0,)))


def _me():
    return lax.axis_index("x"), lax.axis_index("y"), lax.axis_index("c")


def all_gather(x_loc, name):
    r, c = x_loc.shape

    def body(x_ref, out_ref, send_sems, recv_sems, local_sem):
        x, y, cc = _me()
        me, sibling = (x, y, cc), (x, y, 1 - cc)
        chips = [(1 - x, y), (x, 1 - y), (1 - x, 1 - y)]

        def rows(px, py, pc):
            return out_ref.at[4 * px + 2 * py + pc]

        def copy(k, block, to, src=None):
            return pltpu.make_async_remote_copy(
                src_ref=rows(*block) if src is None else src, dst_ref=rows(*block),
                send_sem=send_sems.at[k], recv_sem=recv_sems.at[k], device_id=to, device_id_type=MESH)

        mine = pltpu.make_async_copy(x_ref, rows(*me), local_sem)
        mine.start()
        first = [copy(0, me, sibling, src=x_ref)]
        first += [copy(1 + j, me, (*chip, cc), src=x_ref) for j, chip in enumerate(chips)]
        for cp in first:
            cp.start()
        passed = [copy(4 + j, (*chip, cc), sibling) for j, chip in enumerate(chips)]
        for j, chip in enumerate(chips):
            copy(1 + j, (*chip, cc), me).wait_recv()
            passed[j].start()
        copy(0, sibling, me).wait_recv()
        for j, chip in enumerate(chips):
            copy(4 + j, (*chip, 1 - cc), me).wait_recv()
        for cp in first + passed:
            cp.wait_send()
        mine.wait()

    return pl.pallas_call(
        body, name=name,
        out_shape=jax.ShapeDtypeStruct((N_DEV, r, c), x_loc.dtype),
        in_specs=[pl.BlockSpec(memory_space=pl.ANY)],
        out_specs=pl.BlockSpec(memory_space=pl.ANY),
        scratch_shapes=[pltpu.SemaphoreType.DMA((7,)), pltpu.SemaphoreType.DMA((7,)), pltpu.SemaphoreType.DMA(())],
    )(x_loc)


HBM_SPEC = pl.BlockSpec(memory_space=pltpu.HBM)
SEM_SPEC = pl.BlockSpec(memory_space=pltpu.SEMAPHORE)
N_PEERS = N_DEV - 1


def _peer(k):
    x, y, c = _me()
    return (1 - x if k & 4 else x, 1 - y if k & 2 else y, 1 - c if k & 1 else c)


def _exchange_copies(src_refs, land_refs, send_sems, recv_sems, scatter):
    x, y, c = _me()
    mine = 4 * x + 2 * y + c
    copies = []
    for n, (src_ref, land_ref) in enumerate(zip(src_refs, land_refs)):
        for k in range(1, N_DEV):
            px, py, pc = _peer(k)
            src = src_ref.at[4 * px + 2 * py + pc] if scatter else src_ref
            sem = n * N_PEERS + k - 1
            copies.append(pltpu.make_async_remote_copy(
                src_ref=src, dst_ref=land_ref.at[mine], send_sem=send_sems.at[sem], recv_sem=recv_sems.at[sem],
                device_id=(px, py, pc), device_id_type=MESH))
    return copies


def exchange_start(srcs, lands, name, scatter):
    n = len(srcs)

    def body(*refs):
        send_sems, recv_sems = refs[2 * n], refs[2 * n + 1]
        for cp in _exchange_copies(refs[:n], refs[n:2 * n], send_sems, recv_sems, scatter):
            cp.start()
        token = refs[-1]
        token[...] = jnp.zeros_like(token)

    outs = pl.pallas_call(
        body, name=name,
        out_shape=(pltpu.SemaphoreType.DMA((n * N_PEERS,)), pltpu.SemaphoreType.DMA((n * N_PEERS,)),
                   *(pltpu.HBM(a.shape, a.dtype) for a in (*srcs, *lands)), jax.ShapeDtypeStruct((8, LANES), F32)),
        in_specs=(HBM_SPEC,) * (2 * n),
        out_specs=(SEM_SPEC, SEM_SPEC, *((HBM_SPEC,) * (2 * n)), pl.BlockSpec(memory_space=pltpu.VMEM)),
        input_output_aliases={i: 2 + i for i in range(2 * n)},
        compiler_params=pltpu.CompilerParams(has_side_effects=pltpu.SideEffectType.DATAFLOW_SIDE_EFFECTING),
    )(*(pltpu.with_memory_space_constraint(a, pltpu.HBM) for a in (*srcs, *lands)))
    return outs[0], outs[1], outs[2:2 + n], outs[2 + n:2 + 2 * n], outs[-1]


def exchange_wait(started, after, name, scatter):
    send_sems, recv_sems, srcs, lands, _ = started
    n = len(srcs)

    def body(*refs):
        send_sems, recv_sems = refs[2 * n], refs[2 * n + 1]
        for cp in _exchange_copies(refs[:n], refs[n:2 * n], send_sems, recv_sems, scatter):
            cp.wait_send()
            cp.wait_recv()

    outs = pl.pallas_call(
        body, name=name,
        out_shape=tuple(pltpu.HBM(a.shape, a.dtype) for a in (*srcs, *lands)),
        in_specs=(*((HBM_SPEC,) * (2 * n)), SEM_SPEC, SEM_SPEC, pl.BlockSpec(memory_space=pl.ANY)),
        out_specs=(HBM_SPEC,) * (2 * n), input_output_aliases={i: i for i in range(2 * n)},
        compiler_params=pltpu.CompilerParams(has_side_effects=pltpu.SideEffectType.DATAFLOW_SIDE_EFFECTING),
    )(*srcs, *lands, send_sems, recv_sems, after)
    return outs[n:]


def after_token(small, started):
    return small + started[4][0, 0]


def sum_leading(x, name):
    n, r, c = x.shape
    tr = _tile(r, 512, 16)

    def body(x_ref, o_ref):
        acc = x_ref[0].astype(F32)
        for k in range(1, n):
            acc = acc + x_ref[k].astype(F32)
        o_ref[...] = acc

    return pl.pallas_call(
        body, name=name,
        out_shape=jax.ShapeDtypeStruct((r, c), F32),
        grid=(r // tr,),
        in_specs=[pl.BlockSpec((n, tr, c), lambda i: (0, i, 0))],
        out_specs=pl.BlockSpec((tr, c), lambda i: (i, 0)),
        compiler_params=_params("arbitrary"),
    )(x)


MM_VMEM_BUDGET = 36 * 1024 * 1024
GRID_STEP_AS_BYTES = 1 << 20


def _mm_tiles(m, n, a_row_bytes, b_col_bytes, out_bytes):
    tms = [c for c in (2048, 1024, 512, 256, 128, 64, 32, 16, 8) if m % c == 0] or [m]
    tns = [c for c in range(LANES, min(n, 2048) + 1, LANES) if n % c == 0] or [n]
    best = None
    for tm in tms:
        for tn in tns:
            vmem = 2 * (tm * a_row_bytes + tn * b_col_bytes) + 2 * tm * tn * out_bytes + tm * tn * 4
            if vmem > MM_VMEM_BUDGET:
                continue
            steps = (m // tm) * (n // tn)
            cost = steps * GRID_STEP_AS_BYTES + (m // tm) * n * b_col_bytes + m * a_row_bytes
            if best is None or cost < best[0]:
                best = (cost, tm, tn)
    assert best is not None, (m, n, a_row_bytes, b_col_bytes)
    return best[1], best[2]


def mm(pairs, *, trans_b, out_dtype, name, out_slab=False, bias=None):
    a0 = pairs[0][0]
    m = a0.shape[1] if a0.ndim == 3 else a0.shape[0]
    n = pairs[0][1].shape[0] if trans_b else pairs[0][1].shape[1]
    a_row_bytes = sum((b.shape[1] if trans_b else b.shape[0]) * a.dtype.itemsize for a, b in pairs)
    b_col_bytes = sum((b.shape[1] if trans_b else b.shape[0]) * b.dtype.itemsize for _, b in pairs)
    tm, tn = _mm_tiles(m, n, a_row_bytes, b_col_bytes, jnp.dtype(out_dtype).itemsize)
    slabs = [a.ndim == 3 for a, _ in pairs]
    n_pairs = len(pairs)

    def body(*refs):
        o_ref = refs[-1]
        acc = bias_ref = None
        if bias is not None:
            bias_ref = refs[2 * n_pairs]
        for i in range(n_pairs):
            a_ref, b_ref = refs[2 * i], refs[2 * i + 1]
            if slabs[i]:
                a = jnp.concatenate([a_ref[s].astype(BF16) for s in range(a_ref.shape[0])], axis=1)
            else:
                a = a_ref[...].astype(BF16)
            b = b_ref[...].astype(BF16)
            part = _nt(a, b) if trans_b else _nn(a, b)
            acc = part if acc is None else acc + part
        if bias_ref is not None:
            acc = acc + bias_ref[...]
        if out_slab:
            for s in range(tn // LANES):
                o_ref[s] = acc[:, s * LANES:(s + 1) * LANES].astype(out_dtype)
        else:
            o_ref[...] = acc.astype(out_dtype)

    in_specs, args = [], []
    for (a, b), slab in zip(pairs, slabs):
        if slab:
            in_specs.append(pl.BlockSpec((a.shape[0], tm, LANES), lambda i, j: (0, i, 0)))
        else:
            in_specs.append(pl.BlockSpec((tm, a.shape[1]), lambda i, j: (i, 0)))
        if trans_b:
            in_specs.append(pl.BlockSpec((tn, b.shape[1]), lambda i, j: (j, 0)))
        else:
            in_specs.append(pl.BlockSpec((b.shape[0], tn), lambda i, j: (0, j)))
        args += [a, b]
    if bias is not None:
        in_specs.append(pl.BlockSpec((1, tn), lambda i, j: (0, j)))
        args.append(bias)
    if out_slab:
        out_shape = jax.ShapeDtypeStruct((n // LANES, m, LANES), out_dtype)
        out_spec = pl.BlockSpec((tn // LANES, tm, LANES), lambda i, j: (j, i, 0))
    else:
        out_shape = jax.ShapeDtypeStruct((m, n), out_dtype)
        out_spec = pl.BlockSpec((tm, tn), lambda i, j: (i, j))
    return pl.pallas_call(
        body, name=name, out_shape=out_shape, grid=(m // tm, n // tn),
        in_specs=in_specs, out_specs=out_spec,
        compiler_params=_params("arbitrary", "arbitrary"),
    )(*args)


def mm_tn(a, b, *, name, out_dtype=F32, tk_cap=1536, tn_cap=1024, tm_cap=2048):
    slab = a.ndim == 3
    m = a.shape[1] if slab else a.shape[0]
    k = a.shape[0] * LANES if slab else a.shape[1]
    n = b.shape[1]
    tk = _tile(k, tk_cap)
    tn = _tile(n, tn_cap)
    tm = _tile(m, tm_cap, 8)
    n_steps = m // tm

    def body(a_ref, b_ref, o_ref, acc_ref):
        step = pl.program_id(2)

        @pl.when(step == 0)
        def _():
            acc_ref[...] = jnp.zeros_like(acc_ref)

        bb = b_ref[...].astype(BF16)
        if slab:
            for s in range(tk // LANES):
                acc_ref[s * LANES:(s + 1) * LANES, :] += _tn(a_ref[s].astype(BF16), bb)
        else:
            acc_ref[...] += _tn(a_ref[...].astype(BF16), bb)

        @pl.when(step == n_steps - 1)
        def _():
            o_ref[...] = acc_ref[...].astype(out_dtype)

    if slab:
        a_spec = pl.BlockSpec((tk // LANES, tm, LANES), lambda i, j, t: (i, t, 0))
    else:
        a_spec = pl.BlockSpec((tm, tk), lambda i, j, t: (t, i))
    return pl.pallas_call(
        body, name=name, out_shape=jax.ShapeDtypeStruct((k, n), out_dtype), grid=(k // tk, n // tn, n_steps),
        in_specs=[a_spec, pl.BlockSpec((tm, tn), lambda i, j, t: (t, j))],
        out_specs=pl.BlockSpec((tk, tn), lambda i, j, t: (i, j)),
        scratch_shapes=[pltpu.VMEM((tk, tn), F32)],
        compiler_params=_params("arbitrary", "arbitrary", "arbitrary"),
    )(a, b)


def _row_spec(d, k):
    return pl.BlockSpec((1, 1, d), lambda b, i: (6 * b + k, 0, 0))


def modulate(x, mod, k_shift, k_scale, bl, name):
    t, d = x.shape
    s = t // bl
    tm = _tile(s, 512, 8)
    nt = s // tm

    def body(x_ref, sh_ref, sc_ref, o_ref):
        o_ref[...] = (x_ref[...] * (1.0 + sc_ref[0]) + sh_ref[0]).astype(BF16)

    return pl.pallas_call(
        body, name=name, out_shape=jax.ShapeDtypeStruct((t, d), BF16), grid=(bl, nt),
        in_specs=[pl.BlockSpec((tm, d), lambda b, i: (b * nt + i, 0)), _row_spec(d, k_shift), _row_spec(d, k_scale)],
        out_specs=pl.BlockSpec((tm, d), lambda b, i: (b * nt + i, 0)),
        compiler_params=_params("arbitrary", "arbitrary"),
    )(x, mod, mod)


def _layer_norm_stats(r):
    mu = jnp.mean(r, axis=-1, keepdims=True)
    rc = r - mu
    var = jnp.mean(rc * rc, axis=-1, keepdims=True)
    rstd = lax.rsqrt(var + NORM_EPS)
    return rc * rstd, rstd


def residual_layer_norm(x, y, mod, k_gate, g, b, bl, name, next_mod=None):
    t, d = x.shape
    s = t // bl
    tm = _tile(s, 512, 8)
    nt = s // tm
    has_next = next_mod is not None

    def body(*refs):
        x_ref, y_ref, gt_ref, g_ref, b_ref = refs[:5]
        rest = refs[5:]
        if has_next:
            sh_ref, sc_ref, o_ref, r_ref, u_ref = rest
        else:
            o_ref, r_ref = rest
        r = ALPHA * x_ref[...] + (1.0 + gt_ref[0]) * y_ref[...]
        xhat, _ = _layer_norm_stats(r)
        out = xhat * g_ref[...] + b_ref[...]
        o_ref[...] = out
        r_ref[...] = r
        if has_next:
            u_ref[...] = (out * (1.0 + sc_ref[0]) + sh_ref[0]).astype(BF16)

    tok = pl.BlockSpec((tm, d), lambda bb, i: (bb * nt + i, 0))
    vec = pl.BlockSpec((1, d), lambda bb, i: (0, 0))
    in_specs = [tok, tok, _row_spec(d, k_gate), vec, vec]
    args = [x, y, mod, g, b]
    out_shape = [jax.ShapeDtypeStruct((t, d), F32), jax.ShapeDtypeStruct((t, d), F32)]
    out_specs = [tok, tok]
    if has_next:
        in_specs += [_row_spec(d, next_mod[0]), _row_spec(d, next_mod[1])]
        args += [mod if len(next_mod) == 2 else next_mod[2]] * 2
        out_shape.append(jax.ShapeDtypeStruct((t, d), BF16))
        out_specs.append(tok)
    return pl.pallas_call(
        body, name=name, out_shape=out_shape, grid=(bl, nt), in_specs=in_specs, out_specs=out_specs,
        compiler_params=_params("arbitrary", "arbitrary"),
    )(*args)


def loss_head(xo, target, name):
    t, d = xo.shape
    tm = _tile(t, 512, 8)

    def body(x_ref, t_ref, l_ref, dx_ref):
        @pl.when(pl.program_id(0) == 0)
        def _():
            l_ref[...] = jnp.zeros_like(l_ref)

        e = x_ref[...] - t_ref[...]
        l_ref[...] += jnp.sum(e * e, axis=0, keepdims=True) * (0.5 / d)
        dx_ref[...] = e * (1.0 / d)

    tok = pl.BlockSpec((tm, d), lambda i: (i, 0))
    return pl.pallas_call(
        body, name=name,
        out_shape=[jax.ShapeDtypeStruct((1, d), F32), jax.ShapeDtypeStruct((t, d), F32)],
        grid=(t // tm,), in_specs=[tok, tok],
        out_specs=[pl.BlockSpec((1, d), lambda i: (0, 0)), tok],
        compiler_params=_params("arbitrary"),
    )(xo, target)


def sublayer_backward(d_a, bl, name, *, du=None, scale=None, x_in=None, ln=None):
    t, d = d_a.shape
    s = t // bl
    tm = _tile(s, 512, 8)
    nt = s // tm
    has_mod = du is not None
    has_ln = ln is not None
    assert has_mod or has_ln
    assert has_ln or x_in is not None

    def body(*refs):
        refs = list(refs)
        da_ref = refs.pop(0)
        if has_mod:
            du_ref, sc_ref = refs.pop(0), refs.pop(0)
        if has_ln:
            r_ref, y_ref, g_ref, b_ref, gt_ref = (refs.pop(0) for _ in range(5))
        elif has_mod:
            xin_ref = refs.pop(0)
        dx_ref = refs.pop(0)
        if has_ln:
            dy_ref, dg_ref, db_ref, dgt_ref = (refs.pop(0) for _ in range(4))
        if has_mod:
            dsc_ref, dsh_ref = refs.pop(0), refs.pop(0)
        first_tile = pl.program_id(1) == 0
        first_step = jnp.logical_and(pl.program_id(0) == 0, first_tile)

        dout = da_ref[...]
        if has_ln:
            xhat, rstd = _layer_norm_stats(r_ref[...])
        if has_mod:
            duv = du_ref[...]
            dout = dout + duv * (1.0 + sc_ref[0])
            xin = xhat * g_ref[...] + b_ref[...] if has_ln else xin_ref[...]

            @pl.when(first_tile)
            def _():
                dsc_ref[...] = jnp.zeros_like(dsc_ref)
                dsh_ref[...] = jnp.zeros_like(dsh_ref)

            dsc_ref[0] += jnp.sum(duv * xin, axis=0, keepdims=True)
            dsh_ref[0] += jnp.sum(duv, axis=0, keepdims=True)
        if not has_ln:
            dx_ref[...] = dout
            return

        @pl.when(first_step)
        def _():
            dg_ref[...] = jnp.zeros_like(dg_ref)
            db_ref[...] = jnp.zeros_like(db_ref)

        @pl.when(first_tile)
        def _():
            dgt_ref[...] = jnp.zeros_like(dgt_ref)

        dg_ref[...] += jnp.sum(dout * xhat, axis=0, keepdims=True)
        db_ref[...] += jnp.sum(dout, axis=0, keepdims=True)
        dxh = dout * g_ref[...]
        dr = rstd * (dxh - jnp.mean(dxh, axis=-1, keepdims=True) - xhat * jnp.mean(dxh * xhat, axis=-1, keepdims=True))
        dx_ref[...] = ALPHA * dr
        dy_ref[...] = ((1.0 + gt_ref[0]) * dr).astype(BF16)
        dgt_ref[0] += jnp.sum(dr * y_ref[...], axis=0, keepdims=True)

    tok = pl.BlockSpec((tm, d), lambda bb, i: (bb * nt + i, 0))
    vec = pl.BlockSpec((1, d), lambda bb, i: (0, 0))
    seq = pl.BlockSpec((1, 1, d), lambda bb, i: (bb, 0, 0))
    in_specs, args = [tok], [d_a]
    if has_mod:
        in_specs += [tok, _row_spec(d, scale[1])]
        args += [du, scale[0]]
    if has_ln:
        r, y, g, b, gate = ln
        in_specs += [tok, tok, vec, vec, _row_spec(d, gate[1])]
        args += [r, y, g, b, gate[0]]
    elif has_mod:
        in_specs.append(tok)
        args.append(x_in)
    names = ["dx"]
    out_shape, out_specs = [jax.ShapeDtypeStruct((t, d), F32)], [tok]
    if has_ln:
        names += ["dy", "dg", "db", "dgate"]
        out_shape += [jax.ShapeDtypeStruct((t, d), BF16), jax.ShapeDtypeStruct((1, d), F32),
                      jax.ShapeDtypeStruct((1, d), F32), jax.ShapeDtypeStruct((bl, 1, d), F32)]
        out_specs += [tok, vec, vec, seq]
    if has_mod:
        names += ["dscale", "dshift"]
        out_shape += [jax.ShapeDtypeStruct((bl, 1, d), F32)] * 2
        out_specs += [seq, seq]
    outs = pl.pallas_call(
        body, name=name, out_shape=out_shape, grid=(bl, nt), in_specs=in_specs, out_specs=out_specs,
        compiler_params=_params("arbitrary", "arbitrary"),
    )(*args)
    return dict(zip(names, outs))


def _silu(a):
    return a * jax.nn.sigmoid(a)


def silu_rows(a, name):
    def body(a_ref, o_ref):
        o_ref[...] = _silu(a_ref[...]).astype(BF16)

    return pl.pallas_call(body, name=name, out_shape=jax.ShapeDtypeStruct(a.shape, BF16))(a)


def _swiglu_tiles(t, f):
    return _tile(t, 1024, 8), _tile(f, 1536)


def swiglu_in(u, wt_gate, wt_up, name):
    t, d = u.shape
    f = wt_gate.shape[0]
    tm, tf = _swiglu_tiles(t, f)

    def body(u_ref, g_ref, w_ref, a_ref, b_ref, h_ref):
        uv = u_ref[...]
        a = _nt(uv, g_ref[...])
        b = _nt(uv, w_ref[...])
        a_ref[...] = a.astype(BF16)
        b_ref[...] = b.astype(BF16)
        h_ref[...] = (_silu(a) * b).astype(BF16)

    w_spec = pl.BlockSpec((tf, d), lambda i, j: (j, 0))
    o_spec = pl.BlockSpec((tm, tf), lambda i, j: (i, j))
    return pl.pallas_call(
        body, name=name,
        out_shape=[jax.ShapeDtypeStruct((t, f), BF16)] * 3,
        grid=(t // tm, f // tf), in_specs=[pl.BlockSpec((tm, d), lambda i, j: (i, 0)), w_spec, w_spec],
        out_specs=[o_spec, o_spec, o_spec], compiler_params=_params("arbitrary", "arbitrary"),
    )(u, wt_gate, wt_up)


def swiglu_out_backward(dy, w_down, a, b, name):
    t, d = dy.shape
    f = w_down.shape[0]
    tm, tf = _swiglu_tiles(t, f)

    def body(dy_ref, w_ref, a_ref, b_ref, da_ref, db_ref):
        dh = _nt(dy_ref[...], w_ref[...])
        av = a_ref[...].astype(F32)
        sig = jax.nn.sigmoid(av)
        da_ref[...] = (dh * b_ref[...].astype(F32) * (sig * (1.0 + av * (1.0 - sig)))).astype(BF16)
        db_ref[...] = (dh * (av * sig)).astype(BF16)

    spec = pl.BlockSpec((tm, tf), lambda i, j: (i, j))
    return pl.pallas_call(
        body, name=name, out_shape=[jax.ShapeDtypeStruct((t, f), BF16)] * 2, grid=(t // tm, f // tf),
        in_specs=[pl.BlockSpec((tm, d), lambda i, j: (i, 0)), pl.BlockSpec((tf, d), lambda i, j: (j, 0)), spec, spec],
        out_specs=[spec, spec], compiler_params=_params("arbitrary", "arbitrary"),
    )(dy, w_down, a, b)


def rope_tables(pos, inv_freq, sign, name):
    t = pos.shape[0]
    tm = _tile(t, 512, 8)

    def body(p_ref, f_ref, s_ref, c_out, s_out):
        ang = p_ref[...] * f_ref[...]
        c_out[...] = jnp.cos(ang)
        s_out[...] = jnp.sin(ang) * s_ref[...]

    vec = pl.BlockSpec((1, LANES), lambda i: (0, 0))
    tab = pl.BlockSpec((tm, LANES), lambda i: (i, 0))
    return pl.pallas_call(
        body, name=name, out_shape=[jax.ShapeDtypeStruct((t, LANES), F32)] * 2, grid=(t // tm,),
        in_specs=[pl.BlockSpec((tm, 1), lambda i: (i, 0)), vec, vec], out_specs=[tab, tab],
        compiler_params=_params("arbitrary"),
    )(pos, inv_freq, sign)


def _rot_half(v):
    lane = lax.broadcasted_iota(jnp.int32, v.shape, v.ndim - 1)
    up = pltpu.roll(v, LANES - MLA_ROPE // 2, v.ndim - 1)
    down = pltpu.roll(v, MLA_ROPE // 2, v.ndim - 1)
    return jnp.where(lane % MLA_ROPE < MLA_ROPE // 2, up, down)


def _rope(v, cos, sin_signed):
    return v * cos + _rot_half(v) * sin_signed


def _rope_transposed(dv, cos, sin_signed):
    return dv * cos + _rot_half(dv * sin_signed)


def rope_slabs(v, cos, sin_signed, out_dtype, name, transposed=False):
    ns, t, _ = v.shape
    tm = _tile(t, 1024, 8)
    fn = _rope_transposed if transposed else _rope

    def body(v_ref, c_ref, s_ref, o_ref):
        for j in range(ns):
            o_ref[j] = fn(v_ref[j].astype(F32), c_ref[...], s_ref[...]).astype(out_dtype)

    tab = pl.BlockSpec((tm, LANES), lambda i: (i, 0))
    spec = pl.BlockSpec((ns, tm, LANES), lambda i: (0, i, 0))
    return pl.pallas_call(
        body, name=name, out_shape=jax.ShapeDtypeStruct(v.shape, out_dtype), grid=(t // tm,),
        in_specs=[spec, tab, tab], out_specs=spec, compiler_params=_params("arbitrary"),
    )(v, cos, sin_signed)


def _rms(x):
    rinv = lax.rsqrt(jnp.mean(x * x, axis=-1, keepdims=True) + NORM_EPS)
    return x * rinv, rinv


def mla_latents_forward(h_in, g_q, g_kv, cos, sin_signed, name):
    t = h_in.shape[0]
    tm = _tile(t, 512, 8)

    def body(h_ref, gq_ref, gkv_ref, c_ref, s_ref, cq_ref, ckv_ref, kr_ref):
        cq_ref[...] = (_rms(h_ref[:, 0:MLA_QR])[0] * gq_ref[...]).astype(BF16)
        ckv_ref[...] = (_rms(h_ref[:, MLA_QR:MLA_QR + MLA_KVR])[0] * gkv_ref[...]).astype(BF16)
        kr_ref[...] = _rope(h_ref[:, MLA_QR + MLA_KVR:], c_ref[...], s_ref[...]).astype(BF16)

    def tok(w):
        return pl.BlockSpec((tm, w), lambda i: (i, 0))

    def vec(w):
        return pl.BlockSpec((1, w), lambda i: (0, 0))

    return pl.pallas_call(
        body, name=name,
        out_shape=[jax.ShapeDtypeStruct((t, MLA_QR), BF16), jax.ShapeDtypeStruct((t, MLA_KVR), BF16),
                   jax.ShapeDtypeStruct((t, LANES), BF16)],
        grid=(t // tm,),
        in_specs=[tok(h_in.shape[1]), vec(MLA_QR), vec(MLA_KVR), tok(LANES), tok(LANES)],
        out_specs=[tok(MLA_QR), tok(MLA_KVR), tok(LANES)],
        compiler_params=_params("arbitrary"),
    )(h_in, g_q, g_kv, cos, sin_signed)


def mla_latents_backward(h_in, dcq, dckv, dkr, g_q, g_kv, cos, sin_signed, name):
    t, w = h_in.shape
    tm = _tile(t, 512, 8)

    def body(h_ref, dcq_ref, dckv_ref, dkr_ref, gq_ref, gkv_ref, c_ref, s_ref, dh_ref, dgq_ref, dgkv_ref):
        @pl.when(pl.program_id(0) == 0)
        def _():
            dgq_ref[...] = jnp.zeros_like(dgq_ref)
            dgkv_ref[...] = jnp.zeros_like(dgkv_ref)

        def rms_bwd(x, dc, g_ref, dg_ref):
            xn, rinv = _rms(x)
            dg_ref[...] += jnp.sum(dc * xn, axis=0, keepdims=True)
            dxn = dc * g_ref[...]
            return rinv * (dxn - xn * jnp.mean(dxn * xn, axis=-1, keepdims=True))

        dq = rms_bwd(h_ref[:, 0:MLA_QR], dcq_ref[...], gq_ref, dgq_ref)
        dkv = rms_bwd(h_ref[:, MLA_QR:MLA_QR + MLA_KVR], dckv_ref[...], gkv_ref, dgkv_ref)
        dr = _rope_transposed(dkr_ref[...], c_ref[...], s_ref[...])
        dh_ref[...] = jnp.concatenate([dq, dkv, dr], axis=1).astype(BF16)

    def tok(ww):
        return pl.BlockSpec((tm, ww), lambda i: (i, 0))

    def vec(ww):
        return pl.BlockSpec((1, ww), lambda i: (0, 0))

    return pl.pallas_call(
        body, name=name,
        out_shape=[jax.ShapeDtypeStruct((t, w), BF16), jax.ShapeDtypeStruct((1, MLA_QR), F32),
                   jax.ShapeDtypeStruct((1, MLA_KVR), F32)],
        grid=(t // tm,),
        in_specs=[tok(w), tok(MLA_QR), tok(MLA_KVR), tok(LANES), vec(MLA_QR), vec(MLA_KVR), tok(LANES), tok(LANES)],
        out_specs=[tok(w), vec(MLA_QR), vec(MLA_KVR)],
        compiler_params=_params("arbitrary"),
    )(h_in, dcq, dckv, dkr, g_q, g_kv, cos, sin_signed)


def _tri(n, lower):
    r = lax.broadcasted_iota(jnp.int32, (n, n), 0)
    c = lax.broadcasted_iota(jnp.int32, (n, n), 1)
    return jnp.where(r >= c if lower else r <= c, 1.0, 0.0).astype(F32)


def _dot_exact(tri, v):
    hi = v.astype(BF16)
    mid = (v - hi.astype(F32)).astype(BF16)
    lo = (v - hi.astype(F32) - mid.astype(F32)).astype(BF16)
    t = tri.astype(BF16)
    return _nn(t, hi) + _nn(t, mid) + _nn(t, lo)


def fox_gate_forward(z, b_f, bl, name):
    t = z.shape[0]
    s = t // bl
    ch = LANES
    n_ch = s // ch

    def body(z_ref, b_ref, f_ref, fs_ref):
        tri = _tri(ch, True)
        carry = jnp.zeros((1, LANES), F32)
        for k in range(n_ch):
            x = z_ref[k * ch:(k + 1) * ch, :] + b_ref[...]
            logf = jnp.minimum(x, 0.0) - jnp.log(1.0 + jnp.exp(-jnp.abs(x)))
            cs = _dot_exact(tri, logf) + carry
            carry = cs[ch - 1:ch, :]
            f_ref[k * ch:(k + 1) * ch, :] = cs
            for h in range(FOX_HEADS):
                fs_ref[h, k * ch:(k + 1) * ch, :] = jnp.broadcast_to(cs[:, h:h + 1], (ch, LANES))

    return pl.pallas_call(
        body, name=name,
        out_shape=[jax.ShapeDtypeStruct((t, LANES), F32), jax.ShapeDtypeStruct((FOX_HEADS, t, LANES), F32)],
        grid=(bl,),
        in_specs=[pl.BlockSpec((s, LANES), lambda b: (b, 0)), pl.BlockSpec((1, LANES), lambda b: (0, 0))],
        out_specs=[pl.BlockSpec((s, LANES), lambda b: (b, 0)),
                   pl.BlockSpec((FOX_HEADS, s, LANES), lambda b: (0, b, 0))],
        compiler_params=_params("arbitrary"),
    )(z, b_f)


def fox_gate_backward(z, b_f, df, bl, name):
    t = z.shape[0]
    s = t // bl
    ch = LANES
    n_ch = s // ch

    def body(z_ref, b_ref, df_ref, dz_ref, db_ref):
        @pl.when(pl.program_id(0) == 0)
        def _():
            db_ref[...] = jnp.zeros_like(db_ref)

        tri = _tri(ch, False)
        carry = jnp.zeros((1, LANES), F32)
        for k in reversed(range(n_ch)):
            cs = _dot_exact(tri, df_ref[k * ch:(k + 1) * ch, :]) + carry
            carry = cs[0:1, :]
            x = z_ref[k * ch:(k + 1) * ch, :] + b_ref[...]
            dz = cs * (1.0 - jax.nn.sigmoid(x))
            dz_ref[k * ch:(k + 1) * ch, :] = dz
            db_ref[...] += jnp.sum(dz, axis=0, keepdims=True)

    tok = pl.BlockSpec((s, LANES), lambda b: (b, 0))
    vec = pl.BlockSpec((1, LANES), lambda b: (0, 0))
    return pl.pallas_call(
        body, name=name,
        out_shape=[jax.ShapeDtypeStruct((t, LANES), F32), jax.ShapeDtypeStruct((1, LANES), F32)],
        grid=(bl,), in_specs=[tok, vec, tok], out_specs=[tok, vec],
        compiler_params=_params("arbitrary"),
    )(z, b_f, df)


NEG_INF = float("-inf")


def _attn_tiles(s):
    return _tile(s, 1024, 8)


def attention_forward(kind, ops, bl, scale, name):
    fox = kind == "fox"
    if fox:
        assert math.frexp(scale)[0] == 0.5, "the FoX scale is folded into bf16 queries: it must be a power of two"
        qkv, fq, fk = ops
        t = qkv.shape[1]
        n_pair = FOX_HEADS // 2
    else:
        qn, qr, kn, kr, v = ops
        t = qn.shape[1]
        n_pair = MLA_HEADS // 2
    s = t // bl
    tq = _attn_tiles(s)
    nq = s // tq
    half = LANES // 2

    def body(*refs):
        if fox:
            q_ref, k_ref, v_ref, fq_ref, fk_ref, o_ref, lse_ref, o32_ref = refs
        else:
            qn_ref, qr_ref, kn_ref, kr_ref, v_ref, o_ref, lse_ref = refs
        i = pl.program_id(2)
        row = lax.broadcasted_iota(jnp.int32, (tq, tq), 0)
        col = lax.broadcasted_iota(jnp.int32, (tq, tq), 1)
        heads = []
        for e in range(2):
            sl = slice(e * half, (e + 1) * half)
            if fox:
                heads.append((sl, q_ref[0, :, sl] * jnp.asarray(scale, BF16), None))
            else:
                heads.append((sl, jnp.concatenate([qn_ref[e], qr_ref[0, :, sl], jnp.zeros((tq, half), BF16)], axis=1),
                              None))
        dv = half if fox else LANES

        def wide(stat):
            return jnp.concatenate([stat] * (tq // LANES), axis=1)

        def step(j, carry, masked):
            rows = pl.ds(pl.multiple_of(j * tq, tq), tq)
            new = []
            for e, (sl, qa, qb) in enumerate(heads):
                m, l, acc = carry[e]
                if fox:
                    sc = _nt(qa, k_ref[0, rows, sl]) + wide(fq_ref[e]) - fk_ref[0, j, e:e + 1, :]
                    vv = v_ref[0, rows, sl]
                else:
                    k_cat = jnp.concatenate([kn_ref[e, rows, :], kr_ref[rows, :]], axis=1)
                    sc = _nt(qa, k_cat) * scale
                    vv = v_ref[e, rows, :]
                if masked:
                    sc = jnp.where(row >= col, sc, NEG_INF)
                m_new = jnp.maximum(m, jnp.max(sc, axis=1, keepdims=True))
                p = jnp.exp(sc - m_new)
                a = jnp.exp(m - m_new)
                p_hi = p.astype(BF16)
                if fox:
                    vv = jnp.concatenate([vv, ones], axis=1)
                    acc = a * acc + _nn(p_hi, vv) + _nn((p - p_hi.astype(F32)).astype(BF16), vv)
                else:
                    l = a * l + jnp.sum(p, axis=1, keepdims=True)
                    acc = a * acc + _nn(p_hi, vv)
                new.append((m_new, l, acc))
            return tuple(new)

        ones = jnp.ones((tq, half), BF16)
        acc_w = LANES if fox else dv
        init = (jnp.full((tq, 1), NEG_INF, F32), jnp.zeros((tq, 1), F32), jnp.zeros((tq, acc_w), F32))
        carry = step(i, (init, init), True)
        carry = lax.fori_loop(0, i, lambda j, c: step(j, c, False), carry)
        if fox:
            carry = [(m, acc[:, dv:dv + 1], acc[:, :dv]) for m, _, acc in carry]
        outs = [acc / l for _, l, acc in carry]
        for e, (m, l, _) in enumerate(carry):
            lse_ref[e] = jnp.broadcast_to(m + jnp.log(l), (tq, LANES))
        if fox:
            o32 = jnp.concatenate(outs, axis=1)
            o32_ref[0] = o32
            o_ref[0] = o32.astype(BF16)
        else:
            o_ref[0] = outs[0].astype(BF16)
            o_ref[1] = outs[1].astype(BF16)

    def q_idx(b, g, i):
        return (g, b * nq + i, 0)

    if fox:
        nk = fk.shape[1]
        in_specs = [pl.BlockSpec((1, tq, LANES), q_idx),
                    pl.BlockSpec((1, s, LANES), lambda b, g, i: (n_pair + g, b, 0)),
                    pl.BlockSpec((1, s, LANES), lambda b, g, i: (2 * n_pair + g, b, 0)),
                    pl.BlockSpec((2, tq, LANES), q_idx),
                    pl.BlockSpec((1, nk, 8, tq), lambda b, g, i: (b * n_pair + g, 0, 0, 0))]
        args = [qkv, qkv, qkv, fq, fk]
        o_spec = pl.BlockSpec((1, tq, LANES), q_idx)
    else:
        in_specs = [pl.BlockSpec((2, tq, LANES), q_idx),
                    pl.BlockSpec((1, tq, LANES), q_idx),
                    pl.BlockSpec((2, s, LANES), lambda b, g, i: (g, b, 0)),
                    pl.BlockSpec((s, LANES), lambda b, g, i: (b, 0)),
                    pl.BlockSpec((2, s, LANES), lambda b, g, i: (g, b, 0))]
        args = [qn, qr, kn, kr, v]
        o_spec = pl.BlockSpec((2, tq, LANES), q_idx)
    out_shape = [jax.ShapeDtypeStruct((8, t, LANES), BF16), jax.ShapeDtypeStruct((2 * n_pair, t, LANES), F32)]
    out_specs = [o_spec, pl.BlockSpec((2, tq, LANES), q_idx)]
    if fox:
        out_shape.append(jax.ShapeDtypeStruct((8, t, LANES), F32))
        out_specs.append(o_spec)
    outs = pl.pallas_call(
        body, name=name, out_shape=out_shape, grid=(bl, n_pair, nq), in_specs=in_specs, out_specs=out_specs,
        compiler_params=_params("arbitrary", "arbitrary", "arbitrary"),
    )(*args)
    return (outs[0], outs[1], outs[2] if fox else outs[0])


def attention_backward(kind, ops, o, do, lse, bl, scale, name):
    fox = kind == "fox"
    if fox:
        assert math.frexp(scale)[0] == 0.5, "the FoX scale is folded into bf16 queries: it must be a power of two"
        qkv, fq, fk = ops
        t = qkv.shape[1]
        n_pair = FOX_HEADS // 2
    else:
        qn, qr, kn, kr, v = ops
        t = qn.shape[1]
        n_pair = MLA_HEADS // 2
    s = t // bl
    tq = _attn_tiles(s)
    nq = s // tq
    half = LANES // 2

    def body(*refs):
        if fox:
            (q_ref, k_ref, v_ref, fq_ref, fk_ref, o_ref, do_ref, lse_ref,
             dq_ref, dk_ref, dv_ref, dfk_ref, delta_scr, qt_scr, dot_scr) = refs
        else:
            (qn_ref, qr_ref, kn_ref, kr_ref, v_ref, o_ref, do_ref, lse_ref,
             dqn_ref, dqr_ref, dkn_ref, dv_ref, dkr_ref, delta_scr, qt_scr, qrt_scr, dot_scr) = refs
        g, j = pl.program_id(1), pl.program_id(2)
        row = lax.broadcasted_iota(jnp.int32, (tq, tq), 0)
        col = lax.broadcasted_iota(jnp.int32, (tq, tq), 1)
        krows = pl.ds(pl.multiple_of(j * tq, tq), tq)
        q_scale = jnp.asarray(scale, BF16)

        def transposed(v):
            return v.astype(F32).T.astype(BF16)

        def wide(stat):
            return jnp.concatenate([stat] * (tq // LANES), axis=1)

        @pl.when(j == 0)
        def _():
            if fox:
                dq_ref[...] = jnp.zeros_like(dq_ref)
            else:
                dqn_ref[...] = jnp.zeros_like(dqn_ref)
                dqr_ref[...] = jnp.zeros_like(dqr_ref)
            for ii in range(nq):
                rws = slice(ii * tq, (ii + 1) * tq)
                deltas = []
                if fox:
                    prod = do_ref[0, rws, :].astype(F32) * o_ref[0, rws, :].astype(F32)
                    for e in range(2):
                        deltas.append(jnp.sum(prod[:, e * half:(e + 1) * half], axis=1, keepdims=True))
                    qt_scr[ii] = transposed(q_ref[0, rws, :] * q_scale)
                    dot_scr[ii] = transposed(do_ref[0, rws, :])
                else:
                    for e in range(2):
                        prod = do_ref[e, rws, :].astype(F32) * o_ref[e, rws, :].astype(F32)
                        deltas.append(jnp.sum(prod, axis=1, keepdims=True))
                        qt_scr[e, ii] = transposed(qn_ref[e, rws, :])
                        dot_scr[e, ii] = transposed(do_ref[e, rws, :])
                    qrt_scr[ii] = transposed(qr_ref[0, rws, :])
                for e in range(2):
                    delta_scr[e, rws, :] = jnp.broadcast_to(deltas[e], (tq, LANES))

        if fox:
            dfk_ref[...] = jnp.zeros_like(dfk_ref)
        else:
            @pl.when(jnp.logical_and(g == 0, j == 0))
            def _():
                dkr_ref[...] = jnp.zeros_like(dkr_ref)

        heads = []
        for e in range(2):
            sl = slice(e * half, (e + 1) * half)
            if fox:
                heads.append((sl, k_ref[0, :, sl], v_ref[0, :, sl], fk_ref[0, 0, e:e + 1, :]))
            else:
                heads.append((sl, jnp.concatenate([kn_ref[e], kr_ref[krows, :]], axis=1), v_ref[e], None))
        dk_w = dv_w = half if fox else LANES

        def step(i, carry, masked):
            rows = pl.ds(pl.multiple_of(i * tq, tq), tq)
            new = []
            for e, (sl, k_e, v_e, x_e) in enumerate(heads):
                dk_acc, dv_acc, last = carry[e]
                if fox:
                    do_i = do_ref[0, rows, sl]
                    sc = _nt(q_ref[0, rows, sl] * q_scale, k_e) + wide(fq_ref[e, rows, :]) - x_e
                else:
                    do_i = do_ref[e, rows, :]
                    q_cat = jnp.concatenate([qn_ref[e, rows, :], qr_ref[0, rows, sl], jnp.zeros((tq, half), BF16)], axis=1)
                    sc = _nt(q_cat, k_e) * scale
                if masked:
                    sc = jnp.where(row >= col, sc, NEG_INF)
                p = jnp.exp(sc - wide(lse_ref[e, rows, :]))
                dp = _nt(do_i, v_e)
                ds = p * (dp - wide(delta_scr[e, rows, :]))
                dsb = ds.astype(BF16) if fox else (ds * scale).astype(BF16)
                if fox:
                    fsl = slice(e * half, (e + 1) * half)
                    dv_acc = dv_acc + _nn(dot_scr[i, fsl, :], p.astype(BF16))
                    dk_acc = dk_acc + _nn(qt_scr[i, fsl, :], dsb)
                    dq_ref[0, rows, sl] += _nn(dsb, k_e) * scale
                    last = last - jnp.sum(ds, axis=0, keepdims=True)
                else:
                    dv_acc = dv_acc + _nn(dot_scr[e, i], p.astype(BF16))
                    dk_acc = dk_acc + _nn(qt_scr[e, i], dsb)
                    dq_cat = _nn(dsb, k_e)
                    dqn_ref[e, rows, :] += dq_cat[:, :LANES]
                    dqr_ref[0, rows, sl] += dq_cat[:, LANES:LANES + half]
                    last = last + _nn(qrt_scr[i, e * half:(e + 1) * half, :], dsb)
                new.append((dk_acc, dv_acc, last))
            return tuple(new)

        last0 = jnp.zeros((1, tq), F32) if fox else jnp.zeros((half, tq), F32)
        init = (jnp.zeros((dk_w, tq), F32), jnp.zeros((dv_w, tq), F32), last0)
        carry = step(j, (init, init), True)
        carry = lax.fori_loop(j + 1, nq, lambda i, c: step(i, c, False), carry)
        if fox:
            for e in range(2):
                dfk_ref[0, 0, e:e + 1, :] = carry[e][2]
            dk_ref[0] = jnp.concatenate([carry[0][0], carry[1][0]], axis=0).T.astype(BF16)
            dv_ref[0] = jnp.concatenate([carry[0][1], carry[1][1]], axis=0).T.astype(BF16)
        else:
            for e in range(2):
                dkn_ref[e] = carry[e][0].T.astype(BF16)
                dv_ref[e] = carry[e][1].T.astype(BF16)
            dkr_t = carry[0][2] + carry[1][2]
            dkr_ref[krows, :] += jnp.concatenate([dkr_t, jnp.zeros_like(dkr_t)], axis=0).T

    def whole(b, g, j):
        return (g, b, 0)

    def kblk(b, g, j):
        return (g, b * nq + j, 0)

    if fox:
        in_specs = [pl.BlockSpec((1, s, LANES), whole),
                    pl.BlockSpec((1, tq, LANES), lambda b, g, j: (n_pair + g, b * nq + j, 0)),
                    pl.BlockSpec((1, tq, LANES), lambda b, g, j: (2 * n_pair + g, b * nq + j, 0)),
                    pl.BlockSpec((2, s, LANES), whole),
                    pl.BlockSpec((1, 1, 8, tq), lambda b, g, j: (b * n_pair + g, j, 0, 0)),
                    pl.BlockSpec((1, s, LANES), whole), pl.BlockSpec((1, s, LANES), whole),
                    pl.BlockSpec((2, s, LANES), whole)]
        args = [qkv, qkv, qkv, fq, fk, o, do, lse]
        out_shape = [jax.ShapeDtypeStruct((8, t, LANES), F32), jax.ShapeDtypeStruct((8, t, LANES), BF16),
                     jax.ShapeDtypeStruct((8, t, LANES), BF16), jax.ShapeDtypeStruct(fk.shape, F32)]
        out_specs = [pl.BlockSpec((1, s, LANES), whole), pl.BlockSpec((1, tq, LANES), kblk),
                     pl.BlockSpec((1, tq, LANES), kblk),
                     pl.BlockSpec((1, 1, 8, tq), lambda b, g, j: (b * n_pair + g, j, 0, 0))]
    else:
        pair = pl.BlockSpec((2, s, LANES), whole)
        pair_k = pl.BlockSpec((2, tq, LANES), kblk)
        in_specs = [pair, pl.BlockSpec((1, s, LANES), whole), pair_k,
                    pl.BlockSpec((s, LANES), lambda b, g, j: (b, 0)), pair_k,
                    pair, pair, pair]
        args = [qn, qr, kn, kr, v, o, do, lse]
        out_shape = [jax.ShapeDtypeStruct((8, t, LANES), F32), jax.ShapeDtypeStruct((4, t, LANES), F32),
                     jax.ShapeDtypeStruct((8, t, LANES), BF16), jax.ShapeDtypeStruct((8, t, LANES), BF16),
                     jax.ShapeDtypeStruct((t, LANES), F32)]
        out_specs = [pair, pl.BlockSpec((1, s, LANES), whole), pair_k, pair_k,
                     pl.BlockSpec((s, LANES), lambda b, g, j: (b, 0))]
    t_blocks = pltpu.VMEM((nq, LANES, tq), BF16)
    t_pairs = pltpu.VMEM((2, nq, LANES, tq), BF16)
    scratch = [pltpu.VMEM((2, s, LANES), F32)] + ([t_blocks, t_blocks] if fox else [t_pairs, t_blocks, t_pairs])
    return pl.pallas_call(
        body, name=name, out_shape=out_shape, grid=(bl, n_pair, nq), in_specs=in_specs, out_specs=out_specs,
        scratch_shapes=scratch, compiler_params=_params("arbitrary", "arbitrary", "arbitrary"),
    )(*args)


def adamw(w, g, m, v, name):
    shape = w.shape
    c = shape[-1]
    r = w.size // c
    tr = _tile(r, 512, 8)

    def body(w_ref, g_ref, m_ref, v_ref, d_ref, nm_ref, nv_ref):
        gv = g_ref[...]
        m2 = ADAM_B1 * m_ref[...] + (1.0 - ADAM_B1) * gv
        v2 = ADAM_B2 * v_ref[...] + (1.0 - ADAM_B2) * (gv * gv)
        m_hat = m2 / (1.0 - ADAM_B1 ** ADAM_STEP)
        v_hat = v2 / (1.0 - ADAM_B2 ** ADAM_STEP)
        d_ref[...] = -ADAM_LR * (m_hat / (jnp.sqrt(v_hat) + ADAM_EPS) + ADAM_WD * w_ref[...])
        nm_ref[...] = m2
        nv_ref[...] = v2

    spec = pl.BlockSpec((tr, c), lambda i: (i, 0))
    outs = pl.pallas_call(
        body, name=name, out_shape=[jax.ShapeDtypeStruct((r, c), F32)] * 3, grid=(r // tr,),
        in_specs=[spec] * 4, out_specs=[spec] * 3, compiler_params=_params("arbitrary"),
    )(*(a.reshape(r, c) for a in (w, g, m, v)))
    return tuple(a.reshape(shape) for a in outs)


PACK_COLS = 1024


def _pack_rows(a):
    return a.reshape(-1, PACK_COLS)


def kernel(x, c, positions, mla_w_in, mla_g_q, mla_w_uq, mla_g_kv, mla_w_uk, mla_w_uv, mla_w_o, fox_w_in, fox_b_f, fox_w_o, ada_w, ada_b, ffn_w_gate, ffn_w_up, ffn_w_down, ln_g, ln_b, loss_target, m_mla_w_in, m_mla_g_q, m_mla_w_uq, m_mla_g_kv, m_mla_w_uk, m_mla_w_uv, m_mla_w_o, m_fox_w_in, m_fox_b_f, m_fox_w_o, m_ada_w, m_ada_b, m_ffn_w_gate, m_ffn_w_up, m_ffn_w_down, m_ln_g, m_ln_b, v_mla_w_in, v_mla_g_q, v_mla_w_uq, v_mla_g_kv, v_mla_w_uk, v_mla_w_uv, v_mla_w_o, v_fox_w_in, v_fox_b_f, v_fox_w_o, v_ada_w, v_ada_b, v_ffn_w_gate, v_ffn_w_up, v_ffn_w_down, v_ln_g, v_ln_b):
    bl, s, d = x.shape
    t = bl * s
    ff = ffn_w_gate.shape[-1] * N_DEV
    dev = 4 * lax.axis_index("x") + 2 * lax.axis_index("y") + lax.axis_index("c")
    ada_cols = ada_w.shape[-1]
    mla_in = mla_w_in.shape[-1]
    mla_in_pad = mla_in + (-mla_in) % LANES

    def t_last(a):
        return jnp.swapaxes(a, -1, -2)

    local = {
        "mla_w_in": mla_w_in[0],
        "mla_w_uq": t_last(mla_w_uq[0]),
        "mla_w_uk": t_last(mla_w_uk[0]),
        "mla_w_uv": t_last(mla_w_uv[0]),
        "mla_w_o": mla_w_o[0],
        "fox_w_in": t_last(fox_w_in[0]),
        "fox_w_o": fox_w_o[0],
    }
    for i in range(DEPTH):
        local.update({f"gate{i}": t_last(ffn_w_gate[i]), f"up{i}": t_last(ffn_w_up[i]), f"down{i}": ffn_w_down[i]})
    groups = [["mla_w_in", "mla_w_uq", "mla_w_uk", "mla_w_uv", "mla_w_o"],
              ["gate0", "up0", "down0"],
              ["fox_w_in", "fox_w_o"],
              ["gate1", "up1", "down1"]]
    offsets, rows_of, slot_of, group_of = {}, {}, {}, {}
    group_rows = []
    for gi, names in enumerate(groups):
        rows = 0
        for nm in names:
            rows_of[nm] = local[nm].size // PACK_COLS
            slot_of[nm] = rows_of[nm] + (-rows_of[nm]) % 16
            offsets[nm] = rows
            group_of[nm] = gi
            rows += slot_of[nm]
        group_rows.append(rows)

    def slot(nm, rows):
        pad = [(0, 0)] * rows.ndim
        pad[-2] = (0, slot_of[nm] - rows_of[nm])
        return jnp.pad(rows, pad)

    def held_until(block, arrays):
        zero = sum((a.reshape(-1)[0] * 0).astype(F32) for a in jax.tree.leaves(arrays))
        return block + zero.astype(block.dtype)

    def landing(block):
        land = lax.empty((N_DEV,) + block.shape, block.dtype)
        return lax.dynamic_update_slice(land, block[None], (dev, 0, 0))

    packed0 = jnp.concatenate([slot(nm, _pack_rows(local[nm]).astype(BF16)) for nm in groups[0]], axis=0)
    gathered0 = all_gather(packed0, "gather_mla_weights")
    gathered = {nm: gathered0[:, offsets[nm]:offsets[nm] + rows_of[nm], :] for nm in groups[0]}
    gather_started = [None] * len(groups)

    def depart(gi, after):
        blocks = [held_until(_pack_rows(local[nm]).astype(BF16), after) for nm in groups[gi]]
        gather_started[gi] = exchange_start(blocks, [landing(b) for b in blocks], f"gather_group{gi}_start", False)
        return gather_started[gi][4]

    def full(nm, cols):
        return gathered[nm].reshape(-1, cols)

    w_in = jnp.pad(full("mla_w_in", mla_in), ((0, 0), (0, mla_in_pad - mla_in)))
    wt_uq = full("mla_w_uq", MLA_QR).reshape(MLA_HEADS, MLA_NOPE + MLA_ROPE, MLA_QR)
    wt_uq_n = wt_uq[:, :MLA_NOPE].reshape(MLA_HEADS * MLA_NOPE, MLA_QR)
    wt_uq_r = wt_uq[:, MLA_NOPE:].reshape(MLA_HEADS * MLA_ROPE, MLA_QR)
    wt_uk = full("mla_w_uk", MLA_KVR)
    wt_uv = full("mla_w_uv", MLA_KVR)
    w_mo = full("mla_w_o", d)
    wt_gate, wt_up, w_down = [None] * DEPTH, [None] * DEPTH, [None] * DEPTH

    def arrive(gi, after):
        if gi + 1 < len(groups):
            after = depart(gi + 1, after)
        landed = list(exchange_wait(gather_started[gi], after, f"gather_group{gi}_wait", False))
        gathered.update(zip(groups[gi], landed))
        for i in range(DEPTH):
            if group_of[f"gate{i}"] == gi:
                wt_gate[i], wt_up[i], w_down[i] = full(f"gate{i}", d), full(f"up{i}", d), full(f"down{i}", d)

    small = jnp.concatenate([c.reshape(-1, LANES), ln_g.reshape(-1, LANES), ln_b.reshape(-1, LANES)], axis=0)
    small_rows = small.shape[0]
    small = jnp.pad(small, ((0, (-small_rows) % 8), (0, 0)))
    small_all = all_gather(small, "gather_small")
    c_rows = bl * d // LANES
    c_all = small_all[:, :c_rows].reshape(N_DEV * bl, d)
    n_ln = DEPTH * 2
    ln_g_all = small_all[:, c_rows:c_rows + n_ln, :].transpose(1, 0, 2).reshape(DEPTH, 2, 1, d)
    ln_b_all = small_all[:, c_rows + n_ln:c_rows + 2 * n_ln, :].transpose(1, 0, 2).reshape(DEPTH, 2, 1, d)

    c_act = silu_rows(c_all, "silu_c")
    ada_b_loc = lax.dynamic_slice_in_dim(ada_b, dev * ada_cols, ada_cols, axis=1)
    mod_cols = [mm([(c_act, ada_w[i])], trans_b=False, out_dtype=F32, name=f"ada_fwd{i}", bias=ada_b_loc[i][None, :])
                for i in range(DEPTH)]
    mod_all = all_gather(jnp.concatenate(mod_cols, axis=0), "gather_mod")
    mod_all = mod_all.reshape(N_DEV, DEPTH, N_DEV * bl, ada_cols).transpose(1, 2, 0, 3).reshape(DEPTH, N_DEV * bl, 6 * d)
    mod_mine = lax.dynamic_slice_in_dim(mod_all, dev * bl, bl, axis=1)
    mods = [mod_mine[i].reshape(bl * 6, 1, d) for i in range(DEPTH)]
    mods[0] = mods[0] + depart(1, (mod_mine, gathered0))[0, 0]

    half_r = MLA_ROPE // 2
    inv_freq = ROPE_THETA ** (-jnp.arange(half_r, dtype=F32) / half_r)
    inv_freq = jnp.tile(inv_freq, LANES // half_r)[None, :]
    sign = jnp.tile(jnp.concatenate([-jnp.ones((half_r,), F32), jnp.ones((half_r,), F32)]), LANES // MLA_ROPE)[None, :]
    cos_t, sin_t = rope_tables(positions.astype(F32).reshape(t, 1), inv_freq, sign, "rope_tables")

    x2d = x.reshape(t, d)
    g_q, g_kv = mla_g_q.reshape(1, MLA_QR), mla_g_kv.reshape(1, MLA_KVR)
    b_f = jnp.pad(fox_b_f.reshape(1, FOX_HEADS), ((0, 0), (0, LANES - FOX_HEADS)))
    mla_scale = (MLA_NOPE + MLA_ROPE) ** -0.5
    fox_scale = FOX_HD ** -0.5
    tq = _attn_tiles(s)
    nk = s // tq

    saved = []
    u = modulate(x2d, mods[0], 0, 1, bl, "modulate0")
    xin = x2d
    for i in range(DEPTH):
        sv = {"u": u, "x_in": xin}
        if i % 2 == 0:
            h_in = mm([(u, w_in)], trans_b=False, out_dtype=F32, name=f"mla_in{i}")
            c_q, c_kv, k_r = mla_latents_forward(h_in, g_q, g_kv, cos_t, sin_t, f"mla_latents{i}")
            q_n = mm([(c_q, wt_uq_n)], trans_b=True, out_dtype=BF16, out_slab=True, name=f"mla_qn{i}")
            q_r_raw = mm([(c_q, wt_uq_r)], trans_b=True, out_dtype=F32, out_slab=True, name=f"mla_qr{i}")
            q_r = rope_slabs(q_r_raw, cos_t, sin_t, BF16, f"mla_qrope{i}")
            k_n = mm([(c_kv, wt_uk)], trans_b=True, out_dtype=BF16, out_slab=True, name=f"mla_kn{i}")
            v_m = mm([(c_kv, wt_uv)], trans_b=True, out_dtype=BF16, out_slab=True, name=f"mla_v{i}")
            ops = (q_n, q_r, k_n, k_r, v_m)
            o, lse, o_delta = attention_forward("mla", ops, bl, mla_scale, f"mla_attn{i}")
            y = mm([(o, w_mo)], trans_b=False, out_dtype=F32, name=f"mla_out{i}")
            sv.update(h_in=h_in, c_q=c_q, c_kv=c_kv, ops=ops, o=o, lse=lse, o_delta=o_delta)
        else:
            arrive(2, u)
            wt_fox = full("fox_w_in", d)
            wt_qkv = wt_fox[:3 * d]
            wt_f = jnp.pad(wt_fox[3 * d:], ((0, LANES - FOX_HEADS), (0, 0)))
            w_fo = full("fox_w_o", d)
            qkv = mm([(u, wt_qkv)], trans_b=True, out_dtype=BF16, out_slab=True, name=f"fox_qkv{i}")
            z = mm([(u, wt_f)], trans_b=True, out_dtype=F32, name=f"fox_z{i}")
            f_tok, f_q = fox_gate_forward(z, b_f, bl, f"fox_gate{i}")
            f_k = f_tok[:, :FOX_HEADS].reshape(bl, nk, tq, FOX_HEADS // 2, 2).transpose(0, 3, 1, 4, 2)
            f_k = jnp.pad(f_k.reshape(bl * FOX_HEADS // 2, nk, 2, tq), ((0, 0), (0, 0), (0, 6), (0, 0)))
            ops = (qkv, f_q, f_k)
            o, lse, o_delta = attention_forward("fox", ops, bl, fox_scale, f"fox_attn{i}")
            y = mm([(o, w_fo)], trans_b=False, out_dtype=F32, name=f"fox_out{i}")
            sv.update(z=z, ops=ops, o=o, lse=lse, o_delta=o_delta)
        x1, r1, u2 = residual_layer_norm(xin, y, mods[i], 2, ln_g_all[i, 0], ln_b_all[i, 0], bl, f"ln_mix{i}",
                                         next_mod=(3, 4))
        if wt_gate[i] is None:
            arrive(group_of[f"gate{i}"], u2)
        a, bb, h = swiglu_in(u2, wt_gate[i], wt_up[i], f"ffn_in{i}")
        y2 = mm([(h, w_down[i])], trans_b=False, out_dtype=F32, name=f"ffn_down{i}")
        sv.update(y=y, r1=r1, u2=u2, a=a, bb=bb, h=h, y2=y2)
        if i + 1 < DEPTH:
            xin, r2, u = residual_layer_norm(x1, y2, mods[i], 5, ln_g_all[i, 1], ln_b_all[i, 1], bl, f"ln_ffn{i}",
                                             next_mod=(0, 1, mods[i + 1]))
        else:
            xin, r2 = residual_layer_norm(x1, y2, mods[i], 5, ln_g_all[i, 1], ln_b_all[i, 1], bl, f"ln_ffn{i}")
        sv.update(r2=r2)
        saved.append(sv)

    loss_cols, d_x = loss_head(xin, loss_target.reshape(t, d), "loss_head")

    grads_full = {}
    wgrad = functools.partial(mm_tn, out_dtype=BF16)
    dmod = [[None] * 6 for _ in range(DEPTH)]
    dg_ln = [[None, None] for _ in range(DEPTH)]
    db_ln = [[None, None] for _ in range(DEPTH)]
    dg_q = dg_kv = db_f = None
    d_a, du = d_x, None
    scatter_started = [None] * len(groups)

    def scatter_start(gi, after=None):
        gs = [grads_full[nm].reshape(N_DEV, rows_of[nm], PACK_COLS).astype(BF16) for nm in groups[gi]]
        if gi == 0:
            gs = [jnp.concatenate([slot(nm, g) for nm, g in zip(groups[gi], gs)], axis=1)]
        if after is not None:
            gs = [held_until(g, after) for g in gs]
        lands = [landing(lax.dynamic_index_in_dim(g, dev, 0, keepdims=False)) for g in gs]
        scatter_started[gi] = exchange_start(gs, lands, f"scatter_group{gi}_start", True)

    ln_g_bwd = [[ln_g_all[i, k] for k in range(2)] for i in range(DEPTH)]
    for i in reversed(range(DEPTH)):
        sv = saved[i]
        if i + 1 < DEPTH:
            gi = group_of["fox_w_in"]
            scatter_start(gi)
            ln_g_bwd[i][1] = after_token(ln_g_bwd[i][1], scatter_started[gi])
        ln2 = (sv["r2"], sv["y2"], ln_g_bwd[i][1], ln_b_all[i, 1], (mods[i], 5))
        if du is None:
            bw = sublayer_backward(d_a, bl, f"bwd_ln_ffn{i}", ln=ln2)
        else:
            bw = sublayer_backward(d_a, bl, f"bwd_ln_ffn{i}", du=du, scale=(mods[i + 1], 1), ln=ln2)
            dmod[i + 1][0], dmod[i + 1][1] = bw["dshift"], bw["dscale"]
        dmod[i][5], dg_ln[i][1], db_ln[i][1] = bw["dgate"], bw["dg"], bw["db"]
        dy2 = bw["dy"]
        da, dbb = swiglu_out_backward(dy2, w_down[i], sv["a"], sv["bb"], f"bwd_ffn_act{i}")
        du2 = mm([(da, wt_gate[i]), (dbb, wt_up[i])], trans_b=False, out_dtype=F32, name=f"bwd_ffn_du{i}")
        grads_full[f"down{i}"] = wgrad(sv["h"], dy2, name=f"bwd_w_down{i}")
        grads_full[f"gate{i}"] = wgrad(da, sv["u2"], name=f"bwd_w_gate{i}")
        grads_full[f"up{i}"] = wgrad(dbb, sv["u2"], name=f"bwd_w_up{i}")
        gi = group_of[f"gate{i}"]
        scatter_start(gi)
        ln_g_bwd[i][0] = after_token(ln_g_bwd[i][0], scatter_started[gi])
        bw = sublayer_backward(bw["dx"], bl, f"bwd_ln_mix{i}", du=du2, scale=(mods[i], 4),
                               ln=(sv["r1"], sv["y"], ln_g_bwd[i][0], ln_b_all[i, 0], (mods[i], 2)))
        dmod[i][3], dmod[i][4], dmod[i][2] = bw["dshift"], bw["dscale"], bw["dgate"]
        dg_ln[i][0], db_ln[i][0] = bw["dg"], bw["db"]
        d_a, dy = bw["dx"], bw["dy"]
        o, lse, ops = sv["o"], sv["lse"], sv["ops"]
        if i % 2 == 0:
            do = mm([(dy, w_mo)], trans_b=True, out_dtype=BF16, out_slab=True, name=f"bwd_mla_do{i}")
            grads_full["mla_w_o"] = wgrad(o, dy, name=f"bwd_w_mla_o{i}")
            dqn, dqr, dkn, dvm, dkr = attention_backward("mla", ops, sv["o_delta"], do, lse, bl, mla_scale,
                                                         f"bwd_mla_attn{i}")
            dqr = rope_slabs(dqr, cos_t, sin_t, F32, f"bwd_mla_qrope{i}", transposed=True)
            dcq = mm([(dqn, wt_uq_n), (dqr, wt_uq_r)], trans_b=False, out_dtype=F32, name=f"bwd_mla_dcq{i}")
            dckv = mm([(dkn, wt_uk), (dvm, wt_uv)], trans_b=False, out_dtype=F32, name=f"bwd_mla_dckv{i}")
            d_uq_n = wgrad(dqn, sv["c_q"], name=f"bwd_w_uq_n{i}").reshape(MLA_HEADS, MLA_NOPE, MLA_QR)
            d_uq_r = wgrad(dqr, sv["c_q"], name=f"bwd_w_uq_r{i}").reshape(MLA_HEADS, MLA_ROPE, MLA_QR)
            grads_full["mla_w_uq"] = jnp.concatenate([d_uq_n, d_uq_r], axis=1)
            grads_full["mla_w_uk"] = wgrad(dkn, sv["c_kv"], name=f"bwd_w_uk{i}")
            grads_full["mla_w_uv"] = wgrad(dvm, sv["c_kv"], name=f"bwd_w_uv{i}")
            dh_in, dg_q, dg_kv = mla_latents_backward(sv["h_in"], dcq, dckv, dkr, g_q, g_kv, cos_t, sin_t,
                                                      f"bwd_mla_latents{i}")
            du = mm([(dh_in, w_in)], trans_b=True, out_dtype=F32, name=f"bwd_mla_du{i}")
            grads_full["mla_w_in"] = wgrad(sv["u"], dh_in, name=f"bwd_w_mla_in{i}")[:, :mla_in]
        else:
            do = mm([(dy, w_fo)], trans_b=True, out_dtype=BF16, out_slab=True, name=f"bwd_fox_do{i}")
            grads_full["fox_w_o"] = wgrad(o, dy, name=f"bwd_w_fox_o{i}")
            dq, dk, dvf, dfk = attention_backward("fox", ops, sv["o_delta"], do, lse, bl, fox_scale, f"bwd_fox_attn{i}")
            df = dfk[:, :, :2, :].reshape(bl, FOX_HEADS // 2, nk, 2, tq).transpose(0, 2, 4, 1, 3).reshape(t, FOX_HEADS)
            df = jnp.pad(df, ((0, 0), (0, LANES - FOX_HEADS)))
            dz, db_f = fox_gate_backward(sv["z"], b_f, df, bl, f"bwd_fox_gate{i}")
            du = mm([(dq, wt_fox[0:d]), (dk, wt_fox[d:2 * d]), (dvf, wt_fox[2 * d:3 * d]), (dz, wt_f)],
                    trans_b=False, out_dtype=F32, name=f"bwd_fox_du{i}")
            u_f = sv["u"]
            grads_full["fox_w_in"] = jnp.concatenate(
                [wgrad(dq, u_f, name=f"bwd_w_fox_q{i}"), wgrad(dk, u_f, name=f"bwd_w_fox_k{i}"),
                 wgrad(dvf, u_f, name=f"bwd_w_fox_v{i}"), wgrad(dz, u_f, name=f"bwd_w_fox_f{i}")[:FOX_HEADS]], axis=0)
    scatter_start(0)
    bw = sublayer_backward(d_a, bl, "bwd_input", du=du, scale=(after_token(mods[0], scatter_started[0]), 1), x_in=x2d)
    dmod[0][0], dmod[0][1] = bw["dshift"], bw["dscale"]
    grad_x = bw["dx"].reshape(bl, s, d)

    dmod_rows = jnp.concatenate([r.reshape(bl, d) for layer in dmod for r in layer], axis=0)
    dmod_rows = dmod_rows.reshape(DEPTH, 6, bl, d).transpose(0, 2, 1, 3)
    n_mod = dmod_rows.size // LANES
    ln_parts = [dg_ln[i][k] for i in range(DEPTH) for k in range(2)] + [db_ln[i][k] for i in range(DEPTH) for k in range(2)]
    small_g = jnp.concatenate([dmod_rows.reshape(-1, LANES), dg_q.reshape(-1, LANES), dg_kv.reshape(-1, LANES), db_f]
                              + [p.reshape(-1, LANES) for p in ln_parts] + [loss_cols.reshape(-1, LANES)], axis=0)
    n_small = small_g.shape[0]
    small_g = jnp.pad(small_g, ((0, (-n_small) % 8), (0, 0)))
    small_g_all = all_gather(small_g, "gather_small_grads")
    small_sum = sum_leading(small_g_all, "sum_small_grads")
    per_seq = DEPTH * 6 * d // LANES
    dmod_all = small_g_all[:, :n_mod].reshape(N_DEV, DEPTH, bl, 6 * d).transpose(1, 0, 2, 3)
    dmod_all = dmod_all.reshape(DEPTH, N_DEV * bl, 6 * d)
    o1 = n_mod
    grad_g_q = small_sum[o1:o1 + MLA_QR // LANES].reshape(1, MLA_QR)
    o1 += MLA_QR // LANES
    grad_g_kv = small_sum[o1:o1 + MLA_KVR // LANES].reshape(1, MLA_KVR)
    o1 += MLA_KVR // LANES
    grad_b_f = small_sum[o1:o1 + 1, :FOX_HEADS]
    o1 += 1
    n_ln_rows = DEPTH * 2 * d // LANES
    grad_ln_g_full = small_sum[o1:o1 + n_ln_rows].reshape(DEPTH, 2, d)
    grad_ln_b_full = small_sum[o1 + n_ln_rows:o1 + 2 * n_ln_rows].reshape(DEPTH, 2, d)
    loss = jnp.sum(small_sum[o1 + 2 * n_ln_rows:o1 + 2 * n_ln_rows + d // LANES])
    shard = d // N_DEV
    grad_ln_g = lax.dynamic_slice_in_dim(grad_ln_g_full, dev * shard, shard, axis=2)
    grad_ln_b = lax.dynamic_slice_in_dim(grad_ln_b_full, dev * shard, shard, axis=2)
    by_seq = small_g_all[:, :n_mod].reshape(N_DEV, DEPTH, bl, 6 * d // LANES, LANES).transpose(0, 2, 1, 3, 4)
    grad_ada_b = sum_leading(by_seq.reshape(N_DEV * bl, per_seq, LANES), "sum_ada_b").reshape(DEPTH, 6 * d)
    dmod_cols = lax.dynamic_slice_in_dim(dmod_all, dev * ada_cols, ada_cols, axis=2)
    grad_ada_w = jnp.stack([mm_tn(c_act, dmod_cols[i], name=f"bwd_w_ada{i}") for i in range(DEPTH)])

    g_mine = {}

    def scatter_arrive(gi, after):
        landed = exchange_wait(scatter_started[gi], after, f"scatter_group{gi}_wait", True)
        if gi == 0:
            total = sum_leading(landed[0], f"scatter_group{gi}_sum")
            g_mine.update({nm: total[offsets[nm]:offsets[nm] + rows_of[nm]] for nm in groups[gi]})
            return total
        for nm, land in zip(groups[gi], landed):
            g_mine[nm] = sum_leading(land, f"scatter_sum_{nm}")
        return g_mine[groups[gi][-1]]

    after = scatter_started[0][4]
    for gi in reversed(range(1, len(groups))):
        after = scatter_arrive(gi, after)

    def mine(nm, shape):
        return g_mine[nm].reshape(shape)

    def shard_t(nm, a):
        return mine(nm, t_last(a).shape)

    transposed = {"mla_w_uq", "mla_w_uk", "mla_w_uv", "fox_w_in", "ffn_w_gate", "ffn_w_up"}
    grads = {
        "mla_w_in": lambda: mine("mla_w_in", mla_w_in[0].shape)[None],
        "mla_g_q": lambda: grad_g_q,
        "mla_w_uq": lambda: shard_t("mla_w_uq", mla_w_uq[0])[None],
        "mla_g_kv": lambda: grad_g_kv,
        "mla_w_uk": lambda: shard_t("mla_w_uk", mla_w_uk[0])[None],
        "mla_w_uv": lambda: shard_t("mla_w_uv", mla_w_uv[0])[None],
        "mla_w_o": lambda: mine("mla_w_o", mla_w_o[0].shape)[None],
        "fox_w_in": lambda: shard_t("fox_w_in", fox_w_in[0])[None],
        "fox_b_f": lambda: grad_b_f,
        "fox_w_o": lambda: mine("fox_w_o", fox_w_o[0].shape)[None],
        "ada_w": lambda: grad_ada_w,
        "ada_b": lambda: grad_ada_b,
        "ffn_w_gate": lambda: jnp.stack([shard_t(f"gate{i}", ffn_w_gate[i]) for i in range(DEPTH)]),
        "ffn_w_up": lambda: jnp.stack([shard_t(f"up{i}", ffn_w_up[i]) for i in range(DEPTH)]),
        "ffn_w_down": lambda: jnp.stack([mine(f"down{i}", ffn_w_down[i].shape) for i in range(DEPTH)]),
        "ln_g": lambda: grad_ln_g,
        "ln_b": lambda: grad_ln_b,
    }
    weights = dict(mla_w_in=mla_w_in, mla_g_q=mla_g_q, mla_w_uq=mla_w_uq, mla_g_kv=mla_g_kv, mla_w_uk=mla_w_uk,
                   mla_w_uv=mla_w_uv, mla_w_o=mla_w_o, fox_w_in=fox_w_in, fox_b_f=fox_b_f, fox_w_o=fox_w_o,
                   ada_w=ada_w, ada_b=ada_b, ffn_w_gate=ffn_w_gate, ffn_w_up=ffn_w_up, ffn_w_down=ffn_w_down,
                   ln_g=ln_g, ln_b=ln_b)
    first = dict(mla_w_in=m_mla_w_in, mla_g_q=m_mla_g_q, mla_w_uq=m_mla_w_uq, mla_g_kv=m_mla_g_kv, mla_w_uk=m_mla_w_uk,
                 mla_w_uv=m_mla_w_uv, mla_w_o=m_mla_w_o, fox_w_in=m_fox_w_in, fox_b_f=m_fox_b_f, fox_w_o=m_fox_w_o,
                 ada_w=m_ada_w, ada_b=m_ada_b, ffn_w_gate=m_ffn_w_gate, ffn_w_up=m_ffn_w_up, ffn_w_down=m_ffn_w_down,
                 ln_g=m_ln_g, ln_b=m_ln_b)
    second = dict(mla_w_in=v_mla_w_in, mla_g_q=v_mla_g_q, mla_w_uq=v_mla_w_uq, mla_g_kv=v_mla_g_kv, mla_w_uk=v_mla_w_uk,
                  mla_w_uv=v_mla_w_uv, mla_w_o=v_mla_w_o, fox_w_in=v_fox_w_in, fox_b_f=v_fox_b_f, fox_w_o=v_fox_w_o,
                  ada_w=v_ada_w, ada_b=v_ada_b, ffn_w_gate=v_ffn_w_gate, ffn_w_up=v_ffn_w_up, ffn_w_down=v_ffn_w_down,
                  ln_g=v_ln_g, ln_b=v_ln_b)
    order = list(weights)
    last = [nm for nm in order if group_of.get(nm) == 0]
    updated = {}
    for nm in [nm for nm in order if nm not in last] + last:
        if last and nm == last[0]:
            scatter_arrive(0, after)
        lay = t_last if nm in transposed else (lambda a: a)
        w = lay(weights[nm])
        g = grads[nm]().reshape(w.shape)
        delta, new_m, new_v = adamw(w, g, lay(first[nm]), lay(second[nm]), f"adamw_{nm}")
        updated[nm] = (lay(g), lay(delta), lay(new_m), lay(new_v))
        after = new_v
    return (loss, grad_x, *(updated[nm][k] for k in range(4) for nm in order))
```

```python
import functools
import math

import jax
import jax.numpy as jnp
from jax import lax
from jax.experimental import pallas as pl
from jax.experimental.pallas import tpu as pltpu

F32 = jnp.float32
BF16 = jnp.bfloat16
LANES = 128
N_DEV = 8
VMEM_LIMIT_BYTES = 56 * 1024 * 1024

DEPTH = 2
MLA_HEADS = 8
MLA_NOPE = 128
MLA_ROPE = 64
MLA_V = 128
MLA_QR = 256
MLA_KVR = 256
ROPE_THETA = 10000.0
FOX_HEADS = 16
FOX_HD = 64
ALPHA = (2.0 * DEPTH) ** 0.25
NORM_EPS = 1e-5
ADAM_LR = 0.001
ADAM_B1 = 0.9
ADAM_B2 = 0.999
ADAM_EPS = 1e-08
ADAM_WD = 0.01
ADAM_STEP = 10

MESH = pl.DeviceIdType.MESH


def _params(*sem):
    return pltpu.CompilerParams(dimension_semantics=sem, vmem_limit_bytes=VMEM_LIMIT_BYTES)


def _tile(n, cap, mult=LANES):
    if n <= cap:
        return n
    best = None
    for t in range(mult, cap + 1, mult):
        if n % t == 0:
            best = t
    assert best is not None, (n, cap, mult)
    return best


def _dot(a, b, dims):
    return lax.dot_general(a, b, (dims, ((), ())), preferred_element_type=F32)


def _nn(a, b):
    return _dot(a, b, ((1,), (0,)))


def _nt(a, b):
    return _dot(a, b, ((1,), (1,)))


def _tn(a, b):
    return _dot(a, b, ((0,), (0,)))


def _me():
    return lax.axis_index("x"), lax.axis_index("y"), lax.axis_index("c")


def all_gather(x_loc, name):
    r, c = x_loc.shape

    def body(x_ref, out_ref, send_sems, recv_sems, local_sem):
        x, y, cc = _me()
        me, sibling = (x, y, cc), (x, y, 1 - cc)
        chips = [(1 - x, y), (x, 1 - y), (1 - x, 1 - y)]

        def rows(px, py, pc):
            return out_ref.at[4 * px + 2 * py + pc]

        def copy(k, block, to, src=None):
            return pltpu.make_async_remote_copy(
                src_ref=rows(*block) if src is None else src, dst_ref=rows(*block),
                send_sem=send_sems.at[k], recv_sem=recv_sems.at[k], device_id=to, device_id_type=MESH)

        mine = pltpu.make_async_copy(x_ref, rows(*me), local_sem)
        mine.start()
        first = [copy(0, me, sibling, src=x_ref)]
        first += [copy(1 + j, me, (*chip, cc), src=x_ref) for j, chip in enumerate(chips)]
        for cp in first:
            cp.start()
        passed = [copy(4 + j, (*chip, cc), sibling) for j, chip in enumerate(chips)]
        for j, chip in enumerate(chips):
            copy(1 + j, (*chip, cc), me).wait_recv()
            passed[j].start()
        copy(0, sibling, me).wait_recv()
        for j, chip in enumerate(chips):
            copy(4 + j, (*chip, 1 - cc), me).wait_recv()
        for cp in first + passed:
            cp.wait_send()
        mine.wait()

    return pl.pallas_call(
        body, name=name,
        out_shape=jax.ShapeDtypeStruct((N_DEV, r, c), x_loc.dtype),
        in_specs=[pl.BlockSpec(memory_space=pl.ANY)],
        out_specs=pl.BlockSpec(memory_space=pl.ANY),
        scratch_shapes=[pltpu.SemaphoreType.DMA((7,)), pltpu.SemaphoreType.DMA((7,)), pltpu.SemaphoreType.DMA(())],
    )(x_loc)


HBM_SPEC = pl.BlockSpec(memory_space=pltpu.HBM)
SEM_SPEC = pl.BlockSpec(memory_space=pltpu.SEMAPHORE)
N_PEERS = N_DEV - 1


def _peer(k):
    x, y, c = _me()
    return (1 - x if k & 4 else x, 1 - y if k & 2 else y, 1 - c if k & 1 else c)


def _exchange_copies(src_refs, land_refs, send_sems, recv_sems, scatter):
    x, y, c = _me()
    mine = 4 * x + 2 * y + c
    copies = []
    for n, (src_ref, land_ref) in enumerate(zip(src_refs, land_refs)):
        for k in range(1, N_DEV):
            px, py, pc = _peer(k)
            src = src_ref.at[4 * px + 2 * py + pc] if scatter else src_ref
            sem = n * N_PEERS + k - 1
            copies.append(pltpu.make_async_remote_copy(
                src_ref=src, dst_ref=land_ref.at[mine], send_sem=send_sems.at[sem], recv_sem=recv_sems.at[sem],
                device_id=(px, py, pc), device_id_type=MESH))
    return copies


def exchange_start(srcs, lands, name, scatter):
    n = len(srcs)

    def body(*refs):
        send_sems, recv_sems = refs[2 * n], refs[2 * n + 1]
        for cp in _exchange_copies(refs[:n], refs[n:2 * n], send_sems, recv_sems, scatter):
            cp.start()
        token = refs[-1]
        token[...] = jnp.zeros_like(token)

    outs = pl.pallas_call(
        body, name=name,
        out_shape=(pltpu.SemaphoreType.DMA((n * N_PEERS,)), pltpu.SemaphoreType.DMA((n * N_PEERS,)),
                   *(pltpu.HBM(a.shape, a.dtype) for a in (*srcs, *lands)), jax.ShapeDtypeStruct((8, LANES), F32)),
        in_specs=(HBM_SPEC,) * (2 * n),
        out_specs=(SEM_SPEC, SEM_SPEC, *((HBM_SPEC,) * (2 * n)), pl.BlockSpec(memory_space=pltpu.VMEM)),
        input_output_aliases={i: 2 + i for i in range(2 * n)},
        compiler_params=pltpu.CompilerParams(has_side_effects=pltpu.SideEffectType.DATAFLOW_SIDE_EFFECTING),
    )(*(pltpu.with_memory_space_constraint(a, pltpu.HBM) for a in (*srcs, *lands)))
    return outs[0], outs[1], outs[2:2 + n], outs[2 + n:2 + 2 * n], outs[-1]


def exchange_wait(started, after, name, scatter):
    send_sems, recv_sems, srcs, lands, _ = started
    n = len(srcs)

    def body(*refs):
        send_sems, recv_sems = refs[2 * n], refs[2 * n + 1]
        for cp in _exchange_copies(refs[:n], refs[n:2 * n], send_sems, recv_sems, scatter):
            cp.wait_send()
            cp.wait_recv()

    outs = pl.pallas_call(
        body, name=name,
        out_shape=tuple(pltpu.HBM(a.shape, a.dtype) for a in (*srcs, *lands)),
        in_specs=(*((HBM_SPEC,) * (2 * n)), SEM_SPEC, SEM_SPEC, pl.BlockSpec(memory_space=pl.ANY)),
        out_specs=(HBM_SPEC,) * (2 * n), input_output_aliases={i: i for i in range(2 * n)},
        compiler_params=pltpu.CompilerParams(has_side_effects=pltpu.SideEffectType.DATAFLOW_SIDE_EFFECTING),
    )(*srcs, *lands, send_sems, recv_sems, after)
    return outs[n:]


def after_token(small, started):
    return small + started[4][0, 0]


def sum_leading(x, name):
    n, r, c = x.shape
    tr = _tile(r, 512, 16)

    def body(x_ref, o_ref):
        acc = x_ref[0].astype(F32)
        for k in range(1, n):
            acc = acc + x_ref[k].astype(F32)
        o_ref[...] = acc

    return pl.pallas_call(
        body, name=name,
        out_shape=jax.ShapeDtypeStruct((r, c), F32),
        grid=(r // tr,),
        in_specs=[pl.BlockSpec((n, tr, c), lambda i: (0, i, 0))],
        out_specs=pl.BlockSpec((tr, c), lambda i: (i, 0)),
        compiler_params=_params("arbitrary"),
    )(x)


MM_VMEM_BUDGET = 36 * 1024 * 1024
GRID_STEP_AS_BYTES = 1 << 20


def _mm_tiles(m, n, a_row_bytes, b_col_bytes, out_bytes):
    tms = [c for c in (2048, 1024, 512, 256, 128, 64, 32, 16, 8) if m % c == 0] or [m]
    tns = [c for c in range(LANES, min(n, 2048) + 1, LANES) if n % c == 0] or [n]
    best = None
    for tm in tms:
        for tn in tns:
            vmem = 2 * (tm * a_row_bytes + tn * b_col_bytes) + 2 * tm * tn * out_bytes + tm * tn * 4
            if vmem > MM_VMEM_BUDGET:
                continue
            steps = (m // tm) * (n // tn)
            cost = steps * GRID_STEP_AS_BYTES + (m // tm) * n * b_col_bytes + m * a_row_bytes
            if best is None or cost < best[0]:
                best = (cost, tm, tn)
    assert best is not None, (m, n, a_row_bytes, b_col_bytes)
    return best[1], best[2]


def mm(pairs, *, trans_b, out_dtype, name, out_slab=False, bias=None):
    a0 = pairs[0][0]
    m = a0.shape[1] if a0.ndim == 3 else a0.shape[0]
    n = pairs[0][1].shape[0] if trans_b else pairs[0][1].shape[1]
    a_row_bytes = sum((b.shape[1] if trans_b else b.shape[0]) * a.dtype.itemsize for a, b in pairs)
    b_col_bytes = sum((b.shape[1] if trans_b else b.shape[0]) * b.dtype.itemsize for _, b in pairs)
    tm, tn = _mm_tiles(m, n, a_row_bytes, b_col_bytes, jnp.dtype(out_dtype).itemsize)
    slabs = [a.ndim == 3 for a, _ in pairs]
    n_pairs = len(pairs)

    def body(*refs):
        o_ref = refs[-1]
        acc = bias_ref = None
        if bias is not None:
            bias_ref = refs[2 * n_pairs]
        for i in range(n_pairs):
            a_ref, b_ref = refs[2 * i], refs[2 * i + 1]
            if slabs[i]:
                a = jnp.concatenate([a_ref[s].astype(BF16) for s in range(a_ref.shape[0])], axis=1)
            else:
                a = a_ref[...].astype(BF16)
            b = b_ref[...].astype(BF16)
            part = _nt(a, b) if trans_b else _nn(a, b)
            acc = part if acc is None else acc + part
        if bias_ref is not None:
            acc = acc + bias_ref[...]
        if out_slab:
            for s in range(tn // LANES):
                o_ref[s] = acc[:, s * LANES:(s + 1) * LANES].astype(out_dtype)
        else:
            o_ref[...] = acc.astype(out_dtype)

    in_specs, args = [], []
    for (a, b), slab in zip(pairs, slabs):
        if slab:
            in_specs.append(pl.BlockSpec((a.shape[0], tm, LANES), lambda i, j: (0, i, 0)))
        else:
            in_specs.append(pl.BlockSpec((tm, a.shape[1]), lambda i, j: (i, 0)))
        if trans_b:
            in_specs.append(pl.BlockSpec((tn, b.shape[1]), lambda i, j: (j, 0)))
        else:
            in_specs.append(pl.BlockSpec((b.shape[0], tn), lambda i, j: (0, j)))
        args += [a, b]
    if bias is not None:
        in_specs.append(pl.BlockSpec((1, tn), lambda i, j: (0, j)))
        args.append(bias)
    if out_slab:
        out_shape = jax.ShapeDtypeStruct((n // LANES, m, LANES), out_dtype)
        out_spec = pl.BlockSpec((tn // LANES, tm, LANES), lambda i, j: (j, i, 0))
    else:
        out_shape = jax.ShapeDtypeStruct((m, n), out_dtype)
        out_spec = pl.BlockSpec((tm, tn), lambda i, j: (i, j))
    return pl.pallas_call(
        body, name=name, out_shape=out_shape, grid=(m // tm, n // tn),
        in_specs=in_specs, out_specs=out_spec,
        compiler_params=_params("arbitrary", "arbitrary"),
    )(*args)


def mm_tn(a, b, *, name, out_dtype=F32, tk_cap=1536, tn_cap=1024, tm_cap=1024):
    slab = a.ndim == 3
    m = a.shape[1] if slab else a.shape[0]
    k = a.shape[0] * LANES if slab else a.shape[1]
    n = b.shape[1]
    tk = _tile(k, tk_cap)
    tn = _tile(n, tn_cap)
    tm = _tile(m, tm_cap, 8)
    n_steps = m // tm

    def body(a_ref, b_ref, o_ref, acc_ref):
        step = pl.program_id(2)

        @pl.when(step == 0)
        def _():
            acc_ref[...] = jnp.zeros_like(acc_ref)

        bb = b_ref[...].astype(BF16)
        if slab:
            for s in range(tk // LANES):
                acc_ref[s * LANES:(s + 1) * LANES, :] += _tn(a_ref[s].astype(BF16), bb)
        else:
            acc_ref[...] += _tn(a_ref[...].astype(BF16), bb)

        @pl.when(step == n_steps - 1)
        def _():
            o_ref[...] = acc_ref[...].astype(out_dtype)

    if slab:
        a_spec = pl.BlockSpec((tk // LANES, tm, LANES), lambda i, j, t: (i, t, 0))
    else:
        a_spec = pl.BlockSpec((tm, tk), lambda i, j, t: (t, i))
    return pl.pallas_call(
        body, name=name, out_shape=jax.ShapeDtypeStruct((k, n), out_dtype), grid=(k // tk, n // tn, n_steps),
        in_specs=[a_spec, pl.BlockSpec((tm, tn), lambda i, j, t: (t, j))],
        out_specs=pl.BlockSpec((tk, tn), lambda i, j, t: (i, j)),
        scratch_shapes=[pltpu.VMEM((tk, tn), F32)],
        compiler_params=_params("arbitrary", "arbitrary", "arbitrary"),
    )(a, b)


def _row_spec(d, k):
    return pl.BlockSpec((1, 1, d), lambda b, i: (6 * b + k, 0, 0))


def modulate(x, mod, k_shift, k_scale, bl, name):
    t, d = x.shape
    s = t // bl
    tm = _tile(s, 512, 8)
    nt = s // tm

    def body(x_ref, sh_ref, sc_ref, o_ref):
        o_ref[...] = (x_ref[...] * (1.0 + sc_ref[0]) + sh_ref[0]).astype(BF16)

    return pl.pallas_call(
        body, name=name, out_shape=jax.ShapeDtypeStruct((t, d), BF16), grid=(bl, nt),
        in_specs=[pl.BlockSpec((tm, d), lambda b, i: (b * nt + i, 0)), _row_spec(d, k_shift), _row_spec(d, k_scale)],
        out_specs=pl.BlockSpec((tm, d), lambda b, i: (b * nt + i, 0)),
        compiler_params=_params("arbitrary", "arbitrary"),
    )(x, mod, mod)


def _layer_norm_stats(r):
    mu = jnp.mean(r, axis=-1, keepdims=True)
    rc = r - mu
    var = jnp.mean(rc * rc, axis=-1, keepdims=True)
    rstd = lax.rsqrt(var + NORM_EPS)
    return rc * rstd, rstd


def residual_layer_norm(x, y, mod, k_gate, g, b, bl, name, next_mod=None):
    t, d = x.shape
    s = t // bl
    tm = _tile(s, 1024, 8)
    nt = s // tm
    has_next = next_mod is not None

    def body(*refs):
        x_ref, y_ref, gt_ref, g_ref, b_ref = refs[:5]
        rest = refs[5:]
        if has_next:
            sh_ref, sc_ref, o_ref, r_ref, u_ref = rest
        else:
            o_ref, r_ref = rest
        r = ALPHA * x_ref[...] + (1.0 + gt_ref[0]) * y_ref[...]
        xhat, _ = _layer_norm_stats(r)
        out = xhat * g_ref[...] + b_ref[...]
        o_ref[...] = out
        r_ref[...] = r
        if has_next:
            u_ref[...] = (out * (1.0 + sc_ref[0]) + sh_ref[0]).astype(BF16)

    tok = pl.BlockSpec((tm, d), lambda bb, i: (bb * nt + i, 0))
    vec = pl.BlockSpec((1, d), lambda bb, i: (0, 0))
    in_specs = [tok, tok, _row_spec(d, k_gate), vec, vec]
    args = [x, y, mod, g, b]
    out_shape = [jax.ShapeDtypeStruct((t, d), F32), jax.ShapeDtypeStruct((t, d), F32)]
    out_specs = [tok, tok]
    if has_next:
        in_specs += [_row_spec(d, next_mod[0]), _row_spec(d, next_mod[1])]
        args += [mod if len(next_mod) == 2 else next_mod[2]] * 2
        out_shape.append(jax.ShapeDtypeStruct((t, d), BF16))
        out_specs.append(tok)
    return pl.pallas_call(
        body, name=name, out_shape=out_shape, grid=(bl, nt), in_specs=in_specs, out_specs=out_specs,
        compiler_params=_params("arbitrary", "arbitrary"),
    )(*args)


def loss_head(xo, target, name):
    t, d = xo.shape
    tm = _tile(t, 512, 8)

    def body(x_ref, t_ref, l_ref, dx_ref):
        @pl.when(pl.program_id(0) == 0)
        def _():
            l_ref[...] = jnp.zeros_like(l_ref)

        e = x_ref[...] - t_ref[...]
        l_ref[...] += jnp.sum(e * e, axis=0, keepdims=True) * (0.5 / d)
        dx_ref[...] = e * (1.0 / d)

    tok = pl.BlockSpec((tm, d), lambda i: (i, 0))
    return pl.pallas_call(
        body, name=name,
        out_shape=[jax.ShapeDtypeStruct((1, d), F32), jax.ShapeDtypeStruct((t, d), F32)],
        grid=(t // tm,), in_specs=[tok, tok],
        out_specs=[pl.BlockSpec((1, d), lambda i: (0, 0)), tok],
        compiler_params=_params("arbitrary"),
    )(xo, target)


def sublayer_backward(d_a, bl, name, *, du=None, scale=None, x_in=None, ln=None):
    t, d = d_a.shape
    s = t // bl
    tm = _tile(s, 512, 8)
    nt = s // tm
    has_mod = du is not None
    has_ln = ln is not None
    assert has_mod or has_ln
    assert has_ln or x_in is not None

    def body(*refs):
        refs = list(refs)
        da_ref = refs.pop(0)
        if has_mod:
            du_ref, sc_ref = refs.pop(0), refs.pop(0)
        if has_ln:
            r_ref, y_ref, g_ref, b_ref, gt_ref = (refs.pop(0) for _ in range(5))
        elif has_mod:
            xin_ref = refs.pop(0)
        dx_ref = refs.pop(0)
        if has_ln:
            dy_ref, dg_ref, db_ref, dgt_ref = (refs.pop(0) for _ in range(4))
        if has_mod:
            dsc_ref, dsh_ref = refs.pop(0), refs.pop(0)
        first_tile = pl.program_id(1) == 0
        first_step = jnp.logical_and(pl.program_id(0) == 0, first_tile)

        dout = da_ref[...]
        if has_ln:
            xhat, rstd = _layer_norm_stats(r_ref[...])
        if has_mod:
            duv = du_ref[...]
            dout = dout + duv * (1.0 + sc_ref[0])
            xin = xhat * g_ref[...] + b_ref[...] if has_ln else xin_ref[...]

            @pl.when(first_tile)
            def _():
                dsc_ref[...] = jnp.zeros_like(dsc_ref)
                dsh_ref[...] = jnp.zeros_like(dsh_ref)

            dsc_ref[0] += jnp.sum(duv * xin, axis=0, keepdims=True)
            dsh_ref[0] += jnp.sum(duv, axis=0, keepdims=True)
        if not has_ln:
            dx_ref[...] = dout
            return

        @pl.when(first_step)
        def _():
            dg_ref[...] = jnp.zeros_like(dg_ref)
            db_ref[...] = jnp.zeros_like(db_ref)

        @pl.when(first_tile)
        def _():
            dgt_ref[...] = jnp.zeros_like(dgt_ref)

        dg_ref[...] += jnp.sum(dout * xhat, axis=0, keepdims=True)
        db_ref[...] += jnp.sum(dout, axis=0, keepdims=True)
        dxh = dout * g_ref[...]
        dr = rstd * (dxh - jnp.mean(dxh, axis=-1, keepdims=True) - xhat * jnp.mean(dxh * xhat, axis=-1, keepdims=True))
        dx_ref[...] = ALPHA * dr
        dy_ref[...] = ((1.0 + gt_ref[0]) * dr).astype(BF16)
        dgt_ref[0] += jnp.sum(dr * y_ref[...], axis=0, keepdims=True)

    tok = pl.BlockSpec((tm, d), lambda bb, i: (bb * nt + i, 0))
    vec = pl.BlockSpec((1, d), lambda bb, i: (0, 0))
    seq = pl.BlockSpec((1, 1, d), lambda bb, i: (bb, 0, 0))
    in_specs, args = [tok], [d_a]
    if has_mod:
        in_specs += [tok, _row_spec(d, scale[1])]
        args += [du, scale[0]]
    if has_ln:
        r, y, g, b, gate = ln
        in_specs += [tok, tok, vec, vec, _row_spec(d, gate[1])]
        args += [r, y, g, b, gate[0]]
    elif has_mod:
        in_specs.append(tok)
        args.append(x_in)
    names = ["dx"]
    out_shape, out_specs = [jax.ShapeDtypeStruct((t, d), F32)], [tok]
    if has_ln:
        names += ["dy", "dg", "db", "dgate"]
        out_shape += [jax.ShapeDtypeStruct((t, d), BF16), jax.ShapeDtypeStruct((1, d), F32),
                      jax.ShapeDtypeStruct((1, d), F32), jax.ShapeDtypeStruct((bl, 1, d), F32)]
        out_specs += [tok, vec, vec, seq]
    if has_mod:
        names += ["dscale", "dshift"]
        out_shape += [jax.ShapeDtypeStruct((bl, 1, d), F32)] * 2
        out_specs += [seq, seq]
    outs = pl.pallas_call(
        body, name=name, out_shape=out_shape, grid=(bl, nt), in_specs=in_specs, out_specs=out_specs,
        compiler_params=_params("arbitrary", "arbitrary"),
    )(*args)
    return dict(zip(names, outs))


def _silu(a):
    return a * jax.nn.sigmoid(a)


def silu_rows(a, name):
    def body(a_ref, o_ref):
        o_ref[...] = _silu(a_ref[...]).astype(BF16)

    return pl.pallas_call(body, name=name, out_shape=jax.ShapeDtypeStruct(a.shape, BF16))(a)


def _swiglu_tiles(t, f):
    return _tile(t, 1024, 8), _tile(f, 1536)


def swiglu_in(u, wt_gate, wt_up, name):
    t, d = u.shape
    f = wt_gate.shape[0]
    tm, tf = _swiglu_tiles(t, f)

    def body(u_ref, g_ref, w_ref, a_ref, b_ref, h_ref):
        uv = u_ref[...]
        a = _nt(uv, g_ref[...])
        b = _nt(uv, w_ref[...])
        a_ref[...] = a.astype(BF16)
        b_ref[...] = b.astype(BF16)
        h_ref[...] = (_silu(a) * b).astype(BF16)

    w_spec = pl.BlockSpec((tf, d), lambda i, j: (j, 0))
    o_spec = pl.BlockSpec((tm, tf), lambda i, j: (i, j))
    return pl.pallas_call(
        body, name=name,
        out_shape=[jax.ShapeDtypeStruct((t, f), BF16)] * 3,
        grid=(t // tm, f // tf), in_specs=[pl.BlockSpec((tm, d), lambda i, j: (i, 0)), w_spec, w_spec],
        out_specs=[o_spec, o_spec, o_spec], compiler_params=_params("arbitrary", "arbitrary"),
    )(u, wt_gate, wt_up)


def swiglu_out_backward(dy, w_down, a, b, name):
    t, d = dy.shape
    f = w_down.shape[0]
    tm, tf = _swiglu_tiles(t, f)

    def body(dy_ref, w_ref, a_ref, b_ref, da_ref, db_ref):
        dh = _nt(dy_ref[...], w_ref[...])
        av = a_ref[...].astype(F32)
        sig = jax.nn.sigmoid(av)
        da_ref[...] = (dh * b_ref[...].astype(F32) * (sig * (1.0 + av * (1.0 - sig)))).astype(BF16)
        db_ref[...] = (dh * (av * sig)).astype(BF16)

    spec = pl.BlockSpec((tm, tf), lambda i, j: (i, j))
    return pl.pallas_call(
        body, name=name, out_shape=[jax.ShapeDtypeStruct((t, f), BF16)] * 2, grid=(t // tm, f // tf),
        in_specs=[pl.BlockSpec((tm, d), lambda i, j: (i, 0)), pl.BlockSpec((tf, d), lambda i, j: (j, 0)), spec, spec],
        out_specs=[spec, spec], compiler_params=_params("arbitrary", "arbitrary"),
    )(dy, w_down, a, b)


def rope_tables(pos, inv_freq, sign, name):
    t = pos.shape[0]
    tm = _tile(t, 512, 8)

    def body(p_ref, f_ref, s_ref, c_out, s_out):
        ang = p_ref[...] * f_ref[...]
        c_out[...] = jnp.cos(ang)
        s_out[...] = jnp.sin(ang) * s_ref[...]

    vec = pl.BlockSpec((1, LANES), lambda i: (0, 0))
    tab = pl.BlockSpec((tm, LANES), lambda i: (i, 0))
    return pl.pallas_call(
        body, name=name, out_shape=[jax.ShapeDtypeStruct((t, LANES), F32)] * 2, grid=(t // tm,),
        in_specs=[pl.BlockSpec((tm, 1), lambda i: (i, 0)), vec, vec], out_specs=[tab, tab],
        compiler_params=_params("arbitrary"),
    )(pos, inv_freq, sign)


def _rot_half(v):
    lane = lax.broadcasted_iota(jnp.int32, v.shape, v.ndim - 1)
    up = pltpu.roll(v, LANES - MLA_ROPE // 2, v.ndim - 1)
    down = pltpu.roll(v, MLA_ROPE // 2, v.ndim - 1)
    return jnp.where(lane % MLA_ROPE < MLA_ROPE // 2, up, down)


def _rope(v, cos, sin_signed):
    return v * cos + _rot_half(v) * sin_signed


def _rope_transposed(dv, cos, sin_signed):
    return dv * cos + _rot_half(dv * sin_signed)


def rope_slabs(v, cos, sin_signed, out_dtype, name, transposed=False):
    ns, t, _ = v.shape
    tm = _tile(t, 1024, 8)
    fn = _rope_transposed if transposed else _rope

    def body(v_ref, c_ref, s_ref, o_ref):
        for j in range(ns):
            o_ref[j] = fn(v_ref[j].astype(F32), c_ref[...], s_ref[...]).astype(out_dtype)

    tab = pl.BlockSpec((tm, LANES), lambda i: (i, 0))
    spec = pl.BlockSpec((ns, tm, LANES), lambda i: (0, i, 0))
    return pl.pallas_call(
        body, name=name, out_shape=jax.ShapeDtypeStruct(v.shape, out_dtype), grid=(t // tm,),
        in_specs=[spec, tab, tab], out_specs=spec, compiler_params=_params("arbitrary"),
    )(v, cos, sin_signed)


def _rms(x):
    rinv = lax.rsqrt(jnp.mean(x * x, axis=-1, keepdims=True) + NORM_EPS)
    return x * rinv, rinv


def mla_latents_forward(h_in, g_q, g_kv, cos, sin_signed, name):
    t = h_in.shape[0]
    tm = _tile(t, 512, 8)

    def body(h_ref, gq_ref, gkv_ref, c_ref, s_ref, cq_ref, ckv_ref, kr_ref):
        cq_ref[...] = (_rms(h_ref[:, 0:MLA_QR])[0] * gq_ref[...]).astype(BF16)
        ckv_ref[...] = (_rms(h_ref[:, MLA_QR:MLA_QR + MLA_KVR])[0] * gkv_ref[...]).astype(BF16)
        kr_ref[...] = _rope(h_ref[:, MLA_QR + MLA_KVR:], c_ref[...], s_ref[...]).astype(BF16)

    def tok(w):
        return pl.BlockSpec((tm, w), lambda i: (i, 0))

    def vec(w):
        return pl.BlockSpec((1, w), lambda i: (0, 0))

    return pl.pallas_call(
        body, name=name,
        out_shape=[jax.ShapeDtypeStruct((t, MLA_QR), BF16), jax.ShapeDtypeStruct((t, MLA_KVR), BF16),
                   jax.ShapeDtypeStruct((t, LANES), BF16)],
        grid=(t // tm,),
        in_specs=[tok(h_in.shape[1]), vec(MLA_QR), vec(MLA_KVR), tok(LANES), tok(LANES)],
        out_specs=[tok(MLA_QR), tok(MLA_KVR), tok(LANES)],
        compiler_params=_params("arbitrary"),
    )(h_in, g_q, g_kv, cos, sin_signed)


def mla_latents_backward(h_in, dcq, dckv, dkr, g_q, g_kv, cos, sin_signed, name):
    t, w = h_in.shape
    tm = _tile(t, 512, 8)

    def body(h_ref, dcq_ref, dckv_ref, dkr_ref, gq_ref, gkv_ref, c_ref, s_ref, dh_ref, dgq_ref, dgkv_ref):
        @pl.when(pl.program_id(0) == 0)
        def _():
            dgq_ref[...] = jnp.zeros_like(dgq_ref)
            dgkv_ref[...] = jnp.zeros_like(dgkv_ref)

        def rms_bwd(x, dc, g_ref, dg_ref):
            xn, rinv = _rms(x)
            dg_ref[...] += jnp.sum(dc * xn, axis=0, keepdims=True)
            dxn = dc * g_ref[...]
            return rinv * (dxn - xn * jnp.mean(dxn * xn, axis=-1, keepdims=True))

        dq = rms_bwd(h_ref[:, 0:MLA_QR], dcq_ref[...], gq_ref, dgq_ref)
        dkv = rms_bwd(h_ref[:, MLA_QR:MLA_QR + MLA_KVR], dckv_ref[...], gkv_ref, dgkv_ref)
        dr = _rope_transposed(dkr_ref[...], c_ref[...], s_ref[...])
        dh_ref[...] = jnp.concatenate([dq, dkv, dr], axis=1).astype(BF16)

    def tok(ww):
        return pl.BlockSpec((tm, ww), lambda i: (i, 0))

    def vec(ww):
        return pl.BlockSpec((1, ww), lambda i: (0, 0))

    return pl.pallas_call(
        body, name=name,
        out_shape=[jax.ShapeDtypeStruct((t, w), BF16), jax.ShapeDtypeStruct((1, MLA_QR), F32),
                   jax.ShapeDtypeStruct((1, MLA_KVR), F32)],
        grid=(t // tm,),
        in_specs=[tok(w), tok(MLA_QR), tok(MLA_KVR), tok(LANES), vec(MLA_QR), vec(MLA_KVR), tok(LANES), tok(LANES)],
        out_specs=[tok(w), vec(MLA_QR), vec(MLA_KVR)],
        compiler_params=_params("arbitrary"),
    )(h_in, dcq, dckv, dkr, g_q, g_kv, cos, sin_signed)


def _tri(n, lower):
    r = lax.broadcasted_iota(jnp.int32, (n, n), 0)
    c = lax.broadcasted_iota(jnp.int32, (n, n), 1)
    return jnp.where(r >= c if lower else r <= c, 1.0, 0.0).astype(F32)


def _dot_exact(tri, v):
    hi = v.astype(BF16)
    mid = (v - hi.astype(F32)).astype(BF16)
    lo = (v - hi.astype(F32) - mid.astype(F32)).astype(BF16)
    t = tri.astype(BF16)
    return _nn(t, hi) + _nn(t, mid) + _nn(t, lo)


def fox_gate_forward(z, b_f, bl, name):
    t = z.shape[0]
    s = t // bl
    ch = LANES
    n_ch = s // ch

    def body(z_ref, b_ref, f_ref, fs_ref):
        tri = _tri(ch, True)
        carry = jnp.zeros((1, LANES), F32)
        for k in range(n_ch):
            x = z_ref[k * ch:(k + 1) * ch, :] + b_ref[...]
            logf = jnp.minimum(x, 0.0) - jnp.log(1.0 + jnp.exp(-jnp.abs(x)))
            cs = _dot_exact(tri, logf) + carry
            carry = cs[ch - 1:ch, :]
            f_ref[k * ch:(k + 1) * ch, :] = cs
            for h in range(FOX_HEADS):
                fs_ref[h, k * ch:(k + 1) * ch, :] = jnp.broadcast_to(cs[:, h:h + 1], (ch, LANES))

    return pl.pallas_call(
        body, name=name,
        out_shape=[jax.ShapeDtypeStruct((t, LANES), F32), jax.ShapeDtypeStruct((FOX_HEADS, t, LANES), F32)],
        grid=(bl,),
        in_specs=[pl.BlockSpec((s, LANES), lambda b: (b, 0)), pl.BlockSpec((1, LANES), lambda b: (0, 0))],
        out_specs=[pl.BlockSpec((s, LANES), lambda b: (b, 0)),
                   pl.BlockSpec((FOX_HEADS, s, LANES), lambda b: (0, b, 0))],
        compiler_params=_params("arbitrary"),
    )(z, b_f)


def fox_gate_backward(z, b_f, df, bl, name):
    t = z.shape[0]
    s = t // bl
    ch = LANES
    n_ch = s // ch

    def body(z_ref, b_ref, df_ref, dz_ref, db_ref):
        @pl.when(pl.program_id(0) == 0)
        def _():
            db_ref[...] = jnp.zeros_like(db_ref)

        tri = _tri(ch, False)
        carry = jnp.zeros((1, LANES), F32)
        for k in reversed(range(n_ch)):
            cs = _dot_exact(tri, df_ref[k * ch:(k + 1) * ch, :]) + carry
            carry = cs[0:1, :]
            x = z_ref[k * ch:(k + 1) * ch, :] + b_ref[...]
            dz = cs * (1.0 - jax.nn.sigmoid(x))
            dz_ref[k * ch:(k + 1) * ch, :] = dz
            db_ref[...] += jnp.sum(dz, axis=0, keepdims=True)

    tok = pl.BlockSpec((s, LANES), lambda b: (b, 0))
    vec = pl.BlockSpec((1, LANES), lambda b: (0, 0))
    return pl.pallas_call(
        body, name=name,
        out_shape=[jax.ShapeDtypeStruct((t, LANES), F32), jax.ShapeDtypeStruct((1, LANES), F32)],
        grid=(bl,), in_specs=[tok, vec, tok], out_specs=[tok, vec],
        compiler_params=_params("arbitrary"),
    )(z, b_f, df)


NEG_INF = float("-inf")


def _attn_tiles(s):
    return _tile(s, 1024, 8)


def attention_forward(kind, ops, bl, scale, name):
    fox = kind == "fox"
    if fox:
        assert math.frexp(scale)[0] == 0.5, "the FoX scale is folded into bf16 queries: it must be a power of two"
        qkv, fq, fk = ops
        t = qkv.shape[1]
        n_pair = FOX_HEADS // 2
    else:
        qn, qr, kn, kr, v = ops
        t = qn.shape[1]
        n_pair = MLA_HEADS // 2
    s = t // bl
    tq = _attn_tiles(s)
    nq = s // tq
    half = LANES // 2

    def body(*refs):
        if fox:
            q_ref, k_ref, v_ref, fq_ref, fk_ref, o_ref, lse_ref, o32_ref = refs
        else:
            qn_ref, qr_ref, kn_ref, kr_ref, v_ref, o_ref, lse_ref = refs
        i = pl.program_id(2)
        row = lax.broadcasted_iota(jnp.int32, (tq, tq), 0)
        col = lax.broadcasted_iota(jnp.int32, (tq, tq), 1)
        heads = []
        for e in range(2):
            sl = slice(e * half, (e + 1) * half)
            if fox:
                heads.append((sl, q_ref[0, :, sl] * jnp.asarray(scale, BF16), None))
            else:
                heads.append((sl, jnp.concatenate([qn_ref[e], qr_ref[0, :, sl], jnp.zeros((tq, half), BF16)], axis=1),
                              None))
        dv = half if fox else LANES

        def wide(stat):
            return jnp.concatenate([stat] * (tq // LANES), axis=1)

        def step(j, carry, masked):
            rows = pl.ds(pl.multiple_of(j * tq, tq), tq)
            new = []
            for e, (sl, qa, qb) in enumerate(heads):
                m, l, acc = carry[e]
                if fox:
                    sc = _nt(qa, k_ref[0, rows, sl]) + wide(fq_ref[e]) - fk_ref[0, j, e:e + 1, :]
                    vv = v_ref[0, rows, sl]
                else:
                    k_cat = jnp.concatenate([kn_ref[e, rows, :], kr_ref[rows, :]], axis=1)
                    sc = _nt(qa, k_cat) * scale
                    vv = v_ref[e, rows, :]
                if masked:
                    sc = jnp.where(row >= col, sc, NEG_INF)
                m_new = jnp.maximum(m, jnp.max(sc, axis=1, keepdims=True))
                p = jnp.exp(sc - m_new)
                a = jnp.exp(m - m_new)
                p_hi = p.astype(BF16)
                if fox:
                    vv = jnp.concatenate([vv, ones], axis=1)
                    acc = a * acc + _nn(p_hi, vv) + _nn((p - p_hi.astype(F32)).astype(BF16), vv)
                else:
                    l = a * l + jnp.sum(p, axis=1, keepdims=True)
                    acc = a * acc + _nn(p_hi, vv)
                new.append((m_new, l, acc))
            return tuple(new)

        ones = jnp.ones((tq, half), BF16)
        acc_w = LANES if fox else dv
        init = (jnp.full((tq, 1), NEG_INF, F32), jnp.zeros((tq, 1), F32), jnp.zeros((tq, acc_w), F32))
        carry = step(i, (init, init), True)
        carry = lax.fori_loop(0, i, lambda j, c: step(j, c, False), carry)
        if fox:
            carry = [(m, acc[:, dv:dv + 1], acc[:, :dv]) for m, _, acc in carry]
        outs = [acc / l for _, l, acc in carry]
        for e, (m, l, _) in enumerate(carry):
            lse_ref[e] = jnp.broadcast_to(m + jnp.log(l), (tq, LANES))
        if fox:
            o32 = jnp.concatenate(outs, axis=1)
            o32_ref[0] = o32
            o_ref[0] = o32.astype(BF16)
        else:
            o_ref[0] = outs[0].astype(BF16)
            o_ref[1] = outs[1].astype(BF16)

    def q_idx(b, g, i):
        return (g, b * nq + i, 0)

    if fox:
        nk = fk.shape[1]
        in_specs = [pl.BlockSpec((1, tq, LANES), q_idx),
                    pl.BlockSpec((1, s, LANES), lambda b, g, i: (n_pair + g, b, 0)),
                    pl.BlockSpec((1, s, LANES), lambda b, g, i: (2 * n_pair + g, b, 0)),
                    pl.BlockSpec((2, tq, LANES), q_idx),
                    pl.BlockSpec((1, nk, 8, tq), lambda b, g, i: (b * n_pair + g, 0, 0, 0))]
        args = [qkv, qkv, qkv, fq, fk]
        o_spec = pl.BlockSpec((1, tq, LANES), q_idx)
    else:
        in_specs = [pl.BlockSpec((2, tq, LANES), q_idx),
                    pl.BlockSpec((1, tq, LANES), q_idx),
                    pl.BlockSpec((2, s, LANES), lambda b, g, i: (g, b, 0)),
                    pl.BlockSpec((s, LANES), lambda b, g, i: (b, 0)),
                    pl.BlockSpec((2, s, LANES), lambda b, g, i: (g, b, 0))]
        args = [qn, qr, kn, kr, v]
        o_spec = pl.BlockSpec((2, tq, LANES), q_idx)
    out_shape = [jax.ShapeDtypeStruct((8, t, LANES), BF16), jax.ShapeDtypeStruct((2 * n_pair, t, LANES), F32)]
    out_specs = [o_spec, pl.BlockSpec((2, tq, LANES), q_idx)]
    if fox:
        out_shape.append(jax.ShapeDtypeStruct((8, t, LANES), F32))
        out_specs.append(o_spec)
    outs = pl.pallas_call(
        body, name=name, out_shape=out_shape, grid=(bl, n_pair, nq), in_specs=in_specs, out_specs=out_specs,
        compiler_params=_params("arbitrary", "arbitrary", "arbitrary"),
    )(*args)
    return (outs[0], outs[1], outs[2] if fox else outs[0])


def attention_backward(kind, ops, o, do, lse, bl, scale, name):
    fox = kind == "fox"
    if fox:
        assert math.frexp(scale)[0] == 0.5, "the FoX scale is folded into bf16 queries: it must be a power of two"
        qkv, fq, fk = ops
        t = qkv.shape[1]
        n_pair = FOX_HEADS // 2
    else:
        qn, qr, kn, kr, v = ops
        t = qn.shape[1]
        n_pair = MLA_HEADS // 2
    s = t // bl
    tq = _attn_tiles(s)
    nq = s // tq
    half = LANES // 2

    def body(*refs):
        if fox:
            (q_ref, k_ref, v_ref, fq_ref, fk_ref, o_ref, do_ref, lse_ref,
             dq_ref, dk_ref, dv_ref, dfk_ref, delta_scr, qt_scr, dot_scr) = refs
        else:
            (qn_ref, qr_ref, kn_ref, kr_ref, v_ref, o_ref, do_ref, lse_ref,
             dqn_ref, dqr_ref, dkn_ref, dv_ref, dkr_ref, delta_scr, qt_scr, qrt_scr, dot_scr) = refs
        g, j = pl.program_id(1), pl.program_id(2)
        row = lax.broadcasted_iota(jnp.int32, (tq, tq), 0)
        col = lax.broadcasted_iota(jnp.int32, (tq, tq), 1)
        krows = pl.ds(pl.multiple_of(j * tq, tq), tq)
        q_scale = jnp.asarray(scale, BF16)

        def transposed(v):
            return v.astype(F32).T.astype(BF16)

        def wide(stat):
            return jnp.concatenate([stat] * (tq // LANES), axis=1)

        @pl.when(j == 0)
        def _():
            if fox:
                dq_ref[...] = jnp.zeros_like(dq_ref)
            else:
                dqn_ref[...] = jnp.zeros_like(dqn_ref)
                dqr_ref[...] = jnp.zeros_like(dqr_ref)
            for ii in range(nq):
                rws = slice(ii * tq, (ii + 1) * tq)
                deltas = []
                if fox:
                    prod = do_ref[0, rws, :].astype(F32) * o_ref[0, rws, :].astype(F32)
                    for e in range(2):
                        deltas.append(jnp.sum(prod[:, e * half:(e + 1) * half], axis=1, keepdims=True))
                    qt_scr[ii] = transposed(q_ref[0, rws, :] * q_scale)
                    dot_scr[ii] = transposed(do_ref[0, rws, :])
                else:
                    for e in range(2):
                        prod = do_ref[e, rws, :].astype(F32) * o_ref[e, rws, :].astype(F32)
                        deltas.append(jnp.sum(prod, axis=1, keepdims=True))
                        qt_scr[e, ii] = transposed(qn_ref[e, rws, :])
                        dot_scr[e, ii] = transposed(do_ref[e, rws, :])
                    qrt_scr[ii] = transposed(qr_ref[0, rws, :])
                for e in range(2):
                    delta_scr[e, rws, :] = jnp.broadcast_to(deltas[e], (tq, LANES))

        if fox:
            dfk_ref[...] = jnp.zeros_like(dfk_ref)
        else:
            @pl.when(jnp.logical_and(g == 0, j == 0))
            def _():
                dkr_ref[...] = jnp.zeros_like(dkr_ref)

        heads = []
        for e in range(2):
            sl = slice(e * half, (e + 1) * half)
            if fox:
                heads.append((sl, k_ref[0, :, sl], v_ref[0, :, sl], fk_ref[0, 0, e:e + 1, :]))
            else:
                heads.append((sl, jnp.concatenate([kn_ref[e], kr_ref[krows, :]], axis=1), v_ref[e], None))
        dk_w = dv_w = half if fox else LANES

        def step(i, carry, masked):
            rows = pl.ds(pl.multiple_of(i * tq, tq), tq)
            new = []
            for e, (sl, k_e, v_e, x_e) in enumerate(heads):
                dk_acc, dv_acc, last = carry[e]
                if fox:
                    do_i = do_ref[0, rows, sl]
                    sc = _nt(q_ref[0, rows, sl] * q_scale, k_e) + wide(fq_ref[e, rows, :]) - x_e
                else:
                    do_i = do_ref[e, rows, :]
                    q_cat = jnp.concatenate([qn_ref[e, rows, :], qr_ref[0, rows, sl], jnp.zeros((tq, half), BF16)], axis=1)
                    sc = _nt(q_cat, k_e) * scale
                if masked:
                    sc = jnp.where(row >= col, sc, NEG_INF)
                p = jnp.exp(sc - wide(lse_ref[e, rows, :]))
                dp = _nt(do_i, v_e)
                ds = p * (dp - wide(delta_scr[e, rows, :]))
                dsb = ds.astype(BF16) if fox else (ds * scale).astype(BF16)
                if fox:
                    fsl = slice(e * half, (e + 1) * half)
                    dv_acc = dv_acc + _nn(dot_scr[i, fsl, :], p.astype(BF16))
                    dk_acc = dk_acc + _nn(qt_scr[i, fsl, :], dsb)
                    dq_ref[0, rows, sl] += _nn(dsb, k_e) * scale
                    last = last - jnp.sum(ds, axis=0, keepdims=True)
                else:
                    dv_acc = dv_acc + _nn(dot_scr[e, i], p.astype(BF16))
                    dk_acc = dk_acc + _nn(qt_scr[e, i], dsb)
                    dq_cat = _nn(dsb, k_e)
                    dqn_ref[e, rows, :] += dq_cat[:, :LANES]
                    dqr_ref[0, rows, sl] += dq_cat[:, LANES:LANES + half]
                    last = last + _nn(qrt_scr[i, e * half:(e + 1) * half, :], dsb)
                new.append((dk_acc, dv_acc, last))
            return tuple(new)

        last0 = jnp.zeros((1, tq), F32) if fox else jnp.zeros((half, tq), F32)
        init = (jnp.zeros((dk_w, tq), F32), jnp.zeros((dv_w, tq), F32), last0)
        carry = step(j, (init, init), True)
        carry = lax.fori_loop(j + 1, nq, lambda i, c: step(i, c, False), carry)
        if fox:
            for e in range(2):
                dfk_ref[0, 0, e:e + 1, :] = carry[e][2]
            dk_ref[0] = jnp.concatenate([carry[0][0], carry[1][0]], axis=0).T.astype(BF16)
            dv_ref[0] = jnp.concatenate([carry[0][1], carry[1][1]], axis=0).T.astype(BF16)
        else:
            for e in range(2):
                dkn_ref[e] = carry[e][0].T.astype(BF16)
                dv_ref[e] = carry[e][1].T.astype(BF16)
            dkr_t = carry[0][2] + carry[1][2]
            dkr_ref[krows, :] += jnp.concatenate([dkr_t, jnp.zeros_like(dkr_t)], axis=0).T

    def whole(b, g, j):
        return (g, b, 0)

    def kblk(b, g, j):
        return (g, b * nq + j, 0)

    if fox:
        in_specs = [pl.BlockSpec((1, s, LANES), whole),
                    pl.BlockSpec((1, tq, LANES), lambda b, g, j: (n_pair + g, b * nq + j, 0)),
                    pl.BlockSpec((1, tq, LANES), lambda b, g, j: (2 * n_pair + g, b * nq + j, 0)),
                    pl.BlockSpec((2, s, LANES), whole),
                    pl.BlockSpec((1, 1, 8, tq), lambda b, g, j: (b * n_pair + g, j, 0, 0)),
                    pl.BlockSpec((1, s, LANES), whole), pl.BlockSpec((1, s, LANES), whole),
                    pl.BlockSpec((2, s, LANES), whole)]
        args = [qkv, qkv, qkv, fq, fk, o, do, lse]
        out_shape = [jax.ShapeDtypeStruct((8, t, LANES), F32), jax.ShapeDtypeStruct((8, t, LANES), BF16),
                     jax.ShapeDtypeStruct((8, t, LANES), BF16), jax.ShapeDtypeStruct(fk.shape, F32)]
        out_specs = [pl.BlockSpec((1, s, LANES), whole), pl.BlockSpec((1, tq, LANES), kblk),
                     pl.BlockSpec((1, tq, LANES), kblk),
                     pl.BlockSpec((1, 1, 8, tq), lambda b, g, j: (b * n_pair + g, j, 0, 0))]
    else:
        pair = pl.BlockSpec((2, s, LANES), whole)
        pair_k = pl.BlockSpec((2, tq, LANES), kblk)
        in_specs = [pair, pl.BlockSpec((1, s, LANES), whole), pair_k,
                    pl.BlockSpec((s, LANES), lambda b, g, j: (b, 0)), pair_k,
                    pair, pair, pair]
        args = [qn, qr, kn, kr, v, o, do, lse]
        out_shape = [jax.ShapeDtypeStruct((8, t, LANES), F32), jax.ShapeDtypeStruct((4, t, LANES), F32),
                     jax.ShapeDtypeStruct((8, t, LANES), BF16), jax.ShapeDtypeStruct((8, t, LANES), BF16),
                     jax.ShapeDtypeStruct((t, LANES), F32)]
        out_specs = [pair, pl.BlockSpec((1, s, LANES), whole), pair_k, pair_k,
                     pl.BlockSpec((s, LANES), lambda b, g, j: (b, 0))]
    t_blocks = pltpu.VMEM((nq, LANES, tq), BF16)
    t_pairs = pltpu.VMEM((2, nq, LANES, tq), BF16)
    scratch = [pltpu.VMEM((2, s, LANES), F32)] + ([t_blocks, t_blocks] if fox else [t_pairs, t_blocks, t_pairs])
    return pl.pallas_call(
        body, name=name, out_shape=out_shape, grid=(bl, n_pair, nq), in_specs=in_specs, out_specs=out_specs,
        scratch_shapes=scratch, compiler_params=_params("arbitrary", "arbitrary", "arbitrary"),
    )(*args)


def adamw(w, g, m, v, name):
    shape = w.shape
    c = shape[-1]
    r = w.size // c
    tr = _tile(r, 512, 8)

    def body(w_ref, g_ref, m_ref, v_ref, d_ref, nm_ref, nv_ref):
        gv = g_ref[...]
        m2 = ADAM_B1 * m_ref[...] + (1.0 - ADAM_B1) * gv
        v2 = ADAM_B2 * v_ref[...] + (1.0 - ADAM_B2) * (gv * gv)
        m_hat = m2 / (1.0 - ADAM_B1 ** ADAM_STEP)
        v_hat = v2 / (1.0 - ADAM_B2 ** ADAM_STEP)
        d_ref[...] = -ADAM_LR * (m_hat / (jnp.sqrt(v_hat) + ADAM_EPS) + ADAM_WD * w_ref[...])
        nm_ref[...] = m2
        nv_ref[...] = v2

    spec = pl.BlockSpec((tr, c), lambda i: (i, 0))
    outs = pl.pallas_call(
        body, name=name, out_shape=[jax.ShapeDtypeStruct((r, c), F32)] * 3, grid=(r // tr,),
        in_specs=[spec] * 4, out_specs=[spec] * 3, compiler_params=_params("arbitrary"),
    )(*(a.reshape(r, c) for a in (w, g, m, v)))
    return tuple(a.reshape(shape) for a in outs)


PACK_COLS = 1024


def _pack_rows(a):
    return a.reshape(-1, PACK_COLS)


def kernel(x, c, positions, mla_w_in, mla_g_q, mla_w_uq, mla_g_kv, mla_w_uk, mla_w_uv, mla_w_o, fox_w_in, fox_b_f, fox_w_o, ada_w, ada_b, ffn_w_gate, ffn_w_up, ffn_w_down, ln_g, ln_b, loss_target, m_mla_w_in, m_mla_g_q, m_mla_w_uq, m_mla_g_kv, m_mla_w_uk, m_mla_w_uv, m_mla_w_o, m_fox_w_in, m_fox_b_f, m_fox_w_o, m_ada_w, m_ada_b, m_ffn_w_gate, m_ffn_w_up, m_ffn_w_down, m_ln_g, m_ln_b, v_mla_w_in, v_mla_g_q, v_mla_w_uq, v_mla_g_kv, v_mla_w_uk, v_mla_w_uv, v_mla_w_o, v_fox_w_in, v_fox_b_f, v_fox_w_o, v_ada_w, v_ada_b, v_ffn_w_gate, v_ffn_w_up, v_ffn_w_down, v_ln_g, v_ln_b):
    bl, s, d = x.shape
    t = bl * s
    ff = ffn_w_gate.shape[-1] * N_DEV
    dev = 4 * lax.axis_index("x") + 2 * lax.axis_index("y") + lax.axis_index("c")
    ada_cols = ada_w.shape[-1]
    mla_in = mla_w_in.shape[-1]
    mla_in_pad = mla_in + (-mla_in) % LANES

    def t_last(a):
        return jnp.swapaxes(a, -1, -2)

    local = {
        "mla_w_in": mla_w_in[0],
        "mla_w_uq": t_last(mla_w_uq[0]),
        "mla_w_uk": t_last(mla_w_uk[0]),
        "mla_w_uv": t_last(mla_w_uv[0]),
        "mla_w_o": mla_w_o[0],
        "fox_w_in": t_last(fox_w_in[0]),
        "fox_w_o": fox_w_o[0],
    }
    for i in range(DEPTH):
        local.update({f"gate{i}": t_last(ffn_w_gate[i]), f"up{i}": t_last(ffn_w_up[i]), f"down{i}": ffn_w_down[i]})
    groups = [["mla_w_in", "mla_w_uq", "mla_w_uk", "mla_w_uv", "mla_w_o"],
              ["gate0", "up0", "down0"],
              ["fox_w_in", "fox_w_o"],
              ["gate1", "up1", "down1"]]
    offsets, rows_of, slot_of, group_of = {}, {}, {}, {}
    group_rows = []
    for gi, names in enumerate(groups):
        rows = 0
        for nm in names:
            rows_of[nm] = local[nm].size // PACK_COLS
            slot_of[nm] = rows_of[nm] + (-rows_of[nm]) % 16
            offsets[nm] = rows
            group_of[nm] = gi
            rows += slot_of[nm]
        group_rows.append(rows)

    def slot(nm, rows):
        pad = [(0, 0)] * rows.ndim
        pad[-2] = (0, slot_of[nm] - rows_of[nm])
        return jnp.pad(rows, pad)

    def held_until(block, arrays):
        zero = sum((a.reshape(-1)[0] * 0).astype(F32) for a in jax.tree.leaves(arrays))
        return block + zero.astype(block.dtype)

    def landing(block):
        land = lax.empty((N_DEV,) + block.shape, block.dtype)
        return lax.dynamic_update_slice(land, block[None], (dev, 0, 0))

    packed0 = jnp.concatenate([slot(nm, _pack_rows(local[nm]).astype(BF16)) for nm in groups[0]], axis=0)
    gathered0 = all_gather(packed0, "gather_mla_weights")
    gathered = {nm: gathered0[:, offsets[nm]:offsets[nm] + rows_of[nm], :] for nm in groups[0]}
    gather_started = [None] * len(groups)

    def depart(gi, after):
        blocks = [held_until(_pack_rows(local[nm]).astype(BF16), after) for nm in groups[gi]]
        gather_started[gi] = exchange_start(blocks, [landing(b) for b in blocks], f"gather_group{gi}_start", False)
        return gather_started[gi][4]

    def full(nm, cols):
        return gathered[nm].reshape(-1, cols)

    w_in = jnp.pad(full("mla_w_in", mla_in), ((0, 0), (0, mla_in_pad - mla_in)))
    wt_uq = full("mla_w_uq", MLA_QR).reshape(MLA_HEADS, MLA_NOPE + MLA_ROPE, MLA_QR)
    wt_uq_n = wt_uq[:, :MLA_NOPE].reshape(MLA_HEADS * MLA_NOPE, MLA_QR)
    wt_uq_r = wt_uq[:, MLA_NOPE:].reshape(MLA_HEADS * MLA_ROPE, MLA_QR)
    wt_uk = full("mla_w_uk", MLA_KVR)
    wt_uv = full("mla_w_uv", MLA_KVR)
    w_mo = full("mla_w_o", d)
    wt_gate, wt_up, w_down = [None] * DEPTH, [None] * DEPTH, [None] * DEPTH

    def arrive(gi, after):
        if gi + 1 < len(groups):
            after = depart(gi + 1, after)
        landed = list(exchange_wait(gather_started[gi], after, f"gather_group{gi}_wait", False))
        gathered.update(zip(groups[gi], landed))
        for i in range(DEPTH):
            if group_of[f"gate{i}"] == gi:
                wt_gate[i], wt_up[i], w_down[i] = full(f"gate{i}", d), full(f"up{i}", d), full(f"down{i}", d)

    small = jnp.concatenate([c.reshape(-1, LANES), ln_g.reshape(-1, LANES), ln_b.reshape(-1, LANES)], axis=0)
    small_rows = small.shape[0]
    small = jnp.pad(small, ((0, (-small_rows) % 8), (0, 0)))
    small_all = all_gather(small, "gather_small")
    c_rows = bl * d // LANES
    c_all = small_all[:, :c_rows].reshape(N_DEV * bl, d)
    n_ln = DEPTH * 2
    ln_g_all = small_all[:, c_rows:c_rows + n_ln, :].transpose(1, 0, 2).reshape(DEPTH, 2, 1, d)
    ln_b_all = small_all[:, c_rows + n_ln:c_rows + 2 * n_ln, :].transpose(1, 0, 2).reshape(DEPTH, 2, 1, d)

    c_act = silu_rows(c_all, "silu_c")
    ada_b_loc = lax.dynamic_slice_in_dim(ada_b, dev * ada_cols, ada_cols, axis=1)
    mod_cols = [mm([(c_act, ada_w[i])], trans_b=False, out_dtype=F32, name=f"ada_fwd{i}", bias=ada_b_loc[i][None, :])
                for i in range(DEPTH)]
    mod_all = all_gather(jnp.concatenate(mod_cols, axis=0), "gather_mod")
    mod_all = mod_all.reshape(N_DEV, DEPTH, N_DEV * bl, ada_cols).transpose(1, 2, 0, 3).reshape(DEPTH, N_DEV * bl, 6 * d)
    mod_mine = lax.dynamic_slice_in_dim(mod_all, dev * bl, bl, axis=1)
    mods = [mod_mine[i].reshape(bl * 6, 1, d) for i in range(DEPTH)]
    mods[0] = mods[0] + depart(1, (mod_mine, gathered0))[0, 0]

    half_r = MLA_ROPE // 2
    inv_freq = ROPE_THETA ** (-jnp.arange(half_r, dtype=F32) / half_r)
    inv_freq = jnp.tile(inv_freq, LANES // half_r)[None, :]
    sign = jnp.tile(jnp.concatenate([-jnp.ones((half_r,), F32), jnp.ones((half_r,), F32)]), LANES // MLA_ROPE)[None, :]
    cos_t, sin_t = rope_tables(positions.astype(F32).reshape(t, 1), inv_freq, sign, "rope_tables")

    x2d = x.reshape(t, d)
    g_q, g_kv = mla_g_q.reshape(1, MLA_QR), mla_g_kv.reshape(1, MLA_KVR)
    b_f = jnp.pad(fox_b_f.reshape(1, FOX_HEADS), ((0, 0), (0, LANES - FOX_HEADS)))
    mla_scale = (MLA_NOPE + MLA_ROPE) ** -0.5
    fox_scale = FOX_HD ** -0.5
    tq = _attn_tiles(s)
    nk = s // tq

    saved = []
    u = modulate(x2d, mods[0], 0, 1, bl, "modulate0")
    xin = x2d
    for i in range(DEPTH):
        sv = {"u": u, "x_in": xin}
        if i % 2 == 0:
            h_in = mm([(u, w_in)], trans_b=False, out_dtype=F32, name=f"mla_in{i}")
            c_q, c_kv, k_r = mla_latents_forward(h_in, g_q, g_kv, cos_t, sin_t, f"mla_latents{i}")
            q_n = mm([(c_q, wt_uq_n)], trans_b=True, out_dtype=BF16, out_slab=True, name=f"mla_qn{i}")
            q_r_raw = mm([(c_q, wt_uq_r)], trans_b=True, out_dtype=F32, out_slab=True, name=f"mla_qr{i}")
            q_r = rope_slabs(q_r_raw, cos_t, sin_t, BF16, f"mla_qrope{i}")
            k_n = mm([(c_kv, wt_uk)], trans_b=True, out_dtype=BF16, out_slab=True, name=f"mla_kn{i}")
            v_m = mm([(c_kv, wt_uv)], trans_b=True, out_dtype=BF16, out_slab=True, name=f"mla_v{i}")
            ops = (q_n, q_r, k_n, k_r, v_m)
            o, lse, o_delta = attention_forward("mla", ops, bl, mla_scale, f"mla_attn{i}")
            y = mm([(o, w_mo)], trans_b=False, out_dtype=F32, name=f"mla_out{i}")
            sv.update(h_in=h_in, c_q=c_q, c_kv=c_kv, ops=ops, o=o, lse=lse, o_delta=o_delta)
        else:
            arrive(2, u)
            wt_fox = full("fox_w_in", d)
            wt_qkv = wt_fox[:3 * d]
            wt_f = jnp.pad(wt_fox[3 * d:], ((0, LANES - FOX_HEADS), (0, 0)))
            w_fo = full("fox_w_o", d)
            qkv = mm([(u, wt_qkv)], trans_b=True, out_dtype=BF16, out_slab=True, name=f"fox_qkv{i}")
            z = mm([(u, wt_f)], trans_b=True, out_dtype=F32, name=f"fox_z{i}")
            f_tok, f_q = fox_gate_forward(z, b_f, bl, f"fox_gate{i}")
            f_k = f_tok[:, :FOX_HEADS].reshape(bl, nk, tq, FOX_HEADS // 2, 2).transpose(0, 3, 1, 4, 2)
            f_k = jnp.pad(f_k.reshape(bl * FOX_HEADS // 2, nk, 2, tq), ((0, 0), (0, 0), (0, 6), (0, 0)))
            ops = (qkv, f_q, f_k)
            o, lse, o_delta = attention_forward("fox", ops, bl, fox_scale, f"fox_attn{i}")
            y = mm([(o, w_fo)], trans_b=False, out_dtype=F32, name=f"fox_out{i}")
            sv.update(z=z, ops=ops, o=o, lse=lse, o_delta=o_delta)
        x1, r1, u2 = residual_layer_norm(xin, y, mods[i], 2, ln_g_all[i, 0], ln_b_all[i, 0], bl, f"ln_mix{i}",
                                         next_mod=(3, 4))
        if wt_gate[i] is None:
            arrive(group_of[f"gate{i}"], u2)
        a, bb, h = swiglu_in(u2, wt_gate[i], wt_up[i], f"ffn_in{i}")
        y2 = mm([(h, w_down[i])], trans_b=False, out_dtype=F32, name=f"ffn_down{i}")
        sv.update(y=y, r1=r1, u2=u2, a=a, bb=bb, h=h, y2=y2)
        if i + 1 < DEPTH:
            xin, r2, u = residual_layer_norm(x1, y2, mods[i], 5, ln_g_all[i, 1], ln_b_all[i, 1], bl, f"ln_ffn{i}",
                                             next_mod=(0, 1, mods[i + 1]))
        else:
            xin, r2 = residual_layer_norm(x1, y2, mods[i], 5, ln_g_all[i, 1], ln_b_all[i, 1], bl, f"ln_ffn{i}")
        sv.update(r2=r2)
        saved.append(sv)

    loss_cols, d_x = loss_head(xin, loss_target.reshape(t, d), "loss_head")

    grads_full = {}
    wgrad = functools.partial(mm_tn, out_dtype=BF16)
    dmod = [[None] * 6 for _ in range(DEPTH)]
    dg_ln = [[None, None] for _ in range(DEPTH)]
    db_ln = [[None, None] for _ in range(DEPTH)]
    dg_q = dg_kv = db_f = None
    d_a, du = d_x, None
    scatter_started = [None] * len(groups)

    def scatter_start(gi, after=None):
        gs = [grads_full[nm].reshape(N_DEV, rows_of[nm], PACK_COLS).astype(BF16) for nm in groups[gi]]
        if gi == 0:
            gs = [jnp.concatenate([slot(nm, g) for nm, g in zip(groups[gi], gs)], axis=1)]
        if after is not None:
            gs = [held_until(g, after) for g in gs]
        lands = [landing(lax.dynamic_index_in_dim(g, dev, 0, keepdims=False)) for g in gs]
        scatter_started[gi] = exchange_start(gs, lands, f"scatter_group{gi}_start", True)

    ln_g_bwd = [[ln_g_all[i, k] for k in range(2)] for i in range(DEPTH)]
    for i in reversed(range(DEPTH)):
        sv = saved[i]
        if i + 1 < DEPTH:
            gi = group_of["fox_w_in"]
            scatter_start(gi)
            ln_g_bwd[i][1] = after_token(ln_g_bwd[i][1], scatter_started[gi])
        ln2 = (sv["r2"], sv["y2"], ln_g_bwd[i][1], ln_b_all[i, 1], (mods[i], 5))
        if du is None:
            bw = sublayer_backward(d_a, bl, f"bwd_ln_ffn{i}", ln=ln2)
        else:
            bw = sublayer_backward(d_a, bl, f"bwd_ln_ffn{i}", du=du, scale=(mods[i + 1], 1), ln=ln2)
            dmod[i + 1][0], dmod[i + 1][1] = bw["dshift"], bw["dscale"]
        dmod[i][5], dg_ln[i][1], db_ln[i][1] = bw["dgate"], bw["dg"], bw["db"]
        dy2 = bw["dy"]
        da, dbb = swiglu_out_backward(dy2, w_down[i], sv["a"], sv["bb"], f"bwd_ffn_act{i}")
        du2 = mm([(da, wt_gate[i]), (dbb, wt_up[i])], trans_b=False, out_dtype=F32, name=f"bwd_ffn_du{i}")
        grads_full[f"down{i}"] = wgrad(sv["h"], dy2, name=f"bwd_w_down{i}")
        grads_full[f"gate{i}"] = wgrad(da, sv["u2"], name=f"bwd_w_gate{i}")
        grads_full[f"up{i}"] = wgrad(dbb, sv["u2"], name=f"bwd_w_up{i}")
        gi = group_of[f"gate{i}"]
        scatter_start(gi)
        ln_g_bwd[i][0] = after_token(ln_g_bwd[i][0], scatter_started[gi])
        bw = sublayer_backward(bw["dx"], bl, f"bwd_ln_mix{i}", du=du2, scale=(mods[i], 4),
                               ln=(sv["r1"], sv["y"], ln_g_bwd[i][0], ln_b_all[i, 0], (mods[i], 2)))
        dmod[i][3], dmod[i][4], dmod[i][2] = bw["dshift"], bw["dscale"], bw["dgate"]
        dg_ln[i][0], db_ln[i][0] = bw["dg"], bw["db"]
        d_a, dy = bw["dx"], bw["dy"]
        o, lse, ops = sv["o"], sv["lse"], sv["ops"]
        if i % 2 == 0:
            do = mm([(dy, w_mo)], trans_b=True, out_dtype=BF16, out_slab=True, name=f"bwd_mla_do{i}")
            grads_full["mla_w_o"] = wgrad(o, dy, name=f"bwd_w_mla_o{i}")
            dqn, dqr, dkn, dvm, dkr = attention_backward("mla", ops, sv["o_delta"], do, lse, bl, mla_scale,
                                                         f"bwd_mla_attn{i}")
            dqr = rope_slabs(dqr, cos_t, sin_t, F32, f"bwd_mla_qrope{i}", transposed=True)
            dcq = mm([(dqn, wt_uq_n), (dqr, wt_uq_r)], trans_b=False, out_dtype=F32, name=f"bwd_mla_dcq{i}")
            dckv = mm([(dkn, wt_uk), (dvm, wt_uv)], trans_b=False, out_dtype=F32, name=f"bwd_mla_dckv{i}")
            d_uq_n = wgrad(dqn, sv["c_q"], name=f"bwd_w_uq_n{i}").reshape(MLA_HEADS, MLA_NOPE, MLA_QR)
            d_uq_r = wgrad(dqr, sv["c_q"], name=f"bwd_w_uq_r{i}").reshape(MLA_HEADS, MLA_ROPE, MLA_QR)
            grads_full["mla_w_uq"] = jnp.concatenate([d_uq_n, d_uq_r], axis=1)
            grads_full["mla_w_uk"] = wgrad(dkn, sv["c_kv"], name=f"bwd_w_uk{i}")
            grads_full["mla_w_uv"] = wgrad(dvm, sv["c_kv"], name=f"bwd_w_uv{i}")
            dh_in, dg_q, dg_kv = mla_latents_backward(sv["h_in"], dcq, dckv, dkr, g_q, g_kv, cos_t, sin_t,
                                                      f"bwd_mla_latents{i}")
            du = mm([(dh_in, w_in)], trans_b=True, out_dtype=F32, name=f"bwd_mla_du{i}")
            grads_full["mla_w_in"] = wgrad(sv["u"], dh_in, name=f"bwd_w_mla_in{i}")[:, :mla_in]
        else:
            do = mm([(dy, w_fo)], trans_b=True, out_dtype=BF16, out_slab=True, name=f"bwd_fox_do{i}")
            grads_full["fox_w_o"] = wgrad(o, dy, name=f"bwd_w_fox_o{i}")
            dq, dk, dvf, dfk = attention_backward("fox", ops, sv["o_delta"], do, lse, bl, fox_scale, f"bwd_fox_attn{i}")
            df = dfk[:, :, :2, :].reshape(bl, FOX_HEADS // 2, nk, 2, tq).transpose(0, 2, 4, 1, 3).reshape(t, FOX_HEADS)
            df = jnp.pad(df, ((0, 0), (0, LANES - FOX_HEADS)))
            dz, db_f = fox_gate_backward(sv["z"], b_f, df, bl, f"bwd_fox_gate{i}")
            du = mm([(dq, wt_fox[0:d]), (dk, wt_fox[d:2 * d]), (dvf, wt_fox[2 * d:3 * d]), (dz, wt_f)],
                    trans_b=False, out_dtype=F32, name=f"bwd_fox_du{i}")
            u_f = sv["u"]
            grads_full["fox_w_in"] = jnp.concatenate(
                [wgrad(dq, u_f, name=f"bwd_w_fox_q{i}"), wgrad(dk, u_f, name=f"bwd_w_fox_k{i}"),
                 wgrad(dvf, u_f, name=f"bwd_w_fox_v{i}"), wgrad(dz, u_f, name=f"bwd_w_fox_f{i}")[:FOX_HEADS]], axis=0)
    scatter_start(0)
    bw = sublayer_backward(d_a, bl, "bwd_input", du=du, scale=(after_token(mods[0], scatter_started[0]), 1), x_in=x2d)
    dmod[0][0], dmod[0][1] = bw["dshift"], bw["dscale"]
    grad_x = bw["dx"].reshape(bl, s, d)

    dmod_rows = jnp.concatenate([r.reshape(bl, d) for layer in dmod for r in layer], axis=0)
    dmod_rows = dmod_rows.reshape(DEPTH, 6, bl, d).transpose(0, 2, 1, 3)
    n_mod = dmod_rows.size // LANES
    ln_parts = [dg_ln[i][k] for i in range(DEPTH) for k in range(2)] + [db_ln[i][k] for i in range(DEPTH) for k in range(2)]
    small_g = jnp.concatenate([dmod_rows.reshape(-1, LANES), dg_q.reshape(-1, LANES), dg_kv.reshape(-1, LANES), db_f]
                              + [p.reshape(-1, LANES) for p in ln_parts] + [loss_cols.reshape(-1, LANES)], axis=0)
    n_small = small_g.shape[0]
    small_g = jnp.pad(small_g, ((0, (-n_small) % 8), (0, 0)))
    small_g_all = all_gather(small_g, "gather_small_grads")
    small_sum = sum_leading(small_g_all, "sum_small_grads")
    per_seq = DEPTH * 6 * d // LANES
    dmod_all = small_g_all[:, :n_mod].reshape(N_DEV, DEPTH, bl, 6 * d).transpose(1, 0, 2, 3)
    dmod_all = dmod_all.reshape(DEPTH, N_DEV * bl, 6 * d)
    o1 = n_mod
    grad_g_q = small_sum[o1:o1 + MLA_QR // LANES].reshape(1, MLA_QR)
    o1 += MLA_QR // LANES
    grad_g_kv = small_sum[o1:o1 + MLA_KVR // LANES].reshape(1, MLA_KVR)
    o1 += MLA_KVR // LANES
    grad_b_f = small_sum[o1:o1 + 1, :FOX_HEADS]
    o1 += 1
    n_ln_rows = DEPTH * 2 * d // LANES
    grad_ln_g_full = small_sum[o1:o1 + n_ln_rows].reshape(DEPTH, 2, d)
    grad_ln_b_full = small_sum[o1 + n_ln_rows:o1 + 2 * n_ln_rows].reshape(DEPTH, 2, d)
    loss = jnp.sum(small_sum[o1 + 2 * n_ln_rows:o1 + 2 * n_ln_rows + d // LANES])
    shard = d // N_DEV
    grad_ln_g = lax.dynamic_slice_in_dim(grad_ln_g_full, dev * shard, shard, axis=2)
    grad_ln_b = lax.dynamic_slice_in_dim(grad_ln_b_full, dev * shard, shard, axis=2)
    by_seq = small_g_all[:, :n_mod].reshape(N_DEV, DEPTH, bl, 6 * d // LANES, LANES).transpose(0, 2, 1, 3, 4)
    grad_ada_b = sum_leading(by_seq.reshape(N_DEV * bl, per_seq, LANES), "sum_ada_b").reshape(DEPTH, 6 * d)
    dmod_cols = lax.dynamic_slice_in_dim(dmod_all, dev * ada_cols, ada_cols, axis=2)
    grad_ada_w = jnp.stack([mm_tn(c_act, dmod_cols[i], name=f"bwd_w_ada{i}") for i in range(DEPTH)])

    g_mine = {}

    def scatter_arrive(gi, after):
        landed = exchange_wait(scatter_started[gi], after, f"scatter_group{gi}_wait", True)
        if gi == 0:
            total = sum_leading(landed[0], f"scatter_group{gi}_sum")
            g_mine.update({nm: total[offsets[nm]:offsets[nm] + rows_of[nm]] for nm in groups[gi]})
            return total
        for nm, land in zip(groups[gi], landed):
            g_mine[nm] = sum_leading(land, f"scatter_sum_{nm}")
        return g_mine[groups[gi][-1]]

    after = scatter_started[0][4]
    for gi in reversed(range(1, len(groups))):
        after = scatter_arrive(gi, after)

    def mine(nm, shape):
        return g_mine[nm].reshape(shape)

    def shard_t(nm, a):
        return mine(nm, t_last(a).shape)

    transposed = {"mla_w_uq", "mla_w_uk", "mla_w_uv", "fox_w_in", "ffn_w_gate", "ffn_w_up"}
    grads = {
        "mla_w_in": lambda: mine("mla_w_in", mla_w_in[0].shape)[None],
        "mla_g_q": lambda: grad_g_q,
        "mla_w_uq": lambda: shard_t("mla_w_uq", mla_w_uq[0])[None],
        "mla_g_kv": lambda: grad_g_kv,
        "mla_w_uk": lambda: shard_t("mla_w_uk", mla_w_uk[0])[None],
        "mla_w_uv": lambda: shard_t("mla_w_uv", mla_w_uv[0])[None],
        "mla_w_o": lambda: mine("mla_w_o", mla_w_o[0].shape)[None],
        "fox_w_in": lambda: shard_t("fox_w_in", fox_w_in[0])[None],
        "fox_b_f": lambda: grad_b_f,
        "fox_w_o": lambda: mine("fox_w_o", fox_w_o[0].shape)[None],
        "ada_w": lambda: grad_ada_w,
        "ada_b": lambda: grad_ada_b,
        "ffn_w_gate": lambda: jnp.stack([shard_t(f"gate{i}", ffn_w_gate[i]) for i in range(DEPTH)]),
        "ffn_w_up": lambda: jnp.stack([shard_t(f"up{i}", ffn_w_up[i]) for i in range(DEPTH)]),
        "ffn_w_down": lambda: jnp.stack([mine(f"down{i}", ffn_w_down[i].shape) for i in range(DEPTH)]),
        "ln_g": lambda: grad_ln_g,
        "ln_b": lambda: grad_ln_b,
    }
    weights = dict(mla_w_in=mla_w_in, mla_g_q=mla_g_q, mla_w_uq=mla_w_uq, mla_g_kv=mla_g_kv, mla_w_uk=mla_w_uk,
                   mla_w_uv=mla_w_uv, mla_w_o=mla_w_o, fox_w_in=fox_w_in, fox_b_f=fox_b_f, fox_w_o=fox_w_o,
                   ada_w=ada_w, ada_b=ada_b, ffn_w_gate=ffn_w_gate, ffn_w_up=ffn_w_up, ffn_w_down=ffn_w_down,
                   ln_g=ln_g, ln_b=ln_b)
    first = dict(mla_w_in=m_mla_w_in, mla_g_q=m_mla_g_q, mla_w_uq=m_mla_w_uq, mla_g_kv=m_mla_g_kv, mla_w_uk=m_mla_w_uk,
                 mla_w_uv=m_mla_w_uv, mla_w_o=m_mla_w_o, fox_w_in=m_fox_w_in, fox_b_f=m_fox_b_f, fox_w_o=m_fox_w_o,
                 ada_w=m_ada_w, ada_b=m_ada_b, ffn_w_gate=m_ffn_w_gate, ffn_w_up=m_ffn_w_up, ffn_w_down=m_ffn_w_down,
                 ln_g=m_ln_g, ln_b=m_ln_b)
    second = dict(mla_w_in=v_mla_w_in, mla_g_q=v_mla_g_q, mla_w_uq=v_mla_w_uq, mla_g_kv=v_mla_g_kv, mla_w_uk=v_mla_w_uk,
                  mla_w_uv=v_mla_w_uv, mla_w_o=v_mla_w_o, fox_w_in=v_fox_w_in, fox_b_f=v_fox_b_f, fox_w_o=v_fox_w_o,
                  ada_w=v_ada_w, ada_b=v_ada_b, ffn_w_gate=v_ffn_w_gate, ffn_w_up=v_ffn_w_up, ffn_w_down=v_ffn_w_down,
                  ln_g=v_ln_g, ln_b=v_ln_b)
    order = list(weights)
    last = [nm for nm in order if group_of.get(nm) == 0]
    updated = {}
    for nm in [nm for nm in order if nm not in last] + last:
        if last and nm == last[0]:
            scatter_arrive(0, after)
        lay = t_last if nm in transposed else (lambda a: a)
        w = lay(weights[nm])
        g = grads[nm]().reshape(w.shape)
        delta, new_m, new_v = adamw(w, g, lay(first[nm]), lay(second[nm]), f"adamw_{nm}")
        updated[nm] = (lay(g), lay(delta), lay(new_m), lay(new_v))
        after = new_v
    return (loss, grad_x, *(updated[nm][k] for k in range(4) for nm in order))
```

```python
import functools
import math

import jax
import jax.numpy as jnp
from jax import lax
from jax.experimental import pallas as pl
from jax.experimental.pallas import tpu as pltpu

F32 = jnp.float32
BF16 = jnp.bfloat16
LANES = 128
N_DEV = 8
VMEM_LIMIT_BYTES = 56 * 1024 * 1024

DEPTH = 2
MLA_HEADS = 8
MLA_NOPE = 128
MLA_ROPE = 64
MLA_V = 128
MLA_QR = 256
MLA_KVR = 256
ROPE_THETA = 10000.0
FOX_HEADS = 16
FOX_HD = 64
ALPHA = (2.0 * DEPTH) ** 0.25
NORM_EPS = 1e-5
ADAM_LR = 0.001
ADAM_B1 = 0.9
ADAM_B2 = 0.999
ADAM_EPS = 1e-08
ADAM_WD = 0.01
ADAM_STEP = 10

MESH = pl.DeviceIdType.MESH


def _params(*sem):
    return pltpu.CompilerParams(dimension_semantics=sem, vmem_limit_bytes=VMEM_LIMIT_BYTES)


def _tile(n, cap, mult=LANES):
    if n <= cap:
        return n
    best = None
    for t in range(mult, cap + 1, mult):
        if n % t == 0:
            best = t
    assert best is not None, (n, cap, mult)
    return best


def _dot(a, b, dims):
    return lax.dot_general(a, b, (dims, ((), ())), preferred_element_type=F32)


def _nn(a, b):
    return _dot(a, b, ((1,), (0,)))


def _nt(a, b):
    return _dot(a, b, ((1,), (1,)))


def _tn(a, b):
    return _dot(a, b, ((0,), (0,)))


def _me():
    return lax.axis_index("x"), lax.axis_index("y"), lax.axis_index("c")


def all_gather(x_loc, name):
    r, c = x_loc.shape

    def body(x_ref, out_ref, send_sems, recv_sems, local_sem):
        x, y, cc = _me()
        me, sibling = (x, y, cc), (x, y, 1 - cc)
        chips = [(1 - x, y), (x, 1 - y), (1 - x, 1 - y)]

        def rows(px, py, pc):
            return out_ref.at[4 * px + 2 * py + pc]

        def copy(k, block, to, src=None):
            return pltpu.make_async_remote_copy(
                src_ref=rows(*block) if src is None else src, dst_ref=rows(*block),
                send_sem=send_sems.at[k], recv_sem=recv_sems.at[k], device_id=to, device_id_type=MESH)

        mine = pltpu.make_async_copy(x_ref, rows(*me), local_sem)
        mine.start()
        first = [copy(0, me, sibling, src=x_ref)]
        first += [copy(1 + j, me, (*chip, cc), src=x_ref) for j, chip in enumerate(chips)]
        for cp in first:
            cp.start()
        passed = [copy(4 + j, (*chip, cc), sibling) for j, chip in enumerate(chips)]
        for j, chip in enumerate(chips):
            copy(1 + j, (*chip, cc), me).wait_recv()
            passed[j].start()
        copy(0, sibling, me).wait_recv()
        for j, chip in enumerate(chips):
            copy(4 + j, (*chip, 1 - cc), me).wait_recv()
        for cp in first + passed:
            cp.wait_send()
        mine.wait()

    return pl.pallas_call(
        body, name=name,
        out_shape=jax.ShapeDtypeStruct((N_DEV, r, c), x_loc.dtype),
        in_specs=[pl.BlockSpec(memory_space=pl.ANY)],
        out_specs=pl.BlockSpec(memory_space=pl.ANY),
        scratch_shapes=[pltpu.SemaphoreType.DMA((7,)), pltpu.SemaphoreType.DMA((7,)), pltpu.SemaphoreType.DMA(())],
    )(x_loc)


HBM_SPEC = pl.BlockSpec(memory_space=pltpu.HBM)
SEM_SPEC = pl.BlockSpec(memory_space=pltpu.SEMAPHORE)
N_PEERS = N_DEV - 1


def _peer(k):
    x, y, c = _me()
    return (1 - x if k & 4 else x, 1 - y if k & 2 else y, 1 - c if k & 1 else c)


def _exchange_copies(src_refs, land_refs, send_sems, recv_sems, scatter):
    x, y, c = _me()
    mine = 4 * x + 2 * y + c
    copies = []
    for n, (src_ref, land_ref) in enumerate(zip(src_refs, land_refs)):
        for k in range(1, N_DEV):
            px, py, pc = _peer(k)
            src = src_ref.at[4 * px + 2 * py + pc] if scatter else src_ref
            sem = n * N_PEERS + k - 1
            copies.append(pltpu.make_async_remote_copy(
                src_ref=src, dst_ref=land_ref.at[mine], send_sem=send_sems.at[sem], recv_sem=recv_sems.at[sem],
                device_id=(px, py, pc), device_id_type=MESH))
    return copies


def exchange_start(srcs, lands, name, scatter):
    n = len(srcs)

    def body(*refs):
        send_sems, recv_sems = refs[2 * n], refs[2 * n + 1]
        for cp in _exchange_copies(refs[:n], refs[n:2 * n], send_sems, recv_sems, scatter):
            cp.start()
        token = refs[-1]
        token[...] = jnp.zeros_like(token)

    outs = pl.pallas_call(
        body, name=name,
        out_shape=(pltpu.SemaphoreType.DMA((n * N_PEERS,)), pltpu.SemaphoreType.DMA((n * N_PEERS,)),
                   *(pltpu.HBM(a.shape, a.dtype) for a in (*srcs, *lands)), jax.ShapeDtypeStruct((8, LANES), F32)),
        in_specs=(HBM_SPEC,) * (2 * n),
        out_specs=(SEM_SPEC, SEM_SPEC, *((HBM_SPEC,) * (2 * n)), pl.BlockSpec(memory_space=pltpu.VMEM)),
        input_output_aliases={i: 2 + i for i in range(2 * n)},
        compiler_params=pltpu.CompilerParams(has_side_effects=pltpu.SideEffectType.DATAFLOW_SIDE_EFFECTING),
    )(*(pltpu.with_memory_space_constraint(a, pltpu.HBM) for a in (*srcs, *lands)))
    return outs[0], outs[1], outs[2:2 + n], outs[2 + n:2 + 2 * n], outs[-1]


def exchange_wait(started, after, name, scatter):
    send_sems, recv_sems, srcs, lands, _ = started
    n = len(srcs)

    def body(*refs):
        send_sems, recv_sems = refs[2 * n], refs[2 * n + 1]
        for cp in _exchange_copies(refs[:n], refs[n:2 * n], send_sems, recv_sems, scatter):
            cp.wait_send()
            cp.wait_recv()

    outs = pl.pallas_call(
        body, name=name,
        out_shape=tuple(pltpu.HBM(a.shape, a.dtype) for a in (*srcs, *lands)),
        in_specs=(*((HBM_SPEC,) * (2 * n)), SEM_SPEC, SEM_SPEC, pl.BlockSpec(memory_space=pl.ANY)),
        out_specs=(HBM_SPEC,) * (2 * n), input_output_aliases={i: i for i in range(2 * n)},
        compiler_params=pltpu.CompilerParams(has_side_effects=pltpu.SideEffectType.DATAFLOW_SIDE_EFFECTING),
    )(*srcs, *lands, send_sems, recv_sems, after)
    return outs[n:]


def after_token(small, started):
    return small + started[4][0, 0]


def sum_leading(x, name):
    n, r, c = x.shape
    tr = _tile(r, 512, 16)

    def body(x_ref, o_ref):
        acc = x_ref[0].astype(F32)
        for k in range(1, n):
            acc = acc + x_ref[k].astype(F32)
        o_ref[...] = acc

    return pl.pallas_call(
        body, name=name,
        out_shape=jax.ShapeDtypeStruct((r, c), F32),
        grid=(r // tr,),
        in_specs=[pl.BlockSpec((n, tr, c), lambda i: (0, i, 0))],
        out_specs=pl.BlockSpec((tr, c), lambda i: (i, 0)),
        compiler_params=_params("arbitrary"),
    )(x)


MM_VMEM_BUDGET = 36 * 1024 * 1024
GRID_STEP_AS_BYTES = 1 << 20


def _mm_tiles(m, n, a_row_bytes, b_col_bytes, out_bytes):
    tms = [c for c in (2048, 1024, 512, 256, 128, 64, 32, 16, 8) if m % c == 0] or [m]
    tns = [c for c in range(LANES, min(n, 2048) + 1, LANES) if n % c == 0] or [n]
    best = None
    for tm in tms:
        for tn in tns:
            vmem = 2 * (tm * a_row_bytes + tn * b_col_bytes) + 2 * tm * tn * out_bytes + tm * tn * 4
            if vmem > MM_VMEM_BUDGET:
                continue
            steps = (m // tm) * (n // tn)
            cost = steps * GRID_STEP_AS_BYTES + (m // tm) * n * b_col_bytes + m * a_row_bytes
            if best is None or cost < best[0]:
                best = (cost, tm, tn)
    assert best is not None, (m, n, a_row_bytes, b_col_bytes)
    return best[1], best[2]


def mm(pairs, *, trans_b, out_dtype, name, out_slab=False, bias=None):
    a0 = pairs[0][0]
    m = a0.shape[1] if a0.ndim == 3 else a0.shape[0]
    n = pairs[0][1].shape[0] if trans_b else pairs[0][1].shape[1]
    a_row_bytes = sum((b.shape[1] if trans_b else b.shape[0]) * a.dtype.itemsize for a, b in pairs)
    b_col_bytes = sum((b.shape[1] if trans_b else b.shape[0]) * b.dtype.itemsize for _, b in pairs)
    tm, tn = _mm_tiles(m, n, a_row_bytes, b_col_bytes, jnp.dtype(out_dtype).itemsize)
    slabs = [a.ndim == 3 for a, _ in pairs]
    n_pairs = len(pairs)

    def body(*refs):
        o_ref = refs[-1]
        acc = bias_ref = None
        if bias is not None:
            bias_ref = refs[2 * n_pairs]
        for i in range(n_pairs):
            a_ref, b_ref = refs[2 * i], refs[2 * i + 1]
            if slabs[i]:
                a = jnp.concatenate([a_ref[s].astype(BF16) for s in range(a_ref.shape[0])], axis=1)
            else:
                a = a_ref[...].astype(BF16)
            b = b_ref[...].astype(BF16)
            part = _nt(a, b) if trans_b else _nn(a, b)
            acc = part if acc is None else acc + part
        if bias_ref is not None:
            acc = acc + bias_ref[...]
        if out_slab:
            for s in range(tn // LANES):
                o_ref[s] = acc[:, s * LANES:(s + 1) * LANES].astype(out_dtype)
        else:
            o_ref[...] = acc.astype(out_dtype)

    in_specs, args = [], []
    for (a, b), slab in zip(pairs, slabs):
        if slab:
            in_specs.append(pl.BlockSpec((a.shape[0], tm, LANES), lambda i, j: (0, i, 0)))
        else:
            in_specs.append(pl.BlockSpec((tm, a.shape[1]), lambda i, j: (i, 0)))
        if trans_b:
            in_specs.append(pl.BlockSpec((tn, b.shape[1]), lambda i, j: (j, 0)))
        else:
            in_specs.append(pl.BlockSpec((b.shape[0], tn), lambda i, j: (0, j)))
        args += [a, b]
    if bias is not None:
        in_specs.append(pl.BlockSpec((1, tn), lambda i, j: (0, j)))
        args.append(bias)
    if out_slab:
        out_shape = jax.ShapeDtypeStruct((n // LANES, m, LANES), out_dtype)
        out_spec = pl.BlockSpec((tn // LANES, tm, LANES), lambda i, j: (j, i, 0))
    else:
        out_shape = jax.ShapeDtypeStruct((m, n), out_dtype)
        out_spec = pl.BlockSpec((tm, tn), lambda i, j: (i, j))
    return pl.pallas_call(
        body, name=name, out_shape=out_shape, grid=(m // tm, n // tn),
        in_specs=in_specs, out_specs=out_spec,
        compiler_params=_params("arbitrary", "arbitrary"),
    )(*args)


def mm_tn(a, b, *, name, out_dtype=F32, tk_cap=1536, tn_cap=1024, tm_cap=1024):
    slab = a.ndim == 3
    m = a.shape[1] if slab else a.shape[0]
    k = a.shape[0] * LANES if slab else a.shape[1]
    n = b.shape[1]
    tk = _tile(k, tk_cap)
    tn = _tile(n, tn_cap)
    tm = _tile(m, tm_cap, 8)
    n_steps = m // tm

    def body(a_ref, b_ref, o_ref, acc_ref):
        step = pl.program_id(2)

        @pl.when(step == 0)
        def _():
            acc_ref[...] = jnp.zeros_like(acc_ref)

        bb = b_ref[...].astype(BF16)
        if slab:
            for s in range(tk // LANES):
                acc_ref[s * LANES:(s + 1) * LANES, :] += _tn(a_ref[s].astype(BF16), bb)
        else:
            acc_ref[...] += _tn(a_ref[...].astype(BF16), bb)

        @pl.when(step == n_steps - 1)
        def _():
            o_ref[...] = acc_ref[...].astype(out_dtype)

    if slab:
        a_spec = pl.BlockSpec((tk // LANES, tm, LANES), lambda i, j, t: (i, t, 0))
    else:
        a_spec = pl.BlockSpec((tm, tk), lambda i, j, t: (t, i))
    return pl.pallas_call(
        body, name=name, out_shape=jax.ShapeDtypeStruct((k, n), out_dtype), grid=(k // tk, n // tn, n_steps),
        in_specs=[a_spec, pl.BlockSpec((tm, tn), lambda i, j, t: (t, j))],
        out_specs=pl.BlockSpec((tk, tn), lambda i, j, t: (i, j)),
        scratch_shapes=[pltpu.VMEM((tk, tn), F32)],
        compiler_params=_params("arbitrary", "arbitrary", "arbitrary"),
    )(a, b)


def _row_spec(d, k):
    return pl.BlockSpec((1, 1, d), lambda b, i: (6 * b + k, 0, 0))


def modulate(x, mod, k_shift, k_scale, bl, name):
    t, d = x.shape
    s = t // bl
    tm = _tile(s, 1024, 8)
    nt = s // tm

    def body(x_ref, sh_ref, sc_ref, o_ref):
        o_ref[...] = (x_ref[...] * (1.0 + sc_ref[0]) + sh_ref[0]).astype(BF16)

    return pl.pallas_call(
        body, name=name, out_shape=jax.ShapeDtypeStruct((t, d), BF16), grid=(bl, nt),
        in_specs=[pl.BlockSpec((tm, d), lambda b, i: (b * nt + i, 0)), _row_spec(d, k_shift), _row_spec(d, k_scale)],
        out_specs=pl.BlockSpec((tm, d), lambda b, i: (b * nt + i, 0)),
        compiler_params=_params("arbitrary", "arbitrary"),
    )(x, mod, mod)


def _layer_norm_stats(r):
    mu = jnp.mean(r, axis=-1, keepdims=True)
    rc = r - mu
    var = jnp.mean(rc * rc, axis=-1, keepdims=True)
    rstd = lax.rsqrt(var + NORM_EPS)
    return rc * rstd, rstd


def residual_layer_norm(x, y, mod, k_gate, g, b, bl, name, next_mod=None):
    t, d = x.shape
    s = t // bl
    tm = _tile(s, 1024, 8)
    nt = s // tm
    has_next = next_mod is not None

    def body(*refs):
        x_ref, y_ref, gt_ref, g_ref, b_ref = refs[:5]
        rest = refs[5:]
        if has_next:
            sh_ref, sc_ref, o_ref, r_ref, u_ref = rest
        else:
            o_ref, r_ref = rest
        r = ALPHA * x_ref[...] + (1.0 + gt_ref[0]) * y_ref[...]
        xhat, _ = _layer_norm_stats(r)
        out = xhat * g_ref[...] + b_ref[...]
        o_ref[...] = out
        r_ref[...] = r
        if has_next:
            u_ref[...] = (out * (1.0 + sc_ref[0]) + sh_ref[0]).astype(BF16)

    tok = pl.BlockSpec((tm, d), lambda bb, i: (bb * nt + i, 0))
    vec = pl.BlockSpec((1, d), lambda bb, i: (0, 0))
    in_specs = [tok, tok, _row_spec(d, k_gate), vec, vec]
    args = [x, y, mod, g, b]
    out_shape = [jax.ShapeDtypeStruct((t, d), F32), jax.ShapeDtypeStruct((t, d), F32)]
    out_specs = [tok, tok]
    if has_next:
        in_specs += [_row_spec(d, next_mod[0]), _row_spec(d, next_mod[1])]
        args += [mod if len(next_mod) == 2 else next_mod[2]] * 2
        out_shape.append(jax.ShapeDtypeStruct((t, d), BF16))
        out_specs.append(tok)
    return pl.pallas_call(
        body, name=name, out_shape=out_shape, grid=(bl, nt), in_specs=in_specs, out_specs=out_specs,
        compiler_params=_params("arbitrary", "arbitrary"),
    )(*args)


def loss_head(xo, target, name):
    t, d = xo.shape
    tm = _tile(t, 1024, 8)

    def body(x_ref, t_ref, l_ref, dx_ref):
        @pl.when(pl.program_id(0) == 0)
        def _():
            l_ref[...] = jnp.zeros_like(l_ref)

        e = x_ref[...] - t_ref[...]
        l_ref[...] += jnp.sum(e * e, axis=0, keepdims=True) * (0.5 / d)
        dx_ref[...] = e * (1.0 / d)

    tok = pl.BlockSpec((tm, d), lambda i: (i, 0))
    return pl.pallas_call(
        body, name=name,
        out_shape=[jax.ShapeDtypeStruct((1, d), F32), jax.ShapeDtypeStruct((t, d), F32)],
        grid=(t // tm,), in_specs=[tok, tok],
        out_specs=[pl.BlockSpec((1, d), lambda i: (0, 0)), tok],
        compiler_params=_params("arbitrary"),
    )(xo, target)


def sublayer_backward(d_a, bl, name, *, du=None, scale=None, x_in=None, ln=None):
    t, d = d_a.shape
    s = t // bl
    tm = _tile(s, 512, 8)
    nt = s // tm
    has_mod = du is not None
    has_ln = ln is not None
    assert has_mod or has_ln
    assert has_ln or x_in is not None

    def body(*refs):
        refs = list(refs)
        da_ref = refs.pop(0)
        if has_mod:
            du_ref, sc_ref = refs.pop(0), refs.pop(0)
        if has_ln:
            r_ref, y_ref, g_ref, b_ref, gt_ref = (refs.pop(0) for _ in range(5))
        elif has_mod:
            xin_ref = refs.pop(0)
        dx_ref = refs.pop(0)
        if has_ln:
            dy_ref, dg_ref, db_ref, dgt_ref = (refs.pop(0) for _ in range(4))
        if has_mod:
            dsc_ref, dsh_ref = refs.pop(0), refs.pop(0)
        first_tile = pl.program_id(1) == 0
        first_step = jnp.logical_and(pl.program_id(0) == 0, first_tile)

        dout = da_ref[...]
        if has_ln:
            xhat, rstd = _layer_norm_stats(r_ref[...])
        if has_mod:
            duv = du_ref[...]
            dout = dout + duv * (1.0 + sc_ref[0])
            xin = xhat * g_ref[...] + b_ref[...] if has_ln else xin_ref[...]

            @pl.when(first_tile)
            def _():
                dsc_ref[...] = jnp.zeros_like(dsc_ref)
                dsh_ref[...] = jnp.zeros_like(dsh_ref)

            dsc_ref[0] += jnp.sum(duv * xin, axis=0, keepdims=True)
            dsh_ref[0] += jnp.sum(duv, axis=0, keepdims=True)
        if not has_ln:
            dx_ref[...] = dout
            return

        @pl.when(first_step)
        def _():
            dg_ref[...] = jnp.zeros_like(dg_ref)
            db_ref[...] = jnp.zeros_like(db_ref)

        @pl.when(first_tile)
        def _():
            dgt_ref[...] = jnp.zeros_like(dgt_ref)

        dg_ref[...] += jnp.sum(dout * xhat, axis=0, keepdims=True)
        db_ref[...] += jnp.sum(dout, axis=0, keepdims=True)
        dxh = dout * g_ref[...]
        dr = rstd * (dxh - jnp.mean(dxh, axis=-1, keepdims=True) - xhat * jnp.mean(dxh * xhat, axis=-1, keepdims=True))
        dx_ref[...] = ALPHA * dr
        dy_ref[...] = ((1.0 + gt_ref[0]) * dr).astype(BF16)
        dgt_ref[0] += jnp.sum(dr * y_ref[...], axis=0, keepdims=True)

    tok = pl.BlockSpec((tm, d), lambda bb, i: (bb * nt + i, 0))
    vec = pl.BlockSpec((1, d), lambda bb, i: (0, 0))
    seq = pl.BlockSpec((1, 1, d), lambda bb, i: (bb, 0, 0))
    in_specs, args = [tok], [d_a]
    if has_mod:
        in_specs += [tok, _row_spec(d, scale[1])]
        args += [du, scale[0]]
    if has_ln:
        r, y, g, b, gate = ln
        in_specs += [tok, tok, vec, vec, _row_spec(d, gate[1])]
        args += [r, y, g, b, gate[0]]
    elif has_mod:
        in_specs.append(tok)
        args.append(x_in)
    names = ["dx"]
    out_shape, out_specs = [jax.ShapeDtypeStruct((t, d), F32)], [tok]
    if has_ln:
        names += ["dy", "dg", "db", "dgate"]
        out_shape += [jax.ShapeDtypeStruct((t, d), BF16), jax.ShapeDtypeStruct((1, d), F32),
                      jax.ShapeDtypeStruct((1, d), F32), jax.ShapeDtypeStruct((bl, 1, d), F32)]
        out_specs += [tok, vec, vec, seq]
    if has_mod:
        names += ["dscale", "dshift"]
        out_shape += [jax.ShapeDtypeStruct((bl, 1, d), F32)] * 2
        out_specs += [seq, seq]
    outs = pl.pallas_call(
        body, name=name, out_shape=out_shape, grid=(bl, nt), in_specs=in_specs, out_specs=out_specs,
        compiler_params=_params("arbitrary", "arbitrary"),
    )(*args)
    return dict(zip(names, outs))


def _silu(a):
    return a * jax.nn.sigmoid(a)


def silu_rows(a, name):
    def body(a_ref, o_ref):
        o_ref[...] = _silu(a_ref[...]).astype(BF16)

    return pl.pallas_call(body, name=name, out_shape=jax.ShapeDtypeStruct(a.shape, BF16))(a)


def _swiglu_tiles(t, f):
    return _tile(t, 1024, 8), _tile(f, 1536)


def swiglu_in(u, wt_gate, wt_up, name):
    t, d = u.shape
    f = wt_gate.shape[0]
    tm, tf = _swiglu_tiles(t, f)

    def body(u_ref, g_ref, w_ref, a_ref, b_ref, h_ref):
        uv = u_ref[...]
        a = _nt(uv, g_ref[...])
        b = _nt(uv, w_ref[...])
        a_ref[...] = a.astype(BF16)
        b_ref[...] = b.astype(BF16)
        h_ref[...] = (_silu(a) * b).astype(BF16)

    w_spec = pl.BlockSpec((tf, d), lambda i, j: (j, 0))
    o_spec = pl.BlockSpec((tm, tf), lambda i, j: (i, j))
    return pl.pallas_call(
        body, name=name,
        out_shape=[jax.ShapeDtypeStruct((t, f), BF16)] * 3,
        grid=(t // tm, f // tf), in_specs=[pl.BlockSpec((tm, d), lambda i, j: (i, 0)), w_spec, w_spec],
        out_specs=[o_spec, o_spec, o_spec], compiler_params=_params("arbitrary", "arbitrary"),
    )(u, wt_gate, wt_up)


def swiglu_out_backward(dy, w_down, a, b, name):
    t, d = dy.shape
    f = w_down.shape[0]
    tm, tf = _swiglu_tiles(t, f)

    def body(dy_ref, w_ref, a_ref, b_ref, da_ref, db_ref):
        dh = _nt(dy_ref[...], w_ref[...])
        av = a_ref[...].astype(F32)
        sig = jax.nn.sigmoid(av)
        da_ref[...] = (dh * b_ref[...].astype(F32) * (sig * (1.0 + av * (1.0 - sig)))).astype(BF16)
        db_ref[...] = (dh * (av * sig)).astype(BF16)

    spec = pl.BlockSpec((tm, tf), lambda i, j: (i, j))
    return pl.pallas_call(
        body, name=name, out_shape=[jax.ShapeDtypeStruct((t, f), BF16)] * 2, grid=(t // tm, f // tf),
        in_specs=[pl.BlockSpec((tm, d), lambda i, j: (i, 0)), pl.BlockSpec((tf, d), lambda i, j: (j, 0)), spec, spec],
        out_specs=[spec, spec], compiler_params=_params("arbitrary", "arbitrary"),
    )(dy, w_down, a, b)


def rope_tables(pos, inv_freq, sign, name):
    t = pos.shape[0]
    tm = _tile(t, 1024, 8)

    def body(p_ref, f_ref, s_ref, c_out, s_out):
        ang = p_ref[...] * f_ref[...]
        c_out[...] = jnp.cos(ang)
        s_out[...] = jnp.sin(ang) * s_ref[...]

    vec = pl.BlockSpec((1, LANES), lambda i: (0, 0))
    tab = pl.BlockSpec((tm, LANES), lambda i: (i, 0))
    return pl.pallas_call(
        body, name=name, out_shape=[jax.ShapeDtypeStruct((t, LANES), F32)] * 2, grid=(t // tm,),
        in_specs=[pl.BlockSpec((tm, 1), lambda i: (i, 0)), vec, vec], out_specs=[tab, tab],
        compiler_params=_params("arbitrary"),
    )(pos, inv_freq, sign)


def _rot_half(v):
    lane = lax.broadcasted_iota(jnp.int32, v.shape, v.ndim - 1)
    up = pltpu.roll(v, LANES - MLA_ROPE // 2, v.ndim - 1)
    down = pltpu.roll(v, MLA_ROPE // 2, v.ndim - 1)
    return jnp.where(lane % MLA_ROPE < MLA_ROPE // 2, up, down)


def _rope(v, cos, sin_signed):
    return v * cos + _rot_half(v) * sin_signed


def _rope_transposed(dv, cos, sin_signed):
    return dv * cos + _rot_half(dv * sin_signed)


def rope_slabs(v, cos, sin_signed, out_dtype, name, transposed=False):
    ns, t, _ = v.shape
    tm = _tile(t, 1024, 8)
    fn = _rope_transposed if transposed else _rope

    def body(v_ref, c_ref, s_ref, o_ref):
        for j in range(ns):
            o_ref[j] = fn(v_ref[j].astype(F32), c_ref[...], s_ref[...]).astype(out_dtype)

    tab = pl.BlockSpec((tm, LANES), lambda i: (i, 0))
    spec = pl.BlockSpec((ns, tm, LANES), lambda i: (0, i, 0))
    return pl.pallas_call(
        body, name=name, out_shape=jax.ShapeDtypeStruct(v.shape, out_dtype), grid=(t // tm,),
        in_specs=[spec, tab, tab], out_specs=spec, compiler_params=_params("arbitrary"),
    )(v, cos, sin_signed)


def _rms(x):
    rinv = lax.rsqrt(jnp.mean(x * x, axis=-1, keepdims=True) + NORM_EPS)
    return x * rinv, rinv


def mla_latents_forward(h_in, g_q, g_kv, cos, sin_signed, name):
    t = h_in.shape[0]
    tm = _tile(t, 1024, 8)

    def body(h_ref, gq_ref, gkv_ref, c_ref, s_ref, cq_ref, ckv_ref, kr_ref):
        cq_ref[...] = (_rms(h_ref[:, 0:MLA_QR])[0] * gq_ref[...]).astype(BF16)
        ckv_ref[...] = (_rms(h_ref[:, MLA_QR:MLA_QR + MLA_KVR])[0] * gkv_ref[...]).astype(BF16)
        kr_ref[...] = _rope(h_ref[:, MLA_QR + MLA_KVR:], c_ref[...], s_ref[...]).astype(BF16)

    def tok(w):
        return pl.BlockSpec((tm, w), lambda i: (i, 0))

    def vec(w):
        return pl.BlockSpec((1, w), lambda i: (0, 0))

    return pl.pallas_call(
        body, name=name,
        out_shape=[jax.ShapeDtypeStruct((t, MLA_QR), BF16), jax.ShapeDtypeStruct((t, MLA_KVR), BF16),
                   jax.ShapeDtypeStruct((t, LANES), BF16)],
        grid=(t // tm,),
        in_specs=[tok(h_in.shape[1]), vec(MLA_QR), vec(MLA_KVR), tok(LANES), tok(LANES)],
        out_specs=[tok(MLA_QR), tok(MLA_KVR), tok(LANES)],
        compiler_params=_params("arbitrary"),
    )(h_in, g_q, g_kv, cos, sin_signed)


def mla_latents_backward(h_in, dcq, dckv, dkr, g_q, g_kv, cos, sin_signed, name):
    t, w = h_in.shape
    tm = _tile(t, 1024, 8)

    def body(h_ref, dcq_ref, dckv_ref, dkr_ref, gq_ref, gkv_ref, c_ref, s_ref, dh_ref, dgq_ref, dgkv_ref):
        @pl.when(pl.program_id(0) == 0)
        def _():
            dgq_ref[...] = jnp.zeros_like(dgq_ref)
            dgkv_ref[...] = jnp.zeros_like(dgkv_ref)

        def rms_bwd(x, dc, g_ref, dg_ref):
            xn, rinv = _rms(x)
            dg_ref[...] += jnp.sum(dc * xn, axis=0, keepdims=True)
            dxn = dc * g_ref[...]
            return rinv * (dxn - xn * jnp.mean(dxn * xn, axis=-1, keepdims=True))

        dq = rms_bwd(h_ref[:, 0:MLA_QR], dcq_ref[...], gq_ref, dgq_ref)
        dkv = rms_bwd(h_ref[:, MLA_QR:MLA_QR + MLA_KVR], dckv_ref[...], gkv_ref, dgkv_ref)
        dr = _rope_transposed(dkr_ref[...], c_ref[...], s_ref[...])
        dh_ref[...] = jnp.concatenate([dq, dkv, dr], axis=1).astype(BF16)

    def tok(ww):
        return pl.BlockSpec((tm, ww), lambda i: (i, 0))

    def vec(ww):
        return pl.BlockSpec((1, ww), lambda i: (0, 0))

    return pl.pallas_call(
        body, name=name,
        out_shape=[jax.ShapeDtypeStruct((t, w), BF16), jax.ShapeDtypeStruct((1, MLA_QR), F32),
                   jax.ShapeDtypeStruct((1, MLA_KVR), F32)],
        grid=(t // tm,),
        in_specs=[tok(w), tok(MLA_QR), tok(MLA_KVR), tok(LANES), vec(MLA_QR), vec(MLA_KVR), tok(LANES), tok(LANES)],
        out_specs=[tok(w), vec(MLA_QR), vec(MLA_KVR)],
        compiler_params=_params("arbitrary"),
    )(h_in, dcq, dckv, dkr, g_q, g_kv, cos, sin_signed)


def _tri(n, lower):
    r = lax.broadcasted_iota(jnp.int32, (n, n), 0)
    c = lax.broadcasted_iota(jnp.int32, (n, n), 1)
    return jnp.where(r >= c if lower else r <= c, 1.0, 0.0).astype(F32)


def _dot_exact(tri, v):
    hi = v.astype(BF16)
    mid = (v - hi.astype(F32)).astype(BF16)
    lo = (v - hi.astype(F32) - mid.astype(F32)).astype(BF16)
    t = tri.astype(BF16)
    return _nn(t, hi) + _nn(t, mid) + _nn(t, lo)


def fox_gate_forward(z, b_f, bl, name):
    t = z.shape[0]
    s = t // bl
    ch = LANES
    n_ch = s // ch

    def body(z_ref, b_ref, f_ref, fs_ref):
        tri = _tri(ch, True)
        carry = jnp.zeros((1, LANES), F32)
        for k in range(n_ch):
            x = z_ref[k * ch:(k + 1) * ch, :] + b_ref[...]
            logf = jnp.minimum(x, 0.0) - jnp.log(1.0 + jnp.exp(-jnp.abs(x)))
            cs = _dot_exact(tri, logf) + carry
            carry = cs[ch - 1:ch, :]
            f_ref[k * ch:(k + 1) * ch, :] = cs
            for h in range(FOX_HEADS):
                fs_ref[h, k * ch:(k + 1) * ch, :] = jnp.broadcast_to(cs[:, h:h + 1], (ch, LANES))

    return pl.pallas_call(
        body, name=name,
        out_shape=[jax.ShapeDtypeStruct((t, LANES), F32), jax.ShapeDtypeStruct((FOX_HEADS, t, LANES), F32)],
        grid=(bl,),
        in_specs=[pl.BlockSpec((s, LANES), lambda b: (b, 0)), pl.BlockSpec((1, LANES), lambda b: (0, 0))],
        out_specs=[pl.BlockSpec((s, LANES), lambda b: (b, 0)),
                   pl.BlockSpec((FOX_HEADS, s, LANES), lambda b: (0, b, 0))],
        compiler_params=_params("arbitrary"),
    )(z, b_f)


def fox_gate_backward(z, b_f, df, bl, name):
    t = z.shape[0]
    s = t // bl
    ch = LANES
    n_ch = s // ch

    def body(z_ref, b_ref, df_ref, dz_ref, db_ref):
        @pl.when(pl.program_id(0) == 0)
        def _():
            db_ref[...] = jnp.zeros_like(db_ref)

        tri = _tri(ch, False)
        carry = jnp.zeros((1, LANES), F32)
        for k in reversed(range(n_ch)):
            cs = _dot_exact(tri, df_ref[k * ch:(k + 1) * ch, :]) + carry
            carry = cs[0:1, :]
            x = z_ref[k * ch:(k + 1) * ch, :] + b_ref[...]
            dz = cs * (1.0 - jax.nn.sigmoid(x))
            dz_ref[k * ch:(k + 1) * ch, :] = dz
            db_ref[...] += jnp.sum(dz, axis=0, keepdims=True)

    tok = pl.BlockSpec((s, LANES), lambda b: (b, 0))
    vec = pl.BlockSpec((1, LANES), lambda b: (0, 0))
    return pl.pallas_call(
        body, name=name,
        out_shape=[jax.ShapeDtypeStruct((t, LANES), F32), jax.ShapeDtypeStruct((1, LANES), F32)],
        grid=(bl,), in_specs=[tok, vec, tok], out_specs=[tok, vec],
        compiler_params=_params("arbitrary"),
    )(z, b_f, df)


NEG_INF = float("-inf")


def _attn_tiles(s):
    return _tile(s, 1024, 8)


def attention_forward(kind, ops, bl, scale, name):
    fox = kind == "fox"
    if fox:
        assert math.frexp(scale)[0] == 0.5, "the FoX scale is folded into bf16 queries: it must be a power of two"
        qkv, fq, fk = ops
        t = qkv.shape[1]
        n_pair = FOX_HEADS // 2
    else:
        qn, qr, kn, kr, v = ops
        t = qn.shape[1]
        n_pair = MLA_HEADS // 2
    s = t // bl
    tq = _attn_tiles(s)
    nq = s // tq
    half = LANES // 2

    def body(*refs):
        if fox:
            q_ref, k_ref, v_ref, fq_ref, fk_ref, o_ref, lse_ref, o32_ref = refs
        else:
            qn_ref, qr_ref, kn_ref, kr_ref, v_ref, o_ref, lse_ref = refs
        i = pl.program_id(2)
        row = lax.broadcasted_iota(jnp.int32, (tq, tq), 0)
        col = lax.broadcasted_iota(jnp.int32, (tq, tq), 1)
        heads = []
        for e in range(2):
            sl = slice(e * half, (e + 1) * half)
            if fox:
                heads.append((sl, q_ref[0, :, sl] * jnp.asarray(scale, BF16), None))
            else:
                heads.append((sl, jnp.concatenate([qn_ref[e], qr_ref[0, :, sl], jnp.zeros((tq, half), BF16)], axis=1),
                              None))
        dv = half if fox else LANES

        def wide(stat):
            return jnp.concatenate([stat] * (tq // LANES), axis=1)

        def step(j, carry, masked):
            rows = pl.ds(pl.multiple_of(j * tq, tq), tq)
            new = []
            for e, (sl, qa, qb) in enumerate(heads):
                m, l, acc = carry[e]
                if fox:
                    sc = _nt(qa, k_ref[0, rows, sl]) + wide(fq_ref[e]) - fk_ref[0, j, e:e + 1, :]
                    vv = v_ref[0, rows, sl]
                else:
                    k_cat = jnp.concatenate([kn_ref[e, rows, :], kr_ref[rows, :]], axis=1)
                    sc = _nt(qa, k_cat) * scale
                    vv = v_ref[e, rows, :]
                if masked:
                    sc = jnp.where(row >= col, sc, NEG_INF)
                m_new = jnp.maximum(m, jnp.max(sc, axis=1, keepdims=True))
                p = jnp.exp(sc - m_new)
                a = jnp.exp(m - m_new)
                p_hi = p.astype(BF16)
                if fox:
                    vv = jnp.concatenate([vv, ones], axis=1)
                    acc = a * acc + _nn(p_hi, vv) + _nn((p - p_hi.astype(F32)).astype(BF16), vv)
                else:
                    l = a * l + jnp.sum(p, axis=1, keepdims=True)
                    acc = a * acc + _nn(p_hi, vv)
                new.append((m_new, l, acc))
            return tuple(new)

        ones = jnp.ones((tq, half), BF16)
        acc_w = LANES if fox else dv
        init = (jnp.full((tq, 1), NEG_INF, F32), jnp.zeros((tq, 1), F32), jnp.zeros((tq, acc_w), F32))
        carry = step(i, (init, init), True)
        carry = lax.fori_loop(0, i, lambda j, c: step(j, c, False), carry)
        if fox:
            carry = [(m, acc[:, dv:dv + 1], acc[:, :dv]) for m, _, acc in carry]
        outs = [acc / l for _, l, acc in carry]
        for e, (m, l, _) in enumerate(carry):
            lse_ref[e] = jnp.broadcast_to(m + jnp.log(l), (tq, LANES))
        if fox:
            o32 = jnp.concatenate(outs, axis=1)
            o32_ref[0] = o32
            o_ref[0] = o32.astype(BF16)
        else:
            o_ref[0] = outs[0].astype(BF16)
            o_ref[1] = outs[1].astype(BF16)

    def q_idx(b, g, i):
        return (g, b * nq + i, 0)

    if fox:
        nk = fk.shape[1]
        in_specs = [pl.BlockSpec((1, tq, LANES), q_idx),
                    pl.BlockSpec((1, s, LANES), lambda b, g, i: (n_pair + g, b, 0)),
                    pl.BlockSpec((1, s, LANES), lambda b, g, i: (2 * n_pair + g, b, 0)),
                    pl.BlockSpec((2, tq, LANES), q_idx),
                    pl.BlockSpec((1, nk, 8, tq), lambda b, g, i: (b * n_pair + g, 0, 0, 0))]
        args = [qkv, qkv, qkv, fq, fk]
        o_spec = pl.BlockSpec((1, tq, LANES), q_idx)
    else:
        in_specs = [pl.BlockSpec((2, tq, LANES), q_idx),
                    pl.BlockSpec((1, tq, LANES), q_idx),
                    pl.BlockSpec((2, s, LANES), lambda b, g, i: (g, b, 0)),
                    pl.BlockSpec((s, LANES), lambda b, g, i: (b, 0)),
                    pl.BlockSpec((2, s, LANES), lambda b, g, i: (g, b, 0))]
        args = [qn, qr, kn, kr, v]
        o_spec = pl.BlockSpec((2, tq, LANES), q_idx)
    out_shape = [jax.ShapeDtypeStruct((8, t, LANES), BF16), jax.ShapeDtypeStruct((2 * n_pair, t, LANES), F32)]
    out_specs = [o_spec, pl.BlockSpec((2, tq, LANES), q_idx)]
    if fox:
        out_shape.append(jax.ShapeDtypeStruct((8, t, LANES), F32))
        out_specs.append(o_spec)
    outs = pl.pallas_call(
        body, name=name, out_shape=out_shape, grid=(bl, n_pair, nq), in_specs=in_specs, out_specs=out_specs,
        compiler_params=_params("arbitrary", "arbitrary", "arbitrary"),
    )(*args)
    return (outs[0], outs[1], outs[2] if fox else outs[0])


def attention_backward(kind, ops, o, do, lse, bl, scale, name):
    fox = kind == "fox"
    if fox:
        assert math.frexp(scale)[0] == 0.5, "the FoX scale is folded into bf16 queries: it must be a power of two"
        qkv, fq, fk = ops
        t = qkv.shape[1]
        n_pair = FOX_HEADS // 2
    else:
        qn, qr, kn, kr, v = ops
        t = qn.shape[1]
        n_pair = MLA_HEADS // 2
    s = t // bl
    tq = _attn_tiles(s)
    nq = s // tq
    half = LANES // 2

    def body(*refs):
        if fox:
            (q_ref, k_ref, v_ref, fq_ref, fk_ref, o_ref, do_ref, lse_ref,
             dq_ref, dk_ref, dv_ref, dfk_ref, delta_scr, qt_scr, dot_scr) = refs
        else:
            (qn_ref, qr_ref, kn_ref, kr_ref, v_ref, o_ref, do_ref, lse_ref,
             dqn_ref, dqr_ref, dkn_ref, dv_ref, dkr_ref, delta_scr, qt_scr, qrt_scr, dot_scr) = refs
        g, j = pl.program_id(1), pl.program_id(2)
        row = lax.broadcasted_iota(jnp.int32, (tq, tq), 0)
        col = lax.broadcasted_iota(jnp.int32, (tq, tq), 1)
        krows = pl.ds(pl.multiple_of(j * tq, tq), tq)
        q_scale = jnp.asarray(scale, BF16)

        def transposed(v):
            return v.astype(F32).T.astype(BF16)

        def wide(stat):
            return jnp.concatenate([stat] * (tq // LANES), axis=1)

        @pl.when(j == 0)
        def _():
            if fox:
                dq_ref[...] = jnp.zeros_like(dq_ref)
            else:
                dqn_ref[...] = jnp.zeros_like(dqn_ref)
                dqr_ref[...] = jnp.zeros_like(dqr_ref)
            for ii in range(nq):
                rws = slice(ii * tq, (ii + 1) * tq)
                deltas = []
                if fox:
                    prod = do_ref[0, rws, :].astype(F32) * o_ref[0, rws, :].astype(F32)
                    for e in range(2):
                        deltas.append(jnp.sum(prod[:, e * half:(e + 1) * half], axis=1, keepdims=True))
                    qt_scr[ii] = transposed(q_ref[0, rws, :] * q_scale)
                    dot_scr[ii] = transposed(do_ref[0, rws, :])
                else:
                    for e in range(2):
                        prod = do_ref[e, rws, :].astype(F32) * o_ref[e, rws, :].astype(F32)
                        deltas.append(jnp.sum(prod, axis=1, keepdims=True))
                        qt_scr[e, ii] = transposed(qn_ref[e, rws, :])
                        dot_scr[e, ii] = transposed(do_ref[e, rws, :])
                    qrt_scr[ii] = transposed(qr_ref[0, rws, :])
                for e in range(2):
                    delta_scr[e, rws, :] = jnp.broadcast_to(deltas[e], (tq, LANES))

        if fox:
            dfk_ref[...] = jnp.zeros_like(dfk_ref)
        else:
            @pl.when(jnp.logical_and(g == 0, j == 0))
            def _():
                dkr_ref[...] = jnp.zeros_like(dkr_ref)

        heads = []
        for e in range(2):
            sl = slice(e * half, (e + 1) * half)
            if fox:
                heads.append((sl, k_ref[0, :, sl], v_ref[0, :, sl], fk_ref[0, 0, e:e + 1, :]))
            else:
                heads.append((sl, jnp.concatenate([kn_ref[e], kr_ref[krows, :]], axis=1), v_ref[e], None))
        dk_w = dv_w = half if fox else LANES

        def step(i, carry, masked):
            rows = pl.ds(pl.multiple_of(i * tq, tq), tq)
            new = []
            for e, (sl, k_e, v_e, x_e) in enumerate(heads):
                dk_acc, dv_acc, last = carry[e]
                if fox:
                    do_i = do_ref[0, rows, sl]
                    sc = _nt(q_ref[0, rows, sl] * q_scale, k_e) + wide(fq_ref[e, rows, :]) - x_e
                else:
                    do_i = do_ref[e, rows, :]
                    q_cat = jnp.concatenate([qn_ref[e, rows, :], qr_ref[0, rows, sl], jnp.zeros((tq, half), BF16)], axis=1)
                    sc = _nt(q_cat, k_e) * scale
                if masked:
                    sc = jnp.where(row >= col, sc, NEG_INF)
                p = jnp.exp(sc - wide(lse_ref[e, rows, :]))
                dp = _nt(do_i, v_e)
                ds = p * (dp - wide(delta_scr[e, rows, :]))
                dsb = ds.astype(BF16) if fox else (ds * scale).astype(BF16)
                if fox:
                    fsl = slice(e * half, (e + 1) * half)
                    dv_acc = dv_acc + _nn(dot_scr[i, fsl, :], p.astype(BF16))
                    dk_acc = dk_acc + _nn(qt_scr[i, fsl, :], dsb)
                    dq_ref[0, rows, sl] += _nn(dsb, k_e) * scale
                    last = last - jnp.sum(ds, axis=0, keepdims=True)
                else:
                    dv_acc = dv_acc + _nn(dot_scr[e, i], p.astype(BF16))
                    dk_acc = dk_acc + _nn(qt_scr[e, i], dsb)
                    dq_cat = _nn(dsb, k_e)
                    dqn_ref[e, rows, :] += dq_cat[:, :LANES]
                    dqr_ref[0, rows, sl] += dq_cat[:, LANES:LANES + half]
                    last = last + _nn(qrt_scr[i, e * half:(e + 1) * half, :], dsb)
                new.append((dk_acc, dv_acc, last))
            return tuple(new)

        last0 = jnp.zeros((1, tq), F32) if fox else jnp.zeros((half, tq), F32)
        init = (jnp.zeros((dk_w, tq), F32), jnp.zeros((dv_w, tq), F32), last0)
        carry = step(j, (init, init), True)
        carry = lax.fori_loop(j + 1, nq, lambda i, c: step(i, c, False), carry)
        if fox:
            for e in range(2):
                dfk_ref[0, 0, e:e + 1, :] = carry[e][2]
            dk_ref[0] = jnp.concatenate([carry[0][0], carry[1][0]], axis=0).T.astype(BF16)
            dv_ref[0] = jnp.concatenate([carry[0][1], carry[1][1]], axis=0).T.astype(BF16)
        else:
            for e in range(2):
                dkn_ref[e] = carry[e][0].T.astype(BF16)
                dv_ref[e] = carry[e][1].T.astype(BF16)
            dkr_t = carry[0][2] + carry[1][2]
            dkr_ref[krows, :] += jnp.concatenate([dkr_t, jnp.zeros_like(dkr_t)], axis=0).T

    def whole(b, g, j):
        return (g, b, 0)

    def kblk(b, g, j):
        return (g, b * nq + j, 0)

    if fox:
        in_specs = [pl.BlockSpec((1, s, LANES), whole),
                    pl.BlockSpec((1, tq, LANES), lambda b, g, j: (n_pair + g, b * nq + j, 0)),
                    pl.BlockSpec((1, tq, LANES), lambda b, g, j: (2 * n_pair + g, b * nq + j, 0)),
                    pl.BlockSpec((2, s, LANES), whole),
                    pl.BlockSpec((1, 1, 8, tq), lambda b, g, j: (b * n_pair + g, j, 0, 0)),
                    pl.BlockSpec((1, s, LANES), whole), pl.BlockSpec((1, s, LANES), whole),
                    pl.BlockSpec((2, s, LANES), whole)]
        args = [qkv, qkv, qkv, fq, fk, o, do, lse]
        out_shape = [jax.ShapeDtypeStruct((8, t, LANES), F32), jax.ShapeDtypeStruct((8, t, LANES), BF16),
                     jax.ShapeDtypeStruct((8, t, LANES), BF16), jax.ShapeDtypeStruct(fk.shape, F32)]
        out_specs = [pl.BlockSpec((1, s, LANES), whole), pl.BlockSpec((1, tq, LANES), kblk),
                     pl.BlockSpec((1, tq, LANES), kblk),
                     pl.BlockSpec((1, 1, 8, tq), lambda b, g, j: (b * n_pair + g, j, 0, 0))]
    else:
        pair = pl.BlockSpec((2, s, LANES), whole)
        pair_k = pl.BlockSpec((2, tq, LANES), kblk)
        in_specs = [pair, pl.BlockSpec((1, s, LANES), whole), pair_k,
                    pl.BlockSpec((s, LANES), lambda b, g, j: (b, 0)), pair_k,
                    pair, pair, pair]
        args = [qn, qr, kn, kr, v, o, do, lse]
        out_shape = [jax.ShapeDtypeStruct((8, t, LANES), F32), jax.ShapeDtypeStruct((4, t, LANES), F32),
                     jax.ShapeDtypeStruct((8, t, LANES), BF16), jax.ShapeDtypeStruct((8, t, LANES), BF16),
                     jax.ShapeDtypeStruct((t, LANES), F32)]
        out_specs = [pair, pl.BlockSpec((1, s, LANES), whole), pair_k, pair_k,
                     pl.BlockSpec((s, LANES), lambda b, g, j: (b, 0))]
    t_blocks = pltpu.VMEM((nq, LANES, tq), BF16)
    t_pairs = pltpu.VMEM((2, nq, LANES, tq), BF16)
    scratch = [pltpu.VMEM((2, s, LANES), F32)] + ([t_blocks, t_blocks] if fox else [t_pairs, t_blocks, t_pairs])
    return pl.pallas_call(
        body, name=name, out_shape=out_shape, grid=(bl, n_pair, nq), in_specs=in_specs, out_specs=out_specs,
        scratch_shapes=scratch, compiler_params=_params("arbitrary", "arbitrary", "arbitrary"),
    )(*args)


def adamw(w, g, m, v, name):
    shape = w.shape
    c = shape[-1]
    r = w.size // c
    tr = _tile(r, 512, 8)

    def body(w_ref, g_ref, m_ref, v_ref, d_ref, nm_ref, nv_ref):
        gv = g_ref[...]
        m2 = ADAM_B1 * m_ref[...] + (1.0 - ADAM_B1) * gv
        v2 = ADAM_B2 * v_ref[...] + (1.0 - ADAM_B2) * (gv * gv)
        m_hat = m2 / (1.0 - ADAM_B1 ** ADAM_STEP)
        v_hat = v2 / (1.0 - ADAM_B2 ** ADAM_STEP)
        d_ref[...] = -ADAM_LR * (m_hat / (jnp.sqrt(v_hat) + ADAM_EPS) + ADAM_WD * w_ref[...])
        nm_ref[...] = m2
        nv_ref[...] = v2

    spec = pl.BlockSpec((tr, c), lambda i: (i, 0))
    outs = pl.pallas_call(
        body, name=name, out_shape=[jax.ShapeDtypeStruct((r, c), F32)] * 3, grid=(r // tr,),
        in_specs=[spec] * 4, out_specs=[spec] * 3, compiler_params=_params("arbitrary"),
    )(*(a.reshape(r, c) for a in (w, g, m, v)))
    return tuple(a.reshape(shape) for a in outs)


PACK_COLS = 1024


def _pack_rows(a):
    return a.reshape(-1, PACK_COLS)


def kernel(x, c, positions, mla_w_in, mla_g_q, mla_w_uq, mla_g_kv, mla_w_uk, mla_w_uv, mla_w_o, fox_w_in, fox_b_f, fox_w_o, ada_w, ada_b, ffn_w_gate, ffn_w_up, ffn_w_down, ln_g, ln_b, loss_target, m_mla_w_in, m_mla_g_q, m_mla_w_uq, m_mla_g_kv, m_mla_w_uk, m_mla_w_uv, m_mla_w_o, m_fox_w_in, m_fox_b_f, m_fox_w_o, m_ada_w, m_ada_b, m_ffn_w_gate, m_ffn_w_up, m_ffn_w_down, m_ln_g, m_ln_b, v_mla_w_in, v_mla_g_q, v_mla_w_uq, v_mla_g_kv, v_mla_w_uk, v_mla_w_uv, v_mla_w_o, v_fox_w_in, v_fox_b_f, v_fox_w_o, v_ada_w, v_ada_b, v_ffn_w_gate, v_ffn_w_up, v_ffn_w_down, v_ln_g, v_ln_b):
    bl, s, d = x.shape
    t = bl * s
    ff = ffn_w_gate.shape[-1] * N_DEV
    dev = 4 * lax.axis_index("x") + 2 * lax.axis_index("y") + lax.axis_index("c")
    ada_cols = ada_w.shape[-1]
    mla_in = mla_w_in.shape[-1]
    mla_in_pad = mla_in + (-mla_in) % LANES

    def t_last(a):
        return jnp.swapaxes(a, -1, -2)

    local = {
        "mla_w_in": mla_w_in[0],
        "mla_w_uq": t_last(mla_w_uq[0]),
        "mla_w_uk": t_last(mla_w_uk[0]),
        "mla_w_uv": t_last(mla_w_uv[0]),
        "mla_w_o": mla_w_o[0],
        "fox_w_in": t_last(fox_w_in[0]),
        "fox_w_o": fox_w_o[0],
    }
    for i in range(DEPTH):
        local.update({f"gate{i}": t_last(ffn_w_gate[i]), f"up{i}": t_last(ffn_w_up[i]), f"down{i}": ffn_w_down[i]})
    groups = [["mla_w_in", "mla_w_uq", "mla_w_uk", "mla_w_uv", "mla_w_o"],
              ["gate0", "up0", "down0"],
              ["fox_w_in", "fox_w_o"],
              ["gate1", "up1", "down1"]]
    offsets, rows_of, slot_of, group_of = {}, {}, {}, {}
    group_rows = []
    for gi, names in enumerate(groups):
        rows = 0
        for nm in names:
            rows_of[nm] = local[nm].size // PACK_COLS
            slot_of[nm] = rows_of[nm] + (-rows_of[nm]) % 16
            offsets[nm] = rows
            group_of[nm] = gi
            rows += slot_of[nm]
        group_rows.append(rows)

    def slot(nm, rows):
        pad = [(0, 0)] * rows.ndim
        pad[-2] = (0, slot_of[nm] - rows_of[nm])
        return jnp.pad(rows, pad)

    def held_until(block, arrays):
        zero = sum((a.reshape(-1)[0] * 0).astype(F32) for a in jax.tree.leaves(arrays))
        return block + zero.astype(block.dtype)

    def landing(block):
        land = lax.empty((N_DEV,) + block.shape, block.dtype)
        return lax.dynamic_update_slice(land, block[None], (dev, 0, 0))

    packed0 = jnp.concatenate([slot(nm, _pack_rows(local[nm]).astype(BF16)) for nm in groups[0]], axis=0)
    gathered0 = all_gather(packed0, "gather_mla_weights")
    gathered = {nm: gathered0[:, offsets[nm]:offsets[nm] + rows_of[nm], :] for nm in groups[0]}
    gather_started = [None] * len(groups)

    def depart(gi, after):
        blocks = [held_until(_pack_rows(local[nm]).astype(BF16), after) for nm in groups[gi]]
        gather_started[gi] = exchange_start(blocks, [landing(b) for b in blocks], f"gather_group{gi}_start", False)
        return gather_started[gi][4]

    def full(nm, cols):
        return gathered[nm].reshape(-1, cols)

    w_in = jnp.pad(full("mla_w_in", mla_in), ((0, 0), (0, mla_in_pad - mla_in)))
    wt_uq = full("mla_w_uq", MLA_QR).reshape(MLA_HEADS, MLA_NOPE + MLA_ROPE, MLA_QR)
    wt_uq_n = wt_uq[:, :MLA_NOPE].reshape(MLA_HEADS * MLA_NOPE, MLA_QR)
    wt_uq_r = wt_uq[:, MLA_NOPE:].reshape(MLA_HEADS * MLA_ROPE, MLA_QR)
    wt_uk = full("mla_w_uk", MLA_KVR)
    wt_uv = full("mla_w_uv", MLA_KVR)
    w_mo = full("mla_w_o", d)
    wt_gate, wt_up, w_down = [None] * DEPTH, [None] * DEPTH, [None] * DEPTH

    def arrive(gi, after):
        if gi + 1 < len(groups):
            after = depart(gi + 1, after)
        landed = list(exchange_wait(gather_started[gi], after, f"gather_group{gi}_wait", False))
        gathered.update(zip(groups[gi], landed))
        for i in range(DEPTH):
            if group_of[f"gate{i}"] == gi:
                wt_gate[i], wt_up[i], w_down[i] = full(f"gate{i}", d), full(f"up{i}", d), full(f"down{i}", d)

    small = jnp.concatenate([c.reshape(-1, LANES), ln_g.reshape(-1, LANES), ln_b.reshape(-1, LANES)], axis=0)
    small_rows = small.shape[0]
    small = jnp.pad(small, ((0, (-small_rows) % 8), (0, 0)))
    small_all = all_gather(small, "gather_small")
    c_rows = bl * d // LANES
    c_all = small_all[:, :c_rows].reshape(N_DEV * bl, d)
    n_ln = DEPTH * 2
    ln_g_all = small_all[:, c_rows:c_rows + n_ln, :].transpose(1, 0, 2).reshape(DEPTH, 2, 1, d)
    ln_b_all = small_all[:, c_rows + n_ln:c_rows + 2 * n_ln, :].transpose(1, 0, 2).reshape(DEPTH, 2, 1, d)

    c_act = silu_rows(c_all, "silu_c")
    ada_b_loc = lax.dynamic_slice_in_dim(ada_b, dev * ada_cols, ada_cols, axis=1)
    mod_cols = [mm([(c_act, ada_w[i])], trans_b=False, out_dtype=F32, name=f"ada_fwd{i}", bias=ada_b_loc[i][None, :])
                for i in range(DEPTH)]
    mod_all = all_gather(jnp.concatenate(mod_cols, axis=0), "gather_mod")
    mod_all = mod_all.reshape(N_DEV, DEPTH, N_DEV * bl, ada_cols).transpose(1, 2, 0, 3).reshape(DEPTH, N_DEV * bl, 6 * d)
    mod_mine = lax.dynamic_slice_in_dim(mod_all, dev * bl, bl, axis=1)
    mods = [mod_mine[i].reshape(bl * 6, 1, d) for i in range(DEPTH)]
    mods[0] = mods[0] + depart(1, (mod_mine, gathered0))[0, 0]

    half_r = MLA_ROPE // 2
    inv_freq = ROPE_THETA ** (-jnp.arange(half_r, dtype=F32) / half_r)
    inv_freq = jnp.tile(inv_freq, LANES // half_r)[None, :]
    sign = jnp.tile(jnp.concatenate([-jnp.ones((half_r,), F32), jnp.ones((half_r,), F32)]), LANES // MLA_ROPE)[None, :]
    cos_t, sin_t = rope_tables(positions.astype(F32).reshape(t, 1), inv_freq, sign, "rope_tables")

    x2d = x.reshape(t, d)
    g_q, g_kv = mla_g_q.reshape(1, MLA_QR), mla_g_kv.reshape(1, MLA_KVR)
    b_f = jnp.pad(fox_b_f.reshape(1, FOX_HEADS), ((0, 0), (0, LANES - FOX_HEADS)))
    mla_scale = (MLA_NOPE + MLA_ROPE) ** -0.5
    fox_scale = FOX_HD ** -0.5
    tq = _attn_tiles(s)
    nk = s // tq

    saved = []
    u = modulate(x2d, mods[0], 0, 1, bl, "modulate0")
    xin = x2d
    for i in range(DEPTH):
        sv = {"u": u, "x_in": xin}
        if i % 2 == 0:
            h_in = mm([(u, w_in)], trans_b=False, out_dtype=F32, name=f"mla_in{i}")
            c_q, c_kv, k_r = mla_latents_forward(h_in, g_q, g_kv, cos_t, sin_t, f"mla_latents{i}")
            q_n = mm([(c_q, wt_uq_n)], trans_b=True, out_dtype=BF16, out_slab=True, name=f"mla_qn{i}")
            q_r_raw = mm([(c_q, wt_uq_r)], trans_b=True, out_dtype=F32, out_slab=True, name=f"mla_qr{i}")
            q_r = rope_slabs(q_r_raw, cos_t, sin_t, BF16, f"mla_qrope{i}")
            k_n = mm([(c_kv, wt_uk)], trans_b=True, out_dtype=BF16, out_slab=True, name=f"mla_kn{i}")
            v_m = mm([(c_kv, wt_uv)], trans_b=True, out_dtype=BF16, out_slab=True, name=f"mla_v{i}")
            ops = (q_n, q_r, k_n, k_r, v_m)
            o, lse, o_delta = attention_forward("mla", ops, bl, mla_scale, f"mla_attn{i}")
            y = mm([(o, w_mo)], trans_b=False, out_dtype=F32, name=f"mla_out{i}")
            sv.update(h_in=h_in, c_q=c_q, c_kv=c_kv, ops=ops, o=o, lse=lse, o_delta=o_delta)
        else:
            arrive(2, u)
            wt_fox = full("fox_w_in", d)
            wt_qkv = wt_fox[:3 * d]
            wt_f = jnp.pad(wt_fox[3 * d:], ((0, LANES - FOX_HEADS), (0, 0)))
            w_fo = full("fox_w_o", d)
            qkv = mm([(u, wt_qkv)], trans_b=True, out_dtype=BF16, out_slab=True, name=f"fox_qkv{i}")
            z = mm([(u, wt_f)], trans_b=True, out_dtype=F32, name=f"fox_z{i}")
            f_tok, f_q = fox_gate_forward(z, b_f, bl, f"fox_gate{i}")
            f_k = f_tok[:, :FOX_HEADS].reshape(bl, nk, tq, FOX_HEADS // 2, 2).transpose(0, 3, 1, 4, 2)
            f_k = jnp.pad(f_k.reshape(bl * FOX_HEADS // 2, nk, 2, tq), ((0, 0), (0, 0), (0, 6), (0, 0)))
            ops = (qkv, f_q, f_k)
            o, lse, o_delta = attention_forward("fox", ops, bl, fox_scale, f"fox_attn{i}")
            y = mm([(o, w_fo)], trans_b=False, out_dtype=F32, name=f"fox_out{i}")
            sv.update(z=z, ops=ops, o=o, lse=lse, o_delta=o_delta)
        x1, r1, u2 = residual_layer_norm(xin, y, mods[i], 2, ln_g_all[i, 0], ln_b_all[i, 0], bl, f"ln_mix{i}",
                                         next_mod=(3, 4))
        if wt_gate[i] is None:
            arrive(group_of[f"gate{i}"], u2)
        a, bb, h = swiglu_in(u2, wt_gate[i], wt_up[i], f"ffn_in{i}")
        y2 = mm([(h, w_down[i])], trans_b=False, out_dtype=F32, name=f"ffn_down{i}")
        sv.update(y=y, r1=r1, u2=u2, a=a, bb=bb, h=h, y2=y2)
        if i + 1 < DEPTH:
            xin, r2, u = residual_layer_norm(x1, y2, mods[i], 5, ln_g_all[i, 1], ln_b_all[i, 1], bl, f"ln_ffn{i}",
                                             next_mod=(0, 1, mods[i + 1]))
        else:
            xin, r2 = residual_layer_norm(x1, y2, mods[i], 5, ln_g_all[i, 1], ln_b_all[i, 1], bl, f"ln_ffn{i}")
        sv.update(r2=r2)
        saved.append(sv)

    loss_cols, d_x = loss_head(xin, loss_target.reshape(t, d), "loss_head")

    grads_full = {}
    wgrad = functools.partial(mm_tn, out_dtype=BF16)
    dmod = [[None] * 6 for _ in range(DEPTH)]
    dg_ln = [[None, None] for _ in range(DEPTH)]
    db_ln = [[None, None] for _ in range(DEPTH)]
    dg_q = dg_kv = db_f = None
    d_a, du = d_x, None
    scatter_started = [None] * len(groups)

    def scatter_start(gi, after=None):
        gs = [grads_full[nm].reshape(N_DEV, rows_of[nm], PACK_COLS).astype(BF16) for nm in groups[gi]]
        if gi == 0:
            gs = [jnp.concatenate([slot(nm, g) for nm, g in zip(groups[gi], gs)], axis=1)]
        if after is not None:
            gs = [held_until(g, after) for g in gs]
        lands = [landing(lax.dynamic_index_in_dim(g, dev, 0, keepdims=False)) for g in gs]
        scatter_started[gi] = exchange_start(gs, lands, f"scatter_group{gi}_start", True)

    ln_g_bwd = [[ln_g_all[i, k] for k in range(2)] for i in range(DEPTH)]
    for i in reversed(range(DEPTH)):
        sv = saved[i]
        if i + 1 < DEPTH:
            gi = group_of["fox_w_in"]
            scatter_start(gi)
            ln_g_bwd[i][1] = after_token(ln_g_bwd[i][1], scatter_started[gi])
        ln2 = (sv["r2"], sv["y2"], ln_g_bwd[i][1], ln_b_all[i, 1], (mods[i], 5))
        if du is None:
            bw = sublayer_backward(d_a, bl, f"bwd_ln_ffn{i}", ln=ln2)
        else:
            bw = sublayer_backward(d_a, bl, f"bwd_ln_ffn{i}", du=du, scale=(mods[i + 1], 1), ln=ln2)
            dmod[i + 1][0], dmod[i + 1][1] = bw["dshift"], bw["dscale"]
        dmod[i][5], dg_ln[i][1], db_ln[i][1] = bw["dgate"], bw["dg"], bw["db"]
        dy2 = bw["dy"]
        da, dbb = swiglu_out_backward(dy2, w_down[i], sv["a"], sv["bb"], f"bwd_ffn_act{i}")
        du2 = mm([(da, wt_gate[i]), (dbb, wt_up[i])], trans_b=False, out_dtype=F32, name=f"bwd_ffn_du{i}")
        grads_full[f"down{i}"] = wgrad(sv["h"], dy2, name=f"bwd_w_down{i}")
        grads_full[f"gate{i}"] = wgrad(da, sv["u2"], name=f"bwd_w_gate{i}")
        grads_full[f"up{i}"] = wgrad(dbb, sv["u2"], name=f"bwd_w_up{i}")
        gi = group_of[f"gate{i}"]
        scatter_start(gi)
        ln_g_bwd[i][0] = after_token(ln_g_bwd[i][0], scatter_started[gi])
        bw = sublayer_backward(bw["dx"], bl, f"bwd_ln_mix{i}", du=du2, scale=(mods[i], 4),
                               ln=(sv["r1"], sv["y"], ln_g_bwd[i][0], ln_b_all[i, 0], (mods[i], 2)))
        dmod[i][3], dmod[i][4], dmod[i][2] = bw["dshift"], bw["dscale"], bw["dgate"]
        dg_ln[i][0], db_ln[i][0] = bw["dg"], bw["db"]
        d_a, dy = bw["dx"], bw["dy"]
        o, lse, ops = sv["o"], sv["lse"], sv["ops"]
        if i % 2 == 0:
            do = mm([(dy, w_mo)], trans_b=True, out_dtype=BF16, out_slab=True, name=f"bwd_mla_do{i}")
            grads_full["mla_w_o"] = wgrad(o, dy, name=f"bwd_w_mla_o{i}")
            dqn, dqr, dkn, dvm, dkr = attention_backward("mla", ops, sv["o_delta"], do, lse, bl, mla_scale,
                                                         f"bwd_mla_attn{i}")
            dqr = rope_slabs(dqr, cos_t, sin_t, F32, f"bwd_mla_qrope{i}", transposed=True)
            dcq = mm([(dqn, wt_uq_n), (dqr, wt_uq_r)], trans_b=False, out_dtype=F32, name=f"bwd_mla_dcq{i}")
            dckv = mm([(dkn, wt_uk), (dvm, wt_uv)], trans_b=False, out_dtype=F32, name=f"bwd_mla_dckv{i}")
            d_uq_n = wgrad(dqn, sv["c_q"], name=f"bwd_w_uq_n{i}").reshape(MLA_HEADS, MLA_NOPE, MLA_QR)
            d_uq_r = wgrad(dqr, sv["c_q"], name=f"bwd_w_uq_r{i}").reshape(MLA_HEADS, MLA_ROPE, MLA_QR)
            grads_full["mla_w_uq"] = jnp.concatenate([d_uq_n, d_uq_r], axis=1)
            grads_full["mla_w_uk"] = wgrad(dkn, sv["c_kv"], name=f"bwd_w_uk{i}")
            grads_full["mla_w_uv"] = wgrad(dvm, sv["c_kv"], name=f"bwd_w_uv{i}")
            dh_in, dg_q, dg_kv = mla_latents_backward(sv["h_in"], dcq, dckv, dkr, g_q, g_kv, cos_t, sin_t,
                                                      f"bwd_mla_latents{i}")
            du = mm([(dh_in, w_in)], trans_b=True, out_dtype=F32, name=f"bwd_mla_du{i}")
            grads_full["mla_w_in"] = wgrad(sv["u"], dh_in, name=f"bwd_w_mla_in{i}")[:, :mla_in]
        else:
            do = mm([(dy, w_fo)], trans_b=True, out_dtype=BF16, out_slab=True, name=f"bwd_fox_do{i}")
            grads_full["fox_w_o"] = wgrad(o, dy, name=f"bwd_w_fox_o{i}")
            dq, dk, dvf, dfk = attention_backward("fox", ops, sv["o_delta"], do, lse, bl, fox_scale, f"bwd_fox_attn{i}")
            df = dfk[:, :, :2, :].reshape(bl, FOX_HEADS // 2, nk, 2, tq).transpose(0, 2, 4, 1, 3).reshape(t, FOX_HEADS)
            df = jnp.pad(df, ((0, 0), (0, LANES - FOX_HEADS)))
            dz, db_f = fox_gate_backward(sv["z"], b_f, df, bl, f"bwd_fox_gate{i}")
            du = mm([(dq, wt_fox[0:d]), (dk, wt_fox[d:2 * d]), (dvf, wt_fox[2 * d:3 * d]), (dz, wt_f)],
                    trans_b=False, out_dtype=F32, name=f"bwd_fox_du{i}")
            u_f = sv["u"]
            grads_full["fox_w_in"] = jnp.concatenate(
                [wgrad(dq, u_f, name=f"bwd_w_fox_q{i}"), wgrad(dk, u_f, name=f"bwd_w_fox_k{i}"),
                 wgrad(dvf, u_f, name=f"bwd_w_fox_v{i}"), wgrad(dz, u_f, name=f"bwd_w_fox_f{i}")[:FOX_HEADS]], axis=0)
    scatter_start(0)
    bw = sublayer_backward(d_a, bl, "bwd_input", du=du, scale=(after_token(mods[0], scatter_started[0]), 1), x_in=x2d)
    dmod[0][0], dmod[0][1] = bw["dshift"], bw["dscale"]
    grad_x = bw["dx"].reshape(bl, s, d)

    dmod_rows = jnp.concatenate([r.reshape(bl, d) for layer in dmod for r in layer], axis=0)
    dmod_rows = dmod_rows.reshape(DEPTH, 6, bl, d).transpose(0, 2, 1, 3)
    n_mod = dmod_rows.size // LANES
    ln_parts = [dg_ln[i][k] for i in range(DEPTH) for k in range(2)] + [db_ln[i][k] for i in range(DEPTH) for k in range(2)]
    small_g = jnp.concatenate([dmod_rows.reshape(-1, LANES), dg_q.reshape(-1, LANES), dg_kv.reshape(-1, LANES), db_f]
                              + [p.reshape(-1, LANES) for p in ln_parts] + [loss_cols.reshape(-1, LANES)], axis=0)
    n_small = small_g.shape[0]
    small_g = jnp.pad(small_g, ((0, (-n_small) % 8), (0, 0)))
    small_g_all = all_gather(small_g, "gather_small_grads")
    small_sum = sum_leading(small_g_all, "sum_small_grads")
    per_seq = DEPTH * 6 * d // LANES
    dmod_all = small_g_all[:, :n_mod].reshape(N_DEV, DEPTH, bl, 6 * d).transpose(1, 0, 2, 3)
    dmod_all = dmod_all.reshape(DEPTH, N_DEV * bl, 6 * d)
    o1 = n_mod
    grad_g_q = small_sum[o1:o1 + MLA_QR // LANES].reshape(1, MLA_QR)
    o1 += MLA_QR // LANES
    grad_g_kv = small_sum[o1:o1 + MLA_KVR // LANES].reshape(1, MLA_KVR)
    o1 += MLA_KVR // LANES
    grad_b_f = small_sum[o1:o1 + 1, :FOX_HEADS]
    o1 += 1
    n_ln_rows = DEPTH * 2 * d // LANES
    grad_ln_g_full = small_sum[o1:o1 + n_ln_rows].reshape(DEPTH, 2, d)
    grad_ln_b_full = small_sum[o1 + n_ln_rows:o1 + 2 * n_ln_rows].reshape(DEPTH, 2, d)
    loss = jnp.sum(small_sum[o1 + 2 * n_ln_rows:o1 + 2 * n_ln_rows + d // LANES])
    shard = d // N_DEV
    grad_ln_g = lax.dynamic_slice_in_dim(grad_ln_g_full, dev * shard, shard, axis=2)
    grad_ln_b = lax.dynamic_slice_in_dim(grad_ln_b_full, dev * shard, shard, axis=2)
    by_seq = small_g_all[:, :n_mod].reshape(N_DEV, DEPTH, bl, 6 * d // LANES, LANES).transpose(0, 2, 1, 3, 4)
    grad_ada_b = sum_leading(by_seq.reshape(N_DEV * bl, per_seq, LANES), "sum_ada_b").reshape(DEPTH, 6 * d)
    dmod_cols = lax.dynamic_slice_in_dim(dmod_all, dev * ada_cols, ada_cols, axis=2)
    grad_ada_w = jnp.stack([mm_tn(c_act, dmod_cols[i], name=f"bwd_w_ada{i}") for i in range(DEPTH)])

    g_mine = {}

    def scatter_arrive(gi, after):
        landed = exchange_wait(scatter_started[gi], after, f"scatter_group{gi}_wait", True)
        if gi == 0:
            total = sum_leading(landed[0], f"scatter_group{gi}_sum")
            g_mine.update({nm: total[offsets[nm]:offsets[nm] + rows_of[nm]] for nm in groups[gi]})
            return total
        for nm, land in zip(groups[gi], landed):
            g_mine[nm] = sum_leading(land, f"scatter_sum_{nm}")
        return g_mine[groups[gi][-1]]

    after = scatter_started[0][4]
    for gi in reversed(range(1, len(groups))):
        after = scatter_arrive(gi, after)

    def mine(nm, shape):
        return g_mine[nm].reshape(shape)

    def shard_t(nm, a):
        return mine(nm, t_last(a).shape)

    transposed = {"mla_w_uq", "mla_w_uk", "mla_w_uv", "fox_w_in", "ffn_w_gate", "ffn_w_up"}
    grads = {
        "mla_w_in": lambda: mine("mla_w_in", mla_w_in[0].shape)[None],
        "mla_g_q": lambda: grad_g_q,
        "mla_w_uq": lambda: shard_t("mla_w_uq", mla_w_uq[0])[None],
        "mla_g_kv": lambda: grad_g_kv,
        "mla_w_uk": lambda: shard_t("mla_w_uk", mla_w_uk[0])[None],
        "mla_w_uv": lambda: shard_t("mla_w_uv", mla_w_uv[0])[None],
        "mla_w_o": lambda: mine("mla_w_o", mla_w_o[0].shape)[None],
        "fox_w_in": lambda: shard_t("fox_w_in", fox_w_in[0])[None],
        "fox_b_f": lambda: grad_b_f,
        "fox_w_o": lambda: mine("fox_w_o", fox_w_o[0].shape)[None],
        "ada_w": lambda: grad_ada_w,
        "ada_b": lambda: grad_ada_b,
        "ffn_w_gate": lambda: jnp.stack([shard_t(f"gate{i}", ffn_w_gate[i]) for i in range(DEPTH)]),
        "ffn_w_up": lambda: jnp.stack([shard_t(f"up{i}", ffn_w_up[i]) for i in range(DEPTH)]),
        "ffn_w_down": lambda: jnp.stack([mine(f"down{i}", ffn_w_down[i].shape) for i in range(DEPTH)]),
        "ln_g": lambda: grad_ln_g,
        "ln_b": lambda: grad_ln_b,
    }
    weights = dict(mla_w_in=mla_w_in, mla_g_q=mla_g_q, mla_w_uq=mla_w_uq, mla_g_kv=mla_g_kv, mla_w_uk=mla_w_uk,
                   mla_w_uv=mla_w_uv, mla_w_o=mla_w_o, fox_w_in=fox_w_in, fox_b_f=fox_b_f, fox_w_o=fox_w_o,
                   ada_w=ada_w, ada_b=ada_b, ffn_w_gate=ffn_w_gate, ffn_w_up=ffn_w_up, ffn_w_down=ffn_w_down,
                   ln_g=ln_g, ln_b=ln_b)
    first = dict(mla_w_in=m_mla_w_in, mla_g_q=m_mla_g_q, mla_w_uq=m_mla_w_uq, mla_g_kv=m_mla_g_kv, mla_w_uk=m_mla_w_uk,
                 mla_w_uv=m_mla_w_uv, mla_w_o=m_mla_w_o, fox_w_in=m_fox_w_in, fox_b_f=m_fox_b_f, fox_w_o=m_fox_w_o,
                 ada_w=m_ada_w, ada_b=m_ada_b, ffn_w_gate=m_ffn_w_gate, ffn_w_up=m_ffn_w_up, ffn_w_down=m_ffn_w_down,
                 ln_g=m_ln_g, ln_b=m_ln_b)
    second = dict(mla_w_in=v_mla_w_in, mla_g_q=v_mla_g_q, mla_w_uq=v_mla_w_uq, mla_g_kv=v_mla_g_kv, mla_w_uk=v_mla_w_uk,
                  mla_w_uv=v_mla_w_uv, mla_w_o=v_mla_w_o, fox_w_in=v_fox_w_in, fox_b_f=v_fox_b_f, fox_w_o=v_fox_w_o,
                  ada_w=v_ada_w, ada_b=v_ada_b, ffn_w_gate=v_ffn_w_gate, ffn_w_up=v_ffn_w_up, ffn_w_down=v_ffn_w_down,
                  ln_g=v_ln_g, ln_b=v_ln_b)
    order = list(weights)
    last = [nm for nm in order if group_of.get(nm) == 0]
    updated = {}
    for nm in [nm for nm in order if nm not in last] + last:
        if last and nm == last[0]:
            scatter_arrive(0, after)
        lay = t_last if nm in transposed else (lambda a: a)
        w = lay(weights[nm])
        g = grads[nm]().reshape(w.shape)
        delta, new_m, new_v = adamw(w, g, lay(first[nm]), lay(second[nm]), f"adamw_{nm}")
        updated[nm] = (lay(g), lay(delta), lay(new_m), lay(new_v))
        after = new_v
    return (loss, grad_x, *(updated[nm][k] for k in range(4) for nm in order))
```
